```python
import jax, jax.numpy as jnp
from jax import lax
import numpy as np

D_MODEL = 1024
BATCH = 8
SEQ = 2048
DEPTH = 4

CTX_LEN = 256
GRID_W = 64
N_MIXERS = 3
WIDTH = D_MODEL
POOL_WINDOWS = (2, 4, 8, 16)
N_POOL_GROUPS = 4
POOL_GROUP = WIDTH // N_POOL_GROUPS
HEAD_DIM = 64
N_HEADS = WIDTH // HEAD_DIM
WIN_ROWS_MAX = 8
WIN_COLS = 16
CONV_WIDTH = 3
EPS = 1e-6
N_POOL_LAYERS = (DEPTH + 2) // 3
N_NA_LAYERS = (DEPTH + 1) // 3
N_CONV_LAYERS = DEPTH // 3

kernel_name = "hybrid_pool_natten_shortconv_dit"


def _rmsnorm(x, g):
    xf = x.astype(jnp.float32)
    y = xf * lax.rsqrt(jnp.mean(xf * xf, axis=-1, keepdims=True) + EPS)
    return (y * g.astype(jnp.float32)).astype(x.dtype)


def _modulation(cond, w, b):
    m = jax.nn.silu(cond) @ w + b
    return jnp.split(m, 3, axis=-1)


def _centred_mean(u, w):
    b_, l_, c_ = u.shape
    cs = jnp.concatenate([jnp.zeros((b_, 1, c_), jnp.float32),
                          jnp.cumsum(u.astype(jnp.float32), axis=1)], axis=1)
    t = jnp.arange(l_)
    lo = jnp.clip(t - w // 2, 0, l_)
    hi = jnp.clip(t + w // 2, 0, l_)
    cnt = (hi - lo).astype(jnp.float32)
    return ((cs[:, hi] - cs[:, lo]) / cnt[None, :, None]).astype(u.dtype)


def _pool_mixer(h, w_in, w_grp, scale, w_out):
    u, g = jnp.split(h @ w_in, 2, axis=-1)
    b_, l_, _ = u.shape
    ug = u.reshape(b_, l_, N_POOL_GROUPS, POOL_GROUP)
    pooled = jnp.stack([_centred_mean(ug[:, :, i], w) for i, w in enumerate(POOL_WINDOWS)], axis=2)
    mixed = jnp.einsum('blgc,gcd->blgd', pooled - ug, w_grp).reshape(b_, l_, WIDTH)
    return (mixed * scale * jax.nn.silu(g)) @ w_out


def _dwconv3(y, w, b):
    out = lax.conv_general_dilated(y, w[:, None, :], window_strides=(1,), padding=((1, 1),),
                                   dimension_numbers=('NWC', 'WIO', 'NWC'),
                                   feature_group_count=WIDTH)
    return out + b


def _conv_mixer(h, w_in, conv_w, conv_b, w_out):
    bg, cg, v, g = jnp.split(h @ w_in, 4, axis=-1)
    y = bg * _dwconv3(cg * v, conv_w, conv_b)
    return (y * jax.nn.silu(g)) @ w_out


def _na_mixer(h, hc, w_in, rpb, w_out, need_ctx_out):
    b_, l_, _ = h.shape
    rows = l_ // GRID_W
    wr = min(WIN_ROWS_MAX, rows)
    q, k, v, g = jnp.split(h @ w_in, 4, axis=-1)
    q = q.reshape(b_, rows, GRID_W, N_HEADS, HEAD_DIM) * HEAD_DIM ** -0.5
    k = k.reshape(b_, rows, GRID_W, N_HEADS, HEAD_DIM)
    v = v.reshape(b_, rows, GRID_W, N_HEADS, HEAD_DIM)
    if need_ctx_out:
        qc, kc, vc, gc = jnp.split(hc @ w_in, 4, axis=-1)
    else:
        kc, vc = jnp.split(hc @ w_in[:, WIDTH:3 * WIDTH], 2, axis=-1)
    n_ctx = hc.shape[1]
    kc = kc.reshape(b_, n_ctx, N_HEADS, HEAD_DIM)
    vc = vc.reshape(b_, n_ctx, N_HEADS, HEAD_DIM)

    r = jnp.arange(rows)
    row_idx = jnp.clip(r - wr // 2, 0, rows - wr)[:, None] + jnp.arange(wr)[None, :]
    col = jnp.arange(GRID_W)
    col_idx = jnp.clip(col - WIN_COLS // 2, 0, GRID_W - WIN_COLS)[:, None] + jnp.arange(WIN_COLS)[None, :]
    sel = jax.nn.one_hot(col_idx, GRID_W, dtype=h.dtype)

    kb = k[:, row_idx]
    vb = v[:, row_idx]
    s_blk = jnp.einsum('brqhd,brikhd->bhrqik', q, kb)
    s_loc = jnp.einsum('bhrqik,qjk->bhrqij', s_blk, sel).astype(jnp.float32)
    dr_idx = row_idx - r[:, None] + WIN_ROWS_MAX - 1
    dc_idx = col_idx - col[:, None] + WIN_COLS - 1
    bias = rpb[:, dr_idx[:, None, :, None], dc_idx[None, :, None, :]].astype(jnp.float32)
    s_loc = s_loc + bias[None]
    s_ctx = jnp.einsum('brqhd,bchd->bhrqc', q, kc).astype(jnp.float32)
    n_loc = wr * WIN_COLS
    logits = jnp.concatenate([s_loc.reshape(b_, N_HEADS, rows, GRID_W, n_loc), s_ctx], axis=-1)
    p = jax.nn.softmax(logits, axis=-1).astype(v.dtype)
    p_loc = p[..., :n_loc].reshape(b_, N_HEADS, rows, GRID_W, wr, WIN_COLS)
    p_ctx = p[..., n_loc:]
    p_blk = jnp.einsum('bhrqij,qjk->bhrqik', p_loc, sel)
    o = jnp.einsum('bhrqik,brikhd->brqhd', p_blk, vb) + jnp.einsum('bhrqc,bchd->brqhd', p_ctx, vc)
    y = (o.reshape(b_, l_, WIDTH) * jax.nn.silu(g)) @ w_out

    yc = None
    if need_ctx_out:
        qc = qc.reshape(b_, n_ctx, N_HEADS, HEAD_DIM) * HEAD_DIM ** -0.5
        sc = jnp.einsum('bqhd,bkhd->bhqk', qc, kc).astype(jnp.float32)
        pc = jax.nn.softmax(sc, axis=-1).astype(vc.dtype)
        oc = jnp.einsum('bhqk,bkhd->bqhd', pc, vc).reshape(b_, n_ctx, WIDTH)
        yc = (oc * jax.nn.silu(gc)) @ w_out
    return y, yc


def _fwd_setup_inputs(seed: int = 0) -> dict:
    key = jax.random.key(seed)
    ks = jax.random.split(key, 20)
    nrm = jax.random.normal
    d, w = D_MODEL, WIDTH
    return {
        "x": nrm(ks[0], (BATCH, SEQ, d), jnp.float32),
        "c": nrm(ks[1], (BATCH, d), jnp.float32),
        "ctx": nrm(ks[2], (BATCH, CTX_LEN, d), jnp.float32),
        "c_ctx": nrm(ks[3], (d,), jnp.float32),
        "norm_g": 1.0 + 0.02 * nrm(ks[4], (DEPTH, d), jnp.float32),
        "ada_w": 0.5 * d ** -0.5 * nrm(ks[5], (DEPTH, d, 3 * d), jnp.float32),
        "ada_b": 0.01 * nrm(ks[6], (DEPTH, 3 * d), jnp.float32),
        "pool_w_in": d ** -0.5 * nrm(ks[7], (N_POOL_LAYERS, d, 2 * w), jnp.float32),
        "pool_w_grp": POOL_GROUP ** -0.5 * nrm(ks[8], (N_POOL_LAYERS, N_POOL_GROUPS, POOL_GROUP, POOL_GROUP), jnp.float32),
        "pool_scale": 1.0 + 0.1 * nrm(ks[9], (N_POOL_LAYERS, w), jnp.float32),
        "pool_w_out": w ** -0.5 * nrm(ks[10], (N_POOL_LAYERS, w, d), jnp.float32),
        "na_w_in": d ** -0.5 * nrm(ks[11], (N_NA_LAYERS, d, 4 * w), jnp.float32),
        "na_rpb": 0.1 * nrm(ks[12], (N_NA_LAYERS, N_HEADS, 2 * WIN_ROWS_MAX - 1, 2 * WIN_COLS - 1), jnp.float32),
        "na_w_out": w ** -0.5 * nrm(ks[13], (N_NA_LAYERS, w, d), jnp.float32),
        "conv_w_in": d ** -0.5 * nrm(ks[14], (N_CONV_LAYERS, d, 4 * w), jnp.float32),
        "conv_dw": CONV_WIDTH ** -0.5 * nrm(ks[15], (N_CONV_LAYERS, CONV_WIDTH, w), jnp.float32),
        "conv_db": 0.01 * nrm(ks[16], (N_CONV_LAYERS, w), jnp.float32),
        "conv_w_out": w ** -0.5 * nrm(ks[17], (N_CONV_LAYERS, w, d), jnp.float32),
        "final_g": 1.0 + 0.02 * nrm(ks[18], (d,), jnp.float32),
    }


def _fwd_reference(x, c, ctx, c_ctx, norm_g, ada_w, ada_b, pool_w_in, pool_w_grp, pool_scale, pool_w_out,
              na_w_in, na_rpb, na_w_out, conv_w_in, conv_dw, conv_db, conv_w_out, final_g):
    last_ctx_reader = max([i for i in range(DEPTH) if i % N_MIXERS == 1], default=-1)
    for i in range(DEPTH):
        kind, j = i % N_MIXERS, i // N_MIXERS
        update_ctx = i < last_ctx_reader
        shift, scale, gate = _modulation(c, ada_w[i], ada_b[i])
        hx = _rmsnorm(x, norm_g[i]) * (1.0 + scale[:, None]) + shift[:, None]
        if kind == 1 or update_ctx:
            cshift, cscale, cgate = _modulation(c_ctx, ada_w[i], ada_b[i])
            hc = _rmsnorm(ctx, norm_g[i]) * (1.0 + cscale) + cshift
        if kind == 0:
            yx = _pool_mixer(hx, pool_w_in[j], pool_w_grp[j], pool_scale[j], pool_w_out[j])
            if update_ctx:
                yc = _pool_mixer(hc, pool_w_in[j], pool_w_grp[j], pool_scale[j], pool_w_out[j])
        elif kind == 1:
            yx, yc = _na_mixer(hx, hc, na_w_in[j], na_rpb[j], na_w_out[j], update_ctx)
        else:
            yx = _conv_mixer(hx, conv_w_in[j], conv_dw[j], conv_db[j], conv_w_out[j])
            if update_ctx:
                yc = _conv_mixer(hc, conv_w_in[j], conv_dw[j], conv_db[j], conv_w_out[j])
        x = x + gate[:, None] * yx
        if update_ctx:
            ctx = ctx + cgate * yc
    return _rmsnorm(x, final_g)


import jax as _jax
import jax.numpy as _jnp

TWIN_FORMAT = 'train_step'
FWD_PARAMS = ['x', 'c', 'ctx', 'c_ctx', 'norm_g', 'ada_w', 'ada_b', 'pool_w_in', 'pool_w_grp', 'pool_scale', 'pool_w_out', 'na_w_in', 'na_rpb', 'na_w_out', 'conv_w_in', 'conv_dw', 'conv_db', 'conv_w_out', 'final_g']
TWIN_WEIGHTS = ['c_ctx', 'norm_g', 'ada_w', 'ada_b', 'pool_w_in', 'pool_w_grp', 'pool_scale', 'pool_w_out', 'na_w_in', 'na_rpb', 'na_w_out', 'conv_w_in', 'conv_dw', 'conv_db', 'conv_w_out', 'final_g']
TWIN_DIFF_INPUT = 'x'
TWIN_INPUTS = ['x', 'c', 'ctx', 'c_ctx', 'norm_g', 'ada_w', 'ada_b', 'pool_w_in', 'pool_w_grp', 'pool_scale', 'pool_w_out', 'na_w_in', 'na_rpb', 'na_w_out', 'conv_w_in', 'conv_dw', 'conv_db', 'conv_w_out', 'final_g', 'loss_target', 'm_c_ctx', 'm_norm_g', 'm_ada_w', 'm_ada_b', 'm_pool_w_in', 'm_pool_w_grp', 'm_pool_scale', 'm_pool_w_out', 'm_na_w_in', 'm_na_rpb', 'm_na_w_out', 'm_conv_w_in', 'm_conv_dw', 'm_conv_db', 'm_conv_w_out', 'm_final_g', 'v_c_ctx', 'v_norm_g', 'v_ada_w', 'v_ada_b', 'v_pool_w_in', 'v_pool_w_grp', 'v_pool_scale', 'v_pool_w_out', 'v_na_w_in', 'v_na_rpb', 'v_na_w_out', 'v_conv_w_in', 'v_conv_dw', 'v_conv_db', 'v_conv_w_out', 'v_final_g']
TWIN_OUTPUTS = ['loss', 'grad_x', 'grad_c_ctx', 'grad_norm_g', 'grad_ada_w', 'grad_ada_b', 'grad_pool_w_in', 'grad_pool_w_grp', 'grad_pool_scale', 'grad_pool_w_out', 'grad_na_w_in', 'grad_na_rpb', 'grad_na_w_out', 'grad_conv_w_in', 'grad_conv_dw', 'grad_conv_db', 'grad_conv_w_out', 'grad_final_g', 'delta_c_ctx', 'delta_norm_g', 'delta_ada_w', 'delta_ada_b', 'delta_pool_w_in', 'delta_pool_w_grp', 'delta_pool_scale', 'delta_pool_w_out', 'delta_na_w_in', 'delta_na_rpb', 'delta_na_w_out', 'delta_conv_w_in', 'delta_conv_dw', 'delta_conv_db', 'delta_conv_w_out', 'delta_final_g', 'new_m_c_ctx', 'new_m_norm_g', 'new_m_ada_w', 'new_m_ada_b', 'new_m_pool_w_in', 'new_m_pool_w_grp', 'new_m_pool_scale', 'new_m_pool_w_out', 'new_m_na_w_in', 'new_m_na_rpb', 'new_m_na_w_out', 'new_m_conv_w_in', 'new_m_conv_dw', 'new_m_conv_db', 'new_m_conv_w_out', 'new_m_final_g', 'new_v_c_ctx', 'new_v_norm_g', 'new_v_ada_w', 'new_v_ada_b', 'new_v_pool_w_in', 'new_v_pool_w_grp', 'new_v_pool_scale', 'new_v_pool_w_out', 'new_v_na_w_in', 'new_v_na_rpb', 'new_v_na_w_out', 'new_v_conv_w_in', 'new_v_conv_dw', 'new_v_conv_db', 'new_v_conv_w_out', 'new_v_final_g']
TWIN_LEAF_KINDS = {'loss': 'loss', 'grad_x': 'grad_x', 'grad_c_ctx': 'grad_w', 'grad_norm_g': 'grad_w', 'grad_ada_w': 'grad_w', 'grad_ada_b': 'grad_w', 'grad_pool_w_in': 'grad_w', 'grad_pool_w_grp': 'grad_w', 'grad_pool_scale': 'grad_w', 'grad_pool_w_out': 'grad_w', 'grad_na_w_in': 'grad_w', 'grad_na_rpb': 'grad_w', 'grad_na_w_out': 'grad_w', 'grad_conv_w_in': 'grad_w', 'grad_conv_dw': 'grad_w', 'grad_conv_db': 'grad_w', 'grad_conv_w_out': 'grad_w', 'grad_final_g': 'grad_w', 'delta_c_ctx': 'delta_w', 'delta_norm_g': 'delta_w', 'delta_ada_w': 'delta_w', 'delta_ada_b': 'delta_w', 'delta_pool_w_in': 'delta_w', 'delta_pool_w_grp': 'delta_w', 'delta_pool_scale': 'delta_w', 'delta_pool_w_out': 'delta_w', 'delta_na_w_in': 'delta_w', 'delta_na_rpb': 'delta_w', 'delta_na_w_out': 'delta_w', 'delta_conv_w_in': 'delta_w', 'delta_conv_dw': 'delta_w', 'delta_conv_db': 'delta_w', 'delta_conv_w_out': 'delta_w', 'delta_final_g': 'delta_w', 'new_m_c_ctx': 'new_m', 'new_m_norm_g': 'new_m', 'new_m_ada_w': 'new_m', 'new_m_ada_b': 'new_m', 'new_m_pool_w_in': 'new_m', 'new_m_pool_w_grp': 'new_m', 'new_m_pool_scale': 'new_m', 'new_m_pool_w_out': 'new_m', 'new_m_na_w_in': 'new_m', 'new_m_na_rpb': 'new_m', 'new_m_na_w_out': 'new_m', 'new_m_conv_w_in': 'new_m', 'new_m_conv_dw': 'new_m', 'new_m_conv_db': 'new_m', 'new_m_conv_w_out': 'new_m', 'new_m_final_g': 'new_m', 'new_v_c_ctx': 'new_v', 'new_v_norm_g': 'new_v', 'new_v_ada_w': 'new_v', 'new_v_ada_b': 'new_v', 'new_v_pool_w_in': 'new_v', 'new_v_pool_w_grp': 'new_v', 'new_v_pool_scale': 'new_v', 'new_v_pool_w_out': 'new_v', 'new_v_na_w_in': 'new_v', 'new_v_na_rpb': 'new_v', 'new_v_na_w_out': 'new_v', 'new_v_conv_w_in': 'new_v', 'new_v_conv_dw': 'new_v', 'new_v_conv_db': 'new_v', 'new_v_conv_w_out': 'new_v', 'new_v_final_g': 'new_v'}


def _forward(args):
    return _fwd_reference(*[args[k] for k in FWD_PARAMS])


def _output_shape():
    out = _jax.eval_shape(lambda: _forward(_fwd_setup_inputs(0)))
    return out.shape, out.dtype

N_MICROBATCH = 1
ADAM_LR = 0.001
ADAM_B1 = 0.9
ADAM_B2 = 0.999
ADAM_EPS = 1e-08
ADAM_WD = 0.01
ADAM_STEP = 10
PER_EXAMPLE_BATCH_AXIS = {'x': 0, 'c': 0, 'ctx': 0, 'loss_target': 0}
SHARED_INPUTS = []
_WEIGHT_DTYPES = {'c_ctx': _jnp.float32, 'norm_g': _jnp.float32, 'ada_w': _jnp.float32, 'ada_b': _jnp.float32, 'pool_w_in': _jnp.float32, 'pool_w_grp': _jnp.float32, 'pool_scale': _jnp.float32, 'pool_w_out': _jnp.float32, 'na_w_in': _jnp.float32, 'na_rpb': _jnp.float32, 'na_w_out': _jnp.float32, 'conv_w_in': _jnp.float32, 'conv_dw': _jnp.float32, 'conv_db': _jnp.float32, 'conv_w_out': _jnp.float32, 'final_g': _jnp.float32}
MOMENT_SCALE = {'c_ctx': 7.024127e-03, 'norm_g': 3.914135e-02, 'ada_w': 3.327138e-02, 'ada_b': 5.403601e-02, 'pool_w_in': 2.387291e-02, 'pool_w_grp': 2.349443e-02, 'pool_scale': 2.384654e-02, 'pool_w_out': 2.350001e-02, 'na_w_in': 6.168462e-03, 'na_rpb': 6.835083e-04, 'na_w_out': 7.789122e-03, 'conv_w_in': 3.206374e-02, 'conv_dw': 3.118468e-02, 'conv_db': 2.775000e-02, 'conv_w_out': 3.177722e-02, 'final_g': 1.603069e+01}


def _to_microbatches(a, axis):
    t = _jnp.moveaxis(a, axis, 0)
    t = t.reshape((N_MICROBATCH, t.shape[0] // N_MICROBATCH) + t.shape[1:])
    return _jnp.moveaxis(t, 1, axis + 1)


def setup_inputs(seed: int = 0) -> dict:
    inp = _fwd_setup_inputs(seed)
    key = _jax.random.fold_in(_jax.random.key(seed), 7919)
    shape, _ = _output_shape()
    out = dict(inp)
    out["loss_target"] = _jax.random.normal(_jax.random.fold_in(key, 0), shape, _jnp.float32)
    for i, name in enumerate(TWIN_WEIGHTS):
        w = inp[name].astype(_jnp.float32)
        if MOMENT_SCALE is None:
            s = _jnp.sqrt(_jnp.mean(_jnp.square(w)) + 1e-30)
        else:
            s = MOMENT_SCALE[name]
        km, kv = _jax.random.split(_jax.random.fold_in(key, i + 1))
        out[name] = w
        out["m_" + name] = s * _jax.random.normal(km, w.shape, _jnp.float32)
        out["v_" + name] = (s * s) * _jax.random.uniform(kv, w.shape, _jnp.float32, 0.5, 1.5)
    if N_MICROBATCH > 1:
        for name, axis in PER_EXAMPLE_BATCH_AXIS.items():
            out[name] = _to_microbatches(out[name], axis)
    return {'x': out['x'], 'c': out['c'], 'ctx': out['ctx'], 'c_ctx': out['c_ctx'], 'norm_g': out['norm_g'], 'ada_w': out['ada_w'], 'ada_b': out['ada_b'], 'pool_w_in': out['pool_w_in'], 'pool_w_grp': out['pool_w_grp'], 'pool_scale': out['pool_scale'], 'pool_w_out': out['pool_w_out'], 'na_w_in': out['na_w_in'], 'na_rpb': out['na_rpb'], 'na_w_out': out['na_w_out'], 'conv_w_in': out['conv_w_in'], 'conv_dw': out['conv_dw'], 'conv_db': out['conv_db'], 'conv_w_out': out['conv_w_out'], 'final_g': out['final_g'], 'loss_target': out['loss_target'], 'm_c_ctx': out['m_c_ctx'], 'm_norm_g': out['m_norm_g'], 'm_ada_w': out['m_ada_w'], 'm_ada_b': out['m_ada_b'], 'm_pool_w_in': out['m_pool_w_in'], 'm_pool_w_grp': out['m_pool_w_grp'], 'm_pool_scale': out['m_pool_scale'], 'm_pool_w_out': out['m_pool_w_out'], 'm_na_w_in': out['m_na_w_in'], 'm_na_rpb': out['m_na_rpb'], 'm_na_w_out': out['m_na_w_out'], 'm_conv_w_in': out['m_conv_w_in'], 'm_conv_dw': out['m_conv_dw'], 'm_conv_db': out['m_conv_db'], 'm_conv_w_out': out['m_conv_w_out'], 'm_final_g': out['m_final_g'], 'v_c_ctx': out['v_c_ctx'], 'v_norm_g': out['v_norm_g'], 'v_ada_w': out['v_ada_w'], 'v_ada_b': out['v_ada_b'], 'v_pool_w_in': out['v_pool_w_in'], 'v_pool_w_grp': out['v_pool_w_grp'], 'v_pool_scale': out['v_pool_scale'], 'v_pool_w_out': out['v_pool_w_out'], 'v_na_w_in': out['v_na_w_in'], 'v_na_rpb': out['v_na_rpb'], 'v_na_w_out': out['v_na_w_out'], 'v_conv_w_in': out['v_conv_w_in'], 'v_conv_dw': out['v_conv_dw'], 'v_conv_db': out['v_conv_db'], 'v_conv_w_out': out['v_conv_w_out'], 'v_final_g': out['v_final_g']}


def _loss(weights, diff, rest, loss_target):
    with _jax.named_scope("forward"):
        args = {**rest, TWIN_DIFF_INPUT: diff, **{k: w.astype(_WEIGHT_DTYPES[k]) for k, w in weights.items()}}
        y = _forward(args)
    with _jax.named_scope("loss_head"):
        err = _jnp.square(y.astype(_jnp.float32) - loss_target)
        return 0.5 * _jnp.sum(_jnp.mean(err, axis=-1)) if err.ndim else 0.5 * err


def _adamw(w, g, m, v):
    m = ADAM_B1 * m + (1.0 - ADAM_B1) * g
    v = ADAM_B2 * v + (1.0 - ADAM_B2) * _jnp.square(g)
    m_hat = m / (1.0 - ADAM_B1 ** ADAM_STEP)
    v_hat = v / (1.0 - ADAM_B2 ** ADAM_STEP)
    delta = -ADAM_LR * (m_hat / (_jnp.sqrt(v_hat) + ADAM_EPS) + ADAM_WD * w)
    return delta, m, v


def reference(x, c, ctx, c_ctx, norm_g, ada_w, ada_b, pool_w_in, pool_w_grp, pool_scale, pool_w_out, na_w_in, na_rpb, na_w_out, conv_w_in, conv_dw, conv_db, conv_w_out, final_g, loss_target, m_c_ctx, m_norm_g, m_ada_w, m_ada_b, m_pool_w_in, m_pool_w_grp, m_pool_scale, m_pool_w_out, m_na_w_in, m_na_rpb, m_na_w_out, m_conv_w_in, m_conv_dw, m_conv_db, m_conv_w_out, m_final_g, v_c_ctx, v_norm_g, v_ada_w, v_ada_b, v_pool_w_in, v_pool_w_grp, v_pool_scale, v_pool_w_out, v_na_w_in, v_na_rpb, v_na_w_out, v_conv_w_in, v_conv_dw, v_conv_db, v_conv_w_out, v_final_g):
    given = dict(x=x, c=c, ctx=ctx, c_ctx=c_ctx, norm_g=norm_g, ada_w=ada_w, ada_b=ada_b, pool_w_in=pool_w_in, pool_w_grp=pool_w_grp, pool_scale=pool_scale, pool_w_out=pool_w_out, na_w_in=na_w_in, na_rpb=na_rpb, na_w_out=na_w_out, conv_w_in=conv_w_in, conv_dw=conv_dw, conv_db=conv_db, conv_w_out=conv_w_out, final_g=final_g, loss_target=loss_target, m_c_ctx=m_c_ctx, m_norm_g=m_norm_g, m_ada_w=m_ada_w, m_ada_b=m_ada_b, m_pool_w_in=m_pool_w_in, m_pool_w_grp=m_pool_w_grp, m_pool_scale=m_pool_scale, m_pool_w_out=m_pool_w_out, m_na_w_in=m_na_w_in, m_na_rpb=m_na_rpb, m_na_w_out=m_na_w_out, m_conv_w_in=m_conv_w_in, m_conv_dw=m_conv_dw, m_conv_db=m_conv_db, m_conv_w_out=m_conv_w_out, m_final_g=m_final_g, v_c_ctx=v_c_ctx, v_norm_g=v_norm_g, v_ada_w=v_ada_w, v_ada_b=v_ada_b, v_pool_w_in=v_pool_w_in, v_pool_w_grp=v_pool_w_grp, v_pool_scale=v_pool_scale, v_pool_w_out=v_pool_w_out, v_na_w_in=v_na_w_in, v_na_rpb=v_na_rpb, v_na_w_out=v_na_w_out, v_conv_w_in=v_conv_w_in, v_conv_dw=v_conv_dw, v_conv_db=v_conv_db, v_conv_w_out=v_conv_w_out, v_final_g=v_final_g)
    weights = {n: given[n] for n in TWIN_WEIGHTS}
    shared = {n: given[n] for n in SHARED_INPUTS}
    per_example = {n: given[n] for n in ['x', 'c', 'ctx']}
    grad_fn = _jax.value_and_grad(_loss, argnums=(0, 1))

    def one_microbatch(ex, loss_target):
        ex = dict(ex)
        diff = ex.pop(TWIN_DIFF_INPUT)
        return grad_fn(weights, diff, {**shared, **ex}, loss_target)

    if N_MICROBATCH == 1:
        loss, (grad_w, grad_x) = one_microbatch(per_example, given["loss_target"])
    else:
        def body(carry, xs):
            loss_sum, grad_sum = carry
            l_k, (gw_k, gx_k) = one_microbatch(xs[0], xs[1])
            with _jax.named_scope("update"):
                return (loss_sum + l_k, _jax.tree.map(_jnp.add, grad_sum, gw_k)), gx_k

        init = (_jnp.zeros((), _jnp.float32), _jax.tree.map(_jnp.zeros_like, weights))
        (loss, grad_w), grad_x = _jax.lax.scan(body, init, (per_example, given["loss_target"]))
    with _jax.named_scope("update"):
        delta_w, new_m, new_v = {}, {}, {}
        for n in TWIN_WEIGHTS:
            delta_w[n], new_m[n], new_v[n] = _adamw(weights[n], grad_w[n], given["m_" + n], given["v_" + n])
    return (loss, grad_x, *[grad_w[n] for n in TWIN_WEIGHTS], *[delta_w[n] for n in TWIN_WEIGHTS],
            *[new_m[n] for n in TWIN_WEIGHTS], *[new_v[n] for n in TWIN_WEIGHTS])
```

```python
import functools

import numpy as np
import jax
import jax.numpy as jnp
from jax import lax
from jax.experimental import pallas as pl
from jax.experimental.pallas import tpu as pltpu

F32 = jnp.float32
BF16 = jnp.bfloat16

EPS = 1e-6
GRID_W = 64
HEAD_DIM = 64
WIN_ROWS = 8
WIN_COLS = 16
POOL_WINDOWS = (2, 4, 8, 16)
Q_ROWS = 4
K_ROWS = 12
PAD_ROWS = 4
NEG = -1e30

ADAM_LR = 0.001
ADAM_B1 = 0.9
ADAM_B2 = 0.999
ADAM_EPS = 1e-08
ADAM_WD = 0.01
ADAM_STEP = 10

ROW_BLOCK = 256
VMEM_LIMIT = 56 * 1024 * 1024

MESH = pl.DeviceIdType.MESH
HBM_SPEC = pl.BlockSpec(memory_space=pltpu.HBM)


def _cparams(*sem):
    return pltpu.CompilerParams(dimension_semantics=sem or None, vmem_limit_bytes=VMEM_LIMIT)


def _sds(shape, dtype):
    return jax.ShapeDtypeStruct(tuple(shape), dtype)


def _sigmoid(x):
    return 1.0 / (1.0 + jnp.exp(-x))


def _silu(x):
    return x * _sigmoid(x)


def _dsilu(x):
    s = _sigmoid(x)
    return s * (1.0 + x * (1.0 - s))


_DIMS = {
    "nn": (((1,), (0,)), ((), ())),
    "nt": (((1,), (1,)), ((), ())),
    "tn": (((0,), (0,)), ((), ())),
}


def _matmul(a, b, *, mode, grid, a_spec, b_spec, out_shapes, out_specs, name, nk=1,
            a_silu=False, exact=False, epilogue=None, extra=(), extra_specs=(), acc_shape=None):
    n_extra = len(extra)
    n_out = len(out_shapes)

    def body(*refs):
        a_ref, b_ref = refs[:2]
        ex = refs[2:2 + n_extra]
        outs = refs[2 + n_extra:2 + n_extra + n_out]
        av = a_ref[...]
        bv = b_ref[...]
        if a_silu:
            av = _silu(av.astype(F32))
        if exact:
            prod = lax.dot_general(av.astype(F32), bv.astype(F32), _DIMS[mode],
                                   precision=lax.Precision.HIGHEST, preferred_element_type=F32)
        else:
            prod = lax.dot_general(av.astype(BF16), bv.astype(BF16), _DIMS[mode], preferred_element_type=F32)

        def finish(res):
            if epilogue is None:
                outs[0][...] = res.astype(outs[0].dtype)
            elif epilogue == "bias":
                outs[0][...] = (res + ex[0][...]).astype(outs[0].dtype)
            else:
                outs[0][...] = res.astype(outs[0].dtype)
                outs[1][...] = ex[0][...] + ex[1][...] * res

        if nk == 1:
            finish(prod)
        else:
            acc = refs[-1]
            k = pl.program_id(len(grid) - 1)

            @pl.when(k == 0)
            def _():
                acc[...] = prod

            @pl.when(k > 0)
            def _():
                acc[...] += prod

            @pl.when(k == nk - 1)
            def _():
                finish(acc[...])

    scratch = [pltpu.VMEM(acc_shape, F32)] if nk > 1 else []
    sem = ("parallel",) * (len(grid) - 1) + ("arbitrary",)
    return pl.pallas_call(
        body, grid=grid, in_specs=[a_spec, b_spec, *extra_specs], out_specs=list(out_specs),
        out_shape=list(out_shapes), scratch_shapes=scratch, name=name, compiler_params=_cparams(*sem),
    )(a, b, *extra)


def _row_tile(rows):
    for t in (768, 512, 256):
        if rows % t == 0:
            return t
    return rows


def _mm_nn(a, b, name, out_dtype=F32, tn=1024):
    m, k = a.shape
    n = b.shape[1]
    tm = _row_tile(m)
    tn = min(tn, n)
    return _matmul(
        a, b, mode="nn", grid=(m // tm, n // tn),
        a_spec=pl.BlockSpec((tm, k), lambda i, j: (i, 0)), b_spec=pl.BlockSpec((k, tn), lambda i, j: (0, j)),
        out_shapes=[_sds((m, n), out_dtype)], out_specs=[pl.BlockSpec((tm, tn), lambda i, j: (i, j))], name=name)[0]


def _mm_out_resid(a, w_out, xres, gate, nxb, name):
    m, k = a.shape
    n = w_out.shape[1]
    tm = ROW_BLOCK
    seg = lambda i, j: (jnp.where(i >= nxb, 1, 0), 0, 0)
    return _matmul(
        a, w_out, mode="nn", grid=(m // tm, 1),
        a_spec=pl.BlockSpec((tm, k), lambda i, j: (i, 0)), b_spec=pl.BlockSpec((k, n), lambda i, j: (0, 0)),
        extra=(xres, gate), extra_specs=(pl.BlockSpec((tm, n), lambda i, j: (i, 0)), pl.BlockSpec((None, 1, n), seg)),
        out_shapes=[_sds((m, n), F32), _sds((m, n), F32)],
        out_specs=[pl.BlockSpec((tm, n), lambda i, j: (i, 0))] * 2, epilogue="resid", name=name)


def _mm_nt(a, b, name, out_dtype=F32):
    m, n = a.shape
    k = b.shape[0]
    tm = _row_tile(m)
    return _matmul(
        a, b, mode="nt", grid=(m // tm, 1),
        a_spec=pl.BlockSpec((tm, n), lambda i, j: (i, 0)), b_spec=pl.BlockSpec((k, n), lambda i, j: (0, 0)),
        out_shapes=[_sds((m, k), out_dtype)], out_specs=[pl.BlockSpec((tm, k), lambda i, j: (i, 0))], name=name)[0]


def _mm_nt_parts(a, b, name):
    p, m, kp = a.shape
    d = b.shape[0]
    tm = _row_tile(m)
    return _matmul(
        a, b, mode="nt", grid=(m // tm, p), nk=p, acc_shape=(tm, d),
        a_spec=pl.BlockSpec((None, tm, kp), lambda i, q: (q, i, 0)), b_spec=pl.BlockSpec((d, kp), lambda i, q: (0, q)),
        out_shapes=[_sds((m, d), F32)], out_specs=[pl.BlockSpec((tm, d), lambda i, q: (i, 0))], name=name)[0]


def _mm_tn(a, b, name, out_dtype, tm=512):
    r, m = a.shape
    n = b.shape[1]
    tm = min(tm, m)
    tn = min(1024, n)
    return _matmul(
        a, b, mode="tn", grid=(m // tm, n // tn),
        a_spec=pl.BlockSpec((r, tm), lambda i, j: (0, i)), b_spec=pl.BlockSpec((r, tn), lambda i, j: (0, j)),
        out_shapes=[_sds((m, n), out_dtype)], out_specs=[pl.BlockSpec((tm, tn), lambda i, j: (i, j))], name=name)[0]


def _mm_tn_parts(a, b, name, out_dtype, tm=512):
    r, m = a.shape
    p, _, np_ = b.shape
    tm = min(tm, m)
    return _matmul(
        a, b, mode="tn", grid=(m // tm, p),
        a_spec=pl.BlockSpec((r, tm), lambda i, q: (0, i)), b_spec=pl.BlockSpec((None, r, np_), lambda i, q: (q, 0, 0)),
        out_shapes=[_sds((m, p * np_), out_dtype)], out_specs=[pl.BlockSpec((tm, np_), lambda i, q: (i, q))],
        name=name)[0]


def _seg_map(nxb):
    return lambda i: (jnp.where(i >= nxb, 1, 0), 0, 0)


def _normmod_fwd(x, g, scale, shift, nxb, name):
    rows, d = x.shape
    tr = ROW_BLOCK

    def body(x_ref, g_ref, sc_ref, sh_ref, h_ref, r_ref):
        xv = x_ref[...]
        r = lax.rsqrt(jnp.mean(xv * xv, axis=-1, keepdims=True) + EPS)
        h = (xv * r) * g_ref[...] * (1.0 + sc_ref[...]) + sh_ref[...]
        h_ref[...] = h.astype(BF16)
        r_ref[...] = r

    row = pl.BlockSpec((tr, d), lambda i: (i, 0))
    vec = pl.BlockSpec((None, 1, d), _seg_map(nxb))
    return pl.pallas_call(
        body, grid=(rows // tr,), in_specs=[row, pl.BlockSpec((1, d), lambda i: (0, 0)), vec, vec],
        out_specs=[row, pl.BlockSpec((tr, 1), lambda i: (i, 0))],
        out_shape=[_sds((rows, d), BF16), _sds((rows, 1), F32)], name=name, compiler_params=_cparams("parallel"),
    )(x, g, scale, shift)


def _normmod_bwd(dh, x, r, g, scale, dres, nxb, name):
    rows, d = x.shape
    tr = ROW_BLOCK
    nres = dres.shape[0] // tr
    nseg = scale.shape[0]

    def body(dh_ref, x_ref, r_ref, g_ref, sc_ref, dres_ref, dx_ref, dsh_ref, dge_ref):
        i = pl.program_id(0)
        dhv = dh_ref[...]
        rv = r_ref[...]
        xn = x_ref[...] * rv
        dxn = dhv * (g_ref[...] * (1.0 + sc_ref[...]))
        dx = rv * (dxn - xn * jnp.mean(dxn * xn, axis=-1, keepdims=True))

        @pl.when(i < nres)
        def _():
            dx_ref[...] = dx + dres_ref[...]

        @pl.when(i >= nres)
        def _():
            dx_ref[...] = dx

        first = jnp.logical_or(i == 0, i == nxb)
        s_dh = jnp.sum(dhv, axis=0, keepdims=True)
        s_ge = jnp.sum(dhv * xn, axis=0, keepdims=True)

        @pl.when(first)
        def _():
            dsh_ref[...] = s_dh
            dge_ref[...] = s_ge

        @pl.when(jnp.logical_not(first))
        def _():
            dsh_ref[...] += s_dh
            dge_ref[...] += s_ge

    row = pl.BlockSpec((tr, d), lambda i: (i, 0))
    vec = pl.BlockSpec((None, 1, d), _seg_map(nxb))
    return pl.pallas_call(
        body, grid=(rows // tr,),
        in_specs=[row, row, pl.BlockSpec((tr, 1), lambda i: (i, 0)), pl.BlockSpec((1, d), lambda i: (0, 0)), vec,
                  pl.BlockSpec((tr, d), lambda i: (jnp.minimum(i, nres - 1), 0))],
        out_specs=[row, vec, vec],
        out_shape=[_sds((rows, d), F32), _sds((nseg, 1, d), F32), _sds((nseg, 1, d), F32)],
        name=name, compiler_params=_cparams("arbitrary"),
    )(dh, x, r, g, scale, dres)


def _gate_bwd(dxo, yx, gate, nxb, name):
    rows, d = yx.shape
    tr = ROW_BLOCK
    nseg = gate.shape[0]

    def body(dx_ref, yx_ref, gt_ref, dyx_ref, dg_ref):
        i = pl.program_id(0)
        dxv = dx_ref[...]
        dyx_ref[...] = (dxv * gt_ref[...]).astype(BF16)
        s = jnp.sum(dxv * yx_ref[...], axis=0, keepdims=True)
        first = jnp.logical_or(i == 0, i == nxb)

        @pl.when(first)
        def _():
            dg_ref[...] = s

        @pl.when(jnp.logical_not(first))
        def _():
            dg_ref[...] += s

    row = pl.BlockSpec((tr, d), lambda i: (i, 0))
    vec = pl.BlockSpec((None, 1, d), _seg_map(nxb))
    return pl.pallas_call(
        body, grid=(rows // tr,), in_specs=[row, row, vec], out_specs=[row, vec],
        out_shape=[_sds((rows, d), BF16), _sds((nseg, 1, d), F32)], name=name, compiler_params=_cparams("arbitrary"),
    )(dxo, yx, gate)


_PAD_TOP = 16
_PAD_BOT = 32


def _window_sum(buf, xv, lo, n):
    t = xv.shape[0]
    c = xv.shape[1]
    tp = t + _PAD_TOP + _PAD_BOT
    buf[pl.ds(0, _PAD_TOP), :] = jnp.zeros((_PAD_TOP, c), F32)
    buf[pl.ds(_PAD_TOP, t), :] = xv
    buf[pl.ds(_PAD_TOP + t, _PAD_BOT), :] = jnp.zeros((_PAD_BOT, c), F32)
    p = buf[...]
    k = 1
    while k < n:
        p = p + pltpu.roll(p, tp - k, 0)
        k *= 2
    if lo:
        p = pltpu.roll(p, -lo, 0)
    buf[...] = p
    return buf[pl.ds(_PAD_TOP, t), :]


def _window_count(t, half):
    pos = lax.broadcasted_iota(jnp.int32, (t, 1), 0)
    return (jnp.minimum(pos + half, t) - jnp.maximum(pos - half, 0)).astype(F32)


def _segments(rows, nx):
    return [(0, nx)] + ([(nx, rows - nx)] if rows > nx else [])


def _pool_fwd(uv, nx, name):
    rows = uv.shape[0]
    w = uv.shape[1] // 2
    cb = 128
    per_group = w // len(POOL_WINDOWS) // cb
    segs = _segments(rows, nx)

    def body(u_ref, z_ref, *bufs):
        j = pl.program_id(0)
        for gi, win in enumerate(POOL_WINDOWS):
            half = win // 2

            @pl.when(jnp.logical_and(j >= gi * per_group, j < (gi + 1) * per_group))
            def _():
                for (start, length), buf in zip(segs, bufs):
                    uvv = u_ref[pl.ds(start, length), :]
                    s = _window_sum(buf, uvv, -half, win)
                    z_ref[pl.ds(start, length), :] = (s / _window_count(length, half) - uvv).astype(BF16)

    scratch = [pltpu.VMEM((length + _PAD_TOP + _PAD_BOT, cb), F32) for _, length in segs]
    return pl.pallas_call(
        body, grid=(w // cb,), in_specs=[pl.BlockSpec((rows, cb), lambda j: (0, j))],
        out_specs=pl.BlockSpec((rows, cb), lambda j: (0, j)), out_shape=_sds((rows, w), BF16),
        scratch_shapes=scratch, name=name, compiler_params=_cparams("parallel"),
    )(uv)


def _pool_bwd(dz, dgt, nx, name):
    rows, w = dz.shape
    cb = 128
    per_group = w // len(POOL_WINDOWS) // cb
    segs = _segments(rows, nx)

    def body(dz_ref, dgt_ref, o_ref, *bufs):
        j = pl.program_id(0)
        o_ref[1] = dgt_ref[...]
        for gi, win in enumerate(POOL_WINDOWS):
            half = win // 2

            @pl.when(jnp.logical_and(j >= gi * per_group, j < (gi + 1) * per_group))
            def _():
                for (start, length), buf in zip(segs, bufs):
                    dzv = dz_ref[pl.ds(start, length), :]
                    s = _window_sum(buf, dzv / _window_count(length, half), 1 - half, win)
                    o_ref[0, pl.ds(start, length), :] = (s - dzv).astype(BF16)

    scratch = [pltpu.VMEM((length + _PAD_TOP + _PAD_BOT, cb), F32) for _, length in segs]
    col = pl.BlockSpec((rows, cb), lambda j: (0, j))
    return pl.pallas_call(
        body, grid=(w // cb,), in_specs=[col, col], out_specs=pl.BlockSpec((2, rows, cb), lambda j: (0, 0, j)),
        out_shape=_sds((2, rows, w), BF16), scratch_shapes=scratch, name=name, compiler_params=_cparams("parallel"),
    )(dz, dgt)


def _grp_fwd(z, w_grp, uv, scale, name):
    rows, w = z.shape
    ng, gc, _ = w_grp.shape
    tm = _row_tile(rows)

    def body(z_ref, w_ref, gt_ref, sc_ref, mx_ref, a_ref):
        mixed = jnp.dot(z_ref[...], w_ref[...], preferred_element_type=F32)
        mx_ref[...] = mixed
        a_ref[...] = (mixed * sc_ref[...] * _silu(gt_ref[...])).astype(BF16)

    blk = pl.BlockSpec((tm, gc), lambda g, i: (i, g))
    return pl.pallas_call(
        body, grid=(ng, rows // tm),
        in_specs=[blk, pl.BlockSpec((None, gc, gc), lambda g, i: (g, 0, 0)),
                  pl.BlockSpec((tm, gc), lambda g, i: (i, ng + g)), pl.BlockSpec((1, gc), lambda g, i: (0, g))],
        out_specs=[blk, blk], out_shape=[_sds((rows, w), F32), _sds((rows, w), BF16)],
        name=name, compiler_params=_cparams("parallel", "parallel"),
    )(z, w_grp, uv, scale)


def _grp_bwd(da, mixed, uv, scale, w_grp, name):
    rows, w = da.shape
    ng, gc, _ = w_grp.shape
    tm = _row_tile(rows)

    def body(da_ref, mx_ref, gt_ref, sc_ref, w_ref, dm_ref, dz_ref, dgt_ref, dsc_ref):
        i = pl.program_id(1)
        dav = da_ref[...]
        mixed = mx_ref[...]
        gt = gt_ref[...]
        sg = _silu(gt)
        sc = sc_ref[...]
        dm = (dav * sc * sg).astype(BF16)
        dm_ref[...] = dm
        dz_ref[...] = lax.dot_general(dm, w_ref[...], _DIMS["nt"], preferred_element_type=F32)
        dgt_ref[...] = (dav * mixed * sc * _dsilu(gt)).astype(BF16)
        s = jnp.sum(dav * mixed * sg, axis=0, keepdims=True)

        @pl.when(i == 0)
        def _():
            dsc_ref[...] = s

        @pl.when(i > 0)
        def _():
            dsc_ref[...] += s

    blk = pl.BlockSpec((tm, gc), lambda g, i: (i, g))
    vec = pl.BlockSpec((1, gc), lambda g, i: (0, g))
    return pl.pallas_call(
        body, grid=(ng, rows // tm),
        in_specs=[blk, blk, pl.BlockSpec((tm, gc), lambda g, i: (i, ng + g)), vec,
                  pl.BlockSpec((None, gc, gc), lambda g, i: (g, 0, 0))],
        out_specs=[blk, blk, blk, vec],
        out_shape=[_sds((rows, w), BF16), _sds((rows, w), F32), _sds((rows, w), BF16), _sds((1, w), F32)],
        name=name, compiler_params=_cparams("parallel", "arbitrary"),
    )(da, mixed, uv, scale, w_grp)


def _grp_wgrad(z, dm, ng, name, out_dtype):
    rows, w = z.shape
    gc = w // ng

    def body(z_ref, dm_ref, o_ref):
        o_ref[...] = lax.dot_general(z_ref[...], dm_ref[...], _DIMS["tn"],
                                     preferred_element_type=F32).astype(o_ref.dtype)

    blk = pl.BlockSpec((rows, gc), lambda g: (0, g))
    return pl.pallas_call(
        body, grid=(ng,), in_specs=[blk, blk], out_specs=pl.BlockSpec((None, gc, gc), lambda g: (g, 0, 0)),
        out_shape=_sds((ng, gc, gc), out_dtype), name=name, compiler_params=_cparams("parallel"),
    )(z, dm)


def _shift_rows(v, by):
    t = v.shape[0]
    pos = lax.broadcasted_iota(jnp.int32, v.shape, 0)
    rolled = pltpu.roll(v, by % t, 0)
    keep = pos >= by if by > 0 else pos < t + by
    return jnp.where(keep, rolled, 0.0)


def _conv_specs(t, w, cb):
    return [pl.BlockSpec((t, cb), (lambda j, q=q: (0, q * (w // cb) + j))) for q in range(4)]


def _conv_fwd(p4, dw, db, name):
    t = p4.shape[0]
    w = p4.shape[1] // 4
    cb = 128

    def body(bg_ref, cg_ref, v_ref, g_ref, dw_ref, db_ref, a_ref):
        tv = cg_ref[...] * v_ref[...]
        conv = (dw_ref[0:1, :] * _shift_rows(tv, 1) + dw_ref[1:2, :] * tv + dw_ref[2:3, :] * _shift_rows(tv, -1)
                + db_ref[...])
        a_ref[...] = (bg_ref[...] * conv * _silu(g_ref[...])).astype(BF16)

    return pl.pallas_call(
        body, grid=(w // cb,),
        in_specs=_conv_specs(t, w, cb) + [pl.BlockSpec((3, cb), lambda j: (0, j)), pl.BlockSpec((1, cb), lambda j: (0, j))],
        out_specs=pl.BlockSpec((t, cb), lambda j: (0, j)), out_shape=_sds((t, w), BF16),
        name=name, compiler_params=_cparams("parallel"),
    )(p4, p4, p4, p4, dw, db)


def _conv_bwd(da, p4, dw, db, name):
    t, w = da.shape
    cb = 128

    def body(da_ref, bg_ref, cg_ref, v_ref, g_ref, dw_ref, db_ref, d4_ref, ddw_ref, ddb_ref):
        cg = cg_ref[...]
        vv = v_ref[...]
        bg = bg_ref[...]
        gv = g_ref[...]
        tv = cg * vv
        tm1 = _shift_rows(tv, 1)
        tp1 = _shift_rows(tv, -1)
        w0, w1, w2 = dw_ref[0:1, :], dw_ref[1:2, :], dw_ref[2:3, :]
        conv = w0 * tm1 + w1 * tv + w2 * tp1 + db_ref[...]
        y = bg * conv
        dav = da_ref[...]
        dy = dav * _silu(gv)
        d4_ref[3] = (dav * y * _dsilu(gv)).astype(BF16)
        d4_ref[0] = (dy * conv).astype(BF16)
        dconv = dy * bg
        ddb_ref[...] = jnp.sum(dconv, axis=0, keepdims=True)
        ddw_ref[0:1, :] = jnp.sum(dconv * tm1, axis=0, keepdims=True)
        ddw_ref[1:2, :] = jnp.sum(dconv * tv, axis=0, keepdims=True)
        ddw_ref[2:3, :] = jnp.sum(dconv * tp1, axis=0, keepdims=True)
        dt = w0 * _shift_rows(dconv, -1) + w1 * dconv + w2 * _shift_rows(dconv, 1)
        d4_ref[1] = (dt * vv).astype(BF16)
        d4_ref[2] = (dt * cg).astype(BF16)

    col = pl.BlockSpec((t, cb), lambda j: (0, j))
    tap = pl.BlockSpec((3, cb), lambda j: (0, j))
    bias = pl.BlockSpec((1, cb), lambda j: (0, j))
    return pl.pallas_call(
        body, grid=(w // cb,), in_specs=[col] + _conv_specs(t, w, cb) + [tap, bias],
        out_specs=[pl.BlockSpec((4, t, cb), lambda j: (0, 0, j)), tap, bias],
        out_shape=[_sds((4, t, w), BF16), _sds((3, w), F32), _sds((1, w), F32)],
        name=name, compiler_params=_cparams("parallel"),
    )(da, p4, p4, p4, p4, dw, db)


def _attn_mask():
    qn, kn = Q_ROWS * GRID_W, K_ROWS * GRID_W
    qr, qc = np.divmod(np.arange(qn), GRID_W)
    kr, kc = np.divmod(np.arange(kn), GRID_W)
    col0 = np.clip(qc - WIN_COLS // 2, 0, GRID_W - WIN_COLS)
    col_ok = (kc[None, :] >= col0[:, None]) & (kc[None, :] < col0[:, None] + WIN_COLS)
    first = np.full(qn, PAD_ROWS)
    last = np.zeros(qn, np.int64)
    out = []
    for row0 in (first, qr, last):
        row_ok = (kr[None, :] >= row0[:, None]) & (kr[None, :] < row0[:, None] + WIN_ROWS)
        out.append(np.where(row_ok & col_ok, 0.0, NEG))
    return jnp.asarray(np.stack(out), F32)


def _rpb_onehot():
    qc, kc = np.divmod(np.arange(GRID_W * GRID_W), GRID_W)
    e = (kc - qc + WIN_COLS - 1)[None, :] == np.arange(128)[:, None]
    return jnp.asarray(e, F32)


def _mask_class(nblk):
    return lambda h, b: (jnp.where(b == 0, 0, jnp.where(b == nblk - 1, 2, 1)), 0, 0)


_KW = K_ROWS * GRID_W
_QB = Q_ROWS * GRID_W


def _attn_bias(t3_ref, t3s_ref):
    rows = []
    for qr in range(Q_ROWS):
        off = (Q_ROWS - 1 - qr) * GRID_W
        if off % 128 == 0:
            rows.append(t3_ref[:, off:off + _KW])
        else:
            rows.append(t3s_ref[:, off - GRID_W:off - GRID_W + _KW])
    return jnp.concatenate(rows, axis=0)


def _attn_specs(nh, seq, n_ctx):
    lp = seq + 2 * PAD_ROWS * GRID_W
    nblk = seq // _QB
    qspec = pl.BlockSpec((None, _QB, HEAD_DIM), lambda h, b: (h, b, 0))
    kspec = pl.BlockSpec((None, lp, HEAD_DIM), lambda h, b: (h, 0, 0))
    cspec = pl.BlockSpec((None, n_ctx, HEAD_DIM), lambda h, b: (h, 0, 0))
    tspec = pl.BlockSpec((None, GRID_W, 1024), lambda h, b: (h, 0, 0))
    mspec = pl.BlockSpec((None, _QB, _KW), _mask_class(nblk))
    lspec = pl.BlockSpec((None, _QB, 1), lambda h, b: (h, b, 0))
    return nblk, qspec, kspec, cspec, tspec, mspec, lspec


def _attn_fwd(q, kp, vp, kc, vc, t3, t3s, mask, name):
    nh, seq, _ = q.shape
    n_ctx = kc.shape[1]
    nblk, qspec, kspec, cspec, tspec, mspec, lspec = _attn_specs(nh, seq, n_ctx)

    def body(q_ref, kp_ref, vp_ref, kc_ref, vc_ref, t3_ref, t3s_ref, m_ref, o_ref, lse_ref):
        start = pl.multiple_of(pl.program_id(1) * _QB, _QB)
        qv = q_ref[...]
        kw = kp_ref[pl.ds(start, _KW), :]
        vw = vp_ref[pl.ds(start, _KW), :]
        s_loc = lax.dot_general(qv, kw, _DIMS["nt"], preferred_element_type=F32)
        s_loc = s_loc + _attn_bias(t3_ref, t3s_ref) + m_ref[...]
        s_ctx = lax.dot_general(qv, kc_ref[...], _DIMS["nt"], preferred_element_type=F32)
        mx = jnp.maximum(jnp.max(s_loc, axis=-1, keepdims=True), jnp.max(s_ctx, axis=-1, keepdims=True))
        p_loc = jnp.exp(s_loc - mx)
        p_ctx = jnp.exp(s_ctx - mx)
        den = jnp.sum(p_loc, axis=-1, keepdims=True) + jnp.sum(p_ctx, axis=-1, keepdims=True)
        inv = 1.0 / den
        o = jnp.dot((p_loc * inv).astype(BF16), vw, preferred_element_type=F32)
        o = o + jnp.dot((p_ctx * inv).astype(BF16), vc_ref[...], preferred_element_type=F32)
        o_ref[...] = o
        lse_ref[...] = mx + jnp.log(den)

    return pl.pallas_call(
        body, grid=(nh, nblk), in_specs=[qspec, kspec, kspec, cspec, cspec, tspec, tspec, mspec],
        out_specs=[qspec, lspec], out_shape=[_sds((nh, seq, HEAD_DIM), F32), _sds((nh, seq, 1), F32)],
        name=name, compiler_params=_cparams("parallel", "parallel"),
    )(q, kp, vp, kc, vc, t3, t3s, mask)


def _attn_bwd(q, kp, vp, kc, vc, t3, t3s, mask, o, do, lse, name):
    nh, seq, _ = q.shape
    n_ctx = kc.shape[1]
    lp = kp.shape[1]
    nblk, qspec, kspec, cspec, tspec, mspec, lspec = _attn_specs(nh, seq, n_ctx)

    def body(q_ref, kp_ref, vp_ref, kc_ref, vc_ref, t3_ref, t3s_ref, m_ref, o_ref, do_ref, lse_ref,
             dq_ref, dkp_ref, dvp_ref, dkc_ref, dvc_ref, dt3_ref, dt3s_ref):
        b = pl.program_id(1)
        start = pl.multiple_of(b * _QB, _QB)

        @pl.when(b == 0)
        def _():
            dkp_ref[...] = jnp.zeros(dkp_ref.shape, F32)
            dvp_ref[...] = jnp.zeros(dvp_ref.shape, F32)
            dkc_ref[...] = jnp.zeros(dkc_ref.shape, F32)
            dvc_ref[...] = jnp.zeros(dvc_ref.shape, F32)
            dt3_ref[...] = jnp.zeros(dt3_ref.shape, F32)
            dt3s_ref[...] = jnp.zeros(dt3s_ref.shape, F32)

        qv = q_ref[...]
        kw = kp_ref[pl.ds(start, _KW), :]
        vw = vp_ref[pl.ds(start, _KW), :]
        kcv = kc_ref[...]
        vcv = vc_ref[...]
        lse = lse_ref[...]
        s_loc = lax.dot_general(qv, kw, _DIMS["nt"], preferred_element_type=F32)
        p_loc = jnp.exp(s_loc + _attn_bias(t3_ref, t3s_ref) + m_ref[...] - lse)
        p_ctx = jnp.exp(lax.dot_general(qv, kcv, _DIMS["nt"], preferred_element_type=F32) - lse)
        dov = do_ref[...]
        dob = dov.astype(BF16)
        delta = jnp.sum(dov * o_ref[...], axis=-1, keepdims=True)
        ds_loc = p_loc * (lax.dot_general(dob, vw, _DIMS["nt"], preferred_element_type=F32) - delta)
        ds_ctx = p_ctx * (lax.dot_general(dob, vcv, _DIMS["nt"], preferred_element_type=F32) - delta)
        dsb_loc = ds_loc.astype(BF16)
        dsb_ctx = ds_ctx.astype(BF16)
        dq_ref[...] = (jnp.dot(dsb_loc, kw, preferred_element_type=F32)
                       + jnp.dot(dsb_ctx, kcv, preferred_element_type=F32))
        dkp_ref[pl.ds(start, _KW), :] += lax.dot_general(dsb_loc, qv, _DIMS["tn"], preferred_element_type=F32)
        dvp_ref[pl.ds(start, _KW), :] += lax.dot_general(p_loc.astype(BF16), dob, _DIMS["tn"],
                                                         preferred_element_type=F32)
        dkc_ref[...] += lax.dot_general(dsb_ctx, qv, _DIMS["tn"], preferred_element_type=F32)
        dvc_ref[...] += lax.dot_general(p_ctx.astype(BF16), dob, _DIMS["tn"], preferred_element_type=F32)
        for qr in range(Q_ROWS):
            off = (Q_ROWS - 1 - qr) * GRID_W
            piece = ds_loc[qr * GRID_W:(qr + 1) * GRID_W, :]
            if off % 128 == 0:
                dt3_ref[:, off:off + _KW] += piece
            else:
                dt3s_ref[:, off - GRID_W:off - GRID_W + _KW] += piece

    pshape = _sds((nh, lp, HEAD_DIM), F32)
    cshape = _sds((nh, n_ctx, HEAD_DIM), F32)
    tshape = _sds((nh, GRID_W, 1024), F32)
    return pl.pallas_call(
        body, grid=(nh, nblk),
        in_specs=[qspec, kspec, kspec, cspec, cspec, tspec, tspec, mspec, qspec, qspec, lspec],
        out_specs=[qspec, kspec, kspec, cspec, cspec, tspec, tspec],
        out_shape=[_sds((nh, seq, HEAD_DIM), F32), pshape, pshape, cshape, cshape, tshape, tshape],
        name=name, compiler_params=_cparams("parallel", "arbitrary"),
    )(q, kp, vp, kc, vc, t3, t3s, mask, o, do, lse)


def _gate_act(o, p4, gcol, name):
    rows, w = o.shape
    tr = ROW_BLOCK

    def body(o_ref, g_ref, a_ref):
        a_ref[...] = (o_ref[...] * _silu(g_ref[...])).astype(BF16)

    row = pl.BlockSpec((tr, w), lambda i: (i, 0))
    return pl.pallas_call(
        body, grid=(rows // tr,), in_specs=[row, pl.BlockSpec((tr, w), lambda i: (i, gcol))], out_specs=row,
        out_shape=_sds((rows, w), BF16), name=name, compiler_params=_cparams("parallel"),
    )(o, p4)


def _gate_act_bwd(da, o, p4, gcol, name):
    rows, w = o.shape
    tr = ROW_BLOCK

    def body(da_ref, o_ref, g_ref, do_ref, dg_ref):
        dav = da_ref[...]
        gv = g_ref[...]
        do_ref[...] = dav * _silu(gv)
        dg_ref[...] = (dav * o_ref[...] * _dsilu(gv)).astype(BF16)

    row = pl.BlockSpec((tr, w), lambda i: (i, 0))
    return pl.pallas_call(
        body, grid=(rows // tr,), in_specs=[row, row, pl.BlockSpec((tr, w), lambda i: (i, gcol))],
        out_specs=[row, row], out_shape=[_sds((rows, w), F32), _sds((rows, w), BF16)],
        name=name, compiler_params=_cparams("parallel"),
    )(da, o, p4)


def _to_heads(m):
    t = m.shape[0]
    return m.reshape(t, -1, HEAD_DIM).transpose(1, 0, 2)


def _from_heads(m):
    return m.transpose(1, 0, 2).reshape(m.shape[1], -1)


def _final(x, g, target, name):
    rows, d = x.shape
    tr = ROW_BLOCK
    nblk = rows // tr

    def body(x_ref, g_ref, t_ref, loss_ref, dx_ref, dg_ref, acc_ref):
        i = pl.program_id(0)
        xv = x_ref[...]
        gv = g_ref[...]
        r = lax.rsqrt(jnp.mean(xv * xv, axis=-1, keepdims=True) + EPS)
        xn = xv * r
        err = xn * gv - t_ref[...]
        dy = err * (1.0 / d)
        dxn = dy * gv
        dx_ref[...] = r * (dxn - xn * jnp.mean(dxn * xn, axis=-1, keepdims=True))
        s_g = jnp.sum(dy * xn, axis=0, keepdims=True)
        s_l = jnp.sum(jnp.mean(err * err, axis=-1, keepdims=True), axis=0, keepdims=True)

        @pl.when(i == 0)
        def _():
            dg_ref[...] = s_g
            acc_ref[...] = s_l

        @pl.when(i > 0)
        def _():
            dg_ref[...] += s_g
            acc_ref[...] += s_l

        @pl.when(i == nblk - 1)
        def _():
            loss_ref[...] = jnp.broadcast_to(0.5 * acc_ref[...], loss_ref.shape)

    row = pl.BlockSpec((tr, d), lambda i: (i, 0))
    vec = pl.BlockSpec((1, d), lambda i: (0, 0))
    return pl.pallas_call(
        body, grid=(nblk,), in_specs=[row, vec, row],
        out_specs=[pl.BlockSpec((1, 128), lambda i: (0, 0)), row, vec],
        out_shape=[_sds((1, 128), F32), _sds((rows, d), F32), _sds((1, d), F32)],
        scratch_shapes=[pltpu.VMEM((1, 1), F32)], name=name, compiler_params=_cparams("arbitrary"),
    )(x, g, target)


def _as2d(a):
    if a.ndim == 1:
        return a.reshape(-1, 128) if a.shape[0] % 128 == 0 else a.reshape(1, -1)
    return a.reshape(-1, a.shape[-1])


def _adamw(w, g, m, v, name):
    shape = w.shape
    w2, g2, m2, v2 = (_as2d(t) for t in (w, g.reshape(shape), m, v))
    rows, cols = w2.shape
    tr = 512 if rows % 512 == 0 else rows
    c1 = 1.0 - ADAM_B1 ** ADAM_STEP
    c2 = 1.0 - ADAM_B2 ** ADAM_STEP

    def body(w_ref, g_ref, m_ref, v_ref, d_ref, nm_ref, nv_ref):
        gv = g_ref[...]
        nm = ADAM_B1 * m_ref[...] + (1.0 - ADAM_B1) * gv
        nv = ADAM_B2 * v_ref[...] + (1.0 - ADAM_B2) * (gv * gv)
        nm_ref[...] = nm
        nv_ref[...] = nv
        d_ref[...] = -ADAM_LR * ((nm / c1) / (jnp.sqrt(nv / c2) + ADAM_EPS) + ADAM_WD * w_ref[...])

    blk = pl.BlockSpec((tr, cols), lambda i: (i, 0))
    outs = pl.pallas_call(
        body, grid=(rows // tr,), in_specs=[blk] * 4, out_specs=[blk] * 3,
        out_shape=[_sds((rows, cols), F32)] * 3, name=name, compiler_params=_cparams("parallel"),
    )(w2, g2, m2, v2)
    return tuple(t.reshape(shape) for t in outs)


def _sum_lead(x, name, out_dtype=F32):
    n, rows, cols = x.shape
    tr = 512 if rows % 512 == 0 else rows

    def body(x_ref, o_ref):
        acc = x_ref[0].astype(F32)
        for k in range(1, n):
            acc = acc + x_ref[k].astype(F32)
        o_ref[...] = acc.astype(out_dtype)

    return pl.pallas_call(
        body, grid=(rows // tr,), in_specs=[pl.BlockSpec((n, tr, cols), lambda i: (0, i, 0))],
        out_specs=pl.BlockSpec((tr, cols), lambda i: (i, 0)), out_shape=_sds((rows, cols), out_dtype),
        name=name, compiler_params=_cparams("parallel"),
    )(x)


def _add2(a, b, name, out_dtype):
    shape = a.shape
    a2, b2 = _as2d(a), _as2d(b)
    rows, cols = a2.shape
    tr = 512 if rows % 512 == 0 else rows

    def body(a_ref, b_ref, o_ref):
        o_ref[...] = (a_ref[...].astype(F32) + b_ref[...].astype(F32)).astype(out_dtype)

    blk = pl.BlockSpec((tr, cols), lambda i: (i, 0))
    return pl.pallas_call(
        body, grid=(rows // tr,), in_specs=[blk, blk], out_specs=blk, out_shape=_sds((rows, cols), out_dtype),
        name=name, compiler_params=_cparams("parallel"),
    )(a2, b2).reshape(shape)


_NO_CTX = 1 << 30


def _seg_vecs(mod_l, which, nseg):
    return mod_l[:nseg, which][:, None, :]


def _norm_grads(dshift, dgeff, dgate, g, scale):
    nseg, _, d = dshift.shape
    dmod = jnp.stack([dshift[:, 0], dgeff[:, 0] * g, dgate[:, 0]], axis=1)
    if nseg == 1:
        dmod = jnp.concatenate([dmod, jnp.zeros((1, 3, d), F32)], axis=0)
    dg = jnp.sum(dgeff[:, 0] * (1.0 + scale[:, 0]), axis=0)
    return dmod, dg


def _pool_layer(xin, g, mod_l, w_in, w_grp, w_out, pscale, nx, tag):
    rows = xin.shape[0]
    nseg = 2 if rows > nx else 1
    nxb = nx // ROW_BLOCK if nseg == 2 else _NO_CTX
    shift, scale, gate = (_seg_vecs(mod_l, k, nseg) for k in range(3))
    h, r = _normmod_fwd(xin, g, scale, shift, nxb, f"norm_fwd_{tag}")
    uv = _mm_nn(h, w_in, f"w_in_fwd_{tag}")
    z = _pool_fwd(uv, nx, f"pool_fwd_{tag}")
    mixed, a = _grp_fwd(z, w_grp, uv, pscale, f"grp_fwd_{tag}")
    yx, xout = _mm_out_resid(a, w_out, xin, gate, nxb, f"w_out_fwd_{tag}")

    def backward(dxo):
        dyx, dgate = _gate_bwd(dxo, yx, gate, nxb, f"gate_bwd_{tag}")
        da = _mm_nt(dyx, w_out, f"w_out_bwd_{tag}")
        gw_out = _mm_tn(a, dyx, f"w_out_grad_{tag}", BF16)
        dm, dz, dgt, dscale = _grp_bwd(da, mixed, uv, pscale, w_grp, f"grp_bwd_{tag}")
        gw_grp = _grp_wgrad(z, dm, w_grp.shape[0], f"grp_grad_{tag}", BF16)
        duv = _pool_bwd(dz, dgt, nx, f"pool_bwd_{tag}")
        dh = _mm_nt_parts(duv, w_in, f"w_in_bwd_{tag}")
        gw_in = _mm_tn_parts(h, duv, f"w_in_grad_{tag}", BF16)
        dx, dshift, dgeff = _normmod_bwd(dh, xin, r, g, scale, dxo, nxb, f"norm_bwd_{tag}")
        dmod, dg = _norm_grads(dshift, dgeff, dgate, g[0], scale)
        return dx, dmod, dg, dict(w_in=gw_in, w_grp=gw_grp, w_out=gw_out, scale=dscale)

    return xout, backward


def _rpb_tables(rpb, onehot):
    nh, na, nb = rpb.shape
    flat = jnp.pad(rpb.reshape(nh * na, nb), ((0, 0), (0, 128 - nb)))
    t1 = _matmul(
        flat, onehot, mode="nn", grid=(1, 4), exact=True,
        a_spec=pl.BlockSpec((nh * na, 128), lambda i, j: (0, 0)), b_spec=pl.BlockSpec((128, 1024), lambda i, j: (0, j)),
        out_shapes=[_sds((nh * na, GRID_W * GRID_W), F32)], out_specs=[pl.BlockSpec((nh * na, 1024), lambda i, j: (0, j))],
        name="rpb_table")[0]
    t3 = t1.reshape(nh, na, GRID_W, GRID_W).transpose(0, 2, 1, 3).reshape(nh, GRID_W, na * GRID_W)
    pad = 1024 - na * GRID_W
    t3s = jnp.pad(t3[:, :, GRID_W:], ((0, 0), (0, 0), (0, pad + GRID_W)))
    return jnp.pad(t3, ((0, 0), (0, 0), (0, pad))), t3s


def _rpb_grad(dt3, dt3s, onehot, nh, na, nb):
    def fold(t, name):
        flat = t[:, :, :na * GRID_W].reshape(nh, GRID_W, na, GRID_W).transpose(0, 2, 1, 3).reshape(nh * na, -1)
        out = _matmul(
            flat, onehot, mode="nt", grid=(1, 4), nk=4, acc_shape=(nh * na, 128), exact=True,
            a_spec=pl.BlockSpec((nh * na, 1024), lambda i, k: (0, k)), b_spec=pl.BlockSpec((128, 1024), lambda i, k: (0, k)),
            out_shapes=[_sds((nh * na, 128), F32)], out_specs=[pl.BlockSpec((nh * na, 128), lambda i, k: (0, 0))],
            name=name)[0]
        return out[:, :nb].reshape(nh, na, nb)

    direct = fold(dt3, "rpb_grad_a")
    shifted = fold(dt3s, "rpb_grad_b")
    return direct + jnp.pad(shifted[:, :-1], ((0, 0), (1, 0), (0, 0)))


def _na_layer(xc, g, mod_l, w_in, rpb, w_out, nx, consts):
    rows, d = xc.shape
    w = w_out.shape[0]
    n_ctx = rows - nx
    nxb = nx // ROW_BLOCK
    mask, onehot = consts
    shift, scale = _seg_vecs(mod_l, 0, 2), _seg_vecs(mod_l, 1, 2)
    gate = _seg_vecs(mod_l, 2, 1)
    h, r = _normmod_fwd(xc, g, scale, shift, nxb, "norm_fwd_na")
    p4 = _mm_nn(h, w_in, "w_in_fwd_na")
    pad = PAD_ROWS * GRID_W

    def heads(col, lo, n):
        return _to_heads(lax.slice(p4, (lo, col * w), (lo + n, (col + 1) * w)))

    q = (heads(0, 0, nx) * HEAD_DIM ** -0.5).astype(BF16)
    kp = jnp.pad(heads(1, 0, nx).astype(BF16), ((0, 0), (pad, pad), (0, 0)))
    vp = jnp.pad(heads(2, 0, nx).astype(BF16), ((0, 0), (pad, pad), (0, 0)))
    kc = heads(1, nx, n_ctx).astype(BF16)
    vc = heads(2, nx, n_ctx).astype(BF16)
    t3, t3s = _rpb_tables(rpb, onehot)
    o, lse = _attn_fwd(q, kp, vp, kc, vc, t3, t3s, mask, "attn_fwd")
    o2 = _from_heads(o)
    a = _gate_act(o2, p4, 3, "attn_gate_fwd")
    yx, xout = _mm_out_resid(a, w_out, xc, gate, _NO_CTX, "w_out_fwd_na")

    def backward(dxo):
        dyx, dgate = _gate_bwd(dxo, yx, gate, _NO_CTX, "gate_bwd_na")
        da = _mm_nt(dyx, w_out, "w_out_bwd_na")
        gw_out = _mm_tn(a, dyx, "w_out_grad_na", BF16)
        do2, dg4 = _gate_act_bwd(da, o2, p4, 3, "attn_gate_bwd")
        dq, dkp, dvp, dkc, dvc, dt3, dt3s = _attn_bwd(q, kp, vp, kc, vc, t3, t3s, mask, o, _to_heads(do2), lse,
                                                      "attn_bwd")
        zeros = jnp.zeros((n_ctx, w), BF16)
        d4 = jnp.stack([
            jnp.concatenate([(_from_heads(dq) * HEAD_DIM ** -0.5).astype(BF16), zeros], axis=0),
            jnp.concatenate([_from_heads(dkp[:, pad:pad + nx]), _from_heads(dkc)], axis=0).astype(BF16),
            jnp.concatenate([_from_heads(dvp[:, pad:pad + nx]), _from_heads(dvc)], axis=0).astype(BF16),
            jnp.concatenate([dg4, zeros], axis=0)])
        dh = _mm_nt_parts(d4, w_in, "w_in_bwd_na")
        gw_in = _mm_tn_parts(h, d4, "w_in_grad_na", BF16)
        dx, dshift, dgeff = _normmod_bwd(dh, xc, r, g, scale, dxo, nxb, "norm_bwd_na")
        dgate2 = jnp.concatenate([dgate, jnp.zeros_like(dgate)], axis=0)
        dmod, dg = _norm_grads(dshift, dgeff, dgate2, g[0], scale)
        drpb = _rpb_grad(dt3, dt3s, onehot, *rpb.shape)
        return dx, dmod, dg, dict(w_in=gw_in, w_out=gw_out, rpb=drpb)

    return xout, backward


def _conv_layer(xin, g, mod_l, w_in, dw, db, w_out):
    shift, scale, gate = (_seg_vecs(mod_l, k, 1) for k in range(3))
    h, r = _normmod_fwd(xin, g, scale, shift, _NO_CTX, "norm_fwd_conv")
    p4 = _mm_nn(h, w_in, "w_in_fwd_conv")
    a = _conv_fwd(p4, dw, db, "conv_fwd")
    yx, xout = _mm_out_resid(a, w_out, xin, gate, _NO_CTX, "w_out_fwd_conv")

    def backward(dxo):
        dyx, dgate = _gate_bwd(dxo, yx, gate, _NO_CTX, "gate_bwd_conv")
        da = _mm_nt(dyx, w_out, "w_out_bwd_conv")
        gw_out = _mm_tn(a, dyx, "w_out_grad_conv", BF16)
        d4, ddw, ddb = _conv_bwd(da, p4, dw, db, "conv_bwd")
        dh = _mm_nt_parts(d4, w_in, "w_in_bwd_conv")
        gw_in = _mm_tn_parts(h, d4, "w_in_grad_conv", BF16)
        dx, dshift, dgeff = _normmod_bwd(dh, xin, r, g, scale, dxo, _NO_CTX, "norm_bwd_conv")
        dmod, dg = _norm_grads(dshift, dgeff, dgate, g[0], scale)
        return dx, dmod, dg, dict(w_in=gw_in, w_out=gw_out, dw=ddw, db=ddb)

    return xout, backward


def _example_step(x, ctx, target, mod, norm_g, final_g, wts):
    nx = x.shape[0]
    consts = (_attn_mask(), _rpb_onehot())
    g_rows = [norm_g[i:i + 1] for i in range(4)]
    xc0 = jnp.concatenate([x, ctx], axis=0)
    xc1, bwd0 = _pool_layer(xc0, g_rows[0], mod[0], wts["pool_w_in"][0], wts["pool_w_grp"][0], wts["pool_w_out"][0],
                            wts["pool_scale"][0:1], nx, "p0")
    x2, bwd1 = _na_layer(xc1, g_rows[1], mod[1], wts["na_w_in"], wts["na_rpb"], wts["na_w_out"], nx, consts)
    x3, bwd2 = _conv_layer(x2, g_rows[2], mod[2], wts["conv_w_in"], wts["conv_dw"], wts["conv_db"], wts["conv_w_out"])
    x4, bwd3 = _pool_layer(x3, g_rows[3], mod[3], wts["pool_w_in"][1], wts["pool_w_grp"][1], wts["pool_w_out"][1],
                           wts["pool_scale"][1:2], nx, "p3")
    loss, dx4, dfinal_g = _final(x4, final_g, target, "loss_head")
    dx3, dmod3, dg3, gr3 = bwd3(dx4)
    dx2, dmod2, dg2, gr2 = bwd2(dx3)
    dxc1, dmod1, dg1, gr1 = bwd1(dx2)
    dxc0, dmod0, dg0, gr0 = bwd0(dxc1)
    return dict(
        loss=loss, grad_x=dxc0[:nx], dmod=jnp.stack([dmod0, dmod1, dmod2, dmod3]),
        dnorm_g=jnp.stack([dg0, dg1, dg2, dg3]), dfinal_g=dfinal_g, layers=(gr0, gr1, gr2, gr3))


_AXES = ("x", "y", "c")
_CHIP_FLIPS = ((1, 0), (0, 1), (1, 1))


def _position():
    return tuple(lax.axis_index(a) for a in _AXES)


def _flipped(pos, flip):
    return tuple(1 - p if f else p for p, f in zip(pos, flip))


def _all_gather(v, axes, name):
    flips = [f for f in np.ndindex(2, 2, 2) if any(f) and all(a in axes or not b for a, b in zip(_AXES, f))]
    n = len(flips) + 1

    def body(v_ref, o_ref, send_sems, recv_sems, local_sem):
        pos = _position()
        slot = 0
        for a, p in zip(_AXES, pos):
            if a in axes:
                slot = 2 * slot + p
        local = pltpu.make_async_copy(v_ref, o_ref.at[slot], local_sem)
        local.start()
        copies = []
        for k, flip in enumerate(flips):
            cp = pltpu.make_async_remote_copy(v_ref, o_ref.at[slot], send_sems.at[k], recv_sems.at[k],
                                              device_id=_flipped(pos, flip), device_id_type=MESH)
            cp.start()
            copies.append(cp)
        for cp in copies:
            cp.wait()
        local.wait()

    return pl.pallas_call(
        body, in_specs=[HBM_SPEC], out_specs=HBM_SPEC, out_shape=_sds((n,) + v.shape, v.dtype),
        scratch_shapes=[pltpu.SemaphoreType.DMA((n - 1,)), pltpu.SemaphoreType.DMA((n - 1,)), pltpu.SemaphoreType.DMA(())],
        name=name,
    )(v)


class _Item:
    def __init__(self, key, layer, shape, shard_axis, half_axis):
        self.key, self.layer, self.shape = key, layer, tuple(shape)
        self.shard_axis, self.half_axis = shard_axis, half_axis
        self.shard = shape[shard_axis] // 4
        self.half = shape[half_axis] // 2

    def sized(self, shard=False, half=False):
        s = list(self.shape)
        if shard:
            s[self.shard_axis] = self.shard
        if half:
            s[self.half_axis] = self.half
        return tuple(s)

    def window(self, ref, chip=None, half=None):
        idx = [slice(None)] * len(self.shape)
        if chip is not None:
            idx[self.shard_axis] = pl.ds(chip * self.shard, self.shard)
        if half is not None:
            idx[self.half_axis] = pl.ds(half * self.half, self.half)
        return ref.at[tuple(idx)]


def _items(d, w):
    out = []
    for j in range(2):
        out += [_Item("pool_w_in", j, (d, 2 * w), 1, 0), _Item("pool_w_grp", j, (4, w // 4, w // 4), 1, 0),
                _Item("pool_w_out", j, (w, d), 0, 1)]
    out += [_Item("na_w_in", 0, (d, 4 * w), 1, 0), _Item("na_w_out", 0, (w, d), 0, 1),
            _Item("conv_w_in", 0, (d, 4 * w), 1, 0), _Item("conv_w_out", 0, (w, d), 0, 1)]
    return out


def _sem_scratch(n):
    return [pltpu.SemaphoreType.DMA((n,)) for _ in range(5)]


def _gather_weights(shards, items):
    n = len(items)

    def body(*refs):
        src, dst = refs[:n], refs[n:2 * n]
        send_a, recv_a, send_b, recv_b, local_sems = refs[2 * n:]
        x, y, c = _position()
        chip = 2 * x + y
        peers = [_flipped((x, y), f) for f in _CHIP_FLIPS]
        local, sent = [], []
        for i, it in enumerate(items):
            cp = pltpu.make_async_copy(src[i], it.window(dst[i], chip=chip), local_sems.at[i])
            cp.start()
            local.append(cp)
            for k, (px, py) in enumerate(peers):
                cp = pltpu.make_async_remote_copy(
                    it.window(src[i], half=c), it.window(dst[i], chip=chip, half=c),
                    send_a.at[3 * i + k], recv_a.at[3 * i + k], device_id=(px, py, c), device_id_type=MESH)
                cp.start()
                sent.append(cp)
        for i, it in enumerate(items):
            for k, (px, py) in enumerate(peers):
                got = it.window(dst[i], chip=2 * px + py, half=c)
                pltpu.make_async_remote_copy(got, got, send_a.at[3 * i + k], recv_a.at[3 * i + k],
                                             device_id=(px, py, c), device_id_type=MESH).wait_recv()
                cp = pltpu.make_async_remote_copy(got, got, send_b.at[3 * i + k], recv_b.at[3 * i + k],
                                                  device_id=(x, y, 1 - c), device_id_type=MESH)
                cp.start()
                sent.append(cp)
        for i, it in enumerate(items):
            for k, (px, py) in enumerate(peers):
                other = it.window(dst[i], chip=2 * px + py, half=1 - c)
                pltpu.make_async_remote_copy(other, other, send_b.at[3 * i + k], recv_b.at[3 * i + k],
                                             device_id=(x, y, 1 - c), device_id_type=MESH).wait_recv()
        for cp in sent:
            cp.wait_send()
        for cp in local:
            cp.wait()

    return pl.pallas_call(
        body, in_specs=[HBM_SPEC] * n, out_specs=[HBM_SPEC] * n, out_shape=[_sds(it.shape, BF16) for it in items],
        scratch_shapes=_sem_scratch(3 * n), name="gather_weights",
    )(*shards)


def _pair_exchange(grads, items):
    n = len(items)

    def body(*refs):
        src, own, got = refs[:n], refs[n:2 * n], refs[2 * n:3 * n]
        send_sems, recv_sems, local_sems = refs[3 * n:]
        x, y, c = _position()
        copies = []
        for i, it in enumerate(items):
            lc = pltpu.make_async_copy(it.window(src[i], half=c), own[i], local_sems.at[i])
            lc.start()
            cp = pltpu.make_async_remote_copy(it.window(src[i], half=1 - c), got[i], send_sems.at[i], recv_sems.at[i],
                                              device_id=(x, y, 1 - c), device_id_type=MESH)
            cp.start()
            copies += [lc, cp]
        for cp in copies:
            cp.wait()

    half = [_sds(it.sized(half=True), BF16) for it in items]
    outs = pl.pallas_call(
        body, in_specs=[HBM_SPEC] * n, out_specs=[HBM_SPEC] * (2 * n), out_shape=half + half,
        scratch_shapes=[pltpu.SemaphoreType.DMA((n,)) for _ in range(3)], name="pair_exchange",
    )(*grads)
    return outs[:n], outs[n:]


def _chip_exchange(partials, items):
    n = len(items)

    def body(*refs):
        src, dst = refs[:n], refs[n:2 * n]
        send_sems, recv_sems, local_sems = refs[2 * n:]
        x, y, c = _position()
        chip = 2 * x + y
        peers = [_flipped((x, y), f) for f in _CHIP_FLIPS]
        copies = []
        for i, it in enumerate(items):
            lc = pltpu.make_async_copy(it.window(src[i], chip=chip), dst[i].at[chip], local_sems.at[i])
            lc.start()
            copies.append(lc)
            for k, (px, py) in enumerate(peers):
                cp = pltpu.make_async_remote_copy(
                    it.window(src[i], chip=2 * px + py), dst[i].at[chip], send_sems.at[3 * i + k],
                    recv_sems.at[3 * i + k], device_id=(px, py, c), device_id_type=MESH)
                cp.start()
                copies.append(cp)
        for cp in copies:
            cp.wait()

    return pl.pallas_call(
        body, in_specs=[HBM_SPEC] * n, out_specs=[HBM_SPEC] * n,
        out_shape=[_sds((4,) + it.sized(shard=True, half=True), BF16) for it in items],
        scratch_shapes=[pltpu.SemaphoreType.DMA((3 * n,)), pltpu.SemaphoreType.DMA((3 * n,)),
                        pltpu.SemaphoreType.DMA((n,))], name="chip_exchange",
    )(*partials)


_GRAD_KEYS = ("pool_w_in", "pool_w_grp", "pool_w_out", "na_w_in", "na_w_out", "conv_w_in", "conv_w_out")


def _pair_assemble(reduced, items, lead):
    n = len(items)
    keys = list(_GRAD_KEYS)

    def body(*refs):
        src, dst = refs[:n], refs[n:n + len(keys)]
        send_sems, recv_sems, local_sems = refs[n + len(keys):]
        x, y, c = _position()
        copies = []
        for i, it in enumerate(items):
            place = it.window(dst[keys.index(it.key)].at[it.layer], half=c)
            lc = pltpu.make_async_copy(src[i], place, local_sems.at[i])
            lc.start()
            cp = pltpu.make_async_remote_copy(src[i], place, send_sems.at[i], recv_sems.at[i],
                                              device_id=(x, y, 1 - c), device_id_type=MESH)
            cp.start()
            copies += [lc, cp]
        for cp in copies:
            cp.wait()

    first = {it.key: it for it in reversed(items)}
    return pl.pallas_call(
        body, in_specs=[HBM_SPEC] * n, out_specs=[HBM_SPEC] * len(keys),
        out_shape=[_sds((lead[k],) + first[k].sized(shard=True), F32) for k in keys],
        scratch_shapes=[pltpu.SemaphoreType.DMA((n,)) for _ in range(3)], name="pair_assemble",
    )(*reduced)


_WEIGHTS = ("c_ctx", "norm_g", "ada_w", "ada_b", "pool_w_in", "pool_w_grp", "pool_scale", "pool_w_out", "na_w_in",
            "na_rpb", "na_w_out", "conv_w_in", "conv_dw", "conv_db", "conv_w_out", "final_g")
_COND_ROWS = 16


def _modulations(cond, ada_w, ada_b_cols):
    nl, d, n = ada_w.shape
    return _matmul(
        cond, ada_w, mode="nn", grid=(nl, 1), a_silu=True, epilogue="bias",
        a_spec=pl.BlockSpec((_COND_ROWS, d), lambda i, j: (0, 0)), b_spec=pl.BlockSpec((None, d, n), lambda i, j: (i, 0, 0)),
        extra=(ada_b_cols,), extra_specs=(pl.BlockSpec((None, 1, n), lambda i, j: (i, 0, 0)),),
        out_shapes=[_sds((nl, _COND_ROWS, n), F32)], out_specs=[pl.BlockSpec((None, _COND_ROWS, n), lambda i, j: (i, 0, 0))],
        name="modulations")[0]


def _ada_w_grad(cond, dm_cols):
    d = cond.shape[1]
    nl, _, n = dm_cols.shape
    return _matmul(
        cond, dm_cols, mode="tn", grid=(nl, 1), a_silu=True,
        a_spec=pl.BlockSpec((_COND_ROWS, d), lambda i, j: (0, 0)), b_spec=pl.BlockSpec((None, _COND_ROWS, n), lambda i, j: (i, 0, 0)),
        out_shapes=[_sds((nl, d, n), F32)], out_specs=[pl.BlockSpec((None, d, n), lambda i, j: (i, 0, 0))],
        name="ada_w_grad")[0]


def _cond_grad(dm_cols, ada_w):
    nl, d, n = ada_w.shape
    return _matmul(
        dm_cols, ada_w, mode="nt", grid=(1, nl), nk=nl, acc_shape=(_COND_ROWS, d),
        a_spec=pl.BlockSpec((None, _COND_ROWS, n), lambda i, q: (q, 0, 0)), b_spec=pl.BlockSpec((None, d, n), lambda i, q: (q, 0, 0)),
        out_shapes=[_sds((_COND_ROWS, d), F32)], out_specs=[pl.BlockSpec((_COND_ROWS, d), lambda i, q: (0, 0))],
        name="cond_grad")[0]


def _pack(parts):
    flat = [p.reshape(-1) for p in parts]
    sizes = [f.shape[0] for f in flat]
    total = sum(sizes)
    rows = -(-total // 1024) * 8
    packed = jnp.concatenate(flat + [jnp.zeros((rows * 128 - total,), F32)]).reshape(rows, 128)
    offs = np.concatenate([[0], np.cumsum(sizes)])[:-1]
    return packed, [(int(o), p.shape) for o, p in zip(offs, parts)]


def _unpack(flat, layout, k):
    off, shape = layout[k]
    return flat[..., off:off + int(np.prod(shape))].reshape(flat.shape[:-1] + tuple(shape))


def kernel(x, c, ctx, c_ctx, norm_g, ada_w, ada_b, pool_w_in, pool_w_grp, pool_scale, pool_w_out, na_w_in, na_rpb, na_w_out, conv_w_in, conv_dw, conv_db, conv_w_out, final_g, loss_target, m_c_ctx, m_norm_g, m_ada_w, m_ada_b, m_pool_w_in, m_pool_w_grp, m_pool_scale, m_pool_w_out, m_na_w_in, m_na_rpb, m_na_w_out, m_conv_w_in, m_conv_dw, m_conv_db, m_conv_w_out, m_final_g, v_c_ctx, v_norm_g, v_ada_w, v_ada_b, v_pool_w_in, v_pool_w_grp, v_pool_scale, v_pool_w_out, v_na_w_in, v_na_rpb, v_na_w_out, v_conv_w_in, v_conv_dw, v_conv_db, v_conv_w_out, v_final_g):
    params = dict(c_ctx=c_ctx, norm_g=norm_g, ada_w=ada_w, ada_b=ada_b, pool_w_in=pool_w_in, pool_w_grp=pool_w_grp,
                  pool_scale=pool_scale, pool_w_out=pool_w_out, na_w_in=na_w_in, na_rpb=na_rpb, na_w_out=na_w_out,
                  conv_w_in=conv_w_in, conv_dw=conv_dw, conv_db=conv_db, conv_w_out=conv_w_out, final_g=final_g)
    mom1 = dict(c_ctx=m_c_ctx, norm_g=m_norm_g, ada_w=m_ada_w, ada_b=m_ada_b, pool_w_in=m_pool_w_in,
                pool_w_grp=m_pool_w_grp, pool_scale=m_pool_scale, pool_w_out=m_pool_w_out, na_w_in=m_na_w_in,
                na_rpb=m_na_rpb, na_w_out=m_na_w_out, conv_w_in=m_conv_w_in, conv_dw=m_conv_dw, conv_db=m_conv_db,
                conv_w_out=m_conv_w_out, final_g=m_final_g)
    mom2 = dict(c_ctx=v_c_ctx, norm_g=v_norm_g, ada_w=v_ada_w, ada_b=v_ada_b, pool_w_in=v_pool_w_in,
                pool_w_grp=v_pool_w_grp, pool_scale=v_pool_scale, pool_w_out=v_pool_w_out, na_w_in=v_na_w_in,
                na_rpb=v_na_rpb, na_w_out=v_na_w_out, conv_w_in=v_conv_w_in, conv_dw=v_conv_dw, conv_db=v_conv_db,
                conv_w_out=v_conv_w_out, final_g=v_final_g)
    d = x.shape[-1]
    w = na_w_out.shape[1] * 4
    xi, yi, ci = _position()
    chip = 2 * xi + yi
    dev = 2 * chip + ci
    n_ada = ada_w.shape[-1]

    def chip_cols(a, size):
        return lax.dynamic_slice_in_dim(a, chip * size, size, axis=a.ndim - 1)

    conds = _all_gather(c.reshape(8, d // 8), _AXES, "gather_cond").reshape(8, d)
    cond = jnp.concatenate([conds, c_ctx[None], jnp.zeros((_COND_ROWS - 9, d), F32)], axis=0)
    mod_cols = _modulations(cond, ada_w, chip_cols(ada_b, n_ada)[:, None, :])
    mod_all = _all_gather(mod_cols, ("x", "y"), "gather_mod")
    mod_all = mod_all.transpose(1, 2, 0, 3).reshape(4, _COND_ROWS, 3, d)
    mod = jnp.stack([lax.dynamic_index_in_dim(mod_all, dev, axis=1, keepdims=False), mod_all[:, 8]], axis=1)

    items = _items(d, w)
    shards = [params[it.key][it.layer].astype(BF16) for it in items]
    full = dict()
    for it, mat in zip(items, _gather_weights(shards, items)):
        full.setdefault(it.key, []).append(mat)
    small = _all_gather(_pack([pool_scale, conv_dw, conv_db])[0], ("x", "y"), "gather_small")
    small_layout = _pack([pool_scale, conv_dw, conv_db])[1]
    small = small.reshape(4, -1)

    def whole(k):
        parts = _unpack(small, small_layout, k)
        return jnp.moveaxis(parts, 0, -2).reshape(parts.shape[1:-1] + (-1,))

    wts = dict(pool_w_in=full["pool_w_in"], pool_w_grp=full["pool_w_grp"], pool_w_out=full["pool_w_out"],
               na_w_in=full["na_w_in"][0], na_w_out=full["na_w_out"][0], conv_w_in=full["conv_w_in"][0],
               conv_w_out=full["conv_w_out"][0], pool_scale=whole(0), na_rpb=na_rpb[0], conv_dw=whole(1)[0],
               conv_db=whole(2))
    res = _example_step(x[0], ctx[0], loss_target[0], mod, norm_g, final_g[None], wts)

    g0, g1, g2, g3 = res["layers"]
    by_item = {("pool_w_in", 0): g0["w_in"], ("pool_w_grp", 0): g0["w_grp"], ("pool_w_out", 0): g0["w_out"],
               ("pool_w_in", 1): g3["w_in"], ("pool_w_grp", 1): g3["w_grp"], ("pool_w_out", 1): g3["w_out"],
               ("na_w_in", 0): g1["w_in"], ("na_w_out", 0): g1["w_out"],
               ("conv_w_in", 0): g2["w_in"], ("conv_w_out", 0): g2["w_out"]}
    own, got = _pair_exchange([by_item[(it.key, it.layer)] for it in items], items)
    pair = [_add2(a, b, f"pair_sum_{i}", BF16) for i, (a, b) in enumerate(zip(own, got))]
    slots = _chip_exchange(pair, items)
    reduced = [_sum_lead(s.reshape(4, -1, s.shape[-1]), f"chip_sum_{i}").reshape(s.shape[1:])
               for i, s in enumerate(slots)]
    lead = {k: params[k].shape[0] for k in _GRAD_KEYS}
    grads = dict(zip(_GRAD_KEYS, _pair_assemble(reduced, items, lead)))

    packed, layout = _pack([res["dfinal_g"], res["dnorm_g"], res["dmod"], g1["rpb"],
                            jnp.concatenate([g0["scale"], g3["scale"]], axis=0), g2["dw"], g2["db"]])
    every = _all_gather(packed, _AXES, "gather_vec_grads")
    total = _sum_lead(every, "sum_vec_grads").reshape(-1)
    every = every.reshape(8, -1)
    grads["final_g"] = _unpack(total, layout, 0).reshape(final_g.shape)
    grads["norm_g"] = _unpack(total, layout, 1)
    grads["na_rpb"] = _unpack(total, layout, 3)[None]
    grads["pool_scale"] = chip_cols(_unpack(total, layout, 4), pool_scale.shape[-1])
    grads["conv_dw"] = chip_cols(_unpack(total, layout, 5), conv_dw.shape[-1])[None]
    grads["conv_db"] = chip_cols(_unpack(total, layout, 6), conv_db.shape[-1])
    dmod_sum = _unpack(total, layout, 2).reshape(4, 2, 3 * d)
    dmod_each = _unpack(every, layout, 2).reshape(8, 4, 2, 3 * d)
    grads["ada_b"] = dmod_sum[:, 0] + dmod_sum[:, 1]
    dm = jnp.concatenate([dmod_each[:, :, 0].transpose(1, 0, 2), dmod_sum[:, 1][:, None],
                          jnp.zeros((4, _COND_ROWS - 9, 3 * d), F32)], axis=1)
    dm_cols = chip_cols(dm, n_ada)
    grads["ada_w"] = _ada_w_grad(cond, dm_cols)
    dcond = _cond_grad(dm_cols, ada_w)[8].reshape(8, d // 8)
    dcond = _sum_lead(_all_gather(dcond, ("x", "y"), "gather_cond_grad"), "sum_cond_grad").reshape(d)
    grads["c_ctx"] = dcond * _dsilu(c_ctx)

    outs = [[], [], []]
    for k in _WEIGHTS:
        for lst, val in zip(outs, _adamw(params[k], grads[k], mom1[k], mom2[k], f"adamw_{k}")):
            lst.append(val)
    loss = lax.psum(res["loss"][0, 0], _AXES)
    return (loss, res["grad_x"][None], *[grads[k].reshape(params[k].shape) for k in _WEIGHTS],
            *outs[0], *outs[1], *outs[2])
```

```python
import functools

import numpy as np
import jax
import jax.numpy as jnp
from jax import lax
from jax.experimental import pallas as pl
from jax.experimental.pallas import tpu as pltpu

F32 = jnp.float32
BF16 = jnp.bfloat16

EPS = 1e-6
GRID_W = 64
HEAD_DIM = 64
WIN_ROWS = 8
WIN_COLS = 16
POOL_WINDOWS = (2, 4, 8, 16)
Q_ROWS = 4
K_ROWS = 12
PAD_ROWS = 4
NEG = -1e30

ADAM_LR = 0.001
ADAM_B1 = 0.9
ADAM_B2 = 0.999
ADAM_EPS = 1e-08
ADAM_WD = 0.01
ADAM_STEP = 10

ROW_BLOCK = 256
VMEM_LIMIT = 56 * 1024 * 1024

MESH = pl.DeviceIdType.MESH
HBM_SPEC = pl.BlockSpec(memory_space=pltpu.HBM)


def _cparams(*sem):
    return pltpu.CompilerParams(dimension_semantics=sem or None, vmem_limit_bytes=VMEM_LIMIT)


def _sds(shape, dtype):
    return jax.ShapeDtypeStruct(tuple(shape), dtype)


def _sigmoid(x):
    return 1.0 / (1.0 + jnp.exp(-x))


def _silu(x):
    return x * _sigmoid(x)


def _dsilu(x):
    s = _sigmoid(x)
    return s * (1.0 + x * (1.0 - s))


_DIMS = {
    "nn": (((1,), (0,)), ((), ())),
    "nt": (((1,), (1,)), ((), ())),
    "tn": (((0,), (0,)), ((), ())),
}


def _matmul(a, b, *, mode, grid, a_spec, b_spec, out_shapes, out_specs, name, nk=1,
            a_silu=False, exact=False, epilogue=None, extra=(), extra_specs=(), acc_shape=None):
    n_extra = len(extra)
    n_out = len(out_shapes)

    def body(*refs):
        a_ref, b_ref = refs[:2]
        ex = refs[2:2 + n_extra]
        outs = refs[2 + n_extra:2 + n_extra + n_out]
        av = a_ref[...]
        bv = b_ref[...]
        if a_silu:
            av = _silu(av.astype(F32))
        if exact:
            prod = lax.dot_general(av.astype(F32), bv.astype(F32), _DIMS[mode],
                                   precision=lax.Precision.HIGHEST, preferred_element_type=F32)
        else:
            prod = lax.dot_general(av.astype(BF16), bv.astype(BF16), _DIMS[mode], preferred_element_type=F32)

        def finish(res):
            if epilogue is None:
                outs[0][...] = res.astype(outs[0].dtype)
            elif epilogue == "bias":
                outs[0][...] = (res + ex[0][...]).astype(outs[0].dtype)
            else:
                outs[0][...] = res.astype(outs[0].dtype)
                outs[1][...] = ex[0][...] + ex[1][...] * res

        if nk == 1:
            finish(prod)
        else:
            acc = refs[-1]
            k = pl.program_id(len(grid) - 1)

            @pl.when(k == 0)
            def _():
                acc[...] = prod

            @pl.when(k > 0)
            def _():
                acc[...] += prod

            @pl.when(k == nk - 1)
            def _():
                finish(acc[...])

    scratch = [pltpu.VMEM(acc_shape, F32)] if nk > 1 else []
    sem = ("parallel",) * (len(grid) - 1) + ("arbitrary",)
    return pl.pallas_call(
        body, grid=grid, in_specs=[a_spec, b_spec, *extra_specs], out_specs=list(out_specs),
        out_shape=list(out_shapes), scratch_shapes=scratch, name=name, compiler_params=_cparams(*sem),
    )(a, b, *extra)


def _row_tile(rows):
    for t in (768, 512, 256):
        if rows % t == 0:
            return t
    return rows


def _mm_nn(a, b, name, out_dtype=F32, tn=1024):
    m, k = a.shape
    n = b.shape[1]
    tm = _row_tile(m)
    tn = min(tn, n)
    return _matmul(
        a, b, mode="nn", grid=(m // tm, n // tn),
        a_spec=pl.BlockSpec((tm, k), lambda i, j: (i, 0)), b_spec=pl.BlockSpec((k, tn), lambda i, j: (0, j)),
        out_shapes=[_sds((m, n), out_dtype)], out_specs=[pl.BlockSpec((tm, tn), lambda i, j: (i, j))], name=name)[0]


def _mm_out_resid(a, w_out, xres, gate, nxb, name):
    m, k = a.shape
    n = w_out.shape[1]
    tm = ROW_BLOCK
    seg = lambda i, j: (jnp.where(i >= nxb, 1, 0), 0, 0)
    return _matmul(
        a, w_out, mode="nn", grid=(m // tm, 1),
        a_spec=pl.BlockSpec((tm, k), lambda i, j: (i, 0)), b_spec=pl.BlockSpec((k, n), lambda i, j: (0, 0)),
        extra=(xres, gate), extra_specs=(pl.BlockSpec((tm, n), lambda i, j: (i, 0)), pl.BlockSpec((None, 1, n), seg)),
        out_shapes=[_sds((m, n), F32), _sds((m, n), F32)],
        out_specs=[pl.BlockSpec((tm, n), lambda i, j: (i, 0))] * 2, epilogue="resid", name=name)


def _mm_nt(a, b, name, out_dtype=F32):
    m, n = a.shape
    k = b.shape[0]
    tm = _row_tile(m)
    return _matmul(
        a, b, mode="nt", grid=(m // tm, 1),
        a_spec=pl.BlockSpec((tm, n), lambda i, j: (i, 0)), b_spec=pl.BlockSpec((k, n), lambda i, j: (0, 0)),
        out_shapes=[_sds((m, k), out_dtype)], out_specs=[pl.BlockSpec((tm, k), lambda i, j: (i, 0))], name=name)[0]


def _mm_nt_parts(a, b, name):
    p, m, kp = a.shape
    d = b.shape[0]
    tm = _row_tile(m)
    return _matmul(
        a, b, mode="nt", grid=(m // tm, p), nk=p, acc_shape=(tm, d),
        a_spec=pl.BlockSpec((None, tm, kp), lambda i, q: (q, i, 0)), b_spec=pl.BlockSpec((d, kp), lambda i, q: (0, q)),
        out_shapes=[_sds((m, d), F32)], out_specs=[pl.BlockSpec((tm, d), lambda i, q: (i, 0))], name=name)[0]


def _mm_tn(a, b, name, out_dtype, tm=512):
    r, m = a.shape
    n = b.shape[1]
    tm = min(tm, m)
    tn = min(1024, n)
    return _matmul(
        a, b, mode="tn", grid=(m // tm, n // tn),
        a_spec=pl.BlockSpec((r, tm), lambda i, j: (0, i)), b_spec=pl.BlockSpec((r, tn), lambda i, j: (0, j)),
        out_shapes=[_sds((m, n), out_dtype)], out_specs=[pl.BlockSpec((tm, tn), lambda i, j: (i, j))], name=name)[0]


def _mm_tn_parts(a, b, name, out_dtype, tm=512):
    r, m = a.shape
    p, _, np_ = b.shape
    tm = min(tm, m)
    return _matmul(
        a, b, mode="tn", grid=(m // tm, p),
        a_spec=pl.BlockSpec((r, tm), lambda i, q: (0, i)), b_spec=pl.BlockSpec((None, r, np_), lambda i, q: (q, 0, 0)),
        out_shapes=[_sds((m, p * np_), out_dtype)], out_specs=[pl.BlockSpec((tm, np_), lambda i, q: (i, q))],
        name=name)[0]


def _seg_map(nxb):
    return lambda i: (jnp.where(i >= nxb, 1, 0), 0, 0)


def _normmod_fwd(x, g, scale, shift, nxb, name):
    rows, d = x.shape
    tr = ROW_BLOCK

    def body(x_ref, g_ref, sc_ref, sh_ref, h_ref, r_ref):
        xv = x_ref[...]
        r = lax.rsqrt(jnp.mean(xv * xv, axis=-1, keepdims=True) + EPS)
        h = (xv * r) * g_ref[...] * (1.0 + sc_ref[...]) + sh_ref[...]
        h_ref[...] = h.astype(BF16)
        r_ref[...] = r

    row = pl.BlockSpec((tr, d), lambda i: (i, 0))
    vec = pl.BlockSpec((None, 1, d), _seg_map(nxb))
    return pl.pallas_call(
        body, grid=(rows // tr,), in_specs=[row, pl.BlockSpec((1, d), lambda i: (0, 0)), vec, vec],
        out_specs=[row, pl.BlockSpec((tr, 1), lambda i: (i, 0))],
        out_shape=[_sds((rows, d), BF16), _sds((rows, 1), F32)], name=name, compiler_params=_cparams("parallel"),
    )(x, g, scale, shift)


def _normmod_bwd(dh, x, r, g, scale, dres, nxb, name):
    rows, d = x.shape
    tr = ROW_BLOCK
    nres = dres.shape[0] // tr
    nseg = scale.shape[0]

    def body(dh_ref, x_ref, r_ref, g_ref, sc_ref, dres_ref, dx_ref, dsh_ref, dge_ref):
        i = pl.program_id(0)
        dhv = dh_ref[...]
        rv = r_ref[...]
        xn = x_ref[...] * rv
        dxn = dhv * (g_ref[...] * (1.0 + sc_ref[...]))
        dx = rv * (dxn - xn * jnp.mean(dxn * xn, axis=-1, keepdims=True))

        @pl.when(i < nres)
        def _():
            dx_ref[...] = dx + dres_ref[...]

        @pl.when(i >= nres)
        def _():
            dx_ref[...] = dx

        first = jnp.logical_or(i == 0, i == nxb)
        s_dh = jnp.sum(dhv, axis=0, keepdims=True)
        s_ge = jnp.sum(dhv * xn, axis=0, keepdims=True)

        @pl.when(first)
        def _():
            dsh_ref[...] = s_dh
            dge_ref[...] = s_ge

        @pl.when(jnp.logical_not(first))
        def _():
            dsh_ref[...] += s_dh
            dge_ref[...] += s_ge

    row = pl.BlockSpec((tr, d), lambda i: (i, 0))
    vec = pl.BlockSpec((None, 1, d), _seg_map(nxb))
    return pl.pallas_call(
        body, grid=(rows // tr,),
        in_specs=[row, row, pl.BlockSpec((tr, 1), lambda i: (i, 0)), pl.BlockSpec((1, d), lambda i: (0, 0)), vec,
                  pl.BlockSpec((tr, d), lambda i: (jnp.minimum(i, nres - 1), 0))],
        out_specs=[row, vec, vec],
        out_shape=[_sds((rows, d), F32), _sds((nseg, 1, d), F32), _sds((nseg, 1, d), F32)],
        name=name, compiler_params=_cparams("arbitrary"),
    )(dh, x, r, g, scale, dres)


def _gate_bwd(dxo, yx, gate, nxb, name):
    rows, d = yx.shape
    tr = ROW_BLOCK
    nseg = gate.shape[0]

    def body(dx_ref, yx_ref, gt_ref, dyx_ref, dg_ref):
        i = pl.program_id(0)
        dxv = dx_ref[...]
        dyx_ref[...] = (dxv * gt_ref[...]).astype(BF16)
        s = jnp.sum(dxv * yx_ref[...], axis=0, keepdims=True)
        first = jnp.logical_or(i == 0, i == nxb)

        @pl.when(first)
        def _():
            dg_ref[...] = s

        @pl.when(jnp.logical_not(first))
        def _():
            dg_ref[...] += s

    row = pl.BlockSpec((tr, d), lambda i: (i, 0))
    vec = pl.BlockSpec((None, 1, d), _seg_map(nxb))
    return pl.pallas_call(
        body, grid=(rows // tr,), in_specs=[row, row, vec], out_specs=[row, vec],
        out_shape=[_sds((rows, d), BF16), _sds((nseg, 1, d), F32)], name=name, compiler_params=_cparams("arbitrary"),
    )(dxo, yx, gate)


_PAD_TOP = 16
_PAD_BOT = 32


def _window_sum(buf, xv, lo, n):
    t = xv.shape[0]
    c = xv.shape[1]
    tp = t + _PAD_TOP + _PAD_BOT
    buf[pl.ds(0, _PAD_TOP), :] = jnp.zeros((_PAD_TOP, c), F32)
    buf[pl.ds(_PAD_TOP, t), :] = xv
    buf[pl.ds(_PAD_TOP + t, _PAD_BOT), :] = jnp.zeros((_PAD_BOT, c), F32)
    p = buf[...]
    k = 1
    while k < n:
        p = p + pltpu.roll(p, tp - k, 0)
        k *= 2
    if lo:
        p = pltpu.roll(p, -lo, 0)
    buf[...] = p
    return buf[pl.ds(_PAD_TOP, t), :]


def _window_count(t, half):
    pos = lax.broadcasted_iota(jnp.int32, (t, 1), 0)
    return (jnp.minimum(pos + half, t) - jnp.maximum(pos - half, 0)).astype(F32)


def _segments(rows, nx):
    return [(0, nx)] + ([(nx, rows - nx)] if rows > nx else [])


def _pool_fwd(uv, nx, name):
    rows = uv.shape[0]
    w = uv.shape[1] // 2
    cb = 128
    per_group = w // len(POOL_WINDOWS) // cb
    segs = _segments(rows, nx)

    def body(u_ref, z_ref, *bufs):
        j = pl.program_id(0)
        for gi, win in enumerate(POOL_WINDOWS):
            half = win // 2

            @pl.when(jnp.logical_and(j >= gi * per_group, j < (gi + 1) * per_group))
            def _():
                for (start, length), buf in zip(segs, bufs):
                    uvv = u_ref[pl.ds(start, length), :]
                    s = _window_sum(buf, uvv, -half, win)
                    z_ref[pl.ds(start, length), :] = (s / _window_count(length, half) - uvv).astype(BF16)

    scratch = [pltpu.VMEM((length + _PAD_TOP + _PAD_BOT, cb), F32) for _, length in segs]
    return pl.pallas_call(
        body, grid=(w // cb,), in_specs=[pl.BlockSpec((rows, cb), lambda j: (0, j))],
        out_specs=pl.BlockSpec((rows, cb), lambda j: (0, j)), out_shape=_sds((rows, w), BF16),
        scratch_shapes=scratch, name=name, compiler_params=_cparams("parallel"),
    )(uv)


def _pool_bwd(dz, dgt, nx, name):
    rows, w = dz.shape
    cb = 128
    per_group = w // len(POOL_WINDOWS) // cb
    segs = _segments(rows, nx)

    def body(dz_ref, dgt_ref, o_ref, *bufs):
        j = pl.program_id(0)
        o_ref[1] = dgt_ref[...]
        for gi, win in enumerate(POOL_WINDOWS):
            half = win // 2

            @pl.when(jnp.logical_and(j >= gi * per_group, j < (gi + 1) * per_group))
            def _():
                for (start, length), buf in zip(segs, bufs):
                    dzv = dz_ref[pl.ds(start, length), :]
                    s = _window_sum(buf, dzv / _window_count(length, half), 1 - half, win)
                    o_ref[0, pl.ds(start, length), :] = (s - dzv).astype(BF16)

    scratch = [pltpu.VMEM((length + _PAD_TOP + _PAD_BOT, cb), F32) for _, length in segs]
    col = pl.BlockSpec((rows, cb), lambda j: (0, j))
    return pl.pallas_call(
        body, grid=(w // cb,), in_specs=[col, col], out_specs=pl.BlockSpec((2, rows, cb), lambda j: (0, 0, j)),
        out_shape=_sds((2, rows, w), BF16), scratch_shapes=scratch, name=name, compiler_params=_cparams("parallel"),
    )(dz, dgt)


def _grp_fwd(z, w_grp, uv, scale, name):
    rows, w = z.shape
    ng, gc, _ = w_grp.shape
    tm = _row_tile(rows)

    def body(z_ref, w_ref, gt_ref, sc_ref, mx_ref, a_ref):
        mixed = jnp.dot(z_ref[...], w_ref[...], preferred_element_type=F32)
        mx_ref[...] = mixed
        a_ref[...] = (mixed * sc_ref[...] * _silu(gt_ref[...])).astype(BF16)

    blk = pl.BlockSpec((tm, gc), lambda g, i: (i, g))
    return pl.pallas_call(
        body, grid=(ng, rows // tm),
        in_specs=[blk, pl.BlockSpec((None, gc, gc), lambda g, i: (g, 0, 0)),
                  pl.BlockSpec((tm, gc), lambda g, i: (i, ng + g)), pl.BlockSpec((1, gc), lambda g, i: (0, g))],
        out_specs=[blk, blk], out_shape=[_sds((rows, w), F32), _sds((rows, w), BF16)],
        name=name, compiler_params=_cparams("parallel", "parallel"),
    )(z, w_grp, uv, scale)


def _grp_bwd(da, mixed, uv, scale, w_grp, name):
    rows, w = da.shape
    ng, gc, _ = w_grp.shape
    tm = _row_tile(rows)

    def body(da_ref, mx_ref, gt_ref, sc_ref, w_ref, dm_ref, dz_ref, dgt_ref, dsc_ref):
        i = pl.program_id(1)
        dav = da_ref[...]
        mixed = mx_ref[...]
        gt = gt_ref[...]
        sg = _silu(gt)
        sc = sc_ref[...]
        dm = (dav * sc * sg).astype(BF16)
        dm_ref[...] = dm
        dz_ref[...] = lax.dot_general(dm, w_ref[...], _DIMS["nt"], preferred_element_type=F32)
        dgt_ref[...] = (dav * mixed * sc * _dsilu(gt)).astype(BF16)
        s = jnp.sum(dav * mixed * sg, axis=0, keepdims=True)

        @pl.when(i == 0)
        def _():
            dsc_ref[...] = s

        @pl.when(i > 0)
        def _():
            dsc_ref[...] += s

    blk = pl.BlockSpec((tm, gc), lambda g, i: (i, g))
    vec = pl.BlockSpec((1, gc), lambda g, i: (0, g))
    return pl.pallas_call(
        body, grid=(ng, rows // tm),
        in_specs=[blk, blk, pl.BlockSpec((tm, gc), lambda g, i: (i, ng + g)), vec,
                  pl.BlockSpec((None, gc, gc), lambda g, i: (g, 0, 0))],
        out_specs=[blk, blk, blk, vec],
        out_shape=[_sds((rows, w), BF16), _sds((rows, w), F32), _sds((rows, w), BF16), _sds((1, w), F32)],
        name=name, compiler_params=_cparams("parallel", "arbitrary"),
    )(da, mixed, uv, scale, w_grp)


def _grp_wgrad(z, dm, ng, name, out_dtype):
    rows, w = z.shape
    gc = w // ng

    def body(z_ref, dm_ref, o_ref):
        o_ref[...] = lax.dot_general(z_ref[...], dm_ref[...], _DIMS["tn"],
                                     preferred_element_type=F32).astype(o_ref.dtype)

    blk = pl.BlockSpec((rows, gc), lambda g: (0, g))
    return pl.pallas_call(
        body, grid=(ng,), in_specs=[blk, blk], out_specs=pl.BlockSpec((None, gc, gc), lambda g: (g, 0, 0)),
        out_shape=_sds((ng, gc, gc), out_dtype), name=name, compiler_params=_cparams("parallel"),
    )(z, dm)


def _shift_rows(v, by):
    t = v.shape[0]
    pos = lax.broadcasted_iota(jnp.int32, v.shape, 0)
    rolled = pltpu.roll(v, by % t, 0)
    keep = pos >= by if by > 0 else pos < t + by
    return jnp.where(keep, rolled, 0.0)


def _conv_specs(t, w, cb):
    return [pl.BlockSpec((t, cb), (lambda j, q=q: (0, q * (w // cb) + j))) for q in range(4)]


def _conv_fwd(p4, dw, db, name):
    t = p4.shape[0]
    w = p4.shape[1] // 4
    cb = 128

    def body(bg_ref, cg_ref, v_ref, g_ref, dw_ref, db_ref, a_ref):
        tv = cg_ref[...] * v_ref[...]
        conv = (dw_ref[0:1, :] * _shift_rows(tv, 1) + dw_ref[1:2, :] * tv + dw_ref[2:3, :] * _shift_rows(tv, -1)
                + db_ref[...])
        a_ref[...] = (bg_ref[...] * conv * _silu(g_ref[...])).astype(BF16)

    return pl.pallas_call(
        body, grid=(w // cb,),
        in_specs=_conv_specs(t, w, cb) + [pl.BlockSpec((3, cb), lambda j: (0, j)), pl.BlockSpec((1, cb), lambda j: (0, j))],
        out_specs=pl.BlockSpec((t, cb), lambda j: (0, j)), out_shape=_sds((t, w), BF16),
        name=name, compiler_params=_cparams("parallel"),
    )(p4, p4, p4, p4, dw, db)


def _conv_bwd(da, p4, dw, db, name):
    t, w = da.shape
    cb = 128

    def body(da_ref, bg_ref, cg_ref, v_ref, g_ref, dw_ref, db_ref, d4_ref, ddw_ref, ddb_ref):
        cg = cg_ref[...]
        vv = v_ref[...]
        bg = bg_ref[...]
        gv = g_ref[...]
        tv = cg * vv
        tm1 = _shift_rows(tv, 1)
        tp1 = _shift_rows(tv, -1)
        w0, w1, w2 = dw_ref[0:1, :], dw_ref[1:2, :], dw_ref[2:3, :]
        conv = w0 * tm1 + w1 * tv + w2 * tp1 + db_ref[...]
        y = bg * conv
        dav = da_ref[...]
        dy = dav * _silu(gv)
        d4_ref[3] = (dav * y * _dsilu(gv)).astype(BF16)
        d4_ref[0] = (dy * conv).astype(BF16)
        dconv = dy * bg
        ddb_ref[...] = jnp.sum(dconv, axis=0, keepdims=True)
        ddw_ref[0:1, :] = jnp.sum(dconv * tm1, axis=0, keepdims=True)
        ddw_ref[1:2, :] = jnp.sum(dconv * tv, axis=0, keepdims=True)
        ddw_ref[2:3, :] = jnp.sum(dconv * tp1, axis=0, keepdims=True)
        dt = w0 * _shift_rows(dconv, -1) + w1 * dconv + w2 * _shift_rows(dconv, 1)
        d4_ref[1] = (dt * vv).astype(BF16)
        d4_ref[2] = (dt * cg).astype(BF16)

    col = pl.BlockSpec((t, cb), lambda j: (0, j))
    tap = pl.BlockSpec((3, cb), lambda j: (0, j))
    bias = pl.BlockSpec((1, cb), lambda j: (0, j))
    return pl.pallas_call(
        body, grid=(w // cb,), in_specs=[col] + _conv_specs(t, w, cb) + [tap, bias],
        out_specs=[pl.BlockSpec((4, t, cb), lambda j: (0, 0, j)), tap, bias],
        out_shape=[_sds((4, t, w), BF16), _sds((3, w), F32), _sds((1, w), F32)],
        name=name, compiler_params=_cparams("parallel"),
    )(da, p4, p4, p4, p4, dw, db)


def _attn_mask():
    qn, kn = Q_ROWS * GRID_W, K_ROWS * GRID_W
    qr, qc = np.divmod(np.arange(qn), GRID_W)
    kr, kc = np.divmod(np.arange(kn), GRID_W)
    col0 = np.clip(qc - WIN_COLS // 2, 0, GRID_W - WIN_COLS)
    col_ok = (kc[None, :] >= col0[:, None]) & (kc[None, :] < col0[:, None] + WIN_COLS)
    first = np.full(qn, PAD_ROWS)
    last = np.zeros(qn, np.int64)
    out = []
    for row0 in (first, qr, last):
        row_ok = (kr[None, :] >= row0[:, None]) & (kr[None, :] < row0[:, None] + WIN_ROWS)
        out.append(np.where(row_ok & col_ok, 0.0, NEG))
    return jnp.asarray(np.stack(out), F32)


def _rpb_onehot():
    qc, kc = np.divmod(np.arange(GRID_W * GRID_W), GRID_W)
    e = (kc - qc + WIN_COLS - 1)[None, :] == np.arange(128)[:, None]
    return jnp.asarray(e, F32)


def _mask_class(nblk):
    return lambda h, b: (jnp.where(b == 0, 0, jnp.where(b == nblk - 1, 2, 1)), 0, 0)


_KW = K_ROWS * GRID_W
_QB = Q_ROWS * GRID_W


def _attn_bias(t3_ref, t3s_ref):
    rows = []
    for qr in range(Q_ROWS):
        off = (Q_ROWS - 1 - qr) * GRID_W
        if off % 128 == 0:
            rows.append(t3_ref[:, off:off + _KW])
        else:
            rows.append(t3s_ref[:, off - GRID_W:off - GRID_W + _KW])
    return jnp.concatenate(rows, axis=0)


def _attn_specs(nh, seq, n_ctx):
    lp = seq + 2 * PAD_ROWS * GRID_W
    nblk = seq // _QB
    qspec = pl.BlockSpec((None, _QB, HEAD_DIM), lambda h, b: (h, b, 0))
    kspec = pl.BlockSpec((None, lp, HEAD_DIM), lambda h, b: (h, 0, 0))
    cspec = pl.BlockSpec((None, n_ctx, HEAD_DIM), lambda h, b: (h, 0, 0))
    tspec = pl.BlockSpec((None, GRID_W, 1024), lambda h, b: (h, 0, 0))
    mspec = pl.BlockSpec((None, _QB, _KW), _mask_class(nblk))
    lspec = pl.BlockSpec((None, _QB, 1), lambda h, b: (h, b, 0))
    return nblk, qspec, kspec, cspec, tspec, mspec, lspec


def _attn_fwd(q, kp, vp, kc, vc, t3, t3s, mask, name):
    nh, seq, _ = q.shape
    n_ctx = kc.shape[1]
    nblk, qspec, kspec, cspec, tspec, mspec, lspec = _attn_specs(nh, seq, n_ctx)

    def body(q_ref, kp_ref, vp_ref, kc_ref, vc_ref, t3_ref, t3s_ref, m_ref, o_ref, lse_ref):
        start = pl.multiple_of(pl.program_id(1) * _QB, _QB)
        qv = q_ref[...]
        kw = kp_ref[pl.ds(start, _KW), :]
        vw = vp_ref[pl.ds(start, _KW), :]
        s_loc = lax.dot_general(qv, kw, _DIMS["nt"], preferred_element_type=F32)
        s_loc = s_loc + _attn_bias(t3_ref, t3s_ref) + m_ref[...]
        s_ctx = lax.dot_general(qv, kc_ref[...], _DIMS["nt"], preferred_element_type=F32)
        mx = jnp.maximum(jnp.max(s_loc, axis=-1, keepdims=True), jnp.max(s_ctx, axis=-1, keepdims=True))
        p_loc = jnp.exp(s_loc - mx)
        p_ctx = jnp.exp(s_ctx - mx)
        den = jnp.sum(p_loc, axis=-1, keepdims=True) + jnp.sum(p_ctx, axis=-1, keepdims=True)
        inv = 1.0 / den
        o = jnp.dot((p_loc * inv).astype(BF16), vw, preferred_element_type=F32)
        o = o + jnp.dot((p_ctx * inv).astype(BF16), vc_ref[...], preferred_element_type=F32)
        o_ref[...] = o
        lse_ref[...] = mx + jnp.log(den)

    return pl.pallas_call(
        body, grid=(nh, nblk), in_specs=[qspec, kspec, kspec, cspec, cspec, tspec, tspec, mspec],
        out_specs=[qspec, lspec], out_shape=[_sds((nh, seq, HEAD_DIM), F32), _sds((nh, seq, 1), F32)],
        name=name, compiler_params=_cparams("parallel", "parallel"),
    )(q, kp, vp, kc, vc, t3, t3s, mask)


def _attn_bwd(q, kp, vp, kc, vc, t3, t3s, mask, o, do, lse, name):
    nh, seq, _ = q.shape
    n_ctx = kc.shape[1]
    lp = kp.shape[1]
    nblk, qspec, kspec, cspec, tspec, mspec, lspec = _attn_specs(nh, seq, n_ctx)

    def body(q_ref, kp_ref, vp_ref, kc_ref, vc_ref, t3_ref, t3s_ref, m_ref, o_ref, do_ref, lse_ref,
             dq_ref, dkp_ref, dvp_ref, dkc_ref, dvc_ref, dt3_ref, dt3s_ref):
        b = pl.program_id(1)
        start = pl.multiple_of(b * _QB, _QB)

        @pl.when(b == 0)
        def _():
            dkp_ref[...] = jnp.zeros(dkp_ref.shape, F32)
            dvp_ref[...] = jnp.zeros(dvp_ref.shape, F32)
            dkc_ref[...] = jnp.zeros(dkc_ref.shape, F32)
            dvc_ref[...] = jnp.zeros(dvc_ref.shape, F32)
            dt3_ref[...] = jnp.zeros(dt3_ref.shape, F32)
            dt3s_ref[...] = jnp.zeros(dt3s_ref.shape, F32)

        qv = q_ref[...]
        kw = kp_ref[pl.ds(start, _KW), :]
        vw = vp_ref[pl.ds(start, _KW), :]
        kcv = kc_ref[...]
        vcv = vc_ref[...]
        lse = lse_ref[...]
        s_loc = lax.dot_general(qv, kw, _DIMS["nt"], preferred_element_type=F32)
        p_loc = jnp.exp(s_loc + _attn_bias(t3_ref, t3s_ref) + m_ref[...] - lse)
        p_ctx = jnp.exp(lax.dot_general(qv, kcv, _DIMS["nt"], preferred_element_type=F32) - lse)
        dov = do_ref[...]
        dob = dov.astype(BF16)
        delta = jnp.sum(dov * o_ref[...], axis=-1, keepdims=True)
        ds_loc = p_loc * (lax.dot_general(dob, vw, _DIMS["nt"], preferred_element_type=F32) - delta)
        ds_ctx = p_ctx * (lax.dot_general(dob, vcv, _DIMS["nt"], preferred_element_type=F32) - delta)
        dsb_loc = ds_loc.astype(BF16)
        dsb_ctx = ds_ctx.astype(BF16)
        dq_ref[...] = (jnp.dot(dsb_loc, kw, preferred_element_type=F32)
                       + jnp.dot(dsb_ctx, kcv, preferred_element_type=F32))
        dkp_ref[pl.ds(start, _KW), :] += lax.dot_general(dsb_loc, qv, _DIMS["tn"], preferred_element_type=F32)
        dvp_ref[pl.ds(start, _KW), :] += lax.dot_general(p_loc.astype(BF16), dob, _DIMS["tn"],
                                                         preferred_element_type=F32)
        dkc_ref[...] += lax.dot_general(dsb_ctx, qv, _DIMS["tn"], preferred_element_type=F32)
        dvc_ref[...] += lax.dot_general(p_ctx.astype(BF16), dob, _DIMS["tn"], preferred_element_type=F32)
        for qr in range(Q_ROWS):
            off = (Q_ROWS - 1 - qr) * GRID_W
            piece = ds_loc[qr * GRID_W:(qr + 1) * GRID_W, :]
            if off % 128 == 0:
                dt3_ref[:, off:off + _KW] += piece
            else:
                dt3s_ref[:, off - GRID_W:off - GRID_W + _KW] += piece

    pshape = _sds((nh, lp, HEAD_DIM), F32)
    cshape = _sds((nh, n_ctx, HEAD_DIM), F32)
    tshape = _sds((nh, GRID_W, 1024), F32)
    return pl.pallas_call(
        body, grid=(nh, nblk),
        in_specs=[qspec, kspec, kspec, cspec, cspec, tspec, tspec, mspec, qspec, qspec, lspec],
        out_specs=[qspec, kspec, kspec, cspec, cspec, tspec, tspec],
        out_shape=[_sds((nh, seq, HEAD_DIM), F32), pshape, pshape, cshape, cshape, tshape, tshape],
        name=name, compiler_params=_cparams("parallel", "arbitrary"),
    )(q, kp, vp, kc, vc, t3, t3s, mask, o, do, lse)


def _gate_act(o, p4, gcol, name):
    rows, w = o.shape
    tr = ROW_BLOCK

    def body(o_ref, g_ref, a_ref):
        a_ref[...] = (o_ref[...] * _silu(g_ref[...])).astype(BF16)

    row = pl.BlockSpec((tr, w), lambda i: (i, 0))
    return pl.pallas_call(
        body, grid=(rows // tr,), in_specs=[row, pl.BlockSpec((tr, w), lambda i: (i, gcol))], out_specs=row,
        out_shape=_sds((rows, w), BF16), name=name, compiler_params=_cparams("parallel"),
    )(o, p4)


def _gate_act_bwd(da, o, p4, gcol, name):
    rows, w = o.shape
    tr = ROW_BLOCK

    def body(da_ref, o_ref, g_ref, do_ref, dg_ref):
        dav = da_ref[...]
        gv = g_ref[...]
        do_ref[...] = dav * _silu(gv)
        dg_ref[...] = (dav * o_ref[...] * _dsilu(gv)).astype(BF16)

    row = pl.BlockSpec((tr, w), lambda i: (i, 0))
    return pl.pallas_call(
        body, grid=(rows // tr,), in_specs=[row, row, pl.BlockSpec((tr, w), lambda i: (i, gcol))],
        out_specs=[row, row], out_shape=[_sds((rows, w), F32), _sds((rows, w), BF16)],
        name=name, compiler_params=_cparams("parallel"),
    )(da, o, p4)


def _to_heads(m):
    t = m.shape[0]
    return m.reshape(t, -1, HEAD_DIM).transpose(1, 0, 2)


def _from_heads(m):
    return m.transpose(1, 0, 2).reshape(m.shape[1], -1)


def _final(x, g, target, name):
    rows, d = x.shape
    tr = ROW_BLOCK
    nblk = rows // tr

    def body(x_ref, g_ref, t_ref, loss_ref, dx_ref, dg_ref, acc_ref):
        i = pl.program_id(0)
        xv = x_ref[...]
        gv = g_ref[...]
        r = lax.rsqrt(jnp.mean(xv * xv, axis=-1, keepdims=True) + EPS)
        xn = xv * r
        err = xn * gv - t_ref[...]
        dy = err * (1.0 / d)
        dxn = dy * gv
        dx_ref[...] = r * (dxn - xn * jnp.mean(dxn * xn, axis=-1, keepdims=True))
        s_g = jnp.sum(dy * xn, axis=0, keepdims=True)
        s_l = jnp.sum(jnp.mean(err * err, axis=-1, keepdims=True), axis=0, keepdims=True)

        @pl.when(i == 0)
        def _():
            dg_ref[...] = s_g
            acc_ref[...] = s_l

        @pl.when(i > 0)
        def _():
            dg_ref[...] += s_g
            acc_ref[...] += s_l

        @pl.when(i == nblk - 1)
        def _():
            loss_ref[...] = jnp.broadcast_to(0.5 * acc_ref[...], loss_ref.shape)

    row = pl.BlockSpec((tr, d), lambda i: (i, 0))
    vec = pl.BlockSpec((1, d), lambda i: (0, 0))
    return pl.pallas_call(
        body, grid=(nblk,), in_specs=[row, vec, row],
        out_specs=[pl.BlockSpec((1, 128), lambda i: (0, 0)), row, vec],
        out_shape=[_sds((1, 128), F32), _sds((rows, d), F32), _sds((1, d), F32)],
        scratch_shapes=[pltpu.VMEM((1, 1), F32)], name=name, compiler_params=_cparams("arbitrary"),
    )(x, g, target)


def _as2d(a):
    if a.ndim == 1:
        return a.reshape(-1, 128) if a.shape[0] % 128 == 0 else a.reshape(1, -1)
    return a.reshape(-1, a.shape[-1])


def _adamw(w, g, m, v, name):
    shape = w.shape
    w2, g2, m2, v2 = (_as2d(t) for t in (w, g.reshape(shape), m, v))
    rows, cols = w2.shape
    tr = 512 if rows % 512 == 0 else rows
    c1 = 1.0 - ADAM_B1 ** ADAM_STEP
    c2 = 1.0 - ADAM_B2 ** ADAM_STEP

    def body(w_ref, g_ref, m_ref, v_ref, d_ref, nm_ref, nv_ref):
        gv = g_ref[...]
        nm = ADAM_B1 * m_ref[...] + (1.0 - ADAM_B1) * gv
        nv = ADAM_B2 * v_ref[...] + (1.0 - ADAM_B2) * (gv * gv)
        nm_ref[...] = nm
        nv_ref[...] = nv
        d_ref[...] = -ADAM_LR * ((nm / c1) / (jnp.sqrt(nv / c2) + ADAM_EPS) + ADAM_WD * w_ref[...])

    blk = pl.BlockSpec((tr, cols), lambda i: (i, 0))
    outs = pl.pallas_call(
        body, grid=(rows // tr,), in_specs=[blk] * 4, out_specs=[blk] * 3,
        out_shape=[_sds((rows, cols), F32)] * 3, name=name, compiler_params=_cparams("parallel"),
    )(w2, g2, m2, v2)
    return tuple(t.reshape(shape) for t in outs)


def _sum_lead(x, name, out_dtype=F32):
    n, rows, cols = x.shape
    tr = 512 if rows % 512 == 0 else rows

    def body(x_ref, o_ref):
        acc = x_ref[0].astype(F32)
        for k in range(1, n):
            acc = acc + x_ref[k].astype(F32)
        o_ref[...] = acc.astype(out_dtype)

    return pl.pallas_call(
        body, grid=(rows // tr,), in_specs=[pl.BlockSpec((n, tr, cols), lambda i: (0, i, 0))],
        out_specs=pl.BlockSpec((tr, cols), lambda i: (i, 0)), out_shape=_sds((rows, cols), out_dtype),
        name=name, compiler_params=_cparams("parallel"),
    )(x)


_NO_CTX = 1 << 30


def _seg_vecs(mod_l, which, nseg):
    return mod_l[:nseg, which][:, None, :]


def _norm_grads(dshift, dgeff, dgate, g, scale):
    nseg, _, d = dshift.shape
    dmod = jnp.stack([dshift[:, 0], dgeff[:, 0] * g, dgate[:, 0]], axis=1)
    if nseg == 1:
        dmod = jnp.concatenate([dmod, jnp.zeros((1, 3, d), F32)], axis=0)
    dg = jnp.sum(dgeff[:, 0] * (1.0 + scale[:, 0]), axis=0)
    return dmod, dg


def _pool_layer(xin, g, mod_l, w_in, w_grp, w_out, pscale, nx, tag):
    rows = xin.shape[0]
    nseg = 2 if rows > nx else 1
    nxb = nx // ROW_BLOCK if nseg == 2 else _NO_CTX
    shift, scale, gate = (_seg_vecs(mod_l, k, nseg) for k in range(3))
    h, r = _normmod_fwd(xin, g, scale, shift, nxb, f"norm_fwd_{tag}")
    uv = _mm_nn(h, w_in, f"w_in_fwd_{tag}")
    z = _pool_fwd(uv, nx, f"pool_fwd_{tag}")
    mixed, a = _grp_fwd(z, w_grp, uv, pscale, f"grp_fwd_{tag}")
    yx, xout = _mm_out_resid(a, w_out, xin, gate, nxb, f"w_out_fwd_{tag}")

    def backward(dxo):
        dyx, dgate = _gate_bwd(dxo, yx, gate, nxb, f"gate_bwd_{tag}")
        da = _mm_nt(dyx, w_out, f"w_out_bwd_{tag}")
        gw_out = _mm_tn(a, dyx, f"w_out_grad_{tag}", BF16)
        dm, dz, dgt, dscale = _grp_bwd(da, mixed, uv, pscale, w_grp, f"grp_bwd_{tag}")
        gw_grp = _grp_wgrad(z, dm, w_grp.shape[0], f"grp_grad_{tag}", BF16)
        duv = _pool_bwd(dz, dgt, nx, f"pool_bwd_{tag}")
        dh = _mm_nt_parts(duv, w_in, f"w_in_bwd_{tag}")
        gw_in = _mm_tn_parts(h, duv, f"w_in_grad_{tag}", BF16)
        dx, dshift, dgeff = _normmod_bwd(dh, xin, r, g, scale, dxo, nxb, f"norm_bwd_{tag}")
        dmod, dg = _norm_grads(dshift, dgeff, dgate, g[0], scale)
        return dx, dmod, dg, dict(w_in=gw_in, w_grp=gw_grp, w_out=gw_out, scale=dscale)

    return xout, backward


def _rpb_tables(rpb, onehot):
    nh, na, nb = rpb.shape
    flat = jnp.pad(rpb.reshape(nh * na, nb), ((0, 0), (0, 128 - nb)))
    t1 = _matmul(
        flat, onehot, mode="nn", grid=(1, 4), exact=True,
        a_spec=pl.BlockSpec((nh * na, 128), lambda i, j: (0, 0)), b_spec=pl.BlockSpec((128, 1024), lambda i, j: (0, j)),
        out_shapes=[_sds((nh * na, GRID_W * GRID_W), F32)], out_specs=[pl.BlockSpec((nh * na, 1024), lambda i, j: (0, j))],
        name="rpb_table")[0]
    t3 = t1.reshape(nh, na, GRID_W, GRID_W).transpose(0, 2, 1, 3).reshape(nh, GRID_W, na * GRID_W)
    pad = 1024 - na * GRID_W
    t3s = jnp.pad(t3[:, :, GRID_W:], ((0, 0), (0, 0), (0, pad + GRID_W)))
    return jnp.pad(t3, ((0, 0), (0, 0), (0, pad))), t3s


def _rpb_grad(dt3, dt3s, onehot, nh, na, nb):
    def fold(t, name):
        flat = t[:, :, :na * GRID_W].reshape(nh, GRID_W, na, GRID_W).transpose(0, 2, 1, 3).reshape(nh * na, -1)
        out = _matmul(
            flat, onehot, mode="nt", grid=(1, 4), nk=4, acc_shape=(nh * na, 128), exact=True,
            a_spec=pl.BlockSpec((nh * na, 1024), lambda i, k: (0, k)), b_spec=pl.BlockSpec((128, 1024), lambda i, k: (0, k)),
            out_shapes=[_sds((nh * na, 128), F32)], out_specs=[pl.BlockSpec((nh * na, 128), lambda i, k: (0, 0))],
            name=name)[0]
        return out[:, :nb].reshape(nh, na, nb)

    direct = fold(dt3, "rpb_grad_a")
    shifted = fold(dt3s, "rpb_grad_b")
    return direct + jnp.pad(shifted[:, :-1], ((0, 0), (1, 0), (0, 0)))


def _na_layer(xc, g, mod_l, w_in, rpb, w_out, nx, consts):
    rows, d = xc.shape
    w = w_out.shape[0]
    n_ctx = rows - nx
    nxb = nx // ROW_BLOCK
    mask, onehot = consts
    shift, scale = _seg_vecs(mod_l, 0, 2), _seg_vecs(mod_l, 1, 2)
    gate = _seg_vecs(mod_l, 2, 1)
    h, r = _normmod_fwd(xc, g, scale, shift, nxb, "norm_fwd_na")
    p4 = _mm_nn(h, w_in, "w_in_fwd_na")
    pad = PAD_ROWS * GRID_W

    def heads(col, lo, n):
        return _to_heads(lax.slice(p4, (lo, col * w), (lo + n, (col + 1) * w)))

    q = (heads(0, 0, nx) * HEAD_DIM ** -0.5).astype(BF16)
    kp = jnp.pad(heads(1, 0, nx).astype(BF16), ((0, 0), (pad, pad), (0, 0)))
    vp = jnp.pad(heads(2, 0, nx).astype(BF16), ((0, 0), (pad, pad), (0, 0)))
    kc = heads(1, nx, n_ctx).astype(BF16)
    vc = heads(2, nx, n_ctx).astype(BF16)
    t3, t3s = _rpb_tables(rpb, onehot)
    o, lse = _attn_fwd(q, kp, vp, kc, vc, t3, t3s, mask, "attn_fwd")
    o2 = _from_heads(o)
    a = _gate_act(o2, p4, 3, "attn_gate_fwd")
    yx, xout = _mm_out_resid(a, w_out, xc, gate, _NO_CTX, "w_out_fwd_na")

    def backward(dxo):
        dyx, dgate = _gate_bwd(dxo, yx, gate, _NO_CTX, "gate_bwd_na")
        da = _mm_nt(dyx, w_out, "w_out_bwd_na")
        gw_out = _mm_tn(a, dyx, "w_out_grad_na", BF16)
        do2, dg4 = _gate_act_bwd(da, o2, p4, 3, "attn_gate_bwd")
        dq, dkp, dvp, dkc, dvc, dt3, dt3s = _attn_bwd(q, kp, vp, kc, vc, t3, t3s, mask, o, _to_heads(do2), lse,
                                                      "attn_bwd")
        zeros = jnp.zeros((n_ctx, w), BF16)
        d4 = jnp.stack([
            jnp.concatenate([(_from_heads(dq) * HEAD_DIM ** -0.5).astype(BF16), zeros], axis=0),
            jnp.concatenate([_from_heads(dkp[:, pad:pad + nx]), _from_heads(dkc)], axis=0).astype(BF16),
            jnp.concatenate([_from_heads(dvp[:, pad:pad + nx]), _from_heads(dvc)], axis=0).astype(BF16),
            jnp.concatenate([dg4, zeros], axis=0)])
        dh = _mm_nt_parts(d4, w_in, "w_in_bwd_na")
        gw_in = _mm_tn_parts(h, d4, "w_in_grad_na", BF16)
        dx, dshift, dgeff = _normmod_bwd(dh, xc, r, g, scale, dxo, nxb, "norm_bwd_na")
        dgate2 = jnp.concatenate([dgate, jnp.zeros_like(dgate)], axis=0)
        dmod, dg = _norm_grads(dshift, dgeff, dgate2, g[0], scale)
        drpb = _rpb_grad(dt3, dt3s, onehot, *rpb.shape)
        return dx, dmod, dg, dict(w_in=gw_in, w_out=gw_out, rpb=drpb)

    return xout, backward


def _conv_layer(xin, g, mod_l, w_in, dw, db, w_out):
    shift, scale, gate = (_seg_vecs(mod_l, k, 1) for k in range(3))
    h, r = _normmod_fwd(xin, g, scale, shift, _NO_CTX, "norm_fwd_conv")
    p4 = _mm_nn(h, w_in, "w_in_fwd_conv")
    a = _conv_fwd(p4, dw, db, "conv_fwd")
    yx, xout = _mm_out_resid(a, w_out, xin, gate, _NO_CTX, "w_out_fwd_conv")

    def backward(dxo):
        dyx, dgate = _gate_bwd(dxo, yx, gate, _NO_CTX, "gate_bwd_conv")
        da = _mm_nt(dyx, w_out, "w_out_bwd_conv")
        gw_out = _mm_tn(a, dyx, "w_out_grad_conv", BF16)
        d4, ddw, ddb = _conv_bwd(da, p4, dw, db, "conv_bwd")
        dh = _mm_nt_parts(d4, w_in, "w_in_bwd_conv")
        gw_in = _mm_tn_parts(h, d4, "w_in_grad_conv", BF16)
        dx, dshift, dgeff = _normmod_bwd(dh, xin, r, g, scale, dxo, _NO_CTX, "norm_bwd_conv")
        dmod, dg = _norm_grads(dshift, dgeff, dgate, g[0], scale)
        return dx, dmod, dg, dict(w_in=gw_in, w_out=gw_out, dw=ddw, db=ddb)

    return xout, backward


def _example_step(x, ctx, target, mod, norm_g, final_g, wts):
    nx = x.shape[0]
    consts = (_attn_mask(), _rpb_onehot())
    g_rows = [norm_g[i:i + 1] for i in range(4)]
    xc0 = jnp.concatenate([x, ctx], axis=0)
    xc1, bwd0 = _pool_layer(xc0, g_rows[0], mod[0], wts["pool_w_in"][0], wts["pool_w_grp"][0], wts["pool_w_out"][0],
                            wts["pool_scale"][0:1], nx, "p0")
    x2, bwd1 = _na_layer(xc1, g_rows[1], mod[1], wts["na_w_in"], wts["na_rpb"], wts["na_w_out"], nx, consts)
    x3, bwd2 = _conv_layer(x2, g_rows[2], mod[2], wts["conv_w_in"], wts["conv_dw"], wts["conv_db"], wts["conv_w_out"])
    x4, bwd3 = _pool_layer(x3, g_rows[3], mod[3], wts["pool_w_in"][1], wts["pool_w_grp"][1], wts["pool_w_out"][1],
                           wts["pool_scale"][1:2], nx, "p3")
    loss, dx4, dfinal_g = _final(x4, final_g, target, "loss_head")
    dx3, dmod3, dg3, gr3 = bwd3(dx4)
    dx2, dmod2, dg2, gr2 = bwd2(dx3)
    dxc1, dmod1, dg1, gr1 = bwd1(dx2)
    dxc0, dmod0, dg0, gr0 = bwd0(dxc1)
    return dict(
        loss=loss, grad_x=dxc0[:nx], dmod=jnp.stack([dmod0, dmod1, dmod2, dmod3]),
        dnorm_g=jnp.stack([dg0, dg1, dg2, dg3]), dfinal_g=dfinal_g, layers=(gr0, gr1, gr2, gr3))


_AXES = ("x", "y", "c")
_CHIP_FLIPS = ((1, 0), (0, 1), (1, 1))


def _position():
    return tuple(lax.axis_index(a) for a in _AXES)


def _flipped(pos, flip):
    return tuple(1 - p if f else p for p, f in zip(pos, flip))


def _all_gather(v, axes, name):
    flips = [f for f in np.ndindex(2, 2, 2) if any(f) and all(a in axes or not b for a, b in zip(_AXES, f))]
    n = len(flips) + 1

    def body(v_ref, o_ref, send_sems, recv_sems, local_sem):
        pos = _position()
        slot = 0
        for a, p in zip(_AXES, pos):
            if a in axes:
                slot = 2 * slot + p
        local = pltpu.make_async_copy(v_ref, o_ref.at[slot], local_sem)
        local.start()
        copies = []
        for k, flip in enumerate(flips):
            cp = pltpu.make_async_remote_copy(v_ref, o_ref.at[slot], send_sems.at[k], recv_sems.at[k],
                                              device_id=_flipped(pos, flip), device_id_type=MESH)
            cp.start()
            copies.append(cp)
        for cp in copies:
            cp.wait()
        local.wait()

    return pl.pallas_call(
        body, in_specs=[HBM_SPEC], out_specs=HBM_SPEC, out_shape=_sds((n,) + v.shape, v.dtype),
        scratch_shapes=[pltpu.SemaphoreType.DMA((n - 1,)), pltpu.SemaphoreType.DMA((n - 1,)), pltpu.SemaphoreType.DMA(())],
        name=name,
    )(v)


class _Item:
    def __init__(self, key, layer, shape, shard_axis, half_axis):
        self.key, self.layer, self.shape = key, layer, tuple(shape)
        self.shard_axis, self.half_axis = shard_axis, half_axis
        self.shard = shape[shard_axis] // 4
        self.half = shape[half_axis] // 2

    def sized(self, shard=False, half=False):
        s = list(self.shape)
        if shard:
            s[self.shard_axis] = self.shard
        if half:
            s[self.half_axis] = self.half
        return tuple(s)

    def window(self, ref, chip=None, half=None):
        idx = [slice(None)] * len(self.shape)
        if chip is not None:
            idx[self.shard_axis] = pl.ds(chip * self.shard, self.shard)
        if half is not None:
            idx[self.half_axis] = pl.ds(half * self.half, self.half)
        return ref.at[tuple(idx)]


def _items(d, w):
    out = []
    for j in range(2):
        out += [_Item("pool_w_in", j, (d, 2 * w), 1, 0), _Item("pool_w_grp", j, (4, w // 4, w // 4), 1, 0),
                _Item("pool_w_out", j, (w, d), 0, 1)]
    out += [_Item("na_w_in", 0, (d, 4 * w), 1, 0), _Item("na_w_out", 0, (w, d), 0, 1),
            _Item("conv_w_in", 0, (d, 4 * w), 1, 0), _Item("conv_w_out", 0, (w, d), 0, 1)]
    return out


def _gather_weights(shards, items):
    n = len(items)

    def body(*refs):
        src, dst = refs[:n], refs[n:2 * n]
        send_a, recv_a, send_b, recv_b, send_c, recv_c = refs[2 * n:]
        x, y, c = _position()
        chip = 2 * x + y
        peers = [_flipped((x, y), f) for f in _CHIP_FLIPS]
        swapped, sent = [], []
        for i, it in enumerate(items):
            cp = pltpu.make_async_remote_copy(src[i], it.window(dst[i], chip=chip), send_c.at[i], recv_c.at[i],
                                              device_id=(x, y, 1 - c), device_id_type=MESH)
            cp.start()
            swapped.append(cp)
            for k, (px, py) in enumerate(peers):
                cp = pltpu.make_async_remote_copy(
                    it.window(src[i], half=c), it.window(dst[i], chip=chip, half=c),
                    send_a.at[3 * i + k], recv_a.at[3 * i + k], device_id=(px, py, c), device_id_type=MESH)
                cp.start()
                sent.append(cp)
        for i, it in enumerate(items):
            for k, (px, py) in enumerate(peers):
                got = it.window(dst[i], chip=2 * px + py, half=c)
                pltpu.make_async_remote_copy(got, got, send_a.at[3 * i + k], recv_a.at[3 * i + k],
                                             device_id=(px, py, c), device_id_type=MESH).wait_recv()
                cp = pltpu.make_async_remote_copy(got, got, send_b.at[3 * i + k], recv_b.at[3 * i + k],
                                                  device_id=(x, y, 1 - c), device_id_type=MESH)
                cp.start()
                sent.append(cp)
        for i, it in enumerate(items):
            for k, (px, py) in enumerate(peers):
                other = it.window(dst[i], chip=2 * px + py, half=1 - c)
                pltpu.make_async_remote_copy(other, other, send_b.at[3 * i + k], recv_b.at[3 * i + k],
                                             device_id=(x, y, 1 - c), device_id_type=MESH).wait_recv()
        for cp in sent:
            cp.wait_send()
        for cp in swapped:
            cp.wait()

    return pl.pallas_call(
        body, in_specs=[HBM_SPEC] * n, out_specs=[HBM_SPEC] * n, out_shape=[_sds(it.shape, BF16) for it in items],
        scratch_shapes=[pltpu.SemaphoreType.DMA((3 * n,)) for _ in range(4)]
        + [pltpu.SemaphoreType.DMA((n,)), pltpu.SemaphoreType.DMA((n,))], name="gather_weights",
    )(*shards)


def _pair_swap(arrays, windows, out_shapes, name):
    n = len(arrays)

    def body(*refs):
        src, got = refs[:n], refs[n:2 * n]
        send_sems, recv_sems = refs[2 * n:]
        x, y, c = _position()
        copies = []
        for i in range(n):
            cp = pltpu.make_async_remote_copy(windows[i](src[i], 1 - c), got[i], send_sems.at[i], recv_sems.at[i],
                                              device_id=(x, y, 1 - c), device_id_type=MESH)
            cp.start()
            copies.append(cp)
        for cp in copies:
            cp.wait()

    return pl.pallas_call(
        body, in_specs=[HBM_SPEC] * n, out_specs=[HBM_SPEC] * n, out_shape=list(out_shapes),
        scratch_shapes=[pltpu.SemaphoreType.DMA((n,)), pltpu.SemaphoreType.DMA((n,))], name=name,
    )(*arrays)


def _chip_exchange(partials, items):
    n = len(items)

    def body(*refs):
        src, dst = refs[:n], refs[n:2 * n]
        send_sems, recv_sems = refs[2 * n:]
        x, y, c = _position()
        peers = [_flipped((x, y), f) for f in _CHIP_FLIPS]
        copies = []
        for i, it in enumerate(items):
            for k, (px, py) in enumerate(peers):
                cp = pltpu.make_async_remote_copy(
                    it.window(src[i], chip=2 * px + py), dst[i].at[k], send_sems.at[3 * i + k],
                    recv_sems.at[3 * i + k], device_id=(px, py, c), device_id_type=MESH)
                cp.start()
                copies.append(cp)
        for cp in copies:
            cp.wait()

    return pl.pallas_call(
        body, in_specs=[HBM_SPEC] * n, out_specs=[HBM_SPEC] * n,
        out_shape=[_sds((3,) + it.sized(shard=True, half=True), BF16) for it in items],
        scratch_shapes=[pltpu.SemaphoreType.DMA((3 * n,)), pltpu.SemaphoreType.DMA((3 * n,))], name="chip_exchange",
    )(*partials)


def _pair_sum(g, got, it, pos, name):
    rows_split = it.half_axis == 0
    g2 = g.reshape(-1, g.shape[-1])
    got2 = got.reshape(-1, got.shape[-1])
    rows, cols = got2.shape
    tr = min(rows, 256)
    nb = rows // tr

    def body(pos_ref, g_ref, got_ref, o_ref):
        o_ref[...] = (g_ref[...].astype(F32) + got_ref[...].astype(F32)).astype(BF16)

    g_map = (lambda i, pos: (pos[1] * nb + i, 0)) if rows_split else (lambda i, pos: (i, pos[1]))
    blk = pl.BlockSpec((tr, cols), lambda i, pos: (i, 0))
    return pl.pallas_call(
        body, grid_spec=pltpu.PrefetchScalarGridSpec(
            num_scalar_prefetch=1, grid=(nb,), in_specs=[pl.BlockSpec((tr, cols), g_map), blk], out_specs=blk),
        out_shape=_sds((rows, cols), BF16), name=name, compiler_params=_cparams("parallel"),
    )(pos, g2, got2).reshape(got.shape)


_FLIP_SLOT = {2: 0, 1: 1, 3: 2}


def _chip_sum(pair, slots, it, pos, name):
    shape = it.sized(shard=True, half=True)
    nd = len(shape)

    def body(pos_ref, p_ref, s_ref, o_ref):
        chip = pos_ref[0]
        for own in range(4):
            @pl.when(chip == own)
            def _():
                acc = None
                for k in range(4):
                    v = (p_ref[...] if k == own else s_ref[_FLIP_SLOT[own ^ k]]).astype(F32)
                    acc = v if acc is None else acc + v
                o_ref[...] = acc

    p_map = lambda i, pos: tuple(pos[0] if ax == it.shard_axis else 0 for ax in range(nd))
    return pl.pallas_call(
        body, grid_spec=pltpu.PrefetchScalarGridSpec(
            num_scalar_prefetch=1, grid=(1,),
            in_specs=[pl.BlockSpec(shape, p_map), pl.BlockSpec((3,) + shape, lambda i, pos: (0,) * (nd + 1))],
            out_specs=pl.BlockSpec(shape, lambda i, pos: (0,) * nd)),
        out_shape=_sds(shape, F32), name=name, compiler_params=_cparams("arbitrary"),
    )(pos, pair, slots)


_GRAD_KEYS = ("pool_w_in", "pool_w_grp", "pool_w_out", "na_w_in", "na_w_out", "conv_w_in", "conv_w_out")


def _adamw_matrix(w, m, v, owns, others, it, pos, name):
    nl = w.shape[0]
    rows_split = it.half_axis == 0
    r, cdim = int(np.prod(w.shape[1:-1])), w.shape[-1]
    hr, hc = (r // 2, cdim) if rows_split else (r, cdim // 2)
    br = min(hr, 256)
    nb = hr // br
    c1 = 1.0 - ADAM_B1 ** ADAM_STEP
    c2 = 1.0 - ADAM_B2 ** ADAM_STEP

    def body(pos_ref, w_ref, m_ref, v_ref, *rest):
        own_refs, other_refs = rest[:nl], rest[nl:2 * nl]
        g_ref, d_ref, nm_ref, nv_ref = rest[2 * nl:]
        j, h = pl.program_id(0), pl.program_id(1)
        own, other = own_refs[0][...], other_refs[0][...]
        for q in range(1, nl):
            own = jnp.where(j == q, own_refs[q][...], own)
            other = jnp.where(j == q, other_refs[q][...], other)
        gv = jnp.where(h == pos_ref[1], own, other)
        nm = ADAM_B1 * m_ref[...] + (1.0 - ADAM_B1) * gv
        nv = ADAM_B2 * v_ref[...] + (1.0 - ADAM_B2) * (gv * gv)
        g_ref[...] = gv
        nm_ref[...] = nm
        nv_ref[...] = nv
        d_ref[...] = -ADAM_LR * ((nm / c1) / (jnp.sqrt(nv / c2) + ADAM_EPS) + ADAM_WD * w_ref[...])

    if rows_split:
        full = pl.BlockSpec((None, br, hc), lambda j, h, i, pos: (j, h * nb + i, 0))
    else:
        full = pl.BlockSpec((None, br, hc), lambda j, h, i, pos: (j, i, h))
    half = pl.BlockSpec((br, hc), lambda j, h, i, pos: (i, 0))
    flat = lambda t: t.reshape(nl, r, cdim)
    outs = pl.pallas_call(
        body, grid_spec=pltpu.PrefetchScalarGridSpec(
            num_scalar_prefetch=1, grid=(nl, 2, nb), in_specs=[full] * 3 + [half] * (2 * nl), out_specs=[full] * 4),
        out_shape=[_sds((nl, r, cdim), F32)] * 4, name=name,
        compiler_params=_cparams("parallel", "parallel", "parallel"),
    )(pos, flat(w), flat(m), flat(v), *[t.reshape(hr, hc) for t in list(owns) + list(others)])
    return tuple(t.reshape(w.shape) for t in outs)


_WEIGHTS = ("c_ctx", "norm_g", "ada_w", "ada_b", "pool_w_in", "pool_w_grp", "pool_scale", "pool_w_out", "na_w_in",
            "na_rpb", "na_w_out", "conv_w_in", "conv_dw", "conv_db", "conv_w_out", "final_g")
_COND_ROWS = 16


def _modulations(cond, ada_w, ada_b_cols):
    nl, d, n = ada_w.shape
    return _matmul(
        cond, ada_w, mode="nn", grid=(nl, 1), a_silu=True, epilogue="bias",
        a_spec=pl.BlockSpec((_COND_ROWS, d), lambda i, j: (0, 0)), b_spec=pl.BlockSpec((None, d, n), lambda i, j: (i, 0, 0)),
        extra=(ada_b_cols,), extra_specs=(pl.BlockSpec((None, 1, n), lambda i, j: (i, 0, 0)),),
        out_shapes=[_sds((nl, _COND_ROWS, n), F32)], out_specs=[pl.BlockSpec((None, _COND_ROWS, n), lambda i, j: (i, 0, 0))],
        name="modulations")[0]


def _ada_w_grad(cond, dm_cols):
    d = cond.shape[1]
    nl, _, n = dm_cols.shape
    return _matmul(
        cond, dm_cols, mode="tn", grid=(nl, 1), a_silu=True,
        a_spec=pl.BlockSpec((_COND_ROWS, d), lambda i, j: (0, 0)), b_spec=pl.BlockSpec((None, _COND_ROWS, n), lambda i, j: (i, 0, 0)),
        out_shapes=[_sds((nl, d, n), F32)], out_specs=[pl.BlockSpec((None, d, n), lambda i, j: (i, 0, 0))],
        name="ada_w_grad")[0]


def _cond_grad(dm_cols, ada_w):
    nl, d, n = ada_w.shape
    return _matmul(
        dm_cols, ada_w, mode="nt", grid=(1, nl), nk=nl, acc_shape=(_COND_ROWS, d),
        a_spec=pl.BlockSpec((None, _COND_ROWS, n), lambda i, q: (q, 0, 0)), b_spec=pl.BlockSpec((None, d, n), lambda i, q: (q, 0, 0)),
        out_shapes=[_sds((_COND_ROWS, d), F32)], out_specs=[pl.BlockSpec((_COND_ROWS, d), lambda i, q: (0, 0))],
        name="cond_grad")[0]


def _pack(parts):
    flat = [p.reshape(-1) for p in parts]
    sizes = [f.shape[0] for f in flat]
    total = sum(sizes)
    rows = -(-total // 1024) * 8
    packed = jnp.concatenate(flat + [jnp.zeros((rows * 128 - total,), F32)]).reshape(rows, 128)
    offs = np.concatenate([[0], np.cumsum(sizes)])[:-1]
    return packed, [(int(o), p.shape) for o, p in zip(offs, parts)]


def _unpack(flat, layout, k):
    off, shape = layout[k]
    return flat[..., off:off + int(np.prod(shape))].reshape(flat.shape[:-1] + tuple(shape))


def kernel(x, c, ctx, c_ctx, norm_g, ada_w, ada_b, pool_w_in, pool_w_grp, pool_scale, pool_w_out, na_w_in, na_rpb, na_w_out, conv_w_in, conv_dw, conv_db, conv_w_out, final_g, loss_target, m_c_ctx, m_norm_g, m_ada_w, m_ada_b, m_pool_w_in, m_pool_w_grp, m_pool_scale, m_pool_w_out, m_na_w_in, m_na_rpb, m_na_w_out, m_conv_w_in, m_conv_dw, m_conv_db, m_conv_w_out, m_final_g, v_c_ctx, v_norm_g, v_ada_w, v_ada_b, v_pool_w_in, v_pool_w_grp, v_pool_scale, v_pool_w_out, v_na_w_in, v_na_rpb, v_na_w_out, v_conv_w_in, v_conv_dw, v_conv_db, v_conv_w_out, v_final_g):
    params = dict(c_ctx=c_ctx, norm_g=norm_g, ada_w=ada_w, ada_b=ada_b, pool_w_in=pool_w_in, pool_w_grp=pool_w_grp,
                  pool_scale=pool_scale, pool_w_out=pool_w_out, na_w_in=na_w_in, na_rpb=na_rpb, na_w_out=na_w_out,
                  conv_w_in=conv_w_in, conv_dw=conv_dw, conv_db=conv_db, conv_w_out=conv_w_out, final_g=final_g)
    mom1 = dict(c_ctx=m_c_ctx, norm_g=m_norm_g, ada_w=m_ada_w, ada_b=m_ada_b, pool_w_in=m_pool_w_in,
                pool_w_grp=m_pool_w_grp, pool_scale=m_pool_scale, pool_w_out=m_pool_w_out, na_w_in=m_na_w_in,
                na_rpb=m_na_rpb, na_w_out=m_na_w_out, conv_w_in=m_conv_w_in, conv_dw=m_conv_dw, conv_db=m_conv_db,
                conv_w_out=m_conv_w_out, final_g=m_final_g)
    mom2 = dict(c_ctx=v_c_ctx, norm_g=v_norm_g, ada_w=v_ada_w, ada_b=v_ada_b, pool_w_in=v_pool_w_in,
                pool_w_grp=v_pool_w_grp, pool_scale=v_pool_scale, pool_w_out=v_pool_w_out, na_w_in=v_na_w_in,
                na_rpb=v_na_rpb, na_w_out=v_na_w_out, conv_w_in=v_conv_w_in, conv_dw=v_conv_dw, conv_db=v_conv_db,
                conv_w_out=v_conv_w_out, final_g=v_final_g)
    d = x.shape[-1]
    w = na_w_out.shape[1] * 4
    xi, yi, ci = _position()
    chip = 2 * xi + yi
    dev = 2 * chip + ci
    n_ada = ada_w.shape[-1]

    def chip_cols(a, size):
        return lax.dynamic_slice_in_dim(a, chip * size, size, axis=a.ndim - 1)

    conds = _all_gather(c.reshape(8, d // 8), _AXES, "gather_cond").reshape(8, d)
    cond = jnp.concatenate([conds, c_ctx[None], jnp.zeros((_COND_ROWS - 9, d), F32)], axis=0)
    mod_cols = _modulations(cond, ada_w, chip_cols(ada_b, n_ada)[:, None, :])
    mod_all = _all_gather(mod_cols, ("x", "y"), "gather_mod")
    mod_all = mod_all.transpose(1, 2, 0, 3).reshape(4, _COND_ROWS, 3, d)
    mod = jnp.stack([lax.dynamic_index_in_dim(mod_all, dev, axis=1, keepdims=False), mod_all[:, 8]], axis=1)

    items = _items(d, w)
    shards = [params[it.key][it.layer].astype(BF16) for it in items]
    full = dict()
    for it, mat in zip(items, _gather_weights(shards, items)):
        full.setdefault(it.key, []).append(mat)
    small = _all_gather(_pack([pool_scale, conv_dw, conv_db])[0], ("x", "y"), "gather_small")
    small_layout = _pack([pool_scale, conv_dw, conv_db])[1]
    small = small.reshape(4, -1)

    def whole(k):
        parts = _unpack(small, small_layout, k)
        return jnp.moveaxis(parts, 0, -2).reshape(parts.shape[1:-1] + (-1,))

    wts = dict(pool_w_in=full["pool_w_in"], pool_w_grp=full["pool_w_grp"], pool_w_out=full["pool_w_out"],
               na_w_in=full["na_w_in"][0], na_w_out=full["na_w_out"][0], conv_w_in=full["conv_w_in"][0],
               conv_w_out=full["conv_w_out"][0], pool_scale=whole(0), na_rpb=na_rpb[0], conv_dw=whole(1)[0],
               conv_db=whole(2))
    res = _example_step(x[0], ctx[0], loss_target[0], mod, norm_g, final_g[None], wts)

    g0, g1, g2, g3 = res["layers"]
    by_item = {("pool_w_in", 0): g0["w_in"], ("pool_w_grp", 0): g0["w_grp"], ("pool_w_out", 0): g0["w_out"],
               ("pool_w_in", 1): g3["w_in"], ("pool_w_grp", 1): g3["w_grp"], ("pool_w_out", 1): g3["w_out"],
               ("na_w_in", 0): g1["w_in"], ("na_w_out", 0): g1["w_out"],
               ("conv_w_in", 0): g2["w_in"], ("conv_w_out", 0): g2["w_out"]}
    pos = jnp.stack([chip, ci]).astype(jnp.int32)
    full_grads = [by_item[(it.key, it.layer)] for it in items]
    got = _pair_swap(full_grads, [(lambda ref, half, it=it: it.window(ref, half=half)) for it in items],
                     [_sds(it.sized(half=True), BF16) for it in items], "pair_exchange")
    pair = [_pair_sum(g, s, it, pos, f"pair_sum_{i}") for i, (g, s, it) in enumerate(zip(full_grads, got, items))]
    slots = _chip_exchange(pair, items)
    reduced = [_chip_sum(p, s, it, pos, f"chip_sum_{i}") for i, (p, s, it) in enumerate(zip(pair, slots, items))]
    theirs = _pair_swap(reduced, [lambda ref, half: ref] * len(items),
                        [_sds(t.shape, F32) for t in reduced], "pair_return")
    grads, matrix_out = dict(), dict()
    for k in _GRAD_KEYS:
        idx = [i for i, it in enumerate(items) if it.key == k]
        res_k = _adamw_matrix(params[k], mom1[k], mom2[k], [reduced[i] for i in idx], [theirs[i] for i in idx],
                              items[idx[0]], pos, f"adamw_{k}")
        grads[k], matrix_out[k] = res_k[0], res_k[1:]

    packed, layout = _pack([res["dfinal_g"], res["dnorm_g"], res["dmod"], g1["rpb"],
                            jnp.concatenate([g0["scale"], g3["scale"]], axis=0), g2["dw"], g2["db"]])
    every = _all_gather(packed, _AXES, "gather_vec_grads")
    total = _sum_lead(every, "sum_vec_grads").reshape(-1)
    every = every.reshape(8, -1)
    grads["final_g"] = _unpack(total, layout, 0).reshape(final_g.shape)
    grads["norm_g"] = _unpack(total, layout, 1)
    grads["na_rpb"] = _unpack(total, layout, 3)[None]
    grads["pool_scale"] = chip_cols(_unpack(total, layout, 4), pool_scale.shape[-1])
    grads["conv_dw"] = chip_cols(_unpack(total, layout, 5), conv_dw.shape[-1])[None]
    grads["conv_db"] = chip_cols(_unpack(total, layout, 6), conv_db.shape[-1])
    dmod_sum = _unpack(total, layout, 2).reshape(4, 2, 3 * d)
    dmod_each = _unpack(every, layout, 2).reshape(8, 4, 2, 3 * d)
    grads["ada_b"] = dmod_sum[:, 0] + dmod_sum[:, 1]
    dm = jnp.concatenate([dmod_each[:, :, 0].transpose(1, 0, 2), dmod_sum[:, 1][:, None],
                          jnp.zeros((4, _COND_ROWS - 9, 3 * d), F32)], axis=1)
    dm_cols = chip_cols(dm, n_ada)
    grads["ada_w"] = _ada_w_grad(cond, dm_cols)
    dcond = _cond_grad(dm_cols, ada_w)[8].reshape(8, d // 8)
    dcond = _sum_lead(_all_gather(dcond, ("x", "y"), "gather_cond_grad"), "sum_cond_grad").reshape(d)
    grads["c_ctx"] = dcond * _dsilu(c_ctx)

    outs = [[], [], []]
    for k in _WEIGHTS:
        step = matrix_out[k] if k in matrix_out else _adamw(params[k], grads[k], mom1[k], mom2[k], f"adamw_{k}")
        for lst, val in zip(outs, step):
            lst.append(val)
    loss = lax.psum(res["loss"][0, 0], _AXES)
    return (loss, res["grad_x"][None], *[grads[k].reshape(params[k].shape) for k in _WEIGHTS],
            *outs[0], *outs[1], *outs[2])
```

```python
import functools

import numpy as np
import jax
import jax.numpy as jnp
from jax import lax
from jax.experimental import pallas as pl
from jax.experimental.pallas import tpu as pltpu

F32 = jnp.float32
BF16 = jnp.bfloat16

EPS = 1e-6
GRID_W = 64
HEAD_DIM = 64
WIN_ROWS = 8
WIN_COLS = 16
POOL_WINDOWS = (2, 4, 8, 16)
Q_ROWS = 4
K_ROWS = 12
PAD_ROWS = 4
NEG = -1e30

ADAM_LR = 0.001
ADAM_B1 = 0.9
ADAM_B2 = 0.999
ADAM_EPS = 1e-08
ADAM_WD = 0.01
ADAM_STEP = 10

ROW_BLOCK = 256
VMEM_LIMIT = 56 * 1024 * 1024

MESH = pl.DeviceIdType.MESH
HBM_SPEC = pl.BlockSpec(memory_space=pltpu.HBM)


def _cparams(*sem):
    return pltpu.CompilerParams(dimension_semantics=sem or None, vmem_limit_bytes=VMEM_LIMIT)


def _sds(shape, dtype):
    return jax.ShapeDtypeStruct(tuple(shape), dtype)


def _sigmoid(x):
    return 1.0 / (1.0 + jnp.exp(-x))


def _silu(x):
    return x * _sigmoid(x)


def _dsilu(x):
    s = _sigmoid(x)
    return s * (1.0 + x * (1.0 - s))


_DIMS = {
    "nn": (((1,), (0,)), ((), ())),
    "nt": (((1,), (1,)), ((), ())),
    "tn": (((0,), (0,)), ((), ())),
}


def _matmul(a, b, *, mode, grid, a_spec, b_spec, out_shapes, out_specs, name, nk=1,
            a_silu=False, exact=False, epilogue=None, extra=(), extra_specs=(), acc_shape=None):
    n_extra = len(extra)
    n_out = len(out_shapes)

    def body(*refs):
        a_ref, b_ref = refs[:2]
        ex = refs[2:2 + n_extra]
        outs = refs[2 + n_extra:2 + n_extra + n_out]
        av = a_ref[...]
        bv = b_ref[...]
        if a_silu:
            av = _silu(av.astype(F32))
        if exact:
            prod = lax.dot_general(av.astype(F32), bv.astype(F32), _DIMS[mode],
                                   precision=lax.Precision.HIGHEST, preferred_element_type=F32)
        else:
            prod = lax.dot_general(av.astype(BF16), bv.astype(BF16), _DIMS[mode], preferred_element_type=F32)

        def finish(res):
            if epilogue is None:
                outs[0][...] = res.astype(outs[0].dtype)
            elif epilogue == "bias":
                outs[0][...] = (res + ex[0][...]).astype(outs[0].dtype)
            else:
                outs[0][...] = res.astype(outs[0].dtype)
                outs[1][...] = ex[0][...] + ex[1][...] * res

        if nk == 1:
            finish(prod)
        else:
            acc = refs[-1]
            k = pl.program_id(len(grid) - 1)

            @pl.when(k == 0)
            def _():
                acc[...] = prod

            @pl.when(k > 0)
            def _():
                acc[...] += prod

            @pl.when(k == nk - 1)
            def _():
                finish(acc[...])

    scratch = [pltpu.VMEM(acc_shape, F32)] if nk > 1 else []
    sem = ("parallel",) * (len(grid) - 1) + ("arbitrary",)
    return pl.pallas_call(
        body, grid=grid, in_specs=[a_spec, b_spec, *extra_specs], out_specs=list(out_specs),
        out_shape=list(out_shapes), scratch_shapes=scratch, name=name, compiler_params=_cparams(*sem),
    )(a, b, *extra)


def _row_tile(rows):
    for t in (768, 512, 256):
        if rows % t == 0:
            return t
    return rows


def _mm_nn(a, b, name, out_dtype=F32, tn=1024):
    m, k = a.shape
    n = b.shape[1]
    tm = _row_tile(m)
    tn = min(tn, n)
    return _matmul(
        a, b, mode="nn", grid=(m // tm, n // tn),
        a_spec=pl.BlockSpec((tm, k), lambda i, j: (i, 0)), b_spec=pl.BlockSpec((k, tn), lambda i, j: (0, j)),
        out_shapes=[_sds((m, n), out_dtype)], out_specs=[pl.BlockSpec((tm, tn), lambda i, j: (i, j))], name=name)[0]


def _mm_out_resid(a, w_out, xres, gate, nxb, name):
    m, k = a.shape
    n = w_out.shape[1]
    tm = ROW_BLOCK
    seg = lambda i, j: (jnp.where(i >= nxb, 1, 0), 0, 0)
    return _matmul(
        a, w_out, mode="nn", grid=(m // tm, 1),
        a_spec=pl.BlockSpec((tm, k), lambda i, j: (i, 0)), b_spec=pl.BlockSpec((k, n), lambda i, j: (0, 0)),
        extra=(xres, gate), extra_specs=(pl.BlockSpec((tm, n), lambda i, j: (i, 0)), pl.BlockSpec((None, 1, n), seg)),
        out_shapes=[_sds((m, n), F32), _sds((m, n), F32)],
        out_specs=[pl.BlockSpec((tm, n), lambda i, j: (i, 0))] * 2, epilogue="resid", name=name)


def _mm_nt(a, b, name, out_dtype=F32):
    m, n = a.shape
    k = b.shape[0]
    tm = _row_tile(m)
    return _matmul(
        a, b, mode="nt", grid=(m // tm, 1),
        a_spec=pl.BlockSpec((tm, n), lambda i, j: (i, 0)), b_spec=pl.BlockSpec((k, n), lambda i, j: (0, 0)),
        out_shapes=[_sds((m, k), out_dtype)], out_specs=[pl.BlockSpec((tm, k), lambda i, j: (i, 0))], name=name)[0]


def _mm_nt_parts(a, b, name):
    p, m, kp = a.shape
    d = b.shape[0]
    tm = _row_tile(m)
    return _matmul(
        a, b, mode="nt", grid=(m // tm, p), nk=p, acc_shape=(tm, d),
        a_spec=pl.BlockSpec((None, tm, kp), lambda i, q: (q, i, 0)), b_spec=pl.BlockSpec((d, kp), lambda i, q: (0, q)),
        out_shapes=[_sds((m, d), F32)], out_specs=[pl.BlockSpec((tm, d), lambda i, q: (i, 0))], name=name)[0]


def _mm_tn(a, b, name, out_dtype, tm=512):
    r, m = a.shape
    n = b.shape[1]
    tm = min(tm, m)
    tn = min(1024, n)
    return _matmul(
        a, b, mode="tn", grid=(m // tm, n // tn),
        a_spec=pl.BlockSpec((r, tm), lambda i, j: (0, i)), b_spec=pl.BlockSpec((r, tn), lambda i, j: (0, j)),
        out_shapes=[_sds((m, n), out_dtype)], out_specs=[pl.BlockSpec((tm, tn), lambda i, j: (i, j))], name=name)[0]


def _mm_tn_parts(a, b, name, out_dtype, tm=512):
    r, m = a.shape
    p, _, np_ = b.shape
    tm = min(tm, m)
    return _matmul(
        a, b, mode="tn", grid=(m // tm, p),
        a_spec=pl.BlockSpec((r, tm), lambda i, q: (0, i)), b_spec=pl.BlockSpec((None, r, np_), lambda i, q: (q, 0, 0)),
        out_shapes=[_sds((m, p * np_), out_dtype)], out_specs=[pl.BlockSpec((tm, np_), lambda i, q: (i, q))],
        name=name)[0]


def _seg_map(nxb):
    return lambda i: (jnp.where(i >= nxb, 1, 0), 0, 0)


def _normmod_fwd(x, g, scale, shift, nxb, name):
    rows, d = x.shape
    tr = ROW_BLOCK

    def body(x_ref, g_ref, sc_ref, sh_ref, h_ref, r_ref):
        xv = x_ref[...]
        r = lax.rsqrt(jnp.mean(xv * xv, axis=-1, keepdims=True) + EPS)
        h = (xv * r) * g_ref[...] * (1.0 + sc_ref[...]) + sh_ref[...]
        h_ref[...] = h.astype(BF16)
        r_ref[...] = r

    row = pl.BlockSpec((tr, d), lambda i: (i, 0))
    vec = pl.BlockSpec((None, 1, d), _seg_map(nxb))
    return pl.pallas_call(
        body, grid=(rows // tr,), in_specs=[row, pl.BlockSpec((1, d), lambda i: (0, 0)), vec, vec],
        out_specs=[row, pl.BlockSpec((tr, 1), lambda i: (i, 0))],
        out_shape=[_sds((rows, d), BF16), _sds((rows, 1), F32)], name=name, compiler_params=_cparams("parallel"),
    )(x, g, scale, shift)


def _normmod_bwd(dh, x, r, g, scale, dres, nxb, name):
    rows, d = x.shape
    tr = ROW_BLOCK
    nres = dres.shape[0] // tr
    nseg = scale.shape[0]

    def body(dh_ref, x_ref, r_ref, g_ref, sc_ref, dres_ref, dx_ref, dsh_ref, dge_ref):
        i = pl.program_id(0)
        dhv = dh_ref[...]
        rv = r_ref[...]
        xn = x_ref[...] * rv
        dxn = dhv * (g_ref[...] * (1.0 + sc_ref[...]))
        dx = rv * (dxn - xn * jnp.mean(dxn * xn, axis=-1, keepdims=True))

        @pl.when(i < nres)
        def _():
            dx_ref[...] = dx + dres_ref[...]

        @pl.when(i >= nres)
        def _():
            dx_ref[...] = dx

        first = jnp.logical_or(i == 0, i == nxb)
        s_dh = jnp.sum(dhv, axis=0, keepdims=True)
        s_ge = jnp.sum(dhv * xn, axis=0, keepdims=True)

        @pl.when(first)
        def _():
            dsh_ref[...] = s_dh
            dge_ref[...] = s_ge

        @pl.when(jnp.logical_not(first))
        def _():
            dsh_ref[...] += s_dh
            dge_ref[...] += s_ge

    row = pl.BlockSpec((tr, d), lambda i: (i, 0))
    vec = pl.BlockSpec((None, 1, d), _seg_map(nxb))
    return pl.pallas_call(
        body, grid=(rows // tr,),
        in_specs=[row, row, pl.BlockSpec((tr, 1), lambda i: (i, 0)), pl.BlockSpec((1, d), lambda i: (0, 0)), vec,
                  pl.BlockSpec((tr, d), lambda i: (jnp.minimum(i, nres - 1), 0))],
        out_specs=[row, vec, vec],
        out_shape=[_sds((rows, d), F32), _sds((nseg, 1, d), F32), _sds((nseg, 1, d), F32)],
        name=name, compiler_params=_cparams("arbitrary"),
    )(dh, x, r, g, scale, dres)


def _gate_bwd(dxo, yx, gate, nxb, name):
    rows, d = yx.shape
    tr = ROW_BLOCK
    nseg = gate.shape[0]

    def body(dx_ref, yx_ref, gt_ref, dyx_ref, dg_ref):
        i = pl.program_id(0)
        dxv = dx_ref[...]
        dyx_ref[...] = (dxv * gt_ref[...]).astype(BF16)
        s = jnp.sum(dxv * yx_ref[...], axis=0, keepdims=True)
        first = jnp.logical_or(i == 0, i == nxb)

        @pl.when(first)
        def _():
            dg_ref[...] = s

        @pl.when(jnp.logical_not(first))
        def _():
            dg_ref[...] += s

    row = pl.BlockSpec((tr, d), lambda i: (i, 0))
    vec = pl.BlockSpec((None, 1, d), _seg_map(nxb))
    return pl.pallas_call(
        body, grid=(rows // tr,), in_specs=[row, row, vec], out_specs=[row, vec],
        out_shape=[_sds((rows, d), BF16), _sds((nseg, 1, d), F32)], name=name, compiler_params=_cparams("arbitrary"),
    )(dxo, yx, gate)


_PAD_TOP = 16
_PAD_BOT = 32


def _window_sum(buf, xv, lo, n):
    t = xv.shape[0]
    c = xv.shape[1]
    tp = t + _PAD_TOP + _PAD_BOT
    buf[pl.ds(0, _PAD_TOP), :] = jnp.zeros((_PAD_TOP, c), F32)
    buf[pl.ds(_PAD_TOP, t), :] = xv
    buf[pl.ds(_PAD_TOP + t, _PAD_BOT), :] = jnp.zeros((_PAD_BOT, c), F32)
    p = buf[...]
    k = 1
    while k < n:
        p = p + pltpu.roll(p, tp - k, 0)
        k *= 2
    if lo:
        p = pltpu.roll(p, -lo, 0)
    buf[...] = p
    return buf[pl.ds(_PAD_TOP, t), :]


def _window_count(t, half):
    pos = lax.broadcasted_iota(jnp.int32, (t, 1), 0)
    return (jnp.minimum(pos + half, t) - jnp.maximum(pos - half, 0)).astype(F32)


def _segments(rows, nx):
    return [(0, nx)] + ([(nx, rows - nx)] if rows > nx else [])


def _pool_fwd(uv, nx, name):
    rows = uv.shape[0]
    w = uv.shape[1] // 2
    cb = 128
    per_group = w // len(POOL_WINDOWS) // cb
    segs = _segments(rows, nx)

    def body(u_ref, z_ref, *bufs):
        j = pl.program_id(0)
        for gi, win in enumerate(POOL_WINDOWS):
            half = win // 2

            @pl.when(jnp.logical_and(j >= gi * per_group, j < (gi + 1) * per_group))
            def _():
                for (start, length), buf in zip(segs, bufs):
                    uvv = u_ref[pl.ds(start, length), :]
                    s = _window_sum(buf, uvv, -half, win)
                    z_ref[pl.ds(start, length), :] = (s / _window_count(length, half) - uvv).astype(BF16)

    scratch = [pltpu.VMEM((length + _PAD_TOP + _PAD_BOT, cb), F32) for _, length in segs]
    return pl.pallas_call(
        body, grid=(w // cb,), in_specs=[pl.BlockSpec((rows, cb), lambda j: (0, j))],
        out_specs=pl.BlockSpec((rows, cb), lambda j: (0, j)), out_shape=_sds((rows, w), BF16),
        scratch_shapes=scratch, name=name, compiler_params=_cparams("parallel"),
    )(uv)


def _pool_bwd(dz, dgt, nx, name):
    rows, w = dz.shape
    cb = 128
    per_group = w // len(POOL_WINDOWS) // cb
    segs = _segments(rows, nx)

    def body(dz_ref, dgt_ref, o_ref, *bufs):
        j = pl.program_id(0)
        o_ref[1] = dgt_ref[...]
        for gi, win in enumerate(POOL_WINDOWS):
            half = win // 2

            @pl.when(jnp.logical_and(j >= gi * per_group, j < (gi + 1) * per_group))
            def _():
                for (start, length), buf in zip(segs, bufs):
                    dzv = dz_ref[pl.ds(start, length), :]
                    s = _window_sum(buf, dzv / _window_count(length, half), 1 - half, win)
                    o_ref[0, pl.ds(start, length), :] = (s - dzv).astype(BF16)

    scratch = [pltpu.VMEM((length + _PAD_TOP + _PAD_BOT, cb), F32) for _, length in segs]
    col = pl.BlockSpec((rows, cb), lambda j: (0, j))
    return pl.pallas_call(
        body, grid=(w // cb,), in_specs=[col, col], out_specs=pl.BlockSpec((2, rows, cb), lambda j: (0, 0, j)),
        out_shape=_sds((2, rows, w), BF16), scratch_shapes=scratch, name=name, compiler_params=_cparams("parallel"),
    )(dz, dgt)


def _grp_fwd(z, w_grp, uv, scale, name):
    rows, w = z.shape
    ng, gc, _ = w_grp.shape
    tm = _row_tile(rows)

    def body(z_ref, w_ref, gt_ref, sc_ref, mx_ref, a_ref):
        mixed = jnp.dot(z_ref[...], w_ref[...], preferred_element_type=F32)
        mx_ref[...] = mixed
        a_ref[...] = (mixed * sc_ref[...] * _silu(gt_ref[...])).astype(BF16)

    blk = pl.BlockSpec((tm, gc), lambda g, i: (i, g))
    return pl.pallas_call(
        body, grid=(ng, rows // tm),
        in_specs=[blk, pl.BlockSpec((None, gc, gc), lambda g, i: (g, 0, 0)),
                  pl.BlockSpec((tm, gc), lambda g, i: (i, ng + g)), pl.BlockSpec((1, gc), lambda g, i: (0, g))],
        out_specs=[blk, blk], out_shape=[_sds((rows, w), F32), _sds((rows, w), BF16)],
        name=name, compiler_params=_cparams("parallel", "parallel"),
    )(z, w_grp, uv, scale)


def _grp_bwd(da, mixed, uv, scale, w_grp, name):
    rows, w = da.shape
    ng, gc, _ = w_grp.shape
    tm = _row_tile(rows)

    def body(da_ref, mx_ref, gt_ref, sc_ref, w_ref, dm_ref, dz_ref, dgt_ref, dsc_ref):
        i = pl.program_id(1)
        dav = da_ref[...]
        mixed = mx_ref[...]
        gt = gt_ref[...]
        sg = _silu(gt)
        sc = sc_ref[...]
        dm = (dav * sc * sg).astype(BF16)
        dm_ref[...] = dm
        dz_ref[...] = lax.dot_general(dm, w_ref[...], _DIMS["nt"], preferred_element_type=F32)
        dgt_ref[...] = (dav * mixed * sc * _dsilu(gt)).astype(BF16)
        s = jnp.sum(dav * mixed * sg, axis=0, keepdims=True)

        @pl.when(i == 0)
        def _():
            dsc_ref[...] = s

        @pl.when(i > 0)
        def _():
            dsc_ref[...] += s

    blk = pl.BlockSpec((tm, gc), lambda g, i: (i, g))
    vec = pl.BlockSpec((1, gc), lambda g, i: (0, g))
    return pl.pallas_call(
        body, grid=(ng, rows // tm),
        in_specs=[blk, blk, pl.BlockSpec((tm, gc), lambda g, i: (i, ng + g)), vec,
                  pl.BlockSpec((None, gc, gc), lambda g, i: (g, 0, 0))],
        out_specs=[blk, blk, blk, vec],
        out_shape=[_sds((rows, w), BF16), _sds((rows, w), F32), _sds((rows, w), BF16), _sds((1, w), F32)],
        name=name, compiler_params=_cparams("parallel", "arbitrary"),
    )(da, mixed, uv, scale, w_grp)


def _grp_wgrad(z, dm, ng, name, out_dtype):
    rows, w = z.shape
    gc = w // ng

    def body(z_ref, dm_ref, o_ref):
        o_ref[...] = lax.dot_general(z_ref[...], dm_ref[...], _DIMS["tn"],
                                     preferred_element_type=F32).astype(o_ref.dtype)

    blk = pl.BlockSpec((rows, gc), lambda g: (0, g))
    return pl.pallas_call(
        body, grid=(ng,), in_specs=[blk, blk], out_specs=pl.BlockSpec((None, gc, gc), lambda g: (g, 0, 0)),
        out_shape=_sds((ng, gc, gc), out_dtype), name=name, compiler_params=_cparams("parallel"),
    )(z, dm)


def _shift_rows(v, by):
    t = v.shape[0]
    pos = lax.broadcasted_iota(jnp.int32, v.shape, 0)
    rolled = pltpu.roll(v, by % t, 0)
    keep = pos >= by if by > 0 else pos < t + by
    return jnp.where(keep, rolled, 0.0)


def _conv_specs(t, w, cb):
    return [pl.BlockSpec((t, cb), (lambda j, q=q: (0, q * (w // cb) + j))) for q in range(4)]


def _conv_fwd(p4, dw, db, name):
    t = p4.shape[0]
    w = p4.shape[1] // 4
    cb = 128

    def body(bg_ref, cg_ref, v_ref, g_ref, dw_ref, db_ref, a_ref):
        tv = cg_ref[...] * v_ref[...]
        conv = (dw_ref[0:1, :] * _shift_rows(tv, 1) + dw_ref[1:2, :] * tv + dw_ref[2:3, :] * _shift_rows(tv, -1)
                + db_ref[...])
        a_ref[...] = (bg_ref[...] * conv * _silu(g_ref[...])).astype(BF16)

    return pl.pallas_call(
        body, grid=(w // cb,),
        in_specs=_conv_specs(t, w, cb) + [pl.BlockSpec((3, cb), lambda j: (0, j)), pl.BlockSpec((1, cb), lambda j: (0, j))],
        out_specs=pl.BlockSpec((t, cb), lambda j: (0, j)), out_shape=_sds((t, w), BF16),
        name=name, compiler_params=_cparams("parallel"),
    )(p4, p4, p4, p4, dw, db)


def _conv_bwd(da, p4, dw, db, name):
    t, w = da.shape
    cb = 128

    def body(da_ref, bg_ref, cg_ref, v_ref, g_ref, dw_ref, db_ref, d4_ref, ddw_ref, ddb_ref):
        cg = cg_ref[...]
        vv = v_ref[...]
        bg = bg_ref[...]
        gv = g_ref[...]
        tv = cg * vv
        tm1 = _shift_rows(tv, 1)
        tp1 = _shift_rows(tv, -1)
        w0, w1, w2 = dw_ref[0:1, :], dw_ref[1:2, :], dw_ref[2:3, :]
        conv = w0 * tm1 + w1 * tv + w2 * tp1 + db_ref[...]
        y = bg * conv
        dav = da_ref[...]
        dy = dav * _silu(gv)
        d4_ref[3] = (dav * y * _dsilu(gv)).astype(BF16)
        d4_ref[0] = (dy * conv).astype(BF16)
        dconv = dy * bg
        ddb_ref[...] = jnp.sum(dconv, axis=0, keepdims=True)
        ddw_ref[0:1, :] = jnp.sum(dconv * tm1, axis=0, keepdims=True)
        ddw_ref[1:2, :] = jnp.sum(dconv * tv, axis=0, keepdims=True)
        ddw_ref[2:3, :] = jnp.sum(dconv * tp1, axis=0, keepdims=True)
        dt = w0 * _shift_rows(dconv, -1) + w1 * dconv + w2 * _shift_rows(dconv, 1)
        d4_ref[1] = (dt * vv).astype(BF16)
        d4_ref[2] = (dt * cg).astype(BF16)

    col = pl.BlockSpec((t, cb), lambda j: (0, j))
    tap = pl.BlockSpec((3, cb), lambda j: (0, j))
    bias = pl.BlockSpec((1, cb), lambda j: (0, j))
    return pl.pallas_call(
        body, grid=(w // cb,), in_specs=[col] + _conv_specs(t, w, cb) + [tap, bias],
        out_specs=[pl.BlockSpec((4, t, cb), lambda j: (0, 0, j)), tap, bias],
        out_shape=[_sds((4, t, w), BF16), _sds((3, w), F32), _sds((1, w), F32)],
        name=name, compiler_params=_cparams("parallel"),
    )(da, p4, p4, p4, p4, dw, db)


def _attn_mask():
    qn, kn = Q_ROWS * GRID_W, K_ROWS * GRID_W
    qr, qc = np.divmod(np.arange(qn), GRID_W)
    kr, kc = np.divmod(np.arange(kn), GRID_W)
    col0 = np.clip(qc - WIN_COLS // 2, 0, GRID_W - WIN_COLS)
    col_ok = (kc[None, :] >= col0[:, None]) & (kc[None, :] < col0[:, None] + WIN_COLS)
    first = np.zeros(qn, np.int64)
    last = np.full(qn, K_ROWS - WIN_ROWS)
    out = []
    for row0 in (first, qr, last):
        row_ok = (kr[None, :] >= row0[:, None]) & (kr[None, :] < row0[:, None] + WIN_ROWS)
        out.append(np.where(row_ok & col_ok, 0.0, NEG))
    return jnp.asarray(np.stack(out), F32)


def _rpb_onehot():
    qc, kc = np.divmod(np.arange(GRID_W * GRID_W), GRID_W)
    e = (kc - qc + WIN_COLS - 1)[None, :] == np.arange(128)[:, None]
    return jnp.asarray(e, F32)


_KW = K_ROWS * GRID_W
_QB = Q_ROWS * GRID_W
_T3_FRONT = 4
_T3_LANES = 1536
_PAIR = 2 * HEAD_DIM
_BIAS_BASE = (WIN_ROWS - 1 + _T3_FRONT, WIN_ROWS // 2 - 1 + _T3_FRONT, _T3_FRONT - 1)


class _Comm:
    def __init__(self, ins, outs, sems, start, finish):
        self.ins, self.outs, self.sems, self.start, self.finish = list(ins), list(outs), list(sems), start, finish


def _bias_pieces(cls):
    out = []
    for qr in range(Q_ROWS):
        off = (_BIAS_BASE[cls] - qr) * GRID_W
        out.append((qr, off % 128 != 0, off - (off % 128)))
    return out


def _block_class(b, nblk, fn):
    conds = (b == 0, jnp.logical_and(b > 0, b < nblk - 1), b == nblk - 1)
    for cls, cond in enumerate(conds):
        pl.when(cond)(functools.partial(fn, cls))


def _attn_geometry(p4, nx):
    rows = p4.shape[0]
    w = p4.shape[1] // 4
    nhp = w // _PAIR
    nblk = nx // _QB
    qspec = lambda col: pl.BlockSpec((_QB, _PAIR), lambda hp, b: (b, col * nhp + hp))
    kspec = lambda col: pl.BlockSpec((rows, _PAIR), lambda hp, b: (0, col * nhp + hp))
    tspec = pl.BlockSpec((2, GRID_W, _T3_LANES), lambda hp, b: (hp, 0, 0))
    mspec = pl.BlockSpec((None, _QB, _KW), lambda hp, b: (jnp.where(b == 0, 0, jnp.where(b == nblk - 1, 2, 1)), 0, 0))
    lspec = pl.BlockSpec((None, _QB, 2), lambda hp, b: (hp, b, 0))
    ospec = pl.BlockSpec((_QB, _PAIR), lambda hp, b: (b, hp))
    return rows, w, nhp, nblk, qspec, kspec, tspec, mspec, lspec, ospec


def _window_start(b, nx):
    return pl.multiple_of(jnp.clip(b * _QB - PAD_ROWS * GRID_W, 0, nx - _KW), _QB)


def _load_bias(bias_ref, t3_ref, t3s_ref, b, nblk):
    def fill(cls):
        for h in range(2):
            for qr, shifted, off in _bias_pieces(cls):
                src = t3s_ref if shifted else t3_ref
                bias_ref[h, qr * GRID_W:(qr + 1) * GRID_W, :] = src[h, :, off:off + _KW]

    _block_class(b, nblk, fill)


def _attn_fwd(p4, t3, t3s, mask, nx, name, comm=None):
    rows, w, nhp, nblk, qspec, kspec, tspec, mspec, lspec, ospec = _attn_geometry(p4, nx)
    n_ctx = rows - nx
    n_cin, n_cout = (len(comm.ins), len(comm.outs)) if comm else (0, 0)

    def body(*refs):
        q_ref, k_ref, v_ref, g_ref, t3_ref, t3s_ref, m_ref = refs[:7]
        cin = refs[7:7 + n_cin]
        a_ref, o_ref, lse_ref = refs[7 + n_cin:10 + n_cin]
        cout = refs[10 + n_cin:10 + n_cin + n_cout]
        bias_ref = refs[10 + n_cin + n_cout]
        sems = refs[11 + n_cin + n_cout:]
        hp, b = pl.program_id(0), pl.program_id(1)
        if comm:
            pl.when(jnp.logical_and(hp == 0, b == 0))(lambda: comm.start(cin, cout, sems))
        start = _window_start(b, nx)
        _load_bias(bias_ref, t3_ref, t3s_ref, b, nblk)
        qf = q_ref[...] * HEAD_DIM ** -0.5
        kw = k_ref[pl.ds(start, _KW), :].astype(BF16)
        vw = v_ref[pl.ds(start, _KW), :].astype(BF16)
        kcv = k_ref[pl.ds(nx, n_ctx), :].astype(BF16)
        vcv = v_ref[pl.ds(nx, n_ctx), :].astype(BF16)
        lane = lax.broadcasted_iota(jnp.int32, (1, _PAIR), 1)
        outs, lses = [], []
        for h in range(2):
            mine = (lane >= HEAD_DIM) if h else (lane < HEAD_DIM)
            qm = jnp.where(mine, qf, 0.0).astype(BF16)
            s_loc = lax.dot_general(qm, kw, _DIMS["nt"], preferred_element_type=F32) + bias_ref[h] + m_ref[...]
            s_ctx = lax.dot_general(qm, kcv, _DIMS["nt"], preferred_element_type=F32)
            mx = jnp.maximum(jnp.max(s_loc, axis=-1, keepdims=True), jnp.max(s_ctx, axis=-1, keepdims=True))
            p_loc = jnp.exp(s_loc - mx)
            p_ctx = jnp.exp(s_ctx - mx)
            den = jnp.sum(p_loc, axis=-1, keepdims=True) + jnp.sum(p_ctx, axis=-1, keepdims=True)
            inv = 1.0 / den
            o = jnp.dot((p_loc * inv).astype(BF16), vw, preferred_element_type=F32)
            o = o + jnp.dot((p_ctx * inv).astype(BF16), vcv, preferred_element_type=F32)
            outs.append(o)
            lses.append(mx + jnp.log(den))
        o = jnp.where(lane < HEAD_DIM, outs[0], outs[1])
        o_ref[...] = o
        a_ref[...] = (o * _silu(g_ref[...])).astype(BF16)
        col = lax.broadcasted_iota(jnp.int32, (1, 2), 1)
        lse_ref[...] = jnp.where(col == 0, lses[0], lses[1])
        if comm:
            pl.when(jnp.logical_and(hp == nhp - 1, b == nblk - 1))(lambda: comm.finish(cin, cout, sems))

    res = pl.pallas_call(
        body, grid=(nhp, nblk),
        in_specs=[qspec(0), kspec(1), kspec(2), qspec(3), tspec, tspec, mspec] + [HBM_SPEC] * n_cin,
        out_specs=[ospec, ospec, lspec] + [HBM_SPEC] * n_cout,
        out_shape=[_sds((nx, w), BF16), _sds((nx, w), F32), _sds((nhp, nx, 2), F32)] + (comm.outs if comm else []),
        scratch_shapes=[pltpu.VMEM((2, _QB, _KW), F32)] + (comm.sems if comm else []),
        name=name, compiler_params=_cparams("arbitrary", "arbitrary"),
    )(p4, p4, p4, p4, t3, t3s, mask, *(comm.ins if comm else []))
    return res[:3], res[3:]


def _attn_bwd(p4, t3, t3s, mask, o, lse, da, nx, name, comm=None):
    rows, w, nhp, nblk, qspec, kspec, tspec, mspec, lspec, ospec = _attn_geometry(p4, nx)
    n_ctx = rows - nx
    n_cin, n_cout = (len(comm.ins), len(comm.outs)) if comm else (0, 0)

    def body(*refs):
        q_ref, k_ref, v_ref, g_ref, t3_ref, t3s_ref, m_ref, o_ref, lse_ref, da_ref = refs[:10]
        cin = refs[10:10 + n_cin]
        d4_ref, dt3_ref, dt3s_ref = refs[10 + n_cin:13 + n_cin]
        cout = refs[13 + n_cin:13 + n_cin + n_cout]
        bias_ref, dk_ref, dv_ref = refs[13 + n_cin + n_cout:16 + n_cin + n_cout]
        sems = refs[16 + n_cin + n_cout:]
        hp, b = pl.program_id(0), pl.program_id(1)
        if comm:
            pl.when(jnp.logical_and(hp == 0, b == 0))(lambda: comm.start(cin, cout, sems))
        start = _window_start(b, nx)
        here = pl.multiple_of(b * _QB, _QB)

        @pl.when(b == 0)
        def _():
            dk_ref[...] = jnp.zeros(dk_ref.shape, F32)
            dv_ref[...] = jnp.zeros(dv_ref.shape, F32)
            dt3_ref[...] = jnp.zeros(dt3_ref.shape, F32)
            dt3s_ref[...] = jnp.zeros(dt3s_ref.shape, F32)
            d4_ref[0, pl.ds(nx, n_ctx), :] = jnp.zeros((n_ctx, _PAIR), BF16)
            d4_ref[3, pl.ds(nx, n_ctx), :] = jnp.zeros((n_ctx, _PAIR), BF16)

        _load_bias(bias_ref, t3_ref, t3s_ref, b, nblk)
        gv = g_ref[...]
        dav = da_ref[...]
        ov = o_ref[...]
        dov = dav * _silu(gv)
        d4_ref[3, pl.ds(here, _QB), :] = (dav * ov * _dsilu(gv)).astype(BF16)
        qf = q_ref[...] * HEAD_DIM ** -0.5
        kw = k_ref[pl.ds(start, _KW), :].astype(BF16)
        vw = v_ref[pl.ds(start, _KW), :].astype(BF16)
        kcv = k_ref[pl.ds(nx, n_ctx), :].astype(BF16)
        vcv = v_ref[pl.ds(nx, n_ctx), :].astype(BF16)
        lane = lax.broadcasted_iota(jnp.int32, (1, _PAIR), 1)
        dq = jnp.zeros((_QB, _PAIR), F32)
        for h in range(2):
            mine = (lane >= HEAD_DIM) if h else (lane < HEAD_DIM)
            qm = jnp.where(mine, qf, 0.0).astype(BF16)
            dom = jnp.where(mine, dov, 0.0)
            dob = dom.astype(BF16)
            lse = lse_ref[:, h:h + 1]
            s_loc = lax.dot_general(qm, kw, _DIMS["nt"], preferred_element_type=F32)
            p_loc = jnp.exp(s_loc + bias_ref[h] + m_ref[...] - lse)
            p_ctx = jnp.exp(lax.dot_general(qm, kcv, _DIMS["nt"], preferred_element_type=F32) - lse)
            delta = jnp.sum(dom * ov, axis=-1, keepdims=True)
            ds_loc = p_loc * (lax.dot_general(dob, vw, _DIMS["nt"], preferred_element_type=F32) - delta)
            ds_ctx = p_ctx * (lax.dot_general(dob, vcv, _DIMS["nt"], preferred_element_type=F32) - delta)
            dsb_loc = ds_loc.astype(BF16)
            dsb_ctx = ds_ctx.astype(BF16)
            dq_h = (jnp.dot(dsb_loc, kw, preferred_element_type=F32)
                    + jnp.dot(dsb_ctx, kcv, preferred_element_type=F32))
            dq = dq + jnp.where(mine, dq_h, 0.0)
            dk_ref[pl.ds(start, _KW), :] += lax.dot_general(dsb_loc, qm, _DIMS["tn"], preferred_element_type=F32)
            dv_ref[pl.ds(start, _KW), :] += lax.dot_general(p_loc.astype(BF16), dob, _DIMS["tn"],
                                                            preferred_element_type=F32)
            dk_ref[pl.ds(nx, n_ctx), :] += lax.dot_general(dsb_ctx, qm, _DIMS["tn"], preferred_element_type=F32)
            dv_ref[pl.ds(nx, n_ctx), :] += lax.dot_general(p_ctx.astype(BF16), dob, _DIMS["tn"],
                                                           preferred_element_type=F32)
            bias_ref[h] = ds_loc
        d4_ref[0, pl.ds(here, _QB), :] = (dq * HEAD_DIM ** -0.5).astype(BF16)

        def scatter(cls):
            for h in range(2):
                for qr, shifted, off in _bias_pieces(cls):
                    dst = dt3s_ref if shifted else dt3_ref
                    dst[h, :, off:off + _KW] += bias_ref[h, qr * GRID_W:(qr + 1) * GRID_W, :]

        _block_class(b, nblk, scatter)

        @pl.when(b == nblk - 1)
        def _():
            d4_ref[1] = dk_ref[...].astype(BF16)
            d4_ref[2] = dv_ref[...].astype(BF16)

        if comm:
            pl.when(jnp.logical_and(hp == nhp - 1, b == nblk - 1))(lambda: comm.finish(cin, cout, sems))

    tshape = _sds(t3.shape, F32)
    res = pl.pallas_call(
        body, grid=(nhp, nblk),
        in_specs=[qspec(0), kspec(1), kspec(2), qspec(3), tspec, tspec, mspec, ospec, lspec, ospec] + [HBM_SPEC] * n_cin,
        out_specs=[pl.BlockSpec((4, rows, _PAIR), lambda hp, b: (0, 0, hp)), tspec, tspec] + [HBM_SPEC] * n_cout,
        out_shape=[_sds((4, rows, w), BF16), tshape, tshape] + (comm.outs if comm else []),
        scratch_shapes=[pltpu.VMEM((2, _QB, _KW), F32), pltpu.VMEM((rows, _PAIR), F32), pltpu.VMEM((rows, _PAIR), F32)]
        + (comm.sems if comm else []),
        name=name, compiler_params=_cparams("arbitrary", "arbitrary"),
    )(p4, p4, p4, p4, t3, t3s, mask, o, lse, da, *(comm.ins if comm else []))
    return res[:3], res[3:]


def _final(x, g, target, name):
    rows, d = x.shape
    tr = ROW_BLOCK
    nblk = rows // tr

    def body(x_ref, g_ref, t_ref, loss_ref, dx_ref, dg_ref, acc_ref):
        i = pl.program_id(0)
        xv = x_ref[...]
        gv = g_ref[...]
        r = lax.rsqrt(jnp.mean(xv * xv, axis=-1, keepdims=True) + EPS)
        xn = xv * r
        err = xn * gv - t_ref[...]
        dy = err * (1.0 / d)
        dxn = dy * gv
        dx_ref[...] = r * (dxn - xn * jnp.mean(dxn * xn, axis=-1, keepdims=True))
        s_g = jnp.sum(dy * xn, axis=0, keepdims=True)
        s_l = jnp.sum(jnp.mean(err * err, axis=-1, keepdims=True), axis=0, keepdims=True)

        @pl.when(i == 0)
        def _():
            dg_ref[...] = s_g
            acc_ref[...] = s_l

        @pl.when(i > 0)
        def _():
            dg_ref[...] += s_g
            acc_ref[...] += s_l

        @pl.when(i == nblk - 1)
        def _():
            loss_ref[...] = jnp.broadcast_to(0.5 * acc_ref[...], loss_ref.shape)

    row = pl.BlockSpec((tr, d), lambda i: (i, 0))
    vec = pl.BlockSpec((1, d), lambda i: (0, 0))
    return pl.pallas_call(
        body, grid=(nblk,), in_specs=[row, vec, row],
        out_specs=[pl.BlockSpec((1, 128), lambda i: (0, 0)), row, vec],
        out_shape=[_sds((1, 128), F32), _sds((rows, d), F32), _sds((1, d), F32)],
        scratch_shapes=[pltpu.VMEM((1, 1), F32)], name=name, compiler_params=_cparams("arbitrary"),
    )(x, g, target)


def _as2d(a):
    if a.ndim == 1:
        return a.reshape(-1, 128) if a.shape[0] % 128 == 0 else a.reshape(1, -1)
    return a.reshape(-1, a.shape[-1])


def _adamw(w, g, m, v, name):
    shape = w.shape
    w2, g2, m2, v2 = (_as2d(t) for t in (w, g.reshape(shape), m, v))
    rows, cols = w2.shape
    tr = 512 if rows % 512 == 0 else rows
    c1 = 1.0 - ADAM_B1 ** ADAM_STEP
    c2 = 1.0 - ADAM_B2 ** ADAM_STEP

    def body(w_ref, g_ref, m_ref, v_ref, d_ref, nm_ref, nv_ref):
        gv = g_ref[...]
        nm = ADAM_B1 * m_ref[...] + (1.0 - ADAM_B1) * gv
        nv = ADAM_B2 * v_ref[...] + (1.0 - ADAM_B2) * (gv * gv)
        nm_ref[...] = nm
        nv_ref[...] = nv
        d_ref[...] = -ADAM_LR * ((nm / c1) / (jnp.sqrt(nv / c2) + ADAM_EPS) + ADAM_WD * w_ref[...])

    blk = pl.BlockSpec((tr, cols), lambda i: (i, 0))
    outs = pl.pallas_call(
        body, grid=(rows // tr,), in_specs=[blk] * 4, out_specs=[blk] * 3,
        out_shape=[_sds((rows, cols), F32)] * 3, name=name, compiler_params=_cparams("parallel"),
    )(w2, g2, m2, v2)
    return tuple(t.reshape(shape) for t in outs)


def _sum_lead(x, name, out_dtype=F32):
    n, rows, cols = x.shape
    tr = 512 if rows % 512 == 0 else rows

    def body(x_ref, o_ref):
        acc = x_ref[0].astype(F32)
        for k in range(1, n):
            acc = acc + x_ref[k].astype(F32)
        o_ref[...] = acc.astype(out_dtype)

    return pl.pallas_call(
        body, grid=(rows // tr,), in_specs=[pl.BlockSpec((n, tr, cols), lambda i: (0, i, 0))],
        out_specs=pl.BlockSpec((tr, cols), lambda i: (i, 0)), out_shape=_sds((rows, cols), out_dtype),
        name=name, compiler_params=_cparams("parallel"),
    )(x)


_NO_CTX = 1 << 30


def _seg_vecs(mod_l, which, nseg):
    return mod_l[:nseg, which][:, None, :]


def _norm_grads(dshift, dgeff, dgate, g, scale):
    nseg, _, d = dshift.shape
    dmod = jnp.stack([dshift[:, 0], dgeff[:, 0] * g, dgate[:, 0]], axis=1)
    if nseg == 1:
        dmod = jnp.concatenate([dmod, jnp.zeros((1, 3, d), F32)], axis=0)
    dg = jnp.sum(dgeff[:, 0] * (1.0 + scale[:, 0]), axis=0)
    return dmod, dg


def _pool_layer(xin, g, mod_l, w_in, w_grp, w_out, pscale, nx, tag):
    rows = xin.shape[0]
    nseg = 2 if rows > nx else 1
    nxb = nx // ROW_BLOCK if nseg == 2 else _NO_CTX
    shift, scale, gate = (_seg_vecs(mod_l, k, nseg) for k in range(3))
    h, r = _normmod_fwd(xin, g, scale, shift, nxb, f"norm_fwd_{tag}")
    uv = _mm_nn(h, w_in, f"w_in_fwd_{tag}")
    z = _pool_fwd(uv, nx, f"pool_fwd_{tag}")
    mixed, a = _grp_fwd(z, w_grp, uv, pscale, f"grp_fwd_{tag}")
    yx, xout = _mm_out_resid(a, w_out, xin, gate, nxb, f"w_out_fwd_{tag}")

    def backward(dxo):
        dyx, dgate = _gate_bwd(dxo, yx, gate, nxb, f"gate_bwd_{tag}")
        da = _mm_nt(dyx, w_out, f"w_out_bwd_{tag}")
        gw_out = _mm_tn(a, dyx, f"w_out_grad_{tag}", BF16)
        dm, dz, dgt, dscale = _grp_bwd(da, mixed, uv, pscale, w_grp, f"grp_bwd_{tag}")
        gw_grp = _grp_wgrad(z, dm, w_grp.shape[0], f"grp_grad_{tag}", BF16)
        duv = _pool_bwd(dz, dgt, nx, f"pool_bwd_{tag}")
        dh = _mm_nt_parts(duv, w_in, f"w_in_bwd_{tag}")
        gw_in = _mm_tn_parts(h, duv, f"w_in_grad_{tag}", BF16)
        dx, dshift, dgeff = _normmod_bwd(dh, xin, r, g, scale, dxo, nxb, f"norm_bwd_{tag}")
        dmod, dg = _norm_grads(dshift, dgeff, dgate, g[0], scale)
        return dx, dmod, dg, dict(w_in=gw_in, w_grp=gw_grp, w_out=gw_out, scale=dscale)

    return xout, backward


def _rpb_tables(rpb, onehot):
    nh, na, nb = rpb.shape
    flat = jnp.pad(rpb.reshape(nh * na, nb), ((0, 0), (0, 128 - nb)))
    t1 = _matmul(
        flat, onehot, mode="nn", grid=(1, 4), exact=True,
        a_spec=pl.BlockSpec((nh * na, 128), lambda i, j: (0, 0)), b_spec=pl.BlockSpec((128, 1024), lambda i, j: (0, j)),
        out_shapes=[_sds((nh * na, GRID_W * GRID_W), F32)], out_specs=[pl.BlockSpec((nh * na, 1024), lambda i, j: (0, j))],
        name="rpb_table")[0]
    t3 = t1.reshape(nh, na, GRID_W, GRID_W).transpose(0, 2, 1, 3).reshape(nh, GRID_W, na * GRID_W)
    front = _T3_FRONT * GRID_W
    back = _T3_LANES - na * GRID_W - front
    return (jnp.pad(t3, ((0, 0), (0, 0), (front, back))),
            jnp.pad(t3, ((0, 0), (0, 0), (front - GRID_W, back + GRID_W))))


def _rpb_grad(dt3, dt3s, onehot, nh, na, nb):
    def fold(t, front, name):
        flat = t[:, :, front:front + na * GRID_W].reshape(nh, GRID_W, na, GRID_W).transpose(0, 2, 1, 3)
        flat = flat.reshape(nh * na, -1)
        out = _matmul(
            flat, onehot, mode="nt", grid=(1, 4), nk=4, acc_shape=(nh * na, 128), exact=True,
            a_spec=pl.BlockSpec((nh * na, 1024), lambda i, k: (0, k)), b_spec=pl.BlockSpec((128, 1024), lambda i, k: (0, k)),
            out_shapes=[_sds((nh * na, 128), F32)], out_specs=[pl.BlockSpec((nh * na, 128), lambda i, k: (0, 0))],
            name=name)[0]
        return out[:, :nb].reshape(nh, na, nb)

    front = _T3_FRONT * GRID_W
    return fold(dt3, front, "rpb_grad_a") + fold(dt3s, front - GRID_W, "rpb_grad_b")


def _na_layer(xc, g, mod_l, w_in, rpb, w_out, nx, consts, comm=None):
    nxb = nx // ROW_BLOCK
    mask, onehot = consts
    shift, scale = _seg_vecs(mod_l, 0, 2), _seg_vecs(mod_l, 1, 2)
    gate = _seg_vecs(mod_l, 2, 1)
    h, r = _normmod_fwd(xc, g, scale, shift, nxb, "norm_fwd_na")
    p4 = _mm_nn(h, w_in, "w_in_fwd_na")
    t3, t3s = _rpb_tables(rpb, onehot)
    (a, o, lse), carried = _attn_fwd(p4, t3, t3s, mask, nx, "attn_fwd", comm)
    yx, xout = _mm_out_resid(a, w_out, xc, gate, _NO_CTX, "w_out_fwd_na")

    def backward(dxo, comm=None):
        dyx, dgate = _gate_bwd(dxo, yx, gate, _NO_CTX, "gate_bwd_na")
        da = _mm_nt(dyx, w_out, "w_out_bwd_na")
        gw_out = _mm_tn(a, dyx, "w_out_grad_na", BF16)
        (d4, dt3, dt3s), carried_bwd = _attn_bwd(p4, t3, t3s, mask, o, lse, da, nx, "attn_bwd", comm)
        dh = _mm_nt_parts(d4, w_in, "w_in_bwd_na")
        gw_in = _mm_tn_parts(h, d4, "w_in_grad_na", BF16)
        dx, dshift, dgeff = _normmod_bwd(dh, xc, r, g, scale, dxo, nxb, "norm_bwd_na")
        dgate2 = jnp.concatenate([dgate, jnp.zeros_like(dgate)], axis=0)
        dmod, dg = _norm_grads(dshift, dgeff, dgate2, g[0], scale)
        drpb = _rpb_grad(dt3, dt3s, onehot, *rpb.shape)
        return dx, dmod, dg, dict(w_in=gw_in, w_out=gw_out, rpb=drpb), carried_bwd

    return xout, backward, carried


def _conv_layer(xin, g, mod_l, w_in, dw, db, w_out):
    shift, scale, gate = (_seg_vecs(mod_l, k, 1) for k in range(3))
    h, r = _normmod_fwd(xin, g, scale, shift, _NO_CTX, "norm_fwd_conv")
    p4 = _mm_nn(h, w_in, "w_in_fwd_conv")
    a = _conv_fwd(p4, dw, db, "conv_fwd")
    yx, xout = _mm_out_resid(a, w_out, xin, gate, _NO_CTX, "w_out_fwd_conv")

    def backward(dxo):
        dyx, dgate = _gate_bwd(dxo, yx, gate, _NO_CTX, "gate_bwd_conv")
        da = _mm_nt(dyx, w_out, "w_out_bwd_conv")
        gw_out = _mm_tn(a, dyx, "w_out_grad_conv", BF16)
        d4, ddw, ddb = _conv_bwd(da, p4, dw, db, "conv_bwd")
        dh = _mm_nt_parts(d4, w_in, "w_in_bwd_conv")
        gw_in = _mm_tn_parts(h, d4, "w_in_grad_conv", BF16)
        dx, dshift, dgeff = _normmod_bwd(dh, xin, r, g, scale, dxo, _NO_CTX, "norm_bwd_conv")
        dmod, dg = _norm_grads(dshift, dgeff, dgate, g[0], scale)
        return dx, dmod, dg, dict(w_in=gw_in, w_out=gw_out, dw=ddw, db=ddb)

    return xout, backward


def _example_step(x, ctx, target, mod, norm_g, final_g, wts, late_comm=None, late_weights=None, grad_comm=None):
    nx = x.shape[0]
    consts = (_attn_mask(), _rpb_onehot())
    g_rows = [norm_g[i:i + 1] for i in range(4)]
    xc0 = jnp.concatenate([x, ctx], axis=0)
    xc1, bwd0 = _pool_layer(xc0, g_rows[0], mod[0], wts["pool_w_in"][0], wts["pool_w_grp"][0], wts["pool_w_out"][0],
                            wts["pool_scale"][0:1], nx, "p0")
    x2, bwd1, carried = _na_layer(xc1, g_rows[1], mod[1], wts["na_w_in"], wts["na_rpb"], wts["na_w_out"], nx, consts,
                                  late_comm)
    if late_weights is not None:
        wts = {**wts, **late_weights(carried)}
    x3, bwd2 = _conv_layer(x2, g_rows[2], mod[2], wts["conv_w_in"], wts["conv_dw"], wts["conv_db"], wts["conv_w_out"])
    x4, bwd3 = _pool_layer(x3, g_rows[3], mod[3], wts["pool_w_in"][1], wts["pool_w_grp"][1], wts["pool_w_out"][1],
                           wts["pool_scale"][1:2], nx, "p3")
    loss, dx4, dfinal_g = _final(x4, final_g, target, "loss_head")
    dx3, dmod3, dg3, gr3 = bwd3(dx4)
    dx2, dmod2, dg2, gr2 = bwd2(dx3)
    dxc1, dmod1, dg1, gr1, carried_bwd = bwd1(dx2, grad_comm(gr3, gr2) if grad_comm else None)
    dxc0, dmod0, dg0, gr0 = bwd0(dxc1)
    return dict(
        loss=loss, grad_x=dxc0[:nx], dmod=jnp.stack([dmod0, dmod1, dmod2, dmod3]),
        dnorm_g=jnp.stack([dg0, dg1, dg2, dg3]), dfinal_g=dfinal_g, layers=(gr0, gr1, gr2, gr3), carried=carried_bwd)


_AXES = ("x", "y", "c")
_CHIP_FLIPS = ((1, 0), (0, 1), (1, 1))


def _position():
    return tuple(lax.axis_index(a) for a in _AXES)


def _flipped(pos, flip):
    return tuple(1 - p if f else p for p, f in zip(pos, flip))


def _all_gather(v, axes, name):
    flips = [f for f in np.ndindex(2, 2, 2) if any(f) and all(a in axes or not b for a, b in zip(_AXES, f))]
    n = len(flips) + 1

    def body(v_ref, o_ref, send_sems, recv_sems, local_sem):
        pos = _position()
        slot = 0
        for a, p in zip(_AXES, pos):
            if a in axes:
                slot = 2 * slot + p
        local = pltpu.make_async_copy(v_ref, o_ref.at[slot], local_sem)
        local.start()
        copies = []
        for k, flip in enumerate(flips):
            cp = pltpu.make_async_remote_copy(v_ref, o_ref.at[slot], send_sems.at[k], recv_sems.at[k],
                                              device_id=_flipped(pos, flip), device_id_type=MESH)
            cp.start()
            copies.append(cp)
        for cp in copies:
            cp.wait()
        local.wait()

    return pl.pallas_call(
        body, in_specs=[HBM_SPEC], out_specs=HBM_SPEC, out_shape=_sds((n,) + v.shape, v.dtype),
        scratch_shapes=[pltpu.SemaphoreType.DMA((n - 1,)), pltpu.SemaphoreType.DMA((n - 1,)), pltpu.SemaphoreType.DMA(())],
        name=name,
    )(v)


class _Item:
    def __init__(self, key, layer, shape, shard_axis, half_axis):
        self.key, self.layer, self.shape = key, layer, tuple(shape)
        self.shard_axis, self.half_axis = shard_axis, half_axis
        self.shard = shape[shard_axis] // 4
        self.half = shape[half_axis] // 2

    def sized(self, shard=False, half=False):
        s = list(self.shape)
        if shard:
            s[self.shard_axis] = self.shard
        if half:
            s[self.half_axis] = self.half
        return tuple(s)

    def window(self, ref, chip=None, half=None):
        idx = [slice(None)] * len(self.shape)
        if chip is not None:
            idx[self.shard_axis] = pl.ds(chip * self.shard, self.shard)
        if half is not None:
            idx[self.half_axis] = pl.ds(half * self.half, self.half)
        return ref.at[tuple(idx)]


def _items(d, w):
    out = []
    for j in range(2):
        out += [_Item("pool_w_in", j, (d, 2 * w), 1, 0), _Item("pool_w_grp", j, (4, w // 4, w // 4), 1, 0),
                _Item("pool_w_out", j, (w, d), 0, 1)]
    out += [_Item("na_w_in", 0, (d, 4 * w), 1, 0), _Item("na_w_out", 0, (w, d), 0, 1),
            _Item("conv_w_in", 0, (d, 4 * w), 1, 0), _Item("conv_w_out", 0, (w, d), 0, 1)]
    return out


def _gather_weights(shards, items, name):
    comm = _gather_comm(shards, items)

    def body(*refs):
        n = len(items)
        comm.start(refs[:n], refs[n:2 * n], refs[2 * n:])
        comm.finish(refs[:n], refs[n:2 * n], refs[2 * n:])

    return pl.pallas_call(
        body, in_specs=[HBM_SPEC] * len(items), out_specs=[HBM_SPEC] * len(items), out_shape=comm.outs,
        scratch_shapes=comm.sems, name=name,
    )(*shards)


def _gather_comm(shards, items):
    n = len(items)

    def copies(src, dst, sems, onward):
        send_a, recv_a, send_b, recv_b, send_c, recv_c = sems
        x, y, c = _position()
        chip = 2 * x + y
        sibling = (x, y, 1 - c)
        own, out, fwd, fwd_in = [], [], [], []
        for i, it in enumerate(items):
            own.append(pltpu.make_async_remote_copy(src[i], it.window(dst[i], chip=chip), send_c.at[i], recv_c.at[i],
                                                    device_id=sibling, device_id_type=MESH))
            for k, flip in enumerate(_CHIP_FLIPS):
                px, py = _flipped((x, y), flip)
                s = 3 * i + k
                out.append(pltpu.make_async_remote_copy(
                    it.window(src[i], half=c), it.window(dst[i], chip=chip, half=c), send_a.at[s], recv_a.at[s],
                    device_id=(px, py, c), device_id_type=MESH))
                if onward:
                    got = it.window(dst[i], chip=2 * px + py, half=c)
                    fwd.append(pltpu.make_async_remote_copy(got, got, send_b.at[s], recv_b.at[s],
                                                            device_id=sibling, device_id_type=MESH))
                    other = it.window(dst[i], chip=2 * px + py, half=1 - c)
                    fwd_in.append(pltpu.make_async_remote_copy(other, other, send_b.at[s], recv_b.at[s],
                                                               device_id=sibling, device_id_type=MESH))
        return own, out, fwd, fwd_in

    def start(src, dst, sems):
        own, out, _, _ = copies(src, dst, sems, False)
        for cp in own + out:
            cp.start()

    def finish(src, dst, sems):
        own, out, fwd, fwd_in = copies(src, dst, sems, True)
        for arrived, onward in zip(out, fwd):
            arrived.wait_recv()
            onward.start()
        for cp in fwd_in:
            cp.wait_recv()
        for cp in out + fwd:
            cp.wait_send()
        for cp in own:
            cp.wait()

    sems = [pltpu.SemaphoreType.DMA((3 * n,)) for _ in range(4)] + [pltpu.SemaphoreType.DMA((n,)) for _ in range(2)]
    return _Comm(shards, [_sds(it.shape, BF16) for it in items], sems, start, finish)


def _pair_swap(arrays, windows, out_shapes, name):
    n = len(arrays)

    def body(*refs):
        src, got = refs[:n], refs[n:2 * n]
        send_sems, recv_sems = refs[2 * n:]
        x, y, c = _position()
        copies = []
        for i in range(n):
            cp = pltpu.make_async_remote_copy(windows[i](src[i], 1 - c), got[i], send_sems.at[i], recv_sems.at[i],
                                              device_id=(x, y, 1 - c), device_id_type=MESH)
            cp.start()
            copies.append(cp)
        for cp in copies:
            cp.wait()

    return pl.pallas_call(
        body, in_specs=[HBM_SPEC] * n, out_specs=[HBM_SPEC] * n, out_shape=list(out_shapes),
        scratch_shapes=[pltpu.SemaphoreType.DMA((n,)), pltpu.SemaphoreType.DMA((n,))], name=name,
    )(*arrays)


def _chip_exchange(partials, items, name):
    comm = _chip_exchange_comm(partials, items)

    def body(*refs):
        n = len(items)
        comm.start(refs[:n], refs[n:2 * n], refs[2 * n:])
        comm.finish(refs[:n], refs[n:2 * n], refs[2 * n:])

    return pl.pallas_call(
        body, in_specs=[HBM_SPEC] * len(items), out_specs=[HBM_SPEC] * len(items), out_shape=comm.outs,
        scratch_shapes=comm.sems, name=name,
    )(*partials)


def _chip_exchange_comm(partials, items):
    n = len(items)

    def copies(src, dst, sems):
        send_sems, recv_sems = sems
        x, y, c = _position()
        out = []
        for i, it in enumerate(items):
            for k, flip in enumerate(_CHIP_FLIPS):
                px, py = _flipped((x, y), flip)
                out.append(pltpu.make_async_remote_copy(
                    it.window(src[i], chip=2 * px + py), dst[i].at[k], send_sems.at[3 * i + k],
                    recv_sems.at[3 * i + k], device_id=(px, py, c), device_id_type=MESH))
        return out

    def start(src, dst, sems):
        for cp in copies(src, dst, sems):
            cp.start()

    def finish(src, dst, sems):
        for cp in copies(src, dst, sems):
            cp.wait()

    return _Comm(partials, [_sds((3,) + it.sized(shard=True, half=True), BF16) for it in items],
                 [pltpu.SemaphoreType.DMA((3 * n,)), pltpu.SemaphoreType.DMA((3 * n,))], start, finish)


def _pair_sum(g, got, it, pos, name):
    rows_split = it.half_axis == 0
    g2 = g.reshape(-1, g.shape[-1])
    got2 = got.reshape(-1, got.shape[-1])
    rows, cols = got2.shape
    tr = min(rows, 256)
    nb = rows // tr

    def body(pos_ref, g_ref, got_ref, o_ref):
        o_ref[...] = (g_ref[...].astype(F32) + got_ref[...].astype(F32)).astype(BF16)

    g_map = (lambda i, pos: (pos[1] * nb + i, 0)) if rows_split else (lambda i, pos: (i, pos[1]))
    blk = pl.BlockSpec((tr, cols), lambda i, pos: (i, 0))
    return pl.pallas_call(
        body, grid_spec=pltpu.PrefetchScalarGridSpec(
            num_scalar_prefetch=1, grid=(nb,), in_specs=[pl.BlockSpec((tr, cols), g_map), blk], out_specs=blk),
        out_shape=_sds((rows, cols), BF16), name=name, compiler_params=_cparams("parallel"),
    )(pos, g2, got2).reshape(got.shape)


_FLIP_SLOT = {2: 0, 1: 1, 3: 2}


def _chip_sum(pair, slots, it, pos, name):
    shape = it.sized(shard=True, half=True)
    nd = len(shape)

    def body(pos_ref, p_ref, s_ref, o_ref):
        chip = pos_ref[0]
        for own in range(4):
            @pl.when(chip == own)
            def _():
                acc = None
                for k in range(4):
                    v = (p_ref[...] if k == own else s_ref[_FLIP_SLOT[own ^ k]]).astype(F32)
                    acc = v if acc is None else acc + v
                o_ref[...] = acc

    p_map = lambda i, pos: tuple(pos[0] if ax == it.shard_axis else 0 for ax in range(nd))
    return pl.pallas_call(
        body, grid_spec=pltpu.PrefetchScalarGridSpec(
            num_scalar_prefetch=1, grid=(1,),
            in_specs=[pl.BlockSpec(shape, p_map), pl.BlockSpec((3,) + shape, lambda i, pos: (0,) * (nd + 1))],
            out_specs=pl.BlockSpec(shape, lambda i, pos: (0,) * nd)),
        out_shape=_sds(shape, F32), name=name, compiler_params=_cparams("arbitrary"),
    )(pos, pair, slots)


_GRAD_KEYS = ("pool_w_in", "pool_w_grp", "pool_w_out", "na_w_in", "na_w_out", "conv_w_in", "conv_w_out")


def _adamw_matrix(w, m, v, owns, others, it, pos, name):
    nl = w.shape[0]
    rows_split = it.half_axis == 0
    r, cdim = int(np.prod(w.shape[1:-1])), w.shape[-1]
    hr, hc = (r // 2, cdim) if rows_split else (r, cdim // 2)
    br = min(hr, 256)
    nb = hr // br
    c1 = 1.0 - ADAM_B1 ** ADAM_STEP
    c2 = 1.0 - ADAM_B2 ** ADAM_STEP

    def body(pos_ref, w_ref, m_ref, v_ref, *rest):
        own_refs, other_refs = rest[:nl], rest[nl:2 * nl]
        g_ref, d_ref, nm_ref, nv_ref = rest[2 * nl:]
        j, h = pl.program_id(0), pl.program_id(1)
        own, other = own_refs[0][...], other_refs[0][...]
        for q in range(1, nl):
            own = jnp.where(j == q, own_refs[q][...], own)
            other = jnp.where(j == q, other_refs[q][...], other)
        gv = jnp.where(h == pos_ref[1], own, other)
        nm = ADAM_B1 * m_ref[...] + (1.0 - ADAM_B1) * gv
        nv = ADAM_B2 * v_ref[...] + (1.0 - ADAM_B2) * (gv * gv)
        g_ref[...] = gv
        nm_ref[...] = nm
        nv_ref[...] = nv
        d_ref[...] = -ADAM_LR * ((nm / c1) / (jnp.sqrt(nv / c2) + ADAM_EPS) + ADAM_WD * w_ref[...])

    if rows_split:
        full = pl.BlockSpec((None, br, hc), lambda j, h, i, pos: (j, h * nb + i, 0))
    else:
        full = pl.BlockSpec((None, br, hc), lambda j, h, i, pos: (j, i, h))
    half = pl.BlockSpec((br, hc), lambda j, h, i, pos: (i, 0))
    flat = lambda t: t.reshape(nl, r, cdim)
    outs = pl.pallas_call(
        body, grid_spec=pltpu.PrefetchScalarGridSpec(
            num_scalar_prefetch=1, grid=(nl, 2, nb), in_specs=[full] * 3 + [half] * (2 * nl), out_specs=[full] * 4),
        out_shape=[_sds((nl, r, cdim), F32)] * 4, name=name,
        compiler_params=_cparams("parallel", "parallel", "parallel"),
    )(pos, flat(w), flat(m), flat(v), *[t.reshape(hr, hc) for t in list(owns) + list(others)])
    return tuple(t.reshape(w.shape) for t in outs)


_WEIGHTS = ("c_ctx", "norm_g", "ada_w", "ada_b", "pool_w_in", "pool_w_grp", "pool_scale", "pool_w_out", "na_w_in",
            "na_rpb", "na_w_out", "conv_w_in", "conv_dw", "conv_db", "conv_w_out", "final_g")
_COND_ROWS = 16


def _modulations(cond, ada_w, ada_b_cols):
    nl, d, n = ada_w.shape
    return _matmul(
        cond, ada_w, mode="nn", grid=(nl, 1), a_silu=True, epilogue="bias",
        a_spec=pl.BlockSpec((_COND_ROWS, d), lambda i, j: (0, 0)), b_spec=pl.BlockSpec((None, d, n), lambda i, j: (i, 0, 0)),
        extra=(ada_b_cols,), extra_specs=(pl.BlockSpec((None, 1, n), lambda i, j: (i, 0, 0)),),
        out_shapes=[_sds((nl, _COND_ROWS, n), F32)], out_specs=[pl.BlockSpec((None, _COND_ROWS, n), lambda i, j: (i, 0, 0))],
        name="modulations")[0]


def _ada_w_grad(cond, dm_cols):
    d = cond.shape[1]
    nl, _, n = dm_cols.shape
    return _matmul(
        cond, dm_cols, mode="tn", grid=(nl, 1), a_silu=True,
        a_spec=pl.BlockSpec((_COND_ROWS, d), lambda i, j: (0, 0)), b_spec=pl.BlockSpec((None, _COND_ROWS, n), lambda i, j: (i, 0, 0)),
        out_shapes=[_sds((nl, d, n), F32)], out_specs=[pl.BlockSpec((None, d, n), lambda i, j: (i, 0, 0))],
        name="ada_w_grad")[0]


def _cond_grad(dm_cols, ada_w):
    nl, d, n = ada_w.shape
    return _matmul(
        dm_cols, ada_w, mode="nt", grid=(1, nl), nk=nl, acc_shape=(_COND_ROWS, d),
        a_spec=pl.BlockSpec((None, _COND_ROWS, n), lambda i, q: (q, 0, 0)), b_spec=pl.BlockSpec((None, d, n), lambda i, q: (q, 0, 0)),
        out_shapes=[_sds((_COND_ROWS, d), F32)], out_specs=[pl.BlockSpec((_COND_ROWS, d), lambda i, q: (0, 0))],
        name="cond_grad")[0]


def _pack(parts):
    flat = [p.reshape(-1) for p in parts]
    sizes = [f.shape[0] for f in flat]
    total = sum(sizes)
    rows = -(-total // 1024) * 8
    packed = jnp.concatenate(flat + [jnp.zeros((rows * 128 - total,), F32)]).reshape(rows, 128)
    offs = np.concatenate([[0], np.cumsum(sizes)])[:-1]
    return packed, [(int(o), p.shape) for o, p in zip(offs, parts)]


def _unpack(flat, layout, k):
    off, shape = layout[k]
    return flat[..., off:off + int(np.prod(shape))].reshape(flat.shape[:-1] + tuple(shape))


def kernel(x, c, ctx, c_ctx, norm_g, ada_w, ada_b, pool_w_in, pool_w_grp, pool_scale, pool_w_out, na_w_in, na_rpb, na_w_out, conv_w_in, conv_dw, conv_db, conv_w_out, final_g, loss_target, m_c_ctx, m_norm_g, m_ada_w, m_ada_b, m_pool_w_in, m_pool_w_grp, m_pool_scale, m_pool_w_out, m_na_w_in, m_na_rpb, m_na_w_out, m_conv_w_in, m_conv_dw, m_conv_db, m_conv_w_out, m_final_g, v_c_ctx, v_norm_g, v_ada_w, v_ada_b, v_pool_w_in, v_pool_w_grp, v_pool_scale, v_pool_w_out, v_na_w_in, v_na_rpb, v_na_w_out, v_conv_w_in, v_conv_dw, v_conv_db, v_conv_w_out, v_final_g):
    params = dict(c_ctx=c_ctx, norm_g=norm_g, ada_w=ada_w, ada_b=ada_b, pool_w_in=pool_w_in, pool_w_grp=pool_w_grp,
                  pool_scale=pool_scale, pool_w_out=pool_w_out, na_w_in=na_w_in, na_rpb=na_rpb, na_w_out=na_w_out,
                  conv_w_in=conv_w_in, conv_dw=conv_dw, conv_db=conv_db, conv_w_out=conv_w_out, final_g=final_g)
    mom1 = dict(c_ctx=m_c_ctx, norm_g=m_norm_g, ada_w=m_ada_w, ada_b=m_ada_b, pool_w_in=m_pool_w_in,
                pool_w_grp=m_pool_w_grp, pool_scale=m_pool_scale, pool_w_out=m_pool_w_out, na_w_in=m_na_w_in,
                na_rpb=m_na_rpb, na_w_out=m_na_w_out, conv_w_in=m_conv_w_in, conv_dw=m_conv_dw, conv_db=m_conv_db,
                conv_w_out=m_conv_w_out, final_g=m_final_g)
    mom2 = dict(c_ctx=v_c_ctx, norm_g=v_norm_g, ada_w=v_ada_w, ada_b=v_ada_b, pool_w_in=v_pool_w_in,
                pool_w_grp=v_pool_w_grp, pool_scale=v_pool_scale, pool_w_out=v_pool_w_out, na_w_in=v_na_w_in,
                na_rpb=v_na_rpb, na_w_out=v_na_w_out, conv_w_in=v_conv_w_in, conv_dw=v_conv_dw, conv_db=v_conv_db,
                conv_w_out=v_conv_w_out, final_g=v_final_g)
    d = x.shape[-1]
    w = na_w_out.shape[1] * 4
    xi, yi, ci = _position()
    chip = 2 * xi + yi
    dev = 2 * chip + ci
    n_ada = ada_w.shape[-1]

    def chip_cols(a, size):
        return lax.dynamic_slice_in_dim(a, chip * size, size, axis=a.ndim - 1)

    conds = _all_gather(c.reshape(8, d // 8), _AXES, "gather_cond").reshape(8, d)
    cond = jnp.concatenate([conds, c_ctx[None], jnp.zeros((_COND_ROWS - 9, d), F32)], axis=0)
    mod_cols = _modulations(cond, ada_w, chip_cols(ada_b, n_ada)[:, None, :])
    mod_all = _all_gather(mod_cols, ("x", "y"), "gather_mod")
    mod_all = mod_all.transpose(1, 2, 0, 3).reshape(4, _COND_ROWS, 3, d)
    mod = jnp.stack([lax.dynamic_index_in_dim(mod_all, dev, axis=1, keepdims=False), mod_all[:, 8]], axis=1)

    items = _items(d, w)
    early = [it for it in items if (it.key.startswith("pool") and it.layer == 0) or it.key.startswith("na")]
    late = [it for it in items if it not in early]
    shard_of = lambda it: params[it.key][it.layer].astype(BF16)
    full = {(it.key, it.layer): mat
            for it, mat in zip(early, _gather_weights([shard_of(it) for it in early], early, "gather_weights"))}
    late_comm = _gather_comm([shard_of(it) for it in late], late)

    def late_weights(mats):
        full.update({(it.key, it.layer): mat for it, mat in zip(late, mats)})
        return dict(pool_w_in=[full[("pool_w_in", j)] for j in range(2)],
                    pool_w_grp=[full[("pool_w_grp", j)] for j in range(2)],
                    pool_w_out=[full[("pool_w_out", j)] for j in range(2)],
                    conv_w_in=full[("conv_w_in", 0)], conv_w_out=full[("conv_w_out", 0)])

    small = _all_gather(_pack([pool_scale, conv_dw, conv_db])[0], ("x", "y"), "gather_small")
    small_layout = _pack([pool_scale, conv_dw, conv_db])[1]
    small = small.reshape(4, -1)

    def whole(k):
        parts = _unpack(small, small_layout, k)
        return jnp.moveaxis(parts, 0, -2).reshape(parts.shape[1:-1] + (-1,))

    wts = dict(pool_w_in=[full[("pool_w_in", 0)]], pool_w_grp=[full[("pool_w_grp", 0)]],
               pool_w_out=[full[("pool_w_out", 0)]], na_w_in=full[("na_w_in", 0)], na_w_out=full[("na_w_out", 0)],
               pool_scale=whole(0), na_rpb=na_rpb[0], conv_dw=whole(1)[0], conv_db=whole(2))
    pos = jnp.stack([chip, ci]).astype(jnp.int32)

    def layer_grads(its, by_layer):
        pick = {"pool_w_in": "w_in", "pool_w_grp": "w_grp", "pool_w_out": "w_out", "na_w_in": "w_in",
                "na_w_out": "w_out", "conv_w_in": "w_in", "conv_w_out": "w_out"}
        return [by_layer[(it.key.split("_")[0], it.layer)][pick[it.key]] for it in its]

    def pair_sums(its, mats, tag):
        got = _pair_swap(mats, [(lambda ref, half, it=it: it.window(ref, half=half)) for it in its],
                         [_sds(it.sized(half=True), BF16) for it in its], f"pair_exchange_{tag}")
        return [_pair_sum(g, s, it, pos, f"pair_sum_{tag}{i}") for i, (g, s, it) in enumerate(zip(mats, got, its))]

    pairs = dict()

    def grad_comm(gr3, gr2):
        pairs["late"] = pair_sums(late, layer_grads(late, {("pool", 1): gr3, ("conv", 0): gr2}), "late")
        return _chip_exchange_comm(pairs["late"], late)

    res = _example_step(x[0], ctx[0], loss_target[0], mod, norm_g, final_g[None], wts, late_comm, late_weights,
                        grad_comm)
    g0, g1, g2, g3 = res["layers"]
    pairs["early"] = pair_sums(early, layer_grads(early, {("pool", 0): g0, ("na", 0): g1}), "early")
    slots = dict(zip(late, res["carried"]))
    slots.update(zip(early, _chip_exchange(pairs["early"], early, "chip_exchange")))
    pair_of = dict(zip(late, pairs["late"]))
    pair_of.update(zip(early, pairs["early"]))
    reduced = [_chip_sum(pair_of[it], slots[it], it, pos, f"chip_sum_{i}") for i, it in enumerate(items)]
    theirs = _pair_swap(reduced, [lambda ref, half: ref] * len(items),
                        [_sds(t.shape, F32) for t in reduced], "pair_return")
    grads, matrix_out = dict(), dict()
    for k in _GRAD_KEYS:
        idx = [i for i, it in enumerate(items) if it.key == k]
        res_k = _adamw_matrix(params[k], mom1[k], mom2[k], [reduced[i] for i in idx], [theirs[i] for i in idx],
                              items[idx[0]], pos, f"adamw_{k}")
        grads[k], matrix_out[k] = res_k[0], res_k[1:]

    packed, layout = _pack([res["dfinal_g"], res["dnorm_g"], res["dmod"], g1["rpb"],
                            jnp.concatenate([g0["scale"], g3["scale"]], axis=0), g2["dw"], g2["db"]])
    every = _all_gather(packed, _AXES, "gather_vec_grads")
    total = _sum_lead(every, "sum_vec_grads").reshape(-1)
    every = every.reshape(8, -1)
    grads["final_g"] = _unpack(total, layout, 0).reshape(final_g.shape)
    grads["norm_g"] = _unpack(total, layout, 1)
    grads["na_rpb"] = _unpack(total, layout, 3)[None]
    grads["pool_scale"] = chip_cols(_unpack(total, layout, 4), pool_scale.shape[-1])
    grads["conv_dw"] = chip_cols(_unpack(total, layout, 5), conv_dw.shape[-1])[None]
    grads["conv_db"] = chip_cols(_unpack(total, layout, 6), conv_db.shape[-1])
    dmod_sum = _unpack(total, layout, 2).reshape(4, 2, 3 * d)
    dmod_each = _unpack(every, layout, 2).reshape(8, 4, 2, 3 * d)
    grads["ada_b"] = dmod_sum[:, 0] + dmod_sum[:, 1]
    dm = jnp.concatenate([dmod_each[:, :, 0].transpose(1, 0, 2), dmod_sum[:, 1][:, None],
                          jnp.zeros((4, _COND_ROWS - 9, 3 * d), F32)], axis=1)
    dm_cols = chip_cols(dm, n_ada)
    grads["ada_w"] = _ada_w_grad(cond, dm_cols)
    dcond = _cond_grad(dm_cols, ada_w)[8].reshape(8, d // 8)
    dcond = _sum_lead(_all_gather(dcond, ("x", "y"), "gather_cond_grad"), "sum_cond_grad").reshape(d)
    grads["c_ctx"] = dcond * _dsilu(c_ctx)

    outs = [[], [], []]
    for k in _WEIGHTS:
        step = matrix_out[k] if k in matrix_out else _adamw(params[k], grads[k], mom1[k], mom2[k], f"adamw_{k}")
        for lst, val in zip(outs, step):
            lst.append(val)
    loss = lax.psum(res["loss"][0, 0], _AXES)
    return (loss, res["grad_x"][None], *[grads[k].reshape(params[k].shape) for k in _WEIGHTS],
            *outs[0], *outs[1], *outs[2])
```

```python
import functools

import numpy as np
import jax
import jax.numpy as jnp
from jax import lax
from jax.experimental import pallas as pl
from jax.experimental.pallas import tpu as pltpu

F32 = jnp.float32
BF16 = jnp.bfloat16

EPS = 1e-6
GRID_W = 64
HEAD_DIM = 64
WIN_ROWS = 8
WIN_COLS = 16
POOL_WINDOWS = (2, 4, 8, 16)
Q_ROWS = 4
K_ROWS = 12
PAD_ROWS = 4
NEG = -1e30

ADAM_LR = 0.001
ADAM_B1 = 0.9
ADAM_B2 = 0.999
ADAM_EPS = 1e-08
ADAM_WD = 0.01
ADAM_STEP = 10

ROW_BLOCK = 256
VMEM_LIMIT = 56 * 1024 * 1024
ACT = BF16

MESH = pl.DeviceIdType.MESH
HBM_SPEC = pl.BlockSpec(memory_space=pltpu.HBM)


def _cparams(*sem):
    return pltpu.CompilerParams(dimension_semantics=sem or None, vmem_limit_bytes=VMEM_LIMIT)


def _sds(shape, dtype):
    return jax.ShapeDtypeStruct(tuple(shape), dtype)


def _sigmoid(x):
    return 1.0 / (1.0 + jnp.exp(-x))


def _silu(x):
    return x * _sigmoid(x)


def _dsilu(x):
    s = _sigmoid(x)
    return s * (1.0 + x * (1.0 - s))


_DIMS = {
    "nn": (((1,), (0,)), ((), ())),
    "nt": (((1,), (1,)), ((), ())),
    "tn": (((0,), (0,)), ((), ())),
}


def _matmul(a, b, *, mode, grid, a_spec, b_spec, out_shapes, out_specs, name, nk=1,
            a_silu=False, exact=False, epilogue=None, extra=(), extra_specs=(), acc_shape=None):
    n_extra = len(extra)
    n_out = len(out_shapes)

    def body(*refs):
        a_ref, b_ref = refs[:2]
        ex = refs[2:2 + n_extra]
        outs = refs[2 + n_extra:2 + n_extra + n_out]
        av = a_ref[...]
        bv = b_ref[...]
        if a_silu:
            av = _silu(av.astype(F32))
        if exact:
            prod = lax.dot_general(av.astype(F32), bv.astype(F32), _DIMS[mode],
                                   precision=lax.Precision.HIGHEST, preferred_element_type=F32)
        else:
            prod = lax.dot_general(av.astype(BF16), bv.astype(BF16), _DIMS[mode], preferred_element_type=F32)

        def finish(res):
            if epilogue is None:
                outs[0][...] = res.astype(outs[0].dtype)
            elif epilogue == "bias":
                outs[0][...] = (res + ex[0][...]).astype(outs[0].dtype)
            else:
                outs[0][...] = res.astype(outs[0].dtype)
                outs[1][...] = ex[0][...] + ex[1][...] * res

        if nk == 1:
            finish(prod)
        else:
            acc = refs[-1]
            k = pl.program_id(len(grid) - 1)

            @pl.when(k == 0)
            def _():
                acc[...] = prod

            @pl.when(k > 0)
            def _():
                acc[...] += prod

            @pl.when(k == nk - 1)
            def _():
                finish(acc[...])

    scratch = [pltpu.VMEM(acc_shape, F32)] if nk > 1 else []
    sem = ("parallel",) * (len(grid) - 1) + ("arbitrary",)
    return pl.pallas_call(
        body, grid=grid, in_specs=[a_spec, b_spec, *extra_specs], out_specs=list(out_specs),
        out_shape=list(out_shapes), scratch_shapes=scratch, name=name, compiler_params=_cparams(*sem),
    )(a, b, *extra)


def _row_tile(rows):
    for t in (768, 512, 256):
        if rows % t == 0:
            return t
    return rows


def _mm_nn(a, b, name, out_dtype=F32, tn=1024):
    m, k = a.shape
    n = b.shape[1]
    tm = _row_tile(m)
    tn = min(tn, n)
    return _matmul(
        a, b, mode="nn", grid=(m // tm, n // tn),
        a_spec=pl.BlockSpec((tm, k), lambda i, j: (i, 0)), b_spec=pl.BlockSpec((k, tn), lambda i, j: (0, j)),
        out_shapes=[_sds((m, n), out_dtype)], out_specs=[pl.BlockSpec((tm, tn), lambda i, j: (i, j))], name=name)[0]


def _mm_out_resid(a, w_out, xres, gate, nxb, name):
    m, k = a.shape
    n = w_out.shape[1]
    tm = ROW_BLOCK
    seg = lambda i, j: (jnp.where(i >= nxb, 1, 0), 0, 0)
    return _matmul(
        a, w_out, mode="nn", grid=(m // tm, 1),
        a_spec=pl.BlockSpec((tm, k), lambda i, j: (i, 0)), b_spec=pl.BlockSpec((k, n), lambda i, j: (0, 0)),
        extra=(xres, gate), extra_specs=(pl.BlockSpec((tm, n), lambda i, j: (i, 0)), pl.BlockSpec((None, 1, n), seg)),
        out_shapes=[_sds((m, n), ACT), _sds((m, n), F32)],
        out_specs=[pl.BlockSpec((tm, n), lambda i, j: (i, 0))] * 2, epilogue="resid", name=name)


def _mm_nt(a, b, name, out_dtype=F32):
    m, n = a.shape
    k = b.shape[0]
    tm = _row_tile(m)
    return _matmul(
        a, b, mode="nt", grid=(m // tm, 1),
        a_spec=pl.BlockSpec((tm, n), lambda i, j: (i, 0)), b_spec=pl.BlockSpec((k, n), lambda i, j: (0, 0)),
        out_shapes=[_sds((m, k), out_dtype)], out_specs=[pl.BlockSpec((tm, k), lambda i, j: (i, 0))], name=name)[0]


def _mm_nt_parts(a, b, name):
    p, m, kp = a.shape
    d = b.shape[0]
    tm = _row_tile(m)
    return _matmul(
        a, b, mode="nt", grid=(m // tm, p), nk=p, acc_shape=(tm, d),
        a_spec=pl.BlockSpec((None, tm, kp), lambda i, q: (q, i, 0)), b_spec=pl.BlockSpec((d, kp), lambda i, q: (0, q)),
        out_shapes=[_sds((m, d), F32)], out_specs=[pl.BlockSpec((tm, d), lambda i, q: (i, 0))], name=name)[0]


def _mm_tn(a, b, name, out_dtype, tm=512):
    r, m = a.shape
    n = b.shape[1]
    tm = min(tm, m)
    tn = min(1024, n)
    return _matmul(
        a, b, mode="tn", grid=(m // tm, n // tn),
        a_spec=pl.BlockSpec((r, tm), lambda i, j: (0, i)), b_spec=pl.BlockSpec((r, tn), lambda i, j: (0, j)),
        out_shapes=[_sds((m, n), out_dtype)], out_specs=[pl.BlockSpec((tm, tn), lambda i, j: (i, j))], name=name)[0]


def _mm_tn_parts(a, b, name, out_dtype, tm=512):
    r, m = a.shape
    p, _, np_ = b.shape
    tm = min(tm, m)
    return _matmul(
        a, b, mode="tn", grid=(m // tm, p),
        a_spec=pl.BlockSpec((r, tm), lambda i, q: (0, i)), b_spec=pl.BlockSpec((None, r, np_), lambda i, q: (q, 0, 0)),
        out_shapes=[_sds((m, p * np_), out_dtype)], out_specs=[pl.BlockSpec((tm, np_), lambda i, q: (i, q))],
        name=name)[0]


def _seg_map(nxb):
    return lambda i: (jnp.where(i >= nxb, 1, 0), 0, 0)


def _normmod_fwd(x, g, scale, shift, nxb, name):
    rows, d = x.shape
    tr = ROW_BLOCK

    def body(x_ref, g_ref, sc_ref, sh_ref, h_ref, r_ref):
        xv = x_ref[...]
        r = lax.rsqrt(jnp.mean(xv * xv, axis=-1, keepdims=True) + EPS)
        h = (xv * r) * g_ref[...] * (1.0 + sc_ref[...]) + sh_ref[...]
        h_ref[...] = h.astype(BF16)
        r_ref[...] = r

    row = pl.BlockSpec((tr, d), lambda i: (i, 0))
    vec = pl.BlockSpec((None, 1, d), _seg_map(nxb))
    return pl.pallas_call(
        body, grid=(rows // tr,), in_specs=[row, pl.BlockSpec((1, d), lambda i: (0, 0)), vec, vec],
        out_specs=[row, pl.BlockSpec((tr, 1), lambda i: (i, 0))],
        out_shape=[_sds((rows, d), BF16), _sds((rows, 1), F32)], name=name, compiler_params=_cparams("parallel"),
    )(x, g, scale, shift)


def _normmod_bwd(dh, x, r, g, scale, dres, nxb, name):
    rows, d = x.shape
    tr = ROW_BLOCK
    nres = dres.shape[0] // tr
    nseg = scale.shape[0]

    def body(dh_ref, x_ref, r_ref, g_ref, sc_ref, dres_ref, dx_ref, dsh_ref, dge_ref):
        i = pl.program_id(0)
        dhv = dh_ref[...]
        rv = r_ref[...]
        xn = x_ref[...] * rv
        dxn = dhv * (g_ref[...] * (1.0 + sc_ref[...]))
        dx = rv * (dxn - xn * jnp.mean(dxn * xn, axis=-1, keepdims=True))

        @pl.when(i < nres)
        def _():
            dx_ref[...] = dx + dres_ref[...]

        @pl.when(i >= nres)
        def _():
            dx_ref[...] = dx

        first = jnp.logical_or(i == 0, i == nxb)
        s_dh = jnp.sum(dhv, axis=0, keepdims=True)
        s_ge = jnp.sum(dhv * xn, axis=0, keepdims=True)

        @pl.when(first)
        def _():
            dsh_ref[...] = s_dh
            dge_ref[...] = s_ge

        @pl.when(jnp.logical_not(first))
        def _():
            dsh_ref[...] += s_dh
            dge_ref[...] += s_ge

    row = pl.BlockSpec((tr, d), lambda i: (i, 0))
    vec = pl.BlockSpec((None, 1, d), _seg_map(nxb))
    return pl.pallas_call(
        body, grid=(rows // tr,),
        in_specs=[row, row, pl.BlockSpec((tr, 1), lambda i: (i, 0)), pl.BlockSpec((1, d), lambda i: (0, 0)), vec,
                  pl.BlockSpec((tr, d), lambda i: (jnp.minimum(i, nres - 1), 0))],
        out_specs=[row, vec, vec],
        out_shape=[_sds((rows, d), F32), _sds((nseg, 1, d), F32), _sds((nseg, 1, d), F32)],
        name=name, compiler_params=_cparams("arbitrary"),
    )(dh, x, r, g, scale, dres)


def _gate_bwd(dxo, yx, gate, nxb, name):
    rows, d = yx.shape
    tr = ROW_BLOCK
    nseg = gate.shape[0]

    def body(dx_ref, yx_ref, gt_ref, dyx_ref, dg_ref):
        i = pl.program_id(0)
        dxv = dx_ref[...]
        dyx_ref[...] = (dxv * gt_ref[...]).astype(BF16)
        s = jnp.sum(dxv * yx_ref[...].astype(F32), axis=0, keepdims=True)
        first = jnp.logical_or(i == 0, i == nxb)

        @pl.when(first)
        def _():
            dg_ref[...] = s

        @pl.when(jnp.logical_not(first))
        def _():
            dg_ref[...] += s

    row = pl.BlockSpec((tr, d), lambda i: (i, 0))
    vec = pl.BlockSpec((None, 1, d), _seg_map(nxb))
    return pl.pallas_call(
        body, grid=(rows // tr,), in_specs=[row, row, vec], out_specs=[row, vec],
        out_shape=[_sds((rows, d), BF16), _sds((nseg, 1, d), F32)], name=name, compiler_params=_cparams("arbitrary"),
    )(dxo, yx, gate)


_PAD_TOP = 16
_PAD_BOT = 32


def _window_sum(buf, xv, lo, n):
    t = xv.shape[0]
    c = xv.shape[1]
    tp = t + _PAD_TOP + _PAD_BOT
    buf[pl.ds(0, _PAD_TOP), :] = jnp.zeros((_PAD_TOP, c), F32)
    buf[pl.ds(_PAD_TOP, t), :] = xv
    buf[pl.ds(_PAD_TOP + t, _PAD_BOT), :] = jnp.zeros((_PAD_BOT, c), F32)
    p = buf[...]
    k = 1
    while k < n:
        p = p + pltpu.roll(p, tp - k, 0)
        k *= 2
    if lo:
        p = pltpu.roll(p, -lo, 0)
    buf[...] = p
    return buf[pl.ds(_PAD_TOP, t), :]


def _window_count(t, half):
    pos = lax.broadcasted_iota(jnp.int32, (t, 1), 0)
    return (jnp.minimum(pos + half, t) - jnp.maximum(pos - half, 0)).astype(F32)


def _segments(rows, nx):
    return [(0, nx)] + ([(nx, rows - nx)] if rows > nx else [])


def _pool_fwd(uv, nx, name):
    rows = uv.shape[0]
    w = uv.shape[1] // 2
    cb = 128
    per_group = w // len(POOL_WINDOWS) // cb
    segs = _segments(rows, nx)

    def body(u_ref, z_ref, *bufs):
        j = pl.program_id(0)
        for gi, win in enumerate(POOL_WINDOWS):
            half = win // 2

            @pl.when(jnp.logical_and(j >= gi * per_group, j < (gi + 1) * per_group))
            def _():
                for (start, length), buf in zip(segs, bufs):
                    uvv = u_ref[pl.ds(start, length), :].astype(F32)
                    s = _window_sum(buf, uvv, -half, win)
                    z_ref[pl.ds(start, length), :] = (s / _window_count(length, half) - uvv).astype(BF16)

    scratch = [pltpu.VMEM((length + _PAD_TOP + _PAD_BOT, cb), F32) for _, length in segs]
    return pl.pallas_call(
        body, grid=(w // cb,), in_specs=[pl.BlockSpec((rows, cb), lambda j: (0, j))],
        out_specs=pl.BlockSpec((rows, cb), lambda j: (0, j)), out_shape=_sds((rows, w), BF16),
        scratch_shapes=scratch, name=name, compiler_params=_cparams("parallel"),
    )(uv)


def _pool_bwd(dz, dgt, nx, name):
    rows, w = dz.shape
    cb = 128
    per_group = w // len(POOL_WINDOWS) // cb
    segs = _segments(rows, nx)

    def body(dz_ref, dgt_ref, o_ref, *bufs):
        j = pl.program_id(0)
        o_ref[1] = dgt_ref[...]
        for gi, win in enumerate(POOL_WINDOWS):
            half = win // 2

            @pl.when(jnp.logical_and(j >= gi * per_group, j < (gi + 1) * per_group))
            def _():
                for (start, length), buf in zip(segs, bufs):
                    dzv = dz_ref[pl.ds(start, length), :].astype(F32)
                    s = _window_sum(buf, dzv / _window_count(length, half), 1 - half, win)
                    o_ref[0, pl.ds(start, length), :] = (s - dzv).astype(BF16)

    scratch = [pltpu.VMEM((length + _PAD_TOP + _PAD_BOT, cb), F32) for _, length in segs]
    col = pl.BlockSpec((rows, cb), lambda j: (0, j))
    return pl.pallas_call(
        body, grid=(w // cb,), in_specs=[col, col], out_specs=pl.BlockSpec((2, rows, cb), lambda j: (0, 0, j)),
        out_shape=_sds((2, rows, w), BF16), scratch_shapes=scratch, name=name, compiler_params=_cparams("parallel"),
    )(dz, dgt)


def _grp_fwd(z, w_grp, uv, scale, name):
    rows, w = z.shape
    ng, gc, _ = w_grp.shape
    tm = _row_tile(rows)

    def body(z_ref, w_ref, gt_ref, sc_ref, mx_ref, a_ref):
        mixed = jnp.dot(z_ref[...], w_ref[...], preferred_element_type=F32)
        mx_ref[...] = mixed.astype(ACT)
        a_ref[...] = (mixed * sc_ref[...] * _silu(gt_ref[...].astype(F32))).astype(BF16)

    blk = pl.BlockSpec((tm, gc), lambda g, i: (i, g))
    return pl.pallas_call(
        body, grid=(ng, rows // tm),
        in_specs=[blk, pl.BlockSpec((None, gc, gc), lambda g, i: (g, 0, 0)),
                  pl.BlockSpec((tm, gc), lambda g, i: (i, ng + g)), pl.BlockSpec((1, gc), lambda g, i: (0, g))],
        out_specs=[blk, blk], out_shape=[_sds((rows, w), ACT), _sds((rows, w), BF16)],
        name=name, compiler_params=_cparams("parallel", "parallel"),
    )(z, w_grp, uv, scale)


def _grp_bwd(da, mixed, uv, scale, w_grp, name):
    rows, w = da.shape
    ng, gc, _ = w_grp.shape
    tm = _row_tile(rows)

    def body(da_ref, mx_ref, gt_ref, sc_ref, w_ref, dm_ref, dz_ref, dgt_ref, dsc_ref):
        i = pl.program_id(1)
        dav = da_ref[...].astype(F32)
        mixed = mx_ref[...].astype(F32)
        gt = gt_ref[...].astype(F32)
        sg = _silu(gt)
        sc = sc_ref[...]
        dm = (dav * sc * sg).astype(BF16)
        dm_ref[...] = dm
        dz_ref[...] = lax.dot_general(dm, w_ref[...], _DIMS["nt"], preferred_element_type=F32).astype(ACT)
        dgt_ref[...] = (dav * mixed * sc * _dsilu(gt)).astype(BF16)
        s = jnp.sum(dav * mixed * sg, axis=0, keepdims=True)

        @pl.when(i == 0)
        def _():
            dsc_ref[...] = s

        @pl.when(i > 0)
        def _():
            dsc_ref[...] += s

    blk = pl.BlockSpec((tm, gc), lambda g, i: (i, g))
    vec = pl.BlockSpec((1, gc), lambda g, i: (0, g))
    return pl.pallas_call(
        body, grid=(ng, rows // tm),
        in_specs=[blk, blk, pl.BlockSpec((tm, gc), lambda g, i: (i, ng + g)), vec,
                  pl.BlockSpec((None, gc, gc), lambda g, i: (g, 0, 0))],
        out_specs=[blk, blk, blk, vec],
        out_shape=[_sds((rows, w), BF16), _sds((rows, w), ACT), _sds((rows, w), BF16), _sds((1, w), F32)],
        name=name, compiler_params=_cparams("parallel", "arbitrary"),
    )(da, mixed, uv, scale, w_grp)


def _grp_wgrad(z, dm, ng, name, out_dtype):
    rows, w = z.shape
    gc = w // ng

    def body(z_ref, dm_ref, o_ref):
        o_ref[...] = lax.dot_general(z_ref[...], dm_ref[...], _DIMS["tn"],
                                     preferred_element_type=F32).astype(o_ref.dtype)

    blk = pl.BlockSpec((rows, gc), lambda g: (0, g))
    return pl.pallas_call(
        body, grid=(ng,), in_specs=[blk, blk], out_specs=pl.BlockSpec((None, gc, gc), lambda g: (g, 0, 0)),
        out_shape=_sds((ng, gc, gc), out_dtype), name=name, compiler_params=_cparams("parallel"),
    )(z, dm)


def _shift_rows(v, by):
    t = v.shape[0]
    pos = lax.broadcasted_iota(jnp.int32, v.shape, 0)
    rolled = pltpu.roll(v, by % t, 0)
    keep = pos >= by if by > 0 else pos < t + by
    return jnp.where(keep, rolled, 0.0)


def _conv_specs(t, w, cb):
    return [pl.BlockSpec((t, cb), (lambda j, q=q: (0, q * (w // cb) + j))) for q in range(4)]


def _conv_fwd(p4, dw, db, name):
    t = p4.shape[0]
    w = p4.shape[1] // 4
    cb = 128

    def body(bg_ref, cg_ref, v_ref, g_ref, dw_ref, db_ref, a_ref):
        tv = cg_ref[...].astype(F32) * v_ref[...].astype(F32)
        conv = (dw_ref[0:1, :] * _shift_rows(tv, 1) + dw_ref[1:2, :] * tv + dw_ref[2:3, :] * _shift_rows(tv, -1)
                + db_ref[...])
        a_ref[...] = (bg_ref[...].astype(F32) * conv * _silu(g_ref[...].astype(F32))).astype(BF16)

    return pl.pallas_call(
        body, grid=(w // cb,),
        in_specs=_conv_specs(t, w, cb) + [pl.BlockSpec((3, cb), lambda j: (0, j)), pl.BlockSpec((1, cb), lambda j: (0, j))],
        out_specs=pl.BlockSpec((t, cb), lambda j: (0, j)), out_shape=_sds((t, w), BF16),
        name=name, compiler_params=_cparams("parallel"),
    )(p4, p4, p4, p4, dw, db)


def _conv_bwd(da, p4, dw, db, name):
    t, w = da.shape
    cb = 128

    def body(da_ref, bg_ref, cg_ref, v_ref, g_ref, dw_ref, db_ref, d4_ref, ddw_ref, ddb_ref):
        cg = cg_ref[...].astype(F32)
        vv = v_ref[...].astype(F32)
        bg = bg_ref[...].astype(F32)
        gv = g_ref[...].astype(F32)
        tv = cg * vv
        tm1 = _shift_rows(tv, 1)
        tp1 = _shift_rows(tv, -1)
        w0, w1, w2 = dw_ref[0:1, :], dw_ref[1:2, :], dw_ref[2:3, :]
        conv = w0 * tm1 + w1 * tv + w2 * tp1 + db_ref[...]
        y = bg * conv
        dav = da_ref[...].astype(F32)
        dy = dav * _silu(gv)
        d4_ref[3] = (dav * y * _dsilu(gv)).astype(BF16)
        d4_ref[0] = (dy * conv).astype(BF16)
        dconv = dy * bg
        ddb_ref[...] = jnp.sum(dconv, axis=0, keepdims=True)
        ddw_ref[0:1, :] = jnp.sum(dconv * tm1, axis=0, keepdims=True)
        ddw_ref[1:2, :] = jnp.sum(dconv * tv, axis=0, keepdims=True)
        ddw_ref[2:3, :] = jnp.sum(dconv * tp1, axis=0, keepdims=True)
        dt = w0 * _shift_rows(dconv, -1) + w1 * dconv + w2 * _shift_rows(dconv, 1)
        d4_ref[1] = (dt * vv).astype(BF16)
        d4_ref[2] = (dt * cg).astype(BF16)

    col = pl.BlockSpec((t, cb), lambda j: (0, j))
    tap = pl.BlockSpec((3, cb), lambda j: (0, j))
    bias = pl.BlockSpec((1, cb), lambda j: (0, j))
    return pl.pallas_call(
        body, grid=(w // cb,), in_specs=[col] + _conv_specs(t, w, cb) + [tap, bias],
        out_specs=[pl.BlockSpec((4, t, cb), lambda j: (0, 0, j)), tap, bias],
        out_shape=[_sds((4, t, w), BF16), _sds((3, w), F32), _sds((1, w), F32)],
        name=name, compiler_params=_cparams("parallel"),
    )(da, p4, p4, p4, p4, dw, db)


def _attn_mask():
    qn, kn = Q_ROWS * GRID_W, K_ROWS * GRID_W
    qr, qc = np.divmod(np.arange(qn), GRID_W)
    kr, kc = np.divmod(np.arange(kn), GRID_W)
    col0 = np.clip(qc - WIN_COLS // 2, 0, GRID_W - WIN_COLS)
    col_ok = (kc[None, :] >= col0[:, None]) & (kc[None, :] < col0[:, None] + WIN_COLS)
    first = np.zeros(qn, np.int64)
    last = np.full(qn, K_ROWS - WIN_ROWS)
    out = []
    for row0 in (first, qr, last):
        row_ok = (kr[None, :] >= row0[:, None]) & (kr[None, :] < row0[:, None] + WIN_ROWS)
        out.append(np.where(row_ok & col_ok, 0.0, NEG))
    return jnp.asarray(np.stack(out), F32)


def _rpb_onehot():
    qc, kc = np.divmod(np.arange(GRID_W * GRID_W), GRID_W)
    e = (kc - qc + WIN_COLS - 1)[None, :] == np.arange(128)[:, None]
    return jnp.asarray(e, F32)


_KW = K_ROWS * GRID_W
_QB = Q_ROWS * GRID_W
_T3_FRONT = 4
_T3_LANES = 1536
_PAIR = 2 * HEAD_DIM
_BIAS_BASE = (WIN_ROWS - 1 + _T3_FRONT, WIN_ROWS // 2 - 1 + _T3_FRONT, _T3_FRONT - 1)


class _Comm:
    def __init__(self, ins, outs, sems, start, finish):
        self.ins, self.outs, self.sems, self.start, self.finish = list(ins), list(outs), list(sems), start, finish


def _bias_pieces(cls):
    out = []
    for qr in range(Q_ROWS):
        off = (_BIAS_BASE[cls] - qr) * GRID_W
        out.append((qr, off % 128 != 0, off - (off % 128)))
    return out


def _block_class(b, nblk, fn):
    conds = (b == 0, jnp.logical_and(b > 0, b < nblk - 1), b == nblk - 1)
    for cls, cond in enumerate(conds):
        pl.when(cond)(functools.partial(fn, cls))


def _attn_geometry(p4, nx):
    rows = p4.shape[0]
    w = p4.shape[1] // 4
    nhp = w // _PAIR
    nblk = nx // _QB
    qspec = lambda col: pl.BlockSpec((_QB, _PAIR), lambda hp, b: (b, col * nhp + hp))
    kspec = lambda col: pl.BlockSpec((rows, _PAIR), lambda hp, b: (0, col * nhp + hp))
    tspec = pl.BlockSpec((2, GRID_W, _T3_LANES), lambda hp, b: (hp, 0, 0))
    mspec = pl.BlockSpec((None, _QB, _KW), lambda hp, b: (jnp.where(b == 0, 0, jnp.where(b == nblk - 1, 2, 1)), 0, 0))
    lspec = pl.BlockSpec((None, _QB, 2), lambda hp, b: (hp, b, 0))
    ospec = pl.BlockSpec((_QB, _PAIR), lambda hp, b: (b, hp))
    return rows, w, nhp, nblk, qspec, kspec, tspec, mspec, lspec, ospec


def _window_start(b, nx):
    return pl.multiple_of(jnp.clip(b * _QB - PAD_ROWS * GRID_W, 0, nx - _KW), _QB)


def _load_bias(bias_ref, t3_ref, t3s_ref, b, nblk):
    def fill(cls):
        for h in range(2):
            for qr, shifted, off in _bias_pieces(cls):
                src = t3s_ref if shifted else t3_ref
                bias_ref[h, qr * GRID_W:(qr + 1) * GRID_W, :] = src[h, :, off:off + _KW]

    _block_class(b, nblk, fill)


def _attn_fwd(p4, t3, t3s, mask, nx, name, comm=None):
    rows, w, nhp, nblk, qspec, kspec, tspec, mspec, lspec, ospec = _attn_geometry(p4, nx)
    n_ctx = rows - nx
    n_cin, n_cout = (len(comm.ins), len(comm.outs)) if comm else (0, 0)

    def body(*refs):
        q_ref, k_ref, v_ref, g_ref, t3_ref, t3s_ref, m_ref = refs[:7]
        cin = refs[7:7 + n_cin]
        a_ref, o_ref, lse_ref = refs[7 + n_cin:10 + n_cin]
        cout = refs[10 + n_cin:10 + n_cin + n_cout]
        bias_ref = refs[10 + n_cin + n_cout]
        sems = refs[11 + n_cin + n_cout:]
        hp, b = pl.program_id(0), pl.program_id(1)
        if comm:
            pl.when(jnp.logical_and(hp == 0, b == 0))(lambda: comm.start(cin, cout, sems))
        start = _window_start(b, nx)
        _load_bias(bias_ref, t3_ref, t3s_ref, b, nblk)
        qf = q_ref[...].astype(F32) * HEAD_DIM ** -0.5
        kw = k_ref[pl.ds(start, _KW), :].astype(BF16)
        vw = v_ref[pl.ds(start, _KW), :].astype(BF16)
        kcv = k_ref[pl.ds(nx, n_ctx), :].astype(BF16)
        vcv = v_ref[pl.ds(nx, n_ctx), :].astype(BF16)
        lane = lax.broadcasted_iota(jnp.int32, (1, _PAIR), 1)
        outs, lses = [], []
        for h in range(2):
            mine = (lane >= HEAD_DIM) if h else (lane < HEAD_DIM)
            qm = jnp.where(mine, qf, 0.0).astype(BF16)
            s_loc = lax.dot_general(qm, kw, _DIMS["nt"], preferred_element_type=F32) + bias_ref[h] + m_ref[...]
            s_ctx = lax.dot_general(qm, kcv, _DIMS["nt"], preferred_element_type=F32)
            mx = jnp.maximum(jnp.max(s_loc, axis=-1, keepdims=True), jnp.max(s_ctx, axis=-1, keepdims=True))
            p_loc = jnp.exp(s_loc - mx)
            p_ctx = jnp.exp(s_ctx - mx)
            den = jnp.sum(p_loc, axis=-1, keepdims=True) + jnp.sum(p_ctx, axis=-1, keepdims=True)
            inv = 1.0 / den
            o = jnp.dot((p_loc * inv).astype(BF16), vw, preferred_element_type=F32)
            o = o + jnp.dot((p_ctx * inv).astype(BF16), vcv, preferred_element_type=F32)
            outs.append(o)
            lses.append(mx + jnp.log(den))
        o = jnp.where(lane < HEAD_DIM, outs[0], outs[1])
        o_ref[...] = o.astype(ACT)
        a_ref[...] = (o * _silu(g_ref[...].astype(F32))).astype(BF16)
        col = lax.broadcasted_iota(jnp.int32, (1, 2), 1)
        lse_ref[...] = jnp.where(col == 0, lses[0], lses[1])
        if comm:
            pl.when(jnp.logical_and(hp == nhp - 1, b == nblk - 1))(lambda: comm.finish(cin, cout, sems))

    res = pl.pallas_call(
        body, grid=(nhp, nblk),
        in_specs=[qspec(0), kspec(1), kspec(2), qspec(3), tspec, tspec, mspec] + [HBM_SPEC] * n_cin,
        out_specs=[ospec, ospec, lspec] + [HBM_SPEC] * n_cout,
        out_shape=[_sds((nx, w), BF16), _sds((nx, w), ACT), _sds((nhp, nx, 2), F32)] + (comm.outs if comm else []),
        scratch_shapes=[pltpu.VMEM((2, _QB, _KW), F32)] + (comm.sems if comm else []),
        name=name, compiler_params=_cparams("arbitrary", "arbitrary"),
    )(p4, p4, p4, p4, t3, t3s, mask, *(comm.ins if comm else []))
    return res[:3], res[3:]


def _attn_bwd(p4, t3, t3s, mask, o, lse, da, nx, name, comm=None):
    rows, w, nhp, nblk, qspec, kspec, tspec, mspec, lspec, ospec = _attn_geometry(p4, nx)
    n_ctx = rows - nx
    n_cin, n_cout = (len(comm.ins), len(comm.outs)) if comm else (0, 0)

    def body(*refs):
        q_ref, k_ref, v_ref, g_ref, t3_ref, t3s_ref, m_ref, o_ref, lse_ref, da_ref = refs[:10]
        cin = refs[10:10 + n_cin]
        d4_ref, dt3_ref, dt3s_ref = refs[10 + n_cin:13 + n_cin]
        cout = refs[13 + n_cin:13 + n_cin + n_cout]
        bias_ref, dk_ref, dv_ref = refs[13 + n_cin + n_cout:16 + n_cin + n_cout]
        sems = refs[16 + n_cin + n_cout:]
        hp, b = pl.program_id(0), pl.program_id(1)
        if comm:
            pl.when(jnp.logical_and(hp == 0, b == 0))(lambda: comm.start(cin, cout, sems))
        start = _window_start(b, nx)
        here = pl.multiple_of(b * _QB, _QB)

        @pl.when(b == 0)
        def _():
            dk_ref[...] = jnp.zeros(dk_ref.shape, F32)
            dv_ref[...] = jnp.zeros(dv_ref.shape, F32)
            dt3_ref[...] = jnp.zeros(dt3_ref.shape, F32)
            dt3s_ref[...] = jnp.zeros(dt3s_ref.shape, F32)
            d4_ref[0, pl.ds(nx, n_ctx), :] = jnp.zeros((n_ctx, _PAIR), BF16)
            d4_ref[3, pl.ds(nx, n_ctx), :] = jnp.zeros((n_ctx, _PAIR), BF16)

        _load_bias(bias_ref, t3_ref, t3s_ref, b, nblk)
        gv = g_ref[...].astype(F32)
        dav = da_ref[...].astype(F32)
        ov = o_ref[...].astype(F32)
        dov = dav * _silu(gv)
        d4_ref[3, pl.ds(here, _QB), :] = (dav * ov * _dsilu(gv)).astype(BF16)
        qf = q_ref[...].astype(F32) * HEAD_DIM ** -0.5
        kw = k_ref[pl.ds(start, _KW), :].astype(BF16)
        vw = v_ref[pl.ds(start, _KW), :].astype(BF16)
        kcv = k_ref[pl.ds(nx, n_ctx), :].astype(BF16)
        vcv = v_ref[pl.ds(nx, n_ctx), :].astype(BF16)
        lane = lax.broadcasted_iota(jnp.int32, (1, _PAIR), 1)
        dq = jnp.zeros((_QB, _PAIR), F32)
        for h in range(2):
            mine = (lane >= HEAD_DIM) if h else (lane < HEAD_DIM)
            qm = jnp.where(mine, qf, 0.0).astype(BF16)
            dom = jnp.where(mine, dov, 0.0)
            dob = dom.astype(BF16)
            lse = lse_ref[:, h:h + 1]
            s_loc = lax.dot_general(qm, kw, _DIMS["nt"], preferred_element_type=F32)
            p_loc = jnp.exp(s_loc + bias_ref[h] + m_ref[...] - lse)
            p_ctx = jnp.exp(lax.dot_general(qm, kcv, _DIMS["nt"], preferred_element_type=F32) - lse)
            delta = jnp.sum(dom * ov, axis=-1, keepdims=True)
            ds_loc = p_loc * (lax.dot_general(dob, vw, _DIMS["nt"], preferred_element_type=F32) - delta)
            ds_ctx = p_ctx * (lax.dot_general(dob, vcv, _DIMS["nt"], preferred_element_type=F32) - delta)
            dsb_loc = ds_loc.astype(BF16)
            dsb_ctx = ds_ctx.astype(BF16)
            dq_h = (jnp.dot(dsb_loc, kw, preferred_element_type=F32)
                    + jnp.dot(dsb_ctx, kcv, preferred_element_type=F32))
            dq = dq + jnp.where(mine, dq_h, 0.0)
            dk_ref[pl.ds(start, _KW), :] += lax.dot_general(dsb_loc, qm, _DIMS["tn"], preferred_element_type=F32)
            dv_ref[pl.ds(start, _KW), :] += lax.dot_general(p_loc.astype(BF16), dob, _DIMS["tn"],
                                                            preferred_element_type=F32)
            dk_ref[pl.ds(nx, n_ctx), :] += lax.dot_general(dsb_ctx, qm, _DIMS["tn"], preferred_element_type=F32)
            dv_ref[pl.ds(nx, n_ctx), :] += lax.dot_general(p_ctx.astype(BF16), dob, _DIMS["tn"],
                                                           preferred_element_type=F32)
            bias_ref[h] = ds_loc
        d4_ref[0, pl.ds(here, _QB), :] = (dq * HEAD_DIM ** -0.5).astype(BF16)

        def scatter(cls):
            for h in range(2):
                for qr, shifted, off in _bias_pieces(cls):
                    dst = dt3s_ref if shifted else dt3_ref
                    dst[h, :, off:off + _KW] += bias_ref[h, qr * GRID_W:(qr + 1) * GRID_W, :]

        _block_class(b, nblk, scatter)

        @pl.when(b == nblk - 1)
        def _():
            d4_ref[1] = dk_ref[...].astype(BF16)
            d4_ref[2] = dv_ref[...].astype(BF16)

        if comm:
            pl.when(jnp.logical_and(hp == nhp - 1, b == nblk - 1))(lambda: comm.finish(cin, cout, sems))

    tshape = _sds(t3.shape, F32)
    res = pl.pallas_call(
        body, grid=(nhp, nblk),
        in_specs=[qspec(0), kspec(1), kspec(2), qspec(3), tspec, tspec, mspec, ospec, lspec, ospec] + [HBM_SPEC] * n_cin,
        out_specs=[pl.BlockSpec((4, rows, _PAIR), lambda hp, b: (0, 0, hp)), tspec, tspec] + [HBM_SPEC] * n_cout,
        out_shape=[_sds((4, rows, w), BF16), tshape, tshape] + (comm.outs if comm else []),
        scratch_shapes=[pltpu.VMEM((2, _QB, _KW), F32), pltpu.VMEM((rows, _PAIR), F32), pltpu.VMEM((rows, _PAIR), F32)]
        + (comm.sems if comm else []),
        name=name, compiler_params=_cparams("arbitrary", "arbitrary"),
    )(p4, p4, p4, p4, t3, t3s, mask, o, lse, da, *(comm.ins if comm else []))
    return res[:3], res[3:]


def _final(x, g, target, name):
    rows, d = x.shape
    tr = ROW_BLOCK
    nblk = rows // tr

    def body(x_ref, g_ref, t_ref, loss_ref, dx_ref, dg_ref, acc_ref):
        i = pl.program_id(0)
        xv = x_ref[...]
        gv = g_ref[...]
        r = lax.rsqrt(jnp.mean(xv * xv, axis=-1, keepdims=True) + EPS)
        xn = xv * r
        err = xn * gv - t_ref[...]
        dy = err * (1.0 / d)
        dxn = dy * gv
        dx_ref[...] = r * (dxn - xn * jnp.mean(dxn * xn, axis=-1, keepdims=True))
        s_g = jnp.sum(dy * xn, axis=0, keepdims=True)
        s_l = jnp.sum(jnp.mean(err * err, axis=-1, keepdims=True), axis=0, keepdims=True)

        @pl.when(i == 0)
        def _():
            dg_ref[...] = s_g
            acc_ref[...] = s_l

        @pl.when(i > 0)
        def _():
            dg_ref[...] += s_g
            acc_ref[...] += s_l

        @pl.when(i == nblk - 1)
        def _():
            loss_ref[...] = jnp.broadcast_to(0.5 * acc_ref[...], loss_ref.shape)

    row = pl.BlockSpec((tr, d), lambda i: (i, 0))
    vec = pl.BlockSpec((1, d), lambda i: (0, 0))
    return pl.pallas_call(
        body, grid=(nblk,), in_specs=[row, vec, row],
        out_specs=[pl.BlockSpec((1, 128), lambda i: (0, 0)), row, vec],
        out_shape=[_sds((1, 128), F32), _sds((rows, d), F32), _sds((1, d), F32)],
        scratch_shapes=[pltpu.VMEM((1, 1), F32)], name=name, compiler_params=_cparams("arbitrary"),
    )(x, g, target)


def _as2d(a):
    if a.ndim == 1:
        return a.reshape(-1, 128) if a.shape[0] % 128 == 0 else a.reshape(1, -1)
    return a.reshape(-1, a.shape[-1])


def _adamw(w, g, m, v, name):
    shape = w.shape
    w2, g2, m2, v2 = (_as2d(t) for t in (w, g.reshape(shape), m, v))
    rows, cols = w2.shape
    tr = 512 if rows % 512 == 0 else rows
    c1 = 1.0 - ADAM_B1 ** ADAM_STEP
    c2 = 1.0 - ADAM_B2 ** ADAM_STEP

    def body(w_ref, g_ref, m_ref, v_ref, d_ref, nm_ref, nv_ref):
        gv = g_ref[...]
        nm = ADAM_B1 * m_ref[...] + (1.0 - ADAM_B1) * gv
        nv = ADAM_B2 * v_ref[...] + (1.0 - ADAM_B2) * (gv * gv)
        nm_ref[...] = nm
        nv_ref[...] = nv
        d_ref[...] = -ADAM_LR * ((nm / c1) / (jnp.sqrt(nv / c2) + ADAM_EPS) + ADAM_WD * w_ref[...])

    blk = pl.BlockSpec((tr, cols), lambda i: (i, 0))
    outs = pl.pallas_call(
        body, grid=(rows // tr,), in_specs=[blk] * 4, out_specs=[blk] * 3,
        out_shape=[_sds((rows, cols), F32)] * 3, name=name, compiler_params=_cparams("parallel"),
    )(w2, g2, m2, v2)
    return tuple(t.reshape(shape) for t in outs)


def _sum_lead(x, name, out_dtype=F32):
    n, rows, cols = x.shape
    tr = 512 if rows % 512 == 0 else rows

    def body(x_ref, o_ref):
        acc = x_ref[0].astype(F32)
        for k in range(1, n):
            acc = acc + x_ref[k].astype(F32)
        o_ref[...] = acc.astype(out_dtype)

    return pl.pallas_call(
        body, grid=(rows // tr,), in_specs=[pl.BlockSpec((n, tr, cols), lambda i: (0, i, 0))],
        out_specs=pl.BlockSpec((tr, cols), lambda i: (i, 0)), out_shape=_sds((rows, cols), out_dtype),
        name=name, compiler_params=_cparams("parallel"),
    )(x)


_NO_CTX = 1 << 30


def _seg_vecs(mod_l, which, nseg):
    return mod_l[:nseg, which][:, None, :]


def _norm_grads(dshift, dgeff, dgate, g, scale):
    nseg, _, d = dshift.shape
    dmod = jnp.stack([dshift[:, 0], dgeff[:, 0] * g, dgate[:, 0]], axis=1)
    if nseg == 1:
        dmod = jnp.concatenate([dmod, jnp.zeros((1, 3, d), F32)], axis=0)
    dg = jnp.sum(dgeff[:, 0] * (1.0 + scale[:, 0]), axis=0)
    return dmod, dg


def _pool_layer(xin, g, mod_l, w_in, w_grp, w_out, pscale, nx, tag):
    rows = xin.shape[0]
    nseg = 2 if rows > nx else 1
    nxb = nx // ROW_BLOCK if nseg == 2 else _NO_CTX
    shift, scale, gate = (_seg_vecs(mod_l, k, nseg) for k in range(3))
    h, r = _normmod_fwd(xin, g, scale, shift, nxb, f"norm_fwd_{tag}")
    uv = _mm_nn(h, w_in, f"w_in_fwd_{tag}", ACT)
    z = _pool_fwd(uv, nx, f"pool_fwd_{tag}")
    mixed, a = _grp_fwd(z, w_grp, uv, pscale, f"grp_fwd_{tag}")
    yx, xout = _mm_out_resid(a, w_out, xin, gate, nxb, f"w_out_fwd_{tag}")

    def backward(dxo):
        dyx, dgate = _gate_bwd(dxo, yx, gate, nxb, f"gate_bwd_{tag}")
        da = _mm_nt(dyx, w_out, f"w_out_bwd_{tag}", ACT)
        gw_out = _mm_tn(a, dyx, f"w_out_grad_{tag}", BF16)
        dm, dz, dgt, dscale = _grp_bwd(da, mixed, uv, pscale, w_grp, f"grp_bwd_{tag}")
        gw_grp = _grp_wgrad(z, dm, w_grp.shape[0], f"grp_grad_{tag}", BF16)
        duv = _pool_bwd(dz, dgt, nx, f"pool_bwd_{tag}")
        dh = _mm_nt_parts(duv, w_in, f"w_in_bwd_{tag}")
        gw_in = _mm_tn_parts(h, duv, f"w_in_grad_{tag}", BF16)
        dx, dshift, dgeff = _normmod_bwd(dh, xin, r, g, scale, dxo, nxb, f"norm_bwd_{tag}")
        dmod, dg = _norm_grads(dshift, dgeff, dgate, g[0], scale)
        return dx, dmod, dg, dict(w_in=gw_in, w_grp=gw_grp, w_out=gw_out, scale=dscale)

    return xout, backward


def _rpb_tables(rpb, onehot):
    nh, na, nb = rpb.shape
    flat = jnp.pad(rpb.reshape(nh * na, nb), ((0, 0), (0, 128 - nb)))
    t1 = _matmul(
        flat, onehot, mode="nn", grid=(1, 4), exact=True,
        a_spec=pl.BlockSpec((nh * na, 128), lambda i, j: (0, 0)), b_spec=pl.BlockSpec((128, 1024), lambda i, j: (0, j)),
        out_shapes=[_sds((nh * na, GRID_W * GRID_W), F32)], out_specs=[pl.BlockSpec((nh * na, 1024), lambda i, j: (0, j))],
        name="rpb_table")[0]
    t3 = t1.reshape(nh, na, GRID_W, GRID_W).transpose(0, 2, 1, 3).reshape(nh, GRID_W, na * GRID_W)
    front = _T3_FRONT * GRID_W
    back = _T3_LANES - na * GRID_W - front
    return (jnp.pad(t3, ((0, 0), (0, 0), (front, back))),
            jnp.pad(t3, ((0, 0), (0, 0), (front - GRID_W, back + GRID_W))))


def _rpb_grad(dt3, dt3s, onehot, nh, na, nb):
    def fold(t, front, name):
        flat = t[:, :, front:front + na * GRID_W].reshape(nh, GRID_W, na, GRID_W).transpose(0, 2, 1, 3)
        flat = flat.reshape(nh * na, -1)
        out = _matmul(
            flat, onehot, mode="nt", grid=(1, 4), nk=4, acc_shape=(nh * na, 128), exact=True,
            a_spec=pl.BlockSpec((nh * na, 1024), lambda i, k: (0, k)), b_spec=pl.BlockSpec((128, 1024), lambda i, k: (0, k)),
            out_shapes=[_sds((nh * na, 128), F32)], out_specs=[pl.BlockSpec((nh * na, 128), lambda i, k: (0, 0))],
            name=name)[0]
        return out[:, :nb].reshape(nh, na, nb)

    front = _T3_FRONT * GRID_W
    return fold(dt3, front, "rpb_grad_a") + fold(dt3s, front - GRID_W, "rpb_grad_b")


def _na_layer(xc, g, mod_l, w_in, rpb, w_out, nx, consts, comm=None):
    nxb = nx // ROW_BLOCK
    mask, onehot = consts
    shift, scale = _seg_vecs(mod_l, 0, 2), _seg_vecs(mod_l, 1, 2)
    gate = _seg_vecs(mod_l, 2, 1)
    h, r = _normmod_fwd(xc, g, scale, shift, nxb, "norm_fwd_na")
    p4 = _mm_nn(h, w_in, "w_in_fwd_na", ACT)
    t3, t3s = _rpb_tables(rpb, onehot)
    (a, o, lse), carried = _attn_fwd(p4, t3, t3s, mask, nx, "attn_fwd", comm)
    yx, xout = _mm_out_resid(a, w_out, xc, gate, _NO_CTX, "w_out_fwd_na")

    def backward(dxo, comm=None):
        dyx, dgate = _gate_bwd(dxo, yx, gate, _NO_CTX, "gate_bwd_na")
        da = _mm_nt(dyx, w_out, "w_out_bwd_na", ACT)
        gw_out = _mm_tn(a, dyx, "w_out_grad_na", BF16)
        (d4, dt3, dt3s), carried_bwd = _attn_bwd(p4, t3, t3s, mask, o, lse, da, nx, "attn_bwd", comm)
        dh = _mm_nt_parts(d4, w_in, "w_in_bwd_na")
        gw_in = _mm_tn_parts(h, d4, "w_in_grad_na", BF16)
        dx, dshift, dgeff = _normmod_bwd(dh, xc, r, g, scale, dxo, nxb, "norm_bwd_na")
        dgate2 = jnp.concatenate([dgate, jnp.zeros_like(dgate)], axis=0)
        dmod, dg = _norm_grads(dshift, dgeff, dgate2, g[0], scale)
        drpb = _rpb_grad(dt3, dt3s, onehot, *rpb.shape)
        return dx, dmod, dg, dict(w_in=gw_in, w_out=gw_out, rpb=drpb), carried_bwd

    return xout, backward, carried


def _conv_layer(xin, g, mod_l, w_in, dw, db, w_out):
    shift, scale, gate = (_seg_vecs(mod_l, k, 1) for k in range(3))
    h, r = _normmod_fwd(xin, g, scale, shift, _NO_CTX, "norm_fwd_conv")
    p4 = _mm_nn(h, w_in, "w_in_fwd_conv", ACT)
    a = _conv_fwd(p4, dw, db, "conv_fwd")
    yx, xout = _mm_out_resid(a, w_out, xin, gate, _NO_CTX, "w_out_fwd_conv")

    def backward(dxo):
        dyx, dgate = _gate_bwd(dxo, yx, gate, _NO_CTX, "gate_bwd_conv")
        da = _mm_nt(dyx, w_out, "w_out_bwd_conv", ACT)
        gw_out = _mm_tn(a, dyx, "w_out_grad_conv", BF16)
        d4, ddw, ddb = _conv_bwd(da, p4, dw, db, "conv_bwd")
        dh = _mm_nt_parts(d4, w_in, "w_in_bwd_conv")
        gw_in = _mm_tn_parts(h, d4, "w_in_grad_conv", BF16)
        dx, dshift, dgeff = _normmod_bwd(dh, xin, r, g, scale, dxo, _NO_CTX, "norm_bwd_conv")
        dmod, dg = _norm_grads(dshift, dgeff, dgate, g[0], scale)
        return dx, dmod, dg, dict(w_in=gw_in, w_out=gw_out, dw=ddw, db=ddb)

    return xout, backward


def _example_step(x, ctx, target, mod, norm_g, final_g, wts, late_comm=None, late_weights=None, grad_comm=None):
    nx = x.shape[0]
    consts = (_attn_mask(), _rpb_onehot())
    g_rows = [norm_g[i:i + 1] for i in range(4)]
    xc0 = jnp.concatenate([x, ctx], axis=0)
    xc1, bwd0 = _pool_layer(xc0, g_rows[0], mod[0], wts["pool_w_in"][0], wts["pool_w_grp"][0], wts["pool_w_out"][0],
                            wts["pool_scale"][0:1], nx, "p0")
    x2, bwd1, carried = _na_layer(xc1, g_rows[1], mod[1], wts["na_w_in"], wts["na_rpb"], wts["na_w_out"], nx, consts,
                                  late_comm)
    if late_weights is not None:
        wts = {**wts, **late_weights(carried)}
    x3, bwd2 = _conv_layer(x2, g_rows[2], mod[2], wts["conv_w_in"], wts["conv_dw"], wts["conv_db"], wts["conv_w_out"])
    x4, bwd3 = _pool_layer(x3, g_rows[3], mod[3], wts["pool_w_in"][1], wts["pool_w_grp"][1], wts["pool_w_out"][1],
                           wts["pool_scale"][1:2], nx, "p3")
    loss, dx4, dfinal_g = _final(x4, final_g, target, "loss_head")
    dx3, dmod3, dg3, gr3 = bwd3(dx4)
    dx2, dmod2, dg2, gr2 = bwd2(dx3)
    dxc1, dmod1, dg1, gr1, carried_bwd = bwd1(dx2, grad_comm(gr3, gr2) if grad_comm else None)
    dxc0, dmod0, dg0, gr0 = bwd0(dxc1)
    return dict(
        loss=loss, grad_x=dxc0[:nx], dmod=jnp.stack([dmod0, dmod1, dmod2, dmod3]),
        dnorm_g=jnp.stack([dg0, dg1, dg2, dg3]), dfinal_g=dfinal_g, layers=(gr0, gr1, gr2, gr3), carried=carried_bwd)


_AXES = ("x", "y", "c")
_CHIP_FLIPS = ((1, 0), (0, 1), (1, 1))


def _position():
    return tuple(lax.axis_index(a) for a in _AXES)


def _flipped(pos, flip):
    return tuple(1 - p if f else p for p, f in zip(pos, flip))


def _all_gather(v, axes, name):
    flips = [f for f in np.ndindex(2, 2, 2) if any(f) and all(a in axes or not b for a, b in zip(_AXES, f))]
    n = len(flips) + 1

    def body(v_ref, o_ref, send_sems, recv_sems, local_sem):
        pos = _position()
        slot = 0
        for a, p in zip(_AXES, pos):
            if a in axes:
                slot = 2 * slot + p
        local = pltpu.make_async_copy(v_ref, o_ref.at[slot], local_sem)
        local.start()
        copies = []
        for k, flip in enumerate(flips):
            cp = pltpu.make_async_remote_copy(v_ref, o_ref.at[slot], send_sems.at[k], recv_sems.at[k],
                                              device_id=_flipped(pos, flip), device_id_type=MESH)
            cp.start()
            copies.append(cp)
        for cp in copies:
            cp.wait()
        local.wait()

    return pl.pallas_call(
        body, in_specs=[HBM_SPEC], out_specs=HBM_SPEC, out_shape=_sds((n,) + v.shape, v.dtype),
        scratch_shapes=[pltpu.SemaphoreType.DMA((n - 1,)), pltpu.SemaphoreType.DMA((n - 1,)), pltpu.SemaphoreType.DMA(())],
        name=name,
    )(v)


class _Item:
    def __init__(self, key, layer, shape, shard_axis, half_axis):
        self.key, self.layer, self.shape = key, layer, tuple(shape)
        self.shard_axis, self.half_axis = shard_axis, half_axis
        self.shard = shape[shard_axis] // 4
        self.half = shape[half_axis] // 2

    def sized(self, shard=False, half=False):
        s = list(self.shape)
        if shard:
            s[self.shard_axis] = self.shard
        if half:
            s[self.half_axis] = self.half
        return tuple(s)

    def window(self, ref, chip=None, half=None):
        idx = [slice(None)] * len(self.shape)
        if chip is not None:
            idx[self.shard_axis] = pl.ds(chip * self.shard, self.shard)
        if half is not None:
            idx[self.half_axis] = pl.ds(half * self.half, self.half)
        return ref.at[tuple(idx)]


def _items(d, w):
    out = []
    for j in range(2):
        out += [_Item("pool_w_in", j, (d, 2 * w), 1, 0), _Item("pool_w_grp", j, (4, w // 4, w // 4), 1, 0),
                _Item("pool_w_out", j, (w, d), 0, 1)]
    out += [_Item("na_w_in", 0, (d, 4 * w), 1, 0), _Item("na_w_out", 0, (w, d), 0, 1),
            _Item("conv_w_in", 0, (d, 4 * w), 1, 0), _Item("conv_w_out", 0, (w, d), 0, 1)]
    return out


def _gather_weights(shards, items, name):
    comm = _gather_comm(shards, items)

    def body(*refs):
        n = len(items)
        comm.start(refs[:n], refs[n:2 * n], refs[2 * n:])
        comm.finish(refs[:n], refs[n:2 * n], refs[2 * n:])

    return pl.pallas_call(
        body, in_specs=[HBM_SPEC] * len(items), out_specs=[HBM_SPEC] * len(items), out_shape=comm.outs,
        scratch_shapes=comm.sems, name=name,
    )(*shards)


def _gather_comm(shards, items):
    n = len(items)

    def copies(src, dst, sems, onward):
        send_a, recv_a, send_b, recv_b, send_c, recv_c = sems
        x, y, c = _position()
        chip = 2 * x + y
        sibling = (x, y, 1 - c)
        own, out, fwd, fwd_in = [], [], [], []
        for i, it in enumerate(items):
            own.append(pltpu.make_async_remote_copy(src[i], it.window(dst[i], chip=chip), send_c.at[i], recv_c.at[i],
                                                    device_id=sibling, device_id_type=MESH))
            for k, flip in enumerate(_CHIP_FLIPS):
                px, py = _flipped((x, y), flip)
                s = 3 * i + k
                out.append(pltpu.make_async_remote_copy(
                    it.window(src[i], half=c), it.window(dst[i], chip=chip, half=c), send_a.at[s], recv_a.at[s],
                    device_id=(px, py, c), device_id_type=MESH))
                if onward:
                    got = it.window(dst[i], chip=2 * px + py, half=c)
                    fwd.append(pltpu.make_async_remote_copy(got, got, send_b.at[s], recv_b.at[s],
                                                            device_id=sibling, device_id_type=MESH))
                    other = it.window(dst[i], chip=2 * px + py, half=1 - c)
                    fwd_in.append(pltpu.make_async_remote_copy(other, other, send_b.at[s], recv_b.at[s],
                                                               device_id=sibling, device_id_type=MESH))
        return own, out, fwd, fwd_in

    def start(src, dst, sems):
        own, out, _, _ = copies(src, dst, sems, False)
        for cp in own + out:
            cp.start()

    def finish(src, dst, sems):
        own, out, fwd, fwd_in = copies(src, dst, sems, True)
        for arrived, onward in zip(out, fwd):
            arrived.wait_recv()
            onward.start()
        for cp in fwd_in:
            cp.wait_recv()
        for cp in out + fwd:
            cp.wait_send()
        for cp in own:
            cp.wait()

    sems = [pltpu.SemaphoreType.DMA((3 * n,)) for _ in range(4)] + [pltpu.SemaphoreType.DMA((n,)) for _ in range(2)]
    return _Comm(shards, [_sds(it.shape, BF16) for it in items], sems, start, finish)


def _pair_swap(arrays, windows, out_shapes, name):
    n = len(arrays)

    def body(*refs):
        src, got = refs[:n], refs[n:2 * n]
        send_sems, recv_sems = refs[2 * n:]
        x, y, c = _position()
        copies = []
        for i in range(n):
            cp = pltpu.make_async_remote_copy(windows[i](src[i], 1 - c), got[i], send_sems.at[i], recv_sems.at[i],
                                              device_id=(x, y, 1 - c), device_id_type=MESH)
            cp.start()
            copies.append(cp)
        for cp in copies:
            cp.wait()

    return pl.pallas_call(
        body, in_specs=[HBM_SPEC] * n, out_specs=[HBM_SPEC] * n, out_shape=list(out_shapes),
        scratch_shapes=[pltpu.SemaphoreType.DMA((n,)), pltpu.SemaphoreType.DMA((n,))], name=name,
    )(*arrays)


def _chip_exchange(partials, items, name):
    comm = _chip_exchange_comm(partials, items)

    def body(*refs):
        n = len(items)
        comm.start(refs[:n], refs[n:2 * n], refs[2 * n:])
        comm.finish(refs[:n], refs[n:2 * n], refs[2 * n:])

    return pl.pallas_call(
        body, in_specs=[HBM_SPEC] * len(items), out_specs=[HBM_SPEC] * len(items), out_shape=comm.outs,
        scratch_shapes=comm.sems, name=name,
    )(*partials)


def _chip_exchange_comm(partials, items):
    n = len(items)

    def copies(src, dst, sems):
        send_sems, recv_sems = sems
        x, y, c = _position()
        out = []
        for i, it in enumerate(items):
            for k, flip in enumerate(_CHIP_FLIPS):
                px, py = _flipped((x, y), flip)
                out.append(pltpu.make_async_remote_copy(
                    it.window(src[i], chip=2 * px + py), dst[i].at[k], send_sems.at[3 * i + k],
                    recv_sems.at[3 * i + k], device_id=(px, py, c), device_id_type=MESH))
        return out

    def start(src, dst, sems):
        for cp in copies(src, dst, sems):
            cp.start()

    def finish(src, dst, sems):
        for cp in copies(src, dst, sems):
            cp.wait()

    return _Comm(partials, [_sds((3,) + it.sized(shard=True, half=True), BF16) for it in items],
                 [pltpu.SemaphoreType.DMA((3 * n,)), pltpu.SemaphoreType.DMA((3 * n,))], start, finish)


def _pair_sum(g, got, it, pos, name):
    rows_split = it.half_axis == 0
    g2 = g.reshape(-1, g.shape[-1])
    got2 = got.reshape(-1, got.shape[-1])
    rows, cols = got2.shape
    tr = min(rows, 256)
    nb = rows // tr

    def body(pos_ref, g_ref, got_ref, o_ref):
        o_ref[...] = (g_ref[...].astype(F32) + got_ref[...].astype(F32)).astype(BF16)

    g_map = (lambda i, pos: (pos[1] * nb + i, 0)) if rows_split else (lambda i, pos: (i, pos[1]))
    blk = pl.BlockSpec((tr, cols), lambda i, pos: (i, 0))
    return pl.pallas_call(
        body, grid_spec=pltpu.PrefetchScalarGridSpec(
            num_scalar_prefetch=1, grid=(nb,), in_specs=[pl.BlockSpec((tr, cols), g_map), blk], out_specs=blk),
        out_shape=_sds((rows, cols), BF16), name=name, compiler_params=_cparams("parallel"),
    )(pos, g2, got2).reshape(got.shape)


_FLIP_SLOT = {2: 0, 1: 1, 3: 2}


def _chip_sum(pair, slots, it, pos, name):
    shape = it.sized(shard=True, half=True)
    nd = len(shape)

    def body(pos_ref, p_ref, s_ref, o_ref):
        chip = pos_ref[0]
        for own in range(4):
            @pl.when(chip == own)
            def _():
                acc = None
                for k in range(4):
                    v = (p_ref[...] if k == own else s_ref[_FLIP_SLOT[own ^ k]]).astype(F32)
                    acc = v if acc is None else acc + v
                o_ref[...] = acc

    p_map = lambda i, pos: tuple(pos[0] if ax == it.shard_axis else 0 for ax in range(nd))
    return pl.pallas_call(
        body, grid_spec=pltpu.PrefetchScalarGridSpec(
            num_scalar_prefetch=1, grid=(1,),
            in_specs=[pl.BlockSpec(shape, p_map), pl.BlockSpec((3,) + shape, lambda i, pos: (0,) * (nd + 1))],
            out_specs=pl.BlockSpec(shape, lambda i, pos: (0,) * nd)),
        out_shape=_sds(shape, F32), name=name, compiler_params=_cparams("arbitrary"),
    )(pos, pair, slots)


_GRAD_KEYS = ("pool_w_in", "pool_w_grp", "pool_w_out", "na_w_in", "na_w_out", "conv_w_in", "conv_w_out")


def _adamw_matrix(w, m, v, owns, others, it, pos, name):
    nl = w.shape[0]
    rows_split = it.half_axis == 0
    r, cdim = int(np.prod(w.shape[1:-1])), w.shape[-1]
    hr, hc = (r // 2, cdim) if rows_split else (r, cdim // 2)
    br = min(hr, 256)
    nb = hr // br
    c1 = 1.0 - ADAM_B1 ** ADAM_STEP
    c2 = 1.0 - ADAM_B2 ** ADAM_STEP

    def body(pos_ref, w_ref, m_ref, v_ref, *rest):
        own_refs, other_refs = rest[:nl], rest[nl:2 * nl]
        g_ref, d_ref, nm_ref, nv_ref = rest[2 * nl:]
        j, h = pl.program_id(0), pl.program_id(1)
        own, other = own_refs[0][...], other_refs[0][...]
        for q in range(1, nl):
            own = jnp.where(j == q, own_refs[q][...], own)
            other = jnp.where(j == q, other_refs[q][...], other)
        gv = jnp.where(h == pos_ref[1], own, other)
        nm = ADAM_B1 * m_ref[...] + (1.0 - ADAM_B1) * gv
        nv = ADAM_B2 * v_ref[...] + (1.0 - ADAM_B2) * (gv * gv)
        g_ref[...] = gv
        nm_ref[...] = nm
        nv_ref[...] = nv
        d_ref[...] = -ADAM_LR * ((nm / c1) / (jnp.sqrt(nv / c2) + ADAM_EPS) + ADAM_WD * w_ref[...])

    if rows_split:
        full = pl.BlockSpec((None, br, hc), lambda j, h, i, pos: (j, h * nb + i, 0))
    else:
        full = pl.BlockSpec((None, br, hc), lambda j, h, i, pos: (j, i, h))
    half = pl.BlockSpec((br, hc), lambda j, h, i, pos: (i, 0))
    flat = lambda t: t.reshape(nl, r, cdim)
    outs = pl.pallas_call(
        body, grid_spec=pltpu.PrefetchScalarGridSpec(
            num_scalar_prefetch=1, grid=(nl, 2, nb), in_specs=[full] * 3 + [half] * (2 * nl), out_specs=[full] * 4),
        out_shape=[_sds((nl, r, cdim), F32)] * 4, name=name,
        compiler_params=_cparams("parallel", "parallel", "parallel"),
    )(pos, flat(w), flat(m), flat(v), *[t.reshape(hr, hc) for t in list(owns) + list(others)])
    return tuple(t.reshape(w.shape) for t in outs)


_WEIGHTS = ("c_ctx", "norm_g", "ada_w", "ada_b", "pool_w_in", "pool_w_grp", "pool_scale", "pool_w_out", "na_w_in",
            "na_rpb", "na_w_out", "conv_w_in", "conv_dw", "conv_db", "conv_w_out", "final_g")
_COND_ROWS = 16


def _modulations(cond, ada_w, ada_b_cols):
    nl, d, n = ada_w.shape
    return _matmul(
        cond, ada_w, mode="nn", grid=(nl, 1), a_silu=True, epilogue="bias",
        a_spec=pl.BlockSpec((_COND_ROWS, d), lambda i, j: (0, 0)), b_spec=pl.BlockSpec((None, d, n), lambda i, j: (i, 0, 0)),
        extra=(ada_b_cols,), extra_specs=(pl.BlockSpec((None, 1, n), lambda i, j: (i, 0, 0)),),
        out_shapes=[_sds((nl, _COND_ROWS, n), F32)], out_specs=[pl.BlockSpec((None, _COND_ROWS, n), lambda i, j: (i, 0, 0))],
        name="modulations")[0]


def _ada_w_grad(cond, dm_cols):
    d = cond.shape[1]
    nl, _, n = dm_cols.shape
    return _matmul(
        cond, dm_cols, mode="tn", grid=(nl, 1), a_silu=True,
        a_spec=pl.BlockSpec((_COND_ROWS, d), lambda i, j: (0, 0)), b_spec=pl.BlockSpec((None, _COND_ROWS, n), lambda i, j: (i, 0, 0)),
        out_shapes=[_sds((nl, d, n), F32)], out_specs=[pl.BlockSpec((None, d, n), lambda i, j: (i, 0, 0))],
        name="ada_w_grad")[0]


def _cond_grad(dm_cols, ada_w):
    nl, d, n = ada_w.shape
    return _matmul(
        dm_cols, ada_w, mode="nt", grid=(1, nl), nk=nl, acc_shape=(_COND_ROWS, d),
        a_spec=pl.BlockSpec((None, _COND_ROWS, n), lambda i, q: (q, 0, 0)), b_spec=pl.BlockSpec((None, d, n), lambda i, q: (q, 0, 0)),
        out_shapes=[_sds((_COND_ROWS, d), F32)], out_specs=[pl.BlockSpec((_COND_ROWS, d), lambda i, q: (0, 0))],
        name="cond_grad")[0]


def _pack(parts):
    flat = [p.reshape(-1) for p in parts]
    sizes = [f.shape[0] for f in flat]
    total = sum(sizes)
    rows = -(-total // 1024) * 8
    packed = jnp.concatenate(flat + [jnp.zeros((rows * 128 - total,), F32)]).reshape(rows, 128)
    offs = np.concatenate([[0], np.cumsum(sizes)])[:-1]
    return packed, [(int(o), p.shape) for o, p in zip(offs, parts)]


def _unpack(flat, layout, k):
    off, shape = layout[k]
    return flat[..., off:off + int(np.prod(shape))].reshape(flat.shape[:-1] + tuple(shape))


def kernel(x, c, ctx, c_ctx, norm_g, ada_w, ada_b, pool_w_in, pool_w_grp, pool_scale, pool_w_out, na_w_in, na_rpb, na_w_out, conv_w_in, conv_dw, conv_db, conv_w_out, final_g, loss_target, m_c_ctx, m_norm_g, m_ada_w, m_ada_b, m_pool_w_in, m_pool_w_grp, m_pool_scale, m_pool_w_out, m_na_w_in, m_na_rpb, m_na_w_out, m_conv_w_in, m_conv_dw, m_conv_db, m_conv_w_out, m_final_g, v_c_ctx, v_norm_g, v_ada_w, v_ada_b, v_pool_w_in, v_pool_w_grp, v_pool_scale, v_pool_w_out, v_na_w_in, v_na_rpb, v_na_w_out, v_conv_w_in, v_conv_dw, v_conv_db, v_conv_w_out, v_final_g):
    params = dict(c_ctx=c_ctx, norm_g=norm_g, ada_w=ada_w, ada_b=ada_b, pool_w_in=pool_w_in, pool_w_grp=pool_w_grp,
                  pool_scale=pool_scale, pool_w_out=pool_w_out, na_w_in=na_w_in, na_rpb=na_rpb, na_w_out=na_w_out,
                  conv_w_in=conv_w_in, conv_dw=conv_dw, conv_db=conv_db, conv_w_out=conv_w_out, final_g=final_g)
    mom1 = dict(c_ctx=m_c_ctx, norm_g=m_norm_g, ada_w=m_ada_w, ada_b=m_ada_b, pool_w_in=m_pool_w_in,
                pool_w_grp=m_pool_w_grp, pool_scale=m_pool_scale, pool_w_out=m_pool_w_out, na_w_in=m_na_w_in,
                na_rpb=m_na_rpb, na_w_out=m_na_w_out, conv_w_in=m_conv_w_in, conv_dw=m_conv_dw, conv_db=m_conv_db,
                conv_w_out=m_conv_w_out, final_g=m_final_g)
    mom2 = dict(c_ctx=v_c_ctx, norm_g=v_norm_g, ada_w=v_ada_w, ada_b=v_ada_b, pool_w_in=v_pool_w_in,
                pool_w_grp=v_pool_w_grp, pool_scale=v_pool_scale, pool_w_out=v_pool_w_out, na_w_in=v_na_w_in,
                na_rpb=v_na_rpb, na_w_out=v_na_w_out, conv_w_in=v_conv_w_in, conv_dw=v_conv_dw, conv_db=v_conv_db,
                conv_w_out=v_conv_w_out, final_g=v_final_g)
    d = x.shape[-1]
    w = na_w_out.shape[1] * 4
    xi, yi, ci = _position()
    chip = 2 * xi + yi
    dev = 2 * chip + ci
    n_ada = ada_w.shape[-1]

    def chip_cols(a, size):
        return lax.dynamic_slice_in_dim(a, chip * size, size, axis=a.ndim - 1)

    conds = _all_gather(c.reshape(8, d // 8), _AXES, "gather_cond").reshape(8, d)
    cond = jnp.concatenate([conds, c_ctx[None], jnp.zeros((_COND_ROWS - 9, d), F32)], axis=0)
    mod_cols = _modulations(cond, ada_w, chip_cols(ada_b, n_ada)[:, None, :])
    mod_all = _all_gather(mod_cols, ("x", "y"), "gather_mod")
    mod_all = mod_all.transpose(1, 2, 0, 3).reshape(4, _COND_ROWS, 3, d)
    mod = jnp.stack([lax.dynamic_index_in_dim(mod_all, dev, axis=1, keepdims=False), mod_all[:, 8]], axis=1)

    items = _items(d, w)
    early = [it for it in items if (it.key.startswith("pool") and it.layer == 0) or it.key.startswith("na")]
    late = [it for it in items if it not in early]
    shard_of = lambda it: params[it.key][it.layer].astype(BF16)
    full = {(it.key, it.layer): mat
            for it, mat in zip(early, _gather_weights([shard_of(it) for it in early], early, "gather_weights"))}
    late_comm = _gather_comm([shard_of(it) for it in late], late)

    def late_weights(mats):
        full.update({(it.key, it.layer): mat for it, mat in zip(late, mats)})
        return dict(pool_w_in=[full[("pool_w_in", j)] for j in range(2)],
                    pool_w_grp=[full[("pool_w_grp", j)] for j in range(2)],
                    pool_w_out=[full[("pool_w_out", j)] for j in range(2)],
                    conv_w_in=full[("conv_w_in", 0)], conv_w_out=full[("conv_w_out", 0)])

    small = _all_gather(_pack([pool_scale, conv_dw, conv_db])[0], ("x", "y"), "gather_small")
    small_layout = _pack([pool_scale, conv_dw, conv_db])[1]
    small = small.reshape(4, -1)

    def whole(k):
        parts = _unpack(small, small_layout, k)
        return jnp.moveaxis(parts, 0, -2).reshape(parts.shape[1:-1] + (-1,))

    wts = dict(pool_w_in=[full[("pool_w_in", 0)]], pool_w_grp=[full[("pool_w_grp", 0)]],
               pool_w_out=[full[("pool_w_out", 0)]], na_w_in=full[("na_w_in", 0)], na_w_out=full[("na_w_out", 0)],
               pool_scale=whole(0), na_rpb=na_rpb[0], conv_dw=whole(1)[0], conv_db=whole(2))
    pos = jnp.stack([chip, ci]).astype(jnp.int32)

    def layer_grads(its, by_layer):
        pick = {"pool_w_in": "w_in", "pool_w_grp": "w_grp", "pool_w_out": "w_out", "na_w_in": "w_in",
                "na_w_out": "w_out", "conv_w_in": "w_in", "conv_w_out": "w_out"}
        return [by_layer[(it.key.split("_")[0], it.layer)][pick[it.key]] for it in its]

    def pair_sums(its, mats, tag):
        got = _pair_swap(mats, [(lambda ref, half, it=it: it.window(ref, half=half)) for it in its],
                         [_sds(it.sized(half=True), BF16) for it in its], f"pair_exchange_{tag}")
        return [_pair_sum(g, s, it, pos, f"pair_sum_{tag}{i}") for i, (g, s, it) in enumerate(zip(mats, got, its))]

    pairs = dict()

    def grad_comm(gr3, gr2):
        pairs["late"] = pair_sums(late, layer_grads(late, {("pool", 1): gr3, ("conv", 0): gr2}), "late")
        return _chip_exchange_comm(pairs["late"], late)

    res = _example_step(x[0], ctx[0], loss_target[0], mod, norm_g, final_g[None], wts, late_comm, late_weights,
                        grad_comm)
    g0, g1, g2, g3 = res["layers"]
    pairs["early"] = pair_sums(early, layer_grads(early, {("pool", 0): g0, ("na", 0): g1}), "early")
    slots = dict(zip(late, res["carried"]))
    slots.update(zip(early, _chip_exchange(pairs["early"], early, "chip_exchange")))
    pair_of = dict(zip(late, pairs["late"]))
    pair_of.update(zip(early, pairs["early"]))
    reduced = [_chip_sum(pair_of[it], slots[it], it, pos, f"chip_sum_{i}") for i, it in enumerate(items)]
    theirs = _pair_swap(reduced, [lambda ref, half: ref] * len(items),
                        [_sds(t.shape, F32) for t in reduced], "pair_return")
    grads, matrix_out = dict(), dict()
    for k in _GRAD_KEYS:
        idx = [i for i, it in enumerate(items) if it.key == k]
        res_k = _adamw_matrix(params[k], mom1[k], mom2[k], [reduced[i] for i in idx], [theirs[i] for i in idx],
                              items[idx[0]], pos, f"adamw_{k}")
        grads[k], matrix_out[k] = res_k[0], res_k[1:]

    packed, layout = _pack([res["dfinal_g"], res["dnorm_g"], res["dmod"], g1["rpb"],
                            jnp.concatenate([g0["scale"], g3["scale"]], axis=0), g2["dw"], g2["db"]])
    every = _all_gather(packed, _AXES, "gather_vec_grads")
    total = _sum_lead(every, "sum_vec_grads").reshape(-1)
    every = every.reshape(8, -1)
    grads["final_g"] = _unpack(total, layout, 0).reshape(final_g.shape)
    grads["norm_g"] = _unpack(total, layout, 1)
    grads["na_rpb"] = _unpack(total, layout, 3)[None]
    grads["pool_scale"] = chip_cols(_unpack(total, layout, 4), pool_scale.shape[-1])
    grads["conv_dw"] = chip_cols(_unpack(total, layout, 5), conv_dw.shape[-1])[None]
    grads["conv_db"] = chip_cols(_unpack(total, layout, 6), conv_db.shape[-1])
    dmod_sum = _unpack(total, layout, 2).reshape(4, 2, 3 * d)
    dmod_each = _unpack(every, layout, 2).reshape(8, 4, 2, 3 * d)
    grads["ada_b"] = dmod_sum[:, 0] + dmod_sum[:, 1]
    dm = jnp.concatenate([dmod_each[:, :, 0].transpose(1, 0, 2), dmod_sum[:, 1][:, None],
                          jnp.zeros((4, _COND_ROWS - 9, 3 * d), F32)], axis=1)
    dm_cols = chip_cols(dm, n_ada)
    grads["ada_w"] = _ada_w_grad(cond, dm_cols)
    dcond = _cond_grad(dm_cols, ada_w)[8].reshape(8, d // 8)
    dcond = _sum_lead(_all_gather(dcond, ("x", "y"), "gather_cond_grad"), "sum_cond_grad").reshape(d)
    grads["c_ctx"] = dcond * _dsilu(c_ctx)

    outs = [[], [], []]
    for k in _WEIGHTS:
        step = matrix_out[k] if k in matrix_out else _adamw(params[k], grads[k], mom1[k], mom2[k], f"adamw_{k}")
        for lst, val in zip(outs, step):
            lst.append(val)
    loss = lax.psum(res["loss"][0, 0], _AXES)
    return (loss, res["grad_x"][None], *[grads[k].reshape(params[k].shape) for k in _WEIGHTS],
            *outs[0], *outs[1], *outs[2])
```

```python
import functools

import numpy as np
import jax
import jax.numpy as jnp
from jax import lax
from jax.experimental import pallas as pl
from jax.experimental.pallas import tpu as pltpu

F32 = jnp.float32
BF16 = jnp.bfloat16

EPS = 1e-6
GRID_W = 64
HEAD_DIM = 64
WIN_ROWS = 8
WIN_COLS = 16
POOL_WINDOWS = (2, 4, 8, 16)
Q_ROWS = 4
K_ROWS = 12
PAD_ROWS = 4
NEG = -1e30

ADAM_LR = 0.001
ADAM_B1 = 0.9
ADAM_B2 = 0.999
ADAM_EPS = 1e-08
ADAM_WD = 0.01
ADAM_STEP = 10

ROW_BLOCK = 256
VMEM_LIMIT = 56 * 1024 * 1024
ACT = BF16

MESH = pl.DeviceIdType.MESH
HBM_SPEC = pl.BlockSpec(memory_space=pltpu.HBM)


def _cparams(*sem):
    return pltpu.CompilerParams(dimension_semantics=sem or None, vmem_limit_bytes=VMEM_LIMIT)


def _sds(shape, dtype):
    return jax.ShapeDtypeStruct(tuple(shape), dtype)


def _sigmoid(x):
    return 1.0 / (1.0 + jnp.exp(-x))


def _silu(x):
    return x * _sigmoid(x)


def _dsilu(x):
    s = _sigmoid(x)
    return s * (1.0 + x * (1.0 - s))


_DIMS = {
    "nn": (((1,), (0,)), ((), ())),
    "nt": (((1,), (1,)), ((), ())),
    "tn": (((0,), (0,)), ((), ())),
}


def _matmul(a, b, *, mode, grid, a_spec, b_spec, out_shapes, out_specs, name, nk=1,
            a_silu=False, exact=False, epilogue=None, extra=(), extra_specs=(), acc_shape=None):
    n_extra = len(extra)
    n_out = len(out_shapes)

    def body(*refs):
        a_ref, b_ref = refs[:2]
        ex = refs[2:2 + n_extra]
        outs = refs[2 + n_extra:2 + n_extra + n_out]
        av = a_ref[...]
        bv = b_ref[...]
        if a_silu:
            av = _silu(av.astype(F32))
        if exact:
            prod = lax.dot_general(av.astype(F32), bv.astype(F32), _DIMS[mode],
                                   precision=lax.Precision.HIGHEST, preferred_element_type=F32)
        else:
            prod = lax.dot_general(av.astype(BF16), bv.astype(BF16), _DIMS[mode], preferred_element_type=F32)

        def finish(res):
            if epilogue is None:
                outs[0][...] = res.astype(outs[0].dtype)
            elif epilogue == "bias":
                outs[0][...] = (res + ex[0][...]).astype(outs[0].dtype)
            else:
                outs[0][...] = res.astype(outs[0].dtype)
                outs[1][...] = ex[0][...] + ex[1][...] * res

        if nk == 1:
            finish(prod)
        else:
            acc = refs[-1]
            k = pl.program_id(len(grid) - 1)

            @pl.when(k == 0)
            def _():
                acc[...] = prod

            @pl.when(k > 0)
            def _():
                acc[...] += prod

            @pl.when(k == nk - 1)
            def _():
                finish(acc[...])

    scratch = [pltpu.VMEM(acc_shape, F32)] if nk > 1 else []
    sem = ("parallel",) * (len(grid) - 1) + ("arbitrary",)
    return pl.pallas_call(
        body, grid=grid, in_specs=[a_spec, b_spec, *extra_specs], out_specs=list(out_specs),
        out_shape=list(out_shapes), scratch_shapes=scratch, name=name, compiler_params=_cparams(*sem),
    )(a, b, *extra)


def _row_tile(rows):
    for t in (768, 512, 256):
        if rows % t == 0:
            return t
    return rows


def _mm_nn(a, b, name, out_dtype=F32, tn=1024):
    m, k = a.shape
    n = b.shape[1]
    tm = _row_tile(m)
    tn = min(tn, n)
    return _matmul(
        a, b, mode="nn", grid=(m // tm, n // tn),
        a_spec=pl.BlockSpec((tm, k), lambda i, j: (i, 0)), b_spec=pl.BlockSpec((k, tn), lambda i, j: (0, j)),
        out_shapes=[_sds((m, n), out_dtype)], out_specs=[pl.BlockSpec((tm, tn), lambda i, j: (i, j))], name=name)[0]


def _mm_out_resid(a, w_out, xres, gate, nxb, name):
    m, k = a.shape
    n = w_out.shape[1]
    tm = ROW_BLOCK
    seg = lambda i, j: (jnp.where(i >= nxb, 1, 0), 0, 0)
    return _matmul(
        a, w_out, mode="nn", grid=(m // tm, 1),
        a_spec=pl.BlockSpec((tm, k), lambda i, j: (i, 0)), b_spec=pl.BlockSpec((k, n), lambda i, j: (0, 0)),
        extra=(xres, gate), extra_specs=(pl.BlockSpec((tm, n), lambda i, j: (i, 0)), pl.BlockSpec((None, 1, n), seg)),
        out_shapes=[_sds((m, n), ACT), _sds((m, n), F32)],
        out_specs=[pl.BlockSpec((tm, n), lambda i, j: (i, 0))] * 2, epilogue="resid", name=name)


def _mm_nt(a, b, name, out_dtype=F32):
    m, n = a.shape
    k = b.shape[0]
    tm = _row_tile(m)
    return _matmul(
        a, b, mode="nt", grid=(m // tm, 1),
        a_spec=pl.BlockSpec((tm, n), lambda i, j: (i, 0)), b_spec=pl.BlockSpec((k, n), lambda i, j: (0, 0)),
        out_shapes=[_sds((m, k), out_dtype)], out_specs=[pl.BlockSpec((tm, k), lambda i, j: (i, 0))], name=name)[0]


def _mm_nt_parts(a, b, name):
    p, m, kp = a.shape
    d = b.shape[0]
    tm = _row_tile(m)
    return _matmul(
        a, b, mode="nt", grid=(m // tm, p), nk=p, acc_shape=(tm, d),
        a_spec=pl.BlockSpec((None, tm, kp), lambda i, q: (q, i, 0)), b_spec=pl.BlockSpec((d, kp), lambda i, q: (0, q)),
        out_shapes=[_sds((m, d), F32)], out_specs=[pl.BlockSpec((tm, d), lambda i, q: (i, 0))], name=name)[0]


def _mm_tn(a, b, name, out_dtype, tm=512):
    r, m = a.shape
    n = b.shape[1]
    tm = min(tm, m)
    tn = min(1024, n)
    return _matmul(
        a, b, mode="tn", grid=(m // tm, n // tn),
        a_spec=pl.BlockSpec((r, tm), lambda i, j: (0, i)), b_spec=pl.BlockSpec((r, tn), lambda i, j: (0, j)),
        out_shapes=[_sds((m, n), out_dtype)], out_specs=[pl.BlockSpec((tm, tn), lambda i, j: (i, j))], name=name)[0]


def _mm_tn_parts(a, b, name, out_dtype, tm=512):
    r, m = a.shape
    p, _, np_ = b.shape
    tm = min(tm, m)
    return _matmul(
        a, b, mode="tn", grid=(m // tm, p),
        a_spec=pl.BlockSpec((r, tm), lambda i, q: (0, i)), b_spec=pl.BlockSpec((None, r, np_), lambda i, q: (q, 0, 0)),
        out_shapes=[_sds((m, p * np_), out_dtype)], out_specs=[pl.BlockSpec((tm, np_), lambda i, q: (i, q))],
        name=name)[0]


def _seg_map(nxb):
    return lambda i: (jnp.where(i >= nxb, 1, 0), 0, 0)


def _normmod_fwd(x, g, scale, shift, nxb, name):
    rows, d = x.shape
    tr = ROW_BLOCK

    def body(x_ref, g_ref, sc_ref, sh_ref, h_ref, r_ref):
        xv = x_ref[...]
        r = lax.rsqrt(jnp.mean(xv * xv, axis=-1, keepdims=True) + EPS)
        h = (xv * r) * g_ref[...] * (1.0 + sc_ref[...]) + sh_ref[...]
        h_ref[...] = h.astype(BF16)
        r_ref[...] = r

    row = pl.BlockSpec((tr, d), lambda i: (i, 0))
    vec = pl.BlockSpec((None, 1, d), _seg_map(nxb))
    return pl.pallas_call(
        body, grid=(rows // tr,), in_specs=[row, pl.BlockSpec((1, d), lambda i: (0, 0)), vec, vec],
        out_specs=[row, pl.BlockSpec((tr, 1), lambda i: (i, 0))],
        out_shape=[_sds((rows, d), BF16), _sds((rows, 1), F32)], name=name, compiler_params=_cparams("parallel"),
    )(x, g, scale, shift)


def _normmod_bwd(dh, x, r, g, scale, dres, nxb, name):
    rows, d = x.shape
    tr = ROW_BLOCK
    nres = dres.shape[0] // tr
    nseg = scale.shape[0]

    def body(dh_ref, x_ref, r_ref, g_ref, sc_ref, dres_ref, dx_ref, dsh_ref, dge_ref):
        i = pl.program_id(0)
        dhv = dh_ref[...]
        rv = r_ref[...]
        xn = x_ref[...] * rv
        dxn = dhv * (g_ref[...] * (1.0 + sc_ref[...]))
        dx = rv * (dxn - xn * jnp.mean(dxn * xn, axis=-1, keepdims=True))

        @pl.when(i < nres)
        def _():
            dx_ref[...] = dx + dres_ref[...]

        @pl.when(i >= nres)
        def _():
            dx_ref[...] = dx

        first = jnp.logical_or(i == 0, i == nxb)
        s_dh = jnp.sum(dhv, axis=0, keepdims=True)
        s_ge = jnp.sum(dhv * xn, axis=0, keepdims=True)

        @pl.when(first)
        def _():
            dsh_ref[...] = s_dh
            dge_ref[...] = s_ge

        @pl.when(jnp.logical_not(first))
        def _():
            dsh_ref[...] += s_dh
            dge_ref[...] += s_ge

    row = pl.BlockSpec((tr, d), lambda i: (i, 0))
    vec = pl.BlockSpec((None, 1, d), _seg_map(nxb))
    return pl.pallas_call(
        body, grid=(rows // tr,),
        in_specs=[row, row, pl.BlockSpec((tr, 1), lambda i: (i, 0)), pl.BlockSpec((1, d), lambda i: (0, 0)), vec,
                  pl.BlockSpec((tr, d), lambda i: (jnp.minimum(i, nres - 1), 0))],
        out_specs=[row, vec, vec],
        out_shape=[_sds((rows, d), F32), _sds((nseg, 1, d), F32), _sds((nseg, 1, d), F32)],
        name=name, compiler_params=_cparams("arbitrary"),
    )(dh, x, r, g, scale, dres)


def _gate_bwd(dxo, yx, gate, nxb, name):
    rows, d = yx.shape
    tr = ROW_BLOCK
    nseg = gate.shape[0]

    def body(dx_ref, yx_ref, gt_ref, dyx_ref, dg_ref):
        i = pl.program_id(0)
        dxv = dx_ref[...]
        dyx_ref[...] = (dxv * gt_ref[...]).astype(BF16)
        s = jnp.sum(dxv * yx_ref[...].astype(F32), axis=0, keepdims=True)
        first = jnp.logical_or(i == 0, i == nxb)

        @pl.when(first)
        def _():
            dg_ref[...] = s

        @pl.when(jnp.logical_not(first))
        def _():
            dg_ref[...] += s

    row = pl.BlockSpec((tr, d), lambda i: (i, 0))
    vec = pl.BlockSpec((None, 1, d), _seg_map(nxb))
    return pl.pallas_call(
        body, grid=(rows // tr,), in_specs=[row, row, vec], out_specs=[row, vec],
        out_shape=[_sds((rows, d), BF16), _sds((nseg, 1, d), F32)], name=name, compiler_params=_cparams("arbitrary"),
    )(dxo, yx, gate)


def _row_vec(ref, is_ctx):
    return ref[0] if is_ctx is None else jnp.where(is_ctx, ref[1], ref[0])


def _ctx_rows(i, tm, nx, nseg):
    if nseg == 1:
        return None
    return i * tm + lax.broadcasted_iota(jnp.int32, (tm, 1), 0) >= nx


def _seg_sums(ref, val, is_ctx, first):
    if is_ctx is None:
        parts = [jnp.sum(val, axis=0, keepdims=True)]
    else:
        parts = [jnp.sum(jnp.where(is_ctx, 0.0, val), axis=0, keepdims=True),
                 jnp.sum(jnp.where(is_ctx, val, 0.0), axis=0, keepdims=True)]

    @pl.when(first)
    def _():
        for k, p in enumerate(parts):
            ref[k] = p

    @pl.when(jnp.logical_not(first))
    def _():
        for k, p in enumerate(parts):
            ref[k] += p


def _norm_w_in(x, g, scale, shift, w_in, nx, name):
    rows, d = x.shape
    n = w_in.shape[1]
    nseg = scale.shape[0]
    tm = _row_tile(rows)
    tn = min(1024, n)

    def body(x_ref, g_ref, sc_ref, sh_ref, w_ref, h_ref, r_ref, p_ref):
        i, j = pl.program_id(0), pl.program_id(1)

        @pl.when(j == 0)
        def _():
            xv = x_ref[...]
            r = lax.rsqrt(jnp.mean(xv * xv, axis=-1, keepdims=True) + EPS)
            is_ctx = _ctx_rows(i, tm, nx, nseg)
            h = (xv * r) * g_ref[...] * (1.0 + _row_vec(sc_ref, is_ctx)) + _row_vec(sh_ref, is_ctx)
            h_ref[...] = h.astype(BF16)
            r_ref[...] = r

        p_ref[...] = jnp.dot(h_ref[...], w_ref[...], preferred_element_type=F32).astype(ACT)

    vec = pl.BlockSpec((nseg, 1, d), lambda i, j: (0, 0, 0))
    return pl.pallas_call(
        body, grid=(rows // tm, n // tn),
        in_specs=[pl.BlockSpec((tm, d), lambda i, j: (i, 0)), pl.BlockSpec((1, d), lambda i, j: (0, 0)), vec, vec,
                  pl.BlockSpec((d, tn), lambda i, j: (0, j))],
        out_specs=[pl.BlockSpec((tm, d), lambda i, j: (i, 0)), pl.BlockSpec((tm, 1), lambda i, j: (i, 0)),
                   pl.BlockSpec((tm, tn), lambda i, j: (i, j))],
        out_shape=[_sds((rows, d), BF16), _sds((rows, 1), F32), _sds((rows, n), ACT)],
        name=name, compiler_params=_cparams("parallel", "arbitrary"),
    )(x, g, scale, shift, w_in)


def _gate_w_out_bwd(dxo, yx, gate, w_out, nx, name):
    rows, d = yx.shape
    w = w_out.shape[0]
    nseg = gate.shape[0]
    tm = _row_tile(rows)

    def body(dx_ref, yx_ref, gt_ref, w_ref, dyx_ref, da_ref, dg_ref):
        i = pl.program_id(0)
        is_ctx = _ctx_rows(i, tm, nx, nseg)
        dxv = dx_ref[...]
        dyx = (dxv * _row_vec(gt_ref, is_ctx)).astype(BF16)
        dyx_ref[...] = dyx
        da_ref[...] = lax.dot_general(dyx, w_ref[...], _DIMS["nt"], preferred_element_type=F32).astype(ACT)
        _seg_sums(dg_ref, dxv * yx_ref[...].astype(F32), is_ctx, i == 0)

    row = pl.BlockSpec((tm, d), lambda i: (i, 0))
    vec = pl.BlockSpec((nseg, 1, d), lambda i: (0, 0, 0))
    return pl.pallas_call(
        body, grid=(rows // tm,), in_specs=[row, row, vec, pl.BlockSpec((w, d), lambda i: (0, 0))],
        out_specs=[row, pl.BlockSpec((tm, w), lambda i: (i, 0)), vec],
        out_shape=[_sds((rows, d), BF16), _sds((rows, w), ACT), _sds((nseg, 1, d), F32)],
        name=name, compiler_params=_cparams("arbitrary"),
    )(dxo, yx, gate, w_out)


def _w_in_bwd_norm(dparts, w_in, x, r, g, scale, dres, nx, name):
    np_, rows, kp = dparts.shape
    d = w_in.shape[0]
    nseg = scale.shape[0]
    nres = dres.shape[0]
    tm = _row_tile(rows) if nres == rows else ROW_BLOCK
    nres_blocks = nres // tm

    def body(dp_ref, w_ref, x_ref, r_ref, g_ref, sc_ref, dres_ref, dx_ref, dsh_ref, dge_ref, acc):
        i, k = pl.program_id(0), pl.program_id(1)
        prod = lax.dot_general(dp_ref[...], w_ref[...], _DIMS["nt"], preferred_element_type=F32)

        @pl.when(k == 0)
        def _():
            acc[...] = prod

        @pl.when(k > 0)
        def _():
            acc[...] += prod

        @pl.when(k == np_ - 1)
        def _():
            is_ctx = _ctx_rows(i, tm, nx, nseg)
            dhv = acc[...]
            rv = r_ref[...]
            xn = x_ref[...] * rv
            dxn = dhv * (g_ref[...] * (1.0 + _row_vec(sc_ref, is_ctx)))
            dx = rv * (dxn - xn * jnp.mean(dxn * xn, axis=-1, keepdims=True))
            if nres_blocks == rows // tm:
                dx_ref[...] = dx + dres_ref[...]
            else:
                @pl.when(i < nres_blocks)
                def _():
                    dx_ref[...] = dx + dres_ref[...]

                @pl.when(i >= nres_blocks)
                def _():
                    dx_ref[...] = dx
            _seg_sums(dsh_ref, dhv, is_ctx, i == 0)
            _seg_sums(dge_ref, dhv * xn, is_ctx, i == 0)

    row = pl.BlockSpec((tm, d), lambda i, k: (i, 0))
    vec = pl.BlockSpec((nseg, 1, d), lambda i, k: (0, 0, 0))
    return pl.pallas_call(
        body, grid=(rows // tm, np_),
        in_specs=[pl.BlockSpec((None, tm, kp), lambda i, k: (k, i, 0)), pl.BlockSpec((d, kp), lambda i, k: (0, k)),
                  row, pl.BlockSpec((tm, 1), lambda i, k: (i, 0)), pl.BlockSpec((1, d), lambda i, k: (0, 0)), vec,
                  pl.BlockSpec((tm, d), lambda i, k: (jnp.minimum(i, nres_blocks - 1), 0))],
        out_specs=[row, vec, vec],
        out_shape=[_sds((rows, d), F32), _sds((nseg, 1, d), F32), _sds((nseg, 1, d), F32)],
        scratch_shapes=[pltpu.VMEM((tm, d), F32)], name=name, compiler_params=_cparams("arbitrary", "arbitrary"),
    )(dparts, w_in, x, r, g, scale, dres)


_PAD_TOP = 16
_PAD_BOT = 32


def _window_sum(buf, xv, lo, n):
    t = xv.shape[0]
    c = xv.shape[1]
    tp = t + _PAD_TOP + _PAD_BOT
    buf[pl.ds(0, _PAD_TOP), :] = jnp.zeros((_PAD_TOP, c), F32)
    buf[pl.ds(_PAD_TOP, t), :] = xv
    buf[pl.ds(_PAD_TOP + t, _PAD_BOT), :] = jnp.zeros((_PAD_BOT, c), F32)
    p = buf[...]
    k = 1
    while k < n:
        p = p + pltpu.roll(p, tp - k, 0)
        k *= 2
    if lo:
        p = pltpu.roll(p, -lo, 0)
    buf[...] = p
    return buf[pl.ds(_PAD_TOP, t), :]


def _window_count(t, half):
    pos = lax.broadcasted_iota(jnp.int32, (t, 1), 0)
    return (jnp.minimum(pos + half, t) - jnp.maximum(pos - half, 0)).astype(F32)


def _segments(rows, nx):
    return [(0, nx)] + ([(nx, rows - nx)] if rows > nx else [])


def _pool_fwd(uv, nx, name):
    rows = uv.shape[0]
    w = uv.shape[1] // 2
    cb = 128
    per_group = w // len(POOL_WINDOWS) // cb
    segs = _segments(rows, nx)

    def body(u_ref, z_ref, *bufs):
        j = pl.program_id(0)
        for gi, win in enumerate(POOL_WINDOWS):
            half = win // 2

            @pl.when(jnp.logical_and(j >= gi * per_group, j < (gi + 1) * per_group))
            def _():
                for (start, length), buf in zip(segs, bufs):
                    uvv = u_ref[pl.ds(start, length), :].astype(F32)
                    s = _window_sum(buf, uvv, -half, win)
                    z_ref[pl.ds(start, length), :] = (s / _window_count(length, half) - uvv).astype(BF16)

    scratch = [pltpu.VMEM((length + _PAD_TOP + _PAD_BOT, cb), F32) for _, length in segs]
    return pl.pallas_call(
        body, grid=(w // cb,), in_specs=[pl.BlockSpec((rows, cb), lambda j: (0, j))],
        out_specs=pl.BlockSpec((rows, cb), lambda j: (0, j)), out_shape=_sds((rows, w), BF16),
        scratch_shapes=scratch, name=name, compiler_params=_cparams("parallel"),
    )(uv)


def _pool_bwd(dz, dgt, nx, name):
    rows, w = dz.shape
    cb = 128
    per_group = w // len(POOL_WINDOWS) // cb
    segs = _segments(rows, nx)

    def body(dz_ref, dgt_ref, o_ref, *bufs):
        j = pl.program_id(0)
        o_ref[1] = dgt_ref[...]
        for gi, win in enumerate(POOL_WINDOWS):
            half = win // 2

            @pl.when(jnp.logical_and(j >= gi * per_group, j < (gi + 1) * per_group))
            def _():
                for (start, length), buf in zip(segs, bufs):
                    dzv = dz_ref[pl.ds(start, length), :].astype(F32)
                    s = _window_sum(buf, dzv / _window_count(length, half), 1 - half, win)
                    o_ref[0, pl.ds(start, length), :] = (s - dzv).astype(BF16)

    scratch = [pltpu.VMEM((length + _PAD_TOP + _PAD_BOT, cb), F32) for _, length in segs]
    col = pl.BlockSpec((rows, cb), lambda j: (0, j))
    return pl.pallas_call(
        body, grid=(w // cb,), in_specs=[col, col], out_specs=pl.BlockSpec((2, rows, cb), lambda j: (0, 0, j)),
        out_shape=_sds((2, rows, w), BF16), scratch_shapes=scratch, name=name, compiler_params=_cparams("parallel"),
    )(dz, dgt)


def _grp_fwd(z, w_grp, uv, scale, name):
    rows, w = z.shape
    ng, gc, _ = w_grp.shape
    tm = _row_tile(rows)

    def body(z_ref, w_ref, gt_ref, sc_ref, mx_ref, a_ref):
        mixed = jnp.dot(z_ref[...], w_ref[...], preferred_element_type=F32)
        mx_ref[...] = mixed.astype(ACT)
        a_ref[...] = (mixed * sc_ref[...] * _silu(gt_ref[...].astype(F32))).astype(BF16)

    blk = pl.BlockSpec((tm, gc), lambda g, i: (i, g))
    return pl.pallas_call(
        body, grid=(ng, rows // tm),
        in_specs=[blk, pl.BlockSpec((None, gc, gc), lambda g, i: (g, 0, 0)),
                  pl.BlockSpec((tm, gc), lambda g, i: (i, ng + g)), pl.BlockSpec((1, gc), lambda g, i: (0, g))],
        out_specs=[blk, blk], out_shape=[_sds((rows, w), ACT), _sds((rows, w), BF16)],
        name=name, compiler_params=_cparams("parallel", "parallel"),
    )(z, w_grp, uv, scale)


def _grp_bwd(da, mixed, uv, scale, w_grp, name):
    rows, w = da.shape
    ng, gc, _ = w_grp.shape
    tm = _row_tile(rows)

    def body(da_ref, mx_ref, gt_ref, sc_ref, w_ref, dm_ref, dz_ref, dgt_ref, dsc_ref):
        i = pl.program_id(1)
        dav = da_ref[...].astype(F32)
        mixed = mx_ref[...].astype(F32)
        gt = gt_ref[...].astype(F32)
        sg = _silu(gt)
        sc = sc_ref[...]
        dm = (dav * sc * sg).astype(BF16)
        dm_ref[...] = dm
        dz_ref[...] = lax.dot_general(dm, w_ref[...], _DIMS["nt"], preferred_element_type=F32).astype(ACT)
        dgt_ref[...] = (dav * mixed * sc * _dsilu(gt)).astype(BF16)
        s = jnp.sum(dav * mixed * sg, axis=0, keepdims=True)

        @pl.when(i == 0)
        def _():
            dsc_ref[...] = s

        @pl.when(i > 0)
        def _():
            dsc_ref[...] += s

    blk = pl.BlockSpec((tm, gc), lambda g, i: (i, g))
    vec = pl.BlockSpec((1, gc), lambda g, i: (0, g))
    return pl.pallas_call(
        body, grid=(ng, rows // tm),
        in_specs=[blk, blk, pl.BlockSpec((tm, gc), lambda g, i: (i, ng + g)), vec,
                  pl.BlockSpec((None, gc, gc), lambda g, i: (g, 0, 0))],
        out_specs=[blk, blk, blk, vec],
        out_shape=[_sds((rows, w), BF16), _sds((rows, w), ACT), _sds((rows, w), BF16), _sds((1, w), F32)],
        name=name, compiler_params=_cparams("parallel", "arbitrary"),
    )(da, mixed, uv, scale, w_grp)


def _grp_wgrad(z, dm, ng, name, out_dtype):
    rows, w = z.shape
    gc = w // ng

    def body(z_ref, dm_ref, o_ref):
        o_ref[...] = lax.dot_general(z_ref[...], dm_ref[...], _DIMS["tn"],
                                     preferred_element_type=F32).astype(o_ref.dtype)

    blk = pl.BlockSpec((rows, gc), lambda g: (0, g))
    return pl.pallas_call(
        body, grid=(ng,), in_specs=[blk, blk], out_specs=pl.BlockSpec((None, gc, gc), lambda g: (g, 0, 0)),
        out_shape=_sds((ng, gc, gc), out_dtype), name=name, compiler_params=_cparams("parallel"),
    )(z, dm)


def _shift_rows(v, by):
    t = v.shape[0]
    pos = lax.broadcasted_iota(jnp.int32, v.shape, 0)
    rolled = pltpu.roll(v, by % t, 0)
    keep = pos >= by if by > 0 else pos < t + by
    return jnp.where(keep, rolled, 0.0)


def _conv_specs(t, w, cb):
    return [pl.BlockSpec((t, cb), (lambda j, q=q: (0, q * (w // cb) + j))) for q in range(4)]


def _conv_fwd(p4, dw, db, name):
    t = p4.shape[0]
    w = p4.shape[1] // 4
    cb = 128

    def body(bg_ref, cg_ref, v_ref, g_ref, dw_ref, db_ref, a_ref):
        tv = cg_ref[...].astype(F32) * v_ref[...].astype(F32)
        conv = (dw_ref[0:1, :] * _shift_rows(tv, 1) + dw_ref[1:2, :] * tv + dw_ref[2:3, :] * _shift_rows(tv, -1)
                + db_ref[...])
        a_ref[...] = (bg_ref[...].astype(F32) * conv * _silu(g_ref[...].astype(F32))).astype(BF16)

    return pl.pallas_call(
        body, grid=(w // cb,),
        in_specs=_conv_specs(t, w, cb) + [pl.BlockSpec((3, cb), lambda j: (0, j)), pl.BlockSpec((1, cb), lambda j: (0, j))],
        out_specs=pl.BlockSpec((t, cb), lambda j: (0, j)), out_shape=_sds((t, w), BF16),
        name=name, compiler_params=_cparams("parallel"),
    )(p4, p4, p4, p4, dw, db)


def _conv_bwd(da, p4, dw, db, name):
    t, w = da.shape
    cb = 128

    def body(da_ref, bg_ref, cg_ref, v_ref, g_ref, dw_ref, db_ref, d4_ref, ddw_ref, ddb_ref):
        cg = cg_ref[...].astype(F32)
        vv = v_ref[...].astype(F32)
        bg = bg_ref[...].astype(F32)
        gv = g_ref[...].astype(F32)
        tv = cg * vv
        tm1 = _shift_rows(tv, 1)
        tp1 = _shift_rows(tv, -1)
        w0, w1, w2 = dw_ref[0:1, :], dw_ref[1:2, :], dw_ref[2:3, :]
        conv = w0 * tm1 + w1 * tv + w2 * tp1 + db_ref[...]
        y = bg * conv
        dav = da_ref[...].astype(F32)
        dy = dav * _silu(gv)
        d4_ref[3] = (dav * y * _dsilu(gv)).astype(BF16)
        d4_ref[0] = (dy * conv).astype(BF16)
        dconv = dy * bg
        ddb_ref[...] = jnp.sum(dconv, axis=0, keepdims=True)
        ddw_ref[0:1, :] = jnp.sum(dconv * tm1, axis=0, keepdims=True)
        ddw_ref[1:2, :] = jnp.sum(dconv * tv, axis=0, keepdims=True)
        ddw_ref[2:3, :] = jnp.sum(dconv * tp1, axis=0, keepdims=True)
        dt = w0 * _shift_rows(dconv, -1) + w1 * dconv + w2 * _shift_rows(dconv, 1)
        d4_ref[1] = (dt * vv).astype(BF16)
        d4_ref[2] = (dt * cg).astype(BF16)

    col = pl.BlockSpec((t, cb), lambda j: (0, j))
    tap = pl.BlockSpec((3, cb), lambda j: (0, j))
    bias = pl.BlockSpec((1, cb), lambda j: (0, j))
    return pl.pallas_call(
        body, grid=(w // cb,), in_specs=[col] + _conv_specs(t, w, cb) + [tap, bias],
        out_specs=[pl.BlockSpec((4, t, cb), lambda j: (0, 0, j)), tap, bias],
        out_shape=[_sds((4, t, w), BF16), _sds((3, w), F32), _sds((1, w), F32)],
        name=name, compiler_params=_cparams("parallel"),
    )(da, p4, p4, p4, p4, dw, db)


def _attn_mask():
    qn, kn = Q_ROWS * GRID_W, K_ROWS * GRID_W
    qr, qc = np.divmod(np.arange(qn), GRID_W)
    kr, kc = np.divmod(np.arange(kn), GRID_W)
    col0 = np.clip(qc - WIN_COLS // 2, 0, GRID_W - WIN_COLS)
    col_ok = (kc[None, :] >= col0[:, None]) & (kc[None, :] < col0[:, None] + WIN_COLS)
    first = np.zeros(qn, np.int64)
    last = np.full(qn, K_ROWS - WIN_ROWS)
    out = []
    for row0 in (first, qr, last):
        row_ok = (kr[None, :] >= row0[:, None]) & (kr[None, :] < row0[:, None] + WIN_ROWS)
        out.append(np.where(row_ok & col_ok, 0.0, NEG))
    return jnp.asarray(np.stack(out), F32)


def _rpb_onehot():
    qc, kc = np.divmod(np.arange(GRID_W * GRID_W), GRID_W)
    e = (kc - qc + WIN_COLS - 1)[None, :] == np.arange(128)[:, None]
    return jnp.asarray(e, F32)


_KW = K_ROWS * GRID_W
_QB = Q_ROWS * GRID_W
_T3_FRONT = 4
_T3_LANES = 1536
_PAIR = 2 * HEAD_DIM
_BIAS_BASE = (WIN_ROWS - 1 + _T3_FRONT, WIN_ROWS // 2 - 1 + _T3_FRONT, _T3_FRONT - 1)


class _Comm:
    def __init__(self, ins, outs, sems, start, finish):
        self.ins, self.outs, self.sems, self.start, self.finish = list(ins), list(outs), list(sems), start, finish


def _bias_pieces(cls):
    out = []
    for qr in range(Q_ROWS):
        off = (_BIAS_BASE[cls] - qr) * GRID_W
        out.append((qr, off % 128 != 0, off - (off % 128)))
    return out


def _block_class(b, nblk, fn, entering=False):
    interior = (b == 1) if entering else jnp.logical_and(b > 0, b < nblk - 1)
    for cls, cond in enumerate((b == 0, interior, b == nblk - 1)):
        pl.when(cond)(functools.partial(fn, cls))


def _attn_geometry(p4, nx):
    rows = p4.shape[0]
    w = p4.shape[1] // 4
    nhp = w // _PAIR
    nblk = nx // _QB
    qspec = lambda col: pl.BlockSpec((_QB, _PAIR), lambda hp, b: (b, col * nhp + hp))
    kspec = lambda col: pl.BlockSpec((rows, _PAIR), lambda hp, b: (0, col * nhp + hp))
    tspec = pl.BlockSpec((2, GRID_W, _T3_LANES), lambda hp, b: (hp, 0, 0))
    mspec = pl.BlockSpec((None, _QB, _KW), lambda hp, b: (jnp.where(b == 0, 0, jnp.where(b == nblk - 1, 2, 1)), 0, 0))
    lspec = pl.BlockSpec((None, _QB, 2), lambda hp, b: (hp, b, 0))
    ospec = pl.BlockSpec((_QB, _PAIR), lambda hp, b: (b, hp))
    return rows, w, nhp, nblk, qspec, kspec, tspec, mspec, lspec, ospec


def _window_start(b, nx):
    return pl.multiple_of(jnp.clip(b * _QB - PAD_ROWS * GRID_W, 0, nx - _KW), _QB)


def _load_bias(bias_ref, t3_ref, t3s_ref, m_ref, b, nblk):
    def fill(cls):
        for h in range(2):
            for qr, shifted, off in _bias_pieces(cls):
                src = t3s_ref if shifted else t3_ref
                rows = slice(qr * GRID_W, (qr + 1) * GRID_W)
                bias_ref[h, rows, :] = src[h, :, off:off + _KW] + m_ref[rows, :]

    _block_class(b, nblk, fill, entering=True)


def _attn_fwd(p4, t3, t3s, mask, nx, name, comm=None):
    rows, w, nhp, nblk, qspec, kspec, tspec, mspec, lspec, ospec = _attn_geometry(p4, nx)
    n_ctx = rows - nx
    n_cin, n_cout = (len(comm.ins), len(comm.outs)) if comm else (0, 0)

    def body(*refs):
        q_ref, k_ref, v_ref, g_ref, t3_ref, t3s_ref, m_ref = refs[:7]
        cin = refs[7:7 + n_cin]
        a_ref, o_ref, lse_ref = refs[7 + n_cin:10 + n_cin]
        cout = refs[10 + n_cin:10 + n_cin + n_cout]
        bias_ref = refs[10 + n_cin + n_cout]
        sems = refs[11 + n_cin + n_cout:]
        hp, b = pl.program_id(0), pl.program_id(1)
        if comm:
            pl.when(jnp.logical_and(hp == 0, b == 0))(lambda: comm.start(cin, cout, sems))
        start = _window_start(b, nx)
        _load_bias(bias_ref, t3_ref, t3s_ref, m_ref, b, nblk)
        qf = q_ref[...].astype(F32) * HEAD_DIM ** -0.5
        kw = k_ref[pl.ds(start, _KW), :].astype(BF16)
        vw = v_ref[pl.ds(start, _KW), :].astype(BF16)
        kcv = k_ref[pl.ds(nx, n_ctx), :].astype(BF16)
        vcv = v_ref[pl.ds(nx, n_ctx), :].astype(BF16)
        lane = lax.broadcasted_iota(jnp.int32, (1, _PAIR), 1)
        outs, lses = [], []
        for h in range(2):
            mine = (lane >= HEAD_DIM) if h else (lane < HEAD_DIM)
            qm = jnp.where(mine, qf, 0.0).astype(BF16)
            s_loc = lax.dot_general(qm, kw, _DIMS["nt"], preferred_element_type=F32) + bias_ref[h]
            s_ctx = lax.dot_general(qm, kcv, _DIMS["nt"], preferred_element_type=F32)
            mx = jnp.maximum(jnp.max(s_loc, axis=-1, keepdims=True), jnp.max(s_ctx, axis=-1, keepdims=True))
            p_loc = jnp.exp(s_loc - mx)
            p_ctx = jnp.exp(s_ctx - mx)
            den = jnp.sum(p_loc, axis=-1, keepdims=True) + jnp.sum(p_ctx, axis=-1, keepdims=True)
            o = jnp.dot(p_loc.astype(BF16), vw, preferred_element_type=F32)
            o = o + jnp.dot(p_ctx.astype(BF16), vcv, preferred_element_type=F32)
            outs.append(o * (1.0 / den))
            lses.append(mx + jnp.log(den))
        o = jnp.where(lane < HEAD_DIM, outs[0], outs[1])
        o_ref[...] = o.astype(ACT)
        a_ref[...] = (o * _silu(g_ref[...].astype(F32))).astype(BF16)
        col = lax.broadcasted_iota(jnp.int32, (1, 2), 1)
        lse_ref[...] = jnp.where(col == 0, lses[0], lses[1])
        if comm:
            pl.when(jnp.logical_and(hp == nhp - 1, b == nblk - 1))(lambda: comm.finish(cin, cout, sems))

    res = pl.pallas_call(
        body, grid=(nhp, nblk),
        in_specs=[qspec(0), kspec(1), kspec(2), qspec(3), tspec, tspec, mspec] + [HBM_SPEC] * n_cin,
        out_specs=[ospec, ospec, lspec] + [HBM_SPEC] * n_cout,
        out_shape=[_sds((nx, w), BF16), _sds((nx, w), ACT), _sds((nhp, nx, 2), F32)] + (comm.outs if comm else []),
        scratch_shapes=[pltpu.VMEM((2, _QB, _KW), F32)] + (comm.sems if comm else []),
        name=name, compiler_params=_cparams("arbitrary", "arbitrary"),
    )(p4, p4, p4, p4, t3, t3s, mask, *(comm.ins if comm else []))
    return res[:3], res[3:]


def _attn_bwd(p4, t3, t3s, mask, o, lse, da, nx, name, comm=None):
    rows, w, nhp, nblk, qspec, kspec, tspec, mspec, lspec, ospec = _attn_geometry(p4, nx)
    n_ctx = rows - nx
    n_cin, n_cout = (len(comm.ins), len(comm.outs)) if comm else (0, 0)

    def body(*refs):
        q_ref, k_ref, v_ref, g_ref, t3_ref, t3s_ref, m_ref, o_ref, lse_ref, da_ref = refs[:10]
        cin = refs[10:10 + n_cin]
        d4_ref, dt3_ref, dt3s_ref = refs[10 + n_cin:13 + n_cin]
        cout = refs[13 + n_cin:13 + n_cin + n_cout]
        bias_ref, ds_ref, dk_ref, dv_ref = refs[13 + n_cin + n_cout:17 + n_cin + n_cout]
        sems = refs[17 + n_cin + n_cout:]
        hp, b = pl.program_id(0), pl.program_id(1)
        if comm:
            pl.when(jnp.logical_and(hp == 0, b == 0))(lambda: comm.start(cin, cout, sems))
        start = _window_start(b, nx)
        here = pl.multiple_of(b * _QB, _QB)

        @pl.when(b == 0)
        def _():
            dk_ref[...] = jnp.zeros(dk_ref.shape, F32)
            dv_ref[...] = jnp.zeros(dv_ref.shape, F32)
            dt3_ref[...] = jnp.zeros(dt3_ref.shape, F32)
            dt3s_ref[...] = jnp.zeros(dt3s_ref.shape, F32)
            d4_ref[0, pl.ds(nx, n_ctx), :] = jnp.zeros((n_ctx, _PAIR), BF16)
            d4_ref[3, pl.ds(nx, n_ctx), :] = jnp.zeros((n_ctx, _PAIR), BF16)

        _load_bias(bias_ref, t3_ref, t3s_ref, m_ref, b, nblk)
        gv = g_ref[...].astype(F32)
        dav = da_ref[...].astype(F32)
        ov = o_ref[...].astype(F32)
        dov = dav * _silu(gv)
        d4_ref[3, pl.ds(here, _QB), :] = (dav * ov * _dsilu(gv)).astype(BF16)
        qf = q_ref[...].astype(F32) * HEAD_DIM ** -0.5
        kw = k_ref[pl.ds(start, _KW), :].astype(BF16)
        vw = v_ref[pl.ds(start, _KW), :].astype(BF16)
        kcv = k_ref[pl.ds(nx, n_ctx), :].astype(BF16)
        vcv = v_ref[pl.ds(nx, n_ctx), :].astype(BF16)
        lane = lax.broadcasted_iota(jnp.int32, (1, _PAIR), 1)
        dq = jnp.zeros((_QB, _PAIR), F32)
        for h in range(2):
            mine = (lane >= HEAD_DIM) if h else (lane < HEAD_DIM)
            qm = jnp.where(mine, qf, 0.0).astype(BF16)
            dom = jnp.where(mine, dov, 0.0)
            dob = dom.astype(BF16)
            lse = lse_ref[:, h:h + 1]
            s_loc = lax.dot_general(qm, kw, _DIMS["nt"], preferred_element_type=F32)
            p_loc = jnp.exp(s_loc + bias_ref[h] - lse)
            p_ctx = jnp.exp(lax.dot_general(qm, kcv, _DIMS["nt"], preferred_element_type=F32) - lse)
            delta = jnp.sum(dom * ov, axis=-1, keepdims=True)
            ds_loc = p_loc * (lax.dot_general(dob, vw, _DIMS["nt"], preferred_element_type=F32) - delta)
            ds_ctx = p_ctx * (lax.dot_general(dob, vcv, _DIMS["nt"], preferred_element_type=F32) - delta)
            dsb_loc = ds_loc.astype(BF16)
            dsb_ctx = ds_ctx.astype(BF16)
            dq_h = (jnp.dot(dsb_loc, kw, preferred_element_type=F32)
                    + jnp.dot(dsb_ctx, kcv, preferred_element_type=F32))
            dq = dq + jnp.where(mine, dq_h, 0.0)
            dk_ref[pl.ds(start, _KW), :] += lax.dot_general(dsb_loc, qm, _DIMS["tn"], preferred_element_type=F32)
            dv_ref[pl.ds(start, _KW), :] += lax.dot_general(p_loc.astype(BF16), dob, _DIMS["tn"],
                                                            preferred_element_type=F32)
            dk_ref[pl.ds(nx, n_ctx), :] += lax.dot_general(dsb_ctx, qm, _DIMS["tn"], preferred_element_type=F32)
            dv_ref[pl.ds(nx, n_ctx), :] += lax.dot_general(p_ctx.astype(BF16), dob, _DIMS["tn"],
                                                           preferred_element_type=F32)
            ds_ref[h] = ds_loc
        d4_ref[0, pl.ds(here, _QB), :] = (dq * HEAD_DIM ** -0.5).astype(BF16)

        def scatter(cls):
            for h in range(2):
                for qr, shifted, off in _bias_pieces(cls):
                    dst = dt3s_ref if shifted else dt3_ref
                    dst[h, :, off:off + _KW] += ds_ref[h, qr * GRID_W:(qr + 1) * GRID_W, :]

        _block_class(b, nblk, scatter)

        @pl.when(b == nblk - 1)
        def _():
            d4_ref[1] = dk_ref[...].astype(BF16)
            d4_ref[2] = dv_ref[...].astype(BF16)

        if comm:
            pl.when(jnp.logical_and(hp == nhp - 1, b == nblk - 1))(lambda: comm.finish(cin, cout, sems))

    tshape = _sds(t3.shape, F32)
    res = pl.pallas_call(
        body, grid=(nhp, nblk),
        in_specs=[qspec(0), kspec(1), kspec(2), qspec(3), tspec, tspec, mspec, ospec, lspec, ospec] + [HBM_SPEC] * n_cin,
        out_specs=[pl.BlockSpec((4, rows, _PAIR), lambda hp, b: (0, 0, hp)), tspec, tspec] + [HBM_SPEC] * n_cout,
        out_shape=[_sds((4, rows, w), BF16), tshape, tshape] + (comm.outs if comm else []),
        scratch_shapes=[pltpu.VMEM((2, _QB, _KW), F32), pltpu.VMEM((2, _QB, _KW), F32),
                        pltpu.VMEM((rows, _PAIR), F32), pltpu.VMEM((rows, _PAIR), F32)] + (comm.sems if comm else []),
        name=name, compiler_params=_cparams("arbitrary", "arbitrary"),
    )(p4, p4, p4, p4, t3, t3s, mask, o, lse, da, *(comm.ins if comm else []))
    return res[:3], res[3:]


def _final(x, g, target, name):
    rows, d = x.shape
    tr = ROW_BLOCK
    nblk = rows // tr

    def body(x_ref, g_ref, t_ref, loss_ref, dx_ref, dg_ref, acc_ref):
        i = pl.program_id(0)
        xv = x_ref[...]
        gv = g_ref[...]
        r = lax.rsqrt(jnp.mean(xv * xv, axis=-1, keepdims=True) + EPS)
        xn = xv * r
        err = xn * gv - t_ref[...]
        dy = err * (1.0 / d)
        dxn = dy * gv
        dx_ref[...] = r * (dxn - xn * jnp.mean(dxn * xn, axis=-1, keepdims=True))
        s_g = jnp.sum(dy * xn, axis=0, keepdims=True)
        s_l = jnp.sum(jnp.mean(err * err, axis=-1, keepdims=True), axis=0, keepdims=True)

        @pl.when(i == 0)
        def _():
            dg_ref[...] = s_g
            acc_ref[...] = s_l

        @pl.when(i > 0)
        def _():
            dg_ref[...] += s_g
            acc_ref[...] += s_l

        @pl.when(i == nblk - 1)
        def _():
            loss_ref[...] = jnp.broadcast_to(0.5 * acc_ref[...], loss_ref.shape)

    row = pl.BlockSpec((tr, d), lambda i: (i, 0))
    vec = pl.BlockSpec((1, d), lambda i: (0, 0))
    return pl.pallas_call(
        body, grid=(nblk,), in_specs=[row, vec, row],
        out_specs=[pl.BlockSpec((1, 128), lambda i: (0, 0)), row, vec],
        out_shape=[_sds((1, 128), F32), _sds((rows, d), F32), _sds((1, d), F32)],
        scratch_shapes=[pltpu.VMEM((1, 1), F32)], name=name, compiler_params=_cparams("arbitrary"),
    )(x, g, target)


def _as2d(a):
    if a.ndim == 1:
        return a.reshape(-1, 128) if a.shape[0] % 128 == 0 else a.reshape(1, -1)
    return a.reshape(-1, a.shape[-1])


def _adamw(w, g, m, v, name):
    shape = w.shape
    w2, g2, m2, v2 = (_as2d(t) for t in (w, g.reshape(shape), m, v))
    rows, cols = w2.shape
    tr = 512 if rows % 512 == 0 else rows
    c1 = 1.0 - ADAM_B1 ** ADAM_STEP
    c2 = 1.0 - ADAM_B2 ** ADAM_STEP

    def body(w_ref, g_ref, m_ref, v_ref, d_ref, nm_ref, nv_ref):
        gv = g_ref[...]
        nm = ADAM_B1 * m_ref[...] + (1.0 - ADAM_B1) * gv
        nv = ADAM_B2 * v_ref[...] + (1.0 - ADAM_B2) * (gv * gv)
        nm_ref[...] = nm
        nv_ref[...] = nv
        d_ref[...] = -ADAM_LR * ((nm / c1) / (jnp.sqrt(nv / c2) + ADAM_EPS) + ADAM_WD * w_ref[...])

    blk = pl.BlockSpec((tr, cols), lambda i: (i, 0))
    outs = pl.pallas_call(
        body, grid=(rows // tr,), in_specs=[blk] * 4, out_specs=[blk] * 3,
        out_shape=[_sds((rows, cols), F32)] * 3, name=name, compiler_params=_cparams("parallel"),
    )(w2, g2, m2, v2)
    return tuple(t.reshape(shape) for t in outs)


def _sum_lead(x, name, out_dtype=F32):
    n, rows, cols = x.shape
    tr = 512 if rows % 512 == 0 else rows

    def body(x_ref, o_ref):
        acc = x_ref[0].astype(F32)
        for k in range(1, n):
            acc = acc + x_ref[k].astype(F32)
        o_ref[...] = acc.astype(out_dtype)

    return pl.pallas_call(
        body, grid=(rows // tr,), in_specs=[pl.BlockSpec((n, tr, cols), lambda i: (0, i, 0))],
        out_specs=pl.BlockSpec((tr, cols), lambda i: (i, 0)), out_shape=_sds((rows, cols), out_dtype),
        name=name, compiler_params=_cparams("parallel"),
    )(x)


_NO_CTX = 1 << 30


def _seg_vecs(mod_l, which, nseg):
    return mod_l[:nseg, which][:, None, :]


def _norm_grads(dshift, dgeff, dgate, g, scale):
    nseg, _, d = dshift.shape
    dmod = jnp.stack([dshift[:, 0], dgeff[:, 0] * g, dgate[:, 0]], axis=1)
    if nseg == 1:
        dmod = jnp.concatenate([dmod, jnp.zeros((1, 3, d), F32)], axis=0)
    dg = jnp.sum(dgeff[:, 0] * (1.0 + scale[:, 0]), axis=0)
    return dmod, dg


def _pool_layer(xin, g, mod_l, w_in, w_grp, w_out, pscale, nx, tag):
    rows = xin.shape[0]
    nseg = 2 if rows > nx else 1
    nxb = nx // ROW_BLOCK if nseg == 2 else _NO_CTX
    shift, scale, gate = (_seg_vecs(mod_l, k, nseg) for k in range(3))
    h, r, uv = _norm_w_in(xin, g, scale, shift, w_in, nx, f"w_in_fwd_{tag}")
    z = _pool_fwd(uv, nx, f"pool_fwd_{tag}")
    mixed, a = _grp_fwd(z, w_grp, uv, pscale, f"grp_fwd_{tag}")
    yx, xout = _mm_out_resid(a, w_out, xin, gate, nxb, f"w_out_fwd_{tag}")

    def backward(dxo):
        dyx, da, dgate = _gate_w_out_bwd(dxo, yx, gate, w_out, nx, f"w_out_bwd_{tag}")
        gw_out = _mm_tn(a, dyx, f"w_out_grad_{tag}", BF16)
        dm, dz, dgt, dscale = _grp_bwd(da, mixed, uv, pscale, w_grp, f"grp_bwd_{tag}")
        gw_grp = _grp_wgrad(z, dm, w_grp.shape[0], f"grp_grad_{tag}", BF16)
        duv = _pool_bwd(dz, dgt, nx, f"pool_bwd_{tag}")
        gw_in = _mm_tn_parts(h, duv, f"w_in_grad_{tag}", BF16)
        dx, dshift, dgeff = _w_in_bwd_norm(duv, w_in, xin, r, g, scale, dxo, nx, f"w_in_bwd_{tag}")
        dmod, dg = _norm_grads(dshift, dgeff, dgate, g[0], scale)
        return dx, dmod, dg, dict(w_in=gw_in, w_grp=gw_grp, w_out=gw_out, scale=dscale)

    return xout, backward


def _rpb_tables(rpb, onehot):
    nh, na, nb = rpb.shape
    flat = jnp.pad(rpb.reshape(nh * na, nb), ((0, 0), (0, 128 - nb)))
    t1 = _matmul(
        flat, onehot, mode="nn", grid=(1, 4), exact=True,
        a_spec=pl.BlockSpec((nh * na, 128), lambda i, j: (0, 0)), b_spec=pl.BlockSpec((128, 1024), lambda i, j: (0, j)),
        out_shapes=[_sds((nh * na, GRID_W * GRID_W), F32)], out_specs=[pl.BlockSpec((nh * na, 1024), lambda i, j: (0, j))],
        name="rpb_table")[0]
    t3 = t1.reshape(nh, na, GRID_W, GRID_W).transpose(0, 2, 1, 3).reshape(nh, GRID_W, na * GRID_W)
    front = _T3_FRONT * GRID_W
    back = _T3_LANES - na * GRID_W - front
    return (jnp.pad(t3, ((0, 0), (0, 0), (front, back))),
            jnp.pad(t3, ((0, 0), (0, 0), (front - GRID_W, back + GRID_W))))


def _rpb_grad(dt3, dt3s, onehot, nh, na, nb):
    def fold(t, front, name):
        flat = t[:, :, front:front + na * GRID_W].reshape(nh, GRID_W, na, GRID_W).transpose(0, 2, 1, 3)
        flat = flat.reshape(nh * na, -1)
        out = _matmul(
            flat, onehot, mode="nt", grid=(1, 4), nk=4, acc_shape=(nh * na, 128), exact=True,
            a_spec=pl.BlockSpec((nh * na, 1024), lambda i, k: (0, k)), b_spec=pl.BlockSpec((128, 1024), lambda i, k: (0, k)),
            out_shapes=[_sds((nh * na, 128), F32)], out_specs=[pl.BlockSpec((nh * na, 128), lambda i, k: (0, 0))],
            name=name)[0]
        return out[:, :nb].reshape(nh, na, nb)

    front = _T3_FRONT * GRID_W
    return fold(dt3, front, "rpb_grad_a") + fold(dt3s, front - GRID_W, "rpb_grad_b")


def _na_layer(xc, g, mod_l, w_in, rpb, w_out, nx, consts, comm=None):
    mask, onehot = consts
    shift, scale = _seg_vecs(mod_l, 0, 2), _seg_vecs(mod_l, 1, 2)
    gate = _seg_vecs(mod_l, 2, 1)
    h, r, p4 = _norm_w_in(xc, g, scale, shift, w_in, nx, "w_in_fwd_na")
    t3, t3s = _rpb_tables(rpb, onehot)
    (a, o, lse), carried = _attn_fwd(p4, t3, t3s, mask, nx, "attn_fwd", comm)
    yx, xout = _mm_out_resid(a, w_out, xc, gate, _NO_CTX, "w_out_fwd_na")

    def backward(dxo, comm=None):
        dyx, da, dgate = _gate_w_out_bwd(dxo, yx, gate, w_out, nx, "w_out_bwd_na")
        gw_out = _mm_tn(a, dyx, "w_out_grad_na", BF16)
        (d4, dt3, dt3s), carried_bwd = _attn_bwd(p4, t3, t3s, mask, o, lse, da, nx, "attn_bwd", comm)
        gw_in = _mm_tn_parts(h, d4, "w_in_grad_na", BF16)
        dx, dshift, dgeff = _w_in_bwd_norm(d4, w_in, xc, r, g, scale, dxo, nx, "w_in_bwd_na")
        dgate2 = jnp.concatenate([dgate, jnp.zeros_like(dgate)], axis=0)
        dmod, dg = _norm_grads(dshift, dgeff, dgate2, g[0], scale)
        drpb = _rpb_grad(dt3, dt3s, onehot, *rpb.shape)
        return dx, dmod, dg, dict(w_in=gw_in, w_out=gw_out, rpb=drpb), carried_bwd

    return xout, backward, carried


def _conv_layer(xin, g, mod_l, w_in, dw, db, w_out):
    shift, scale, gate = (_seg_vecs(mod_l, k, 1) for k in range(3))
    nx = xin.shape[0]
    h, r, p4 = _norm_w_in(xin, g, scale, shift, w_in, nx, "w_in_fwd_conv")
    a = _conv_fwd(p4, dw, db, "conv_fwd")
    yx, xout = _mm_out_resid(a, w_out, xin, gate, _NO_CTX, "w_out_fwd_conv")

    def backward(dxo):
        dyx, da, dgate = _gate_w_out_bwd(dxo, yx, gate, w_out, nx, "w_out_bwd_conv")
        gw_out = _mm_tn(a, dyx, "w_out_grad_conv", BF16)
        d4, ddw, ddb = _conv_bwd(da, p4, dw, db, "conv_bwd")
        gw_in = _mm_tn_parts(h, d4, "w_in_grad_conv", BF16)
        dx, dshift, dgeff = _w_in_bwd_norm(d4, w_in, xin, r, g, scale, dxo, nx, "w_in_bwd_conv")
        dmod, dg = _norm_grads(dshift, dgeff, dgate, g[0], scale)
        return dx, dmod, dg, dict(w_in=gw_in, w_out=gw_out, dw=ddw, db=ddb)

    return xout, backward


def _example_step(x, ctx, target, mod, norm_g, final_g, wts, late_comm=None, late_weights=None, grad_comm=None):
    nx = x.shape[0]
    consts = (_attn_mask(), _rpb_onehot())
    g_rows = [norm_g[i:i + 1] for i in range(4)]
    xc0 = jnp.concatenate([x, ctx], axis=0)
    xc1, bwd0 = _pool_layer(xc0, g_rows[0], mod[0], wts["pool_w_in"][0], wts["pool_w_grp"][0], wts["pool_w_out"][0],
                            wts["pool_scale"][0:1], nx, "p0")
    x2, bwd1, carried = _na_layer(xc1, g_rows[1], mod[1], wts["na_w_in"], wts["na_rpb"], wts["na_w_out"], nx, consts,
                                  late_comm)
    if late_weights is not None:
        wts = {**wts, **late_weights(carried)}
    x3, bwd2 = _conv_layer(x2, g_rows[2], mod[2], wts["conv_w_in"], wts["conv_dw"], wts["conv_db"], wts["conv_w_out"])
    x4, bwd3 = _pool_layer(x3, g_rows[3], mod[3], wts["pool_w_in"][1], wts["pool_w_grp"][1], wts["pool_w_out"][1],
                           wts["pool_scale"][1:2], nx, "p3")
    loss, dx4, dfinal_g = _final(x4, final_g, target, "loss_head")
    dx3, dmod3, dg3, gr3 = bwd3(dx4)
    dx2, dmod2, dg2, gr2 = bwd2(dx3)
    dxc1, dmod1, dg1, gr1, carried_bwd = bwd1(dx2, grad_comm(gr3, gr2) if grad_comm else None)
    dxc0, dmod0, dg0, gr0 = bwd0(dxc1)
    return dict(
        loss=loss, grad_x=dxc0[:nx], dmod=jnp.stack([dmod0, dmod1, dmod2, dmod3]),
        dnorm_g=jnp.stack([dg0, dg1, dg2, dg3]), dfinal_g=dfinal_g, layers=(gr0, gr1, gr2, gr3), carried=carried_bwd)


_AXES = ("x", "y", "c")
_CHIP_FLIPS = ((1, 0), (0, 1), (1, 1))


def _position():
    return tuple(lax.axis_index(a) for a in _AXES)


def _flipped(pos, flip):
    return tuple(1 - p if f else p for p, f in zip(pos, flip))


def _all_gather(v, axes, name):
    flips = [f for f in np.ndindex(2, 2, 2) if any(f) and all(a in axes or not b for a, b in zip(_AXES, f))]
    n = len(flips) + 1

    def body(v_ref, o_ref, send_sems, recv_sems, local_sem):
        pos = _position()
        slot = 0
        for a, p in zip(_AXES, pos):
            if a in axes:
                slot = 2 * slot + p
        local = pltpu.make_async_copy(v_ref, o_ref.at[slot], local_sem)
        local.start()
        copies = []
        for k, flip in enumerate(flips):
            cp = pltpu.make_async_remote_copy(v_ref, o_ref.at[slot], send_sems.at[k], recv_sems.at[k],
                                              device_id=_flipped(pos, flip), device_id_type=MESH)
            cp.start()
            copies.append(cp)
        for cp in copies:
            cp.wait()
        local.wait()

    return pl.pallas_call(
        body, in_specs=[HBM_SPEC], out_specs=HBM_SPEC, out_shape=_sds((n,) + v.shape, v.dtype),
        scratch_shapes=[pltpu.SemaphoreType.DMA((n - 1,)), pltpu.SemaphoreType.DMA((n - 1,)), pltpu.SemaphoreType.DMA(())],
        name=name,
    )(v)


class _Item:
    def __init__(self, key, layer, shape, shard_axis, half_axis):
        self.key, self.layer, self.shape = key, layer, tuple(shape)
        self.shard_axis, self.half_axis = shard_axis, half_axis
        self.shard = shape[shard_axis] // 4
        self.half = shape[half_axis] // 2

    def sized(self, shard=False, half=False):
        s = list(self.shape)
        if shard:
            s[self.shard_axis] = self.shard
        if half:
            s[self.half_axis] = self.half
        return tuple(s)

    def window(self, ref, chip=None, half=None):
        idx = [slice(None)] * len(self.shape)
        if chip is not None:
            idx[self.shard_axis] = pl.ds(chip * self.shard, self.shard)
        if half is not None:
            idx[self.half_axis] = pl.ds(half * self.half, self.half)
        return ref.at[tuple(idx)]


def _items(d, w):
    out = []
    for j in range(2):
        out += [_Item("pool_w_in", j, (d, 2 * w), 1, 0), _Item("pool_w_grp", j, (4, w // 4, w // 4), 1, 0),
                _Item("pool_w_out", j, (w, d), 0, 1)]
    out += [_Item("na_w_in", 0, (d, 4 * w), 1, 0), _Item("na_w_out", 0, (w, d), 0, 1),
            _Item("conv_w_in", 0, (d, 4 * w), 1, 0), _Item("conv_w_out", 0, (w, d), 0, 1)]
    return out


def _gather_weights(shards, items, name):
    comm = _gather_comm(shards, items)

    def body(*refs):
        n = len(items)
        comm.start(refs[:n], refs[n:2 * n], refs[2 * n:])
        comm.finish(refs[:n], refs[n:2 * n], refs[2 * n:])

    return pl.pallas_call(
        body, in_specs=[HBM_SPEC] * len(items), out_specs=[HBM_SPEC] * len(items), out_shape=comm.outs,
        scratch_shapes=comm.sems, name=name,
    )(*shards)


def _gather_comm(shards, items):
    n = len(items)

    def copies(src, dst, sems, onward):
        send_a, recv_a, send_b, recv_b, send_c, recv_c = sems
        x, y, c = _position()
        chip = 2 * x + y
        sibling = (x, y, 1 - c)
        own, out, fwd, fwd_in = [], [], [], []
        for i, it in enumerate(items):
            own.append(pltpu.make_async_remote_copy(src[i], it.window(dst[i], chip=chip), send_c.at[i], recv_c.at[i],
                                                    device_id=sibling, device_id_type=MESH))
            for k, flip in enumerate(_CHIP_FLIPS):
                px, py = _flipped((x, y), flip)
                s = 3 * i + k
                out.append(pltpu.make_async_remote_copy(
                    it.window(src[i], half=c), it.window(dst[i], chip=chip, half=c), send_a.at[s], recv_a.at[s],
                    device_id=(px, py, c), device_id_type=MESH))
                if onward:
                    got = it.window(dst[i], chip=2 * px + py, half=c)
                    fwd.append(pltpu.make_async_remote_copy(got, got, send_b.at[s], recv_b.at[s],
                                                            device_id=sibling, device_id_type=MESH))
                    other = it.window(dst[i], chip=2 * px + py, half=1 - c)
                    fwd_in.append(pltpu.make_async_remote_copy(other, other, send_b.at[s], recv_b.at[s],
                                                               device_id=sibling, device_id_type=MESH))
        return own, out, fwd, fwd_in

    def start(src, dst, sems):
        own, out, _, _ = copies(src, dst, sems, False)
        for cp in own + out:
            cp.start()

    def finish(src, dst, sems):
        own, out, fwd, fwd_in = copies(src, dst, sems, True)
        for arrived, onward in zip(out, fwd):
            arrived.wait_recv()
            onward.start()
        for cp in fwd_in:
            cp.wait_recv()
        for cp in out + fwd:
            cp.wait_send()
        for cp in own:
            cp.wait()

    sems = [pltpu.SemaphoreType.DMA((3 * n,)) for _ in range(4)] + [pltpu.SemaphoreType.DMA((n,)) for _ in range(2)]
    return _Comm(shards, [_sds(it.shape, BF16) for it in items], sems, start, finish)


def _pair_swap(arrays, windows, out_shapes, name):
    n = len(arrays)

    def body(*refs):
        src, got = refs[:n], refs[n:2 * n]
        send_sems, recv_sems = refs[2 * n:]
        x, y, c = _position()
        copies = []
        for i in range(n):
            cp = pltpu.make_async_remote_copy(windows[i](src[i], 1 - c), got[i], send_sems.at[i], recv_sems.at[i],
                                              device_id=(x, y, 1 - c), device_id_type=MESH)
            cp.start()
            copies.append(cp)
        for cp in copies:
            cp.wait()

    return pl.pallas_call(
        body, in_specs=[HBM_SPEC] * n, out_specs=[HBM_SPEC] * n, out_shape=list(out_shapes),
        scratch_shapes=[pltpu.SemaphoreType.DMA((n,)), pltpu.SemaphoreType.DMA((n,))], name=name,
    )(*arrays)


def _chip_exchange(partials, items, name):
    comm = _chip_exchange_comm(partials, items)

    def body(*refs):
        n = len(items)
        comm.start(refs[:n], refs[n:2 * n], refs[2 * n:])
        comm.finish(refs[:n], refs[n:2 * n], refs[2 * n:])

    return pl.pallas_call(
        body, in_specs=[HBM_SPEC] * len(items), out_specs=[HBM_SPEC] * len(items), out_shape=comm.outs,
        scratch_shapes=comm.sems, name=name,
    )(*partials)


def _chip_exchange_comm(partials, items):
    n = len(items)

    def copies(src, dst, sems):
        send_sems, recv_sems = sems
        x, y, c = _position()
        out = []
        for i, it in enumerate(items):
            for k, flip in enumerate(_CHIP_FLIPS):
                px, py = _flipped((x, y), flip)
                out.append(pltpu.make_async_remote_copy(
                    it.window(src[i], chip=2 * px + py), dst[i].at[k], send_sems.at[3 * i + k],
                    recv_sems.at[3 * i + k], device_id=(px, py, c), device_id_type=MESH))
        return out

    def start(src, dst, sems):
        for cp in copies(src, dst, sems):
            cp.start()

    def finish(src, dst, sems):
        for cp in copies(src, dst, sems):
            cp.wait()

    return _Comm(partials, [_sds((3,) + it.sized(shard=True, half=True), BF16) for it in items],
                 [pltpu.SemaphoreType.DMA((3 * n,)), pltpu.SemaphoreType.DMA((3 * n,))], start, finish)


def _pair_sum(g, got, it, pos, name):
    rows_split = it.half_axis == 0
    g2 = g.reshape(-1, g.shape[-1])
    got2 = got.reshape(-1, got.shape[-1])
    rows, cols = got2.shape
    tr = min(rows, 256)
    nb = rows // tr

    def body(pos_ref, g_ref, got_ref, o_ref):
        o_ref[...] = (g_ref[...].astype(F32) + got_ref[...].astype(F32)).astype(BF16)

    g_map = (lambda i, pos: (pos[1] * nb + i, 0)) if rows_split else (lambda i, pos: (i, pos[1]))
    blk = pl.BlockSpec((tr, cols), lambda i, pos: (i, 0))
    return pl.pallas_call(
        body, grid_spec=pltpu.PrefetchScalarGridSpec(
            num_scalar_prefetch=1, grid=(nb,), in_specs=[pl.BlockSpec((tr, cols), g_map), blk], out_specs=blk),
        out_shape=_sds((rows, cols), BF16), name=name, compiler_params=_cparams("parallel"),
    )(pos, g2, got2).reshape(got.shape)


_FLIP_SLOT = {2: 0, 1: 1, 3: 2}


def _chip_sum(pair, slots, it, pos, name):
    shape = it.sized(shard=True, half=True)
    nd = len(shape)
    nb = 4 if shape[0] % 64 == 0 else shape[0]
    blk = (shape[0] // nb,) + shape[1:]

    def body(pos_ref, p_ref, s_ref, o_ref):
        chip = pos_ref[0]
        for own in range(4):
            @pl.when(chip == own)
            def _():
                acc = None
                for k in range(4):
                    v = (p_ref[...] if k == own else s_ref[_FLIP_SLOT[own ^ k]]).astype(F32)
                    acc = v if acc is None else acc + v
                o_ref[...] = acc

    def p_map(i, pos):
        lead = i + (pos[0] * nb if it.shard_axis == 0 else 0)
        return (lead,) + tuple(pos[0] if ax == it.shard_axis else 0 for ax in range(1, nd))

    rest = (0,) * (nd - 1)
    return pl.pallas_call(
        body, grid_spec=pltpu.PrefetchScalarGridSpec(
            num_scalar_prefetch=1, grid=(nb,),
            in_specs=[pl.BlockSpec(blk, p_map), pl.BlockSpec((3,) + blk, lambda i, pos: (0, i) + rest)],
            out_specs=pl.BlockSpec(blk, lambda i, pos: (i,) + rest)),
        out_shape=_sds(shape, F32), name=name, compiler_params=_cparams("parallel"),
    )(pos, pair, slots)


_GRAD_KEYS = ("pool_w_in", "pool_w_grp", "pool_w_out", "na_w_in", "na_w_out", "conv_w_in", "conv_w_out")


def _adamw_matrix(w, m, v, owns, others, it, pos, name):
    nl = w.shape[0]
    rows_split = it.half_axis == 0
    r, cdim = int(np.prod(w.shape[1:-1])), w.shape[-1]
    hr, hc = (r // 2, cdim) if rows_split else (r, cdim // 2)
    br = min(hr, 256)
    nb = hr // br
    c1 = 1.0 - ADAM_B1 ** ADAM_STEP
    c2 = 1.0 - ADAM_B2 ** ADAM_STEP

    def body(pos_ref, w_ref, m_ref, v_ref, *rest):
        own_refs, other_refs = rest[:nl], rest[nl:2 * nl]
        g_ref, d_ref, nm_ref, nv_ref = rest[2 * nl:]
        j, h = pl.program_id(0), pl.program_id(1)
        own, other = own_refs[0][...], other_refs[0][...]
        for q in range(1, nl):
            own = jnp.where(j == q, own_refs[q][...], own)
            other = jnp.where(j == q, other_refs[q][...], other)
        gv = jnp.where(h == pos_ref[1], own, other)
        nm = ADAM_B1 * m_ref[...] + (1.0 - ADAM_B1) * gv
        nv = ADAM_B2 * v_ref[...] + (1.0 - ADAM_B2) * (gv * gv)
        g_ref[...] = gv
        nm_ref[...] = nm
        nv_ref[...] = nv
        d_ref[...] = -ADAM_LR * ((nm / c1) / (jnp.sqrt(nv / c2) + ADAM_EPS) + ADAM_WD * w_ref[...])

    if rows_split:
        full = pl.BlockSpec((None, br, hc), lambda j, h, i, pos: (j, h * nb + i, 0))
    else:
        full = pl.BlockSpec((None, br, hc), lambda j, h, i, pos: (j, i, h))
    half = pl.BlockSpec((br, hc), lambda j, h, i, pos: (i, 0))
    flat = lambda t: t.reshape(nl, r, cdim)
    outs = pl.pallas_call(
        body, grid_spec=pltpu.PrefetchScalarGridSpec(
            num_scalar_prefetch=1, grid=(nl, 2, nb), in_specs=[full] * 3 + [half] * (2 * nl), out_specs=[full] * 4),
        out_shape=[_sds((nl, r, cdim), F32)] * 4, name=name,
        compiler_params=_cparams("parallel", "parallel", "parallel"),
    )(pos, flat(w), flat(m), flat(v), *[t.reshape(hr, hc) for t in list(owns) + list(others)])
    return tuple(t.reshape(w.shape) for t in outs)


_WEIGHTS = ("c_ctx", "norm_g", "ada_w", "ada_b", "pool_w_in", "pool_w_grp", "pool_scale", "pool_w_out", "na_w_in",
            "na_rpb", "na_w_out", "conv_w_in", "conv_dw", "conv_db", "conv_w_out", "final_g")
_COND_ROWS = 16


def _modulations(cond, ada_w, ada_b_cols):
    nl, d, n = ada_w.shape
    return _matmul(
        cond, ada_w, mode="nn", grid=(nl, 1), a_silu=True, epilogue="bias",
        a_spec=pl.BlockSpec((_COND_ROWS, d), lambda i, j: (0, 0)), b_spec=pl.BlockSpec((None, d, n), lambda i, j: (i, 0, 0)),
        extra=(ada_b_cols,), extra_specs=(pl.BlockSpec((None, 1, n), lambda i, j: (i, 0, 0)),),
        out_shapes=[_sds((nl, _COND_ROWS, n), F32)], out_specs=[pl.BlockSpec((None, _COND_ROWS, n), lambda i, j: (i, 0, 0))],
        name="modulations")[0]


def _ada_w_grad(cond, dm_cols):
    d = cond.shape[1]
    nl, _, n = dm_cols.shape
    return _matmul(
        cond, dm_cols, mode="tn", grid=(nl, 1), a_silu=True,
        a_spec=pl.BlockSpec((_COND_ROWS, d), lambda i, j: (0, 0)), b_spec=pl.BlockSpec((None, _COND_ROWS, n), lambda i, j: (i, 0, 0)),
        out_shapes=[_sds((nl, d, n), F32)], out_specs=[pl.BlockSpec((None, d, n), lambda i, j: (i, 0, 0))],
        name="ada_w_grad")[0]


def _cond_grad(dm_cols, ada_w):
    nl, d, n = ada_w.shape
    return _matmul(
        dm_cols, ada_w, mode="nt", grid=(1, nl), nk=nl, acc_shape=(_COND_ROWS, d),
        a_spec=pl.BlockSpec((None, _COND_ROWS, n), lambda i, q: (q, 0, 0)), b_spec=pl.BlockSpec((None, d, n), lambda i, q: (q, 0, 0)),
        out_shapes=[_sds((_COND_ROWS, d), F32)], out_specs=[pl.BlockSpec((_COND_ROWS, d), lambda i, q: (0, 0))],
        name="cond_grad")[0]


def _pack(parts):
    flat = [p.reshape(-1) for p in parts]
    sizes = [f.shape[0] for f in flat]
    total = sum(sizes)
    rows = -(-total // 1024) * 8
    packed = jnp.concatenate(flat + [jnp.zeros((rows * 128 - total,), F32)]).reshape(rows, 128)
    offs = np.concatenate([[0], np.cumsum(sizes)])[:-1]
    return packed, [(int(o), p.shape) for o, p in zip(offs, parts)]


def _unpack(flat, layout, k):
    off, shape = layout[k]
    return flat[..., off:off + int(np.prod(shape))].reshape(flat.shape[:-1] + tuple(shape))


def kernel(x, c, ctx, c_ctx, norm_g, ada_w, ada_b, pool_w_in, pool_w_grp, pool_scale, pool_w_out, na_w_in, na_rpb, na_w_out, conv_w_in, conv_dw, conv_db, conv_w_out, final_g, loss_target, m_c_ctx, m_norm_g, m_ada_w, m_ada_b, m_pool_w_in, m_pool_w_grp, m_pool_scale, m_pool_w_out, m_na_w_in, m_na_rpb, m_na_w_out, m_conv_w_in, m_conv_dw, m_conv_db, m_conv_w_out, m_final_g, v_c_ctx, v_norm_g, v_ada_w, v_ada_b, v_pool_w_in, v_pool_w_grp, v_pool_scale, v_pool_w_out, v_na_w_in, v_na_rpb, v_na_w_out, v_conv_w_in, v_conv_dw, v_conv_db, v_conv_w_out, v_final_g):
    params = dict(c_ctx=c_ctx, norm_g=norm_g, ada_w=ada_w, ada_b=ada_b, pool_w_in=pool_w_in, pool_w_grp=pool_w_grp,
                  pool_scale=pool_scale, pool_w_out=pool_w_out, na_w_in=na_w_in, na_rpb=na_rpb, na_w_out=na_w_out,
                  conv_w_in=conv_w_in, conv_dw=conv_dw, conv_db=conv_db, conv_w_out=conv_w_out, final_g=final_g)
    mom1 = dict(c_ctx=m_c_ctx, norm_g=m_norm_g, ada_w=m_ada_w, ada_b=m_ada_b, pool_w_in=m_pool_w_in,
                pool_w_grp=m_pool_w_grp, pool_scale=m_pool_scale, pool_w_out=m_pool_w_out, na_w_in=m_na_w_in,
                na_rpb=m_na_rpb, na_w_out=m_na_w_out, conv_w_in=m_conv_w_in, conv_dw=m_conv_dw, conv_db=m_conv_db,
                conv_w_out=m_conv_w_out, final_g=m_final_g)
    mom2 = dict(c_ctx=v_c_ctx, norm_g=v_norm_g, ada_w=v_ada_w, ada_b=v_ada_b, pool_w_in=v_pool_w_in,
                pool_w_grp=v_pool_w_grp, pool_scale=v_pool_scale, pool_w_out=v_pool_w_out, na_w_in=v_na_w_in,
                na_rpb=v_na_rpb, na_w_out=v_na_w_out, conv_w_in=v_conv_w_in, conv_dw=v_conv_dw, conv_db=v_conv_db,
                conv_w_out=v_conv_w_out, final_g=v_final_g)
    d = x.shape[-1]
    w = na_w_out.shape[1] * 4
    xi, yi, ci = _position()
    chip = 2 * xi + yi
    dev = 2 * chip + ci
    n_ada = ada_w.shape[-1]

    def chip_cols(a, size):
        return lax.dynamic_slice_in_dim(a, chip * size, size, axis=a.ndim - 1)

    conds = _all_gather(c.reshape(8, d // 8), _AXES, "gather_cond").reshape(8, d)
    cond = jnp.concatenate([conds, c_ctx[None], jnp.zeros((_COND_ROWS - 9, d), F32)], axis=0)
    mod_cols = _modulations(cond, ada_w, chip_cols(ada_b, n_ada)[:, None, :])
    mod_all = _all_gather(mod_cols, ("x", "y"), "gather_mod")
    mod_all = mod_all.transpose(1, 2, 0, 3).reshape(4, _COND_ROWS, 3, d)
    mod = jnp.stack([lax.dynamic_index_in_dim(mod_all, dev, axis=1, keepdims=False), mod_all[:, 8]], axis=1)

    items = _items(d, w)
    early = [it for it in items if (it.key.startswith("pool") and it.layer == 0) or it.key.startswith("na")]
    late = [it for it in items if it not in early]
    shard_of = lambda it: params[it.key][it.layer].astype(BF16)
    full = {(it.key, it.layer): mat
            for it, mat in zip(early, _gather_weights([shard_of(it) for it in early], early, "gather_weights"))}
    late_comm = _gather_comm([shard_of(it) for it in late], late)

    def late_weights(mats):
        full.update({(it.key, it.layer): mat for it, mat in zip(late, mats)})
        return dict(pool_w_in=[full[("pool_w_in", j)] for j in range(2)],
                    pool_w_grp=[full[("pool_w_grp", j)] for j in range(2)],
                    pool_w_out=[full[("pool_w_out", j)] for j in range(2)],
                    conv_w_in=full[("conv_w_in", 0)], conv_w_out=full[("conv_w_out", 0)])

    small = _all_gather(_pack([pool_scale, conv_dw, conv_db])[0], ("x", "y"), "gather_small")
    small_layout = _pack([pool_scale, conv_dw, conv_db])[1]
    small = small.reshape(4, -1)

    def whole(k):
        parts = _unpack(small, small_layout, k)
        return jnp.moveaxis(parts, 0, -2).reshape(parts.shape[1:-1] + (-1,))

    wts = dict(pool_w_in=[full[("pool_w_in", 0)]], pool_w_grp=[full[("pool_w_grp", 0)]],
               pool_w_out=[full[("pool_w_out", 0)]], na_w_in=full[("na_w_in", 0)], na_w_out=full[("na_w_out", 0)],
               pool_scale=whole(0), na_rpb=na_rpb[0], conv_dw=whole(1)[0], conv_db=whole(2))
    pos = jnp.stack([chip, ci]).astype(jnp.int32)

    def layer_grads(its, by_layer):
        pick = {"pool_w_in": "w_in", "pool_w_grp": "w_grp", "pool_w_out": "w_out", "na_w_in": "w_in",
                "na_w_out": "w_out", "conv_w_in": "w_in", "conv_w_out": "w_out"}
        return [by_layer[(it.key.split("_")[0], it.layer)][pick[it.key]] for it in its]

    def pair_sums(its, mats, tag):
        got = _pair_swap(mats, [(lambda ref, half, it=it: it.window(ref, half=half)) for it in its],
                         [_sds(it.sized(half=True), BF16) for it in its], f"pair_exchange_{tag}")
        return [_pair_sum(g, s, it, pos, f"pair_sum_{tag}{i}") for i, (g, s, it) in enumerate(zip(mats, got, its))]

    pairs = dict()

    def grad_comm(gr3, gr2):
        pairs["late"] = pair_sums(late, layer_grads(late, {("pool", 1): gr3, ("conv", 0): gr2}), "late")
        return _chip_exchange_comm(pairs["late"], late)

    res = _example_step(x[0], ctx[0], loss_target[0], mod, norm_g, final_g[None], wts, late_comm, late_weights,
                        grad_comm)
    g0, g1, g2, g3 = res["layers"]
    pairs["early"] = pair_sums(early, layer_grads(early, {("pool", 0): g0, ("na", 0): g1}), "early")
    slots = dict(zip(late, res["carried"]))
    slots.update(zip(early, _chip_exchange(pairs["early"], early, "chip_exchange")))
    pair_of = dict(zip(late, pairs["late"]))
    pair_of.update(zip(early, pairs["early"]))
    reduced = [_chip_sum(pair_of[it], slots[it], it, pos, f"chip_sum_{i}") for i, it in enumerate(items)]
    theirs = _pair_swap(reduced, [lambda ref, half: ref] * len(items),
                        [_sds(t.shape, F32) for t in reduced], "pair_return")
    grads, matrix_out = dict(), dict()
    for k in _GRAD_KEYS:
        idx = [i for i, it in enumerate(items) if it.key == k]
        res_k = _adamw_matrix(params[k], mom1[k], mom2[k], [reduced[i] for i in idx], [theirs[i] for i in idx],
                              items[idx[0]], pos, f"adamw_{k}")
        grads[k], matrix_out[k] = res_k[0], res_k[1:]

    packed, layout = _pack([res["dfinal_g"], res["dnorm_g"], res["dmod"], g1["rpb"],
                            jnp.concatenate([g0["scale"], g3["scale"]], axis=0), g2["dw"], g2["db"]])
    every = _all_gather(packed, _AXES, "gather_vec_grads")
    total = _sum_lead(every, "sum_vec_grads").reshape(-1)
    every = every.reshape(8, -1)
    grads["final_g"] = _unpack(total, layout, 0).reshape(final_g.shape)
    grads["norm_g"] = _unpack(total, layout, 1)
    grads["na_rpb"] = _unpack(total, layout, 3)[None]
    grads["pool_scale"] = chip_cols(_unpack(total, layout, 4), pool_scale.shape[-1])
    grads["conv_dw"] = chip_cols(_unpack(total, layout, 5), conv_dw.shape[-1])[None]
    grads["conv_db"] = chip_cols(_unpack(total, layout, 6), conv_db.shape[-1])
    dmod_sum = _unpack(total, layout, 2).reshape(4, 2, 3 * d)
    dmod_each = _unpack(every, layout, 2).reshape(8, 4, 2, 3 * d)
    grads["ada_b"] = dmod_sum[:, 0] + dmod_sum[:, 1]
    dm = jnp.concatenate([dmod_each[:, :, 0].transpose(1, 0, 2), dmod_sum[:, 1][:, None],
                          jnp.zeros((4, _COND_ROWS - 9, 3 * d), F32)], axis=1)
    dm_cols = chip_cols(dm, n_ada)
    grads["ada_w"] = _ada_w_grad(cond, dm_cols)
    dcond = _cond_grad(dm_cols, ada_w)[8].reshape(8, d // 8)
    dcond = _sum_lead(_all_gather(dcond, ("x", "y"), "gather_cond_grad"), "sum_cond_grad").reshape(d)
    grads["c_ctx"] = dcond * _dsilu(c_ctx)

    outs = [[], [], []]
    for k in _WEIGHTS:
        step = matrix_out[k] if k in matrix_out else _adamw(params[k], grads[k], mom1[k], mom2[k], f"adamw_{k}")
        for lst, val in zip(outs, step):
            lst.append(val)
    loss = lax.psum(res["loss"][0, 0], _AXES)
    return (loss, res["grad_x"][None], *[grads[k].reshape(params[k].shape) for k in _WEIGHTS],
            *outs[0], *outs[1], *outs[2])
```

```python
import functools

import numpy as np
import jax
import jax.numpy as jnp
from jax import lax
from jax.experimental import pallas as pl
from jax.experimental.pallas import tpu as pltpu

F32 = jnp.float32
BF16 = jnp.bfloat16

EPS = 1e-6
GRID_W = 64
HEAD_DIM = 64
WIN_ROWS = 8
WIN_COLS = 16
POOL_WINDOWS = (2, 4, 8, 16)
Q_ROWS = 4
K_ROWS = 12
PAD_ROWS = 4
NEG = -1e30

ADAM_LR = 0.001
ADAM_B1 = 0.9
ADAM_B2 = 0.999
ADAM_EPS = 1e-08
ADAM_WD = 0.01
ADAM_STEP = 10

ROW_BLOCK = 256
VMEM_LIMIT = 56 * 1024 * 1024
ACT = BF16

MESH = pl.DeviceIdType.MESH
HBM_SPEC = pl.BlockSpec(memory_space=pltpu.HBM)


def _cparams(*sem):
    return pltpu.CompilerParams(dimension_semantics=sem or None, vmem_limit_bytes=VMEM_LIMIT)


def _sds(shape, dtype):
    return jax.ShapeDtypeStruct(tuple(shape), dtype)


def _sigmoid(x):
    return 1.0 / (1.0 + jnp.exp(-x))


def _silu(x):
    return x * _sigmoid(x)


def _dsilu(x):
    s = _sigmoid(x)
    return s * (1.0 + x * (1.0 - s))


_DIMS = {
    "nn": (((1,), (0,)), ((), ())),
    "nt": (((1,), (1,)), ((), ())),
    "tn": (((0,), (0,)), ((), ())),
}


def _matmul(a, b, *, mode, grid, a_spec, b_spec, out_shapes, out_specs, name, nk=1,
            a_silu=False, exact=False, epilogue=None, extra=(), extra_specs=(), acc_shape=None):
    n_extra = len(extra)
    n_out = len(out_shapes)

    def body(*refs):
        a_ref, b_ref = refs[:2]
        ex = refs[2:2 + n_extra]
        outs = refs[2 + n_extra:2 + n_extra + n_out]
        av = a_ref[...]
        bv = b_ref[...]
        if a_silu:
            av = _silu(av.astype(F32))
        if exact:
            prod = lax.dot_general(av.astype(F32), bv.astype(F32), _DIMS[mode],
                                   precision=lax.Precision.HIGHEST, preferred_element_type=F32)
        else:
            prod = lax.dot_general(av.astype(BF16), bv.astype(BF16), _DIMS[mode], preferred_element_type=F32)

        def finish(res):
            if epilogue is None:
                outs[0][...] = res.astype(outs[0].dtype)
            elif epilogue == "bias":
                outs[0][...] = (res + ex[0][...]).astype(outs[0].dtype)
            else:
                outs[0][...] = res.astype(outs[0].dtype)
                outs[1][...] = ex[0][...] + ex[1][...] * res

        if nk == 1:
            finish(prod)
        else:
            acc = refs[-1]
            k = pl.program_id(len(grid) - 1)

            @pl.when(k == 0)
            def _():
                acc[...] = prod

            @pl.when(k > 0)
            def _():
                acc[...] += prod

            @pl.when(k == nk - 1)
            def _():
                finish(acc[...])

    scratch = [pltpu.VMEM(acc_shape, F32)] if nk > 1 else []
    sem = ("parallel",) * (len(grid) - 1) + ("arbitrary",)
    return pl.pallas_call(
        body, grid=grid, in_specs=[a_spec, b_spec, *extra_specs], out_specs=list(out_specs),
        out_shape=list(out_shapes), scratch_shapes=scratch, name=name, compiler_params=_cparams(*sem),
    )(a, b, *extra)


def _row_tile(rows):
    for t in (768, 512, 256):
        if rows % t == 0:
            return t
    return rows


def _mm_nn(a, b, name, out_dtype=F32, tn=1024):
    m, k = a.shape
    n = b.shape[1]
    tm = _row_tile(m)
    tn = min(tn, n)
    return _matmul(
        a, b, mode="nn", grid=(m // tm, n // tn),
        a_spec=pl.BlockSpec((tm, k), lambda i, j: (i, 0)), b_spec=pl.BlockSpec((k, tn), lambda i, j: (0, j)),
        out_shapes=[_sds((m, n), out_dtype)], out_specs=[pl.BlockSpec((tm, tn), lambda i, j: (i, j))], name=name)[0]


def _mm_out_resid(a, w_out, xres, gate, nxb, name):
    m, k = a.shape
    n = w_out.shape[1]
    tm = ROW_BLOCK
    seg = lambda i, j: (jnp.where(i >= nxb, 1, 0), 0, 0)
    return _matmul(
        a, w_out, mode="nn", grid=(m // tm, 1),
        a_spec=pl.BlockSpec((tm, k), lambda i, j: (i, 0)), b_spec=pl.BlockSpec((k, n), lambda i, j: (0, 0)),
        extra=(xres, gate), extra_specs=(pl.BlockSpec((tm, n), lambda i, j: (i, 0)), pl.BlockSpec((None, 1, n), seg)),
        out_shapes=[_sds((m, n), ACT), _sds((m, n), F32)],
        out_specs=[pl.BlockSpec((tm, n), lambda i, j: (i, 0))] * 2, epilogue="resid", name=name)


def _mm_nt(a, b, name, out_dtype=F32):
    m, n = a.shape
    k = b.shape[0]
    tm = _row_tile(m)
    return _matmul(
        a, b, mode="nt", grid=(m // tm, 1),
        a_spec=pl.BlockSpec((tm, n), lambda i, j: (i, 0)), b_spec=pl.BlockSpec((k, n), lambda i, j: (0, 0)),
        out_shapes=[_sds((m, k), out_dtype)], out_specs=[pl.BlockSpec((tm, k), lambda i, j: (i, 0))], name=name)[0]


def _mm_nt_parts(a, b, name):
    p, m, kp = a.shape
    d = b.shape[0]
    tm = _row_tile(m)
    return _matmul(
        a, b, mode="nt", grid=(m // tm, p), nk=p, acc_shape=(tm, d),
        a_spec=pl.BlockSpec((None, tm, kp), lambda i, q: (q, i, 0)), b_spec=pl.BlockSpec((d, kp), lambda i, q: (0, q)),
        out_shapes=[_sds((m, d), F32)], out_specs=[pl.BlockSpec((tm, d), lambda i, q: (i, 0))], name=name)[0]


def _mm_tn(a, b, name, out_dtype, tm=512):
    r, m = a.shape
    n = b.shape[1]
    tm = min(tm, m)
    tn = min(1024, n)
    return _matmul(
        a, b, mode="tn", grid=(m // tm, n // tn),
        a_spec=pl.BlockSpec((r, tm), lambda i, j: (0, i)), b_spec=pl.BlockSpec((r, tn), lambda i, j: (0, j)),
        out_shapes=[_sds((m, n), out_dtype)], out_specs=[pl.BlockSpec((tm, tn), lambda i, j: (i, j))], name=name)[0]


def _mm_tn_parts(a, b, name, out_dtype, tm=512):
    r, m = a.shape
    p, _, np_ = b.shape
    tm = min(tm, m)
    return _matmul(
        a, b, mode="tn", grid=(m // tm, p),
        a_spec=pl.BlockSpec((r, tm), lambda i, q: (0, i)), b_spec=pl.BlockSpec((None, r, np_), lambda i, q: (q, 0, 0)),
        out_shapes=[_sds((m, p * np_), out_dtype)], out_specs=[pl.BlockSpec((tm, np_), lambda i, q: (i, q))],
        name=name)[0]


def _seg_map(nxb):
    return lambda i: (jnp.where(i >= nxb, 1, 0), 0, 0)


def _normmod_fwd(x, g, scale, shift, nxb, name):
    rows, d = x.shape
    tr = ROW_BLOCK

    def body(x_ref, g_ref, sc_ref, sh_ref, h_ref, r_ref):
        xv = x_ref[...]
        r = lax.rsqrt(jnp.mean(xv * xv, axis=-1, keepdims=True) + EPS)
        h = (xv * r) * g_ref[...] * (1.0 + sc_ref[...]) + sh_ref[...]
        h_ref[...] = h.astype(BF16)
        r_ref[...] = r

    row = pl.BlockSpec((tr, d), lambda i: (i, 0))
    vec = pl.BlockSpec((None, 1, d), _seg_map(nxb))
    return pl.pallas_call(
        body, grid=(rows // tr,), in_specs=[row, pl.BlockSpec((1, d), lambda i: (0, 0)), vec, vec],
        out_specs=[row, pl.BlockSpec((tr, 1), lambda i: (i, 0))],
        out_shape=[_sds((rows, d), BF16), _sds((rows, 1), F32)], name=name, compiler_params=_cparams("parallel"),
    )(x, g, scale, shift)


def _normmod_bwd(dh, x, r, g, scale, dres, nxb, name):
    rows, d = x.shape
    tr = ROW_BLOCK
    nres = dres.shape[0] // tr
    nseg = scale.shape[0]

    def body(dh_ref, x_ref, r_ref, g_ref, sc_ref, dres_ref, dx_ref, dsh_ref, dge_ref):
        i = pl.program_id(0)
        dhv = dh_ref[...]
        rv = r_ref[...]
        xn = x_ref[...] * rv
        dxn = dhv * (g_ref[...] * (1.0 + sc_ref[...]))
        dx = rv * (dxn - xn * jnp.mean(dxn * xn, axis=-1, keepdims=True))

        @pl.when(i < nres)
        def _():
            dx_ref[...] = dx + dres_ref[...]

        @pl.when(i >= nres)
        def _():
            dx_ref[...] = dx

        first = jnp.logical_or(i == 0, i == nxb)
        s_dh = jnp.sum(dhv, axis=0, keepdims=True)
        s_ge = jnp.sum(dhv * xn, axis=0, keepdims=True)

        @pl.when(first)
        def _():
            dsh_ref[...] = s_dh
            dge_ref[...] = s_ge

        @pl.when(jnp.logical_not(first))
        def _():
            dsh_ref[...] += s_dh
            dge_ref[...] += s_ge

    row = pl.BlockSpec((tr, d), lambda i: (i, 0))
    vec = pl.BlockSpec((None, 1, d), _seg_map(nxb))
    return pl.pallas_call(
        body, grid=(rows // tr,),
        in_specs=[row, row, pl.BlockSpec((tr, 1), lambda i: (i, 0)), pl.BlockSpec((1, d), lambda i: (0, 0)), vec,
                  pl.BlockSpec((tr, d), lambda i: (jnp.minimum(i, nres - 1), 0))],
        out_specs=[row, vec, vec],
        out_shape=[_sds((rows, d), F32), _sds((nseg, 1, d), F32), _sds((nseg, 1, d), F32)],
        name=name, compiler_params=_cparams("arbitrary"),
    )(dh, x, r, g, scale, dres)


def _gate_bwd(dxo, yx, gate, nxb, name):
    rows, d = yx.shape
    tr = ROW_BLOCK
    nseg = gate.shape[0]

    def body(dx_ref, yx_ref, gt_ref, dyx_ref, dg_ref):
        i = pl.program_id(0)
        dxv = dx_ref[...]
        dyx_ref[...] = (dxv * gt_ref[...]).astype(BF16)
        s = jnp.sum(dxv * yx_ref[...].astype(F32), axis=0, keepdims=True)
        first = jnp.logical_or(i == 0, i == nxb)

        @pl.when(first)
        def _():
            dg_ref[...] = s

        @pl.when(jnp.logical_not(first))
        def _():
            dg_ref[...] += s

    row = pl.BlockSpec((tr, d), lambda i: (i, 0))
    vec = pl.BlockSpec((None, 1, d), _seg_map(nxb))
    return pl.pallas_call(
        body, grid=(rows // tr,), in_specs=[row, row, vec], out_specs=[row, vec],
        out_shape=[_sds((rows, d), BF16), _sds((nseg, 1, d), F32)], name=name, compiler_params=_cparams("arbitrary"),
    )(dxo, yx, gate)


def _row_vec(ref, is_ctx):
    return ref[0] if is_ctx is None else jnp.where(is_ctx, ref[1], ref[0])


def _ctx_rows(i, tm, nx, nseg):
    if nseg == 1:
        return None
    return i * tm + lax.broadcasted_iota(jnp.int32, (tm, 1), 0) >= nx


def _seg_sums(ref, val, is_ctx, first):
    if is_ctx is None:
        parts = [jnp.sum(val, axis=0, keepdims=True)]
    else:
        parts = [jnp.sum(jnp.where(is_ctx, 0.0, val), axis=0, keepdims=True),
                 jnp.sum(jnp.where(is_ctx, val, 0.0), axis=0, keepdims=True)]

    @pl.when(first)
    def _():
        for k, p in enumerate(parts):
            ref[k] = p

    @pl.when(jnp.logical_not(first))
    def _():
        for k, p in enumerate(parts):
            ref[k] += p


def _norm_w_in(x, g, scale, shift, w_in, nx, name):
    rows, d = x.shape
    n = w_in.shape[1]
    nseg = scale.shape[0]
    tm = _row_tile(rows)
    tn = min(1024, n)

    def body(x_ref, g_ref, sc_ref, sh_ref, w_ref, h_ref, r_ref, p_ref):
        i, j = pl.program_id(0), pl.program_id(1)

        @pl.when(j == 0)
        def _():
            xv = x_ref[...]
            r = lax.rsqrt(jnp.mean(xv * xv, axis=-1, keepdims=True) + EPS)
            is_ctx = _ctx_rows(i, tm, nx, nseg)
            h = (xv * r) * g_ref[...] * (1.0 + _row_vec(sc_ref, is_ctx)) + _row_vec(sh_ref, is_ctx)
            h_ref[...] = h.astype(BF16)
            r_ref[...] = r

        p_ref[...] = jnp.dot(h_ref[...], w_ref[...], preferred_element_type=F32).astype(ACT)

    vec = pl.BlockSpec((nseg, 1, d), lambda i, j: (0, 0, 0))
    return pl.pallas_call(
        body, grid=(rows // tm, n // tn),
        in_specs=[pl.BlockSpec((tm, d), lambda i, j: (i, 0)), pl.BlockSpec((1, d), lambda i, j: (0, 0)), vec, vec,
                  pl.BlockSpec((d, tn), lambda i, j: (0, j))],
        out_specs=[pl.BlockSpec((tm, d), lambda i, j: (i, 0)), pl.BlockSpec((tm, 1), lambda i, j: (i, 0)),
                   pl.BlockSpec((tm, tn), lambda i, j: (i, j))],
        out_shape=[_sds((rows, d), BF16), _sds((rows, 1), F32), _sds((rows, n), ACT)],
        name=name, compiler_params=_cparams("parallel", "arbitrary"),
    )(x, g, scale, shift, w_in)


def _gate_w_out_bwd(dxo, yx, gate, w_out, nx, name):
    rows, d = yx.shape
    w = w_out.shape[0]
    nseg = gate.shape[0]
    tm = _row_tile(rows)

    def body(dx_ref, yx_ref, gt_ref, w_ref, dyx_ref, da_ref, dg_ref):
        i = pl.program_id(0)
        is_ctx = _ctx_rows(i, tm, nx, nseg)
        dxv = dx_ref[...]
        dyx = (dxv * _row_vec(gt_ref, is_ctx)).astype(BF16)
        dyx_ref[...] = dyx
        da_ref[...] = lax.dot_general(dyx, w_ref[...], _DIMS["nt"], preferred_element_type=F32).astype(ACT)
        _seg_sums(dg_ref, dxv * yx_ref[...].astype(F32), is_ctx, i == 0)

    row = pl.BlockSpec((tm, d), lambda i: (i, 0))
    vec = pl.BlockSpec((nseg, 1, d), lambda i: (0, 0, 0))
    return pl.pallas_call(
        body, grid=(rows // tm,), in_specs=[row, row, vec, pl.BlockSpec((w, d), lambda i: (0, 0))],
        out_specs=[row, pl.BlockSpec((tm, w), lambda i: (i, 0)), vec],
        out_shape=[_sds((rows, d), BF16), _sds((rows, w), ACT), _sds((nseg, 1, d), F32)],
        name=name, compiler_params=_cparams("arbitrary"),
    )(dxo, yx, gate, w_out)


def _w_in_bwd_norm(dparts, w_in, x, r, g, scale, dres, nx, name):
    np_, rows, kp = dparts.shape
    d = w_in.shape[0]
    nseg = scale.shape[0]
    tm = _row_tile(rows)
    nsub = tm // ROW_BLOCK
    nres_blocks = dres.shape[0] // ROW_BLOCK

    def body(dp_ref, w_ref, x_ref, r_ref, g_ref, sc_ref, *rest):
        dres_refs = rest[:nsub]
        dx_ref, dsh_ref, dge_ref, acc = rest[nsub:]
        i, k = pl.program_id(0), pl.program_id(1)
        prod = lax.dot_general(dp_ref[...], w_ref[...], _DIMS["nt"], preferred_element_type=F32)

        @pl.when(k == 0)
        def _():
            acc[...] = prod

        @pl.when(k > 0)
        def _():
            acc[...] += prod

        @pl.when(k == np_ - 1)
        def _():
            is_ctx = _ctx_rows(i, tm, nx, nseg)
            dhv = acc[...]
            rv = r_ref[...]
            xn = x_ref[...] * rv
            dxn = dhv * (g_ref[...] * (1.0 + _row_vec(sc_ref, is_ctx)))
            dx = rv * (dxn - xn * jnp.mean(dxn * xn, axis=-1, keepdims=True))
            for s in range(nsub):
                piece = slice(s * ROW_BLOCK, (s + 1) * ROW_BLOCK)
                res = dres_refs[s][...]
                if nres_blocks * ROW_BLOCK < rows:
                    res = jnp.where(i * nsub + s < nres_blocks, res, 0.0)
                dx_ref[piece, :] = dx[piece, :] + res
            _seg_sums(dsh_ref, dhv, is_ctx, i == 0)
            _seg_sums(dge_ref, dhv * xn, is_ctx, i == 0)

    row = pl.BlockSpec((tm, d), lambda i, k: (i, 0))
    vec = pl.BlockSpec((nseg, 1, d), lambda i, k: (0, 0, 0))
    return pl.pallas_call(
        body, grid=(rows // tm, np_),
        in_specs=[pl.BlockSpec((None, tm, kp), lambda i, k: (k, i, 0)), pl.BlockSpec((d, kp), lambda i, k: (0, k)),
                  row, pl.BlockSpec((tm, 1), lambda i, k: (i, 0)), pl.BlockSpec((1, d), lambda i, k: (0, 0)), vec]
        + [pl.BlockSpec((ROW_BLOCK, d), (lambda i, k, s=s: (jnp.minimum(i * nsub + s, nres_blocks - 1), 0)))
           for s in range(nsub)],
        out_specs=[row, vec, vec],
        out_shape=[_sds((rows, d), F32), _sds((nseg, 1, d), F32), _sds((nseg, 1, d), F32)],
        scratch_shapes=[pltpu.VMEM((tm, d), F32)], name=name, compiler_params=_cparams("arbitrary", "arbitrary"),
    )(dparts, w_in, x, r, g, scale, *([dres] * nsub))


_PAD_TOP = 16
_PAD_BOT = 32


def _window_sum(buf, xv, lo, n):
    t = xv.shape[0]
    c = xv.shape[1]
    tp = t + _PAD_TOP + _PAD_BOT
    buf[pl.ds(0, _PAD_TOP), :] = jnp.zeros((_PAD_TOP, c), F32)
    buf[pl.ds(_PAD_TOP, t), :] = xv
    buf[pl.ds(_PAD_TOP + t, _PAD_BOT), :] = jnp.zeros((_PAD_BOT, c), F32)
    p = buf[...]
    k = 1
    while k < n:
        p = p + pltpu.roll(p, tp - k, 0)
        k *= 2
    if lo:
        p = pltpu.roll(p, -lo, 0)
    buf[...] = p
    return buf[pl.ds(_PAD_TOP, t), :]


def _window_count(t, half):
    pos = lax.broadcasted_iota(jnp.int32, (t, 1), 0)
    return (jnp.minimum(pos + half, t) - jnp.maximum(pos - half, 0)).astype(F32)


def _segments(rows, nx):
    return [(0, nx)] + ([(nx, rows - nx)] if rows > nx else [])


def _pool_fwd(uv, nx, name):
    rows = uv.shape[0]
    w = uv.shape[1] // 2
    cb = 128
    per_group = w // len(POOL_WINDOWS) // cb
    segs = _segments(rows, nx)

    def body(u_ref, z_ref, *bufs):
        j = pl.program_id(0)
        for gi, win in enumerate(POOL_WINDOWS):
            half = win // 2

            @pl.when(jnp.logical_and(j >= gi * per_group, j < (gi + 1) * per_group))
            def _():
                for (start, length), buf in zip(segs, bufs):
                    uvv = u_ref[pl.ds(start, length), :].astype(F32)
                    s = _window_sum(buf, uvv, -half, win)
                    z_ref[pl.ds(start, length), :] = (s / _window_count(length, half) - uvv).astype(BF16)

    scratch = [pltpu.VMEM((length + _PAD_TOP + _PAD_BOT, cb), F32) for _, length in segs]
    return pl.pallas_call(
        body, grid=(w // cb,), in_specs=[pl.BlockSpec((rows, cb), lambda j: (0, j))],
        out_specs=pl.BlockSpec((rows, cb), lambda j: (0, j)), out_shape=_sds((rows, w), BF16),
        scratch_shapes=scratch, name=name, compiler_params=_cparams("parallel"),
    )(uv)


def _pool_bwd(dz, dgt, nx, name):
    rows, w = dz.shape
    cb = 128
    per_group = w // len(POOL_WINDOWS) // cb
    segs = _segments(rows, nx)

    def body(dz_ref, dgt_ref, o_ref, *bufs):
        j = pl.program_id(0)
        o_ref[1] = dgt_ref[...]
        for gi, win in enumerate(POOL_WINDOWS):
            half = win // 2

            @pl.when(jnp.logical_and(j >= gi * per_group, j < (gi + 1) * per_group))
            def _():
                for (start, length), buf in zip(segs, bufs):
                    dzv = dz_ref[pl.ds(start, length), :].astype(F32)
                    s = _window_sum(buf, dzv / _window_count(length, half), 1 - half, win)
                    o_ref[0, pl.ds(start, length), :] = (s - dzv).astype(BF16)

    scratch = [pltpu.VMEM((length + _PAD_TOP + _PAD_BOT, cb), F32) for _, length in segs]
    col = pl.BlockSpec((rows, cb), lambda j: (0, j))
    return pl.pallas_call(
        body, grid=(w // cb,), in_specs=[col, col], out_specs=pl.BlockSpec((2, rows, cb), lambda j: (0, 0, j)),
        out_shape=_sds((2, rows, w), BF16), scratch_shapes=scratch, name=name, compiler_params=_cparams("parallel"),
    )(dz, dgt)


def _grp_fwd(z, w_grp, uv, scale, name):
    rows, w = z.shape
    ng, gc, _ = w_grp.shape
    tm = _row_tile(rows)

    def body(z_ref, w_ref, gt_ref, sc_ref, mx_ref, a_ref):
        mixed = jnp.dot(z_ref[...], w_ref[...], preferred_element_type=F32)
        mx_ref[...] = mixed.astype(ACT)
        a_ref[...] = (mixed * sc_ref[...] * _silu(gt_ref[...].astype(F32))).astype(BF16)

    blk = pl.BlockSpec((tm, gc), lambda g, i: (i, g))
    return pl.pallas_call(
        body, grid=(ng, rows // tm),
        in_specs=[blk, pl.BlockSpec((None, gc, gc), lambda g, i: (g, 0, 0)),
                  pl.BlockSpec((tm, gc), lambda g, i: (i, ng + g)), pl.BlockSpec((1, gc), lambda g, i: (0, g))],
        out_specs=[blk, blk], out_shape=[_sds((rows, w), ACT), _sds((rows, w), BF16)],
        name=name, compiler_params=_cparams("parallel", "parallel"),
    )(z, w_grp, uv, scale)


def _grp_bwd(da, mixed, uv, scale, w_grp, name):
    rows, w = da.shape
    ng, gc, _ = w_grp.shape
    tm = _row_tile(rows)

    def body(da_ref, mx_ref, gt_ref, sc_ref, w_ref, dm_ref, dz_ref, dgt_ref, dsc_ref):
        i = pl.program_id(1)
        dav = da_ref[...].astype(F32)
        mixed = mx_ref[...].astype(F32)
        gt = gt_ref[...].astype(F32)
        sg = _silu(gt)
        sc = sc_ref[...]
        dm = (dav * sc * sg).astype(BF16)
        dm_ref[...] = dm
        dz_ref[...] = lax.dot_general(dm, w_ref[...], _DIMS["nt"], preferred_element_type=F32).astype(ACT)
        dgt_ref[...] = (dav * mixed * sc * _dsilu(gt)).astype(BF16)
        s = jnp.sum(dav * mixed * sg, axis=0, keepdims=True)

        @pl.when(i == 0)
        def _():
            dsc_ref[...] = s

        @pl.when(i > 0)
        def _():
            dsc_ref[...] += s

    blk = pl.BlockSpec((tm, gc), lambda g, i: (i, g))
    vec = pl.BlockSpec((1, gc), lambda g, i: (0, g))
    return pl.pallas_call(
        body, grid=(ng, rows // tm),
        in_specs=[blk, blk, pl.BlockSpec((tm, gc), lambda g, i: (i, ng + g)), vec,
                  pl.BlockSpec((None, gc, gc), lambda g, i: (g, 0, 0))],
        out_specs=[blk, blk, blk, vec],
        out_shape=[_sds((rows, w), BF16), _sds((rows, w), ACT), _sds((rows, w), BF16), _sds((1, w), F32)],
        name=name, compiler_params=_cparams("parallel", "arbitrary"),
    )(da, mixed, uv, scale, w_grp)


def _pool_scratch(rows, nx, cols):
    return [pltpu.VMEM((length + _PAD_TOP + _PAD_BOT, cols), F32) for _, length in _segments(rows, nx)]


def _per_group(g, fn):
    for gi, win in enumerate(POOL_WINDOWS):
        pl.when(g == gi)(functools.partial(fn, win))


def _pool_grp_fwd(uv, w_grp, scale, nx, name):
    rows = uv.shape[0]
    ng, gc, _ = w_grp.shape
    w = ng * gc
    segs = _segments(rows, nx)

    def body(u_ref, gt_ref, w_ref, sc_ref, z_ref, mx_ref, a_ref, *bufs):
        def pool(win):
            half = win // 2
            for (start, length), buf in zip(segs, bufs):
                uvv = u_ref[pl.ds(start, length), :].astype(F32)
                s = _window_sum(buf, uvv, -half, win)
                z_ref[pl.ds(start, length), :] = (s / _window_count(length, half) - uvv).astype(BF16)

        _per_group(pl.program_id(0), pool)
        mixed = jnp.dot(z_ref[...], w_ref[...], preferred_element_type=F32)
        mx_ref[...] = mixed.astype(ACT)
        a_ref[...] = (mixed * sc_ref[...] * _silu(gt_ref[...].astype(F32))).astype(BF16)

    col = pl.BlockSpec((rows, gc), lambda g: (0, g))
    return pl.pallas_call(
        body, grid=(ng,),
        in_specs=[col, pl.BlockSpec((rows, gc), lambda g: (0, ng + g)), pl.BlockSpec((None, gc, gc), lambda g: (g, 0, 0)),
                  pl.BlockSpec((1, gc), lambda g: (0, g))],
        out_specs=[col, col, col], out_shape=[_sds((rows, w), BF16), _sds((rows, w), ACT), _sds((rows, w), BF16)],
        scratch_shapes=_pool_scratch(rows, nx, gc), name=name, compiler_params=_cparams("parallel"),
    )(uv, uv, w_grp, scale)


def _pool_grp_bwd(da, mixed, uv, scale, w_grp, nx, name):
    rows, w = da.shape
    ng, gc, _ = w_grp.shape
    segs = _segments(rows, nx)

    def body(da_ref, mx_ref, gt_ref, sc_ref, w_ref, dm_ref, duv_ref, dsc_ref, dz_ref, *bufs):
        dav = da_ref[...].astype(F32)
        mixed = mx_ref[...].astype(F32)
        gt = gt_ref[...].astype(F32)
        sg = _silu(gt)
        sc = sc_ref[...]
        dm = (dav * sc * sg).astype(BF16)
        dm_ref[...] = dm
        dz_ref[...] = lax.dot_general(dm, w_ref[...], _DIMS["nt"], preferred_element_type=F32)
        duv_ref[1] = (dav * mixed * sc * _dsilu(gt)).astype(BF16)
        dsc_ref[...] = jnp.sum(dav * mixed * sg, axis=0, keepdims=True)

        def unpool(win):
            half = win // 2
            for (start, length), buf in zip(segs, bufs):
                dzv = dz_ref[pl.ds(start, length), :]
                s = _window_sum(buf, dzv / _window_count(length, half), 1 - half, win)
                duv_ref[0, pl.ds(start, length), :] = (s - dzv).astype(BF16)

        _per_group(pl.program_id(0), unpool)

    col = pl.BlockSpec((rows, gc), lambda g: (0, g))
    vec = pl.BlockSpec((1, gc), lambda g: (0, g))
    return pl.pallas_call(
        body, grid=(ng,),
        in_specs=[col, col, pl.BlockSpec((rows, gc), lambda g: (0, ng + g)), vec,
                  pl.BlockSpec((None, gc, gc), lambda g: (g, 0, 0))],
        out_specs=[col, pl.BlockSpec((2, rows, gc), lambda g: (0, 0, g)), vec],
        out_shape=[_sds((rows, w), BF16), _sds((2, rows, w), BF16), _sds((1, w), F32)],
        scratch_shapes=[pltpu.VMEM((rows, gc), F32)] + _pool_scratch(rows, nx, gc),
        name=name, compiler_params=_cparams("parallel"),
    )(da, mixed, uv, scale, w_grp)


def _grp_wgrad(z, dm, ng, name, out_dtype):
    rows, w = z.shape
    gc = w // ng

    def body(z_ref, dm_ref, o_ref):
        o_ref[...] = lax.dot_general(z_ref[...], dm_ref[...], _DIMS["tn"],
                                     preferred_element_type=F32).astype(o_ref.dtype)

    blk = pl.BlockSpec((rows, gc), lambda g: (0, g))
    return pl.pallas_call(
        body, grid=(ng,), in_specs=[blk, blk], out_specs=pl.BlockSpec((None, gc, gc), lambda g: (g, 0, 0)),
        out_shape=_sds((ng, gc, gc), out_dtype), name=name, compiler_params=_cparams("parallel"),
    )(z, dm)


def _shift_rows(v, by):
    t = v.shape[0]
    pos = lax.broadcasted_iota(jnp.int32, v.shape, 0)
    rolled = pltpu.roll(v, by % t, 0)
    keep = pos >= by if by > 0 else pos < t + by
    return jnp.where(keep, rolled, 0.0)


def _conv_specs(t, w, cb):
    return [pl.BlockSpec((t, cb), (lambda j, q=q: (0, q * (w // cb) + j))) for q in range(4)]


def _conv_fwd(p4, dw, db, name):
    t = p4.shape[0]
    w = p4.shape[1] // 4
    cb = 128

    def body(bg_ref, cg_ref, v_ref, g_ref, dw_ref, db_ref, a_ref):
        tv = cg_ref[...].astype(F32) * v_ref[...].astype(F32)
        conv = (dw_ref[0:1, :] * _shift_rows(tv, 1) + dw_ref[1:2, :] * tv + dw_ref[2:3, :] * _shift_rows(tv, -1)
                + db_ref[...])
        a_ref[...] = (bg_ref[...].astype(F32) * conv * _silu(g_ref[...].astype(F32))).astype(BF16)

    return pl.pallas_call(
        body, grid=(w // cb,),
        in_specs=_conv_specs(t, w, cb) + [pl.BlockSpec((3, cb), lambda j: (0, j)), pl.BlockSpec((1, cb), lambda j: (0, j))],
        out_specs=pl.BlockSpec((t, cb), lambda j: (0, j)), out_shape=_sds((t, w), BF16),
        name=name, compiler_params=_cparams("parallel"),
    )(p4, p4, p4, p4, dw, db)


def _conv_bwd(da, p4, dw, db, name):
    t, w = da.shape
    cb = 128

    def body(da_ref, bg_ref, cg_ref, v_ref, g_ref, dw_ref, db_ref, d4_ref, ddw_ref, ddb_ref):
        cg = cg_ref[...].astype(F32)
        vv = v_ref[...].astype(F32)
        bg = bg_ref[...].astype(F32)
        gv = g_ref[...].astype(F32)
        tv = cg * vv
        tm1 = _shift_rows(tv, 1)
        tp1 = _shift_rows(tv, -1)
        w0, w1, w2 = dw_ref[0:1, :], dw_ref[1:2, :], dw_ref[2:3, :]
        conv = w0 * tm1 + w1 * tv + w2 * tp1 + db_ref[...]
        y = bg * conv
        dav = da_ref[...].astype(F32)
        dy = dav * _silu(gv)
        d4_ref[3] = (dav * y * _dsilu(gv)).astype(BF16)
        d4_ref[0] = (dy * conv).astype(BF16)
        dconv = dy * bg
        ddb_ref[...] = jnp.sum(dconv, axis=0, keepdims=True)
        ddw_ref[0:1, :] = jnp.sum(dconv * tm1, axis=0, keepdims=True)
        ddw_ref[1:2, :] = jnp.sum(dconv * tv, axis=0, keepdims=True)
        ddw_ref[2:3, :] = jnp.sum(dconv * tp1, axis=0, keepdims=True)
        dt = w0 * _shift_rows(dconv, -1) + w1 * dconv + w2 * _shift_rows(dconv, 1)
        d4_ref[1] = (dt * vv).astype(BF16)
        d4_ref[2] = (dt * cg).astype(BF16)

    col = pl.BlockSpec((t, cb), lambda j: (0, j))
    tap = pl.BlockSpec((3, cb), lambda j: (0, j))
    bias = pl.BlockSpec((1, cb), lambda j: (0, j))
    return pl.pallas_call(
        body, grid=(w // cb,), in_specs=[col] + _conv_specs(t, w, cb) + [tap, bias],
        out_specs=[pl.BlockSpec((4, t, cb), lambda j: (0, 0, j)), tap, bias],
        out_shape=[_sds((4, t, w), BF16), _sds((3, w), F32), _sds((1, w), F32)],
        name=name, compiler_params=_cparams("parallel"),
    )(da, p4, p4, p4, p4, dw, db)


def _attn_mask():
    qn, kn = Q_ROWS * GRID_W, K_ROWS * GRID_W
    qr, qc = np.divmod(np.arange(qn), GRID_W)
    kr, kc = np.divmod(np.arange(kn), GRID_W)
    col0 = np.clip(qc - WIN_COLS // 2, 0, GRID_W - WIN_COLS)
    col_ok = (kc[None, :] >= col0[:, None]) & (kc[None, :] < col0[:, None] + WIN_COLS)
    first = np.zeros(qn, np.int64)
    last = np.full(qn, K_ROWS - WIN_ROWS)
    out = []
    for row0 in (first, qr, last):
        row_ok = (kr[None, :] >= row0[:, None]) & (kr[None, :] < row0[:, None] + WIN_ROWS)
        out.append(np.where(row_ok & col_ok, 0.0, NEG))
    return jnp.asarray(np.stack(out), F32)


def _rpb_onehot():
    qc, kc = np.divmod(np.arange(GRID_W * GRID_W), GRID_W)
    e = (kc - qc + WIN_COLS - 1)[None, :] == np.arange(128)[:, None]
    return jnp.asarray(e, F32)


_KW = K_ROWS * GRID_W
_QB = Q_ROWS * GRID_W
_T3_FRONT = 4
_T3_LANES = 1536
_PAIR = 2 * HEAD_DIM
_BIAS_BASE = (WIN_ROWS - 1 + _T3_FRONT, WIN_ROWS // 2 - 1 + _T3_FRONT, _T3_FRONT - 1)


class _Comm:
    def __init__(self, ins, outs, sems, start, finish):
        self.ins, self.outs, self.sems, self.start, self.finish = list(ins), list(outs), list(sems), start, finish


def _bias_pieces(cls):
    out = []
    for qr in range(Q_ROWS):
        off = (_BIAS_BASE[cls] - qr) * GRID_W
        out.append((qr, off % 128 != 0, off - (off % 128)))
    return out


def _block_class(b, nblk, fn, entering=False):
    interior = (b == 1) if entering else jnp.logical_and(b > 0, b < nblk - 1)
    for cls, cond in enumerate((b == 0, interior, b == nblk - 1)):
        pl.when(cond)(functools.partial(fn, cls))


def _attn_geometry(p4, nx):
    rows = p4.shape[0]
    w = p4.shape[1] // 4
    nhp = w // _PAIR
    nblk = nx // _QB
    qspec = lambda col: pl.BlockSpec((_QB, _PAIR), lambda hp, b: (b, col * nhp + hp))
    kspec = lambda col: pl.BlockSpec((rows, _PAIR), lambda hp, b: (0, col * nhp + hp))
    tspec = pl.BlockSpec((2, GRID_W, _T3_LANES), lambda hp, b: (hp, 0, 0))
    mspec = pl.BlockSpec((None, _QB, _KW), lambda hp, b: (jnp.where(b == 0, 0, jnp.where(b == nblk - 1, 2, 1)), 0, 0))
    lspec = pl.BlockSpec((None, _QB, 2), lambda hp, b: (hp, b, 0))
    ospec = pl.BlockSpec((_QB, _PAIR), lambda hp, b: (b, hp))
    return rows, w, nhp, nblk, qspec, kspec, tspec, mspec, lspec, ospec


def _window_start(b, nx):
    return pl.multiple_of(jnp.clip(b * _QB - PAD_ROWS * GRID_W, 0, nx - _KW), _QB)


def _load_bias(bias_ref, t3_ref, t3s_ref, m_ref, b, nblk):
    def fill(cls):
        for h in range(2):
            for qr, shifted, off in _bias_pieces(cls):
                src = t3s_ref if shifted else t3_ref
                rows = slice(qr * GRID_W, (qr + 1) * GRID_W)
                bias_ref[h, rows, :] = src[h, :, off:off + _KW] + m_ref[rows, :]

    _block_class(b, nblk, fill, entering=True)


def _attn_fwd(p4, t3, t3s, mask, nx, name, comm=None):
    rows, w, nhp, nblk, qspec, kspec, tspec, mspec, lspec, ospec = _attn_geometry(p4, nx)
    n_ctx = rows - nx
    n_cin, n_cout = (len(comm.ins), len(comm.outs)) if comm else (0, 0)

    def body(*refs):
        q_ref, k_ref, v_ref, g_ref, t3_ref, t3s_ref, m_ref = refs[:7]
        cin = refs[7:7 + n_cin]
        a_ref, o_ref, lse_ref = refs[7 + n_cin:10 + n_cin]
        cout = refs[10 + n_cin:10 + n_cin + n_cout]
        bias_ref = refs[10 + n_cin + n_cout]
        sems = refs[11 + n_cin + n_cout:]
        hp, b = pl.program_id(0), pl.program_id(1)
        if comm:
            pl.when(jnp.logical_and(hp == 0, b == 0))(lambda: comm.start(cin, cout, sems))
        start = _window_start(b, nx)
        _load_bias(bias_ref, t3_ref, t3s_ref, m_ref, b, nblk)
        qf = q_ref[...].astype(F32) * HEAD_DIM ** -0.5
        kw = k_ref[pl.ds(start, _KW), :].astype(BF16)
        vw = v_ref[pl.ds(start, _KW), :].astype(BF16)
        kcv = k_ref[pl.ds(nx, n_ctx), :].astype(BF16)
        vcv = v_ref[pl.ds(nx, n_ctx), :].astype(BF16)
        lane = lax.broadcasted_iota(jnp.int32, (1, _PAIR), 1)
        outs, lses = [], []
        for h in range(2):
            mine = (lane >= HEAD_DIM) if h else (lane < HEAD_DIM)
            qm = jnp.where(mine, qf, 0.0).astype(BF16)
            s_loc = lax.dot_general(qm, kw, _DIMS["nt"], preferred_element_type=F32) + bias_ref[h]
            s_ctx = lax.dot_general(qm, kcv, _DIMS["nt"], preferred_element_type=F32)
            mx = jnp.maximum(jnp.max(s_loc, axis=-1, keepdims=True), jnp.max(s_ctx, axis=-1, keepdims=True))
            p_loc = jnp.exp(s_loc - mx)
            p_ctx = jnp.exp(s_ctx - mx)
            den = jnp.sum(p_loc, axis=-1, keepdims=True) + jnp.sum(p_ctx, axis=-1, keepdims=True)
            o = jnp.dot(p_loc.astype(BF16), vw, preferred_element_type=F32)
            o = o + jnp.dot(p_ctx.astype(BF16), vcv, preferred_element_type=F32)
            outs.append(o * (1.0 / den))
            lses.append(mx + jnp.log(den))
        o = jnp.where(lane < HEAD_DIM, outs[0], outs[1])
        o_ref[...] = o.astype(ACT)
        a_ref[...] = (o * _silu(g_ref[...].astype(F32))).astype(BF16)
        col = lax.broadcasted_iota(jnp.int32, (1, 2), 1)
        lse_ref[...] = jnp.where(col == 0, lses[0], lses[1])
        if comm:
            pl.when(jnp.logical_and(hp == nhp - 1, b == nblk - 1))(lambda: comm.finish(cin, cout, sems))

    res = pl.pallas_call(
        body, grid=(nhp, nblk),
        in_specs=[qspec(0), kspec(1), kspec(2), qspec(3), tspec, tspec, mspec] + [HBM_SPEC] * n_cin,
        out_specs=[ospec, ospec, lspec] + [HBM_SPEC] * n_cout,
        out_shape=[_sds((nx, w), BF16), _sds((nx, w), ACT), _sds((nhp, nx, 2), F32)] + (comm.outs if comm else []),
        scratch_shapes=[pltpu.VMEM((2, _QB, _KW), F32)] + (comm.sems if comm else []),
        name=name, compiler_params=_cparams("arbitrary", "arbitrary"),
    )(p4, p4, p4, p4, t3, t3s, mask, *(comm.ins if comm else []))
    return res[:3], res[3:]


def _attn_bwd(p4, t3, t3s, mask, o, lse, da, nx, name, comm=None):
    rows, w, nhp, nblk, qspec, kspec, tspec, mspec, lspec, ospec = _attn_geometry(p4, nx)
    n_ctx = rows - nx
    n_cin, n_cout = (len(comm.ins), len(comm.outs)) if comm else (0, 0)

    def body(*refs):
        q_ref, k_ref, v_ref, g_ref, t3_ref, t3s_ref, m_ref, o_ref, lse_ref, da_ref = refs[:10]
        cin = refs[10:10 + n_cin]
        d4_ref, dt3_ref, dt3s_ref = refs[10 + n_cin:13 + n_cin]
        cout = refs[13 + n_cin:13 + n_cin + n_cout]
        bias_ref, ds_ref, dk_ref, dv_ref = refs[13 + n_cin + n_cout:17 + n_cin + n_cout]
        sems = refs[17 + n_cin + n_cout:]
        hp, b = pl.program_id(0), pl.program_id(1)
        if comm:
            pl.when(jnp.logical_and(hp == 0, b == 0))(lambda: comm.start(cin, cout, sems))
        start = _window_start(b, nx)
        here = pl.multiple_of(b * _QB, _QB)

        @pl.when(b == 0)
        def _():
            dk_ref[...] = jnp.zeros(dk_ref.shape, F32)
            dv_ref[...] = jnp.zeros(dv_ref.shape, F32)
            dt3_ref[...] = jnp.zeros(dt3_ref.shape, F32)
            dt3s_ref[...] = jnp.zeros(dt3s_ref.shape, F32)
            d4_ref[0, pl.ds(nx, n_ctx), :] = jnp.zeros((n_ctx, _PAIR), BF16)
            d4_ref[3, pl.ds(nx, n_ctx), :] = jnp.zeros((n_ctx, _PAIR), BF16)

        _load_bias(bias_ref, t3_ref, t3s_ref, m_ref, b, nblk)
        gv = g_ref[...].astype(F32)
        dav = da_ref[...].astype(F32)
        ov = o_ref[...].astype(F32)
        dov = dav * _silu(gv)
        d4_ref[3, pl.ds(here, _QB), :] = (dav * ov * _dsilu(gv)).astype(BF16)
        qf = q_ref[...].astype(F32) * HEAD_DIM ** -0.5
        kw = k_ref[pl.ds(start, _KW), :].astype(BF16)
        vw = v_ref[pl.ds(start, _KW), :].astype(BF16)
        kcv = k_ref[pl.ds(nx, n_ctx), :].astype(BF16)
        vcv = v_ref[pl.ds(nx, n_ctx), :].astype(BF16)
        lane = lax.broadcasted_iota(jnp.int32, (1, _PAIR), 1)
        dq = jnp.zeros((_QB, _PAIR), F32)
        for h in range(2):
            mine = (lane >= HEAD_DIM) if h else (lane < HEAD_DIM)
            qm = jnp.where(mine, qf, 0.0).astype(BF16)
            dom = jnp.where(mine, dov, 0.0)
            dob = dom.astype(BF16)
            lse = lse_ref[:, h:h + 1]
            s_loc = lax.dot_general(qm, kw, _DIMS["nt"], preferred_element_type=F32)
            p_loc = jnp.exp(s_loc + bias_ref[h] - lse)
            p_ctx = jnp.exp(lax.dot_general(qm, kcv, _DIMS["nt"], preferred_element_type=F32) - lse)
            delta = jnp.sum(dom * ov, axis=-1, keepdims=True)
            ds_loc = p_loc * (lax.dot_general(dob, vw, _DIMS["nt"], preferred_element_type=F32) - delta)
            ds_ctx = p_ctx * (lax.dot_general(dob, vcv, _DIMS["nt"], preferred_element_type=F32) - delta)
            dsb_loc = ds_loc.astype(BF16)
            dsb_ctx = ds_ctx.astype(BF16)
            dq_h = (jnp.dot(dsb_loc, kw, preferred_element_type=F32)
                    + jnp.dot(dsb_ctx, kcv, preferred_element_type=F32))
            dq = dq + jnp.where(mine, dq_h, 0.0)
            dk_ref[pl.ds(start, _KW), :] += lax.dot_general(dsb_loc, qm, _DIMS["tn"], preferred_element_type=F32)
            dv_ref[pl.ds(start, _KW), :] += lax.dot_general(p_loc.astype(BF16), dob, _DIMS["tn"],
                                                            preferred_element_type=F32)
            dk_ref[pl.ds(nx, n_ctx), :] += lax.dot_general(dsb_ctx, qm, _DIMS["tn"], preferred_element_type=F32)
            dv_ref[pl.ds(nx, n_ctx), :] += lax.dot_general(p_ctx.astype(BF16), dob, _DIMS["tn"],
                                                           preferred_element_type=F32)
            ds_ref[h] = ds_loc
        d4_ref[0, pl.ds(here, _QB), :] = (dq * HEAD_DIM ** -0.5).astype(BF16)

        def scatter(cls):
            for h in range(2):
                for qr, shifted, off in _bias_pieces(cls):
                    dst = dt3s_ref if shifted else dt3_ref
                    dst[h, :, off:off + _KW] += ds_ref[h, qr * GRID_W:(qr + 1) * GRID_W, :]

        _block_class(b, nblk, scatter)

        @pl.when(b == nblk - 1)
        def _():
            d4_ref[1] = dk_ref[...].astype(BF16)
            d4_ref[2] = dv_ref[...].astype(BF16)

        if comm:
            pl.when(jnp.logical_and(hp == nhp - 1, b == nblk - 1))(lambda: comm.finish(cin, cout, sems))

    tshape = _sds(t3.shape, F32)
    res = pl.pallas_call(
        body, grid=(nhp, nblk),
        in_specs=[qspec(0), kspec(1), kspec(2), qspec(3), tspec, tspec, mspec, ospec, lspec, ospec] + [HBM_SPEC] * n_cin,
        out_specs=[pl.BlockSpec((4, rows, _PAIR), lambda hp, b: (0, 0, hp)), tspec, tspec] + [HBM_SPEC] * n_cout,
        out_shape=[_sds((4, rows, w), BF16), tshape, tshape] + (comm.outs if comm else []),
        scratch_shapes=[pltpu.VMEM((2, _QB, _KW), F32), pltpu.VMEM((2, _QB, _KW), F32),
                        pltpu.VMEM((rows, _PAIR), F32), pltpu.VMEM((rows, _PAIR), F32)] + (comm.sems if comm else []),
        name=name, compiler_params=_cparams("arbitrary", "arbitrary"),
    )(p4, p4, p4, p4, t3, t3s, mask, o, lse, da, *(comm.ins if comm else []))
    return res[:3], res[3:]


def _final(x, g, target, name):
    rows, d = x.shape
    tr = ROW_BLOCK
    nblk = rows // tr

    def body(x_ref, g_ref, t_ref, loss_ref, dx_ref, dg_ref, acc_ref):
        i = pl.program_id(0)
        xv = x_ref[...]
        gv = g_ref[...]
        r = lax.rsqrt(jnp.mean(xv * xv, axis=-1, keepdims=True) + EPS)
        xn = xv * r
        err = xn * gv - t_ref[...]
        dy = err * (1.0 / d)
        dxn = dy * gv
        dx_ref[...] = r * (dxn - xn * jnp.mean(dxn * xn, axis=-1, keepdims=True))
        s_g = jnp.sum(dy * xn, axis=0, keepdims=True)
        s_l = jnp.sum(jnp.mean(err * err, axis=-1, keepdims=True), axis=0, keepdims=True)

        @pl.when(i == 0)
        def _():
            dg_ref[...] = s_g
            acc_ref[...] = s_l

        @pl.when(i > 0)
        def _():
            dg_ref[...] += s_g
            acc_ref[...] += s_l

        @pl.when(i == nblk - 1)
        def _():
            loss_ref[...] = jnp.broadcast_to(0.5 * acc_ref[...], loss_ref.shape)

    row = pl.BlockSpec((tr, d), lambda i: (i, 0))
    vec = pl.BlockSpec((1, d), lambda i: (0, 0))
    return pl.pallas_call(
        body, grid=(nblk,), in_specs=[row, vec, row],
        out_specs=[pl.BlockSpec((1, 128), lambda i: (0, 0)), row, vec],
        out_shape=[_sds((1, 128), F32), _sds((rows, d), F32), _sds((1, d), F32)],
        scratch_shapes=[pltpu.VMEM((1, 1), F32)], name=name, compiler_params=_cparams("arbitrary"),
    )(x, g, target)


def _as2d(a):
    if a.ndim == 1:
        return a.reshape(-1, 128) if a.shape[0] % 128 == 0 else a.reshape(1, -1)
    return a.reshape(-1, a.shape[-1])


def _adamw(w, g, m, v, name):
    shape = w.shape
    w2, g2, m2, v2 = (_as2d(t) for t in (w, g.reshape(shape), m, v))
    rows, cols = w2.shape
    tr = 512 if rows % 512 == 0 else rows
    c1 = 1.0 - ADAM_B1 ** ADAM_STEP
    c2 = 1.0 - ADAM_B2 ** ADAM_STEP

    def body(w_ref, g_ref, m_ref, v_ref, d_ref, nm_ref, nv_ref):
        gv = g_ref[...]
        nm = ADAM_B1 * m_ref[...] + (1.0 - ADAM_B1) * gv
        nv = ADAM_B2 * v_ref[...] + (1.0 - ADAM_B2) * (gv * gv)
        nm_ref[...] = nm
        nv_ref[...] = nv
        d_ref[...] = -ADAM_LR * ((nm / c1) / (jnp.sqrt(nv / c2) + ADAM_EPS) + ADAM_WD * w_ref[...])

    blk = pl.BlockSpec((tr, cols), lambda i: (i, 0))
    outs = pl.pallas_call(
        body, grid=(rows // tr,), in_specs=[blk] * 4, out_specs=[blk] * 3,
        out_shape=[_sds((rows, cols), F32)] * 3, name=name, compiler_params=_cparams("parallel"),
    )(w2, g2, m2, v2)
    return tuple(t.reshape(shape) for t in outs)


def _sum_lead(x, name, out_dtype=F32):
    n, rows, cols = x.shape
    tr = 512 if rows % 512 == 0 else rows

    def body(x_ref, o_ref):
        acc = x_ref[0].astype(F32)
        for k in range(1, n):
            acc = acc + x_ref[k].astype(F32)
        o_ref[...] = acc.astype(out_dtype)

    return pl.pallas_call(
        body, grid=(rows // tr,), in_specs=[pl.BlockSpec((n, tr, cols), lambda i: (0, i, 0))],
        out_specs=pl.BlockSpec((tr, cols), lambda i: (i, 0)), out_shape=_sds((rows, cols), out_dtype),
        name=name, compiler_params=_cparams("parallel"),
    )(x)


_NO_CTX = 1 << 30


def _seg_vecs(mod_l, which, nseg):
    return mod_l[:nseg, which][:, None, :]


def _norm_grads(dshift, dgeff, dgate, g, scale):
    nseg, _, d = dshift.shape
    dmod = jnp.stack([dshift[:, 0], dgeff[:, 0] * g, dgate[:, 0]], axis=1)
    if nseg == 1:
        dmod = jnp.concatenate([dmod, jnp.zeros((1, 3, d), F32)], axis=0)
    dg = jnp.sum(dgeff[:, 0] * (1.0 + scale[:, 0]), axis=0)
    return dmod, dg


def _pool_layer(xin, g, mod_l, w_in, w_grp, w_out, pscale, nx, tag):
    rows = xin.shape[0]
    nseg = 2 if rows > nx else 1
    nxb = nx // ROW_BLOCK if nseg == 2 else _NO_CTX
    shift, scale, gate = (_seg_vecs(mod_l, k, nseg) for k in range(3))
    h, r, uv = _norm_w_in(xin, g, scale, shift, w_in, nx, f"w_in_fwd_{tag}")
    z, mixed, a = _pool_grp_fwd(uv, w_grp, pscale, nx, f"pool_fwd_{tag}")
    yx, xout = _mm_out_resid(a, w_out, xin, gate, nxb, f"w_out_fwd_{tag}")

    def backward(dxo):
        dyx, da, dgate = _gate_w_out_bwd(dxo, yx, gate, w_out, nx, f"w_out_bwd_{tag}")
        gw_out = _mm_tn(a, dyx, f"w_out_grad_{tag}", BF16)
        dm, duv, dscale = _pool_grp_bwd(da, mixed, uv, pscale, w_grp, nx, f"pool_bwd_{tag}")
        gw_grp = _grp_wgrad(z, dm, w_grp.shape[0], f"grp_grad_{tag}", BF16)
        gw_in = _mm_tn_parts(h, duv, f"w_in_grad_{tag}", BF16)
        dx, dshift, dgeff = _w_in_bwd_norm(duv, w_in, xin, r, g, scale, dxo, nx, f"w_in_bwd_{tag}")
        dmod, dg = _norm_grads(dshift, dgeff, dgate, g[0], scale)
        return dx, dmod, dg, dict(w_in=gw_in, w_grp=gw_grp, w_out=gw_out, scale=dscale)

    return xout, backward


def _rpb_tables(rpb, onehot):
    nh, na, nb = rpb.shape
    flat = jnp.pad(rpb.reshape(nh * na, nb), ((0, 0), (0, 128 - nb)))
    t1 = _matmul(
        flat, onehot, mode="nn", grid=(1, 4), exact=True,
        a_spec=pl.BlockSpec((nh * na, 128), lambda i, j: (0, 0)), b_spec=pl.BlockSpec((128, 1024), lambda i, j: (0, j)),
        out_shapes=[_sds((nh * na, GRID_W * GRID_W), F32)], out_specs=[pl.BlockSpec((nh * na, 1024), lambda i, j: (0, j))],
        name="rpb_table")[0]
    t3 = t1.reshape(nh, na, GRID_W, GRID_W).transpose(0, 2, 1, 3).reshape(nh, GRID_W, na * GRID_W)
    front = _T3_FRONT * GRID_W
    back = _T3_LANES - na * GRID_W - front
    return (jnp.pad(t3, ((0, 0), (0, 0), (front, back))),
            jnp.pad(t3, ((0, 0), (0, 0), (front - GRID_W, back + GRID_W))))


def _rpb_grad(dt3, dt3s, onehot, nh, na, nb):
    def fold(t, front, name):
        flat = t[:, :, front:front + na * GRID_W].reshape(nh, GRID_W, na, GRID_W).transpose(0, 2, 1, 3)
        flat = flat.reshape(nh * na, -1)
        out = _matmul(
            flat, onehot, mode="nt", grid=(1, 4), nk=4, acc_shape=(nh * na, 128), exact=True,
            a_spec=pl.BlockSpec((nh * na, 1024), lambda i, k: (0, k)), b_spec=pl.BlockSpec((128, 1024), lambda i, k: (0, k)),
            out_shapes=[_sds((nh * na, 128), F32)], out_specs=[pl.BlockSpec((nh * na, 128), lambda i, k: (0, 0))],
            name=name)[0]
        return out[:, :nb].reshape(nh, na, nb)

    front = _T3_FRONT * GRID_W
    return fold(dt3, front, "rpb_grad_a") + fold(dt3s, front - GRID_W, "rpb_grad_b")


def _na_layer(xc, g, mod_l, w_in, rpb, w_out, nx, consts, comm=None):
    mask, onehot = consts
    shift, scale = _seg_vecs(mod_l, 0, 2), _seg_vecs(mod_l, 1, 2)
    gate = _seg_vecs(mod_l, 2, 1)
    h, r, p4 = _norm_w_in(xc, g, scale, shift, w_in, nx, "w_in_fwd_na")
    t3, t3s = _rpb_tables(rpb, onehot)
    (a, o, lse), carried = _attn_fwd(p4, t3, t3s, mask, nx, "attn_fwd", comm)
    yx, xout = _mm_out_resid(a, w_out, xc, gate, _NO_CTX, "w_out_fwd_na")

    def backward(dxo, comm=None):
        dyx, da, dgate = _gate_w_out_bwd(dxo, yx, gate, w_out, nx, "w_out_bwd_na")
        gw_out = _mm_tn(a, dyx, "w_out_grad_na", BF16)
        (d4, dt3, dt3s), carried_bwd = _attn_bwd(p4, t3, t3s, mask, o, lse, da, nx, "attn_bwd", comm)
        gw_in = _mm_tn_parts(h, d4, "w_in_grad_na", BF16)
        dx, dshift, dgeff = _w_in_bwd_norm(d4, w_in, xc, r, g, scale, dxo, nx, "w_in_bwd_na")
        dgate2 = jnp.concatenate([dgate, jnp.zeros_like(dgate)], axis=0)
        dmod, dg = _norm_grads(dshift, dgeff, dgate2, g[0], scale)
        drpb = _rpb_grad(dt3, dt3s, onehot, *rpb.shape)
        return dx, dmod, dg, dict(w_in=gw_in, w_out=gw_out, rpb=drpb), carried_bwd

    return xout, backward, carried


def _conv_layer(xin, g, mod_l, w_in, dw, db, w_out):
    shift, scale, gate = (_seg_vecs(mod_l, k, 1) for k in range(3))
    nx = xin.shape[0]
    h, r, p4 = _norm_w_in(xin, g, scale, shift, w_in, nx, "w_in_fwd_conv")
    a = _conv_fwd(p4, dw, db, "conv_fwd")
    yx, xout = _mm_out_resid(a, w_out, xin, gate, _NO_CTX, "w_out_fwd_conv")

    def backward(dxo):
        dyx, da, dgate = _gate_w_out_bwd(dxo, yx, gate, w_out, nx, "w_out_bwd_conv")
        gw_out = _mm_tn(a, dyx, "w_out_grad_conv", BF16)
        d4, ddw, ddb = _conv_bwd(da, p4, dw, db, "conv_bwd")
        gw_in = _mm_tn_parts(h, d4, "w_in_grad_conv", BF16)
        dx, dshift, dgeff = _w_in_bwd_norm(d4, w_in, xin, r, g, scale, dxo, nx, "w_in_bwd_conv")
        dmod, dg = _norm_grads(dshift, dgeff, dgate, g[0], scale)
        return dx, dmod, dg, dict(w_in=gw_in, w_out=gw_out, dw=ddw, db=ddb)

    return xout, backward


def _example_step(x, ctx, target, mod, norm_g, final_g, wts, late_comm=None, late_weights=None, grad_comm=None):
    nx = x.shape[0]
    consts = (_attn_mask(), _rpb_onehot())
    g_rows = [norm_g[i:i + 1] for i in range(4)]
    xc0 = jnp.concatenate([x, ctx], axis=0)
    xc1, bwd0 = _pool_layer(xc0, g_rows[0], mod[0], wts["pool_w_in"][0], wts["pool_w_grp"][0], wts["pool_w_out"][0],
                            wts["pool_scale"][0:1], nx, "p0")
    x2, bwd1, carried = _na_layer(xc1, g_rows[1], mod[1], wts["na_w_in"], wts["na_rpb"], wts["na_w_out"], nx, consts,
                                  late_comm)
    if late_weights is not None:
        wts = {**wts, **late_weights(carried)}
    x3, bwd2 = _conv_layer(x2, g_rows[2], mod[2], wts["conv_w_in"], wts["conv_dw"], wts["conv_db"], wts["conv_w_out"])
    x4, bwd3 = _pool_layer(x3, g_rows[3], mod[3], wts["pool_w_in"][1], wts["pool_w_grp"][1], wts["pool_w_out"][1],
                           wts["pool_scale"][1:2], nx, "p3")
    loss, dx4, dfinal_g = _final(x4, final_g, target, "loss_head")
    dx3, dmod3, dg3, gr3 = bwd3(dx4)
    dx2, dmod2, dg2, gr2 = bwd2(dx3)
    dxc1, dmod1, dg1, gr1, carried_bwd = bwd1(dx2, grad_comm(gr3, gr2) if grad_comm else None)
    dxc0, dmod0, dg0, gr0 = bwd0(dxc1)
    return dict(
        loss=loss, grad_x=dxc0[:nx], dmod=jnp.stack([dmod0, dmod1, dmod2, dmod3]),
        dnorm_g=jnp.stack([dg0, dg1, dg2, dg3]), dfinal_g=dfinal_g, layers=(gr0, gr1, gr2, gr3), carried=carried_bwd)


_AXES = ("x", "y", "c")
_CHIP_FLIPS = ((1, 0), (0, 1), (1, 1))


def _position():
    return tuple(lax.axis_index(a) for a in _AXES)


def _flipped(pos, flip):
    return tuple(1 - p if f else p for p, f in zip(pos, flip))


def _all_gather(v, axes, name):
    flips = [f for f in np.ndindex(2, 2, 2) if any(f) and all(a in axes or not b for a, b in zip(_AXES, f))]
    n = len(flips) + 1

    def body(v_ref, o_ref, send_sems, recv_sems, local_sem):
        pos = _position()
        slot = 0
        for a, p in zip(_AXES, pos):
            if a in axes:
                slot = 2 * slot + p
        local = pltpu.make_async_copy(v_ref, o_ref.at[slot], local_sem)
        local.start()
        copies = []
        for k, flip in enumerate(flips):
            cp = pltpu.make_async_remote_copy(v_ref, o_ref.at[slot], send_sems.at[k], recv_sems.at[k],
                                              device_id=_flipped(pos, flip), device_id_type=MESH)
            cp.start()
            copies.append(cp)
        for cp in copies:
            cp.wait()
        local.wait()

    return pl.pallas_call(
        body, in_specs=[HBM_SPEC], out_specs=HBM_SPEC, out_shape=_sds((n,) + v.shape, v.dtype),
        scratch_shapes=[pltpu.SemaphoreType.DMA((n - 1,)), pltpu.SemaphoreType.DMA((n - 1,)), pltpu.SemaphoreType.DMA(())],
        name=name,
    )(v)


class _Item:
    def __init__(self, key, layer, shape, shard_axis, half_axis):
        self.key, self.layer, self.shape = key, layer, tuple(shape)
        self.shard_axis, self.half_axis = shard_axis, half_axis
        self.shard = shape[shard_axis] // 4
        self.half = shape[half_axis] // 2

    def sized(self, shard=False, half=False):
        s = list(self.shape)
        if shard:
            s[self.shard_axis] = self.shard
        if half:
            s[self.half_axis] = self.half
        return tuple(s)

    def window(self, ref, chip=None, half=None):
        idx = [slice(None)] * len(self.shape)
        if chip is not None:
            idx[self.shard_axis] = pl.ds(chip * self.shard, self.shard)
        if half is not None:
            idx[self.half_axis] = pl.ds(half * self.half, self.half)
        return ref.at[tuple(idx)]


def _items(d, w):
    out = []
    for j in range(2):
        out += [_Item("pool_w_in", j, (d, 2 * w), 1, 0), _Item("pool_w_grp", j, (4, w // 4, w // 4), 1, 0),
                _Item("pool_w_out", j, (w, d), 0, 1)]
    out += [_Item("na_w_in", 0, (d, 4 * w), 1, 0), _Item("na_w_out", 0, (w, d), 0, 1),
            _Item("conv_w_in", 0, (d, 4 * w), 1, 0), _Item("conv_w_out", 0, (w, d), 0, 1)]
    return out


def _gather_weights(shards, items, name):
    comm = _gather_comm(shards, items)

    def body(*refs):
        n = len(items)
        comm.start(refs[:n], refs[n:2 * n], refs[2 * n:])
        comm.finish(refs[:n], refs[n:2 * n], refs[2 * n:])

    return pl.pallas_call(
        body, in_specs=[HBM_SPEC] * len(items), out_specs=[HBM_SPEC] * len(items), out_shape=comm.outs,
        scratch_shapes=comm.sems, name=name,
    )(*shards)


def _gather_comm(shards, items):
    n = len(items)

    def copies(src, dst, sems, onward):
        send_a, recv_a, send_b, recv_b, send_c, recv_c = sems
        x, y, c = _position()
        chip = 2 * x + y
        sibling = (x, y, 1 - c)
        own, out, fwd, fwd_in = [], [], [], []
        for i, it in enumerate(items):
            own.append(pltpu.make_async_remote_copy(src[i], it.window(dst[i], chip=chip), send_c.at[i], recv_c.at[i],
                                                    device_id=sibling, device_id_type=MESH))
            for k, flip in enumerate(_CHIP_FLIPS):
                px, py = _flipped((x, y), flip)
                s = 3 * i + k
                out.append(pltpu.make_async_remote_copy(
                    it.window(src[i], half=c), it.window(dst[i], chip=chip, half=c), send_a.at[s], recv_a.at[s],
                    device_id=(px, py, c), device_id_type=MESH))
                if onward:
                    got = it.window(dst[i], chip=2 * px + py, half=c)
                    fwd.append(pltpu.make_async_remote_copy(got, got, send_b.at[s], recv_b.at[s],
                                                            device_id=sibling, device_id_type=MESH))
                    other = it.window(dst[i], chip=2 * px + py, half=1 - c)
                    fwd_in.append(pltpu.make_async_remote_copy(other, other, send_b.at[s], recv_b.at[s],
                                                               device_id=sibling, device_id_type=MESH))
        return own, out, fwd, fwd_in

    def start(src, dst, sems):
        own, out, _, _ = copies(src, dst, sems, False)
        for cp in own + out:
            cp.start()

    def finish(src, dst, sems):
        own, out, fwd, fwd_in = copies(src, dst, sems, True)
        for arrived, onward in zip(out, fwd):
            arrived.wait_recv()
            onward.start()
        for cp in fwd_in:
            cp.wait_recv()
        for cp in out + fwd:
            cp.wait_send()
        for cp in own:
            cp.wait()

    sems = [pltpu.SemaphoreType.DMA((3 * n,)) for _ in range(4)] + [pltpu.SemaphoreType.DMA((n,)) for _ in range(2)]
    return _Comm(shards, [_sds(it.shape, BF16) for it in items], sems, start, finish)


def _pair_swap(arrays, windows, out_shapes, name):
    n = len(arrays)

    def body(*refs):
        src, got = refs[:n], refs[n:2 * n]
        send_sems, recv_sems = refs[2 * n:]
        x, y, c = _position()
        copies = []
        for i in range(n):
            cp = pltpu.make_async_remote_copy(windows[i](src[i], 1 - c), got[i], send_sems.at[i], recv_sems.at[i],
                                              device_id=(x, y, 1 - c), device_id_type=MESH)
            cp.start()
            copies.append(cp)
        for cp in copies:
            cp.wait()

    return pl.pallas_call(
        body, in_specs=[HBM_SPEC] * n, out_specs=[HBM_SPEC] * n, out_shape=list(out_shapes),
        scratch_shapes=[pltpu.SemaphoreType.DMA((n,)), pltpu.SemaphoreType.DMA((n,))], name=name,
    )(*arrays)


def _chip_exchange(partials, items, name):
    comm = _chip_exchange_comm(partials, items)

    def body(*refs):
        n = len(items)
        comm.start(refs[:n], refs[n:2 * n], refs[2 * n:])
        comm.finish(refs[:n], refs[n:2 * n], refs[2 * n:])

    return pl.pallas_call(
        body, in_specs=[HBM_SPEC] * len(items), out_specs=[HBM_SPEC] * len(items), out_shape=comm.outs,
        scratch_shapes=comm.sems, name=name,
    )(*partials)


def _chip_exchange_comm(partials, items):
    n = len(items)

    def copies(src, dst, sems):
        send_sems, recv_sems = sems
        x, y, c = _position()
        out = []
        for i, it in enumerate(items):
            for k, flip in enumerate(_CHIP_FLIPS):
                px, py = _flipped((x, y), flip)
                out.append(pltpu.make_async_remote_copy(
                    it.window(src[i], chip=2 * px + py), dst[i].at[k], send_sems.at[3 * i + k],
                    recv_sems.at[3 * i + k], device_id=(px, py, c), device_id_type=MESH))
        return out

    def start(src, dst, sems):
        for cp in copies(src, dst, sems):
            cp.start()

    def finish(src, dst, sems):
        for cp in copies(src, dst, sems):
            cp.wait()

    return _Comm(partials, [_sds((3,) + it.sized(shard=True, half=True), BF16) for it in items],
                 [pltpu.SemaphoreType.DMA((3 * n,)), pltpu.SemaphoreType.DMA((3 * n,))], start, finish)


_SUM_STEPS = 2


def _pair_sums(gs, gots, its, pos, name):
    n = len(its)
    nb = _SUM_STEPS
    g2 = [g.reshape(-1, g.shape[-1]) for g in gs]
    got2 = [t.reshape(-1, t.shape[-1]) for t in gots]

    def body(pos_ref, *refs):
        for g_ref, got_ref, o_ref in zip(refs[:n], refs[n:2 * n], refs[2 * n:]):
            o_ref[...] = (g_ref[...].astype(F32) + got_ref[...].astype(F32)).astype(BF16)

    g_specs, got_specs = [], []
    for it, t in zip(its, got2):
        rows, cols = t.shape
        blk = (rows // nb, cols)
        g_map = (lambda i, pos: (pos[1] * nb + i, 0)) if it.half_axis == 0 else (lambda i, pos: (i, pos[1]))
        g_specs.append(pl.BlockSpec(blk, g_map))
        got_specs.append(pl.BlockSpec(blk, lambda i, pos: (i, 0)))
    outs = pl.pallas_call(
        body, grid_spec=pltpu.PrefetchScalarGridSpec(
            num_scalar_prefetch=1, grid=(nb,), in_specs=g_specs + got_specs, out_specs=got_specs),
        out_shape=[_sds(t.shape, BF16) for t in got2], name=name, compiler_params=_cparams("parallel"),
    )(pos, *g2, *got2)
    return [o.reshape(t.shape) for o, t in zip(outs, gots)]


_FLIP_SLOT = {2: 0, 1: 1, 3: 2}


def _chip_sums(pairs, slots, its, pos, name):
    n = len(its)
    nb = _SUM_STEPS

    def body(pos_ref, *refs):
        chip = pos_ref[0]
        for own in range(4):
            @pl.when(chip == own)
            def _():
                for p_ref, s_ref, o_ref in zip(refs[:n], refs[n:2 * n], refs[2 * n:]):
                    acc = None
                    for k in range(4):
                        v = (p_ref[...] if k == own else s_ref[_FLIP_SLOT[own ^ k]]).astype(F32)
                        acc = v if acc is None else acc + v
                    o_ref[...] = acc

    p_specs, s_specs, o_specs, shapes = [], [], [], []
    for it in its:
        shape = it.sized(shard=True, half=True)
        blk = (shape[0] // nb,) + shape[1:]
        rest = (0,) * (len(shape) - 1)

        def p_map(i, pos, it=it, nd=len(shape)):
            lead = i + (pos[0] * nb if it.shard_axis == 0 else 0)
            return (lead,) + tuple(pos[0] if ax == it.shard_axis else 0 for ax in range(1, nd))

        p_specs.append(pl.BlockSpec(blk, p_map))
        s_specs.append(pl.BlockSpec((3,) + blk, lambda i, pos, rest=rest: (0, i) + rest))
        o_specs.append(pl.BlockSpec(blk, lambda i, pos, rest=rest: (i,) + rest))
        shapes.append(_sds(shape, F32))
    return pl.pallas_call(
        body, grid_spec=pltpu.PrefetchScalarGridSpec(
            num_scalar_prefetch=1, grid=(nb,), in_specs=p_specs + s_specs, out_specs=o_specs),
        out_shape=shapes, name=name, compiler_params=_cparams("parallel"),
    )(pos, *pairs, *slots)


_GRAD_KEYS = ("pool_w_in", "pool_w_grp", "pool_w_out", "na_w_in", "na_w_out", "conv_w_in", "conv_w_out")


def _adamw_matrix(w, m, v, owns, others, it, pos, name):
    nl = w.shape[0]
    rows_split = it.half_axis == 0
    r, cdim = int(np.prod(w.shape[1:-1])), w.shape[-1]
    hr, hc = (r // 2, cdim) if rows_split else (r, cdim // 2)
    br = min(hr, 256)
    nb = hr // br
    c1 = 1.0 - ADAM_B1 ** ADAM_STEP
    c2 = 1.0 - ADAM_B2 ** ADAM_STEP

    def body(pos_ref, w_ref, m_ref, v_ref, *rest):
        own_refs, other_refs = rest[:nl], rest[nl:2 * nl]
        g_ref, d_ref, nm_ref, nv_ref = rest[2 * nl:]
        j, h = pl.program_id(0), pl.program_id(1)
        own, other = own_refs[0][...], other_refs[0][...]
        for q in range(1, nl):
            own = jnp.where(j == q, own_refs[q][...], own)
            other = jnp.where(j == q, other_refs[q][...], other)
        gv = jnp.where(h == pos_ref[1], own, other)
        nm = ADAM_B1 * m_ref[...] + (1.0 - ADAM_B1) * gv
        nv = ADAM_B2 * v_ref[...] + (1.0 - ADAM_B2) * (gv * gv)
        g_ref[...] = gv
        nm_ref[...] = nm
        nv_ref[...] = nv
        d_ref[...] = -ADAM_LR * ((nm / c1) / (jnp.sqrt(nv / c2) + ADAM_EPS) + ADAM_WD * w_ref[...])

    if rows_split:
        full = pl.BlockSpec((None, br, hc), lambda j, h, i, pos: (j, h * nb + i, 0))
    else:
        full = pl.BlockSpec((None, br, hc), lambda j, h, i, pos: (j, i, h))
    half = pl.BlockSpec((br, hc), lambda j, h, i, pos: (i, 0))
    flat = lambda t: t.reshape(nl, r, cdim)
    outs = pl.pallas_call(
        body, grid_spec=pltpu.PrefetchScalarGridSpec(
            num_scalar_prefetch=1, grid=(nl, 2, nb), in_specs=[full] * 3 + [half] * (2 * nl), out_specs=[full] * 4),
        out_shape=[_sds((nl, r, cdim), F32)] * 4, name=name,
        compiler_params=_cparams("parallel", "parallel", "parallel"),
    )(pos, flat(w), flat(m), flat(v), *[t.reshape(hr, hc) for t in list(owns) + list(others)])
    return tuple(t.reshape(w.shape) for t in outs)


_WEIGHTS = ("c_ctx", "norm_g", "ada_w", "ada_b", "pool_w_in", "pool_w_grp", "pool_scale", "pool_w_out", "na_w_in",
            "na_rpb", "na_w_out", "conv_w_in", "conv_dw", "conv_db", "conv_w_out", "final_g")
_COND_ROWS = 16


def _modulations(cond, ada_w, ada_b_cols):
    nl, d, n = ada_w.shape
    return _matmul(
        cond, ada_w, mode="nn", grid=(nl, 1), a_silu=True, epilogue="bias",
        a_spec=pl.BlockSpec((_COND_ROWS, d), lambda i, j: (0, 0)), b_spec=pl.BlockSpec((None, d, n), lambda i, j: (i, 0, 0)),
        extra=(ada_b_cols,), extra_specs=(pl.BlockSpec((None, 1, n), lambda i, j: (i, 0, 0)),),
        out_shapes=[_sds((nl, _COND_ROWS, n), F32)], out_specs=[pl.BlockSpec((None, _COND_ROWS, n), lambda i, j: (i, 0, 0))],
        name="modulations")[0]


def _ada_w_grad(cond, dm_cols):
    d = cond.shape[1]
    nl, _, n = dm_cols.shape
    return _matmul(
        cond, dm_cols, mode="tn", grid=(nl, 1), a_silu=True,
        a_spec=pl.BlockSpec((_COND_ROWS, d), lambda i, j: (0, 0)), b_spec=pl.BlockSpec((None, _COND_ROWS, n), lambda i, j: (i, 0, 0)),
        out_shapes=[_sds((nl, d, n), F32)], out_specs=[pl.BlockSpec((None, d, n), lambda i, j: (i, 0, 0))],
        name="ada_w_grad")[0]


def _cond_grad(dm_cols, ada_w):
    nl, d, n = ada_w.shape
    return _matmul(
        dm_cols, ada_w, mode="nt", grid=(1, nl), nk=nl, acc_shape=(_COND_ROWS, d),
        a_spec=pl.BlockSpec((None, _COND_ROWS, n), lambda i, q: (q, 0, 0)), b_spec=pl.BlockSpec((None, d, n), lambda i, q: (q, 0, 0)),
        out_shapes=[_sds((_COND_ROWS, d), F32)], out_specs=[pl.BlockSpec((_COND_ROWS, d), lambda i, q: (0, 0))],
        name="cond_grad")[0]


def _pack(parts):
    flat = [p.reshape(-1) for p in parts]
    sizes = [f.shape[0] for f in flat]
    total = sum(sizes)
    rows = -(-total // 1024) * 8
    packed = jnp.concatenate(flat + [jnp.zeros((rows * 128 - total,), F32)]).reshape(rows, 128)
    offs = np.concatenate([[0], np.cumsum(sizes)])[:-1]
    return packed, [(int(o), p.shape) for o, p in zip(offs, parts)]


def _unpack(flat, layout, k):
    off, shape = layout[k]
    return flat[..., off:off + int(np.prod(shape))].reshape(flat.shape[:-1] + tuple(shape))


def kernel(x, c, ctx, c_ctx, norm_g, ada_w, ada_b, pool_w_in, pool_w_grp, pool_scale, pool_w_out, na_w_in, na_rpb, na_w_out, conv_w_in, conv_dw, conv_db, conv_w_out, final_g, loss_target, m_c_ctx, m_norm_g, m_ada_w, m_ada_b, m_pool_w_in, m_pool_w_grp, m_pool_scale, m_pool_w_out, m_na_w_in, m_na_rpb, m_na_w_out, m_conv_w_in, m_conv_dw, m_conv_db, m_conv_w_out, m_final_g, v_c_ctx, v_norm_g, v_ada_w, v_ada_b, v_pool_w_in, v_pool_w_grp, v_pool_scale, v_pool_w_out, v_na_w_in, v_na_rpb, v_na_w_out, v_conv_w_in, v_conv_dw, v_conv_db, v_conv_w_out, v_final_g):
    params = dict(c_ctx=c_ctx, norm_g=norm_g, ada_w=ada_w, ada_b=ada_b, pool_w_in=pool_w_in, pool_w_grp=pool_w_grp,
                  pool_scale=pool_scale, pool_w_out=pool_w_out, na_w_in=na_w_in, na_rpb=na_rpb, na_w_out=na_w_out,
                  conv_w_in=conv_w_in, conv_dw=conv_dw, conv_db=conv_db, conv_w_out=conv_w_out, final_g=final_g)
    mom1 = dict(c_ctx=m_c_ctx, norm_g=m_norm_g, ada_w=m_ada_w, ada_b=m_ada_b, pool_w_in=m_pool_w_in,
                pool_w_grp=m_pool_w_grp, pool_scale=m_pool_scale, pool_w_out=m_pool_w_out, na_w_in=m_na_w_in,
                na_rpb=m_na_rpb, na_w_out=m_na_w_out, conv_w_in=m_conv_w_in, conv_dw=m_conv_dw, conv_db=m_conv_db,
                conv_w_out=m_conv_w_out, final_g=m_final_g)
    mom2 = dict(c_ctx=v_c_ctx, norm_g=v_norm_g, ada_w=v_ada_w, ada_b=v_ada_b, pool_w_in=v_pool_w_in,
                pool_w_grp=v_pool_w_grp, pool_scale=v_pool_scale, pool_w_out=v_pool_w_out, na_w_in=v_na_w_in,
                na_rpb=v_na_rpb, na_w_out=v_na_w_out, conv_w_in=v_conv_w_in, conv_dw=v_conv_dw, conv_db=v_conv_db,
                conv_w_out=v_conv_w_out, final_g=v_final_g)
    d = x.shape[-1]
    w = na_w_out.shape[1] * 4
    xi, yi, ci = _position()
    chip = 2 * xi + yi
    dev = 2 * chip + ci
    n_ada = ada_w.shape[-1]

    def chip_cols(a, size):
        return lax.dynamic_slice_in_dim(a, chip * size, size, axis=a.ndim - 1)

    conds = _all_gather(c.reshape(8, d // 8), _AXES, "gather_cond").reshape(8, d)
    cond = jnp.concatenate([conds, c_ctx[None], jnp.zeros((_COND_ROWS - 9, d), F32)], axis=0)
    mod_cols = _modulations(cond, ada_w, chip_cols(ada_b, n_ada)[:, None, :])
    mod_all = _all_gather(mod_cols, ("x", "y"), "gather_mod")
    mod_all = mod_all.transpose(1, 2, 0, 3).reshape(4, _COND_ROWS, 3, d)
    mod = jnp.stack([lax.dynamic_index_in_dim(mod_all, dev, axis=1, keepdims=False), mod_all[:, 8]], axis=1)

    items = _items(d, w)
    early = [it for it in items if (it.key.startswith("pool") and it.layer == 0) or it.key.startswith("na")]
    late = [it for it in items if it not in early]
    shard_of = lambda it: params[it.key][it.layer].astype(BF16)
    full = {(it.key, it.layer): mat
            for it, mat in zip(early, _gather_weights([shard_of(it) for it in early], early, "gather_weights"))}
    late_comm = _gather_comm([shard_of(it) for it in late], late)

    def late_weights(mats):
        full.update({(it.key, it.layer): mat for it, mat in zip(late, mats)})
        return dict(pool_w_in=[full[("pool_w_in", j)] for j in range(2)],
                    pool_w_grp=[full[("pool_w_grp", j)] for j in range(2)],
                    pool_w_out=[full[("pool_w_out", j)] for j in range(2)],
                    conv_w_in=full[("conv_w_in", 0)], conv_w_out=full[("conv_w_out", 0)])

    small = _all_gather(_pack([pool_scale, conv_dw, conv_db])[0], ("x", "y"), "gather_small")
    small_layout = _pack([pool_scale, conv_dw, conv_db])[1]
    small = small.reshape(4, -1)

    def whole(k):
        parts = _unpack(small, small_layout, k)
        return jnp.moveaxis(parts, 0, -2).reshape(parts.shape[1:-1] + (-1,))

    wts = dict(pool_w_in=[full[("pool_w_in", 0)]], pool_w_grp=[full[("pool_w_grp", 0)]],
               pool_w_out=[full[("pool_w_out", 0)]], na_w_in=full[("na_w_in", 0)], na_w_out=full[("na_w_out", 0)],
               pool_scale=whole(0), na_rpb=na_rpb[0], conv_dw=whole(1)[0], conv_db=whole(2))
    pos = jnp.stack([chip, ci]).astype(jnp.int32)

    def layer_grads(its, by_layer):
        pick = {"pool_w_in": "w_in", "pool_w_grp": "w_grp", "pool_w_out": "w_out", "na_w_in": "w_in",
                "na_w_out": "w_out", "conv_w_in": "w_in", "conv_w_out": "w_out"}
        return [by_layer[(it.key.split("_")[0], it.layer)][pick[it.key]] for it in its]

    def pair_sums(its, mats, tag):
        got = _pair_swap(mats, [(lambda ref, half, it=it: it.window(ref, half=half)) for it in its],
                         [_sds(it.sized(half=True), BF16) for it in its], f"pair_exchange_{tag}")
        return _pair_sums(mats, got, its, pos, f"pair_sum_{tag}")

    pairs = dict()

    def grad_comm(gr3, gr2):
        pairs["late"] = pair_sums(late, layer_grads(late, {("pool", 1): gr3, ("conv", 0): gr2}), "late")
        return _chip_exchange_comm(pairs["late"], late)

    res = _example_step(x[0], ctx[0], loss_target[0], mod, norm_g, final_g[None], wts, late_comm, late_weights,
                        grad_comm)
    g0, g1, g2, g3 = res["layers"]
    pairs["early"] = pair_sums(early, layer_grads(early, {("pool", 0): g0, ("na", 0): g1}), "early")
    slots = dict(zip(late, res["carried"]))
    slots.update(zip(early, _chip_exchange(pairs["early"], early, "chip_exchange")))
    pair_of = dict(zip(late, pairs["late"]))
    pair_of.update(zip(early, pairs["early"]))
    reduced = _chip_sums([pair_of[it] for it in items], [slots[it] for it in items], items, pos, "chip_sum")
    theirs = _pair_swap(reduced, [lambda ref, half: ref] * len(items),
                        [_sds(t.shape, F32) for t in reduced], "pair_return")
    grads, matrix_out = dict(), dict()
    for k in _GRAD_KEYS:
        idx = [i for i, it in enumerate(items) if it.key == k]
        res_k = _adamw_matrix(params[k], mom1[k], mom2[k], [reduced[i] for i in idx], [theirs[i] for i in idx],
                              items[idx[0]], pos, f"adamw_{k}")
        grads[k], matrix_out[k] = res_k[0], res_k[1:]

    packed, layout = _pack([res["dfinal_g"], res["dnorm_g"], res["dmod"], g1["rpb"],
                            jnp.concatenate([g0["scale"], g3["scale"]], axis=0), g2["dw"], g2["db"],
                            res["loss"][0, :1]])
    every = _all_gather(packed, _AXES, "gather_vec_grads")
    total = _sum_lead(every, "sum_vec_grads").reshape(-1)
    every = every.reshape(8, -1)
    grads["final_g"] = _unpack(total, layout, 0).reshape(final_g.shape)
    grads["norm_g"] = _unpack(total, layout, 1)
    grads["na_rpb"] = _unpack(total, layout, 3)[None]
    grads["pool_scale"] = chip_cols(_unpack(total, layout, 4), pool_scale.shape[-1])
    grads["conv_dw"] = chip_cols(_unpack(total, layout, 5), conv_dw.shape[-1])[None]
    grads["conv_db"] = chip_cols(_unpack(total, layout, 6), conv_db.shape[-1])
    dmod_sum = _unpack(total, layout, 2).reshape(4, 2, 3 * d)
    dmod_each = _unpack(every, layout, 2).reshape(8, 4, 2, 3 * d)
    grads["ada_b"] = dmod_sum[:, 0] + dmod_sum[:, 1]
    dm = jnp.concatenate([dmod_each[:, :, 0].transpose(1, 0, 2), dmod_sum[:, 1][:, None],
                          jnp.zeros((4, _COND_ROWS - 9, 3 * d), F32)], axis=1)
    dm_cols = chip_cols(dm, n_ada)
    grads["ada_w"] = _ada_w_grad(cond, dm_cols)
    dcond = _cond_grad(dm_cols, ada_w)[8].reshape(8, d // 8)
    dcond = _sum_lead(_all_gather(dcond, ("x", "y"), "gather_cond_grad"), "sum_cond_grad").reshape(d)
    grads["c_ctx"] = dcond * _dsilu(c_ctx)

    outs = [[], [], []]
    for k in _WEIGHTS:
        step = matrix_out[k] if k in matrix_out else _adamw(params[k], grads[k], mom1[k], mom2[k], f"adamw_{k}")
        for lst, val in zip(outs, step):
            lst.append(val)
    loss = _unpack(total, layout, 7)[0]
    return (loss, res["grad_x"][None], *[grads[k].reshape(params[k].shape) for k in _WEIGHTS],
            *outs[0], *outs[1], *outs[2])
```

```python
import functools

import numpy as np
import jax
import jax.numpy as jnp
from jax import lax
from jax.experimental import pallas as pl
from jax.experimental.pallas import tpu as pltpu

F32 = jnp.float32
BF16 = jnp.bfloat16

EPS = 1e-6
GRID_W = 64
HEAD_DIM = 64
WIN_ROWS = 8
WIN_COLS = 16
POOL_WINDOWS = (2, 4, 8, 16)
Q_ROWS = 4
K_ROWS = 12
PAD_ROWS = 4
NEG = -1e30

ADAM_LR = 0.001
ADAM_B1 = 0.9
ADAM_B2 = 0.999
ADAM_EPS = 1e-08
ADAM_WD = 0.01
ADAM_STEP = 10

ROW_BLOCK = 256
VMEM_LIMIT = 56 * 1024 * 1024
ACT = BF16

MESH = pl.DeviceIdType.MESH
HBM_SPEC = pl.BlockSpec(memory_space=pltpu.HBM)


def _cparams(*sem):
    return pltpu.CompilerParams(dimension_semantics=sem or None, vmem_limit_bytes=VMEM_LIMIT)


def _sds(shape, dtype):
    return jax.ShapeDtypeStruct(tuple(shape), dtype)


def _sigmoid(x):
    return 1.0 / (1.0 + jnp.exp(-x))


def _silu(x):
    return x * _sigmoid(x)


def _dsilu(x):
    s = _sigmoid(x)
    return s * (1.0 + x * (1.0 - s))


_DIMS = {
    "nn": (((1,), (0,)), ((), ())),
    "nt": (((1,), (1,)), ((), ())),
    "tn": (((0,), (0,)), ((), ())),
}


def _matmul(a, b, *, mode, grid, a_spec, b_spec, out_shapes, out_specs, name, nk=1,
            a_silu=False, exact=False, epilogue=None, extra=(), extra_specs=(), acc_shape=None):
    n_extra = len(extra)
    n_out = len(out_shapes)

    def body(*refs):
        a_ref, b_ref = refs[:2]
        ex = refs[2:2 + n_extra]
        outs = refs[2 + n_extra:2 + n_extra + n_out]
        av = a_ref[...]
        bv = b_ref[...]
        if a_silu:
            av = _silu(av.astype(F32))
        if exact:
            prod = lax.dot_general(av.astype(F32), bv.astype(F32), _DIMS[mode],
                                   precision=lax.Precision.HIGHEST, preferred_element_type=F32)
        else:
            prod = lax.dot_general(av.astype(BF16), bv.astype(BF16), _DIMS[mode], preferred_element_type=F32)

        def finish(res):
            if epilogue is None:
                outs[0][...] = res.astype(outs[0].dtype)
            elif epilogue == "bias":
                outs[0][...] = (res + ex[0][...]).astype(outs[0].dtype)
            else:
                outs[0][...] = res.astype(outs[0].dtype)
                outs[1][...] = ex[0][...] + ex[1][...] * res

        if nk == 1:
            finish(prod)
        else:
            acc = refs[-1]
            k = pl.program_id(len(grid) - 1)

            @pl.when(k == 0)
            def _():
                acc[...] = prod

            @pl.when(k > 0)
            def _():
                acc[...] += prod

            @pl.when(k == nk - 1)
            def _():
                finish(acc[...])

    scratch = [pltpu.VMEM(acc_shape, F32)] if nk > 1 else []
    sem = ("parallel",) * (len(grid) - 1) + ("arbitrary",)
    return pl.pallas_call(
        body, grid=grid, in_specs=[a_spec, b_spec, *extra_specs], out_specs=list(out_specs),
        out_shape=list(out_shapes), scratch_shapes=scratch, name=name, compiler_params=_cparams(*sem),
    )(a, b, *extra)


def _row_tile(rows):
    for t in (768, 512, 256):
        if rows % t == 0:
            return t
    return rows


def _mm_nn(a, b, name, out_dtype=F32, tn=1024):
    m, k = a.shape
    n = b.shape[1]
    tm = _row_tile(m)
    tn = min(tn, n)
    return _matmul(
        a, b, mode="nn", grid=(m // tm, n // tn),
        a_spec=pl.BlockSpec((tm, k), lambda i, j: (i, 0)), b_spec=pl.BlockSpec((k, tn), lambda i, j: (0, j)),
        out_shapes=[_sds((m, n), out_dtype)], out_specs=[pl.BlockSpec((tm, tn), lambda i, j: (i, j))], name=name)[0]


def _mm_out_resid(a, w_out, xres, gate, nxb, name):
    m, k = a.shape
    n = w_out.shape[1]
    tm = ROW_BLOCK
    seg = lambda i, j: (jnp.where(i >= nxb, 1, 0), 0, 0)
    return _matmul(
        a, w_out, mode="nn", grid=(m // tm, 1),
        a_spec=pl.BlockSpec((tm, k), lambda i, j: (i, 0)), b_spec=pl.BlockSpec((k, n), lambda i, j: (0, 0)),
        extra=(xres, gate), extra_specs=(pl.BlockSpec((tm, n), lambda i, j: (i, 0)), pl.BlockSpec((None, 1, n), seg)),
        out_shapes=[_sds((m, n), ACT), _sds((m, n), F32)],
        out_specs=[pl.BlockSpec((tm, n), lambda i, j: (i, 0))] * 2, epilogue="resid", name=name)


def _mm_nt(a, b, name, out_dtype=F32):
    m, n = a.shape
    k = b.shape[0]
    tm = _row_tile(m)
    return _matmul(
        a, b, mode="nt", grid=(m // tm, 1),
        a_spec=pl.BlockSpec((tm, n), lambda i, j: (i, 0)), b_spec=pl.BlockSpec((k, n), lambda i, j: (0, 0)),
        out_shapes=[_sds((m, k), out_dtype)], out_specs=[pl.BlockSpec((tm, k), lambda i, j: (i, 0))], name=name)[0]


def _mm_nt_parts(a, b, name):
    p, m, kp = a.shape
    d = b.shape[0]
    tm = _row_tile(m)
    return _matmul(
        a, b, mode="nt", grid=(m // tm, p), nk=p, acc_shape=(tm, d),
        a_spec=pl.BlockSpec((None, tm, kp), lambda i, q: (q, i, 0)), b_spec=pl.BlockSpec((d, kp), lambda i, q: (0, q)),
        out_shapes=[_sds((m, d), F32)], out_specs=[pl.BlockSpec((tm, d), lambda i, q: (i, 0))], name=name)[0]


def _mm_tn(a, b, name, out_dtype, tm=512):
    r, m = a.shape
    n = b.shape[1]
    tm = min(tm, m)
    tn = min(1024, n)
    return _matmul(
        a, b, mode="tn", grid=(m // tm, n // tn),
        a_spec=pl.BlockSpec((r, tm), lambda i, j: (0, i)), b_spec=pl.BlockSpec((r, tn), lambda i, j: (0, j)),
        out_shapes=[_sds((m, n), out_dtype)], out_specs=[pl.BlockSpec((tm, tn), lambda i, j: (i, j))], name=name)[0]


def _mm_tn_parts(a, b, name, out_dtype, tm=512):
    r, m = a.shape
    p, _, np_ = b.shape
    tm = min(tm, m)
    return _matmul(
        a, b, mode="tn", grid=(m // tm, p),
        a_spec=pl.BlockSpec((r, tm), lambda i, q: (0, i)), b_spec=pl.BlockSpec((None, r, np_), lambda i, q: (q, 0, 0)),
        out_shapes=[_sds((m, p * np_), out_dtype)], out_specs=[pl.BlockSpec((tm, np_), lambda i, q: (i, q))],
        name=name)[0]


def _seg_map(nxb):
    return lambda i: (jnp.where(i >= nxb, 1, 0), 0, 0)


def _normmod_fwd(x, g, scale, shift, nxb, name):
    rows, d = x.shape
    tr = ROW_BLOCK

    def body(x_ref, g_ref, sc_ref, sh_ref, h_ref, r_ref):
        xv = x_ref[...]
        r = lax.rsqrt(jnp.mean(xv * xv, axis=-1, keepdims=True) + EPS)
        h = (xv * r) * g_ref[...] * (1.0 + sc_ref[...]) + sh_ref[...]
        h_ref[...] = h.astype(BF16)
        r_ref[...] = r

    row = pl.BlockSpec((tr, d), lambda i: (i, 0))
    vec = pl.BlockSpec((None, 1, d), _seg_map(nxb))
    return pl.pallas_call(
        body, grid=(rows // tr,), in_specs=[row, pl.BlockSpec((1, d), lambda i: (0, 0)), vec, vec],
        out_specs=[row, pl.BlockSpec((tr, 1), lambda i: (i, 0))],
        out_shape=[_sds((rows, d), BF16), _sds((rows, 1), F32)], name=name, compiler_params=_cparams("parallel"),
    )(x, g, scale, shift)


def _normmod_bwd(dh, x, r, g, scale, dres, nxb, name):
    rows, d = x.shape
    tr = ROW_BLOCK
    nres = dres.shape[0] // tr
    nseg = scale.shape[0]

    def body(dh_ref, x_ref, r_ref, g_ref, sc_ref, dres_ref, dx_ref, dsh_ref, dge_ref):
        i = pl.program_id(0)
        dhv = dh_ref[...]
        rv = r_ref[...]
        xn = x_ref[...] * rv
        dxn = dhv * (g_ref[...] * (1.0 + sc_ref[...]))
        dx = rv * (dxn - xn * jnp.mean(dxn * xn, axis=-1, keepdims=True))

        @pl.when(i < nres)
        def _():
            dx_ref[...] = dx + dres_ref[...]

        @pl.when(i >= nres)
        def _():
            dx_ref[...] = dx

        first = jnp.logical_or(i == 0, i == nxb)
        s_dh = jnp.sum(dhv, axis=0, keepdims=True)
        s_ge = jnp.sum(dhv * xn, axis=0, keepdims=True)

        @pl.when(first)
        def _():
            dsh_ref[...] = s_dh
            dge_ref[...] = s_ge

        @pl.when(jnp.logical_not(first))
        def _():
            dsh_ref[...] += s_dh
            dge_ref[...] += s_ge

    row = pl.BlockSpec((tr, d), lambda i: (i, 0))
    vec = pl.BlockSpec((None, 1, d), _seg_map(nxb))
    return pl.pallas_call(
        body, grid=(rows // tr,),
        in_specs=[row, row, pl.BlockSpec((tr, 1), lambda i: (i, 0)), pl.BlockSpec((1, d), lambda i: (0, 0)), vec,
                  pl.BlockSpec((tr, d), lambda i: (jnp.minimum(i, nres - 1), 0))],
        out_specs=[row, vec, vec],
        out_shape=[_sds((rows, d), F32), _sds((nseg, 1, d), F32), _sds((nseg, 1, d), F32)],
        name=name, compiler_params=_cparams("arbitrary"),
    )(dh, x, r, g, scale, dres)


def _gate_bwd(dxo, yx, gate, nxb, name):
    rows, d = yx.shape
    tr = ROW_BLOCK
    nseg = gate.shape[0]

    def body(dx_ref, yx_ref, gt_ref, dyx_ref, dg_ref):
        i = pl.program_id(0)
        dxv = dx_ref[...]
        dyx_ref[...] = (dxv * gt_ref[...]).astype(BF16)
        s = jnp.sum(dxv * yx_ref[...].astype(F32), axis=0, keepdims=True)
        first = jnp.logical_or(i == 0, i == nxb)

        @pl.when(first)
        def _():
            dg_ref[...] = s

        @pl.when(jnp.logical_not(first))
        def _():
            dg_ref[...] += s

    row = pl.BlockSpec((tr, d), lambda i: (i, 0))
    vec = pl.BlockSpec((None, 1, d), _seg_map(nxb))
    return pl.pallas_call(
        body, grid=(rows // tr,), in_specs=[row, row, vec], out_specs=[row, vec],
        out_shape=[_sds((rows, d), BF16), _sds((nseg, 1, d), F32)], name=name, compiler_params=_cparams("arbitrary"),
    )(dxo, yx, gate)


def _row_vec(ref, is_ctx):
    return ref[0] if is_ctx is None else jnp.where(is_ctx, ref[1], ref[0])


def _ctx_rows(i, tm, nx, nseg):
    if nseg == 1:
        return None
    return i * tm + lax.broadcasted_iota(jnp.int32, (tm, 1), 0) >= nx


def _seg_sums(ref, val, is_ctx, first):
    if is_ctx is None:
        parts = [jnp.sum(val, axis=0, keepdims=True)]
    else:
        parts = [jnp.sum(jnp.where(is_ctx, 0.0, val), axis=0, keepdims=True),
                 jnp.sum(jnp.where(is_ctx, val, 0.0), axis=0, keepdims=True)]

    @pl.when(first)
    def _():
        for k, p in enumerate(parts):
            ref[k] = p

    @pl.when(jnp.logical_not(first))
    def _():
        for k, p in enumerate(parts):
            ref[k] += p


def _norm_w_in(x, g, scale, shift, w_in, nx, name):
    rows, d = x.shape
    n = w_in.shape[1]
    nseg = scale.shape[0]
    tm = _row_tile(rows)
    tn = min(1024, n)

    def body(x_ref, g_ref, sc_ref, sh_ref, w_ref, h_ref, r_ref, p_ref):
        i, j = pl.program_id(0), pl.program_id(1)

        @pl.when(j == 0)
        def _():
            xv = x_ref[...]
            r = lax.rsqrt(jnp.mean(xv * xv, axis=-1, keepdims=True) + EPS)
            is_ctx = _ctx_rows(i, tm, nx, nseg)
            h = (xv * r) * g_ref[...] * (1.0 + _row_vec(sc_ref, is_ctx)) + _row_vec(sh_ref, is_ctx)
            h_ref[...] = h.astype(BF16)
            r_ref[...] = r

        p_ref[...] = jnp.dot(h_ref[...], w_ref[...], preferred_element_type=F32).astype(ACT)

    vec = pl.BlockSpec((nseg, 1, d), lambda i, j: (0, 0, 0))
    return pl.pallas_call(
        body, grid=(rows // tm, n // tn),
        in_specs=[pl.BlockSpec((tm, d), lambda i, j: (i, 0)), pl.BlockSpec((1, d), lambda i, j: (0, 0)), vec, vec,
                  pl.BlockSpec((d, tn), lambda i, j: (0, j))],
        out_specs=[pl.BlockSpec((tm, d), lambda i, j: (i, 0)), pl.BlockSpec((tm, 1), lambda i, j: (i, 0)),
                   pl.BlockSpec((tm, tn), lambda i, j: (i, j))],
        out_shape=[_sds((rows, d), BF16), _sds((rows, 1), F32), _sds((rows, n), ACT)],
        name=name, compiler_params=_cparams("parallel", "arbitrary"),
    )(x, g, scale, shift, w_in)


def _gate_w_out_bwd(dxo, yx, gate, w_out, nx, name):
    rows, d = yx.shape
    w = w_out.shape[0]
    nseg = gate.shape[0]
    tm = _row_tile(rows)

    def body(dx_ref, yx_ref, gt_ref, w_ref, dyx_ref, da_ref, dg_ref):
        i = pl.program_id(0)
        is_ctx = _ctx_rows(i, tm, nx, nseg)
        dxv = dx_ref[...]
        dyx = (dxv * _row_vec(gt_ref, is_ctx)).astype(BF16)
        dyx_ref[...] = dyx
        da_ref[...] = lax.dot_general(dyx, w_ref[...], _DIMS["nt"], preferred_element_type=F32).astype(ACT)
        _seg_sums(dg_ref, dxv * yx_ref[...].astype(F32), is_ctx, i == 0)

    row = pl.BlockSpec((tm, d), lambda i: (i, 0))
    vec = pl.BlockSpec((nseg, 1, d), lambda i: (0, 0, 0))
    return pl.pallas_call(
        body, grid=(rows // tm,), in_specs=[row, row, vec, pl.BlockSpec((w, d), lambda i: (0, 0))],
        out_specs=[row, pl.BlockSpec((tm, w), lambda i: (i, 0)), vec],
        out_shape=[_sds((rows, d), BF16), _sds((rows, w), ACT), _sds((nseg, 1, d), F32)],
        name=name, compiler_params=_cparams("arbitrary"),
    )(dxo, yx, gate, w_out)


def _w_in_bwd_norm(dparts, w_in, x, r, g, scale, dres, nx, name):
    np_, rows, kp = dparts.shape
    d = w_in.shape[0]
    nseg = scale.shape[0]
    tm = _row_tile(rows)
    nsub = tm // ROW_BLOCK
    nres_blocks = dres.shape[0] // ROW_BLOCK

    def body(dp_ref, w_ref, x_ref, r_ref, g_ref, sc_ref, *rest):
        dres_refs = rest[:nsub]
        dx_ref, dsh_ref, dge_ref, acc = rest[nsub:]
        i, k = pl.program_id(0), pl.program_id(1)
        prod = lax.dot_general(dp_ref[...], w_ref[...], _DIMS["nt"], preferred_element_type=F32)

        @pl.when(k == 0)
        def _():
            acc[...] = prod

        @pl.when(k > 0)
        def _():
            acc[...] += prod

        @pl.when(k == np_ - 1)
        def _():
            is_ctx = _ctx_rows(i, tm, nx, nseg)
            dhv = acc[...]
            rv = r_ref[...]
            xn = x_ref[...] * rv
            dxn = dhv * (g_ref[...] * (1.0 + _row_vec(sc_ref, is_ctx)))
            dx = rv * (dxn - xn * jnp.mean(dxn * xn, axis=-1, keepdims=True))
            for s in range(nsub):
                piece = slice(s * ROW_BLOCK, (s + 1) * ROW_BLOCK)
                res = dres_refs[s][...]
                if nres_blocks * ROW_BLOCK < rows:
                    res = jnp.where(i * nsub + s < nres_blocks, res, 0.0)
                dx_ref[piece, :] = dx[piece, :] + res
            _seg_sums(dsh_ref, dhv, is_ctx, i == 0)
            _seg_sums(dge_ref, dhv * xn, is_ctx, i == 0)

    row = pl.BlockSpec((tm, d), lambda i, k: (i, 0))
    vec = pl.BlockSpec((nseg, 1, d), lambda i, k: (0, 0, 0))
    return pl.pallas_call(
        body, grid=(rows // tm, np_),
        in_specs=[pl.BlockSpec((None, tm, kp), lambda i, k: (k, i, 0)), pl.BlockSpec((d, kp), lambda i, k: (0, k)),
                  row, pl.BlockSpec((tm, 1), lambda i, k: (i, 0)), pl.BlockSpec((1, d), lambda i, k: (0, 0)), vec]
        + [pl.BlockSpec((ROW_BLOCK, d), (lambda i, k, s=s: (jnp.minimum(i * nsub + s, nres_blocks - 1), 0)))
           for s in range(nsub)],
        out_specs=[row, vec, vec],
        out_shape=[_sds((rows, d), F32), _sds((nseg, 1, d), F32), _sds((nseg, 1, d), F32)],
        scratch_shapes=[pltpu.VMEM((tm, d), F32)], name=name, compiler_params=_cparams("arbitrary", "arbitrary"),
    )(dparts, w_in, x, r, g, scale, *([dres] * nsub))


_PAD_TOP = 16
_PAD_BOT = 32


def _window_sum(buf, xv, lo, n):
    t = xv.shape[0]
    c = xv.shape[1]
    tp = t + _PAD_TOP + _PAD_BOT
    buf[pl.ds(0, _PAD_TOP), :] = jnp.zeros((_PAD_TOP, c), F32)
    buf[pl.ds(_PAD_TOP, t), :] = xv
    buf[pl.ds(_PAD_TOP + t, _PAD_BOT), :] = jnp.zeros((_PAD_BOT, c), F32)
    p = buf[...]
    k = 1
    while k < n:
        p = p + pltpu.roll(p, tp - k, 0)
        k *= 2
    if lo:
        p = pltpu.roll(p, -lo, 0)
    buf[...] = p
    return buf[pl.ds(_PAD_TOP, t), :]


def _window_count(t, half):
    pos = lax.broadcasted_iota(jnp.int32, (t, 1), 0)
    return (jnp.minimum(pos + half, t) - jnp.maximum(pos - half, 0)).astype(F32)


def _segments(rows, nx):
    return [(0, nx)] + ([(nx, rows - nx)] if rows > nx else [])


def _pool_fwd(uv, nx, name):
    rows = uv.shape[0]
    w = uv.shape[1] // 2
    cb = 128
    per_group = w // len(POOL_WINDOWS) // cb
    segs = _segments(rows, nx)

    def body(u_ref, z_ref, *bufs):
        j = pl.program_id(0)
        for gi, win in enumerate(POOL_WINDOWS):
            half = win // 2

            @pl.when(jnp.logical_and(j >= gi * per_group, j < (gi + 1) * per_group))
            def _():
                for (start, length), buf in zip(segs, bufs):
                    uvv = u_ref[pl.ds(start, length), :].astype(F32)
                    s = _window_sum(buf, uvv, -half, win)
                    z_ref[pl.ds(start, length), :] = (s / _window_count(length, half) - uvv).astype(BF16)

    scratch = [pltpu.VMEM((length + _PAD_TOP + _PAD_BOT, cb), F32) for _, length in segs]
    return pl.pallas_call(
        body, grid=(w // cb,), in_specs=[pl.BlockSpec((rows, cb), lambda j: (0, j))],
        out_specs=pl.BlockSpec((rows, cb), lambda j: (0, j)), out_shape=_sds((rows, w), BF16),
        scratch_shapes=scratch, name=name, compiler_params=_cparams("parallel"),
    )(uv)


def _pool_bwd(dz, dgt, nx, name):
    rows, w = dz.shape
    cb = 128
    per_group = w // len(POOL_WINDOWS) // cb
    segs = _segments(rows, nx)

    def body(dz_ref, dgt_ref, o_ref, *bufs):
        j = pl.program_id(0)
        o_ref[1] = dgt_ref[...]
        for gi, win in enumerate(POOL_WINDOWS):
            half = win // 2

            @pl.when(jnp.logical_and(j >= gi * per_group, j < (gi + 1) * per_group))
            def _():
                for (start, length), buf in zip(segs, bufs):
                    dzv = dz_ref[pl.ds(start, length), :].astype(F32)
                    s = _window_sum(buf, dzv / _window_count(length, half), 1 - half, win)
                    o_ref[0, pl.ds(start, length), :] = (s - dzv).astype(BF16)

    scratch = [pltpu.VMEM((length + _PAD_TOP + _PAD_BOT, cb), F32) for _, length in segs]
    col = pl.BlockSpec((rows, cb), lambda j: (0, j))
    return pl.pallas_call(
        body, grid=(w // cb,), in_specs=[col, col], out_specs=pl.BlockSpec((2, rows, cb), lambda j: (0, 0, j)),
        out_shape=_sds((2, rows, w), BF16), scratch_shapes=scratch, name=name, compiler_params=_cparams("parallel"),
    )(dz, dgt)


def _grp_fwd(z, w_grp, uv, scale, name):
    rows, w = z.shape
    ng, gc, _ = w_grp.shape
    tm = _row_tile(rows)

    def body(z_ref, w_ref, gt_ref, sc_ref, mx_ref, a_ref):
        mixed = jnp.dot(z_ref[...], w_ref[...], preferred_element_type=F32)
        mx_ref[...] = mixed.astype(ACT)
        a_ref[...] = (mixed * sc_ref[...] * _silu(gt_ref[...].astype(F32))).astype(BF16)

    blk = pl.BlockSpec((tm, gc), lambda g, i: (i, g))
    return pl.pallas_call(
        body, grid=(ng, rows // tm),
        in_specs=[blk, pl.BlockSpec((None, gc, gc), lambda g, i: (g, 0, 0)),
                  pl.BlockSpec((tm, gc), lambda g, i: (i, ng + g)), pl.BlockSpec((1, gc), lambda g, i: (0, g))],
        out_specs=[blk, blk], out_shape=[_sds((rows, w), ACT), _sds((rows, w), BF16)],
        name=name, compiler_params=_cparams("parallel", "parallel"),
    )(z, w_grp, uv, scale)


def _grp_bwd(da, mixed, uv, scale, w_grp, name):
    rows, w = da.shape
    ng, gc, _ = w_grp.shape
    tm = _row_tile(rows)

    def body(da_ref, mx_ref, gt_ref, sc_ref, w_ref, dm_ref, dz_ref, dgt_ref, dsc_ref):
        i = pl.program_id(1)
        dav = da_ref[...].astype(F32)
        mixed = mx_ref[...].astype(F32)
        gt = gt_ref[...].astype(F32)
        sg = _silu(gt)
        sc = sc_ref[...]
        dm = (dav * sc * sg).astype(BF16)
        dm_ref[...] = dm
        dz_ref[...] = lax.dot_general(dm, w_ref[...], _DIMS["nt"], preferred_element_type=F32).astype(ACT)
        dgt_ref[...] = (dav * mixed * sc * _dsilu(gt)).astype(BF16)
        s = jnp.sum(dav * mixed * sg, axis=0, keepdims=True)

        @pl.when(i == 0)
        def _():
            dsc_ref[...] = s

        @pl.when(i > 0)
        def _():
            dsc_ref[...] += s

    blk = pl.BlockSpec((tm, gc), lambda g, i: (i, g))
    vec = pl.BlockSpec((1, gc), lambda g, i: (0, g))
    return pl.pallas_call(
        body, grid=(ng, rows // tm),
        in_specs=[blk, blk, pl.BlockSpec((tm, gc), lambda g, i: (i, ng + g)), vec,
                  pl.BlockSpec((None, gc, gc), lambda g, i: (g, 0, 0))],
        out_specs=[blk, blk, blk, vec],
        out_shape=[_sds((rows, w), BF16), _sds((rows, w), ACT), _sds((rows, w), BF16), _sds((1, w), F32)],
        name=name, compiler_params=_cparams("parallel", "arbitrary"),
    )(da, mixed, uv, scale, w_grp)


def _pool_scratch(rows, nx, cols):
    return [pltpu.VMEM((length + _PAD_TOP + _PAD_BOT, cols), F32) for _, length in _segments(rows, nx)]


def _per_group(g, fn):
    for gi, win in enumerate(POOL_WINDOWS):
        pl.when(g == gi)(functools.partial(fn, win))


def _pool_grp_fwd(uv, w_grp, scale, nx, name):
    rows = uv.shape[0]
    ng, gc, _ = w_grp.shape
    w = ng * gc
    segs = _segments(rows, nx)

    def body(u_ref, gt_ref, w_ref, sc_ref, z_ref, mx_ref, a_ref, *bufs):
        def pool(win):
            half = win // 2
            for (start, length), buf in zip(segs, bufs):
                uvv = u_ref[pl.ds(start, length), :].astype(F32)
                s = _window_sum(buf, uvv, -half, win)
                z_ref[pl.ds(start, length), :] = (s / _window_count(length, half) - uvv).astype(BF16)

        _per_group(pl.program_id(0), pool)
        mixed = jnp.dot(z_ref[...], w_ref[...], preferred_element_type=F32)
        mx_ref[...] = mixed.astype(ACT)
        a_ref[...] = (mixed * sc_ref[...] * _silu(gt_ref[...].astype(F32))).astype(BF16)

    col = pl.BlockSpec((rows, gc), lambda g: (0, g))
    return pl.pallas_call(
        body, grid=(ng,),
        in_specs=[col, pl.BlockSpec((rows, gc), lambda g: (0, ng + g)), pl.BlockSpec((None, gc, gc), lambda g: (g, 0, 0)),
                  pl.BlockSpec((1, gc), lambda g: (0, g))],
        out_specs=[col, col, col], out_shape=[_sds((rows, w), BF16), _sds((rows, w), ACT), _sds((rows, w), BF16)],
        scratch_shapes=_pool_scratch(rows, nx, gc), name=name, compiler_params=_cparams("parallel"),
    )(uv, uv, w_grp, scale)


def _pool_grp_bwd(da, mixed, uv, scale, w_grp, nx, name):
    rows, w = da.shape
    ng, gc, _ = w_grp.shape
    segs = _segments(rows, nx)

    def body(da_ref, mx_ref, gt_ref, sc_ref, w_ref, dm_ref, duv_ref, dsc_ref, dz_ref, *bufs):
        dav = da_ref[...].astype(F32)
        mixed = mx_ref[...].astype(F32)
        gt = gt_ref[...].astype(F32)
        sg = _silu(gt)
        sc = sc_ref[...]
        dm = (dav * sc * sg).astype(BF16)
        dm_ref[...] = dm
        dz_ref[...] = lax.dot_general(dm, w_ref[...], _DIMS["nt"], preferred_element_type=F32)
        duv_ref[1] = (dav * mixed * sc * _dsilu(gt)).astype(BF16)
        dsc_ref[...] = jnp.sum(dav * mixed * sg, axis=0, keepdims=True)

        def unpool(win):
            half = win // 2
            for (start, length), buf in zip(segs, bufs):
                dzv = dz_ref[pl.ds(start, length), :]
                s = _window_sum(buf, dzv / _window_count(length, half), 1 - half, win)
                duv_ref[0, pl.ds(start, length), :] = (s - dzv).astype(BF16)

        _per_group(pl.program_id(0), unpool)

    col = pl.BlockSpec((rows, gc), lambda g: (0, g))
    vec = pl.BlockSpec((1, gc), lambda g: (0, g))
    return pl.pallas_call(
        body, grid=(ng,),
        in_specs=[col, col, pl.BlockSpec((rows, gc), lambda g: (0, ng + g)), vec,
                  pl.BlockSpec((None, gc, gc), lambda g: (g, 0, 0))],
        out_specs=[col, pl.BlockSpec((2, rows, gc), lambda g: (0, 0, g)), vec],
        out_shape=[_sds((rows, w), BF16), _sds((2, rows, w), BF16), _sds((1, w), F32)],
        scratch_shapes=[pltpu.VMEM((rows, gc), F32)] + _pool_scratch(rows, nx, gc),
        name=name, compiler_params=_cparams("parallel"),
    )(da, mixed, uv, scale, w_grp)


def _grp_wgrad(z, dm, ng, name, out_dtype):
    rows, w = z.shape
    gc = w // ng

    def body(z_ref, dm_ref, o_ref):
        o_ref[...] = lax.dot_general(z_ref[...], dm_ref[...], _DIMS["tn"],
                                     preferred_element_type=F32).astype(o_ref.dtype)

    blk = pl.BlockSpec((rows, gc), lambda g: (0, g))
    return pl.pallas_call(
        body, grid=(ng,), in_specs=[blk, blk], out_specs=pl.BlockSpec((None, gc, gc), lambda g: (g, 0, 0)),
        out_shape=_sds((ng, gc, gc), out_dtype), name=name, compiler_params=_cparams("parallel"),
    )(z, dm)


def _shift_rows(v, by):
    t = v.shape[0]
    pos = lax.broadcasted_iota(jnp.int32, v.shape, 0)
    rolled = pltpu.roll(v, by % t, 0)
    keep = pos >= by if by > 0 else pos < t + by
    return jnp.where(keep, rolled, 0.0)


def _conv_specs(t, w, cb):
    return [pl.BlockSpec((t, cb), (lambda j, q=q: (0, q * (w // cb) + j))) for q in range(4)]


def _conv_fwd(p4, dw, db, name):
    t = p4.shape[0]
    w = p4.shape[1] // 4
    cb = 128

    def body(bg_ref, cg_ref, v_ref, g_ref, dw_ref, db_ref, a_ref):
        tv = cg_ref[...].astype(F32) * v_ref[...].astype(F32)
        conv = (dw_ref[0:1, :] * _shift_rows(tv, 1) + dw_ref[1:2, :] * tv + dw_ref[2:3, :] * _shift_rows(tv, -1)
                + db_ref[...])
        a_ref[...] = (bg_ref[...].astype(F32) * conv * _silu(g_ref[...].astype(F32))).astype(BF16)

    return pl.pallas_call(
        body, grid=(w // cb,),
        in_specs=_conv_specs(t, w, cb) + [pl.BlockSpec((3, cb), lambda j: (0, j)), pl.BlockSpec((1, cb), lambda j: (0, j))],
        out_specs=pl.BlockSpec((t, cb), lambda j: (0, j)), out_shape=_sds((t, w), BF16),
        name=name, compiler_params=_cparams("parallel"),
    )(p4, p4, p4, p4, dw, db)


def _conv_bwd(da, p4, dw, db, name):
    t, w = da.shape
    cb = 128

    def body(da_ref, bg_ref, cg_ref, v_ref, g_ref, dw_ref, db_ref, d4_ref, ddw_ref, ddb_ref):
        cg = cg_ref[...].astype(F32)
        vv = v_ref[...].astype(F32)
        bg = bg_ref[...].astype(F32)
        gv = g_ref[...].astype(F32)
        tv = cg * vv
        tm1 = _shift_rows(tv, 1)
        tp1 = _shift_rows(tv, -1)
        w0, w1, w2 = dw_ref[0:1, :], dw_ref[1:2, :], dw_ref[2:3, :]
        conv = w0 * tm1 + w1 * tv + w2 * tp1 + db_ref[...]
        y = bg * conv
        dav = da_ref[...].astype(F32)
        dy = dav * _silu(gv)
        d4_ref[3] = (dav * y * _dsilu(gv)).astype(BF16)
        d4_ref[0] = (dy * conv).astype(BF16)
        dconv = dy * bg
        ddb_ref[...] = jnp.sum(dconv, axis=0, keepdims=True)
        ddw_ref[0:1, :] = jnp.sum(dconv * tm1, axis=0, keepdims=True)
        ddw_ref[1:2, :] = jnp.sum(dconv * tv, axis=0, keepdims=True)
        ddw_ref[2:3, :] = jnp.sum(dconv * tp1, axis=0, keepdims=True)
        dt = w0 * _shift_rows(dconv, -1) + w1 * dconv + w2 * _shift_rows(dconv, 1)
        d4_ref[1] = (dt * vv).astype(BF16)
        d4_ref[2] = (dt * cg).astype(BF16)

    col = pl.BlockSpec((t, cb), lambda j: (0, j))
    tap = pl.BlockSpec((3, cb), lambda j: (0, j))
    bias = pl.BlockSpec((1, cb), lambda j: (0, j))
    return pl.pallas_call(
        body, grid=(w // cb,), in_specs=[col] + _conv_specs(t, w, cb) + [tap, bias],
        out_specs=[pl.BlockSpec((4, t, cb), lambda j: (0, 0, j)), tap, bias],
        out_shape=[_sds((4, t, w), BF16), _sds((3, w), F32), _sds((1, w), F32)],
        name=name, compiler_params=_cparams("parallel"),
    )(da, p4, p4, p4, p4, dw, db)


def _attn_mask():
    qn, kn = Q_ROWS * GRID_W, K_ROWS * GRID_W
    qr, qc = np.divmod(np.arange(qn), GRID_W)
    kr, kc = np.divmod(np.arange(kn), GRID_W)
    col0 = np.clip(qc - WIN_COLS // 2, 0, GRID_W - WIN_COLS)
    col_ok = (kc[None, :] >= col0[:, None]) & (kc[None, :] < col0[:, None] + WIN_COLS)
    first = np.zeros(qn, np.int64)
    last = np.full(qn, K_ROWS - WIN_ROWS)
    out = []
    for row0 in (first, qr, last):
        row_ok = (kr[None, :] >= row0[:, None]) & (kr[None, :] < row0[:, None] + WIN_ROWS)
        out.append(np.where(row_ok & col_ok, 0.0, NEG))
    return jnp.asarray(np.stack(out), F32)


_KW = K_ROWS * GRID_W
_QB = Q_ROWS * GRID_W
_PAIR = 2 * HEAD_DIM
_N_DR = 2 * WIN_ROWS - 1
_N_DC = 2 * WIN_COLS - 1
_RP_ROWS = 24
_N_TILES = _N_DR + 1
_BIAS_BASE = (WIN_ROWS - 1, WIN_ROWS // 2 - 1, -1)


class _Comm:
    def __init__(self, ins, outs, sems, start, finish):
        self.ins, self.outs, self.sems, self.start, self.finish = list(ins), list(outs), list(sems), start, finish


def _bias_pieces(cls):
    out = []
    for qr in range(Q_ROWS):
        for kr in range(0, K_ROWS, 2):
            tile = _BIAS_BASE[cls] - qr + kr + 1
            out.append((qr, kr, tile if 0 <= tile < _N_TILES else None))
    return out


def _toeplitz_pair(left_row, right_row):
    lane = lax.broadcasted_iota(jnp.int32, (GRID_W, _PAIR), 1)
    shape = (GRID_W, _PAIR)
    left = pltpu.roll(jnp.broadcast_to(left_row, shape), _PAIR - (WIN_COLS - 1), 1, stride=1, stride_axis=0)
    right = pltpu.roll(jnp.broadcast_to(right_row, shape), GRID_W - (WIN_COLS - 1), 1, stride=1, stride_axis=0)
    return jnp.where(lane < GRID_W, left, right)


def _build_tiles(tiles_ref, rp_ref):
    for h in range(2):
        for t in range(_N_TILES):
            tiles_ref[h, t] = _toeplitz_pair(rp_ref[h, t:t + 1, :], rp_ref[h, t + 1:t + 2, :])


def _block_class(b, nblk, fn, entering=False):
    interior = (b == 1) if entering else jnp.logical_and(b > 0, b < nblk - 1)
    for cls, cond in enumerate((b == 0, interior, b == nblk - 1)):
        pl.when(cond)(functools.partial(fn, cls))


def _attn_geometry(p4, nx):
    rows = p4.shape[0]
    w = p4.shape[1] // 4
    nhp = w // _PAIR
    nblk = nx // _QB
    qspec = lambda col: pl.BlockSpec((_QB, _PAIR), lambda hp, b: (b, col * nhp + hp))
    kspec = lambda col: pl.BlockSpec((rows, _PAIR), lambda hp, b: (0, col * nhp + hp))
    tspec = pl.BlockSpec((2, _RP_ROWS, _PAIR), lambda hp, b: (hp, 0, 0))
    mspec = pl.BlockSpec((None, _QB, _KW), lambda hp, b: (jnp.where(b == 0, 0, jnp.where(b == nblk - 1, 2, 1)), 0, 0))
    lspec = pl.BlockSpec((None, _QB, 2), lambda hp, b: (hp, b, 0))
    ospec = pl.BlockSpec((_QB, _PAIR), lambda hp, b: (b, hp))
    return rows, w, nhp, nblk, qspec, kspec, tspec, mspec, lspec, ospec


def _window_start(b, nx):
    return pl.multiple_of(jnp.clip(b * _QB - PAD_ROWS * GRID_W, 0, nx - _KW), _QB)


def _load_bias(bias_ref, tiles_ref, rp_ref, m_ref, b, nblk):
    pl.when(b == 0)(lambda: _build_tiles(tiles_ref, rp_ref))

    def fill(cls):
        for h in range(2):
            for qr, kr, tile in _bias_pieces(cls):
                rows = slice(qr * GRID_W, (qr + 1) * GRID_W)
                cols = slice(kr * GRID_W, (kr + 2) * GRID_W)
                m = m_ref[rows, cols]
                bias_ref[h, rows, cols] = m if tile is None else tiles_ref[h, tile] + m

    _block_class(b, nblk, fill, entering=True)


def _attn_fwd(p4, rp, mask, nx, name, comm=None):
    rows, w, nhp, nblk, qspec, kspec, tspec, mspec, lspec, ospec = _attn_geometry(p4, nx)
    n_ctx = rows - nx
    n_cin, n_cout = (len(comm.ins), len(comm.outs)) if comm else (0, 0)

    def body(*refs):
        q_ref, k_ref, v_ref, g_ref, rp_ref, m_ref = refs[:6]
        cin = refs[6:6 + n_cin]
        a_ref, o_ref, lse_ref = refs[6 + n_cin:9 + n_cin]
        cout = refs[9 + n_cin:9 + n_cin + n_cout]
        bias_ref, tiles_ref = refs[9 + n_cin + n_cout:11 + n_cin + n_cout]
        sems = refs[11 + n_cin + n_cout:]
        hp, b = pl.program_id(0), pl.program_id(1)
        if comm:
            pl.when(jnp.logical_and(hp == 0, b == 0))(lambda: comm.start(cin, cout, sems))
        start = _window_start(b, nx)
        _load_bias(bias_ref, tiles_ref, rp_ref, m_ref, b, nblk)
        qf = q_ref[...].astype(F32) * HEAD_DIM ** -0.5
        kw = k_ref[pl.ds(start, _KW), :].astype(BF16)
        vw = v_ref[pl.ds(start, _KW), :].astype(BF16)
        kcv = k_ref[pl.ds(nx, n_ctx), :].astype(BF16)
        vcv = v_ref[pl.ds(nx, n_ctx), :].astype(BF16)
        lane = lax.broadcasted_iota(jnp.int32, (1, _PAIR), 1)
        outs, lses = [], []
        for h in range(2):
            mine = (lane >= HEAD_DIM) if h else (lane < HEAD_DIM)
            qm = jnp.where(mine, qf, 0.0).astype(BF16)
            s_loc = lax.dot_general(qm, kw, _DIMS["nt"], preferred_element_type=F32) + bias_ref[h]
            s_ctx = lax.dot_general(qm, kcv, _DIMS["nt"], preferred_element_type=F32)
            mx = jnp.maximum(jnp.max(s_loc, axis=-1, keepdims=True), jnp.max(s_ctx, axis=-1, keepdims=True))
            p_loc = jnp.exp(s_loc - mx)
            p_ctx = jnp.exp(s_ctx - mx)
            den = jnp.sum(p_loc, axis=-1, keepdims=True) + jnp.sum(p_ctx, axis=-1, keepdims=True)
            o = jnp.dot(p_loc.astype(BF16), vw, preferred_element_type=F32)
            o = o + jnp.dot(p_ctx.astype(BF16), vcv, preferred_element_type=F32)
            outs.append(o * (1.0 / den))
            lses.append(mx + jnp.log(den))
        o = jnp.where(lane < HEAD_DIM, outs[0], outs[1])
        o_ref[...] = o.astype(ACT)
        a_ref[...] = (o * _silu(g_ref[...].astype(F32))).astype(BF16)
        col = lax.broadcasted_iota(jnp.int32, (1, 2), 1)
        lse_ref[...] = jnp.where(col == 0, lses[0], lses[1])
        if comm:
            pl.when(jnp.logical_and(hp == nhp - 1, b == nblk - 1))(lambda: comm.finish(cin, cout, sems))

    res = pl.pallas_call(
        body, grid=(nhp, nblk),
        in_specs=[qspec(0), kspec(1), kspec(2), qspec(3), tspec, mspec] + [HBM_SPEC] * n_cin,
        out_specs=[ospec, ospec, lspec] + [HBM_SPEC] * n_cout,
        out_shape=[_sds((nx, w), BF16), _sds((nx, w), ACT), _sds((nhp, nx, 2), F32)] + (comm.outs if comm else []),
        scratch_shapes=[pltpu.VMEM((2, _QB, _KW), F32), pltpu.VMEM((2, _N_TILES, GRID_W, _PAIR), F32)]
        + (comm.sems if comm else []),
        name=name, compiler_params=_cparams("arbitrary", "arbitrary"),
    )(p4, p4, p4, p4, rp, mask, *(comm.ins if comm else []))
    return res[:3], res[3:]


def _fold_tiles(dtiles_ref, drp_ref):
    shape = (GRID_W, _PAIR)
    lane = lax.broadcasted_iota(jnp.int32, shape, 1)
    flip = (lax.broadcasted_iota(jnp.int32, (_PAIR, _PAIR), 0)
            + lax.broadcasted_iota(jnp.int32, (_PAIR, _PAIR), 1) == _PAIR - 1).astype(F32)
    drp_ref[...] = jnp.zeros(drp_ref.shape, F32)
    for h in range(2):
        for t in range(_N_TILES):
            tile = dtiles_ref[h, t]
            for side in (0, 1):
                shift = _PAIR - GRID_W * side - (WIN_COLS - 1)
                half = jnp.where((lane >= GRID_W) if side else (lane < GRID_W), tile, 0.0)
                rev = jnp.dot(half, flip, precision=lax.Precision.HIGHEST, preferred_element_type=F32)
                diag = pltpu.roll(rev, shift, 1, stride=1, stride_axis=0)
                drp_ref[h, t + side:t + side + 1, :] += jnp.sum(diag, axis=0, keepdims=True)


def _attn_bwd(p4, rp, mask, o, lse, da, nx, name, comm=None):
    rows, w, nhp, nblk, qspec, kspec, tspec, mspec, lspec, ospec = _attn_geometry(p4, nx)
    n_ctx = rows - nx
    n_cin, n_cout = (len(comm.ins), len(comm.outs)) if comm else (0, 0)

    def body(*refs):
        q_ref, k_ref, v_ref, g_ref, rp_ref, m_ref, o_ref, lse_ref, da_ref = refs[:9]
        cin = refs[9:9 + n_cin]
        d4_ref, drp_ref = refs[9 + n_cin:11 + n_cin]
        cout = refs[11 + n_cin:11 + n_cin + n_cout]
        bias_ref, tiles_ref, ds_ref, dtiles_ref, dk_ref, dv_ref = refs[11 + n_cin + n_cout:17 + n_cin + n_cout]
        sems = refs[17 + n_cin + n_cout:]
        hp, b = pl.program_id(0), pl.program_id(1)
        if comm:
            pl.when(jnp.logical_and(hp == 0, b == 0))(lambda: comm.start(cin, cout, sems))
        start = _window_start(b, nx)
        here = pl.multiple_of(b * _QB, _QB)

        @pl.when(b == 0)
        def _():
            dk_ref[...] = jnp.zeros(dk_ref.shape, F32)
            dv_ref[...] = jnp.zeros(dv_ref.shape, F32)
            dtiles_ref[...] = jnp.zeros(dtiles_ref.shape, F32)
            d4_ref[0, pl.ds(nx, n_ctx), :] = jnp.zeros((n_ctx, _PAIR), BF16)
            d4_ref[3, pl.ds(nx, n_ctx), :] = jnp.zeros((n_ctx, _PAIR), BF16)

        _load_bias(bias_ref, tiles_ref, rp_ref, m_ref, b, nblk)
        gv = g_ref[...].astype(F32)
        dav = da_ref[...].astype(F32)
        ov = o_ref[...].astype(F32)
        dov = dav * _silu(gv)
        d4_ref[3, pl.ds(here, _QB), :] = (dav * ov * _dsilu(gv)).astype(BF16)
        qf = q_ref[...].astype(F32) * HEAD_DIM ** -0.5
        kw = k_ref[pl.ds(start, _KW), :].astype(BF16)
        vw = v_ref[pl.ds(start, _KW), :].astype(BF16)
        kcv = k_ref[pl.ds(nx, n_ctx), :].astype(BF16)
        vcv = v_ref[pl.ds(nx, n_ctx), :].astype(BF16)
        lane = lax.broadcasted_iota(jnp.int32, (1, _PAIR), 1)
        dq = jnp.zeros((_QB, _PAIR), F32)
        for h in range(2):
            mine = (lane >= HEAD_DIM) if h else (lane < HEAD_DIM)
            qm = jnp.where(mine, qf, 0.0).astype(BF16)
            dom = jnp.where(mine, dov, 0.0)
            dob = dom.astype(BF16)
            lse = lse_ref[:, h:h + 1]
            s_loc = lax.dot_general(qm, kw, _DIMS["nt"], preferred_element_type=F32)
            p_loc = jnp.exp(s_loc + bias_ref[h] - lse)
            p_ctx = jnp.exp(lax.dot_general(qm, kcv, _DIMS["nt"], preferred_element_type=F32) - lse)
            delta = jnp.sum(dom * ov, axis=-1, keepdims=True)
            ds_loc = p_loc * (lax.dot_general(dob, vw, _DIMS["nt"], preferred_element_type=F32) - delta)
            ds_ctx = p_ctx * (lax.dot_general(dob, vcv, _DIMS["nt"], preferred_element_type=F32) - delta)
            dsb_loc = ds_loc.astype(BF16)
            dsb_ctx = ds_ctx.astype(BF16)
            dq_h = (jnp.dot(dsb_loc, kw, preferred_element_type=F32)
                    + jnp.dot(dsb_ctx, kcv, preferred_element_type=F32))
            dq = dq + jnp.where(mine, dq_h, 0.0)
            dk_ref[pl.ds(start, _KW), :] += lax.dot_general(dsb_loc, qm, _DIMS["tn"], preferred_element_type=F32)
            dv_ref[pl.ds(start, _KW), :] += lax.dot_general(p_loc.astype(BF16), dob, _DIMS["tn"],
                                                            preferred_element_type=F32)
            dk_ref[pl.ds(nx, n_ctx), :] += lax.dot_general(dsb_ctx, qm, _DIMS["tn"], preferred_element_type=F32)
            dv_ref[pl.ds(nx, n_ctx), :] += lax.dot_general(p_ctx.astype(BF16), dob, _DIMS["tn"],
                                                           preferred_element_type=F32)
            ds_ref[h] = ds_loc
        d4_ref[0, pl.ds(here, _QB), :] = (dq * HEAD_DIM ** -0.5).astype(BF16)

        def scatter(cls):
            for h in range(2):
                for qr, kr, tile in _bias_pieces(cls):
                    if tile is not None:
                        dtiles_ref[h, tile] += ds_ref[h, qr * GRID_W:(qr + 1) * GRID_W, kr * GRID_W:(kr + 2) * GRID_W]

        _block_class(b, nblk, scatter)

        @pl.when(b == nblk - 1)
        def _():
            d4_ref[1] = dk_ref[...].astype(BF16)
            d4_ref[2] = dv_ref[...].astype(BF16)
            _fold_tiles(dtiles_ref, drp_ref)

        if comm:
            pl.when(jnp.logical_and(hp == nhp - 1, b == nblk - 1))(lambda: comm.finish(cin, cout, sems))

    tiles = pltpu.VMEM((2, _N_TILES, GRID_W, _PAIR), F32)
    block = pltpu.VMEM((2, _QB, _KW), F32)
    res = pl.pallas_call(
        body, grid=(nhp, nblk),
        in_specs=[qspec(0), kspec(1), kspec(2), qspec(3), tspec, mspec, ospec, lspec, ospec] + [HBM_SPEC] * n_cin,
        out_specs=[pl.BlockSpec((4, rows, _PAIR), lambda hp, b: (0, 0, hp)), tspec] + [HBM_SPEC] * n_cout,
        out_shape=[_sds((4, rows, w), BF16), _sds(rp.shape, F32)] + (comm.outs if comm else []),
        scratch_shapes=[block, tiles, block, tiles, pltpu.VMEM((rows, _PAIR), F32), pltpu.VMEM((rows, _PAIR), F32)]
        + (comm.sems if comm else []),
        name=name, compiler_params=_cparams("arbitrary", "arbitrary"),
    )(p4, p4, p4, p4, rp, mask, o, lse, da, *(comm.ins if comm else []))
    return res[:2], res[2:]


def _final(x, g, target, name):
    rows, d = x.shape
    tr = ROW_BLOCK
    nblk = rows // tr

    def body(x_ref, g_ref, t_ref, loss_ref, dx_ref, dg_ref, acc_ref):
        i = pl.program_id(0)
        xv = x_ref[...]
        gv = g_ref[...]
        r = lax.rsqrt(jnp.mean(xv * xv, axis=-1, keepdims=True) + EPS)
        xn = xv * r
        err = xn * gv - t_ref[...]
        dy = err * (1.0 / d)
        dxn = dy * gv
        dx_ref[...] = r * (dxn - xn * jnp.mean(dxn * xn, axis=-1, keepdims=True))
        s_g = jnp.sum(dy * xn, axis=0, keepdims=True)
        s_l = jnp.sum(jnp.mean(err * err, axis=-1, keepdims=True), axis=0, keepdims=True)

        @pl.when(i == 0)
        def _():
            dg_ref[...] = s_g
            acc_ref[...] = s_l

        @pl.when(i > 0)
        def _():
            dg_ref[...] += s_g
            acc_ref[...] += s_l

        @pl.when(i == nblk - 1)
        def _():
            loss_ref[...] = jnp.broadcast_to(0.5 * acc_ref[...], loss_ref.shape)

    row = pl.BlockSpec((tr, d), lambda i: (i, 0))
    vec = pl.BlockSpec((1, d), lambda i: (0, 0))
    return pl.pallas_call(
        body, grid=(nblk,), in_specs=[row, vec, row],
        out_specs=[pl.BlockSpec((1, 128), lambda i: (0, 0)), row, vec],
        out_shape=[_sds((1, 128), F32), _sds((rows, d), F32), _sds((1, d), F32)],
        scratch_shapes=[pltpu.VMEM((1, 1), F32)], name=name, compiler_params=_cparams("arbitrary"),
    )(x, g, target)


def _as2d(a):
    if a.ndim == 1:
        return a.reshape(-1, 128) if a.shape[0] % 128 == 0 else a.reshape(1, -1)
    return a.reshape(-1, a.shape[-1])


def _adamw(w, g, m, v, name):
    shape = w.shape
    w2, g2, m2, v2 = (_as2d(t) for t in (w, g.reshape(shape), m, v))
    rows, cols = w2.shape
    tr = 512 if rows % 512 == 0 else rows
    c1 = 1.0 - ADAM_B1 ** ADAM_STEP
    c2 = 1.0 - ADAM_B2 ** ADAM_STEP

    def body(w_ref, g_ref, m_ref, v_ref, d_ref, nm_ref, nv_ref):
        gv = g_ref[...]
        nm = ADAM_B1 * m_ref[...] + (1.0 - ADAM_B1) * gv
        nv = ADAM_B2 * v_ref[...] + (1.0 - ADAM_B2) * (gv * gv)
        nm_ref[...] = nm
        nv_ref[...] = nv
        d_ref[...] = -ADAM_LR * ((nm / c1) / (jnp.sqrt(nv / c2) + ADAM_EPS) + ADAM_WD * w_ref[...])

    blk = pl.BlockSpec((tr, cols), lambda i: (i, 0))
    outs = pl.pallas_call(
        body, grid=(rows // tr,), in_specs=[blk] * 4, out_specs=[blk] * 3,
        out_shape=[_sds((rows, cols), F32)] * 3, name=name, compiler_params=_cparams("parallel"),
    )(w2, g2, m2, v2)
    return tuple(t.reshape(shape) for t in outs)


def _sum_lead(x, name, out_dtype=F32):
    n, rows, cols = x.shape
    tr = 512 if rows % 512 == 0 else rows

    def body(x_ref, o_ref):
        acc = x_ref[0].astype(F32)
        for k in range(1, n):
            acc = acc + x_ref[k].astype(F32)
        o_ref[...] = acc.astype(out_dtype)

    return pl.pallas_call(
        body, grid=(rows // tr,), in_specs=[pl.BlockSpec((n, tr, cols), lambda i: (0, i, 0))],
        out_specs=pl.BlockSpec((tr, cols), lambda i: (i, 0)), out_shape=_sds((rows, cols), out_dtype),
        name=name, compiler_params=_cparams("parallel"),
    )(x)


_NO_CTX = 1 << 30


def _seg_vecs(mod_l, which, nseg):
    return mod_l[:nseg, which][:, None, :]


def _norm_grads(dshift, dgeff, dgate, g, scale):
    nseg, _, d = dshift.shape
    dmod = jnp.stack([dshift[:, 0], dgeff[:, 0] * g, dgate[:, 0]], axis=1)
    if nseg == 1:
        dmod = jnp.concatenate([dmod, jnp.zeros((1, 3, d), F32)], axis=0)
    dg = jnp.sum(dgeff[:, 0] * (1.0 + scale[:, 0]), axis=0)
    return dmod, dg


def _pool_layer(xin, g, mod_l, w_in, w_grp, w_out, pscale, nx, tag):
    rows = xin.shape[0]
    nseg = 2 if rows > nx else 1
    nxb = nx // ROW_BLOCK if nseg == 2 else _NO_CTX
    shift, scale, gate = (_seg_vecs(mod_l, k, nseg) for k in range(3))
    h, r, uv = _norm_w_in(xin, g, scale, shift, w_in, nx, f"w_in_fwd_{tag}")
    z, mixed, a = _pool_grp_fwd(uv, w_grp, pscale, nx, f"pool_fwd_{tag}")
    yx, xout = _mm_out_resid(a, w_out, xin, gate, nxb, f"w_out_fwd_{tag}")

    def backward(dxo):
        dyx, da, dgate = _gate_w_out_bwd(dxo, yx, gate, w_out, nx, f"w_out_bwd_{tag}")
        gw_out = _mm_tn(a, dyx, f"w_out_grad_{tag}", BF16)
        dm, duv, dscale = _pool_grp_bwd(da, mixed, uv, pscale, w_grp, nx, f"pool_bwd_{tag}")
        gw_grp = _grp_wgrad(z, dm, w_grp.shape[0], f"grp_grad_{tag}", BF16)
        gw_in = _mm_tn_parts(h, duv, f"w_in_grad_{tag}", BF16)
        dx, dshift, dgeff = _w_in_bwd_norm(duv, w_in, xin, r, g, scale, dxo, nx, f"w_in_bwd_{tag}")
        dmod, dg = _norm_grads(dshift, dgeff, dgate, g[0], scale)
        return dx, dmod, dg, dict(w_in=gw_in, w_grp=gw_grp, w_out=gw_out, scale=dscale)

    return xout, backward


def _na_layer(xc, g, mod_l, w_in, rpb, w_out, nx, mask, comm=None):
    nh, n_dr, n_dc = rpb.shape
    shift, scale = _seg_vecs(mod_l, 0, 2), _seg_vecs(mod_l, 1, 2)
    gate = _seg_vecs(mod_l, 2, 1)
    h, r, p4 = _norm_w_in(xc, g, scale, shift, w_in, nx, "w_in_fwd_na")
    rp = jnp.pad(rpb, ((0, 0), (1, _RP_ROWS - 1 - n_dr), (0, _PAIR - n_dc)))
    (a, o, lse), carried = _attn_fwd(p4, rp, mask, nx, "attn_fwd", comm)
    yx, xout = _mm_out_resid(a, w_out, xc, gate, _NO_CTX, "w_out_fwd_na")

    def backward(dxo, comm=None):
        dyx, da, dgate = _gate_w_out_bwd(dxo, yx, gate, w_out, nx, "w_out_bwd_na")
        gw_out = _mm_tn(a, dyx, "w_out_grad_na", BF16)
        (d4, drp), carried_bwd = _attn_bwd(p4, rp, mask, o, lse, da, nx, "attn_bwd", comm)
        gw_in = _mm_tn_parts(h, d4, "w_in_grad_na", BF16)
        dx, dshift, dgeff = _w_in_bwd_norm(d4, w_in, xc, r, g, scale, dxo, nx, "w_in_bwd_na")
        dgate2 = jnp.concatenate([dgate, jnp.zeros_like(dgate)], axis=0)
        dmod, dg = _norm_grads(dshift, dgeff, dgate2, g[0], scale)
        drpb = drp[:, 1:1 + n_dr, ::-1][:, :, :n_dc]
        return dx, dmod, dg, dict(w_in=gw_in, w_out=gw_out, rpb=drpb), carried_bwd

    return xout, backward, carried


def _conv_layer(xin, g, mod_l, w_in, dw, db, w_out):
    shift, scale, gate = (_seg_vecs(mod_l, k, 1) for k in range(3))
    nx = xin.shape[0]
    h, r, p4 = _norm_w_in(xin, g, scale, shift, w_in, nx, "w_in_fwd_conv")
    a = _conv_fwd(p4, dw, db, "conv_fwd")
    yx, xout = _mm_out_resid(a, w_out, xin, gate, _NO_CTX, "w_out_fwd_conv")

    def backward(dxo):
        dyx, da, dgate = _gate_w_out_bwd(dxo, yx, gate, w_out, nx, "w_out_bwd_conv")
        gw_out = _mm_tn(a, dyx, "w_out_grad_conv", BF16)
        d4, ddw, ddb = _conv_bwd(da, p4, dw, db, "conv_bwd")
        gw_in = _mm_tn_parts(h, d4, "w_in_grad_conv", BF16)
        dx, dshift, dgeff = _w_in_bwd_norm(d4, w_in, xin, r, g, scale, dxo, nx, "w_in_bwd_conv")
        dmod, dg = _norm_grads(dshift, dgeff, dgate, g[0], scale)
        return dx, dmod, dg, dict(w_in=gw_in, w_out=gw_out, dw=ddw, db=ddb)

    return xout, backward


def _example_step(x, ctx, target, mod, norm_g, final_g, wts, late_comm=None, late_weights=None, grad_comm=None):
    nx = x.shape[0]
    consts = _attn_mask()
    g_rows = [norm_g[i:i + 1] for i in range(4)]
    xc0 = jnp.concatenate([x, ctx], axis=0)
    xc1, bwd0 = _pool_layer(xc0, g_rows[0], mod[0], wts["pool_w_in"][0], wts["pool_w_grp"][0], wts["pool_w_out"][0],
                            wts["pool_scale"][0:1], nx, "p0")
    x2, bwd1, carried = _na_layer(xc1, g_rows[1], mod[1], wts["na_w_in"], wts["na_rpb"], wts["na_w_out"], nx, consts,
                                  late_comm)
    if late_weights is not None:
        wts = {**wts, **late_weights(carried)}
    x3, bwd2 = _conv_layer(x2, g_rows[2], mod[2], wts["conv_w_in"], wts["conv_dw"], wts["conv_db"], wts["conv_w_out"])
    x4, bwd3 = _pool_layer(x3, g_rows[3], mod[3], wts["pool_w_in"][1], wts["pool_w_grp"][1], wts["pool_w_out"][1],
                           wts["pool_scale"][1:2], nx, "p3")
    loss, dx4, dfinal_g = _final(x4, final_g, target, "loss_head")
    dx3, dmod3, dg3, gr3 = bwd3(dx4)
    dx2, dmod2, dg2, gr2 = bwd2(dx3)
    dxc1, dmod1, dg1, gr1, carried_bwd = bwd1(dx2, grad_comm(gr3, gr2) if grad_comm else None)
    dxc0, dmod0, dg0, gr0 = bwd0(dxc1)
    return dict(
        loss=loss, grad_x=dxc0[:nx], dmod=jnp.stack([dmod0, dmod1, dmod2, dmod3]),
        dnorm_g=jnp.stack([dg0, dg1, dg2, dg3]), dfinal_g=dfinal_g, layers=(gr0, gr1, gr2, gr3), carried=carried_bwd)


_AXES = ("x", "y", "c")
_CHIP_FLIPS = ((1, 0), (0, 1), (1, 1))


def _position():
    return tuple(lax.axis_index(a) for a in _AXES)


def _flipped(pos, flip):
    return tuple(1 - p if f else p for p, f in zip(pos, flip))


def _all_gather(v, axes, name):
    flips = [f for f in np.ndindex(2, 2, 2) if any(f) and all(a in axes or not b for a, b in zip(_AXES, f))]
    n = len(flips) + 1

    def body(v_ref, o_ref, send_sems, recv_sems, local_sem):
        pos = _position()
        slot = 0
        for a, p in zip(_AXES, pos):
            if a in axes:
                slot = 2 * slot + p
        local = pltpu.make_async_copy(v_ref, o_ref.at[slot], local_sem)
        local.start()
        copies = []
        for k, flip in enumerate(flips):
            cp = pltpu.make_async_remote_copy(v_ref, o_ref.at[slot], send_sems.at[k], recv_sems.at[k],
                                              device_id=_flipped(pos, flip), device_id_type=MESH)
            cp.start()
            copies.append(cp)
        for cp in copies:
            cp.wait()
        local.wait()

    return pl.pallas_call(
        body, in_specs=[HBM_SPEC], out_specs=HBM_SPEC, out_shape=_sds((n,) + v.shape, v.dtype),
        scratch_shapes=[pltpu.SemaphoreType.DMA((n - 1,)), pltpu.SemaphoreType.DMA((n - 1,)), pltpu.SemaphoreType.DMA(())],
        name=name,
    )(v)


class _Item:
    def __init__(self, key, layer, shape, shard_axis, half_axis):
        self.key, self.layer, self.shape = key, layer, tuple(shape)
        self.shard_axis, self.half_axis = shard_axis, half_axis
        self.shard = shape[shard_axis] // 4
        self.half = shape[half_axis] // 2

    def sized(self, shard=False, half=False):
        s = list(self.shape)
        if shard:
            s[self.shard_axis] = self.shard
        if half:
            s[self.half_axis] = self.half
        return tuple(s)

    def window(self, ref, chip=None, half=None):
        idx = [slice(None)] * len(self.shape)
        if chip is not None:
            idx[self.shard_axis] = pl.ds(chip * self.shard, self.shard)
        if half is not None:
            idx[self.half_axis] = pl.ds(half * self.half, self.half)
        return ref.at[tuple(idx)]


def _items(d, w):
    out = []
    for j in range(2):
        out += [_Item("pool_w_in", j, (d, 2 * w), 1, 0), _Item("pool_w_grp", j, (4, w // 4, w // 4), 1, 0),
                _Item("pool_w_out", j, (w, d), 0, 1)]
    out += [_Item("na_w_in", 0, (d, 4 * w), 1, 0), _Item("na_w_out", 0, (w, d), 0, 1),
            _Item("conv_w_in", 0, (d, 4 * w), 1, 0), _Item("conv_w_out", 0, (w, d), 0, 1)]
    return out


def _gather_weights(shards, items, name):
    comm = _gather_comm(shards, items)

    def body(*refs):
        n = len(items)
        comm.start(refs[:n], refs[n:2 * n], refs[2 * n:])
        comm.finish(refs[:n], refs[n:2 * n], refs[2 * n:])

    return pl.pallas_call(
        body, in_specs=[HBM_SPEC] * len(items), out_specs=[HBM_SPEC] * len(items), out_shape=comm.outs,
        scratch_shapes=comm.sems, name=name,
    )(*shards)


def _gather_comm(shards, items):
    n = len(items)

    def copies(src, dst, sems, onward):
        send_a, recv_a, send_b, recv_b, send_c, recv_c = sems
        x, y, c = _position()
        chip = 2 * x + y
        sibling = (x, y, 1 - c)
        own, out, fwd, fwd_in = [], [], [], []
        for i, it in enumerate(items):
            own.append(pltpu.make_async_remote_copy(src[i], it.window(dst[i], chip=chip), send_c.at[i], recv_c.at[i],
                                                    device_id=sibling, device_id_type=MESH))
            for k, flip in enumerate(_CHIP_FLIPS):
                px, py = _flipped((x, y), flip)
                s = 3 * i + k
                out.append(pltpu.make_async_remote_copy(
                    it.window(src[i], half=c), it.window(dst[i], chip=chip, half=c), send_a.at[s], recv_a.at[s],
                    device_id=(px, py, c), device_id_type=MESH))
                if onward:
                    got = it.window(dst[i], chip=2 * px + py, half=c)
                    fwd.append(pltpu.make_async_remote_copy(got, got, send_b.at[s], recv_b.at[s],
                                                            device_id=sibling, device_id_type=MESH))
                    other = it.window(dst[i], chip=2 * px + py, half=1 - c)
                    fwd_in.append(pltpu.make_async_remote_copy(other, other, send_b.at[s], recv_b.at[s],
                                                               device_id=sibling, device_id_type=MESH))
        return own, out, fwd, fwd_in

    def start(src, dst, sems):
        own, out, _, _ = copies(src, dst, sems, False)
        for cp in own + out:
            cp.start()

    def finish(src, dst, sems):
        own, out, fwd, fwd_in = copies(src, dst, sems, True)
        for arrived, onward in zip(out, fwd):
            arrived.wait_recv()
            onward.start()
        for cp in fwd_in:
            cp.wait_recv()
        for cp in out + fwd:
            cp.wait_send()
        for cp in own:
            cp.wait()

    sems = [pltpu.SemaphoreType.DMA((3 * n,)) for _ in range(4)] + [pltpu.SemaphoreType.DMA((n,)) for _ in range(2)]
    return _Comm(shards, [_sds(it.shape, BF16) for it in items], sems, start, finish)


def _pair_swap(arrays, windows, out_shapes, name):
    n = len(arrays)

    def body(*refs):
        src, got = refs[:n], refs[n:2 * n]
        send_sems, recv_sems = refs[2 * n:]
        x, y, c = _position()
        copies = []
        for i in range(n):
            cp = pltpu.make_async_remote_copy(windows[i](src[i], 1 - c), got[i], send_sems.at[i], recv_sems.at[i],
                                              device_id=(x, y, 1 - c), device_id_type=MESH)
            cp.start()
            copies.append(cp)
        for cp in copies:
            cp.wait()

    return pl.pallas_call(
        body, in_specs=[HBM_SPEC] * n, out_specs=[HBM_SPEC] * n, out_shape=list(out_shapes),
        scratch_shapes=[pltpu.SemaphoreType.DMA((n,)), pltpu.SemaphoreType.DMA((n,))], name=name,
    )(*arrays)


def _chip_exchange(partials, items, name):
    comm = _chip_exchange_comm(partials, items)

    def body(*refs):
        n = len(items)
        comm.start(refs[:n], refs[n:2 * n], refs[2 * n:])
        comm.finish(refs[:n], refs[n:2 * n], refs[2 * n:])

    return pl.pallas_call(
        body, in_specs=[HBM_SPEC] * len(items), out_specs=[HBM_SPEC] * len(items), out_shape=comm.outs,
        scratch_shapes=comm.sems, name=name,
    )(*partials)


def _chip_exchange_comm(partials, items):
    n = len(items)

    def copies(src, dst, sems):
        send_sems, recv_sems = sems
        x, y, c = _position()
        out = []
        for i, it in enumerate(items):
            for k, flip in enumerate(_CHIP_FLIPS):
                px, py = _flipped((x, y), flip)
                out.append(pltpu.make_async_remote_copy(
                    it.window(src[i], chip=2 * px + py), dst[i].at[k], send_sems.at[3 * i + k],
                    recv_sems.at[3 * i + k], device_id=(px, py, c), device_id_type=MESH))
        return out

    def start(src, dst, sems):
        for cp in copies(src, dst, sems):
            cp.start()

    def finish(src, dst, sems):
        for cp in copies(src, dst, sems):
            cp.wait()

    return _Comm(partials, [_sds((3,) + it.sized(shard=True, half=True), BF16) for it in items],
                 [pltpu.SemaphoreType.DMA((3 * n,)), pltpu.SemaphoreType.DMA((3 * n,))], start, finish)


_SUM_STEPS = 2


def _pair_sums(gs, gots, its, pos, name):
    n = len(its)
    nb = _SUM_STEPS
    g2 = [g.reshape(-1, g.shape[-1]) for g in gs]
    got2 = [t.reshape(-1, t.shape[-1]) for t in gots]

    def body(pos_ref, *refs):
        for g_ref, got_ref, o_ref in zip(refs[:n], refs[n:2 * n], refs[2 * n:]):
            o_ref[...] = (g_ref[...].astype(F32) + got_ref[...].astype(F32)).astype(BF16)

    g_specs, got_specs = [], []
    for it, t in zip(its, got2):
        rows, cols = t.shape
        blk = (rows // nb, cols)
        g_map = (lambda i, pos: (pos[1] * nb + i, 0)) if it.half_axis == 0 else (lambda i, pos: (i, pos[1]))
        g_specs.append(pl.BlockSpec(blk, g_map))
        got_specs.append(pl.BlockSpec(blk, lambda i, pos: (i, 0)))
    outs = pl.pallas_call(
        body, grid_spec=pltpu.PrefetchScalarGridSpec(
            num_scalar_prefetch=1, grid=(nb,), in_specs=g_specs + got_specs, out_specs=got_specs),
        out_shape=[_sds(t.shape, BF16) for t in got2], name=name, compiler_params=_cparams("parallel"),
    )(pos, *g2, *got2)
    return [o.reshape(t.shape) for o, t in zip(outs, gots)]


_FLIP_SLOT = {2: 0, 1: 1, 3: 2}


def _chip_sums(pairs, slots, its, pos, name):
    n = len(its)
    nb = _SUM_STEPS

    def body(pos_ref, *refs):
        chip = pos_ref[0]
        for own in range(4):
            @pl.when(chip == own)
            def _():
                for p_ref, s_ref, o_ref in zip(refs[:n], refs[n:2 * n], refs[2 * n:]):
                    acc = None
                    for k in range(4):
                        v = (p_ref[...] if k == own else s_ref[_FLIP_SLOT[own ^ k]]).astype(F32)
                        acc = v if acc is None else acc + v
                    o_ref[...] = acc

    p_specs, s_specs, o_specs, shapes = [], [], [], []
    for it in its:
        shape = it.sized(shard=True, half=True)
        blk = (shape[0] // nb,) + shape[1:]
        rest = (0,) * (len(shape) - 1)

        def p_map(i, pos, it=it, nd=len(shape)):
            lead = i + (pos[0] * nb if it.shard_axis == 0 else 0)
            return (lead,) + tuple(pos[0] if ax == it.shard_axis else 0 for ax in range(1, nd))

        p_specs.append(pl.BlockSpec(blk, p_map))
        s_specs.append(pl.BlockSpec((3,) + blk, lambda i, pos, rest=rest: (0, i) + rest))
        o_specs.append(pl.BlockSpec(blk, lambda i, pos, rest=rest: (i,) + rest))
        shapes.append(_sds(shape, F32))
    return pl.pallas_call(
        body, grid_spec=pltpu.PrefetchScalarGridSpec(
            num_scalar_prefetch=1, grid=(nb,), in_specs=p_specs + s_specs, out_specs=o_specs),
        out_shape=shapes, name=name, compiler_params=_cparams("parallel"),
    )(pos, *pairs, *slots)


_GRAD_KEYS = ("pool_w_in", "pool_w_grp", "pool_w_out", "na_w_in", "na_w_out", "conv_w_in", "conv_w_out")


def _adamw_matrix(w, m, v, owns, others, it, pos, name):
    nl = w.shape[0]
    rows_split = it.half_axis == 0
    r, cdim = int(np.prod(w.shape[1:-1])), w.shape[-1]
    hr, hc = (r // 2, cdim) if rows_split else (r, cdim // 2)
    br = min(hr, 256)
    nb = hr // br
    c1 = 1.0 - ADAM_B1 ** ADAM_STEP
    c2 = 1.0 - ADAM_B2 ** ADAM_STEP

    def body(pos_ref, w_ref, m_ref, v_ref, *rest):
        own_refs, other_refs = rest[:nl], rest[nl:2 * nl]
        g_ref, d_ref, nm_ref, nv_ref = rest[2 * nl:]
        j, h = pl.program_id(0), pl.program_id(1)
        own, other = own_refs[0][...], other_refs[0][...]
        for q in range(1, nl):
            own = jnp.where(j == q, own_refs[q][...], own)
            other = jnp.where(j == q, other_refs[q][...], other)
        gv = jnp.where(h == pos_ref[1], own, other)
        nm = ADAM_B1 * m_ref[...] + (1.0 - ADAM_B1) * gv
        nv = ADAM_B2 * v_ref[...] + (1.0 - ADAM_B2) * (gv * gv)
        g_ref[...] = gv
        nm_ref[...] = nm
        nv_ref[...] = nv
        d_ref[...] = -ADAM_LR * ((nm / c1) / (jnp.sqrt(nv / c2) + ADAM_EPS) + ADAM_WD * w_ref[...])

    if rows_split:
        full = pl.BlockSpec((None, br, hc), lambda j, h, i, pos: (j, h * nb + i, 0))
    else:
        full = pl.BlockSpec((None, br, hc), lambda j, h, i, pos: (j, i, h))
    half = pl.BlockSpec((br, hc), lambda j, h, i, pos: (i, 0))
    flat = lambda t: t.reshape(nl, r, cdim)
    outs = pl.pallas_call(
        body, grid_spec=pltpu.PrefetchScalarGridSpec(
            num_scalar_prefetch=1, grid=(nl, 2, nb), in_specs=[full] * 3 + [half] * (2 * nl), out_specs=[full] * 4),
        out_shape=[_sds((nl, r, cdim), F32)] * 4, name=name,
        compiler_params=_cparams("parallel", "parallel", "parallel"),
    )(pos, flat(w), flat(m), flat(v), *[t.reshape(hr, hc) for t in list(owns) + list(others)])
    return tuple(t.reshape(w.shape) for t in outs)


_WEIGHTS = ("c_ctx", "norm_g", "ada_w", "ada_b", "pool_w_in", "pool_w_grp", "pool_scale", "pool_w_out", "na_w_in",
            "na_rpb", "na_w_out", "conv_w_in", "conv_dw", "conv_db", "conv_w_out", "final_g")
_COND_ROWS = 16


def _modulations(cond, ada_w, ada_b_cols):
    nl, d, n = ada_w.shape
    return _matmul(
        cond, ada_w, mode="nn", grid=(nl, 1), a_silu=True, epilogue="bias",
        a_spec=pl.BlockSpec((_COND_ROWS, d), lambda i, j: (0, 0)), b_spec=pl.BlockSpec((None, d, n), lambda i, j: (i, 0, 0)),
        extra=(ada_b_cols,), extra_specs=(pl.BlockSpec((None, 1, n), lambda i, j: (i, 0, 0)),),
        out_shapes=[_sds((nl, _COND_ROWS, n), F32)], out_specs=[pl.BlockSpec((None, _COND_ROWS, n), lambda i, j: (i, 0, 0))],
        name="modulations")[0]


def _ada_w_grad(cond, dm_cols):
    d = cond.shape[1]
    nl, _, n = dm_cols.shape
    return _matmul(
        cond, dm_cols, mode="tn", grid=(nl, 1), a_silu=True,
        a_spec=pl.BlockSpec((_COND_ROWS, d), lambda i, j: (0, 0)), b_spec=pl.BlockSpec((None, _COND_ROWS, n), lambda i, j: (i, 0, 0)),
        out_shapes=[_sds((nl, d, n), F32)], out_specs=[pl.BlockSpec((None, d, n), lambda i, j: (i, 0, 0))],
        name="ada_w_grad")[0]


def _cond_grad(dm_cols, ada_w):
    nl, d, n = ada_w.shape
    return _matmul(
        dm_cols, ada_w, mode="nt", grid=(1, nl), nk=nl, acc_shape=(_COND_ROWS, d),
        a_spec=pl.BlockSpec((None, _COND_ROWS, n), lambda i, q: (q, 0, 0)), b_spec=pl.BlockSpec((None, d, n), lambda i, q: (q, 0, 0)),
        out_shapes=[_sds((_COND_ROWS, d), F32)], out_specs=[pl.BlockSpec((_COND_ROWS, d), lambda i, q: (0, 0))],
        name="cond_grad")[0]


def _pack(parts):
    flat = [p.reshape(-1) for p in parts]
    sizes = [f.shape[0] for f in flat]
    total = sum(sizes)
    rows = -(-total // 1024) * 8
    packed = jnp.concatenate(flat + [jnp.zeros((rows * 128 - total,), F32)]).reshape(rows, 128)
    offs = np.concatenate([[0], np.cumsum(sizes)])[:-1]
    return packed, [(int(o), p.shape) for o, p in zip(offs, parts)]


def _unpack(flat, layout, k):
    off, shape = layout[k]
    return flat[..., off:off + int(np.prod(shape))].reshape(flat.shape[:-1] + tuple(shape))


def kernel(x, c, ctx, c_ctx, norm_g, ada_w, ada_b, pool_w_in, pool_w_grp, pool_scale, pool_w_out, na_w_in, na_rpb, na_w_out, conv_w_in, conv_dw, conv_db, conv_w_out, final_g, loss_target, m_c_ctx, m_norm_g, m_ada_w, m_ada_b, m_pool_w_in, m_pool_w_grp, m_pool_scale, m_pool_w_out, m_na_w_in, m_na_rpb, m_na_w_out, m_conv_w_in, m_conv_dw, m_conv_db, m_conv_w_out, m_final_g, v_c_ctx, v_norm_g, v_ada_w, v_ada_b, v_pool_w_in, v_pool_w_grp, v_pool_scale, v_pool_w_out, v_na_w_in, v_na_rpb, v_na_w_out, v_conv_w_in, v_conv_dw, v_conv_db, v_conv_w_out, v_final_g):
    params = dict(c_ctx=c_ctx, norm_g=norm_g, ada_w=ada_w, ada_b=ada_b, pool_w_in=pool_w_in, pool_w_grp=pool_w_grp,
                  pool_scale=pool_scale, pool_w_out=pool_w_out, na_w_in=na_w_in, na_rpb=na_rpb, na_w_out=na_w_out,
                  conv_w_in=conv_w_in, conv_dw=conv_dw, conv_db=conv_db, conv_w_out=conv_w_out, final_g=final_g)
    mom1 = dict(c_ctx=m_c_ctx, norm_g=m_norm_g, ada_w=m_ada_w, ada_b=m_ada_b, pool_w_in=m_pool_w_in,
                pool_w_grp=m_pool_w_grp, pool_scale=m_pool_scale, pool_w_out=m_pool_w_out, na_w_in=m_na_w_in,
                na_rpb=m_na_rpb, na_w_out=m_na_w_out, conv_w_in=m_conv_w_in, conv_dw=m_conv_dw, conv_db=m_conv_db,
                conv_w_out=m_conv_w_out, final_g=m_final_g)
    mom2 = dict(c_ctx=v_c_ctx, norm_g=v_norm_g, ada_w=v_ada_w, ada_b=v_ada_b, pool_w_in=v_pool_w_in,
                pool_w_grp=v_pool_w_grp, pool_scale=v_pool_scale, pool_w_out=v_pool_w_out, na_w_in=v_na_w_in,
                na_rpb=v_na_rpb, na_w_out=v_na_w_out, conv_w_in=v_conv_w_in, conv_dw=v_conv_dw, conv_db=v_conv_db,
                conv_w_out=v_conv_w_out, final_g=v_final_g)
    d = x.shape[-1]
    w = na_w_out.shape[1] * 4
    xi, yi, ci = _position()
    chip = 2 * xi + yi
    dev = 2 * chip + ci
    n_ada = ada_w.shape[-1]

    def chip_cols(a, size):
        return lax.dynamic_slice_in_dim(a, chip * size, size, axis=a.ndim - 1)

    conds = _all_gather(c.reshape(8, d // 8), _AXES, "gather_cond").reshape(8, d)
    cond = jnp.concatenate([conds, c_ctx[None], jnp.zeros((_COND_ROWS - 9, d), F32)], axis=0)
    mod_cols = _modulations(cond, ada_w, chip_cols(ada_b, n_ada)[:, None, :])
    mod_all = _all_gather(mod_cols, ("x", "y"), "gather_mod")
    mod_all = mod_all.transpose(1, 2, 0, 3).reshape(4, _COND_ROWS, 3, d)
    mod = jnp.stack([lax.dynamic_index_in_dim(mod_all, dev, axis=1, keepdims=False), mod_all[:, 8]], axis=1)

    items = _items(d, w)
    early = [it for it in items if (it.key.startswith("pool") and it.layer == 0) or it.key.startswith("na")]
    late = [it for it in items if it not in early]
    shard_of = lambda it: params[it.key][it.layer].astype(BF16)
    full = {(it.key, it.layer): mat
            for it, mat in zip(early, _gather_weights([shard_of(it) for it in early], early, "gather_weights"))}
    late_comm = _gather_comm([shard_of(it) for it in late], late)

    def late_weights(mats):
        full.update({(it.key, it.layer): mat for it, mat in zip(late, mats)})
        return dict(pool_w_in=[full[("pool_w_in", j)] for j in range(2)],
                    pool_w_grp=[full[("pool_w_grp", j)] for j in range(2)],
                    pool_w_out=[full[("pool_w_out", j)] for j in range(2)],
                    conv_w_in=full[("conv_w_in", 0)], conv_w_out=full[("conv_w_out", 0)])

    small = _all_gather(_pack([pool_scale, conv_dw, conv_db])[0], ("x", "y"), "gather_small")
    small_layout = _pack([pool_scale, conv_dw, conv_db])[1]
    small = small.reshape(4, -1)

    def whole(k):
        parts = _unpack(small, small_layout, k)
        return jnp.moveaxis(parts, 0, -2).reshape(parts.shape[1:-1] + (-1,))

    wts = dict(pool_w_in=[full[("pool_w_in", 0)]], pool_w_grp=[full[("pool_w_grp", 0)]],
               pool_w_out=[full[("pool_w_out", 0)]], na_w_in=full[("na_w_in", 0)], na_w_out=full[("na_w_out", 0)],
               pool_scale=whole(0), na_rpb=na_rpb[0], conv_dw=whole(1)[0], conv_db=whole(2))
    pos = jnp.stack([chip, ci]).astype(jnp.int32)

    def layer_grads(its, by_layer):
        pick = {"pool_w_in": "w_in", "pool_w_grp": "w_grp", "pool_w_out": "w_out", "na_w_in": "w_in",
                "na_w_out": "w_out", "conv_w_in": "w_in", "conv_w_out": "w_out"}
        return [by_layer[(it.key.split("_")[0], it.layer)][pick[it.key]] for it in its]

    def pair_sums(its, mats, tag):
        got = _pair_swap(mats, [(lambda ref, half, it=it: it.window(ref, half=half)) for it in its],
                         [_sds(it.sized(half=True), BF16) for it in its], f"pair_exchange_{tag}")
        return _pair_sums(mats, got, its, pos, f"pair_sum_{tag}")

    pairs = dict()

    def grad_comm(gr3, gr2):
        pairs["late"] = pair_sums(late, layer_grads(late, {("pool", 1): gr3, ("conv", 0): gr2}), "late")
        return _chip_exchange_comm(pairs["late"], late)

    res = _example_step(x[0], ctx[0], loss_target[0], mod, norm_g, final_g[None], wts, late_comm, late_weights,
                        grad_comm)
    g0, g1, g2, g3 = res["layers"]
    pairs["early"] = pair_sums(early, layer_grads(early, {("pool", 0): g0, ("na", 0): g1}), "early")
    slots = dict(zip(late, res["carried"]))
    slots.update(zip(early, _chip_exchange(pairs["early"], early, "chip_exchange")))
    pair_of = dict(zip(late, pairs["late"]))
    pair_of.update(zip(early, pairs["early"]))
    reduced = _chip_sums([pair_of[it] for it in items], [slots[it] for it in items], items, pos, "chip_sum")
    theirs = _pair_swap(reduced, [lambda ref, half: ref] * len(items),
                        [_sds(t.shape, F32) for t in reduced], "pair_return")
    grads, matrix_out = dict(), dict()
    for k in _GRAD_KEYS:
        idx = [i for i, it in enumerate(items) if it.key == k]
        res_k = _adamw_matrix(params[k], mom1[k], mom2[k], [reduced[i] for i in idx], [theirs[i] for i in idx],
                              items[idx[0]], pos, f"adamw_{k}")
        grads[k], matrix_out[k] = res_k[0], res_k[1:]

    packed, layout = _pack([res["dfinal_g"], res["dnorm_g"], res["dmod"], g1["rpb"],
                            jnp.concatenate([g0["scale"], g3["scale"]], axis=0), g2["dw"], g2["db"],
                            res["loss"][0, :1]])
    every = _all_gather(packed, _AXES, "gather_vec_grads")
    total = _sum_lead(every, "sum_vec_grads").reshape(-1)
    every = every.reshape(8, -1)
    grads["final_g"] = _unpack(total, layout, 0).reshape(final_g.shape)
    grads["norm_g"] = _unpack(total, layout, 1)
    grads["na_rpb"] = _unpack(total, layout, 3)[None]
    grads["pool_scale"] = chip_cols(_unpack(total, layout, 4), pool_scale.shape[-1])
    grads["conv_dw"] = chip_cols(_unpack(total, layout, 5), conv_dw.shape[-1])[None]
    grads["conv_db"] = chip_cols(_unpack(total, layout, 6), conv_db.shape[-1])
    dmod_sum = _unpack(total, layout, 2).reshape(4, 2, 3 * d)
    dmod_each = _unpack(every, layout, 2).reshape(8, 4, 2, 3 * d)
    grads["ada_b"] = dmod_sum[:, 0] + dmod_sum[:, 1]
    dm = jnp.concatenate([dmod_each[:, :, 0].transpose(1, 0, 2), dmod_sum[:, 1][:, None],
                          jnp.zeros((4, _COND_ROWS - 9, 3 * d), F32)], axis=1)
    dm_cols = chip_cols(dm, n_ada)
    grads["ada_w"] = _ada_w_grad(cond, dm_cols)
    dcond = _cond_grad(dm_cols, ada_w)[8].reshape(8, d // 8)
    dcond = _sum_lead(_all_gather(dcond, ("x", "y"), "gather_cond_grad"), "sum_cond_grad").reshape(d)
    grads["c_ctx"] = dcond * _dsilu(c_ctx)

    outs = [[], [], []]
    for k in _WEIGHTS:
        step = matrix_out[k] if k in matrix_out else _adamw(params[k], grads[k], mom1[k], mom2[k], f"adamw_{k}")
        for lst, val in zip(outs, step):
            lst.append(val)
    loss = _unpack(total, layout, 7)[0]
    return (loss, res["grad_x"][None], *[grads[k].reshape(params[k].shape) for k in _WEIGHTS],
            *outs[0], *outs[1], *outs[2])
```

```python
import functools

import numpy as np
import jax
import jax.numpy as jnp
from jax import lax
from jax.experimental import pallas as pl
from jax.experimental.pallas import tpu as pltpu

F32 = jnp.float32
BF16 = jnp.bfloat16

EPS = 1e-6
GRID_W = 64
HEAD_DIM = 64
WIN_ROWS = 8
WIN_COLS = 16
POOL_WINDOWS = (2, 4, 8, 16)
Q_ROWS = 4
K_ROWS = 12
PAD_ROWS = 4
NEG = -1e30

ADAM_LR = 0.001
ADAM_B1 = 0.9
ADAM_B2 = 0.999
ADAM_EPS = 1e-08
ADAM_WD = 0.01
ADAM_STEP = 10

ROW_BLOCK = 256
VMEM_LIMIT = 56 * 1024 * 1024
ACT = BF16

MESH = pl.DeviceIdType.MESH
HBM_SPEC = pl.BlockSpec(memory_space=pltpu.HBM)


def _cparams(*sem):
    return pltpu.CompilerParams(dimension_semantics=sem or None, vmem_limit_bytes=VMEM_LIMIT)


def _sds(shape, dtype):
    return jax.ShapeDtypeStruct(tuple(shape), dtype)


def _sigmoid(x):
    return 1.0 / (1.0 + jnp.exp(-x))


def _silu(x):
    return x * _sigmoid(x)


def _dsilu(x):
    s = _sigmoid(x)
    return s * (1.0 + x * (1.0 - s))


_DIMS = {
    "nn": (((1,), (0,)), ((), ())),
    "nt": (((1,), (1,)), ((), ())),
    "tn": (((0,), (0,)), ((), ())),
}


def _matmul(a, b, *, mode, grid, a_spec, b_spec, out_shapes, out_specs, name, nk=1,
            a_silu=False, exact=False, epilogue=None, extra=(), extra_specs=(), acc_shape=None):
    n_extra = len(extra)
    n_out = len(out_shapes)

    def body(*refs):
        a_ref, b_ref = refs[:2]
        ex = refs[2:2 + n_extra]
        outs = refs[2 + n_extra:2 + n_extra + n_out]
        av = a_ref[...]
        bv = b_ref[...]
        if a_silu:
            av = _silu(av.astype(F32))
        if exact:
            prod = lax.dot_general(av.astype(F32), bv.astype(F32), _DIMS[mode],
                                   precision=lax.Precision.HIGHEST, preferred_element_type=F32)
        else:
            prod = lax.dot_general(av.astype(BF16), bv.astype(BF16), _DIMS[mode], preferred_element_type=F32)

        def finish(res):
            if epilogue is None:
                outs[0][...] = res.astype(outs[0].dtype)
            elif epilogue == "bias":
                outs[0][...] = (res + ex[0][...]).astype(outs[0].dtype)
            else:
                outs[0][...] = res.astype(outs[0].dtype)
                outs[1][...] = ex[0][...] + ex[1][...] * res

        if nk == 1:
            finish(prod)
        else:
            acc = refs[-1]
            k = pl.program_id(len(grid) - 1)

            @pl.when(k == 0)
            def _():
                acc[...] = prod

            @pl.when(k > 0)
            def _():
                acc[...] += prod

            @pl.when(k == nk - 1)
            def _():
                finish(acc[...])

    scratch = [pltpu.VMEM(acc_shape, F32)] if nk > 1 else []
    sem = ("parallel",) * (len(grid) - 1) + ("arbitrary",)
    return pl.pallas_call(
        body, grid=grid, in_specs=[a_spec, b_spec, *extra_specs], out_specs=list(out_specs),
        out_shape=list(out_shapes), scratch_shapes=scratch, name=name, compiler_params=_cparams(*sem),
    )(a, b, *extra)


def _row_tile(rows):
    for t in (768, 512, 256):
        if rows % t == 0:
            return t
    return rows


def _mm_nn(a, b, name, out_dtype=F32, tn=1024):
    m, k = a.shape
    n = b.shape[1]
    tm = _row_tile(m)
    tn = min(tn, n)
    return _matmul(
        a, b, mode="nn", grid=(m // tm, n // tn),
        a_spec=pl.BlockSpec((tm, k), lambda i, j: (i, 0)), b_spec=pl.BlockSpec((k, tn), lambda i, j: (0, j)),
        out_shapes=[_sds((m, n), out_dtype)], out_specs=[pl.BlockSpec((tm, tn), lambda i, j: (i, j))], name=name)[0]


def _mm_out_resid(a, w_out, xres, gate, nxb, name):
    m, k = a.shape
    n = w_out.shape[1]
    tm = ROW_BLOCK
    seg = lambda i, j: (jnp.where(i >= nxb, 1, 0), 0, 0)
    return _matmul(
        a, w_out, mode="nn", grid=(m // tm, 1),
        a_spec=pl.BlockSpec((tm, k), lambda i, j: (i, 0)), b_spec=pl.BlockSpec((k, n), lambda i, j: (0, 0)),
        extra=(xres, gate), extra_specs=(pl.BlockSpec((tm, n), lambda i, j: (i, 0)), pl.BlockSpec((None, 1, n), seg)),
        out_shapes=[_sds((m, n), ACT), _sds((m, n), F32)],
        out_specs=[pl.BlockSpec((tm, n), lambda i, j: (i, 0))] * 2, epilogue="resid", name=name)


def _mm_nt(a, b, name, out_dtype=F32):
    m, n = a.shape
    k = b.shape[0]
    tm = _row_tile(m)
    return _matmul(
        a, b, mode="nt", grid=(m // tm, 1),
        a_spec=pl.BlockSpec((tm, n), lambda i, j: (i, 0)), b_spec=pl.BlockSpec((k, n), lambda i, j: (0, 0)),
        out_shapes=[_sds((m, k), out_dtype)], out_specs=[pl.BlockSpec((tm, k), lambda i, j: (i, 0))], name=name)[0]


def _mm_nt_parts(a, b, name):
    p, m, kp = a.shape
    d = b.shape[0]
    tm = _row_tile(m)
    return _matmul(
        a, b, mode="nt", grid=(m // tm, p), nk=p, acc_shape=(tm, d),
        a_spec=pl.BlockSpec((None, tm, kp), lambda i, q: (q, i, 0)), b_spec=pl.BlockSpec((d, kp), lambda i, q: (0, q)),
        out_shapes=[_sds((m, d), F32)], out_specs=[pl.BlockSpec((tm, d), lambda i, q: (i, 0))], name=name)[0]


def _mm_tn(a, b, name, out_dtype, tm=512):
    r, m = a.shape
    n = b.shape[1]
    tm = min(tm, m)
    tn = min(1024, n)
    return _matmul(
        a, b, mode="tn", grid=(m // tm, n // tn),
        a_spec=pl.BlockSpec((r, tm), lambda i, j: (0, i)), b_spec=pl.BlockSpec((r, tn), lambda i, j: (0, j)),
        out_shapes=[_sds((m, n), out_dtype)], out_specs=[pl.BlockSpec((tm, tn), lambda i, j: (i, j))], name=name)[0]


def _mm_tn_parts(a, b, name, out_dtype, tm=512):
    r, m = a.shape
    p, _, np_ = b.shape
    tm = min(tm, m)
    return _matmul(
        a, b, mode="tn", grid=(m // tm, p),
        a_spec=pl.BlockSpec((r, tm), lambda i, q: (0, i)), b_spec=pl.BlockSpec((None, r, np_), lambda i, q: (q, 0, 0)),
        out_shapes=[_sds((m, p * np_), out_dtype)], out_specs=[pl.BlockSpec((tm, np_), lambda i, q: (i, q))],
        name=name)[0]


def _seg_map(nxb):
    return lambda i: (jnp.where(i >= nxb, 1, 0), 0, 0)


def _normmod_fwd(x, g, scale, shift, nxb, name):
    rows, d = x.shape
    tr = ROW_BLOCK

    def body(x_ref, g_ref, sc_ref, sh_ref, h_ref, r_ref):
        xv = x_ref[...]
        r = lax.rsqrt(jnp.mean(xv * xv, axis=-1, keepdims=True) + EPS)
        h = (xv * r) * g_ref[...] * (1.0 + sc_ref[...]) + sh_ref[...]
        h_ref[...] = h.astype(BF16)
        r_ref[...] = r

    row = pl.BlockSpec((tr, d), lambda i: (i, 0))
    vec = pl.BlockSpec((None, 1, d), _seg_map(nxb))
    return pl.pallas_call(
        body, grid=(rows // tr,), in_specs=[row, pl.BlockSpec((1, d), lambda i: (0, 0)), vec, vec],
        out_specs=[row, pl.BlockSpec((tr, 1), lambda i: (i, 0))],
        out_shape=[_sds((rows, d), BF16), _sds((rows, 1), F32)], name=name, compiler_params=_cparams("parallel"),
    )(x, g, scale, shift)


def _normmod_bwd(dh, x, r, g, scale, dres, nxb, name):
    rows, d = x.shape
    tr = ROW_BLOCK
    nres = dres.shape[0] // tr
    nseg = scale.shape[0]

    def body(dh_ref, x_ref, r_ref, g_ref, sc_ref, dres_ref, dx_ref, dsh_ref, dge_ref):
        i = pl.program_id(0)
        dhv = dh_ref[...]
        rv = r_ref[...]
        xn = x_ref[...] * rv
        dxn = dhv * (g_ref[...] * (1.0 + sc_ref[...]))
        dx = rv * (dxn - xn * jnp.mean(dxn * xn, axis=-1, keepdims=True))

        @pl.when(i < nres)
        def _():
            dx_ref[...] = dx + dres_ref[...]

        @pl.when(i >= nres)
        def _():
            dx_ref[...] = dx

        first = jnp.logical_or(i == 0, i == nxb)
        s_dh = jnp.sum(dhv, axis=0, keepdims=True)
        s_ge = jnp.sum(dhv * xn, axis=0, keepdims=True)

        @pl.when(first)
        def _():
            dsh_ref[...] = s_dh
            dge_ref[...] = s_ge

        @pl.when(jnp.logical_not(first))
        def _():
            dsh_ref[...] += s_dh
            dge_ref[...] += s_ge

    row = pl.BlockSpec((tr, d), lambda i: (i, 0))
    vec = pl.BlockSpec((None, 1, d), _seg_map(nxb))
    return pl.pallas_call(
        body, grid=(rows // tr,),
        in_specs=[row, row, pl.BlockSpec((tr, 1), lambda i: (i, 0)), pl.BlockSpec((1, d), lambda i: (0, 0)), vec,
                  pl.BlockSpec((tr, d), lambda i: (jnp.minimum(i, nres - 1), 0))],
        out_specs=[row, vec, vec],
        out_shape=[_sds((rows, d), F32), _sds((nseg, 1, d), F32), _sds((nseg, 1, d), F32)],
        name=name, compiler_params=_cparams("arbitrary"),
    )(dh, x, r, g, scale, dres)


def _gate_bwd(dxo, yx, gate, nxb, name):
    rows, d = yx.shape
    tr = ROW_BLOCK
    nseg = gate.shape[0]

    def body(dx_ref, yx_ref, gt_ref, dyx_ref, dg_ref):
        i = pl.program_id(0)
        dxv = dx_ref[...]
        dyx_ref[...] = (dxv * gt_ref[...]).astype(BF16)
        s = jnp.sum(dxv * yx_ref[...].astype(F32), axis=0, keepdims=True)
        first = jnp.logical_or(i == 0, i == nxb)

        @pl.when(first)
        def _():
            dg_ref[...] = s

        @pl.when(jnp.logical_not(first))
        def _():
            dg_ref[...] += s

    row = pl.BlockSpec((tr, d), lambda i: (i, 0))
    vec = pl.BlockSpec((None, 1, d), _seg_map(nxb))
    return pl.pallas_call(
        body, grid=(rows // tr,), in_specs=[row, row, vec], out_specs=[row, vec],
        out_shape=[_sds((rows, d), BF16), _sds((nseg, 1, d), F32)], name=name, compiler_params=_cparams("arbitrary"),
    )(dxo, yx, gate)


def _row_vec(ref, is_ctx):
    return ref[0] if is_ctx is None else jnp.where(is_ctx, ref[1], ref[0])


def _ctx_rows(i, tm, nx, nseg):
    if nseg == 1:
        return None
    return i * tm + lax.broadcasted_iota(jnp.int32, (tm, 1), 0) >= nx


def _seg_sums(ref, val, is_ctx, first):
    if is_ctx is None:
        parts = [jnp.sum(val, axis=0, keepdims=True)]
    else:
        parts = [jnp.sum(jnp.where(is_ctx, 0.0, val), axis=0, keepdims=True),
                 jnp.sum(jnp.where(is_ctx, val, 0.0), axis=0, keepdims=True)]

    @pl.when(first)
    def _():
        for k, p in enumerate(parts):
            ref[k] = p

    @pl.when(jnp.logical_not(first))
    def _():
        for k, p in enumerate(parts):
            ref[k] += p


def _w_out_resid(a, w_out, xres, gate, nx, name):
    m, k = a.shape
    n = w_out.shape[1]
    nseg = gate.shape[0]
    tm = _row_tile(m)

    def body(a_ref, w_ref, x_ref, gt_ref, yx_ref, xo_ref):
        yx = jnp.dot(a_ref[...], w_ref[...], preferred_element_type=F32)
        yx_ref[...] = yx.astype(ACT)
        xo_ref[...] = x_ref[...] + _row_vec(gt_ref, _ctx_rows(pl.program_id(0), tm, nx, nseg)) * yx

    row = pl.BlockSpec((tm, n), lambda i: (i, 0))
    return pl.pallas_call(
        body, grid=(m // tm,),
        in_specs=[pl.BlockSpec((tm, k), lambda i: (i, 0)), pl.BlockSpec((k, n), lambda i: (0, 0)), row,
                  pl.BlockSpec((nseg, 1, n), lambda i: (0, 0, 0))],
        out_specs=[row, row], out_shape=[_sds((m, n), ACT), _sds((m, n), F32)],
        name=name, compiler_params=_cparams("parallel"),
    )(a, w_out, xres, gate)


def _norm_w_in(x, g, scale, shift, w_in, nx, name):
    rows, d = x.shape
    n = w_in.shape[1]
    nseg = scale.shape[0]
    tm = _row_tile(rows)
    tn = min(1024, n)

    def body(x_ref, g_ref, sc_ref, sh_ref, w_ref, h_ref, r_ref, p_ref):
        i, j = pl.program_id(0), pl.program_id(1)

        @pl.when(j == 0)
        def _():
            xv = x_ref[...]
            r = lax.rsqrt(jnp.mean(xv * xv, axis=-1, keepdims=True) + EPS)
            is_ctx = _ctx_rows(i, tm, nx, nseg)
            h = (xv * r) * g_ref[...] * (1.0 + _row_vec(sc_ref, is_ctx)) + _row_vec(sh_ref, is_ctx)
            h_ref[...] = h.astype(BF16)
            r_ref[...] = r

        p_ref[...] = jnp.dot(h_ref[...], w_ref[...], preferred_element_type=F32).astype(ACT)

    vec = pl.BlockSpec((nseg, 1, d), lambda i, j: (0, 0, 0))
    return pl.pallas_call(
        body, grid=(rows // tm, n // tn),
        in_specs=[pl.BlockSpec((tm, d), lambda i, j: (i, 0)), pl.BlockSpec((1, d), lambda i, j: (0, 0)), vec, vec,
                  pl.BlockSpec((d, tn), lambda i, j: (0, j))],
        out_specs=[pl.BlockSpec((tm, d), lambda i, j: (i, 0)), pl.BlockSpec((tm, 1), lambda i, j: (i, 0)),
                   pl.BlockSpec((tm, tn), lambda i, j: (i, j))],
        out_shape=[_sds((rows, d), BF16), _sds((rows, 1), F32), _sds((rows, n), ACT)],
        name=name, compiler_params=_cparams("parallel", "arbitrary"),
    )(x, g, scale, shift, w_in)


def _gate_w_out_bwd(dxo, yx, gate, w_out, nx, name):
    rows, d = yx.shape
    w = w_out.shape[0]
    nseg = gate.shape[0]
    tm = _row_tile(rows)

    def body(dx_ref, yx_ref, gt_ref, w_ref, dyx_ref, da_ref, dg_ref):
        i = pl.program_id(0)
        is_ctx = _ctx_rows(i, tm, nx, nseg)
        dxv = dx_ref[...]
        dyx = (dxv * _row_vec(gt_ref, is_ctx)).astype(BF16)
        dyx_ref[...] = dyx
        da_ref[...] = lax.dot_general(dyx, w_ref[...], _DIMS["nt"], preferred_element_type=F32).astype(ACT)
        _seg_sums(dg_ref, dxv * yx_ref[...].astype(F32), is_ctx, i == 0)

    row = pl.BlockSpec((tm, d), lambda i: (i, 0))
    vec = pl.BlockSpec((nseg, 1, d), lambda i: (0, 0, 0))
    return pl.pallas_call(
        body, grid=(rows // tm,), in_specs=[row, row, vec, pl.BlockSpec((w, d), lambda i: (0, 0))],
        out_specs=[row, pl.BlockSpec((tm, w), lambda i: (i, 0)), vec],
        out_shape=[_sds((rows, d), BF16), _sds((rows, w), ACT), _sds((nseg, 1, d), F32)],
        name=name, compiler_params=_cparams("arbitrary"),
    )(dxo, yx, gate, w_out)


def _w_in_bwd_norm(dparts, w_in, x, r, g, scale, dres, nx, name):
    np_, rows, kp = dparts.shape
    d = w_in.shape[0]
    nseg = scale.shape[0]
    tm = _row_tile(rows)
    nsub = tm // ROW_BLOCK
    nres_blocks = dres.shape[0] // ROW_BLOCK

    def body(dp_ref, w_ref, x_ref, r_ref, g_ref, sc_ref, *rest):
        dres_refs = rest[:nsub]
        dx_ref, dsh_ref, dge_ref, acc = rest[nsub:]
        i, k = pl.program_id(0), pl.program_id(1)
        prod = lax.dot_general(dp_ref[...], w_ref[...], _DIMS["nt"], preferred_element_type=F32)

        @pl.when(k == 0)
        def _():
            acc[...] = prod

        @pl.when(k > 0)
        def _():
            acc[...] += prod

        @pl.when(k == np_ - 1)
        def _():
            is_ctx = _ctx_rows(i, tm, nx, nseg)
            dhv = acc[...]
            rv = r_ref[...]
            xn = x_ref[...] * rv
            dxn = dhv * (g_ref[...] * (1.0 + _row_vec(sc_ref, is_ctx)))
            dx = rv * (dxn - xn * jnp.mean(dxn * xn, axis=-1, keepdims=True))
            for s in range(nsub):
                piece = slice(s * ROW_BLOCK, (s + 1) * ROW_BLOCK)
                res = dres_refs[s][...]
                if nres_blocks * ROW_BLOCK < rows:
                    res = jnp.where(i * nsub + s < nres_blocks, res, 0.0)
                dx_ref[piece, :] = dx[piece, :] + res
            _seg_sums(dsh_ref, dhv, is_ctx, i == 0)
            _seg_sums(dge_ref, dhv * xn, is_ctx, i == 0)

    row = pl.BlockSpec((tm, d), lambda i, k: (i, 0))
    vec = pl.BlockSpec((nseg, 1, d), lambda i, k: (0, 0, 0))
    return pl.pallas_call(
        body, grid=(rows // tm, np_),
        in_specs=[pl.BlockSpec((None, tm, kp), lambda i, k: (k, i, 0)), pl.BlockSpec((d, kp), lambda i, k: (0, k)),
                  row, pl.BlockSpec((tm, 1), lambda i, k: (i, 0)), pl.BlockSpec((1, d), lambda i, k: (0, 0)), vec]
        + [pl.BlockSpec((ROW_BLOCK, d), (lambda i, k, s=s: (jnp.minimum(i * nsub + s, nres_blocks - 1), 0)))
           for s in range(nsub)],
        out_specs=[row, vec, vec],
        out_shape=[_sds((rows, d), F32), _sds((nseg, 1, d), F32), _sds((nseg, 1, d), F32)],
        scratch_shapes=[pltpu.VMEM((tm, d), F32)], name=name, compiler_params=_cparams("arbitrary", "arbitrary"),
    )(dparts, w_in, x, r, g, scale, *([dres] * nsub))


_PAD_TOP = 16
_PAD_BOT = 32


def _window_sum(buf, xv, lo, n):
    t = xv.shape[0]
    c = xv.shape[1]
    tp = t + _PAD_TOP + _PAD_BOT
    buf[pl.ds(0, _PAD_TOP), :] = jnp.zeros((_PAD_TOP, c), F32)
    buf[pl.ds(_PAD_TOP, t), :] = xv
    buf[pl.ds(_PAD_TOP + t, _PAD_BOT), :] = jnp.zeros((_PAD_BOT, c), F32)
    p = buf[...]
    k = 1
    while k < n:
        p = p + pltpu.roll(p, tp - k, 0)
        k *= 2
    if lo:
        p = pltpu.roll(p, -lo, 0)
    buf[...] = p
    return buf[pl.ds(_PAD_TOP, t), :]


def _window_count(t, half):
    pos = lax.broadcasted_iota(jnp.int32, (t, 1), 0)
    return (jnp.minimum(pos + half, t) - jnp.maximum(pos - half, 0)).astype(F32)


def _segments(rows, nx):
    return [(0, nx)] + ([(nx, rows - nx)] if rows > nx else [])


def _pool_fwd(uv, nx, name):
    rows = uv.shape[0]
    w = uv.shape[1] // 2
    cb = 128
    per_group = w // len(POOL_WINDOWS) // cb
    segs = _segments(rows, nx)

    def body(u_ref, z_ref, *bufs):
        j = pl.program_id(0)
        for gi, win in enumerate(POOL_WINDOWS):
            half = win // 2

            @pl.when(jnp.logical_and(j >= gi * per_group, j < (gi + 1) * per_group))
            def _():
                for (start, length), buf in zip(segs, bufs):
                    uvv = u_ref[pl.ds(start, length), :].astype(F32)
                    s = _window_sum(buf, uvv, -half, win)
                    z_ref[pl.ds(start, length), :] = (s / _window_count(length, half) - uvv).astype(BF16)

    scratch = [pltpu.VMEM((length + _PAD_TOP + _PAD_BOT, cb), F32) for _, length in segs]
    return pl.pallas_call(
        body, grid=(w // cb,), in_specs=[pl.BlockSpec((rows, cb), lambda j: (0, j))],
        out_specs=pl.BlockSpec((rows, cb), lambda j: (0, j)), out_shape=_sds((rows, w), BF16),
        scratch_shapes=scratch, name=name, compiler_params=_cparams("parallel"),
    )(uv)


def _pool_bwd(dz, dgt, nx, name):
    rows, w = dz.shape
    cb = 128
    per_group = w // len(POOL_WINDOWS) // cb
    segs = _segments(rows, nx)

    def body(dz_ref, dgt_ref, o_ref, *bufs):
        j = pl.program_id(0)
        o_ref[1] = dgt_ref[...]
        for gi, win in enumerate(POOL_WINDOWS):
            half = win // 2

            @pl.when(jnp.logical_and(j >= gi * per_group, j < (gi + 1) * per_group))
            def _():
                for (start, length), buf in zip(segs, bufs):
                    dzv = dz_ref[pl.ds(start, length), :].astype(F32)
                    s = _window_sum(buf, dzv / _window_count(length, half), 1 - half, win)
                    o_ref[0, pl.ds(start, length), :] = (s - dzv).astype(BF16)

    scratch = [pltpu.VMEM((length + _PAD_TOP + _PAD_BOT, cb), F32) for _, length in segs]
    col = pl.BlockSpec((rows, cb), lambda j: (0, j))
    return pl.pallas_call(
        body, grid=(w // cb,), in_specs=[col, col], out_specs=pl.BlockSpec((2, rows, cb), lambda j: (0, 0, j)),
        out_shape=_sds((2, rows, w), BF16), scratch_shapes=scratch, name=name, compiler_params=_cparams("parallel"),
    )(dz, dgt)


def _grp_fwd(z, w_grp, uv, scale, name):
    rows, w = z.shape
    ng, gc, _ = w_grp.shape
    tm = _row_tile(rows)

    def body(z_ref, w_ref, gt_ref, sc_ref, mx_ref, a_ref):
        mixed = jnp.dot(z_ref[...], w_ref[...], preferred_element_type=F32)
        mx_ref[...] = mixed.astype(ACT)
        a_ref[...] = (mixed * sc_ref[...] * _silu(gt_ref[...].astype(F32))).astype(BF16)

    blk = pl.BlockSpec((tm, gc), lambda g, i: (i, g))
    return pl.pallas_call(
        body, grid=(ng, rows // tm),
        in_specs=[blk, pl.BlockSpec((None, gc, gc), lambda g, i: (g, 0, 0)),
                  pl.BlockSpec((tm, gc), lambda g, i: (i, ng + g)), pl.BlockSpec((1, gc), lambda g, i: (0, g))],
        out_specs=[blk, blk], out_shape=[_sds((rows, w), ACT), _sds((rows, w), BF16)],
        name=name, compiler_params=_cparams("parallel", "parallel"),
    )(z, w_grp, uv, scale)


def _grp_bwd(da, mixed, uv, scale, w_grp, name):
    rows, w = da.shape
    ng, gc, _ = w_grp.shape
    tm = _row_tile(rows)

    def body(da_ref, mx_ref, gt_ref, sc_ref, w_ref, dm_ref, dz_ref, dgt_ref, dsc_ref):
        i = pl.program_id(1)
        dav = da_ref[...].astype(F32)
        mixed = mx_ref[...].astype(F32)
        gt = gt_ref[...].astype(F32)
        sg = _silu(gt)
        sc = sc_ref[...]
        dm = (dav * sc * sg).astype(BF16)
        dm_ref[...] = dm
        dz_ref[...] = lax.dot_general(dm, w_ref[...], _DIMS["nt"], preferred_element_type=F32).astype(ACT)
        dgt_ref[...] = (dav * mixed * sc * _dsilu(gt)).astype(BF16)
        s = jnp.sum(dav * mixed * sg, axis=0, keepdims=True)

        @pl.when(i == 0)
        def _():
            dsc_ref[...] = s

        @pl.when(i > 0)
        def _():
            dsc_ref[...] += s

    blk = pl.BlockSpec((tm, gc), lambda g, i: (i, g))
    vec = pl.BlockSpec((1, gc), lambda g, i: (0, g))
    return pl.pallas_call(
        body, grid=(ng, rows // tm),
        in_specs=[blk, blk, pl.BlockSpec((tm, gc), lambda g, i: (i, ng + g)), vec,
                  pl.BlockSpec((None, gc, gc), lambda g, i: (g, 0, 0))],
        out_specs=[blk, blk, blk, vec],
        out_shape=[_sds((rows, w), BF16), _sds((rows, w), ACT), _sds((rows, w), BF16), _sds((1, w), F32)],
        name=name, compiler_params=_cparams("parallel", "arbitrary"),
    )(da, mixed, uv, scale, w_grp)


def _pool_scratch(rows, nx, cols):
    return [pltpu.VMEM((length + _PAD_TOP + _PAD_BOT, cols), F32) for _, length in _segments(rows, nx)]


def _per_group(g, fn):
    for gi, win in enumerate(POOL_WINDOWS):
        pl.when(g == gi)(functools.partial(fn, win))


def _pool_grp_fwd(uv, w_grp, scale, nx, name):
    rows = uv.shape[0]
    ng, gc, _ = w_grp.shape
    w = ng * gc
    segs = _segments(rows, nx)

    def body(u_ref, gt_ref, w_ref, sc_ref, z_ref, mx_ref, a_ref, *bufs):
        def pool(win):
            half = win // 2
            for (start, length), buf in zip(segs, bufs):
                uvv = u_ref[pl.ds(start, length), :].astype(F32)
                s = _window_sum(buf, uvv, -half, win)
                z_ref[pl.ds(start, length), :] = (s / _window_count(length, half) - uvv).astype(BF16)

        _per_group(pl.program_id(0), pool)
        mixed = jnp.dot(z_ref[...], w_ref[...], preferred_element_type=F32)
        mx_ref[...] = mixed.astype(ACT)
        a_ref[...] = (mixed * sc_ref[...] * _silu(gt_ref[...].astype(F32))).astype(BF16)

    col = pl.BlockSpec((rows, gc), lambda g: (0, g))
    return pl.pallas_call(
        body, grid=(ng,),
        in_specs=[col, pl.BlockSpec((rows, gc), lambda g: (0, ng + g)), pl.BlockSpec((None, gc, gc), lambda g: (g, 0, 0)),
                  pl.BlockSpec((1, gc), lambda g: (0, g))],
        out_specs=[col, col, col], out_shape=[_sds((rows, w), BF16), _sds((rows, w), ACT), _sds((rows, w), BF16)],
        scratch_shapes=_pool_scratch(rows, nx, gc), name=name, compiler_params=_cparams("parallel"),
    )(uv, uv, w_grp, scale)


def _pool_grp_bwd(da, mixed, uv, scale, w_grp, nx, name):
    rows, w = da.shape
    ng, gc, _ = w_grp.shape
    segs = _segments(rows, nx)

    def body(da_ref, mx_ref, gt_ref, sc_ref, w_ref, dm_ref, duv_ref, dsc_ref, dz_ref, *bufs):
        dav = da_ref[...].astype(F32)
        mixed = mx_ref[...].astype(F32)
        gt = gt_ref[...].astype(F32)
        sg = _silu(gt)
        sc = sc_ref[...]
        dm = (dav * sc * sg).astype(BF16)
        dm_ref[...] = dm
        dz_ref[...] = lax.dot_general(dm, w_ref[...], _DIMS["nt"], preferred_element_type=F32)
        duv_ref[1] = (dav * mixed * sc * _dsilu(gt)).astype(BF16)
        dsc_ref[...] = jnp.sum(dav * mixed * sg, axis=0, keepdims=True)

        def unpool(win):
            half = win // 2
            for (start, length), buf in zip(segs, bufs):
                dzv = dz_ref[pl.ds(start, length), :]
                s = _window_sum(buf, dzv / _window_count(length, half), 1 - half, win)
                duv_ref[0, pl.ds(start, length), :] = (s - dzv).astype(BF16)

        _per_group(pl.program_id(0), unpool)

    col = pl.BlockSpec((rows, gc), lambda g: (0, g))
    vec = pl.BlockSpec((1, gc), lambda g: (0, g))
    return pl.pallas_call(
        body, grid=(ng,),
        in_specs=[col, col, pl.BlockSpec((rows, gc), lambda g: (0, ng + g)), vec,
                  pl.BlockSpec((None, gc, gc), lambda g: (g, 0, 0))],
        out_specs=[col, pl.BlockSpec((2, rows, gc), lambda g: (0, 0, g)), vec],
        out_shape=[_sds((rows, w), BF16), _sds((2, rows, w), BF16), _sds((1, w), F32)],
        scratch_shapes=[pltpu.VMEM((rows, gc), F32)] + _pool_scratch(rows, nx, gc),
        name=name, compiler_params=_cparams("parallel"),
    )(da, mixed, uv, scale, w_grp)


def _grp_wgrad(z, dm, ng, name, out_dtype):
    rows, w = z.shape
    gc = w // ng

    def body(z_ref, dm_ref, o_ref):
        o_ref[...] = lax.dot_general(z_ref[...], dm_ref[...], _DIMS["tn"],
                                     preferred_element_type=F32).astype(o_ref.dtype)

    blk = pl.BlockSpec((rows, gc), lambda g: (0, g))
    return pl.pallas_call(
        body, grid=(ng,), in_specs=[blk, blk], out_specs=pl.BlockSpec((None, gc, gc), lambda g: (g, 0, 0)),
        out_shape=_sds((ng, gc, gc), out_dtype), name=name, compiler_params=_cparams("parallel"),
    )(z, dm)


def _shift_rows(v, by):
    t = v.shape[0]
    pos = lax.broadcasted_iota(jnp.int32, v.shape, 0)
    rolled = pltpu.roll(v, by % t, 0)
    keep = pos >= by if by > 0 else pos < t + by
    return jnp.where(keep, rolled, 0.0)


def _conv_specs(t, w, cb):
    return [pl.BlockSpec((t, cb), (lambda j, q=q: (0, q * (w // cb) + j))) for q in range(4)]


def _conv_fwd(p4, dw, db, name):
    t = p4.shape[0]
    w = p4.shape[1] // 4
    cb = 128

    def body(bg_ref, cg_ref, v_ref, g_ref, dw_ref, db_ref, a_ref):
        tv = cg_ref[...].astype(F32) * v_ref[...].astype(F32)
        conv = (dw_ref[0:1, :] * _shift_rows(tv, 1) + dw_ref[1:2, :] * tv + dw_ref[2:3, :] * _shift_rows(tv, -1)
                + db_ref[...])
        a_ref[...] = (bg_ref[...].astype(F32) * conv * _silu(g_ref[...].astype(F32))).astype(BF16)

    return pl.pallas_call(
        body, grid=(w // cb,),
        in_specs=_conv_specs(t, w, cb) + [pl.BlockSpec((3, cb), lambda j: (0, j)), pl.BlockSpec((1, cb), lambda j: (0, j))],
        out_specs=pl.BlockSpec((t, cb), lambda j: (0, j)), out_shape=_sds((t, w), BF16),
        name=name, compiler_params=_cparams("parallel"),
    )(p4, p4, p4, p4, dw, db)


def _conv_bwd(da, p4, dw, db, name):
    t, w = da.shape
    cb = 128

    def body(da_ref, bg_ref, cg_ref, v_ref, g_ref, dw_ref, db_ref, d4_ref, ddw_ref, ddb_ref):
        cg = cg_ref[...].astype(F32)
        vv = v_ref[...].astype(F32)
        bg = bg_ref[...].astype(F32)
        gv = g_ref[...].astype(F32)
        tv = cg * vv
        tm1 = _shift_rows(tv, 1)
        tp1 = _shift_rows(tv, -1)
        w0, w1, w2 = dw_ref[0:1, :], dw_ref[1:2, :], dw_ref[2:3, :]
        conv = w0 * tm1 + w1 * tv + w2 * tp1 + db_ref[...]
        y = bg * conv
        dav = da_ref[...].astype(F32)
        dy = dav * _silu(gv)
        d4_ref[3] = (dav * y * _dsilu(gv)).astype(BF16)
        d4_ref[0] = (dy * conv).astype(BF16)
        dconv = dy * bg
        ddb_ref[...] = jnp.sum(dconv, axis=0, keepdims=True)
        ddw_ref[0:1, :] = jnp.sum(dconv * tm1, axis=0, keepdims=True)
        ddw_ref[1:2, :] = jnp.sum(dconv * tv, axis=0, keepdims=True)
        ddw_ref[2:3, :] = jnp.sum(dconv * tp1, axis=0, keepdims=True)
        dt = w0 * _shift_rows(dconv, -1) + w1 * dconv + w2 * _shift_rows(dconv, 1)
        d4_ref[1] = (dt * vv).astype(BF16)
        d4_ref[2] = (dt * cg).astype(BF16)

    col = pl.BlockSpec((t, cb), lambda j: (0, j))
    tap = pl.BlockSpec((3, cb), lambda j: (0, j))
    bias = pl.BlockSpec((1, cb), lambda j: (0, j))
    return pl.pallas_call(
        body, grid=(w // cb,), in_specs=[col] + _conv_specs(t, w, cb) + [tap, bias],
        out_specs=[pl.BlockSpec((4, t, cb), lambda j: (0, 0, j)), tap, bias],
        out_shape=[_sds((4, t, w), BF16), _sds((3, w), F32), _sds((1, w), F32)],
        name=name, compiler_params=_cparams("parallel"),
    )(da, p4, p4, p4, p4, dw, db)


def _attn_mask():
    qn, kn = Q_ROWS * GRID_W, K_ROWS * GRID_W
    qr, qc = np.divmod(np.arange(qn), GRID_W)
    kr, kc = np.divmod(np.arange(kn), GRID_W)
    col0 = np.clip(qc - WIN_COLS // 2, 0, GRID_W - WIN_COLS)
    col_ok = (kc[None, :] >= col0[:, None]) & (kc[None, :] < col0[:, None] + WIN_COLS)
    first = np.zeros(qn, np.int64)
    last = np.full(qn, K_ROWS - WIN_ROWS)
    out = []
    for row0 in (first, qr, last):
        row_ok = (kr[None, :] >= row0[:, None]) & (kr[None, :] < row0[:, None] + WIN_ROWS)
        out.append(np.where(row_ok & col_ok, 0.0, NEG))
    return jnp.asarray(np.stack(out), F32)


_KW = K_ROWS * GRID_W
_QB = Q_ROWS * GRID_W
_PAIR = 2 * HEAD_DIM
_N_DR = 2 * WIN_ROWS - 1
_N_DC = 2 * WIN_COLS - 1
_RP_ROWS = 24
_N_TILES = _N_DR + 1
_BIAS_BASE = (WIN_ROWS - 1, WIN_ROWS // 2 - 1, -1)


class _Comm:
    def __init__(self, ins, outs, sems, start, finish):
        self.ins, self.outs, self.sems, self.start, self.finish = list(ins), list(outs), list(sems), start, finish


def _bias_pieces(cls):
    out = []
    for qr in range(Q_ROWS):
        for kr in range(0, K_ROWS, 2):
            tile = _BIAS_BASE[cls] - qr + kr + 1
            out.append((qr, kr, tile if 0 <= tile < _N_TILES else None))
    return out


def _toeplitz_pair(left_row, right_row):
    lane = lax.broadcasted_iota(jnp.int32, (GRID_W, _PAIR), 1)
    shape = (GRID_W, _PAIR)
    left = pltpu.roll(jnp.broadcast_to(left_row, shape), _PAIR - (WIN_COLS - 1), 1, stride=1, stride_axis=0)
    right = pltpu.roll(jnp.broadcast_to(right_row, shape), GRID_W - (WIN_COLS - 1), 1, stride=1, stride_axis=0)
    return jnp.where(lane < GRID_W, left, right)


def _build_tiles(tiles_ref, rp_ref):
    for h in range(2):
        for t in range(_N_TILES):
            tiles_ref[h, t] = _toeplitz_pair(rp_ref[h, t:t + 1, :], rp_ref[h, t + 1:t + 2, :])


def _block_class(b, nblk, fn, entering=False):
    interior = (b == 1) if entering else jnp.logical_and(b > 0, b < nblk - 1)
    for cls, cond in enumerate((b == 0, interior, b == nblk - 1)):
        pl.when(cond)(functools.partial(fn, cls))


def _attn_geometry(p4, nx):
    rows = p4.shape[0]
    w = p4.shape[1] // 4
    nhp = w // _PAIR
    nblk = nx // _QB
    qspec = lambda col: pl.BlockSpec((_QB, _PAIR), lambda hp, b: (b, col * nhp + hp))
    kspec = lambda col: pl.BlockSpec((rows, _PAIR), lambda hp, b: (0, col * nhp + hp))
    tspec = pl.BlockSpec((2, _RP_ROWS, _PAIR), lambda hp, b: (hp, 0, 0))
    mspec = pl.BlockSpec((None, _QB, _KW), lambda hp, b: (jnp.where(b == 0, 0, jnp.where(b == nblk - 1, 2, 1)), 0, 0))
    lspec = pl.BlockSpec((None, _QB, 2), lambda hp, b: (hp, b, 0))
    ospec = pl.BlockSpec((_QB, _PAIR), lambda hp, b: (b, hp))
    return rows, w, nhp, nblk, qspec, kspec, tspec, mspec, lspec, ospec


def _window_start(b, nx):
    return pl.multiple_of(jnp.clip(b * _QB - PAD_ROWS * GRID_W, 0, nx - _KW), _QB)


def _load_bias(bias_ref, tiles_ref, rp_ref, m_ref, b, nblk):
    pl.when(b == 0)(lambda: _build_tiles(tiles_ref, rp_ref))

    def fill(cls):
        for h in range(2):
            for qr, kr, tile in _bias_pieces(cls):
                rows = slice(qr * GRID_W, (qr + 1) * GRID_W)
                cols = slice(kr * GRID_W, (kr + 2) * GRID_W)
                m = m_ref[rows, cols]
                bias_ref[h, rows, cols] = m if tile is None else tiles_ref[h, tile] + m

    _block_class(b, nblk, fill, entering=True)


def _attn_fwd(p4, rp, mask, nx, name, comm=None):
    rows, w, nhp, nblk, qspec, kspec, tspec, mspec, lspec, ospec = _attn_geometry(p4, nx)
    n_ctx = rows - nx
    n_cin, n_cout = (len(comm.ins), len(comm.outs)) if comm else (0, 0)

    def body(*refs):
        q_ref, k_ref, v_ref, g_ref, rp_ref, m_ref = refs[:6]
        cin = refs[6:6 + n_cin]
        a_ref, o_ref, lse_ref = refs[6 + n_cin:9 + n_cin]
        cout = refs[9 + n_cin:9 + n_cin + n_cout]
        bias_ref, tiles_ref = refs[9 + n_cin + n_cout:11 + n_cin + n_cout]
        sems = refs[11 + n_cin + n_cout:]
        hp, b = pl.program_id(0), pl.program_id(1)
        if comm:
            pl.when(jnp.logical_and(hp == 0, b == 0))(lambda: comm.start(cin, cout, sems))
        start = _window_start(b, nx)
        _load_bias(bias_ref, tiles_ref, rp_ref, m_ref, b, nblk)
        qf = q_ref[...].astype(F32) * HEAD_DIM ** -0.5
        kw = k_ref[pl.ds(start, _KW), :].astype(BF16)
        vw = v_ref[pl.ds(start, _KW), :].astype(BF16)
        kcv = k_ref[pl.ds(nx, n_ctx), :].astype(BF16)
        vcv = v_ref[pl.ds(nx, n_ctx), :].astype(BF16)
        lane = lax.broadcasted_iota(jnp.int32, (1, _PAIR), 1)
        outs, lses = [], []
        for h in range(2):
            mine = (lane >= HEAD_DIM) if h else (lane < HEAD_DIM)
            qm = jnp.where(mine, qf, 0.0).astype(BF16)
            s_loc = lax.dot_general(qm, kw, _DIMS["nt"], preferred_element_type=F32) + bias_ref[h]
            s_ctx = lax.dot_general(qm, kcv, _DIMS["nt"], preferred_element_type=F32)
            mx = jnp.maximum(jnp.max(s_loc, axis=-1, keepdims=True), jnp.max(s_ctx, axis=-1, keepdims=True))
            p_loc = jnp.exp(s_loc - mx)
            p_ctx = jnp.exp(s_ctx - mx)
            den = jnp.sum(p_loc, axis=-1, keepdims=True) + jnp.sum(p_ctx, axis=-1, keepdims=True)
            o = jnp.dot(p_loc.astype(BF16), vw, preferred_element_type=F32)
            o = o + jnp.dot(p_ctx.astype(BF16), vcv, preferred_element_type=F32)
            outs.append(o * (1.0 / den))
            lses.append(mx + jnp.log(den))
        o = jnp.where(lane < HEAD_DIM, outs[0], outs[1])
        o_ref[...] = o.astype(ACT)
        a_ref[...] = (o * _silu(g_ref[...].astype(F32))).astype(BF16)
        col = lax.broadcasted_iota(jnp.int32, (1, 2), 1)
        lse_ref[...] = jnp.where(col == 0, lses[0], lses[1])
        if comm:
            pl.when(jnp.logical_and(hp == nhp - 1, b == nblk - 1))(lambda: comm.finish(cin, cout, sems))

    res = pl.pallas_call(
        body, grid=(nhp, nblk),
        in_specs=[qspec(0), kspec(1), kspec(2), qspec(3), tspec, mspec] + [HBM_SPEC] * n_cin,
        out_specs=[ospec, ospec, lspec] + [HBM_SPEC] * n_cout,
        out_shape=[_sds((nx, w), BF16), _sds((nx, w), ACT), _sds((nhp, nx, 2), F32)] + (comm.outs if comm else []),
        scratch_shapes=[pltpu.VMEM((2, _QB, _KW), F32), pltpu.VMEM((2, _N_TILES, GRID_W, _PAIR), F32)]
        + (comm.sems if comm else []),
        name=name, compiler_params=_cparams("arbitrary", "arbitrary"),
    )(p4, p4, p4, p4, rp, mask, *(comm.ins if comm else []))
    return res[:3], res[3:]


def _fold_tiles(dtiles_ref, drp_ref):
    shape = (GRID_W, _PAIR)
    lane = lax.broadcasted_iota(jnp.int32, shape, 1)
    flip = (lax.broadcasted_iota(jnp.int32, (_PAIR, _PAIR), 0)
            + lax.broadcasted_iota(jnp.int32, (_PAIR, _PAIR), 1) == _PAIR - 1).astype(F32)
    drp_ref[...] = jnp.zeros(drp_ref.shape, F32)
    for h in range(2):
        stack = dtiles_ref[h].reshape(_N_TILES * GRID_W, _PAIR)
        rev = jnp.dot(stack, flip, precision=lax.Precision.HIGHEST, preferred_element_type=F32)
        for t in range(_N_TILES):
            tile = rev[t * GRID_W:(t + 1) * GRID_W, :]
            for side in (0, 1):
                shift = _PAIR - GRID_W * side - (WIN_COLS - 1)
                half = jnp.where((lane < GRID_W) if side else (lane >= GRID_W), tile, 0.0)
                diag = pltpu.roll(half, shift, 1, stride=1, stride_axis=0)
                drp_ref[h, t + side:t + side + 1, :] += jnp.sum(diag, axis=0, keepdims=True)


def _attn_bwd(p4, rp, mask, o, lse, da, nx, name, comm=None):
    rows, w, nhp, nblk, qspec, kspec, tspec, mspec, lspec, ospec = _attn_geometry(p4, nx)
    n_ctx = rows - nx
    n_cin, n_cout = (len(comm.ins), len(comm.outs)) if comm else (0, 0)

    def body(*refs):
        q_ref, k_ref, v_ref, g_ref, rp_ref, m_ref, o_ref, lse_ref, da_ref = refs[:9]
        cin = refs[9:9 + n_cin]
        d4_ref, drp_ref = refs[9 + n_cin:11 + n_cin]
        cout = refs[11 + n_cin:11 + n_cin + n_cout]
        bias_ref, tiles_ref, ds_ref, dtiles_ref, dk_ref, dv_ref = refs[11 + n_cin + n_cout:17 + n_cin + n_cout]
        sems = refs[17 + n_cin + n_cout:]
        hp, b = pl.program_id(0), pl.program_id(1)
        if comm:
            pl.when(jnp.logical_and(hp == 0, b == 0))(lambda: comm.start(cin, cout, sems))
        start = _window_start(b, nx)
        here = pl.multiple_of(b * _QB, _QB)

        @pl.when(b == 0)
        def _():
            dk_ref[...] = jnp.zeros(dk_ref.shape, F32)
            dv_ref[...] = jnp.zeros(dv_ref.shape, F32)
            dtiles_ref[...] = jnp.zeros(dtiles_ref.shape, F32)
            d4_ref[0, pl.ds(nx, n_ctx), :] = jnp.zeros((n_ctx, _PAIR), BF16)
            d4_ref[3, pl.ds(nx, n_ctx), :] = jnp.zeros((n_ctx, _PAIR), BF16)

        _load_bias(bias_ref, tiles_ref, rp_ref, m_ref, b, nblk)
        gv = g_ref[...].astype(F32)
        dav = da_ref[...].astype(F32)
        ov = o_ref[...].astype(F32)
        dov = dav * _silu(gv)
        d4_ref[3, pl.ds(here, _QB), :] = (dav * ov * _dsilu(gv)).astype(BF16)
        qf = q_ref[...].astype(F32) * HEAD_DIM ** -0.5
        kw = k_ref[pl.ds(start, _KW), :].astype(BF16)
        vw = v_ref[pl.ds(start, _KW), :].astype(BF16)
        kcv = k_ref[pl.ds(nx, n_ctx), :].astype(BF16)
        vcv = v_ref[pl.ds(nx, n_ctx), :].astype(BF16)
        lane = lax.broadcasted_iota(jnp.int32, (1, _PAIR), 1)
        dq = jnp.zeros((_QB, _PAIR), F32)
        for h in range(2):
            mine = (lane >= HEAD_DIM) if h else (lane < HEAD_DIM)
            qm = jnp.where(mine, qf, 0.0).astype(BF16)
            dom = jnp.where(mine, dov, 0.0)
            dob = dom.astype(BF16)
            lse = lse_ref[:, h:h + 1]
            s_loc = lax.dot_general(qm, kw, _DIMS["nt"], preferred_element_type=F32)
            p_loc = jnp.exp(s_loc + bias_ref[h] - lse)
            p_ctx = jnp.exp(lax.dot_general(qm, kcv, _DIMS["nt"], preferred_element_type=F32) - lse)
            delta = jnp.sum(dom * ov, axis=-1, keepdims=True)
            ds_loc = p_loc * (lax.dot_general(dob, vw, _DIMS["nt"], preferred_element_type=F32) - delta)
            ds_ctx = p_ctx * (lax.dot_general(dob, vcv, _DIMS["nt"], preferred_element_type=F32) - delta)
            dsb_loc = ds_loc.astype(BF16)
            dsb_ctx = ds_ctx.astype(BF16)
            dq_h = (jnp.dot(dsb_loc, kw, preferred_element_type=F32)
                    + jnp.dot(dsb_ctx, kcv, preferred_element_type=F32))
            dq = dq + jnp.where(mine, dq_h, 0.0)
            dk_ref[pl.ds(start, _KW), :] += lax.dot_general(dsb_loc, qm, _DIMS["tn"], preferred_element_type=F32)
            dv_ref[pl.ds(start, _KW), :] += lax.dot_general(p_loc.astype(BF16), dob, _DIMS["tn"],
                                                            preferred_element_type=F32)
            dk_ref[pl.ds(nx, n_ctx), :] += lax.dot_general(dsb_ctx, qm, _DIMS["tn"], preferred_element_type=F32)
            dv_ref[pl.ds(nx, n_ctx), :] += lax.dot_general(p_ctx.astype(BF16), dob, _DIMS["tn"],
                                                           preferred_element_type=F32)
            ds_ref[h] = ds_loc
        d4_ref[0, pl.ds(here, _QB), :] = (dq * HEAD_DIM ** -0.5).astype(BF16)

        def scatter(cls):
            for h in range(2):
                for qr, kr, tile in _bias_pieces(cls):
                    if tile is not None:
                        dtiles_ref[h, tile] += ds_ref[h, qr * GRID_W:(qr + 1) * GRID_W, kr * GRID_W:(kr + 2) * GRID_W]

        _block_class(b, nblk, scatter)

        @pl.when(b == nblk - 1)
        def _():
            d4_ref[1] = dk_ref[...].astype(BF16)
            d4_ref[2] = dv_ref[...].astype(BF16)
            _fold_tiles(dtiles_ref, drp_ref)

        if comm:
            pl.when(jnp.logical_and(hp == nhp - 1, b == nblk - 1))(lambda: comm.finish(cin, cout, sems))

    tiles = pltpu.VMEM((2, _N_TILES, GRID_W, _PAIR), F32)
    block = pltpu.VMEM((2, _QB, _KW), F32)
    res = pl.pallas_call(
        body, grid=(nhp, nblk),
        in_specs=[qspec(0), kspec(1), kspec(2), qspec(3), tspec, mspec, ospec, lspec, ospec] + [HBM_SPEC] * n_cin,
        out_specs=[pl.BlockSpec((4, rows, _PAIR), lambda hp, b: (0, 0, hp)), tspec] + [HBM_SPEC] * n_cout,
        out_shape=[_sds((4, rows, w), BF16), _sds(rp.shape, F32)] + (comm.outs if comm else []),
        scratch_shapes=[block, tiles, block, tiles, pltpu.VMEM((rows, _PAIR), F32), pltpu.VMEM((rows, _PAIR), F32)]
        + (comm.sems if comm else []),
        name=name, compiler_params=_cparams("arbitrary", "arbitrary"),
    )(p4, p4, p4, p4, rp, mask, o, lse, da, *(comm.ins if comm else []))
    return res[:2], res[2:]


def _final(x, g, target, name):
    rows, d = x.shape
    tr = ROW_BLOCK
    nblk = rows // tr

    def body(x_ref, g_ref, t_ref, loss_ref, dx_ref, dg_ref, acc_ref):
        i = pl.program_id(0)
        xv = x_ref[...]
        gv = g_ref[...]
        r = lax.rsqrt(jnp.mean(xv * xv, axis=-1, keepdims=True) + EPS)
        xn = xv * r
        err = xn * gv - t_ref[...]
        dy = err * (1.0 / d)
        dxn = dy * gv
        dx_ref[...] = r * (dxn - xn * jnp.mean(dxn * xn, axis=-1, keepdims=True))
        s_g = jnp.sum(dy * xn, axis=0, keepdims=True)
        s_l = jnp.sum(jnp.mean(err * err, axis=-1, keepdims=True), axis=0, keepdims=True)

        @pl.when(i == 0)
        def _():
            dg_ref[...] = s_g
            acc_ref[...] = s_l

        @pl.when(i > 0)
        def _():
            dg_ref[...] += s_g
            acc_ref[...] += s_l

        @pl.when(i == nblk - 1)
        def _():
            loss_ref[...] = jnp.broadcast_to(0.5 * acc_ref[...], loss_ref.shape)

    row = pl.BlockSpec((tr, d), lambda i: (i, 0))
    vec = pl.BlockSpec((1, d), lambda i: (0, 0))
    return pl.pallas_call(
        body, grid=(nblk,), in_specs=[row, vec, row],
        out_specs=[pl.BlockSpec((1, 128), lambda i: (0, 0)), row, vec],
        out_shape=[_sds((1, 128), F32), _sds((rows, d), F32), _sds((1, d), F32)],
        scratch_shapes=[pltpu.VMEM((1, 1), F32)], name=name, compiler_params=_cparams("arbitrary"),
    )(x, g, target)


def _as2d(a):
    if a.ndim == 1:
        return a.reshape(-1, 128) if a.shape[0] % 128 == 0 else a.reshape(1, -1)
    return a.reshape(-1, a.shape[-1])


def _adamw(w, g, m, v, name):
    shape = w.shape
    w2, g2, m2, v2 = (_as2d(t) for t in (w, g.reshape(shape), m, v))
    rows, cols = w2.shape
    tr = 512 if rows % 512 == 0 else rows
    c1 = 1.0 - ADAM_B1 ** ADAM_STEP
    c2 = 1.0 - ADAM_B2 ** ADAM_STEP

    def body(w_ref, g_ref, m_ref, v_ref, d_ref, nm_ref, nv_ref):
        gv = g_ref[...]
        nm = ADAM_B1 * m_ref[...] + (1.0 - ADAM_B1) * gv
        nv = ADAM_B2 * v_ref[...] + (1.0 - ADAM_B2) * (gv * gv)
        nm_ref[...] = nm
        nv_ref[...] = nv
        d_ref[...] = -ADAM_LR * ((nm / c1) / (jnp.sqrt(nv / c2) + ADAM_EPS) + ADAM_WD * w_ref[...])

    blk = pl.BlockSpec((tr, cols), lambda i: (i, 0))
    outs = pl.pallas_call(
        body, grid=(rows // tr,), in_specs=[blk] * 4, out_specs=[blk] * 3,
        out_shape=[_sds((rows, cols), F32)] * 3, name=name, compiler_params=_cparams("parallel"),
    )(w2, g2, m2, v2)
    return tuple(t.reshape(shape) for t in outs)


def _sum_lead(x, name, out_dtype=F32):
    n, rows, cols = x.shape
    tr = 512 if rows % 512 == 0 else rows

    def body(x_ref, o_ref):
        acc = x_ref[0].astype(F32)
        for k in range(1, n):
            acc = acc + x_ref[k].astype(F32)
        o_ref[...] = acc.astype(out_dtype)

    return pl.pallas_call(
        body, grid=(rows // tr,), in_specs=[pl.BlockSpec((n, tr, cols), lambda i: (0, i, 0))],
        out_specs=pl.BlockSpec((tr, cols), lambda i: (i, 0)), out_shape=_sds((rows, cols), out_dtype),
        name=name, compiler_params=_cparams("parallel"),
    )(x)


_NO_CTX = 1 << 30


def _seg_vecs(mod_l, which, nseg):
    return mod_l[:nseg, which][:, None, :]


def _norm_grads(dshift, dgeff, dgate, g, scale):
    nseg, _, d = dshift.shape
    dmod = jnp.stack([dshift[:, 0], dgeff[:, 0] * g, dgate[:, 0]], axis=1)
    if nseg == 1:
        dmod = jnp.concatenate([dmod, jnp.zeros((1, 3, d), F32)], axis=0)
    dg = jnp.sum(dgeff[:, 0] * (1.0 + scale[:, 0]), axis=0)
    return dmod, dg


def _pool_layer(xin, g, mod_l, w_in, w_grp, w_out, pscale, nx, tag):
    rows = xin.shape[0]
    nseg = 2 if rows > nx else 1
    nxb = nx // ROW_BLOCK if nseg == 2 else _NO_CTX
    shift, scale, gate = (_seg_vecs(mod_l, k, nseg) for k in range(3))
    h, r, uv = _norm_w_in(xin, g, scale, shift, w_in, nx, f"w_in_fwd_{tag}")
    z, mixed, a = _pool_grp_fwd(uv, w_grp, pscale, nx, f"pool_fwd_{tag}")
    yx, xout = _w_out_resid(a, w_out, xin, gate, nx, f"w_out_fwd_{tag}")

    def backward(dxo):
        dyx, da, dgate = _gate_w_out_bwd(dxo, yx, gate, w_out, nx, f"w_out_bwd_{tag}")
        gw_out = _mm_tn(a, dyx, f"w_out_grad_{tag}", BF16)
        dm, duv, dscale = _pool_grp_bwd(da, mixed, uv, pscale, w_grp, nx, f"pool_bwd_{tag}")
        gw_grp = _grp_wgrad(z, dm, w_grp.shape[0], f"grp_grad_{tag}", BF16)
        gw_in = _mm_tn_parts(h, duv, f"w_in_grad_{tag}", BF16)
        dx, dshift, dgeff = _w_in_bwd_norm(duv, w_in, xin, r, g, scale, dxo, nx, f"w_in_bwd_{tag}")
        dmod, dg = _norm_grads(dshift, dgeff, dgate, g[0], scale)
        return dx, dmod, dg, dict(w_in=gw_in, w_grp=gw_grp, w_out=gw_out, scale=dscale)

    return xout, backward


def _na_layer(xc, g, mod_l, w_in, rpb, w_out, nx, mask, comm=None):
    nh, n_dr, n_dc = rpb.shape
    shift, scale = _seg_vecs(mod_l, 0, 2), _seg_vecs(mod_l, 1, 2)
    gate = _seg_vecs(mod_l, 2, 1)
    h, r, p4 = _norm_w_in(xc, g, scale, shift, w_in, nx, "w_in_fwd_na")
    rp = jnp.pad(rpb, ((0, 0), (1, _RP_ROWS - 1 - n_dr), (0, _PAIR - n_dc)))
    (a, o, lse), carried = _attn_fwd(p4, rp, mask, nx, "attn_fwd", comm)
    yx, xout = _w_out_resid(a, w_out, xc, gate, nx, "w_out_fwd_na")

    def backward(dxo, comm=None):
        dyx, da, dgate = _gate_w_out_bwd(dxo, yx, gate, w_out, nx, "w_out_bwd_na")
        gw_out = _mm_tn(a, dyx, "w_out_grad_na", BF16)
        (d4, drp), carried_bwd = _attn_bwd(p4, rp, mask, o, lse, da, nx, "attn_bwd", comm)
        gw_in = _mm_tn_parts(h, d4, "w_in_grad_na", BF16)
        dx, dshift, dgeff = _w_in_bwd_norm(d4, w_in, xc, r, g, scale, dxo, nx, "w_in_bwd_na")
        dgate2 = jnp.concatenate([dgate, jnp.zeros_like(dgate)], axis=0)
        dmod, dg = _norm_grads(dshift, dgeff, dgate2, g[0], scale)
        drpb = drp[:, 1:1 + n_dr, ::-1][:, :, :n_dc]
        return dx, dmod, dg, dict(w_in=gw_in, w_out=gw_out, rpb=drpb), carried_bwd

    return xout, backward, carried


def _conv_layer(xin, g, mod_l, w_in, dw, db, w_out):
    shift, scale, gate = (_seg_vecs(mod_l, k, 1) for k in range(3))
    nx = xin.shape[0]
    h, r, p4 = _norm_w_in(xin, g, scale, shift, w_in, nx, "w_in_fwd_conv")
    a = _conv_fwd(p4, dw, db, "conv_fwd")
    yx, xout = _w_out_resid(a, w_out, xin, gate, nx, "w_out_fwd_conv")

    def backward(dxo):
        dyx, da, dgate = _gate_w_out_bwd(dxo, yx, gate, w_out, nx, "w_out_bwd_conv")
        gw_out = _mm_tn(a, dyx, "w_out_grad_conv", BF16)
        d4, ddw, ddb = _conv_bwd(da, p4, dw, db, "conv_bwd")
        gw_in = _mm_tn_parts(h, d4, "w_in_grad_conv", BF16)
        dx, dshift, dgeff = _w_in_bwd_norm(d4, w_in, xin, r, g, scale, dxo, nx, "w_in_bwd_conv")
        dmod, dg = _norm_grads(dshift, dgeff, dgate, g[0], scale)
        return dx, dmod, dg, dict(w_in=gw_in, w_out=gw_out, dw=ddw, db=ddb)

    return xout, backward


def _example_step(x, ctx, target, mod, norm_g, final_g, wts, late_comm=None, late_weights=None, grad_comm=None):
    nx = x.shape[0]
    consts = _attn_mask()
    g_rows = [norm_g[i:i + 1] for i in range(4)]
    xc0 = jnp.concatenate([x, ctx], axis=0)
    xc1, bwd0 = _pool_layer(xc0, g_rows[0], mod[0], wts["pool_w_in"][0], wts["pool_w_grp"][0], wts["pool_w_out"][0],
                            wts["pool_scale"][0:1], nx, "p0")
    x2, bwd1, carried = _na_layer(xc1, g_rows[1], mod[1], wts["na_w_in"], wts["na_rpb"], wts["na_w_out"], nx, consts,
                                  late_comm)
    if late_weights is not None:
        wts = {**wts, **late_weights(carried)}
    x3, bwd2 = _conv_layer(x2, g_rows[2], mod[2], wts["conv_w_in"], wts["conv_dw"], wts["conv_db"], wts["conv_w_out"])
    x4, bwd3 = _pool_layer(x3, g_rows[3], mod[3], wts["pool_w_in"][1], wts["pool_w_grp"][1], wts["pool_w_out"][1],
                           wts["pool_scale"][1:2], nx, "p3")
    loss, dx4, dfinal_g = _final(x4, final_g, target, "loss_head")
    dx3, dmod3, dg3, gr3 = bwd3(dx4)
    dx2, dmod2, dg2, gr2 = bwd2(dx3)
    dxc1, dmod1, dg1, gr1, carried_bwd = bwd1(dx2, grad_comm(gr3, gr2) if grad_comm else None)
    dxc0, dmod0, dg0, gr0 = bwd0(dxc1)
    return dict(
        loss=loss, grad_x=dxc0[:nx], dmod=jnp.stack([dmod0, dmod1, dmod2, dmod3]),
        dnorm_g=jnp.stack([dg0, dg1, dg2, dg3]), dfinal_g=dfinal_g, layers=(gr0, gr1, gr2, gr3), carried=carried_bwd)


_AXES = ("x", "y", "c")
_CHIP_FLIPS = ((1, 0), (0, 1), (1, 1))


def _position():
    return tuple(lax.axis_index(a) for a in _AXES)


def _flipped(pos, flip):
    return tuple(1 - p if f else p for p, f in zip(pos, flip))


def _run_comms(comms, name):
    n_in = [len(c.ins) for c in comms]
    n_out = [len(c.outs) for c in comms]
    n_sem = [len(c.sems) for c in comms]

    def body(*refs):
        ins, outs, sems = refs[:sum(n_in)], refs[sum(n_in):sum(n_in) + sum(n_out)], refs[sum(n_in) + sum(n_out):]
        parts = []
        for k in range(len(comms)):
            a, b, s = sum(n_in[:k]), sum(n_out[:k]), sum(n_sem[:k])
            parts.append((ins[a:a + n_in[k]], outs[b:b + n_out[k]], sems[s:s + n_sem[k]]))
        for c, part in zip(comms, parts):
            c.start(*part)
        for c, part in zip(comms, parts):
            c.finish(*part)

    res = pl.pallas_call(
        body, in_specs=[HBM_SPEC] * sum(n_in), out_specs=[HBM_SPEC] * sum(n_out),
        out_shape=[o for c in comms for o in c.outs], scratch_shapes=[s for c in comms for s in c.sems], name=name,
    )(*[a for c in comms for a in c.ins])
    return [res[sum(n_out[:k]):sum(n_out[:k + 1])] for k in range(len(comms))]


def _all_gather_comm(v, axes):
    flips = [f for f in np.ndindex(2, 2, 2) if any(f) and all(a in axes or not b for a, b in zip(_AXES, f))]
    n = len(flips) + 1

    def copies(ins, outs, sems):
        (v_ref,), (o_ref,), (send_sems, recv_sems, local_sem) = ins, outs, sems
        pos = _position()
        slot = 0
        for a, p in zip(_AXES, pos):
            if a in axes:
                slot = 2 * slot + p
        local = pltpu.make_async_copy(v_ref, o_ref.at[slot], local_sem)
        remote = [pltpu.make_async_remote_copy(v_ref, o_ref.at[slot], send_sems.at[k], recv_sems.at[k],
                                               device_id=_flipped(pos, flip), device_id_type=MESH)
                  for k, flip in enumerate(flips)]
        return [local] + remote

    def start(ins, outs, sems):
        for cp in copies(ins, outs, sems):
            cp.start()

    def finish(ins, outs, sems):
        for cp in copies(ins, outs, sems):
            cp.wait()

    sems = [pltpu.SemaphoreType.DMA((n - 1,)), pltpu.SemaphoreType.DMA((n - 1,)), pltpu.SemaphoreType.DMA(())]
    return _Comm([v], [_sds((n,) + v.shape, v.dtype)], sems, start, finish)


def _all_gather(v, axes, name):
    return _run_comms([_all_gather_comm(v, axes)], name)[0][0]


class _Item:
    def __init__(self, key, layer, shape, shard_axis, half_axis):
        self.key, self.layer, self.shape = key, layer, tuple(shape)
        self.shard_axis, self.half_axis = shard_axis, half_axis
        self.shard = shape[shard_axis] // 4
        self.half = shape[half_axis] // 2

    def sized(self, shard=False, half=False):
        s = list(self.shape)
        if shard:
            s[self.shard_axis] = self.shard
        if half:
            s[self.half_axis] = self.half
        return tuple(s)

    def window(self, ref, chip=None, half=None):
        idx = [slice(None)] * len(self.shape)
        if chip is not None:
            idx[self.shard_axis] = pl.ds(chip * self.shard, self.shard)
        if half is not None:
            idx[self.half_axis] = pl.ds(half * self.half, self.half)
        return ref.at[tuple(idx)]


def _items(d, w):
    out = []
    for j in range(2):
        out += [_Item("pool_w_in", j, (d, 2 * w), 1, 0), _Item("pool_w_grp", j, (4, w // 4, w // 4), 1, 0),
                _Item("pool_w_out", j, (w, d), 0, 1)]
    out += [_Item("na_w_in", 0, (d, 4 * w), 1, 0), _Item("na_w_out", 0, (w, d), 0, 1),
            _Item("conv_w_in", 0, (d, 4 * w), 1, 0), _Item("conv_w_out", 0, (w, d), 0, 1)]
    return out


def _gather_weights(shards, items, name):
    comm = _gather_comm(shards, items)

    def body(*refs):
        n = len(items)
        comm.start(refs[:n], refs[n:2 * n], refs[2 * n:])
        comm.finish(refs[:n], refs[n:2 * n], refs[2 * n:])

    return pl.pallas_call(
        body, in_specs=[HBM_SPEC] * len(items), out_specs=[HBM_SPEC] * len(items), out_shape=comm.outs,
        scratch_shapes=comm.sems, name=name,
    )(*shards)


def _gather_comm(shards, items):
    n = len(items)

    def copies(src, dst, sems, onward):
        send_a, recv_a, send_b, recv_b, send_c, recv_c = sems
        x, y, c = _position()
        chip = 2 * x + y
        sibling = (x, y, 1 - c)
        own, out, fwd, fwd_in = [], [], [], []
        for i, it in enumerate(items):
            own.append(pltpu.make_async_remote_copy(src[i], it.window(dst[i], chip=chip), send_c.at[i], recv_c.at[i],
                                                    device_id=sibling, device_id_type=MESH))
            for k, flip in enumerate(_CHIP_FLIPS):
                px, py = _flipped((x, y), flip)
                s = 3 * i + k
                out.append(pltpu.make_async_remote_copy(
                    it.window(src[i], half=c), it.window(dst[i], chip=chip, half=c), send_a.at[s], recv_a.at[s],
                    device_id=(px, py, c), device_id_type=MESH))
                if onward:
                    got = it.window(dst[i], chip=2 * px + py, half=c)
                    fwd.append(pltpu.make_async_remote_copy(got, got, send_b.at[s], recv_b.at[s],
                                                            device_id=sibling, device_id_type=MESH))
                    other = it.window(dst[i], chip=2 * px + py, half=1 - c)
                    fwd_in.append(pltpu.make_async_remote_copy(other, other, send_b.at[s], recv_b.at[s],
                                                               device_id=sibling, device_id_type=MESH))
        return own, out, fwd, fwd_in

    def start(src, dst, sems):
        own, out, _, _ = copies(src, dst, sems, False)
        for cp in own + out:
            cp.start()

    def finish(src, dst, sems):
        own, out, fwd, fwd_in = copies(src, dst, sems, True)
        for arrived, onward in zip(out, fwd):
            arrived.wait_recv()
            onward.start()
        for cp in fwd_in:
            cp.wait_recv()
        for cp in out + fwd:
            cp.wait_send()
        for cp in own:
            cp.wait()

    sems = [pltpu.SemaphoreType.DMA((3 * n,)) for _ in range(4)] + [pltpu.SemaphoreType.DMA((n,)) for _ in range(2)]
    return _Comm(shards, [_sds(it.shape, BF16) for it in items], sems, start, finish)


def _pair_swap_comm(arrays, windows, out_shapes):
    n = len(arrays)

    def copies(src, got, sems):
        send_sems, recv_sems = sems
        x, y, c = _position()
        return [pltpu.make_async_remote_copy(windows[i](src[i], 1 - c), got[i], send_sems.at[i], recv_sems.at[i],
                                             device_id=(x, y, 1 - c), device_id_type=MESH) for i in range(n)]

    def start(src, got, sems):
        for cp in copies(src, got, sems):
            cp.start()

    def finish(src, got, sems):
        for cp in copies(src, got, sems):
            cp.wait()

    return _Comm(arrays, out_shapes, [pltpu.SemaphoreType.DMA((n,)), pltpu.SemaphoreType.DMA((n,))], start, finish)


def _pair_swap(arrays, windows, out_shapes, name):
    return _run_comms([_pair_swap_comm(arrays, windows, out_shapes)], name)[0]


def _chip_exchange(partials, items, name):
    comm = _chip_exchange_comm(partials, items)

    def body(*refs):
        n = len(items)
        comm.start(refs[:n], refs[n:2 * n], refs[2 * n:])
        comm.finish(refs[:n], refs[n:2 * n], refs[2 * n:])

    return pl.pallas_call(
        body, in_specs=[HBM_SPEC] * len(items), out_specs=[HBM_SPEC] * len(items), out_shape=comm.outs,
        scratch_shapes=comm.sems, name=name,
    )(*partials)


def _chip_exchange_comm(partials, items):
    n = len(items)

    def copies(src, dst, sems):
        send_sems, recv_sems = sems
        x, y, c = _position()
        out = []
        for i, it in enumerate(items):
            for k, flip in enumerate(_CHIP_FLIPS):
                px, py = _flipped((x, y), flip)
                out.append(pltpu.make_async_remote_copy(
                    it.window(src[i], chip=2 * px + py), dst[i].at[k], send_sems.at[3 * i + k],
                    recv_sems.at[3 * i + k], device_id=(px, py, c), device_id_type=MESH))
        return out

    def start(src, dst, sems):
        for cp in copies(src, dst, sems):
            cp.start()

    def finish(src, dst, sems):
        for cp in copies(src, dst, sems):
            cp.wait()

    return _Comm(partials, [_sds((3,) + it.sized(shard=True, half=True), BF16) for it in items],
                 [pltpu.SemaphoreType.DMA((3 * n,)), pltpu.SemaphoreType.DMA((3 * n,))], start, finish)


_SUM_STEPS = 2


def _pair_sums(gs, gots, its, pos, name):
    n = len(its)
    nb = _SUM_STEPS
    g2 = [g.reshape(-1, g.shape[-1]) for g in gs]
    got2 = [t.reshape(-1, t.shape[-1]) for t in gots]

    def body(pos_ref, *refs):
        for g_ref, got_ref, o_ref in zip(refs[:n], refs[n:2 * n], refs[2 * n:]):
            o_ref[...] = (g_ref[...].astype(F32) + got_ref[...].astype(F32)).astype(BF16)

    g_specs, got_specs = [], []
    for it, t in zip(its, got2):
        rows, cols = t.shape
        blk = (rows // nb, cols)
        g_map = (lambda i, pos: (pos[1] * nb + i, 0)) if it.half_axis == 0 else (lambda i, pos: (i, pos[1]))
        g_specs.append(pl.BlockSpec(blk, g_map))
        got_specs.append(pl.BlockSpec(blk, lambda i, pos: (i, 0)))
    outs = pl.pallas_call(
        body, grid_spec=pltpu.PrefetchScalarGridSpec(
            num_scalar_prefetch=1, grid=(nb,), in_specs=g_specs + got_specs, out_specs=got_specs),
        out_shape=[_sds(t.shape, BF16) for t in got2], name=name, compiler_params=_cparams("parallel"),
    )(pos, *g2, *got2)
    return [o.reshape(t.shape) for o, t in zip(outs, gots)]


_FLIP_SLOT = {2: 0, 1: 1, 3: 2}


def _chip_sums(pairs, slots, its, pos, name):
    n = len(its)
    nb = _SUM_STEPS

    def body(pos_ref, *refs):
        chip = pos_ref[0]
        for own in range(4):
            @pl.when(chip == own)
            def _():
                for p_ref, s_ref, o_ref in zip(refs[:n], refs[n:2 * n], refs[2 * n:]):
                    acc = None
                    for k in range(4):
                        v = (p_ref[...] if k == own else s_ref[_FLIP_SLOT[own ^ k]]).astype(F32)
                        acc = v if acc is None else acc + v
                    o_ref[...] = acc

    p_specs, s_specs, o_specs, shapes = [], [], [], []
    for it in its:
        shape = it.sized(shard=True, half=True)
        blk = (shape[0] // nb,) + shape[1:]
        rest = (0,) * (len(shape) - 1)

        def p_map(i, pos, it=it, nd=len(shape)):
            lead = i + (pos[0] * nb if it.shard_axis == 0 else 0)
            return (lead,) + tuple(pos[0] if ax == it.shard_axis else 0 for ax in range(1, nd))

        p_specs.append(pl.BlockSpec(blk, p_map))
        s_specs.append(pl.BlockSpec((3,) + blk, lambda i, pos, rest=rest: (0, i) + rest))
        o_specs.append(pl.BlockSpec(blk, lambda i, pos, rest=rest: (i,) + rest))
        shapes.append(_sds(shape, F32))
    return pl.pallas_call(
        body, grid_spec=pltpu.PrefetchScalarGridSpec(
            num_scalar_prefetch=1, grid=(nb,), in_specs=p_specs + s_specs, out_specs=o_specs),
        out_shape=shapes, name=name, compiler_params=_cparams("parallel"),
    )(pos, *pairs, *slots)


_GRAD_KEYS = ("pool_w_in", "pool_w_grp", "pool_w_out", "na_w_in", "na_w_out", "conv_w_in", "conv_w_out")


def _adamw_matrix(w, m, v, owns, others, it, pos, name):
    nl = w.shape[0]
    rows_split = it.half_axis == 0
    r, cdim = int(np.prod(w.shape[1:-1])), w.shape[-1]
    hr, hc = (r // 2, cdim) if rows_split else (r, cdim // 2)
    br = min(hr, 256)
    nb = hr // br
    c1 = 1.0 - ADAM_B1 ** ADAM_STEP
    c2 = 1.0 - ADAM_B2 ** ADAM_STEP

    def body(pos_ref, w_ref, m_ref, v_ref, *rest):
        own_refs, other_refs = rest[:nl], rest[nl:2 * nl]
        g_ref, d_ref, nm_ref, nv_ref = rest[2 * nl:]
        j, h = pl.program_id(0), pl.program_id(1)
        own, other = own_refs[0][...], other_refs[0][...]
        for q in range(1, nl):
            own = jnp.where(j == q, own_refs[q][...], own)
            other = jnp.where(j == q, other_refs[q][...], other)
        gv = jnp.where(h == pos_ref[1], own, other)
        nm = ADAM_B1 * m_ref[...] + (1.0 - ADAM_B1) * gv
        nv = ADAM_B2 * v_ref[...] + (1.0 - ADAM_B2) * (gv * gv)
        g_ref[...] = gv
        nm_ref[...] = nm
        nv_ref[...] = nv
        d_ref[...] = -ADAM_LR * ((nm / c1) / (jnp.sqrt(nv / c2) + ADAM_EPS) + ADAM_WD * w_ref[...])

    if rows_split:
        full = pl.BlockSpec((None, br, hc), lambda j, h, i, pos: (j, h * nb + i, 0))
    else:
        full = pl.BlockSpec((None, br, hc), lambda j, h, i, pos: (j, i, h))
    half = pl.BlockSpec((br, hc), lambda j, h, i, pos: (i, 0))
    flat = lambda t: t.reshape(nl, r, cdim)
    outs = pl.pallas_call(
        body, grid_spec=pltpu.PrefetchScalarGridSpec(
            num_scalar_prefetch=1, grid=(nl, 2, nb), in_specs=[full] * 3 + [half] * (2 * nl), out_specs=[full] * 4),
        out_shape=[_sds((nl, r, cdim), F32)] * 4, name=name,
        compiler_params=_cparams("parallel", "parallel", "parallel"),
    )(pos, flat(w), flat(m), flat(v), *[t.reshape(hr, hc) for t in list(owns) + list(others)])
    return tuple(t.reshape(w.shape) for t in outs)


_WEIGHTS = ("c_ctx", "norm_g", "ada_w", "ada_b", "pool_w_in", "pool_w_grp", "pool_scale", "pool_w_out", "na_w_in",
            "na_rpb", "na_w_out", "conv_w_in", "conv_dw", "conv_db", "conv_w_out", "final_g")
_COND_ROWS = 16


def _modulations(cond, ada_w, ada_b_cols):
    nl, d, n = ada_w.shape
    return _matmul(
        cond, ada_w, mode="nn", grid=(nl, 1), a_silu=True, epilogue="bias",
        a_spec=pl.BlockSpec((_COND_ROWS, d), lambda i, j: (0, 0)), b_spec=pl.BlockSpec((None, d, n), lambda i, j: (i, 0, 0)),
        extra=(ada_b_cols,), extra_specs=(pl.BlockSpec((None, 1, n), lambda i, j: (i, 0, 0)),),
        out_shapes=[_sds((nl, _COND_ROWS, n), F32)], out_specs=[pl.BlockSpec((None, _COND_ROWS, n), lambda i, j: (i, 0, 0))],
        name="modulations")[0]


def _ada_w_grad(cond, dm_cols):
    d = cond.shape[1]
    nl, _, n = dm_cols.shape
    return _matmul(
        cond, dm_cols, mode="tn", grid=(nl, 1), a_silu=True,
        a_spec=pl.BlockSpec((_COND_ROWS, d), lambda i, j: (0, 0)), b_spec=pl.BlockSpec((None, _COND_ROWS, n), lambda i, j: (i, 0, 0)),
        out_shapes=[_sds((nl, d, n), F32)], out_specs=[pl.BlockSpec((None, d, n), lambda i, j: (i, 0, 0))],
        name="ada_w_grad")[0]


def _cond_grad(dm_cols, ada_w):
    nl, d, n = ada_w.shape
    return _matmul(
        dm_cols, ada_w, mode="nt", grid=(1, nl), nk=nl, acc_shape=(_COND_ROWS, d),
        a_spec=pl.BlockSpec((None, _COND_ROWS, n), lambda i, q: (q, 0, 0)), b_spec=pl.BlockSpec((None, d, n), lambda i, q: (q, 0, 0)),
        out_shapes=[_sds((_COND_ROWS, d), F32)], out_specs=[pl.BlockSpec((_COND_ROWS, d), lambda i, q: (0, 0))],
        name="cond_grad")[0]


def _pack(parts):
    flat = [p.reshape(-1) for p in parts]
    sizes = [f.shape[0] for f in flat]
    total = sum(sizes)
    rows = -(-total // 1024) * 8
    packed = jnp.concatenate(flat + [jnp.zeros((rows * 128 - total,), F32)]).reshape(rows, 128)
    offs = np.concatenate([[0], np.cumsum(sizes)])[:-1]
    return packed, [(int(o), p.shape) for o, p in zip(offs, parts)]


def _unpack(flat, layout, k):
    off, shape = layout[k]
    return flat[..., off:off + int(np.prod(shape))].reshape(flat.shape[:-1] + tuple(shape))


def kernel(x, c, ctx, c_ctx, norm_g, ada_w, ada_b, pool_w_in, pool_w_grp, pool_scale, pool_w_out, na_w_in, na_rpb, na_w_out, conv_w_in, conv_dw, conv_db, conv_w_out, final_g, loss_target, m_c_ctx, m_norm_g, m_ada_w, m_ada_b, m_pool_w_in, m_pool_w_grp, m_pool_scale, m_pool_w_out, m_na_w_in, m_na_rpb, m_na_w_out, m_conv_w_in, m_conv_dw, m_conv_db, m_conv_w_out, m_final_g, v_c_ctx, v_norm_g, v_ada_w, v_ada_b, v_pool_w_in, v_pool_w_grp, v_pool_scale, v_pool_w_out, v_na_w_in, v_na_rpb, v_na_w_out, v_conv_w_in, v_conv_dw, v_conv_db, v_conv_w_out, v_final_g):
    params = dict(c_ctx=c_ctx, norm_g=norm_g, ada_w=ada_w, ada_b=ada_b, pool_w_in=pool_w_in, pool_w_grp=pool_w_grp,
                  pool_scale=pool_scale, pool_w_out=pool_w_out, na_w_in=na_w_in, na_rpb=na_rpb, na_w_out=na_w_out,
                  conv_w_in=conv_w_in, conv_dw=conv_dw, conv_db=conv_db, conv_w_out=conv_w_out, final_g=final_g)
    mom1 = dict(c_ctx=m_c_ctx, norm_g=m_norm_g, ada_w=m_ada_w, ada_b=m_ada_b, pool_w_in=m_pool_w_in,
                pool_w_grp=m_pool_w_grp, pool_scale=m_pool_scale, pool_w_out=m_pool_w_out, na_w_in=m_na_w_in,
                na_rpb=m_na_rpb, na_w_out=m_na_w_out, conv_w_in=m_conv_w_in, conv_dw=m_conv_dw, conv_db=m_conv_db,
                conv_w_out=m_conv_w_out, final_g=m_final_g)
    mom2 = dict(c_ctx=v_c_ctx, norm_g=v_norm_g, ada_w=v_ada_w, ada_b=v_ada_b, pool_w_in=v_pool_w_in,
                pool_w_grp=v_pool_w_grp, pool_scale=v_pool_scale, pool_w_out=v_pool_w_out, na_w_in=v_na_w_in,
                na_rpb=v_na_rpb, na_w_out=v_na_w_out, conv_w_in=v_conv_w_in, conv_dw=v_conv_dw, conv_db=v_conv_db,
                conv_w_out=v_conv_w_out, final_g=v_final_g)
    d = x.shape[-1]
    w = na_w_out.shape[1] * 4
    xi, yi, ci = _position()
    chip = 2 * xi + yi
    dev = 2 * chip + ci
    n_ada = ada_w.shape[-1]

    def chip_cols(a, size):
        return lax.dynamic_slice_in_dim(a, chip * size, size, axis=a.ndim - 1)

    conds = _all_gather(c.reshape(8, d // 8), _AXES, "gather_cond").reshape(8, d)
    cond = jnp.concatenate([conds, c_ctx[None], jnp.zeros((_COND_ROWS - 9, d), F32)], axis=0)
    mod_cols = _modulations(cond, ada_w, chip_cols(ada_b, n_ada)[:, None, :])

    items = _items(d, w)
    early = [it for it in items if (it.key.startswith("pool") and it.layer == 0) or it.key.startswith("na")]
    late = [it for it in items if it not in early]
    shard_of = lambda it: params[it.key][it.layer].astype(BF16)
    small_pack, small_layout = _pack([pool_scale, conv_dw, conv_db])
    (mod_all,), early_mats, (small,) = _run_comms(
        [_all_gather_comm(mod_cols, ("x", "y")), _gather_comm([shard_of(it) for it in early], early),
         _all_gather_comm(small_pack, ("x", "y"))], "gather_early")
    mod_all = mod_all.transpose(1, 2, 0, 3).reshape(4, _COND_ROWS, 3, d)
    mod = jnp.stack([lax.dynamic_index_in_dim(mod_all, dev, axis=1, keepdims=False), mod_all[:, 8]], axis=1)
    full = {(it.key, it.layer): mat for it, mat in zip(early, early_mats)}
    late_comm = _gather_comm([shard_of(it) for it in late], late)

    def late_weights(mats):
        full.update({(it.key, it.layer): mat for it, mat in zip(late, mats)})
        return dict(pool_w_in=[full[("pool_w_in", j)] for j in range(2)],
                    pool_w_grp=[full[("pool_w_grp", j)] for j in range(2)],
                    pool_w_out=[full[("pool_w_out", j)] for j in range(2)],
                    conv_w_in=full[("conv_w_in", 0)], conv_w_out=full[("conv_w_out", 0)])

    small = small.reshape(4, -1)

    def whole(k):
        parts = _unpack(small, small_layout, k)
        return jnp.moveaxis(parts, 0, -2).reshape(parts.shape[1:-1] + (-1,))

    wts = dict(pool_w_in=[full[("pool_w_in", 0)]], pool_w_grp=[full[("pool_w_grp", 0)]],
               pool_w_out=[full[("pool_w_out", 0)]], na_w_in=full[("na_w_in", 0)], na_w_out=full[("na_w_out", 0)],
               pool_scale=whole(0), na_rpb=na_rpb[0], conv_dw=whole(1)[0], conv_db=whole(2))
    pos = jnp.stack([chip, ci]).astype(jnp.int32)

    def layer_grads(its, by_layer):
        pick = {"pool_w_in": "w_in", "pool_w_grp": "w_grp", "pool_w_out": "w_out", "na_w_in": "w_in",
                "na_w_out": "w_out", "conv_w_in": "w_in", "conv_w_out": "w_out"}
        return [by_layer[(it.key.split("_")[0], it.layer)][pick[it.key]] for it in its]

    def pair_sums(its, mats, tag):
        got = _pair_swap(mats, [(lambda ref, half, it=it: it.window(ref, half=half)) for it in its],
                         [_sds(it.sized(half=True), BF16) for it in its], f"pair_exchange_{tag}")
        return _pair_sums(mats, got, its, pos, f"pair_sum_{tag}")

    pairs = dict()

    def grad_comm(gr3, gr2):
        pairs["late"] = pair_sums(late, layer_grads(late, {("pool", 1): gr3, ("conv", 0): gr2}), "late")
        return _chip_exchange_comm(pairs["late"], late)

    res = _example_step(x[0], ctx[0], loss_target[0], mod, norm_g, final_g[None], wts, late_comm, late_weights,
                        grad_comm)
    g0, g1, g2, g3 = res["layers"]
    pairs["early"] = pair_sums(early, layer_grads(early, {("pool", 0): g0, ("na", 0): g1}), "early")
    packed, layout = _pack([res["dfinal_g"], res["dnorm_g"], res["dmod"], g1["rpb"],
                            jnp.concatenate([g0["scale"], g3["scale"]], axis=0), g2["dw"], g2["db"],
                            res["loss"][0, :1]])
    early_slots, (every,) = _run_comms([_chip_exchange_comm(pairs["early"], early),
                                        _all_gather_comm(packed, _AXES)], "exchange_early")
    slots = dict(zip(late, res["carried"]))
    slots.update(zip(early, early_slots))
    pair_of = dict(zip(late, pairs["late"]))
    pair_of.update(zip(early, pairs["early"]))
    reduced = _chip_sums([pair_of[it] for it in items], [slots[it] for it in items], items, pos, "chip_sum")
    theirs = _pair_swap(reduced, [lambda ref, half: ref] * len(items),
                        [_sds(t.shape, F32) for t in reduced], "pair_return")
    grads, matrix_out = dict(), dict()
    for k in _GRAD_KEYS:
        idx = [i for i, it in enumerate(items) if it.key == k]
        res_k = _adamw_matrix(params[k], mom1[k], mom2[k], [reduced[i] for i in idx], [theirs[i] for i in idx],
                              items[idx[0]], pos, f"adamw_{k}")
        grads[k], matrix_out[k] = res_k[0], res_k[1:]

    total = _sum_lead(every, "sum_vec_grads").reshape(-1)
    every = every.reshape(8, -1)
    grads["final_g"] = _unpack(total, layout, 0).reshape(final_g.shape)
    grads["norm_g"] = _unpack(total, layout, 1)
    grads["na_rpb"] = _unpack(total, layout, 3)[None]
    grads["pool_scale"] = chip_cols(_unpack(total, layout, 4), pool_scale.shape[-1])
    grads["conv_dw"] = chip_cols(_unpack(total, layout, 5), conv_dw.shape[-1])[None]
    grads["conv_db"] = chip_cols(_unpack(total, layout, 6), conv_db.shape[-1])
    dmod_sum = _unpack(total, layout, 2).reshape(4, 2, 3 * d)
    dmod_each = _unpack(every, layout, 2).reshape(8, 4, 2, 3 * d)
    grads["ada_b"] = dmod_sum[:, 0] + dmod_sum[:, 1]
    dm = jnp.concatenate([dmod_each[:, :, 0].transpose(1, 0, 2), dmod_sum[:, 1][:, None],
                          jnp.zeros((4, _COND_ROWS - 9, 3 * d), F32)], axis=1)
    dm_cols = chip_cols(dm, n_ada)
    grads["ada_w"] = _ada_w_grad(cond, dm_cols)
    dcond = _cond_grad(dm_cols, ada_w)[8].reshape(8, d // 8)
    dcond = _sum_lead(_all_gather(dcond, ("x", "y"), "gather_cond_grad"), "sum_cond_grad").reshape(d)
    grads["c_ctx"] = dcond * _dsilu(c_ctx)

    outs = [[], [], []]
    for k in _WEIGHTS:
        step = matrix_out[k] if k in matrix_out else _adamw(params[k], grads[k], mom1[k], mom2[k], f"adamw_{k}")
        for lst, val in zip(outs, step):
            lst.append(val)
    loss = _unpack(total, layout, 7)[0]
    return (loss, res["grad_x"][None], *[grads[k].reshape(params[k].shape) for k in _WEIGHTS],
            *outs[0], *outs[1], *outs[2])
```

```python
import functools

import numpy as np
import jax
import jax.numpy as jnp
from jax import lax
from jax.experimental import pallas as pl
from jax.experimental.pallas import tpu as pltpu

F32 = jnp.float32
BF16 = jnp.bfloat16

EPS = 1e-6
GRID_W = 64
HEAD_DIM = 64
WIN_ROWS = 8
WIN_COLS = 16
POOL_WINDOWS = (2, 4, 8, 16)
Q_ROWS = 4
K_ROWS = 12
PAD_ROWS = 4
NEG = -1e30

ADAM_LR = 0.001
ADAM_B1 = 0.9
ADAM_B2 = 0.999
ADAM_EPS = 1e-08
ADAM_WD = 0.01
ADAM_STEP = 10

ROW_BLOCK = 256
VMEM_LIMIT = 56 * 1024 * 1024
ACT = BF16

MESH = pl.DeviceIdType.MESH
HBM_SPEC = pl.BlockSpec(memory_space=pltpu.HBM)


def _cparams(*sem):
    return pltpu.CompilerParams(dimension_semantics=sem or None, vmem_limit_bytes=VMEM_LIMIT)


def _sds(shape, dtype):
    return jax.ShapeDtypeStruct(tuple(shape), dtype)


def _call(body, args, *, grid, in_specs, out_specs, out_shape, name, scratch_shapes=(), comm=None):
    sem = ("arbitrary",) * len(grid)
    if comm is None:
        res = pl.pallas_call(body, grid=grid, in_specs=list(in_specs), out_specs=list(out_specs), out_shape=list(out_shape),
                             scratch_shapes=list(scratch_shapes), name=name, compiler_params=_cparams(*sem))(*args)
        return list(res), []
    n_in, n_out, n_scr = len(in_specs), len(out_specs), len(scratch_shapes)
    n_cin, n_cout = len(comm.ins), len(comm.outs)

    def carrying(*refs):
        ins, cin = refs[:n_in], refs[n_in:n_in + n_cin]
        outs = refs[n_in + n_cin:n_in + n_cin + n_out]
        cout = refs[n_in + n_cin + n_out:n_in + n_cin + n_out + n_cout]
        rest = refs[n_in + n_cin + n_out + n_cout:]
        scr, sems = rest[:n_scr], rest[n_scr:]
        first, last = True, True
        for ax, size in enumerate(grid):
            first = jnp.logical_and(first, pl.program_id(ax) == 0)
            last = jnp.logical_and(last, pl.program_id(ax) == size - 1)
        pl.when(first)(lambda: comm.start(cin, cout, sems))
        body(*ins, *outs, *scr)
        pl.when(last)(lambda: comm.finish(cin, cout, sems))

    res = pl.pallas_call(
        carrying, grid=grid, in_specs=list(in_specs) + [HBM_SPEC] * n_cin, out_specs=list(out_specs) + [HBM_SPEC] * n_cout,
        out_shape=list(out_shape) + list(comm.outs), scratch_shapes=list(scratch_shapes) + list(comm.sems), name=name,
        compiler_params=_cparams(*sem),
    )(*args, *comm.ins)
    return list(res[:n_out]), list(res[n_out:])


def _sigmoid(x):
    return 1.0 / (1.0 + jnp.exp(-x))


def _silu(x):
    return x * _sigmoid(x)


def _dsilu(x):
    s = _sigmoid(x)
    return s * (1.0 + x * (1.0 - s))


_DIMS = {
    "nn": (((1,), (0,)), ((), ())),
    "nt": (((1,), (1,)), ((), ())),
    "tn": (((0,), (0,)), ((), ())),
}


def _matmul(a, b, *, mode, grid, a_spec, b_spec, out_shapes, out_specs, name, nk=1,
            a_silu=False, exact=False, epilogue=None, extra=(), extra_specs=(), acc_shape=None):
    n_extra = len(extra)
    n_out = len(out_shapes)

    def body(*refs):
        a_ref, b_ref = refs[:2]
        ex = refs[2:2 + n_extra]
        outs = refs[2 + n_extra:2 + n_extra + n_out]
        av = a_ref[...]
        bv = b_ref[...]
        if a_silu:
            av = _silu(av.astype(F32))
        if exact:
            prod = lax.dot_general(av.astype(F32), bv.astype(F32), _DIMS[mode],
                                   precision=lax.Precision.HIGHEST, preferred_element_type=F32)
        else:
            prod = lax.dot_general(av.astype(BF16), bv.astype(BF16), _DIMS[mode], preferred_element_type=F32)

        def finish(res):
            if epilogue is None:
                outs[0][...] = res.astype(outs[0].dtype)
            elif epilogue == "bias":
                outs[0][...] = (res + ex[0][...]).astype(outs[0].dtype)
            else:
                outs[0][...] = res.astype(outs[0].dtype)
                outs[1][...] = ex[0][...] + ex[1][...] * res

        if nk == 1:
            finish(prod)
        else:
            acc = refs[-1]
            k = pl.program_id(len(grid) - 1)

            @pl.when(k == 0)
            def _():
                acc[...] = prod

            @pl.when(k > 0)
            def _():
                acc[...] += prod

            @pl.when(k == nk - 1)
            def _():
                finish(acc[...])

    scratch = [pltpu.VMEM(acc_shape, F32)] if nk > 1 else []
    sem = ("parallel",) * (len(grid) - 1) + ("arbitrary",)
    return pl.pallas_call(
        body, grid=grid, in_specs=[a_spec, b_spec, *extra_specs], out_specs=list(out_specs),
        out_shape=list(out_shapes), scratch_shapes=scratch, name=name, compiler_params=_cparams(*sem),
    )(a, b, *extra)


def _row_tile(rows):
    for t in (768, 512, 256):
        if rows % t == 0:
            return t
    return rows


def _mm_nn(a, b, name, out_dtype=F32, tn=1024):
    m, k = a.shape
    n = b.shape[1]
    tm = _row_tile(m)
    tn = min(tn, n)
    return _matmul(
        a, b, mode="nn", grid=(m // tm, n // tn),
        a_spec=pl.BlockSpec((tm, k), lambda i, j: (i, 0)), b_spec=pl.BlockSpec((k, tn), lambda i, j: (0, j)),
        out_shapes=[_sds((m, n), out_dtype)], out_specs=[pl.BlockSpec((tm, tn), lambda i, j: (i, j))], name=name)[0]


def _mm_out_resid(a, w_out, xres, gate, nxb, name):
    m, k = a.shape
    n = w_out.shape[1]
    tm = ROW_BLOCK
    seg = lambda i, j: (jnp.where(i >= nxb, 1, 0), 0, 0)
    return _matmul(
        a, w_out, mode="nn", grid=(m // tm, 1),
        a_spec=pl.BlockSpec((tm, k), lambda i, j: (i, 0)), b_spec=pl.BlockSpec((k, n), lambda i, j: (0, 0)),
        extra=(xres, gate), extra_specs=(pl.BlockSpec((tm, n), lambda i, j: (i, 0)), pl.BlockSpec((None, 1, n), seg)),
        out_shapes=[_sds((m, n), ACT), _sds((m, n), F32)],
        out_specs=[pl.BlockSpec((tm, n), lambda i, j: (i, 0))] * 2, epilogue="resid", name=name)


def _mm_nt(a, b, name, out_dtype=F32):
    m, n = a.shape
    k = b.shape[0]
    tm = _row_tile(m)
    return _matmul(
        a, b, mode="nt", grid=(m // tm, 1),
        a_spec=pl.BlockSpec((tm, n), lambda i, j: (i, 0)), b_spec=pl.BlockSpec((k, n), lambda i, j: (0, 0)),
        out_shapes=[_sds((m, k), out_dtype)], out_specs=[pl.BlockSpec((tm, k), lambda i, j: (i, 0))], name=name)[0]


def _mm_nt_parts(a, b, name):
    p, m, kp = a.shape
    d = b.shape[0]
    tm = _row_tile(m)
    return _matmul(
        a, b, mode="nt", grid=(m // tm, p), nk=p, acc_shape=(tm, d),
        a_spec=pl.BlockSpec((None, tm, kp), lambda i, q: (q, i, 0)), b_spec=pl.BlockSpec((d, kp), lambda i, q: (0, q)),
        out_shapes=[_sds((m, d), F32)], out_specs=[pl.BlockSpec((tm, d), lambda i, q: (i, 0))], name=name)[0]


def _mm_tn(a, b, name, out_dtype, tm=512):
    r, m = a.shape
    n = b.shape[1]
    tm = min(tm, m)
    tn = min(1024, n)
    return _matmul(
        a, b, mode="tn", grid=(m // tm, n // tn),
        a_spec=pl.BlockSpec((r, tm), lambda i, j: (0, i)), b_spec=pl.BlockSpec((r, tn), lambda i, j: (0, j)),
        out_shapes=[_sds((m, n), out_dtype)], out_specs=[pl.BlockSpec((tm, tn), lambda i, j: (i, j))], name=name)[0]


def _mm_tn_parts(a, b, name, out_dtype, tm=512):
    r, m = a.shape
    p, _, np_ = b.shape
    tm = min(tm, m)
    return _matmul(
        a, b, mode="tn", grid=(m // tm, p),
        a_spec=pl.BlockSpec((r, tm), lambda i, q: (0, i)), b_spec=pl.BlockSpec((None, r, np_), lambda i, q: (q, 0, 0)),
        out_shapes=[_sds((m, p * np_), out_dtype)], out_specs=[pl.BlockSpec((tm, np_), lambda i, q: (i, q))],
        name=name)[0]


def _seg_map(nxb):
    return lambda i: (jnp.where(i >= nxb, 1, 0), 0, 0)


def _normmod_fwd(x, g, scale, shift, nxb, name):
    rows, d = x.shape
    tr = ROW_BLOCK

    def body(x_ref, g_ref, sc_ref, sh_ref, h_ref, r_ref):
        xv = x_ref[...]
        r = lax.rsqrt(jnp.mean(xv * xv, axis=-1, keepdims=True) + EPS)
        h = (xv * r) * g_ref[...] * (1.0 + sc_ref[...]) + sh_ref[...]
        h_ref[...] = h.astype(BF16)
        r_ref[...] = r

    row = pl.BlockSpec((tr, d), lambda i: (i, 0))
    vec = pl.BlockSpec((None, 1, d), _seg_map(nxb))
    return pl.pallas_call(
        body, grid=(rows // tr,), in_specs=[row, pl.BlockSpec((1, d), lambda i: (0, 0)), vec, vec],
        out_specs=[row, pl.BlockSpec((tr, 1), lambda i: (i, 0))],
        out_shape=[_sds((rows, d), BF16), _sds((rows, 1), F32)], name=name, compiler_params=_cparams("parallel"),
    )(x, g, scale, shift)


def _normmod_bwd(dh, x, r, g, scale, dres, nxb, name):
    rows, d = x.shape
    tr = ROW_BLOCK
    nres = dres.shape[0] // tr
    nseg = scale.shape[0]

    def body(dh_ref, x_ref, r_ref, g_ref, sc_ref, dres_ref, dx_ref, dsh_ref, dge_ref):
        i = pl.program_id(0)
        dhv = dh_ref[...]
        rv = r_ref[...]
        xn = x_ref[...] * rv
        dxn = dhv * (g_ref[...] * (1.0 + sc_ref[...]))
        dx = rv * (dxn - xn * jnp.mean(dxn * xn, axis=-1, keepdims=True))

        @pl.when(i < nres)
        def _():
            dx_ref[...] = dx + dres_ref[...]

        @pl.when(i >= nres)
        def _():
            dx_ref[...] = dx

        first = jnp.logical_or(i == 0, i == nxb)
        s_dh = jnp.sum(dhv, axis=0, keepdims=True)
        s_ge = jnp.sum(dhv * xn, axis=0, keepdims=True)

        @pl.when(first)
        def _():
            dsh_ref[...] = s_dh
            dge_ref[...] = s_ge

        @pl.when(jnp.logical_not(first))
        def _():
            dsh_ref[...] += s_dh
            dge_ref[...] += s_ge

    row = pl.BlockSpec((tr, d), lambda i: (i, 0))
    vec = pl.BlockSpec((None, 1, d), _seg_map(nxb))
    return pl.pallas_call(
        body, grid=(rows // tr,),
        in_specs=[row, row, pl.BlockSpec((tr, 1), lambda i: (i, 0)), pl.BlockSpec((1, d), lambda i: (0, 0)), vec,
                  pl.BlockSpec((tr, d), lambda i: (jnp.minimum(i, nres - 1), 0))],
        out_specs=[row, vec, vec],
        out_shape=[_sds((rows, d), F32), _sds((nseg, 1, d), F32), _sds((nseg, 1, d), F32)],
        name=name, compiler_params=_cparams("arbitrary"),
    )(dh, x, r, g, scale, dres)


def _gate_bwd(dxo, yx, gate, nxb, name):
    rows, d = yx.shape
    tr = ROW_BLOCK
    nseg = gate.shape[0]

    def body(dx_ref, yx_ref, gt_ref, dyx_ref, dg_ref):
        i = pl.program_id(0)
        dxv = dx_ref[...]
        dyx_ref[...] = (dxv * gt_ref[...]).astype(BF16)
        s = jnp.sum(dxv * yx_ref[...].astype(F32), axis=0, keepdims=True)
        first = jnp.logical_or(i == 0, i == nxb)

        @pl.when(first)
        def _():
            dg_ref[...] = s

        @pl.when(jnp.logical_not(first))
        def _():
            dg_ref[...] += s

    row = pl.BlockSpec((tr, d), lambda i: (i, 0))
    vec = pl.BlockSpec((None, 1, d), _seg_map(nxb))
    return pl.pallas_call(
        body, grid=(rows // tr,), in_specs=[row, row, vec], out_specs=[row, vec],
        out_shape=[_sds((rows, d), BF16), _sds((nseg, 1, d), F32)], name=name, compiler_params=_cparams("arbitrary"),
    )(dxo, yx, gate)


def _row_vec(ref, is_ctx):
    return ref[0] if is_ctx is None else jnp.where(is_ctx, ref[1], ref[0])


def _ctx_rows(i, tm, nx, nseg):
    if nseg == 1:
        return None
    return i * tm + lax.broadcasted_iota(jnp.int32, (tm, 1), 0) >= nx


def _seg_sums(ref, val, is_ctx, first):
    if is_ctx is None:
        parts = [jnp.sum(val, axis=0, keepdims=True)]
    else:
        parts = [jnp.sum(jnp.where(is_ctx, 0.0, val), axis=0, keepdims=True),
                 jnp.sum(jnp.where(is_ctx, val, 0.0), axis=0, keepdims=True)]

    @pl.when(first)
    def _():
        for k, p in enumerate(parts):
            ref[k] = p

    @pl.when(jnp.logical_not(first))
    def _():
        for k, p in enumerate(parts):
            ref[k] += p


def _w_out_resid(a, w_out, xres, gate, nx, name):
    m, k = a.shape
    n = w_out.shape[1]
    nseg = gate.shape[0]
    tm = _row_tile(m)

    def body(a_ref, w_ref, x_ref, gt_ref, yx_ref, xo_ref):
        yx = jnp.dot(a_ref[...], w_ref[...], preferred_element_type=F32)
        yx_ref[...] = yx.astype(ACT)
        xo_ref[...] = x_ref[...] + _row_vec(gt_ref, _ctx_rows(pl.program_id(0), tm, nx, nseg)) * yx

    row = pl.BlockSpec((tm, n), lambda i: (i, 0))
    return pl.pallas_call(
        body, grid=(m // tm,),
        in_specs=[pl.BlockSpec((tm, k), lambda i: (i, 0)), pl.BlockSpec((k, n), lambda i: (0, 0)), row,
                  pl.BlockSpec((nseg, 1, n), lambda i: (0, 0, 0))],
        out_specs=[row, row], out_shape=[_sds((m, n), ACT), _sds((m, n), F32)],
        name=name, compiler_params=_cparams("parallel"),
    )(a, w_out, xres, gate)


def _norm_w_in(x, g, scale, shift, w_in, nx, name, comm=None):
    rows, d = x.shape
    n = w_in.shape[1]
    nseg = scale.shape[0]
    tm = _row_tile(rows)
    tn = min(1024, n)

    def body(x_ref, g_ref, sc_ref, sh_ref, w_ref, h_ref, r_ref, p_ref):
        i, j = pl.program_id(0), pl.program_id(1)

        @pl.when(j == 0)
        def _():
            xv = x_ref[...]
            r = lax.rsqrt(jnp.mean(xv * xv, axis=-1, keepdims=True) + EPS)
            is_ctx = _ctx_rows(i, tm, nx, nseg)
            h = (xv * r) * g_ref[...] * (1.0 + _row_vec(sc_ref, is_ctx)) + _row_vec(sh_ref, is_ctx)
            h_ref[...] = h.astype(BF16)
            r_ref[...] = r

        p_ref[...] = jnp.dot(h_ref[...], w_ref[...], preferred_element_type=F32).astype(ACT)

    vec = pl.BlockSpec((nseg, 1, d), lambda i, j: (0, 0, 0))
    return _call(
        body, (x, g, scale, shift, w_in), grid=(rows // tm, n // tn),
        in_specs=[pl.BlockSpec((tm, d), lambda i, j: (i, 0)), pl.BlockSpec((1, d), lambda i, j: (0, 0)), vec, vec,
                  pl.BlockSpec((d, tn), lambda i, j: (0, j))],
        out_specs=[pl.BlockSpec((tm, d), lambda i, j: (i, 0)), pl.BlockSpec((tm, 1), lambda i, j: (i, 0)),
                   pl.BlockSpec((tm, tn), lambda i, j: (i, j))],
        out_shape=[_sds((rows, d), BF16), _sds((rows, 1), F32), _sds((rows, n), ACT)], name=name, comm=comm)


def _gate_w_out_bwd(dxo, yx, gate, w_out, nx, name, comm=None):
    rows, d = yx.shape
    w = w_out.shape[0]
    nseg = gate.shape[0]
    tm = _row_tile(rows)

    def body(dx_ref, yx_ref, gt_ref, w_ref, dyx_ref, da_ref, dg_ref):
        i = pl.program_id(0)
        is_ctx = _ctx_rows(i, tm, nx, nseg)
        dxv = dx_ref[...]
        dyx = (dxv * _row_vec(gt_ref, is_ctx)).astype(BF16)
        dyx_ref[...] = dyx
        da_ref[...] = lax.dot_general(dyx, w_ref[...], _DIMS["nt"], preferred_element_type=F32).astype(ACT)
        _seg_sums(dg_ref, dxv * yx_ref[...].astype(F32), is_ctx, i == 0)

    row = pl.BlockSpec((tm, d), lambda i: (i, 0))
    vec = pl.BlockSpec((nseg, 1, d), lambda i: (0, 0, 0))
    return _call(
        body, (dxo, yx, gate, w_out), grid=(rows // tm,),
        in_specs=[row, row, vec, pl.BlockSpec((w, d), lambda i: (0, 0))],
        out_specs=[row, pl.BlockSpec((tm, w), lambda i: (i, 0)), vec],
        out_shape=[_sds((rows, d), BF16), _sds((rows, w), ACT), _sds((nseg, 1, d), F32)], name=name, comm=comm)


def _w_in_bwd_norm(dparts, w_in, x, r, g, scale, dres, nx, name, comm=None):
    np_, rows, kp = dparts.shape
    d = w_in.shape[0]
    nseg = scale.shape[0]
    tm = _row_tile(rows)
    nsub = tm // ROW_BLOCK
    nres_blocks = dres.shape[0] // ROW_BLOCK

    def body(dp_ref, w_ref, x_ref, r_ref, g_ref, sc_ref, *rest):
        dres_refs = rest[:nsub]
        dx_ref, dsh_ref, dge_ref, acc = rest[nsub:]
        i, k = pl.program_id(0), pl.program_id(1)
        prod = lax.dot_general(dp_ref[...], w_ref[...], _DIMS["nt"], preferred_element_type=F32)

        @pl.when(k == 0)
        def _():
            acc[...] = prod

        @pl.when(k > 0)
        def _():
            acc[...] += prod

        @pl.when(k == np_ - 1)
        def _():
            is_ctx = _ctx_rows(i, tm, nx, nseg)
            dhv = acc[...]
            rv = r_ref[...]
            xn = x_ref[...] * rv
            dxn = dhv * (g_ref[...] * (1.0 + _row_vec(sc_ref, is_ctx)))
            dx = rv * (dxn - xn * jnp.mean(dxn * xn, axis=-1, keepdims=True))
            for s in range(nsub):
                piece = slice(s * ROW_BLOCK, (s + 1) * ROW_BLOCK)
                res = dres_refs[s][...]
                if nres_blocks * ROW_BLOCK < rows:
                    res = jnp.where(i * nsub + s < nres_blocks, res, 0.0)
                dx_ref[piece, :] = dx[piece, :] + res
            _seg_sums(dsh_ref, dhv, is_ctx, i == 0)
            _seg_sums(dge_ref, dhv * xn, is_ctx, i == 0)

    row = pl.BlockSpec((tm, d), lambda i, k: (i, 0))
    vec = pl.BlockSpec((nseg, 1, d), lambda i, k: (0, 0, 0))
    return _call(
        body, (dparts, w_in, x, r, g, scale, *([dres] * nsub)), grid=(rows // tm, np_),
        in_specs=[pl.BlockSpec((None, tm, kp), lambda i, k: (k, i, 0)), pl.BlockSpec((d, kp), lambda i, k: (0, k)),
                  row, pl.BlockSpec((tm, 1), lambda i, k: (i, 0)), pl.BlockSpec((1, d), lambda i, k: (0, 0)), vec]
        + [pl.BlockSpec((ROW_BLOCK, d), (lambda i, k, s=s: (jnp.minimum(i * nsub + s, nres_blocks - 1), 0)))
           for s in range(nsub)],
        out_specs=[row, vec, vec],
        out_shape=[_sds((rows, d), F32), _sds((nseg, 1, d), F32), _sds((nseg, 1, d), F32)],
        scratch_shapes=[pltpu.VMEM((tm, d), F32)], name=name, comm=comm)


_PAD_TOP = 16
_PAD_BOT = 32


def _window_sum(buf, xv, lo, n):
    t = xv.shape[0]
    c = xv.shape[1]
    tp = t + _PAD_TOP + _PAD_BOT
    buf[pl.ds(0, _PAD_TOP), :] = jnp.zeros((_PAD_TOP, c), F32)
    buf[pl.ds(_PAD_TOP, t), :] = xv
    buf[pl.ds(_PAD_TOP + t, _PAD_BOT), :] = jnp.zeros((_PAD_BOT, c), F32)
    p = buf[...]
    k = 1
    while k < n:
        p = p + pltpu.roll(p, tp - k, 0)
        k *= 2
    if lo:
        p = pltpu.roll(p, -lo, 0)
    buf[...] = p
    return buf[pl.ds(_PAD_TOP, t), :]


def _window_count(t, half):
    pos = lax.broadcasted_iota(jnp.int32, (t, 1), 0)
    return (jnp.minimum(pos + half, t) - jnp.maximum(pos - half, 0)).astype(F32)


def _segments(rows, nx):
    return [(0, nx)] + ([(nx, rows - nx)] if rows > nx else [])


def _pool_fwd(uv, nx, name):
    rows = uv.shape[0]
    w = uv.shape[1] // 2
    cb = 128
    per_group = w // len(POOL_WINDOWS) // cb
    segs = _segments(rows, nx)

    def body(u_ref, z_ref, *bufs):
        j = pl.program_id(0)
        for gi, win in enumerate(POOL_WINDOWS):
            half = win // 2

            @pl.when(jnp.logical_and(j >= gi * per_group, j < (gi + 1) * per_group))
            def _():
                for (start, length), buf in zip(segs, bufs):
                    uvv = u_ref[pl.ds(start, length), :].astype(F32)
                    s = _window_sum(buf, uvv, -half, win)
                    z_ref[pl.ds(start, length), :] = (s / _window_count(length, half) - uvv).astype(BF16)

    scratch = [pltpu.VMEM((length + _PAD_TOP + _PAD_BOT, cb), F32) for _, length in segs]
    return pl.pallas_call(
        body, grid=(w // cb,), in_specs=[pl.BlockSpec((rows, cb), lambda j: (0, j))],
        out_specs=pl.BlockSpec((rows, cb), lambda j: (0, j)), out_shape=_sds((rows, w), BF16),
        scratch_shapes=scratch, name=name, compiler_params=_cparams("parallel"),
    )(uv)


def _pool_bwd(dz, dgt, nx, name):
    rows, w = dz.shape
    cb = 128
    per_group = w // len(POOL_WINDOWS) // cb
    segs = _segments(rows, nx)

    def body(dz_ref, dgt_ref, o_ref, *bufs):
        j = pl.program_id(0)
        o_ref[1] = dgt_ref[...]
        for gi, win in enumerate(POOL_WINDOWS):
            half = win // 2

            @pl.when(jnp.logical_and(j >= gi * per_group, j < (gi + 1) * per_group))
            def _():
                for (start, length), buf in zip(segs, bufs):
                    dzv = dz_ref[pl.ds(start, length), :].astype(F32)
                    s = _window_sum(buf, dzv / _window_count(length, half), 1 - half, win)
                    o_ref[0, pl.ds(start, length), :] = (s - dzv).astype(BF16)

    scratch = [pltpu.VMEM((length + _PAD_TOP + _PAD_BOT, cb), F32) for _, length in segs]
    col = pl.BlockSpec((rows, cb), lambda j: (0, j))
    return pl.pallas_call(
        body, grid=(w // cb,), in_specs=[col, col], out_specs=pl.BlockSpec((2, rows, cb), lambda j: (0, 0, j)),
        out_shape=_sds((2, rows, w), BF16), scratch_shapes=scratch, name=name, compiler_params=_cparams("parallel"),
    )(dz, dgt)


def _grp_fwd(z, w_grp, uv, scale, name):
    rows, w = z.shape
    ng, gc, _ = w_grp.shape
    tm = _row_tile(rows)

    def body(z_ref, w_ref, gt_ref, sc_ref, mx_ref, a_ref):
        mixed = jnp.dot(z_ref[...], w_ref[...], preferred_element_type=F32)
        mx_ref[...] = mixed.astype(ACT)
        a_ref[...] = (mixed * sc_ref[...] * _silu(gt_ref[...].astype(F32))).astype(BF16)

    blk = pl.BlockSpec((tm, gc), lambda g, i: (i, g))
    return pl.pallas_call(
        body, grid=(ng, rows // tm),
        in_specs=[blk, pl.BlockSpec((None, gc, gc), lambda g, i: (g, 0, 0)),
                  pl.BlockSpec((tm, gc), lambda g, i: (i, ng + g)), pl.BlockSpec((1, gc), lambda g, i: (0, g))],
        out_specs=[blk, blk], out_shape=[_sds((rows, w), ACT), _sds((rows, w), BF16)],
        name=name, compiler_params=_cparams("parallel", "parallel"),
    )(z, w_grp, uv, scale)


def _grp_bwd(da, mixed, uv, scale, w_grp, name):
    rows, w = da.shape
    ng, gc, _ = w_grp.shape
    tm = _row_tile(rows)

    def body(da_ref, mx_ref, gt_ref, sc_ref, w_ref, dm_ref, dz_ref, dgt_ref, dsc_ref):
        i = pl.program_id(1)
        dav = da_ref[...].astype(F32)
        mixed = mx_ref[...].astype(F32)
        gt = gt_ref[...].astype(F32)
        sg = _silu(gt)
        sc = sc_ref[...]
        dm = (dav * sc * sg).astype(BF16)
        dm_ref[...] = dm
        dz_ref[...] = lax.dot_general(dm, w_ref[...], _DIMS["nt"], preferred_element_type=F32).astype(ACT)
        dgt_ref[...] = (dav * mixed * sc * _dsilu(gt)).astype(BF16)
        s = jnp.sum(dav * mixed * sg, axis=0, keepdims=True)

        @pl.when(i == 0)
        def _():
            dsc_ref[...] = s

        @pl.when(i > 0)
        def _():
            dsc_ref[...] += s

    blk = pl.BlockSpec((tm, gc), lambda g, i: (i, g))
    vec = pl.BlockSpec((1, gc), lambda g, i: (0, g))
    return pl.pallas_call(
        body, grid=(ng, rows // tm),
        in_specs=[blk, blk, pl.BlockSpec((tm, gc), lambda g, i: (i, ng + g)), vec,
                  pl.BlockSpec((None, gc, gc), lambda g, i: (g, 0, 0))],
        out_specs=[blk, blk, blk, vec],
        out_shape=[_sds((rows, w), BF16), _sds((rows, w), ACT), _sds((rows, w), BF16), _sds((1, w), F32)],
        name=name, compiler_params=_cparams("parallel", "arbitrary"),
    )(da, mixed, uv, scale, w_grp)


def _pool_scratch(rows, nx, cols):
    return [pltpu.VMEM((length + _PAD_TOP + _PAD_BOT, cols), F32) for _, length in _segments(rows, nx)]


def _per_group(g, fn):
    for gi, win in enumerate(POOL_WINDOWS):
        pl.when(g == gi)(functools.partial(fn, win))


def _pool_grp_fwd(uv, w_grp, scale, nx, name, comm=None):
    rows = uv.shape[0]
    ng, gc, _ = w_grp.shape
    w = ng * gc
    segs = _segments(rows, nx)

    def body(u_ref, gt_ref, w_ref, sc_ref, z_ref, mx_ref, a_ref, *bufs):
        def pool(win):
            half = win // 2
            for (start, length), buf in zip(segs, bufs):
                uvv = u_ref[pl.ds(start, length), :].astype(F32)
                s = _window_sum(buf, uvv, -half, win)
                z_ref[pl.ds(start, length), :] = (s / _window_count(length, half) - uvv).astype(BF16)

        _per_group(pl.program_id(0), pool)
        mixed = jnp.dot(z_ref[...], w_ref[...], preferred_element_type=F32)
        mx_ref[...] = mixed.astype(ACT)
        a_ref[...] = (mixed * sc_ref[...] * _silu(gt_ref[...].astype(F32))).astype(BF16)

    col = pl.BlockSpec((rows, gc), lambda g: (0, g))
    return _call(
        body, (uv, uv, w_grp, scale), grid=(ng,),
        in_specs=[col, pl.BlockSpec((rows, gc), lambda g: (0, ng + g)), pl.BlockSpec((None, gc, gc), lambda g: (g, 0, 0)),
                  pl.BlockSpec((1, gc), lambda g: (0, g))],
        out_specs=[col, col, col], out_shape=[_sds((rows, w), BF16), _sds((rows, w), ACT), _sds((rows, w), BF16)],
        scratch_shapes=_pool_scratch(rows, nx, gc), name=name, comm=comm)


def _pool_grp_bwd(da, mixed, uv, scale, w_grp, nx, name):
    rows, w = da.shape
    ng, gc, _ = w_grp.shape
    segs = _segments(rows, nx)

    def body(da_ref, mx_ref, gt_ref, sc_ref, w_ref, dm_ref, duv_ref, dsc_ref, dz_ref, *bufs):
        dav = da_ref[...].astype(F32)
        mixed = mx_ref[...].astype(F32)
        gt = gt_ref[...].astype(F32)
        sg = _silu(gt)
        sc = sc_ref[...]
        dm = (dav * sc * sg).astype(BF16)
        dm_ref[...] = dm
        dz_ref[...] = lax.dot_general(dm, w_ref[...], _DIMS["nt"], preferred_element_type=F32)
        duv_ref[1] = (dav * mixed * sc * _dsilu(gt)).astype(BF16)
        dsc_ref[...] = jnp.sum(dav * mixed * sg, axis=0, keepdims=True)

        def unpool(win):
            half = win // 2
            for (start, length), buf in zip(segs, bufs):
                dzv = dz_ref[pl.ds(start, length), :]
                s = _window_sum(buf, dzv / _window_count(length, half), 1 - half, win)
                duv_ref[0, pl.ds(start, length), :] = (s - dzv).astype(BF16)

        _per_group(pl.program_id(0), unpool)

    col = pl.BlockSpec((rows, gc), lambda g: (0, g))
    vec = pl.BlockSpec((1, gc), lambda g: (0, g))
    return pl.pallas_call(
        body, grid=(ng,),
        in_specs=[col, col, pl.BlockSpec((rows, gc), lambda g: (0, ng + g)), vec,
                  pl.BlockSpec((None, gc, gc), lambda g: (g, 0, 0))],
        out_specs=[col, pl.BlockSpec((2, rows, gc), lambda g: (0, 0, g)), vec],
        out_shape=[_sds((rows, w), BF16), _sds((2, rows, w), BF16), _sds((1, w), F32)],
        scratch_shapes=[pltpu.VMEM((rows, gc), F32)] + _pool_scratch(rows, nx, gc),
        name=name, compiler_params=_cparams("parallel"),
    )(da, mixed, uv, scale, w_grp)


def _grp_wgrad(z, dm, ng, name, out_dtype):
    rows, w = z.shape
    gc = w // ng

    def body(z_ref, dm_ref, o_ref):
        o_ref[...] = lax.dot_general(z_ref[...], dm_ref[...], _DIMS["tn"],
                                     preferred_element_type=F32).astype(o_ref.dtype)

    blk = pl.BlockSpec((rows, gc), lambda g: (0, g))
    return pl.pallas_call(
        body, grid=(ng,), in_specs=[blk, blk], out_specs=pl.BlockSpec((None, gc, gc), lambda g: (g, 0, 0)),
        out_shape=_sds((ng, gc, gc), out_dtype), name=name, compiler_params=_cparams("parallel"),
    )(z, dm)


def _shift_rows(v, by):
    t = v.shape[0]
    pos = lax.broadcasted_iota(jnp.int32, v.shape, 0)
    rolled = pltpu.roll(v, by % t, 0)
    keep = pos >= by if by > 0 else pos < t + by
    return jnp.where(keep, rolled, 0.0)


def _conv_specs(t, w, cb):
    return [pl.BlockSpec((t, cb), (lambda j, q=q: (0, q * (w // cb) + j))) for q in range(4)]


def _conv_fwd(p4, dw, db, name):
    t = p4.shape[0]
    w = p4.shape[1] // 4
    cb = 128

    def body(bg_ref, cg_ref, v_ref, g_ref, dw_ref, db_ref, a_ref):
        tv = cg_ref[...].astype(F32) * v_ref[...].astype(F32)
        conv = (dw_ref[0:1, :] * _shift_rows(tv, 1) + dw_ref[1:2, :] * tv + dw_ref[2:3, :] * _shift_rows(tv, -1)
                + db_ref[...])
        a_ref[...] = (bg_ref[...].astype(F32) * conv * _silu(g_ref[...].astype(F32))).astype(BF16)

    return pl.pallas_call(
        body, grid=(w // cb,),
        in_specs=_conv_specs(t, w, cb) + [pl.BlockSpec((3, cb), lambda j: (0, j)), pl.BlockSpec((1, cb), lambda j: (0, j))],
        out_specs=pl.BlockSpec((t, cb), lambda j: (0, j)), out_shape=_sds((t, w), BF16),
        name=name, compiler_params=_cparams("parallel"),
    )(p4, p4, p4, p4, dw, db)


def _conv_bwd(da, p4, dw, db, name):
    t, w = da.shape
    cb = 128

    def body(da_ref, bg_ref, cg_ref, v_ref, g_ref, dw_ref, db_ref, d4_ref, ddw_ref, ddb_ref):
        cg = cg_ref[...].astype(F32)
        vv = v_ref[...].astype(F32)
        bg = bg_ref[...].astype(F32)
        gv = g_ref[...].astype(F32)
        tv = cg * vv
        tm1 = _shift_rows(tv, 1)
        tp1 = _shift_rows(tv, -1)
        w0, w1, w2 = dw_ref[0:1, :], dw_ref[1:2, :], dw_ref[2:3, :]
        conv = w0 * tm1 + w1 * tv + w2 * tp1 + db_ref[...]
        y = bg * conv
        dav = da_ref[...].astype(F32)
        dy = dav * _silu(gv)
        d4_ref[3] = (dav * y * _dsilu(gv)).astype(BF16)
        d4_ref[0] = (dy * conv).astype(BF16)
        dconv = dy * bg
        ddb_ref[...] = jnp.sum(dconv, axis=0, keepdims=True)
        ddw_ref[0:1, :] = jnp.sum(dconv * tm1, axis=0, keepdims=True)
        ddw_ref[1:2, :] = jnp.sum(dconv * tv, axis=0, keepdims=True)
        ddw_ref[2:3, :] = jnp.sum(dconv * tp1, axis=0, keepdims=True)
        dt = w0 * _shift_rows(dconv, -1) + w1 * dconv + w2 * _shift_rows(dconv, 1)
        d4_ref[1] = (dt * vv).astype(BF16)
        d4_ref[2] = (dt * cg).astype(BF16)

    col = pl.BlockSpec((t, cb), lambda j: (0, j))
    tap = pl.BlockSpec((3, cb), lambda j: (0, j))
    bias = pl.BlockSpec((1, cb), lambda j: (0, j))
    return pl.pallas_call(
        body, grid=(w // cb,), in_specs=[col] + _conv_specs(t, w, cb) + [tap, bias],
        out_specs=[pl.BlockSpec((4, t, cb), lambda j: (0, 0, j)), tap, bias],
        out_shape=[_sds((4, t, w), BF16), _sds((3, w), F32), _sds((1, w), F32)],
        name=name, compiler_params=_cparams("parallel"),
    )(da, p4, p4, p4, p4, dw, db)


def _attn_mask():
    qn, kn = Q_ROWS * GRID_W, K_ROWS * GRID_W
    qr, qc = np.divmod(np.arange(qn), GRID_W)
    kr, kc = np.divmod(np.arange(kn), GRID_W)
    col0 = np.clip(qc - WIN_COLS // 2, 0, GRID_W - WIN_COLS)
    col_ok = (kc[None, :] >= col0[:, None]) & (kc[None, :] < col0[:, None] + WIN_COLS)
    first = np.zeros(qn, np.int64)
    last = np.full(qn, K_ROWS - WIN_ROWS)
    out = []
    for row0 in (first, qr, last):
        row_ok = (kr[None, :] >= row0[:, None]) & (kr[None, :] < row0[:, None] + WIN_ROWS)
        out.append(np.where(row_ok & col_ok, 0.0, NEG))
    return jnp.asarray(np.stack(out), F32)


_KW = K_ROWS * GRID_W
_QB = Q_ROWS * GRID_W
_PAIR = 2 * HEAD_DIM
_N_DR = 2 * WIN_ROWS - 1
_N_DC = 2 * WIN_COLS - 1
_RP_ROWS = 24
_N_TILES = _N_DR + 1
_BIAS_BASE = (WIN_ROWS - 1, WIN_ROWS // 2 - 1, -1)


class _Comm:
    def __init__(self, ins, outs, sems, start, finish):
        self.ins, self.outs, self.sems, self.start, self.finish = list(ins), list(outs), list(sems), start, finish


def _bias_pieces(cls):
    out = []
    for qr in range(Q_ROWS):
        for kr in range(0, K_ROWS, 2):
            tile = _BIAS_BASE[cls] - qr + kr + 1
            out.append((qr, kr, tile if 0 <= tile < _N_TILES else None))
    return out


def _toeplitz_pair(left_row, right_row):
    lane = lax.broadcasted_iota(jnp.int32, (GRID_W, _PAIR), 1)
    shape = (GRID_W, _PAIR)
    left = pltpu.roll(jnp.broadcast_to(left_row, shape), _PAIR - (WIN_COLS - 1), 1, stride=1, stride_axis=0)
    right = pltpu.roll(jnp.broadcast_to(right_row, shape), GRID_W - (WIN_COLS - 1), 1, stride=1, stride_axis=0)
    return jnp.where(lane < GRID_W, left, right)


def _build_tiles(tiles_ref, rp_ref):
    for h in range(2):
        for t in range(_N_TILES):
            tiles_ref[h, t] = _toeplitz_pair(rp_ref[h, t:t + 1, :], rp_ref[h, t + 1:t + 2, :])


def _block_class(b, nblk, fn, entering=False):
    interior = (b == 1) if entering else jnp.logical_and(b > 0, b < nblk - 1)
    for cls, cond in enumerate((b == 0, interior, b == nblk - 1)):
        pl.when(cond)(functools.partial(fn, cls))


def _attn_geometry(p4, nx):
    rows = p4.shape[0]
    w = p4.shape[1] // 4
    nhp = w // _PAIR
    nblk = nx // _QB
    qspec = lambda col: pl.BlockSpec((_QB, _PAIR), lambda hp, b: (b, col * nhp + hp))
    kspec = lambda col: pl.BlockSpec((rows, _PAIR), lambda hp, b: (0, col * nhp + hp))
    tspec = pl.BlockSpec((2, _RP_ROWS, _PAIR), lambda hp, b: (hp, 0, 0))
    mspec = pl.BlockSpec((None, _QB, _KW), lambda hp, b: (jnp.where(b == 0, 0, jnp.where(b == nblk - 1, 2, 1)), 0, 0))
    lspec = pl.BlockSpec((None, _QB, 2), lambda hp, b: (hp, b, 0))
    ospec = pl.BlockSpec((_QB, _PAIR), lambda hp, b: (b, hp))
    return rows, w, nhp, nblk, qspec, kspec, tspec, mspec, lspec, ospec


def _window_start(b, nx):
    return pl.multiple_of(jnp.clip(b * _QB - PAD_ROWS * GRID_W, 0, nx - _KW), _QB)


def _load_bias(bias_ref, tiles_ref, rp_ref, m_ref, b, nblk):
    pl.when(b == 0)(lambda: _build_tiles(tiles_ref, rp_ref))

    def fill(cls):
        for h in range(2):
            for qr, kr, tile in _bias_pieces(cls):
                rows = slice(qr * GRID_W, (qr + 1) * GRID_W)
                cols = slice(kr * GRID_W, (kr + 2) * GRID_W)
                m = m_ref[rows, cols]
                bias_ref[h, rows, cols] = m if tile is None else tiles_ref[h, tile] + m

    _block_class(b, nblk, fill, entering=True)


def _attn_fwd(p4, rp, mask, nx, name, comm=None):
    rows, w, nhp, nblk, qspec, kspec, tspec, mspec, lspec, ospec = _attn_geometry(p4, nx)
    n_ctx = rows - nx
    n_cin, n_cout = (len(comm.ins), len(comm.outs)) if comm else (0, 0)

    def body(*refs):
        q_ref, k_ref, v_ref, g_ref, rp_ref, m_ref = refs[:6]
        cin = refs[6:6 + n_cin]
        a_ref, o_ref, lse_ref = refs[6 + n_cin:9 + n_cin]
        cout = refs[9 + n_cin:9 + n_cin + n_cout]
        bias_ref, tiles_ref = refs[9 + n_cin + n_cout:11 + n_cin + n_cout]
        sems = refs[11 + n_cin + n_cout:]
        hp, b = pl.program_id(0), pl.program_id(1)
        if comm:
            pl.when(jnp.logical_and(hp == 0, b == 0))(lambda: comm.start(cin, cout, sems))
        start = _window_start(b, nx)
        _load_bias(bias_ref, tiles_ref, rp_ref, m_ref, b, nblk)
        qf = q_ref[...].astype(F32) * HEAD_DIM ** -0.5
        kw = k_ref[pl.ds(start, _KW), :].astype(BF16)
        vw = v_ref[pl.ds(start, _KW), :].astype(BF16)
        kcv = k_ref[pl.ds(nx, n_ctx), :].astype(BF16)
        vcv = v_ref[pl.ds(nx, n_ctx), :].astype(BF16)
        lane = lax.broadcasted_iota(jnp.int32, (1, _PAIR), 1)
        outs, lses = [], []
        for h in range(2):
            mine = (lane >= HEAD_DIM) if h else (lane < HEAD_DIM)
            qm = jnp.where(mine, qf, 0.0).astype(BF16)
            s_loc = lax.dot_general(qm, kw, _DIMS["nt"], preferred_element_type=F32) + bias_ref[h]
            s_ctx = lax.dot_general(qm, kcv, _DIMS["nt"], preferred_element_type=F32)
            mx = jnp.maximum(jnp.max(s_loc, axis=-1, keepdims=True), jnp.max(s_ctx, axis=-1, keepdims=True))
            p_loc = jnp.exp(s_loc - mx)
            p_ctx = jnp.exp(s_ctx - mx)
            den = jnp.sum(p_loc, axis=-1, keepdims=True) + jnp.sum(p_ctx, axis=-1, keepdims=True)
            o = jnp.dot(p_loc.astype(BF16), vw, preferred_element_type=F32)
            o = o + jnp.dot(p_ctx.astype(BF16), vcv, preferred_element_type=F32)
            outs.append(o * (1.0 / den))
            lses.append(mx + jnp.log(den))
        o = jnp.where(lane < HEAD_DIM, outs[0], outs[1])
        o_ref[...] = o.astype(ACT)
        a_ref[...] = (o * _silu(g_ref[...].astype(F32))).astype(BF16)
        col = lax.broadcasted_iota(jnp.int32, (1, 2), 1)
        lse_ref[...] = jnp.where(col == 0, lses[0], lses[1])
        if comm:
            pl.when(jnp.logical_and(hp == nhp - 1, b == nblk - 1))(lambda: comm.finish(cin, cout, sems))

    res = pl.pallas_call(
        body, grid=(nhp, nblk),
        in_specs=[qspec(0), kspec(1), kspec(2), qspec(3), tspec, mspec] + [HBM_SPEC] * n_cin,
        out_specs=[ospec, ospec, lspec] + [HBM_SPEC] * n_cout,
        out_shape=[_sds((nx, w), BF16), _sds((nx, w), ACT), _sds((nhp, nx, 2), F32)] + (comm.outs if comm else []),
        scratch_shapes=[pltpu.VMEM((2, _QB, _KW), F32), pltpu.VMEM((2, _N_TILES, GRID_W, _PAIR), F32)]
        + (comm.sems if comm else []),
        name=name, compiler_params=_cparams("arbitrary", "arbitrary"),
    )(p4, p4, p4, p4, rp, mask, *(comm.ins if comm else []))
    return res[:3], res[3:]


def _fold_tiles(dtiles_ref, drp_ref):
    shape = (GRID_W, _PAIR)
    lane = lax.broadcasted_iota(jnp.int32, shape, 1)
    flip = (lax.broadcasted_iota(jnp.int32, (_PAIR, _PAIR), 0)
            + lax.broadcasted_iota(jnp.int32, (_PAIR, _PAIR), 1) == _PAIR - 1).astype(F32)
    drp_ref[...] = jnp.zeros(drp_ref.shape, F32)
    for h in range(2):
        stack = dtiles_ref[h].reshape(_N_TILES * GRID_W, _PAIR)
        rev = jnp.dot(stack, flip, precision=lax.Precision.HIGHEST, preferred_element_type=F32)
        for t in range(_N_TILES):
            tile = rev[t * GRID_W:(t + 1) * GRID_W, :]
            for side in (0, 1):
                shift = _PAIR - GRID_W * side - (WIN_COLS - 1)
                half = jnp.where((lane < GRID_W) if side else (lane >= GRID_W), tile, 0.0)
                diag = pltpu.roll(half, shift, 1, stride=1, stride_axis=0)
                drp_ref[h, t + side:t + side + 1, :] += jnp.sum(diag, axis=0, keepdims=True)


def _attn_bwd(p4, rp, mask, o, lse, da, nx, name, comm=None):
    rows, w, nhp, nblk, qspec, kspec, tspec, mspec, lspec, ospec = _attn_geometry(p4, nx)
    n_ctx = rows - nx
    n_cin, n_cout = (len(comm.ins), len(comm.outs)) if comm else (0, 0)

    def body(*refs):
        q_ref, k_ref, v_ref, g_ref, rp_ref, m_ref, o_ref, lse_ref, da_ref = refs[:9]
        cin = refs[9:9 + n_cin]
        d4_ref, drp_ref = refs[9 + n_cin:11 + n_cin]
        cout = refs[11 + n_cin:11 + n_cin + n_cout]
        bias_ref, tiles_ref, ds_ref, dtiles_ref, dk_ref, dv_ref = refs[11 + n_cin + n_cout:17 + n_cin + n_cout]
        sems = refs[17 + n_cin + n_cout:]
        hp, b = pl.program_id(0), pl.program_id(1)
        if comm:
            pl.when(jnp.logical_and(hp == 0, b == 0))(lambda: comm.start(cin, cout, sems))
        start = _window_start(b, nx)
        here = pl.multiple_of(b * _QB, _QB)

        @pl.when(b == 0)
        def _():
            dk_ref[...] = jnp.zeros(dk_ref.shape, F32)
            dv_ref[...] = jnp.zeros(dv_ref.shape, F32)
            dtiles_ref[...] = jnp.zeros(dtiles_ref.shape, F32)
            d4_ref[0, pl.ds(nx, n_ctx), :] = jnp.zeros((n_ctx, _PAIR), BF16)
            d4_ref[3, pl.ds(nx, n_ctx), :] = jnp.zeros((n_ctx, _PAIR), BF16)

        _load_bias(bias_ref, tiles_ref, rp_ref, m_ref, b, nblk)
        gv = g_ref[...].astype(F32)
        dav = da_ref[...].astype(F32)
        ov = o_ref[...].astype(F32)
        dov = dav * _silu(gv)
        d4_ref[3, pl.ds(here, _QB), :] = (dav * ov * _dsilu(gv)).astype(BF16)
        qf = q_ref[...].astype(F32) * HEAD_DIM ** -0.5
        kw = k_ref[pl.ds(start, _KW), :].astype(BF16)
        vw = v_ref[pl.ds(start, _KW), :].astype(BF16)
        kcv = k_ref[pl.ds(nx, n_ctx), :].astype(BF16)
        vcv = v_ref[pl.ds(nx, n_ctx), :].astype(BF16)
        lane = lax.broadcasted_iota(jnp.int32, (1, _PAIR), 1)
        dq = jnp.zeros((_QB, _PAIR), F32)
        for h in range(2):
            mine = (lane >= HEAD_DIM) if h else (lane < HEAD_DIM)
            qm = jnp.where(mine, qf, 0.0).astype(BF16)
            dom = jnp.where(mine, dov, 0.0)
            dob = dom.astype(BF16)
            lse = lse_ref[:, h:h + 1]
            s_loc = lax.dot_general(qm, kw, _DIMS["nt"], preferred_element_type=F32)
            p_loc = jnp.exp(s_loc + bias_ref[h] - lse)
            p_ctx = jnp.exp(lax.dot_general(qm, kcv, _DIMS["nt"], preferred_element_type=F32) - lse)
            delta = jnp.sum(dom * ov, axis=-1, keepdims=True)
            ds_loc = p_loc * (lax.dot_general(dob, vw, _DIMS["nt"], preferred_element_type=F32) - delta)
            ds_ctx = p_ctx * (lax.dot_general(dob, vcv, _DIMS["nt"], preferred_element_type=F32) - delta)
            dsb_loc = ds_loc.astype(BF16)
            dsb_ctx = ds_ctx.astype(BF16)
            dq_h = (jnp.dot(dsb_loc, kw, preferred_element_type=F32)
                    + jnp.dot(dsb_ctx, kcv, preferred_element_type=F32))
            dq = dq + jnp.where(mine, dq_h, 0.0)
            dk_ref[pl.ds(start, _KW), :] += lax.dot_general(dsb_loc, qm, _DIMS["tn"], preferred_element_type=F32)
            dv_ref[pl.ds(start, _KW), :] += lax.dot_general(p_loc.astype(BF16), dob, _DIMS["tn"],
                                                            preferred_element_type=F32)
            dk_ref[pl.ds(nx, n_ctx), :] += lax.dot_general(dsb_ctx, qm, _DIMS["tn"], preferred_element_type=F32)
            dv_ref[pl.ds(nx, n_ctx), :] += lax.dot_general(p_ctx.astype(BF16), dob, _DIMS["tn"],
                                                           preferred_element_type=F32)
            ds_ref[h] = ds_loc
        d4_ref[0, pl.ds(here, _QB), :] = (dq * HEAD_DIM ** -0.5).astype(BF16)

        def scatter(cls):
            for h in range(2):
                for qr, kr, tile in _bias_pieces(cls):
                    if tile is not None:
                        dtiles_ref[h, tile] += ds_ref[h, qr * GRID_W:(qr + 1) * GRID_W, kr * GRID_W:(kr + 2) * GRID_W]

        _block_class(b, nblk, scatter)

        @pl.when(b == nblk - 1)
        def _():
            d4_ref[1] = dk_ref[...].astype(BF16)
            d4_ref[2] = dv_ref[...].astype(BF16)
            _fold_tiles(dtiles_ref, drp_ref)

        if comm:
            pl.when(jnp.logical_and(hp == nhp - 1, b == nblk - 1))(lambda: comm.finish(cin, cout, sems))

    tiles = pltpu.VMEM((2, _N_TILES, GRID_W, _PAIR), F32)
    block = pltpu.VMEM((2, _QB, _KW), F32)
    res = pl.pallas_call(
        body, grid=(nhp, nblk),
        in_specs=[qspec(0), kspec(1), kspec(2), qspec(3), tspec, mspec, ospec, lspec, ospec] + [HBM_SPEC] * n_cin,
        out_specs=[pl.BlockSpec((4, rows, _PAIR), lambda hp, b: (0, 0, hp)), tspec] + [HBM_SPEC] * n_cout,
        out_shape=[_sds((4, rows, w), BF16), _sds(rp.shape, F32)] + (comm.outs if comm else []),
        scratch_shapes=[block, tiles, block, tiles, pltpu.VMEM((rows, _PAIR), F32), pltpu.VMEM((rows, _PAIR), F32)]
        + (comm.sems if comm else []),
        name=name, compiler_params=_cparams("arbitrary", "arbitrary"),
    )(p4, p4, p4, p4, rp, mask, o, lse, da, *(comm.ins if comm else []))
    return res[:2], res[2:]


def _final(x, g, target, name):
    rows, d = x.shape
    tr = ROW_BLOCK
    nblk = rows // tr

    def body(x_ref, g_ref, t_ref, loss_ref, dx_ref, dg_ref, acc_ref):
        i = pl.program_id(0)
        xv = x_ref[...]
        gv = g_ref[...]
        r = lax.rsqrt(jnp.mean(xv * xv, axis=-1, keepdims=True) + EPS)
        xn = xv * r
        err = xn * gv - t_ref[...]
        dy = err * (1.0 / d)
        dxn = dy * gv
        dx_ref[...] = r * (dxn - xn * jnp.mean(dxn * xn, axis=-1, keepdims=True))
        s_g = jnp.sum(dy * xn, axis=0, keepdims=True)
        s_l = jnp.sum(jnp.mean(err * err, axis=-1, keepdims=True), axis=0, keepdims=True)

        @pl.when(i == 0)
        def _():
            dg_ref[...] = s_g
            acc_ref[...] = s_l

        @pl.when(i > 0)
        def _():
            dg_ref[...] += s_g
            acc_ref[...] += s_l

        @pl.when(i == nblk - 1)
        def _():
            loss_ref[...] = jnp.broadcast_to(0.5 * acc_ref[...], loss_ref.shape)

    row = pl.BlockSpec((tr, d), lambda i: (i, 0))
    vec = pl.BlockSpec((1, d), lambda i: (0, 0))
    return pl.pallas_call(
        body, grid=(nblk,), in_specs=[row, vec, row],
        out_specs=[pl.BlockSpec((1, 128), lambda i: (0, 0)), row, vec],
        out_shape=[_sds((1, 128), F32), _sds((rows, d), F32), _sds((1, d), F32)],
        scratch_shapes=[pltpu.VMEM((1, 1), F32)], name=name, compiler_params=_cparams("arbitrary"),
    )(x, g, target)


def _as2d(a):
    if a.ndim == 1:
        return a.reshape(-1, 128) if a.shape[0] % 128 == 0 else a.reshape(1, -1)
    return a.reshape(-1, a.shape[-1])


def _adamw(w, g, m, v, name):
    shape = w.shape
    w2, g2, m2, v2 = (_as2d(t) for t in (w, g.reshape(shape), m, v))
    rows, cols = w2.shape
    tr = 512 if rows % 512 == 0 else rows
    c1 = 1.0 - ADAM_B1 ** ADAM_STEP
    c2 = 1.0 - ADAM_B2 ** ADAM_STEP

    def body(w_ref, g_ref, m_ref, v_ref, d_ref, nm_ref, nv_ref):
        gv = g_ref[...]
        nm = ADAM_B1 * m_ref[...] + (1.0 - ADAM_B1) * gv
        nv = ADAM_B2 * v_ref[...] + (1.0 - ADAM_B2) * (gv * gv)
        nm_ref[...] = nm
        nv_ref[...] = nv
        d_ref[...] = -ADAM_LR * ((nm / c1) / (jnp.sqrt(nv / c2) + ADAM_EPS) + ADAM_WD * w_ref[...])

    blk = pl.BlockSpec((tr, cols), lambda i: (i, 0))
    outs = pl.pallas_call(
        body, grid=(rows // tr,), in_specs=[blk] * 4, out_specs=[blk] * 3,
        out_shape=[_sds((rows, cols), F32)] * 3, name=name, compiler_params=_cparams("parallel"),
    )(w2, g2, m2, v2)
    return tuple(t.reshape(shape) for t in outs)


def _sum_lead(x, name, out_dtype=F32):
    n, rows, cols = x.shape
    tr = 512 if rows % 512 == 0 else rows

    def body(x_ref, o_ref):
        acc = x_ref[0].astype(F32)
        for k in range(1, n):
            acc = acc + x_ref[k].astype(F32)
        o_ref[...] = acc.astype(out_dtype)

    return pl.pallas_call(
        body, grid=(rows // tr,), in_specs=[pl.BlockSpec((n, tr, cols), lambda i: (0, i, 0))],
        out_specs=pl.BlockSpec((tr, cols), lambda i: (i, 0)), out_shape=_sds((rows, cols), out_dtype),
        name=name, compiler_params=_cparams("parallel"),
    )(x)


_NO_CTX = 1 << 30


def _seg_vecs(mod_l, which, nseg):
    return mod_l[:nseg, which][:, None, :]


def _norm_grads(dshift, dgeff, dgate, g, scale):
    nseg, _, d = dshift.shape
    dmod = jnp.stack([dshift[:, 0], dgeff[:, 0] * g, dgate[:, 0]], axis=1)
    if nseg == 1:
        dmod = jnp.concatenate([dmod, jnp.zeros((1, 3, d), F32)], axis=0)
    dg = jnp.sum(dgeff[:, 0] * (1.0 + scale[:, 0]), axis=0)
    return dmod, dg


def _pool_layer(xin, g, mod_l, w_in, w_grp, w_out, pscale, nx, tag, comms=None):
    rows = xin.shape[0]
    nseg = 2 if rows > nx else 1
    comms = comms or {}
    shift, scale, gate = (_seg_vecs(mod_l, k, nseg) for k in range(3))
    (h, r, uv), c_in = _norm_w_in(xin, g, scale, shift, w_in, nx, f"w_in_fwd_{tag}", comms.get("w_in_fwd"))
    (z, mixed, a), c_pool = _pool_grp_fwd(uv, w_grp, pscale, nx, f"pool_fwd_{tag}", comms.get("pool_fwd"))
    yx, xout = _w_out_resid(a, w_out, xin, gate, nx, f"w_out_fwd_{tag}")

    def backward(dxo, comms=None):
        comms = comms or {}
        (dyx, da, dgate), c_out = _gate_w_out_bwd(dxo, yx, gate, w_out, nx, f"w_out_bwd_{tag}", comms.get("w_out_bwd"))
        gw_out = _mm_tn(a, dyx, f"w_out_grad_{tag}", BF16)
        dm, duv, dscale = _pool_grp_bwd(da, mixed, uv, pscale, w_grp, nx, f"pool_bwd_{tag}")
        gw_grp = _grp_wgrad(z, dm, w_grp.shape[0], f"grp_grad_{tag}", BF16)
        gw_in = _mm_tn_parts(h, duv, f"w_in_grad_{tag}", BF16)
        (dx, dshift, dgeff), c_bwd = _w_in_bwd_norm(duv, w_in, xin, r, g, scale, dxo, nx, f"w_in_bwd_{tag}",
                                                    comms.get("w_in_bwd"))
        dmod, dg = _norm_grads(dshift, dgeff, dgate, g[0], scale)
        return (dx, dmod, dg, dict(w_in=gw_in, w_grp=gw_grp, w_out=gw_out, scale=dscale),
                dict(w_out_bwd=c_out, w_in_bwd=c_bwd))

    return xout, backward, dict(w_in_fwd=c_in, pool_fwd=c_pool)


def _na_layer(xc, g, mod_l, w_in, rpb, w_out, nx, mask, comm=None):
    nh, n_dr, n_dc = rpb.shape
    shift, scale = _seg_vecs(mod_l, 0, 2), _seg_vecs(mod_l, 1, 2)
    gate = _seg_vecs(mod_l, 2, 1)
    (h, r, p4), _ = _norm_w_in(xc, g, scale, shift, w_in, nx, "w_in_fwd_na")
    rp = jnp.pad(rpb, ((0, 0), (1, _RP_ROWS - 1 - n_dr), (0, _PAIR - n_dc)))
    (a, o, lse), carried = _attn_fwd(p4, rp, mask, nx, "attn_fwd", comm)
    yx, xout = _w_out_resid(a, w_out, xc, gate, nx, "w_out_fwd_na")

    def backward(dxo, comm=None):
        (dyx, da, dgate), _ = _gate_w_out_bwd(dxo, yx, gate, w_out, nx, "w_out_bwd_na")
        gw_out = _mm_tn(a, dyx, "w_out_grad_na", BF16)
        (d4, drp), carried_bwd = _attn_bwd(p4, rp, mask, o, lse, da, nx, "attn_bwd", comm)
        gw_in = _mm_tn_parts(h, d4, "w_in_grad_na", BF16)
        (dx, dshift, dgeff), _ = _w_in_bwd_norm(d4, w_in, xc, r, g, scale, dxo, nx, "w_in_bwd_na")
        dgate2 = jnp.concatenate([dgate, jnp.zeros_like(dgate)], axis=0)
        dmod, dg = _norm_grads(dshift, dgeff, dgate2, g[0], scale)
        drpb = drp[:, 1:1 + n_dr, ::-1][:, :, :n_dc]
        return dx, dmod, dg, dict(w_in=gw_in, w_out=gw_out, rpb=drpb), carried_bwd

    return xout, backward, carried


def _conv_layer(xin, g, mod_l, w_in, dw, db, w_out):
    shift, scale, gate = (_seg_vecs(mod_l, k, 1) for k in range(3))
    nx = xin.shape[0]
    (h, r, p4), _ = _norm_w_in(xin, g, scale, shift, w_in, nx, "w_in_fwd_conv")
    a = _conv_fwd(p4, dw, db, "conv_fwd")
    yx, xout = _w_out_resid(a, w_out, xin, gate, nx, "w_out_fwd_conv")

    def backward(dxo):
        (dyx, da, dgate), _ = _gate_w_out_bwd(dxo, yx, gate, w_out, nx, "w_out_bwd_conv")
        gw_out = _mm_tn(a, dyx, "w_out_grad_conv", BF16)
        d4, ddw, ddb = _conv_bwd(da, p4, dw, db, "conv_bwd")
        gw_in = _mm_tn_parts(h, d4, "w_in_grad_conv", BF16)
        (dx, dshift, dgeff), _ = _w_in_bwd_norm(d4, w_in, xin, r, g, scale, dxo, nx, "w_in_bwd_conv")
        dmod, dg = _norm_grads(dshift, dgeff, dgate, g[0], scale)
        return dx, dmod, dg, dict(w_in=gw_in, w_out=gw_out, dw=ddw, db=ddb)

    return xout, backward


def _example_step(x, ctx, target, mod, norm_g, final_g, wts, na_comms=None, na_weights=None, late_comm=None,
                  late_weights=None, grad_comm=None, na_grad_comms=None):
    nx = x.shape[0]
    consts = _attn_mask()
    g_rows = [norm_g[i:i + 1] for i in range(4)]
    xc0 = jnp.concatenate([x, ctx], axis=0)
    xc1, bwd0, carried0 = _pool_layer(xc0, g_rows[0], mod[0], wts["pool_w_in"][0], wts["pool_w_grp"][0],
                                      wts["pool_w_out"][0], wts["pool_scale"][0:1], nx, "p0", na_comms)
    if na_weights is not None:
        wts = {**wts, **na_weights(carried0)}
    x2, bwd1, carried = _na_layer(xc1, g_rows[1], mod[1], wts["na_w_in"], wts["na_rpb"], wts["na_w_out"], nx, consts,
                                  late_comm)
    if late_weights is not None:
        wts = {**wts, **late_weights(carried)}
    x3, bwd2 = _conv_layer(x2, g_rows[2], mod[2], wts["conv_w_in"], wts["conv_dw"], wts["conv_db"], wts["conv_w_out"])
    x4, bwd3, _ = _pool_layer(x3, g_rows[3], mod[3], wts["pool_w_in"][1], wts["pool_w_grp"][1], wts["pool_w_out"][1],
                              wts["pool_scale"][1:2], nx, "p3")
    loss, dx4, dfinal_g = _final(x4, final_g, target, "loss_head")
    dx3, dmod3, dg3, gr3, _ = bwd3(dx4)
    dx2, dmod2, dg2, gr2 = bwd2(dx3)
    dxc1, dmod1, dg1, gr1, carried_bwd = bwd1(dx2, grad_comm(gr3, gr2) if grad_comm else None)
    dxc0, dmod0, dg0, gr0, carried_bwd0 = bwd0(dxc1, na_grad_comms(gr1) if na_grad_comms else None)
    return dict(
        loss=loss, grad_x=dxc0[:nx], dmod=jnp.stack([dmod0, dmod1, dmod2, dmod3]),
        dnorm_g=jnp.stack([dg0, dg1, dg2, dg3]), dfinal_g=dfinal_g, layers=(gr0, gr1, gr2, gr3), carried=carried_bwd,
        carried0=carried_bwd0)


_AXES = ("x", "y", "c")
_CHIP_FLIPS = ((1, 0), (0, 1), (1, 1))


def _position():
    return tuple(lax.axis_index(a) for a in _AXES)


def _flipped(pos, flip):
    return tuple(1 - p if f else p for p, f in zip(pos, flip))


def _run_comms(comms, name):
    n_in = [len(c.ins) for c in comms]
    n_out = [len(c.outs) for c in comms]
    n_sem = [len(c.sems) for c in comms]

    def body(*refs):
        ins, outs, sems = refs[:sum(n_in)], refs[sum(n_in):sum(n_in) + sum(n_out)], refs[sum(n_in) + sum(n_out):]
        parts = []
        for k in range(len(comms)):
            a, b, s = sum(n_in[:k]), sum(n_out[:k]), sum(n_sem[:k])
            parts.append((ins[a:a + n_in[k]], outs[b:b + n_out[k]], sems[s:s + n_sem[k]]))
        for c, part in zip(comms, parts):
            c.start(*part)
        for c, part in zip(comms, parts):
            c.finish(*part)

    res = pl.pallas_call(
        body, in_specs=[HBM_SPEC] * sum(n_in), out_specs=[HBM_SPEC] * sum(n_out),
        out_shape=[o for c in comms for o in c.outs], scratch_shapes=[s for c in comms for s in c.sems], name=name,
    )(*[a for c in comms for a in c.ins])
    return [res[sum(n_out[:k]):sum(n_out[:k + 1])] for k in range(len(comms))]


def _all_gather_comm(v, axes):
    flips = [f for f in np.ndindex(2, 2, 2) if any(f) and all(a in axes or not b for a, b in zip(_AXES, f))]
    n = len(flips) + 1

    def copies(ins, outs, sems):
        (v_ref,), (o_ref,), (send_sems, recv_sems, local_sem) = ins, outs, sems
        pos = _position()
        slot = 0
        for a, p in zip(_AXES, pos):
            if a in axes:
                slot = 2 * slot + p
        local = pltpu.make_async_copy(v_ref, o_ref.at[slot], local_sem)
        remote = [pltpu.make_async_remote_copy(v_ref, o_ref.at[slot], send_sems.at[k], recv_sems.at[k],
                                               device_id=_flipped(pos, flip), device_id_type=MESH)
                  for k, flip in enumerate(flips)]
        return [local] + remote

    def start(ins, outs, sems):
        for cp in copies(ins, outs, sems):
            cp.start()

    def finish(ins, outs, sems):
        for cp in copies(ins, outs, sems):
            cp.wait()

    sems = [pltpu.SemaphoreType.DMA((n - 1,)), pltpu.SemaphoreType.DMA((n - 1,)), pltpu.SemaphoreType.DMA(())]
    return _Comm([v], [_sds((n,) + v.shape, v.dtype)], sems, start, finish)


def _all_gather(v, axes, name):
    return _run_comms([_all_gather_comm(v, axes)], name)[0][0]


class _Item:
    def __init__(self, key, layer, shape, shard_axis, half_axis):
        self.key, self.layer, self.shape = key, layer, tuple(shape)
        self.shard_axis, self.half_axis = shard_axis, half_axis
        self.shard = shape[shard_axis] // 4
        self.half = shape[half_axis] // 2

    def sized(self, shard=False, half=False):
        s = list(self.shape)
        if shard:
            s[self.shard_axis] = self.shard
        if half:
            s[self.half_axis] = self.half
        return tuple(s)

    def window(self, ref, chip=None, half=None):
        idx = [slice(None)] * len(self.shape)
        if chip is not None:
            idx[self.shard_axis] = pl.ds(chip * self.shard, self.shard)
        if half is not None:
            idx[self.half_axis] = pl.ds(half * self.half, self.half)
        return ref.at[tuple(idx)]


def _items(d, w):
    out = []
    for j in range(2):
        out += [_Item("pool_w_in", j, (d, 2 * w), 1, 0), _Item("pool_w_grp", j, (4, w // 4, w // 4), 1, 0),
                _Item("pool_w_out", j, (w, d), 0, 1)]
    out += [_Item("na_w_in", 0, (d, 4 * w), 1, 0), _Item("na_w_out", 0, (w, d), 0, 1),
            _Item("conv_w_in", 0, (d, 4 * w), 1, 0), _Item("conv_w_out", 0, (w, d), 0, 1)]
    return out


def _gather_weights(shards, items, name):
    comm = _gather_comm(shards, items)

    def body(*refs):
        n = len(items)
        comm.start(refs[:n], refs[n:2 * n], refs[2 * n:])
        comm.finish(refs[:n], refs[n:2 * n], refs[2 * n:])

    return pl.pallas_call(
        body, in_specs=[HBM_SPEC] * len(items), out_specs=[HBM_SPEC] * len(items), out_shape=comm.outs,
        scratch_shapes=comm.sems, name=name,
    )(*shards)


def _gather_comm(shards, items):
    n = len(items)

    def copies(src, dst, sems, onward):
        send_a, recv_a, send_b, recv_b, send_c, recv_c = sems
        x, y, c = _position()
        chip = 2 * x + y
        sibling = (x, y, 1 - c)
        own, out, fwd, fwd_in = [], [], [], []
        for i, it in enumerate(items):
            own.append(pltpu.make_async_remote_copy(src[i], it.window(dst[i], chip=chip), send_c.at[i], recv_c.at[i],
                                                    device_id=sibling, device_id_type=MESH))
            for k, flip in enumerate(_CHIP_FLIPS):
                px, py = _flipped((x, y), flip)
                s = 3 * i + k
                out.append(pltpu.make_async_remote_copy(
                    it.window(src[i], half=c), it.window(dst[i], chip=chip, half=c), send_a.at[s], recv_a.at[s],
                    device_id=(px, py, c), device_id_type=MESH))
                if onward:
                    got = it.window(dst[i], chip=2 * px + py, half=c)
                    fwd.append(pltpu.make_async_remote_copy(got, got, send_b.at[s], recv_b.at[s],
                                                            device_id=sibling, device_id_type=MESH))
                    other = it.window(dst[i], chip=2 * px + py, half=1 - c)
                    fwd_in.append(pltpu.make_async_remote_copy(other, other, send_b.at[s], recv_b.at[s],
                                                               device_id=sibling, device_id_type=MESH))
        return own, out, fwd, fwd_in

    def start(src, dst, sems):
        own, out, _, _ = copies(src, dst, sems, False)
        for cp in own + out:
            cp.start()

    def finish(src, dst, sems):
        own, out, fwd, fwd_in = copies(src, dst, sems, True)
        for arrived, onward in zip(out, fwd):
            arrived.wait_recv()
            onward.start()
        for cp in fwd_in:
            cp.wait_recv()
        for cp in out + fwd:
            cp.wait_send()
        for cp in own:
            cp.wait()

    sems = [pltpu.SemaphoreType.DMA((3 * n,)) for _ in range(4)] + [pltpu.SemaphoreType.DMA((n,)) for _ in range(2)]
    return _Comm(shards, [_sds(it.shape, BF16) for it in items], sems, start, finish)


def _pair_swap_comm(arrays, windows, out_shapes):
    n = len(arrays)

    def copies(src, got, sems):
        send_sems, recv_sems = sems
        x, y, c = _position()
        return [pltpu.make_async_remote_copy(windows[i](src[i], 1 - c), got[i], send_sems.at[i], recv_sems.at[i],
                                             device_id=(x, y, 1 - c), device_id_type=MESH) for i in range(n)]

    def start(src, got, sems):
        for cp in copies(src, got, sems):
            cp.start()

    def finish(src, got, sems):
        for cp in copies(src, got, sems):
            cp.wait()

    return _Comm(arrays, out_shapes, [pltpu.SemaphoreType.DMA((n,)), pltpu.SemaphoreType.DMA((n,))], start, finish)


def _pair_swap(arrays, windows, out_shapes, name):
    return _run_comms([_pair_swap_comm(arrays, windows, out_shapes)], name)[0]


def _chip_exchange(partials, items, name):
    comm = _chip_exchange_comm(partials, items)

    def body(*refs):
        n = len(items)
        comm.start(refs[:n], refs[n:2 * n], refs[2 * n:])
        comm.finish(refs[:n], refs[n:2 * n], refs[2 * n:])

    return pl.pallas_call(
        body, in_specs=[HBM_SPEC] * len(items), out_specs=[HBM_SPEC] * len(items), out_shape=comm.outs,
        scratch_shapes=comm.sems, name=name,
    )(*partials)


def _chip_exchange_comm(partials, items):
    n = len(items)

    def copies(src, dst, sems):
        send_sems, recv_sems = sems
        x, y, c = _position()
        out = []
        for i, it in enumerate(items):
            for k, flip in enumerate(_CHIP_FLIPS):
                px, py = _flipped((x, y), flip)
                out.append(pltpu.make_async_remote_copy(
                    it.window(src[i], chip=2 * px + py), dst[i].at[k], send_sems.at[3 * i + k],
                    recv_sems.at[3 * i + k], device_id=(px, py, c), device_id_type=MESH))
        return out

    def start(src, dst, sems):
        for cp in copies(src, dst, sems):
            cp.start()

    def finish(src, dst, sems):
        for cp in copies(src, dst, sems):
            cp.wait()

    return _Comm(partials, [_sds((3,) + it.sized(shard=True, half=True), BF16) for it in items],
                 [pltpu.SemaphoreType.DMA((3 * n,)), pltpu.SemaphoreType.DMA((3 * n,))], start, finish)


_SUM_STEPS = 2


def _pair_sums(gs, gots, its, pos, name):
    n = len(its)
    nb = _SUM_STEPS
    g2 = [g.reshape(-1, g.shape[-1]) for g in gs]
    got2 = [t.reshape(-1, t.shape[-1]) for t in gots]

    def body(pos_ref, *refs):
        for g_ref, got_ref, o_ref in zip(refs[:n], refs[n:2 * n], refs[2 * n:]):
            o_ref[...] = (g_ref[...].astype(F32) + got_ref[...].astype(F32)).astype(BF16)

    g_specs, got_specs = [], []
    for it, t in zip(its, got2):
        rows, cols = t.shape
        blk = (rows // nb, cols)
        g_map = (lambda i, pos: (pos[1] * nb + i, 0)) if it.half_axis == 0 else (lambda i, pos: (i, pos[1]))
        g_specs.append(pl.BlockSpec(blk, g_map))
        got_specs.append(pl.BlockSpec(blk, lambda i, pos: (i, 0)))
    outs = pl.pallas_call(
        body, grid_spec=pltpu.PrefetchScalarGridSpec(
            num_scalar_prefetch=1, grid=(nb,), in_specs=g_specs + got_specs, out_specs=got_specs),
        out_shape=[_sds(t.shape, BF16) for t in got2], name=name, compiler_params=_cparams("parallel"),
    )(pos, *g2, *got2)
    return [o.reshape(t.shape) for o, t in zip(outs, gots)]


_FLIP_SLOT = {2: 0, 1: 1, 3: 2}


def _chip_sums(pairs, slots, its, pos, name):
    n = len(its)
    nb = _SUM_STEPS

    def body(pos_ref, *refs):
        chip = pos_ref[0]
        for own in range(4):
            @pl.when(chip == own)
            def _():
                for p_ref, s_ref, o_ref in zip(refs[:n], refs[n:2 * n], refs[2 * n:]):
                    acc = None
                    for k in range(4):
                        v = (p_ref[...] if k == own else s_ref[_FLIP_SLOT[own ^ k]]).astype(F32)
                        acc = v if acc is None else acc + v
                    o_ref[...] = acc

    p_specs, s_specs, o_specs, shapes = [], [], [], []
    for it in its:
        shape = it.sized(shard=True, half=True)
        blk = (shape[0] // nb,) + shape[1:]
        rest = (0,) * (len(shape) - 1)

        def p_map(i, pos, it=it, nd=len(shape)):
            lead = i + (pos[0] * nb if it.shard_axis == 0 else 0)
            return (lead,) + tuple(pos[0] if ax == it.shard_axis else 0 for ax in range(1, nd))

        p_specs.append(pl.BlockSpec(blk, p_map))
        s_specs.append(pl.BlockSpec((3,) + blk, lambda i, pos, rest=rest: (0, i) + rest))
        o_specs.append(pl.BlockSpec(blk, lambda i, pos, rest=rest: (i,) + rest))
        shapes.append(_sds(shape, F32))
    return pl.pallas_call(
        body, grid_spec=pltpu.PrefetchScalarGridSpec(
            num_scalar_prefetch=1, grid=(nb,), in_specs=p_specs + s_specs, out_specs=o_specs),
        out_shape=shapes, name=name, compiler_params=_cparams("parallel"),
    )(pos, *pairs, *slots)


_GRAD_KEYS = ("pool_w_in", "pool_w_grp", "pool_w_out", "na_w_in", "na_w_out", "conv_w_in", "conv_w_out")


def _adamw_matrix(w, m, v, owns, others, it, pos, name):
    nl = w.shape[0]
    rows_split = it.half_axis == 0
    r, cdim = int(np.prod(w.shape[1:-1])), w.shape[-1]
    hr, hc = (r // 2, cdim) if rows_split else (r, cdim // 2)
    br = min(hr, 256)
    nb = hr // br
    c1 = 1.0 - ADAM_B1 ** ADAM_STEP
    c2 = 1.0 - ADAM_B2 ** ADAM_STEP

    def body(pos_ref, w_ref, m_ref, v_ref, *rest):
        own_refs, other_refs = rest[:nl], rest[nl:2 * nl]
        g_ref, d_ref, nm_ref, nv_ref = rest[2 * nl:]
        j, h = pl.program_id(0), pl.program_id(1)
        own, other = own_refs[0][...], other_refs[0][...]
        for q in range(1, nl):
            own = jnp.where(j == q, own_refs[q][...], own)
            other = jnp.where(j == q, other_refs[q][...], other)
        gv = jnp.where(h == pos_ref[1], own, other)
        nm = ADAM_B1 * m_ref[...] + (1.0 - ADAM_B1) * gv
        nv = ADAM_B2 * v_ref[...] + (1.0 - ADAM_B2) * (gv * gv)
        g_ref[...] = gv
        nm_ref[...] = nm
        nv_ref[...] = nv
        d_ref[...] = -ADAM_LR * ((nm / c1) / (jnp.sqrt(nv / c2) + ADAM_EPS) + ADAM_WD * w_ref[...])

    if rows_split:
        full = pl.BlockSpec((None, br, hc), lambda j, h, i, pos: (j, h * nb + i, 0))
    else:
        full = pl.BlockSpec((None, br, hc), lambda j, h, i, pos: (j, i, h))
    half = pl.BlockSpec((br, hc), lambda j, h, i, pos: (i, 0))
    flat = lambda t: t.reshape(nl, r, cdim)
    outs = pl.pallas_call(
        body, grid_spec=pltpu.PrefetchScalarGridSpec(
            num_scalar_prefetch=1, grid=(nl, 2, nb), in_specs=[full] * 3 + [half] * (2 * nl), out_specs=[full] * 4),
        out_shape=[_sds((nl, r, cdim), F32)] * 4, name=name,
        compiler_params=_cparams("parallel", "parallel", "parallel"),
    )(pos, flat(w), flat(m), flat(v), *[t.reshape(hr, hc) for t in list(owns) + list(others)])
    return tuple(t.reshape(w.shape) for t in outs)


_WEIGHTS = ("c_ctx", "norm_g", "ada_w", "ada_b", "pool_w_in", "pool_w_grp", "pool_scale", "pool_w_out", "na_w_in",
            "na_rpb", "na_w_out", "conv_w_in", "conv_dw", "conv_db", "conv_w_out", "final_g")
_COND_ROWS = 16


def _modulations(cond, ada_w, ada_b_cols):
    nl, d, n = ada_w.shape
    return _matmul(
        cond, ada_w, mode="nn", grid=(nl, 1), a_silu=True, epilogue="bias",
        a_spec=pl.BlockSpec((_COND_ROWS, d), lambda i, j: (0, 0)), b_spec=pl.BlockSpec((None, d, n), lambda i, j: (i, 0, 0)),
        extra=(ada_b_cols,), extra_specs=(pl.BlockSpec((None, 1, n), lambda i, j: (i, 0, 0)),),
        out_shapes=[_sds((nl, _COND_ROWS, n), F32)], out_specs=[pl.BlockSpec((None, _COND_ROWS, n), lambda i, j: (i, 0, 0))],
        name="modulations")[0]


def _ada_w_grad(cond, dm_cols):
    d = cond.shape[1]
    nl, _, n = dm_cols.shape
    return _matmul(
        cond, dm_cols, mode="tn", grid=(nl, 1), a_silu=True,
        a_spec=pl.BlockSpec((_COND_ROWS, d), lambda i, j: (0, 0)), b_spec=pl.BlockSpec((None, _COND_ROWS, n), lambda i, j: (i, 0, 0)),
        out_shapes=[_sds((nl, d, n), F32)], out_specs=[pl.BlockSpec((None, d, n), lambda i, j: (i, 0, 0))],
        name="ada_w_grad")[0]


def _cond_grad(dm_cols, ada_w):
    nl, d, n = ada_w.shape
    return _matmul(
        dm_cols, ada_w, mode="nt", grid=(1, nl), nk=nl, acc_shape=(_COND_ROWS, d),
        a_spec=pl.BlockSpec((None, _COND_ROWS, n), lambda i, q: (q, 0, 0)), b_spec=pl.BlockSpec((None, d, n), lambda i, q: (q, 0, 0)),
        out_shapes=[_sds((_COND_ROWS, d), F32)], out_specs=[pl.BlockSpec((_COND_ROWS, d), lambda i, q: (0, 0))],
        name="cond_grad")[0]


def _pack(parts):
    flat = [p.reshape(-1) for p in parts]
    sizes = [f.shape[0] for f in flat]
    total = sum(sizes)
    rows = -(-total // 1024) * 8
    packed = jnp.concatenate(flat + [jnp.zeros((rows * 128 - total,), F32)]).reshape(rows, 128)
    offs = np.concatenate([[0], np.cumsum(sizes)])[:-1]
    return packed, [(int(o), p.shape) for o, p in zip(offs, parts)]


def _unpack(flat, layout, k):
    off, shape = layout[k]
    return flat[..., off:off + int(np.prod(shape))].reshape(flat.shape[:-1] + tuple(shape))


def kernel(x, c, ctx, c_ctx, norm_g, ada_w, ada_b, pool_w_in, pool_w_grp, pool_scale, pool_w_out, na_w_in, na_rpb, na_w_out, conv_w_in, conv_dw, conv_db, conv_w_out, final_g, loss_target, m_c_ctx, m_norm_g, m_ada_w, m_ada_b, m_pool_w_in, m_pool_w_grp, m_pool_scale, m_pool_w_out, m_na_w_in, m_na_rpb, m_na_w_out, m_conv_w_in, m_conv_dw, m_conv_db, m_conv_w_out, m_final_g, v_c_ctx, v_norm_g, v_ada_w, v_ada_b, v_pool_w_in, v_pool_w_grp, v_pool_scale, v_pool_w_out, v_na_w_in, v_na_rpb, v_na_w_out, v_conv_w_in, v_conv_dw, v_conv_db, v_conv_w_out, v_final_g):
    params = dict(c_ctx=c_ctx, norm_g=norm_g, ada_w=ada_w, ada_b=ada_b, pool_w_in=pool_w_in, pool_w_grp=pool_w_grp,
                  pool_scale=pool_scale, pool_w_out=pool_w_out, na_w_in=na_w_in, na_rpb=na_rpb, na_w_out=na_w_out,
                  conv_w_in=conv_w_in, conv_dw=conv_dw, conv_db=conv_db, conv_w_out=conv_w_out, final_g=final_g)
    mom1 = dict(c_ctx=m_c_ctx, norm_g=m_norm_g, ada_w=m_ada_w, ada_b=m_ada_b, pool_w_in=m_pool_w_in,
                pool_w_grp=m_pool_w_grp, pool_scale=m_pool_scale, pool_w_out=m_pool_w_out, na_w_in=m_na_w_in,
                na_rpb=m_na_rpb, na_w_out=m_na_w_out, conv_w_in=m_conv_w_in, conv_dw=m_conv_dw, conv_db=m_conv_db,
                conv_w_out=m_conv_w_out, final_g=m_final_g)
    mom2 = dict(c_ctx=v_c_ctx, norm_g=v_norm_g, ada_w=v_ada_w, ada_b=v_ada_b, pool_w_in=v_pool_w_in,
                pool_w_grp=v_pool_w_grp, pool_scale=v_pool_scale, pool_w_out=v_pool_w_out, na_w_in=v_na_w_in,
                na_rpb=v_na_rpb, na_w_out=v_na_w_out, conv_w_in=v_conv_w_in, conv_dw=v_conv_dw, conv_db=v_conv_db,
                conv_w_out=v_conv_w_out, final_g=v_final_g)
    d = x.shape[-1]
    w = na_w_out.shape[1] * 4
    xi, yi, ci = _position()
    chip = 2 * xi + yi
    dev = 2 * chip + ci
    n_ada = ada_w.shape[-1]

    def chip_cols(a, size):
        return lax.dynamic_slice_in_dim(a, chip * size, size, axis=a.ndim - 1)

    conds = _all_gather(c.reshape(8, d // 8), _AXES, "gather_cond").reshape(8, d)
    cond = jnp.concatenate([conds, c_ctx[None], jnp.zeros((_COND_ROWS - 9, d), F32)], axis=0)
    mod_cols = _modulations(cond, ada_w, chip_cols(ada_b, n_ada)[:, None, :])

    items = _items(d, w)
    first = [it for it in items if it.key.startswith("pool") and it.layer == 0]
    na_in, na_out = ([it for it in items if it.key == k] for k in ("na_w_in", "na_w_out"))
    late = [it for it in items if it not in first + na_in + na_out]
    shards_of = lambda its: [params[it.key][it.layer].astype(BF16) for it in its]
    small_pack, small_layout = _pack([pool_scale, conv_dw, conv_db])
    (mod_all,), first_mats, (small,) = _run_comms(
        [_all_gather_comm(mod_cols, ("x", "y")), _gather_comm(shards_of(first), first),
         _all_gather_comm(small_pack, ("x", "y"))], "gather_first")
    mod_all = mod_all.transpose(1, 2, 0, 3).reshape(4, _COND_ROWS, 3, d)
    mod = jnp.stack([lax.dynamic_index_in_dim(mod_all, dev, axis=1, keepdims=False), mod_all[:, 8]], axis=1)
    full = {(it.key, it.layer): mat for it, mat in zip(first, first_mats)}
    na_comms = dict(w_in_fwd=_gather_comm(shards_of(na_in), na_in), pool_fwd=_gather_comm(shards_of(na_out), na_out))
    late_comm = _gather_comm(shards_of(late), late)

    def na_weights(carried):
        return dict(na_w_in=carried["w_in_fwd"][0], na_w_out=carried["pool_fwd"][0])

    def late_weights(mats):
        full.update({(it.key, it.layer): mat for it, mat in zip(late, mats)})
        return dict(pool_w_in=[full[("pool_w_in", j)] for j in range(2)],
                    pool_w_grp=[full[("pool_w_grp", j)] for j in range(2)],
                    pool_w_out=[full[("pool_w_out", j)] for j in range(2)],
                    conv_w_in=full[("conv_w_in", 0)], conv_w_out=full[("conv_w_out", 0)])

    small = small.reshape(4, -1)

    def whole(k):
        parts = _unpack(small, small_layout, k)
        return jnp.moveaxis(parts, 0, -2).reshape(parts.shape[1:-1] + (-1,))

    wts = dict(pool_w_in=[full[("pool_w_in", 0)]], pool_w_grp=[full[("pool_w_grp", 0)]],
               pool_w_out=[full[("pool_w_out", 0)]], pool_scale=whole(0), na_rpb=na_rpb[0], conv_dw=whole(1)[0],
               conv_db=whole(2))
    pos = jnp.stack([chip, ci]).astype(jnp.int32)

    def layer_grads(its, by_layer):
        pick = {"pool_w_in": "w_in", "pool_w_grp": "w_grp", "pool_w_out": "w_out", "na_w_in": "w_in",
                "na_w_out": "w_out", "conv_w_in": "w_in", "conv_w_out": "w_out"}
        return [by_layer[(it.key.split("_")[0], it.layer)][pick[it.key]] for it in its]

    def pair_sums(its, mats, tag):
        got = _pair_swap(mats, [(lambda ref, half, it=it: it.window(ref, half=half)) for it in its],
                         [_sds(it.sized(half=True), BF16) for it in its], f"pair_exchange_{tag}")
        return _pair_sums(mats, got, its, pos, f"pair_sum_{tag}")

    pairs = dict()

    def grad_comm(gr3, gr2):
        pairs["late"] = pair_sums(late, layer_grads(late, {("pool", 1): gr3, ("conv", 0): gr2}), "late")
        return _chip_exchange_comm(pairs["late"], late)

    def na_grad_comms(gr1):
        na = na_in + na_out
        pairs["na"] = pair_sums(na, layer_grads(na, {("na", 0): gr1}), "na")
        return dict(w_in_bwd=_chip_exchange_comm(pairs["na"][:1], na_in),
                    w_out_bwd=_chip_exchange_comm(pairs["na"][1:], na_out))

    res = _example_step(x[0], ctx[0], loss_target[0], mod, norm_g, final_g[None], wts, na_comms, na_weights,
                        late_comm, late_weights, grad_comm, na_grad_comms)
    g0, g1, g2, g3 = res["layers"]
    pairs["first"] = pair_sums(first, layer_grads(first, {("pool", 0): g0}), "first")
    packed, layout = _pack([res["dfinal_g"], res["dnorm_g"], res["dmod"], g1["rpb"],
                            jnp.concatenate([g0["scale"], g3["scale"]], axis=0), g2["dw"], g2["db"],
                            res["loss"][0, :1]])
    first_slots, (every,) = _run_comms([_chip_exchange_comm(pairs["first"], first),
                                        _all_gather_comm(packed, _AXES)], "exchange_first")
    slots = dict(zip(late, res["carried"]))
    slots.update(zip(first, first_slots))
    slots.update(zip(na_in + na_out, res["carried0"]["w_in_bwd"] + res["carried0"]["w_out_bwd"]))
    pair_of = dict(zip(late, pairs["late"]))
    pair_of.update(zip(first, pairs["first"]))
    pair_of.update(zip(na_in + na_out, pairs["na"]))
    reduced = _chip_sums([pair_of[it] for it in items], [slots[it] for it in items], items, pos, "chip_sum")
    theirs = _pair_swap(reduced, [lambda ref, half: ref] * len(items),
                        [_sds(t.shape, F32) for t in reduced], "pair_return")
    grads, matrix_out = dict(), dict()
    for k in _GRAD_KEYS:
        idx = [i for i, it in enumerate(items) if it.key == k]
        res_k = _adamw_matrix(params[k], mom1[k], mom2[k], [reduced[i] for i in idx], [theirs[i] for i in idx],
                              items[idx[0]], pos, f"adamw_{k}")
        grads[k], matrix_out[k] = res_k[0], res_k[1:]

    total = _sum_lead(every, "sum_vec_grads").reshape(-1)
    every = every.reshape(8, -1)
    grads["final_g"] = _unpack(total, layout, 0).reshape(final_g.shape)
    grads["norm_g"] = _unpack(total, layout, 1)
    grads["na_rpb"] = _unpack(total, layout, 3)[None]
    grads["pool_scale"] = chip_cols(_unpack(total, layout, 4), pool_scale.shape[-1])
    grads["conv_dw"] = chip_cols(_unpack(total, layout, 5), conv_dw.shape[-1])[None]
    grads["conv_db"] = chip_cols(_unpack(total, layout, 6), conv_db.shape[-1])
    dmod_sum = _unpack(total, layout, 2).reshape(4, 2, 3 * d)
    dmod_each = _unpack(every, layout, 2).reshape(8, 4, 2, 3 * d)
    grads["ada_b"] = dmod_sum[:, 0] + dmod_sum[:, 1]
    dm = jnp.concatenate([dmod_each[:, :, 0].transpose(1, 0, 2), dmod_sum[:, 1][:, None],
                          jnp.zeros((4, _COND_ROWS - 9, 3 * d), F32)], axis=1)
    dm_cols = chip_cols(dm, n_ada)
    grads["ada_w"] = _ada_w_grad(cond, dm_cols)
    dcond = _cond_grad(dm_cols, ada_w)[8].reshape(8, d // 8)
    dcond = _sum_lead(_all_gather(dcond, ("x", "y"), "gather_cond_grad"), "sum_cond_grad").reshape(d)
    grads["c_ctx"] = dcond * _dsilu(c_ctx)

    outs = [[], [], []]
    for k in _WEIGHTS:
        step = matrix_out[k] if k in matrix_out else _adamw(params[k], grads[k], mom1[k], mom2[k], f"adamw_{k}")
        for lst, val in zip(outs, step):
            lst.append(val)
    loss = _unpack(total, layout, 7)[0]
    return (loss, res["grad_x"][None], *[grads[k].reshape(params[k].shape) for k in _WEIGHTS],
            *outs[0], *outs[1], *outs[2])
```

```python
import functools

import numpy as np
import jax
import jax.numpy as jnp
from jax import lax
from jax.experimental import pallas as pl
from jax.experimental.pallas import tpu as pltpu

F32 = jnp.float32
BF16 = jnp.bfloat16

EPS = 1e-6
GRID_W = 64
HEAD_DIM = 64
WIN_ROWS = 8
WIN_COLS = 16
POOL_WINDOWS = (2, 4, 8, 16)
Q_ROWS = 4
K_ROWS = 12
PAD_ROWS = 4
NEG = -1e30

ADAM_LR = 0.001
ADAM_B1 = 0.9
ADAM_B2 = 0.999
ADAM_EPS = 1e-08
ADAM_WD = 0.01
ADAM_STEP = 10

ROW_BLOCK = 256
VMEM_LIMIT = 56 * 1024 * 1024
ACT = BF16

MESH = pl.DeviceIdType.MESH
HBM_SPEC = pl.BlockSpec(memory_space=pltpu.HBM)


def _cparams(*sem):
    return pltpu.CompilerParams(dimension_semantics=sem or None, vmem_limit_bytes=VMEM_LIMIT)


def _sds(shape, dtype):
    return jax.ShapeDtypeStruct(tuple(shape), dtype)


def _call(body, args, *, grid, in_specs, out_specs, out_shape, name, scratch_shapes=(), comm=None):
    sem = ("arbitrary",) * len(grid)
    if comm is None:
        res = pl.pallas_call(body, grid=grid, in_specs=list(in_specs), out_specs=list(out_specs), out_shape=list(out_shape),
                             scratch_shapes=list(scratch_shapes), name=name, compiler_params=_cparams(*sem))(*args)
        return list(res), []
    n_in, n_out, n_scr = len(in_specs), len(out_specs), len(scratch_shapes)
    n_cin, n_cout = len(comm.ins), len(comm.outs)

    def carrying(*refs):
        ins, cin = refs[:n_in], refs[n_in:n_in + n_cin]
        outs = refs[n_in + n_cin:n_in + n_cin + n_out]
        cout = refs[n_in + n_cin + n_out:n_in + n_cin + n_out + n_cout]
        rest = refs[n_in + n_cin + n_out + n_cout:]
        scr, sems = rest[:n_scr], rest[n_scr:]
        first, last = True, True
        for ax, size in enumerate(grid):
            first = jnp.logical_and(first, pl.program_id(ax) == 0)
            last = jnp.logical_and(last, pl.program_id(ax) == size - 1)
        pl.when(first)(lambda: comm.start(cin, cout, sems))
        body(*ins, *outs, *scr)
        pl.when(last)(lambda: comm.finish(cin, cout, sems))

    res = pl.pallas_call(
        carrying, grid=grid, in_specs=list(in_specs) + [HBM_SPEC] * n_cin, out_specs=list(out_specs) + [HBM_SPEC] * n_cout,
        out_shape=list(out_shape) + list(comm.outs), scratch_shapes=list(scratch_shapes) + list(comm.sems), name=name,
        compiler_params=_cparams(*sem),
    )(*args, *comm.ins)
    return list(res[:n_out]), list(res[n_out:])


def _sigmoid(x):
    return 1.0 / (1.0 + jnp.exp(-x))


def _silu(x):
    return x * _sigmoid(x)


def _dsilu(x):
    s = _sigmoid(x)
    return s * (1.0 + x * (1.0 - s))


_DIMS = {
    "nn": (((1,), (0,)), ((), ())),
    "nt": (((1,), (1,)), ((), ())),
    "tn": (((0,), (0,)), ((), ())),
}


def _matmul(a, b, *, mode, grid, a_spec, b_spec, out_shapes, out_specs, name, nk=1,
            a_silu=False, exact=False, epilogue=None, extra=(), extra_specs=(), acc_shape=None):
    n_extra = len(extra)
    n_out = len(out_shapes)

    def body(*refs):
        a_ref, b_ref = refs[:2]
        ex = refs[2:2 + n_extra]
        outs = refs[2 + n_extra:2 + n_extra + n_out]
        av = a_ref[...]
        bv = b_ref[...]
        if a_silu:
            av = _silu(av.astype(F32))
        if exact:
            prod = lax.dot_general(av.astype(F32), bv.astype(F32), _DIMS[mode],
                                   precision=lax.Precision.HIGHEST, preferred_element_type=F32)
        else:
            prod = lax.dot_general(av.astype(BF16), bv.astype(BF16), _DIMS[mode], preferred_element_type=F32)

        def finish(res):
            if epilogue is None:
                outs[0][...] = res.astype(outs[0].dtype)
            elif epilogue == "bias":
                outs[0][...] = (res + ex[0][...]).astype(outs[0].dtype)
            else:
                outs[0][...] = res.astype(outs[0].dtype)
                outs[1][...] = ex[0][...] + ex[1][...] * res

        if nk == 1:
            finish(prod)
        else:
            acc = refs[-1]
            k = pl.program_id(len(grid) - 1)

            @pl.when(k == 0)
            def _():
                acc[...] = prod

            @pl.when(k > 0)
            def _():
                acc[...] += prod

            @pl.when(k == nk - 1)
            def _():
                finish(acc[...])

    scratch = [pltpu.VMEM(acc_shape, F32)] if nk > 1 else []
    sem = ("parallel",) * (len(grid) - 1) + ("arbitrary",)
    return pl.pallas_call(
        body, grid=grid, in_specs=[a_spec, b_spec, *extra_specs], out_specs=list(out_specs),
        out_shape=list(out_shapes), scratch_shapes=scratch, name=name, compiler_params=_cparams(*sem),
    )(a, b, *extra)


def _row_tile(rows):
    for t in (768, 512, 256):
        if rows % t == 0:
            return t
    return rows


def _mm_nn(a, b, name, out_dtype=F32, tn=1024):
    m, k = a.shape
    n = b.shape[1]
    tm = _row_tile(m)
    tn = min(tn, n)
    return _matmul(
        a, b, mode="nn", grid=(m // tm, n // tn),
        a_spec=pl.BlockSpec((tm, k), lambda i, j: (i, 0)), b_spec=pl.BlockSpec((k, tn), lambda i, j: (0, j)),
        out_shapes=[_sds((m, n), out_dtype)], out_specs=[pl.BlockSpec((tm, tn), lambda i, j: (i, j))], name=name)[0]


def _mm_out_resid(a, w_out, xres, gate, nxb, name):
    m, k = a.shape
    n = w_out.shape[1]
    tm = ROW_BLOCK
    seg = lambda i, j: (jnp.where(i >= nxb, 1, 0), 0, 0)
    return _matmul(
        a, w_out, mode="nn", grid=(m // tm, 1),
        a_spec=pl.BlockSpec((tm, k), lambda i, j: (i, 0)), b_spec=pl.BlockSpec((k, n), lambda i, j: (0, 0)),
        extra=(xres, gate), extra_specs=(pl.BlockSpec((tm, n), lambda i, j: (i, 0)), pl.BlockSpec((None, 1, n), seg)),
        out_shapes=[_sds((m, n), ACT), _sds((m, n), F32)],
        out_specs=[pl.BlockSpec((tm, n), lambda i, j: (i, 0))] * 2, epilogue="resid", name=name)


def _mm_nt(a, b, name, out_dtype=F32):
    m, n = a.shape
    k = b.shape[0]
    tm = _row_tile(m)
    return _matmul(
        a, b, mode="nt", grid=(m // tm, 1),
        a_spec=pl.BlockSpec((tm, n), lambda i, j: (i, 0)), b_spec=pl.BlockSpec((k, n), lambda i, j: (0, 0)),
        out_shapes=[_sds((m, k), out_dtype)], out_specs=[pl.BlockSpec((tm, k), lambda i, j: (i, 0))], name=name)[0]


def _mm_nt_parts(a, b, name):
    p, m, kp = a.shape
    d = b.shape[0]
    tm = _row_tile(m)
    return _matmul(
        a, b, mode="nt", grid=(m // tm, p), nk=p, acc_shape=(tm, d),
        a_spec=pl.BlockSpec((None, tm, kp), lambda i, q: (q, i, 0)), b_spec=pl.BlockSpec((d, kp), lambda i, q: (0, q)),
        out_shapes=[_sds((m, d), F32)], out_specs=[pl.BlockSpec((tm, d), lambda i, q: (i, 0))], name=name)[0]


def _mm_tn(a, b, name, out_dtype, tm=512):
    r, m = a.shape
    n = b.shape[1]
    tm = min(tm, m)
    tn = min(1024, n)
    return _matmul(
        a, b, mode="tn", grid=(m // tm, n // tn),
        a_spec=pl.BlockSpec((r, tm), lambda i, j: (0, i)), b_spec=pl.BlockSpec((r, tn), lambda i, j: (0, j)),
        out_shapes=[_sds((m, n), out_dtype)], out_specs=[pl.BlockSpec((tm, tn), lambda i, j: (i, j))], name=name)[0]


def _mm_tn_parts(a, b, name, out_dtype, tm=512):
    r, m = a.shape
    p, _, np_ = b.shape
    tm = min(tm, m)
    return _matmul(
        a, b, mode="tn", grid=(m // tm, p),
        a_spec=pl.BlockSpec((r, tm), lambda i, q: (0, i)), b_spec=pl.BlockSpec((None, r, np_), lambda i, q: (q, 0, 0)),
        out_shapes=[_sds((m, p * np_), out_dtype)], out_specs=[pl.BlockSpec((tm, np_), lambda i, q: (i, q))],
        name=name)[0]


def _seg_map(nxb):
    return lambda i: (jnp.where(i >= nxb, 1, 0), 0, 0)


def _normmod_fwd(x, g, scale, shift, nxb, name):
    rows, d = x.shape
    tr = ROW_BLOCK

    def body(x_ref, g_ref, sc_ref, sh_ref, h_ref, r_ref):
        xv = x_ref[...]
        r = lax.rsqrt(jnp.mean(xv * xv, axis=-1, keepdims=True) + EPS)
        h = (xv * r) * g_ref[...] * (1.0 + sc_ref[...]) + sh_ref[...]
        h_ref[...] = h.astype(BF16)
        r_ref[...] = r

    row = pl.BlockSpec((tr, d), lambda i: (i, 0))
    vec = pl.BlockSpec((None, 1, d), _seg_map(nxb))
    return pl.pallas_call(
        body, grid=(rows // tr,), in_specs=[row, pl.BlockSpec((1, d), lambda i: (0, 0)), vec, vec],
        out_specs=[row, pl.BlockSpec((tr, 1), lambda i: (i, 0))],
        out_shape=[_sds((rows, d), BF16), _sds((rows, 1), F32)], name=name, compiler_params=_cparams("parallel"),
    )(x, g, scale, shift)


def _normmod_bwd(dh, x, r, g, scale, dres, nxb, name):
    rows, d = x.shape
    tr = ROW_BLOCK
    nres = dres.shape[0] // tr
    nseg = scale.shape[0]

    def body(dh_ref, x_ref, r_ref, g_ref, sc_ref, dres_ref, dx_ref, dsh_ref, dge_ref):
        i = pl.program_id(0)
        dhv = dh_ref[...]
        rv = r_ref[...]
        xn = x_ref[...] * rv
        dxn = dhv * (g_ref[...] * (1.0 + sc_ref[...]))
        dx = rv * (dxn - xn * jnp.mean(dxn * xn, axis=-1, keepdims=True))

        @pl.when(i < nres)
        def _():
            dx_ref[...] = dx + dres_ref[...]

        @pl.when(i >= nres)
        def _():
            dx_ref[...] = dx

        first = jnp.logical_or(i == 0, i == nxb)
        s_dh = jnp.sum(dhv, axis=0, keepdims=True)
        s_ge = jnp.sum(dhv * xn, axis=0, keepdims=True)

        @pl.when(first)
        def _():
            dsh_ref[...] = s_dh
            dge_ref[...] = s_ge

        @pl.when(jnp.logical_not(first))
        def _():
            dsh_ref[...] += s_dh
            dge_ref[...] += s_ge

    row = pl.BlockSpec((tr, d), lambda i: (i, 0))
    vec = pl.BlockSpec((None, 1, d), _seg_map(nxb))
    return pl.pallas_call(
        body, grid=(rows // tr,),
        in_specs=[row, row, pl.BlockSpec((tr, 1), lambda i: (i, 0)), pl.BlockSpec((1, d), lambda i: (0, 0)), vec,
                  pl.BlockSpec((tr, d), lambda i: (jnp.minimum(i, nres - 1), 0))],
        out_specs=[row, vec, vec],
        out_shape=[_sds((rows, d), F32), _sds((nseg, 1, d), F32), _sds((nseg, 1, d), F32)],
        name=name, compiler_params=_cparams("arbitrary"),
    )(dh, x, r, g, scale, dres)


def _gate_bwd(dxo, yx, gate, nxb, name):
    rows, d = yx.shape
    tr = ROW_BLOCK
    nseg = gate.shape[0]

    def body(dx_ref, yx_ref, gt_ref, dyx_ref, dg_ref):
        i = pl.program_id(0)
        dxv = dx_ref[...]
        dyx_ref[...] = (dxv * gt_ref[...]).astype(BF16)
        s = jnp.sum(dxv * yx_ref[...].astype(F32), axis=0, keepdims=True)
        first = jnp.logical_or(i == 0, i == nxb)

        @pl.when(first)
        def _():
            dg_ref[...] = s

        @pl.when(jnp.logical_not(first))
        def _():
            dg_ref[...] += s

    row = pl.BlockSpec((tr, d), lambda i: (i, 0))
    vec = pl.BlockSpec((None, 1, d), _seg_map(nxb))
    return pl.pallas_call(
        body, grid=(rows // tr,), in_specs=[row, row, vec], out_specs=[row, vec],
        out_shape=[_sds((rows, d), BF16), _sds((nseg, 1, d), F32)], name=name, compiler_params=_cparams("arbitrary"),
    )(dxo, yx, gate)


def _row_vec(ref, is_ctx):
    return ref[0] if is_ctx is None else jnp.where(is_ctx, ref[1], ref[0])


def _ctx_rows(i, tm, nx, nseg):
    if nseg == 1:
        return None
    return i * tm + lax.broadcasted_iota(jnp.int32, (tm, 1), 0) >= nx


def _seg_sums(ref, val, is_ctx, first):
    if is_ctx is None:
        parts = [jnp.sum(val, axis=0, keepdims=True)]
    else:
        parts = [jnp.sum(jnp.where(is_ctx, 0.0, val), axis=0, keepdims=True),
                 jnp.sum(jnp.where(is_ctx, val, 0.0), axis=0, keepdims=True)]

    @pl.when(first)
    def _():
        for k, p in enumerate(parts):
            ref[k] = p

    @pl.when(jnp.logical_not(first))
    def _():
        for k, p in enumerate(parts):
            ref[k] += p


def _w_out_resid(a, w_out, xres, gate, nx, name):
    m, k = a.shape
    n = w_out.shape[1]
    nseg = gate.shape[0]
    tm = _row_tile(m)

    def body(a_ref, w_ref, x_ref, gt_ref, yx_ref, xo_ref):
        yx = jnp.dot(a_ref[...], w_ref[...], preferred_element_type=F32)
        yx_ref[...] = yx.astype(ACT)
        xo_ref[...] = x_ref[...] + _row_vec(gt_ref, _ctx_rows(pl.program_id(0), tm, nx, nseg)) * yx

    row = pl.BlockSpec((tm, n), lambda i: (i, 0))
    return pl.pallas_call(
        body, grid=(m // tm,),
        in_specs=[pl.BlockSpec((tm, k), lambda i: (i, 0)), pl.BlockSpec((k, n), lambda i: (0, 0)), row,
                  pl.BlockSpec((nseg, 1, n), lambda i: (0, 0, 0))],
        out_specs=[row, row], out_shape=[_sds((m, n), ACT), _sds((m, n), F32)],
        name=name, compiler_params=_cparams("parallel"),
    )(a, w_out, xres, gate)


def _norm_w_in(x, g, scale, shift, w_in, nx, name, comm=None):
    rows, d = x.shape
    n = w_in.shape[1]
    nseg = scale.shape[0]
    tm = _row_tile(rows)
    tn = min(1024, n)

    def body(x_ref, g_ref, sc_ref, sh_ref, w_ref, h_ref, r_ref, p_ref):
        i, j = pl.program_id(0), pl.program_id(1)

        @pl.when(j == 0)
        def _():
            xv = x_ref[...]
            r = lax.rsqrt(jnp.mean(xv * xv, axis=-1, keepdims=True) + EPS)
            is_ctx = _ctx_rows(i, tm, nx, nseg)
            h = (xv * r) * g_ref[...] * (1.0 + _row_vec(sc_ref, is_ctx)) + _row_vec(sh_ref, is_ctx)
            h_ref[...] = h.astype(BF16)
            r_ref[...] = r

        p_ref[...] = jnp.dot(h_ref[...], w_ref[...], preferred_element_type=F32).astype(ACT)

    vec = pl.BlockSpec((nseg, 1, d), lambda i, j: (0, 0, 0))
    return _call(
        body, (x, g, scale, shift, w_in), grid=(rows // tm, n // tn),
        in_specs=[pl.BlockSpec((tm, d), lambda i, j: (i, 0)), pl.BlockSpec((1, d), lambda i, j: (0, 0)), vec, vec,
                  pl.BlockSpec((d, tn), lambda i, j: (0, j))],
        out_specs=[pl.BlockSpec((tm, d), lambda i, j: (i, 0)), pl.BlockSpec((tm, 1), lambda i, j: (i, 0)),
                   pl.BlockSpec((tm, tn), lambda i, j: (i, j))],
        out_shape=[_sds((rows, d), BF16), _sds((rows, 1), F32), _sds((rows, n), ACT)], name=name, comm=comm)


def _gate_w_out_bwd(dxo, yx, gate, w_out, nx, name, comm=None):
    rows, d = yx.shape
    w = w_out.shape[0]
    nseg = gate.shape[0]
    tm = _row_tile(rows)

    def body(dx_ref, yx_ref, gt_ref, w_ref, dyx_ref, da_ref, dg_ref):
        i = pl.program_id(0)
        is_ctx = _ctx_rows(i, tm, nx, nseg)
        dxv = dx_ref[...]
        dyx = (dxv * _row_vec(gt_ref, is_ctx)).astype(BF16)
        dyx_ref[...] = dyx
        da_ref[...] = lax.dot_general(dyx, w_ref[...], _DIMS["nt"], preferred_element_type=F32).astype(ACT)
        _seg_sums(dg_ref, dxv * yx_ref[...].astype(F32), is_ctx, i == 0)

    row = pl.BlockSpec((tm, d), lambda i: (i, 0))
    vec = pl.BlockSpec((nseg, 1, d), lambda i: (0, 0, 0))
    return _call(
        body, (dxo, yx, gate, w_out), grid=(rows // tm,),
        in_specs=[row, row, vec, pl.BlockSpec((w, d), lambda i: (0, 0))],
        out_specs=[row, pl.BlockSpec((tm, w), lambda i: (i, 0)), vec],
        out_shape=[_sds((rows, d), BF16), _sds((rows, w), ACT), _sds((nseg, 1, d), F32)], name=name, comm=comm)


def _w_in_bwd_norm(dparts, w_in, x, r, g, scale, dres, nx, name, comm=None):
    np_, rows, kp = dparts.shape
    d = w_in.shape[0]
    nseg = scale.shape[0]
    tm = _row_tile(rows)
    nsub = tm // ROW_BLOCK
    nres_blocks = dres.shape[0] // ROW_BLOCK

    def body(dp_ref, w_ref, x_ref, r_ref, g_ref, sc_ref, *rest):
        dres_refs = rest[:nsub]
        dx_ref, dsh_ref, dge_ref, acc = rest[nsub:]
        i, k = pl.program_id(0), pl.program_id(1)
        prod = lax.dot_general(dp_ref[...], w_ref[...], _DIMS["nt"], preferred_element_type=F32)

        @pl.when(k == 0)
        def _():
            acc[...] = prod

        @pl.when(k > 0)
        def _():
            acc[...] += prod

        @pl.when(k == np_ - 1)
        def _():
            is_ctx = _ctx_rows(i, tm, nx, nseg)
            dhv = acc[...]
            rv = r_ref[...]
            xn = x_ref[...] * rv
            dxn = dhv * (g_ref[...] * (1.0 + _row_vec(sc_ref, is_ctx)))
            dx = rv * (dxn - xn * jnp.mean(dxn * xn, axis=-1, keepdims=True))
            for s in range(nsub):
                piece = slice(s * ROW_BLOCK, (s + 1) * ROW_BLOCK)
                res = dres_refs[s][...]
                if nres_blocks * ROW_BLOCK < rows:
                    res = jnp.where(i * nsub + s < nres_blocks, res, 0.0)
                dx_ref[piece, :] = dx[piece, :] + res
            _seg_sums(dsh_ref, dhv, is_ctx, i == 0)
            _seg_sums(dge_ref, dhv * xn, is_ctx, i == 0)

    row = pl.BlockSpec((tm, d), lambda i, k: (i, 0))
    vec = pl.BlockSpec((nseg, 1, d), lambda i, k: (0, 0, 0))
    return _call(
        body, (dparts, w_in, x, r, g, scale, *([dres] * nsub)), grid=(rows // tm, np_),
        in_specs=[pl.BlockSpec((None, tm, kp), lambda i, k: (k, i, 0)), pl.BlockSpec((d, kp), lambda i, k: (0, k)),
                  row, pl.BlockSpec((tm, 1), lambda i, k: (i, 0)), pl.BlockSpec((1, d), lambda i, k: (0, 0)), vec]
        + [pl.BlockSpec((ROW_BLOCK, d), (lambda i, k, s=s: (jnp.minimum(i * nsub + s, nres_blocks - 1), 0)))
           for s in range(nsub)],
        out_specs=[row, vec, vec],
        out_shape=[_sds((rows, d), F32), _sds((nseg, 1, d), F32), _sds((nseg, 1, d), F32)],
        scratch_shapes=[pltpu.VMEM((tm, d), F32)], name=name, comm=comm)


_PAD_TOP = 16
_PAD_BOT = 32


def _window_sum(buf, xv, lo, n):
    t = xv.shape[0]
    c = xv.shape[1]
    tp = t + _PAD_TOP + _PAD_BOT
    buf[pl.ds(0, _PAD_TOP), :] = jnp.zeros((_PAD_TOP, c), F32)
    buf[pl.ds(_PAD_TOP, t), :] = xv
    buf[pl.ds(_PAD_TOP + t, _PAD_BOT), :] = jnp.zeros((_PAD_BOT, c), F32)
    p = buf[...]
    k = 1
    while k < n:
        p = p + pltpu.roll(p, tp - k, 0)
        k *= 2
    if lo:
        p = pltpu.roll(p, -lo, 0)
    buf[...] = p
    return buf[pl.ds(_PAD_TOP, t), :]


def _window_count(t, half):
    pos = lax.broadcasted_iota(jnp.int32, (t, 1), 0)
    return (jnp.minimum(pos + half, t) - jnp.maximum(pos - half, 0)).astype(F32)


def _segments(rows, nx):
    return [(0, nx)] + ([(nx, rows - nx)] if rows > nx else [])


def _pool_fwd(uv, nx, name):
    rows = uv.shape[0]
    w = uv.shape[1] // 2
    cb = 128
    per_group = w // len(POOL_WINDOWS) // cb
    segs = _segments(rows, nx)

    def body(u_ref, z_ref, *bufs):
        j = pl.program_id(0)
        for gi, win in enumerate(POOL_WINDOWS):
            half = win // 2

            @pl.when(jnp.logical_and(j >= gi * per_group, j < (gi + 1) * per_group))
            def _():
                for (start, length), buf in zip(segs, bufs):
                    uvv = u_ref[pl.ds(start, length), :].astype(F32)
                    s = _window_sum(buf, uvv, -half, win)
                    z_ref[pl.ds(start, length), :] = (s / _window_count(length, half) - uvv).astype(BF16)

    scratch = [pltpu.VMEM((length + _PAD_TOP + _PAD_BOT, cb), F32) for _, length in segs]
    return pl.pallas_call(
        body, grid=(w // cb,), in_specs=[pl.BlockSpec((rows, cb), lambda j: (0, j))],
        out_specs=pl.BlockSpec((rows, cb), lambda j: (0, j)), out_shape=_sds((rows, w), BF16),
        scratch_shapes=scratch, name=name, compiler_params=_cparams("parallel"),
    )(uv)


def _pool_bwd(dz, dgt, nx, name):
    rows, w = dz.shape
    cb = 128
    per_group = w // len(POOL_WINDOWS) // cb
    segs = _segments(rows, nx)

    def body(dz_ref, dgt_ref, o_ref, *bufs):
        j = pl.program_id(0)
        o_ref[1] = dgt_ref[...]
        for gi, win in enumerate(POOL_WINDOWS):
            half = win // 2

            @pl.when(jnp.logical_and(j >= gi * per_group, j < (gi + 1) * per_group))
            def _():
                for (start, length), buf in zip(segs, bufs):
                    dzv = dz_ref[pl.ds(start, length), :].astype(F32)
                    s = _window_sum(buf, dzv / _window_count(length, half), 1 - half, win)
                    o_ref[0, pl.ds(start, length), :] = (s - dzv).astype(BF16)

    scratch = [pltpu.VMEM((length + _PAD_TOP + _PAD_BOT, cb), F32) for _, length in segs]
    col = pl.BlockSpec((rows, cb), lambda j: (0, j))
    return pl.pallas_call(
        body, grid=(w // cb,), in_specs=[col, col], out_specs=pl.BlockSpec((2, rows, cb), lambda j: (0, 0, j)),
        out_shape=_sds((2, rows, w), BF16), scratch_shapes=scratch, name=name, compiler_params=_cparams("parallel"),
    )(dz, dgt)


def _grp_fwd(z, w_grp, uv, scale, name):
    rows, w = z.shape
    ng, gc, _ = w_grp.shape
    tm = _row_tile(rows)

    def body(z_ref, w_ref, gt_ref, sc_ref, mx_ref, a_ref):
        mixed = jnp.dot(z_ref[...], w_ref[...], preferred_element_type=F32)
        mx_ref[...] = mixed.astype(ACT)
        a_ref[...] = (mixed * sc_ref[...] * _silu(gt_ref[...].astype(F32))).astype(BF16)

    blk = pl.BlockSpec((tm, gc), lambda g, i: (i, g))
    return pl.pallas_call(
        body, grid=(ng, rows // tm),
        in_specs=[blk, pl.BlockSpec((None, gc, gc), lambda g, i: (g, 0, 0)),
                  pl.BlockSpec((tm, gc), lambda g, i: (i, ng + g)), pl.BlockSpec((1, gc), lambda g, i: (0, g))],
        out_specs=[blk, blk], out_shape=[_sds((rows, w), ACT), _sds((rows, w), BF16)],
        name=name, compiler_params=_cparams("parallel", "parallel"),
    )(z, w_grp, uv, scale)


def _grp_bwd(da, mixed, uv, scale, w_grp, name):
    rows, w = da.shape
    ng, gc, _ = w_grp.shape
    tm = _row_tile(rows)

    def body(da_ref, mx_ref, gt_ref, sc_ref, w_ref, dm_ref, dz_ref, dgt_ref, dsc_ref):
        i = pl.program_id(1)
        dav = da_ref[...].astype(F32)
        mixed = mx_ref[...].astype(F32)
        gt = gt_ref[...].astype(F32)
        sg = _silu(gt)
        sc = sc_ref[...]
        dm = (dav * sc * sg).astype(BF16)
        dm_ref[...] = dm
        dz_ref[...] = lax.dot_general(dm, w_ref[...], _DIMS["nt"], preferred_element_type=F32).astype(ACT)
        dgt_ref[...] = (dav * mixed * sc * _dsilu(gt)).astype(BF16)
        s = jnp.sum(dav * mixed * sg, axis=0, keepdims=True)

        @pl.when(i == 0)
        def _():
            dsc_ref[...] = s

        @pl.when(i > 0)
        def _():
            dsc_ref[...] += s

    blk = pl.BlockSpec((tm, gc), lambda g, i: (i, g))
    vec = pl.BlockSpec((1, gc), lambda g, i: (0, g))
    return pl.pallas_call(
        body, grid=(ng, rows // tm),
        in_specs=[blk, blk, pl.BlockSpec((tm, gc), lambda g, i: (i, ng + g)), vec,
                  pl.BlockSpec((None, gc, gc), lambda g, i: (g, 0, 0))],
        out_specs=[blk, blk, blk, vec],
        out_shape=[_sds((rows, w), BF16), _sds((rows, w), ACT), _sds((rows, w), BF16), _sds((1, w), F32)],
        name=name, compiler_params=_cparams("parallel", "arbitrary"),
    )(da, mixed, uv, scale, w_grp)


def _pool_scratch(rows, nx, cols):
    return [pltpu.VMEM((length + _PAD_TOP + _PAD_BOT, cols), F32) for _, length in _segments(rows, nx)]


def _per_group(g, fn):
    for gi, win in enumerate(POOL_WINDOWS):
        pl.when(g == gi)(functools.partial(fn, win))


def _pool_grp_fwd(uv, w_grp, scale, nx, name, comm=None):
    rows = uv.shape[0]
    ng, gc, _ = w_grp.shape
    w = ng * gc
    segs = _segments(rows, nx)

    def body(u_ref, gt_ref, w_ref, sc_ref, z_ref, mx_ref, a_ref, *bufs):
        def pool(win):
            half = win // 2
            for (start, length), buf in zip(segs, bufs):
                uvv = u_ref[pl.ds(start, length), :].astype(F32)
                s = _window_sum(buf, uvv, -half, win)
                z_ref[pl.ds(start, length), :] = (s / _window_count(length, half) - uvv).astype(BF16)

        _per_group(pl.program_id(0), pool)
        mixed = jnp.dot(z_ref[...], w_ref[...], preferred_element_type=F32)
        mx_ref[...] = mixed.astype(ACT)
        a_ref[...] = (mixed * sc_ref[...] * _silu(gt_ref[...].astype(F32))).astype(BF16)

    col = pl.BlockSpec((rows, gc), lambda g: (0, g))
    return _call(
        body, (uv, uv, w_grp, scale), grid=(ng,),
        in_specs=[col, pl.BlockSpec((rows, gc), lambda g: (0, ng + g)), pl.BlockSpec((None, gc, gc), lambda g: (g, 0, 0)),
                  pl.BlockSpec((1, gc), lambda g: (0, g))],
        out_specs=[col, col, col], out_shape=[_sds((rows, w), BF16), _sds((rows, w), ACT), _sds((rows, w), BF16)],
        scratch_shapes=_pool_scratch(rows, nx, gc), name=name, comm=comm)


def _pool_grp_bwd(da, mixed, uv, scale, w_grp, nx, name):
    rows, w = da.shape
    ng, gc, _ = w_grp.shape
    segs = _segments(rows, nx)

    def body(da_ref, mx_ref, gt_ref, sc_ref, w_ref, dm_ref, duv_ref, dsc_ref, dz_ref, *bufs):
        dav = da_ref[...].astype(F32)
        mixed = mx_ref[...].astype(F32)
        gt = gt_ref[...].astype(F32)
        sg = _silu(gt)
        sc = sc_ref[...]
        dm = (dav * sc * sg).astype(BF16)
        dm_ref[...] = dm
        dz_ref[...] = lax.dot_general(dm, w_ref[...], _DIMS["nt"], preferred_element_type=F32)
        duv_ref[1] = (dav * mixed * sc * _dsilu(gt)).astype(BF16)
        dsc_ref[...] = jnp.sum(dav * mixed * sg, axis=0, keepdims=True)

        def unpool(win):
            half = win // 2
            for (start, length), buf in zip(segs, bufs):
                dzv = dz_ref[pl.ds(start, length), :]
                s = _window_sum(buf, dzv / _window_count(length, half), 1 - half, win)
                duv_ref[0, pl.ds(start, length), :] = (s - dzv).astype(BF16)

        _per_group(pl.program_id(0), unpool)

    col = pl.BlockSpec((rows, gc), lambda g: (0, g))
    vec = pl.BlockSpec((1, gc), lambda g: (0, g))
    return pl.pallas_call(
        body, grid=(ng,),
        in_specs=[col, col, pl.BlockSpec((rows, gc), lambda g: (0, ng + g)), vec,
                  pl.BlockSpec((None, gc, gc), lambda g: (g, 0, 0))],
        out_specs=[col, pl.BlockSpec((2, rows, gc), lambda g: (0, 0, g)), vec],
        out_shape=[_sds((rows, w), BF16), _sds((2, rows, w), BF16), _sds((1, w), F32)],
        scratch_shapes=[pltpu.VMEM((rows, gc), F32)] + _pool_scratch(rows, nx, gc),
        name=name, compiler_params=_cparams("parallel"),
    )(da, mixed, uv, scale, w_grp)


def _grp_wgrad(z, dm, ng, name, out_dtype):
    rows, w = z.shape
    gc = w // ng

    def body(z_ref, dm_ref, o_ref):
        o_ref[...] = lax.dot_general(z_ref[...], dm_ref[...], _DIMS["tn"],
                                     preferred_element_type=F32).astype(o_ref.dtype)

    blk = pl.BlockSpec((rows, gc), lambda g: (0, g))
    return pl.pallas_call(
        body, grid=(ng,), in_specs=[blk, blk], out_specs=pl.BlockSpec((None, gc, gc), lambda g: (g, 0, 0)),
        out_shape=_sds((ng, gc, gc), out_dtype), name=name, compiler_params=_cparams("parallel"),
    )(z, dm)


def _shift_rows(v, by):
    t = v.shape[0]
    pos = lax.broadcasted_iota(jnp.int32, v.shape, 0)
    rolled = pltpu.roll(v, by % t, 0)
    keep = pos >= by if by > 0 else pos < t + by
    return jnp.where(keep, rolled, 0.0)


def _conv_specs(t, w, cb):
    return [pl.BlockSpec((t, cb), (lambda j, q=q: (0, q * (w // cb) + j))) for q in range(4)]


def _conv_fwd(p4, dw, db, name):
    t = p4.shape[0]
    w = p4.shape[1] // 4
    cb = 128

    def body(bg_ref, cg_ref, v_ref, g_ref, dw_ref, db_ref, a_ref):
        tv = cg_ref[...].astype(F32) * v_ref[...].astype(F32)
        conv = (dw_ref[0:1, :] * _shift_rows(tv, 1) + dw_ref[1:2, :] * tv + dw_ref[2:3, :] * _shift_rows(tv, -1)
                + db_ref[...])
        a_ref[...] = (bg_ref[...].astype(F32) * conv * _silu(g_ref[...].astype(F32))).astype(BF16)

    return pl.pallas_call(
        body, grid=(w // cb,),
        in_specs=_conv_specs(t, w, cb) + [pl.BlockSpec((3, cb), lambda j: (0, j)), pl.BlockSpec((1, cb), lambda j: (0, j))],
        out_specs=pl.BlockSpec((t, cb), lambda j: (0, j)), out_shape=_sds((t, w), BF16),
        name=name, compiler_params=_cparams("parallel"),
    )(p4, p4, p4, p4, dw, db)


def _conv_bwd(da, p4, dw, db, name):
    t, w = da.shape
    cb = 128

    def body(da_ref, bg_ref, cg_ref, v_ref, g_ref, dw_ref, db_ref, d4_ref, ddw_ref, ddb_ref):
        cg = cg_ref[...].astype(F32)
        vv = v_ref[...].astype(F32)
        bg = bg_ref[...].astype(F32)
        gv = g_ref[...].astype(F32)
        tv = cg * vv
        tm1 = _shift_rows(tv, 1)
        tp1 = _shift_rows(tv, -1)
        w0, w1, w2 = dw_ref[0:1, :], dw_ref[1:2, :], dw_ref[2:3, :]
        conv = w0 * tm1 + w1 * tv + w2 * tp1 + db_ref[...]
        y = bg * conv
        dav = da_ref[...].astype(F32)
        dy = dav * _silu(gv)
        d4_ref[3] = (dav * y * _dsilu(gv)).astype(BF16)
        d4_ref[0] = (dy * conv).astype(BF16)
        dconv = dy * bg
        ddb_ref[...] = jnp.sum(dconv, axis=0, keepdims=True)
        ddw_ref[0:1, :] = jnp.sum(dconv * tm1, axis=0, keepdims=True)
        ddw_ref[1:2, :] = jnp.sum(dconv * tv, axis=0, keepdims=True)
        ddw_ref[2:3, :] = jnp.sum(dconv * tp1, axis=0, keepdims=True)
        dt = w0 * _shift_rows(dconv, -1) + w1 * dconv + w2 * _shift_rows(dconv, 1)
        d4_ref[1] = (dt * vv).astype(BF16)
        d4_ref[2] = (dt * cg).astype(BF16)

    col = pl.BlockSpec((t, cb), lambda j: (0, j))
    tap = pl.BlockSpec((3, cb), lambda j: (0, j))
    bias = pl.BlockSpec((1, cb), lambda j: (0, j))
    return pl.pallas_call(
        body, grid=(w // cb,), in_specs=[col] + _conv_specs(t, w, cb) + [tap, bias],
        out_specs=[pl.BlockSpec((4, t, cb), lambda j: (0, 0, j)), tap, bias],
        out_shape=[_sds((4, t, w), BF16), _sds((3, w), F32), _sds((1, w), F32)],
        name=name, compiler_params=_cparams("parallel"),
    )(da, p4, p4, p4, p4, dw, db)


def _attn_mask():
    qn, kn = Q_ROWS * GRID_W, K_ROWS * GRID_W
    qr, qc = np.divmod(np.arange(qn), GRID_W)
    kr, kc = np.divmod(np.arange(kn), GRID_W)
    col0 = np.clip(qc - WIN_COLS // 2, 0, GRID_W - WIN_COLS)
    col_ok = (kc[None, :] >= col0[:, None]) & (kc[None, :] < col0[:, None] + WIN_COLS)
    first = np.zeros(qn, np.int64)
    last = np.full(qn, K_ROWS - WIN_ROWS)
    out = []
    for row0 in (first, qr, last):
        row_ok = (kr[None, :] >= row0[:, None]) & (kr[None, :] < row0[:, None] + WIN_ROWS)
        out.append(np.where(row_ok & col_ok, 0.0, NEG))
    return jnp.asarray(np.stack(out), F32)


_KW = K_ROWS * GRID_W
_QB = Q_ROWS * GRID_W
_PAIR = 2 * HEAD_DIM
_N_DR = 2 * WIN_ROWS - 1
_N_DC = 2 * WIN_COLS - 1
_RP_ROWS = 24
_N_TILES = _N_DR + 1
_BIAS_BASE = (WIN_ROWS - 1, WIN_ROWS // 2 - 1, -1)


class _Comm:
    def __init__(self, ins, outs, sems, start, finish):
        self.ins, self.outs, self.sems, self.start, self.finish = list(ins), list(outs), list(sems), start, finish


def _bias_pieces(cls):
    out = []
    for qr in range(Q_ROWS):
        for kr in range(0, K_ROWS, 2):
            tile = _BIAS_BASE[cls] - qr + kr + 1
            out.append((qr, kr, tile if 0 <= tile < _N_TILES else None))
    return out


def _toeplitz_pair(left_row, right_row):
    lane = lax.broadcasted_iota(jnp.int32, (GRID_W, _PAIR), 1)
    shape = (GRID_W, _PAIR)
    left = pltpu.roll(jnp.broadcast_to(left_row, shape), _PAIR - (WIN_COLS - 1), 1, stride=1, stride_axis=0)
    right = pltpu.roll(jnp.broadcast_to(right_row, shape), GRID_W - (WIN_COLS - 1), 1, stride=1, stride_axis=0)
    return jnp.where(lane < GRID_W, left, right)


def _build_tiles(tiles_ref, rp_ref):
    for h in range(2):
        for t in range(_N_TILES):
            tiles_ref[h, t] = _toeplitz_pair(rp_ref[h, t:t + 1, :], rp_ref[h, t + 1:t + 2, :])


def _block_class(b, nblk, fn, entering=False):
    interior = (b == 1) if entering else jnp.logical_and(b > 0, b < nblk - 1)
    for cls, cond in enumerate((b == 0, interior, b == nblk - 1)):
        pl.when(cond)(functools.partial(fn, cls))


def _attn_geometry(p4, nx):
    rows = p4.shape[0]
    w = p4.shape[1] // 4
    nhp = w // _PAIR
    nblk = nx // _QB
    qspec = lambda col: pl.BlockSpec((_QB, _PAIR), lambda hp, b: (b, col * nhp + hp))
    kspec = lambda col: pl.BlockSpec((rows, _PAIR), lambda hp, b: (0, col * nhp + hp))
    tspec = pl.BlockSpec((2, _RP_ROWS, _PAIR), lambda hp, b: (hp, 0, 0))
    mspec = pl.BlockSpec((None, _QB, _KW), lambda hp, b: (jnp.where(b == 0, 0, jnp.where(b == nblk - 1, 2, 1)), 0, 0))
    lspec = pl.BlockSpec((None, _QB, 2), lambda hp, b: (hp, b, 0))
    ospec = pl.BlockSpec((_QB, _PAIR), lambda hp, b: (b, hp))
    return rows, w, nhp, nblk, qspec, kspec, tspec, mspec, lspec, ospec


def _window_start(b, nx):
    return pl.multiple_of(jnp.clip(b * _QB - PAD_ROWS * GRID_W, 0, nx - _KW), _QB)


def _load_bias(bias_ref, tiles_ref, rp_ref, m_ref, b, nblk):
    pl.when(b == 0)(lambda: _build_tiles(tiles_ref, rp_ref))

    def fill(cls):
        for h in range(2):
            for qr, kr, tile in _bias_pieces(cls):
                rows = slice(qr * GRID_W, (qr + 1) * GRID_W)
                cols = slice(kr * GRID_W, (kr + 2) * GRID_W)
                m = m_ref[rows, cols]
                bias_ref[h, rows, cols] = m if tile is None else tiles_ref[h, tile] + m

    _block_class(b, nblk, fill, entering=True)


def _attn_fwd(p4, rp, mask, nx, name, comm=None):
    rows, w, nhp, nblk, qspec, kspec, tspec, mspec, lspec, ospec = _attn_geometry(p4, nx)
    n_ctx = rows - nx
    n_cin, n_cout = (len(comm.ins), len(comm.outs)) if comm else (0, 0)

    def body(*refs):
        q_ref, k_ref, v_ref, g_ref, rp_ref, m_ref = refs[:6]
        cin = refs[6:6 + n_cin]
        a_ref, o_ref, lse_ref = refs[6 + n_cin:9 + n_cin]
        cout = refs[9 + n_cin:9 + n_cin + n_cout]
        bias_ref, tiles_ref = refs[9 + n_cin + n_cout:11 + n_cin + n_cout]
        sems = refs[11 + n_cin + n_cout:]
        hp, b = pl.program_id(0), pl.program_id(1)
        if comm:
            pl.when(jnp.logical_and(hp == 0, b == 0))(lambda: comm.start(cin, cout, sems))
        start = _window_start(b, nx)
        _load_bias(bias_ref, tiles_ref, rp_ref, m_ref, b, nblk)
        qf = q_ref[...].astype(F32) * HEAD_DIM ** -0.5
        kw = k_ref[pl.ds(start, _KW), :].astype(BF16)
        vw = v_ref[pl.ds(start, _KW), :].astype(BF16)
        kcv = k_ref[pl.ds(nx, n_ctx), :].astype(BF16)
        vcv = v_ref[pl.ds(nx, n_ctx), :].astype(BF16)
        lane = lax.broadcasted_iota(jnp.int32, (1, _PAIR), 1)
        outs, lses = [], []
        for h in range(2):
            mine = (lane >= HEAD_DIM) if h else (lane < HEAD_DIM)
            qm = jnp.where(mine, qf, 0.0).astype(BF16)
            s_loc = lax.dot_general(qm, kw, _DIMS["nt"], preferred_element_type=F32) + bias_ref[h]
            s_ctx = lax.dot_general(qm, kcv, _DIMS["nt"], preferred_element_type=F32)
            mx = jnp.maximum(jnp.max(s_loc, axis=-1, keepdims=True), jnp.max(s_ctx, axis=-1, keepdims=True))
            p_loc = jnp.exp(s_loc - mx)
            p_ctx = jnp.exp(s_ctx - mx)
            den = jnp.sum(p_loc, axis=-1, keepdims=True) + jnp.sum(p_ctx, axis=-1, keepdims=True)
            o = jnp.dot(p_loc.astype(BF16), vw, preferred_element_type=F32)
            o = o + jnp.dot(p_ctx.astype(BF16), vcv, preferred_element_type=F32)
            outs.append(o * (1.0 / den))
            lses.append(mx + jnp.log(den))
        o = jnp.where(lane < HEAD_DIM, outs[0], outs[1])
        o_ref[...] = o.astype(ACT)
        a_ref[...] = (o * _silu(g_ref[...].astype(F32))).astype(BF16)
        col = lax.broadcasted_iota(jnp.int32, (1, 2), 1)
        lse_ref[...] = jnp.where(col == 0, lses[0], lses[1])
        if comm:
            pl.when(jnp.logical_and(hp == nhp - 1, b == nblk - 1))(lambda: comm.finish(cin, cout, sems))

    res = pl.pallas_call(
        body, grid=(nhp, nblk),
        in_specs=[qspec(0), kspec(1), kspec(2), qspec(3), tspec, mspec] + [HBM_SPEC] * n_cin,
        out_specs=[ospec, ospec, lspec] + [HBM_SPEC] * n_cout,
        out_shape=[_sds((nx, w), BF16), _sds((nx, w), ACT), _sds((nhp, nx, 2), F32)] + (comm.outs if comm else []),
        scratch_shapes=[pltpu.VMEM((2, _QB, _KW), F32), pltpu.VMEM((2, _N_TILES, GRID_W, _PAIR), F32)]
        + (comm.sems if comm else []),
        name=name, compiler_params=_cparams("arbitrary", "arbitrary"),
    )(p4, p4, p4, p4, rp, mask, *(comm.ins if comm else []))
    return res[:3], res[3:]


def _fold_tiles(dtiles_ref, drp_ref):
    shape = (GRID_W, _PAIR)
    lane = lax.broadcasted_iota(jnp.int32, shape, 1)
    flip = (lax.broadcasted_iota(jnp.int32, (_PAIR, _PAIR), 0)
            + lax.broadcasted_iota(jnp.int32, (_PAIR, _PAIR), 1) == _PAIR - 1).astype(F32)
    drp_ref[...] = jnp.zeros(drp_ref.shape, F32)
    for h in range(2):
        stack = dtiles_ref[h].reshape(_N_TILES * GRID_W, _PAIR)
        rev = jnp.dot(stack, flip, precision=lax.Precision.HIGHEST, preferred_element_type=F32)
        for t in range(_N_TILES):
            tile = rev[t * GRID_W:(t + 1) * GRID_W, :]
            for side in (0, 1):
                shift = _PAIR - GRID_W * side - (WIN_COLS - 1)
                half = jnp.where((lane < GRID_W) if side else (lane >= GRID_W), tile, 0.0)
                diag = pltpu.roll(half, shift, 1, stride=1, stride_axis=0)
                drp_ref[h, t + side:t + side + 1, :] += jnp.sum(diag, axis=0, keepdims=True)


def _attn_bwd(p4, rp, mask, o, lse, da, nx, name, comm=None):
    rows, w, nhp, nblk, qspec, kspec, tspec, mspec, lspec, ospec = _attn_geometry(p4, nx)
    n_ctx = rows - nx
    n_cin, n_cout = (len(comm.ins), len(comm.outs)) if comm else (0, 0)

    def body(*refs):
        q_ref, k_ref, v_ref, g_ref, rp_ref, m_ref, o_ref, lse_ref, da_ref = refs[:9]
        cin = refs[9:9 + n_cin]
        d4_ref, drp_ref = refs[9 + n_cin:11 + n_cin]
        cout = refs[11 + n_cin:11 + n_cin + n_cout]
        bias_ref, tiles_ref, ds_ref, dtiles_ref, dk_ref, dv_ref = refs[11 + n_cin + n_cout:17 + n_cin + n_cout]
        sems = refs[17 + n_cin + n_cout:]
        hp, b = pl.program_id(0), pl.program_id(1)
        if comm:
            pl.when(jnp.logical_and(hp == 0, b == 0))(lambda: comm.start(cin, cout, sems))
        start = _window_start(b, nx)
        here = pl.multiple_of(b * _QB, _QB)

        @pl.when(b == 0)
        def _():
            dk_ref[...] = jnp.zeros(dk_ref.shape, F32)
            dv_ref[...] = jnp.zeros(dv_ref.shape, F32)
            dtiles_ref[...] = jnp.zeros(dtiles_ref.shape, F32)
            d4_ref[0, pl.ds(nx, n_ctx), :] = jnp.zeros((n_ctx, _PAIR), BF16)
            d4_ref[3, pl.ds(nx, n_ctx), :] = jnp.zeros((n_ctx, _PAIR), BF16)

        _load_bias(bias_ref, tiles_ref, rp_ref, m_ref, b, nblk)
        gv = g_ref[...].astype(F32)
        dav = da_ref[...].astype(F32)
        ov = o_ref[...].astype(F32)
        dov = dav * _silu(gv)
        d4_ref[3, pl.ds(here, _QB), :] = (dav * ov * _dsilu(gv)).astype(BF16)
        qf = q_ref[...].astype(F32) * HEAD_DIM ** -0.5
        kw = k_ref[pl.ds(start, _KW), :].astype(BF16)
        vw = v_ref[pl.ds(start, _KW), :].astype(BF16)
        kcv = k_ref[pl.ds(nx, n_ctx), :].astype(BF16)
        vcv = v_ref[pl.ds(nx, n_ctx), :].astype(BF16)
        lane = lax.broadcasted_iota(jnp.int32, (1, _PAIR), 1)
        dq = jnp.zeros((_QB, _PAIR), F32)
        for h in range(2):
            mine = (lane >= HEAD_DIM) if h else (lane < HEAD_DIM)
            qm = jnp.where(mine, qf, 0.0).astype(BF16)
            dom = jnp.where(mine, dov, 0.0)
            dob = dom.astype(BF16)
            lse = lse_ref[:, h:h + 1]
            s_loc = lax.dot_general(qm, kw, _DIMS["nt"], preferred_element_type=F32)
            p_loc = jnp.exp(s_loc + bias_ref[h] - lse)
            p_ctx = jnp.exp(lax.dot_general(qm, kcv, _DIMS["nt"], preferred_element_type=F32) - lse)
            delta = jnp.sum(dom * ov, axis=-1, keepdims=True)
            ds_loc = p_loc * (lax.dot_general(dob, vw, _DIMS["nt"], preferred_element_type=F32) - delta)
            ds_ctx = p_ctx * (lax.dot_general(dob, vcv, _DIMS["nt"], preferred_element_type=F32) - delta)
            dsb_loc = ds_loc.astype(BF16)
            dsb_ctx = ds_ctx.astype(BF16)
            dq_h = (jnp.dot(dsb_loc, kw, preferred_element_type=F32)
                    + jnp.dot(dsb_ctx, kcv, preferred_element_type=F32))
            dq = dq + jnp.where(mine, dq_h, 0.0)
            dk_ref[pl.ds(start, _KW), :] += lax.dot_general(dsb_loc, qm, _DIMS["tn"], preferred_element_type=F32)
            dv_ref[pl.ds(start, _KW), :] += lax.dot_general(p_loc.astype(BF16), dob, _DIMS["tn"],
                                                            preferred_element_type=F32)
            dk_ref[pl.ds(nx, n_ctx), :] += lax.dot_general(dsb_ctx, qm, _DIMS["tn"], preferred_element_type=F32)
            dv_ref[pl.ds(nx, n_ctx), :] += lax.dot_general(p_ctx.astype(BF16), dob, _DIMS["tn"],
                                                           preferred_element_type=F32)
            ds_ref[h] = ds_loc
        d4_ref[0, pl.ds(here, _QB), :] = (dq * HEAD_DIM ** -0.5).astype(BF16)

        def scatter(cls):
            for h in range(2):
                for qr, kr, tile in _bias_pieces(cls):
                    if tile is not None:
                        dtiles_ref[h, tile] += ds_ref[h, qr * GRID_W:(qr + 1) * GRID_W, kr * GRID_W:(kr + 2) * GRID_W]

        _block_class(b, nblk, scatter)

        @pl.when(b == nblk - 1)
        def _():
            d4_ref[1] = dk_ref[...].astype(BF16)
            d4_ref[2] = dv_ref[...].astype(BF16)
            _fold_tiles(dtiles_ref, drp_ref)

        if comm:
            pl.when(jnp.logical_and(hp == nhp - 1, b == nblk - 1))(lambda: comm.finish(cin, cout, sems))

    tiles = pltpu.VMEM((2, _N_TILES, GRID_W, _PAIR), F32)
    block = pltpu.VMEM((2, _QB, _KW), F32)
    res = pl.pallas_call(
        body, grid=(nhp, nblk),
        in_specs=[qspec(0), kspec(1), kspec(2), qspec(3), tspec, mspec, ospec, lspec, ospec] + [HBM_SPEC] * n_cin,
        out_specs=[pl.BlockSpec((4, rows, _PAIR), lambda hp, b: (0, 0, hp)), tspec] + [HBM_SPEC] * n_cout,
        out_shape=[_sds((4, rows, w), BF16), _sds(rp.shape, F32)] + (comm.outs if comm else []),
        scratch_shapes=[block, tiles, block, tiles, pltpu.VMEM((rows, _PAIR), F32), pltpu.VMEM((rows, _PAIR), F32)]
        + (comm.sems if comm else []),
        name=name, compiler_params=_cparams("arbitrary", "arbitrary"),
    )(p4, p4, p4, p4, rp, mask, o, lse, da, *(comm.ins if comm else []))
    return res[:2], res[2:]


def _final(x, g, target, name):
    rows, d = x.shape
    tr = ROW_BLOCK
    nblk = rows // tr

    def body(x_ref, g_ref, t_ref, loss_ref, dx_ref, dg_ref, acc_ref):
        i = pl.program_id(0)
        xv = x_ref[...]
        gv = g_ref[...]
        r = lax.rsqrt(jnp.mean(xv * xv, axis=-1, keepdims=True) + EPS)
        xn = xv * r
        err = xn * gv - t_ref[...]
        dy = err * (1.0 / d)
        dxn = dy * gv
        dx_ref[...] = r * (dxn - xn * jnp.mean(dxn * xn, axis=-1, keepdims=True))
        s_g = jnp.sum(dy * xn, axis=0, keepdims=True)
        s_l = jnp.sum(jnp.mean(err * err, axis=-1, keepdims=True), axis=0, keepdims=True)

        @pl.when(i == 0)
        def _():
            dg_ref[...] = s_g
            acc_ref[...] = s_l

        @pl.when(i > 0)
        def _():
            dg_ref[...] += s_g
            acc_ref[...] += s_l

        @pl.when(i == nblk - 1)
        def _():
            loss_ref[...] = jnp.broadcast_to(0.5 * acc_ref[...], loss_ref.shape)

    row = pl.BlockSpec((tr, d), lambda i: (i, 0))
    vec = pl.BlockSpec((1, d), lambda i: (0, 0))
    return pl.pallas_call(
        body, grid=(nblk,), in_specs=[row, vec, row],
        out_specs=[pl.BlockSpec((1, 128), lambda i: (0, 0)), row, vec],
        out_shape=[_sds((1, 128), F32), _sds((rows, d), F32), _sds((1, d), F32)],
        scratch_shapes=[pltpu.VMEM((1, 1), F32)], name=name, compiler_params=_cparams("arbitrary"),
    )(x, g, target)


def _as2d(a):
    if a.ndim == 1:
        return a.reshape(-1, 128) if a.shape[0] % 128 == 0 else a.reshape(1, -1)
    return a.reshape(-1, a.shape[-1])


def _adamw(w, g, m, v, name, comm=None):
    shape = w.shape
    w2, g2, m2, v2 = (_as2d(t) for t in (w, g.reshape(shape), m, v))
    rows, cols = w2.shape
    tr = 512 if rows % 512 == 0 else rows
    c1 = 1.0 - ADAM_B1 ** ADAM_STEP
    c2 = 1.0 - ADAM_B2 ** ADAM_STEP

    def body(w_ref, g_ref, m_ref, v_ref, d_ref, nm_ref, nv_ref):
        gv = g_ref[...]
        nm = ADAM_B1 * m_ref[...] + (1.0 - ADAM_B1) * gv
        nv = ADAM_B2 * v_ref[...] + (1.0 - ADAM_B2) * (gv * gv)
        nm_ref[...] = nm
        nv_ref[...] = nv
        d_ref[...] = -ADAM_LR * ((nm / c1) / (jnp.sqrt(nv / c2) + ADAM_EPS) + ADAM_WD * w_ref[...])

    blk = pl.BlockSpec((tr, cols), lambda i: (i, 0))
    outs, carried = _call(body, (w2, g2, m2, v2), grid=(rows // tr,), in_specs=[blk] * 4, out_specs=[blk] * 3,
                          out_shape=[_sds((rows, cols), F32)] * 3, name=name, comm=comm)
    outs = tuple(t.reshape(shape) for t in outs)
    return outs if comm is None else (outs, carried)


def _sum_lead(x, name, out_dtype=F32):
    n, rows, cols = x.shape
    tr = 512 if rows % 512 == 0 else rows

    def body(x_ref, o_ref):
        acc = x_ref[0].astype(F32)
        for k in range(1, n):
            acc = acc + x_ref[k].astype(F32)
        o_ref[...] = acc.astype(out_dtype)

    return pl.pallas_call(
        body, grid=(rows // tr,), in_specs=[pl.BlockSpec((n, tr, cols), lambda i: (0, i, 0))],
        out_specs=pl.BlockSpec((tr, cols), lambda i: (i, 0)), out_shape=_sds((rows, cols), out_dtype),
        name=name, compiler_params=_cparams("parallel"),
    )(x)


_NO_CTX = 1 << 30


def _seg_vecs(mod_l, which, nseg):
    return mod_l[:nseg, which][:, None, :]


def _norm_grads(dshift, dgeff, dgate, g, scale):
    nseg, _, d = dshift.shape
    dmod = jnp.stack([dshift[:, 0], dgeff[:, 0] * g, dgate[:, 0]], axis=1)
    if nseg == 1:
        dmod = jnp.concatenate([dmod, jnp.zeros((1, 3, d), F32)], axis=0)
    dg = jnp.sum(dgeff[:, 0] * (1.0 + scale[:, 0]), axis=0)
    return dmod, dg


def _pool_layer(xin, g, mod_l, w_in, w_grp, w_out, pscale, nx, tag, comms=None):
    rows = xin.shape[0]
    nseg = 2 if rows > nx else 1
    comms = comms or {}
    shift, scale, gate = (_seg_vecs(mod_l, k, nseg) for k in range(3))
    (h, r, uv), c_in = _norm_w_in(xin, g, scale, shift, w_in, nx, f"w_in_fwd_{tag}", comms.get("w_in_fwd"))
    (z, mixed, a), c_pool = _pool_grp_fwd(uv, w_grp, pscale, nx, f"pool_fwd_{tag}", comms.get("pool_fwd"))
    yx, xout = _w_out_resid(a, w_out, xin, gate, nx, f"w_out_fwd_{tag}")

    def backward(dxo, comms=None):
        comms = comms or {}
        (dyx, da, dgate), c_out = _gate_w_out_bwd(dxo, yx, gate, w_out, nx, f"w_out_bwd_{tag}", comms.get("w_out_bwd"))
        gw_out = _mm_tn(a, dyx, f"w_out_grad_{tag}", BF16)
        dm, duv, dscale = _pool_grp_bwd(da, mixed, uv, pscale, w_grp, nx, f"pool_bwd_{tag}")
        gw_grp = _grp_wgrad(z, dm, w_grp.shape[0], f"grp_grad_{tag}", BF16)
        gw_in = _mm_tn_parts(h, duv, f"w_in_grad_{tag}", BF16)
        (dx, dshift, dgeff), c_bwd = _w_in_bwd_norm(duv, w_in, xin, r, g, scale, dxo, nx, f"w_in_bwd_{tag}",
                                                    comms.get("w_in_bwd"))
        dmod, dg = _norm_grads(dshift, dgeff, dgate, g[0], scale)
        return (dx, dmod, dg, dict(w_in=gw_in, w_grp=gw_grp, w_out=gw_out, scale=dscale),
                dict(w_out_bwd=c_out, w_in_bwd=c_bwd))

    return xout, backward, dict(w_in_fwd=c_in, pool_fwd=c_pool)


def _na_layer(xc, g, mod_l, w_in, rpb, w_out, nx, mask, comm=None):
    nh, n_dr, n_dc = rpb.shape
    shift, scale = _seg_vecs(mod_l, 0, 2), _seg_vecs(mod_l, 1, 2)
    gate = _seg_vecs(mod_l, 2, 1)
    (h, r, p4), _ = _norm_w_in(xc, g, scale, shift, w_in, nx, "w_in_fwd_na")
    rp = jnp.pad(rpb, ((0, 0), (1, _RP_ROWS - 1 - n_dr), (0, _PAIR - n_dc)))
    (a, o, lse), carried = _attn_fwd(p4, rp, mask, nx, "attn_fwd", comm)
    yx, xout = _w_out_resid(a, w_out, xc, gate, nx, "w_out_fwd_na")

    def backward(dxo, comm=None):
        (dyx, da, dgate), _ = _gate_w_out_bwd(dxo, yx, gate, w_out, nx, "w_out_bwd_na")
        gw_out = _mm_tn(a, dyx, "w_out_grad_na", BF16)
        (d4, drp), carried_bwd = _attn_bwd(p4, rp, mask, o, lse, da, nx, "attn_bwd", comm)
        gw_in = _mm_tn_parts(h, d4, "w_in_grad_na", BF16)
        (dx, dshift, dgeff), _ = _w_in_bwd_norm(d4, w_in, xc, r, g, scale, dxo, nx, "w_in_bwd_na")
        dgate2 = jnp.concatenate([dgate, jnp.zeros_like(dgate)], axis=0)
        dmod, dg = _norm_grads(dshift, dgeff, dgate2, g[0], scale)
        drpb = drp[:, 1:1 + n_dr, ::-1][:, :, :n_dc]
        return dx, dmod, dg, dict(w_in=gw_in, w_out=gw_out, rpb=drpb), carried_bwd

    return xout, backward, carried


def _conv_layer(xin, g, mod_l, w_in, dw, db, w_out):
    shift, scale, gate = (_seg_vecs(mod_l, k, 1) for k in range(3))
    nx = xin.shape[0]
    (h, r, p4), _ = _norm_w_in(xin, g, scale, shift, w_in, nx, "w_in_fwd_conv")
    a = _conv_fwd(p4, dw, db, "conv_fwd")
    yx, xout = _w_out_resid(a, w_out, xin, gate, nx, "w_out_fwd_conv")

    def backward(dxo):
        (dyx, da, dgate), _ = _gate_w_out_bwd(dxo, yx, gate, w_out, nx, "w_out_bwd_conv")
        gw_out = _mm_tn(a, dyx, "w_out_grad_conv", BF16)
        d4, ddw, ddb = _conv_bwd(da, p4, dw, db, "conv_bwd")
        gw_in = _mm_tn_parts(h, d4, "w_in_grad_conv", BF16)
        (dx, dshift, dgeff), _ = _w_in_bwd_norm(d4, w_in, xin, r, g, scale, dxo, nx, "w_in_bwd_conv")
        dmod, dg = _norm_grads(dshift, dgeff, dgate, g[0], scale)
        return dx, dmod, dg, dict(w_in=gw_in, w_out=gw_out, dw=ddw, db=ddb)

    return xout, backward


def _example_step(x, ctx, target, mod, norm_g, final_g, wts, na_comms=None, na_weights=None, late_comm=None,
                  late_weights=None, grad_comm=None, na_grad_comms=None):
    nx = x.shape[0]
    consts = _attn_mask()
    g_rows = [norm_g[i:i + 1] for i in range(4)]
    xc0 = jnp.concatenate([x, ctx], axis=0)
    xc1, bwd0, carried0 = _pool_layer(xc0, g_rows[0], mod[0], wts["pool_w_in"][0], wts["pool_w_grp"][0],
                                      wts["pool_w_out"][0], wts["pool_scale"][0:1], nx, "p0", na_comms)
    if na_weights is not None:
        wts = {**wts, **na_weights(carried0)}
    x2, bwd1, carried = _na_layer(xc1, g_rows[1], mod[1], wts["na_w_in"], wts["na_rpb"], wts["na_w_out"], nx, consts,
                                  late_comm)
    if late_weights is not None:
        wts = {**wts, **late_weights(carried)}
    x3, bwd2 = _conv_layer(x2, g_rows[2], mod[2], wts["conv_w_in"], wts["conv_dw"], wts["conv_db"], wts["conv_w_out"])
    x4, bwd3, _ = _pool_layer(x3, g_rows[3], mod[3], wts["pool_w_in"][1], wts["pool_w_grp"][1], wts["pool_w_out"][1],
                              wts["pool_scale"][1:2], nx, "p3")
    loss, dx4, dfinal_g = _final(x4, final_g, target, "loss_head")
    dx3, dmod3, dg3, gr3, _ = bwd3(dx4)
    dx2, dmod2, dg2, gr2 = bwd2(dx3)
    dxc1, dmod1, dg1, gr1, carried_bwd = bwd1(dx2, grad_comm(gr3, gr2) if grad_comm else None)
    dxc0, dmod0, dg0, gr0, carried_bwd0 = bwd0(dxc1, na_grad_comms(gr1) if na_grad_comms else None)
    return dict(
        loss=loss, grad_x=dxc0[:nx], dmod=jnp.stack([dmod0, dmod1, dmod2, dmod3]),
        dnorm_g=jnp.stack([dg0, dg1, dg2, dg3]), dfinal_g=dfinal_g, layers=(gr0, gr1, gr2, gr3), carried=carried_bwd,
        carried0=carried_bwd0)


_AXES = ("x", "y", "c")
_CHIP_FLIPS = ((1, 0), (0, 1), (1, 1))


def _position():
    return tuple(lax.axis_index(a) for a in _AXES)


def _flipped(pos, flip):
    return tuple(1 - p if f else p for p, f in zip(pos, flip))


def _join_comms(comms):
    n_in = [len(c.ins) for c in comms]
    n_out = [len(c.outs) for c in comms]
    n_sem = [len(c.sems) for c in comms]

    def parts(ins, outs, sems):
        for k in range(len(comms)):
            a, b, s = sum(n_in[:k]), sum(n_out[:k]), sum(n_sem[:k])
            yield comms[k], (ins[a:a + n_in[k]], outs[b:b + n_out[k]], sems[s:s + n_sem[k]])

    def start(ins, outs, sems):
        for c, part in parts(ins, outs, sems):
            c.start(*part)

    def finish(ins, outs, sems):
        for c, part in parts(ins, outs, sems):
            c.finish(*part)

    joint = _Comm([a for c in comms for a in c.ins], [o for c in comms for o in c.outs],
                  [s for c in comms for s in c.sems], start, finish)
    return joint, lambda res: [list(res[sum(n_out[:k]):sum(n_out[:k + 1])]) for k in range(len(comms))]


def _run_comms(comms, name):
    joint, split = _join_comms(comms)

    def body(*refs):
        n_in, n_out = len(joint.ins), len(joint.outs)
        joint.start(refs[:n_in], refs[n_in:n_in + n_out], refs[n_in + n_out:])
        joint.finish(refs[:n_in], refs[n_in:n_in + n_out], refs[n_in + n_out:])

    res = pl.pallas_call(
        body, in_specs=[HBM_SPEC] * len(joint.ins), out_specs=[HBM_SPEC] * len(joint.outs), out_shape=joint.outs,
        scratch_shapes=joint.sems, name=name,
    )(*joint.ins)
    return split(res)


def _all_gather_comm(v, axes):
    flips = [f for f in np.ndindex(2, 2, 2) if any(f) and all(a in axes or not b for a, b in zip(_AXES, f))]
    n = len(flips) + 1

    def copies(ins, outs, sems):
        (v_ref,), (o_ref,), (send_sems, recv_sems, local_sem) = ins, outs, sems
        pos = _position()
        slot = 0
        for a, p in zip(_AXES, pos):
            if a in axes:
                slot = 2 * slot + p
        local = pltpu.make_async_copy(v_ref, o_ref.at[slot], local_sem)
        remote = [pltpu.make_async_remote_copy(v_ref, o_ref.at[slot], send_sems.at[k], recv_sems.at[k],
                                               device_id=_flipped(pos, flip), device_id_type=MESH)
                  for k, flip in enumerate(flips)]
        return [local] + remote

    def start(ins, outs, sems):
        for cp in copies(ins, outs, sems):
            cp.start()

    def finish(ins, outs, sems):
        for cp in copies(ins, outs, sems):
            cp.wait()

    sems = [pltpu.SemaphoreType.DMA((n - 1,)), pltpu.SemaphoreType.DMA((n - 1,)), pltpu.SemaphoreType.DMA(())]
    return _Comm([v], [_sds((n,) + v.shape, v.dtype)], sems, start, finish)


def _all_gather(v, axes, name):
    return _run_comms([_all_gather_comm(v, axes)], name)[0][0]


class _Item:
    def __init__(self, key, layer, shape, shard_axis, half_axis):
        self.key, self.layer, self.shape = key, layer, tuple(shape)
        self.shard_axis, self.half_axis = shard_axis, half_axis
        self.shard = shape[shard_axis] // 4
        self.half = shape[half_axis] // 2

    def sized(self, shard=False, half=False):
        s = list(self.shape)
        if shard:
            s[self.shard_axis] = self.shard
        if half:
            s[self.half_axis] = self.half
        return tuple(s)

    def window(self, ref, chip=None, half=None):
        idx = [slice(None)] * len(self.shape)
        if chip is not None:
            idx[self.shard_axis] = pl.ds(chip * self.shard, self.shard)
        if half is not None:
            idx[self.half_axis] = pl.ds(half * self.half, self.half)
        return ref.at[tuple(idx)]


def _items(d, w):
    out = []
    for j in range(2):
        out += [_Item("pool_w_in", j, (d, 2 * w), 1, 0), _Item("pool_w_grp", j, (4, w // 4, w // 4), 1, 0),
                _Item("pool_w_out", j, (w, d), 0, 1)]
    out += [_Item("na_w_in", 0, (d, 4 * w), 1, 0), _Item("na_w_out", 0, (w, d), 0, 1),
            _Item("conv_w_in", 0, (d, 4 * w), 1, 0), _Item("conv_w_out", 0, (w, d), 0, 1)]
    return out


def _gather_weights(shards, items, name):
    comm = _gather_comm(shards, items)

    def body(*refs):
        n = len(items)
        comm.start(refs[:n], refs[n:2 * n], refs[2 * n:])
        comm.finish(refs[:n], refs[n:2 * n], refs[2 * n:])

    return pl.pallas_call(
        body, in_specs=[HBM_SPEC] * len(items), out_specs=[HBM_SPEC] * len(items), out_shape=comm.outs,
        scratch_shapes=comm.sems, name=name,
    )(*shards)


def _gather_comm(shards, items):
    n = len(items)

    def copies(src, dst, sems, onward):
        send_a, recv_a, send_b, recv_b, send_c, recv_c = sems
        x, y, c = _position()
        chip = 2 * x + y
        sibling = (x, y, 1 - c)
        own, out, fwd, fwd_in = [], [], [], []
        for i, it in enumerate(items):
            own.append(pltpu.make_async_remote_copy(src[i], it.window(dst[i], chip=chip), send_c.at[i], recv_c.at[i],
                                                    device_id=sibling, device_id_type=MESH))
            for k, flip in enumerate(_CHIP_FLIPS):
                px, py = _flipped((x, y), flip)
                s = 3 * i + k
                out.append(pltpu.make_async_remote_copy(
                    it.window(src[i], half=c), it.window(dst[i], chip=chip, half=c), send_a.at[s], recv_a.at[s],
                    device_id=(px, py, c), device_id_type=MESH))
                if onward:
                    got = it.window(dst[i], chip=2 * px + py, half=c)
                    fwd.append(pltpu.make_async_remote_copy(got, got, send_b.at[s], recv_b.at[s],
                                                            device_id=sibling, device_id_type=MESH))
                    other = it.window(dst[i], chip=2 * px + py, half=1 - c)
                    fwd_in.append(pltpu.make_async_remote_copy(other, other, send_b.at[s], recv_b.at[s],
                                                               device_id=sibling, device_id_type=MESH))
        return own, out, fwd, fwd_in

    def start(src, dst, sems):
        own, out, _, _ = copies(src, dst, sems, False)
        for cp in own + out:
            cp.start()

    def finish(src, dst, sems):
        own, out, fwd, fwd_in = copies(src, dst, sems, True)
        for arrived, onward in zip(out, fwd):
            arrived.wait_recv()
            onward.start()
        for cp in fwd_in:
            cp.wait_recv()
        for cp in out + fwd:
            cp.wait_send()
        for cp in own:
            cp.wait()

    sems = [pltpu.SemaphoreType.DMA((3 * n,)) for _ in range(4)] + [pltpu.SemaphoreType.DMA((n,)) for _ in range(2)]
    return _Comm(shards, [_sds(it.shape, BF16) for it in items], sems, start, finish)


def _pair_swap_comm(arrays, windows, out_shapes):
    n = len(arrays)

    def copies(src, got, sems):
        send_sems, recv_sems = sems
        x, y, c = _position()
        return [pltpu.make_async_remote_copy(windows[i](src[i], 1 - c), got[i], send_sems.at[i], recv_sems.at[i],
                                             device_id=(x, y, 1 - c), device_id_type=MESH) for i in range(n)]

    def start(src, got, sems):
        for cp in copies(src, got, sems):
            cp.start()

    def finish(src, got, sems):
        for cp in copies(src, got, sems):
            cp.wait()

    return _Comm(arrays, out_shapes, [pltpu.SemaphoreType.DMA((n,)), pltpu.SemaphoreType.DMA((n,))], start, finish)


def _pair_swap(arrays, windows, out_shapes, name):
    return _run_comms([_pair_swap_comm(arrays, windows, out_shapes)], name)[0]


def _chip_exchange(partials, items, name):
    comm = _chip_exchange_comm(partials, items)

    def body(*refs):
        n = len(items)
        comm.start(refs[:n], refs[n:2 * n], refs[2 * n:])
        comm.finish(refs[:n], refs[n:2 * n], refs[2 * n:])

    return pl.pallas_call(
        body, in_specs=[HBM_SPEC] * len(items), out_specs=[HBM_SPEC] * len(items), out_shape=comm.outs,
        scratch_shapes=comm.sems, name=name,
    )(*partials)


def _chip_exchange_comm(partials, items):
    n = len(items)

    def copies(src, dst, sems):
        send_sems, recv_sems = sems
        x, y, c = _position()
        out = []
        for i, it in enumerate(items):
            for k, flip in enumerate(_CHIP_FLIPS):
                px, py = _flipped((x, y), flip)
                out.append(pltpu.make_async_remote_copy(
                    it.window(src[i], chip=2 * px + py), dst[i].at[k], send_sems.at[3 * i + k],
                    recv_sems.at[3 * i + k], device_id=(px, py, c), device_id_type=MESH))
        return out

    def start(src, dst, sems):
        for cp in copies(src, dst, sems):
            cp.start()

    def finish(src, dst, sems):
        for cp in copies(src, dst, sems):
            cp.wait()

    return _Comm(partials, [_sds((3,) + it.sized(shard=True, half=True), BF16) for it in items],
                 [pltpu.SemaphoreType.DMA((3 * n,)), pltpu.SemaphoreType.DMA((3 * n,))], start, finish)


_SUM_STEPS = 2


def _pair_sums(gs, gots, its, pos, name):
    n = len(its)
    nb = _SUM_STEPS
    g2 = [g.reshape(-1, g.shape[-1]) for g in gs]
    got2 = [t.reshape(-1, t.shape[-1]) for t in gots]

    def body(pos_ref, *refs):
        for g_ref, got_ref, o_ref in zip(refs[:n], refs[n:2 * n], refs[2 * n:]):
            o_ref[...] = (g_ref[...].astype(F32) + got_ref[...].astype(F32)).astype(BF16)

    g_specs, got_specs = [], []
    for it, t in zip(its, got2):
        rows, cols = t.shape
        blk = (rows // nb, cols)
        g_map = (lambda i, pos: (pos[1] * nb + i, 0)) if it.half_axis == 0 else (lambda i, pos: (i, pos[1]))
        g_specs.append(pl.BlockSpec(blk, g_map))
        got_specs.append(pl.BlockSpec(blk, lambda i, pos: (i, 0)))
    outs = pl.pallas_call(
        body, grid_spec=pltpu.PrefetchScalarGridSpec(
            num_scalar_prefetch=1, grid=(nb,), in_specs=g_specs + got_specs, out_specs=got_specs),
        out_shape=[_sds(t.shape, BF16) for t in got2], name=name, compiler_params=_cparams("parallel"),
    )(pos, *g2, *got2)
    return [o.reshape(t.shape) for o, t in zip(outs, gots)]


_FLIP_SLOT = {2: 0, 1: 1, 3: 2}


def _chip_sums(pairs, slots, its, pos, name):
    n = len(its)
    nb = _SUM_STEPS

    def body(pos_ref, *refs):
        chip = pos_ref[0]
        for own in range(4):
            @pl.when(chip == own)
            def _():
                for p_ref, s_ref, o_ref in zip(refs[:n], refs[n:2 * n], refs[2 * n:]):
                    acc = None
                    for k in range(4):
                        v = (p_ref[...] if k == own else s_ref[_FLIP_SLOT[own ^ k]]).astype(F32)
                        acc = v if acc is None else acc + v
                    o_ref[...] = acc

    p_specs, s_specs, o_specs, shapes = [], [], [], []
    for it in its:
        shape = it.sized(shard=True, half=True)
        blk = (shape[0] // nb,) + shape[1:]
        rest = (0,) * (len(shape) - 1)

        def p_map(i, pos, it=it, nd=len(shape)):
            lead = i + (pos[0] * nb if it.shard_axis == 0 else 0)
            return (lead,) + tuple(pos[0] if ax == it.shard_axis else 0 for ax in range(1, nd))

        p_specs.append(pl.BlockSpec(blk, p_map))
        s_specs.append(pl.BlockSpec((3,) + blk, lambda i, pos, rest=rest: (0, i) + rest))
        o_specs.append(pl.BlockSpec(blk, lambda i, pos, rest=rest: (i,) + rest))
        shapes.append(_sds(shape, F32))
    return pl.pallas_call(
        body, grid_spec=pltpu.PrefetchScalarGridSpec(
            num_scalar_prefetch=1, grid=(nb,), in_specs=p_specs + s_specs, out_specs=o_specs),
        out_shape=shapes, name=name, compiler_params=_cparams("parallel"),
    )(pos, *pairs, *slots)


_GRAD_KEYS = ("pool_w_in", "pool_w_grp", "pool_w_out", "na_w_in", "na_w_out", "conv_w_in", "conv_w_out")


def _adamw_matrix(w, m, v, owns, others, it, pos, name):
    nl = w.shape[0]
    rows_split = it.half_axis == 0
    r, cdim = int(np.prod(w.shape[1:-1])), w.shape[-1]
    hr, hc = (r // 2, cdim) if rows_split else (r, cdim // 2)
    br = min(hr, 256)
    nb = hr // br
    c1 = 1.0 - ADAM_B1 ** ADAM_STEP
    c2 = 1.0 - ADAM_B2 ** ADAM_STEP

    def body(pos_ref, w_ref, m_ref, v_ref, *rest):
        own_refs, other_refs = rest[:nl], rest[nl:2 * nl]
        g_ref, d_ref, nm_ref, nv_ref = rest[2 * nl:]
        j, h = pl.program_id(0), pl.program_id(1)
        own, other = own_refs[0][...], other_refs[0][...]
        for q in range(1, nl):
            own = jnp.where(j == q, own_refs[q][...], own)
            other = jnp.where(j == q, other_refs[q][...], other)
        gv = jnp.where(h == pos_ref[1], own, other)
        nm = ADAM_B1 * m_ref[...] + (1.0 - ADAM_B1) * gv
        nv = ADAM_B2 * v_ref[...] + (1.0 - ADAM_B2) * (gv * gv)
        g_ref[...] = gv
        nm_ref[...] = nm
        nv_ref[...] = nv
        d_ref[...] = -ADAM_LR * ((nm / c1) / (jnp.sqrt(nv / c2) + ADAM_EPS) + ADAM_WD * w_ref[...])

    if rows_split:
        full = pl.BlockSpec((None, br, hc), lambda j, h, i, pos: (j, h * nb + i, 0))
    else:
        full = pl.BlockSpec((None, br, hc), lambda j, h, i, pos: (j, i, h))
    half = pl.BlockSpec((br, hc), lambda j, h, i, pos: (i, 0))
    flat = lambda t: t.reshape(nl, r, cdim)
    outs = pl.pallas_call(
        body, grid_spec=pltpu.PrefetchScalarGridSpec(
            num_scalar_prefetch=1, grid=(nl, 2, nb), in_specs=[full] * 3 + [half] * (2 * nl), out_specs=[full] * 4),
        out_shape=[_sds((nl, r, cdim), F32)] * 4, name=name,
        compiler_params=_cparams("parallel", "parallel", "parallel"),
    )(pos, flat(w), flat(m), flat(v), *[t.reshape(hr, hc) for t in list(owns) + list(others)])
    return tuple(t.reshape(w.shape) for t in outs)


_WEIGHTS = ("c_ctx", "norm_g", "ada_w", "ada_b", "pool_w_in", "pool_w_grp", "pool_scale", "pool_w_out", "na_w_in",
            "na_rpb", "na_w_out", "conv_w_in", "conv_dw", "conv_db", "conv_w_out", "final_g")
_COND_ROWS = 16


def _modulations(cond, ada_w, ada_b_cols):
    nl, d, n = ada_w.shape
    return _matmul(
        cond, ada_w, mode="nn", grid=(nl, 1), a_silu=True, epilogue="bias",
        a_spec=pl.BlockSpec((_COND_ROWS, d), lambda i, j: (0, 0)), b_spec=pl.BlockSpec((None, d, n), lambda i, j: (i, 0, 0)),
        extra=(ada_b_cols,), extra_specs=(pl.BlockSpec((None, 1, n), lambda i, j: (i, 0, 0)),),
        out_shapes=[_sds((nl, _COND_ROWS, n), F32)], out_specs=[pl.BlockSpec((None, _COND_ROWS, n), lambda i, j: (i, 0, 0))],
        name="modulations")[0]


def _ada_w_grad(cond, dm_cols):
    d = cond.shape[1]
    nl, _, n = dm_cols.shape
    return _matmul(
        cond, dm_cols, mode="tn", grid=(nl, 1), a_silu=True,
        a_spec=pl.BlockSpec((_COND_ROWS, d), lambda i, j: (0, 0)), b_spec=pl.BlockSpec((None, _COND_ROWS, n), lambda i, j: (i, 0, 0)),
        out_shapes=[_sds((nl, d, n), F32)], out_specs=[pl.BlockSpec((None, d, n), lambda i, j: (i, 0, 0))],
        name="ada_w_grad")[0]


def _cond_grad(dm_cols, ada_w):
    nl, d, n = ada_w.shape
    return _matmul(
        dm_cols, ada_w, mode="nt", grid=(1, nl), nk=nl, acc_shape=(_COND_ROWS, d),
        a_spec=pl.BlockSpec((None, _COND_ROWS, n), lambda i, q: (q, 0, 0)), b_spec=pl.BlockSpec((None, d, n), lambda i, q: (q, 0, 0)),
        out_shapes=[_sds((_COND_ROWS, d), F32)], out_specs=[pl.BlockSpec((_COND_ROWS, d), lambda i, q: (0, 0))],
        name="cond_grad")[0]


def _pack(parts):
    flat = [p.reshape(-1) for p in parts]
    sizes = [f.shape[0] for f in flat]
    total = sum(sizes)
    rows = -(-total // 1024) * 8
    packed = jnp.concatenate(flat + [jnp.zeros((rows * 128 - total,), F32)]).reshape(rows, 128)
    offs = np.concatenate([[0], np.cumsum(sizes)])[:-1]
    return packed, [(int(o), p.shape) for o, p in zip(offs, parts)]


def _unpack(flat, layout, k):
    off, shape = layout[k]
    return flat[..., off:off + int(np.prod(shape))].reshape(flat.shape[:-1] + tuple(shape))


def kernel(x, c, ctx, c_ctx, norm_g, ada_w, ada_b, pool_w_in, pool_w_grp, pool_scale, pool_w_out, na_w_in, na_rpb, na_w_out, conv_w_in, conv_dw, conv_db, conv_w_out, final_g, loss_target, m_c_ctx, m_norm_g, m_ada_w, m_ada_b, m_pool_w_in, m_pool_w_grp, m_pool_scale, m_pool_w_out, m_na_w_in, m_na_rpb, m_na_w_out, m_conv_w_in, m_conv_dw, m_conv_db, m_conv_w_out, m_final_g, v_c_ctx, v_norm_g, v_ada_w, v_ada_b, v_pool_w_in, v_pool_w_grp, v_pool_scale, v_pool_w_out, v_na_w_in, v_na_rpb, v_na_w_out, v_conv_w_in, v_conv_dw, v_conv_db, v_conv_w_out, v_final_g):
    params = dict(c_ctx=c_ctx, norm_g=norm_g, ada_w=ada_w, ada_b=ada_b, pool_w_in=pool_w_in, pool_w_grp=pool_w_grp,
                  pool_scale=pool_scale, pool_w_out=pool_w_out, na_w_in=na_w_in, na_rpb=na_rpb, na_w_out=na_w_out,
                  conv_w_in=conv_w_in, conv_dw=conv_dw, conv_db=conv_db, conv_w_out=conv_w_out, final_g=final_g)
    mom1 = dict(c_ctx=m_c_ctx, norm_g=m_norm_g, ada_w=m_ada_w, ada_b=m_ada_b, pool_w_in=m_pool_w_in,
                pool_w_grp=m_pool_w_grp, pool_scale=m_pool_scale, pool_w_out=m_pool_w_out, na_w_in=m_na_w_in,
                na_rpb=m_na_rpb, na_w_out=m_na_w_out, conv_w_in=m_conv_w_in, conv_dw=m_conv_dw, conv_db=m_conv_db,
                conv_w_out=m_conv_w_out, final_g=m_final_g)
    mom2 = dict(c_ctx=v_c_ctx, norm_g=v_norm_g, ada_w=v_ada_w, ada_b=v_ada_b, pool_w_in=v_pool_w_in,
                pool_w_grp=v_pool_w_grp, pool_scale=v_pool_scale, pool_w_out=v_pool_w_out, na_w_in=v_na_w_in,
                na_rpb=v_na_rpb, na_w_out=v_na_w_out, conv_w_in=v_conv_w_in, conv_dw=v_conv_dw, conv_db=v_conv_db,
                conv_w_out=v_conv_w_out, final_g=v_final_g)
    d = x.shape[-1]
    w = na_w_out.shape[1] * 4
    xi, yi, ci = _position()
    chip = 2 * xi + yi
    dev = 2 * chip + ci
    n_ada = ada_w.shape[-1]

    def chip_cols(a, size):
        return lax.dynamic_slice_in_dim(a, chip * size, size, axis=a.ndim - 1)

    conds = _all_gather(c.reshape(8, d // 8), _AXES, "gather_cond").reshape(8, d)
    cond = jnp.concatenate([conds, c_ctx[None], jnp.zeros((_COND_ROWS - 9, d), F32)], axis=0)
    mod_cols = _modulations(cond, ada_w, chip_cols(ada_b, n_ada)[:, None, :])

    items = _items(d, w)
    first = [it for it in items if it.key.startswith("pool") and it.layer == 0]
    na_in, na_out = ([it for it in items if it.key == k] for k in ("na_w_in", "na_w_out"))
    late = [it for it in items if it not in first + na_in + na_out]
    shards_of = lambda its: [params[it.key][it.layer].astype(BF16) for it in its]
    small_pack, small_layout = _pack([pool_scale, conv_dw, conv_db])
    (mod_all,), first_mats, (small,) = _run_comms(
        [_all_gather_comm(mod_cols, ("x", "y")), _gather_comm(shards_of(first), first),
         _all_gather_comm(small_pack, ("x", "y"))], "gather_first")
    mod_all = mod_all.transpose(1, 2, 0, 3).reshape(4, _COND_ROWS, 3, d)
    mod = jnp.stack([lax.dynamic_index_in_dim(mod_all, dev, axis=1, keepdims=False), mod_all[:, 8]], axis=1)
    full = {(it.key, it.layer): mat for it, mat in zip(first, first_mats)}
    na_comms = dict(w_in_fwd=_gather_comm(shards_of(na_in), na_in), pool_fwd=_gather_comm(shards_of(na_out), na_out))
    late_comm = _gather_comm(shards_of(late), late)

    def na_weights(carried):
        return dict(na_w_in=carried["w_in_fwd"][0], na_w_out=carried["pool_fwd"][0])

    def late_weights(mats):
        full.update({(it.key, it.layer): mat for it, mat in zip(late, mats)})
        return dict(pool_w_in=[full[("pool_w_in", j)] for j in range(2)],
                    pool_w_grp=[full[("pool_w_grp", j)] for j in range(2)],
                    pool_w_out=[full[("pool_w_out", j)] for j in range(2)],
                    conv_w_in=full[("conv_w_in", 0)], conv_w_out=full[("conv_w_out", 0)])

    small = small.reshape(4, -1)

    def whole(k):
        parts = _unpack(small, small_layout, k)
        return jnp.moveaxis(parts, 0, -2).reshape(parts.shape[1:-1] + (-1,))

    wts = dict(pool_w_in=[full[("pool_w_in", 0)]], pool_w_grp=[full[("pool_w_grp", 0)]],
               pool_w_out=[full[("pool_w_out", 0)]], pool_scale=whole(0), na_rpb=na_rpb[0], conv_dw=whole(1)[0],
               conv_db=whole(2))
    pos = jnp.stack([chip, ci]).astype(jnp.int32)

    def layer_grads(its, by_layer):
        pick = {"pool_w_in": "w_in", "pool_w_grp": "w_grp", "pool_w_out": "w_out", "na_w_in": "w_in",
                "na_w_out": "w_out", "conv_w_in": "w_in", "conv_w_out": "w_out"}
        return [by_layer[(it.key.split("_")[0], it.layer)][pick[it.key]] for it in its]

    def pair_sums(its, mats, tag):
        got = _pair_swap(mats, [(lambda ref, half, it=it: it.window(ref, half=half)) for it in its],
                         [_sds(it.sized(half=True), BF16) for it in its], f"pair_exchange_{tag}")
        return _pair_sums(mats, got, its, pos, f"pair_sum_{tag}")

    pairs = dict()

    def grad_comm(gr3, gr2):
        pairs["late"] = pair_sums(late, layer_grads(late, {("pool", 1): gr3, ("conv", 0): gr2}), "late")
        return _chip_exchange_comm(pairs["late"], late)

    def na_grad_comms(gr1):
        na = na_in + na_out
        pairs["na"] = pair_sums(na, layer_grads(na, {("na", 0): gr1}), "na")
        return dict(w_in_bwd=_chip_exchange_comm(pairs["na"][:1], na_in),
                    w_out_bwd=_chip_exchange_comm(pairs["na"][1:], na_out))

    res = _example_step(x[0], ctx[0], loss_target[0], mod, norm_g, final_g[None], wts, na_comms, na_weights,
                        late_comm, late_weights, grad_comm, na_grad_comms)
    g0, g1, g2, g3 = res["layers"]
    first_grads = layer_grads(first, {("pool", 0): g0})
    packed, layout = _pack([res["dfinal_g"], res["dnorm_g"], res["dmod"], g1["rpb"],
                            jnp.concatenate([g0["scale"], g3["scale"]], axis=0), g2["dw"], g2["db"],
                            res["loss"][0, :1]])
    first_got, (every,) = _run_comms(
        [_pair_swap_comm(first_grads, [(lambda ref, half, it=it: it.window(ref, half=half)) for it in first],
                         [_sds(it.sized(half=True), BF16) for it in first]), _all_gather_comm(packed, _AXES)],
        "pair_exchange_first")
    pairs["first"] = _pair_sums(first_grads, first_got, first, pos, "pair_sum_first")

    grads = dict()
    total = _sum_lead(every, "sum_vec_grads").reshape(-1)
    every = every.reshape(8, -1)
    grads["final_g"] = _unpack(total, layout, 0).reshape(final_g.shape)
    grads["norm_g"] = _unpack(total, layout, 1)
    grads["na_rpb"] = _unpack(total, layout, 3)[None]
    grads["pool_scale"] = chip_cols(_unpack(total, layout, 4), pool_scale.shape[-1])
    grads["conv_dw"] = chip_cols(_unpack(total, layout, 5), conv_dw.shape[-1])[None]
    grads["conv_db"] = chip_cols(_unpack(total, layout, 6), conv_db.shape[-1])
    dmod_sum = _unpack(total, layout, 2).reshape(4, 2, 3 * d)
    dmod_each = _unpack(every, layout, 2).reshape(8, 4, 2, 3 * d)
    grads["ada_b"] = dmod_sum[:, 0] + dmod_sum[:, 1]
    dm = jnp.concatenate([dmod_each[:, :, 0].transpose(1, 0, 2), dmod_sum[:, 1][:, None],
                          jnp.zeros((4, _COND_ROWS - 9, 3 * d), F32)], axis=1)
    dm_cols = chip_cols(dm, n_ada)
    grads["ada_w"] = _ada_w_grad(cond, dm_cols)
    dcond = _cond_grad(dm_cols, ada_w)[8].reshape(8, d // 8)
    tail_comm, tail_split = _join_comms([_chip_exchange_comm(pairs["first"], first),
                                         _all_gather_comm(dcond, ("x", "y"))])
    ada_w_step, tail = _adamw(params["ada_w"], grads["ada_w"], mom1["ada_w"], mom2["ada_w"], "adamw_ada_w", tail_comm)
    first_slots, (dcond_all,) = tail_split(tail)
    grads["c_ctx"] = _sum_lead(dcond_all, "sum_cond_grad").reshape(d) * _dsilu(c_ctx)

    slots = dict(zip(late, res["carried"]))
    slots.update(zip(first, first_slots))
    slots.update(zip(na_in + na_out, res["carried0"]["w_in_bwd"] + res["carried0"]["w_out_bwd"]))
    pair_of = dict(zip(late, pairs["late"]))
    pair_of.update(zip(first, pairs["first"]))
    pair_of.update(zip(na_in + na_out, pairs["na"]))
    reduced = _chip_sums([pair_of[it] for it in items], [slots[it] for it in items], items, pos, "chip_sum")
    theirs = _pair_swap(reduced, [lambda ref, half: ref] * len(items),
                        [_sds(t.shape, F32) for t in reduced], "pair_return")
    matrix_out = dict(ada_w=ada_w_step)
    for k in _GRAD_KEYS:
        idx = [i for i, it in enumerate(items) if it.key == k]
        res_k = _adamw_matrix(params[k], mom1[k], mom2[k], [reduced[i] for i in idx], [theirs[i] for i in idx],
                              items[idx[0]], pos, f"adamw_{k}")
        grads[k], matrix_out[k] = res_k[0], res_k[1:]

    outs = [[], [], []]
    for k in _WEIGHTS:
        step = matrix_out[k] if k in matrix_out else _adamw(params[k], grads[k], mom1[k], mom2[k], f"adamw_{k}")
        for lst, val in zip(outs, step):
            lst.append(val)
    loss = _unpack(total, layout, 7)[0]
    return (loss, res["grad_x"][None], *[grads[k].reshape(params[k].shape) for k in _WEIGHTS],
            *outs[0], *outs[1], *outs[2])
```

```python
import functools

import numpy as np
import jax
import jax.numpy as jnp
from jax import lax
from jax.experimental import pallas as pl
from jax.experimental.pallas import tpu as pltpu

F32 = jnp.float32
BF16 = jnp.bfloat16

EPS = 1e-6
GRID_W = 64
HEAD_DIM = 64
WIN_ROWS = 8
WIN_COLS = 16
POOL_WINDOWS = (2, 4, 8, 16)
Q_ROWS = 4
K_ROWS = 12
PAD_ROWS = 4
NEG = -1e30

ADAM_LR = 0.001
ADAM_B1 = 0.9
ADAM_B2 = 0.999
ADAM_EPS = 1e-08
ADAM_WD = 0.01
ADAM_STEP = 10

ROW_BLOCK = 256
VMEM_LIMIT = 56 * 1024 * 1024
ACT = BF16

MESH = pl.DeviceIdType.MESH
HBM_SPEC = pl.BlockSpec(memory_space=pltpu.HBM)


def _cparams(*sem):
    return pltpu.CompilerParams(dimension_semantics=sem or None, vmem_limit_bytes=VMEM_LIMIT)


def _sds(shape, dtype):
    return jax.ShapeDtypeStruct(tuple(shape), dtype)


def _call(body, args, *, grid, in_specs, out_specs, out_shape, name, scratch_shapes=(), comm=None):
    sem = ("arbitrary",) * len(grid)
    if comm is None:
        res = pl.pallas_call(body, grid=grid, in_specs=list(in_specs), out_specs=list(out_specs), out_shape=list(out_shape),
                             scratch_shapes=list(scratch_shapes), name=name, compiler_params=_cparams(*sem))(*args)
        return list(res), []
    n_in, n_out, n_scr = len(in_specs), len(out_specs), len(scratch_shapes)
    n_cin, n_cout = len(comm.ins), len(comm.outs)

    def carrying(*refs):
        ins, cin = refs[:n_in], refs[n_in:n_in + n_cin]
        outs = refs[n_in + n_cin:n_in + n_cin + n_out]
        cout = refs[n_in + n_cin + n_out:n_in + n_cin + n_out + n_cout]
        rest = refs[n_in + n_cin + n_out + n_cout:]
        scr, sems = rest[:n_scr], rest[n_scr:]
        first, last = True, True
        for ax, size in enumerate(grid):
            first = jnp.logical_and(first, pl.program_id(ax) == 0)
            last = jnp.logical_and(last, pl.program_id(ax) == size - 1)
        pl.when(first)(lambda: comm.start(cin, cout, sems))
        body(*ins, *outs, *scr)
        pl.when(last)(lambda: comm.finish(cin, cout, sems))

    res = pl.pallas_call(
        carrying, grid=grid, in_specs=list(in_specs) + [HBM_SPEC] * n_cin, out_specs=list(out_specs) + [HBM_SPEC] * n_cout,
        out_shape=list(out_shape) + list(comm.outs), scratch_shapes=list(scratch_shapes) + list(comm.sems), name=name,
        compiler_params=_cparams(*sem),
    )(*args, *comm.ins)
    return list(res[:n_out]), list(res[n_out:])


def _sigmoid(x):
    return 1.0 / (1.0 + jnp.exp(-x))


def _silu(x):
    return x * _sigmoid(x)


def _dsilu(x):
    s = _sigmoid(x)
    return s * (1.0 + x * (1.0 - s))


_DIMS = {
    "nn": (((1,), (0,)), ((), ())),
    "nt": (((1,), (1,)), ((), ())),
    "tn": (((0,), (0,)), ((), ())),
}


def _matmul(a, b, *, mode, grid, a_spec, b_spec, out_shapes, out_specs, name, nk=1,
            a_silu=False, exact=False, epilogue=None, extra=(), extra_specs=(), acc_shape=None):
    n_extra = len(extra)
    n_out = len(out_shapes)

    def body(*refs):
        a_ref, b_ref = refs[:2]
        ex = refs[2:2 + n_extra]
        outs = refs[2 + n_extra:2 + n_extra + n_out]
        av = a_ref[...]
        bv = b_ref[...]
        if a_silu:
            av = _silu(av.astype(F32))
        if exact:
            prod = lax.dot_general(av.astype(F32), bv.astype(F32), _DIMS[mode],
                                   precision=lax.Precision.HIGHEST, preferred_element_type=F32)
        else:
            prod = lax.dot_general(av.astype(BF16), bv.astype(BF16), _DIMS[mode], preferred_element_type=F32)

        def finish(res):
            if epilogue is None:
                outs[0][...] = res.astype(outs[0].dtype)
            elif epilogue == "bias":
                outs[0][...] = (res + ex[0][...]).astype(outs[0].dtype)
            else:
                outs[0][...] = res.astype(outs[0].dtype)
                outs[1][...] = ex[0][...] + ex[1][...] * res

        if nk == 1:
            finish(prod)
        else:
            acc = refs[-1]
            k = pl.program_id(len(grid) - 1)

            @pl.when(k == 0)
            def _():
                acc[...] = prod

            @pl.when(k > 0)
            def _():
                acc[...] += prod

            @pl.when(k == nk - 1)
            def _():
                finish(acc[...])

    scratch = [pltpu.VMEM(acc_shape, F32)] if nk > 1 else []
    sem = ("parallel",) * (len(grid) - 1) + ("arbitrary",)
    return pl.pallas_call(
        body, grid=grid, in_specs=[a_spec, b_spec, *extra_specs], out_specs=list(out_specs),
        out_shape=list(out_shapes), scratch_shapes=scratch, name=name, compiler_params=_cparams(*sem),
    )(a, b, *extra)


def _row_tile(rows):
    for t in (768, 512, 256):
        if rows % t == 0:
            return t
    return rows


def _mm_nn(a, b, name, out_dtype=F32, tn=1024):
    m, k = a.shape
    n = b.shape[1]
    tm = _row_tile(m)
    tn = min(tn, n)
    return _matmul(
        a, b, mode="nn", grid=(m // tm, n // tn),
        a_spec=pl.BlockSpec((tm, k), lambda i, j: (i, 0)), b_spec=pl.BlockSpec((k, tn), lambda i, j: (0, j)),
        out_shapes=[_sds((m, n), out_dtype)], out_specs=[pl.BlockSpec((tm, tn), lambda i, j: (i, j))], name=name)[0]


def _mm_out_resid(a, w_out, xres, gate, nxb, name):
    m, k = a.shape
    n = w_out.shape[1]
    tm = ROW_BLOCK
    seg = lambda i, j: (jnp.where(i >= nxb, 1, 0), 0, 0)
    return _matmul(
        a, w_out, mode="nn", grid=(m // tm, 1),
        a_spec=pl.BlockSpec((tm, k), lambda i, j: (i, 0)), b_spec=pl.BlockSpec((k, n), lambda i, j: (0, 0)),
        extra=(xres, gate), extra_specs=(pl.BlockSpec((tm, n), lambda i, j: (i, 0)), pl.BlockSpec((None, 1, n), seg)),
        out_shapes=[_sds((m, n), ACT), _sds((m, n), F32)],
        out_specs=[pl.BlockSpec((tm, n), lambda i, j: (i, 0))] * 2, epilogue="resid", name=name)


def _mm_nt(a, b, name, out_dtype=F32):
    m, n = a.shape
    k = b.shape[0]
    tm = _row_tile(m)
    return _matmul(
        a, b, mode="nt", grid=(m // tm, 1),
        a_spec=pl.BlockSpec((tm, n), lambda i, j: (i, 0)), b_spec=pl.BlockSpec((k, n), lambda i, j: (0, 0)),
        out_shapes=[_sds((m, k), out_dtype)], out_specs=[pl.BlockSpec((tm, k), lambda i, j: (i, 0))], name=name)[0]


def _mm_nt_parts(a, b, name):
    p, m, kp = a.shape
    d = b.shape[0]
    tm = _row_tile(m)
    return _matmul(
        a, b, mode="nt", grid=(m // tm, p), nk=p, acc_shape=(tm, d),
        a_spec=pl.BlockSpec((None, tm, kp), lambda i, q: (q, i, 0)), b_spec=pl.BlockSpec((d, kp), lambda i, q: (0, q)),
        out_shapes=[_sds((m, d), F32)], out_specs=[pl.BlockSpec((tm, d), lambda i, q: (i, 0))], name=name)[0]


def _mm_tn(a, b, name, out_dtype, tm=512):
    r, m = a.shape
    n = b.shape[1]
    tm = min(tm, m)
    tn = min(1024, n)
    return _matmul(
        a, b, mode="tn", grid=(m // tm, n // tn),
        a_spec=pl.BlockSpec((r, tm), lambda i, j: (0, i)), b_spec=pl.BlockSpec((r, tn), lambda i, j: (0, j)),
        out_shapes=[_sds((m, n), out_dtype)], out_specs=[pl.BlockSpec((tm, tn), lambda i, j: (i, j))], name=name)[0]


def _mm_tn_parts(a, b, name, out_dtype, tm=512):
    r, m = a.shape
    p, _, np_ = b.shape
    tm = min(tm, m)
    return _matmul(
        a, b, mode="tn", grid=(m // tm, p),
        a_spec=pl.BlockSpec((r, tm), lambda i, q: (0, i)), b_spec=pl.BlockSpec((None, r, np_), lambda i, q: (q, 0, 0)),
        out_shapes=[_sds((m, p * np_), out_dtype)], out_specs=[pl.BlockSpec((tm, np_), lambda i, q: (i, q))],
        name=name)[0]


def _seg_map(nxb):
    return lambda i: (jnp.where(i >= nxb, 1, 0), 0, 0)


def _normmod_fwd(x, g, scale, shift, nxb, name):
    rows, d = x.shape
    tr = ROW_BLOCK

    def body(x_ref, g_ref, sc_ref, sh_ref, h_ref, r_ref):
        xv = x_ref[...]
        r = lax.rsqrt(jnp.mean(xv * xv, axis=-1, keepdims=True) + EPS)
        h = (xv * r) * g_ref[...] * (1.0 + sc_ref[...]) + sh_ref[...]
        h_ref[...] = h.astype(BF16)
        r_ref[...] = r

    row = pl.BlockSpec((tr, d), lambda i: (i, 0))
    vec = pl.BlockSpec((None, 1, d), _seg_map(nxb))
    return pl.pallas_call(
        body, grid=(rows // tr,), in_specs=[row, pl.BlockSpec((1, d), lambda i: (0, 0)), vec, vec],
        out_specs=[row, pl.BlockSpec((tr, 1), lambda i: (i, 0))],
        out_shape=[_sds((rows, d), BF16), _sds((rows, 1), F32)], name=name, compiler_params=_cparams("parallel"),
    )(x, g, scale, shift)


def _normmod_bwd(dh, x, r, g, scale, dres, nxb, name):
    rows, d = x.shape
    tr = ROW_BLOCK
    nres = dres.shape[0] // tr
    nseg = scale.shape[0]

    def body(dh_ref, x_ref, r_ref, g_ref, sc_ref, dres_ref, dx_ref, dsh_ref, dge_ref):
        i = pl.program_id(0)
        dhv = dh_ref[...]
        rv = r_ref[...]
        xn = x_ref[...] * rv
        dxn = dhv * (g_ref[...] * (1.0 + sc_ref[...]))
        dx = rv * (dxn - xn * jnp.mean(dxn * xn, axis=-1, keepdims=True))

        @pl.when(i < nres)
        def _():
            dx_ref[...] = dx + dres_ref[...]

        @pl.when(i >= nres)
        def _():
            dx_ref[...] = dx

        first = jnp.logical_or(i == 0, i == nxb)
        s_dh = jnp.sum(dhv, axis=0, keepdims=True)
        s_ge = jnp.sum(dhv * xn, axis=0, keepdims=True)

        @pl.when(first)
        def _():
            dsh_ref[...] = s_dh
            dge_ref[...] = s_ge

        @pl.when(jnp.logical_not(first))
        def _():
            dsh_ref[...] += s_dh
            dge_ref[...] += s_ge

    row = pl.BlockSpec((tr, d), lambda i: (i, 0))
    vec = pl.BlockSpec((None, 1, d), _seg_map(nxb))
    return pl.pallas_call(
        body, grid=(rows // tr,),
        in_specs=[row, row, pl.BlockSpec((tr, 1), lambda i: (i, 0)), pl.BlockSpec((1, d), lambda i: (0, 0)), vec,
                  pl.BlockSpec((tr, d), lambda i: (jnp.minimum(i, nres - 1), 0))],
        out_specs=[row, vec, vec],
        out_shape=[_sds((rows, d), F32), _sds((nseg, 1, d), F32), _sds((nseg, 1, d), F32)],
        name=name, compiler_params=_cparams("arbitrary"),
    )(dh, x, r, g, scale, dres)


def _gate_bwd(dxo, yx, gate, nxb, name):
    rows, d = yx.shape
    tr = ROW_BLOCK
    nseg = gate.shape[0]

    def body(dx_ref, yx_ref, gt_ref, dyx_ref, dg_ref):
        i = pl.program_id(0)
        dxv = dx_ref[...]
        dyx_ref[...] = (dxv * gt_ref[...]).astype(BF16)
        s = jnp.sum(dxv * yx_ref[...].astype(F32), axis=0, keepdims=True)
        first = jnp.logical_or(i == 0, i == nxb)

        @pl.when(first)
        def _():
            dg_ref[...] = s

        @pl.when(jnp.logical_not(first))
        def _():
            dg_ref[...] += s

    row = pl.BlockSpec((tr, d), lambda i: (i, 0))
    vec = pl.BlockSpec((None, 1, d), _seg_map(nxb))
    return pl.pallas_call(
        body, grid=(rows // tr,), in_specs=[row, row, vec], out_specs=[row, vec],
        out_shape=[_sds((rows, d), BF16), _sds((nseg, 1, d), F32)], name=name, compiler_params=_cparams("arbitrary"),
    )(dxo, yx, gate)


def _row_vec(ref, is_ctx):
    return ref[0] if is_ctx is None else jnp.where(is_ctx, ref[1], ref[0])


def _ctx_rows(i, tm, nx, nseg):
    if nseg == 1:
        return None
    return i * tm + lax.broadcasted_iota(jnp.int32, (tm, 1), 0) >= nx


def _seg_sums(ref, val, is_ctx, first):
    if is_ctx is None:
        parts = [jnp.sum(val, axis=0, keepdims=True)]
    else:
        parts = [jnp.sum(jnp.where(is_ctx, 0.0, val), axis=0, keepdims=True),
                 jnp.sum(jnp.where(is_ctx, val, 0.0), axis=0, keepdims=True)]

    @pl.when(first)
    def _():
        for k, p in enumerate(parts):
            ref[k] = p

    @pl.when(jnp.logical_not(first))
    def _():
        for k, p in enumerate(parts):
            ref[k] += p


def _w_out_resid(a, w_out, xres, gate, nx, name):
    m, k = a.shape
    n = w_out.shape[1]
    nseg = gate.shape[0]
    tm = _row_tile(m)

    def body(a_ref, w_ref, x_ref, gt_ref, yx_ref, xo_ref):
        yx = jnp.dot(a_ref[...], w_ref[...], preferred_element_type=F32)
        yx_ref[...] = yx.astype(ACT)
        xo_ref[...] = x_ref[...] + _row_vec(gt_ref, _ctx_rows(pl.program_id(0), tm, nx, nseg)) * yx

    row = pl.BlockSpec((tm, n), lambda i: (i, 0))
    return pl.pallas_call(
        body, grid=(m // tm,),
        in_specs=[pl.BlockSpec((tm, k), lambda i: (i, 0)), pl.BlockSpec((k, n), lambda i: (0, 0)), row,
                  pl.BlockSpec((nseg, 1, n), lambda i: (0, 0, 0))],
        out_specs=[row, row], out_shape=[_sds((m, n), ACT), _sds((m, n), F32)],
        name=name, compiler_params=_cparams("parallel"),
    )(a, w_out, xres, gate)


def _norm_w_in(x, g, scale, shift, w_in, nx, name, comm=None):
    rows, d = x.shape
    n = w_in.shape[1]
    nseg = scale.shape[0]
    tm = _row_tile(rows)
    tn = min(1024, n)

    def body(x_ref, g_ref, sc_ref, sh_ref, w_ref, h_ref, r_ref, p_ref):
        i, j = pl.program_id(0), pl.program_id(1)

        @pl.when(j == 0)
        def _():
            xv = x_ref[...]
            r = lax.rsqrt(jnp.mean(xv * xv, axis=-1, keepdims=True) + EPS)
            is_ctx = _ctx_rows(i, tm, nx, nseg)
            h = (xv * r) * g_ref[...] * (1.0 + _row_vec(sc_ref, is_ctx)) + _row_vec(sh_ref, is_ctx)
            h_ref[...] = h.astype(BF16)
            r_ref[...] = r

        p_ref[...] = jnp.dot(h_ref[...], w_ref[...], preferred_element_type=F32).astype(ACT)

    vec = pl.BlockSpec((nseg, 1, d), lambda i, j: (0, 0, 0))
    return _call(
        body, (x, g, scale, shift, w_in), grid=(rows // tm, n // tn),
        in_specs=[pl.BlockSpec((tm, d), lambda i, j: (i, 0)), pl.BlockSpec((1, d), lambda i, j: (0, 0)), vec, vec,
                  pl.BlockSpec((d, tn), lambda i, j: (0, j))],
        out_specs=[pl.BlockSpec((tm, d), lambda i, j: (i, 0)), pl.BlockSpec((tm, 1), lambda i, j: (i, 0)),
                   pl.BlockSpec((tm, tn), lambda i, j: (i, j))],
        out_shape=[_sds((rows, d), BF16), _sds((rows, 1), F32), _sds((rows, n), ACT)], name=name, comm=comm)


def _gate_w_out_bwd(dxo, yx, gate, w_out, nx, name, comm=None):
    rows, d = yx.shape
    w = w_out.shape[0]
    nseg = gate.shape[0]
    tm = _row_tile(rows)

    def body(dx_ref, yx_ref, gt_ref, w_ref, dyx_ref, da_ref, dg_ref):
        i = pl.program_id(0)
        is_ctx = _ctx_rows(i, tm, nx, nseg)
        dxv = dx_ref[...]
        dyx = (dxv * _row_vec(gt_ref, is_ctx)).astype(BF16)
        dyx_ref[...] = dyx
        da_ref[...] = lax.dot_general(dyx, w_ref[...], _DIMS["nt"], preferred_element_type=F32).astype(ACT)
        _seg_sums(dg_ref, dxv * yx_ref[...].astype(F32), is_ctx, i == 0)

    row = pl.BlockSpec((tm, d), lambda i: (i, 0))
    vec = pl.BlockSpec((nseg, 1, d), lambda i: (0, 0, 0))
    return _call(
        body, (dxo, yx, gate, w_out), grid=(rows // tm,),
        in_specs=[row, row, vec, pl.BlockSpec((w, d), lambda i: (0, 0))],
        out_specs=[row, pl.BlockSpec((tm, w), lambda i: (i, 0)), vec],
        out_shape=[_sds((rows, d), BF16), _sds((rows, w), ACT), _sds((nseg, 1, d), F32)], name=name, comm=comm)


def _w_in_bwd_norm(dparts, w_in, x, r, g, scale, dres, nx, name, comm=None):
    np_, rows, kp = dparts.shape
    d = w_in.shape[0]
    nseg = scale.shape[0]
    tm = _row_tile(rows)
    nsub = tm // ROW_BLOCK
    nres_blocks = dres.shape[0] // ROW_BLOCK

    def body(dp_ref, w_ref, x_ref, r_ref, g_ref, sc_ref, *rest):
        dres_refs = rest[:nsub]
        dx_ref, dsh_ref, dge_ref, acc = rest[nsub:]
        i, k = pl.program_id(0), pl.program_id(1)
        prod = lax.dot_general(dp_ref[...], w_ref[...], _DIMS["nt"], preferred_element_type=F32)

        @pl.when(k == 0)
        def _():
            acc[...] = prod

        @pl.when(k > 0)
        def _():
            acc[...] += prod

        @pl.when(k == np_ - 1)
        def _():
            is_ctx = _ctx_rows(i, tm, nx, nseg)
            dhv = acc[...]
            rv = r_ref[...]
            xn = x_ref[...] * rv
            dxn = dhv * (g_ref[...] * (1.0 + _row_vec(sc_ref, is_ctx)))
            dx = rv * (dxn - xn * jnp.mean(dxn * xn, axis=-1, keepdims=True))
            for s in range(nsub):
                piece = slice(s * ROW_BLOCK, (s + 1) * ROW_BLOCK)
                res = dres_refs[s][...]
                if nres_blocks * ROW_BLOCK < rows:
                    res = jnp.where(i * nsub + s < nres_blocks, res, 0.0)
                dx_ref[piece, :] = dx[piece, :] + res
            _seg_sums(dsh_ref, dhv, is_ctx, i == 0)
            _seg_sums(dge_ref, dhv * xn, is_ctx, i == 0)

    row = pl.BlockSpec((tm, d), lambda i, k: (i, 0))
    vec = pl.BlockSpec((nseg, 1, d), lambda i, k: (0, 0, 0))
    return _call(
        body, (dparts, w_in, x, r, g, scale, *([dres] * nsub)), grid=(rows // tm, np_),
        in_specs=[pl.BlockSpec((None, tm, kp), lambda i, k: (k, i, 0)), pl.BlockSpec((d, kp), lambda i, k: (0, k)),
                  row, pl.BlockSpec((tm, 1), lambda i, k: (i, 0)), pl.BlockSpec((1, d), lambda i, k: (0, 0)), vec]
        + [pl.BlockSpec((ROW_BLOCK, d), (lambda i, k, s=s: (jnp.minimum(i * nsub + s, nres_blocks - 1), 0)))
           for s in range(nsub)],
        out_specs=[row, vec, vec],
        out_shape=[_sds((rows, d), F32), _sds((nseg, 1, d), F32), _sds((nseg, 1, d), F32)],
        scratch_shapes=[pltpu.VMEM((tm, d), F32)], name=name, comm=comm)


_PAD_TOP = 16
_PAD_BOT = 32


def _window_sum(buf, xv, lo, n):
    t = xv.shape[0]
    c = xv.shape[1]
    tp = t + _PAD_TOP + _PAD_BOT
    buf[pl.ds(0, _PAD_TOP), :] = jnp.zeros((_PAD_TOP, c), F32)
    buf[pl.ds(_PAD_TOP, t), :] = xv
    buf[pl.ds(_PAD_TOP + t, _PAD_BOT), :] = jnp.zeros((_PAD_BOT, c), F32)
    p = buf[...]
    k = 1
    while k < n:
        p = p + pltpu.roll(p, tp - k, 0)
        k *= 2
    if lo:
        p = pltpu.roll(p, -lo, 0)
    buf[...] = p
    return buf[pl.ds(_PAD_TOP, t), :]


def _window_count(t, half):
    pos = lax.broadcasted_iota(jnp.int32, (t, 1), 0)
    return (jnp.minimum(pos + half, t) - jnp.maximum(pos - half, 0)).astype(F32)


def _segments(rows, nx):
    return [(0, nx)] + ([(nx, rows - nx)] if rows > nx else [])


def _pool_fwd(uv, nx, name):
    rows = uv.shape[0]
    w = uv.shape[1] // 2
    cb = 128
    per_group = w // len(POOL_WINDOWS) // cb
    segs = _segments(rows, nx)

    def body(u_ref, z_ref, *bufs):
        j = pl.program_id(0)
        for gi, win in enumerate(POOL_WINDOWS):
            half = win // 2

            @pl.when(jnp.logical_and(j >= gi * per_group, j < (gi + 1) * per_group))
            def _():
                for (start, length), buf in zip(segs, bufs):
                    uvv = u_ref[pl.ds(start, length), :].astype(F32)
                    s = _window_sum(buf, uvv, -half, win)
                    z_ref[pl.ds(start, length), :] = (s / _window_count(length, half) - uvv).astype(BF16)

    scratch = [pltpu.VMEM((length + _PAD_TOP + _PAD_BOT, cb), F32) for _, length in segs]
    return pl.pallas_call(
        body, grid=(w // cb,), in_specs=[pl.BlockSpec((rows, cb), lambda j: (0, j))],
        out_specs=pl.BlockSpec((rows, cb), lambda j: (0, j)), out_shape=_sds((rows, w), BF16),
        scratch_shapes=scratch, name=name, compiler_params=_cparams("parallel"),
    )(uv)


def _pool_bwd(dz, dgt, nx, name):
    rows, w = dz.shape
    cb = 128
    per_group = w // len(POOL_WINDOWS) // cb
    segs = _segments(rows, nx)

    def body(dz_ref, dgt_ref, o_ref, *bufs):
        j = pl.program_id(0)
        o_ref[1] = dgt_ref[...]
        for gi, win in enumerate(POOL_WINDOWS):
            half = win // 2

            @pl.when(jnp.logical_and(j >= gi * per_group, j < (gi + 1) * per_group))
            def _():
                for (start, length), buf in zip(segs, bufs):
                    dzv = dz_ref[pl.ds(start, length), :].astype(F32)
                    s = _window_sum(buf, dzv / _window_count(length, half), 1 - half, win)
                    o_ref[0, pl.ds(start, length), :] = (s - dzv).astype(BF16)

    scratch = [pltpu.VMEM((length + _PAD_TOP + _PAD_BOT, cb), F32) for _, length in segs]
    col = pl.BlockSpec((rows, cb), lambda j: (0, j))
    return pl.pallas_call(
        body, grid=(w // cb,), in_specs=[col, col], out_specs=pl.BlockSpec((2, rows, cb), lambda j: (0, 0, j)),
        out_shape=_sds((2, rows, w), BF16), scratch_shapes=scratch, name=name, compiler_params=_cparams("parallel"),
    )(dz, dgt)


def _grp_fwd(z, w_grp, uv, scale, name):
    rows, w = z.shape
    ng, gc, _ = w_grp.shape
    tm = _row_tile(rows)

    def body(z_ref, w_ref, gt_ref, sc_ref, mx_ref, a_ref):
        mixed = jnp.dot(z_ref[...], w_ref[...], preferred_element_type=F32)
        mx_ref[...] = mixed.astype(ACT)
        a_ref[...] = (mixed * sc_ref[...] * _silu(gt_ref[...].astype(F32))).astype(BF16)

    blk = pl.BlockSpec((tm, gc), lambda g, i: (i, g))
    return pl.pallas_call(
        body, grid=(ng, rows // tm),
        in_specs=[blk, pl.BlockSpec((None, gc, gc), lambda g, i: (g, 0, 0)),
                  pl.BlockSpec((tm, gc), lambda g, i: (i, ng + g)), pl.BlockSpec((1, gc), lambda g, i: (0, g))],
        out_specs=[blk, blk], out_shape=[_sds((rows, w), ACT), _sds((rows, w), BF16)],
        name=name, compiler_params=_cparams("parallel", "parallel"),
    )(z, w_grp, uv, scale)


def _grp_bwd(da, mixed, uv, scale, w_grp, name):
    rows, w = da.shape
    ng, gc, _ = w_grp.shape
    tm = _row_tile(rows)

    def body(da_ref, mx_ref, gt_ref, sc_ref, w_ref, dm_ref, dz_ref, dgt_ref, dsc_ref):
        i = pl.program_id(1)
        dav = da_ref[...].astype(F32)
        mixed = mx_ref[...].astype(F32)
        gt = gt_ref[...].astype(F32)
        sg = _silu(gt)
        sc = sc_ref[...]
        dm = (dav * sc * sg).astype(BF16)
        dm_ref[...] = dm
        dz_ref[...] = lax.dot_general(dm, w_ref[...], _DIMS["nt"], preferred_element_type=F32).astype(ACT)
        dgt_ref[...] = (dav * mixed * sc * _dsilu(gt)).astype(BF16)
        s = jnp.sum(dav * mixed * sg, axis=0, keepdims=True)

        @pl.when(i == 0)
        def _():
            dsc_ref[...] = s

        @pl.when(i > 0)
        def _():
            dsc_ref[...] += s

    blk = pl.BlockSpec((tm, gc), lambda g, i: (i, g))
    vec = pl.BlockSpec((1, gc), lambda g, i: (0, g))
    return pl.pallas_call(
        body, grid=(ng, rows // tm),
        in_specs=[blk, blk, pl.BlockSpec((tm, gc), lambda g, i: (i, ng + g)), vec,
                  pl.BlockSpec((None, gc, gc), lambda g, i: (g, 0, 0))],
        out_specs=[blk, blk, blk, vec],
        out_shape=[_sds((rows, w), BF16), _sds((rows, w), ACT), _sds((rows, w), BF16), _sds((1, w), F32)],
        name=name, compiler_params=_cparams("parallel", "arbitrary"),
    )(da, mixed, uv, scale, w_grp)


def _pool_scratch(rows, nx, cols):
    return [pltpu.VMEM((length + _PAD_TOP + _PAD_BOT, cols), F32) for _, length in _segments(rows, nx)]


def _per_group(g, fn):
    for gi, win in enumerate(POOL_WINDOWS):
        pl.when(g == gi)(functools.partial(fn, win))


def _pool_grp_fwd(uv, w_grp, scale, nx, name, comm=None):
    rows = uv.shape[0]
    ng, gc, _ = w_grp.shape
    w = ng * gc
    segs = _segments(rows, nx)

    def body(u_ref, gt_ref, w_ref, sc_ref, z_ref, mx_ref, a_ref, *bufs):
        def pool(win):
            half = win // 2
            for (start, length), buf in zip(segs, bufs):
                uvv = u_ref[pl.ds(start, length), :].astype(F32)
                s = _window_sum(buf, uvv, -half, win)
                z_ref[pl.ds(start, length), :] = (s / _window_count(length, half) - uvv).astype(BF16)

        _per_group(pl.program_id(0), pool)
        mixed = jnp.dot(z_ref[...], w_ref[...], preferred_element_type=F32)
        mx_ref[...] = mixed.astype(ACT)
        a_ref[...] = (mixed * sc_ref[...] * _silu(gt_ref[...].astype(F32))).astype(BF16)

    col = pl.BlockSpec((rows, gc), lambda g: (0, g))
    return _call(
        body, (uv, uv, w_grp, scale), grid=(ng,),
        in_specs=[col, pl.BlockSpec((rows, gc), lambda g: (0, ng + g)), pl.BlockSpec((None, gc, gc), lambda g: (g, 0, 0)),
                  pl.BlockSpec((1, gc), lambda g: (0, g))],
        out_specs=[col, col, col], out_shape=[_sds((rows, w), BF16), _sds((rows, w), ACT), _sds((rows, w), BF16)],
        scratch_shapes=_pool_scratch(rows, nx, gc), name=name, comm=comm)


def _pool_grp_bwd(da, mixed, uv, scale, w_grp, nx, name):
    rows, w = da.shape
    ng, gc, _ = w_grp.shape
    segs = _segments(rows, nx)

    def body(da_ref, mx_ref, gt_ref, sc_ref, w_ref, dm_ref, duv_ref, dsc_ref, dz_ref, *bufs):
        dav = da_ref[...].astype(F32)
        mixed = mx_ref[...].astype(F32)
        gt = gt_ref[...].astype(F32)
        sg = _silu(gt)
        sc = sc_ref[...]
        dm = (dav * sc * sg).astype(BF16)
        dm_ref[...] = dm
        dz_ref[...] = lax.dot_general(dm, w_ref[...], _DIMS["nt"], preferred_element_type=F32)
        duv_ref[1] = (dav * mixed * sc * _dsilu(gt)).astype(BF16)
        dsc_ref[...] = jnp.sum(dav * mixed * sg, axis=0, keepdims=True)

        def unpool(win):
            half = win // 2
            for (start, length), buf in zip(segs, bufs):
                dzv = dz_ref[pl.ds(start, length), :]
                s = _window_sum(buf, dzv / _window_count(length, half), 1 - half, win)
                duv_ref[0, pl.ds(start, length), :] = (s - dzv).astype(BF16)

        _per_group(pl.program_id(0), unpool)

    col = pl.BlockSpec((rows, gc), lambda g: (0, g))
    vec = pl.BlockSpec((1, gc), lambda g: (0, g))
    return pl.pallas_call(
        body, grid=(ng,),
        in_specs=[col, col, pl.BlockSpec((rows, gc), lambda g: (0, ng + g)), vec,
                  pl.BlockSpec((None, gc, gc), lambda g: (g, 0, 0))],
        out_specs=[col, pl.BlockSpec((2, rows, gc), lambda g: (0, 0, g)), vec],
        out_shape=[_sds((rows, w), BF16), _sds((2, rows, w), BF16), _sds((1, w), F32)],
        scratch_shapes=[pltpu.VMEM((rows, gc), F32)] + _pool_scratch(rows, nx, gc),
        name=name, compiler_params=_cparams("parallel"),
    )(da, mixed, uv, scale, w_grp)


def _grp_wgrad(z, dm, ng, name, out_dtype):
    rows, w = z.shape
    gc = w // ng

    def body(z_ref, dm_ref, o_ref):
        o_ref[...] = lax.dot_general(z_ref[...], dm_ref[...], _DIMS["tn"],
                                     preferred_element_type=F32).astype(o_ref.dtype)

    blk = pl.BlockSpec((rows, gc), lambda g: (0, g))
    return pl.pallas_call(
        body, grid=(ng,), in_specs=[blk, blk], out_specs=pl.BlockSpec((None, gc, gc), lambda g: (g, 0, 0)),
        out_shape=_sds((ng, gc, gc), out_dtype), name=name, compiler_params=_cparams("parallel"),
    )(z, dm)


def _shift_rows(v, by):
    t = v.shape[0]
    pos = lax.broadcasted_iota(jnp.int32, v.shape, 0)
    rolled = pltpu.roll(v, by % t, 0)
    keep = pos >= by if by > 0 else pos < t + by
    return jnp.where(keep, rolled, 0.0)


def _conv_specs(t, w, cb):
    return [pl.BlockSpec((t, cb), (lambda j, q=q: (0, q * (w // cb) + j))) for q in range(4)]


def _conv_fwd(p4, dw, db, name):
    t = p4.shape[0]
    w = p4.shape[1] // 4
    cb = 128

    def body(bg_ref, cg_ref, v_ref, g_ref, dw_ref, db_ref, a_ref):
        tv = cg_ref[...].astype(F32) * v_ref[...].astype(F32)
        conv = (dw_ref[0:1, :] * _shift_rows(tv, 1) + dw_ref[1:2, :] * tv + dw_ref[2:3, :] * _shift_rows(tv, -1)
                + db_ref[...])
        a_ref[...] = (bg_ref[...].astype(F32) * conv * _silu(g_ref[...].astype(F32))).astype(BF16)

    return pl.pallas_call(
        body, grid=(w // cb,),
        in_specs=_conv_specs(t, w, cb) + [pl.BlockSpec((3, cb), lambda j: (0, j)), pl.BlockSpec((1, cb), lambda j: (0, j))],
        out_specs=pl.BlockSpec((t, cb), lambda j: (0, j)), out_shape=_sds((t, w), BF16),
        name=name, compiler_params=_cparams("parallel"),
    )(p4, p4, p4, p4, dw, db)


def _conv_bwd(da, p4, dw, db, name):
    t, w = da.shape
    cb = 128

    def body(da_ref, bg_ref, cg_ref, v_ref, g_ref, dw_ref, db_ref, d4_ref, ddw_ref, ddb_ref):
        cg = cg_ref[...].astype(F32)
        vv = v_ref[...].astype(F32)
        bg = bg_ref[...].astype(F32)
        gv = g_ref[...].astype(F32)
        tv = cg * vv
        tm1 = _shift_rows(tv, 1)
        tp1 = _shift_rows(tv, -1)
        w0, w1, w2 = dw_ref[0:1, :], dw_ref[1:2, :], dw_ref[2:3, :]
        conv = w0 * tm1 + w1 * tv + w2 * tp1 + db_ref[...]
        y = bg * conv
        dav = da_ref[...].astype(F32)
        dy = dav * _silu(gv)
        d4_ref[3] = (dav * y * _dsilu(gv)).astype(BF16)
        d4_ref[0] = (dy * conv).astype(BF16)
        dconv = dy * bg
        ddb_ref[...] = jnp.sum(dconv, axis=0, keepdims=True)
        ddw_ref[0:1, :] = jnp.sum(dconv * tm1, axis=0, keepdims=True)
        ddw_ref[1:2, :] = jnp.sum(dconv * tv, axis=0, keepdims=True)
        ddw_ref[2:3, :] = jnp.sum(dconv * tp1, axis=0, keepdims=True)
        dt = w0 * _shift_rows(dconv, -1) + w1 * dconv + w2 * _shift_rows(dconv, 1)
        d4_ref[1] = (dt * vv).astype(BF16)
        d4_ref[2] = (dt * cg).astype(BF16)

    col = pl.BlockSpec((t, cb), lambda j: (0, j))
    tap = pl.BlockSpec((3, cb), lambda j: (0, j))
    bias = pl.BlockSpec((1, cb), lambda j: (0, j))
    return pl.pallas_call(
        body, grid=(w // cb,), in_specs=[col] + _conv_specs(t, w, cb) + [tap, bias],
        out_specs=[pl.BlockSpec((4, t, cb), lambda j: (0, 0, j)), tap, bias],
        out_shape=[_sds((4, t, w), BF16), _sds((3, w), F32), _sds((1, w), F32)],
        name=name, compiler_params=_cparams("parallel"),
    )(da, p4, p4, p4, p4, dw, db)


def _attn_mask():
    qn, kn = Q_ROWS * GRID_W, K_ROWS * GRID_W
    qr, qc = np.divmod(np.arange(qn), GRID_W)
    kr, kc = np.divmod(np.arange(kn), GRID_W)
    col0 = np.clip(qc - WIN_COLS // 2, 0, GRID_W - WIN_COLS)
    col_ok = (kc[None, :] >= col0[:, None]) & (kc[None, :] < col0[:, None] + WIN_COLS)
    first = np.zeros(qn, np.int64)
    last = np.full(qn, K_ROWS - WIN_ROWS)
    out = []
    for row0 in (first, qr, last):
        row_ok = (kr[None, :] >= row0[:, None]) & (kr[None, :] < row0[:, None] + WIN_ROWS)
        out.append(np.where(row_ok & col_ok, 0.0, NEG))
    return jnp.asarray(np.stack(out), F32)


_KW = K_ROWS * GRID_W
_QB = Q_ROWS * GRID_W
_PAIR = 2 * HEAD_DIM
_N_DR = 2 * WIN_ROWS - 1
_N_DC = 2 * WIN_COLS - 1
_RP_ROWS = 24
_N_TILES = _N_DR + 1
_BIAS_BASE = (WIN_ROWS - 1, WIN_ROWS // 2 - 1, -1)


class _Comm:
    def __init__(self, ins, outs, sems, start, finish):
        self.ins, self.outs, self.sems, self.start, self.finish = list(ins), list(outs), list(sems), start, finish


def _bias_pieces(cls):
    out = []
    for qr in range(Q_ROWS):
        for kr in range(0, K_ROWS, 2):
            tile = _BIAS_BASE[cls] - qr + kr + 1
            out.append((qr, kr, tile if 0 <= tile < _N_TILES else None))
    return out


def _toeplitz_pair(left_row, right_row):
    lane = lax.broadcasted_iota(jnp.int32, (GRID_W, _PAIR), 1)
    shape = (GRID_W, _PAIR)
    left = pltpu.roll(jnp.broadcast_to(left_row, shape), _PAIR - (WIN_COLS - 1), 1, stride=1, stride_axis=0)
    right = pltpu.roll(jnp.broadcast_to(right_row, shape), GRID_W - (WIN_COLS - 1), 1, stride=1, stride_axis=0)
    return jnp.where(lane < GRID_W, left, right)


def _build_tiles(tiles_ref, rp_ref):
    for h in range(2):
        for t in range(_N_TILES):
            tiles_ref[h, t] = _toeplitz_pair(rp_ref[h, t:t + 1, :], rp_ref[h, t + 1:t + 2, :])


def _block_class(b, nblk, fn, entering=False):
    interior = (b == 1) if entering else jnp.logical_and(b > 0, b < nblk - 1)
    for cls, cond in enumerate((b == 0, interior, b == nblk - 1)):
        pl.when(cond)(functools.partial(fn, cls))


def _attn_geometry(p4, nx):
    rows = p4.shape[0]
    w = p4.shape[1] // 4
    nhp = w // _PAIR
    nblk = nx // _QB
    qspec = lambda col: pl.BlockSpec((_QB, _PAIR), lambda hp, b: (b, col * nhp + hp))
    kspec = lambda col: pl.BlockSpec((rows, _PAIR), lambda hp, b: (0, col * nhp + hp))
    tspec = pl.BlockSpec((2, _RP_ROWS, _PAIR), lambda hp, b: (hp, 0, 0))
    mspec = pl.BlockSpec((None, _QB, _KW), lambda hp, b: (jnp.where(b == 0, 0, jnp.where(b == nblk - 1, 2, 1)), 0, 0))
    lspec = pl.BlockSpec((None, _QB, 2), lambda hp, b: (hp, b, 0))
    ospec = pl.BlockSpec((_QB, _PAIR), lambda hp, b: (b, hp))
    return rows, w, nhp, nblk, qspec, kspec, tspec, mspec, lspec, ospec


def _window_start(b, nx):
    return pl.multiple_of(jnp.clip(b * _QB - PAD_ROWS * GRID_W, 0, nx - _KW), _QB)


def _load_bias(bias_ref, tiles_ref, rp_ref, m_ref, b, nblk):
    pl.when(b == 0)(lambda: _build_tiles(tiles_ref, rp_ref))

    def fill(cls):
        for h in range(2):
            for qr, kr, tile in _bias_pieces(cls):
                rows = slice(qr * GRID_W, (qr + 1) * GRID_W)
                cols = slice(kr * GRID_W, (kr + 2) * GRID_W)
                m = m_ref[rows, cols]
                bias_ref[h, rows, cols] = m if tile is None else tiles_ref[h, tile] + m

    _block_class(b, nblk, fill, entering=True)


def _attn_fwd(p4, rp, mask, nx, name, comm=None):
    rows, w, nhp, nblk, qspec, kspec, tspec, mspec, lspec, ospec = _attn_geometry(p4, nx)
    n_ctx = rows - nx
    n_cin, n_cout = (len(comm.ins), len(comm.outs)) if comm else (0, 0)

    def body(*refs):
        q_ref, k_ref, v_ref, g_ref, rp_ref, m_ref = refs[:6]
        cin = refs[6:6 + n_cin]
        a_ref, o_ref, lse_ref = refs[6 + n_cin:9 + n_cin]
        cout = refs[9 + n_cin:9 + n_cin + n_cout]
        bias_ref, tiles_ref = refs[9 + n_cin + n_cout:11 + n_cin + n_cout]
        sems = refs[11 + n_cin + n_cout:]
        hp, b = pl.program_id(0), pl.program_id(1)
        if comm:
            pl.when(jnp.logical_and(hp == 0, b == 0))(lambda: comm.start(cin, cout, sems))
        start = _window_start(b, nx)
        _load_bias(bias_ref, tiles_ref, rp_ref, m_ref, b, nblk)
        qf = q_ref[...].astype(F32) * HEAD_DIM ** -0.5
        kw = k_ref[pl.ds(start, _KW), :].astype(BF16)
        vw = v_ref[pl.ds(start, _KW), :].astype(BF16)
        kcv = k_ref[pl.ds(nx, n_ctx), :].astype(BF16)
        vcv = v_ref[pl.ds(nx, n_ctx), :].astype(BF16)
        lane = lax.broadcasted_iota(jnp.int32, (1, _PAIR), 1)
        outs, lses = [], []
        for h in range(2):
            mine = (lane >= HEAD_DIM) if h else (lane < HEAD_DIM)
            qm = jnp.where(mine, qf, 0.0).astype(BF16)
            s_loc = lax.dot_general(qm, kw, _DIMS["nt"], preferred_element_type=F32) + bias_ref[h]
            s_ctx = lax.dot_general(qm, kcv, _DIMS["nt"], preferred_element_type=F32)
            mx = jnp.maximum(jnp.max(s_loc, axis=-1, keepdims=True), jnp.max(s_ctx, axis=-1, keepdims=True))
            p_loc = jnp.exp(s_loc - mx)
            p_ctx = jnp.exp(s_ctx - mx)
            den = jnp.sum(p_loc, axis=-1, keepdims=True) + jnp.sum(p_ctx, axis=-1, keepdims=True)
            o = jnp.dot(p_loc.astype(BF16), vw, preferred_element_type=F32)
            o = o + jnp.dot(p_ctx.astype(BF16), vcv, preferred_element_type=F32)
            outs.append(o * (1.0 / den))
            lses.append(mx + jnp.log(den))
        o = jnp.where(lane < HEAD_DIM, outs[0], outs[1])
        o_ref[...] = o.astype(ACT)
        a_ref[...] = (o * _silu(g_ref[...].astype(F32))).astype(BF16)
        col = lax.broadcasted_iota(jnp.int32, (1, 2), 1)
        lse_ref[...] = jnp.where(col == 0, lses[0], lses[1])
        if comm:
            pl.when(jnp.logical_and(hp == nhp - 1, b == nblk - 1))(lambda: comm.finish(cin, cout, sems))

    res = pl.pallas_call(
        body, grid=(nhp, nblk),
        in_specs=[qspec(0), kspec(1), kspec(2), qspec(3), tspec, mspec] + [HBM_SPEC] * n_cin,
        out_specs=[ospec, ospec, lspec] + [HBM_SPEC] * n_cout,
        out_shape=[_sds((nx, w), BF16), _sds((nx, w), ACT), _sds((nhp, nx, 2), F32)] + (comm.outs if comm else []),
        scratch_shapes=[pltpu.VMEM((2, _QB, _KW), F32), pltpu.VMEM((2, _N_TILES, GRID_W, _PAIR), F32)]
        + (comm.sems if comm else []),
        name=name, compiler_params=_cparams("arbitrary", "arbitrary"),
    )(p4, p4, p4, p4, rp, mask, *(comm.ins if comm else []))
    return res[:3], res[3:]


def _fold_tiles(dtiles_ref, drp_ref):
    shape = (GRID_W, _PAIR)
    lane = lax.broadcasted_iota(jnp.int32, shape, 1)
    flip = (lax.broadcasted_iota(jnp.int32, (_PAIR, _PAIR), 0)
            + lax.broadcasted_iota(jnp.int32, (_PAIR, _PAIR), 1) == _PAIR - 1).astype(F32)
    drp_ref[...] = jnp.zeros(drp_ref.shape, F32)
    for h in range(2):
        stack = dtiles_ref[h].reshape(_N_TILES * GRID_W, _PAIR)
        rev = jnp.dot(stack, flip, precision=lax.Precision.HIGHEST, preferred_element_type=F32)
        for t in range(_N_TILES):
            tile = rev[t * GRID_W:(t + 1) * GRID_W, :]
            for side in (0, 1):
                shift = _PAIR - GRID_W * side - (WIN_COLS - 1)
                half = jnp.where((lane < GRID_W) if side else (lane >= GRID_W), tile, 0.0)
                diag = pltpu.roll(half, shift, 1, stride=1, stride_axis=0)
                drp_ref[h, t + side:t + side + 1, :] += jnp.sum(diag, axis=0, keepdims=True)


def _attn_bwd(p4, rp, mask, o, lse, da, nx, name, comm=None):
    rows, w, nhp, nblk, qspec, kspec, tspec, mspec, lspec, ospec = _attn_geometry(p4, nx)
    n_ctx = rows - nx
    n_cin, n_cout = (len(comm.ins), len(comm.outs)) if comm else (0, 0)

    def body(*refs):
        q_ref, k_ref, v_ref, g_ref, rp_ref, m_ref, o_ref, lse_ref, da_ref = refs[:9]
        cin = refs[9:9 + n_cin]
        d4_ref, drp_ref = refs[9 + n_cin:11 + n_cin]
        cout = refs[11 + n_cin:11 + n_cin + n_cout]
        bias_ref, tiles_ref, ds_ref, dtiles_ref, dk_ref, dv_ref = refs[11 + n_cin + n_cout:17 + n_cin + n_cout]
        sems = refs[17 + n_cin + n_cout:]
        hp, b = pl.program_id(0), pl.program_id(1)
        if comm:
            pl.when(jnp.logical_and(hp == 0, b == 0))(lambda: comm.start(cin, cout, sems))
        start = _window_start(b, nx)
        here = pl.multiple_of(b * _QB, _QB)

        @pl.when(b == 0)
        def _():
            dk_ref[...] = jnp.zeros(dk_ref.shape, F32)
            dv_ref[...] = jnp.zeros(dv_ref.shape, F32)
            dtiles_ref[...] = jnp.zeros(dtiles_ref.shape, F32)
            d4_ref[0, pl.ds(nx, n_ctx), :] = jnp.zeros((n_ctx, _PAIR), BF16)
            d4_ref[3, pl.ds(nx, n_ctx), :] = jnp.zeros((n_ctx, _PAIR), BF16)

        _load_bias(bias_ref, tiles_ref, rp_ref, m_ref, b, nblk)
        gv = g_ref[...].astype(F32)
        dav = da_ref[...].astype(F32)
        ov = o_ref[...].astype(F32)
        dov = dav * _silu(gv)
        d4_ref[3, pl.ds(here, _QB), :] = (dav * ov * _dsilu(gv)).astype(BF16)
        qf = q_ref[...].astype(F32) * HEAD_DIM ** -0.5
        kw = k_ref[pl.ds(start, _KW), :].astype(BF16)
        vw = v_ref[pl.ds(start, _KW), :].astype(BF16)
        kcv = k_ref[pl.ds(nx, n_ctx), :].astype(BF16)
        vcv = v_ref[pl.ds(nx, n_ctx), :].astype(BF16)
        lane = lax.broadcasted_iota(jnp.int32, (1, _PAIR), 1)
        dq = jnp.zeros((_QB, _PAIR), F32)
        for h in range(2):
            mine = (lane >= HEAD_DIM) if h else (lane < HEAD_DIM)
            qm = jnp.where(mine, qf, 0.0).astype(BF16)
            dom = jnp.where(mine, dov, 0.0)
            dob = dom.astype(BF16)
            lse = lse_ref[:, h:h + 1]
            s_loc = lax.dot_general(qm, kw, _DIMS["nt"], preferred_element_type=F32)
            p_loc = jnp.exp(s_loc + bias_ref[h] - lse)
            p_ctx = jnp.exp(lax.dot_general(qm, kcv, _DIMS["nt"], preferred_element_type=F32) - lse)
            delta = jnp.sum(dom * ov, axis=-1, keepdims=True)
            ds_loc = p_loc * (lax.dot_general(dob, vw, _DIMS["nt"], preferred_element_type=F32) - delta)
            ds_ctx = p_ctx * (lax.dot_general(dob, vcv, _DIMS["nt"], preferred_element_type=F32) - delta)
            dsb_loc = ds_loc.astype(BF16)
            dsb_ctx = ds_ctx.astype(BF16)
            dq_h = (jnp.dot(dsb_loc, kw, preferred_element_type=F32)
                    + jnp.dot(dsb_ctx, kcv, preferred_element_type=F32))
            dq = dq + jnp.where(mine, dq_h, 0.0)
            dk_ref[pl.ds(start, _KW), :] += lax.dot_general(dsb_loc, qm, _DIMS["tn"], preferred_element_type=F32)
            dv_ref[pl.ds(start, _KW), :] += lax.dot_general(p_loc.astype(BF16), dob, _DIMS["tn"],
                                                            preferred_element_type=F32)
            dk_ref[pl.ds(nx, n_ctx), :] += lax.dot_general(dsb_ctx, qm, _DIMS["tn"], preferred_element_type=F32)
            dv_ref[pl.ds(nx, n_ctx), :] += lax.dot_general(p_ctx.astype(BF16), dob, _DIMS["tn"],
                                                           preferred_element_type=F32)
            ds_ref[h] = ds_loc
        d4_ref[0, pl.ds(here, _QB), :] = (dq * HEAD_DIM ** -0.5).astype(BF16)

        def scatter(cls):
            for h in range(2):
                for qr, kr, tile in _bias_pieces(cls):
                    if tile is not None:
                        dtiles_ref[h, tile] += ds_ref[h, qr * GRID_W:(qr + 1) * GRID_W, kr * GRID_W:(kr + 2) * GRID_W]

        _block_class(b, nblk, scatter)

        @pl.when(b == nblk - 1)
        def _():
            d4_ref[1] = dk_ref[...].astype(BF16)
            d4_ref[2] = dv_ref[...].astype(BF16)
            _fold_tiles(dtiles_ref, drp_ref)

        if comm:
            pl.when(jnp.logical_and(hp == nhp - 1, b == nblk - 1))(lambda: comm.finish(cin, cout, sems))

    tiles = pltpu.VMEM((2, _N_TILES, GRID_W, _PAIR), F32)
    block = pltpu.VMEM((2, _QB, _KW), F32)
    res = pl.pallas_call(
        body, grid=(nhp, nblk),
        in_specs=[qspec(0), kspec(1), kspec(2), qspec(3), tspec, mspec, ospec, lspec, ospec] + [HBM_SPEC] * n_cin,
        out_specs=[pl.BlockSpec((4, rows, _PAIR), lambda hp, b: (0, 0, hp)), tspec] + [HBM_SPEC] * n_cout,
        out_shape=[_sds((4, rows, w), BF16), _sds(rp.shape, F32)] + (comm.outs if comm else []),
        scratch_shapes=[block, tiles, block, tiles, pltpu.VMEM((rows, _PAIR), F32), pltpu.VMEM((rows, _PAIR), F32)]
        + (comm.sems if comm else []),
        name=name, compiler_params=_cparams("arbitrary", "arbitrary"),
    )(p4, p4, p4, p4, rp, mask, o, lse, da, *(comm.ins if comm else []))
    return res[:2], res[2:]


def _final(x, g, target, name):
    rows, d = x.shape
    tr = ROW_BLOCK
    nblk = rows // tr

    def body(x_ref, g_ref, t_ref, loss_ref, dx_ref, dg_ref, acc_ref):
        i = pl.program_id(0)
        xv = x_ref[...]
        gv = g_ref[...]
        r = lax.rsqrt(jnp.mean(xv * xv, axis=-1, keepdims=True) + EPS)
        xn = xv * r
        err = xn * gv - t_ref[...]
        dy = err * (1.0 / d)
        dxn = dy * gv
        dx_ref[...] = r * (dxn - xn * jnp.mean(dxn * xn, axis=-1, keepdims=True))
        s_g = jnp.sum(dy * xn, axis=0, keepdims=True)
        s_l = jnp.sum(jnp.mean(err * err, axis=-1, keepdims=True), axis=0, keepdims=True)

        @pl.when(i == 0)
        def _():
            dg_ref[...] = s_g
            acc_ref[...] = s_l

        @pl.when(i > 0)
        def _():
            dg_ref[...] += s_g
            acc_ref[...] += s_l

        @pl.when(i == nblk - 1)
        def _():
            loss_ref[...] = jnp.broadcast_to(0.5 * acc_ref[...], loss_ref.shape)

    row = pl.BlockSpec((tr, d), lambda i: (i, 0))
    vec = pl.BlockSpec((1, d), lambda i: (0, 0))
    return pl.pallas_call(
        body, grid=(nblk,), in_specs=[row, vec, row],
        out_specs=[pl.BlockSpec((1, 128), lambda i: (0, 0)), row, vec],
        out_shape=[_sds((1, 128), F32), _sds((rows, d), F32), _sds((1, d), F32)],
        scratch_shapes=[pltpu.VMEM((1, 1), F32)], name=name, compiler_params=_cparams("arbitrary"),
    )(x, g, target)


def _as2d(a):
    if a.ndim == 1:
        return a.reshape(-1, 128) if a.shape[0] % 128 == 0 else a.reshape(1, -1)
    return a.reshape(-1, a.shape[-1])


def _adamw(w, g, m, v, name, comm=None):
    shape = w.shape
    w2, g2, m2, v2 = (_as2d(t) for t in (w, g.reshape(shape), m, v))
    rows, cols = w2.shape
    tr = 512 if rows % 512 == 0 else rows
    c1 = 1.0 - ADAM_B1 ** ADAM_STEP
    c2 = 1.0 - ADAM_B2 ** ADAM_STEP

    def body(w_ref, g_ref, m_ref, v_ref, d_ref, nm_ref, nv_ref):
        gv = g_ref[...]
        nm = ADAM_B1 * m_ref[...] + (1.0 - ADAM_B1) * gv
        nv = ADAM_B2 * v_ref[...] + (1.0 - ADAM_B2) * (gv * gv)
        nm_ref[...] = nm
        nv_ref[...] = nv
        d_ref[...] = -ADAM_LR * ((nm / c1) / (jnp.sqrt(nv / c2) + ADAM_EPS) + ADAM_WD * w_ref[...])

    blk = pl.BlockSpec((tr, cols), lambda i: (i, 0))
    outs, carried = _call(body, (w2, g2, m2, v2), grid=(rows // tr,), in_specs=[blk] * 4, out_specs=[blk] * 3,
                          out_shape=[_sds((rows, cols), F32)] * 3, name=name, comm=comm)
    outs = tuple(t.reshape(shape) for t in outs)
    return outs if comm is None else (outs, carried)


def _sum_lead(x, name, out_dtype=F32):
    n, rows, cols = x.shape
    tr = 512 if rows % 512 == 0 else rows

    def body(x_ref, o_ref):
        acc = x_ref[0].astype(F32)
        for k in range(1, n):
            acc = acc + x_ref[k].astype(F32)
        o_ref[...] = acc.astype(out_dtype)

    return pl.pallas_call(
        body, grid=(rows // tr,), in_specs=[pl.BlockSpec((n, tr, cols), lambda i: (0, i, 0))],
        out_specs=pl.BlockSpec((tr, cols), lambda i: (i, 0)), out_shape=_sds((rows, cols), out_dtype),
        name=name, compiler_params=_cparams("parallel"),
    )(x)


_NO_CTX = 1 << 30


def _seg_vecs(mod_l, which, nseg):
    return mod_l[:nseg, which][:, None, :]


def _norm_grads(dshift, dgeff, dgate, g, scale):
    nseg, _, d = dshift.shape
    dmod = jnp.stack([dshift[:, 0], dgeff[:, 0] * g, dgate[:, 0]], axis=1)
    if nseg == 1:
        dmod = jnp.concatenate([dmod, jnp.zeros((1, 3, d), F32)], axis=0)
    dg = jnp.sum(dgeff[:, 0] * (1.0 + scale[:, 0]), axis=0)
    return dmod, dg


def _pool_layer(xin, g, mod_l, w_in, w_grp, w_out, pscale, nx, tag, comms=None):
    rows = xin.shape[0]
    nseg = 2 if rows > nx else 1
    comms = comms or {}
    shift, scale, gate = (_seg_vecs(mod_l, k, nseg) for k in range(3))
    (h, r, uv), c_in = _norm_w_in(xin, g, scale, shift, w_in, nx, f"w_in_fwd_{tag}", comms.get("w_in_fwd"))
    (z, mixed, a), c_pool = _pool_grp_fwd(uv, w_grp, pscale, nx, f"pool_fwd_{tag}", comms.get("pool_fwd"))
    yx, xout = _w_out_resid(a, w_out, xin, gate, nx, f"w_out_fwd_{tag}")

    def backward(dxo, comms=None):
        comms = comms or {}
        (dyx, da, dgate), c_out = _gate_w_out_bwd(dxo, yx, gate, w_out, nx, f"w_out_bwd_{tag}", comms.get("w_out_bwd"))
        gw_out = _mm_tn(a, dyx, f"w_out_grad_{tag}", BF16)
        dm, duv, dscale = _pool_grp_bwd(da, mixed, uv, pscale, w_grp, nx, f"pool_bwd_{tag}")
        gw_grp = _grp_wgrad(z, dm, w_grp.shape[0], f"grp_grad_{tag}", BF16)
        gw_in = _mm_tn_parts(h, duv, f"w_in_grad_{tag}", BF16)
        (dx, dshift, dgeff), c_bwd = _w_in_bwd_norm(duv, w_in, xin, r, g, scale, dxo, nx, f"w_in_bwd_{tag}",
                                                    comms.get("w_in_bwd"))
        dmod, dg = _norm_grads(dshift, dgeff, dgate, g[0], scale)
        return (dx, dmod, dg, dict(w_in=gw_in, w_grp=gw_grp, w_out=gw_out, scale=dscale),
                dict(w_out_bwd=c_out, w_in_bwd=c_bwd))

    return xout, backward, dict(w_in_fwd=c_in, pool_fwd=c_pool)


def _na_layer(xc, g, mod_l, w_in, rpb, w_out, nx, mask, comm=None):
    nh, n_dr, n_dc = rpb.shape
    shift, scale = _seg_vecs(mod_l, 0, 2), _seg_vecs(mod_l, 1, 2)
    gate = _seg_vecs(mod_l, 2, 1)
    (h, r, p4), _ = _norm_w_in(xc, g, scale, shift, w_in, nx, "w_in_fwd_na")
    rp = jnp.pad(rpb, ((0, 0), (1, _RP_ROWS - 1 - n_dr), (0, _PAIR - n_dc)))
    (a, o, lse), carried = _attn_fwd(p4, rp, mask, nx, "attn_fwd", comm)
    yx, xout = _w_out_resid(a, w_out, xc, gate, nx, "w_out_fwd_na")

    def backward(dxo, comm=None):
        (dyx, da, dgate), _ = _gate_w_out_bwd(dxo, yx, gate, w_out, nx, "w_out_bwd_na")
        gw_out = _mm_tn(a, dyx, "w_out_grad_na", BF16)
        (d4, drp), carried_bwd = _attn_bwd(p4, rp, mask, o, lse, da, nx, "attn_bwd", comm)
        gw_in = _mm_tn_parts(h, d4, "w_in_grad_na", BF16)
        (dx, dshift, dgeff), _ = _w_in_bwd_norm(d4, w_in, xc, r, g, scale, dxo, nx, "w_in_bwd_na")
        dgate2 = jnp.concatenate([dgate, jnp.zeros_like(dgate)], axis=0)
        dmod, dg = _norm_grads(dshift, dgeff, dgate2, g[0], scale)
        drpb = drp[:, 1:1 + n_dr, ::-1][:, :, :n_dc]
        return dx, dmod, dg, dict(w_in=gw_in, w_out=gw_out, rpb=drpb), carried_bwd

    return xout, backward, carried


def _conv_layer(xin, g, mod_l, w_in, dw, db, w_out):
    shift, scale, gate = (_seg_vecs(mod_l, k, 1) for k in range(3))
    nx = xin.shape[0]
    (h, r, p4), _ = _norm_w_in(xin, g, scale, shift, w_in, nx, "w_in_fwd_conv")
    a = _conv_fwd(p4, dw, db, "conv_fwd")
    yx, xout = _w_out_resid(a, w_out, xin, gate, nx, "w_out_fwd_conv")

    def backward(dxo):
        (dyx, da, dgate), _ = _gate_w_out_bwd(dxo, yx, gate, w_out, nx, "w_out_bwd_conv")
        gw_out = _mm_tn(a, dyx, "w_out_grad_conv", BF16)
        d4, ddw, ddb = _conv_bwd(da, p4, dw, db, "conv_bwd")
        gw_in = _mm_tn_parts(h, d4, "w_in_grad_conv", BF16)
        (dx, dshift, dgeff), _ = _w_in_bwd_norm(d4, w_in, xin, r, g, scale, dxo, nx, "w_in_bwd_conv")
        dmod, dg = _norm_grads(dshift, dgeff, dgate, g[0], scale)
        return dx, dmod, dg, dict(w_in=gw_in, w_out=gw_out, dw=ddw, db=ddb)

    return xout, backward


def _example_step(x, ctx, target, mod, norm_g, final_g, wts, na_comms=None, na_weights=None, late_comm=None,
                  late_weights=None, grad_comm=None, na_grad_comms=None):
    nx = x.shape[0]
    consts = _attn_mask()
    g_rows = [norm_g[i:i + 1] for i in range(4)]
    xc0 = jnp.concatenate([x, ctx], axis=0)
    xc1, bwd0, carried0 = _pool_layer(xc0, g_rows[0], mod[0], wts["pool_w_in"][0], wts["pool_w_grp"][0],
                                      wts["pool_w_out"][0], wts["pool_scale"][0:1], nx, "p0", na_comms)
    if na_weights is not None:
        wts = {**wts, **na_weights(carried0)}
    x2, bwd1, carried = _na_layer(xc1, g_rows[1], mod[1], wts["na_w_in"], wts["na_rpb"], wts["na_w_out"], nx, consts,
                                  late_comm)
    if late_weights is not None:
        wts = {**wts, **late_weights(carried)}
    x3, bwd2 = _conv_layer(x2, g_rows[2], mod[2], wts["conv_w_in"], wts["conv_dw"], wts["conv_db"], wts["conv_w_out"])
    x4, bwd3, _ = _pool_layer(x3, g_rows[3], mod[3], wts["pool_w_in"][1], wts["pool_w_grp"][1], wts["pool_w_out"][1],
                              wts["pool_scale"][1:2], nx, "p3")
    loss, dx4, dfinal_g = _final(x4, final_g, target, "loss_head")
    dx3, dmod3, dg3, gr3, _ = bwd3(dx4)
    dx2, dmod2, dg2, gr2 = bwd2(dx3)
    dxc1, dmod1, dg1, gr1, carried_bwd = bwd1(dx2, grad_comm(gr3, gr2) if grad_comm else None)
    dxc0, dmod0, dg0, gr0, carried_bwd0 = bwd0(dxc1, na_grad_comms(gr1) if na_grad_comms else None)
    return dict(
        loss=loss, grad_x=dxc0[:nx], dmod=jnp.stack([dmod0, dmod1, dmod2, dmod3]),
        dnorm_g=jnp.stack([dg0, dg1, dg2, dg3]), dfinal_g=dfinal_g, layers=(gr0, gr1, gr2, gr3), carried=carried_bwd,
        carried0=carried_bwd0)


_AXES = ("x", "y", "c")
_CHIP_FLIPS = ((1, 0), (0, 1), (1, 1))


def _position():
    return tuple(lax.axis_index(a) for a in _AXES)


def _flipped(pos, flip):
    return tuple(1 - p if f else p for p, f in zip(pos, flip))


def _join_comms(comms):
    n_in = [len(c.ins) for c in comms]
    n_out = [len(c.outs) for c in comms]
    n_sem = [len(c.sems) for c in comms]

    def parts(ins, outs, sems):
        for k in range(len(comms)):
            a, b, s = sum(n_in[:k]), sum(n_out[:k]), sum(n_sem[:k])
            yield comms[k], (ins[a:a + n_in[k]], outs[b:b + n_out[k]], sems[s:s + n_sem[k]])

    def start(ins, outs, sems):
        for c, part in parts(ins, outs, sems):
            c.start(*part)

    def finish(ins, outs, sems):
        for c, part in parts(ins, outs, sems):
            c.finish(*part)

    joint = _Comm([a for c in comms for a in c.ins], [o for c in comms for o in c.outs],
                  [s for c in comms for s in c.sems], start, finish)
    return joint, lambda res: [list(res[sum(n_out[:k]):sum(n_out[:k + 1])]) for k in range(len(comms))]


def _run_comms(comms, name):
    joint, split = _join_comms(comms)

    def body(*refs):
        n_in, n_out = len(joint.ins), len(joint.outs)
        joint.start(refs[:n_in], refs[n_in:n_in + n_out], refs[n_in + n_out:])
        joint.finish(refs[:n_in], refs[n_in:n_in + n_out], refs[n_in + n_out:])

    res = pl.pallas_call(
        body, in_specs=[HBM_SPEC] * len(joint.ins), out_specs=[HBM_SPEC] * len(joint.outs), out_shape=joint.outs,
        scratch_shapes=joint.sems, name=name,
    )(*joint.ins)
    return split(res)


def _all_gather_comm(v, axes):
    flips = [f for f in np.ndindex(2, 2, 2) if any(f) and all(a in axes or not b for a, b in zip(_AXES, f))]
    n = len(flips) + 1

    def copies(ins, outs, sems):
        (v_ref,), (o_ref,), (send_sems, recv_sems, local_sem) = ins, outs, sems
        pos = _position()
        slot = 0
        for a, p in zip(_AXES, pos):
            if a in axes:
                slot = 2 * slot + p
        local = pltpu.make_async_copy(v_ref, o_ref.at[slot], local_sem)
        remote = [pltpu.make_async_remote_copy(v_ref, o_ref.at[slot], send_sems.at[k], recv_sems.at[k],
                                               device_id=_flipped(pos, flip), device_id_type=MESH)
                  for k, flip in enumerate(flips)]
        return [local] + remote

    def start(ins, outs, sems):
        for cp in copies(ins, outs, sems):
            cp.start()

    def finish(ins, outs, sems):
        for cp in copies(ins, outs, sems):
            cp.wait()

    sems = [pltpu.SemaphoreType.DMA((n - 1,)), pltpu.SemaphoreType.DMA((n - 1,)), pltpu.SemaphoreType.DMA(())]
    return _Comm([v], [_sds((n,) + v.shape, v.dtype)], sems, start, finish)


def _all_gather(v, axes, name):
    return _run_comms([_all_gather_comm(v, axes)], name)[0][0]


class _Item:
    def __init__(self, key, layer, shape, shard_axis, half_axis):
        self.key, self.layer, self.shape = key, layer, tuple(shape)
        self.shard_axis, self.half_axis = shard_axis, half_axis
        self.shard = shape[shard_axis] // 4
        self.half = shape[half_axis] // 2

    def sized(self, shard=False, half=False):
        s = list(self.shape)
        if shard:
            s[self.shard_axis] = self.shard
        if half:
            s[self.half_axis] = self.half
        return tuple(s)

    def window(self, ref, chip=None, half=None):
        idx = [slice(None)] * len(self.shape)
        if chip is not None:
            idx[self.shard_axis] = pl.ds(chip * self.shard, self.shard)
        if half is not None:
            idx[self.half_axis] = pl.ds(half * self.half, self.half)
        return ref.at[tuple(idx)]


def _items(d, w):
    out = []
    for j in range(2):
        out += [_Item("pool_w_in", j, (d, 2 * w), 1, 0), _Item("pool_w_grp", j, (4, w // 4, w // 4), 1, 0),
                _Item("pool_w_out", j, (w, d), 0, 1)]
    out += [_Item("na_w_in", 0, (d, 4 * w), 1, 0), _Item("na_w_out", 0, (w, d), 0, 1),
            _Item("conv_w_in", 0, (d, 4 * w), 1, 0), _Item("conv_w_out", 0, (w, d), 0, 1)]
    return out


def _gather_weights(shards, items, name):
    comm = _gather_comm(shards, items)

    def body(*refs):
        n = len(items)
        comm.start(refs[:n], refs[n:2 * n], refs[2 * n:])
        comm.finish(refs[:n], refs[n:2 * n], refs[2 * n:])

    return pl.pallas_call(
        body, in_specs=[HBM_SPEC] * len(items), out_specs=[HBM_SPEC] * len(items), out_shape=comm.outs,
        scratch_shapes=comm.sems, name=name,
    )(*shards)


def _gather_comm(shards, items):
    n = len(items)

    def copies(src, dst, sems, onward):
        send_a, recv_a, send_b, recv_b, send_c, recv_c = sems
        x, y, c = _position()
        chip = 2 * x + y
        sibling = (x, y, 1 - c)
        own, out, fwd, fwd_in = [], [], [], []
        for i, it in enumerate(items):
            own.append(pltpu.make_async_remote_copy(src[i], it.window(dst[i], chip=chip), send_c.at[i], recv_c.at[i],
                                                    device_id=sibling, device_id_type=MESH))
            for k, flip in enumerate(_CHIP_FLIPS):
                px, py = _flipped((x, y), flip)
                s = 3 * i + k
                out.append(pltpu.make_async_remote_copy(
                    it.window(src[i], half=c), it.window(dst[i], chip=chip, half=c), send_a.at[s], recv_a.at[s],
                    device_id=(px, py, c), device_id_type=MESH))
                if onward:
                    got = it.window(dst[i], chip=2 * px + py, half=c)
                    fwd.append(pltpu.make_async_remote_copy(got, got, send_b.at[s], recv_b.at[s],
                                                            device_id=sibling, device_id_type=MESH))
                    other = it.window(dst[i], chip=2 * px + py, half=1 - c)
                    fwd_in.append(pltpu.make_async_remote_copy(other, other, send_b.at[s], recv_b.at[s],
                                                               device_id=sibling, device_id_type=MESH))
        return own, out, fwd, fwd_in

    def start(src, dst, sems):
        own, out, _, _ = copies(src, dst, sems, False)
        for cp in own + out:
            cp.start()

    def finish(src, dst, sems):
        own, out, fwd, fwd_in = copies(src, dst, sems, True)
        for arrived, onward in zip(out, fwd):
            arrived.wait_recv()
            onward.start()
        for cp in fwd_in:
            cp.wait_recv()
        for cp in out + fwd:
            cp.wait_send()
        for cp in own:
            cp.wait()

    sems = [pltpu.SemaphoreType.DMA((3 * n,)) for _ in range(4)] + [pltpu.SemaphoreType.DMA((n,)) for _ in range(2)]
    return _Comm(shards, [_sds(it.shape, BF16) for it in items], sems, start, finish)


def _pair_swap_comm(arrays, windows, out_shapes):
    n = len(arrays)

    def copies(src, got, sems):
        send_sems, recv_sems = sems
        x, y, c = _position()
        return [pltpu.make_async_remote_copy(windows[i](src[i], 1 - c), got[i], send_sems.at[i], recv_sems.at[i],
                                             device_id=(x, y, 1 - c), device_id_type=MESH) for i in range(n)]

    def start(src, got, sems):
        for cp in copies(src, got, sems):
            cp.start()

    def finish(src, got, sems):
        for cp in copies(src, got, sems):
            cp.wait()

    return _Comm(arrays, out_shapes, [pltpu.SemaphoreType.DMA((n,)), pltpu.SemaphoreType.DMA((n,))], start, finish)


def _pair_swap(arrays, windows, out_shapes, name):
    return _run_comms([_pair_swap_comm(arrays, windows, out_shapes)], name)[0]


def _chip_exchange(partials, items, name):
    comm = _chip_exchange_comm(partials, items)

    def body(*refs):
        n = len(items)
        comm.start(refs[:n], refs[n:2 * n], refs[2 * n:])
        comm.finish(refs[:n], refs[n:2 * n], refs[2 * n:])

    return pl.pallas_call(
        body, in_specs=[HBM_SPEC] * len(items), out_specs=[HBM_SPEC] * len(items), out_shape=comm.outs,
        scratch_shapes=comm.sems, name=name,
    )(*partials)


def _chip_exchange_comm(partials, items):
    n = len(items)

    def copies(src, dst, sems):
        send_sems, recv_sems = sems
        x, y, c = _position()
        out = []
        for i, it in enumerate(items):
            for k, flip in enumerate(_CHIP_FLIPS):
                px, py = _flipped((x, y), flip)
                out.append(pltpu.make_async_remote_copy(
                    it.window(src[i], chip=2 * px + py), dst[i].at[k], send_sems.at[3 * i + k],
                    recv_sems.at[3 * i + k], device_id=(px, py, c), device_id_type=MESH))
        return out

    def start(src, dst, sems):
        for cp in copies(src, dst, sems):
            cp.start()

    def finish(src, dst, sems):
        for cp in copies(src, dst, sems):
            cp.wait()

    return _Comm(partials, [_sds((3,) + it.sized(shard=True, half=True), BF16) for it in items],
                 [pltpu.SemaphoreType.DMA((3 * n,)), pltpu.SemaphoreType.DMA((3 * n,))], start, finish)


_SUM_STEPS = 2


def _pair_sums(gs, gots, its, pos, name):
    n = len(its)
    nb = _SUM_STEPS
    g2 = [g.reshape(-1, g.shape[-1]) for g in gs]
    got2 = [t.reshape(-1, t.shape[-1]) for t in gots]

    def body(pos_ref, *refs):
        for g_ref, got_ref, o_ref in zip(refs[:n], refs[n:2 * n], refs[2 * n:]):
            o_ref[...] = (g_ref[...].astype(F32) + got_ref[...].astype(F32)).astype(BF16)

    g_specs, got_specs = [], []
    for it, t in zip(its, got2):
        rows, cols = t.shape
        blk = (rows // nb, cols)
        g_map = (lambda i, pos: (pos[1] * nb + i, 0)) if it.half_axis == 0 else (lambda i, pos: (i, pos[1]))
        g_specs.append(pl.BlockSpec(blk, g_map))
        got_specs.append(pl.BlockSpec(blk, lambda i, pos: (i, 0)))
    outs = pl.pallas_call(
        body, grid_spec=pltpu.PrefetchScalarGridSpec(
            num_scalar_prefetch=1, grid=(nb,), in_specs=g_specs + got_specs, out_specs=got_specs),
        out_shape=[_sds(t.shape, BF16) for t in got2], name=name, compiler_params=_cparams("parallel"),
    )(pos, *g2, *got2)
    return [o.reshape(t.shape) for o, t in zip(outs, gots)]


_FLIP_SLOT = {2: 0, 1: 1, 3: 2}


def _chip_sums(pairs, slots, its, pos, name):
    n = len(its)
    nb = _SUM_STEPS

    def body(pos_ref, *refs):
        chip = pos_ref[0]
        for own in range(4):
            @pl.when(chip == own)
            def _():
                for p_ref, s_ref, o_ref in zip(refs[:n], refs[n:2 * n], refs[2 * n:]):
                    acc = None
                    for k in range(4):
                        v = (p_ref[...] if k == own else s_ref[_FLIP_SLOT[own ^ k]]).astype(F32)
                        acc = v if acc is None else acc + v
                    o_ref[...] = acc

    p_specs, s_specs, o_specs, shapes = [], [], [], []
    for it in its:
        shape = it.sized(shard=True, half=True)
        blk = (shape[0] // nb,) + shape[1:]
        rest = (0,) * (len(shape) - 1)

        def p_map(i, pos, it=it, nd=len(shape)):
            lead = i + (pos[0] * nb if it.shard_axis == 0 else 0)
            return (lead,) + tuple(pos[0] if ax == it.shard_axis else 0 for ax in range(1, nd))

        p_specs.append(pl.BlockSpec(blk, p_map))
        s_specs.append(pl.BlockSpec((3,) + blk, lambda i, pos, rest=rest: (0, i) + rest))
        o_specs.append(pl.BlockSpec(blk, lambda i, pos, rest=rest: (i,) + rest))
        shapes.append(_sds(shape, F32))
    return pl.pallas_call(
        body, grid_spec=pltpu.PrefetchScalarGridSpec(
            num_scalar_prefetch=1, grid=(nb,), in_specs=p_specs + s_specs, out_specs=o_specs),
        out_shape=shapes, name=name, compiler_params=_cparams("parallel"),
    )(pos, *pairs, *slots)


_GRAD_KEYS = ("pool_w_in", "pool_w_grp", "pool_w_out", "na_w_in", "na_w_out", "conv_w_in", "conv_w_out")


def _adamw_matrix(w, m, v, owns, others, it, pos, name):
    nl = w.shape[0]
    rows_split = it.half_axis == 0
    r, cdim = int(np.prod(w.shape[1:-1])), w.shape[-1]
    hr, hc = (r // 2, cdim) if rows_split else (r, cdim // 2)
    br = min(hr, 256)
    nb = hr // br
    c1 = 1.0 - ADAM_B1 ** ADAM_STEP
    c2 = 1.0 - ADAM_B2 ** ADAM_STEP

    def body(pos_ref, w_ref, m_ref, v_ref, *rest):
        own_refs, other_refs = rest[:nl], rest[nl:2 * nl]
        g_ref, d_ref, nm_ref, nv_ref = rest[2 * nl:]
        j, h = pl.program_id(0), pl.program_id(1)
        own, other = own_refs[0][...], other_refs[0][...]
        for q in range(1, nl):
            own = jnp.where(j == q, own_refs[q][...], own)
            other = jnp.where(j == q, other_refs[q][...], other)
        gv = jnp.where(h == pos_ref[1], own, other)
        nm = ADAM_B1 * m_ref[...] + (1.0 - ADAM_B1) * gv
        nv = ADAM_B2 * v_ref[...] + (1.0 - ADAM_B2) * (gv * gv)
        g_ref[...] = gv
        nm_ref[...] = nm
        nv_ref[...] = nv
        d_ref[...] = -ADAM_LR * ((nm / c1) / (jnp.sqrt(nv / c2) + ADAM_EPS) + ADAM_WD * w_ref[...])

    if rows_split:
        full = pl.BlockSpec((None, br, hc), lambda j, h, i, pos: (j, h * nb + i, 0))
    else:
        full = pl.BlockSpec((None, br, hc), lambda j, h, i, pos: (j, i, h))
    half = pl.BlockSpec((br, hc), lambda j, h, i, pos: (i, 0))
    flat = lambda t: t.reshape(nl, r, cdim)
    outs = pl.pallas_call(
        body, grid_spec=pltpu.PrefetchScalarGridSpec(
            num_scalar_prefetch=1, grid=(nl, 2, nb), in_specs=[full] * 3 + [half] * (2 * nl), out_specs=[full] * 4),
        out_shape=[_sds((nl, r, cdim), F32)] * 4, name=name,
        compiler_params=_cparams("parallel", "parallel", "parallel"),
    )(pos, flat(w), flat(m), flat(v), *[t.reshape(hr, hc) for t in list(owns) + list(others)])
    return tuple(t.reshape(w.shape) for t in outs)


_WEIGHTS = ("c_ctx", "norm_g", "ada_w", "ada_b", "pool_w_in", "pool_w_grp", "pool_scale", "pool_w_out", "na_w_in",
            "na_rpb", "na_w_out", "conv_w_in", "conv_dw", "conv_db", "conv_w_out", "final_g")
_COND_ROWS = 16


def _modulations(cond, ada_w, ada_b_cols):
    nl, d, n = ada_w.shape
    return _matmul(
        cond, ada_w, mode="nn", grid=(nl, 1), a_silu=True, epilogue="bias",
        a_spec=pl.BlockSpec((_COND_ROWS, d), lambda i, j: (0, 0)), b_spec=pl.BlockSpec((None, d, n), lambda i, j: (i, 0, 0)),
        extra=(ada_b_cols,), extra_specs=(pl.BlockSpec((None, 1, n), lambda i, j: (i, 0, 0)),),
        out_shapes=[_sds((nl, _COND_ROWS, n), F32)], out_specs=[pl.BlockSpec((None, _COND_ROWS, n), lambda i, j: (i, 0, 0))],
        name="modulations")[0]


def _ada_w_grad(cond, dm_cols):
    d = cond.shape[1]
    nl, _, n = dm_cols.shape
    return _matmul(
        cond, dm_cols, mode="tn", grid=(nl, 1), a_silu=True,
        a_spec=pl.BlockSpec((_COND_ROWS, d), lambda i, j: (0, 0)), b_spec=pl.BlockSpec((None, _COND_ROWS, n), lambda i, j: (i, 0, 0)),
        out_shapes=[_sds((nl, d, n), F32)], out_specs=[pl.BlockSpec((None, d, n), lambda i, j: (i, 0, 0))],
        name="ada_w_grad")[0]


def _cond_grad(dm_cols, ada_w):
    nl, d, n = ada_w.shape
    return _matmul(
        dm_cols, ada_w, mode="nt", grid=(1, nl), nk=nl, acc_shape=(_COND_ROWS, d),
        a_spec=pl.BlockSpec((None, _COND_ROWS, n), lambda i, q: (q, 0, 0)), b_spec=pl.BlockSpec((None, d, n), lambda i, q: (q, 0, 0)),
        out_shapes=[_sds((_COND_ROWS, d), F32)], out_specs=[pl.BlockSpec((_COND_ROWS, d), lambda i, q: (0, 0))],
        name="cond_grad")[0]


def _pack(parts):
    flat = [p.reshape(-1) for p in parts]
    sizes = [f.shape[0] for f in flat]
    total = sum(sizes)
    rows = -(-total // 1024) * 8
    packed = jnp.concatenate(flat + [jnp.zeros((rows * 128 - total,), F32)]).reshape(rows, 128)
    offs = np.concatenate([[0], np.cumsum(sizes)])[:-1]
    return packed, [(int(o), p.shape) for o, p in zip(offs, parts)]


def _unpack(flat, layout, k):
    off, shape = layout[k]
    return flat[..., off:off + int(np.prod(shape))].reshape(flat.shape[:-1] + tuple(shape))


def kernel(x, c, ctx, c_ctx, norm_g, ada_w, ada_b, pool_w_in, pool_w_grp, pool_scale, pool_w_out, na_w_in, na_rpb, na_w_out, conv_w_in, conv_dw, conv_db, conv_w_out, final_g, loss_target, m_c_ctx, m_norm_g, m_ada_w, m_ada_b, m_pool_w_in, m_pool_w_grp, m_pool_scale, m_pool_w_out, m_na_w_in, m_na_rpb, m_na_w_out, m_conv_w_in, m_conv_dw, m_conv_db, m_conv_w_out, m_final_g, v_c_ctx, v_norm_g, v_ada_w, v_ada_b, v_pool_w_in, v_pool_w_grp, v_pool_scale, v_pool_w_out, v_na_w_in, v_na_rpb, v_na_w_out, v_conv_w_in, v_conv_dw, v_conv_db, v_conv_w_out, v_final_g):
    params = dict(c_ctx=c_ctx, norm_g=norm_g, ada_w=ada_w, ada_b=ada_b, pool_w_in=pool_w_in, pool_w_grp=pool_w_grp,
                  pool_scale=pool_scale, pool_w_out=pool_w_out, na_w_in=na_w_in, na_rpb=na_rpb, na_w_out=na_w_out,
                  conv_w_in=conv_w_in, conv_dw=conv_dw, conv_db=conv_db, conv_w_out=conv_w_out, final_g=final_g)
    mom1 = dict(c_ctx=m_c_ctx, norm_g=m_norm_g, ada_w=m_ada_w, ada_b=m_ada_b, pool_w_in=m_pool_w_in,
                pool_w_grp=m_pool_w_grp, pool_scale=m_pool_scale, pool_w_out=m_pool_w_out, na_w_in=m_na_w_in,
                na_rpb=m_na_rpb, na_w_out=m_na_w_out, conv_w_in=m_conv_w_in, conv_dw=m_conv_dw, conv_db=m_conv_db,
                conv_w_out=m_conv_w_out, final_g=m_final_g)
    mom2 = dict(c_ctx=v_c_ctx, norm_g=v_norm_g, ada_w=v_ada_w, ada_b=v_ada_b, pool_w_in=v_pool_w_in,
                pool_w_grp=v_pool_w_grp, pool_scale=v_pool_scale, pool_w_out=v_pool_w_out, na_w_in=v_na_w_in,
                na_rpb=v_na_rpb, na_w_out=v_na_w_out, conv_w_in=v_conv_w_in, conv_dw=v_conv_dw, conv_db=v_conv_db,
                conv_w_out=v_conv_w_out, final_g=v_final_g)
    d = x.shape[-1]
    w = na_w_out.shape[1] * 4
    xi, yi, ci = _position()
    chip = 2 * xi + yi
    dev = 2 * chip + ci
    n_ada = ada_w.shape[-1]

    def chip_cols(a, size):
        return lax.dynamic_slice_in_dim(a, chip * size, size, axis=a.ndim - 1)

    conds = _all_gather(c.reshape(8, d // 8), _AXES, "gather_cond").reshape(8, d)
    cond = jnp.concatenate([conds, c_ctx[None], jnp.zeros((_COND_ROWS - 9, d), F32)], axis=0)
    mod_cols = _modulations(cond, ada_w, chip_cols(ada_b, n_ada)[:, None, :])

    items = _items(d, w)
    first = [it for it in items if it.key.startswith("pool") and it.layer == 0]
    na_in, na_out = ([it for it in items if it.key == k] for k in ("na_w_in", "na_w_out"))
    late = [it for it in items if it not in first + na_in + na_out]
    shards_of = lambda its: [params[it.key][it.layer].astype(BF16) for it in its]
    small_pack, small_layout = _pack([pool_scale, conv_dw, conv_db])
    (mod_all,), first_mats, (small,) = _run_comms(
        [_all_gather_comm(mod_cols, ("x", "y")), _gather_comm(shards_of(first), first),
         _all_gather_comm(small_pack, ("x", "y"))], "gather_first")
    mod_all = mod_all.transpose(1, 2, 0, 3).reshape(4, _COND_ROWS, 3, d)
    mod = jnp.stack([lax.dynamic_index_in_dim(mod_all, dev, axis=1, keepdims=False), mod_all[:, 8]], axis=1)
    full = {(it.key, it.layer): mat for it, mat in zip(first, first_mats)}
    na_comms = dict(w_in_fwd=_gather_comm(shards_of(na_in), na_in), pool_fwd=_gather_comm(shards_of(na_out), na_out))
    late_comm = _gather_comm(shards_of(late), late)

    def na_weights(carried):
        return dict(na_w_in=carried["w_in_fwd"][0], na_w_out=carried["pool_fwd"][0])

    def late_weights(mats):
        full.update({(it.key, it.layer): mat for it, mat in zip(late, mats)})
        return dict(pool_w_in=[full[("pool_w_in", j)] for j in range(2)],
                    pool_w_grp=[full[("pool_w_grp", j)] for j in range(2)],
                    pool_w_out=[full[("pool_w_out", j)] for j in range(2)],
                    conv_w_in=full[("conv_w_in", 0)], conv_w_out=full[("conv_w_out", 0)])

    small = small.reshape(4, -1)

    def whole(k):
        parts = _unpack(small, small_layout, k)
        return jnp.moveaxis(parts, 0, -2).reshape(parts.shape[1:-1] + (-1,))

    wts = dict(pool_w_in=[full[("pool_w_in", 0)]], pool_w_grp=[full[("pool_w_grp", 0)]],
               pool_w_out=[full[("pool_w_out", 0)]], pool_scale=whole(0), na_rpb=na_rpb[0], conv_dw=whole(1)[0],
               conv_db=whole(2))
    pos = jnp.stack([chip, ci]).astype(jnp.int32)

    def layer_grads(its, by_layer):
        pick = {"pool_w_in": "w_in", "pool_w_grp": "w_grp", "pool_w_out": "w_out", "na_w_in": "w_in",
                "na_w_out": "w_out", "conv_w_in": "w_in", "conv_w_out": "w_out"}
        return [by_layer[(it.key.split("_")[0], it.layer)][pick[it.key]] for it in its]

    def pair_sums(its, mats, tag):
        got = _pair_swap(mats, [(lambda ref, half, it=it: it.window(ref, half=half)) for it in its],
                         [_sds(it.sized(half=True), BF16) for it in its], f"pair_exchange_{tag}")
        return _pair_sums(mats, got, its, pos, f"pair_sum_{tag}")

    pairs = dict()

    def grad_comm(gr3, gr2):
        pairs["late"] = pair_sums(late, layer_grads(late, {("pool", 1): gr3, ("conv", 0): gr2}), "late")
        return _chip_exchange_comm(pairs["late"], late)

    def na_grad_comms(gr1):
        na = na_in + na_out
        pairs["na"] = pair_sums(na, layer_grads(na, {("na", 0): gr1}), "na")
        return dict(w_in_bwd=_chip_exchange_comm(pairs["na"][:1], na_in),
                    w_out_bwd=_chip_exchange_comm(pairs["na"][1:], na_out))

    res = _example_step(x[0], ctx[0], loss_target[0], mod, norm_g, final_g[None], wts, na_comms, na_weights,
                        late_comm, late_weights, grad_comm, na_grad_comms)
    g0, g1, g2, g3 = res["layers"]
    first_grads = layer_grads(first, {("pool", 0): g0})
    packed, layout = _pack([res["dfinal_g"], res["dnorm_g"], res["dmod"], g1["rpb"],
                            jnp.concatenate([g0["scale"], g3["scale"]], axis=0), g2["dw"], g2["db"],
                            res["loss"][0, :1]])
    first_got, (every,) = _run_comms(
        [_pair_swap_comm(first_grads, [(lambda ref, half, it=it: it.window(ref, half=half)) for it in first],
                         [_sds(it.sized(half=True), BF16) for it in first]), _all_gather_comm(packed, _AXES)],
        "pair_exchange_first")
    pairs["first"] = _pair_sums(first_grads, first_got, first, pos, "pair_sum_first")

    grads = dict()
    total = _sum_lead(every, "sum_vec_grads").reshape(-1)
    every = every.reshape(8, -1)
    grads["final_g"] = _unpack(total, layout, 0).reshape(final_g.shape)
    grads["norm_g"] = _unpack(total, layout, 1)
    grads["na_rpb"] = _unpack(total, layout, 3)[None]
    grads["pool_scale"] = chip_cols(_unpack(total, layout, 4), pool_scale.shape[-1])
    grads["conv_dw"] = chip_cols(_unpack(total, layout, 5), conv_dw.shape[-1])[None]
    grads["conv_db"] = chip_cols(_unpack(total, layout, 6), conv_db.shape[-1])
    dmod_sum = _unpack(total, layout, 2).reshape(4, 2, 3 * d)
    dmod_each = _unpack(every, layout, 2).reshape(8, 4, 2, 3 * d)
    grads["ada_b"] = dmod_sum[:, 0] + dmod_sum[:, 1]
    dm = jnp.concatenate([dmod_each[:, :, 0].transpose(1, 0, 2), dmod_sum[:, 1][:, None],
                          jnp.zeros((4, _COND_ROWS - 9, 3 * d), F32)], axis=1)
    dm_cols = chip_cols(dm, n_ada)
    grads["ada_w"] = _ada_w_grad(cond, dm_cols)
    dcond = _cond_grad(dm_cols, ada_w)[8].reshape(8, d // 8)
    first_slots, (dcond_all,) = _run_comms([_chip_exchange_comm(pairs["first"], first),
                                            _all_gather_comm(dcond, ("x", "y"))], "exchange_first")
    grads["c_ctx"] = _sum_lead(dcond_all, "sum_cond_grad").reshape(d) * _dsilu(c_ctx)

    slots = dict(zip(late, res["carried"]))
    slots.update(zip(first, first_slots))
    slots.update(zip(na_in + na_out, res["carried0"]["w_in_bwd"] + res["carried0"]["w_out_bwd"]))
    pair_of = dict(zip(late, pairs["late"]))
    pair_of.update(zip(first, pairs["first"]))
    pair_of.update(zip(na_in + na_out, pairs["na"]))
    reduced = _chip_sums([pair_of[it] for it in items], [slots[it] for it in items], items, pos, "chip_sum")
    theirs = _pair_swap(reduced, [lambda ref, half: ref] * len(items),
                        [_sds(t.shape, F32) for t in reduced], "pair_return")
    matrix_out = dict()
    for k in _GRAD_KEYS:
        idx = [i for i, it in enumerate(items) if it.key == k]
        res_k = _adamw_matrix(params[k], mom1[k], mom2[k], [reduced[i] for i in idx], [theirs[i] for i in idx],
                              items[idx[0]], pos, f"adamw_{k}")
        grads[k], matrix_out[k] = res_k[0], res_k[1:]

    outs = [[], [], []]
    for k in _WEIGHTS:
        step = matrix_out[k] if k in matrix_out else _adamw(params[k], grads[k], mom1[k], mom2[k], f"adamw_{k}")
        for lst, val in zip(outs, step):
            lst.append(val)
    loss = _unpack(total, layout, 7)[0]
    return (loss, res["grad_x"][None], *[grads[k].reshape(params[k].shape) for k in _WEIGHTS],
            *outs[0], *outs[1], *outs[2])
```

```python
import functools

import numpy as np
import jax
import jax.numpy as jnp
from jax import lax
from jax.experimental import pallas as pl
from jax.experimental.pallas import tpu as pltpu

F32 = jnp.float32
BF16 = jnp.bfloat16

EPS = 1e-6
GRID_W = 64
HEAD_DIM = 64
WIN_ROWS = 8
WIN_COLS = 16
POOL_WINDOWS = (2, 4, 8, 16)
Q_ROWS = 4
K_ROWS = 12
PAD_ROWS = 4
NEG = -1e30

ADAM_LR = 0.001
ADAM_B1 = 0.9
ADAM_B2 = 0.999
ADAM_EPS = 1e-08
ADAM_WD = 0.01
ADAM_STEP = 10

ROW_BLOCK = 256
VMEM_LIMIT = 56 * 1024 * 1024
ACT = BF16

MESH = pl.DeviceIdType.MESH
HBM_SPEC = pl.BlockSpec(memory_space=pltpu.HBM)


def _cparams(*sem):
    return pltpu.CompilerParams(dimension_semantics=sem or None, vmem_limit_bytes=VMEM_LIMIT)


def _sds(shape, dtype):
    return jax.ShapeDtypeStruct(tuple(shape), dtype)


def _call(body, args, *, grid, in_specs, out_specs, out_shape, name, scratch_shapes=(), comm=None):
    sem = ("arbitrary",) * len(grid)
    if comm is None:
        res = pl.pallas_call(body, grid=grid, in_specs=list(in_specs), out_specs=list(out_specs), out_shape=list(out_shape),
                             scratch_shapes=list(scratch_shapes), name=name, compiler_params=_cparams(*sem))(*args)
        return list(res), []
    n_in, n_out, n_scr = len(in_specs), len(out_specs), len(scratch_shapes)
    n_cin, n_cout = len(comm.ins), len(comm.outs)

    def carrying(*refs):
        ins, cin = refs[:n_in], refs[n_in:n_in + n_cin]
        outs = refs[n_in + n_cin:n_in + n_cin + n_out]
        cout = refs[n_in + n_cin + n_out:n_in + n_cin + n_out + n_cout]
        rest = refs[n_in + n_cin + n_out + n_cout:]
        scr, sems = rest[:n_scr], rest[n_scr:]
        first, last = True, True
        for ax, size in enumerate(grid):
            first = jnp.logical_and(first, pl.program_id(ax) == 0)
            last = jnp.logical_and(last, pl.program_id(ax) == size - 1)
        pl.when(first)(lambda: comm.start(cin, cout, sems))
        body(*ins, *outs, *scr)
        pl.when(last)(lambda: comm.finish(cin, cout, sems))

    res = pl.pallas_call(
        carrying, grid=grid, in_specs=list(in_specs) + [HBM_SPEC] * n_cin, out_specs=list(out_specs) + [HBM_SPEC] * n_cout,
        out_shape=list(out_shape) + list(comm.outs), scratch_shapes=list(scratch_shapes) + list(comm.sems), name=name,
        compiler_params=_cparams(*sem),
    )(*args, *comm.ins)
    return list(res[:n_out]), list(res[n_out:])


def _sigmoid(x):
    return 1.0 / (1.0 + jnp.exp(-x))


def _silu(x):
    return x * _sigmoid(x)


def _dsilu(x):
    s = _sigmoid(x)
    return s * (1.0 + x * (1.0 - s))


_DIMS = {
    "nn": (((1,), (0,)), ((), ())),
    "nt": (((1,), (1,)), ((), ())),
    "tn": (((0,), (0,)), ((), ())),
}


def _matmul(a, b, *, mode, grid, a_spec, b_spec, out_shapes, out_specs, name, nk=1,
            a_silu=False, exact=False, epilogue=None, extra=(), extra_specs=(), acc_shape=None):
    n_extra = len(extra)
    n_out = len(out_shapes)

    def body(*refs):
        a_ref, b_ref = refs[:2]
        ex = refs[2:2 + n_extra]
        outs = refs[2 + n_extra:2 + n_extra + n_out]
        av = a_ref[...]
        bv = b_ref[...]
        if a_silu:
            av = _silu(av.astype(F32))
        if exact:
            prod = lax.dot_general(av.astype(F32), bv.astype(F32), _DIMS[mode],
                                   precision=lax.Precision.HIGHEST, preferred_element_type=F32)
        else:
            prod = lax.dot_general(av.astype(BF16), bv.astype(BF16), _DIMS[mode], preferred_element_type=F32)

        def finish(res):
            if epilogue is None:
                outs[0][...] = res.astype(outs[0].dtype)
            elif epilogue == "bias":
                outs[0][...] = (res + ex[0][...]).astype(outs[0].dtype)
            else:
                outs[0][...] = res.astype(outs[0].dtype)
                outs[1][...] = ex[0][...] + ex[1][...] * res

        if nk == 1:
            finish(prod)
        else:
            acc = refs[-1]
            k = pl.program_id(len(grid) - 1)

            @pl.when(k == 0)
            def _():
                acc[...] = prod

            @pl.when(k > 0)
            def _():
                acc[...] += prod

            @pl.when(k == nk - 1)
            def _():
                finish(acc[...])

    scratch = [pltpu.VMEM(acc_shape, F32)] if nk > 1 else []
    sem = ("parallel",) * (len(grid) - 1) + ("arbitrary",)
    return pl.pallas_call(
        body, grid=grid, in_specs=[a_spec, b_spec, *extra_specs], out_specs=list(out_specs),
        out_shape=list(out_shapes), scratch_shapes=scratch, name=name, compiler_params=_cparams(*sem),
    )(a, b, *extra)


def _row_tile(rows):
    for t in (768, 512, 256):
        if rows % t == 0:
            return t
    return rows


def _mm_nn(a, b, name, out_dtype=F32, tn=1024):
    m, k = a.shape
    n = b.shape[1]
    tm = _row_tile(m)
    tn = min(tn, n)
    return _matmul(
        a, b, mode="nn", grid=(m // tm, n // tn),
        a_spec=pl.BlockSpec((tm, k), lambda i, j: (i, 0)), b_spec=pl.BlockSpec((k, tn), lambda i, j: (0, j)),
        out_shapes=[_sds((m, n), out_dtype)], out_specs=[pl.BlockSpec((tm, tn), lambda i, j: (i, j))], name=name)[0]


def _mm_out_resid(a, w_out, xres, gate, nxb, name):
    m, k = a.shape
    n = w_out.shape[1]
    tm = ROW_BLOCK
    seg = lambda i, j: (jnp.where(i >= nxb, 1, 0), 0, 0)
    return _matmul(
        a, w_out, mode="nn", grid=(m // tm, 1),
        a_spec=pl.BlockSpec((tm, k), lambda i, j: (i, 0)), b_spec=pl.BlockSpec((k, n), lambda i, j: (0, 0)),
        extra=(xres, gate), extra_specs=(pl.BlockSpec((tm, n), lambda i, j: (i, 0)), pl.BlockSpec((None, 1, n), seg)),
        out_shapes=[_sds((m, n), ACT), _sds((m, n), F32)],
        out_specs=[pl.BlockSpec((tm, n), lambda i, j: (i, 0))] * 2, epilogue="resid", name=name)


def _mm_nt(a, b, name, out_dtype=F32):
    m, n = a.shape
    k = b.shape[0]
    tm = _row_tile(m)
    return _matmul(
        a, b, mode="nt", grid=(m // tm, 1),
        a_spec=pl.BlockSpec((tm, n), lambda i, j: (i, 0)), b_spec=pl.BlockSpec((k, n), lambda i, j: (0, 0)),
        out_shapes=[_sds((m, k), out_dtype)], out_specs=[pl.BlockSpec((tm, k), lambda i, j: (i, 0))], name=name)[0]


def _mm_nt_parts(a, b, name):
    p, m, kp = a.shape
    d = b.shape[0]
    tm = _row_tile(m)
    return _matmul(
        a, b, mode="nt", grid=(m // tm, p), nk=p, acc_shape=(tm, d),
        a_spec=pl.BlockSpec((None, tm, kp), lambda i, q: (q, i, 0)), b_spec=pl.BlockSpec((d, kp), lambda i, q: (0, q)),
        out_shapes=[_sds((m, d), F32)], out_specs=[pl.BlockSpec((tm, d), lambda i, q: (i, 0))], name=name)[0]


def _mm_tn(a, b, name, out_dtype, tm=512):
    r, m = a.shape
    n = b.shape[1]
    tm = min(tm, m)
    tn = min(1024, n)
    return _matmul(
        a, b, mode="tn", grid=(m // tm, n // tn),
        a_spec=pl.BlockSpec((r, tm), lambda i, j: (0, i)), b_spec=pl.BlockSpec((r, tn), lambda i, j: (0, j)),
        out_shapes=[_sds((m, n), out_dtype)], out_specs=[pl.BlockSpec((tm, tn), lambda i, j: (i, j))], name=name)[0]


def _mm_tn_parts(a, b, name, out_dtype, tm=512):
    r, m = a.shape
    p, _, np_ = b.shape
    tm = min(tm, m)
    return _matmul(
        a, b, mode="tn", grid=(m // tm, p),
        a_spec=pl.BlockSpec((r, tm), lambda i, q: (0, i)), b_spec=pl.BlockSpec((None, r, np_), lambda i, q: (q, 0, 0)),
        out_shapes=[_sds((m, p * np_), out_dtype)], out_specs=[pl.BlockSpec((tm, np_), lambda i, q: (i, q))],
        name=name)[0]


def _seg_map(nxb):
    return lambda i: (jnp.where(i >= nxb, 1, 0), 0, 0)


def _normmod_fwd(x, g, scale, shift, nxb, name):
    rows, d = x.shape
    tr = ROW_BLOCK

    def body(x_ref, g_ref, sc_ref, sh_ref, h_ref, r_ref):
        xv = x_ref[...]
        r = lax.rsqrt(jnp.mean(xv * xv, axis=-1, keepdims=True) + EPS)
        h = (xv * r) * g_ref[...] * (1.0 + sc_ref[...]) + sh_ref[...]
        h_ref[...] = h.astype(BF16)
        r_ref[...] = r

    row = pl.BlockSpec((tr, d), lambda i: (i, 0))
    vec = pl.BlockSpec((None, 1, d), _seg_map(nxb))
    return pl.pallas_call(
        body, grid=(rows // tr,), in_specs=[row, pl.BlockSpec((1, d), lambda i: (0, 0)), vec, vec],
        out_specs=[row, pl.BlockSpec((tr, 1), lambda i: (i, 0))],
        out_shape=[_sds((rows, d), BF16), _sds((rows, 1), F32)], name=name, compiler_params=_cparams("parallel"),
    )(x, g, scale, shift)


def _normmod_bwd(dh, x, r, g, scale, dres, nxb, name):
    rows, d = x.shape
    tr = ROW_BLOCK
    nres = dres.shape[0] // tr
    nseg = scale.shape[0]

    def body(dh_ref, x_ref, r_ref, g_ref, sc_ref, dres_ref, dx_ref, dsh_ref, dge_ref):
        i = pl.program_id(0)
        dhv = dh_ref[...]
        rv = r_ref[...]
        xn = x_ref[...] * rv
        dxn = dhv * (g_ref[...] * (1.0 + sc_ref[...]))
        dx = rv * (dxn - xn * jnp.mean(dxn * xn, axis=-1, keepdims=True))

        @pl.when(i < nres)
        def _():
            dx_ref[...] = dx + dres_ref[...]

        @pl.when(i >= nres)
        def _():
            dx_ref[...] = dx

        first = jnp.logical_or(i == 0, i == nxb)
        s_dh = jnp.sum(dhv, axis=0, keepdims=True)
        s_ge = jnp.sum(dhv * xn, axis=0, keepdims=True)

        @pl.when(first)
        def _():
            dsh_ref[...] = s_dh
            dge_ref[...] = s_ge

        @pl.when(jnp.logical_not(first))
        def _():
            dsh_ref[...] += s_dh
            dge_ref[...] += s_ge

    row = pl.BlockSpec((tr, d), lambda i: (i, 0))
    vec = pl.BlockSpec((None, 1, d), _seg_map(nxb))
    return pl.pallas_call(
        body, grid=(rows // tr,),
        in_specs=[row, row, pl.BlockSpec((tr, 1), lambda i: (i, 0)), pl.BlockSpec((1, d), lambda i: (0, 0)), vec,
                  pl.BlockSpec((tr, d), lambda i: (jnp.minimum(i, nres - 1), 0))],
        out_specs=[row, vec, vec],
        out_shape=[_sds((rows, d), F32), _sds((nseg, 1, d), F32), _sds((nseg, 1, d), F32)],
        name=name, compiler_params=_cparams("arbitrary"),
    )(dh, x, r, g, scale, dres)


def _gate_bwd(dxo, yx, gate, nxb, name):
    rows, d = yx.shape
    tr = ROW_BLOCK
    nseg = gate.shape[0]

    def body(dx_ref, yx_ref, gt_ref, dyx_ref, dg_ref):
        i = pl.program_id(0)
        dxv = dx_ref[...]
        dyx_ref[...] = (dxv * gt_ref[...]).astype(BF16)
        s = jnp.sum(dxv * yx_ref[...].astype(F32), axis=0, keepdims=True)
        first = jnp.logical_or(i == 0, i == nxb)

        @pl.when(first)
        def _():
            dg_ref[...] = s

        @pl.when(jnp.logical_not(first))
        def _():
            dg_ref[...] += s

    row = pl.BlockSpec((tr, d), lambda i: (i, 0))
    vec = pl.BlockSpec((None, 1, d), _seg_map(nxb))
    return pl.pallas_call(
        body, grid=(rows // tr,), in_specs=[row, row, vec], out_specs=[row, vec],
        out_shape=[_sds((rows, d), BF16), _sds((nseg, 1, d), F32)], name=name, compiler_params=_cparams("arbitrary"),
    )(dxo, yx, gate)


def _row_vec(ref, is_ctx):
    return ref[0] if is_ctx is None else jnp.where(is_ctx, ref[1], ref[0])


def _ctx_rows(i, tm, nx, nseg):
    if nseg == 1:
        return None
    return i * tm + lax.broadcasted_iota(jnp.int32, (tm, 1), 0) >= nx


def _seg_sums(ref, val, is_ctx, first):
    if is_ctx is None:
        parts = [jnp.sum(val, axis=0, keepdims=True)]
    else:
        parts = [jnp.sum(jnp.where(is_ctx, 0.0, val), axis=0, keepdims=True),
                 jnp.sum(jnp.where(is_ctx, val, 0.0), axis=0, keepdims=True)]

    @pl.when(first)
    def _():
        for k, p in enumerate(parts):
            ref[k] = p

    @pl.when(jnp.logical_not(first))
    def _():
        for k, p in enumerate(parts):
            ref[k] += p


def _w_out_resid(a, w_out, xres, gate, nx, name):
    m, k = a.shape
    n = w_out.shape[1]
    nseg = gate.shape[0]
    tm = _row_tile(m)

    def body(a_ref, w_ref, x_ref, gt_ref, yx_ref, xo_ref):
        yx = jnp.dot(a_ref[...], w_ref[...], preferred_element_type=F32)
        yx_ref[...] = yx.astype(ACT)
        xo_ref[...] = x_ref[...] + _row_vec(gt_ref, _ctx_rows(pl.program_id(0), tm, nx, nseg)) * yx

    row = pl.BlockSpec((tm, n), lambda i: (i, 0))
    return pl.pallas_call(
        body, grid=(m // tm,),
        in_specs=[pl.BlockSpec((tm, k), lambda i: (i, 0)), pl.BlockSpec((k, n), lambda i: (0, 0)), row,
                  pl.BlockSpec((nseg, 1, n), lambda i: (0, 0, 0))],
        out_specs=[row, row], out_shape=[_sds((m, n), ACT), _sds((m, n), F32)],
        name=name, compiler_params=_cparams("parallel"),
    )(a, w_out, xres, gate)


def _norm_w_in(x, g, scale, shift, w_in, nx, name, comm=None):
    rows, d = x.shape
    n = w_in.shape[1]
    nseg = scale.shape[0]
    tm = _row_tile(rows)
    tn = min(1024, n)

    def body(x_ref, g_ref, sc_ref, sh_ref, w_ref, h_ref, r_ref, p_ref):
        i, j = pl.program_id(0), pl.program_id(1)

        @pl.when(j == 0)
        def _():
            xv = x_ref[...]
            r = lax.rsqrt(jnp.mean(xv * xv, axis=-1, keepdims=True) + EPS)
            is_ctx = _ctx_rows(i, tm, nx, nseg)
            h = (xv * r) * g_ref[...] * (1.0 + _row_vec(sc_ref, is_ctx)) + _row_vec(sh_ref, is_ctx)
            h_ref[...] = h.astype(BF16)
            r_ref[...] = r

        p_ref[...] = jnp.dot(h_ref[...], w_ref[...], preferred_element_type=F32).astype(ACT)

    vec = pl.BlockSpec((nseg, 1, d), lambda i, j: (0, 0, 0))
    return _call(
        body, (x, g, scale, shift, w_in), grid=(rows // tm, n // tn),
        in_specs=[pl.BlockSpec((tm, d), lambda i, j: (i, 0)), pl.BlockSpec((1, d), lambda i, j: (0, 0)), vec, vec,
                  pl.BlockSpec((d, tn), lambda i, j: (0, j))],
        out_specs=[pl.BlockSpec((tm, d), lambda i, j: (i, 0)), pl.BlockSpec((tm, 1), lambda i, j: (i, 0)),
                   pl.BlockSpec((tm, tn), lambda i, j: (i, j))],
        out_shape=[_sds((rows, d), BF16), _sds((rows, 1), F32), _sds((rows, n), ACT)], name=name, comm=comm)


def _gate_w_out_bwd(dxo, yx, gate, w_out, nx, name, comm=None):
    rows, d = yx.shape
    w = w_out.shape[0]
    nseg = gate.shape[0]
    tm = _row_tile(rows)

    def body(dx_ref, yx_ref, gt_ref, w_ref, dyx_ref, da_ref, dg_ref):
        i = pl.program_id(0)
        is_ctx = _ctx_rows(i, tm, nx, nseg)
        dxv = dx_ref[...]
        dyx = (dxv * _row_vec(gt_ref, is_ctx)).astype(BF16)
        dyx_ref[...] = dyx
        da_ref[...] = lax.dot_general(dyx, w_ref[...], _DIMS["nt"], preferred_element_type=F32).astype(ACT)
        _seg_sums(dg_ref, dxv * yx_ref[...].astype(F32), is_ctx, i == 0)

    row = pl.BlockSpec((tm, d), lambda i: (i, 0))
    vec = pl.BlockSpec((nseg, 1, d), lambda i: (0, 0, 0))
    return _call(
        body, (dxo, yx, gate, w_out), grid=(rows // tm,),
        in_specs=[row, row, vec, pl.BlockSpec((w, d), lambda i: (0, 0))],
        out_specs=[row, pl.BlockSpec((tm, w), lambda i: (i, 0)), vec],
        out_shape=[_sds((rows, d), BF16), _sds((rows, w), ACT), _sds((nseg, 1, d), F32)], name=name, comm=comm)


def _w_in_bwd_norm(dparts, w_in, x, r, g, scale, dres, nx, name, comm=None):
    np_, rows, kp = dparts.shape
    d = w_in.shape[0]
    nseg = scale.shape[0]
    tm = _row_tile(rows)
    nsub = tm // ROW_BLOCK
    nres_blocks = dres.shape[0] // ROW_BLOCK

    def body(dp_ref, w_ref, x_ref, r_ref, g_ref, sc_ref, *rest):
        dres_refs = rest[:nsub]
        dx_ref, dsh_ref, dge_ref, acc = rest[nsub:]
        i, k = pl.program_id(0), pl.program_id(1)
        prod = lax.dot_general(dp_ref[...], w_ref[...], _DIMS["nt"], preferred_element_type=F32)

        @pl.when(k == 0)
        def _():
            acc[...] = prod

        @pl.when(k > 0)
        def _():
            acc[...] += prod

        @pl.when(k == np_ - 1)
        def _():
            is_ctx = _ctx_rows(i, tm, nx, nseg)
            dhv = acc[...]
            rv = r_ref[...]
            xn = x_ref[...] * rv
            dxn = dhv * (g_ref[...] * (1.0 + _row_vec(sc_ref, is_ctx)))
            dx = rv * (dxn - xn * jnp.mean(dxn * xn, axis=-1, keepdims=True))
            for s in range(nsub):
                piece = slice(s * ROW_BLOCK, (s + 1) * ROW_BLOCK)
                res = dres_refs[s][...]
                if nres_blocks * ROW_BLOCK < rows:
                    res = jnp.where(i * nsub + s < nres_blocks, res, 0.0)
                dx_ref[piece, :] = dx[piece, :] + res
            _seg_sums(dsh_ref, dhv, is_ctx, i == 0)
            _seg_sums(dge_ref, dhv * xn, is_ctx, i == 0)

    row = pl.BlockSpec((tm, d), lambda i, k: (i, 0))
    vec = pl.BlockSpec((nseg, 1, d), lambda i, k: (0, 0, 0))
    return _call(
        body, (dparts, w_in, x, r, g, scale, *([dres] * nsub)), grid=(rows // tm, np_),
        in_specs=[pl.BlockSpec((None, tm, kp), lambda i, k: (k, i, 0)), pl.BlockSpec((d, kp), lambda i, k: (0, k)),
                  row, pl.BlockSpec((tm, 1), lambda i, k: (i, 0)), pl.BlockSpec((1, d), lambda i, k: (0, 0)), vec]
        + [pl.BlockSpec((ROW_BLOCK, d), (lambda i, k, s=s: (jnp.minimum(i * nsub + s, nres_blocks - 1), 0)))
           for s in range(nsub)],
        out_specs=[row, vec, vec],
        out_shape=[_sds((rows, d), F32), _sds((nseg, 1, d), F32), _sds((nseg, 1, d), F32)],
        scratch_shapes=[pltpu.VMEM((tm, d), F32)], name=name, comm=comm)


_PAD_TOP = 16
_PAD_BOT = 32


def _window_sum(buf, xv, lo, n):
    t = xv.shape[0]
    c = xv.shape[1]
    tp = t + _PAD_TOP + _PAD_BOT
    buf[pl.ds(0, _PAD_TOP), :] = jnp.zeros((_PAD_TOP, c), F32)
    buf[pl.ds(_PAD_TOP, t), :] = xv
    buf[pl.ds(_PAD_TOP + t, _PAD_BOT), :] = jnp.zeros((_PAD_BOT, c), F32)
    p = buf[...]
    k = 1
    while k < n:
        p = p + pltpu.roll(p, tp - k, 0)
        k *= 2
    if lo:
        p = pltpu.roll(p, -lo, 0)
    buf[...] = p
    return buf[pl.ds(_PAD_TOP, t), :]


def _window_count(t, half):
    pos = lax.broadcasted_iota(jnp.int32, (t, 1), 0)
    return (jnp.minimum(pos + half, t) - jnp.maximum(pos - half, 0)).astype(F32)


def _segments(rows, nx):
    return [(0, nx)] + ([(nx, rows - nx)] if rows > nx else [])


def _pool_fwd(uv, nx, name):
    rows = uv.shape[0]
    w = uv.shape[1] // 2
    cb = 128
    per_group = w // len(POOL_WINDOWS) // cb
    segs = _segments(rows, nx)

    def body(u_ref, z_ref, *bufs):
        j = pl.program_id(0)
        for gi, win in enumerate(POOL_WINDOWS):
            half = win // 2

            @pl.when(jnp.logical_and(j >= gi * per_group, j < (gi + 1) * per_group))
            def _():
                for (start, length), buf in zip(segs, bufs):
                    uvv = u_ref[pl.ds(start, length), :].astype(F32)
                    s = _window_sum(buf, uvv, -half, win)
                    z_ref[pl.ds(start, length), :] = (s / _window_count(length, half) - uvv).astype(BF16)

    scratch = [pltpu.VMEM((length + _PAD_TOP + _PAD_BOT, cb), F32) for _, length in segs]
    return pl.pallas_call(
        body, grid=(w // cb,), in_specs=[pl.BlockSpec((rows, cb), lambda j: (0, j))],
        out_specs=pl.BlockSpec((rows, cb), lambda j: (0, j)), out_shape=_sds((rows, w), BF16),
        scratch_shapes=scratch, name=name, compiler_params=_cparams("parallel"),
    )(uv)


def _pool_bwd(dz, dgt, nx, name):
    rows, w = dz.shape
    cb = 128
    per_group = w // len(POOL_WINDOWS) // cb
    segs = _segments(rows, nx)

    def body(dz_ref, dgt_ref, o_ref, *bufs):
        j = pl.program_id(0)
        o_ref[1] = dgt_ref[...]
        for gi, win in enumerate(POOL_WINDOWS):
            half = win // 2

            @pl.when(jnp.logical_and(j >= gi * per_group, j < (gi + 1) * per_group))
            def _():
                for (start, length), buf in zip(segs, bufs):
                    dzv = dz_ref[pl.ds(start, length), :].astype(F32)
                    s = _window_sum(buf, dzv / _window_count(length, half), 1 - half, win)
                    o_ref[0, pl.ds(start, length), :] = (s - dzv).astype(BF16)

    scratch = [pltpu.VMEM((length + _PAD_TOP + _PAD_BOT, cb), F32) for _, length in segs]
    col = pl.BlockSpec((rows, cb), lambda j: (0, j))
    return pl.pallas_call(
        body, grid=(w // cb,), in_specs=[col, col], out_specs=pl.BlockSpec((2, rows, cb), lambda j: (0, 0, j)),
        out_shape=_sds((2, rows, w), BF16), scratch_shapes=scratch, name=name, compiler_params=_cparams("parallel"),
    )(dz, dgt)


def _grp_fwd(z, w_grp, uv, scale, name):
    rows, w = z.shape
    ng, gc, _ = w_grp.shape
    tm = _row_tile(rows)

    def body(z_ref, w_ref, gt_ref, sc_ref, mx_ref, a_ref):
        mixed = jnp.dot(z_ref[...], w_ref[...], preferred_element_type=F32)
        mx_ref[...] = mixed.astype(ACT)
        a_ref[...] = (mixed * sc_ref[...] * _silu(gt_ref[...].astype(F32))).astype(BF16)

    blk = pl.BlockSpec((tm, gc), lambda g, i: (i, g))
    return pl.pallas_call(
        body, grid=(ng, rows // tm),
        in_specs=[blk, pl.BlockSpec((None, gc, gc), lambda g, i: (g, 0, 0)),
                  pl.BlockSpec((tm, gc), lambda g, i: (i, ng + g)), pl.BlockSpec((1, gc), lambda g, i: (0, g))],
        out_specs=[blk, blk], out_shape=[_sds((rows, w), ACT), _sds((rows, w), BF16)],
        name=name, compiler_params=_cparams("parallel", "parallel"),
    )(z, w_grp, uv, scale)


def _grp_bwd(da, mixed, uv, scale, w_grp, name):
    rows, w = da.shape
    ng, gc, _ = w_grp.shape
    tm = _row_tile(rows)

    def body(da_ref, mx_ref, gt_ref, sc_ref, w_ref, dm_ref, dz_ref, dgt_ref, dsc_ref):
        i = pl.program_id(1)
        dav = da_ref[...].astype(F32)
        mixed = mx_ref[...].astype(F32)
        gt = gt_ref[...].astype(F32)
        sg = _silu(gt)
        sc = sc_ref[...]
        dm = (dav * sc * sg).astype(BF16)
        dm_ref[...] = dm
        dz_ref[...] = lax.dot_general(dm, w_ref[...], _DIMS["nt"], preferred_element_type=F32).astype(ACT)
        dgt_ref[...] = (dav * mixed * sc * _dsilu(gt)).astype(BF16)
        s = jnp.sum(dav * mixed * sg, axis=0, keepdims=True)

        @pl.when(i == 0)
        def _():
            dsc_ref[...] = s

        @pl.when(i > 0)
        def _():
            dsc_ref[...] += s

    blk = pl.BlockSpec((tm, gc), lambda g, i: (i, g))
    vec = pl.BlockSpec((1, gc), lambda g, i: (0, g))
    return pl.pallas_call(
        body, grid=(ng, rows // tm),
        in_specs=[blk, blk, pl.BlockSpec((tm, gc), lambda g, i: (i, ng + g)), vec,
                  pl.BlockSpec((None, gc, gc), lambda g, i: (g, 0, 0))],
        out_specs=[blk, blk, blk, vec],
        out_shape=[_sds((rows, w), BF16), _sds((rows, w), ACT), _sds((rows, w), BF16), _sds((1, w), F32)],
        name=name, compiler_params=_cparams("parallel", "arbitrary"),
    )(da, mixed, uv, scale, w_grp)


def _pool_scratch(rows, nx, cols):
    return [pltpu.VMEM((length + _PAD_TOP + _PAD_BOT, cols), F32) for _, length in _segments(rows, nx)]


def _per_group(g, fn):
    for gi, win in enumerate(POOL_WINDOWS):
        pl.when(g == gi)(functools.partial(fn, win))


def _pool_grp_fwd(uv, w_grp, scale, nx, name, comm=None):
    rows = uv.shape[0]
    ng, gc, _ = w_grp.shape
    w = ng * gc
    segs = _segments(rows, nx)

    def body(u_ref, gt_ref, w_ref, sc_ref, z_ref, mx_ref, a_ref, *bufs):
        def pool(win):
            half = win // 2
            for (start, length), buf in zip(segs, bufs):
                uvv = u_ref[pl.ds(start, length), :].astype(F32)
                s = _window_sum(buf, uvv, -half, win)
                z_ref[pl.ds(start, length), :] = (s / _window_count(length, half) - uvv).astype(BF16)

        _per_group(pl.program_id(0), pool)
        mixed = jnp.dot(z_ref[...], w_ref[...], preferred_element_type=F32)
        mx_ref[...] = mixed.astype(ACT)
        a_ref[...] = (mixed * sc_ref[...] * _silu(gt_ref[...].astype(F32))).astype(BF16)

    col = pl.BlockSpec((rows, gc), lambda g: (0, g))
    return _call(
        body, (uv, uv, w_grp, scale), grid=(ng,),
        in_specs=[col, pl.BlockSpec((rows, gc), lambda g: (0, ng + g)), pl.BlockSpec((None, gc, gc), lambda g: (g, 0, 0)),
                  pl.BlockSpec((1, gc), lambda g: (0, g))],
        out_specs=[col, col, col], out_shape=[_sds((rows, w), BF16), _sds((rows, w), ACT), _sds((rows, w), BF16)],
        scratch_shapes=_pool_scratch(rows, nx, gc), name=name, comm=comm)


def _pool_grp_bwd(da, mixed, uv, scale, w_grp, nx, name):
    rows, w = da.shape
    ng, gc, _ = w_grp.shape
    segs = _segments(rows, nx)

    def body(da_ref, mx_ref, gt_ref, sc_ref, w_ref, dm_ref, duv_ref, dsc_ref, dz_ref, *bufs):
        dav = da_ref[...].astype(F32)
        mixed = mx_ref[...].astype(F32)
        gt = gt_ref[...].astype(F32)
        sg = _silu(gt)
        sc = sc_ref[...]
        dm = (dav * sc * sg).astype(BF16)
        dm_ref[...] = dm
        dz_ref[...] = lax.dot_general(dm, w_ref[...], _DIMS["nt"], preferred_element_type=F32)
        duv_ref[1] = (dav * mixed * sc * _dsilu(gt)).astype(BF16)
        dsc_ref[...] = jnp.sum(dav * mixed * sg, axis=0, keepdims=True)

        def unpool(win):
            half = win // 2
            for (start, length), buf in zip(segs, bufs):
                dzv = dz_ref[pl.ds(start, length), :]
                s = _window_sum(buf, dzv / _window_count(length, half), 1 - half, win)
                duv_ref[0, pl.ds(start, length), :] = (s - dzv).astype(BF16)

        _per_group(pl.program_id(0), unpool)

    col = pl.BlockSpec((rows, gc), lambda g: (0, g))
    vec = pl.BlockSpec((1, gc), lambda g: (0, g))
    return pl.pallas_call(
        body, grid=(ng,),
        in_specs=[col, col, pl.BlockSpec((rows, gc), lambda g: (0, ng + g)), vec,
                  pl.BlockSpec((None, gc, gc), lambda g: (g, 0, 0))],
        out_specs=[col, pl.BlockSpec((2, rows, gc), lambda g: (0, 0, g)), vec],
        out_shape=[_sds((rows, w), BF16), _sds((2, rows, w), BF16), _sds((1, w), F32)],
        scratch_shapes=[pltpu.VMEM((rows, gc), F32)] + _pool_scratch(rows, nx, gc),
        name=name, compiler_params=_cparams("parallel"),
    )(da, mixed, uv, scale, w_grp)


def _grp_wgrad(z, dm, ng, name, out_dtype):
    rows, w = z.shape
    gc = w // ng

    def body(z_ref, dm_ref, o_ref):
        o_ref[...] = lax.dot_general(z_ref[...], dm_ref[...], _DIMS["tn"],
                                     preferred_element_type=F32).astype(o_ref.dtype)

    blk = pl.BlockSpec((rows, gc), lambda g: (0, g))
    return pl.pallas_call(
        body, grid=(ng,), in_specs=[blk, blk], out_specs=pl.BlockSpec((None, gc, gc), lambda g: (g, 0, 0)),
        out_shape=_sds((ng, gc, gc), out_dtype), name=name, compiler_params=_cparams("parallel"),
    )(z, dm)


def _shift_rows(v, by):
    t = v.shape[0]
    pos = lax.broadcasted_iota(jnp.int32, v.shape, 0)
    rolled = pltpu.roll(v, by % t, 0)
    keep = pos >= by if by > 0 else pos < t + by
    return jnp.where(keep, rolled, 0.0)


def _conv_specs(t, w, cb):
    return [pl.BlockSpec((t, cb), (lambda j, q=q: (0, q * (w // cb) + j))) for q in range(4)]


def _conv_fwd(p4, dw, db, name):
    t = p4.shape[0]
    w = p4.shape[1] // 4
    cb = 128

    def body(bg_ref, cg_ref, v_ref, g_ref, dw_ref, db_ref, a_ref):
        tv = cg_ref[...].astype(F32) * v_ref[...].astype(F32)
        conv = (dw_ref[0:1, :] * _shift_rows(tv, 1) + dw_ref[1:2, :] * tv + dw_ref[2:3, :] * _shift_rows(tv, -1)
                + db_ref[...])
        a_ref[...] = (bg_ref[...].astype(F32) * conv * _silu(g_ref[...].astype(F32))).astype(BF16)

    return pl.pallas_call(
        body, grid=(w // cb,),
        in_specs=_conv_specs(t, w, cb) + [pl.BlockSpec((3, cb), lambda j: (0, j)), pl.BlockSpec((1, cb), lambda j: (0, j))],
        out_specs=pl.BlockSpec((t, cb), lambda j: (0, j)), out_shape=_sds((t, w), BF16),
        name=name, compiler_params=_cparams("parallel"),
    )(p4, p4, p4, p4, dw, db)


def _conv_bwd(da, p4, dw, db, name):
    t, w = da.shape
    cb = 128

    def body(da_ref, bg_ref, cg_ref, v_ref, g_ref, dw_ref, db_ref, d4_ref, ddw_ref, ddb_ref):
        cg = cg_ref[...].astype(F32)
        vv = v_ref[...].astype(F32)
        bg = bg_ref[...].astype(F32)
        gv = g_ref[...].astype(F32)
        tv = cg * vv
        tm1 = _shift_rows(tv, 1)
        tp1 = _shift_rows(tv, -1)
        w0, w1, w2 = dw_ref[0:1, :], dw_ref[1:2, :], dw_ref[2:3, :]
        conv = w0 * tm1 + w1 * tv + w2 * tp1 + db_ref[...]
        y = bg * conv
        dav = da_ref[...].astype(F32)
        dy = dav * _silu(gv)
        d4_ref[3] = (dav * y * _dsilu(gv)).astype(BF16)
        d4_ref[0] = (dy * conv).astype(BF16)
        dconv = dy * bg
        ddb_ref[...] = jnp.sum(dconv, axis=0, keepdims=True)
        ddw_ref[0:1, :] = jnp.sum(dconv * tm1, axis=0, keepdims=True)
        ddw_ref[1:2, :] = jnp.sum(dconv * tv, axis=0, keepdims=True)
        ddw_ref[2:3, :] = jnp.sum(dconv * tp1, axis=0, keepdims=True)
        dt = w0 * _shift_rows(dconv, -1) + w1 * dconv + w2 * _shift_rows(dconv, 1)
        d4_ref[1] = (dt * vv).astype(BF16)
        d4_ref[2] = (dt * cg).astype(BF16)

    col = pl.BlockSpec((t, cb), lambda j: (0, j))
    tap = pl.BlockSpec((3, cb), lambda j: (0, j))
    bias = pl.BlockSpec((1, cb), lambda j: (0, j))
    return pl.pallas_call(
        body, grid=(w // cb,), in_specs=[col] + _conv_specs(t, w, cb) + [tap, bias],
        out_specs=[pl.BlockSpec((4, t, cb), lambda j: (0, 0, j)), tap, bias],
        out_shape=[_sds((4, t, w), BF16), _sds((3, w), F32), _sds((1, w), F32)],
        name=name, compiler_params=_cparams("parallel"),
    )(da, p4, p4, p4, p4, dw, db)


def _attn_mask():
    qn, kn = Q_ROWS * GRID_W, K_ROWS * GRID_W
    qr, qc = np.divmod(np.arange(qn), GRID_W)
    kr, kc = np.divmod(np.arange(kn), GRID_W)
    col0 = np.clip(qc - WIN_COLS // 2, 0, GRID_W - WIN_COLS)
    col_ok = (kc[None, :] >= col0[:, None]) & (kc[None, :] < col0[:, None] + WIN_COLS)
    first = np.zeros(qn, np.int64)
    last = np.full(qn, K_ROWS - WIN_ROWS)
    out = []
    for row0 in (first, qr, last):
        row_ok = (kr[None, :] >= row0[:, None]) & (kr[None, :] < row0[:, None] + WIN_ROWS)
        out.append(np.where(row_ok & col_ok, 0.0, NEG))
    return jnp.asarray(np.stack(out), F32)


_KW = K_ROWS * GRID_W
_QB = Q_ROWS * GRID_W
_PAIR = 2 * HEAD_DIM
_N_DR = 2 * WIN_ROWS - 1
_N_DC = 2 * WIN_COLS - 1
_RP_ROWS = 24
_N_TILES = _N_DR + 1
_BIAS_BASE = (WIN_ROWS - 1, WIN_ROWS // 2 - 1, -1)


class _Comm:
    def __init__(self, ins, outs, sems, start, finish):
        self.ins, self.outs, self.sems, self.start, self.finish = list(ins), list(outs), list(sems), start, finish


def _bias_pieces(cls):
    out = []
    for qr in range(Q_ROWS):
        for kr in range(0, K_ROWS, 2):
            tile = _BIAS_BASE[cls] - qr + kr + 1
            out.append((qr, kr, tile if 0 <= tile < _N_TILES else None))
    return out


def _toeplitz_pair(left_row, right_row):
    lane = lax.broadcasted_iota(jnp.int32, (GRID_W, _PAIR), 1)
    shape = (GRID_W, _PAIR)
    left = pltpu.roll(jnp.broadcast_to(left_row, shape), _PAIR - (WIN_COLS - 1), 1, stride=1, stride_axis=0)
    right = pltpu.roll(jnp.broadcast_to(right_row, shape), GRID_W - (WIN_COLS - 1), 1, stride=1, stride_axis=0)
    return jnp.where(lane < GRID_W, left, right)


def _build_tiles(tiles_ref, rp_ref):
    for h in range(2):
        for t in range(_N_TILES):
            tiles_ref[h, t] = _toeplitz_pair(rp_ref[h, t:t + 1, :], rp_ref[h, t + 1:t + 2, :])


def _block_class(b, nblk, fn, entering=False):
    interior = (b == 1) if entering else jnp.logical_and(b > 0, b < nblk - 1)
    for cls, cond in enumerate((b == 0, interior, b == nblk - 1)):
        pl.when(cond)(functools.partial(fn, cls))


def _attn_geometry(p4, nx):
    rows = p4.shape[0]
    w = p4.shape[1] // 4
    nhp = w // _PAIR
    nblk = nx // _QB
    qspec = lambda col: pl.BlockSpec((_QB, _PAIR), lambda hp, b: (b, col * nhp + hp))
    kspec = lambda col: pl.BlockSpec((rows, _PAIR), lambda hp, b: (0, col * nhp + hp))
    tspec = pl.BlockSpec((2, _RP_ROWS, _PAIR), lambda hp, b: (hp, 0, 0))
    mspec = pl.BlockSpec((None, _QB, _KW), lambda hp, b: (jnp.where(b == 0, 0, jnp.where(b == nblk - 1, 2, 1)), 0, 0))
    lspec = pl.BlockSpec((None, _QB, 2), lambda hp, b: (hp, b, 0))
    ospec = pl.BlockSpec((_QB, _PAIR), lambda hp, b: (b, hp))
    return rows, w, nhp, nblk, qspec, kspec, tspec, mspec, lspec, ospec


def _window_start(b, nx):
    return pl.multiple_of(jnp.clip(b * _QB - PAD_ROWS * GRID_W, 0, nx - _KW), _QB)


def _load_bias(bias_ref, tiles_ref, rp_ref, m_ref, b, nblk):
    pl.when(b == 0)(lambda: _build_tiles(tiles_ref, rp_ref))

    def fill(cls):
        for h in range(2):
            for qr, kr, tile in _bias_pieces(cls):
                rows = slice(qr * GRID_W, (qr + 1) * GRID_W)
                cols = slice(kr * GRID_W, (kr + 2) * GRID_W)
                m = m_ref[rows, cols]
                bias_ref[h, rows, cols] = m if tile is None else tiles_ref[h, tile] + m

    _block_class(b, nblk, fill, entering=True)


def _attn_fwd(p4, rp, mask, nx, name, comm=None):
    rows, w, nhp, nblk, qspec, kspec, tspec, mspec, lspec, ospec = _attn_geometry(p4, nx)
    n_ctx = rows - nx
    n_cin, n_cout = (len(comm.ins), len(comm.outs)) if comm else (0, 0)

    def body(*refs):
        q_ref, k_ref, v_ref, g_ref, rp_ref, m_ref = refs[:6]
        cin = refs[6:6 + n_cin]
        a_ref, o_ref, lse_ref = refs[6 + n_cin:9 + n_cin]
        cout = refs[9 + n_cin:9 + n_cin + n_cout]
        bias_ref, tiles_ref = refs[9 + n_cin + n_cout:11 + n_cin + n_cout]
        sems = refs[11 + n_cin + n_cout:]
        hp, b = pl.program_id(0), pl.program_id(1)
        if comm:
            pl.when(jnp.logical_and(hp == 0, b == 0))(lambda: comm.start(cin, cout, sems))
        start = _window_start(b, nx)
        _load_bias(bias_ref, tiles_ref, rp_ref, m_ref, b, nblk)
        qf = q_ref[...].astype(F32) * HEAD_DIM ** -0.5
        kw = k_ref[pl.ds(start, _KW), :].astype(BF16)
        vw = v_ref[pl.ds(start, _KW), :].astype(BF16)
        kcv = k_ref[pl.ds(nx, n_ctx), :].astype(BF16)
        vcv = v_ref[pl.ds(nx, n_ctx), :].astype(BF16)
        lane = lax.broadcasted_iota(jnp.int32, (1, _PAIR), 1)
        outs, lses = [], []
        for h in range(2):
            mine = (lane >= HEAD_DIM) if h else (lane < HEAD_DIM)
            qm = jnp.where(mine, qf, 0.0).astype(BF16)
            s_loc = lax.dot_general(qm, kw, _DIMS["nt"], preferred_element_type=F32) + bias_ref[h]
            s_ctx = lax.dot_general(qm, kcv, _DIMS["nt"], preferred_element_type=F32)
            mx = jnp.maximum(jnp.max(s_loc, axis=-1, keepdims=True), jnp.max(s_ctx, axis=-1, keepdims=True))
            p_loc = jnp.exp(s_loc - mx)
            p_ctx = jnp.exp(s_ctx - mx)
            den = jnp.sum(p_loc, axis=-1, keepdims=True) + jnp.sum(p_ctx, axis=-1, keepdims=True)
            o = jnp.dot(p_loc.astype(BF16), vw, preferred_element_type=F32)
            o = o + jnp.dot(p_ctx.astype(BF16), vcv, preferred_element_type=F32)
            outs.append(o * (1.0 / den))
            lses.append(mx + jnp.log(den))
        o = jnp.where(lane < HEAD_DIM, outs[0], outs[1])
        o_ref[...] = o.astype(ACT)
        a_ref[...] = (o * _silu(g_ref[...].astype(F32))).astype(BF16)
        col = lax.broadcasted_iota(jnp.int32, (1, 2), 1)
        lse_ref[...] = jnp.where(col == 0, lses[0], lses[1])
        if comm:
            pl.when(jnp.logical_and(hp == nhp - 1, b == nblk - 1))(lambda: comm.finish(cin, cout, sems))

    res = pl.pallas_call(
        body, grid=(nhp, nblk),
        in_specs=[qspec(0), kspec(1), kspec(2), qspec(3), tspec, mspec] + [HBM_SPEC] * n_cin,
        out_specs=[ospec, ospec, lspec] + [HBM_SPEC] * n_cout,
        out_shape=[_sds((nx, w), BF16), _sds((nx, w), ACT), _sds((nhp, nx, 2), F32)] + (comm.outs if comm else []),
        scratch_shapes=[pltpu.VMEM((2, _QB, _KW), F32), pltpu.VMEM((2, _N_TILES, GRID_W, _PAIR), F32)]
        + (comm.sems if comm else []),
        name=name, compiler_params=_cparams("arbitrary", "arbitrary"),
    )(p4, p4, p4, p4, rp, mask, *(comm.ins if comm else []))
    return res[:3], res[3:]


def _fold_tiles(dtiles_ref, drp_ref):
    shape = (GRID_W, _PAIR)
    lane = lax.broadcasted_iota(jnp.int32, shape, 1)
    flip = (lax.broadcasted_iota(jnp.int32, (_PAIR, _PAIR), 0)
            + lax.broadcasted_iota(jnp.int32, (_PAIR, _PAIR), 1) == _PAIR - 1).astype(F32)
    drp_ref[...] = jnp.zeros(drp_ref.shape, F32)
    for h in range(2):
        stack = dtiles_ref[h].reshape(_N_TILES * GRID_W, _PAIR)
        rev = jnp.dot(stack, flip, precision=lax.Precision.HIGHEST, preferred_element_type=F32)
        for t in range(_N_TILES):
            tile = rev[t * GRID_W:(t + 1) * GRID_W, :]
            for side in (0, 1):
                shift = _PAIR - GRID_W * side - (WIN_COLS - 1)
                half = jnp.where((lane < GRID_W) if side else (lane >= GRID_W), tile, 0.0)
                diag = pltpu.roll(half, shift, 1, stride=1, stride_axis=0)
                drp_ref[h, t + side:t + side + 1, :] += jnp.sum(diag, axis=0, keepdims=True)


def _attn_bwd(p4, rp, mask, o, lse, da, nx, name, comm=None):
    rows, w, nhp, nblk, qspec, kspec, tspec, mspec, lspec, ospec = _attn_geometry(p4, nx)
    n_ctx = rows - nx
    n_cin, n_cout = (len(comm.ins), len(comm.outs)) if comm else (0, 0)

    def body(*refs):
        q_ref, k_ref, v_ref, g_ref, rp_ref, m_ref, o_ref, lse_ref, da_ref = refs[:9]
        cin = refs[9:9 + n_cin]
        d4_ref, drp_ref = refs[9 + n_cin:11 + n_cin]
        cout = refs[11 + n_cin:11 + n_cin + n_cout]
        bias_ref, tiles_ref, ds_ref, dtiles_ref, dk_ref, dv_ref = refs[11 + n_cin + n_cout:17 + n_cin + n_cout]
        sems = refs[17 + n_cin + n_cout:]
        hp, b = pl.program_id(0), pl.program_id(1)
        if comm:
            pl.when(jnp.logical_and(hp == 0, b == 0))(lambda: comm.start(cin, cout, sems))
        start = _window_start(b, nx)
        here = pl.multiple_of(b * _QB, _QB)

        @pl.when(b == 0)
        def _():
            dk_ref[...] = jnp.zeros(dk_ref.shape, F32)
            dv_ref[...] = jnp.zeros(dv_ref.shape, F32)
            dtiles_ref[...] = jnp.zeros(dtiles_ref.shape, F32)
            d4_ref[0, pl.ds(nx, n_ctx), :] = jnp.zeros((n_ctx, _PAIR), BF16)
            d4_ref[3, pl.ds(nx, n_ctx), :] = jnp.zeros((n_ctx, _PAIR), BF16)

        _load_bias(bias_ref, tiles_ref, rp_ref, m_ref, b, nblk)
        gv = g_ref[...].astype(F32)
        dav = da_ref[...].astype(F32)
        ov = o_ref[...].astype(F32)
        dov = dav * _silu(gv)
        d4_ref[3, pl.ds(here, _QB), :] = (dav * ov * _dsilu(gv)).astype(BF16)
        qf = q_ref[...].astype(F32) * HEAD_DIM ** -0.5
        kw = k_ref[pl.ds(start, _KW), :].astype(BF16)
        vw = v_ref[pl.ds(start, _KW), :].astype(BF16)
        kcv = k_ref[pl.ds(nx, n_ctx), :].astype(BF16)
        vcv = v_ref[pl.ds(nx, n_ctx), :].astype(BF16)
        lane = lax.broadcasted_iota(jnp.int32, (1, _PAIR), 1)
        dq = jnp.zeros((_QB, _PAIR), F32)
        for h in range(2):
            mine = (lane >= HEAD_DIM) if h else (lane < HEAD_DIM)
            qm = jnp.where(mine, qf, 0.0).astype(BF16)
            dom = jnp.where(mine, dov, 0.0)
            dob = dom.astype(BF16)
            lse = lse_ref[:, h:h + 1]
            s_loc = lax.dot_general(qm, kw, _DIMS["nt"], preferred_element_type=F32)
            p_loc = jnp.exp(s_loc + bias_ref[h] - lse)
            p_ctx = jnp.exp(lax.dot_general(qm, kcv, _DIMS["nt"], preferred_element_type=F32) - lse)
            delta = jnp.sum(dom * ov, axis=-1, keepdims=True)
            ds_loc = p_loc * (lax.dot_general(dob, vw, _DIMS["nt"], preferred_element_type=F32) - delta)
            ds_ctx = p_ctx * (lax.dot_general(dob, vcv, _DIMS["nt"], preferred_element_type=F32) - delta)
            dsb_loc = ds_loc.astype(BF16)
            dsb_ctx = ds_ctx.astype(BF16)
            dq_h = (jnp.dot(dsb_loc, kw, preferred_element_type=F32)
                    + jnp.dot(dsb_ctx, kcv, preferred_element_type=F32))
            dq = dq + jnp.where(mine, dq_h, 0.0)
            dk_ref[pl.ds(start, _KW), :] += lax.dot_general(dsb_loc, qm, _DIMS["tn"], preferred_element_type=F32)
            dv_ref[pl.ds(start, _KW), :] += lax.dot_general(p_loc.astype(BF16), dob, _DIMS["tn"],
                                                            preferred_element_type=F32)
            dk_ref[pl.ds(nx, n_ctx), :] += lax.dot_general(dsb_ctx, qm, _DIMS["tn"], preferred_element_type=F32)
            dv_ref[pl.ds(nx, n_ctx), :] += lax.dot_general(p_ctx.astype(BF16), dob, _DIMS["tn"],
                                                           preferred_element_type=F32)
            ds_ref[h] = ds_loc
        d4_ref[0, pl.ds(here, _QB), :] = (dq * HEAD_DIM ** -0.5).astype(BF16)

        def scatter(cls):
            for h in range(2):
                for qr, kr, tile in _bias_pieces(cls):
                    if tile is not None:
                        dtiles_ref[h, tile] += ds_ref[h, qr * GRID_W:(qr + 1) * GRID_W, kr * GRID_W:(kr + 2) * GRID_W]

        _block_class(b, nblk, scatter)

        @pl.when(b == nblk - 1)
        def _():
            d4_ref[1] = dk_ref[...].astype(BF16)
            d4_ref[2] = dv_ref[...].astype(BF16)
            _fold_tiles(dtiles_ref, drp_ref)

        if comm:
            pl.when(jnp.logical_and(hp == nhp - 1, b == nblk - 1))(lambda: comm.finish(cin, cout, sems))

    tiles = pltpu.VMEM((2, _N_TILES, GRID_W, _PAIR), F32)
    block = pltpu.VMEM((2, _QB, _KW), F32)
    res = pl.pallas_call(
        body, grid=(nhp, nblk),
        in_specs=[qspec(0), kspec(1), kspec(2), qspec(3), tspec, mspec, ospec, lspec, ospec] + [HBM_SPEC] * n_cin,
        out_specs=[pl.BlockSpec((4, rows, _PAIR), lambda hp, b: (0, 0, hp)), tspec] + [HBM_SPEC] * n_cout,
        out_shape=[_sds((4, rows, w), BF16), _sds(rp.shape, F32)] + (comm.outs if comm else []),
        scratch_shapes=[block, tiles, block, tiles, pltpu.VMEM((rows, _PAIR), F32), pltpu.VMEM((rows, _PAIR), F32)]
        + (comm.sems if comm else []),
        name=name, compiler_params=_cparams("arbitrary", "arbitrary"),
    )(p4, p4, p4, p4, rp, mask, o, lse, da, *(comm.ins if comm else []))
    return res[:2], res[2:]


def _final(x, g, target, name):
    rows, d = x.shape
    tr = ROW_BLOCK
    nblk = rows // tr

    def body(x_ref, g_ref, t_ref, loss_ref, dx_ref, dg_ref, acc_ref):
        i = pl.program_id(0)
        xv = x_ref[...]
        gv = g_ref[...]
        r = lax.rsqrt(jnp.mean(xv * xv, axis=-1, keepdims=True) + EPS)
        xn = xv * r
        err = xn * gv - t_ref[...]
        dy = err * (1.0 / d)
        dxn = dy * gv
        dx_ref[...] = r * (dxn - xn * jnp.mean(dxn * xn, axis=-1, keepdims=True))
        s_g = jnp.sum(dy * xn, axis=0, keepdims=True)
        s_l = jnp.sum(jnp.mean(err * err, axis=-1, keepdims=True), axis=0, keepdims=True)

        @pl.when(i == 0)
        def _():
            dg_ref[...] = s_g
            acc_ref[...] = s_l

        @pl.when(i > 0)
        def _():
            dg_ref[...] += s_g
            acc_ref[...] += s_l

        @pl.when(i == nblk - 1)
        def _():
            loss_ref[...] = jnp.broadcast_to(0.5 * acc_ref[...], loss_ref.shape)

    row = pl.BlockSpec((tr, d), lambda i: (i, 0))
    vec = pl.BlockSpec((1, d), lambda i: (0, 0))
    return pl.pallas_call(
        body, grid=(nblk,), in_specs=[row, vec, row],
        out_specs=[pl.BlockSpec((1, 128), lambda i: (0, 0)), row, vec],
        out_shape=[_sds((1, 128), F32), _sds((rows, d), F32), _sds((1, d), F32)],
        scratch_shapes=[pltpu.VMEM((1, 1), F32)], name=name, compiler_params=_cparams("arbitrary"),
    )(x, g, target)


def _as2d(a):
    if a.ndim == 1:
        return a.reshape(-1, 128) if a.shape[0] % 128 == 0 else a.reshape(1, -1)
    return a.reshape(-1, a.shape[-1])


def _adamw(w, g, m, v, name, comm=None):
    shape = w.shape
    w2, g2, m2, v2 = (_as2d(t) for t in (w, g.reshape(shape), m, v))
    rows, cols = w2.shape
    tr = 512 if rows % 512 == 0 else rows
    c1 = 1.0 - ADAM_B1 ** ADAM_STEP
    c2 = 1.0 - ADAM_B2 ** ADAM_STEP

    def body(w_ref, g_ref, m_ref, v_ref, d_ref, nm_ref, nv_ref):
        gv = g_ref[...]
        nm = ADAM_B1 * m_ref[...] + (1.0 - ADAM_B1) * gv
        nv = ADAM_B2 * v_ref[...] + (1.0 - ADAM_B2) * (gv * gv)
        nm_ref[...] = nm
        nv_ref[...] = nv
        d_ref[...] = -ADAM_LR * ((nm / c1) / (jnp.sqrt(nv / c2) + ADAM_EPS) + ADAM_WD * w_ref[...])

    blk = pl.BlockSpec((tr, cols), lambda i: (i, 0))
    outs, carried = _call(body, (w2, g2, m2, v2), grid=(rows // tr,), in_specs=[blk] * 4, out_specs=[blk] * 3,
                          out_shape=[_sds((rows, cols), F32)] * 3, name=name, comm=comm)
    outs = tuple(t.reshape(shape) for t in outs)
    return outs if comm is None else (outs, carried)


def _sum_lead(x, name, out_dtype=F32):
    n, rows, cols = x.shape
    tr = 512 if rows % 512 == 0 else rows

    def body(x_ref, o_ref):
        acc = x_ref[0].astype(F32)
        for k in range(1, n):
            acc = acc + x_ref[k].astype(F32)
        o_ref[...] = acc.astype(out_dtype)

    return pl.pallas_call(
        body, grid=(rows // tr,), in_specs=[pl.BlockSpec((n, tr, cols), lambda i: (0, i, 0))],
        out_specs=pl.BlockSpec((tr, cols), lambda i: (i, 0)), out_shape=_sds((rows, cols), out_dtype),
        name=name, compiler_params=_cparams("parallel"),
    )(x)


_NO_CTX = 1 << 30


def _seg_vecs(mod_l, which, nseg):
    return mod_l[:nseg, which][:, None, :]


def _norm_grads(dshift, dgeff, dgate, g, scale):
    nseg, _, d = dshift.shape
    dmod = jnp.stack([dshift[:, 0], dgeff[:, 0] * g, dgate[:, 0]], axis=1)
    if nseg == 1:
        dmod = jnp.concatenate([dmod, jnp.zeros((1, 3, d), F32)], axis=0)
    dg = jnp.sum(dgeff[:, 0] * (1.0 + scale[:, 0]), axis=0)
    return dmod, dg


def _pool_layer(xin, g, mod_l, w_in, w_grp, w_out, pscale, nx, tag, comms=None):
    rows = xin.shape[0]
    nseg = 2 if rows > nx else 1
    comms = comms or {}
    shift, scale, gate = (_seg_vecs(mod_l, k, nseg) for k in range(3))
    (h, r, uv), c_in = _norm_w_in(xin, g, scale, shift, w_in, nx, f"w_in_fwd_{tag}", comms.get("w_in_fwd"))
    (z, mixed, a), c_pool = _pool_grp_fwd(uv, w_grp, pscale, nx, f"pool_fwd_{tag}", comms.get("pool_fwd"))
    yx, xout = _w_out_resid(a, w_out, xin, gate, nx, f"w_out_fwd_{tag}")

    def backward(dxo, comms=None):
        comms = comms or {}
        (dyx, da, dgate), c_out = _gate_w_out_bwd(dxo, yx, gate, w_out, nx, f"w_out_bwd_{tag}", comms.get("w_out_bwd"))
        gw_out = _mm_tn(a, dyx, f"w_out_grad_{tag}", BF16)
        dm, duv, dscale = _pool_grp_bwd(da, mixed, uv, pscale, w_grp, nx, f"pool_bwd_{tag}")
        gw_grp = _grp_wgrad(z, dm, w_grp.shape[0], f"grp_grad_{tag}", BF16)
        gw_in = _mm_tn_parts(h, duv, f"w_in_grad_{tag}", BF16)
        (dx, dshift, dgeff), c_bwd = _w_in_bwd_norm(duv, w_in, xin, r, g, scale, dxo, nx, f"w_in_bwd_{tag}",
                                                    comms.get("w_in_bwd"))
        dmod, dg = _norm_grads(dshift, dgeff, dgate, g[0], scale)
        return (dx, dmod, dg, dict(w_in=gw_in, w_grp=gw_grp, w_out=gw_out, scale=dscale),
                dict(w_out_bwd=c_out, w_in_bwd=c_bwd))

    return xout, backward, dict(w_in_fwd=c_in, pool_fwd=c_pool)


def _na_layer(xc, g, mod_l, w_in, rpb, w_out, nx, mask, comm=None):
    nh, n_dr, n_dc = rpb.shape
    shift, scale = _seg_vecs(mod_l, 0, 2), _seg_vecs(mod_l, 1, 2)
    gate = _seg_vecs(mod_l, 2, 1)
    (h, r, p4), _ = _norm_w_in(xc, g, scale, shift, w_in, nx, "w_in_fwd_na")
    rp = jnp.pad(rpb, ((0, 0), (1, _RP_ROWS - 1 - n_dr), (0, _PAIR - n_dc)))
    (a, o, lse), carried = _attn_fwd(p4, rp, mask, nx, "attn_fwd", comm)
    yx, xout = _w_out_resid(a, w_out, xc, gate, nx, "w_out_fwd_na")

    def backward(dxo, comm=None):
        (dyx, da, dgate), _ = _gate_w_out_bwd(dxo, yx, gate, w_out, nx, "w_out_bwd_na")
        gw_out = _mm_tn(a, dyx, "w_out_grad_na", BF16)
        (d4, drp), carried_bwd = _attn_bwd(p4, rp, mask, o, lse, da, nx, "attn_bwd", comm)
        gw_in = _mm_tn_parts(h, d4, "w_in_grad_na", BF16)
        (dx, dshift, dgeff), _ = _w_in_bwd_norm(d4, w_in, xc, r, g, scale, dxo, nx, "w_in_bwd_na")
        dgate2 = jnp.concatenate([dgate, jnp.zeros_like(dgate)], axis=0)
        dmod, dg = _norm_grads(dshift, dgeff, dgate2, g[0], scale)
        drpb = drp[:, 1:1 + n_dr, ::-1][:, :, :n_dc]
        return dx, dmod, dg, dict(w_in=gw_in, w_out=gw_out, rpb=drpb), carried_bwd

    return xout, backward, carried


def _conv_layer(xin, g, mod_l, w_in, dw, db, w_out):
    shift, scale, gate = (_seg_vecs(mod_l, k, 1) for k in range(3))
    nx = xin.shape[0]
    (h, r, p4), _ = _norm_w_in(xin, g, scale, shift, w_in, nx, "w_in_fwd_conv")
    a = _conv_fwd(p4, dw, db, "conv_fwd")
    yx, xout = _w_out_resid(a, w_out, xin, gate, nx, "w_out_fwd_conv")

    def backward(dxo):
        (dyx, da, dgate), _ = _gate_w_out_bwd(dxo, yx, gate, w_out, nx, "w_out_bwd_conv")
        gw_out = _mm_tn(a, dyx, "w_out_grad_conv", BF16)
        d4, ddw, ddb = _conv_bwd(da, p4, dw, db, "conv_bwd")
        gw_in = _mm_tn_parts(h, d4, "w_in_grad_conv", BF16)
        (dx, dshift, dgeff), _ = _w_in_bwd_norm(d4, w_in, xin, r, g, scale, dxo, nx, "w_in_bwd_conv")
        dmod, dg = _norm_grads(dshift, dgeff, dgate, g[0], scale)
        return dx, dmod, dg, dict(w_in=gw_in, w_out=gw_out, dw=ddw, db=ddb)

    return xout, backward


def _example_step(x, ctx, target, mod, norm_g, final_g, wts, na_comms=None, na_weights=None, late_comm=None,
                  late_weights=None, grad_comm=None, na_grad_comms=None):
    nx = x.shape[0]
    consts = _attn_mask()
    g_rows = [norm_g[i:i + 1] for i in range(4)]
    xc0 = jnp.concatenate([x, ctx], axis=0)
    xc1, bwd0, carried0 = _pool_layer(xc0, g_rows[0], mod[0], wts["pool_w_in"][0], wts["pool_w_grp"][0],
                                      wts["pool_w_out"][0], wts["pool_scale"][0:1], nx, "p0", na_comms)
    if na_weights is not None:
        wts = {**wts, **na_weights(carried0)}
    x2, bwd1, carried = _na_layer(xc1, g_rows[1], mod[1], wts["na_w_in"], wts["na_rpb"], wts["na_w_out"], nx, consts,
                                  late_comm)
    if late_weights is not None:
        wts = {**wts, **late_weights(carried)}
    x3, bwd2 = _conv_layer(x2, g_rows[2], mod[2], wts["conv_w_in"], wts["conv_dw"], wts["conv_db"], wts["conv_w_out"])
    x4, bwd3, _ = _pool_layer(x3, g_rows[3], mod[3], wts["pool_w_in"][1], wts["pool_w_grp"][1], wts["pool_w_out"][1],
                              wts["pool_scale"][1:2], nx, "p3")
    loss, dx4, dfinal_g = _final(x4, final_g, target, "loss_head")
    dx3, dmod3, dg3, gr3, _ = bwd3(dx4)
    dx2, dmod2, dg2, gr2 = bwd2(dx3)
    dxc1, dmod1, dg1, gr1, carried_bwd = bwd1(dx2, grad_comm(gr3, gr2) if grad_comm else None)
    dxc0, dmod0, dg0, gr0, carried_bwd0 = bwd0(dxc1, na_grad_comms(gr1) if na_grad_comms else None)
    return dict(
        loss=loss, grad_x=dxc0[:nx], dmod=jnp.stack([dmod0, dmod1, dmod2, dmod3]),
        dnorm_g=jnp.stack([dg0, dg1, dg2, dg3]), dfinal_g=dfinal_g, layers=(gr0, gr1, gr2, gr3), carried=carried_bwd,
        carried0=carried_bwd0)


_AXES = ("x", "y", "c")
_CHIP_FLIPS = ((1, 0), (0, 1), (1, 1))


def _position():
    return tuple(lax.axis_index(a) for a in _AXES)


def _flipped(pos, flip):
    return tuple(1 - p if f else p for p, f in zip(pos, flip))


def _join_comms(comms):
    n_in = [len(c.ins) for c in comms]
    n_out = [len(c.outs) for c in comms]
    n_sem = [len(c.sems) for c in comms]

    def parts(ins, outs, sems):
        for k in range(len(comms)):
            a, b, s = sum(n_in[:k]), sum(n_out[:k]), sum(n_sem[:k])
            yield comms[k], (ins[a:a + n_in[k]], outs[b:b + n_out[k]], sems[s:s + n_sem[k]])

    def start(ins, outs, sems):
        for c, part in parts(ins, outs, sems):
            c.start(*part)

    def finish(ins, outs, sems):
        for c, part in parts(ins, outs, sems):
            c.finish(*part)

    joint = _Comm([a for c in comms for a in c.ins], [o for c in comms for o in c.outs],
                  [s for c in comms for s in c.sems], start, finish)
    return joint, lambda res: [list(res[sum(n_out[:k]):sum(n_out[:k + 1])]) for k in range(len(comms))]


def _run_comms(comms, name):
    joint, split = _join_comms(comms)

    def body(*refs):
        n_in, n_out = len(joint.ins), len(joint.outs)
        joint.start(refs[:n_in], refs[n_in:n_in + n_out], refs[n_in + n_out:])
        joint.finish(refs[:n_in], refs[n_in:n_in + n_out], refs[n_in + n_out:])

    res = pl.pallas_call(
        body, in_specs=[HBM_SPEC] * len(joint.ins), out_specs=[HBM_SPEC] * len(joint.outs), out_shape=joint.outs,
        scratch_shapes=joint.sems, name=name,
    )(*joint.ins)
    return split(res)


def _all_gather_comm(v, axes):
    flips = [f for f in np.ndindex(2, 2, 2) if any(f) and all(a in axes or not b for a, b in zip(_AXES, f))]
    n = len(flips) + 1

    def copies(ins, outs, sems):
        (v_ref,), (o_ref,), (send_sems, recv_sems, local_sem) = ins, outs, sems
        pos = _position()
        slot = 0
        for a, p in zip(_AXES, pos):
            if a in axes:
                slot = 2 * slot + p
        local = pltpu.make_async_copy(v_ref, o_ref.at[slot], local_sem)
        remote = [pltpu.make_async_remote_copy(v_ref, o_ref.at[slot], send_sems.at[k], recv_sems.at[k],
                                               device_id=_flipped(pos, flip), device_id_type=MESH)
                  for k, flip in enumerate(flips)]
        return [local] + remote

    def start(ins, outs, sems):
        for cp in copies(ins, outs, sems):
            cp.start()

    def finish(ins, outs, sems):
        for cp in copies(ins, outs, sems):
            cp.wait()

    sems = [pltpu.SemaphoreType.DMA((n - 1,)), pltpu.SemaphoreType.DMA((n - 1,)), pltpu.SemaphoreType.DMA(())]
    return _Comm([v], [_sds((n,) + v.shape, v.dtype)], sems, start, finish)


def _all_gather(v, axes, name):
    return _run_comms([_all_gather_comm(v, axes)], name)[0][0]


class _Item:
    def __init__(self, key, layer, shape, shard_axis, half_axis):
        self.key, self.layer, self.shape = key, layer, tuple(shape)
        self.shard_axis, self.half_axis = shard_axis, half_axis
        self.shard = shape[shard_axis] // 4
        self.half = shape[half_axis] // 2

    def sized(self, shard=False, half=False):
        s = list(self.shape)
        if shard:
            s[self.shard_axis] = self.shard
        if half:
            s[self.half_axis] = self.half
        return tuple(s)

    def window(self, ref, chip=None, half=None):
        idx = [slice(None)] * len(self.shape)
        if chip is not None:
            idx[self.shard_axis] = pl.ds(chip * self.shard, self.shard)
        if half is not None:
            idx[self.half_axis] = pl.ds(half * self.half, self.half)
        return ref.at[tuple(idx)]


def _items(d, w):
    out = []
    for j in range(2):
        out += [_Item("pool_w_in", j, (d, 2 * w), 1, 0), _Item("pool_w_grp", j, (4, w // 4, w // 4), 1, 0),
                _Item("pool_w_out", j, (w, d), 0, 1)]
    out += [_Item("na_w_in", 0, (d, 4 * w), 1, 0), _Item("na_w_out", 0, (w, d), 0, 1),
            _Item("conv_w_in", 0, (d, 4 * w), 1, 0), _Item("conv_w_out", 0, (w, d), 0, 1)]
    return out


def _gather_weights(shards, items, name):
    comm = _gather_comm(shards, items)

    def body(*refs):
        n = len(items)
        comm.start(refs[:n], refs[n:2 * n], refs[2 * n:])
        comm.finish(refs[:n], refs[n:2 * n], refs[2 * n:])

    return pl.pallas_call(
        body, in_specs=[HBM_SPEC] * len(items), out_specs=[HBM_SPEC] * len(items), out_shape=comm.outs,
        scratch_shapes=comm.sems, name=name,
    )(*shards)


def _gather_comm(shards, items):
    n = len(items)

    def copies(src, dst, sems, onward):
        send_a, recv_a, send_b, recv_b, send_c, recv_c = sems
        x, y, c = _position()
        chip = 2 * x + y
        sibling = (x, y, 1 - c)
        own, out, fwd, fwd_in = [], [], [], []
        for i, it in enumerate(items):
            own.append(pltpu.make_async_remote_copy(src[i], it.window(dst[i], chip=chip), send_c.at[i], recv_c.at[i],
                                                    device_id=sibling, device_id_type=MESH))
            for k, flip in enumerate(_CHIP_FLIPS):
                px, py = _flipped((x, y), flip)
                s = 3 * i + k
                out.append(pltpu.make_async_remote_copy(
                    it.window(src[i], half=c), it.window(dst[i], chip=chip, half=c), send_a.at[s], recv_a.at[s],
                    device_id=(px, py, c), device_id_type=MESH))
                if onward:
                    got = it.window(dst[i], chip=2 * px + py, half=c)
                    fwd.append(pltpu.make_async_remote_copy(got, got, send_b.at[s], recv_b.at[s],
                                                            device_id=sibling, device_id_type=MESH))
                    other = it.window(dst[i], chip=2 * px + py, half=1 - c)
                    fwd_in.append(pltpu.make_async_remote_copy(other, other, send_b.at[s], recv_b.at[s],
                                                               device_id=sibling, device_id_type=MESH))
        return own, out, fwd, fwd_in

    def start(src, dst, sems):
        own, out, _, _ = copies(src, dst, sems, False)
        for cp in own + out:
            cp.start()

    def finish(src, dst, sems):
        own, out, fwd, fwd_in = copies(src, dst, sems, True)
        for arrived, onward in zip(out, fwd):
            arrived.wait_recv()
            onward.start()
        for cp in fwd_in:
            cp.wait_recv()
        for cp in out + fwd:
            cp.wait_send()
        for cp in own:
            cp.wait()

    sems = [pltpu.SemaphoreType.DMA((3 * n,)) for _ in range(4)] + [pltpu.SemaphoreType.DMA((n,)) for _ in range(2)]
    return _Comm(shards, [_sds(it.shape, BF16) for it in items], sems, start, finish)


def _pair_swap_comm(arrays, windows, out_shapes):
    n = len(arrays)

    def copies(src, got, sems):
        send_sems, recv_sems = sems
        x, y, c = _position()
        return [pltpu.make_async_remote_copy(windows[i](src[i], 1 - c), got[i], send_sems.at[i], recv_sems.at[i],
                                             device_id=(x, y, 1 - c), device_id_type=MESH) for i in range(n)]

    def start(src, got, sems):
        for cp in copies(src, got, sems):
            cp.start()

    def finish(src, got, sems):
        for cp in copies(src, got, sems):
            cp.wait()

    return _Comm(arrays, out_shapes, [pltpu.SemaphoreType.DMA((n,)), pltpu.SemaphoreType.DMA((n,))], start, finish)


def _pair_swap(arrays, windows, out_shapes, name):
    return _run_comms([_pair_swap_comm(arrays, windows, out_shapes)], name)[0]


def _chip_exchange(partials, items, name):
    comm = _chip_exchange_comm(partials, items)

    def body(*refs):
        n = len(items)
        comm.start(refs[:n], refs[n:2 * n], refs[2 * n:])
        comm.finish(refs[:n], refs[n:2 * n], refs[2 * n:])

    return pl.pallas_call(
        body, in_specs=[HBM_SPEC] * len(items), out_specs=[HBM_SPEC] * len(items), out_shape=comm.outs,
        scratch_shapes=comm.sems, name=name,
    )(*partials)


def _chip_exchange_copies(items):
    def copies(src, dst, sems):
        send_sems, recv_sems = sems
        x, y, c = _position()
        out = []
        for i, it in enumerate(items):
            for k, flip in enumerate(_CHIP_FLIPS):
                px, py = _flipped((x, y), flip)
                out.append(pltpu.make_async_remote_copy(
                    it.window(src[i], chip=2 * px + py), dst[i].at[k], send_sems.at[3 * i + k],
                    recv_sems.at[3 * i + k], device_id=(px, py, c), device_id_type=MESH))
        return out

    return copies


_SEM_SPEC = pl.BlockSpec(memory_space=pltpu.SEMAPHORE)
_DATAFLOW = pltpu.SideEffectType.DATAFLOW_SIDE_EFFECTING


def _chip_exchange_start(partials, items, name):
    n = len(items)
    copies = _chip_exchange_copies(items)
    zones = [lax.empty((3,) + it.sized(shard=True, half=True), BF16) for it in items]

    def body(*refs):
        src, land = refs[:n], refs[n:2 * n]
        send_sems, recv_sems = refs[2 * n:2 * n + 2]
        token = refs[-1]
        for cp in copies(src, land, (send_sems, recv_sems)):
            cp.start()
        token[...] = jnp.zeros(token.shape, F32)

    hbm = lambda t: pltpu.HBM(t.shape, t.dtype)
    res = pl.pallas_call(
        body, name=name,
        out_shape=(pltpu.SemaphoreType.DMA((3 * n,)), pltpu.SemaphoreType.DMA((3 * n,)), *[hbm(t) for t in partials],
                   *[hbm(t) for t in zones], _sds((8, 128), F32)),
        in_specs=[HBM_SPEC] * (2 * n),
        out_specs=(_SEM_SPEC, _SEM_SPEC, *[HBM_SPEC] * (2 * n), pl.BlockSpec(memory_space=pltpu.VMEM)),
        input_output_aliases={i: 2 + i for i in range(2 * n)},
        compiler_params=pltpu.CompilerParams(has_side_effects=_DATAFLOW),
    )(*[pltpu.with_memory_space_constraint(t, pltpu.HBM) for t in list(partials) + zones])
    return res[0], res[1], list(res[2:2 + n]), list(res[2 + n:2 + 2 * n]), res[-1]


def _chip_exchange_wait(send_sems, recv_sems, partials, zones, after, items, name):
    n = len(items)
    copies = _chip_exchange_copies(items)

    def body(*refs):
        src, land = refs[:n], refs[n:2 * n]
        send, recv = refs[2 * n:2 * n + 2]
        for cp in copies(src, land, (send, recv)):
            cp.wait_send()
            cp.wait_recv()

    hbm = lambda t: pltpu.HBM(t.shape, t.dtype)
    res = pl.pallas_call(
        body, name=name, out_shape=tuple(hbm(t) for t in list(partials) + list(zones)),
        in_specs=[HBM_SPEC] * (2 * n) + [_SEM_SPEC, _SEM_SPEC, pl.BlockSpec(memory_space=pl.ANY)],
        out_specs=tuple([HBM_SPEC] * (2 * n)), input_output_aliases={i: i for i in range(2 * n)},
        compiler_params=pltpu.CompilerParams(has_side_effects=_DATAFLOW),
    )(*partials, *zones, send_sems, recv_sems, after)
    return list(res[n:])


def _chip_exchange_comm(partials, items):
    n = len(items)
    copies = _chip_exchange_copies(items)

    def start(src, dst, sems):
        for cp in copies(src, dst, sems):
            cp.start()

    def finish(src, dst, sems):
        for cp in copies(src, dst, sems):
            cp.wait()

    return _Comm(partials, [_sds((3,) + it.sized(shard=True, half=True), BF16) for it in items],
                 [pltpu.SemaphoreType.DMA((3 * n,)), pltpu.SemaphoreType.DMA((3 * n,))], start, finish)


_SUM_STEPS = 2


def _pair_sums(gs, gots, its, pos, name):
    n = len(its)
    nb = _SUM_STEPS
    g2 = [g.reshape(-1, g.shape[-1]) for g in gs]
    got2 = [t.reshape(-1, t.shape[-1]) for t in gots]

    def body(pos_ref, *refs):
        for g_ref, got_ref, o_ref in zip(refs[:n], refs[n:2 * n], refs[2 * n:]):
            o_ref[...] = (g_ref[...].astype(F32) + got_ref[...].astype(F32)).astype(BF16)

    g_specs, got_specs = [], []
    for it, t in zip(its, got2):
        rows, cols = t.shape
        blk = (rows // nb, cols)
        g_map = (lambda i, pos: (pos[1] * nb + i, 0)) if it.half_axis == 0 else (lambda i, pos: (i, pos[1]))
        g_specs.append(pl.BlockSpec(blk, g_map))
        got_specs.append(pl.BlockSpec(blk, lambda i, pos: (i, 0)))
    outs = pl.pallas_call(
        body, grid_spec=pltpu.PrefetchScalarGridSpec(
            num_scalar_prefetch=1, grid=(nb,), in_specs=g_specs + got_specs, out_specs=got_specs),
        out_shape=[_sds(t.shape, BF16) for t in got2], name=name, compiler_params=_cparams("parallel"),
    )(pos, *g2, *got2)
    return [o.reshape(t.shape) for o, t in zip(outs, gots)]


_FLIP_SLOT = {2: 0, 1: 1, 3: 2}


def _chip_sums(pairs, slots, its, pos, name):
    n = len(its)
    nb = _SUM_STEPS

    def body(pos_ref, *refs):
        chip = pos_ref[0]
        for own in range(4):
            @pl.when(chip == own)
            def _():
                for p_ref, s_ref, o_ref in zip(refs[:n], refs[n:2 * n], refs[2 * n:]):
                    acc = None
                    for k in range(4):
                        v = (p_ref[...] if k == own else s_ref[_FLIP_SLOT[own ^ k]]).astype(F32)
                        acc = v if acc is None else acc + v
                    o_ref[...] = acc

    p_specs, s_specs, o_specs, shapes = [], [], [], []
    for it in its:
        shape = it.sized(shard=True, half=True)
        blk = (shape[0] // nb,) + shape[1:]
        rest = (0,) * (len(shape) - 1)

        def p_map(i, pos, it=it, nd=len(shape)):
            lead = i + (pos[0] * nb if it.shard_axis == 0 else 0)
            return (lead,) + tuple(pos[0] if ax == it.shard_axis else 0 for ax in range(1, nd))

        p_specs.append(pl.BlockSpec(blk, p_map))
        s_specs.append(pl.BlockSpec((3,) + blk, lambda i, pos, rest=rest: (0, i) + rest))
        o_specs.append(pl.BlockSpec(blk, lambda i, pos, rest=rest: (i,) + rest))
        shapes.append(_sds(shape, F32))
    return pl.pallas_call(
        body, grid_spec=pltpu.PrefetchScalarGridSpec(
            num_scalar_prefetch=1, grid=(nb,), in_specs=p_specs + s_specs, out_specs=o_specs),
        out_shape=shapes, name=name, compiler_params=_cparams("parallel"),
    )(pos, *pairs, *slots)


_GRAD_KEYS = ("pool_w_in", "pool_w_grp", "pool_w_out", "na_w_in", "na_w_out", "conv_w_in", "conv_w_out")


def _adamw_matrix(w, m, v, owns, others, it, pos, name):
    nl = w.shape[0]
    rows_split = it.half_axis == 0
    r, cdim = int(np.prod(w.shape[1:-1])), w.shape[-1]
    hr, hc = (r // 2, cdim) if rows_split else (r, cdim // 2)
    br = min(hr, 256)
    nb = hr // br
    c1 = 1.0 - ADAM_B1 ** ADAM_STEP
    c2 = 1.0 - ADAM_B2 ** ADAM_STEP

    def body(pos_ref, w_ref, m_ref, v_ref, *rest):
        own_refs, other_refs = rest[:nl], rest[nl:2 * nl]
        g_ref, d_ref, nm_ref, nv_ref = rest[2 * nl:]
        j, h = pl.program_id(0), pl.program_id(1)
        own, other = own_refs[0][...], other_refs[0][...]
        for q in range(1, nl):
            own = jnp.where(j == q, own_refs[q][...], own)
            other = jnp.where(j == q, other_refs[q][...], other)
        gv = jnp.where(h == pos_ref[1], own, other)
        nm = ADAM_B1 * m_ref[...] + (1.0 - ADAM_B1) * gv
        nv = ADAM_B2 * v_ref[...] + (1.0 - ADAM_B2) * (gv * gv)
        g_ref[...] = gv
        nm_ref[...] = nm
        nv_ref[...] = nv
        d_ref[...] = -ADAM_LR * ((nm / c1) / (jnp.sqrt(nv / c2) + ADAM_EPS) + ADAM_WD * w_ref[...])

    if rows_split:
        full = pl.BlockSpec((None, br, hc), lambda j, h, i, pos: (j, h * nb + i, 0))
    else:
        full = pl.BlockSpec((None, br, hc), lambda j, h, i, pos: (j, i, h))
    half = pl.BlockSpec((br, hc), lambda j, h, i, pos: (i, 0))
    flat = lambda t: t.reshape(nl, r, cdim)
    outs = pl.pallas_call(
        body, grid_spec=pltpu.PrefetchScalarGridSpec(
            num_scalar_prefetch=1, grid=(nl, 2, nb), in_specs=[full] * 3 + [half] * (2 * nl), out_specs=[full] * 4),
        out_shape=[_sds((nl, r, cdim), F32)] * 4, name=name,
        compiler_params=_cparams("parallel", "parallel", "parallel"),
    )(pos, flat(w), flat(m), flat(v), *[t.reshape(hr, hc) for t in list(owns) + list(others)])
    return tuple(t.reshape(w.shape) for t in outs)


_WEIGHTS = ("c_ctx", "norm_g", "ada_w", "ada_b", "pool_w_in", "pool_w_grp", "pool_scale", "pool_w_out", "na_w_in",
            "na_rpb", "na_w_out", "conv_w_in", "conv_dw", "conv_db", "conv_w_out", "final_g")
_COND_ROWS = 16


def _modulations(cond, ada_w, ada_b_cols):
    nl, d, n = ada_w.shape
    return _matmul(
        cond, ada_w, mode="nn", grid=(nl, 1), a_silu=True, epilogue="bias",
        a_spec=pl.BlockSpec((_COND_ROWS, d), lambda i, j: (0, 0)), b_spec=pl.BlockSpec((None, d, n), lambda i, j: (i, 0, 0)),
        extra=(ada_b_cols,), extra_specs=(pl.BlockSpec((None, 1, n), lambda i, j: (i, 0, 0)),),
        out_shapes=[_sds((nl, _COND_ROWS, n), F32)], out_specs=[pl.BlockSpec((None, _COND_ROWS, n), lambda i, j: (i, 0, 0))],
        name="modulations")[0]


def _ada_w_grad(cond, dm_cols):
    d = cond.shape[1]
    nl, _, n = dm_cols.shape
    return _matmul(
        cond, dm_cols, mode="tn", grid=(nl, 1), a_silu=True,
        a_spec=pl.BlockSpec((_COND_ROWS, d), lambda i, j: (0, 0)), b_spec=pl.BlockSpec((None, _COND_ROWS, n), lambda i, j: (i, 0, 0)),
        out_shapes=[_sds((nl, d, n), F32)], out_specs=[pl.BlockSpec((None, d, n), lambda i, j: (i, 0, 0))],
        name="ada_w_grad")[0]


def _cond_grad(dm_cols, ada_w):
    nl, d, n = ada_w.shape
    return _matmul(
        dm_cols, ada_w, mode="nt", grid=(1, nl), nk=nl, acc_shape=(_COND_ROWS, d),
        a_spec=pl.BlockSpec((None, _COND_ROWS, n), lambda i, q: (q, 0, 0)), b_spec=pl.BlockSpec((None, d, n), lambda i, q: (q, 0, 0)),
        out_shapes=[_sds((_COND_ROWS, d), F32)], out_specs=[pl.BlockSpec((_COND_ROWS, d), lambda i, q: (0, 0))],
        name="cond_grad")[0]


def _pack(parts):
    flat = [p.reshape(-1) for p in parts]
    sizes = [f.shape[0] for f in flat]
    total = sum(sizes)
    rows = -(-total // 1024) * 8
    packed = jnp.concatenate(flat + [jnp.zeros((rows * 128 - total,), F32)]).reshape(rows, 128)
    offs = np.concatenate([[0], np.cumsum(sizes)])[:-1]
    return packed, [(int(o), p.shape) for o, p in zip(offs, parts)]


def _unpack(flat, layout, k):
    off, shape = layout[k]
    return flat[..., off:off + int(np.prod(shape))].reshape(flat.shape[:-1] + tuple(shape))


def kernel(x, c, ctx, c_ctx, norm_g, ada_w, ada_b, pool_w_in, pool_w_grp, pool_scale, pool_w_out, na_w_in, na_rpb, na_w_out, conv_w_in, conv_dw, conv_db, conv_w_out, final_g, loss_target, m_c_ctx, m_norm_g, m_ada_w, m_ada_b, m_pool_w_in, m_pool_w_grp, m_pool_scale, m_pool_w_out, m_na_w_in, m_na_rpb, m_na_w_out, m_conv_w_in, m_conv_dw, m_conv_db, m_conv_w_out, m_final_g, v_c_ctx, v_norm_g, v_ada_w, v_ada_b, v_pool_w_in, v_pool_w_grp, v_pool_scale, v_pool_w_out, v_na_w_in, v_na_rpb, v_na_w_out, v_conv_w_in, v_conv_dw, v_conv_db, v_conv_w_out, v_final_g):
    params = dict(c_ctx=c_ctx, norm_g=norm_g, ada_w=ada_w, ada_b=ada_b, pool_w_in=pool_w_in, pool_w_grp=pool_w_grp,
                  pool_scale=pool_scale, pool_w_out=pool_w_out, na_w_in=na_w_in, na_rpb=na_rpb, na_w_out=na_w_out,
                  conv_w_in=conv_w_in, conv_dw=conv_dw, conv_db=conv_db, conv_w_out=conv_w_out, final_g=final_g)
    mom1 = dict(c_ctx=m_c_ctx, norm_g=m_norm_g, ada_w=m_ada_w, ada_b=m_ada_b, pool_w_in=m_pool_w_in,
                pool_w_grp=m_pool_w_grp, pool_scale=m_pool_scale, pool_w_out=m_pool_w_out, na_w_in=m_na_w_in,
                na_rpb=m_na_rpb, na_w_out=m_na_w_out, conv_w_in=m_conv_w_in, conv_dw=m_conv_dw, conv_db=m_conv_db,
                conv_w_out=m_conv_w_out, final_g=m_final_g)
    mom2 = dict(c_ctx=v_c_ctx, norm_g=v_norm_g, ada_w=v_ada_w, ada_b=v_ada_b, pool_w_in=v_pool_w_in,
                pool_w_grp=v_pool_w_grp, pool_scale=v_pool_scale, pool_w_out=v_pool_w_out, na_w_in=v_na_w_in,
                na_rpb=v_na_rpb, na_w_out=v_na_w_out, conv_w_in=v_conv_w_in, conv_dw=v_conv_dw, conv_db=v_conv_db,
                conv_w_out=v_conv_w_out, final_g=v_final_g)
    d = x.shape[-1]
    w = na_w_out.shape[1] * 4
    xi, yi, ci = _position()
    chip = 2 * xi + yi
    dev = 2 * chip + ci
    n_ada = ada_w.shape[-1]

    def chip_cols(a, size):
        return lax.dynamic_slice_in_dim(a, chip * size, size, axis=a.ndim - 1)

    conds = _all_gather(c.reshape(8, d // 8), _AXES, "gather_cond").reshape(8, d)
    cond = jnp.concatenate([conds, c_ctx[None], jnp.zeros((_COND_ROWS - 9, d), F32)], axis=0)
    mod_cols = _modulations(cond, ada_w, chip_cols(ada_b, n_ada)[:, None, :])

    items = _items(d, w)
    first = [it for it in items if it.key.startswith("pool") and it.layer == 0]
    na_in, na_out = ([it for it in items if it.key == k] for k in ("na_w_in", "na_w_out"))
    late = [it for it in items if it not in first + na_in + na_out]
    shards_of = lambda its: [params[it.key][it.layer].astype(BF16) for it in its]
    small_pack, small_layout = _pack([pool_scale, conv_dw, conv_db])
    (mod_all,), first_mats, (small,) = _run_comms(
        [_all_gather_comm(mod_cols, ("x", "y")), _gather_comm(shards_of(first), first),
         _all_gather_comm(small_pack, ("x", "y"))], "gather_first")
    mod_all = mod_all.transpose(1, 2, 0, 3).reshape(4, _COND_ROWS, 3, d)
    mod = jnp.stack([lax.dynamic_index_in_dim(mod_all, dev, axis=1, keepdims=False), mod_all[:, 8]], axis=1)
    full = {(it.key, it.layer): mat for it, mat in zip(first, first_mats)}
    na_comms = dict(w_in_fwd=_gather_comm(shards_of(na_in), na_in), pool_fwd=_gather_comm(shards_of(na_out), na_out))
    late_comm = _gather_comm(shards_of(late), late)

    def na_weights(carried):
        return dict(na_w_in=carried["w_in_fwd"][0], na_w_out=carried["pool_fwd"][0])

    def late_weights(mats):
        full.update({(it.key, it.layer): mat for it, mat in zip(late, mats)})
        return dict(pool_w_in=[full[("pool_w_in", j)] for j in range(2)],
                    pool_w_grp=[full[("pool_w_grp", j)] for j in range(2)],
                    pool_w_out=[full[("pool_w_out", j)] for j in range(2)],
                    conv_w_in=full[("conv_w_in", 0)], conv_w_out=full[("conv_w_out", 0)])

    small = small.reshape(4, -1)

    def whole(k):
        parts = _unpack(small, small_layout, k)
        return jnp.moveaxis(parts, 0, -2).reshape(parts.shape[1:-1] + (-1,))

    wts = dict(pool_w_in=[full[("pool_w_in", 0)]], pool_w_grp=[full[("pool_w_grp", 0)]],
               pool_w_out=[full[("pool_w_out", 0)]], pool_scale=whole(0), na_rpb=na_rpb[0], conv_dw=whole(1)[0],
               conv_db=whole(2))
    pos = jnp.stack([chip, ci]).astype(jnp.int32)

    def layer_grads(its, by_layer):
        pick = {"pool_w_in": "w_in", "pool_w_grp": "w_grp", "pool_w_out": "w_out", "na_w_in": "w_in",
                "na_w_out": "w_out", "conv_w_in": "w_in", "conv_w_out": "w_out"}
        return [by_layer[(it.key.split("_")[0], it.layer)][pick[it.key]] for it in its]

    def pair_sums(its, mats, tag):
        got = _pair_swap(mats, [(lambda ref, half, it=it: it.window(ref, half=half)) for it in its],
                         [_sds(it.sized(half=True), BF16) for it in its], f"pair_exchange_{tag}")
        return _pair_sums(mats, got, its, pos, f"pair_sum_{tag}")

    pairs = dict()

    def grad_comm(gr3, gr2):
        pairs["late"] = pair_sums(late, layer_grads(late, {("pool", 1): gr3, ("conv", 0): gr2}), "late")
        return _chip_exchange_comm(pairs["late"], late)

    def na_grad_comms(gr1):
        na = na_in + na_out
        pairs["na"] = pair_sums(na, layer_grads(na, {("na", 0): gr1}), "na")
        return dict(w_in_bwd=_chip_exchange_comm(pairs["na"][:1], na_in),
                    w_out_bwd=_chip_exchange_comm(pairs["na"][1:], na_out))

    res = _example_step(x[0], ctx[0], loss_target[0], mod, norm_g, final_g[None], wts, na_comms, na_weights,
                        late_comm, late_weights, grad_comm, na_grad_comms)
    g0, g1, g2, g3 = res["layers"]
    first_grads = layer_grads(first, {("pool", 0): g0})
    packed, layout = _pack([res["dfinal_g"], res["dnorm_g"], res["dmod"], g1["rpb"],
                            jnp.concatenate([g0["scale"], g3["scale"]], axis=0), g2["dw"], g2["db"],
                            res["loss"][0, :1]])
    first_got, (every,) = _run_comms(
        [_pair_swap_comm(first_grads, [(lambda ref, half, it=it: it.window(ref, half=half)) for it in first],
                         [_sds(it.sized(half=True), BF16) for it in first]), _all_gather_comm(packed, _AXES)],
        "pair_exchange_first")
    pairs["first"] = _pair_sums(first_grads, first_got, first, pos, "pair_sum_first")
    xsend, xrecv, pairs["first"], zones, token = _chip_exchange_start(pairs["first"], first, "exchange_first_start")

    grads = dict()
    total = _sum_lead(every + token[0, 0], "sum_vec_grads").reshape(-1)
    every = every.reshape(8, -1)
    grads["final_g"] = _unpack(total, layout, 0).reshape(final_g.shape)
    grads["norm_g"] = _unpack(total, layout, 1)
    grads["na_rpb"] = _unpack(total, layout, 3)[None]
    grads["pool_scale"] = chip_cols(_unpack(total, layout, 4), pool_scale.shape[-1])
    grads["conv_dw"] = chip_cols(_unpack(total, layout, 5), conv_dw.shape[-1])[None]
    grads["conv_db"] = chip_cols(_unpack(total, layout, 6), conv_db.shape[-1])
    dmod_sum = _unpack(total, layout, 2).reshape(4, 2, 3 * d)
    dmod_each = _unpack(every, layout, 2).reshape(8, 4, 2, 3 * d)
    grads["ada_b"] = dmod_sum[:, 0] + dmod_sum[:, 1]
    dm = jnp.concatenate([dmod_each[:, :, 0].transpose(1, 0, 2), dmod_sum[:, 1][:, None],
                          jnp.zeros((4, _COND_ROWS - 9, 3 * d), F32)], axis=1)
    dm_cols = chip_cols(dm, n_ada)
    grads["ada_w"] = _ada_w_grad(cond, dm_cols)
    dcond = _cond_grad(dm_cols, ada_w)[8].reshape(8, d // 8)
    dcond_all = _all_gather(dcond, ("x", "y"), "gather_cond_grad")
    grads["c_ctx"] = _sum_lead(dcond_all, "sum_cond_grad").reshape(d) * _dsilu(c_ctx)
    vector_out = {k: _adamw(params[k], grads[k], mom1[k], mom2[k], f"adamw_{k}")
                  for k in _WEIGHTS if k not in _GRAD_KEYS}
    first_slots = _chip_exchange_wait(xsend, xrecv, pairs["first"], zones, vector_out["ada_w"][2], first,
                                      "exchange_first_wait")

    slots = dict(zip(late, res["carried"]))
    slots.update(zip(first, first_slots))
    slots.update(zip(na_in + na_out, res["carried0"]["w_in_bwd"] + res["carried0"]["w_out_bwd"]))
    pair_of = dict(zip(late, pairs["late"]))
    pair_of.update(zip(first, pairs["first"]))
    pair_of.update(zip(na_in + na_out, pairs["na"]))
    reduced = _chip_sums([pair_of[it] for it in items], [slots[it] for it in items], items, pos, "chip_sum")
    theirs = _pair_swap(reduced, [lambda ref, half: ref] * len(items),
                        [_sds(t.shape, F32) for t in reduced], "pair_return")
    matrix_out = dict()
    for k in _GRAD_KEYS:
        idx = [i for i, it in enumerate(items) if it.key == k]
        res_k = _adamw_matrix(params[k], mom1[k], mom2[k], [reduced[i] for i in idx], [theirs[i] for i in idx],
                              items[idx[0]], pos, f"adamw_{k}")
        grads[k], matrix_out[k] = res_k[0], res_k[1:]

    outs = [[], [], []]
    for k in _WEIGHTS:
        step = matrix_out[k] if k in matrix_out else vector_out[k]
        for lst, val in zip(outs, step):
            lst.append(val)
    loss = _unpack(total, layout, 7)[0]
    return (loss, res["grad_x"][None], *[grads[k].reshape(params[k].shape) for k in _WEIGHTS],
            *outs[0], *outs[1], *outs[2])
```

```python
import functools

import numpy as np
import jax
import jax.numpy as jnp
from jax import lax
from jax.experimental import pallas as pl
from jax.experimental.pallas import tpu as pltpu

F32 = jnp.float32
BF16 = jnp.bfloat16

EPS = 1e-6
GRID_W = 64
HEAD_DIM = 64
WIN_ROWS = 8
WIN_COLS = 16
POOL_WINDOWS = (2, 4, 8, 16)
Q_ROWS = 4
K_ROWS = 12
PAD_ROWS = 4
NEG = -1e30

ADAM_LR = 0.001
ADAM_B1 = 0.9
ADAM_B2 = 0.999
ADAM_EPS = 1e-08
ADAM_WD = 0.01
ADAM_STEP = 10

ROW_BLOCK = 256
VMEM_LIMIT = 56 * 1024 * 1024
ACT = BF16

MESH = pl.DeviceIdType.MESH
HBM_SPEC = pl.BlockSpec(memory_space=pltpu.HBM)


def _cparams(*sem):
    return pltpu.CompilerParams(dimension_semantics=sem or None, vmem_limit_bytes=VMEM_LIMIT)


def _sds(shape, dtype):
    return jax.ShapeDtypeStruct(tuple(shape), dtype)


def _call(body, args, *, grid, in_specs, out_specs, out_shape, name, scratch_shapes=(), comm=None):
    sem = ("arbitrary",) * len(grid)
    if comm is None:
        res = pl.pallas_call(body, grid=grid, in_specs=list(in_specs), out_specs=list(out_specs), out_shape=list(out_shape),
                             scratch_shapes=list(scratch_shapes), name=name, compiler_params=_cparams(*sem))(*args)
        return list(res), []
    n_in, n_out, n_scr = len(in_specs), len(out_specs), len(scratch_shapes)
    n_cin, n_cout = len(comm.ins), len(comm.outs)

    def carrying(*refs):
        ins, cin = refs[:n_in], refs[n_in:n_in + n_cin]
        outs = refs[n_in + n_cin:n_in + n_cin + n_out]
        cout = refs[n_in + n_cin + n_out:n_in + n_cin + n_out + n_cout]
        rest = refs[n_in + n_cin + n_out + n_cout:]
        scr, sems = rest[:n_scr], rest[n_scr:]
        first, last = True, True
        for ax, size in enumerate(grid):
            first = jnp.logical_and(first, pl.program_id(ax) == 0)
            last = jnp.logical_and(last, pl.program_id(ax) == size - 1)
        pl.when(first)(lambda: comm.start(cin, cout, sems))
        body(*ins, *outs, *scr)
        pl.when(last)(lambda: comm.finish(cin, cout, sems))

    res = pl.pallas_call(
        carrying, grid=grid, in_specs=list(in_specs) + [HBM_SPEC] * n_cin, out_specs=list(out_specs) + [HBM_SPEC] * n_cout,
        out_shape=list(out_shape) + list(comm.outs), scratch_shapes=list(scratch_shapes) + list(comm.sems), name=name,
        compiler_params=_cparams(*sem),
    )(*args, *comm.ins)
    return list(res[:n_out]), list(res[n_out:])


def _sigmoid(x):
    return 1.0 / (1.0 + jnp.exp(-x))


def _silu(x):
    return x * _sigmoid(x)


def _dsilu(x):
    s = _sigmoid(x)
    return s * (1.0 + x * (1.0 - s))


_DIMS = {
    "nn": (((1,), (0,)), ((), ())),
    "nt": (((1,), (1,)), ((), ())),
    "tn": (((0,), (0,)), ((), ())),
}


def _matmul(a, b, *, mode, grid, a_spec, b_spec, out_shapes, out_specs, name, nk=1,
            a_silu=False, exact=False, epilogue=None, extra=(), extra_specs=(), acc_shape=None):
    n_extra = len(extra)
    n_out = len(out_shapes)

    def body(*refs):
        a_ref, b_ref = refs[:2]
        ex = refs[2:2 + n_extra]
        outs = refs[2 + n_extra:2 + n_extra + n_out]
        av = a_ref[...]
        bv = b_ref[...]
        if a_silu:
            av = _silu(av.astype(F32))
        if exact:
            prod = lax.dot_general(av.astype(F32), bv.astype(F32), _DIMS[mode],
                                   precision=lax.Precision.HIGHEST, preferred_element_type=F32)
        else:
            prod = lax.dot_general(av.astype(BF16), bv.astype(BF16), _DIMS[mode], preferred_element_type=F32)

        def finish(res):
            if epilogue is None:
                outs[0][...] = res.astype(outs[0].dtype)
            elif epilogue == "bias":
                outs[0][...] = (res + ex[0][...]).astype(outs[0].dtype)
            else:
                outs[0][...] = res.astype(outs[0].dtype)
                outs[1][...] = ex[0][...] + ex[1][...] * res

        if nk == 1:
            finish(prod)
        else:
            acc = refs[-1]
            k = pl.program_id(len(grid) - 1)

            @pl.when(k == 0)
            def _():
                acc[...] = prod

            @pl.when(k > 0)
            def _():
                acc[...] += prod

            @pl.when(k == nk - 1)
            def _():
                finish(acc[...])

    scratch = [pltpu.VMEM(acc_shape, F32)] if nk > 1 else []
    sem = ("parallel",) * (len(grid) - 1) + ("arbitrary",)
    return pl.pallas_call(
        body, grid=grid, in_specs=[a_spec, b_spec, *extra_specs], out_specs=list(out_specs),
        out_shape=list(out_shapes), scratch_shapes=scratch, name=name, compiler_params=_cparams(*sem),
    )(a, b, *extra)


def _row_tile(rows):
    for t in (768, 512, 256):
        if rows % t == 0:
            return t
    return rows


def _mm_nn(a, b, name, out_dtype=F32, tn=1024):
    m, k = a.shape
    n = b.shape[1]
    tm = _row_tile(m)
    tn = min(tn, n)
    return _matmul(
        a, b, mode="nn", grid=(m // tm, n // tn),
        a_spec=pl.BlockSpec((tm, k), lambda i, j: (i, 0)), b_spec=pl.BlockSpec((k, tn), lambda i, j: (0, j)),
        out_shapes=[_sds((m, n), out_dtype)], out_specs=[pl.BlockSpec((tm, tn), lambda i, j: (i, j))], name=name)[0]


def _mm_out_resid(a, w_out, xres, gate, nxb, name):
    m, k = a.shape
    n = w_out.shape[1]
    tm = ROW_BLOCK
    seg = lambda i, j: (jnp.where(i >= nxb, 1, 0), 0, 0)
    return _matmul(
        a, w_out, mode="nn", grid=(m // tm, 1),
        a_spec=pl.BlockSpec((tm, k), lambda i, j: (i, 0)), b_spec=pl.BlockSpec((k, n), lambda i, j: (0, 0)),
        extra=(xres, gate), extra_specs=(pl.BlockSpec((tm, n), lambda i, j: (i, 0)), pl.BlockSpec((None, 1, n), seg)),
        out_shapes=[_sds((m, n), ACT), _sds((m, n), F32)],
        out_specs=[pl.BlockSpec((tm, n), lambda i, j: (i, 0))] * 2, epilogue="resid", name=name)


def _mm_nt(a, b, name, out_dtype=F32):
    m, n = a.shape
    k = b.shape[0]
    tm = _row_tile(m)
    return _matmul(
        a, b, mode="nt", grid=(m // tm, 1),
        a_spec=pl.BlockSpec((tm, n), lambda i, j: (i, 0)), b_spec=pl.BlockSpec((k, n), lambda i, j: (0, 0)),
        out_shapes=[_sds((m, k), out_dtype)], out_specs=[pl.BlockSpec((tm, k), lambda i, j: (i, 0))], name=name)[0]


def _mm_nt_parts(a, b, name):
    p, m, kp = a.shape
    d = b.shape[0]
    tm = _row_tile(m)
    return _matmul(
        a, b, mode="nt", grid=(m // tm, p), nk=p, acc_shape=(tm, d),
        a_spec=pl.BlockSpec((None, tm, kp), lambda i, q: (q, i, 0)), b_spec=pl.BlockSpec((d, kp), lambda i, q: (0, q)),
        out_shapes=[_sds((m, d), F32)], out_specs=[pl.BlockSpec((tm, d), lambda i, q: (i, 0))], name=name)[0]


def _mm_tn(a, b, name, out_dtype, tm=512):
    r, m = a.shape
    n = b.shape[1]
    tm = min(tm, m)
    tn = min(1024, n)
    return _matmul(
        a, b, mode="tn", grid=(m // tm, n // tn),
        a_spec=pl.BlockSpec((r, tm), lambda i, j: (0, i)), b_spec=pl.BlockSpec((r, tn), lambda i, j: (0, j)),
        out_shapes=[_sds((m, n), out_dtype)], out_specs=[pl.BlockSpec((tm, tn), lambda i, j: (i, j))], name=name)[0]


def _mm_tn_parts(a, b, name, out_dtype, tm=512):
    r, m = a.shape
    p, _, np_ = b.shape
    tm = min(tm, m)
    return _matmul(
        a, b, mode="tn", grid=(m // tm, p),
        a_spec=pl.BlockSpec((r, tm), lambda i, q: (0, i)), b_spec=pl.BlockSpec((None, r, np_), lambda i, q: (q, 0, 0)),
        out_shapes=[_sds((m, p * np_), out_dtype)], out_specs=[pl.BlockSpec((tm, np_), lambda i, q: (i, q))],
        name=name)[0]


def _seg_map(nxb):
    return lambda i: (jnp.where(i >= nxb, 1, 0), 0, 0)


def _normmod_fwd(x, g, scale, shift, nxb, name):
    rows, d = x.shape
    tr = ROW_BLOCK

    def body(x_ref, g_ref, sc_ref, sh_ref, h_ref, r_ref):
        xv = x_ref[...]
        r = lax.rsqrt(jnp.mean(xv * xv, axis=-1, keepdims=True) + EPS)
        h = (xv * r) * g_ref[...] * (1.0 + sc_ref[...]) + sh_ref[...]
        h_ref[...] = h.astype(BF16)
        r_ref[...] = r

    row = pl.BlockSpec((tr, d), lambda i: (i, 0))
    vec = pl.BlockSpec((None, 1, d), _seg_map(nxb))
    return pl.pallas_call(
        body, grid=(rows // tr,), in_specs=[row, pl.BlockSpec((1, d), lambda i: (0, 0)), vec, vec],
        out_specs=[row, pl.BlockSpec((tr, 1), lambda i: (i, 0))],
        out_shape=[_sds((rows, d), BF16), _sds((rows, 1), F32)], name=name, compiler_params=_cparams("parallel"),
    )(x, g, scale, shift)


def _normmod_bwd(dh, x, r, g, scale, dres, nxb, name):
    rows, d = x.shape
    tr = ROW_BLOCK
    nres = dres.shape[0] // tr
    nseg = scale.shape[0]

    def body(dh_ref, x_ref, r_ref, g_ref, sc_ref, dres_ref, dx_ref, dsh_ref, dge_ref):
        i = pl.program_id(0)
        dhv = dh_ref[...]
        rv = r_ref[...]
        xn = x_ref[...] * rv
        dxn = dhv * (g_ref[...] * (1.0 + sc_ref[...]))
        dx = rv * (dxn - xn * jnp.mean(dxn * xn, axis=-1, keepdims=True))

        @pl.when(i < nres)
        def _():
            dx_ref[...] = dx + dres_ref[...]

        @pl.when(i >= nres)
        def _():
            dx_ref[...] = dx

        first = jnp.logical_or(i == 0, i == nxb)
        s_dh = jnp.sum(dhv, axis=0, keepdims=True)
        s_ge = jnp.sum(dhv * xn, axis=0, keepdims=True)

        @pl.when(first)
        def _():
            dsh_ref[...] = s_dh
            dge_ref[...] = s_ge

        @pl.when(jnp.logical_not(first))
        def _():
            dsh_ref[...] += s_dh
            dge_ref[...] += s_ge

    row = pl.BlockSpec((tr, d), lambda i: (i, 0))
    vec = pl.BlockSpec((None, 1, d), _seg_map(nxb))
    return pl.pallas_call(
        body, grid=(rows // tr,),
        in_specs=[row, row, pl.BlockSpec((tr, 1), lambda i: (i, 0)), pl.BlockSpec((1, d), lambda i: (0, 0)), vec,
                  pl.BlockSpec((tr, d), lambda i: (jnp.minimum(i, nres - 1), 0))],
        out_specs=[row, vec, vec],
        out_shape=[_sds((rows, d), F32), _sds((nseg, 1, d), F32), _sds((nseg, 1, d), F32)],
        name=name, compiler_params=_cparams("arbitrary"),
    )(dh, x, r, g, scale, dres)


def _gate_bwd(dxo, yx, gate, nxb, name):
    rows, d = yx.shape
    tr = ROW_BLOCK
    nseg = gate.shape[0]

    def body(dx_ref, yx_ref, gt_ref, dyx_ref, dg_ref):
        i = pl.program_id(0)
        dxv = dx_ref[...]
        dyx_ref[...] = (dxv * gt_ref[...]).astype(BF16)
        s = jnp.sum(dxv * yx_ref[...].astype(F32), axis=0, keepdims=True)
        first = jnp.logical_or(i == 0, i == nxb)

        @pl.when(first)
        def _():
            dg_ref[...] = s

        @pl.when(jnp.logical_not(first))
        def _():
            dg_ref[...] += s

    row = pl.BlockSpec((tr, d), lambda i: (i, 0))
    vec = pl.BlockSpec((None, 1, d), _seg_map(nxb))
    return pl.pallas_call(
        body, grid=(rows // tr,), in_specs=[row, row, vec], out_specs=[row, vec],
        out_shape=[_sds((rows, d), BF16), _sds((nseg, 1, d), F32)], name=name, compiler_params=_cparams("arbitrary"),
    )(dxo, yx, gate)


def _row_vec(ref, is_ctx):
    return ref[0] if is_ctx is None else jnp.where(is_ctx, ref[1], ref[0])


def _ctx_rows(i, tm, nx, nseg):
    if nseg == 1:
        return None
    return i * tm + lax.broadcasted_iota(jnp.int32, (tm, 1), 0) >= nx


def _seg_sums(ref, val, is_ctx, first):
    if is_ctx is None:
        parts = [jnp.sum(val, axis=0, keepdims=True)]
    else:
        parts = [jnp.sum(jnp.where(is_ctx, 0.0, val), axis=0, keepdims=True),
                 jnp.sum(jnp.where(is_ctx, val, 0.0), axis=0, keepdims=True)]

    @pl.when(first)
    def _():
        for k, p in enumerate(parts):
            ref[k] = p

    @pl.when(jnp.logical_not(first))
    def _():
        for k, p in enumerate(parts):
            ref[k] += p


def _w_out_resid(a, w_out, xres, gate, nx, name):
    m, k = a.shape
    n = w_out.shape[1]
    nseg = gate.shape[0]
    tm = _row_tile(m)

    def body(a_ref, w_ref, x_ref, gt_ref, yx_ref, xo_ref):
        yx = jnp.dot(a_ref[...], w_ref[...], preferred_element_type=F32)
        yx_ref[...] = yx.astype(ACT)
        xo_ref[...] = x_ref[...] + _row_vec(gt_ref, _ctx_rows(pl.program_id(0), tm, nx, nseg)) * yx

    row = pl.BlockSpec((tm, n), lambda i: (i, 0))
    return pl.pallas_call(
        body, grid=(m // tm,),
        in_specs=[pl.BlockSpec((tm, k), lambda i: (i, 0)), pl.BlockSpec((k, n), lambda i: (0, 0)), row,
                  pl.BlockSpec((nseg, 1, n), lambda i: (0, 0, 0))],
        out_specs=[row, row], out_shape=[_sds((m, n), ACT), _sds((m, n), F32)],
        name=name, compiler_params=_cparams("parallel"),
    )(a, w_out, xres, gate)


def _norm_w_in(x, g, scale, shift, w_in, nx, name, comm=None):
    rows, d = x.shape
    n = w_in.shape[1]
    nseg = scale.shape[0]
    tm = _row_tile(rows)
    tn = min(1024, n)

    def body(x_ref, g_ref, sc_ref, sh_ref, w_ref, h_ref, r_ref, p_ref):
        i, j = pl.program_id(0), pl.program_id(1)

        @pl.when(j == 0)
        def _():
            xv = x_ref[...]
            r = lax.rsqrt(jnp.mean(xv * xv, axis=-1, keepdims=True) + EPS)
            is_ctx = _ctx_rows(i, tm, nx, nseg)
            h = (xv * r) * g_ref[...] * (1.0 + _row_vec(sc_ref, is_ctx)) + _row_vec(sh_ref, is_ctx)
            h_ref[...] = h.astype(BF16)
            r_ref[...] = r

        p_ref[...] = jnp.dot(h_ref[...], w_ref[...], preferred_element_type=F32).astype(ACT)

    vec = pl.BlockSpec((nseg, 1, d), lambda i, j: (0, 0, 0))
    return _call(
        body, (x, g, scale, shift, w_in), grid=(rows // tm, n // tn),
        in_specs=[pl.BlockSpec((tm, d), lambda i, j: (i, 0)), pl.BlockSpec((1, d), lambda i, j: (0, 0)), vec, vec,
                  pl.BlockSpec((d, tn), lambda i, j: (0, j))],
        out_specs=[pl.BlockSpec((tm, d), lambda i, j: (i, 0)), pl.BlockSpec((tm, 1), lambda i, j: (i, 0)),
                   pl.BlockSpec((tm, tn), lambda i, j: (i, j))],
        out_shape=[_sds((rows, d), BF16), _sds((rows, 1), F32), _sds((rows, n), ACT)], name=name, comm=comm)


def _gate_w_out_bwd(dxo, yx, gate, w_out, nx, name, comm=None):
    rows, d = yx.shape
    w = w_out.shape[0]
    nseg = gate.shape[0]
    tm = _row_tile(rows)

    def body(dx_ref, yx_ref, gt_ref, w_ref, dyx_ref, da_ref, dg_ref):
        i = pl.program_id(0)
        is_ctx = _ctx_rows(i, tm, nx, nseg)
        dxv = dx_ref[...]
        dyx = (dxv * _row_vec(gt_ref, is_ctx)).astype(BF16)
        dyx_ref[...] = dyx
        da_ref[...] = lax.dot_general(dyx, w_ref[...], _DIMS["nt"], preferred_element_type=F32).astype(ACT)
        _seg_sums(dg_ref, dxv * yx_ref[...].astype(F32), is_ctx, i == 0)

    row = pl.BlockSpec((tm, d), lambda i: (i, 0))
    vec = pl.BlockSpec((nseg, 1, d), lambda i: (0, 0, 0))
    return _call(
        body, (dxo, yx, gate, w_out), grid=(rows // tm,),
        in_specs=[row, row, vec, pl.BlockSpec((w, d), lambda i: (0, 0))],
        out_specs=[row, pl.BlockSpec((tm, w), lambda i: (i, 0)), vec],
        out_shape=[_sds((rows, d), BF16), _sds((rows, w), ACT), _sds((nseg, 1, d), F32)], name=name, comm=comm)


def _w_in_bwd_norm(dparts, w_in, x, r, g, scale, dres, nx, name, comm=None):
    np_, rows, kp = dparts.shape
    d = w_in.shape[0]
    nseg = scale.shape[0]
    tm = _row_tile(rows)
    nsub = tm // ROW_BLOCK
    nres_blocks = dres.shape[0] // ROW_BLOCK

    def body(dp_ref, w_ref, x_ref, r_ref, g_ref, sc_ref, *rest):
        dres_refs = rest[:nsub]
        dx_ref, dsh_ref, dge_ref, acc = rest[nsub:]
        i, k = pl.program_id(0), pl.program_id(1)
        prod = lax.dot_general(dp_ref[...], w_ref[...], _DIMS["nt"], preferred_element_type=F32)

        @pl.when(k == 0)
        def _():
            acc[...] = prod

        @pl.when(k > 0)
        def _():
            acc[...] += prod

        @pl.when(k == np_ - 1)
        def _():
            is_ctx = _ctx_rows(i, tm, nx, nseg)
            dhv = acc[...]
            rv = r_ref[...]
            xn = x_ref[...] * rv
            dxn = dhv * (g_ref[...] * (1.0 + _row_vec(sc_ref, is_ctx)))
            dx = rv * (dxn - xn * jnp.mean(dxn * xn, axis=-1, keepdims=True))
            for s in range(nsub):
                piece = slice(s * ROW_BLOCK, (s + 1) * ROW_BLOCK)
                res = dres_refs[s][...]
                if nres_blocks * ROW_BLOCK < rows:
                    res = jnp.where(i * nsub + s < nres_blocks, res, 0.0)
                dx_ref[piece, :] = dx[piece, :] + res
            _seg_sums(dsh_ref, dhv, is_ctx, i == 0)
            _seg_sums(dge_ref, dhv * xn, is_ctx, i == 0)

    row = pl.BlockSpec((tm, d), lambda i, k: (i, 0))
    vec = pl.BlockSpec((nseg, 1, d), lambda i, k: (0, 0, 0))
    return _call(
        body, (dparts, w_in, x, r, g, scale, *([dres] * nsub)), grid=(rows // tm, np_),
        in_specs=[pl.BlockSpec((None, tm, kp), lambda i, k: (k, i, 0)), pl.BlockSpec((d, kp), lambda i, k: (0, k)),
                  row, pl.BlockSpec((tm, 1), lambda i, k: (i, 0)), pl.BlockSpec((1, d), lambda i, k: (0, 0)), vec]
        + [pl.BlockSpec((ROW_BLOCK, d), (lambda i, k, s=s: (jnp.minimum(i * nsub + s, nres_blocks - 1), 0)))
           for s in range(nsub)],
        out_specs=[row, vec, vec],
        out_shape=[_sds((rows, d), F32), _sds((nseg, 1, d), F32), _sds((nseg, 1, d), F32)],
        scratch_shapes=[pltpu.VMEM((tm, d), F32)], name=name, comm=comm)


_PAD_TOP = 16
_PAD_BOT = 32


def _window_sum(buf, xv, lo, n):
    t = xv.shape[0]
    c = xv.shape[1]
    tp = t + _PAD_TOP + _PAD_BOT
    buf[pl.ds(0, _PAD_TOP), :] = jnp.zeros((_PAD_TOP, c), F32)
    buf[pl.ds(_PAD_TOP, t), :] = xv
    buf[pl.ds(_PAD_TOP + t, _PAD_BOT), :] = jnp.zeros((_PAD_BOT, c), F32)
    p = buf[...]
    k = 1
    while k < n:
        p = p + pltpu.roll(p, tp - k, 0)
        k *= 2
    if lo:
        p = pltpu.roll(p, -lo, 0)
    buf[...] = p
    return buf[pl.ds(_PAD_TOP, t), :]


def _window_count(t, half):
    pos = lax.broadcasted_iota(jnp.int32, (t, 1), 0)
    return (jnp.minimum(pos + half, t) - jnp.maximum(pos - half, 0)).astype(F32)


def _segments(rows, nx):
    return [(0, nx)] + ([(nx, rows - nx)] if rows > nx else [])


def _pool_fwd(uv, nx, name):
    rows = uv.shape[0]
    w = uv.shape[1] // 2
    cb = 128
    per_group = w // len(POOL_WINDOWS) // cb
    segs = _segments(rows, nx)

    def body(u_ref, z_ref, *bufs):
        j = pl.program_id(0)
        for gi, win in enumerate(POOL_WINDOWS):
            half = win // 2

            @pl.when(jnp.logical_and(j >= gi * per_group, j < (gi + 1) * per_group))
            def _():
                for (start, length), buf in zip(segs, bufs):
                    uvv = u_ref[pl.ds(start, length), :].astype(F32)
                    s = _window_sum(buf, uvv, -half, win)
                    z_ref[pl.ds(start, length), :] = (s / _window_count(length, half) - uvv).astype(BF16)

    scratch = [pltpu.VMEM((length + _PAD_TOP + _PAD_BOT, cb), F32) for _, length in segs]
    return pl.pallas_call(
        body, grid=(w // cb,), in_specs=[pl.BlockSpec((rows, cb), lambda j: (0, j))],
        out_specs=pl.BlockSpec((rows, cb), lambda j: (0, j)), out_shape=_sds((rows, w), BF16),
        scratch_shapes=scratch, name=name, compiler_params=_cparams("parallel"),
    )(uv)


def _pool_bwd(dz, dgt, nx, name):
    rows, w = dz.shape
    cb = 128
    per_group = w // len(POOL_WINDOWS) // cb
    segs = _segments(rows, nx)

    def body(dz_ref, dgt_ref, o_ref, *bufs):
        j = pl.program_id(0)
        o_ref[1] = dgt_ref[...]
        for gi, win in enumerate(POOL_WINDOWS):
            half = win // 2

            @pl.when(jnp.logical_and(j >= gi * per_group, j < (gi + 1) * per_group))
            def _():
                for (start, length), buf in zip(segs, bufs):
                    dzv = dz_ref[pl.ds(start, length), :].astype(F32)
                    s = _window_sum(buf, dzv / _window_count(length, half), 1 - half, win)
                    o_ref[0, pl.ds(start, length), :] = (s - dzv).astype(BF16)

    scratch = [pltpu.VMEM((length + _PAD_TOP + _PAD_BOT, cb), F32) for _, length in segs]
    col = pl.BlockSpec((rows, cb), lambda j: (0, j))
    return pl.pallas_call(
        body, grid=(w // cb,), in_specs=[col, col], out_specs=pl.BlockSpec((2, rows, cb), lambda j: (0, 0, j)),
        out_shape=_sds((2, rows, w), BF16), scratch_shapes=scratch, name=name, compiler_params=_cparams("parallel"),
    )(dz, dgt)


def _grp_fwd(z, w_grp, uv, scale, name):
    rows, w = z.shape
    ng, gc, _ = w_grp.shape
    tm = _row_tile(rows)

    def body(z_ref, w_ref, gt_ref, sc_ref, mx_ref, a_ref):
        mixed = jnp.dot(z_ref[...], w_ref[...], preferred_element_type=F32)
        mx_ref[...] = mixed.astype(ACT)
        a_ref[...] = (mixed * sc_ref[...] * _silu(gt_ref[...].astype(F32))).astype(BF16)

    blk = pl.BlockSpec((tm, gc), lambda g, i: (i, g))
    return pl.pallas_call(
        body, grid=(ng, rows // tm),
        in_specs=[blk, pl.BlockSpec((None, gc, gc), lambda g, i: (g, 0, 0)),
                  pl.BlockSpec((tm, gc), lambda g, i: (i, ng + g)), pl.BlockSpec((1, gc), lambda g, i: (0, g))],
        out_specs=[blk, blk], out_shape=[_sds((rows, w), ACT), _sds((rows, w), BF16)],
        name=name, compiler_params=_cparams("parallel", "parallel"),
    )(z, w_grp, uv, scale)


def _grp_bwd(da, mixed, uv, scale, w_grp, name):
    rows, w = da.shape
    ng, gc, _ = w_grp.shape
    tm = _row_tile(rows)

    def body(da_ref, mx_ref, gt_ref, sc_ref, w_ref, dm_ref, dz_ref, dgt_ref, dsc_ref):
        i = pl.program_id(1)
        dav = da_ref[...].astype(F32)
        mixed = mx_ref[...].astype(F32)
        gt = gt_ref[...].astype(F32)
        sg = _silu(gt)
        sc = sc_ref[...]
        dm = (dav * sc * sg).astype(BF16)
        dm_ref[...] = dm
        dz_ref[...] = lax.dot_general(dm, w_ref[...], _DIMS["nt"], preferred_element_type=F32).astype(ACT)
        dgt_ref[...] = (dav * mixed * sc * _dsilu(gt)).astype(BF16)
        s = jnp.sum(dav * mixed * sg, axis=0, keepdims=True)

        @pl.when(i == 0)
        def _():
            dsc_ref[...] = s

        @pl.when(i > 0)
        def _():
            dsc_ref[...] += s

    blk = pl.BlockSpec((tm, gc), lambda g, i: (i, g))
    vec = pl.BlockSpec((1, gc), lambda g, i: (0, g))
    return pl.pallas_call(
        body, grid=(ng, rows // tm),
        in_specs=[blk, blk, pl.BlockSpec((tm, gc), lambda g, i: (i, ng + g)), vec,
                  pl.BlockSpec((None, gc, gc), lambda g, i: (g, 0, 0))],
        out_specs=[blk, blk, blk, vec],
        out_shape=[_sds((rows, w), BF16), _sds((rows, w), ACT), _sds((rows, w), BF16), _sds((1, w), F32)],
        name=name, compiler_params=_cparams("parallel", "arbitrary"),
    )(da, mixed, uv, scale, w_grp)


def _pool_scratch(rows, nx, cols):
    return [pltpu.VMEM((length + _PAD_TOP + _PAD_BOT, cols), F32) for _, length in _segments(rows, nx)]


def _per_group(g, fn):
    for gi, win in enumerate(POOL_WINDOWS):
        pl.when(g == gi)(functools.partial(fn, win))


def _pool_grp_fwd(uv, w_grp, scale, nx, name, comm=None):
    rows = uv.shape[0]
    ng, gc, _ = w_grp.shape
    w = ng * gc
    segs = _segments(rows, nx)

    def body(u_ref, gt_ref, w_ref, sc_ref, z_ref, mx_ref, a_ref, *bufs):
        def pool(win):
            half = win // 2
            for (start, length), buf in zip(segs, bufs):
                uvv = u_ref[pl.ds(start, length), :].astype(F32)
                s = _window_sum(buf, uvv, -half, win)
                z_ref[pl.ds(start, length), :] = (s / _window_count(length, half) - uvv).astype(BF16)

        _per_group(pl.program_id(0), pool)
        mixed = jnp.dot(z_ref[...], w_ref[...], preferred_element_type=F32)
        mx_ref[...] = mixed.astype(ACT)
        a_ref[...] = (mixed * sc_ref[...] * _silu(gt_ref[...].astype(F32))).astype(BF16)

    col = pl.BlockSpec((rows, gc), lambda g: (0, g))
    return _call(
        body, (uv, uv, w_grp, scale), grid=(ng,),
        in_specs=[col, pl.BlockSpec((rows, gc), lambda g: (0, ng + g)), pl.BlockSpec((None, gc, gc), lambda g: (g, 0, 0)),
                  pl.BlockSpec((1, gc), lambda g: (0, g))],
        out_specs=[col, col, col], out_shape=[_sds((rows, w), BF16), _sds((rows, w), ACT), _sds((rows, w), BF16)],
        scratch_shapes=_pool_scratch(rows, nx, gc), name=name, comm=comm)


def _pool_grp_bwd(da, mixed, uv, scale, w_grp, nx, name):
    rows, w = da.shape
    ng, gc, _ = w_grp.shape
    segs = _segments(rows, nx)

    def body(da_ref, mx_ref, gt_ref, sc_ref, w_ref, dm_ref, duv_ref, dsc_ref, dz_ref, *bufs):
        dav = da_ref[...].astype(F32)
        mixed = mx_ref[...].astype(F32)
        gt = gt_ref[...].astype(F32)
        sg = _silu(gt)
        sc = sc_ref[...]
        dm = (dav * sc * sg).astype(BF16)
        dm_ref[...] = dm
        dz_ref[...] = lax.dot_general(dm, w_ref[...], _DIMS["nt"], preferred_element_type=F32)
        duv_ref[1] = (dav * mixed * sc * _dsilu(gt)).astype(BF16)
        dsc_ref[...] = jnp.sum(dav * mixed * sg, axis=0, keepdims=True)

        def unpool(win):
            half = win // 2
            for (start, length), buf in zip(segs, bufs):
                dzv = dz_ref[pl.ds(start, length), :]
                s = _window_sum(buf, dzv / _window_count(length, half), 1 - half, win)
                duv_ref[0, pl.ds(start, length), :] = (s - dzv).astype(BF16)

        _per_group(pl.program_id(0), unpool)

    col = pl.BlockSpec((rows, gc), lambda g: (0, g))
    vec = pl.BlockSpec((1, gc), lambda g: (0, g))
    return pl.pallas_call(
        body, grid=(ng,),
        in_specs=[col, col, pl.BlockSpec((rows, gc), lambda g: (0, ng + g)), vec,
                  pl.BlockSpec((None, gc, gc), lambda g: (g, 0, 0))],
        out_specs=[col, pl.BlockSpec((2, rows, gc), lambda g: (0, 0, g)), vec],
        out_shape=[_sds((rows, w), BF16), _sds((2, rows, w), BF16), _sds((1, w), F32)],
        scratch_shapes=[pltpu.VMEM((rows, gc), F32)] + _pool_scratch(rows, nx, gc),
        name=name, compiler_params=_cparams("parallel"),
    )(da, mixed, uv, scale, w_grp)


def _grp_wgrad(z, dm, ng, name, out_dtype):
    rows, w = z.shape
    gc = w // ng

    def body(z_ref, dm_ref, o_ref):
        o_ref[...] = lax.dot_general(z_ref[...], dm_ref[...], _DIMS["tn"],
                                     preferred_element_type=F32).astype(o_ref.dtype)

    blk = pl.BlockSpec((rows, gc), lambda g: (0, g))
    return pl.pallas_call(
        body, grid=(ng,), in_specs=[blk, blk], out_specs=pl.BlockSpec((None, gc, gc), lambda g: (g, 0, 0)),
        out_shape=_sds((ng, gc, gc), out_dtype), name=name, compiler_params=_cparams("parallel"),
    )(z, dm)


def _shift_rows(v, by):
    t = v.shape[0]
    pos = lax.broadcasted_iota(jnp.int32, v.shape, 0)
    rolled = pltpu.roll(v, by % t, 0)
    keep = pos >= by if by > 0 else pos < t + by
    return jnp.where(keep, rolled, 0.0)


def _conv_specs(t, w, cb):
    return [pl.BlockSpec((t, cb), (lambda j, q=q: (0, q * (w // cb) + j))) for q in range(4)]


def _conv_fwd(p4, dw, db, name):
    t = p4.shape[0]
    w = p4.shape[1] // 4
    cb = 128

    def body(bg_ref, cg_ref, v_ref, g_ref, dw_ref, db_ref, a_ref):
        tv = cg_ref[...].astype(F32) * v_ref[...].astype(F32)
        conv = (dw_ref[0:1, :] * _shift_rows(tv, 1) + dw_ref[1:2, :] * tv + dw_ref[2:3, :] * _shift_rows(tv, -1)
                + db_ref[...])
        a_ref[...] = (bg_ref[...].astype(F32) * conv * _silu(g_ref[...].astype(F32))).astype(BF16)

    return pl.pallas_call(
        body, grid=(w // cb,),
        in_specs=_conv_specs(t, w, cb) + [pl.BlockSpec((3, cb), lambda j: (0, j)), pl.BlockSpec((1, cb), lambda j: (0, j))],
        out_specs=pl.BlockSpec((t, cb), lambda j: (0, j)), out_shape=_sds((t, w), BF16),
        name=name, compiler_params=_cparams("parallel"),
    )(p4, p4, p4, p4, dw, db)


def _conv_bwd(da, p4, dw, db, name):
    t, w = da.shape
    cb = 128

    def body(da_ref, bg_ref, cg_ref, v_ref, g_ref, dw_ref, db_ref, d4_ref, ddw_ref, ddb_ref):
        cg = cg_ref[...].astype(F32)
        vv = v_ref[...].astype(F32)
        bg = bg_ref[...].astype(F32)
        gv = g_ref[...].astype(F32)
        tv = cg * vv
        tm1 = _shift_rows(tv, 1)
        tp1 = _shift_rows(tv, -1)
        w0, w1, w2 = dw_ref[0:1, :], dw_ref[1:2, :], dw_ref[2:3, :]
        conv = w0 * tm1 + w1 * tv + w2 * tp1 + db_ref[...]
        y = bg * conv
        dav = da_ref[...].astype(F32)
        dy = dav * _silu(gv)
        d4_ref[3] = (dav * y * _dsilu(gv)).astype(BF16)
        d4_ref[0] = (dy * conv).astype(BF16)
        dconv = dy * bg
        ddb_ref[...] = jnp.sum(dconv, axis=0, keepdims=True)
        ddw_ref[0:1, :] = jnp.sum(dconv * tm1, axis=0, keepdims=True)
        ddw_ref[1:2, :] = jnp.sum(dconv * tv, axis=0, keepdims=True)
        ddw_ref[2:3, :] = jnp.sum(dconv * tp1, axis=0, keepdims=True)
        dt = w0 * _shift_rows(dconv, -1) + w1 * dconv + w2 * _shift_rows(dconv, 1)
        d4_ref[1] = (dt * vv).astype(BF16)
        d4_ref[2] = (dt * cg).astype(BF16)

    col = pl.BlockSpec((t, cb), lambda j: (0, j))
    tap = pl.BlockSpec((3, cb), lambda j: (0, j))
    bias = pl.BlockSpec((1, cb), lambda j: (0, j))
    return pl.pallas_call(
        body, grid=(w // cb,), in_specs=[col] + _conv_specs(t, w, cb) + [tap, bias],
        out_specs=[pl.BlockSpec((4, t, cb), lambda j: (0, 0, j)), tap, bias],
        out_shape=[_sds((4, t, w), BF16), _sds((3, w), F32), _sds((1, w), F32)],
        name=name, compiler_params=_cparams("parallel"),
    )(da, p4, p4, p4, p4, dw, db)


def _attn_mask():
    qn, kn = Q_ROWS * GRID_W, K_ROWS * GRID_W
    qr, qc = np.divmod(np.arange(qn), GRID_W)
    kr, kc = np.divmod(np.arange(kn), GRID_W)
    col0 = np.clip(qc - WIN_COLS // 2, 0, GRID_W - WIN_COLS)
    col_ok = (kc[None, :] >= col0[:, None]) & (kc[None, :] < col0[:, None] + WIN_COLS)
    first = np.zeros(qn, np.int64)
    last = np.full(qn, K_ROWS - WIN_ROWS)
    out = []
    for row0 in (first, qr, last):
        row_ok = (kr[None, :] >= row0[:, None]) & (kr[None, :] < row0[:, None] + WIN_ROWS)
        out.append(np.where(row_ok & col_ok, 0.0, NEG))
    return jnp.asarray(np.stack(out), F32)


_KW = K_ROWS * GRID_W
_QB = Q_ROWS * GRID_W
_PAIR = 2 * HEAD_DIM
_N_DR = 2 * WIN_ROWS - 1
_N_DC = 2 * WIN_COLS - 1
_RP_ROWS = 24
_N_TILES = _N_DR + 1
_BIAS_BASE = (WIN_ROWS - 1, WIN_ROWS // 2 - 1, -1)


class _Comm:
    def __init__(self, ins, outs, sems, start, finish):
        self.ins, self.outs, self.sems, self.start, self.finish = list(ins), list(outs), list(sems), start, finish


def _bias_pieces(cls):
    out = []
    for qr in range(Q_ROWS):
        for kr in range(0, K_ROWS, 2):
            tile = _BIAS_BASE[cls] - qr + kr + 1
            out.append((qr, kr, tile if 0 <= tile < _N_TILES else None))
    return out


def _toeplitz_pair(left_row, right_row):
    lane = lax.broadcasted_iota(jnp.int32, (GRID_W, _PAIR), 1)
    shape = (GRID_W, _PAIR)
    left = pltpu.roll(jnp.broadcast_to(left_row, shape), _PAIR - (WIN_COLS - 1), 1, stride=1, stride_axis=0)
    right = pltpu.roll(jnp.broadcast_to(right_row, shape), GRID_W - (WIN_COLS - 1), 1, stride=1, stride_axis=0)
    return jnp.where(lane < GRID_W, left, right)


def _build_tiles(tiles_ref, rp_ref):
    for h in range(2):
        for t in range(_N_TILES):
            tiles_ref[h, t] = _toeplitz_pair(rp_ref[h, t:t + 1, :], rp_ref[h, t + 1:t + 2, :])


def _block_class(b, nblk, fn, entering=False):
    interior = (b == 1) if entering else jnp.logical_and(b > 0, b < nblk - 1)
    for cls, cond in enumerate((b == 0, interior, b == nblk - 1)):
        pl.when(cond)(functools.partial(fn, cls))


def _attn_geometry(p4, nx):
    rows = p4.shape[0]
    w = p4.shape[1] // 4
    nhp = w // _PAIR
    nblk = nx // _QB
    qspec = lambda col: pl.BlockSpec((_QB, _PAIR), lambda hp, b: (b, col * nhp + hp))
    kspec = lambda col: pl.BlockSpec((rows, _PAIR), lambda hp, b: (0, col * nhp + hp))
    tspec = pl.BlockSpec((2, _RP_ROWS, _PAIR), lambda hp, b: (hp, 0, 0))
    mspec = pl.BlockSpec((None, _QB, _KW), lambda hp, b: (jnp.where(b == 0, 0, jnp.where(b == nblk - 1, 2, 1)), 0, 0))
    lspec = pl.BlockSpec((None, _QB, 2), lambda hp, b: (hp, b, 0))
    ospec = pl.BlockSpec((_QB, _PAIR), lambda hp, b: (b, hp))
    return rows, w, nhp, nblk, qspec, kspec, tspec, mspec, lspec, ospec


def _window_start(b, nx):
    return pl.multiple_of(jnp.clip(b * _QB - PAD_ROWS * GRID_W, 0, nx - _KW), _QB)


def _load_bias(bias_ref, tiles_ref, rp_ref, m_ref, b, nblk):
    pl.when(b == 0)(lambda: _build_tiles(tiles_ref, rp_ref))

    def fill(cls):
        for h in range(2):
            for qr, kr, tile in _bias_pieces(cls):
                rows = slice(qr * GRID_W, (qr + 1) * GRID_W)
                cols = slice(kr * GRID_W, (kr + 2) * GRID_W)
                m = m_ref[rows, cols]
                bias_ref[h, rows, cols] = m if tile is None else tiles_ref[h, tile] + m

    _block_class(b, nblk, fill, entering=True)


def _attn_fwd(p4, rp, mask, nx, name, comm=None):
    rows, w, nhp, nblk, qspec, kspec, tspec, mspec, lspec, ospec = _attn_geometry(p4, nx)
    n_ctx = rows - nx
    n_cin, n_cout = (len(comm.ins), len(comm.outs)) if comm else (0, 0)

    def body(*refs):
        q_ref, k_ref, v_ref, g_ref, rp_ref, m_ref = refs[:6]
        cin = refs[6:6 + n_cin]
        a_ref, o_ref, lse_ref = refs[6 + n_cin:9 + n_cin]
        cout = refs[9 + n_cin:9 + n_cin + n_cout]
        bias_ref, tiles_ref = refs[9 + n_cin + n_cout:11 + n_cin + n_cout]
        sems = refs[11 + n_cin + n_cout:]
        hp, b = pl.program_id(0), pl.program_id(1)
        if comm:
            pl.when(jnp.logical_and(hp == 0, b == 0))(lambda: comm.start(cin, cout, sems))
        start = _window_start(b, nx)
        _load_bias(bias_ref, tiles_ref, rp_ref, m_ref, b, nblk)
        qf = q_ref[...].astype(F32) * HEAD_DIM ** -0.5
        kw = k_ref[pl.ds(start, _KW), :].astype(BF16)
        vw = v_ref[pl.ds(start, _KW), :].astype(BF16)
        kcv = k_ref[pl.ds(nx, n_ctx), :].astype(BF16)
        vcv = v_ref[pl.ds(nx, n_ctx), :].astype(BF16)
        lane = lax.broadcasted_iota(jnp.int32, (1, _PAIR), 1)
        outs, lses = [], []
        for h in range(2):
            mine = (lane >= HEAD_DIM) if h else (lane < HEAD_DIM)
            qm = jnp.where(mine, qf, 0.0).astype(BF16)
            s_loc = lax.dot_general(qm, kw, _DIMS["nt"], preferred_element_type=F32) + bias_ref[h]
            s_ctx = lax.dot_general(qm, kcv, _DIMS["nt"], preferred_element_type=F32)
            mx = jnp.maximum(jnp.max(s_loc, axis=-1, keepdims=True), jnp.max(s_ctx, axis=-1, keepdims=True))
            p_loc = jnp.exp(s_loc - mx)
            p_ctx = jnp.exp(s_ctx - mx)
            den = jnp.sum(p_loc, axis=-1, keepdims=True) + jnp.sum(p_ctx, axis=-1, keepdims=True)
            o = jnp.dot(p_loc.astype(BF16), vw, preferred_element_type=F32)
            o = o + jnp.dot(p_ctx.astype(BF16), vcv, preferred_element_type=F32)
            outs.append(o * (1.0 / den))
            lses.append(mx + jnp.log(den))
        o = jnp.where(lane < HEAD_DIM, outs[0], outs[1])
        o_ref[...] = o.astype(ACT)
        a_ref[...] = (o * _silu(g_ref[...].astype(F32))).astype(BF16)
        col = lax.broadcasted_iota(jnp.int32, (1, 2), 1)
        lse_ref[...] = jnp.where(col == 0, lses[0], lses[1])
        if comm:
            pl.when(jnp.logical_and(hp == nhp - 1, b == nblk - 1))(lambda: comm.finish(cin, cout, sems))

    res = pl.pallas_call(
        body, grid=(nhp, nblk),
        in_specs=[qspec(0), kspec(1), kspec(2), qspec(3), tspec, mspec] + [HBM_SPEC] * n_cin,
        out_specs=[ospec, ospec, lspec] + [HBM_SPEC] * n_cout,
        out_shape=[_sds((nx, w), BF16), _sds((nx, w), ACT), _sds((nhp, nx, 2), F32)] + (comm.outs if comm else []),
        scratch_shapes=[pltpu.VMEM((2, _QB, _KW), F32), pltpu.VMEM((2, _N_TILES, GRID_W, _PAIR), F32)]
        + (comm.sems if comm else []),
        name=name, compiler_params=_cparams("arbitrary", "arbitrary"),
    )(p4, p4, p4, p4, rp, mask, *(comm.ins if comm else []))
    return res[:3], res[3:]


def _fold_tiles(dtiles_ref, drp_ref):
    shape = (GRID_W, _PAIR)
    lane = lax.broadcasted_iota(jnp.int32, shape, 1)
    flip = (lax.broadcasted_iota(jnp.int32, (_PAIR, _PAIR), 0)
            + lax.broadcasted_iota(jnp.int32, (_PAIR, _PAIR), 1) == _PAIR - 1).astype(F32)
    drp_ref[...] = jnp.zeros(drp_ref.shape, F32)
    for h in range(2):
        stack = dtiles_ref[h].reshape(_N_TILES * GRID_W, _PAIR)
        rev = jnp.dot(stack, flip, precision=lax.Precision.HIGHEST, preferred_element_type=F32)
        for t in range(_N_TILES):
            tile = rev[t * GRID_W:(t + 1) * GRID_W, :]
            for side in (0, 1):
                shift = _PAIR - GRID_W * side - (WIN_COLS - 1)
                half = jnp.where((lane < GRID_W) if side else (lane >= GRID_W), tile, 0.0)
                diag = pltpu.roll(half, shift, 1, stride=1, stride_axis=0)
                drp_ref[h, t + side:t + side + 1, :] += jnp.sum(diag, axis=0, keepdims=True)


def _attn_bwd(p4, rp, mask, o, lse, da, nx, name, comm=None):
    rows, w, nhp, nblk, qspec, kspec, tspec, mspec, lspec, ospec = _attn_geometry(p4, nx)
    n_ctx = rows - nx
    n_cin, n_cout = (len(comm.ins), len(comm.outs)) if comm else (0, 0)

    def body(*refs):
        q_ref, k_ref, v_ref, g_ref, rp_ref, m_ref, o_ref, lse_ref, da_ref = refs[:9]
        cin = refs[9:9 + n_cin]
        d4_ref, drp_ref = refs[9 + n_cin:11 + n_cin]
        cout = refs[11 + n_cin:11 + n_cin + n_cout]
        bias_ref, tiles_ref, ds_ref, dtiles_ref, dk_ref, dv_ref = refs[11 + n_cin + n_cout:17 + n_cin + n_cout]
        sems = refs[17 + n_cin + n_cout:]
        hp, b = pl.program_id(0), pl.program_id(1)
        if comm:
            pl.when(jnp.logical_and(hp == 0, b == 0))(lambda: comm.start(cin, cout, sems))
        start = _window_start(b, nx)
        here = pl.multiple_of(b * _QB, _QB)

        @pl.when(b == 0)
        def _():
            dk_ref[...] = jnp.zeros(dk_ref.shape, F32)
            dv_ref[...] = jnp.zeros(dv_ref.shape, F32)
            dtiles_ref[...] = jnp.zeros(dtiles_ref.shape, F32)
            d4_ref[0, pl.ds(nx, n_ctx), :] = jnp.zeros((n_ctx, _PAIR), BF16)
            d4_ref[3, pl.ds(nx, n_ctx), :] = jnp.zeros((n_ctx, _PAIR), BF16)

        _load_bias(bias_ref, tiles_ref, rp_ref, m_ref, b, nblk)
        gv = g_ref[...].astype(F32)
        dav = da_ref[...].astype(F32)
        ov = o_ref[...].astype(F32)
        dov = dav * _silu(gv)
        d4_ref[3, pl.ds(here, _QB), :] = (dav * ov * _dsilu(gv)).astype(BF16)
        qf = q_ref[...].astype(F32) * HEAD_DIM ** -0.5
        kw = k_ref[pl.ds(start, _KW), :].astype(BF16)
        vw = v_ref[pl.ds(start, _KW), :].astype(BF16)
        kcv = k_ref[pl.ds(nx, n_ctx), :].astype(BF16)
        vcv = v_ref[pl.ds(nx, n_ctx), :].astype(BF16)
        lane = lax.broadcasted_iota(jnp.int32, (1, _PAIR), 1)
        dq = jnp.zeros((_QB, _PAIR), F32)
        for h in range(2):
            mine = (lane >= HEAD_DIM) if h else (lane < HEAD_DIM)
            qm = jnp.where(mine, qf, 0.0).astype(BF16)
            dom = jnp.where(mine, dov, 0.0)
            dob = dom.astype(BF16)
            lse = lse_ref[:, h:h + 1]
            s_loc = lax.dot_general(qm, kw, _DIMS["nt"], preferred_element_type=F32)
            p_loc = jnp.exp(s_loc + bias_ref[h] - lse)
            p_ctx = jnp.exp(lax.dot_general(qm, kcv, _DIMS["nt"], preferred_element_type=F32) - lse)
            delta = jnp.sum(dom * ov, axis=-1, keepdims=True)
            ds_loc = p_loc * (lax.dot_general(dob, vw, _DIMS["nt"], preferred_element_type=F32) - delta)
            ds_ctx = p_ctx * (lax.dot_general(dob, vcv, _DIMS["nt"], preferred_element_type=F32) - delta)
            dsb_loc = ds_loc.astype(BF16)
            dsb_ctx = ds_ctx.astype(BF16)
            dq_h = (jnp.dot(dsb_loc, kw, preferred_element_type=F32)
                    + jnp.dot(dsb_ctx, kcv, preferred_element_type=F32))
            dq = dq + jnp.where(mine, dq_h, 0.0)
            dk_ref[pl.ds(start, _KW), :] += lax.dot_general(dsb_loc, qm, _DIMS["tn"], preferred_element_type=F32)
            dv_ref[pl.ds(start, _KW), :] += lax.dot_general(p_loc.astype(BF16), dob, _DIMS["tn"],
                                                            preferred_element_type=F32)
            dk_ref[pl.ds(nx, n_ctx), :] += lax.dot_general(dsb_ctx, qm, _DIMS["tn"], preferred_element_type=F32)
            dv_ref[pl.ds(nx, n_ctx), :] += lax.dot_general(p_ctx.astype(BF16), dob, _DIMS["tn"],
                                                           preferred_element_type=F32)
            ds_ref[h] = ds_loc
        d4_ref[0, pl.ds(here, _QB), :] = (dq * HEAD_DIM ** -0.5).astype(BF16)

        def scatter(cls):
            for h in range(2):
                for qr, kr, tile in _bias_pieces(cls):
                    if tile is not None:
                        dtiles_ref[h, tile] += ds_ref[h, qr * GRID_W:(qr + 1) * GRID_W, kr * GRID_W:(kr + 2) * GRID_W]

        _block_class(b, nblk, scatter)

        @pl.when(b == nblk - 1)
        def _():
            d4_ref[1] = dk_ref[...].astype(BF16)
            d4_ref[2] = dv_ref[...].astype(BF16)
            _fold_tiles(dtiles_ref, drp_ref)

        if comm:
            pl.when(jnp.logical_and(hp == nhp - 1, b == nblk - 1))(lambda: comm.finish(cin, cout, sems))

    tiles = pltpu.VMEM((2, _N_TILES, GRID_W, _PAIR), F32)
    block = pltpu.VMEM((2, _QB, _KW), F32)
    res = pl.pallas_call(
        body, grid=(nhp, nblk),
        in_specs=[qspec(0), kspec(1), kspec(2), qspec(3), tspec, mspec, ospec, lspec, ospec] + [HBM_SPEC] * n_cin,
        out_specs=[pl.BlockSpec((4, rows, _PAIR), lambda hp, b: (0, 0, hp)), tspec] + [HBM_SPEC] * n_cout,
        out_shape=[_sds((4, rows, w), BF16), _sds(rp.shape, F32)] + (comm.outs if comm else []),
        scratch_shapes=[block, tiles, block, tiles, pltpu.VMEM((rows, _PAIR), F32), pltpu.VMEM((rows, _PAIR), F32)]
        + (comm.sems if comm else []),
        name=name, compiler_params=_cparams("arbitrary", "arbitrary"),
    )(p4, p4, p4, p4, rp, mask, o, lse, da, *(comm.ins if comm else []))
    return res[:2], res[2:]


def _final(x, g, target, name):
    rows, d = x.shape
    tr = ROW_BLOCK
    nblk = rows // tr

    def body(x_ref, g_ref, t_ref, loss_ref, dx_ref, dg_ref, acc_ref):
        i = pl.program_id(0)
        xv = x_ref[...]
        gv = g_ref[...]
        r = lax.rsqrt(jnp.mean(xv * xv, axis=-1, keepdims=True) + EPS)
        xn = xv * r
        err = xn * gv - t_ref[...]
        dy = err * (1.0 / d)
        dxn = dy * gv
        dx_ref[...] = r * (dxn - xn * jnp.mean(dxn * xn, axis=-1, keepdims=True))
        s_g = jnp.sum(dy * xn, axis=0, keepdims=True)
        s_l = jnp.sum(jnp.mean(err * err, axis=-1, keepdims=True), axis=0, keepdims=True)

        @pl.when(i == 0)
        def _():
            dg_ref[...] = s_g
            acc_ref[...] = s_l

        @pl.when(i > 0)
        def _():
            dg_ref[...] += s_g
            acc_ref[...] += s_l

        @pl.when(i == nblk - 1)
        def _():
            loss_ref[...] = jnp.broadcast_to(0.5 * acc_ref[...], loss_ref.shape)

    row = pl.BlockSpec((tr, d), lambda i: (i, 0))
    vec = pl.BlockSpec((1, d), lambda i: (0, 0))
    return pl.pallas_call(
        body, grid=(nblk,), in_specs=[row, vec, row],
        out_specs=[pl.BlockSpec((1, 128), lambda i: (0, 0)), row, vec],
        out_shape=[_sds((1, 128), F32), _sds((rows, d), F32), _sds((1, d), F32)],
        scratch_shapes=[pltpu.VMEM((1, 1), F32)], name=name, compiler_params=_cparams("arbitrary"),
    )(x, g, target)


def _as2d(a):
    if a.ndim == 1:
        return a.reshape(-1, 128) if a.shape[0] % 128 == 0 else a.reshape(1, -1)
    return a.reshape(-1, a.shape[-1])


def _adamw(w, g, m, v, name, comm=None):
    shape = w.shape
    w2, g2, m2, v2 = (_as2d(t) for t in (w, g.reshape(shape), m, v))
    rows, cols = w2.shape
    tr = 512 if rows % 512 == 0 else rows
    c1 = 1.0 - ADAM_B1 ** ADAM_STEP
    c2 = 1.0 - ADAM_B2 ** ADAM_STEP

    def body(w_ref, g_ref, m_ref, v_ref, d_ref, nm_ref, nv_ref):
        gv = g_ref[...]
        nm = ADAM_B1 * m_ref[...] + (1.0 - ADAM_B1) * gv
        nv = ADAM_B2 * v_ref[...] + (1.0 - ADAM_B2) * (gv * gv)
        nm_ref[...] = nm
        nv_ref[...] = nv
        d_ref[...] = -ADAM_LR * ((nm / c1) / (jnp.sqrt(nv / c2) + ADAM_EPS) + ADAM_WD * w_ref[...])

    blk = pl.BlockSpec((tr, cols), lambda i: (i, 0))
    outs, carried = _call(body, (w2, g2, m2, v2), grid=(rows // tr,), in_specs=[blk] * 4, out_specs=[blk] * 3,
                          out_shape=[_sds((rows, cols), F32)] * 3, name=name, comm=comm)
    outs = tuple(t.reshape(shape) for t in outs)
    return outs if comm is None else (outs, carried)


def _sum_lead(x, name, out_dtype=F32):
    n, rows, cols = x.shape
    tr = 512 if rows % 512 == 0 else rows

    def body(x_ref, o_ref):
        acc = x_ref[0].astype(F32)
        for k in range(1, n):
            acc = acc + x_ref[k].astype(F32)
        o_ref[...] = acc.astype(out_dtype)

    return pl.pallas_call(
        body, grid=(rows // tr,), in_specs=[pl.BlockSpec((n, tr, cols), lambda i: (0, i, 0))],
        out_specs=pl.BlockSpec((tr, cols), lambda i: (i, 0)), out_shape=_sds((rows, cols), out_dtype),
        name=name, compiler_params=_cparams("parallel"),
    )(x)


_NO_CTX = 1 << 30


def _seg_vecs(mod_l, which, nseg):
    return mod_l[:nseg, which][:, None, :]


def _norm_grads(dshift, dgeff, dgate, g, scale):
    nseg, _, d = dshift.shape
    dmod = jnp.stack([dshift[:, 0], dgeff[:, 0] * g, dgate[:, 0]], axis=1)
    if nseg == 1:
        dmod = jnp.concatenate([dmod, jnp.zeros((1, 3, d), F32)], axis=0)
    dg = jnp.sum(dgeff[:, 0] * (1.0 + scale[:, 0]), axis=0)
    return dmod, dg


def _pool_layer(xin, g, mod_l, w_in, w_grp, w_out, pscale, nx, tag, comms=None):
    rows = xin.shape[0]
    nseg = 2 if rows > nx else 1
    comms = comms or {}
    shift, scale, gate = (_seg_vecs(mod_l, k, nseg) for k in range(3))
    (h, r, uv), c_in = _norm_w_in(xin, g, scale, shift, w_in, nx, f"w_in_fwd_{tag}", comms.get("w_in_fwd"))
    (z, mixed, a), c_pool = _pool_grp_fwd(uv, w_grp, pscale, nx, f"pool_fwd_{tag}", comms.get("pool_fwd"))
    yx, xout = _w_out_resid(a, w_out, xin, gate, nx, f"w_out_fwd_{tag}")

    def backward(dxo, comms=None, token=None):
        comms = comms or {}
        gate_b = gate if token is None else gate + token[0, 0]
        (dyx, da, dgate), c_out = _gate_w_out_bwd(dxo, yx, gate_b, w_out, nx, f"w_out_bwd_{tag}",
                                                  comms.get("w_out_bwd"))
        gw_out = _mm_tn(a, dyx, f"w_out_grad_{tag}", BF16)
        dm, duv, dscale = _pool_grp_bwd(da, mixed, uv, pscale, w_grp, nx, f"pool_bwd_{tag}")
        gw_grp = _grp_wgrad(z, dm, w_grp.shape[0], f"grp_grad_{tag}", BF16)
        gw_in = _mm_tn_parts(h, duv, f"w_in_grad_{tag}", BF16)
        (dx, dshift, dgeff), c_bwd = _w_in_bwd_norm(duv, w_in, xin, r, g, scale, dxo, nx, f"w_in_bwd_{tag}",
                                                    comms.get("w_in_bwd"))
        dmod, dg = _norm_grads(dshift, dgeff, dgate, g[0], scale)
        return (dx, dmod, dg, dict(w_in=gw_in, w_grp=gw_grp, w_out=gw_out, scale=dscale),
                dict(w_out_bwd=c_out, w_in_bwd=c_bwd))

    return xout, backward, dict(w_in_fwd=c_in, pool_fwd=c_pool)


def _na_layer(xc, g, mod_l, w_in, rpb, w_out, nx, mask, comm=None):
    nh, n_dr, n_dc = rpb.shape
    shift, scale = _seg_vecs(mod_l, 0, 2), _seg_vecs(mod_l, 1, 2)
    gate = _seg_vecs(mod_l, 2, 1)
    (h, r, p4), _ = _norm_w_in(xc, g, scale, shift, w_in, nx, "w_in_fwd_na")
    rp = jnp.pad(rpb, ((0, 0), (1, _RP_ROWS - 1 - n_dr), (0, _PAIR - n_dc)))
    (a, o, lse), carried = _attn_fwd(p4, rp, mask, nx, "attn_fwd", comm)
    yx, xout = _w_out_resid(a, w_out, xc, gate, nx, "w_out_fwd_na")

    def backward(dxo, comm=None):
        (dyx, da, dgate), _ = _gate_w_out_bwd(dxo, yx, gate, w_out, nx, "w_out_bwd_na")
        gw_out = _mm_tn(a, dyx, "w_out_grad_na", BF16)
        (d4, drp), carried_bwd = _attn_bwd(p4, rp, mask, o, lse, da, nx, "attn_bwd", comm)
        gw_in = _mm_tn_parts(h, d4, "w_in_grad_na", BF16)
        (dx, dshift, dgeff), _ = _w_in_bwd_norm(d4, w_in, xc, r, g, scale, dxo, nx, "w_in_bwd_na")
        dgate2 = jnp.concatenate([dgate, jnp.zeros_like(dgate)], axis=0)
        dmod, dg = _norm_grads(dshift, dgeff, dgate2, g[0], scale)
        drpb = drp[:, 1:1 + n_dr, ::-1][:, :, :n_dc]
        return dx, dmod, dg, dict(w_in=gw_in, w_out=gw_out, rpb=drpb), carried_bwd

    return xout, backward, carried


def _conv_layer(xin, g, mod_l, w_in, dw, db, w_out):
    shift, scale, gate = (_seg_vecs(mod_l, k, 1) for k in range(3))
    nx = xin.shape[0]
    (h, r, p4), _ = _norm_w_in(xin, g, scale, shift, w_in, nx, "w_in_fwd_conv")
    a = _conv_fwd(p4, dw, db, "conv_fwd")
    yx, xout = _w_out_resid(a, w_out, xin, gate, nx, "w_out_fwd_conv")

    def backward(dxo):
        (dyx, da, dgate), _ = _gate_w_out_bwd(dxo, yx, gate, w_out, nx, "w_out_bwd_conv")
        gw_out = _mm_tn(a, dyx, "w_out_grad_conv", BF16)
        d4, ddw, ddb = _conv_bwd(da, p4, dw, db, "conv_bwd")
        gw_in = _mm_tn_parts(h, d4, "w_in_grad_conv", BF16)
        (dx, dshift, dgeff), _ = _w_in_bwd_norm(d4, w_in, xin, r, g, scale, dxo, nx, "w_in_bwd_conv")
        dmod, dg = _norm_grads(dshift, dgeff, dgate, g[0], scale)
        return dx, dmod, dg, dict(w_in=gw_in, w_out=gw_out, dw=ddw, db=ddb)

    return xout, backward


def _example_step(x, ctx, target, mod, norm_g, final_g, wts, na_weights=None, late_comm=None, late_weights=None,
                  grad_comm=None, na_grads_start=None):
    nx = x.shape[0]
    consts = _attn_mask()
    g_rows = [norm_g[i:i + 1] for i in range(4)]
    xc0 = jnp.concatenate([x, ctx], axis=0)
    xc1, bwd0, _ = _pool_layer(xc0, g_rows[0], mod[0], wts["pool_w_in"][0], wts["pool_w_grp"][0],
                               wts["pool_w_out"][0], wts["pool_scale"][0:1], nx, "p0")
    if na_weights is not None:
        wts = {**wts, **na_weights(xc1)}
    x2, bwd1, carried = _na_layer(xc1, g_rows[1], mod[1], wts["na_w_in"], wts["na_rpb"], wts["na_w_out"], nx, consts,
                                  late_comm)
    if late_weights is not None:
        wts = {**wts, **late_weights(carried)}
    x3, bwd2 = _conv_layer(x2, g_rows[2], mod[2], wts["conv_w_in"], wts["conv_dw"], wts["conv_db"], wts["conv_w_out"])
    x4, bwd3, _ = _pool_layer(x3, g_rows[3], mod[3], wts["pool_w_in"][1], wts["pool_w_grp"][1], wts["pool_w_out"][1],
                              wts["pool_scale"][1:2], nx, "p3")
    loss, dx4, dfinal_g = _final(x4, final_g, target, "loss_head")
    dx3, dmod3, dg3, gr3, _ = bwd3(dx4)
    dx2, dmod2, dg2, gr2 = bwd2(dx3)
    dxc1, dmod1, dg1, gr1, carried_bwd = bwd1(dx2, grad_comm(gr3, gr2) if grad_comm else None)
    dxc0, dmod0, dg0, gr0, _ = bwd0(dxc1, token=na_grads_start(gr1) if na_grads_start else None)
    return dict(
        loss=loss, grad_x=dxc0[:nx], dmod=jnp.stack([dmod0, dmod1, dmod2, dmod3]),
        dnorm_g=jnp.stack([dg0, dg1, dg2, dg3]), dfinal_g=dfinal_g, layers=(gr0, gr1, gr2, gr3), carried=carried_bwd)


_AXES = ("x", "y", "c")
_CHIP_FLIPS = ((1, 0), (0, 1), (1, 1))


def _position():
    return tuple(lax.axis_index(a) for a in _AXES)


def _flipped(pos, flip):
    return tuple(1 - p if f else p for p, f in zip(pos, flip))


def _join_comms(comms):
    n_in = [len(c.ins) for c in comms]
    n_out = [len(c.outs) for c in comms]
    n_sem = [len(c.sems) for c in comms]

    def parts(ins, outs, sems):
        for k in range(len(comms)):
            a, b, s = sum(n_in[:k]), sum(n_out[:k]), sum(n_sem[:k])
            yield comms[k], (ins[a:a + n_in[k]], outs[b:b + n_out[k]], sems[s:s + n_sem[k]])

    def start(ins, outs, sems):
        for c, part in parts(ins, outs, sems):
            c.start(*part)

    def finish(ins, outs, sems):
        for c, part in parts(ins, outs, sems):
            c.finish(*part)

    joint = _Comm([a for c in comms for a in c.ins], [o for c in comms for o in c.outs],
                  [s for c in comms for s in c.sems], start, finish)
    return joint, lambda res: [list(res[sum(n_out[:k]):sum(n_out[:k + 1])]) for k in range(len(comms))]


def _run_comms(comms, name):
    joint, split = _join_comms(comms)

    def body(*refs):
        n_in, n_out = len(joint.ins), len(joint.outs)
        joint.start(refs[:n_in], refs[n_in:n_in + n_out], refs[n_in + n_out:])
        joint.finish(refs[:n_in], refs[n_in:n_in + n_out], refs[n_in + n_out:])

    res = pl.pallas_call(
        body, in_specs=[HBM_SPEC] * len(joint.ins), out_specs=[HBM_SPEC] * len(joint.outs), out_shape=joint.outs,
        scratch_shapes=joint.sems, name=name,
    )(*joint.ins)
    return split(res)


def _all_gather_comm(v, axes):
    flips = [f for f in np.ndindex(2, 2, 2) if any(f) and all(a in axes or not b for a, b in zip(_AXES, f))]
    n = len(flips) + 1

    def copies(ins, outs, sems):
        (v_ref,), (o_ref,), (send_sems, recv_sems, local_sem) = ins, outs, sems
        pos = _position()
        slot = 0
        for a, p in zip(_AXES, pos):
            if a in axes:
                slot = 2 * slot + p
        local = pltpu.make_async_copy(v_ref, o_ref.at[slot], local_sem)
        remote = [pltpu.make_async_remote_copy(v_ref, o_ref.at[slot], send_sems.at[k], recv_sems.at[k],
                                               device_id=_flipped(pos, flip), device_id_type=MESH)
                  for k, flip in enumerate(flips)]
        return [local] + remote

    def start(ins, outs, sems):
        for cp in copies(ins, outs, sems):
            cp.start()

    def finish(ins, outs, sems):
        for cp in copies(ins, outs, sems):
            cp.wait()

    sems = [pltpu.SemaphoreType.DMA((n - 1,)), pltpu.SemaphoreType.DMA((n - 1,)), pltpu.SemaphoreType.DMA(())]
    return _Comm([v], [_sds((n,) + v.shape, v.dtype)], sems, start, finish)


def _all_gather(v, axes, name):
    return _run_comms([_all_gather_comm(v, axes)], name)[0][0]


class _Item:
    def __init__(self, key, layer, shape, shard_axis, half_axis):
        self.key, self.layer, self.shape = key, layer, tuple(shape)
        self.shard_axis, self.half_axis = shard_axis, half_axis
        self.shard = shape[shard_axis] // 4
        self.half = shape[half_axis] // 2

    def sized(self, shard=False, half=False):
        s = list(self.shape)
        if shard:
            s[self.shard_axis] = self.shard
        if half:
            s[self.half_axis] = self.half
        return tuple(s)

    def window(self, ref, chip=None, half=None):
        idx = [slice(None)] * len(self.shape)
        if chip is not None:
            idx[self.shard_axis] = pl.ds(chip * self.shard, self.shard)
        if half is not None:
            idx[self.half_axis] = pl.ds(half * self.half, self.half)
        return ref.at[tuple(idx)]


def _items(d, w):
    out = []
    for j in range(2):
        out += [_Item("pool_w_in", j, (d, 2 * w), 1, 0), _Item("pool_w_grp", j, (4, w // 4, w // 4), 1, 0),
                _Item("pool_w_out", j, (w, d), 0, 1)]
    out += [_Item("na_w_in", 0, (d, 4 * w), 1, 0), _Item("na_w_out", 0, (w, d), 0, 1),
            _Item("conv_w_in", 0, (d, 4 * w), 1, 0), _Item("conv_w_out", 0, (w, d), 0, 1)]
    return out


def _gather_weights(shards, items, name):
    comm = _gather_comm(shards, items)

    def body(*refs):
        n = len(items)
        comm.start(refs[:n], refs[n:2 * n], refs[2 * n:])
        comm.finish(refs[:n], refs[n:2 * n], refs[2 * n:])

    return pl.pallas_call(
        body, in_specs=[HBM_SPEC] * len(items), out_specs=[HBM_SPEC] * len(items), out_shape=comm.outs,
        scratch_shapes=comm.sems, name=name,
    )(*shards)


def _gather_comm(shards, items):
    n = len(items)

    def copies(src, dst, sems, onward):
        send_a, recv_a, send_b, recv_b, send_c, recv_c = sems
        x, y, c = _position()
        chip = 2 * x + y
        sibling = (x, y, 1 - c)
        own, out, fwd, fwd_in = [], [], [], []
        for i, it in enumerate(items):
            own.append(pltpu.make_async_remote_copy(src[i], it.window(dst[i], chip=chip), send_c.at[i], recv_c.at[i],
                                                    device_id=sibling, device_id_type=MESH))
            for k, flip in enumerate(_CHIP_FLIPS):
                px, py = _flipped((x, y), flip)
                s = 3 * i + k
                out.append(pltpu.make_async_remote_copy(
                    it.window(src[i], half=c), it.window(dst[i], chip=chip, half=c), send_a.at[s], recv_a.at[s],
                    device_id=(px, py, c), device_id_type=MESH))
                if onward:
                    got = it.window(dst[i], chip=2 * px + py, half=c)
                    fwd.append(pltpu.make_async_remote_copy(got, got, send_b.at[s], recv_b.at[s],
                                                            device_id=sibling, device_id_type=MESH))
                    other = it.window(dst[i], chip=2 * px + py, half=1 - c)
                    fwd_in.append(pltpu.make_async_remote_copy(other, other, send_b.at[s], recv_b.at[s],
                                                               device_id=sibling, device_id_type=MESH))
        return own, out, fwd, fwd_in

    def start(src, dst, sems):
        own, out, _, _ = copies(src, dst, sems, False)
        for cp in own + out:
            cp.start()

    def finish(src, dst, sems):
        own, out, fwd, fwd_in = copies(src, dst, sems, True)
        for arrived, onward in zip(out, fwd):
            arrived.wait_recv()
            onward.start()
        for cp in fwd_in:
            cp.wait_recv()
        for cp in out + fwd:
            cp.wait_send()
        for cp in own:
            cp.wait()

    sems = [pltpu.SemaphoreType.DMA((3 * n,)) for _ in range(4)] + [pltpu.SemaphoreType.DMA((n,)) for _ in range(2)]
    return _Comm(shards, [_sds(it.shape, BF16) for it in items], sems, start, finish)


def _pair_swap_comm(arrays, windows, out_shapes):
    n = len(arrays)

    def copies(src, got, sems):
        send_sems, recv_sems = sems
        x, y, c = _position()
        return [pltpu.make_async_remote_copy(windows[i](src[i], 1 - c), got[i], send_sems.at[i], recv_sems.at[i],
                                             device_id=(x, y, 1 - c), device_id_type=MESH) for i in range(n)]

    def start(src, got, sems):
        for cp in copies(src, got, sems):
            cp.start()

    def finish(src, got, sems):
        for cp in copies(src, got, sems):
            cp.wait()

    return _Comm(arrays, out_shapes, [pltpu.SemaphoreType.DMA((n,)), pltpu.SemaphoreType.DMA((n,))], start, finish)


def _pair_swap(arrays, windows, out_shapes, name):
    return _run_comms([_pair_swap_comm(arrays, windows, out_shapes)], name)[0]


def _chip_exchange(partials, items, name):
    comm = _chip_exchange_comm(partials, items)

    def body(*refs):
        n = len(items)
        comm.start(refs[:n], refs[n:2 * n], refs[2 * n:])
        comm.finish(refs[:n], refs[n:2 * n], refs[2 * n:])

    return pl.pallas_call(
        body, in_specs=[HBM_SPEC] * len(items), out_specs=[HBM_SPEC] * len(items), out_shape=comm.outs,
        scratch_shapes=comm.sems, name=name,
    )(*partials)


def _chip_exchange_copies(items):
    def copies(src, dst, sems):
        send_sems, recv_sems = sems
        x, y, c = _position()
        out = []
        for i, it in enumerate(items):
            for k, flip in enumerate(_CHIP_FLIPS):
                px, py = _flipped((x, y), flip)
                out.append(pltpu.make_async_remote_copy(
                    it.window(src[i], chip=2 * px + py), dst[i].at[k], send_sems.at[3 * i + k],
                    recv_sems.at[3 * i + k], device_id=(px, py, c), device_id_type=MESH))
        return out

    return copies


_SEM_SPEC = pl.BlockSpec(memory_space=pltpu.SEMAPHORE)
_DATAFLOW = pltpu.SideEffectType.DATAFLOW_SIDE_EFFECTING


def _split_start(copies, srcs, zones, n_copies, name):
    n, nz = len(srcs), len(zones)

    def body(*refs):
        src, land = refs[:n], refs[n:n + nz]
        send_sems, recv_sems = refs[n + nz:n + nz + 2]
        token = refs[-1]
        for cp in copies(src, land, (send_sems, recv_sems)):
            cp.start()
        token[...] = jnp.zeros(token.shape, F32)

    hbm = lambda t: pltpu.HBM(t.shape, t.dtype)
    res = pl.pallas_call(
        body, name=name,
        out_shape=(pltpu.SemaphoreType.DMA((n_copies,)), pltpu.SemaphoreType.DMA((n_copies,)),
                   *[hbm(t) for t in list(srcs) + list(zones)], _sds((8, 128), F32)),
        in_specs=[HBM_SPEC] * (n + nz),
        out_specs=(_SEM_SPEC, _SEM_SPEC, *[HBM_SPEC] * (n + nz), pl.BlockSpec(memory_space=pltpu.VMEM)),
        input_output_aliases={i: 2 + i for i in range(n + nz)},
        compiler_params=pltpu.CompilerParams(has_side_effects=_DATAFLOW),
    )(*[pltpu.with_memory_space_constraint(t, pltpu.HBM) for t in list(srcs) + list(zones)])
    return (res[0], res[1], list(res[2:2 + n]), list(res[2 + n:2 + n + nz])), res[-1]


def _split_wait(copies, handle, after, name):
    send_sems, recv_sems, srcs, zones = handle
    n, nz = len(srcs), len(zones)

    def body(*refs):
        src, land = refs[:n], refs[n:n + nz]
        send, recv = refs[n + nz:n + nz + 2]
        for cp in copies(src, land, (send, recv)):
            cp.wait_send()
            cp.wait_recv()

    hbm = lambda t: pltpu.HBM(t.shape, t.dtype)
    res = pl.pallas_call(
        body, name=name, out_shape=tuple(hbm(t) for t in list(srcs) + list(zones)),
        in_specs=[HBM_SPEC] * (n + nz) + [_SEM_SPEC, _SEM_SPEC, pl.BlockSpec(memory_space=pl.ANY)],
        out_specs=tuple([HBM_SPEC] * (n + nz)), input_output_aliases={i: i for i in range(n + nz)},
        compiler_params=pltpu.CompilerParams(has_side_effects=_DATAFLOW),
    )(*srcs, *zones, send_sems, recv_sems, after)
    return list(res[:n]), list(res[n:])


def _gather_ici_copies(items):
    def copies(src, dst, sems):
        send_sems, recv_sems = sems
        x, y, c = _position()
        chip = 2 * x + y
        out = []
        for i, it in enumerate(items):
            for k, flip in enumerate(_CHIP_FLIPS):
                px, py = _flipped((x, y), flip)
                out.append(pltpu.make_async_remote_copy(
                    it.window(src[i], half=c), it.window(dst[i], chip=chip, half=c), send_sems.at[3 * i + k],
                    recv_sems.at[3 * i + k], device_id=(px, py, c), device_id_type=MESH))
        return out

    return copies


def _gather_pair_finish(shards, mats, items, name):
    n = len(items)

    def body(*refs):
        src, dst = refs[:n], refs[2 * n:3 * n]
        send_own, recv_own, send_fwd, recv_fwd = refs[3 * n:]
        x, y, c = _position()
        chip = 2 * x + y
        sibling = (x, y, 1 - c)
        copies = []
        for i, it in enumerate(items):
            copies.append(pltpu.make_async_remote_copy(src[i], it.window(dst[i], chip=chip), send_own.at[i],
                                                       recv_own.at[i], device_id=sibling, device_id_type=MESH))
            for k, flip in enumerate(_CHIP_FLIPS):
                px, py = _flipped((x, y), flip)
                got = it.window(dst[i], chip=2 * px + py, half=c)
                copies.append(pltpu.make_async_remote_copy(got, got, send_fwd.at[3 * i + k], recv_fwd.at[3 * i + k],
                                                           device_id=sibling, device_id_type=MESH))
        for cp in copies:
            cp.start()
        for cp in copies:
            cp.wait()

    return pl.pallas_call(
        body, in_specs=[HBM_SPEC] * (2 * n), out_specs=[HBM_SPEC] * n, out_shape=[_sds(it.shape, BF16) for it in items],
        input_output_aliases={n + i: i for i in range(n)},
        scratch_shapes=[pltpu.SemaphoreType.DMA((n,)), pltpu.SemaphoreType.DMA((n,)),
                        pltpu.SemaphoreType.DMA((3 * n,)), pltpu.SemaphoreType.DMA((3 * n,))], name=name,
    )(*shards, *mats)


def _chip_exchange_comm(partials, items):
    n = len(items)
    copies = _chip_exchange_copies(items)

    def start(src, dst, sems):
        for cp in copies(src, dst, sems):
            cp.start()

    def finish(src, dst, sems):
        for cp in copies(src, dst, sems):
            cp.wait()

    return _Comm(partials, [_sds((3,) + it.sized(shard=True, half=True), BF16) for it in items],
                 [pltpu.SemaphoreType.DMA((3 * n,)), pltpu.SemaphoreType.DMA((3 * n,))], start, finish)


_SUM_STEPS = 2


def _pair_sums(gs, gots, its, pos, name):
    n = len(its)
    nb = _SUM_STEPS
    g2 = [g.reshape(-1, g.shape[-1]) for g in gs]
    got2 = [t.reshape(-1, t.shape[-1]) for t in gots]

    def body(pos_ref, *refs):
        for g_ref, got_ref, o_ref in zip(refs[:n], refs[n:2 * n], refs[2 * n:]):
            o_ref[...] = (g_ref[...].astype(F32) + got_ref[...].astype(F32)).astype(BF16)

    g_specs, got_specs = [], []
    for it, t in zip(its, got2):
        rows, cols = t.shape
        blk = (rows // nb, cols)
        g_map = (lambda i, pos: (pos[1] * nb + i, 0)) if it.half_axis == 0 else (lambda i, pos: (i, pos[1]))
        g_specs.append(pl.BlockSpec(blk, g_map))
        got_specs.append(pl.BlockSpec(blk, lambda i, pos: (i, 0)))
    outs = pl.pallas_call(
        body, grid_spec=pltpu.PrefetchScalarGridSpec(
            num_scalar_prefetch=1, grid=(nb,), in_specs=g_specs + got_specs, out_specs=got_specs),
        out_shape=[_sds(t.shape, BF16) for t in got2], name=name, compiler_params=_cparams("parallel"),
    )(pos, *g2, *got2)
    return [o.reshape(t.shape) for o, t in zip(outs, gots)]


_FLIP_SLOT = {2: 0, 1: 1, 3: 2}


def _chip_sums(pairs, slots, its, pos, name):
    n = len(its)
    nb = _SUM_STEPS

    def body(pos_ref, *refs):
        chip = pos_ref[0]
        for own in range(4):
            @pl.when(chip == own)
            def _():
                for p_ref, s_ref, o_ref in zip(refs[:n], refs[n:2 * n], refs[2 * n:]):
                    acc = None
                    for k in range(4):
                        v = (p_ref[...] if k == own else s_ref[_FLIP_SLOT[own ^ k]]).astype(F32)
                        acc = v if acc is None else acc + v
                    o_ref[...] = acc

    p_specs, s_specs, o_specs, shapes = [], [], [], []
    for it in its:
        shape = it.sized(shard=True, half=True)
        blk = (shape[0] // nb,) + shape[1:]
        rest = (0,) * (len(shape) - 1)

        def p_map(i, pos, it=it, nd=len(shape)):
            lead = i + (pos[0] * nb if it.shard_axis == 0 else 0)
            return (lead,) + tuple(pos[0] if ax == it.shard_axis else 0 for ax in range(1, nd))

        p_specs.append(pl.BlockSpec(blk, p_map))
        s_specs.append(pl.BlockSpec((3,) + blk, lambda i, pos, rest=rest: (0, i) + rest))
        o_specs.append(pl.BlockSpec(blk, lambda i, pos, rest=rest: (i,) + rest))
        shapes.append(_sds(shape, F32))
    return pl.pallas_call(
        body, grid_spec=pltpu.PrefetchScalarGridSpec(
            num_scalar_prefetch=1, grid=(nb,), in_specs=p_specs + s_specs, out_specs=o_specs),
        out_shape=shapes, name=name, compiler_params=_cparams("parallel"),
    )(pos, *pairs, *slots)


_GRAD_KEYS = ("pool_w_in", "pool_w_grp", "pool_w_out", "na_w_in", "na_w_out", "conv_w_in", "conv_w_out")


def _adamw_matrix(w, m, v, owns, others, it, pos, name):
    nl = w.shape[0]
    rows_split = it.half_axis == 0
    r, cdim = int(np.prod(w.shape[1:-1])), w.shape[-1]
    hr, hc = (r // 2, cdim) if rows_split else (r, cdim // 2)
    br = min(hr, 256)
    nb = hr // br
    c1 = 1.0 - ADAM_B1 ** ADAM_STEP
    c2 = 1.0 - ADAM_B2 ** ADAM_STEP

    def body(pos_ref, w_ref, m_ref, v_ref, *rest):
        own_refs, other_refs = rest[:nl], rest[nl:2 * nl]
        g_ref, d_ref, nm_ref, nv_ref = rest[2 * nl:]
        j, h = pl.program_id(0), pl.program_id(1)
        own, other = own_refs[0][...], other_refs[0][...]
        for q in range(1, nl):
            own = jnp.where(j == q, own_refs[q][...], own)
            other = jnp.where(j == q, other_refs[q][...], other)
        gv = jnp.where(h == pos_ref[1], own, other)
        nm = ADAM_B1 * m_ref[...] + (1.0 - ADAM_B1) * gv
        nv = ADAM_B2 * v_ref[...] + (1.0 - ADAM_B2) * (gv * gv)
        g_ref[...] = gv
        nm_ref[...] = nm
        nv_ref[...] = nv
        d_ref[...] = -ADAM_LR * ((nm / c1) / (jnp.sqrt(nv / c2) + ADAM_EPS) + ADAM_WD * w_ref[...])

    if rows_split:
        full = pl.BlockSpec((None, br, hc), lambda j, h, i, pos: (j, h * nb + i, 0))
    else:
        full = pl.BlockSpec((None, br, hc), lambda j, h, i, pos: (j, i, h))
    half = pl.BlockSpec((br, hc), lambda j, h, i, pos: (i, 0))
    flat = lambda t: t.reshape(nl, r, cdim)
    outs = pl.pallas_call(
        body, grid_spec=pltpu.PrefetchScalarGridSpec(
            num_scalar_prefetch=1, grid=(nl, 2, nb), in_specs=[full] * 3 + [half] * (2 * nl), out_specs=[full] * 4),
        out_shape=[_sds((nl, r, cdim), F32)] * 4, name=name,
        compiler_params=_cparams("parallel", "parallel", "parallel"),
    )(pos, flat(w), flat(m), flat(v), *[t.reshape(hr, hc) for t in list(owns) + list(others)])
    return tuple(t.reshape(w.shape) for t in outs)


_WEIGHTS = ("c_ctx", "norm_g", "ada_w", "ada_b", "pool_w_in", "pool_w_grp", "pool_scale", "pool_w_out", "na_w_in",
            "na_rpb", "na_w_out", "conv_w_in", "conv_dw", "conv_db", "conv_w_out", "final_g")
_COND_ROWS = 16


def _modulations(cond, ada_w, ada_b_cols):
    nl, d, n = ada_w.shape
    return _matmul(
        cond, ada_w, mode="nn", grid=(nl, 1), a_silu=True, epilogue="bias",
        a_spec=pl.BlockSpec((_COND_ROWS, d), lambda i, j: (0, 0)), b_spec=pl.BlockSpec((None, d, n), lambda i, j: (i, 0, 0)),
        extra=(ada_b_cols,), extra_specs=(pl.BlockSpec((None, 1, n), lambda i, j: (i, 0, 0)),),
        out_shapes=[_sds((nl, _COND_ROWS, n), F32)], out_specs=[pl.BlockSpec((None, _COND_ROWS, n), lambda i, j: (i, 0, 0))],
        name="modulations")[0]


def _ada_w_grad(cond, dm_cols):
    d = cond.shape[1]
    nl, _, n = dm_cols.shape
    return _matmul(
        cond, dm_cols, mode="tn", grid=(nl, 1), a_silu=True,
        a_spec=pl.BlockSpec((_COND_ROWS, d), lambda i, j: (0, 0)), b_spec=pl.BlockSpec((None, _COND_ROWS, n), lambda i, j: (i, 0, 0)),
        out_shapes=[_sds((nl, d, n), F32)], out_specs=[pl.BlockSpec((None, d, n), lambda i, j: (i, 0, 0))],
        name="ada_w_grad")[0]


def _cond_grad(dm_cols, ada_w):
    nl, d, n = ada_w.shape
    return _matmul(
        dm_cols, ada_w, mode="nt", grid=(1, nl), nk=nl, acc_shape=(_COND_ROWS, d),
        a_spec=pl.BlockSpec((None, _COND_ROWS, n), lambda i, q: (q, 0, 0)), b_spec=pl.BlockSpec((None, d, n), lambda i, q: (q, 0, 0)),
        out_shapes=[_sds((_COND_ROWS, d), F32)], out_specs=[pl.BlockSpec((_COND_ROWS, d), lambda i, q: (0, 0))],
        name="cond_grad")[0]


def _pack(parts):
    flat = [p.reshape(-1) for p in parts]
    sizes = [f.shape[0] for f in flat]
    total = sum(sizes)
    rows = -(-total // 1024) * 8
    packed = jnp.concatenate(flat + [jnp.zeros((rows * 128 - total,), F32)]).reshape(rows, 128)
    offs = np.concatenate([[0], np.cumsum(sizes)])[:-1]
    return packed, [(int(o), p.shape) for o, p in zip(offs, parts)]


def _unpack(flat, layout, k):
    off, shape = layout[k]
    return flat[..., off:off + int(np.prod(shape))].reshape(flat.shape[:-1] + tuple(shape))


def kernel(x, c, ctx, c_ctx, norm_g, ada_w, ada_b, pool_w_in, pool_w_grp, pool_scale, pool_w_out, na_w_in, na_rpb, na_w_out, conv_w_in, conv_dw, conv_db, conv_w_out, final_g, loss_target, m_c_ctx, m_norm_g, m_ada_w, m_ada_b, m_pool_w_in, m_pool_w_grp, m_pool_scale, m_pool_w_out, m_na_w_in, m_na_rpb, m_na_w_out, m_conv_w_in, m_conv_dw, m_conv_db, m_conv_w_out, m_final_g, v_c_ctx, v_norm_g, v_ada_w, v_ada_b, v_pool_w_in, v_pool_w_grp, v_pool_scale, v_pool_w_out, v_na_w_in, v_na_rpb, v_na_w_out, v_conv_w_in, v_conv_dw, v_conv_db, v_conv_w_out, v_final_g):
    params = dict(c_ctx=c_ctx, norm_g=norm_g, ada_w=ada_w, ada_b=ada_b, pool_w_in=pool_w_in, pool_w_grp=pool_w_grp,
                  pool_scale=pool_scale, pool_w_out=pool_w_out, na_w_in=na_w_in, na_rpb=na_rpb, na_w_out=na_w_out,
                  conv_w_in=conv_w_in, conv_dw=conv_dw, conv_db=conv_db, conv_w_out=conv_w_out, final_g=final_g)
    mom1 = dict(c_ctx=m_c_ctx, norm_g=m_norm_g, ada_w=m_ada_w, ada_b=m_ada_b, pool_w_in=m_pool_w_in,
                pool_w_grp=m_pool_w_grp, pool_scale=m_pool_scale, pool_w_out=m_pool_w_out, na_w_in=m_na_w_in,
                na_rpb=m_na_rpb, na_w_out=m_na_w_out, conv_w_in=m_conv_w_in, conv_dw=m_conv_dw, conv_db=m_conv_db,
                conv_w_out=m_conv_w_out, final_g=m_final_g)
    mom2 = dict(c_ctx=v_c_ctx, norm_g=v_norm_g, ada_w=v_ada_w, ada_b=v_ada_b, pool_w_in=v_pool_w_in,
                pool_w_grp=v_pool_w_grp, pool_scale=v_pool_scale, pool_w_out=v_pool_w_out, na_w_in=v_na_w_in,
                na_rpb=v_na_rpb, na_w_out=v_na_w_out, conv_w_in=v_conv_w_in, conv_dw=v_conv_dw, conv_db=v_conv_db,
                conv_w_out=v_conv_w_out, final_g=v_final_g)
    d = x.shape[-1]
    w = na_w_out.shape[1] * 4
    xi, yi, ci = _position()
    chip = 2 * xi + yi
    dev = 2 * chip + ci
    n_ada = ada_w.shape[-1]

    def chip_cols(a, size):
        return lax.dynamic_slice_in_dim(a, chip * size, size, axis=a.ndim - 1)

    items = _items(d, w)
    first = [it for it in items if it.key.startswith("pool") and it.layer == 0]
    na = [it for it in items if it.key.startswith("na")]
    late = [it for it in items if it not in first + na]
    shards_of = lambda its: [params[it.key][it.layer].astype(BF16) for it in its]
    empties = lambda its: [lax.empty(it.shape, BF16) for it in its]
    first_copies, na_copies = _gather_ici_copies(first), _gather_ici_copies(na)
    first_handle, token = _split_start(first_copies, shards_of(first), empties(first), 3 * len(first),
                                       "gather_first_start")
    behind = token[0, 0].astype(BF16)
    na_handle, token = _split_start(na_copies, [s + behind for s in shards_of(na)], empties(na), 3 * len(na),
                                    "gather_na_start")

    conds = _all_gather((c + token[0, 0]).reshape(8, d // 8), _AXES, "gather_cond").reshape(8, d)
    cond = jnp.concatenate([conds, c_ctx[None], jnp.zeros((_COND_ROWS - 9, d), F32)], axis=0)
    mod_cols = _modulations(cond, ada_w, chip_cols(ada_b, n_ada)[:, None, :])
    small_pack, small_layout = _pack([pool_scale, conv_dw, conv_db])
    (mod_all,), (small,) = _run_comms([_all_gather_comm(mod_cols, ("x", "y")),
                                       _all_gather_comm(small_pack, ("x", "y"))], "gather_mod")
    first_shards, first_mats = _split_wait(first_copies, first_handle, mod_all, "gather_first_wait")
    first_mats = _gather_pair_finish(first_shards, first_mats, first, "gather_first_pair")
    mod_all = mod_all.transpose(1, 2, 0, 3).reshape(4, _COND_ROWS, 3, d)
    mod = jnp.stack([lax.dynamic_index_in_dim(mod_all, dev, axis=1, keepdims=False), mod_all[:, 8]], axis=1)
    full = {(it.key, it.layer): mat for it, mat in zip(first, first_mats)}
    late_comm = _gather_comm(shards_of(late), late)

    def na_weights(after):
        na_shards, na_mats = _split_wait(na_copies, na_handle, after, "gather_na_wait")
        na_mats = _gather_pair_finish(na_shards, na_mats, na, "gather_na_pair")
        return {it.key: mat for it, mat in zip(na, na_mats)}

    def late_weights(mats):
        full.update({(it.key, it.layer): mat for it, mat in zip(late, mats)})
        return dict(pool_w_in=[full[("pool_w_in", j)] for j in range(2)],
                    pool_w_grp=[full[("pool_w_grp", j)] for j in range(2)],
                    pool_w_out=[full[("pool_w_out", j)] for j in range(2)],
                    conv_w_in=full[("conv_w_in", 0)], conv_w_out=full[("conv_w_out", 0)])

    small = small.reshape(4, -1)

    def whole(k):
        parts = _unpack(small, small_layout, k)
        return jnp.moveaxis(parts, 0, -2).reshape(parts.shape[1:-1] + (-1,))

    wts = dict(pool_w_in=[full[("pool_w_in", 0)]], pool_w_grp=[full[("pool_w_grp", 0)]],
               pool_w_out=[full[("pool_w_out", 0)]], pool_scale=whole(0), na_rpb=na_rpb[0], conv_dw=whole(1)[0],
               conv_db=whole(2))
    pos = jnp.stack([chip, ci]).astype(jnp.int32)

    def layer_grads(its, by_layer):
        pick = {"pool_w_in": "w_in", "pool_w_grp": "w_grp", "pool_w_out": "w_out", "na_w_in": "w_in",
                "na_w_out": "w_out", "conv_w_in": "w_in", "conv_w_out": "w_out"}
        return [by_layer[(it.key.split("_")[0], it.layer)][pick[it.key]] for it in its]

    def pair_sums(its, mats, tag):
        got = _pair_swap(mats, [(lambda ref, half, it=it: it.window(ref, half=half)) for it in its],
                         [_sds(it.sized(half=True), BF16) for it in its], f"pair_exchange_{tag}")
        return _pair_sums(mats, got, its, pos, f"pair_sum_{tag}")

    pairs = dict()

    def grad_comm(gr3, gr2):
        pairs["late"] = pair_sums(late, layer_grads(late, {("pool", 1): gr3, ("conv", 0): gr2}), "late")
        return _chip_exchange_comm(pairs["late"], late)

    slot_zones = lambda its: [lax.empty((3,) + it.sized(shard=True, half=True), BF16) for it in its]
    na_xcopies, first_xcopies = _chip_exchange_copies(na), _chip_exchange_copies(first)
    handles = dict()

    def na_grads_start(gr1):
        pairs["na"] = pair_sums(na, layer_grads(na, {("na", 0): gr1}), "na")
        handles["na"], started = _split_start(na_xcopies, pairs["na"], slot_zones(na), 3 * len(na),
                                              "exchange_na_start")
        return started

    res = _example_step(x[0], ctx[0], loss_target[0], mod, norm_g, final_g[None], wts, na_weights, late_comm,
                        late_weights, grad_comm, na_grads_start)
    g0, g1, g2, g3 = res["layers"]
    pairs["na"], na_slots = _split_wait(na_xcopies, handles["na"], g0["w_in"], "exchange_na_wait")
    first_grads = layer_grads(first, {("pool", 0): g0})
    packed, layout = _pack([res["dfinal_g"], res["dnorm_g"], res["dmod"], g1["rpb"],
                            jnp.concatenate([g0["scale"], g3["scale"]], axis=0), g2["dw"], g2["db"],
                            res["loss"][0, :1]])
    first_got, (every,) = _run_comms(
        [_pair_swap_comm(first_grads, [(lambda ref, half, it=it: it.window(ref, half=half)) for it in first],
                         [_sds(it.sized(half=True), BF16) for it in first]), _all_gather_comm(packed, _AXES)],
        "pair_exchange_first")
    pairs["first"] = _pair_sums(first_grads, first_got, first, pos, "pair_sum_first")
    handles["first"], token = _split_start(first_xcopies, pairs["first"], slot_zones(first), 3 * len(first),
                                           "exchange_first_start")

    grads = dict()
    total = _sum_lead(every + token[0, 0], "sum_vec_grads").reshape(-1)
    every = every.reshape(8, -1)
    grads["final_g"] = _unpack(total, layout, 0).reshape(final_g.shape)
    grads["norm_g"] = _unpack(total, layout, 1)
    grads["na_rpb"] = _unpack(total, layout, 3)[None]
    grads["pool_scale"] = chip_cols(_unpack(total, layout, 4), pool_scale.shape[-1])
    grads["conv_dw"] = chip_cols(_unpack(total, layout, 5), conv_dw.shape[-1])[None]
    grads["conv_db"] = chip_cols(_unpack(total, layout, 6), conv_db.shape[-1])
    dmod_sum = _unpack(total, layout, 2).reshape(4, 2, 3 * d)
    dmod_each = _unpack(every, layout, 2).reshape(8, 4, 2, 3 * d)
    grads["ada_b"] = dmod_sum[:, 0] + dmod_sum[:, 1]
    dm = jnp.concatenate([dmod_each[:, :, 0].transpose(1, 0, 2), dmod_sum[:, 1][:, None],
                          jnp.zeros((4, _COND_ROWS - 9, 3 * d), F32)], axis=1)
    dm_cols = chip_cols(dm, n_ada)
    grads["ada_w"] = _ada_w_grad(cond, dm_cols)
    dcond = _cond_grad(dm_cols, ada_w)[8].reshape(8, d // 8)
    dcond_all = _all_gather(dcond, ("x", "y"), "gather_cond_grad")
    grads["c_ctx"] = _sum_lead(dcond_all, "sum_cond_grad").reshape(d) * _dsilu(c_ctx)
    vector_out = {k: _adamw(params[k], grads[k], mom1[k], mom2[k], f"adamw_{k}")
                  for k in _WEIGHTS if k not in _GRAD_KEYS}
    pairs["first"], first_slots = _split_wait(first_xcopies, handles["first"], vector_out["ada_w"][2],
                                              "exchange_first_wait")

    slots = dict(zip(late, res["carried"]))
    slots.update(zip(first, first_slots))
    slots.update(zip(na, na_slots))
    pair_of = dict(zip(late, pairs["late"]))
    pair_of.update(zip(first, pairs["first"]))
    pair_of.update(zip(na, pairs["na"]))
    reduced = _chip_sums([pair_of[it] for it in items], [slots[it] for it in items], items, pos, "chip_sum")
    theirs = _pair_swap(reduced, [lambda ref, half: ref] * len(items),
                        [_sds(t.shape, F32) for t in reduced], "pair_return")
    matrix_out = dict()
    for k in _GRAD_KEYS:
        idx = [i for i, it in enumerate(items) if it.key == k]
        res_k = _adamw_matrix(params[k], mom1[k], mom2[k], [reduced[i] for i in idx], [theirs[i] for i in idx],
                              items[idx[0]], pos, f"adamw_{k}")
        grads[k], matrix_out[k] = res_k[0], res_k[1:]

    outs = [[], [], []]
    for k in _WEIGHTS:
        step = matrix_out[k] if k in matrix_out else vector_out[k]
        for lst, val in zip(outs, step):
            lst.append(val)
    loss = _unpack(total, layout, 7)[0]
    return (loss, res["grad_x"][None], *[grads[k].reshape(params[k].shape) for k in _WEIGHTS],
            *outs[0], *outs[1], *outs[2])
```

```python
import functools

import numpy as np
import jax
import jax.numpy as jnp
from jax import lax
from jax.experimental import pallas as pl
from jax.experimental.pallas import tpu as pltpu

F32 = jnp.float32
BF16 = jnp.bfloat16

EPS = 1e-6
GRID_W = 64
HEAD_DIM = 64
WIN_ROWS = 8
WIN_COLS = 16
POOL_WINDOWS = (2, 4, 8, 16)
Q_ROWS = 4
K_ROWS = 12
PAD_ROWS = 4
NEG = -1e30

ADAM_LR = 0.001
ADAM_B1 = 0.9
ADAM_B2 = 0.999
ADAM_EPS = 1e-08
ADAM_WD = 0.01
ADAM_STEP = 10

ROW_BLOCK = 256
VMEM_LIMIT = 56 * 1024 * 1024
ACT = BF16

MESH = pl.DeviceIdType.MESH
HBM_SPEC = pl.BlockSpec(memory_space=pltpu.HBM)


def _cparams(*sem):
    return pltpu.CompilerParams(dimension_semantics=sem or None, vmem_limit_bytes=VMEM_LIMIT)


def _sds(shape, dtype):
    return jax.ShapeDtypeStruct(tuple(shape), dtype)


def _call(body, args, *, grid, in_specs, out_specs, out_shape, name, scratch_shapes=(), comm=None):
    sem = ("arbitrary",) * len(grid)
    if comm is None:
        res = pl.pallas_call(body, grid=grid, in_specs=list(in_specs), out_specs=list(out_specs), out_shape=list(out_shape),
                             scratch_shapes=list(scratch_shapes), name=name, compiler_params=_cparams(*sem))(*args)
        return list(res), []
    n_in, n_out, n_scr = len(in_specs), len(out_specs), len(scratch_shapes)
    n_cin, n_cout = len(comm.ins), len(comm.outs)

    def carrying(*refs):
        ins, cin = refs[:n_in], refs[n_in:n_in + n_cin]
        outs = refs[n_in + n_cin:n_in + n_cin + n_out]
        cout = refs[n_in + n_cin + n_out:n_in + n_cin + n_out + n_cout]
        rest = refs[n_in + n_cin + n_out + n_cout:]
        scr, sems = rest[:n_scr], rest[n_scr:]
        first, last = True, True
        for ax, size in enumerate(grid):
            first = jnp.logical_and(first, pl.program_id(ax) == 0)
            last = jnp.logical_and(last, pl.program_id(ax) == size - 1)
        pl.when(first)(lambda: comm.start(cin, cout, sems))
        body(*ins, *outs, *scr)
        pl.when(last)(lambda: comm.finish(cin, cout, sems))

    res = pl.pallas_call(
        carrying, grid=grid, in_specs=list(in_specs) + [HBM_SPEC] * n_cin, out_specs=list(out_specs) + [HBM_SPEC] * n_cout,
        out_shape=list(out_shape) + list(comm.outs), scratch_shapes=list(scratch_shapes) + list(comm.sems), name=name,
        compiler_params=_cparams(*sem),
    )(*args, *comm.ins)
    return list(res[:n_out]), list(res[n_out:])


def _sigmoid(x):
    return 1.0 / (1.0 + jnp.exp(-x))


def _silu(x):
    return x * _sigmoid(x)


def _dsilu(x):
    s = _sigmoid(x)
    return s * (1.0 + x * (1.0 - s))


_DIMS = {
    "nn": (((1,), (0,)), ((), ())),
    "nt": (((1,), (1,)), ((), ())),
    "tn": (((0,), (0,)), ((), ())),
}


def _matmul(a, b, *, mode, grid, a_spec, b_spec, out_shapes, out_specs, name, nk=1,
            a_silu=False, exact=False, epilogue=None, extra=(), extra_specs=(), acc_shape=None):
    n_extra = len(extra)
    n_out = len(out_shapes)

    def body(*refs):
        a_ref, b_ref = refs[:2]
        ex = refs[2:2 + n_extra]
        outs = refs[2 + n_extra:2 + n_extra + n_out]
        av = a_ref[...]
        bv = b_ref[...]
        if a_silu:
            av = _silu(av.astype(F32))
        if exact:
            prod = lax.dot_general(av.astype(F32), bv.astype(F32), _DIMS[mode],
                                   precision=lax.Precision.HIGHEST, preferred_element_type=F32)
        else:
            prod = lax.dot_general(av.astype(BF16), bv.astype(BF16), _DIMS[mode], preferred_element_type=F32)

        def finish(res):
            if epilogue is None:
                outs[0][...] = res.astype(outs[0].dtype)
            elif epilogue == "bias":
                outs[0][...] = (res + ex[0][...]).astype(outs[0].dtype)
            else:
                outs[0][...] = res.astype(outs[0].dtype)
                outs[1][...] = ex[0][...] + ex[1][...] * res

        if nk == 1:
            finish(prod)
        else:
            acc = refs[-1]
            k = pl.program_id(len(grid) - 1)

            @pl.when(k == 0)
            def _():
                acc[...] = prod

            @pl.when(k > 0)
            def _():
                acc[...] += prod

            @pl.when(k == nk - 1)
            def _():
                finish(acc[...])

    scratch = [pltpu.VMEM(acc_shape, F32)] if nk > 1 else []
    sem = ("parallel",) * (len(grid) - 1) + ("arbitrary",)
    return pl.pallas_call(
        body, grid=grid, in_specs=[a_spec, b_spec, *extra_specs], out_specs=list(out_specs),
        out_shape=list(out_shapes), scratch_shapes=scratch, name=name, compiler_params=_cparams(*sem),
    )(a, b, *extra)


def _row_tile(rows):
    for t in (768, 512, 256):
        if rows % t == 0:
            return t
    return rows


def _mm_nn(a, b, name, out_dtype=F32, tn=1024):
    m, k = a.shape
    n = b.shape[1]
    tm = _row_tile(m)
    tn = min(tn, n)
    return _matmul(
        a, b, mode="nn", grid=(m // tm, n // tn),
        a_spec=pl.BlockSpec((tm, k), lambda i, j: (i, 0)), b_spec=pl.BlockSpec((k, tn), lambda i, j: (0, j)),
        out_shapes=[_sds((m, n), out_dtype)], out_specs=[pl.BlockSpec((tm, tn), lambda i, j: (i, j))], name=name)[0]


def _mm_out_resid(a, w_out, xres, gate, nxb, name):
    m, k = a.shape
    n = w_out.shape[1]
    tm = ROW_BLOCK
    seg = lambda i, j: (jnp.where(i >= nxb, 1, 0), 0, 0)
    return _matmul(
        a, w_out, mode="nn", grid=(m // tm, 1),
        a_spec=pl.BlockSpec((tm, k), lambda i, j: (i, 0)), b_spec=pl.BlockSpec((k, n), lambda i, j: (0, 0)),
        extra=(xres, gate), extra_specs=(pl.BlockSpec((tm, n), lambda i, j: (i, 0)), pl.BlockSpec((None, 1, n), seg)),
        out_shapes=[_sds((m, n), ACT), _sds((m, n), F32)],
        out_specs=[pl.BlockSpec((tm, n), lambda i, j: (i, 0))] * 2, epilogue="resid", name=name)


def _mm_nt(a, b, name, out_dtype=F32):
    m, n = a.shape
    k = b.shape[0]
    tm = _row_tile(m)
    return _matmul(
        a, b, mode="nt", grid=(m // tm, 1),
        a_spec=pl.BlockSpec((tm, n), lambda i, j: (i, 0)), b_spec=pl.BlockSpec((k, n), lambda i, j: (0, 0)),
        out_shapes=[_sds((m, k), out_dtype)], out_specs=[pl.BlockSpec((tm, k), lambda i, j: (i, 0))], name=name)[0]


def _mm_nt_parts(a, b, name):
    p, m, kp = a.shape
    d = b.shape[0]
    tm = _row_tile(m)
    return _matmul(
        a, b, mode="nt", grid=(m // tm, p), nk=p, acc_shape=(tm, d),
        a_spec=pl.BlockSpec((None, tm, kp), lambda i, q: (q, i, 0)), b_spec=pl.BlockSpec((d, kp), lambda i, q: (0, q)),
        out_shapes=[_sds((m, d), F32)], out_specs=[pl.BlockSpec((tm, d), lambda i, q: (i, 0))], name=name)[0]


def _mm_tn(a, b, name, out_dtype, tm=512):
    r, m = a.shape
    n = b.shape[1]
    tm = min(tm, m)
    tn = min(1024, n)
    return _matmul(
        a, b, mode="tn", grid=(m // tm, n // tn),
        a_spec=pl.BlockSpec((r, tm), lambda i, j: (0, i)), b_spec=pl.BlockSpec((r, tn), lambda i, j: (0, j)),
        out_shapes=[_sds((m, n), out_dtype)], out_specs=[pl.BlockSpec((tm, tn), lambda i, j: (i, j))], name=name)[0]


def _mm_tn_parts(a, b, name, out_dtype, tm=512):
    r, m = a.shape
    p, _, np_ = b.shape
    tm = min(tm, m)
    return _matmul(
        a, b, mode="tn", grid=(m // tm, p),
        a_spec=pl.BlockSpec((r, tm), lambda i, q: (0, i)), b_spec=pl.BlockSpec((None, r, np_), lambda i, q: (q, 0, 0)),
        out_shapes=[_sds((m, p * np_), out_dtype)], out_specs=[pl.BlockSpec((tm, np_), lambda i, q: (i, q))],
        name=name)[0]


def _seg_map(nxb):
    return lambda i: (jnp.where(i >= nxb, 1, 0), 0, 0)


def _normmod_fwd(x, g, scale, shift, nxb, name):
    rows, d = x.shape
    tr = ROW_BLOCK

    def body(x_ref, g_ref, sc_ref, sh_ref, h_ref, r_ref):
        xv = x_ref[...]
        r = lax.rsqrt(jnp.mean(xv * xv, axis=-1, keepdims=True) + EPS)
        h = (xv * r) * g_ref[...] * (1.0 + sc_ref[...]) + sh_ref[...]
        h_ref[...] = h.astype(BF16)
        r_ref[...] = r

    row = pl.BlockSpec((tr, d), lambda i: (i, 0))
    vec = pl.BlockSpec((None, 1, d), _seg_map(nxb))
    return pl.pallas_call(
        body, grid=(rows // tr,), in_specs=[row, pl.BlockSpec((1, d), lambda i: (0, 0)), vec, vec],
        out_specs=[row, pl.BlockSpec((tr, 1), lambda i: (i, 0))],
        out_shape=[_sds((rows, d), BF16), _sds((rows, 1), F32)], name=name, compiler_params=_cparams("parallel"),
    )(x, g, scale, shift)


def _normmod_bwd(dh, x, r, g, scale, dres, nxb, name):
    rows, d = x.shape
    tr = ROW_BLOCK
    nres = dres.shape[0] // tr
    nseg = scale.shape[0]

    def body(dh_ref, x_ref, r_ref, g_ref, sc_ref, dres_ref, dx_ref, dsh_ref, dge_ref):
        i = pl.program_id(0)
        dhv = dh_ref[...]
        rv = r_ref[...]
        xn = x_ref[...] * rv
        dxn = dhv * (g_ref[...] * (1.0 + sc_ref[...]))
        dx = rv * (dxn - xn * jnp.mean(dxn * xn, axis=-1, keepdims=True))

        @pl.when(i < nres)
        def _():
            dx_ref[...] = dx + dres_ref[...]

        @pl.when(i >= nres)
        def _():
            dx_ref[...] = dx

        first = jnp.logical_or(i == 0, i == nxb)
        s_dh = jnp.sum(dhv, axis=0, keepdims=True)
        s_ge = jnp.sum(dhv * xn, axis=0, keepdims=True)

        @pl.when(first)
        def _():
            dsh_ref[...] = s_dh
            dge_ref[...] = s_ge

        @pl.when(jnp.logical_not(first))
        def _():
            dsh_ref[...] += s_dh
            dge_ref[...] += s_ge

    row = pl.BlockSpec((tr, d), lambda i: (i, 0))
    vec = pl.BlockSpec((None, 1, d), _seg_map(nxb))
    return pl.pallas_call(
        body, grid=(rows // tr,),
        in_specs=[row, row, pl.BlockSpec((tr, 1), lambda i: (i, 0)), pl.BlockSpec((1, d), lambda i: (0, 0)), vec,
                  pl.BlockSpec((tr, d), lambda i: (jnp.minimum(i, nres - 1), 0))],
        out_specs=[row, vec, vec],
        out_shape=[_sds((rows, d), F32), _sds((nseg, 1, d), F32), _sds((nseg, 1, d), F32)],
        name=name, compiler_params=_cparams("arbitrary"),
    )(dh, x, r, g, scale, dres)


def _gate_bwd(dxo, yx, gate, nxb, name):
    rows, d = yx.shape
    tr = ROW_BLOCK
    nseg = gate.shape[0]

    def body(dx_ref, yx_ref, gt_ref, dyx_ref, dg_ref):
        i = pl.program_id(0)
        dxv = dx_ref[...]
        dyx_ref[...] = (dxv * gt_ref[...]).astype(BF16)
        s = jnp.sum(dxv * yx_ref[...].astype(F32), axis=0, keepdims=True)
        first = jnp.logical_or(i == 0, i == nxb)

        @pl.when(first)
        def _():
            dg_ref[...] = s

        @pl.when(jnp.logical_not(first))
        def _():
            dg_ref[...] += s

    row = pl.BlockSpec((tr, d), lambda i: (i, 0))
    vec = pl.BlockSpec((None, 1, d), _seg_map(nxb))
    return pl.pallas_call(
        body, grid=(rows // tr,), in_specs=[row, row, vec], out_specs=[row, vec],
        out_shape=[_sds((rows, d), BF16), _sds((nseg, 1, d), F32)], name=name, compiler_params=_cparams("arbitrary"),
    )(dxo, yx, gate)


def _row_vec(ref, is_ctx):
    return ref[0] if is_ctx is None else jnp.where(is_ctx, ref[1], ref[0])


def _ctx_rows(i, tm, nx, nseg):
    if nseg == 1:
        return None
    return i * tm + lax.broadcasted_iota(jnp.int32, (tm, 1), 0) >= nx


def _seg_sums(ref, val, is_ctx, first):
    if is_ctx is None:
        parts = [jnp.sum(val, axis=0, keepdims=True)]
    else:
        parts = [jnp.sum(jnp.where(is_ctx, 0.0, val), axis=0, keepdims=True),
                 jnp.sum(jnp.where(is_ctx, val, 0.0), axis=0, keepdims=True)]

    @pl.when(first)
    def _():
        for k, p in enumerate(parts):
            ref[k] = p

    @pl.when(jnp.logical_not(first))
    def _():
        for k, p in enumerate(parts):
            ref[k] += p


def _w_out_resid(a, w_out, xres, gate, nx, name):
    m, k = a.shape
    n = w_out.shape[1]
    nseg = gate.shape[0]
    tm = _row_tile(m)

    def body(a_ref, w_ref, x_ref, gt_ref, yx_ref, xo_ref):
        yx = jnp.dot(a_ref[...], w_ref[...], preferred_element_type=F32)
        yx_ref[...] = yx.astype(ACT)
        xo_ref[...] = x_ref[...] + _row_vec(gt_ref, _ctx_rows(pl.program_id(0), tm, nx, nseg)) * yx

    row = pl.BlockSpec((tm, n), lambda i: (i, 0))
    return pl.pallas_call(
        body, grid=(m // tm,),
        in_specs=[pl.BlockSpec((tm, k), lambda i: (i, 0)), pl.BlockSpec((k, n), lambda i: (0, 0)), row,
                  pl.BlockSpec((nseg, 1, n), lambda i: (0, 0, 0))],
        out_specs=[row, row], out_shape=[_sds((m, n), ACT), _sds((m, n), F32)],
        name=name, compiler_params=_cparams("parallel"),
    )(a, w_out, xres, gate)


def _norm_w_in(x, g, scale, shift, w_in, nx, name, comm=None):
    rows, d = x.shape
    n = w_in.shape[1]
    nseg = scale.shape[0]
    tm = _row_tile(rows)
    tn = min(1024, n)

    def body(x_ref, g_ref, sc_ref, sh_ref, w_ref, h_ref, r_ref, p_ref):
        i, j = pl.program_id(0), pl.program_id(1)

        @pl.when(j == 0)
        def _():
            xv = x_ref[...]
            r = lax.rsqrt(jnp.mean(xv * xv, axis=-1, keepdims=True) + EPS)
            is_ctx = _ctx_rows(i, tm, nx, nseg)
            h = (xv * r) * g_ref[...] * (1.0 + _row_vec(sc_ref, is_ctx)) + _row_vec(sh_ref, is_ctx)
            h_ref[...] = h.astype(BF16)
            r_ref[...] = r

        p_ref[...] = jnp.dot(h_ref[...], w_ref[...], preferred_element_type=F32).astype(ACT)

    vec = pl.BlockSpec((nseg, 1, d), lambda i, j: (0, 0, 0))
    return _call(
        body, (x, g, scale, shift, w_in), grid=(rows // tm, n // tn),
        in_specs=[pl.BlockSpec((tm, d), lambda i, j: (i, 0)), pl.BlockSpec((1, d), lambda i, j: (0, 0)), vec, vec,
                  pl.BlockSpec((d, tn), lambda i, j: (0, j))],
        out_specs=[pl.BlockSpec((tm, d), lambda i, j: (i, 0)), pl.BlockSpec((tm, 1), lambda i, j: (i, 0)),
                   pl.BlockSpec((tm, tn), lambda i, j: (i, j))],
        out_shape=[_sds((rows, d), BF16), _sds((rows, 1), F32), _sds((rows, n), ACT)], name=name, comm=comm)


def _gate_w_out_bwd(dxo, yx, gate, w_out, nx, name, comm=None):
    rows, d = yx.shape
    w = w_out.shape[0]
    nseg = gate.shape[0]
    tm = _row_tile(rows)

    def body(dx_ref, yx_ref, gt_ref, w_ref, dyx_ref, da_ref, dg_ref):
        i = pl.program_id(0)
        is_ctx = _ctx_rows(i, tm, nx, nseg)
        dxv = dx_ref[...]
        dyx = (dxv * _row_vec(gt_ref, is_ctx)).astype(BF16)
        dyx_ref[...] = dyx
        da_ref[...] = lax.dot_general(dyx, w_ref[...], _DIMS["nt"], preferred_element_type=F32).astype(ACT)
        _seg_sums(dg_ref, dxv * yx_ref[...].astype(F32), is_ctx, i == 0)

    row = pl.BlockSpec((tm, d), lambda i: (i, 0))
    vec = pl.BlockSpec((nseg, 1, d), lambda i: (0, 0, 0))
    return _call(
        body, (dxo, yx, gate, w_out), grid=(rows // tm,),
        in_specs=[row, row, vec, pl.BlockSpec((w, d), lambda i: (0, 0))],
        out_specs=[row, pl.BlockSpec((tm, w), lambda i: (i, 0)), vec],
        out_shape=[_sds((rows, d), BF16), _sds((rows, w), ACT), _sds((nseg, 1, d), F32)], name=name, comm=comm)


def _w_in_bwd_norm(dparts, w_in, x, r, g, scale, dres, nx, name, comm=None):
    np_, rows, kp = dparts.shape
    d = w_in.shape[0]
    nseg = scale.shape[0]
    tm = _row_tile(rows)
    nsub = tm // ROW_BLOCK
    nres_blocks = dres.shape[0] // ROW_BLOCK

    def body(dp_ref, w_ref, x_ref, r_ref, g_ref, sc_ref, *rest):
        dres_refs = rest[:nsub]
        dx_ref, dsh_ref, dge_ref, acc = rest[nsub:]
        i, k = pl.program_id(0), pl.program_id(1)
        prod = lax.dot_general(dp_ref[...], w_ref[...], _DIMS["nt"], preferred_element_type=F32)

        @pl.when(k == 0)
        def _():
            acc[...] = prod

        @pl.when(k > 0)
        def _():
            acc[...] += prod

        @pl.when(k == np_ - 1)
        def _():
            is_ctx = _ctx_rows(i, tm, nx, nseg)
            dhv = acc[...]
            rv = r_ref[...]
            xn = x_ref[...] * rv
            dxn = dhv * (g_ref[...] * (1.0 + _row_vec(sc_ref, is_ctx)))
            dx = rv * (dxn - xn * jnp.mean(dxn * xn, axis=-1, keepdims=True))
            for s in range(nsub):
                piece = slice(s * ROW_BLOCK, (s + 1) * ROW_BLOCK)
                res = dres_refs[s][...]
                if nres_blocks * ROW_BLOCK < rows:
                    res = jnp.where(i * nsub + s < nres_blocks, res, 0.0)
                dx_ref[piece, :] = dx[piece, :] + res
            _seg_sums(dsh_ref, dhv, is_ctx, i == 0)
            _seg_sums(dge_ref, dhv * xn, is_ctx, i == 0)

    row = pl.BlockSpec((tm, d), lambda i, k: (i, 0))
    vec = pl.BlockSpec((nseg, 1, d), lambda i, k: (0, 0, 0))
    return _call(
        body, (dparts, w_in, x, r, g, scale, *([dres] * nsub)), grid=(rows // tm, np_),
        in_specs=[pl.BlockSpec((None, tm, kp), lambda i, k: (k, i, 0)), pl.BlockSpec((d, kp), lambda i, k: (0, k)),
                  row, pl.BlockSpec((tm, 1), lambda i, k: (i, 0)), pl.BlockSpec((1, d), lambda i, k: (0, 0)), vec]
        + [pl.BlockSpec((ROW_BLOCK, d), (lambda i, k, s=s: (jnp.minimum(i * nsub + s, nres_blocks - 1), 0)))
           for s in range(nsub)],
        out_specs=[row, vec, vec],
        out_shape=[_sds((rows, d), F32), _sds((nseg, 1, d), F32), _sds((nseg, 1, d), F32)],
        scratch_shapes=[pltpu.VMEM((tm, d), F32)], name=name, comm=comm)


_PAD_TOP = 16
_PAD_BOT = 32


def _window_sum(buf, xv, lo, n):
    t = xv.shape[0]
    c = xv.shape[1]
    tp = t + _PAD_TOP + _PAD_BOT
    buf[pl.ds(0, _PAD_TOP), :] = jnp.zeros((_PAD_TOP, c), F32)
    buf[pl.ds(_PAD_TOP, t), :] = xv
    buf[pl.ds(_PAD_TOP + t, _PAD_BOT), :] = jnp.zeros((_PAD_BOT, c), F32)
    p = buf[...]
    k = 1
    while k < n:
        p = p + pltpu.roll(p, tp - k, 0)
        k *= 2
    if lo:
        p = pltpu.roll(p, -lo, 0)
    buf[...] = p
    return buf[pl.ds(_PAD_TOP, t), :]


def _window_count(t, half):
    pos = lax.broadcasted_iota(jnp.int32, (t, 1), 0)
    return (jnp.minimum(pos + half, t) - jnp.maximum(pos - half, 0)).astype(F32)


def _segments(rows, nx):
    return [(0, nx)] + ([(nx, rows - nx)] if rows > nx else [])


def _pool_fwd(uv, nx, name):
    rows = uv.shape[0]
    w = uv.shape[1] // 2
    cb = 128
    per_group = w // len(POOL_WINDOWS) // cb
    segs = _segments(rows, nx)

    def body(u_ref, z_ref, *bufs):
        j = pl.program_id(0)
        for gi, win in enumerate(POOL_WINDOWS):
            half = win // 2

            @pl.when(jnp.logical_and(j >= gi * per_group, j < (gi + 1) * per_group))
            def _():
                for (start, length), buf in zip(segs, bufs):
                    uvv = u_ref[pl.ds(start, length), :].astype(F32)
                    s = _window_sum(buf, uvv, -half, win)
                    z_ref[pl.ds(start, length), :] = (s / _window_count(length, half) - uvv).astype(BF16)

    scratch = [pltpu.VMEM((length + _PAD_TOP + _PAD_BOT, cb), F32) for _, length in segs]
    return pl.pallas_call(
        body, grid=(w // cb,), in_specs=[pl.BlockSpec((rows, cb), lambda j: (0, j))],
        out_specs=pl.BlockSpec((rows, cb), lambda j: (0, j)), out_shape=_sds((rows, w), BF16),
        scratch_shapes=scratch, name=name, compiler_params=_cparams("parallel"),
    )(uv)


def _pool_bwd(dz, dgt, nx, name):
    rows, w = dz.shape
    cb = 128
    per_group = w // len(POOL_WINDOWS) // cb
    segs = _segments(rows, nx)

    def body(dz_ref, dgt_ref, o_ref, *bufs):
        j = pl.program_id(0)
        o_ref[1] = dgt_ref[...]
        for gi, win in enumerate(POOL_WINDOWS):
            half = win // 2

            @pl.when(jnp.logical_and(j >= gi * per_group, j < (gi + 1) * per_group))
            def _():
                for (start, length), buf in zip(segs, bufs):
                    dzv = dz_ref[pl.ds(start, length), :].astype(F32)
                    s = _window_sum(buf, dzv / _window_count(length, half), 1 - half, win)
                    o_ref[0, pl.ds(start, length), :] = (s - dzv).astype(BF16)

    scratch = [pltpu.VMEM((length + _PAD_TOP + _PAD_BOT, cb), F32) for _, length in segs]
    col = pl.BlockSpec((rows, cb), lambda j: (0, j))
    return pl.pallas_call(
        body, grid=(w // cb,), in_specs=[col, col], out_specs=pl.BlockSpec((2, rows, cb), lambda j: (0, 0, j)),
        out_shape=_sds((2, rows, w), BF16), scratch_shapes=scratch, name=name, compiler_params=_cparams("parallel"),
    )(dz, dgt)


def _grp_fwd(z, w_grp, uv, scale, name):
    rows, w = z.shape
    ng, gc, _ = w_grp.shape
    tm = _row_tile(rows)

    def body(z_ref, w_ref, gt_ref, sc_ref, mx_ref, a_ref):
        mixed = jnp.dot(z_ref[...], w_ref[...], preferred_element_type=F32)
        mx_ref[...] = mixed.astype(ACT)
        a_ref[...] = (mixed * sc_ref[...] * _silu(gt_ref[...].astype(F32))).astype(BF16)

    blk = pl.BlockSpec((tm, gc), lambda g, i: (i, g))
    return pl.pallas_call(
        body, grid=(ng, rows // tm),
        in_specs=[blk, pl.BlockSpec((None, gc, gc), lambda g, i: (g, 0, 0)),
                  pl.BlockSpec((tm, gc), lambda g, i: (i, ng + g)), pl.BlockSpec((1, gc), lambda g, i: (0, g))],
        out_specs=[blk, blk], out_shape=[_sds((rows, w), ACT), _sds((rows, w), BF16)],
        name=name, compiler_params=_cparams("parallel", "parallel"),
    )(z, w_grp, uv, scale)


def _grp_bwd(da, mixed, uv, scale, w_grp, name):
    rows, w = da.shape
    ng, gc, _ = w_grp.shape
    tm = _row_tile(rows)

    def body(da_ref, mx_ref, gt_ref, sc_ref, w_ref, dm_ref, dz_ref, dgt_ref, dsc_ref):
        i = pl.program_id(1)
        dav = da_ref[...].astype(F32)
        mixed = mx_ref[...].astype(F32)
        gt = gt_ref[...].astype(F32)
        sg = _silu(gt)
        sc = sc_ref[...]
        dm = (dav * sc * sg).astype(BF16)
        dm_ref[...] = dm
        dz_ref[...] = lax.dot_general(dm, w_ref[...], _DIMS["nt"], preferred_element_type=F32).astype(ACT)
        dgt_ref[...] = (dav * mixed * sc * _dsilu(gt)).astype(BF16)
        s = jnp.sum(dav * mixed * sg, axis=0, keepdims=True)

        @pl.when(i == 0)
        def _():
            dsc_ref[...] = s

        @pl.when(i > 0)
        def _():
            dsc_ref[...] += s

    blk = pl.BlockSpec((tm, gc), lambda g, i: (i, g))
    vec = pl.BlockSpec((1, gc), lambda g, i: (0, g))
    return pl.pallas_call(
        body, grid=(ng, rows // tm),
        in_specs=[blk, blk, pl.BlockSpec((tm, gc), lambda g, i: (i, ng + g)), vec,
                  pl.BlockSpec((None, gc, gc), lambda g, i: (g, 0, 0))],
        out_specs=[blk, blk, blk, vec],
        out_shape=[_sds((rows, w), BF16), _sds((rows, w), ACT), _sds((rows, w), BF16), _sds((1, w), F32)],
        name=name, compiler_params=_cparams("parallel", "arbitrary"),
    )(da, mixed, uv, scale, w_grp)


def _pool_scratch(rows, nx, cols):
    return [pltpu.VMEM((length + _PAD_TOP + _PAD_BOT, cols), F32) for _, length in _segments(rows, nx)]


def _per_group(g, fn):
    for gi, win in enumerate(POOL_WINDOWS):
        pl.when(g == gi)(functools.partial(fn, win))


def _pool_grp_fwd(uv, w_grp, scale, nx, name, comm=None):
    rows = uv.shape[0]
    ng, gc, _ = w_grp.shape
    w = ng * gc
    segs = _segments(rows, nx)

    def body(u_ref, gt_ref, w_ref, sc_ref, z_ref, mx_ref, a_ref, *bufs):
        def pool(win):
            half = win // 2
            for (start, length), buf in zip(segs, bufs):
                uvv = u_ref[pl.ds(start, length), :].astype(F32)
                s = _window_sum(buf, uvv, -half, win)
                z_ref[pl.ds(start, length), :] = (s / _window_count(length, half) - uvv).astype(BF16)

        _per_group(pl.program_id(0), pool)
        mixed = jnp.dot(z_ref[...], w_ref[...], preferred_element_type=F32)
        mx_ref[...] = mixed.astype(ACT)
        a_ref[...] = (mixed * sc_ref[...] * _silu(gt_ref[...].astype(F32))).astype(BF16)

    col = pl.BlockSpec((rows, gc), lambda g: (0, g))
    return _call(
        body, (uv, uv, w_grp, scale), grid=(ng,),
        in_specs=[col, pl.BlockSpec((rows, gc), lambda g: (0, ng + g)), pl.BlockSpec((None, gc, gc), lambda g: (g, 0, 0)),
                  pl.BlockSpec((1, gc), lambda g: (0, g))],
        out_specs=[col, col, col], out_shape=[_sds((rows, w), BF16), _sds((rows, w), ACT), _sds((rows, w), BF16)],
        scratch_shapes=_pool_scratch(rows, nx, gc), name=name, comm=comm)


def _pool_grp_bwd(da, mixed, uv, scale, w_grp, nx, name):
    rows, w = da.shape
    ng, gc, _ = w_grp.shape
    segs = _segments(rows, nx)

    def body(da_ref, mx_ref, gt_ref, sc_ref, w_ref, dm_ref, duv_ref, dsc_ref, dz_ref, *bufs):
        dav = da_ref[...].astype(F32)
        mixed = mx_ref[...].astype(F32)
        gt = gt_ref[...].astype(F32)
        sg = _silu(gt)
        sc = sc_ref[...]
        dm = (dav * sc * sg).astype(BF16)
        dm_ref[...] = dm
        dz_ref[...] = lax.dot_general(dm, w_ref[...], _DIMS["nt"], preferred_element_type=F32)
        duv_ref[1] = (dav * mixed * sc * _dsilu(gt)).astype(BF16)
        dsc_ref[...] = jnp.sum(dav * mixed * sg, axis=0, keepdims=True)

        def unpool(win):
            half = win // 2
            for (start, length), buf in zip(segs, bufs):
                dzv = dz_ref[pl.ds(start, length), :]
                s = _window_sum(buf, dzv / _window_count(length, half), 1 - half, win)
                duv_ref[0, pl.ds(start, length), :] = (s - dzv).astype(BF16)

        _per_group(pl.program_id(0), unpool)

    col = pl.BlockSpec((rows, gc), lambda g: (0, g))
    vec = pl.BlockSpec((1, gc), lambda g: (0, g))
    return pl.pallas_call(
        body, grid=(ng,),
        in_specs=[col, col, pl.BlockSpec((rows, gc), lambda g: (0, ng + g)), vec,
                  pl.BlockSpec((None, gc, gc), lambda g: (g, 0, 0))],
        out_specs=[col, pl.BlockSpec((2, rows, gc), lambda g: (0, 0, g)), vec],
        out_shape=[_sds((rows, w), BF16), _sds((2, rows, w), BF16), _sds((1, w), F32)],
        scratch_shapes=[pltpu.VMEM((rows, gc), F32)] + _pool_scratch(rows, nx, gc),
        name=name, compiler_params=_cparams("parallel"),
    )(da, mixed, uv, scale, w_grp)


def _grp_wgrad(z, dm, ng, name, out_dtype):
    rows, w = z.shape
    gc = w // ng

    def body(z_ref, dm_ref, o_ref):
        o_ref[...] = lax.dot_general(z_ref[...], dm_ref[...], _DIMS["tn"],
                                     preferred_element_type=F32).astype(o_ref.dtype)

    blk = pl.BlockSpec((rows, gc), lambda g: (0, g))
    return pl.pallas_call(
        body, grid=(ng,), in_specs=[blk, blk], out_specs=pl.BlockSpec((None, gc, gc), lambda g: (g, 0, 0)),
        out_shape=_sds((ng, gc, gc), out_dtype), name=name, compiler_params=_cparams("parallel"),
    )(z, dm)


def _shift_rows(v, by):
    t = v.shape[0]
    pos = lax.broadcasted_iota(jnp.int32, v.shape, 0)
    rolled = pltpu.roll(v, by % t, 0)
    keep = pos >= by if by > 0 else pos < t + by
    return jnp.where(keep, rolled, 0.0)


def _conv_specs(t, w, cb):
    return [pl.BlockSpec((t, cb), (lambda j, q=q: (0, q * (w // cb) + j))) for q in range(4)]


def _conv_fwd(p4, dw, db, name):
    t = p4.shape[0]
    w = p4.shape[1] // 4
    cb = 128

    def body(bg_ref, cg_ref, v_ref, g_ref, dw_ref, db_ref, a_ref):
        tv = cg_ref[...].astype(F32) * v_ref[...].astype(F32)
        conv = (dw_ref[0:1, :] * _shift_rows(tv, 1) + dw_ref[1:2, :] * tv + dw_ref[2:3, :] * _shift_rows(tv, -1)
                + db_ref[...])
        a_ref[...] = (bg_ref[...].astype(F32) * conv * _silu(g_ref[...].astype(F32))).astype(BF16)

    return pl.pallas_call(
        body, grid=(w // cb,),
        in_specs=_conv_specs(t, w, cb) + [pl.BlockSpec((3, cb), lambda j: (0, j)), pl.BlockSpec((1, cb), lambda j: (0, j))],
        out_specs=pl.BlockSpec((t, cb), lambda j: (0, j)), out_shape=_sds((t, w), BF16),
        name=name, compiler_params=_cparams("parallel"),
    )(p4, p4, p4, p4, dw, db)


def _conv_bwd(da, p4, dw, db, name):
    t, w = da.shape
    cb = 128

    def body(da_ref, bg_ref, cg_ref, v_ref, g_ref, dw_ref, db_ref, d4_ref, ddw_ref, ddb_ref):
        cg = cg_ref[...].astype(F32)
        vv = v_ref[...].astype(F32)
        bg = bg_ref[...].astype(F32)
        gv = g_ref[...].astype(F32)
        tv = cg * vv
        tm1 = _shift_rows(tv, 1)
        tp1 = _shift_rows(tv, -1)
        w0, w1, w2 = dw_ref[0:1, :], dw_ref[1:2, :], dw_ref[2:3, :]
        conv = w0 * tm1 + w1 * tv + w2 * tp1 + db_ref[...]
        y = bg * conv
        dav = da_ref[...].astype(F32)
        dy = dav * _silu(gv)
        d4_ref[3] = (dav * y * _dsilu(gv)).astype(BF16)
        d4_ref[0] = (dy * conv).astype(BF16)
        dconv = dy * bg
        ddb_ref[...] = jnp.sum(dconv, axis=0, keepdims=True)
        ddw_ref[0:1, :] = jnp.sum(dconv * tm1, axis=0, keepdims=True)
        ddw_ref[1:2, :] = jnp.sum(dconv * tv, axis=0, keepdims=True)
        ddw_ref[2:3, :] = jnp.sum(dconv * tp1, axis=0, keepdims=True)
        dt = w0 * _shift_rows(dconv, -1) + w1 * dconv + w2 * _shift_rows(dconv, 1)
        d4_ref[1] = (dt * vv).astype(BF16)
        d4_ref[2] = (dt * cg).astype(BF16)

    col = pl.BlockSpec((t, cb), lambda j: (0, j))
    tap = pl.BlockSpec((3, cb), lambda j: (0, j))
    bias = pl.BlockSpec((1, cb), lambda j: (0, j))
    return pl.pallas_call(
        body, grid=(w // cb,), in_specs=[col] + _conv_specs(t, w, cb) + [tap, bias],
        out_specs=[pl.BlockSpec((4, t, cb), lambda j: (0, 0, j)), tap, bias],
        out_shape=[_sds((4, t, w), BF16), _sds((3, w), F32), _sds((1, w), F32)],
        name=name, compiler_params=_cparams("parallel"),
    )(da, p4, p4, p4, p4, dw, db)


def _attn_mask():
    qn, kn = Q_ROWS * GRID_W, K_ROWS * GRID_W
    qr, qc = np.divmod(np.arange(qn), GRID_W)
    kr, kc = np.divmod(np.arange(kn), GRID_W)
    col0 = np.clip(qc - WIN_COLS // 2, 0, GRID_W - WIN_COLS)
    col_ok = (kc[None, :] >= col0[:, None]) & (kc[None, :] < col0[:, None] + WIN_COLS)
    first = np.zeros(qn, np.int64)
    last = np.full(qn, K_ROWS - WIN_ROWS)
    out = []
    for row0 in (first, qr, last):
        row_ok = (kr[None, :] >= row0[:, None]) & (kr[None, :] < row0[:, None] + WIN_ROWS)
        out.append(np.where(row_ok & col_ok, 0.0, NEG))
    return jnp.asarray(np.stack(out), F32)


_KW = K_ROWS * GRID_W
_QB = Q_ROWS * GRID_W
_PAIR = 2 * HEAD_DIM
_N_DR = 2 * WIN_ROWS - 1
_N_DC = 2 * WIN_COLS - 1
_RP_ROWS = 24
_N_TILES = _N_DR + 1
_BIAS_BASE = (WIN_ROWS - 1, WIN_ROWS // 2 - 1, -1)


class _Comm:
    def __init__(self, ins, outs, sems, start, finish):
        self.ins, self.outs, self.sems, self.start, self.finish = list(ins), list(outs), list(sems), start, finish


def _bias_pieces(cls):
    out = []
    for qr in range(Q_ROWS):
        for kr in range(0, K_ROWS, 2):
            tile = _BIAS_BASE[cls] - qr + kr + 1
            out.append((qr, kr, tile if 0 <= tile < _N_TILES else None))
    return out


def _toeplitz_pair(left_row, right_row):
    lane = lax.broadcasted_iota(jnp.int32, (GRID_W, _PAIR), 1)
    shape = (GRID_W, _PAIR)
    left = pltpu.roll(jnp.broadcast_to(left_row, shape), _PAIR - (WIN_COLS - 1), 1, stride=1, stride_axis=0)
    right = pltpu.roll(jnp.broadcast_to(right_row, shape), GRID_W - (WIN_COLS - 1), 1, stride=1, stride_axis=0)
    return jnp.where(lane < GRID_W, left, right)


def _build_tiles(tiles_ref, rp_ref):
    for h in range(2):
        for t in range(_N_TILES):
            tiles_ref[h, t] = _toeplitz_pair(rp_ref[h, t:t + 1, :], rp_ref[h, t + 1:t + 2, :])


def _block_class(b, nblk, fn, entering=False):
    interior = (b == 1) if entering else jnp.logical_and(b > 0, b < nblk - 1)
    for cls, cond in enumerate((b == 0, interior, b == nblk - 1)):
        pl.when(cond)(functools.partial(fn, cls))


def _attn_geometry(p4, nx):
    rows = p4.shape[0]
    w = p4.shape[1] // 4
    nhp = w // _PAIR
    nblk = nx // _QB
    qspec = lambda col: pl.BlockSpec((_QB, _PAIR), lambda hp, b: (b, col * nhp + hp))
    kspec = lambda col: pl.BlockSpec((rows, _PAIR), lambda hp, b: (0, col * nhp + hp))
    tspec = pl.BlockSpec((2, _RP_ROWS, _PAIR), lambda hp, b: (hp, 0, 0))
    mspec = pl.BlockSpec((None, _QB, _KW), lambda hp, b: (jnp.where(b == 0, 0, jnp.where(b == nblk - 1, 2, 1)), 0, 0))
    lspec = pl.BlockSpec((None, _QB, 2), lambda hp, b: (hp, b, 0))
    ospec = pl.BlockSpec((_QB, _PAIR), lambda hp, b: (b, hp))
    return rows, w, nhp, nblk, qspec, kspec, tspec, mspec, lspec, ospec


def _window_start(b, nx):
    return pl.multiple_of(jnp.clip(b * _QB - PAD_ROWS * GRID_W, 0, nx - _KW), _QB)


def _load_bias(bias_ref, tiles_ref, rp_ref, m_ref, b, nblk):
    pl.when(b == 0)(lambda: _build_tiles(tiles_ref, rp_ref))

    def fill(cls):
        for h in range(2):
            for qr, kr, tile in _bias_pieces(cls):
                rows = slice(qr * GRID_W, (qr + 1) * GRID_W)
                cols = slice(kr * GRID_W, (kr + 2) * GRID_W)
                m = m_ref[rows, cols]
                bias_ref[h, rows, cols] = m if tile is None else tiles_ref[h, tile] + m

    _block_class(b, nblk, fill, entering=True)


def _attn_fwd(p4, rp, mask, nx, name, comm=None):
    rows, w, nhp, nblk, qspec, kspec, tspec, mspec, lspec, ospec = _attn_geometry(p4, nx)
    n_ctx = rows - nx
    n_cin, n_cout = (len(comm.ins), len(comm.outs)) if comm else (0, 0)

    def body(*refs):
        q_ref, k_ref, v_ref, g_ref, rp_ref, m_ref = refs[:6]
        cin = refs[6:6 + n_cin]
        a_ref, o_ref, lse_ref = refs[6 + n_cin:9 + n_cin]
        cout = refs[9 + n_cin:9 + n_cin + n_cout]
        bias_ref, tiles_ref = refs[9 + n_cin + n_cout:11 + n_cin + n_cout]
        sems = refs[11 + n_cin + n_cout:]
        hp, b = pl.program_id(0), pl.program_id(1)
        if comm:
            pl.when(jnp.logical_and(hp == 0, b == 0))(lambda: comm.start(cin, cout, sems))
        start = _window_start(b, nx)
        _load_bias(bias_ref, tiles_ref, rp_ref, m_ref, b, nblk)
        qf = q_ref[...].astype(F32) * HEAD_DIM ** -0.5
        kw = k_ref[pl.ds(start, _KW), :].astype(BF16)
        vw = v_ref[pl.ds(start, _KW), :].astype(BF16)
        kcv = k_ref[pl.ds(nx, n_ctx), :].astype(BF16)
        vcv = v_ref[pl.ds(nx, n_ctx), :].astype(BF16)
        lane = lax.broadcasted_iota(jnp.int32, (1, _PAIR), 1)
        outs, lses = [], []
        for h in range(2):
            mine = (lane >= HEAD_DIM) if h else (lane < HEAD_DIM)
            qm = jnp.where(mine, qf, 0.0).astype(BF16)
            s_loc = lax.dot_general(qm, kw, _DIMS["nt"], preferred_element_type=F32) + bias_ref[h]
            s_ctx = lax.dot_general(qm, kcv, _DIMS["nt"], preferred_element_type=F32)
            mx = jnp.maximum(jnp.max(s_loc, axis=-1, keepdims=True), jnp.max(s_ctx, axis=-1, keepdims=True))
            p_loc = jnp.exp(s_loc - mx)
            p_ctx = jnp.exp(s_ctx - mx)
            den = jnp.sum(p_loc, axis=-1, keepdims=True) + jnp.sum(p_ctx, axis=-1, keepdims=True)
            o = jnp.dot(p_loc.astype(BF16), vw, preferred_element_type=F32)
            o = o + jnp.dot(p_ctx.astype(BF16), vcv, preferred_element_type=F32)
            outs.append(o * (1.0 / den))
            lses.append(mx + jnp.log(den))
        o = jnp.where(lane < HEAD_DIM, outs[0], outs[1])
        o_ref[...] = o.astype(ACT)
        a_ref[...] = (o * _silu(g_ref[...].astype(F32))).astype(BF16)
        col = lax.broadcasted_iota(jnp.int32, (1, 2), 1)
        lse_ref[...] = jnp.where(col == 0, lses[0], lses[1])
        if comm:
            pl.when(jnp.logical_and(hp == nhp - 1, b == nblk - 1))(lambda: comm.finish(cin, cout, sems))

    res = pl.pallas_call(
        body, grid=(nhp, nblk),
        in_specs=[qspec(0), kspec(1), kspec(2), qspec(3), tspec, mspec] + [HBM_SPEC] * n_cin,
        out_specs=[ospec, ospec, lspec] + [HBM_SPEC] * n_cout,
        out_shape=[_sds((nx, w), BF16), _sds((nx, w), ACT), _sds((nhp, nx, 2), F32)] + (comm.outs if comm else []),
        scratch_shapes=[pltpu.VMEM((2, _QB, _KW), F32), pltpu.VMEM((2, _N_TILES, GRID_W, _PAIR), F32)]
        + (comm.sems if comm else []),
        name=name, compiler_params=_cparams("arbitrary", "arbitrary"),
    )(p4, p4, p4, p4, rp, mask, *(comm.ins if comm else []))
    return res[:3], res[3:]


def _fold_tiles(dtiles_ref, drp_ref):
    shape = (GRID_W, _PAIR)
    lane = lax.broadcasted_iota(jnp.int32, shape, 1)
    flip = (lax.broadcasted_iota(jnp.int32, (_PAIR, _PAIR), 0)
            + lax.broadcasted_iota(jnp.int32, (_PAIR, _PAIR), 1) == _PAIR - 1).astype(F32)
    drp_ref[...] = jnp.zeros(drp_ref.shape, F32)
    for h in range(2):
        stack = dtiles_ref[h].reshape(_N_TILES * GRID_W, _PAIR)
        rev = jnp.dot(stack, flip, precision=lax.Precision.HIGHEST, preferred_element_type=F32)
        for t in range(_N_TILES):
            tile = rev[t * GRID_W:(t + 1) * GRID_W, :]
            for side in (0, 1):
                shift = _PAIR - GRID_W * side - (WIN_COLS - 1)
                half = jnp.where((lane < GRID_W) if side else (lane >= GRID_W), tile, 0.0)
                diag = pltpu.roll(half, shift, 1, stride=1, stride_axis=0)
                drp_ref[h, t + side:t + side + 1, :] += jnp.sum(diag, axis=0, keepdims=True)


def _attn_bwd(p4, rp, mask, o, lse, da, nx, name, comm=None):
    rows, w, nhp, nblk, qspec, kspec, tspec, mspec, lspec, ospec = _attn_geometry(p4, nx)
    n_ctx = rows - nx
    n_cin, n_cout = (len(comm.ins), len(comm.outs)) if comm else (0, 0)

    def body(*refs):
        q_ref, k_ref, v_ref, g_ref, rp_ref, m_ref, o_ref, lse_ref, da_ref = refs[:9]
        cin = refs[9:9 + n_cin]
        d4_ref, drp_ref = refs[9 + n_cin:11 + n_cin]
        cout = refs[11 + n_cin:11 + n_cin + n_cout]
        bias_ref, tiles_ref, ds_ref, dtiles_ref, dk_ref, dv_ref = refs[11 + n_cin + n_cout:17 + n_cin + n_cout]
        sems = refs[17 + n_cin + n_cout:]
        hp, b = pl.program_id(0), pl.program_id(1)
        if comm:
            pl.when(jnp.logical_and(hp == 0, b == 0))(lambda: comm.start(cin, cout, sems))
        start = _window_start(b, nx)
        here = pl.multiple_of(b * _QB, _QB)

        @pl.when(b == 0)
        def _():
            dk_ref[...] = jnp.zeros(dk_ref.shape, F32)
            dv_ref[...] = jnp.zeros(dv_ref.shape, F32)
            dtiles_ref[...] = jnp.zeros(dtiles_ref.shape, F32)
            d4_ref[0, pl.ds(nx, n_ctx), :] = jnp.zeros((n_ctx, _PAIR), BF16)
            d4_ref[3, pl.ds(nx, n_ctx), :] = jnp.zeros((n_ctx, _PAIR), BF16)

        _load_bias(bias_ref, tiles_ref, rp_ref, m_ref, b, nblk)
        gv = g_ref[...].astype(F32)
        dav = da_ref[...].astype(F32)
        ov = o_ref[...].astype(F32)
        dov = dav * _silu(gv)
        d4_ref[3, pl.ds(here, _QB), :] = (dav * ov * _dsilu(gv)).astype(BF16)
        qf = q_ref[...].astype(F32) * HEAD_DIM ** -0.5
        kw = k_ref[pl.ds(start, _KW), :].astype(BF16)
        vw = v_ref[pl.ds(start, _KW), :].astype(BF16)
        kcv = k_ref[pl.ds(nx, n_ctx), :].astype(BF16)
        vcv = v_ref[pl.ds(nx, n_ctx), :].astype(BF16)
        lane = lax.broadcasted_iota(jnp.int32, (1, _PAIR), 1)
        dq = jnp.zeros((_QB, _PAIR), F32)
        for h in range(2):
            mine = (lane >= HEAD_DIM) if h else (lane < HEAD_DIM)
            qm = jnp.where(mine, qf, 0.0).astype(BF16)
            dom = jnp.where(mine, dov, 0.0)
            dob = dom.astype(BF16)
            lse = lse_ref[:, h:h + 1]
            s_loc = lax.dot_general(qm, kw, _DIMS["nt"], preferred_element_type=F32)
            p_loc = jnp.exp(s_loc + bias_ref[h] - lse)
            p_ctx = jnp.exp(lax.dot_general(qm, kcv, _DIMS["nt"], preferred_element_type=F32) - lse)
            delta = jnp.sum(dom * ov, axis=-1, keepdims=True)
            ds_loc = p_loc * (lax.dot_general(dob, vw, _DIMS["nt"], preferred_element_type=F32) - delta)
            ds_ctx = p_ctx * (lax.dot_general(dob, vcv, _DIMS["nt"], preferred_element_type=F32) - delta)
            dsb_loc = ds_loc.astype(BF16)
            dsb_ctx = ds_ctx.astype(BF16)
            dq_h = (jnp.dot(dsb_loc, kw, preferred_element_type=F32)
                    + jnp.dot(dsb_ctx, kcv, preferred_element_type=F32))
            dq = dq + jnp.where(mine, dq_h, 0.0)
            dk_ref[pl.ds(start, _KW), :] += lax.dot_general(dsb_loc, qm, _DIMS["tn"], preferred_element_type=F32)
            dv_ref[pl.ds(start, _KW), :] += lax.dot_general(p_loc.astype(BF16), dob, _DIMS["tn"],
                                                            preferred_element_type=F32)
            dk_ref[pl.ds(nx, n_ctx), :] += lax.dot_general(dsb_ctx, qm, _DIMS["tn"], preferred_element_type=F32)
            dv_ref[pl.ds(nx, n_ctx), :] += lax.dot_general(p_ctx.astype(BF16), dob, _DIMS["tn"],
                                                           preferred_element_type=F32)
            ds_ref[h] = ds_loc
        d4_ref[0, pl.ds(here, _QB), :] = (dq * HEAD_DIM ** -0.5).astype(BF16)

        def scatter(cls):
            for h in range(2):
                for qr, kr, tile in _bias_pieces(cls):
                    if tile is not None:
                        dtiles_ref[h, tile] += ds_ref[h, qr * GRID_W:(qr + 1) * GRID_W, kr * GRID_W:(kr + 2) * GRID_W]

        _block_class(b, nblk, scatter)

        @pl.when(b == nblk - 1)
        def _():
            d4_ref[1] = dk_ref[...].astype(BF16)
            d4_ref[2] = dv_ref[...].astype(BF16)
            _fold_tiles(dtiles_ref, drp_ref)

        if comm:
            pl.when(jnp.logical_and(hp == nhp - 1, b == nblk - 1))(lambda: comm.finish(cin, cout, sems))

    tiles = pltpu.VMEM((2, _N_TILES, GRID_W, _PAIR), F32)
    block = pltpu.VMEM((2, _QB, _KW), F32)
    res = pl.pallas_call(
        body, grid=(nhp, nblk),
        in_specs=[qspec(0), kspec(1), kspec(2), qspec(3), tspec, mspec, ospec, lspec, ospec] + [HBM_SPEC] * n_cin,
        out_specs=[pl.BlockSpec((4, rows, _PAIR), lambda hp, b: (0, 0, hp)), tspec] + [HBM_SPEC] * n_cout,
        out_shape=[_sds((4, rows, w), BF16), _sds(rp.shape, F32)] + (comm.outs if comm else []),
        scratch_shapes=[block, tiles, block, tiles, pltpu.VMEM((rows, _PAIR), F32), pltpu.VMEM((rows, _PAIR), F32)]
        + (comm.sems if comm else []),
        name=name, compiler_params=_cparams("arbitrary", "arbitrary"),
    )(p4, p4, p4, p4, rp, mask, o, lse, da, *(comm.ins if comm else []))
    return res[:2], res[2:]


def _final(x, g, target, name):
    rows, d = x.shape
    tr = ROW_BLOCK
    nblk = rows // tr

    def body(x_ref, g_ref, t_ref, loss_ref, dx_ref, dg_ref, acc_ref):
        i = pl.program_id(0)
        xv = x_ref[...]
        gv = g_ref[...]
        r = lax.rsqrt(jnp.mean(xv * xv, axis=-1, keepdims=True) + EPS)
        xn = xv * r
        err = xn * gv - t_ref[...]
        dy = err * (1.0 / d)
        dxn = dy * gv
        dx_ref[...] = r * (dxn - xn * jnp.mean(dxn * xn, axis=-1, keepdims=True))
        s_g = jnp.sum(dy * xn, axis=0, keepdims=True)
        s_l = jnp.sum(jnp.mean(err * err, axis=-1, keepdims=True), axis=0, keepdims=True)

        @pl.when(i == 0)
        def _():
            dg_ref[...] = s_g
            acc_ref[...] = s_l

        @pl.when(i > 0)
        def _():
            dg_ref[...] += s_g
            acc_ref[...] += s_l

        @pl.when(i == nblk - 1)
        def _():
            loss_ref[...] = jnp.broadcast_to(0.5 * acc_ref[...], loss_ref.shape)

    row = pl.BlockSpec((tr, d), lambda i: (i, 0))
    vec = pl.BlockSpec((1, d), lambda i: (0, 0))
    return pl.pallas_call(
        body, grid=(nblk,), in_specs=[row, vec, row],
        out_specs=[pl.BlockSpec((1, 128), lambda i: (0, 0)), row, vec],
        out_shape=[_sds((1, 128), F32), _sds((rows, d), F32), _sds((1, d), F32)],
        scratch_shapes=[pltpu.VMEM((1, 1), F32)], name=name, compiler_params=_cparams("arbitrary"),
    )(x, g, target)


def _as2d(a):
    if a.ndim == 1:
        return a.reshape(-1, 128) if a.shape[0] % 128 == 0 else a.reshape(1, -1)
    return a.reshape(-1, a.shape[-1])


def _adamw(w, g, m, v, name, comm=None):
    shape = w.shape
    w2, g2, m2, v2 = (_as2d(t) for t in (w, g.reshape(shape), m, v))
    rows, cols = w2.shape
    tr = 512 if rows % 512 == 0 else rows
    c1 = 1.0 - ADAM_B1 ** ADAM_STEP
    c2 = 1.0 - ADAM_B2 ** ADAM_STEP

    def body(w_ref, g_ref, m_ref, v_ref, d_ref, nm_ref, nv_ref):
        gv = g_ref[...]
        nm = ADAM_B1 * m_ref[...] + (1.0 - ADAM_B1) * gv
        nv = ADAM_B2 * v_ref[...] + (1.0 - ADAM_B2) * (gv * gv)
        nm_ref[...] = nm
        nv_ref[...] = nv
        d_ref[...] = -ADAM_LR * ((nm / c1) / (jnp.sqrt(nv / c2) + ADAM_EPS) + ADAM_WD * w_ref[...])

    blk = pl.BlockSpec((tr, cols), lambda i: (i, 0))
    outs, carried = _call(body, (w2, g2, m2, v2), grid=(rows // tr,), in_specs=[blk] * 4, out_specs=[blk] * 3,
                          out_shape=[_sds((rows, cols), F32)] * 3, name=name, comm=comm)
    outs = tuple(t.reshape(shape) for t in outs)
    return outs if comm is None else (outs, carried)


def _sum_lead(x, name, out_dtype=F32):
    n, rows, cols = x.shape
    tr = 512 if rows % 512 == 0 else rows

    def body(x_ref, o_ref):
        acc = x_ref[0].astype(F32)
        for k in range(1, n):
            acc = acc + x_ref[k].astype(F32)
        o_ref[...] = acc.astype(out_dtype)

    return pl.pallas_call(
        body, grid=(rows // tr,), in_specs=[pl.BlockSpec((n, tr, cols), lambda i: (0, i, 0))],
        out_specs=pl.BlockSpec((tr, cols), lambda i: (i, 0)), out_shape=_sds((rows, cols), out_dtype),
        name=name, compiler_params=_cparams("parallel"),
    )(x)


_NO_CTX = 1 << 30


def _seg_vecs(mod_l, which, nseg):
    return mod_l[:nseg, which][:, None, :]


def _norm_grads(dshift, dgeff, dgate, g, scale):
    nseg, _, d = dshift.shape
    dmod = jnp.stack([dshift[:, 0], dgeff[:, 0] * g, dgate[:, 0]], axis=1)
    if nseg == 1:
        dmod = jnp.concatenate([dmod, jnp.zeros((1, 3, d), F32)], axis=0)
    dg = jnp.sum(dgeff[:, 0] * (1.0 + scale[:, 0]), axis=0)
    return dmod, dg


def _pool_layer(xin, g, mod_l, w_in, w_grp, w_out, pscale, nx, tag, comms=None):
    rows = xin.shape[0]
    nseg = 2 if rows > nx else 1
    comms = comms or {}
    shift, scale, gate = (_seg_vecs(mod_l, k, nseg) for k in range(3))
    (h, r, uv), c_in = _norm_w_in(xin, g, scale, shift, w_in, nx, f"w_in_fwd_{tag}", comms.get("w_in_fwd"))
    (z, mixed, a), c_pool = _pool_grp_fwd(uv, w_grp, pscale, nx, f"pool_fwd_{tag}", comms.get("pool_fwd"))
    yx, xout = _w_out_resid(a, w_out, xin, gate, nx, f"w_out_fwd_{tag}")

    def backward(dxo, comms=None, token=None):
        comms = comms or {}
        gate_b = gate if token is None else gate + token[0, 0]
        (dyx, da, dgate), c_out = _gate_w_out_bwd(dxo, yx, gate_b, w_out, nx, f"w_out_bwd_{tag}",
                                                  comms.get("w_out_bwd"))
        gw_out = _mm_tn(a, dyx, f"w_out_grad_{tag}", BF16)
        dm, duv, dscale = _pool_grp_bwd(da, mixed, uv, pscale, w_grp, nx, f"pool_bwd_{tag}")
        gw_grp = _grp_wgrad(z, dm, w_grp.shape[0], f"grp_grad_{tag}", BF16)
        gw_in = _mm_tn_parts(h, duv, f"w_in_grad_{tag}", BF16)
        (dx, dshift, dgeff), c_bwd = _w_in_bwd_norm(duv, w_in, xin, r, g, scale, dxo, nx, f"w_in_bwd_{tag}",
                                                    comms.get("w_in_bwd"))
        dmod, dg = _norm_grads(dshift, dgeff, dgate, g[0], scale)
        return (dx, dmod, dg, dict(w_in=gw_in, w_grp=gw_grp, w_out=gw_out, scale=dscale),
                dict(w_out_bwd=c_out, w_in_bwd=c_bwd))

    return xout, backward, dict(w_in_fwd=c_in, pool_fwd=c_pool)


def _na_layer(xc, g, mod_l, w_in, rpb, w_out, nx, mask, comm=None):
    nh, n_dr, n_dc = rpb.shape
    shift, scale = _seg_vecs(mod_l, 0, 2), _seg_vecs(mod_l, 1, 2)
    gate = _seg_vecs(mod_l, 2, 1)
    (h, r, p4), _ = _norm_w_in(xc, g, scale, shift, w_in, nx, "w_in_fwd_na")
    rp = jnp.pad(rpb, ((0, 0), (1, _RP_ROWS - 1 - n_dr), (0, _PAIR - n_dc)))
    (a, o, lse), carried = _attn_fwd(p4, rp, mask, nx, "attn_fwd", comm)
    yx, xout = _w_out_resid(a, w_out, xc, gate, nx, "w_out_fwd_na")

    def backward(dxo, comm=None):
        (dyx, da, dgate), _ = _gate_w_out_bwd(dxo, yx, gate, w_out, nx, "w_out_bwd_na")
        gw_out = _mm_tn(a, dyx, "w_out_grad_na", BF16)
        (d4, drp), carried_bwd = _attn_bwd(p4, rp, mask, o, lse, da, nx, "attn_bwd", comm)
        gw_in = _mm_tn_parts(h, d4, "w_in_grad_na", BF16)
        (dx, dshift, dgeff), _ = _w_in_bwd_norm(d4, w_in, xc, r, g, scale, dxo, nx, "w_in_bwd_na")
        dgate2 = jnp.concatenate([dgate, jnp.zeros_like(dgate)], axis=0)
        dmod, dg = _norm_grads(dshift, dgeff, dgate2, g[0], scale)
        drpb = drp[:, 1:1 + n_dr, ::-1][:, :, :n_dc]
        return dx, dmod, dg, dict(w_in=gw_in, w_out=gw_out, rpb=drpb), carried_bwd

    return xout, backward, carried


def _conv_layer(xin, g, mod_l, w_in, dw, db, w_out):
    shift, scale, gate = (_seg_vecs(mod_l, k, 1) for k in range(3))
    nx = xin.shape[0]
    (h, r, p4), _ = _norm_w_in(xin, g, scale, shift, w_in, nx, "w_in_fwd_conv")
    a = _conv_fwd(p4, dw, db, "conv_fwd")
    yx, xout = _w_out_resid(a, w_out, xin, gate, nx, "w_out_fwd_conv")

    def backward(dxo):
        (dyx, da, dgate), _ = _gate_w_out_bwd(dxo, yx, gate, w_out, nx, "w_out_bwd_conv")
        gw_out = _mm_tn(a, dyx, "w_out_grad_conv", BF16)
        d4, ddw, ddb = _conv_bwd(da, p4, dw, db, "conv_bwd")
        gw_in = _mm_tn_parts(h, d4, "w_in_grad_conv", BF16)
        (dx, dshift, dgeff), _ = _w_in_bwd_norm(d4, w_in, xin, r, g, scale, dxo, nx, "w_in_bwd_conv")
        dmod, dg = _norm_grads(dshift, dgeff, dgate, g[0], scale)
        return dx, dmod, dg, dict(w_in=gw_in, w_out=gw_out, dw=ddw, db=ddb)

    return xout, backward


def _example_step(x, ctx, target, mod, norm_g, final_g, wts, na_weights=None, late_comm=None, late_weights=None,
                  grad_comm=None, na_grads_start=None):
    nx = x.shape[0]
    consts = _attn_mask()
    g_rows = [norm_g[i:i + 1] for i in range(4)]
    xc0 = jnp.concatenate([x, ctx], axis=0)
    xc1, bwd0, _ = _pool_layer(xc0, g_rows[0], mod[0], wts["pool_w_in"][0], wts["pool_w_grp"][0],
                               wts["pool_w_out"][0], wts["pool_scale"][0:1], nx, "p0")
    if na_weights is not None:
        wts = {**wts, **na_weights(xc1)}
    x2, bwd1, carried = _na_layer(xc1, g_rows[1], mod[1], wts["na_w_in"], wts["na_rpb"], wts["na_w_out"], nx, consts,
                                  late_comm)
    if late_weights is not None:
        wts = {**wts, **late_weights(carried)}
    x3, bwd2 = _conv_layer(x2, g_rows[2], mod[2], wts["conv_w_in"], wts["conv_dw"], wts["conv_db"], wts["conv_w_out"])
    x4, bwd3, _ = _pool_layer(x3, g_rows[3], mod[3], wts["pool_w_in"][1], wts["pool_w_grp"][1], wts["pool_w_out"][1],
                              wts["pool_scale"][1:2], nx, "p3")
    loss, dx4, dfinal_g = _final(x4, final_g, target, "loss_head")
    dx3, dmod3, dg3, gr3, _ = bwd3(dx4)
    dx2, dmod2, dg2, gr2 = bwd2(dx3)
    dxc1, dmod1, dg1, gr1, carried_bwd = bwd1(dx2, grad_comm(gr3, gr2) if grad_comm else None)
    dxc0, dmod0, dg0, gr0, _ = bwd0(dxc1, token=na_grads_start(gr1) if na_grads_start else None)
    return dict(
        loss=loss, grad_x=dxc0[:nx], dmod=jnp.stack([dmod0, dmod1, dmod2, dmod3]),
        dnorm_g=jnp.stack([dg0, dg1, dg2, dg3]), dfinal_g=dfinal_g, layers=(gr0, gr1, gr2, gr3), carried=carried_bwd)


_AXES = ("x", "y", "c")
_CHIP_FLIPS = ((1, 0), (0, 1), (1, 1))


def _position():
    return tuple(lax.axis_index(a) for a in _AXES)


def _flipped(pos, flip):
    return tuple(1 - p if f else p for p, f in zip(pos, flip))


def _join_comms(comms):
    n_in = [len(c.ins) for c in comms]
    n_out = [len(c.outs) for c in comms]
    n_sem = [len(c.sems) for c in comms]

    def parts(ins, outs, sems):
        for k in range(len(comms)):
            a, b, s = sum(n_in[:k]), sum(n_out[:k]), sum(n_sem[:k])
            yield comms[k], (ins[a:a + n_in[k]], outs[b:b + n_out[k]], sems[s:s + n_sem[k]])

    def start(ins, outs, sems):
        for c, part in parts(ins, outs, sems):
            c.start(*part)

    def finish(ins, outs, sems):
        for c, part in parts(ins, outs, sems):
            c.finish(*part)

    joint = _Comm([a for c in comms for a in c.ins], [o for c in comms for o in c.outs],
                  [s for c in comms for s in c.sems], start, finish)
    return joint, lambda res: [list(res[sum(n_out[:k]):sum(n_out[:k + 1])]) for k in range(len(comms))]


def _run_comms(comms, name):
    joint, split = _join_comms(comms)

    def body(*refs):
        n_in, n_out = len(joint.ins), len(joint.outs)
        joint.start(refs[:n_in], refs[n_in:n_in + n_out], refs[n_in + n_out:])
        joint.finish(refs[:n_in], refs[n_in:n_in + n_out], refs[n_in + n_out:])

    res = pl.pallas_call(
        body, in_specs=[HBM_SPEC] * len(joint.ins), out_specs=[HBM_SPEC] * len(joint.outs), out_shape=joint.outs,
        scratch_shapes=joint.sems, name=name,
    )(*joint.ins)
    return split(res)


def _all_gather_comm(v, axes):
    flips = [f for f in np.ndindex(2, 2, 2) if any(f) and all(a in axes or not b for a, b in zip(_AXES, f))]
    n = len(flips) + 1

    def copies(ins, outs, sems):
        (v_ref,), (o_ref,), (send_sems, recv_sems, local_sem) = ins, outs, sems
        pos = _position()
        slot = 0
        for a, p in zip(_AXES, pos):
            if a in axes:
                slot = 2 * slot + p
        local = pltpu.make_async_copy(v_ref, o_ref.at[slot], local_sem)
        remote = [pltpu.make_async_remote_copy(v_ref, o_ref.at[slot], send_sems.at[k], recv_sems.at[k],
                                               device_id=_flipped(pos, flip), device_id_type=MESH)
                  for k, flip in enumerate(flips)]
        return [local] + remote

    def start(ins, outs, sems):
        for cp in copies(ins, outs, sems):
            cp.start()

    def finish(ins, outs, sems):
        for cp in copies(ins, outs, sems):
            cp.wait()

    sems = [pltpu.SemaphoreType.DMA((n - 1,)), pltpu.SemaphoreType.DMA((n - 1,)), pltpu.SemaphoreType.DMA(())]
    return _Comm([v], [_sds((n,) + v.shape, v.dtype)], sems, start, finish)


def _all_gather(v, axes, name):
    return _run_comms([_all_gather_comm(v, axes)], name)[0][0]


class _Item:
    def __init__(self, key, layer, shape, shard_axis, half_axis):
        self.key, self.layer, self.shape = key, layer, tuple(shape)
        self.shard_axis, self.half_axis = shard_axis, half_axis
        self.shard = shape[shard_axis] // 4
        self.half = shape[half_axis] // 2

    def sized(self, shard=False, half=False):
        s = list(self.shape)
        if shard:
            s[self.shard_axis] = self.shard
        if half:
            s[self.half_axis] = self.half
        return tuple(s)

    def window(self, ref, chip=None, half=None):
        idx = [slice(None)] * len(self.shape)
        if chip is not None:
            idx[self.shard_axis] = pl.ds(chip * self.shard, self.shard)
        if half is not None:
            idx[self.half_axis] = pl.ds(half * self.half, self.half)
        return ref.at[tuple(idx)]


def _items(d, w):
    out = []
    for j in range(2):
        out += [_Item("pool_w_in", j, (d, 2 * w), 1, 0), _Item("pool_w_grp", j, (4, w // 4, w // 4), 1, 0),
                _Item("pool_w_out", j, (w, d), 0, 1)]
    out += [_Item("na_w_in", 0, (d, 4 * w), 1, 0), _Item("na_w_out", 0, (w, d), 0, 1),
            _Item("conv_w_in", 0, (d, 4 * w), 1, 0), _Item("conv_w_out", 0, (w, d), 0, 1)]
    return out


def _gather_weights(shards, items, name):
    comm = _gather_comm(shards, items)

    def body(*refs):
        n = len(items)
        comm.start(refs[:n], refs[n:2 * n], refs[2 * n:])
        comm.finish(refs[:n], refs[n:2 * n], refs[2 * n:])

    return pl.pallas_call(
        body, in_specs=[HBM_SPEC] * len(items), out_specs=[HBM_SPEC] * len(items), out_shape=comm.outs,
        scratch_shapes=comm.sems, name=name,
    )(*shards)


def _gather_comm(shards, items):
    n = len(items)

    def copies(src, dst, sems, onward):
        send_a, recv_a, send_b, recv_b, send_c, recv_c = sems
        x, y, c = _position()
        chip = 2 * x + y
        sibling = (x, y, 1 - c)
        own, out, fwd, fwd_in = [], [], [], []
        for i, it in enumerate(items):
            own.append(pltpu.make_async_remote_copy(src[i], it.window(dst[i], chip=chip), send_c.at[i], recv_c.at[i],
                                                    device_id=sibling, device_id_type=MESH))
            for k, flip in enumerate(_CHIP_FLIPS):
                px, py = _flipped((x, y), flip)
                s = 3 * i + k
                out.append(pltpu.make_async_remote_copy(
                    it.window(src[i], half=c), it.window(dst[i], chip=chip, half=c), send_a.at[s], recv_a.at[s],
                    device_id=(px, py, c), device_id_type=MESH))
                if onward:
                    got = it.window(dst[i], chip=2 * px + py, half=c)
                    fwd.append(pltpu.make_async_remote_copy(got, got, send_b.at[s], recv_b.at[s],
                                                            device_id=sibling, device_id_type=MESH))
                    other = it.window(dst[i], chip=2 * px + py, half=1 - c)
                    fwd_in.append(pltpu.make_async_remote_copy(other, other, send_b.at[s], recv_b.at[s],
                                                               device_id=sibling, device_id_type=MESH))
        return own, out, fwd, fwd_in

    def start(src, dst, sems):
        own, out, _, _ = copies(src, dst, sems, False)
        for cp in own + out:
            cp.start()

    def finish(src, dst, sems):
        own, out, fwd, fwd_in = copies(src, dst, sems, True)
        for arrived, onward in zip(out, fwd):
            arrived.wait_recv()
            onward.start()
        for cp in fwd_in:
            cp.wait_recv()
        for cp in out + fwd:
            cp.wait_send()
        for cp in own:
            cp.wait()

    sems = [pltpu.SemaphoreType.DMA((3 * n,)) for _ in range(4)] + [pltpu.SemaphoreType.DMA((n,)) for _ in range(2)]
    return _Comm(shards, [_sds(it.shape, BF16) for it in items], sems, start, finish)


def _pair_swap_comm(arrays, windows, out_shapes):
    n = len(arrays)

    def copies(src, got, sems):
        send_sems, recv_sems = sems
        x, y, c = _position()
        return [pltpu.make_async_remote_copy(windows[i](src[i], 1 - c), got[i], send_sems.at[i], recv_sems.at[i],
                                             device_id=(x, y, 1 - c), device_id_type=MESH) for i in range(n)]

    def start(src, got, sems):
        for cp in copies(src, got, sems):
            cp.start()

    def finish(src, got, sems):
        for cp in copies(src, got, sems):
            cp.wait()

    return _Comm(arrays, out_shapes, [pltpu.SemaphoreType.DMA((n,)), pltpu.SemaphoreType.DMA((n,))], start, finish)


def _pair_swap(arrays, windows, out_shapes, name):
    return _run_comms([_pair_swap_comm(arrays, windows, out_shapes)], name)[0]


def _chip_exchange(partials, items, name):
    comm = _chip_exchange_comm(partials, items)

    def body(*refs):
        n = len(items)
        comm.start(refs[:n], refs[n:2 * n], refs[2 * n:])
        comm.finish(refs[:n], refs[n:2 * n], refs[2 * n:])

    return pl.pallas_call(
        body, in_specs=[HBM_SPEC] * len(items), out_specs=[HBM_SPEC] * len(items), out_shape=comm.outs,
        scratch_shapes=comm.sems, name=name,
    )(*partials)


def _chip_exchange_copies(items):
    def copies(src, dst, sems):
        send_sems, recv_sems = sems
        x, y, c = _position()
        out = []
        for i, it in enumerate(items):
            for k, flip in enumerate(_CHIP_FLIPS):
                px, py = _flipped((x, y), flip)
                out.append(pltpu.make_async_remote_copy(
                    it.window(src[i], chip=2 * px + py), dst[i].at[k], send_sems.at[3 * i + k],
                    recv_sems.at[3 * i + k], device_id=(px, py, c), device_id_type=MESH))
        return out

    return copies


_SEM_SPEC = pl.BlockSpec(memory_space=pltpu.SEMAPHORE)
_DATAFLOW = pltpu.SideEffectType.DATAFLOW_SIDE_EFFECTING


def _split_start(copies, srcs, zones, n_copies, name):
    n, nz = len(srcs), len(zones)

    def body(*refs):
        src, land = refs[:n], refs[n:n + nz]
        send_sems, recv_sems = refs[n + nz:n + nz + 2]
        token = refs[-1]
        for cp in copies(src, land, (send_sems, recv_sems)):
            cp.start()
        token[...] = jnp.zeros(token.shape, F32)

    hbm = lambda t: pltpu.HBM(t.shape, t.dtype)
    res = pl.pallas_call(
        body, name=name,
        out_shape=(pltpu.SemaphoreType.DMA((n_copies,)), pltpu.SemaphoreType.DMA((n_copies,)),
                   *[hbm(t) for t in list(srcs) + list(zones)], _sds((8, 128), F32)),
        in_specs=[HBM_SPEC] * (n + nz),
        out_specs=(_SEM_SPEC, _SEM_SPEC, *[HBM_SPEC] * (n + nz), pl.BlockSpec(memory_space=pltpu.VMEM)),
        input_output_aliases={i: 2 + i for i in range(n + nz)},
        compiler_params=pltpu.CompilerParams(has_side_effects=_DATAFLOW),
    )(*[pltpu.with_memory_space_constraint(t, pltpu.HBM) for t in list(srcs) + list(zones)])
    return (res[0], res[1], list(res[2:2 + n]), list(res[2 + n:2 + n + nz])), res[-1]


def _split_wait(copies, handle, after, name):
    send_sems, recv_sems, srcs, zones = handle
    n, nz = len(srcs), len(zones)

    def body(*refs):
        src, land = refs[:n], refs[n:n + nz]
        send, recv = refs[n + nz:n + nz + 2]
        for cp in copies(src, land, (send, recv)):
            cp.wait_send()
            cp.wait_recv()

    hbm = lambda t: pltpu.HBM(t.shape, t.dtype)
    res = pl.pallas_call(
        body, name=name, out_shape=tuple(hbm(t) for t in list(srcs) + list(zones)),
        in_specs=[HBM_SPEC] * (n + nz) + [_SEM_SPEC, _SEM_SPEC, pl.BlockSpec(memory_space=pl.ANY)],
        out_specs=tuple([HBM_SPEC] * (n + nz)), input_output_aliases={i: i for i in range(n + nz)},
        compiler_params=pltpu.CompilerParams(has_side_effects=_DATAFLOW),
    )(*srcs, *zones, send_sems, recv_sems, after)
    return list(res[:n]), list(res[n:])


def _gather_ici_copies(items):
    def copies(src, dst, sems):
        send_sems, recv_sems = sems
        x, y, c = _position()
        chip = 2 * x + y
        out = []
        for i, it in enumerate(items):
            for k, flip in enumerate(_CHIP_FLIPS):
                px, py = _flipped((x, y), flip)
                out.append(pltpu.make_async_remote_copy(
                    it.window(src[i], half=c), it.window(dst[i], chip=chip, half=c), send_sems.at[3 * i + k],
                    recv_sems.at[3 * i + k], device_id=(px, py, c), device_id_type=MESH))
        return out

    return copies


def _gather_pair_finish(shards, mats, items, name):
    n = len(items)

    def body(*refs):
        src, dst = refs[:n], refs[2 * n:3 * n]
        send_own, recv_own, send_fwd, recv_fwd = refs[3 * n:]
        x, y, c = _position()
        chip = 2 * x + y
        sibling = (x, y, 1 - c)
        copies = []
        for i, it in enumerate(items):
            copies.append(pltpu.make_async_remote_copy(src[i], it.window(dst[i], chip=chip), send_own.at[i],
                                                       recv_own.at[i], device_id=sibling, device_id_type=MESH))
            for k, flip in enumerate(_CHIP_FLIPS):
                px, py = _flipped((x, y), flip)
                got = it.window(dst[i], chip=2 * px + py, half=c)
                copies.append(pltpu.make_async_remote_copy(got, got, send_fwd.at[3 * i + k], recv_fwd.at[3 * i + k],
                                                           device_id=sibling, device_id_type=MESH))
        for cp in copies:
            cp.start()
        for cp in copies:
            cp.wait()

    return pl.pallas_call(
        body, in_specs=[HBM_SPEC] * (2 * n), out_specs=[HBM_SPEC] * n, out_shape=[_sds(it.shape, BF16) for it in items],
        input_output_aliases={n + i: i for i in range(n)},
        scratch_shapes=[pltpu.SemaphoreType.DMA((n,)), pltpu.SemaphoreType.DMA((n,)),
                        pltpu.SemaphoreType.DMA((3 * n,)), pltpu.SemaphoreType.DMA((3 * n,))], name=name,
    )(*shards, *mats)


def _chip_exchange_comm(partials, items):
    n = len(items)
    copies = _chip_exchange_copies(items)

    def start(src, dst, sems):
        for cp in copies(src, dst, sems):
            cp.start()

    def finish(src, dst, sems):
        for cp in copies(src, dst, sems):
            cp.wait()

    return _Comm(partials, [_sds((3,) + it.sized(shard=True, half=True), BF16) for it in items],
                 [pltpu.SemaphoreType.DMA((3 * n,)), pltpu.SemaphoreType.DMA((3 * n,))], start, finish)


_SUM_STEPS = 2


def _pair_sums(gs, gots, its, pos, name):
    n = len(its)
    nb = _SUM_STEPS
    g2 = [g.reshape(-1, g.shape[-1]) for g in gs]
    got2 = [t.reshape(-1, t.shape[-1]) for t in gots]

    def body(pos_ref, *refs):
        for g_ref, got_ref, o_ref in zip(refs[:n], refs[n:2 * n], refs[2 * n:]):
            o_ref[...] = (g_ref[...].astype(F32) + got_ref[...].astype(F32)).astype(BF16)

    g_specs, got_specs = [], []
    for it, t in zip(its, got2):
        rows, cols = t.shape
        blk = (rows // nb, cols)
        g_map = (lambda i, pos: (pos[1] * nb + i, 0)) if it.half_axis == 0 else (lambda i, pos: (i, pos[1]))
        g_specs.append(pl.BlockSpec(blk, g_map))
        got_specs.append(pl.BlockSpec(blk, lambda i, pos: (i, 0)))
    outs = pl.pallas_call(
        body, grid_spec=pltpu.PrefetchScalarGridSpec(
            num_scalar_prefetch=1, grid=(nb,), in_specs=g_specs + got_specs, out_specs=got_specs),
        out_shape=[_sds(t.shape, BF16) for t in got2], name=name, compiler_params=_cparams("parallel"),
    )(pos, *g2, *got2)
    return [o.reshape(t.shape) for o, t in zip(outs, gots)]


_FLIP_SLOT = {2: 0, 1: 1, 3: 2}


def _chip_sums(pairs, slots, its, pos, name):
    n = len(its)
    nb = _SUM_STEPS

    def body(pos_ref, *refs):
        chip = pos_ref[0]
        for own in range(4):
            @pl.when(chip == own)
            def _():
                for p_ref, s_ref, o_ref in zip(refs[:n], refs[n:2 * n], refs[2 * n:]):
                    acc = None
                    for k in range(4):
                        v = (p_ref[...] if k == own else s_ref[_FLIP_SLOT[own ^ k]]).astype(F32)
                        acc = v if acc is None else acc + v
                    o_ref[...] = acc

    p_specs, s_specs, o_specs, shapes = [], [], [], []
    for it in its:
        shape = it.sized(shard=True, half=True)
        blk = (shape[0] // nb,) + shape[1:]
        rest = (0,) * (len(shape) - 1)

        def p_map(i, pos, it=it, nd=len(shape)):
            lead = i + (pos[0] * nb if it.shard_axis == 0 else 0)
            return (lead,) + tuple(pos[0] if ax == it.shard_axis else 0 for ax in range(1, nd))

        p_specs.append(pl.BlockSpec(blk, p_map))
        s_specs.append(pl.BlockSpec((3,) + blk, lambda i, pos, rest=rest: (0, i) + rest))
        o_specs.append(pl.BlockSpec(blk, lambda i, pos, rest=rest: (i,) + rest))
        shapes.append(_sds(shape, F32))
    return pl.pallas_call(
        body, grid_spec=pltpu.PrefetchScalarGridSpec(
            num_scalar_prefetch=1, grid=(nb,), in_specs=p_specs + s_specs, out_specs=o_specs),
        out_shape=shapes, name=name, compiler_params=_cparams("parallel"),
    )(pos, *pairs, *slots)


_GRAD_KEYS = ("pool_w_in", "pool_w_grp", "pool_w_out", "na_w_in", "na_w_out", "conv_w_in", "conv_w_out")


def _adamw_matrix(w, m, v, owns, others, it, pos, name):
    nl = w.shape[0]
    rows_split = it.half_axis == 0
    r, cdim = int(np.prod(w.shape[1:-1])), w.shape[-1]
    hr, hc = (r // 2, cdim) if rows_split else (r, cdim // 2)
    br = min(hr, 256)
    nb = hr // br
    c1 = 1.0 - ADAM_B1 ** ADAM_STEP
    c2 = 1.0 - ADAM_B2 ** ADAM_STEP

    def body(pos_ref, w_ref, m_ref, v_ref, *rest):
        own_refs, other_refs = rest[:nl], rest[nl:2 * nl]
        g_ref, d_ref, nm_ref, nv_ref = rest[2 * nl:]
        j, h = pl.program_id(0), pl.program_id(1)
        own, other = own_refs[0][...], other_refs[0][...]
        for q in range(1, nl):
            own = jnp.where(j == q, own_refs[q][...], own)
            other = jnp.where(j == q, other_refs[q][...], other)
        gv = jnp.where(h == pos_ref[1], own, other)
        nm = ADAM_B1 * m_ref[...] + (1.0 - ADAM_B1) * gv
        nv = ADAM_B2 * v_ref[...] + (1.0 - ADAM_B2) * (gv * gv)
        g_ref[...] = gv
        nm_ref[...] = nm
        nv_ref[...] = nv
        d_ref[...] = -ADAM_LR * ((nm / c1) / (jnp.sqrt(nv / c2) + ADAM_EPS) + ADAM_WD * w_ref[...])

    if rows_split:
        full = pl.BlockSpec((None, br, hc), lambda j, h, i, pos: (j, h * nb + i, 0))
    else:
        full = pl.BlockSpec((None, br, hc), lambda j, h, i, pos: (j, i, h))
    half = pl.BlockSpec((br, hc), lambda j, h, i, pos: (i, 0))
    flat = lambda t: t.reshape(nl, r, cdim)
    outs = pl.pallas_call(
        body, grid_spec=pltpu.PrefetchScalarGridSpec(
            num_scalar_prefetch=1, grid=(nl, 2, nb), in_specs=[full] * 3 + [half] * (2 * nl), out_specs=[full] * 4),
        out_shape=[_sds((nl, r, cdim), F32)] * 4, name=name,
        compiler_params=_cparams("parallel", "parallel", "parallel"),
    )(pos, flat(w), flat(m), flat(v), *[t.reshape(hr, hc) for t in list(owns) + list(others)])
    return tuple(t.reshape(w.shape) for t in outs)


_WEIGHTS = ("c_ctx", "norm_g", "ada_w", "ada_b", "pool_w_in", "pool_w_grp", "pool_scale", "pool_w_out", "na_w_in",
            "na_rpb", "na_w_out", "conv_w_in", "conv_dw", "conv_db", "conv_w_out", "final_g")
_COND_ROWS = 16


def _modulations(cond, ada_w, ada_b_cols):
    nl, d, n = ada_w.shape
    return _matmul(
        cond, ada_w, mode="nn", grid=(nl, 1), a_silu=True, epilogue="bias",
        a_spec=pl.BlockSpec((_COND_ROWS, d), lambda i, j: (0, 0)), b_spec=pl.BlockSpec((None, d, n), lambda i, j: (i, 0, 0)),
        extra=(ada_b_cols,), extra_specs=(pl.BlockSpec((None, 1, n), lambda i, j: (i, 0, 0)),),
        out_shapes=[_sds((nl, _COND_ROWS, n), F32)], out_specs=[pl.BlockSpec((None, _COND_ROWS, n), lambda i, j: (i, 0, 0))],
        name="modulations")[0]


def _ada_w_grad(cond, dm_cols):
    d = cond.shape[1]
    nl, _, n = dm_cols.shape
    return _matmul(
        cond, dm_cols, mode="tn", grid=(nl, 1), a_silu=True,
        a_spec=pl.BlockSpec((_COND_ROWS, d), lambda i, j: (0, 0)), b_spec=pl.BlockSpec((None, _COND_ROWS, n), lambda i, j: (i, 0, 0)),
        out_shapes=[_sds((nl, d, n), F32)], out_specs=[pl.BlockSpec((None, d, n), lambda i, j: (i, 0, 0))],
        name="ada_w_grad")[0]


def _cond_grad(dm_cols, ada_w):
    nl, d, n = ada_w.shape
    return _matmul(
        dm_cols, ada_w, mode="nt", grid=(1, nl), nk=nl, acc_shape=(_COND_ROWS, d),
        a_spec=pl.BlockSpec((None, _COND_ROWS, n), lambda i, q: (q, 0, 0)), b_spec=pl.BlockSpec((None, d, n), lambda i, q: (q, 0, 0)),
        out_shapes=[_sds((_COND_ROWS, d), F32)], out_specs=[pl.BlockSpec((_COND_ROWS, d), lambda i, q: (0, 0))],
        name="cond_grad")[0]


def _pack(parts):
    flat = [p.reshape(-1) for p in parts]
    sizes = [f.shape[0] for f in flat]
    total = sum(sizes)
    rows = -(-total // 1024) * 8
    packed = jnp.concatenate(flat + [jnp.zeros((rows * 128 - total,), F32)]).reshape(rows, 128)
    offs = np.concatenate([[0], np.cumsum(sizes)])[:-1]
    return packed, [(int(o), p.shape) for o, p in zip(offs, parts)]


def _unpack(flat, layout, k):
    off, shape = layout[k]
    return flat[..., off:off + int(np.prod(shape))].reshape(flat.shape[:-1] + tuple(shape))


def kernel(x, c, ctx, c_ctx, norm_g, ada_w, ada_b, pool_w_in, pool_w_grp, pool_scale, pool_w_out, na_w_in, na_rpb, na_w_out, conv_w_in, conv_dw, conv_db, conv_w_out, final_g, loss_target, m_c_ctx, m_norm_g, m_ada_w, m_ada_b, m_pool_w_in, m_pool_w_grp, m_pool_scale, m_pool_w_out, m_na_w_in, m_na_rpb, m_na_w_out, m_conv_w_in, m_conv_dw, m_conv_db, m_conv_w_out, m_final_g, v_c_ctx, v_norm_g, v_ada_w, v_ada_b, v_pool_w_in, v_pool_w_grp, v_pool_scale, v_pool_w_out, v_na_w_in, v_na_rpb, v_na_w_out, v_conv_w_in, v_conv_dw, v_conv_db, v_conv_w_out, v_final_g):
    params = dict(c_ctx=c_ctx, norm_g=norm_g, ada_w=ada_w, ada_b=ada_b, pool_w_in=pool_w_in, pool_w_grp=pool_w_grp,
                  pool_scale=pool_scale, pool_w_out=pool_w_out, na_w_in=na_w_in, na_rpb=na_rpb, na_w_out=na_w_out,
                  conv_w_in=conv_w_in, conv_dw=conv_dw, conv_db=conv_db, conv_w_out=conv_w_out, final_g=final_g)
    mom1 = dict(c_ctx=m_c_ctx, norm_g=m_norm_g, ada_w=m_ada_w, ada_b=m_ada_b, pool_w_in=m_pool_w_in,
                pool_w_grp=m_pool_w_grp, pool_scale=m_pool_scale, pool_w_out=m_pool_w_out, na_w_in=m_na_w_in,
                na_rpb=m_na_rpb, na_w_out=m_na_w_out, conv_w_in=m_conv_w_in, conv_dw=m_conv_dw, conv_db=m_conv_db,
                conv_w_out=m_conv_w_out, final_g=m_final_g)
    mom2 = dict(c_ctx=v_c_ctx, norm_g=v_norm_g, ada_w=v_ada_w, ada_b=v_ada_b, pool_w_in=v_pool_w_in,
                pool_w_grp=v_pool_w_grp, pool_scale=v_pool_scale, pool_w_out=v_pool_w_out, na_w_in=v_na_w_in,
                na_rpb=v_na_rpb, na_w_out=v_na_w_out, conv_w_in=v_conv_w_in, conv_dw=v_conv_dw, conv_db=v_conv_db,
                conv_w_out=v_conv_w_out, final_g=v_final_g)
    d = x.shape[-1]
    w = na_w_out.shape[1] * 4
    xi, yi, ci = _position()
    chip = 2 * xi + yi
    dev = 2 * chip + ci
    n_ada = ada_w.shape[-1]

    def chip_cols(a, size):
        return lax.dynamic_slice_in_dim(a, chip * size, size, axis=a.ndim - 1)

    items = _items(d, w)
    first = [it for it in items if it.key.startswith("pool") and it.layer == 0]
    na = [it for it in items if it.key.startswith("na")]
    late = [it for it in items if it not in first + na]
    shards_of = lambda its: [params[it.key][it.layer].astype(BF16) for it in its]
    empties = lambda its: [lax.empty(it.shape, BF16) for it in its]
    first_copies, na_copies = _gather_ici_copies(first), _gather_ici_copies(na)

    conds = _all_gather(c.reshape(8, d // 8), _AXES, "gather_cond").reshape(8, d)
    behind = conds[0, 0] * 0.0
    first_handle, token = _split_start(first_copies, [s + behind.astype(BF16) for s in shards_of(first)],
                                       empties(first), 3 * len(first), "gather_first_start")
    cond = jnp.concatenate([conds + token[0, 0], c_ctx[None], jnp.zeros((_COND_ROWS - 9, d), F32)], axis=0)
    mod_cols = _modulations(cond, ada_w, chip_cols(ada_b, n_ada)[:, None, :])
    small_pack, small_layout = _pack([pool_scale, conv_dw, conv_db])
    (mod_all,), (small,) = _run_comms([_all_gather_comm(mod_cols, ("x", "y")),
                                       _all_gather_comm(small_pack, ("x", "y"))], "gather_mod")
    behind = mod_all[0, 0, 0, 0] * 0.0
    na_handle, token = _split_start(na_copies, [s + behind.astype(BF16) for s in shards_of(na)], empties(na),
                                    3 * len(na), "gather_na_start")
    first_shards, first_mats = _split_wait(first_copies, first_handle, token, "gather_first_wait")
    first_mats = _gather_pair_finish(first_shards, first_mats, first, "gather_first_pair")
    mod_all = mod_all.transpose(1, 2, 0, 3).reshape(4, _COND_ROWS, 3, d)
    mod = jnp.stack([lax.dynamic_index_in_dim(mod_all, dev, axis=1, keepdims=False), mod_all[:, 8]], axis=1)
    full = {(it.key, it.layer): mat for it, mat in zip(first, first_mats)}
    late_comm = _gather_comm(shards_of(late), late)

    def na_weights(after):
        na_shards, na_mats = _split_wait(na_copies, na_handle, after, "gather_na_wait")
        na_mats = _gather_pair_finish(na_shards, na_mats, na, "gather_na_pair")
        return {it.key: mat for it, mat in zip(na, na_mats)}

    def late_weights(mats):
        full.update({(it.key, it.layer): mat for it, mat in zip(late, mats)})
        return dict(pool_w_in=[full[("pool_w_in", j)] for j in range(2)],
                    pool_w_grp=[full[("pool_w_grp", j)] for j in range(2)],
                    pool_w_out=[full[("pool_w_out", j)] for j in range(2)],
                    conv_w_in=full[("conv_w_in", 0)], conv_w_out=full[("conv_w_out", 0)])

    small = small.reshape(4, -1)

    def whole(k):
        parts = _unpack(small, small_layout, k)
        return jnp.moveaxis(parts, 0, -2).reshape(parts.shape[1:-1] + (-1,))

    wts = dict(pool_w_in=[full[("pool_w_in", 0)]], pool_w_grp=[full[("pool_w_grp", 0)]],
               pool_w_out=[full[("pool_w_out", 0)]], pool_scale=whole(0), na_rpb=na_rpb[0], conv_dw=whole(1)[0],
               conv_db=whole(2))
    pos = jnp.stack([chip, ci]).astype(jnp.int32)

    def layer_grads(its, by_layer):
        pick = {"pool_w_in": "w_in", "pool_w_grp": "w_grp", "pool_w_out": "w_out", "na_w_in": "w_in",
                "na_w_out": "w_out", "conv_w_in": "w_in", "conv_w_out": "w_out"}
        return [by_layer[(it.key.split("_")[0], it.layer)][pick[it.key]] for it in its]

    def pair_sums(its, mats, tag):
        got = _pair_swap(mats, [(lambda ref, half, it=it: it.window(ref, half=half)) for it in its],
                         [_sds(it.sized(half=True), BF16) for it in its], f"pair_exchange_{tag}")
        return _pair_sums(mats, got, its, pos, f"pair_sum_{tag}")

    pairs = dict()

    def grad_comm(gr3, gr2):
        pairs["late"] = pair_sums(late, layer_grads(late, {("pool", 1): gr3, ("conv", 0): gr2}), "late")
        return _chip_exchange_comm(pairs["late"], late)

    slot_zones = lambda its: [lax.empty((3,) + it.sized(shard=True, half=True), BF16) for it in its]
    na_xcopies, first_xcopies = _chip_exchange_copies(na), _chip_exchange_copies(first)
    handles = dict()

    def na_grads_start(gr1):
        pairs["na"] = pair_sums(na, layer_grads(na, {("na", 0): gr1}), "na")
        handles["na"], started = _split_start(na_xcopies, pairs["na"], slot_zones(na), 3 * len(na),
                                              "exchange_na_start")
        return started

    res = _example_step(x[0], ctx[0], loss_target[0], mod, norm_g, final_g[None], wts, na_weights, late_comm,
                        late_weights, grad_comm, na_grads_start)
    g0, g1, g2, g3 = res["layers"]
    pairs["na"], na_slots = _split_wait(na_xcopies, handles["na"], g0["w_in"], "exchange_na_wait")
    first_grads = layer_grads(first, {("pool", 0): g0})
    packed, layout = _pack([res["dfinal_g"], res["dnorm_g"], res["dmod"], g1["rpb"],
                            jnp.concatenate([g0["scale"], g3["scale"]], axis=0), g2["dw"], g2["db"],
                            res["loss"][0, :1]])
    first_got, (every,) = _run_comms(
        [_pair_swap_comm(first_grads, [(lambda ref, half, it=it: it.window(ref, half=half)) for it in first],
                         [_sds(it.sized(half=True), BF16) for it in first]), _all_gather_comm(packed, _AXES)],
        "pair_exchange_first")
    pairs["first"] = _pair_sums(first_grads, first_got, first, pos, "pair_sum_first")

    grads = dict()
    total = _sum_lead(every, "sum_vec_grads").reshape(-1)
    every = every.reshape(8, -1)
    grads["final_g"] = _unpack(total, layout, 0).reshape(final_g.shape)
    grads["norm_g"] = _unpack(total, layout, 1)
    grads["na_rpb"] = _unpack(total, layout, 3)[None]
    grads["pool_scale"] = chip_cols(_unpack(total, layout, 4), pool_scale.shape[-1])
    grads["conv_dw"] = chip_cols(_unpack(total, layout, 5), conv_dw.shape[-1])[None]
    grads["conv_db"] = chip_cols(_unpack(total, layout, 6), conv_db.shape[-1])
    dmod_sum = _unpack(total, layout, 2).reshape(4, 2, 3 * d)
    dmod_each = _unpack(every, layout, 2).reshape(8, 4, 2, 3 * d)
    grads["ada_b"] = dmod_sum[:, 0] + dmod_sum[:, 1]
    dm = jnp.concatenate([dmod_each[:, :, 0].transpose(1, 0, 2), dmod_sum[:, 1][:, None],
                          jnp.zeros((4, _COND_ROWS - 9, 3 * d), F32)], axis=1)
    dm_cols = chip_cols(dm, n_ada)
    dcond = _cond_grad(dm_cols, ada_w)[8].reshape(8, d // 8)
    dcond_all = _all_gather(dcond, ("x", "y"), "gather_cond_grad")
    behind = dcond_all[0, 0, 0] * 0.0
    handles["first"], token = _split_start(first_xcopies, [p + behind.astype(BF16) for p in pairs["first"]],
                                           slot_zones(first), 3 * len(first), "exchange_first_start")
    grads["ada_w"] = _ada_w_grad(cond, dm_cols + token[0, 0])
    grads["c_ctx"] = _sum_lead(dcond_all, "sum_cond_grad").reshape(d) * _dsilu(c_ctx)
    vector_out = {k: _adamw(params[k], grads[k], mom1[k], mom2[k], f"adamw_{k}")
                  for k in _WEIGHTS if k not in _GRAD_KEYS}
    pairs["first"], first_slots = _split_wait(first_xcopies, handles["first"], vector_out["ada_w"][2],
                                              "exchange_first_wait")

    slots = dict(zip(late, res["carried"]))
    slots.update(zip(first, first_slots))
    slots.update(zip(na, na_slots))
    pair_of = dict(zip(late, pairs["late"]))
    pair_of.update(zip(first, pairs["first"]))
    pair_of.update(zip(na, pairs["na"]))
    reduced = _chip_sums([pair_of[it] for it in items], [slots[it] for it in items], items, pos, "chip_sum")
    theirs = _pair_swap(reduced, [lambda ref, half: ref] * len(items),
                        [_sds(t.shape, F32) for t in reduced], "pair_return")
    matrix_out = dict()
    for k in _GRAD_KEYS:
        idx = [i for i, it in enumerate(items) if it.key == k]
        res_k = _adamw_matrix(params[k], mom1[k], mom2[k], [reduced[i] for i in idx], [theirs[i] for i in idx],
                              items[idx[0]], pos, f"adamw_{k}")
        grads[k], matrix_out[k] = res_k[0], res_k[1:]

    outs = [[], [], []]
    for k in _WEIGHTS:
        step = matrix_out[k] if k in matrix_out else vector_out[k]
        for lst, val in zip(outs, step):
            lst.append(val)
    loss = _unpack(total, layout, 7)[0]
    return (loss, res["grad_x"][None], *[grads[k].reshape(params[k].shape) for k in _WEIGHTS],
            *outs[0], *outs[1], *outs[2])
```

```python
import functools

import numpy as np
import jax
import jax.numpy as jnp
from jax import lax
from jax.experimental import pallas as pl
from jax.experimental.pallas import tpu as pltpu

F32 = jnp.float32
BF16 = jnp.bfloat16

EPS = 1e-6
GRID_W = 64
HEAD_DIM = 64
WIN_ROWS = 8
WIN_COLS = 16
POOL_WINDOWS = (2, 4, 8, 16)
Q_ROWS = 4
K_ROWS = 12
PAD_ROWS = 4
NEG = -1e30

ADAM_LR = 0.001
ADAM_B1 = 0.9
ADAM_B2 = 0.999
ADAM_EPS = 1e-08
ADAM_WD = 0.01
ADAM_STEP = 10

ROW_BLOCK = 256
VMEM_LIMIT = 56 * 1024 * 1024
ACT = BF16

MESH = pl.DeviceIdType.MESH
HBM_SPEC = pl.BlockSpec(memory_space=pltpu.HBM)


def _cparams(*sem):
    return pltpu.CompilerParams(dimension_semantics=sem or None, vmem_limit_bytes=VMEM_LIMIT)


def _sds(shape, dtype):
    return jax.ShapeDtypeStruct(tuple(shape), dtype)


def _call(body, args, *, grid, in_specs, out_specs, out_shape, name, scratch_shapes=(), comm=None):
    sem = ("arbitrary",) * len(grid)
    if comm is None:
        res = pl.pallas_call(body, grid=grid, in_specs=list(in_specs), out_specs=list(out_specs), out_shape=list(out_shape),
                             scratch_shapes=list(scratch_shapes), name=name, compiler_params=_cparams(*sem))(*args)
        return list(res), []
    n_in, n_out, n_scr = len(in_specs), len(out_specs), len(scratch_shapes)
    n_cin, n_cout = len(comm.ins), len(comm.outs)

    def carrying(*refs):
        ins, cin = refs[:n_in], refs[n_in:n_in + n_cin]
        outs = refs[n_in + n_cin:n_in + n_cin + n_out]
        cout = refs[n_in + n_cin + n_out:n_in + n_cin + n_out + n_cout]
        rest = refs[n_in + n_cin + n_out + n_cout:]
        scr, sems = rest[:n_scr], rest[n_scr:]
        first, last = True, True
        for ax, size in enumerate(grid):
            first = jnp.logical_and(first, pl.program_id(ax) == 0)
            last = jnp.logical_and(last, pl.program_id(ax) == size - 1)
        pl.when(first)(lambda: comm.start(cin, cout, sems))
        body(*ins, *outs, *scr)
        pl.when(last)(lambda: comm.finish(cin, cout, sems))

    res = pl.pallas_call(
        carrying, grid=grid, in_specs=list(in_specs) + [HBM_SPEC] * n_cin, out_specs=list(out_specs) + [HBM_SPEC] * n_cout,
        out_shape=list(out_shape) + list(comm.outs), scratch_shapes=list(scratch_shapes) + list(comm.sems), name=name,
        compiler_params=_cparams(*sem),
    )(*args, *comm.ins)
    return list(res[:n_out]), list(res[n_out:])


def _sigmoid(x):
    return 1.0 / (1.0 + jnp.exp(-x))


def _silu(x):
    return x * _sigmoid(x)


def _dsilu(x):
    s = _sigmoid(x)
    return s * (1.0 + x * (1.0 - s))


_DIMS = {
    "nn": (((1,), (0,)), ((), ())),
    "nt": (((1,), (1,)), ((), ())),
    "tn": (((0,), (0,)), ((), ())),
}


def _matmul(a, b, *, mode, grid, a_spec, b_spec, out_shapes, out_specs, name, nk=1,
            a_silu=False, exact=False, epilogue=None, extra=(), extra_specs=(), acc_shape=None):
    n_extra = len(extra)
    n_out = len(out_shapes)

    def body(*refs):
        a_ref, b_ref = refs[:2]
        ex = refs[2:2 + n_extra]
        outs = refs[2 + n_extra:2 + n_extra + n_out]
        av = a_ref[...]
        bv = b_ref[...]
        if a_silu:
            av = _silu(av.astype(F32))
        if exact:
            prod = lax.dot_general(av.astype(F32), bv.astype(F32), _DIMS[mode],
                                   precision=lax.Precision.HIGHEST, preferred_element_type=F32)
        else:
            prod = lax.dot_general(av.astype(BF16), bv.astype(BF16), _DIMS[mode], preferred_element_type=F32)

        def finish(res):
            if epilogue is None:
                outs[0][...] = res.astype(outs[0].dtype)
            elif epilogue == "bias":
                outs[0][...] = (res + ex[0][...]).astype(outs[0].dtype)
            else:
                outs[0][...] = res.astype(outs[0].dtype)
                outs[1][...] = ex[0][...] + ex[1][...] * res

        if nk == 1:
            finish(prod)
        else:
            acc = refs[-1]
            k = pl.program_id(len(grid) - 1)

            @pl.when(k == 0)
            def _():
                acc[...] = prod

            @pl.when(k > 0)
            def _():
                acc[...] += prod

            @pl.when(k == nk - 1)
            def _():
                finish(acc[...])

    scratch = [pltpu.VMEM(acc_shape, F32)] if nk > 1 else []
    sem = ("parallel",) * (len(grid) - 1) + ("arbitrary",)
    return pl.pallas_call(
        body, grid=grid, in_specs=[a_spec, b_spec, *extra_specs], out_specs=list(out_specs),
        out_shape=list(out_shapes), scratch_shapes=scratch, name=name, compiler_params=_cparams(*sem),
    )(a, b, *extra)


def _row_tile(rows):
    for t in (768, 512, 256):
        if rows % t == 0:
            return t
    return rows


def _mm_nn(a, b, name, out_dtype=F32, tn=1024):
    m, k = a.shape
    n = b.shape[1]
    tm = _row_tile(m)
    tn = min(tn, n)
    return _matmul(
        a, b, mode="nn", grid=(m // tm, n // tn),
        a_spec=pl.BlockSpec((tm, k), lambda i, j: (i, 0)), b_spec=pl.BlockSpec((k, tn), lambda i, j: (0, j)),
        out_shapes=[_sds((m, n), out_dtype)], out_specs=[pl.BlockSpec((tm, tn), lambda i, j: (i, j))], name=name)[0]


def _mm_out_resid(a, w_out, xres, gate, nxb, name):
    m, k = a.shape
    n = w_out.shape[1]
    tm = ROW_BLOCK
    seg = lambda i, j: (jnp.where(i >= nxb, 1, 0), 0, 0)
    return _matmul(
        a, w_out, mode="nn", grid=(m // tm, 1),
        a_spec=pl.BlockSpec((tm, k), lambda i, j: (i, 0)), b_spec=pl.BlockSpec((k, n), lambda i, j: (0, 0)),
        extra=(xres, gate), extra_specs=(pl.BlockSpec((tm, n), lambda i, j: (i, 0)), pl.BlockSpec((None, 1, n), seg)),
        out_shapes=[_sds((m, n), ACT), _sds((m, n), F32)],
        out_specs=[pl.BlockSpec((tm, n), lambda i, j: (i, 0))] * 2, epilogue="resid", name=name)


def _mm_nt(a, b, name, out_dtype=F32):
    m, n = a.shape
    k = b.shape[0]
    tm = _row_tile(m)
    return _matmul(
        a, b, mode="nt", grid=(m // tm, 1),
        a_spec=pl.BlockSpec((tm, n), lambda i, j: (i, 0)), b_spec=pl.BlockSpec((k, n), lambda i, j: (0, 0)),
        out_shapes=[_sds((m, k), out_dtype)], out_specs=[pl.BlockSpec((tm, k), lambda i, j: (i, 0))], name=name)[0]


def _mm_nt_parts(a, b, name):
    p, m, kp = a.shape
    d = b.shape[0]
    tm = _row_tile(m)
    return _matmul(
        a, b, mode="nt", grid=(m // tm, p), nk=p, acc_shape=(tm, d),
        a_spec=pl.BlockSpec((None, tm, kp), lambda i, q: (q, i, 0)), b_spec=pl.BlockSpec((d, kp), lambda i, q: (0, q)),
        out_shapes=[_sds((m, d), F32)], out_specs=[pl.BlockSpec((tm, d), lambda i, q: (i, 0))], name=name)[0]


def _mm_tn(a, b, name, out_dtype, tm=512):
    r, m = a.shape
    n = b.shape[1]
    tm = min(tm, m)
    tn = min(1024, n)
    return _matmul(
        a, b, mode="tn", grid=(m // tm, n // tn),
        a_spec=pl.BlockSpec((r, tm), lambda i, j: (0, i)), b_spec=pl.BlockSpec((r, tn), lambda i, j: (0, j)),
        out_shapes=[_sds((m, n), out_dtype)], out_specs=[pl.BlockSpec((tm, tn), lambda i, j: (i, j))], name=name)[0]


def _mm_tn_parts(a, b, name, out_dtype, tm=512):
    r, m = a.shape
    p, _, np_ = b.shape
    tm = min(tm, m)
    return _matmul(
        a, b, mode="tn", grid=(m // tm, p),
        a_spec=pl.BlockSpec((r, tm), lambda i, q: (0, i)), b_spec=pl.BlockSpec((None, r, np_), lambda i, q: (q, 0, 0)),
        out_shapes=[_sds((m, p * np_), out_dtype)], out_specs=[pl.BlockSpec((tm, np_), lambda i, q: (i, q))],
        name=name)[0]


def _seg_map(nxb):
    return lambda i: (jnp.where(i >= nxb, 1, 0), 0, 0)


def _normmod_fwd(x, g, scale, shift, nxb, name):
    rows, d = x.shape
    tr = ROW_BLOCK

    def body(x_ref, g_ref, sc_ref, sh_ref, h_ref, r_ref):
        xv = x_ref[...]
        r = lax.rsqrt(jnp.mean(xv * xv, axis=-1, keepdims=True) + EPS)
        h = (xv * r) * g_ref[...] * (1.0 + sc_ref[...]) + sh_ref[...]
        h_ref[...] = h.astype(BF16)
        r_ref[...] = r

    row = pl.BlockSpec((tr, d), lambda i: (i, 0))
    vec = pl.BlockSpec((None, 1, d), _seg_map(nxb))
    return pl.pallas_call(
        body, grid=(rows // tr,), in_specs=[row, pl.BlockSpec((1, d), lambda i: (0, 0)), vec, vec],
        out_specs=[row, pl.BlockSpec((tr, 1), lambda i: (i, 0))],
        out_shape=[_sds((rows, d), BF16), _sds((rows, 1), F32)], name=name, compiler_params=_cparams("parallel"),
    )(x, g, scale, shift)


def _normmod_bwd(dh, x, r, g, scale, dres, nxb, name):
    rows, d = x.shape
    tr = ROW_BLOCK
    nres = dres.shape[0] // tr
    nseg = scale.shape[0]

    def body(dh_ref, x_ref, r_ref, g_ref, sc_ref, dres_ref, dx_ref, dsh_ref, dge_ref):
        i = pl.program_id(0)
        dhv = dh_ref[...]
        rv = r_ref[...]
        xn = x_ref[...] * rv
        dxn = dhv * (g_ref[...] * (1.0 + sc_ref[...]))
        dx = rv * (dxn - xn * jnp.mean(dxn * xn, axis=-1, keepdims=True))

        @pl.when(i < nres)
        def _():
            dx_ref[...] = dx + dres_ref[...]

        @pl.when(i >= nres)
        def _():
            dx_ref[...] = dx

        first = jnp.logical_or(i == 0, i == nxb)
        s_dh = jnp.sum(dhv, axis=0, keepdims=True)
        s_ge = jnp.sum(dhv * xn, axis=0, keepdims=True)

        @pl.when(first)
        def _():
            dsh_ref[...] = s_dh
            dge_ref[...] = s_ge

        @pl.when(jnp.logical_not(first))
        def _():
            dsh_ref[...] += s_dh
            dge_ref[...] += s_ge

    row = pl.BlockSpec((tr, d), lambda i: (i, 0))
    vec = pl.BlockSpec((None, 1, d), _seg_map(nxb))
    return pl.pallas_call(
        body, grid=(rows // tr,),
        in_specs=[row, row, pl.BlockSpec((tr, 1), lambda i: (i, 0)), pl.BlockSpec((1, d), lambda i: (0, 0)), vec,
                  pl.BlockSpec((tr, d), lambda i: (jnp.minimum(i, nres - 1), 0))],
        out_specs=[row, vec, vec],
        out_shape=[_sds((rows, d), F32), _sds((nseg, 1, d), F32), _sds((nseg, 1, d), F32)],
        name=name, compiler_params=_cparams("arbitrary"),
    )(dh, x, r, g, scale, dres)


def _gate_bwd(dxo, yx, gate, nxb, name):
    rows, d = yx.shape
    tr = ROW_BLOCK
    nseg = gate.shape[0]

    def body(dx_ref, yx_ref, gt_ref, dyx_ref, dg_ref):
        i = pl.program_id(0)
        dxv = dx_ref[...]
        dyx_ref[...] = (dxv * gt_ref[...]).astype(BF16)
        s = jnp.sum(dxv * yx_ref[...].astype(F32), axis=0, keepdims=True)
        first = jnp.logical_or(i == 0, i == nxb)

        @pl.when(first)
        def _():
            dg_ref[...] = s

        @pl.when(jnp.logical_not(first))
        def _():
            dg_ref[...] += s

    row = pl.BlockSpec((tr, d), lambda i: (i, 0))
    vec = pl.BlockSpec((None, 1, d), _seg_map(nxb))
    return pl.pallas_call(
        body, grid=(rows // tr,), in_specs=[row, row, vec], out_specs=[row, vec],
        out_shape=[_sds((rows, d), BF16), _sds((nseg, 1, d), F32)], name=name, compiler_params=_cparams("arbitrary"),
    )(dxo, yx, gate)


def _row_vec(ref, is_ctx):
    return ref[0] if is_ctx is None else jnp.where(is_ctx, ref[1], ref[0])


def _ctx_rows(i, tm, nx, nseg):
    if nseg == 1:
        return None
    return i * tm + lax.broadcasted_iota(jnp.int32, (tm, 1), 0) >= nx


def _seg_sums(ref, val, is_ctx, first):
    if is_ctx is None:
        parts = [jnp.sum(val, axis=0, keepdims=True)]
    else:
        parts = [jnp.sum(jnp.where(is_ctx, 0.0, val), axis=0, keepdims=True),
                 jnp.sum(jnp.where(is_ctx, val, 0.0), axis=0, keepdims=True)]

    @pl.when(first)
    def _():
        for k, p in enumerate(parts):
            ref[k] = p

    @pl.when(jnp.logical_not(first))
    def _():
        for k, p in enumerate(parts):
            ref[k] += p


def _w_out_resid(a, w_out, xres, gate, nx, name):
    m, k = a.shape
    n = w_out.shape[1]
    nseg = gate.shape[0]
    tm = _row_tile(m)

    def body(a_ref, w_ref, x_ref, gt_ref, yx_ref, xo_ref):
        yx = jnp.dot(a_ref[...], w_ref[...], preferred_element_type=F32)
        yx_ref[...] = yx.astype(ACT)
        xo_ref[...] = x_ref[...] + _row_vec(gt_ref, _ctx_rows(pl.program_id(0), tm, nx, nseg)) * yx

    row = pl.BlockSpec((tm, n), lambda i: (i, 0))
    return pl.pallas_call(
        body, grid=(m // tm,),
        in_specs=[pl.BlockSpec((tm, k), lambda i: (i, 0)), pl.BlockSpec((k, n), lambda i: (0, 0)), row,
                  pl.BlockSpec((nseg, 1, n), lambda i: (0, 0, 0))],
        out_specs=[row, row], out_shape=[_sds((m, n), ACT), _sds((m, n), F32)],
        name=name, compiler_params=_cparams("parallel"),
    )(a, w_out, xres, gate)


def _norm_w_in(x, g, scale, shift, w_in, nx, name, comm=None):
    rows, d = x.shape
    n = w_in.shape[1]
    nseg = scale.shape[0]
    tm = _row_tile(rows)
    tn = min(1024, n)

    def body(x_ref, g_ref, sc_ref, sh_ref, w_ref, h_ref, r_ref, p_ref):
        i, j = pl.program_id(0), pl.program_id(1)

        @pl.when(j == 0)
        def _():
            xv = x_ref[...]
            r = lax.rsqrt(jnp.mean(xv * xv, axis=-1, keepdims=True) + EPS)
            is_ctx = _ctx_rows(i, tm, nx, nseg)
            h = (xv * r) * g_ref[...] * (1.0 + _row_vec(sc_ref, is_ctx)) + _row_vec(sh_ref, is_ctx)
            h_ref[...] = h.astype(BF16)
            r_ref[...] = r

        p_ref[...] = jnp.dot(h_ref[...], w_ref[...], preferred_element_type=F32).astype(ACT)

    vec = pl.BlockSpec((nseg, 1, d), lambda i, j: (0, 0, 0))
    return _call(
        body, (x, g, scale, shift, w_in), grid=(rows // tm, n // tn),
        in_specs=[pl.BlockSpec((tm, d), lambda i, j: (i, 0)), pl.BlockSpec((1, d), lambda i, j: (0, 0)), vec, vec,
                  pl.BlockSpec((d, tn), lambda i, j: (0, j))],
        out_specs=[pl.BlockSpec((tm, d), lambda i, j: (i, 0)), pl.BlockSpec((tm, 1), lambda i, j: (i, 0)),
                   pl.BlockSpec((tm, tn), lambda i, j: (i, j))],
        out_shape=[_sds((rows, d), BF16), _sds((rows, 1), F32), _sds((rows, n), ACT)], name=name, comm=comm)


def _gate_w_out_bwd(dxo, yx, gate, w_out, nx, name, comm=None):
    rows, d = yx.shape
    w = w_out.shape[0]
    nseg = gate.shape[0]
    tm = _row_tile(rows)

    def body(dx_ref, yx_ref, gt_ref, w_ref, dyx_ref, da_ref, dg_ref):
        i = pl.program_id(0)
        is_ctx = _ctx_rows(i, tm, nx, nseg)
        dxv = dx_ref[...]
        dyx = (dxv * _row_vec(gt_ref, is_ctx)).astype(BF16)
        dyx_ref[...] = dyx
        da_ref[...] = lax.dot_general(dyx, w_ref[...], _DIMS["nt"], preferred_element_type=F32).astype(ACT)
        _seg_sums(dg_ref, dxv * yx_ref[...].astype(F32), is_ctx, i == 0)

    row = pl.BlockSpec((tm, d), lambda i: (i, 0))
    vec = pl.BlockSpec((nseg, 1, d), lambda i: (0, 0, 0))
    return _call(
        body, (dxo, yx, gate, w_out), grid=(rows // tm,),
        in_specs=[row, row, vec, pl.BlockSpec((w, d), lambda i: (0, 0))],
        out_specs=[row, pl.BlockSpec((tm, w), lambda i: (i, 0)), vec],
        out_shape=[_sds((rows, d), BF16), _sds((rows, w), ACT), _sds((nseg, 1, d), F32)], name=name, comm=comm)


def _w_in_bwd_norm(dparts, w_in, x, r, g, scale, dres, nx, name, comm=None):
    np_, rows, kp = dparts.shape
    d = w_in.shape[0]
    nseg = scale.shape[0]
    tm = _row_tile(rows)
    nsub = tm // ROW_BLOCK
    nres_blocks = dres.shape[0] // ROW_BLOCK

    def body(dp_ref, w_ref, x_ref, r_ref, g_ref, sc_ref, *rest):
        dres_refs = rest[:nsub]
        dx_ref, dsh_ref, dge_ref, acc = rest[nsub:]
        i, k = pl.program_id(0), pl.program_id(1)
        prod = lax.dot_general(dp_ref[...], w_ref[...], _DIMS["nt"], preferred_element_type=F32)

        @pl.when(k == 0)
        def _():
            acc[...] = prod

        @pl.when(k > 0)
        def _():
            acc[...] += prod

        @pl.when(k == np_ - 1)
        def _():
            is_ctx = _ctx_rows(i, tm, nx, nseg)
            dhv = acc[...]
            rv = r_ref[...]
            xn = x_ref[...] * rv
            dxn = dhv * (g_ref[...] * (1.0 + _row_vec(sc_ref, is_ctx)))
            dx = rv * (dxn - xn * jnp.mean(dxn * xn, axis=-1, keepdims=True))
            for s in range(nsub):
                piece = slice(s * ROW_BLOCK, (s + 1) * ROW_BLOCK)
                res = dres_refs[s][...]
                if nres_blocks * ROW_BLOCK < rows:
                    res = jnp.where(i * nsub + s < nres_blocks, res, 0.0)
                dx_ref[piece, :] = dx[piece, :] + res
            _seg_sums(dsh_ref, dhv, is_ctx, i == 0)
            _seg_sums(dge_ref, dhv * xn, is_ctx, i == 0)

    row = pl.BlockSpec((tm, d), lambda i, k: (i, 0))
    vec = pl.BlockSpec((nseg, 1, d), lambda i, k: (0, 0, 0))
    return _call(
        body, (dparts, w_in, x, r, g, scale, *([dres] * nsub)), grid=(rows // tm, np_),
        in_specs=[pl.BlockSpec((None, tm, kp), lambda i, k: (k, i, 0)), pl.BlockSpec((d, kp), lambda i, k: (0, k)),
                  row, pl.BlockSpec((tm, 1), lambda i, k: (i, 0)), pl.BlockSpec((1, d), lambda i, k: (0, 0)), vec]
        + [pl.BlockSpec((ROW_BLOCK, d), (lambda i, k, s=s: (jnp.minimum(i * nsub + s, nres_blocks - 1), 0)))
           for s in range(nsub)],
        out_specs=[row, vec, vec],
        out_shape=[_sds((rows, d), F32), _sds((nseg, 1, d), F32), _sds((nseg, 1, d), F32)],
        scratch_shapes=[pltpu.VMEM((tm, d), F32)], name=name, comm=comm)


_PAD_TOP = 16
_PAD_BOT = 32


def _window_sum(buf, xv, lo, n):
    t = xv.shape[0]
    c = xv.shape[1]
    tp = t + _PAD_TOP + _PAD_BOT
    buf[pl.ds(0, _PAD_TOP), :] = jnp.zeros((_PAD_TOP, c), F32)
    buf[pl.ds(_PAD_TOP, t), :] = xv
    buf[pl.ds(_PAD_TOP + t, _PAD_BOT), :] = jnp.zeros((_PAD_BOT, c), F32)
    p = buf[...]
    k = 1
    while k < n:
        p = p + pltpu.roll(p, tp - k, 0)
        k *= 2
    if lo:
        p = pltpu.roll(p, -lo, 0)
    buf[...] = p
    return buf[pl.ds(_PAD_TOP, t), :]


def _window_count(t, half):
    pos = lax.broadcasted_iota(jnp.int32, (t, 1), 0)
    return (jnp.minimum(pos + half, t) - jnp.maximum(pos - half, 0)).astype(F32)


def _segments(rows, nx):
    return [(0, nx)] + ([(nx, rows - nx)] if rows > nx else [])


def _pool_fwd(uv, nx, name):
    rows = uv.shape[0]
    w = uv.shape[1] // 2
    cb = 128
    per_group = w // len(POOL_WINDOWS) // cb
    segs = _segments(rows, nx)

    def body(u_ref, z_ref, *bufs):
        j = pl.program_id(0)
        for gi, win in enumerate(POOL_WINDOWS):
            half = win // 2

            @pl.when(jnp.logical_and(j >= gi * per_group, j < (gi + 1) * per_group))
            def _():
                for (start, length), buf in zip(segs, bufs):
                    uvv = u_ref[pl.ds(start, length), :].astype(F32)
                    s = _window_sum(buf, uvv, -half, win)
                    z_ref[pl.ds(start, length), :] = (s / _window_count(length, half) - uvv).astype(BF16)

    scratch = [pltpu.VMEM((length + _PAD_TOP + _PAD_BOT, cb), F32) for _, length in segs]
    return pl.pallas_call(
        body, grid=(w // cb,), in_specs=[pl.BlockSpec((rows, cb), lambda j: (0, j))],
        out_specs=pl.BlockSpec((rows, cb), lambda j: (0, j)), out_shape=_sds((rows, w), BF16),
        scratch_shapes=scratch, name=name, compiler_params=_cparams("parallel"),
    )(uv)


def _pool_bwd(dz, dgt, nx, name):
    rows, w = dz.shape
    cb = 128
    per_group = w // len(POOL_WINDOWS) // cb
    segs = _segments(rows, nx)

    def body(dz_ref, dgt_ref, o_ref, *bufs):
        j = pl.program_id(0)
        o_ref[1] = dgt_ref[...]
        for gi, win in enumerate(POOL_WINDOWS):
            half = win // 2

            @pl.when(jnp.logical_and(j >= gi * per_group, j < (gi + 1) * per_group))
            def _():
                for (start, length), buf in zip(segs, bufs):
                    dzv = dz_ref[pl.ds(start, length), :].astype(F32)
                    s = _window_sum(buf, dzv / _window_count(length, half), 1 - half, win)
                    o_ref[0, pl.ds(start, length), :] = (s - dzv).astype(BF16)

    scratch = [pltpu.VMEM((length + _PAD_TOP + _PAD_BOT, cb), F32) for _, length in segs]
    col = pl.BlockSpec((rows, cb), lambda j: (0, j))
    return pl.pallas_call(
        body, grid=(w // cb,), in_specs=[col, col], out_specs=pl.BlockSpec((2, rows, cb), lambda j: (0, 0, j)),
        out_shape=_sds((2, rows, w), BF16), scratch_shapes=scratch, name=name, compiler_params=_cparams("parallel"),
    )(dz, dgt)


def _grp_fwd(z, w_grp, uv, scale, name):
    rows, w = z.shape
    ng, gc, _ = w_grp.shape
    tm = _row_tile(rows)

    def body(z_ref, w_ref, gt_ref, sc_ref, mx_ref, a_ref):
        mixed = jnp.dot(z_ref[...], w_ref[...], preferred_element_type=F32)
        mx_ref[...] = mixed.astype(ACT)
        a_ref[...] = (mixed * sc_ref[...] * _silu(gt_ref[...].astype(F32))).astype(BF16)

    blk = pl.BlockSpec((tm, gc), lambda g, i: (i, g))
    return pl.pallas_call(
        body, grid=(ng, rows // tm),
        in_specs=[blk, pl.BlockSpec((None, gc, gc), lambda g, i: (g, 0, 0)),
                  pl.BlockSpec((tm, gc), lambda g, i: (i, ng + g)), pl.BlockSpec((1, gc), lambda g, i: (0, g))],
        out_specs=[blk, blk], out_shape=[_sds((rows, w), ACT), _sds((rows, w), BF16)],
        name=name, compiler_params=_cparams("parallel", "parallel"),
    )(z, w_grp, uv, scale)


def _grp_bwd(da, mixed, uv, scale, w_grp, name):
    rows, w = da.shape
    ng, gc, _ = w_grp.shape
    tm = _row_tile(rows)

    def body(da_ref, mx_ref, gt_ref, sc_ref, w_ref, dm_ref, dz_ref, dgt_ref, dsc_ref):
        i = pl.program_id(1)
        dav = da_ref[...].astype(F32)
        mixed = mx_ref[...].astype(F32)
        gt = gt_ref[...].astype(F32)
        sg = _silu(gt)
        sc = sc_ref[...]
        dm = (dav * sc * sg).astype(BF16)
        dm_ref[...] = dm
        dz_ref[...] = lax.dot_general(dm, w_ref[...], _DIMS["nt"], preferred_element_type=F32).astype(ACT)
        dgt_ref[...] = (dav * mixed * sc * _dsilu(gt)).astype(BF16)
        s = jnp.sum(dav * mixed * sg, axis=0, keepdims=True)

        @pl.when(i == 0)
        def _():
            dsc_ref[...] = s

        @pl.when(i > 0)
        def _():
            dsc_ref[...] += s

    blk = pl.BlockSpec((tm, gc), lambda g, i: (i, g))
    vec = pl.BlockSpec((1, gc), lambda g, i: (0, g))
    return pl.pallas_call(
        body, grid=(ng, rows // tm),
        in_specs=[blk, blk, pl.BlockSpec((tm, gc), lambda g, i: (i, ng + g)), vec,
                  pl.BlockSpec((None, gc, gc), lambda g, i: (g, 0, 0))],
        out_specs=[blk, blk, blk, vec],
        out_shape=[_sds((rows, w), BF16), _sds((rows, w), ACT), _sds((rows, w), BF16), _sds((1, w), F32)],
        name=name, compiler_params=_cparams("parallel", "arbitrary"),
    )(da, mixed, uv, scale, w_grp)


def _pool_scratch(rows, nx, cols):
    return [pltpu.VMEM((length + _PAD_TOP + _PAD_BOT, cols), F32) for _, length in _segments(rows, nx)]


def _per_group(g, fn):
    for gi, win in enumerate(POOL_WINDOWS):
        pl.when(g == gi)(functools.partial(fn, win))


def _pool_grp_fwd(uv, w_grp, scale, nx, name, comm=None):
    rows = uv.shape[0]
    ng, gc, _ = w_grp.shape
    w = ng * gc
    segs = _segments(rows, nx)

    def body(u_ref, gt_ref, w_ref, sc_ref, z_ref, mx_ref, a_ref, *bufs):
        def pool(win):
            half = win // 2
            for (start, length), buf in zip(segs, bufs):
                uvv = u_ref[pl.ds(start, length), :].astype(F32)
                s = _window_sum(buf, uvv, -half, win)
                z_ref[pl.ds(start, length), :] = (s / _window_count(length, half) - uvv).astype(BF16)

        _per_group(pl.program_id(0), pool)
        mixed = jnp.dot(z_ref[...], w_ref[...], preferred_element_type=F32)
        mx_ref[...] = mixed.astype(ACT)
        a_ref[...] = (mixed * sc_ref[...] * _silu(gt_ref[...].astype(F32))).astype(BF16)

    col = pl.BlockSpec((rows, gc), lambda g: (0, g))
    return _call(
        body, (uv, uv, w_grp, scale), grid=(ng,),
        in_specs=[col, pl.BlockSpec((rows, gc), lambda g: (0, ng + g)), pl.BlockSpec((None, gc, gc), lambda g: (g, 0, 0)),
                  pl.BlockSpec((1, gc), lambda g: (0, g))],
        out_specs=[col, col, col], out_shape=[_sds((rows, w), BF16), _sds((rows, w), ACT), _sds((rows, w), BF16)],
        scratch_shapes=_pool_scratch(rows, nx, gc), name=name, comm=comm)


def _pool_grp_bwd(da, mixed, uv, scale, w_grp, nx, name):
    rows, w = da.shape
    ng, gc, _ = w_grp.shape
    segs = _segments(rows, nx)

    def body(da_ref, mx_ref, gt_ref, sc_ref, w_ref, dm_ref, duv_ref, dsc_ref, dz_ref, *bufs):
        dav = da_ref[...].astype(F32)
        mixed = mx_ref[...].astype(F32)
        gt = gt_ref[...].astype(F32)
        sg = _silu(gt)
        sc = sc_ref[...]
        dm = (dav * sc * sg).astype(BF16)
        dm_ref[...] = dm
        dz_ref[...] = lax.dot_general(dm, w_ref[...], _DIMS["nt"], preferred_element_type=F32)
        duv_ref[1] = (dav * mixed * sc * _dsilu(gt)).astype(BF16)
        dsc_ref[...] = jnp.sum(dav * mixed * sg, axis=0, keepdims=True)

        def unpool(win):
            half = win // 2
            for (start, length), buf in zip(segs, bufs):
                dzv = dz_ref[pl.ds(start, length), :]
                s = _window_sum(buf, dzv / _window_count(length, half), 1 - half, win)
                duv_ref[0, pl.ds(start, length), :] = (s - dzv).astype(BF16)

        _per_group(pl.program_id(0), unpool)

    col = pl.BlockSpec((rows, gc), lambda g: (0, g))
    vec = pl.BlockSpec((1, gc), lambda g: (0, g))
    return pl.pallas_call(
        body, grid=(ng,),
        in_specs=[col, col, pl.BlockSpec((rows, gc), lambda g: (0, ng + g)), vec,
                  pl.BlockSpec((None, gc, gc), lambda g: (g, 0, 0))],
        out_specs=[col, pl.BlockSpec((2, rows, gc), lambda g: (0, 0, g)), vec],
        out_shape=[_sds((rows, w), BF16), _sds((2, rows, w), BF16), _sds((1, w), F32)],
        scratch_shapes=[pltpu.VMEM((rows, gc), F32)] + _pool_scratch(rows, nx, gc),
        name=name, compiler_params=_cparams("parallel"),
    )(da, mixed, uv, scale, w_grp)


def _grp_wgrad(z, dm, ng, name, out_dtype):
    rows, w = z.shape
    gc = w // ng

    def body(z_ref, dm_ref, o_ref):
        o_ref[...] = lax.dot_general(z_ref[...], dm_ref[...], _DIMS["tn"],
                                     preferred_element_type=F32).astype(o_ref.dtype)

    blk = pl.BlockSpec((rows, gc), lambda g: (0, g))
    return pl.pallas_call(
        body, grid=(ng,), in_specs=[blk, blk], out_specs=pl.BlockSpec((None, gc, gc), lambda g: (g, 0, 0)),
        out_shape=_sds((ng, gc, gc), out_dtype), name=name, compiler_params=_cparams("parallel"),
    )(z, dm)


def _shift_rows(v, by):
    t = v.shape[0]
    pos = lax.broadcasted_iota(jnp.int32, v.shape, 0)
    rolled = pltpu.roll(v, by % t, 0)
    keep = pos >= by if by > 0 else pos < t + by
    return jnp.where(keep, rolled, 0.0)


def _conv_specs(t, w, cb):
    return [pl.BlockSpec((t, cb), (lambda j, q=q: (0, q * (w // cb) + j))) for q in range(4)]


def _conv_fwd(p4, dw, db, name):
    t = p4.shape[0]
    w = p4.shape[1] // 4
    cb = 128

    def body(bg_ref, cg_ref, v_ref, g_ref, dw_ref, db_ref, a_ref):
        tv = cg_ref[...].astype(F32) * v_ref[...].astype(F32)
        conv = (dw_ref[0:1, :] * _shift_rows(tv, 1) + dw_ref[1:2, :] * tv + dw_ref[2:3, :] * _shift_rows(tv, -1)
                + db_ref[...])
        a_ref[...] = (bg_ref[...].astype(F32) * conv * _silu(g_ref[...].astype(F32))).astype(BF16)

    return pl.pallas_call(
        body, grid=(w // cb,),
        in_specs=_conv_specs(t, w, cb) + [pl.BlockSpec((3, cb), lambda j: (0, j)), pl.BlockSpec((1, cb), lambda j: (0, j))],
        out_specs=pl.BlockSpec((t, cb), lambda j: (0, j)), out_shape=_sds((t, w), BF16),
        name=name, compiler_params=_cparams("parallel"),
    )(p4, p4, p4, p4, dw, db)


def _conv_bwd(da, p4, dw, db, name):
    t, w = da.shape
    cb = 128

    def body(da_ref, bg_ref, cg_ref, v_ref, g_ref, dw_ref, db_ref, d4_ref, ddw_ref, ddb_ref):
        cg = cg_ref[...].astype(F32)
        vv = v_ref[...].astype(F32)
        bg = bg_ref[...].astype(F32)
        gv = g_ref[...].astype(F32)
        tv = cg * vv
        tm1 = _shift_rows(tv, 1)
        tp1 = _shift_rows(tv, -1)
        w0, w1, w2 = dw_ref[0:1, :], dw_ref[1:2, :], dw_ref[2:3, :]
        conv = w0 * tm1 + w1 * tv + w2 * tp1 + db_ref[...]
        y = bg * conv
        dav = da_ref[...].astype(F32)
        dy = dav * _silu(gv)
        d4_ref[3] = (dav * y * _dsilu(gv)).astype(BF16)
        d4_ref[0] = (dy * conv).astype(BF16)
        dconv = dy * bg
        ddb_ref[...] = jnp.sum(dconv, axis=0, keepdims=True)
        ddw_ref[0:1, :] = jnp.sum(dconv * tm1, axis=0, keepdims=True)
        ddw_ref[1:2, :] = jnp.sum(dconv * tv, axis=0, keepdims=True)
        ddw_ref[2:3, :] = jnp.sum(dconv * tp1, axis=0, keepdims=True)
        dt = w0 * _shift_rows(dconv, -1) + w1 * dconv + w2 * _shift_rows(dconv, 1)
        d4_ref[1] = (dt * vv).astype(BF16)
        d4_ref[2] = (dt * cg).astype(BF16)

    col = pl.BlockSpec((t, cb), lambda j: (0, j))
    tap = pl.BlockSpec((3, cb), lambda j: (0, j))
    bias = pl.BlockSpec((1, cb), lambda j: (0, j))
    return pl.pallas_call(
        body, grid=(w // cb,), in_specs=[col] + _conv_specs(t, w, cb) + [tap, bias],
        out_specs=[pl.BlockSpec((4, t, cb), lambda j: (0, 0, j)), tap, bias],
        out_shape=[_sds((4, t, w), BF16), _sds((3, w), F32), _sds((1, w), F32)],
        name=name, compiler_params=_cparams("parallel"),
    )(da, p4, p4, p4, p4, dw, db)


def _attn_mask():
    qn, kn = Q_ROWS * GRID_W, K_ROWS * GRID_W
    qr, qc = np.divmod(np.arange(qn), GRID_W)
    kr, kc = np.divmod(np.arange(kn), GRID_W)
    col0 = np.clip(qc - WIN_COLS // 2, 0, GRID_W - WIN_COLS)
    col_ok = (kc[None, :] >= col0[:, None]) & (kc[None, :] < col0[:, None] + WIN_COLS)
    first = np.zeros(qn, np.int64)
    last = np.full(qn, K_ROWS - WIN_ROWS)
    out = []
    for row0 in (first, qr, last):
        row_ok = (kr[None, :] >= row0[:, None]) & (kr[None, :] < row0[:, None] + WIN_ROWS)
        out.append(np.where(row_ok & col_ok, 0.0, NEG))
    return jnp.asarray(np.stack(out), F32)


_KW = K_ROWS * GRID_W
_QB = Q_ROWS * GRID_W
_PAIR = 2 * HEAD_DIM
_N_DR = 2 * WIN_ROWS - 1
_N_DC = 2 * WIN_COLS - 1
_RP_ROWS = 24
_N_TILES = _N_DR + 1
_BIAS_BASE = (WIN_ROWS - 1, WIN_ROWS // 2 - 1, -1)


class _Comm:
    def __init__(self, ins, outs, sems, start, finish):
        self.ins, self.outs, self.sems, self.start, self.finish = list(ins), list(outs), list(sems), start, finish


def _bias_pieces(cls):
    out = []
    for qr in range(Q_ROWS):
        for kr in range(0, K_ROWS, 2):
            tile = _BIAS_BASE[cls] - qr + kr + 1
            out.append((qr, kr, tile if 0 <= tile < _N_TILES else None))
    return out


def _toeplitz_pair(left_row, right_row):
    lane = lax.broadcasted_iota(jnp.int32, (GRID_W, _PAIR), 1)
    shape = (GRID_W, _PAIR)
    left = pltpu.roll(jnp.broadcast_to(left_row, shape), _PAIR - (WIN_COLS - 1), 1, stride=1, stride_axis=0)
    right = pltpu.roll(jnp.broadcast_to(right_row, shape), GRID_W - (WIN_COLS - 1), 1, stride=1, stride_axis=0)
    return jnp.where(lane < GRID_W, left, right)


def _build_tiles(tiles_ref, rp_ref):
    for h in range(2):
        for t in range(_N_TILES):
            tiles_ref[h, t] = _toeplitz_pair(rp_ref[h, t:t + 1, :], rp_ref[h, t + 1:t + 2, :])


def _block_class(b, nblk, fn, entering=False):
    interior = (b == 1) if entering else jnp.logical_and(b > 0, b < nblk - 1)
    for cls, cond in enumerate((b == 0, interior, b == nblk - 1)):
        pl.when(cond)(functools.partial(fn, cls))


def _attn_geometry(p4, nx):
    rows = p4.shape[0]
    w = p4.shape[1] // 4
    nhp = w // _PAIR
    nblk = nx // _QB
    qspec = lambda col: pl.BlockSpec((_QB, _PAIR), lambda hp, b: (b, col * nhp + hp))
    kspec = lambda col: pl.BlockSpec((rows, _PAIR), lambda hp, b: (0, col * nhp + hp))
    tspec = pl.BlockSpec((2, _RP_ROWS, _PAIR), lambda hp, b: (hp, 0, 0))
    mspec = pl.BlockSpec((None, _QB, _KW), lambda hp, b: (jnp.where(b == 0, 0, jnp.where(b == nblk - 1, 2, 1)), 0, 0))
    lspec = pl.BlockSpec((None, _QB, 2), lambda hp, b: (hp, b, 0))
    ospec = pl.BlockSpec((_QB, _PAIR), lambda hp, b: (b, hp))
    return rows, w, nhp, nblk, qspec, kspec, tspec, mspec, lspec, ospec


def _window_start(b, nx):
    return pl.multiple_of(jnp.clip(b * _QB - PAD_ROWS * GRID_W, 0, nx - _KW), _QB)


def _load_bias(bias_ref, tiles_ref, rp_ref, m_ref, b, nblk):
    pl.when(b == 0)(lambda: _build_tiles(tiles_ref, rp_ref))

    def fill(cls):
        for h in range(2):
            for qr, kr, tile in _bias_pieces(cls):
                rows = slice(qr * GRID_W, (qr + 1) * GRID_W)
                cols = slice(kr * GRID_W, (kr + 2) * GRID_W)
                m = m_ref[rows, cols]
                bias_ref[h, rows, cols] = m if tile is None else tiles_ref[h, tile] + m

    _block_class(b, nblk, fill, entering=True)


def _attn_fwd(p4, rp, mask, nx, name, comm=None):
    rows, w, nhp, nblk, qspec, kspec, tspec, mspec, lspec, ospec = _attn_geometry(p4, nx)
    n_ctx = rows - nx
    n_cin, n_cout = (len(comm.ins), len(comm.outs)) if comm else (0, 0)

    def body(*refs):
        q_ref, k_ref, v_ref, g_ref, rp_ref, m_ref = refs[:6]
        cin = refs[6:6 + n_cin]
        a_ref, o_ref, lse_ref = refs[6 + n_cin:9 + n_cin]
        cout = refs[9 + n_cin:9 + n_cin + n_cout]
        bias_ref, tiles_ref = refs[9 + n_cin + n_cout:11 + n_cin + n_cout]
        sems = refs[11 + n_cin + n_cout:]
        hp, b = pl.program_id(0), pl.program_id(1)
        if comm:
            pl.when(jnp.logical_and(hp == 0, b == 0))(lambda: comm.start(cin, cout, sems))
        start = _window_start(b, nx)
        _load_bias(bias_ref, tiles_ref, rp_ref, m_ref, b, nblk)
        qf = q_ref[...].astype(F32) * HEAD_DIM ** -0.5
        kw = k_ref[pl.ds(start, _KW), :].astype(BF16)
        vw = v_ref[pl.ds(start, _KW), :].astype(BF16)
        kcv = k_ref[pl.ds(nx, n_ctx), :].astype(BF16)
        vcv = v_ref[pl.ds(nx, n_ctx), :].astype(BF16)
        lane = lax.broadcasted_iota(jnp.int32, (1, _PAIR), 1)
        outs, lses = [], []
        for h in range(2):
            mine = (lane >= HEAD_DIM) if h else (lane < HEAD_DIM)
            qm = jnp.where(mine, qf, 0.0).astype(BF16)
            s_loc = lax.dot_general(qm, kw, _DIMS["nt"], preferred_element_type=F32) + bias_ref[h]
            s_ctx = lax.dot_general(qm, kcv, _DIMS["nt"], preferred_element_type=F32)
            mx = jnp.maximum(jnp.max(s_loc, axis=-1, keepdims=True), jnp.max(s_ctx, axis=-1, keepdims=True))
            p_loc = jnp.exp(s_loc - mx)
            p_ctx = jnp.exp(s_ctx - mx)
            den = jnp.sum(p_loc, axis=-1, keepdims=True) + jnp.sum(p_ctx, axis=-1, keepdims=True)
            o = jnp.dot(p_loc.astype(BF16), vw, preferred_element_type=F32)
            o = o + jnp.dot(p_ctx.astype(BF16), vcv, preferred_element_type=F32)
            outs.append(o * (1.0 / den))
            lses.append(mx + jnp.log(den))
        o = jnp.where(lane < HEAD_DIM, outs[0], outs[1])
        o_ref[...] = o.astype(ACT)
        a_ref[...] = (o * _silu(g_ref[...].astype(F32))).astype(BF16)
        col = lax.broadcasted_iota(jnp.int32, (1, 2), 1)
        lse_ref[...] = jnp.where(col == 0, lses[0], lses[1])
        if comm:
            pl.when(jnp.logical_and(hp == nhp - 1, b == nblk - 1))(lambda: comm.finish(cin, cout, sems))

    res = pl.pallas_call(
        body, grid=(nhp, nblk),
        in_specs=[qspec(0), kspec(1), kspec(2), qspec(3), tspec, mspec] + [HBM_SPEC] * n_cin,
        out_specs=[ospec, ospec, lspec] + [HBM_SPEC] * n_cout,
        out_shape=[_sds((nx, w), BF16), _sds((nx, w), ACT), _sds((nhp, nx, 2), F32)] + (comm.outs if comm else []),
        scratch_shapes=[pltpu.VMEM((2, _QB, _KW), F32), pltpu.VMEM((2, _N_TILES, GRID_W, _PAIR), F32)]
        + (comm.sems if comm else []),
        name=name, compiler_params=_cparams("arbitrary", "arbitrary"),
    )(p4, p4, p4, p4, rp, mask, *(comm.ins if comm else []))
    return res[:3], res[3:]


def _fold_tiles(dtiles_ref, drp_ref):
    shape = (GRID_W, _PAIR)
    lane = lax.broadcasted_iota(jnp.int32, shape, 1)
    flip = (lax.broadcasted_iota(jnp.int32, (_PAIR, _PAIR), 0)
            + lax.broadcasted_iota(jnp.int32, (_PAIR, _PAIR), 1) == _PAIR - 1).astype(F32)
    drp_ref[...] = jnp.zeros(drp_ref.shape, F32)
    for h in range(2):
        stack = dtiles_ref[h].reshape(_N_TILES * GRID_W, _PAIR)
        rev = jnp.dot(stack, flip, precision=lax.Precision.HIGHEST, preferred_element_type=F32)
        for t in range(_N_TILES):
            tile = rev[t * GRID_W:(t + 1) * GRID_W, :]
            for side in (0, 1):
                shift = _PAIR - GRID_W * side - (WIN_COLS - 1)
                half = jnp.where((lane < GRID_W) if side else (lane >= GRID_W), tile, 0.0)
                diag = pltpu.roll(half, shift, 1, stride=1, stride_axis=0)
                drp_ref[h, t + side:t + side + 1, :] += jnp.sum(diag, axis=0, keepdims=True)


def _attn_bwd(p4, rp, mask, o, lse, da, nx, name, comm=None):
    rows, w, nhp, nblk, qspec, kspec, tspec, mspec, lspec, ospec = _attn_geometry(p4, nx)
    n_ctx = rows - nx
    n_cin, n_cout = (len(comm.ins), len(comm.outs)) if comm else (0, 0)

    def body(*refs):
        q_ref, k_ref, v_ref, g_ref, rp_ref, m_ref, o_ref, lse_ref, da_ref = refs[:9]
        cin = refs[9:9 + n_cin]
        d4_ref, drp_ref = refs[9 + n_cin:11 + n_cin]
        cout = refs[11 + n_cin:11 + n_cin + n_cout]
        bias_ref, tiles_ref, ds_ref, dtiles_ref, dk_ref, dv_ref = refs[11 + n_cin + n_cout:17 + n_cin + n_cout]
        sems = refs[17 + n_cin + n_cout:]
        hp, b = pl.program_id(0), pl.program_id(1)
        if comm:
            pl.when(jnp.logical_and(hp == 0, b == 0))(lambda: comm.start(cin, cout, sems))
        start = _window_start(b, nx)
        here = pl.multiple_of(b * _QB, _QB)

        @pl.when(b == 0)
        def _():
            dk_ref[...] = jnp.zeros(dk_ref.shape, F32)
            dv_ref[...] = jnp.zeros(dv_ref.shape, F32)
            dtiles_ref[...] = jnp.zeros(dtiles_ref.shape, F32)
            d4_ref[0, pl.ds(nx, n_ctx), :] = jnp.zeros((n_ctx, _PAIR), BF16)
            d4_ref[3, pl.ds(nx, n_ctx), :] = jnp.zeros((n_ctx, _PAIR), BF16)

        _load_bias(bias_ref, tiles_ref, rp_ref, m_ref, b, nblk)
        gv = g_ref[...].astype(F32)
        dav = da_ref[...].astype(F32)
        ov = o_ref[...].astype(F32)
        dov = dav * _silu(gv)
        d4_ref[3, pl.ds(here, _QB), :] = (dav * ov * _dsilu(gv)).astype(BF16)
        qf = q_ref[...].astype(F32) * HEAD_DIM ** -0.5
        kw = k_ref[pl.ds(start, _KW), :].astype(BF16)
        vw = v_ref[pl.ds(start, _KW), :].astype(BF16)
        kcv = k_ref[pl.ds(nx, n_ctx), :].astype(BF16)
        vcv = v_ref[pl.ds(nx, n_ctx), :].astype(BF16)
        lane = lax.broadcasted_iota(jnp.int32, (1, _PAIR), 1)
        dq = jnp.zeros((_QB, _PAIR), F32)
        for h in range(2):
            mine = (lane >= HEAD_DIM) if h else (lane < HEAD_DIM)
            qm = jnp.where(mine, qf, 0.0).astype(BF16)
            dom = jnp.where(mine, dov, 0.0)
            dob = dom.astype(BF16)
            lse = lse_ref[:, h:h + 1]
            s_loc = lax.dot_general(qm, kw, _DIMS["nt"], preferred_element_type=F32)
            p_loc = jnp.exp(s_loc + bias_ref[h] - lse)
            p_ctx = jnp.exp(lax.dot_general(qm, kcv, _DIMS["nt"], preferred_element_type=F32) - lse)
            delta = jnp.sum(dom * ov, axis=-1, keepdims=True)
            ds_loc = p_loc * (lax.dot_general(dob, vw, _DIMS["nt"], preferred_element_type=F32) - delta)
            ds_ctx = p_ctx * (lax.dot_general(dob, vcv, _DIMS["nt"], preferred_element_type=F32) - delta)
            dsb_loc = ds_loc.astype(BF16)
            dsb_ctx = ds_ctx.astype(BF16)
            dq_h = (jnp.dot(dsb_loc, kw, preferred_element_type=F32)
                    + jnp.dot(dsb_ctx, kcv, preferred_element_type=F32))
            dq = dq + jnp.where(mine, dq_h, 0.0)
            dk_ref[pl.ds(start, _KW), :] += lax.dot_general(dsb_loc, qm, _DIMS["tn"], preferred_element_type=F32)
            dv_ref[pl.ds(start, _KW), :] += lax.dot_general(p_loc.astype(BF16), dob, _DIMS["tn"],
                                                            preferred_element_type=F32)
            dk_ref[pl.ds(nx, n_ctx), :] += lax.dot_general(dsb_ctx, qm, _DIMS["tn"], preferred_element_type=F32)
            dv_ref[pl.ds(nx, n_ctx), :] += lax.dot_general(p_ctx.astype(BF16), dob, _DIMS["tn"],
                                                           preferred_element_type=F32)
            ds_ref[h] = ds_loc
        d4_ref[0, pl.ds(here, _QB), :] = (dq * HEAD_DIM ** -0.5).astype(BF16)

        def scatter(cls):
            for h in range(2):
                for qr, kr, tile in _bias_pieces(cls):
                    if tile is not None:
                        dtiles_ref[h, tile] += ds_ref[h, qr * GRID_W:(qr + 1) * GRID_W, kr * GRID_W:(kr + 2) * GRID_W]

        _block_class(b, nblk, scatter)

        @pl.when(b == nblk - 1)
        def _():
            d4_ref[1] = dk_ref[...].astype(BF16)
            d4_ref[2] = dv_ref[...].astype(BF16)
            _fold_tiles(dtiles_ref, drp_ref)

        if comm:
            pl.when(jnp.logical_and(hp == nhp - 1, b == nblk - 1))(lambda: comm.finish(cin, cout, sems))

    tiles = pltpu.VMEM((2, _N_TILES, GRID_W, _PAIR), F32)
    block = pltpu.VMEM((2, _QB, _KW), F32)
    res = pl.pallas_call(
        body, grid=(nhp, nblk),
        in_specs=[qspec(0), kspec(1), kspec(2), qspec(3), tspec, mspec, ospec, lspec, ospec] + [HBM_SPEC] * n_cin,
        out_specs=[pl.BlockSpec((4, rows, _PAIR), lambda hp, b: (0, 0, hp)), tspec] + [HBM_SPEC] * n_cout,
        out_shape=[_sds((4, rows, w), BF16), _sds(rp.shape, F32)] + (comm.outs if comm else []),
        scratch_shapes=[block, tiles, block, tiles, pltpu.VMEM((rows, _PAIR), F32), pltpu.VMEM((rows, _PAIR), F32)]
        + (comm.sems if comm else []),
        name=name, compiler_params=_cparams("arbitrary", "arbitrary"),
    )(p4, p4, p4, p4, rp, mask, o, lse, da, *(comm.ins if comm else []))
    return res[:2], res[2:]


def _final(x, g, target, name):
    rows, d = x.shape
    tr = ROW_BLOCK
    nblk = rows // tr

    def body(x_ref, g_ref, t_ref, loss_ref, dx_ref, dg_ref, acc_ref):
        i = pl.program_id(0)
        xv = x_ref[...]
        gv = g_ref[...]
        r = lax.rsqrt(jnp.mean(xv * xv, axis=-1, keepdims=True) + EPS)
        xn = xv * r
        err = xn * gv - t_ref[...]
        dy = err * (1.0 / d)
        dxn = dy * gv
        dx_ref[...] = r * (dxn - xn * jnp.mean(dxn * xn, axis=-1, keepdims=True))
        s_g = jnp.sum(dy * xn, axis=0, keepdims=True)
        s_l = jnp.sum(jnp.mean(err * err, axis=-1, keepdims=True), axis=0, keepdims=True)

        @pl.when(i == 0)
        def _():
            dg_ref[...] = s_g
            acc_ref[...] = s_l

        @pl.when(i > 0)
        def _():
            dg_ref[...] += s_g
            acc_ref[...] += s_l

        @pl.when(i == nblk - 1)
        def _():
            loss_ref[...] = jnp.broadcast_to(0.5 * acc_ref[...], loss_ref.shape)

    row = pl.BlockSpec((tr, d), lambda i: (i, 0))
    vec = pl.BlockSpec((1, d), lambda i: (0, 0))
    return pl.pallas_call(
        body, grid=(nblk,), in_specs=[row, vec, row],
        out_specs=[pl.BlockSpec((1, 128), lambda i: (0, 0)), row, vec],
        out_shape=[_sds((1, 128), F32), _sds((rows, d), F32), _sds((1, d), F32)],
        scratch_shapes=[pltpu.VMEM((1, 1), F32)], name=name, compiler_params=_cparams("arbitrary"),
    )(x, g, target)


def _as2d(a):
    if a.ndim == 1:
        return a.reshape(-1, 128) if a.shape[0] % 128 == 0 else a.reshape(1, -1)
    return a.reshape(-1, a.shape[-1])


def _adamw(w, g, m, v, name, comm=None):
    shape = w.shape
    w2, g2, m2, v2 = (_as2d(t) for t in (w, g.reshape(shape), m, v))
    rows, cols = w2.shape
    tr = 512 if rows % 512 == 0 else rows
    c1 = 1.0 - ADAM_B1 ** ADAM_STEP
    c2 = 1.0 - ADAM_B2 ** ADAM_STEP

    def body(w_ref, g_ref, m_ref, v_ref, d_ref, nm_ref, nv_ref):
        gv = g_ref[...]
        nm = ADAM_B1 * m_ref[...] + (1.0 - ADAM_B1) * gv
        nv = ADAM_B2 * v_ref[...] + (1.0 - ADAM_B2) * (gv * gv)
        nm_ref[...] = nm
        nv_ref[...] = nv
        d_ref[...] = -ADAM_LR * ((nm / c1) / (jnp.sqrt(nv / c2) + ADAM_EPS) + ADAM_WD * w_ref[...])

    blk = pl.BlockSpec((tr, cols), lambda i: (i, 0))
    outs, carried = _call(body, (w2, g2, m2, v2), grid=(rows // tr,), in_specs=[blk] * 4, out_specs=[blk] * 3,
                          out_shape=[_sds((rows, cols), F32)] * 3, name=name, comm=comm)
    outs = tuple(t.reshape(shape) for t in outs)
    return outs if comm is None else (outs, carried)


def _sum_lead(x, name, out_dtype=F32):
    n, rows, cols = x.shape
    tr = 512 if rows % 512 == 0 else rows

    def body(x_ref, o_ref):
        acc = x_ref[0].astype(F32)
        for k in range(1, n):
            acc = acc + x_ref[k].astype(F32)
        o_ref[...] = acc.astype(out_dtype)

    return pl.pallas_call(
        body, grid=(rows // tr,), in_specs=[pl.BlockSpec((n, tr, cols), lambda i: (0, i, 0))],
        out_specs=pl.BlockSpec((tr, cols), lambda i: (i, 0)), out_shape=_sds((rows, cols), out_dtype),
        name=name, compiler_params=_cparams("parallel"),
    )(x)


_NO_CTX = 1 << 30


def _seg_vecs(mod_l, which, nseg):
    return mod_l[:nseg, which][:, None, :]


def _norm_grads(dshift, dgeff, dgate, g, scale):
    nseg, _, d = dshift.shape
    dmod = jnp.stack([dshift[:, 0], dgeff[:, 0] * g, dgate[:, 0]], axis=1)
    if nseg == 1:
        dmod = jnp.concatenate([dmod, jnp.zeros((1, 3, d), F32)], axis=0)
    dg = jnp.sum(dgeff[:, 0] * (1.0 + scale[:, 0]), axis=0)
    return dmod, dg


def _pool_layer(xin, g, mod_l, w_in, w_grp, w_out, pscale, nx, tag, comms=None):
    rows = xin.shape[0]
    nseg = 2 if rows > nx else 1
    comms = comms or {}
    shift, scale, gate = (_seg_vecs(mod_l, k, nseg) for k in range(3))
    (h, r, uv), c_in = _norm_w_in(xin, g, scale, shift, w_in, nx, f"w_in_fwd_{tag}", comms.get("w_in_fwd"))
    (z, mixed, a), c_pool = _pool_grp_fwd(uv, w_grp, pscale, nx, f"pool_fwd_{tag}", comms.get("pool_fwd"))
    yx, xout = _w_out_resid(a, w_out, xin, gate, nx, f"w_out_fwd_{tag}")

    def backward(dxo, comms=None, token=None, grads_ready=None):
        comms = comms or {}
        gate_b = gate if token is None else gate + token[0, 0]
        (dyx, da, dgate), c_out = _gate_w_out_bwd(dxo, yx, gate_b, w_out, nx, f"w_out_bwd_{tag}",
                                                  comms.get("w_out_bwd"))
        gw_out = _mm_tn(a, dyx, f"w_out_grad_{tag}", BF16)
        dm, duv, dscale = _pool_grp_bwd(da, mixed, uv, pscale, w_grp, nx, f"pool_bwd_{tag}")
        gw_grp = _grp_wgrad(z, dm, w_grp.shape[0], f"grp_grad_{tag}", BF16)
        gw_in = _mm_tn_parts(h, duv, f"w_in_grad_{tag}", BF16)
        grads = dict(w_in=gw_in, w_grp=gw_grp, w_out=gw_out, scale=dscale)
        g_b = g if grads_ready is None else g + grads_ready(grads)[0, 0]
        (dx, dshift, dgeff), c_bwd = _w_in_bwd_norm(duv, w_in, xin, r, g_b, scale, dxo, nx, f"w_in_bwd_{tag}",
                                                    comms.get("w_in_bwd"))
        dmod, dg = _norm_grads(dshift, dgeff, dgate, g[0], scale)
        return dx, dmod, dg, grads, dict(w_out_bwd=c_out, w_in_bwd=c_bwd)

    return xout, backward, dict(w_in_fwd=c_in, pool_fwd=c_pool)


def _na_layer(xc, g, mod_l, w_in, rpb, w_out, nx, mask, comm=None):
    nh, n_dr, n_dc = rpb.shape
    shift, scale = _seg_vecs(mod_l, 0, 2), _seg_vecs(mod_l, 1, 2)
    gate = _seg_vecs(mod_l, 2, 1)
    (h, r, p4), _ = _norm_w_in(xc, g, scale, shift, w_in, nx, "w_in_fwd_na")
    rp = jnp.pad(rpb, ((0, 0), (1, _RP_ROWS - 1 - n_dr), (0, _PAIR - n_dc)))
    (a, o, lse), carried = _attn_fwd(p4, rp, mask, nx, "attn_fwd", comm)
    yx, xout = _w_out_resid(a, w_out, xc, gate, nx, "w_out_fwd_na")

    def backward(dxo, comm=None, token=None, grads_ready=None):
        gate_b = gate if token is None else gate + token[0, 0]
        (dyx, da, dgate), _ = _gate_w_out_bwd(dxo, yx, gate_b, w_out, nx, "w_out_bwd_na")
        gw_out = _mm_tn(a, dyx, "w_out_grad_na", BF16)
        (d4, drp), carried_bwd = _attn_bwd(p4, rp, mask, o, lse, da, nx, "attn_bwd", comm(gw_out) if comm else None)
        gw_in = _mm_tn_parts(h, d4, "w_in_grad_na", BF16)
        g_b = g if grads_ready is None else g + grads_ready(dict(w_in=gw_in, w_out=gw_out))[0, 0]
        (dx, dshift, dgeff), _ = _w_in_bwd_norm(d4, w_in, xc, r, g_b, scale, dxo, nx, "w_in_bwd_na")
        dgate2 = jnp.concatenate([dgate, jnp.zeros_like(dgate)], axis=0)
        dmod, dg = _norm_grads(dshift, dgeff, dgate2, g[0], scale)
        drpb = drp[:, 1:1 + n_dr, ::-1][:, :, :n_dc]
        return dx, dmod, dg, dict(w_in=gw_in, w_out=gw_out, rpb=drpb), carried_bwd

    return xout, backward, carried


def _conv_layer(xin, g, mod_l, w_in, dw, db, w_out):
    shift, scale, gate = (_seg_vecs(mod_l, k, 1) for k in range(3))
    nx = xin.shape[0]
    (h, r, p4), _ = _norm_w_in(xin, g, scale, shift, w_in, nx, "w_in_fwd_conv")
    a = _conv_fwd(p4, dw, db, "conv_fwd")
    yx, xout = _w_out_resid(a, w_out, xin, gate, nx, "w_out_fwd_conv")

    def backward(dxo):
        (dyx, da, dgate), _ = _gate_w_out_bwd(dxo, yx, gate, w_out, nx, "w_out_bwd_conv")
        gw_out = _mm_tn(a, dyx, "w_out_grad_conv", BF16)
        d4, ddw, ddb = _conv_bwd(da, p4, dw, db, "conv_bwd")
        gw_in = _mm_tn_parts(h, d4, "w_in_grad_conv", BF16)
        (dx, dshift, dgeff), _ = _w_in_bwd_norm(d4, w_in, xin, r, g, scale, dxo, nx, "w_in_bwd_conv")
        dmod, dg = _norm_grads(dshift, dgeff, dgate, g[0], scale)
        return dx, dmod, dg, dict(w_in=gw_in, w_out=gw_out, dw=ddw, db=ddb)

    return xout, backward


def _example_step(x, ctx, target, mod, norm_g, final_g, wts, hooks=None):
    hooks = hooks or {}
    na_weights, late_comm, late_weights = (hooks.get(k) for k in ("na_weights", "late_comm", "late_weights"))
    nx = x.shape[0]
    consts = _attn_mask()
    g_rows = [norm_g[i:i + 1] for i in range(4)]
    xc0 = jnp.concatenate([x, ctx], axis=0)
    xc1, bwd0, _ = _pool_layer(xc0, g_rows[0], mod[0], wts["pool_w_in"][0], wts["pool_w_grp"][0],
                               wts["pool_w_out"][0], wts["pool_scale"][0:1], nx, "p0")
    if na_weights is not None:
        wts = {**wts, **na_weights(xc1)}
    x2, bwd1, carried = _na_layer(xc1, g_rows[1], mod[1], wts["na_w_in"], wts["na_rpb"], wts["na_w_out"], nx, consts,
                                  late_comm)
    if late_weights is not None:
        wts = {**wts, **late_weights(carried)}
    x3, bwd2 = _conv_layer(x2, g_rows[2], mod[2], wts["conv_w_in"], wts["conv_dw"], wts["conv_db"], wts["conv_w_out"])
    x4, bwd3, _ = _pool_layer(x3, g_rows[3], mod[3], wts["pool_w_in"][1], wts["pool_w_grp"][1], wts["pool_w_out"][1],
                              wts["pool_scale"][1:2], nx, "p3")
    loss, dx4, dfinal_g = _final(x4, final_g, target, "loss_head")
    call = lambda k, *args: hooks[k](*args) if k in hooks else None
    dx3, dmod3, dg3, gr3, _ = bwd3(dx4)
    dx2, dmod2, dg2, gr2 = bwd2(dx3)
    dxc1, dmod1, dg1, gr1, carried_bwd = bwd1(dx2, hooks.get("grad_comm"), call("late_pairs_start", gr3, gr2),
                                              hooks.get("na_pairs_start"))
    dxc0, dmod0, dg0, gr0, _ = bwd0(dxc1, token=call("na_grads_start", dxc1),
                                    grads_ready=hooks.get("first_pairs_start"))
    return dict(
        loss=loss, grad_x=dxc0[:nx], dmod=jnp.stack([dmod0, dmod1, dmod2, dmod3]),
        dnorm_g=jnp.stack([dg0, dg1, dg2, dg3]), dfinal_g=dfinal_g, layers=(gr0, gr1, gr2, gr3), carried=carried_bwd)


_AXES = ("x", "y", "c")
_CHIP_FLIPS = ((1, 0), (0, 1), (1, 1))


def _position():
    return tuple(lax.axis_index(a) for a in _AXES)


def _flipped(pos, flip):
    return tuple(1 - p if f else p for p, f in zip(pos, flip))


def _join_comms(comms):
    n_in = [len(c.ins) for c in comms]
    n_out = [len(c.outs) for c in comms]
    n_sem = [len(c.sems) for c in comms]

    def parts(ins, outs, sems):
        for k in range(len(comms)):
            a, b, s = sum(n_in[:k]), sum(n_out[:k]), sum(n_sem[:k])
            yield comms[k], (ins[a:a + n_in[k]], outs[b:b + n_out[k]], sems[s:s + n_sem[k]])

    def start(ins, outs, sems):
        for c, part in parts(ins, outs, sems):
            c.start(*part)

    def finish(ins, outs, sems):
        for c, part in parts(ins, outs, sems):
            c.finish(*part)

    joint = _Comm([a for c in comms for a in c.ins], [o for c in comms for o in c.outs],
                  [s for c in comms for s in c.sems], start, finish)
    return joint, lambda res: [list(res[sum(n_out[:k]):sum(n_out[:k + 1])]) for k in range(len(comms))]


def _run_comms(comms, name):
    joint, split = _join_comms(comms)

    def body(*refs):
        n_in, n_out = len(joint.ins), len(joint.outs)
        joint.start(refs[:n_in], refs[n_in:n_in + n_out], refs[n_in + n_out:])
        joint.finish(refs[:n_in], refs[n_in:n_in + n_out], refs[n_in + n_out:])

    res = pl.pallas_call(
        body, in_specs=[HBM_SPEC] * len(joint.ins), out_specs=[HBM_SPEC] * len(joint.outs), out_shape=joint.outs,
        scratch_shapes=joint.sems, name=name,
    )(*joint.ins)
    return split(res)


def _all_gather_comm(v, axes):
    flips = [f for f in np.ndindex(2, 2, 2) if any(f) and all(a in axes or not b for a, b in zip(_AXES, f))]
    n = len(flips) + 1

    def copies(ins, outs, sems):
        (v_ref,), (o_ref,), (send_sems, recv_sems, local_sem) = ins, outs, sems
        pos = _position()
        slot = 0
        for a, p in zip(_AXES, pos):
            if a in axes:
                slot = 2 * slot + p
        local = pltpu.make_async_copy(v_ref, o_ref.at[slot], local_sem)
        remote = [pltpu.make_async_remote_copy(v_ref, o_ref.at[slot], send_sems.at[k], recv_sems.at[k],
                                               device_id=_flipped(pos, flip), device_id_type=MESH)
                  for k, flip in enumerate(flips)]
        return [local] + remote

    def start(ins, outs, sems):
        for cp in copies(ins, outs, sems):
            cp.start()

    def finish(ins, outs, sems):
        for cp in copies(ins, outs, sems):
            cp.wait()

    sems = [pltpu.SemaphoreType.DMA((n - 1,)), pltpu.SemaphoreType.DMA((n - 1,)), pltpu.SemaphoreType.DMA(())]
    return _Comm([v], [_sds((n,) + v.shape, v.dtype)], sems, start, finish)


def _all_gather(v, axes, name):
    return _run_comms([_all_gather_comm(v, axes)], name)[0][0]


class _Item:
    def __init__(self, key, layer, shape, shard_axis, half_axis):
        self.key, self.layer, self.shape = key, layer, tuple(shape)
        self.shard_axis, self.half_axis = shard_axis, half_axis
        self.shard = shape[shard_axis] // 4
        self.half = shape[half_axis] // 2

    def sized(self, shard=False, half=False):
        s = list(self.shape)
        if shard:
            s[self.shard_axis] = self.shard
        if half:
            s[self.half_axis] = self.half
        return tuple(s)

    def window(self, ref, chip=None, half=None):
        idx = [slice(None)] * len(self.shape)
        if chip is not None:
            idx[self.shard_axis] = pl.ds(chip * self.shard, self.shard)
        if half is not None:
            idx[self.half_axis] = pl.ds(half * self.half, self.half)
        return ref.at[tuple(idx)]


def _items(d, w):
    out = []
    for j in range(2):
        out += [_Item("pool_w_in", j, (d, 2 * w), 1, 0), _Item("pool_w_grp", j, (4, w // 4, w // 4), 1, 0),
                _Item("pool_w_out", j, (w, d), 0, 1)]
    out += [_Item("na_w_in", 0, (d, 4 * w), 1, 0), _Item("na_w_out", 0, (w, d), 0, 1),
            _Item("conv_w_in", 0, (d, 4 * w), 1, 0), _Item("conv_w_out", 0, (w, d), 0, 1)]
    return out


def _gather_weights(shards, items, name):
    comm = _gather_comm(shards, items)

    def body(*refs):
        n = len(items)
        comm.start(refs[:n], refs[n:2 * n], refs[2 * n:])
        comm.finish(refs[:n], refs[n:2 * n], refs[2 * n:])

    return pl.pallas_call(
        body, in_specs=[HBM_SPEC] * len(items), out_specs=[HBM_SPEC] * len(items), out_shape=comm.outs,
        scratch_shapes=comm.sems, name=name,
    )(*shards)


def _gather_comm(shards, items):
    n = len(items)

    def copies(src, dst, sems, onward):
        send_a, recv_a, send_b, recv_b, send_c, recv_c = sems
        x, y, c = _position()
        chip = 2 * x + y
        sibling = (x, y, 1 - c)
        own, out, fwd, fwd_in = [], [], [], []
        for i, it in enumerate(items):
            own.append(pltpu.make_async_remote_copy(src[i], it.window(dst[i], chip=chip), send_c.at[i], recv_c.at[i],
                                                    device_id=sibling, device_id_type=MESH))
            for k, flip in enumerate(_CHIP_FLIPS):
                px, py = _flipped((x, y), flip)
                s = 3 * i + k
                out.append(pltpu.make_async_remote_copy(
                    it.window(src[i], half=c), it.window(dst[i], chip=chip, half=c), send_a.at[s], recv_a.at[s],
                    device_id=(px, py, c), device_id_type=MESH))
                if onward:
                    got = it.window(dst[i], chip=2 * px + py, half=c)
                    fwd.append(pltpu.make_async_remote_copy(got, got, send_b.at[s], recv_b.at[s],
                                                            device_id=sibling, device_id_type=MESH))
                    other = it.window(dst[i], chip=2 * px + py, half=1 - c)
                    fwd_in.append(pltpu.make_async_remote_copy(other, other, send_b.at[s], recv_b.at[s],
                                                               device_id=sibling, device_id_type=MESH))
        return own, out, fwd, fwd_in

    def start(src, dst, sems):
        own, out, _, _ = copies(src, dst, sems, False)
        for cp in own + out:
            cp.start()

    def finish(src, dst, sems):
        own, out, fwd, fwd_in = copies(src, dst, sems, True)
        for arrived, onward in zip(out, fwd):
            arrived.wait_recv()
            onward.start()
        for cp in fwd_in:
            cp.wait_recv()
        for cp in out + fwd:
            cp.wait_send()
        for cp in own:
            cp.wait()

    sems = [pltpu.SemaphoreType.DMA((3 * n,)) for _ in range(4)] + [pltpu.SemaphoreType.DMA((n,)) for _ in range(2)]
    return _Comm(shards, [_sds(it.shape, BF16) for it in items], sems, start, finish)


def _pair_swap_copies(windows):
    def copies(src, got, sems):
        send_sems, recv_sems = sems
        x, y, c = _position()
        return [pltpu.make_async_remote_copy(windows[i](src[i], 1 - c), got[i], send_sems.at[i], recv_sems.at[i],
                                             device_id=(x, y, 1 - c), device_id_type=MESH)
                for i in range(len(windows))]

    return copies


def _pair_swap_comm(arrays, windows, out_shapes):
    n = len(arrays)
    copies = _pair_swap_copies(windows)

    def start(src, got, sems):
        for cp in copies(src, got, sems):
            cp.start()

    def finish(src, got, sems):
        for cp in copies(src, got, sems):
            cp.wait()

    return _Comm(arrays, out_shapes, [pltpu.SemaphoreType.DMA((n,)), pltpu.SemaphoreType.DMA((n,))], start, finish)


def _pair_swap(arrays, windows, out_shapes, name):
    return _run_comms([_pair_swap_comm(arrays, windows, out_shapes)], name)[0]


def _chip_exchange(partials, items, name):
    comm = _chip_exchange_comm(partials, items)

    def body(*refs):
        n = len(items)
        comm.start(refs[:n], refs[n:2 * n], refs[2 * n:])
        comm.finish(refs[:n], refs[n:2 * n], refs[2 * n:])

    return pl.pallas_call(
        body, in_specs=[HBM_SPEC] * len(items), out_specs=[HBM_SPEC] * len(items), out_shape=comm.outs,
        scratch_shapes=comm.sems, name=name,
    )(*partials)


def _chip_exchange_copies(items):
    def copies(src, dst, sems):
        send_sems, recv_sems = sems
        x, y, c = _position()
        out = []
        for i, it in enumerate(items):
            for k, flip in enumerate(_CHIP_FLIPS):
                px, py = _flipped((x, y), flip)
                out.append(pltpu.make_async_remote_copy(
                    it.window(src[i], chip=2 * px + py), dst[i].at[k], send_sems.at[3 * i + k],
                    recv_sems.at[3 * i + k], device_id=(px, py, c), device_id_type=MESH))
        return out

    return copies


_SEM_SPEC = pl.BlockSpec(memory_space=pltpu.SEMAPHORE)
_DATAFLOW = pltpu.SideEffectType.DATAFLOW_SIDE_EFFECTING


def _split_start(copies, srcs, zones, n_copies, name):
    n, nz = len(srcs), len(zones)

    def body(*refs):
        src, land = refs[:n], refs[n:n + nz]
        send_sems, recv_sems = refs[n + nz:n + nz + 2]
        token = refs[-1]
        for cp in copies(src, land, (send_sems, recv_sems)):
            cp.start()
        token[...] = jnp.zeros(token.shape, F32)

    hbm = lambda t: pltpu.HBM(t.shape, t.dtype)
    res = pl.pallas_call(
        body, name=name,
        out_shape=(pltpu.SemaphoreType.DMA((n_copies,)), pltpu.SemaphoreType.DMA((n_copies,)),
                   *[hbm(t) for t in list(srcs) + list(zones)], _sds((8, 128), F32)),
        in_specs=[HBM_SPEC] * (n + nz),
        out_specs=(_SEM_SPEC, _SEM_SPEC, *[HBM_SPEC] * (n + nz), pl.BlockSpec(memory_space=pltpu.VMEM)),
        input_output_aliases={i: 2 + i for i in range(n + nz)},
        compiler_params=pltpu.CompilerParams(has_side_effects=_DATAFLOW),
    )(*[pltpu.with_memory_space_constraint(t, pltpu.HBM) for t in list(srcs) + list(zones)])
    return (res[0], res[1], list(res[2:2 + n]), list(res[2 + n:2 + n + nz])), res[-1]


def _split_wait(copies, handle, after, name):
    send_sems, recv_sems, srcs, zones = handle
    n, nz = len(srcs), len(zones)

    def body(*refs):
        src, land = refs[:n], refs[n:n + nz]
        send, recv = refs[n + nz:n + nz + 2]
        for cp in copies(src, land, (send, recv)):
            cp.wait_send()
            cp.wait_recv()

    hbm = lambda t: pltpu.HBM(t.shape, t.dtype)
    res = pl.pallas_call(
        body, name=name, out_shape=tuple(hbm(t) for t in list(srcs) + list(zones)),
        in_specs=[HBM_SPEC] * (n + nz) + [_SEM_SPEC, _SEM_SPEC, pl.BlockSpec(memory_space=pl.ANY)],
        out_specs=tuple([HBM_SPEC] * (n + nz)), input_output_aliases={i: i for i in range(n + nz)},
        compiler_params=pltpu.CompilerParams(has_side_effects=_DATAFLOW),
    )(*srcs, *zones, send_sems, recv_sems, after)
    return list(res[:n]), list(res[n:])


def _gather_ici_copies(items):
    def copies(src, dst, sems):
        send_sems, recv_sems = sems
        x, y, c = _position()
        chip = 2 * x + y
        out = []
        for i, it in enumerate(items):
            for k, flip in enumerate(_CHIP_FLIPS):
                px, py = _flipped((x, y), flip)
                out.append(pltpu.make_async_remote_copy(
                    it.window(src[i], half=c), it.window(dst[i], chip=chip, half=c), send_sems.at[3 * i + k],
                    recv_sems.at[3 * i + k], device_id=(px, py, c), device_id_type=MESH))
        return out

    return copies


def _gather_pair_finish(shards, mats, items, name):
    n = len(items)

    def body(*refs):
        src, dst = refs[:n], refs[2 * n:3 * n]
        send_own, recv_own, send_fwd, recv_fwd = refs[3 * n:]
        x, y, c = _position()
        chip = 2 * x + y
        sibling = (x, y, 1 - c)
        copies = []
        for i, it in enumerate(items):
            copies.append(pltpu.make_async_remote_copy(src[i], it.window(dst[i], chip=chip), send_own.at[i],
                                                       recv_own.at[i], device_id=sibling, device_id_type=MESH))
            for k, flip in enumerate(_CHIP_FLIPS):
                px, py = _flipped((x, y), flip)
                got = it.window(dst[i], chip=2 * px + py, half=c)
                copies.append(pltpu.make_async_remote_copy(got, got, send_fwd.at[3 * i + k], recv_fwd.at[3 * i + k],
                                                           device_id=sibling, device_id_type=MESH))
        for cp in copies:
            cp.start()
        for cp in copies:
            cp.wait()

    return pl.pallas_call(
        body, in_specs=[HBM_SPEC] * (2 * n), out_specs=[HBM_SPEC] * n, out_shape=[_sds(it.shape, BF16) for it in items],
        input_output_aliases={n + i: i for i in range(n)},
        scratch_shapes=[pltpu.SemaphoreType.DMA((n,)), pltpu.SemaphoreType.DMA((n,)),
                        pltpu.SemaphoreType.DMA((3 * n,)), pltpu.SemaphoreType.DMA((3 * n,))], name=name,
    )(*shards, *mats)


def _chip_exchange_comm(partials, items):
    n = len(items)
    copies = _chip_exchange_copies(items)

    def start(src, dst, sems):
        for cp in copies(src, dst, sems):
            cp.start()

    def finish(src, dst, sems):
        for cp in copies(src, dst, sems):
            cp.wait()

    return _Comm(partials, [_sds((3,) + it.sized(shard=True, half=True), BF16) for it in items],
                 [pltpu.SemaphoreType.DMA((3 * n,)), pltpu.SemaphoreType.DMA((3 * n,))], start, finish)


_SUM_STEPS = 2


def _pair_sums(gs, gots, its, pos, name):
    n = len(its)
    nb = _SUM_STEPS
    g2 = [g.reshape(-1, g.shape[-1]) for g in gs]
    got2 = [t.reshape(-1, t.shape[-1]) for t in gots]

    def body(pos_ref, *refs):
        for g_ref, got_ref, o_ref in zip(refs[:n], refs[n:2 * n], refs[2 * n:]):
            o_ref[...] = (g_ref[...].astype(F32) + got_ref[...].astype(F32)).astype(BF16)

    g_specs, got_specs = [], []
    for it, t in zip(its, got2):
        rows, cols = t.shape
        blk = (rows // nb, cols)
        g_map = (lambda i, pos: (pos[1] * nb + i, 0)) if it.half_axis == 0 else (lambda i, pos: (i, pos[1]))
        g_specs.append(pl.BlockSpec(blk, g_map))
        got_specs.append(pl.BlockSpec(blk, lambda i, pos: (i, 0)))
    outs = pl.pallas_call(
        body, grid_spec=pltpu.PrefetchScalarGridSpec(
            num_scalar_prefetch=1, grid=(nb,), in_specs=g_specs + got_specs, out_specs=got_specs),
        out_shape=[_sds(t.shape, BF16) for t in got2], name=name, compiler_params=_cparams("parallel"),
    )(pos, *g2, *got2)
    return [o.reshape(t.shape) for o, t in zip(outs, gots)]


_FLIP_SLOT = {2: 0, 1: 1, 3: 2}


def _chip_sums(pairs, slots, its, pos, name):
    n = len(its)
    nb = _SUM_STEPS

    def body(pos_ref, *refs):
        chip = pos_ref[0]
        for own in range(4):
            @pl.when(chip == own)
            def _():
                for p_ref, s_ref, o_ref in zip(refs[:n], refs[n:2 * n], refs[2 * n:]):
                    acc = None
                    for k in range(4):
                        v = (p_ref[...] if k == own else s_ref[_FLIP_SLOT[own ^ k]]).astype(F32)
                        acc = v if acc is None else acc + v
                    o_ref[...] = acc

    p_specs, s_specs, o_specs, shapes = [], [], [], []
    for it in its:
        shape = it.sized(shard=True, half=True)
        blk = (shape[0] // nb,) + shape[1:]
        rest = (0,) * (len(shape) - 1)

        def p_map(i, pos, it=it, nd=len(shape)):
            lead = i + (pos[0] * nb if it.shard_axis == 0 else 0)
            return (lead,) + tuple(pos[0] if ax == it.shard_axis else 0 for ax in range(1, nd))

        p_specs.append(pl.BlockSpec(blk, p_map))
        s_specs.append(pl.BlockSpec((3,) + blk, lambda i, pos, rest=rest: (0, i) + rest))
        o_specs.append(pl.BlockSpec(blk, lambda i, pos, rest=rest: (i,) + rest))
        shapes.append(_sds(shape, F32))
    return pl.pallas_call(
        body, grid_spec=pltpu.PrefetchScalarGridSpec(
            num_scalar_prefetch=1, grid=(nb,), in_specs=p_specs + s_specs, out_specs=o_specs),
        out_shape=shapes, name=name, compiler_params=_cparams("parallel"),
    )(pos, *pairs, *slots)


_GRAD_KEYS = ("pool_w_in", "pool_w_grp", "pool_w_out", "na_w_in", "na_w_out", "conv_w_in", "conv_w_out")


def _adamw_matrix(w, m, v, owns, others, it, pos, name):
    nl = w.shape[0]
    rows_split = it.half_axis == 0
    r, cdim = int(np.prod(w.shape[1:-1])), w.shape[-1]
    hr, hc = (r // 2, cdim) if rows_split else (r, cdim // 2)
    br = min(hr, 256)
    nb = hr // br
    c1 = 1.0 - ADAM_B1 ** ADAM_STEP
    c2 = 1.0 - ADAM_B2 ** ADAM_STEP

    def body(pos_ref, w_ref, m_ref, v_ref, *rest):
        own_refs, other_refs = rest[:nl], rest[nl:2 * nl]
        g_ref, d_ref, nm_ref, nv_ref = rest[2 * nl:]
        j, h = pl.program_id(0), pl.program_id(1)
        own, other = own_refs[0][...], other_refs[0][...]
        for q in range(1, nl):
            own = jnp.where(j == q, own_refs[q][...], own)
            other = jnp.where(j == q, other_refs[q][...], other)
        gv = jnp.where(h == pos_ref[1], own, other)
        nm = ADAM_B1 * m_ref[...] + (1.0 - ADAM_B1) * gv
        nv = ADAM_B2 * v_ref[...] + (1.0 - ADAM_B2) * (gv * gv)
        g_ref[...] = gv
        nm_ref[...] = nm
        nv_ref[...] = nv
        d_ref[...] = -ADAM_LR * ((nm / c1) / (jnp.sqrt(nv / c2) + ADAM_EPS) + ADAM_WD * w_ref[...])

    if rows_split:
        full = pl.BlockSpec((None, br, hc), lambda j, h, i, pos: (j, h * nb + i, 0))
    else:
        full = pl.BlockSpec((None, br, hc), lambda j, h, i, pos: (j, i, h))
    half = pl.BlockSpec((br, hc), lambda j, h, i, pos: (i, 0))
    flat = lambda t: t.reshape(nl, r, cdim)
    outs = pl.pallas_call(
        body, grid_spec=pltpu.PrefetchScalarGridSpec(
            num_scalar_prefetch=1, grid=(nl, 2, nb), in_specs=[full] * 3 + [half] * (2 * nl), out_specs=[full] * 4),
        out_shape=[_sds((nl, r, cdim), F32)] * 4, name=name,
        compiler_params=_cparams("parallel", "parallel", "parallel"),
    )(pos, flat(w), flat(m), flat(v), *[t.reshape(hr, hc) for t in list(owns) + list(others)])
    return tuple(t.reshape(w.shape) for t in outs)


_WEIGHTS = ("c_ctx", "norm_g", "ada_w", "ada_b", "pool_w_in", "pool_w_grp", "pool_scale", "pool_w_out", "na_w_in",
            "na_rpb", "na_w_out", "conv_w_in", "conv_dw", "conv_db", "conv_w_out", "final_g")
_COND_ROWS = 16


def _modulations(cond, ada_w, ada_b_cols):
    nl, d, n = ada_w.shape
    return _matmul(
        cond, ada_w, mode="nn", grid=(nl, 1), a_silu=True, epilogue="bias",
        a_spec=pl.BlockSpec((_COND_ROWS, d), lambda i, j: (0, 0)), b_spec=pl.BlockSpec((None, d, n), lambda i, j: (i, 0, 0)),
        extra=(ada_b_cols,), extra_specs=(pl.BlockSpec((None, 1, n), lambda i, j: (i, 0, 0)),),
        out_shapes=[_sds((nl, _COND_ROWS, n), F32)], out_specs=[pl.BlockSpec((None, _COND_ROWS, n), lambda i, j: (i, 0, 0))],
        name="modulations")[0]


def _ada_w_grad(cond, dm_cols):
    d = cond.shape[1]
    nl, _, n = dm_cols.shape
    return _matmul(
        cond, dm_cols, mode="tn", grid=(nl, 1), a_silu=True,
        a_spec=pl.BlockSpec((_COND_ROWS, d), lambda i, j: (0, 0)), b_spec=pl.BlockSpec((None, _COND_ROWS, n), lambda i, j: (i, 0, 0)),
        out_shapes=[_sds((nl, d, n), F32)], out_specs=[pl.BlockSpec((None, d, n), lambda i, j: (i, 0, 0))],
        name="ada_w_grad")[0]


def _cond_grad(dm_cols, ada_w):
    nl, d, n = ada_w.shape
    return _matmul(
        dm_cols, ada_w, mode="nt", grid=(1, nl), nk=nl, acc_shape=(_COND_ROWS, d),
        a_spec=pl.BlockSpec((None, _COND_ROWS, n), lambda i, q: (q, 0, 0)), b_spec=pl.BlockSpec((None, d, n), lambda i, q: (q, 0, 0)),
        out_shapes=[_sds((_COND_ROWS, d), F32)], out_specs=[pl.BlockSpec((_COND_ROWS, d), lambda i, q: (0, 0))],
        name="cond_grad")[0]


def _pack(parts):
    flat = [p.reshape(-1) for p in parts]
    sizes = [f.shape[0] for f in flat]
    total = sum(sizes)
    rows = -(-total // 1024) * 8
    packed = jnp.concatenate(flat + [jnp.zeros((rows * 128 - total,), F32)]).reshape(rows, 128)
    offs = np.concatenate([[0], np.cumsum(sizes)])[:-1]
    return packed, [(int(o), p.shape) for o, p in zip(offs, parts)]


def _unpack(flat, layout, k):
    off, shape = layout[k]
    return flat[..., off:off + int(np.prod(shape))].reshape(flat.shape[:-1] + tuple(shape))


def kernel(x, c, ctx, c_ctx, norm_g, ada_w, ada_b, pool_w_in, pool_w_grp, pool_scale, pool_w_out, na_w_in, na_rpb, na_w_out, conv_w_in, conv_dw, conv_db, conv_w_out, final_g, loss_target, m_c_ctx, m_norm_g, m_ada_w, m_ada_b, m_pool_w_in, m_pool_w_grp, m_pool_scale, m_pool_w_out, m_na_w_in, m_na_rpb, m_na_w_out, m_conv_w_in, m_conv_dw, m_conv_db, m_conv_w_out, m_final_g, v_c_ctx, v_norm_g, v_ada_w, v_ada_b, v_pool_w_in, v_pool_w_grp, v_pool_scale, v_pool_w_out, v_na_w_in, v_na_rpb, v_na_w_out, v_conv_w_in, v_conv_dw, v_conv_db, v_conv_w_out, v_final_g):
    params = dict(c_ctx=c_ctx, norm_g=norm_g, ada_w=ada_w, ada_b=ada_b, pool_w_in=pool_w_in, pool_w_grp=pool_w_grp,
                  pool_scale=pool_scale, pool_w_out=pool_w_out, na_w_in=na_w_in, na_rpb=na_rpb, na_w_out=na_w_out,
                  conv_w_in=conv_w_in, conv_dw=conv_dw, conv_db=conv_db, conv_w_out=conv_w_out, final_g=final_g)
    mom1 = dict(c_ctx=m_c_ctx, norm_g=m_norm_g, ada_w=m_ada_w, ada_b=m_ada_b, pool_w_in=m_pool_w_in,
                pool_w_grp=m_pool_w_grp, pool_scale=m_pool_scale, pool_w_out=m_pool_w_out, na_w_in=m_na_w_in,
                na_rpb=m_na_rpb, na_w_out=m_na_w_out, conv_w_in=m_conv_w_in, conv_dw=m_conv_dw, conv_db=m_conv_db,
                conv_w_out=m_conv_w_out, final_g=m_final_g)
    mom2 = dict(c_ctx=v_c_ctx, norm_g=v_norm_g, ada_w=v_ada_w, ada_b=v_ada_b, pool_w_in=v_pool_w_in,
                pool_w_grp=v_pool_w_grp, pool_scale=v_pool_scale, pool_w_out=v_pool_w_out, na_w_in=v_na_w_in,
                na_rpb=v_na_rpb, na_w_out=v_na_w_out, conv_w_in=v_conv_w_in, conv_dw=v_conv_dw, conv_db=v_conv_db,
                conv_w_out=v_conv_w_out, final_g=v_final_g)
    d = x.shape[-1]
    w = na_w_out.shape[1] * 4
    xi, yi, ci = _position()
    chip = 2 * xi + yi
    dev = 2 * chip + ci
    n_ada = ada_w.shape[-1]

    def chip_cols(a, size):
        return lax.dynamic_slice_in_dim(a, chip * size, size, axis=a.ndim - 1)

    items = _items(d, w)
    first = [it for it in items if it.key.startswith("pool") and it.layer == 0]
    na = [it for it in items if it.key.startswith("na")]
    late = [it for it in items if it not in first + na]
    shards_of = lambda its: [params[it.key][it.layer].astype(BF16) for it in its]
    empties = lambda its: [lax.empty(it.shape, BF16) for it in its]
    first_copies, na_copies = _gather_ici_copies(first), _gather_ici_copies(na)

    conds = _all_gather(c.reshape(8, d // 8), _AXES, "gather_cond").reshape(8, d)
    behind = conds[0, 0] * 0.0
    first_handle, token = _split_start(first_copies, [s + behind.astype(BF16) for s in shards_of(first)],
                                       empties(first), 3 * len(first), "gather_first_start")
    cond = jnp.concatenate([conds + token[0, 0], c_ctx[None], jnp.zeros((_COND_ROWS - 9, d), F32)], axis=0)
    mod_cols = _modulations(cond, ada_w, chip_cols(ada_b, n_ada)[:, None, :])
    small_pack, small_layout = _pack([pool_scale, conv_dw, conv_db])
    (mod_all,), (small,) = _run_comms([_all_gather_comm(mod_cols, ("x", "y")),
                                       _all_gather_comm(small_pack, ("x", "y"))], "gather_mod")
    behind = mod_all[0, 0, 0, 0] * 0.0
    na_handle, token = _split_start(na_copies, [s + behind.astype(BF16) for s in shards_of(na)], empties(na),
                                    3 * len(na), "gather_na_start")
    first_shards, first_mats = _split_wait(first_copies, first_handle, token, "gather_first_wait")
    first_mats = _gather_pair_finish(first_shards, first_mats, first, "gather_first_pair")
    mod_all = mod_all.transpose(1, 2, 0, 3).reshape(4, _COND_ROWS, 3, d)
    mod = jnp.stack([lax.dynamic_index_in_dim(mod_all, dev, axis=1, keepdims=False), mod_all[:, 8]], axis=1)
    full = {(it.key, it.layer): mat for it, mat in zip(first, first_mats)}
    late_comm = _gather_comm(shards_of(late), late)

    def na_weights(after):
        na_shards, na_mats = _split_wait(na_copies, na_handle, after, "gather_na_wait")
        na_mats = _gather_pair_finish(na_shards, na_mats, na, "gather_na_pair")
        return {it.key: mat for it, mat in zip(na, na_mats)}

    def late_weights(mats):
        full.update({(it.key, it.layer): mat for it, mat in zip(late, mats)})
        return dict(pool_w_in=[full[("pool_w_in", j)] for j in range(2)],
                    pool_w_grp=[full[("pool_w_grp", j)] for j in range(2)],
                    pool_w_out=[full[("pool_w_out", j)] for j in range(2)],
                    conv_w_in=full[("conv_w_in", 0)], conv_w_out=full[("conv_w_out", 0)])

    small = small.reshape(4, -1)

    def whole(k):
        parts = _unpack(small, small_layout, k)
        return jnp.moveaxis(parts, 0, -2).reshape(parts.shape[1:-1] + (-1,))

    wts = dict(pool_w_in=[full[("pool_w_in", 0)]], pool_w_grp=[full[("pool_w_grp", 0)]],
               pool_w_out=[full[("pool_w_out", 0)]], pool_scale=whole(0), na_rpb=na_rpb[0], conv_dw=whole(1)[0],
               conv_db=whole(2))
    pos = jnp.stack([chip, ci]).astype(jnp.int32)

    def layer_grads(its, by_layer):
        pick = {"pool_w_in": "w_in", "pool_w_grp": "w_grp", "pool_w_out": "w_out", "na_w_in": "w_in",
                "na_w_out": "w_out", "conv_w_in": "w_in", "conv_w_out": "w_out"}
        return [by_layer[(it.key.split("_")[0], it.layer)][pick[it.key]] for it in its]

    pairs, handles, swap_copies = dict(), dict(), dict()
    for tag, its in (("late", late), ("na", na), ("first", first)):
        swap_copies[tag] = _pair_swap_copies([(lambda ref, half, it=it: it.window(ref, half=half)) for it in its])

    def pair_swap_start(tag, its, mats):
        handles[tag], started = _split_start(swap_copies[tag], mats, [lax.empty(it.sized(half=True), BF16) for it in its],
                                             len(its), f"pair_exchange_{tag}_start")
        return started

    def pair_swap_finish(tag, its, after):
        mats, got = _split_wait(swap_copies[tag], handles[tag], after, f"pair_exchange_{tag}_wait")
        return _pair_sums(mats, got, its, pos, f"pair_sum_{tag}")

    def late_pairs_start(gr3, gr2):
        return pair_swap_start("late", late, layer_grads(late, {("pool", 1): gr3, ("conv", 0): gr2}))

    def grad_comm(after):
        pairs["late"] = pair_swap_finish("late", late, after)
        return _chip_exchange_comm(pairs["late"], late)

    def na_pairs_start(gr1):
        return pair_swap_start("na", na, layer_grads(na, {("na", 0): gr1}))

    slot_zones = lambda its: [lax.empty((3,) + it.sized(shard=True, half=True), BF16) for it in its]
    na_xcopies, first_xcopies = _chip_exchange_copies(na), _chip_exchange_copies(first)

    def na_grads_start(after):
        pairs["na"] = pair_swap_finish("na", na, after)
        handles["xna"], started = _split_start(na_xcopies, pairs["na"], slot_zones(na), 3 * len(na),
                                               "exchange_na_start")
        return started

    def first_pairs_start(gr0):
        return pair_swap_start("first", first, layer_grads(first, {("pool", 0): gr0}))

    res = _example_step(x[0], ctx[0], loss_target[0], mod, norm_g, final_g[None], wts, dict(
        na_weights=na_weights, late_comm=late_comm, late_weights=late_weights, late_pairs_start=late_pairs_start,
        grad_comm=grad_comm, na_pairs_start=na_pairs_start, na_grads_start=na_grads_start,
        first_pairs_start=first_pairs_start))
    g0, g1, g2, g3 = res["layers"]
    pairs["na"], na_slots = _split_wait(na_xcopies, handles["xna"], res["grad_x"], "exchange_na_wait")
    pairs["first"] = pair_swap_finish("first", first, res["grad_x"])
    packed, layout = _pack([res["dfinal_g"], res["dnorm_g"], res["dmod"], g1["rpb"],
                            jnp.concatenate([g0["scale"], g3["scale"]], axis=0), g2["dw"], g2["db"],
                            res["loss"][0, :1]])
    every = _all_gather(packed, _AXES, "gather_vec_grads")

    grads = dict()
    total = _sum_lead(every, "sum_vec_grads").reshape(-1)
    every = every.reshape(8, -1)
    grads["final_g"] = _unpack(total, layout, 0).reshape(final_g.shape)
    grads["norm_g"] = _unpack(total, layout, 1)
    grads["na_rpb"] = _unpack(total, layout, 3)[None]
    grads["pool_scale"] = chip_cols(_unpack(total, layout, 4), pool_scale.shape[-1])
    grads["conv_dw"] = chip_cols(_unpack(total, layout, 5), conv_dw.shape[-1])[None]
    grads["conv_db"] = chip_cols(_unpack(total, layout, 6), conv_db.shape[-1])
    dmod_sum = _unpack(total, layout, 2).reshape(4, 2, 3 * d)
    dmod_each = _unpack(every, layout, 2).reshape(8, 4, 2, 3 * d)
    grads["ada_b"] = dmod_sum[:, 0] + dmod_sum[:, 1]
    dm = jnp.concatenate([dmod_each[:, :, 0].transpose(1, 0, 2), dmod_sum[:, 1][:, None],
                          jnp.zeros((4, _COND_ROWS - 9, 3 * d), F32)], axis=1)
    dm_cols = chip_cols(dm, n_ada)
    dcond = _cond_grad(dm_cols, ada_w)[8].reshape(8, d // 8)
    dcond_all = _all_gather(dcond, ("x", "y"), "gather_cond_grad")
    behind = dcond_all[0, 0, 0] * 0.0
    handles["first"], token = _split_start(first_xcopies, [p + behind.astype(BF16) for p in pairs["first"]],
                                           slot_zones(first), 3 * len(first), "exchange_first_start")
    grads["ada_w"] = _ada_w_grad(cond, dm_cols + token[0, 0])
    grads["c_ctx"] = _sum_lead(dcond_all, "sum_cond_grad").reshape(d) * _dsilu(c_ctx)
    vector_out = {k: _adamw(params[k], grads[k], mom1[k], mom2[k], f"adamw_{k}")
                  for k in _WEIGHTS if k not in _GRAD_KEYS}
    pairs["first"], first_slots = _split_wait(first_xcopies, handles["first"], vector_out["ada_w"][2],
                                              "exchange_first_wait")

    slots = dict(zip(late, res["carried"]))
    slots.update(zip(first, first_slots))
    slots.update(zip(na, na_slots))
    pair_of = dict(zip(late, pairs["late"]))
    pair_of.update(zip(first, pairs["first"]))
    pair_of.update(zip(na, pairs["na"]))
    reduced = _chip_sums([pair_of[it] for it in items], [slots[it] for it in items], items, pos, "chip_sum")
    theirs = _pair_swap(reduced, [lambda ref, half: ref] * len(items),
                        [_sds(t.shape, F32) for t in reduced], "pair_return")
    matrix_out = dict()
    for k in _GRAD_KEYS:
        idx = [i for i, it in enumerate(items) if it.key == k]
        res_k = _adamw_matrix(params[k], mom1[k], mom2[k], [reduced[i] for i in idx], [theirs[i] for i in idx],
                              items[idx[0]], pos, f"adamw_{k}")
        grads[k], matrix_out[k] = res_k[0], res_k[1:]

    outs = [[], [], []]
    for k in _WEIGHTS:
        step = matrix_out[k] if k in matrix_out else vector_out[k]
        for lst, val in zip(outs, step):
            lst.append(val)
    loss = _unpack(total, layout, 7)[0]
    return (loss, res["grad_x"][None], *[grads[k].reshape(params[k].shape) for k in _WEIGHTS],
            *outs[0], *outs[1], *outs[2])
```

```python
import functools

import numpy as np
import jax
import jax.numpy as jnp
from jax import lax
from jax.experimental import pallas as pl
from jax.experimental.pallas import tpu as pltpu

F32 = jnp.float32
BF16 = jnp.bfloat16

EPS = 1e-6
GRID_W = 64
HEAD_DIM = 64
WIN_ROWS = 8
WIN_COLS = 16
POOL_WINDOWS = (2, 4, 8, 16)
Q_ROWS = 4
K_ROWS = 12
PAD_ROWS = 4
NEG = -1e30

ADAM_LR = 0.001
ADAM_B1 = 0.9
ADAM_B2 = 0.999
ADAM_EPS = 1e-08
ADAM_WD = 0.01
ADAM_STEP = 10

ROW_BLOCK = 256
VMEM_LIMIT = 56 * 1024 * 1024
ACT = BF16

MESH = pl.DeviceIdType.MESH
HBM_SPEC = pl.BlockSpec(memory_space=pltpu.HBM)


def _cparams(*sem):
    return pltpu.CompilerParams(dimension_semantics=sem or None, vmem_limit_bytes=VMEM_LIMIT)


def _sds(shape, dtype):
    return jax.ShapeDtypeStruct(tuple(shape), dtype)


def _call(body, args, *, grid, in_specs, out_specs, out_shape, name, scratch_shapes=()):
    return list(pl.pallas_call(
        body, grid=grid, in_specs=list(in_specs), out_specs=list(out_specs), out_shape=list(out_shape),
        scratch_shapes=list(scratch_shapes), name=name, compiler_params=_cparams(*(("arbitrary",) * len(grid))),
    )(*args))


def _sigmoid(x):
    return 1.0 / (1.0 + jnp.exp(-x))


def _silu(x):
    return x * _sigmoid(x)


def _dsilu(x):
    s = _sigmoid(x)
    return s * (1.0 + x * (1.0 - s))


_DIMS = {
    "nn": (((1,), (0,)), ((), ())),
    "nt": (((1,), (1,)), ((), ())),
    "tn": (((0,), (0,)), ((), ())),
}


def _matmul(a, b, *, mode, grid, a_spec, b_spec, out_shapes, out_specs, name, nk=1,
            a_silu=False, exact=False, epilogue=None, extra=(), extra_specs=(), acc_shape=None):
    n_extra = len(extra)
    n_out = len(out_shapes)

    def body(*refs):
        a_ref, b_ref = refs[:2]
        ex = refs[2:2 + n_extra]
        outs = refs[2 + n_extra:2 + n_extra + n_out]
        av = a_ref[...]
        bv = b_ref[...]
        if a_silu:
            av = _silu(av.astype(F32))
        if exact:
            prod = lax.dot_general(av.astype(F32), bv.astype(F32), _DIMS[mode],
                                   precision=lax.Precision.HIGHEST, preferred_element_type=F32)
        else:
            prod = lax.dot_general(av.astype(BF16), bv.astype(BF16), _DIMS[mode], preferred_element_type=F32)

        def finish(res):
            if epilogue == "bias":
                res = res + ex[0][...]
            outs[0][...] = res.astype(outs[0].dtype)

        if nk == 1:
            finish(prod)
        else:
            acc = refs[-1]
            k = pl.program_id(len(grid) - 1)

            @pl.when(k == 0)
            def _():
                acc[...] = prod

            @pl.when(k > 0)
            def _():
                acc[...] += prod

            @pl.when(k == nk - 1)
            def _():
                finish(acc[...])

    scratch = [pltpu.VMEM(acc_shape, F32)] if nk > 1 else []
    sem = ("parallel",) * (len(grid) - 1) + ("arbitrary",)
    return pl.pallas_call(
        body, grid=grid, in_specs=[a_spec, b_spec, *extra_specs], out_specs=list(out_specs),
        out_shape=list(out_shapes), scratch_shapes=scratch, name=name, compiler_params=_cparams(*sem),
    )(a, b, *extra)


def _row_tile(rows):
    for t in (768, 512, 256):
        if rows % t == 0:
            return t
    return rows


def _mm_tn(a, b, name, out_dtype, tm=512):
    r, m = a.shape
    n = b.shape[1]
    tm = min(tm, m)
    tn = min(1024, n)
    return _matmul(
        a, b, mode="tn", grid=(m // tm, n // tn),
        a_spec=pl.BlockSpec((r, tm), lambda i, j: (0, i)), b_spec=pl.BlockSpec((r, tn), lambda i, j: (0, j)),
        out_shapes=[_sds((m, n), out_dtype)], out_specs=[pl.BlockSpec((tm, tn), lambda i, j: (i, j))], name=name)[0]


def _mm_tn_parts(a, b, name, out_dtype, tm=512):
    r, m = a.shape
    p, _, np_ = b.shape
    tm = min(tm, m)
    return _matmul(
        a, b, mode="tn", grid=(m // tm, p),
        a_spec=pl.BlockSpec((r, tm), lambda i, q: (0, i)), b_spec=pl.BlockSpec((None, r, np_), lambda i, q: (q, 0, 0)),
        out_shapes=[_sds((m, p * np_), out_dtype)], out_specs=[pl.BlockSpec((tm, np_), lambda i, q: (i, q))],
        name=name)[0]


def _row_vec(ref, is_ctx):
    return ref[0] if is_ctx is None else jnp.where(is_ctx, ref[1], ref[0])


def _ctx_rows(i, tm, nx, nseg):
    if nseg == 1:
        return None
    return i * tm + lax.broadcasted_iota(jnp.int32, (tm, 1), 0) >= nx


def _seg_sums(ref, val, is_ctx, first):
    if is_ctx is None:
        parts = [jnp.sum(val, axis=0, keepdims=True)]
    else:
        parts = [jnp.sum(jnp.where(is_ctx, 0.0, val), axis=0, keepdims=True),
                 jnp.sum(jnp.where(is_ctx, val, 0.0), axis=0, keepdims=True)]

    @pl.when(first)
    def _():
        for k, p in enumerate(parts):
            ref[k] = p

    @pl.when(jnp.logical_not(first))
    def _():
        for k, p in enumerate(parts):
            ref[k] += p


def _w_out_resid(a, w_out, xres, gate, nx, name):
    m, k = a.shape
    n = w_out.shape[1]
    nseg = gate.shape[0]
    tm = _row_tile(m)

    def body(a_ref, w_ref, x_ref, gt_ref, yx_ref, xo_ref):
        yx = jnp.dot(a_ref[...], w_ref[...], preferred_element_type=F32)
        yx_ref[...] = yx.astype(ACT)
        xo_ref[...] = x_ref[...] + _row_vec(gt_ref, _ctx_rows(pl.program_id(0), tm, nx, nseg)) * yx

    row = pl.BlockSpec((tm, n), lambda i: (i, 0))
    return pl.pallas_call(
        body, grid=(m // tm,),
        in_specs=[pl.BlockSpec((tm, k), lambda i: (i, 0)), pl.BlockSpec((k, n), lambda i: (0, 0)), row,
                  pl.BlockSpec((nseg, 1, n), lambda i: (0, 0, 0))],
        out_specs=[row, row], out_shape=[_sds((m, n), ACT), _sds((m, n), F32)],
        name=name, compiler_params=_cparams("parallel"),
    )(a, w_out, xres, gate)


def _norm_w_in(x, g, scale, shift, w_in, nx, name):
    rows, d = x.shape
    n = w_in.shape[1]
    nseg = scale.shape[0]
    tm = _row_tile(rows)
    tn = min(1024, n)

    def body(x_ref, g_ref, sc_ref, sh_ref, w_ref, h_ref, r_ref, p_ref):
        i, j = pl.program_id(0), pl.program_id(1)

        @pl.when(j == 0)
        def _():
            xv = x_ref[...]
            r = lax.rsqrt(jnp.mean(xv * xv, axis=-1, keepdims=True) + EPS)
            is_ctx = _ctx_rows(i, tm, nx, nseg)
            h = (xv * r) * g_ref[...] * (1.0 + _row_vec(sc_ref, is_ctx)) + _row_vec(sh_ref, is_ctx)
            h_ref[...] = h.astype(BF16)
            r_ref[...] = r

        p_ref[...] = jnp.dot(h_ref[...], w_ref[...], preferred_element_type=F32).astype(ACT)

    vec = pl.BlockSpec((nseg, 1, d), lambda i, j: (0, 0, 0))
    return _call(
        body, (x, g, scale, shift, w_in), grid=(rows // tm, n // tn),
        in_specs=[pl.BlockSpec((tm, d), lambda i, j: (i, 0)), pl.BlockSpec((1, d), lambda i, j: (0, 0)), vec, vec,
                  pl.BlockSpec((d, tn), lambda i, j: (0, j))],
        out_specs=[pl.BlockSpec((tm, d), lambda i, j: (i, 0)), pl.BlockSpec((tm, 1), lambda i, j: (i, 0)),
                   pl.BlockSpec((tm, tn), lambda i, j: (i, j))],
        out_shape=[_sds((rows, d), BF16), _sds((rows, 1), F32), _sds((rows, n), ACT)], name=name)


def _gate_w_out_bwd(dxo, yx, gate, w_out, nx, name):
    rows, d = yx.shape
    w = w_out.shape[0]
    nseg = gate.shape[0]
    tm = _row_tile(rows)

    def body(dx_ref, yx_ref, gt_ref, w_ref, dyx_ref, da_ref, dg_ref):
        i = pl.program_id(0)
        is_ctx = _ctx_rows(i, tm, nx, nseg)
        dxv = dx_ref[...]
        dyx = (dxv * _row_vec(gt_ref, is_ctx)).astype(BF16)
        dyx_ref[...] = dyx
        da_ref[...] = lax.dot_general(dyx, w_ref[...], _DIMS["nt"], preferred_element_type=F32).astype(ACT)
        _seg_sums(dg_ref, dxv * yx_ref[...].astype(F32), is_ctx, i == 0)

    row = pl.BlockSpec((tm, d), lambda i: (i, 0))
    vec = pl.BlockSpec((nseg, 1, d), lambda i: (0, 0, 0))
    return _call(
        body, (dxo, yx, gate, w_out), grid=(rows // tm,),
        in_specs=[row, row, vec, pl.BlockSpec((w, d), lambda i: (0, 0))],
        out_specs=[row, pl.BlockSpec((tm, w), lambda i: (i, 0)), vec],
        out_shape=[_sds((rows, d), BF16), _sds((rows, w), ACT), _sds((nseg, 1, d), F32)], name=name)


def _w_in_bwd_norm(dparts, w_in, x, r, g, scale, dres, nx, name):
    np_, rows, kp = dparts.shape
    d = w_in.shape[0]
    nseg = scale.shape[0]
    tm = _row_tile(rows)
    nsub = tm // ROW_BLOCK
    nres_blocks = dres.shape[0] // ROW_BLOCK

    def body(dp_ref, w_ref, x_ref, r_ref, g_ref, sc_ref, *rest):
        dres_refs = rest[:nsub]
        dx_ref, dsh_ref, dge_ref, acc = rest[nsub:]
        i, k = pl.program_id(0), pl.program_id(1)
        prod = lax.dot_general(dp_ref[...], w_ref[...], _DIMS["nt"], preferred_element_type=F32)

        @pl.when(k == 0)
        def _():
            acc[...] = prod

        @pl.when(k > 0)
        def _():
            acc[...] += prod

        @pl.when(k == np_ - 1)
        def _():
            is_ctx = _ctx_rows(i, tm, nx, nseg)
            dhv = acc[...]
            rv = r_ref[...]
            xn = x_ref[...] * rv
            dxn = dhv * (g_ref[...] * (1.0 + _row_vec(sc_ref, is_ctx)))
            dx = rv * (dxn - xn * jnp.mean(dxn * xn, axis=-1, keepdims=True))
            for s in range(nsub):
                piece = slice(s * ROW_BLOCK, (s + 1) * ROW_BLOCK)
                res = dres_refs[s][...]
                if nres_blocks * ROW_BLOCK < rows:
                    res = jnp.where(i * nsub + s < nres_blocks, res, 0.0)
                dx_ref[piece, :] = dx[piece, :] + res
            _seg_sums(dsh_ref, dhv, is_ctx, i == 0)
            _seg_sums(dge_ref, dhv * xn, is_ctx, i == 0)

    row = pl.BlockSpec((tm, d), lambda i, k: (i, 0))
    vec = pl.BlockSpec((nseg, 1, d), lambda i, k: (0, 0, 0))
    return _call(
        body, (dparts, w_in, x, r, g, scale, *([dres] * nsub)), grid=(rows // tm, np_),
        in_specs=[pl.BlockSpec((None, tm, kp), lambda i, k: (k, i, 0)), pl.BlockSpec((d, kp), lambda i, k: (0, k)),
                  row, pl.BlockSpec((tm, 1), lambda i, k: (i, 0)), pl.BlockSpec((1, d), lambda i, k: (0, 0)), vec]
        + [pl.BlockSpec((ROW_BLOCK, d), (lambda i, k, s=s: (jnp.minimum(i * nsub + s, nres_blocks - 1), 0)))
           for s in range(nsub)],
        out_specs=[row, vec, vec],
        out_shape=[_sds((rows, d), F32), _sds((nseg, 1, d), F32), _sds((nseg, 1, d), F32)],
        scratch_shapes=[pltpu.VMEM((tm, d), F32)], name=name)


_PAD_TOP = 16
_PAD_BOT = 32


def _window_sum(buf, xv, lo, n):
    t = xv.shape[0]
    c = xv.shape[1]
    tp = t + _PAD_TOP + _PAD_BOT
    buf[pl.ds(0, _PAD_TOP), :] = jnp.zeros((_PAD_TOP, c), F32)
    buf[pl.ds(_PAD_TOP, t), :] = xv
    buf[pl.ds(_PAD_TOP + t, _PAD_BOT), :] = jnp.zeros((_PAD_BOT, c), F32)
    p = buf[...]
    k = 1
    while k < n:
        p = p + pltpu.roll(p, tp - k, 0)
        k *= 2
    if lo:
        p = pltpu.roll(p, -lo, 0)
    buf[...] = p
    return buf[pl.ds(_PAD_TOP, t), :]


def _window_count(t, half):
    pos = lax.broadcasted_iota(jnp.int32, (t, 1), 0)
    return (jnp.minimum(pos + half, t) - jnp.maximum(pos - half, 0)).astype(F32)


def _segments(rows, nx):
    return [(0, nx)] + ([(nx, rows - nx)] if rows > nx else [])


def _pool_scratch(rows, nx, cols):
    return [pltpu.VMEM((length + _PAD_TOP + _PAD_BOT, cols), F32) for _, length in _segments(rows, nx)]


def _per_group(g, fn):
    for gi, win in enumerate(POOL_WINDOWS):
        pl.when(g == gi)(functools.partial(fn, win))


def _pool_grp_fwd(uv, w_grp, scale, nx, name):
    rows = uv.shape[0]
    ng, gc, _ = w_grp.shape
    w = ng * gc
    segs = _segments(rows, nx)

    def body(u_ref, gt_ref, w_ref, sc_ref, z_ref, mx_ref, a_ref, *bufs):
        def pool(win):
            half = win // 2
            for (start, length), buf in zip(segs, bufs):
                uvv = u_ref[pl.ds(start, length), :].astype(F32)
                s = _window_sum(buf, uvv, -half, win)
                z_ref[pl.ds(start, length), :] = (s / _window_count(length, half) - uvv).astype(BF16)

        _per_group(pl.program_id(0), pool)
        mixed = jnp.dot(z_ref[...], w_ref[...], preferred_element_type=F32)
        mx_ref[...] = mixed.astype(ACT)
        a_ref[...] = (mixed * sc_ref[...] * _silu(gt_ref[...].astype(F32))).astype(BF16)

    col = pl.BlockSpec((rows, gc), lambda g: (0, g))
    return _call(
        body, (uv, uv, w_grp, scale), grid=(ng,),
        in_specs=[col, pl.BlockSpec((rows, gc), lambda g: (0, ng + g)), pl.BlockSpec((None, gc, gc), lambda g: (g, 0, 0)),
                  pl.BlockSpec((1, gc), lambda g: (0, g))],
        out_specs=[col, col, col], out_shape=[_sds((rows, w), BF16), _sds((rows, w), ACT), _sds((rows, w), BF16)],
        scratch_shapes=_pool_scratch(rows, nx, gc), name=name)


def _pool_grp_bwd(da, mixed, uv, scale, w_grp, nx, name):
    rows, w = da.shape
    ng, gc, _ = w_grp.shape
    segs = _segments(rows, nx)

    def body(da_ref, mx_ref, gt_ref, sc_ref, w_ref, dm_ref, duv_ref, dsc_ref, dz_ref, *bufs):
        dav = da_ref[...].astype(F32)
        mixed = mx_ref[...].astype(F32)
        gt = gt_ref[...].astype(F32)
        sg = _silu(gt)
        sc = sc_ref[...]
        dm = (dav * sc * sg).astype(BF16)
        dm_ref[...] = dm
        dz_ref[...] = lax.dot_general(dm, w_ref[...], _DIMS["nt"], preferred_element_type=F32)
        duv_ref[1] = (dav * mixed * sc * _dsilu(gt)).astype(BF16)
        dsc_ref[...] = jnp.sum(dav * mixed * sg, axis=0, keepdims=True)

        def unpool(win):
            half = win // 2
            for (start, length), buf in zip(segs, bufs):
                dzv = dz_ref[pl.ds(start, length), :]
                s = _window_sum(buf, dzv / _window_count(length, half), 1 - half, win)
                duv_ref[0, pl.ds(start, length), :] = (s - dzv).astype(BF16)

        _per_group(pl.program_id(0), unpool)

    col = pl.BlockSpec((rows, gc), lambda g: (0, g))
    vec = pl.BlockSpec((1, gc), lambda g: (0, g))
    return pl.pallas_call(
        body, grid=(ng,),
        in_specs=[col, col, pl.BlockSpec((rows, gc), lambda g: (0, ng + g)), vec,
                  pl.BlockSpec((None, gc, gc), lambda g: (g, 0, 0))],
        out_specs=[col, pl.BlockSpec((2, rows, gc), lambda g: (0, 0, g)), vec],
        out_shape=[_sds((rows, w), BF16), _sds((2, rows, w), BF16), _sds((1, w), F32)],
        scratch_shapes=[pltpu.VMEM((rows, gc), F32)] + _pool_scratch(rows, nx, gc),
        name=name, compiler_params=_cparams("parallel"),
    )(da, mixed, uv, scale, w_grp)


def _grp_wgrad(z, dm, ng, name, out_dtype):
    rows, w = z.shape
    gc = w // ng

    def body(z_ref, dm_ref, o_ref):
        o_ref[...] = lax.dot_general(z_ref[...], dm_ref[...], _DIMS["tn"],
                                     preferred_element_type=F32).astype(o_ref.dtype)

    blk = pl.BlockSpec((rows, gc), lambda g: (0, g))
    return pl.pallas_call(
        body, grid=(ng,), in_specs=[blk, blk], out_specs=pl.BlockSpec((None, gc, gc), lambda g: (g, 0, 0)),
        out_shape=_sds((ng, gc, gc), out_dtype), name=name, compiler_params=_cparams("parallel"),
    )(z, dm)


def _shift_rows(v, by):
    t = v.shape[0]
    pos = lax.broadcasted_iota(jnp.int32, v.shape, 0)
    rolled = pltpu.roll(v, by % t, 0)
    keep = pos >= by if by > 0 else pos < t + by
    return jnp.where(keep, rolled, 0.0)


def _conv_specs(t, w, cb):
    return [pl.BlockSpec((t, cb), (lambda j, q=q: (0, q * (w // cb) + j))) for q in range(4)]


def _conv_fwd(p4, dw, db, name):
    t = p4.shape[0]
    w = p4.shape[1] // 4
    cb = 128

    def body(bg_ref, cg_ref, v_ref, g_ref, dw_ref, db_ref, a_ref):
        tv = cg_ref[...].astype(F32) * v_ref[...].astype(F32)
        conv = (dw_ref[0:1, :] * _shift_rows(tv, 1) + dw_ref[1:2, :] * tv + dw_ref[2:3, :] * _shift_rows(tv, -1)
                + db_ref[...])
        a_ref[...] = (bg_ref[...].astype(F32) * conv * _silu(g_ref[...].astype(F32))).astype(BF16)

    return pl.pallas_call(
        body, grid=(w // cb,),
        in_specs=_conv_specs(t, w, cb) + [pl.BlockSpec((3, cb), lambda j: (0, j)), pl.BlockSpec((1, cb), lambda j: (0, j))],
        out_specs=pl.BlockSpec((t, cb), lambda j: (0, j)), out_shape=_sds((t, w), BF16),
        name=name, compiler_params=_cparams("parallel"),
    )(p4, p4, p4, p4, dw, db)


def _conv_bwd(da, p4, dw, db, name):
    t, w = da.shape
    cb = 128

    def body(da_ref, bg_ref, cg_ref, v_ref, g_ref, dw_ref, db_ref, d4_ref, ddw_ref, ddb_ref):
        cg = cg_ref[...].astype(F32)
        vv = v_ref[...].astype(F32)
        bg = bg_ref[...].astype(F32)
        gv = g_ref[...].astype(F32)
        tv = cg * vv
        tm1 = _shift_rows(tv, 1)
        tp1 = _shift_rows(tv, -1)
        w0, w1, w2 = dw_ref[0:1, :], dw_ref[1:2, :], dw_ref[2:3, :]
        conv = w0 * tm1 + w1 * tv + w2 * tp1 + db_ref[...]
        y = bg * conv
        dav = da_ref[...].astype(F32)
        dy = dav * _silu(gv)
        d4_ref[3] = (dav * y * _dsilu(gv)).astype(BF16)
        d4_ref[0] = (dy * conv).astype(BF16)
        dconv = dy * bg
        ddb_ref[...] = jnp.sum(dconv, axis=0, keepdims=True)
        ddw_ref[0:1, :] = jnp.sum(dconv * tm1, axis=0, keepdims=True)
        ddw_ref[1:2, :] = jnp.sum(dconv * tv, axis=0, keepdims=True)
        ddw_ref[2:3, :] = jnp.sum(dconv * tp1, axis=0, keepdims=True)
        dt = w0 * _shift_rows(dconv, -1) + w1 * dconv + w2 * _shift_rows(dconv, 1)
        d4_ref[1] = (dt * vv).astype(BF16)
        d4_ref[2] = (dt * cg).astype(BF16)

    col = pl.BlockSpec((t, cb), lambda j: (0, j))
    tap = pl.BlockSpec((3, cb), lambda j: (0, j))
    bias = pl.BlockSpec((1, cb), lambda j: (0, j))
    return pl.pallas_call(
        body, grid=(w // cb,), in_specs=[col] + _conv_specs(t, w, cb) + [tap, bias],
        out_specs=[pl.BlockSpec((4, t, cb), lambda j: (0, 0, j)), tap, bias],
        out_shape=[_sds((4, t, w), BF16), _sds((3, w), F32), _sds((1, w), F32)],
        name=name, compiler_params=_cparams("parallel"),
    )(da, p4, p4, p4, p4, dw, db)


def _attn_mask():
    qn, kn = Q_ROWS * GRID_W, K_ROWS * GRID_W
    qr, qc = np.divmod(np.arange(qn), GRID_W)
    kr, kc = np.divmod(np.arange(kn), GRID_W)
    col0 = np.clip(qc - WIN_COLS // 2, 0, GRID_W - WIN_COLS)
    col_ok = (kc[None, :] >= col0[:, None]) & (kc[None, :] < col0[:, None] + WIN_COLS)
    first = np.zeros(qn, np.int64)
    last = np.full(qn, K_ROWS - WIN_ROWS)
    out = []
    for row0 in (first, qr, last):
        row_ok = (kr[None, :] >= row0[:, None]) & (kr[None, :] < row0[:, None] + WIN_ROWS)
        out.append(np.where(row_ok & col_ok, 0.0, NEG))
    return jnp.asarray(np.stack(out), F32)


_KW = K_ROWS * GRID_W
_QB = Q_ROWS * GRID_W
_PAIR = 2 * HEAD_DIM
_N_DR = 2 * WIN_ROWS - 1
_N_DC = 2 * WIN_COLS - 1
_RP_ROWS = 24
_N_TILES = _N_DR + 1
_BIAS_BASE = (WIN_ROWS - 1, WIN_ROWS // 2 - 1, -1)


class _Comm:
    def __init__(self, ins, outs, sems, start, finish):
        self.ins, self.outs, self.sems, self.start, self.finish = list(ins), list(outs), list(sems), start, finish


def _bias_pieces(cls):
    out = []
    for qr in range(Q_ROWS):
        for kr in range(0, K_ROWS, 2):
            tile = _BIAS_BASE[cls] - qr + kr + 1
            out.append((qr, kr, tile if 0 <= tile < _N_TILES else None))
    return out


def _toeplitz_pair(left_row, right_row):
    lane = lax.broadcasted_iota(jnp.int32, (GRID_W, _PAIR), 1)
    shape = (GRID_W, _PAIR)
    left = pltpu.roll(jnp.broadcast_to(left_row, shape), _PAIR - (WIN_COLS - 1), 1, stride=1, stride_axis=0)
    right = pltpu.roll(jnp.broadcast_to(right_row, shape), GRID_W - (WIN_COLS - 1), 1, stride=1, stride_axis=0)
    return jnp.where(lane < GRID_W, left, right)


def _build_tiles(tiles_ref, rp_ref):
    for h in range(2):
        for t in range(_N_TILES):
            tiles_ref[h, t] = _toeplitz_pair(rp_ref[h, t:t + 1, :], rp_ref[h, t + 1:t + 2, :])


def _block_class(b, nblk, fn, entering=False):
    interior = (b == 1) if entering else jnp.logical_and(b > 0, b < nblk - 1)
    for cls, cond in enumerate((b == 0, interior, b == nblk - 1)):
        pl.when(cond)(functools.partial(fn, cls))


def _attn_geometry(p4, nx):
    rows = p4.shape[0]
    w = p4.shape[1] // 4
    nhp = w // _PAIR
    nblk = nx // _QB
    qspec = lambda col: pl.BlockSpec((_QB, _PAIR), lambda hp, b: (b, col * nhp + hp))
    kspec = lambda col: pl.BlockSpec((rows, _PAIR), lambda hp, b: (0, col * nhp + hp))
    tspec = pl.BlockSpec((2, _RP_ROWS, _PAIR), lambda hp, b: (hp, 0, 0))
    mspec = pl.BlockSpec((None, _QB, _KW), lambda hp, b: (jnp.where(b == 0, 0, jnp.where(b == nblk - 1, 2, 1)), 0, 0))
    lspec = pl.BlockSpec((None, _QB, 2), lambda hp, b: (hp, b, 0))
    ospec = pl.BlockSpec((_QB, _PAIR), lambda hp, b: (b, hp))
    return rows, w, nhp, nblk, qspec, kspec, tspec, mspec, lspec, ospec


def _window_start(b, nx):
    return pl.multiple_of(jnp.clip(b * _QB - PAD_ROWS * GRID_W, 0, nx - _KW), _QB)


def _load_bias(bias_ref, tiles_ref, rp_ref, m_ref, b, nblk):
    pl.when(b == 0)(lambda: _build_tiles(tiles_ref, rp_ref))

    def fill(cls):
        for h in range(2):
            for qr, kr, tile in _bias_pieces(cls):
                rows = slice(qr * GRID_W, (qr + 1) * GRID_W)
                cols = slice(kr * GRID_W, (kr + 2) * GRID_W)
                m = m_ref[rows, cols]
                bias_ref[h, rows, cols] = m if tile is None else tiles_ref[h, tile] + m

    _block_class(b, nblk, fill, entering=True)


def _attn_fwd(p4, rp, mask, nx, name, comm=None):
    rows, w, nhp, nblk, qspec, kspec, tspec, mspec, lspec, ospec = _attn_geometry(p4, nx)
    n_ctx = rows - nx
    n_cin, n_cout = (len(comm.ins), len(comm.outs)) if comm else (0, 0)

    def body(*refs):
        q_ref, k_ref, v_ref, g_ref, rp_ref, m_ref = refs[:6]
        cin = refs[6:6 + n_cin]
        a_ref, o_ref, lse_ref = refs[6 + n_cin:9 + n_cin]
        cout = refs[9 + n_cin:9 + n_cin + n_cout]
        bias_ref, tiles_ref = refs[9 + n_cin + n_cout:11 + n_cin + n_cout]
        sems = refs[11 + n_cin + n_cout:]
        hp, b = pl.program_id(0), pl.program_id(1)
        if comm:
            pl.when(jnp.logical_and(hp == 0, b == 0))(lambda: comm.start(cin, cout, sems))
        start = _window_start(b, nx)
        _load_bias(bias_ref, tiles_ref, rp_ref, m_ref, b, nblk)
        qf = q_ref[...].astype(F32) * HEAD_DIM ** -0.5
        kw = k_ref[pl.ds(start, _KW), :].astype(BF16)
        vw = v_ref[pl.ds(start, _KW), :].astype(BF16)
        kcv = k_ref[pl.ds(nx, n_ctx), :].astype(BF16)
        vcv = v_ref[pl.ds(nx, n_ctx), :].astype(BF16)
        lane = lax.broadcasted_iota(jnp.int32, (1, _PAIR), 1)
        outs, lses = [], []
        for h in range(2):
            mine = (lane >= HEAD_DIM) if h else (lane < HEAD_DIM)
            qm = jnp.where(mine, qf, 0.0).astype(BF16)
            s_loc = lax.dot_general(qm, kw, _DIMS["nt"], preferred_element_type=F32) + bias_ref[h]
            s_ctx = lax.dot_general(qm, kcv, _DIMS["nt"], preferred_element_type=F32)
            mx = jnp.maximum(jnp.max(s_loc, axis=-1, keepdims=True), jnp.max(s_ctx, axis=-1, keepdims=True))
            p_loc = jnp.exp(s_loc - mx)
            p_ctx = jnp.exp(s_ctx - mx)
            den = jnp.sum(p_loc, axis=-1, keepdims=True) + jnp.sum(p_ctx, axis=-1, keepdims=True)
            o = jnp.dot(p_loc.astype(BF16), vw, preferred_element_type=F32)
            o = o + jnp.dot(p_ctx.astype(BF16), vcv, preferred_element_type=F32)
            outs.append(o * (1.0 / den))
            lses.append(mx + jnp.log(den))
        o = jnp.where(lane < HEAD_DIM, outs[0], outs[1])
        o_ref[...] = o.astype(ACT)
        a_ref[...] = (o * _silu(g_ref[...].astype(F32))).astype(BF16)
        col = lax.broadcasted_iota(jnp.int32, (1, 2), 1)
        lse_ref[...] = jnp.where(col == 0, lses[0], lses[1])
        if comm:
            pl.when(jnp.logical_and(hp == nhp - 1, b == nblk - 1))(lambda: comm.finish(cin, cout, sems))

    res = pl.pallas_call(
        body, grid=(nhp, nblk),
        in_specs=[qspec(0), kspec(1), kspec(2), qspec(3), tspec, mspec] + [HBM_SPEC] * n_cin,
        out_specs=[ospec, ospec, lspec] + [HBM_SPEC] * n_cout,
        out_shape=[_sds((nx, w), BF16), _sds((nx, w), ACT), _sds((nhp, nx, 2), F32)] + (comm.outs if comm else []),
        scratch_shapes=[pltpu.VMEM((2, _QB, _KW), F32), pltpu.VMEM((2, _N_TILES, GRID_W, _PAIR), F32)]
        + (comm.sems if comm else []),
        name=name, compiler_params=_cparams("arbitrary", "arbitrary"),
    )(p4, p4, p4, p4, rp, mask, *(comm.ins if comm else []))
    return res[:3], res[3:]


def _fold_tiles(dtiles_ref, drp_ref):
    shape = (GRID_W, _PAIR)
    lane = lax.broadcasted_iota(jnp.int32, shape, 1)
    flip = (lax.broadcasted_iota(jnp.int32, (_PAIR, _PAIR), 0)
            + lax.broadcasted_iota(jnp.int32, (_PAIR, _PAIR), 1) == _PAIR - 1).astype(F32)
    drp_ref[...] = jnp.zeros(drp_ref.shape, F32)
    for h in range(2):
        stack = dtiles_ref[h].reshape(_N_TILES * GRID_W, _PAIR)
        rev = jnp.dot(stack, flip, precision=lax.Precision.HIGHEST, preferred_element_type=F32)
        for t in range(_N_TILES):
            tile = rev[t * GRID_W:(t + 1) * GRID_W, :]
            for side in (0, 1):
                shift = _PAIR - GRID_W * side - (WIN_COLS - 1)
                half = jnp.where((lane < GRID_W) if side else (lane >= GRID_W), tile, 0.0)
                diag = pltpu.roll(half, shift, 1, stride=1, stride_axis=0)
                drp_ref[h, t + side:t + side + 1, :] += jnp.sum(diag, axis=0, keepdims=True)


def _attn_bwd(p4, rp, mask, o, lse, da, nx, name, comm=None):
    rows, w, nhp, nblk, qspec, kspec, tspec, mspec, lspec, ospec = _attn_geometry(p4, nx)
    n_ctx = rows - nx
    n_cin, n_cout = (len(comm.ins), len(comm.outs)) if comm else (0, 0)

    def body(*refs):
        q_ref, k_ref, v_ref, g_ref, rp_ref, m_ref, o_ref, lse_ref, da_ref = refs[:9]
        cin = refs[9:9 + n_cin]
        d4_ref, drp_ref = refs[9 + n_cin:11 + n_cin]
        cout = refs[11 + n_cin:11 + n_cin + n_cout]
        bias_ref, tiles_ref, ds_ref, dtiles_ref, dk_ref, dv_ref = refs[11 + n_cin + n_cout:17 + n_cin + n_cout]
        sems = refs[17 + n_cin + n_cout:]
        hp, b = pl.program_id(0), pl.program_id(1)
        if comm:
            pl.when(jnp.logical_and(hp == 0, b == 0))(lambda: comm.start(cin, cout, sems))
        start = _window_start(b, nx)
        here = pl.multiple_of(b * _QB, _QB)

        @pl.when(b == 0)
        def _():
            dk_ref[...] = jnp.zeros(dk_ref.shape, F32)
            dv_ref[...] = jnp.zeros(dv_ref.shape, F32)
            dtiles_ref[...] = jnp.zeros(dtiles_ref.shape, F32)
            d4_ref[0, pl.ds(nx, n_ctx), :] = jnp.zeros((n_ctx, _PAIR), BF16)
            d4_ref[3, pl.ds(nx, n_ctx), :] = jnp.zeros((n_ctx, _PAIR), BF16)

        _load_bias(bias_ref, tiles_ref, rp_ref, m_ref, b, nblk)
        gv = g_ref[...].astype(F32)
        dav = da_ref[...].astype(F32)
        ov = o_ref[...].astype(F32)
        dov = dav * _silu(gv)
        d4_ref[3, pl.ds(here, _QB), :] = (dav * ov * _dsilu(gv)).astype(BF16)
        qf = q_ref[...].astype(F32) * HEAD_DIM ** -0.5
        kw = k_ref[pl.ds(start, _KW), :].astype(BF16)
        vw = v_ref[pl.ds(start, _KW), :].astype(BF16)
        kcv = k_ref[pl.ds(nx, n_ctx), :].astype(BF16)
        vcv = v_ref[pl.ds(nx, n_ctx), :].astype(BF16)
        lane = lax.broadcasted_iota(jnp.int32, (1, _PAIR), 1)
        dq = jnp.zeros((_QB, _PAIR), F32)
        for h in range(2):
            mine = (lane >= HEAD_DIM) if h else (lane < HEAD_DIM)
            qm = jnp.where(mine, qf, 0.0).astype(BF16)
            dom = jnp.where(mine, dov, 0.0)
            dob = dom.astype(BF16)
            lse = lse_ref[:, h:h + 1]
            s_loc = lax.dot_general(qm, kw, _DIMS["nt"], preferred_element_type=F32)
            p_loc = jnp.exp(s_loc + bias_ref[h] - lse)
            p_ctx = jnp.exp(lax.dot_general(qm, kcv, _DIMS["nt"], preferred_element_type=F32) - lse)
            delta = jnp.sum(dom * ov, axis=-1, keepdims=True)
            ds_loc = p_loc * (lax.dot_general(dob, vw, _DIMS["nt"], preferred_element_type=F32) - delta)
            ds_ctx = p_ctx * (lax.dot_general(dob, vcv, _DIMS["nt"], preferred_element_type=F32) - delta)
            dsb_loc = ds_loc.astype(BF16)
            dsb_ctx = ds_ctx.astype(BF16)
            dq_h = (jnp.dot(dsb_loc, kw, preferred_element_type=F32)
                    + jnp.dot(dsb_ctx, kcv, preferred_element_type=F32))
            dq = dq + jnp.where(mine, dq_h, 0.0)
            dk_ref[pl.ds(start, _KW), :] += lax.dot_general(dsb_loc, qm, _DIMS["tn"], preferred_element_type=F32)
            dv_ref[pl.ds(start, _KW), :] += lax.dot_general(p_loc.astype(BF16), dob, _DIMS["tn"],
                                                            preferred_element_type=F32)
            dk_ref[pl.ds(nx, n_ctx), :] += lax.dot_general(dsb_ctx, qm, _DIMS["tn"], preferred_element_type=F32)
            dv_ref[pl.ds(nx, n_ctx), :] += lax.dot_general(p_ctx.astype(BF16), dob, _DIMS["tn"],
                                                           preferred_element_type=F32)
            ds_ref[h] = ds_loc
        d4_ref[0, pl.ds(here, _QB), :] = (dq * HEAD_DIM ** -0.5).astype(BF16)

        def scatter(cls):
            for h in range(2):
                for qr, kr, tile in _bias_pieces(cls):
                    if tile is not None:
                        dtiles_ref[h, tile] += ds_ref[h, qr * GRID_W:(qr + 1) * GRID_W, kr * GRID_W:(kr + 2) * GRID_W]

        _block_class(b, nblk, scatter)

        @pl.when(b == nblk - 1)
        def _():
            d4_ref[1] = dk_ref[...].astype(BF16)
            d4_ref[2] = dv_ref[...].astype(BF16)
            _fold_tiles(dtiles_ref, drp_ref)

        if comm:
            pl.when(jnp.logical_and(hp == nhp - 1, b == nblk - 1))(lambda: comm.finish(cin, cout, sems))

    tiles = pltpu.VMEM((2, _N_TILES, GRID_W, _PAIR), F32)
    block = pltpu.VMEM((2, _QB, _KW), F32)
    res = pl.pallas_call(
        body, grid=(nhp, nblk),
        in_specs=[qspec(0), kspec(1), kspec(2), qspec(3), tspec, mspec, ospec, lspec, ospec] + [HBM_SPEC] * n_cin,
        out_specs=[pl.BlockSpec((4, rows, _PAIR), lambda hp, b: (0, 0, hp)), tspec] + [HBM_SPEC] * n_cout,
        out_shape=[_sds((4, rows, w), BF16), _sds(rp.shape, F32)] + (comm.outs if comm else []),
        scratch_shapes=[block, tiles, block, tiles, pltpu.VMEM((rows, _PAIR), F32), pltpu.VMEM((rows, _PAIR), F32)]
        + (comm.sems if comm else []),
        name=name, compiler_params=_cparams("arbitrary", "arbitrary"),
    )(p4, p4, p4, p4, rp, mask, o, lse, da, *(comm.ins if comm else []))
    return res[:2], res[2:]


def _final(x, g, target, name):
    rows, d = x.shape
    tr = ROW_BLOCK
    nblk = rows // tr

    def body(x_ref, g_ref, t_ref, loss_ref, dx_ref, dg_ref, acc_ref):
        i = pl.program_id(0)
        xv = x_ref[...]
        gv = g_ref[...]
        r = lax.rsqrt(jnp.mean(xv * xv, axis=-1, keepdims=True) + EPS)
        xn = xv * r
        err = xn * gv - t_ref[...]
        dy = err * (1.0 / d)
        dxn = dy * gv
        dx_ref[...] = r * (dxn - xn * jnp.mean(dxn * xn, axis=-1, keepdims=True))
        s_g = jnp.sum(dy * xn, axis=0, keepdims=True)
        s_l = jnp.sum(jnp.mean(err * err, axis=-1, keepdims=True), axis=0, keepdims=True)

        @pl.when(i == 0)
        def _():
            dg_ref[...] = s_g
            acc_ref[...] = s_l

        @pl.when(i > 0)
        def _():
            dg_ref[...] += s_g
            acc_ref[...] += s_l

        @pl.when(i == nblk - 1)
        def _():
            loss_ref[...] = jnp.broadcast_to(0.5 * acc_ref[...], loss_ref.shape)

    row = pl.BlockSpec((tr, d), lambda i: (i, 0))
    vec = pl.BlockSpec((1, d), lambda i: (0, 0))
    return pl.pallas_call(
        body, grid=(nblk,), in_specs=[row, vec, row],
        out_specs=[pl.BlockSpec((1, 128), lambda i: (0, 0)), row, vec],
        out_shape=[_sds((1, 128), F32), _sds((rows, d), F32), _sds((1, d), F32)],
        scratch_shapes=[pltpu.VMEM((1, 1), F32)], name=name, compiler_params=_cparams("arbitrary"),
    )(x, g, target)


def _as2d(a):
    if a.ndim == 1:
        return a.reshape(-1, 128) if a.shape[0] % 128 == 0 else a.reshape(1, -1)
    return a.reshape(-1, a.shape[-1])


def _adamw(w, g, m, v, name):
    shape = w.shape
    w2, g2, m2, v2 = (_as2d(t) for t in (w, g.reshape(shape), m, v))
    rows, cols = w2.shape
    tr = 512 if rows % 512 == 0 else rows
    c1 = 1.0 - ADAM_B1 ** ADAM_STEP
    c2 = 1.0 - ADAM_B2 ** ADAM_STEP

    def body(w_ref, g_ref, m_ref, v_ref, d_ref, nm_ref, nv_ref):
        gv = g_ref[...]
        nm = ADAM_B1 * m_ref[...] + (1.0 - ADAM_B1) * gv
        nv = ADAM_B2 * v_ref[...] + (1.0 - ADAM_B2) * (gv * gv)
        nm_ref[...] = nm
        nv_ref[...] = nv
        d_ref[...] = -ADAM_LR * ((nm / c1) / (jnp.sqrt(nv / c2) + ADAM_EPS) + ADAM_WD * w_ref[...])

    blk = pl.BlockSpec((tr, cols), lambda i: (i, 0))
    outs = _call(body, (w2, g2, m2, v2), grid=(rows // tr,), in_specs=[blk] * 4, out_specs=[blk] * 3,
                 out_shape=[_sds((rows, cols), F32)] * 3, name=name)
    return tuple(t.reshape(shape) for t in outs)


def _sum_lead(x, name, out_dtype=F32):
    n, rows, cols = x.shape
    tr = 512 if rows % 512 == 0 else rows

    def body(x_ref, o_ref):
        acc = x_ref[0].astype(F32)
        for k in range(1, n):
            acc = acc + x_ref[k].astype(F32)
        o_ref[...] = acc.astype(out_dtype)

    return pl.pallas_call(
        body, grid=(rows // tr,), in_specs=[pl.BlockSpec((n, tr, cols), lambda i: (0, i, 0))],
        out_specs=pl.BlockSpec((tr, cols), lambda i: (i, 0)), out_shape=_sds((rows, cols), out_dtype),
        name=name, compiler_params=_cparams("parallel"),
    )(x)


def _seg_vecs(mod_l, which, nseg):
    return mod_l[:nseg, which][:, None, :]


def _norm_grads(dshift, dgeff, dgate, g, scale):
    nseg, _, d = dshift.shape
    dmod = jnp.stack([dshift[:, 0], dgeff[:, 0] * g, dgate[:, 0]], axis=1)
    if nseg == 1:
        dmod = jnp.concatenate([dmod, jnp.zeros((1, 3, d), F32)], axis=0)
    dg = jnp.sum(dgeff[:, 0] * (1.0 + scale[:, 0]), axis=0)
    return dmod, dg


def _pool_layer(xin, g, mod_l, w_in, w_grp, w_out, pscale, nx, tag):
    rows = xin.shape[0]
    nseg = 2 if rows > nx else 1
    shift, scale, gate = (_seg_vecs(mod_l, k, nseg) for k in range(3))
    h, r, uv = _norm_w_in(xin, g, scale, shift, w_in, nx, f"w_in_fwd_{tag}")
    z, mixed, a = _pool_grp_fwd(uv, w_grp, pscale, nx, f"pool_fwd_{tag}")
    yx, xout = _w_out_resid(a, w_out, xin, gate, nx, f"w_out_fwd_{tag}")

    def backward(dxo, token=None):
        gate_b = gate if token is None else gate + token[0, 0]
        dyx, da, dgate = _gate_w_out_bwd(dxo, yx, gate_b, w_out, nx, f"w_out_bwd_{tag}")
        gw_out = _mm_tn(a, dyx, f"w_out_grad_{tag}", BF16)
        dm, duv, dscale = _pool_grp_bwd(da, mixed, uv, pscale, w_grp, nx, f"pool_bwd_{tag}")
        gw_grp = _grp_wgrad(z, dm, w_grp.shape[0], f"grp_grad_{tag}", BF16)
        gw_in = _mm_tn_parts(h, duv, f"w_in_grad_{tag}", BF16)
        dx, dshift, dgeff = _w_in_bwd_norm(duv, w_in, xin, r, g, scale, dxo, nx, f"w_in_bwd_{tag}")
        dmod, dg = _norm_grads(dshift, dgeff, dgate, g[0], scale)
        return dx, dmod, dg, dict(w_in=gw_in, w_grp=gw_grp, w_out=gw_out, scale=dscale)

    return xout, backward


def _na_layer(xc, g, mod_l, w_in, rpb, w_out, nx, mask, comm=None):
    nh, n_dr, n_dc = rpb.shape
    shift, scale = _seg_vecs(mod_l, 0, 2), _seg_vecs(mod_l, 1, 2)
    gate = _seg_vecs(mod_l, 2, 1)
    h, r, p4 = _norm_w_in(xc, g, scale, shift, w_in, nx, "w_in_fwd_na")
    rp = jnp.pad(rpb, ((0, 0), (1, _RP_ROWS - 1 - n_dr), (0, _PAIR - n_dc)))
    (a, o, lse), carried = _attn_fwd(p4, rp, mask, nx, "attn_fwd", comm)
    yx, xout = _w_out_resid(a, w_out, xc, gate, nx, "w_out_fwd_na")

    def backward(dxo, comm=None):
        dyx, da, dgate = _gate_w_out_bwd(dxo, yx, gate, w_out, nx, "w_out_bwd_na")
        gw_out = _mm_tn(a, dyx, "w_out_grad_na", BF16)
        (d4, drp), carried_bwd = _attn_bwd(p4, rp, mask, o, lse, da, nx, "attn_bwd", comm)
        gw_in = _mm_tn_parts(h, d4, "w_in_grad_na", BF16)
        dx, dshift, dgeff = _w_in_bwd_norm(d4, w_in, xc, r, g, scale, dxo, nx, "w_in_bwd_na")
        dgate2 = jnp.concatenate([dgate, jnp.zeros_like(dgate)], axis=0)
        dmod, dg = _norm_grads(dshift, dgeff, dgate2, g[0], scale)
        drpb = drp[:, 1:1 + n_dr, ::-1][:, :, :n_dc]
        return dx, dmod, dg, dict(w_in=gw_in, w_out=gw_out, rpb=drpb), carried_bwd

    return xout, backward, carried


def _conv_layer(xin, g, mod_l, w_in, dw, db, w_out):
    shift, scale, gate = (_seg_vecs(mod_l, k, 1) for k in range(3))
    nx = xin.shape[0]
    h, r, p4 = _norm_w_in(xin, g, scale, shift, w_in, nx, "w_in_fwd_conv")
    a = _conv_fwd(p4, dw, db, "conv_fwd")
    yx, xout = _w_out_resid(a, w_out, xin, gate, nx, "w_out_fwd_conv")

    def backward(dxo):
        dyx, da, dgate = _gate_w_out_bwd(dxo, yx, gate, w_out, nx, "w_out_bwd_conv")
        gw_out = _mm_tn(a, dyx, "w_out_grad_conv", BF16)
        d4, ddw, ddb = _conv_bwd(da, p4, dw, db, "conv_bwd")
        gw_in = _mm_tn_parts(h, d4, "w_in_grad_conv", BF16)
        dx, dshift, dgeff = _w_in_bwd_norm(d4, w_in, xin, r, g, scale, dxo, nx, "w_in_bwd_conv")
        dmod, dg = _norm_grads(dshift, dgeff, dgate, g[0], scale)
        return dx, dmod, dg, dict(w_in=gw_in, w_out=gw_out, dw=ddw, db=ddb)

    return xout, backward


def _example_step(x, ctx, target, mod, norm_g, final_g, wts, hooks=None):
    hooks = hooks or {}
    na_weights, late_comm, late_weights = (hooks.get(k) for k in ("na_weights", "late_comm", "late_weights"))
    nx = x.shape[0]
    consts = _attn_mask()
    g_rows = [norm_g[i:i + 1] for i in range(4)]
    xc0 = jnp.concatenate([x, ctx], axis=0)
    xc1, bwd0 = _pool_layer(xc0, g_rows[0], mod[0], wts["pool_w_in"][0], wts["pool_w_grp"][0],
                            wts["pool_w_out"][0], wts["pool_scale"][0:1], nx, "p0")
    if na_weights is not None:
        wts = {**wts, **na_weights(xc1)}
    x2, bwd1, carried = _na_layer(xc1, g_rows[1], mod[1], wts["na_w_in"], wts["na_rpb"], wts["na_w_out"], nx, consts,
                                  late_comm)
    if late_weights is not None:
        wts = {**wts, **late_weights(carried)}
    x3, bwd2 = _conv_layer(x2, g_rows[2], mod[2], wts["conv_w_in"], wts["conv_dw"], wts["conv_db"], wts["conv_w_out"])
    x4, bwd3 = _pool_layer(x3, g_rows[3], mod[3], wts["pool_w_in"][1], wts["pool_w_grp"][1], wts["pool_w_out"][1],
                           wts["pool_scale"][1:2], nx, "p3")
    loss, dx4, dfinal_g = _final(x4, final_g, target, "loss_head")
    call = lambda k, *args: hooks[k](*args) if k in hooks else None
    dx3, dmod3, dg3, gr3 = bwd3(dx4)
    dx2, dmod2, dg2, gr2 = bwd2(dx3)
    dxc1, dmod1, dg1, gr1, carried_bwd = bwd1(dx2, call("grad_comm", gr3, gr2))
    dxc0, dmod0, dg0, gr0 = bwd0(dxc1, call("na_grads_start", gr1))
    return dict(
        loss=loss, grad_x=dxc0[:nx], dmod=jnp.stack([dmod0, dmod1, dmod2, dmod3]),
        dnorm_g=jnp.stack([dg0, dg1, dg2, dg3]), dfinal_g=dfinal_g, layers=(gr0, gr1, gr2, gr3), carried=carried_bwd)


_AXES = ("x", "y", "c")
_CHIP_FLIPS = ((1, 0), (0, 1), (1, 1))


def _position():
    return tuple(lax.axis_index(a) for a in _AXES)


def _flipped(pos, flip):
    return tuple(1 - p if f else p for p, f in zip(pos, flip))


def _join_comms(comms):
    n_in = [len(c.ins) for c in comms]
    n_out = [len(c.outs) for c in comms]
    n_sem = [len(c.sems) for c in comms]

    def parts(ins, outs, sems):
        for k in range(len(comms)):
            a, b, s = sum(n_in[:k]), sum(n_out[:k]), sum(n_sem[:k])
            yield comms[k], (ins[a:a + n_in[k]], outs[b:b + n_out[k]], sems[s:s + n_sem[k]])

    def start(ins, outs, sems):
        for c, part in parts(ins, outs, sems):
            c.start(*part)

    def finish(ins, outs, sems):
        for c, part in parts(ins, outs, sems):
            c.finish(*part)

    joint = _Comm([a for c in comms for a in c.ins], [o for c in comms for o in c.outs],
                  [s for c in comms for s in c.sems], start, finish)
    return joint, lambda res: [list(res[sum(n_out[:k]):sum(n_out[:k + 1])]) for k in range(len(comms))]


def _run_comms(comms, name):
    joint, split = _join_comms(comms)

    def body(*refs):
        n_in, n_out = len(joint.ins), len(joint.outs)
        joint.start(refs[:n_in], refs[n_in:n_in + n_out], refs[n_in + n_out:])
        joint.finish(refs[:n_in], refs[n_in:n_in + n_out], refs[n_in + n_out:])

    res = pl.pallas_call(
        body, in_specs=[HBM_SPEC] * len(joint.ins), out_specs=[HBM_SPEC] * len(joint.outs), out_shape=joint.outs,
        scratch_shapes=joint.sems, name=name,
    )(*joint.ins)
    return split(res)


def _all_gather_comm(v, axes):
    flips = [f for f in np.ndindex(2, 2, 2) if any(f) and all(a in axes or not b for a, b in zip(_AXES, f))]
    n = len(flips) + 1

    def copies(ins, outs, sems):
        (v_ref,), (o_ref,), (send_sems, recv_sems, local_sem) = ins, outs, sems
        pos = _position()
        slot = 0
        for a, p in zip(_AXES, pos):
            if a in axes:
                slot = 2 * slot + p
        local = pltpu.make_async_copy(v_ref, o_ref.at[slot], local_sem)
        remote = [pltpu.make_async_remote_copy(v_ref, o_ref.at[slot], send_sems.at[k], recv_sems.at[k],
                                               device_id=_flipped(pos, flip), device_id_type=MESH)
                  for k, flip in enumerate(flips)]
        return [local] + remote

    def start(ins, outs, sems):
        for cp in copies(ins, outs, sems):
            cp.start()

    def finish(ins, outs, sems):
        for cp in copies(ins, outs, sems):
            cp.wait()

    sems = [pltpu.SemaphoreType.DMA((n - 1,)), pltpu.SemaphoreType.DMA((n - 1,)), pltpu.SemaphoreType.DMA(())]
    return _Comm([v], [_sds((n,) + v.shape, v.dtype)], sems, start, finish)


def _all_gather_two_level_comm(v):
    def copies(ins, outs, sems, onward):
        (v_ref,), (o_ref,), (send_sems, recv_sems, local_sem) = ins, outs, sems
        x, y, c = _position()
        sibling = (x, y, 1 - c)
        slot = lambda px, py, pc: o_ref.at[4 * px + 2 * py + pc]
        own = pltpu.make_async_copy(v_ref, slot(x, y, c), local_sem)
        first = [pltpu.make_async_remote_copy(v_ref, slot(x, y, c), send_sems.at[0], recv_sems.at[0],
                                              device_id=sibling, device_id_type=MESH)]
        fwd = []
        for k, flip in enumerate(_CHIP_FLIPS):
            px, py = _flipped((x, y), flip)
            first.append(pltpu.make_async_remote_copy(v_ref, slot(x, y, c), send_sems.at[1 + k], recv_sems.at[1 + k],
                                                      device_id=(px, py, c), device_id_type=MESH))
            if onward:
                fwd.append(pltpu.make_async_remote_copy(slot(px, py, c), slot(px, py, c), send_sems.at[4 + k],
                                                        recv_sems.at[4 + k], device_id=sibling, device_id_type=MESH))
        return own, first, fwd

    def start(ins, outs, sems):
        own, first, _ = copies(ins, outs, sems, False)
        for cp in [own] + first:
            cp.start()

    def finish(ins, outs, sems):
        own, first, fwd = copies(ins, outs, sems, True)
        for arrived, onward in zip(first[1:], fwd):
            arrived.wait_recv()
            onward.start()
        first[0].wait_recv()
        for cp in fwd:
            cp.wait_recv()
        for cp in first + fwd:
            cp.wait_send()
        own.wait()

    sems = [pltpu.SemaphoreType.DMA((7,)), pltpu.SemaphoreType.DMA((7,)), pltpu.SemaphoreType.DMA(())]
    return _Comm([v], [_sds((8,) + v.shape, v.dtype)], sems, start, finish)


def _all_gather(v, axes, name):
    return _run_comms([_all_gather_comm(v, axes)], name)[0][0]


class _Item:
    def __init__(self, key, layer, shape, shard_axis, half_axis):
        self.key, self.layer, self.shape = key, layer, tuple(shape)
        self.shard_axis, self.half_axis = shard_axis, half_axis
        self.shard = shape[shard_axis] // 4
        self.half = shape[half_axis] // 2

    def sized(self, shard=False, half=False):
        s = list(self.shape)
        if shard:
            s[self.shard_axis] = self.shard
        if half:
            s[self.half_axis] = self.half
        return tuple(s)

    def window(self, ref, chip=None, half=None):
        idx = [slice(None)] * len(self.shape)
        if chip is not None:
            idx[self.shard_axis] = pl.ds(chip * self.shard, self.shard)
        if half is not None:
            idx[self.half_axis] = pl.ds(half * self.half, self.half)
        return ref.at[tuple(idx)]


def _items(d, w):
    out = []
    for j in range(2):
        out += [_Item("pool_w_in", j, (d, 2 * w), 1, 0), _Item("pool_w_grp", j, (4, w // 4, w // 4), 1, 0),
                _Item("pool_w_out", j, (w, d), 0, 1)]
    out += [_Item("na_w_in", 0, (d, 4 * w), 1, 0), _Item("na_w_out", 0, (w, d), 0, 1),
            _Item("conv_w_in", 0, (d, 4 * w), 1, 0), _Item("conv_w_out", 0, (w, d), 0, 1)]
    return out


def _gather_comm(shards, items):
    n = len(items)

    def copies(src, dst, sems, onward):
        send_a, recv_a, send_b, recv_b, send_c, recv_c = sems
        x, y, c = _position()
        chip = 2 * x + y
        sibling = (x, y, 1 - c)
        own, out, fwd, fwd_in = [], [], [], []
        for i, it in enumerate(items):
            own.append(pltpu.make_async_remote_copy(src[i], it.window(dst[i], chip=chip), send_c.at[i], recv_c.at[i],
                                                    device_id=sibling, device_id_type=MESH))
            for k, flip in enumerate(_CHIP_FLIPS):
                px, py = _flipped((x, y), flip)
                s = 3 * i + k
                out.append(pltpu.make_async_remote_copy(
                    it.window(src[i], half=c), it.window(dst[i], chip=chip, half=c), send_a.at[s], recv_a.at[s],
                    device_id=(px, py, c), device_id_type=MESH))
                if onward:
                    got = it.window(dst[i], chip=2 * px + py, half=c)
                    fwd.append(pltpu.make_async_remote_copy(got, got, send_b.at[s], recv_b.at[s],
                                                            device_id=sibling, device_id_type=MESH))
                    other = it.window(dst[i], chip=2 * px + py, half=1 - c)
                    fwd_in.append(pltpu.make_async_remote_copy(other, other, send_b.at[s], recv_b.at[s],
                                                               device_id=sibling, device_id_type=MESH))
        return own, out, fwd, fwd_in

    def start(src, dst, sems):
        own, out, _, _ = copies(src, dst, sems, False)
        for cp in own + out:
            cp.start()

    def finish(src, dst, sems):
        own, out, fwd, fwd_in = copies(src, dst, sems, True)
        for arrived, onward in zip(out, fwd):
            arrived.wait_recv()
            onward.start()
        for cp in fwd_in:
            cp.wait_recv()
        for cp in out + fwd:
            cp.wait_send()
        for cp in own:
            cp.wait()

    sems = [pltpu.SemaphoreType.DMA((3 * n,)) for _ in range(4)] + [pltpu.SemaphoreType.DMA((n,)) for _ in range(2)]
    return _Comm(shards, [_sds(it.shape, BF16) for it in items], sems, start, finish)


def _pair_swap_copies(windows):
    def copies(src, got, sems):
        send_sems, recv_sems = sems
        x, y, c = _position()
        return [pltpu.make_async_remote_copy(windows[i](src[i], 1 - c), got[i], send_sems.at[i], recv_sems.at[i],
                                             device_id=(x, y, 1 - c), device_id_type=MESH)
                for i in range(len(windows))]

    return copies


def _pair_swap_comm(arrays, windows, out_shapes):
    n = len(arrays)
    copies = _pair_swap_copies(windows)

    def start(src, got, sems):
        for cp in copies(src, got, sems):
            cp.start()

    def finish(src, got, sems):
        for cp in copies(src, got, sems):
            cp.wait()

    return _Comm(arrays, out_shapes, [pltpu.SemaphoreType.DMA((n,)), pltpu.SemaphoreType.DMA((n,))], start, finish)


def _pair_swap(arrays, windows, out_shapes, name):
    return _run_comms([_pair_swap_comm(arrays, windows, out_shapes)], name)[0]


def _chip_exchange_copies(items):
    def copies(src, dst, sems):
        send_sems, recv_sems = sems
        x, y, c = _position()
        out = []
        for i, it in enumerate(items):
            for k, flip in enumerate(_CHIP_FLIPS):
                px, py = _flipped((x, y), flip)
                out.append(pltpu.make_async_remote_copy(
                    it.window(src[i], chip=2 * px + py), dst[i].at[k], send_sems.at[3 * i + k],
                    recv_sems.at[3 * i + k], device_id=(px, py, c), device_id_type=MESH))
        return out

    return copies


_SEM_SPEC = pl.BlockSpec(memory_space=pltpu.SEMAPHORE)
_DATAFLOW = pltpu.SideEffectType.DATAFLOW_SIDE_EFFECTING


def _split_start(copies, srcs, zones, n_copies, name):
    n, nz = len(srcs), len(zones)

    def body(*refs):
        src, land = refs[:n], refs[n:n + nz]
        send_sems, recv_sems = refs[n + nz:n + nz + 2]
        token = refs[-1]
        for cp in copies(src, land, (send_sems, recv_sems)):
            cp.start()
        token[...] = jnp.zeros(token.shape, F32)

    hbm = lambda t: pltpu.HBM(t.shape, t.dtype)
    res = pl.pallas_call(
        body, name=name,
        out_shape=(pltpu.SemaphoreType.DMA((n_copies,)), pltpu.SemaphoreType.DMA((n_copies,)),
                   *[hbm(t) for t in list(srcs) + list(zones)], _sds((8, 128), F32)),
        in_specs=[HBM_SPEC] * (n + nz),
        out_specs=(_SEM_SPEC, _SEM_SPEC, *[HBM_SPEC] * (n + nz), pl.BlockSpec(memory_space=pltpu.VMEM)),
        input_output_aliases={i: 2 + i for i in range(n + nz)},
        compiler_params=pltpu.CompilerParams(has_side_effects=_DATAFLOW),
    )(*[pltpu.with_memory_space_constraint(t, pltpu.HBM) for t in list(srcs) + list(zones)])
    return (res[0], res[1], list(res[2:2 + n]), list(res[2 + n:2 + n + nz])), res[-1]


def _split_wait(copies, handle, after, name):
    send_sems, recv_sems, srcs, zones = handle
    n, nz = len(srcs), len(zones)

    def body(*refs):
        src, land = refs[:n], refs[n:n + nz]
        send, recv = refs[n + nz:n + nz + 2]
        for cp in copies(src, land, (send, recv)):
            cp.wait_send()
            cp.wait_recv()

    hbm = lambda t: pltpu.HBM(t.shape, t.dtype)
    res = pl.pallas_call(
        body, name=name, out_shape=tuple(hbm(t) for t in list(srcs) + list(zones)),
        in_specs=[HBM_SPEC] * (n + nz) + [_SEM_SPEC, _SEM_SPEC, pl.BlockSpec(memory_space=pl.ANY)],
        out_specs=tuple([HBM_SPEC] * (n + nz)), input_output_aliases={i: i for i in range(n + nz)},
        compiler_params=pltpu.CompilerParams(has_side_effects=_DATAFLOW),
    )(*srcs, *zones, send_sems, recv_sems, after)
    return list(res[:n]), list(res[n:])


def _gather_ici_copies(items):
    def copies(src, dst, sems):
        send_sems, recv_sems = sems
        x, y, c = _position()
        chip = 2 * x + y
        out = []
        for i, it in enumerate(items):
            for k, flip in enumerate(_CHIP_FLIPS):
                px, py = _flipped((x, y), flip)
                out.append(pltpu.make_async_remote_copy(
                    it.window(src[i], half=c), it.window(dst[i], chip=chip, half=c), send_sems.at[3 * i + k],
                    recv_sems.at[3 * i + k], device_id=(px, py, c), device_id_type=MESH))
        return out

    return copies


def _gather_pair_finish(shards, mats, items, name):
    n = len(items)

    def body(*refs):
        src, dst = refs[:n], refs[2 * n:3 * n]
        send_own, recv_own, send_fwd, recv_fwd = refs[3 * n:]
        x, y, c = _position()
        chip = 2 * x + y
        sibling = (x, y, 1 - c)
        copies = []
        for i, it in enumerate(items):
            copies.append(pltpu.make_async_remote_copy(src[i], it.window(dst[i], chip=chip), send_own.at[i],
                                                       recv_own.at[i], device_id=sibling, device_id_type=MESH))
            for k, flip in enumerate(_CHIP_FLIPS):
                px, py = _flipped((x, y), flip)
                got = it.window(dst[i], chip=2 * px + py, half=c)
                copies.append(pltpu.make_async_remote_copy(got, got, send_fwd.at[3 * i + k], recv_fwd.at[3 * i + k],
                                                           device_id=sibling, device_id_type=MESH))
        for cp in copies:
            cp.start()
        for cp in copies:
            cp.wait()

    return pl.pallas_call(
        body, in_specs=[HBM_SPEC] * (2 * n), out_specs=[HBM_SPEC] * n, out_shape=[_sds(it.shape, BF16) for it in items],
        input_output_aliases={n + i: i for i in range(n)},
        scratch_shapes=[pltpu.SemaphoreType.DMA((n,)), pltpu.SemaphoreType.DMA((n,)),
                        pltpu.SemaphoreType.DMA((3 * n,)), pltpu.SemaphoreType.DMA((3 * n,))], name=name,
    )(*shards, *mats)


def _chip_exchange_comm(partials, items):
    n = len(items)
    copies = _chip_exchange_copies(items)

    def start(src, dst, sems):
        for cp in copies(src, dst, sems):
            cp.start()

    def finish(src, dst, sems):
        for cp in copies(src, dst, sems):
            cp.wait()

    return _Comm(partials, [_sds((3,) + it.sized(shard=True, half=True), BF16) for it in items],
                 [pltpu.SemaphoreType.DMA((3 * n,)), pltpu.SemaphoreType.DMA((3 * n,))], start, finish)


_SUM_STEPS = 2


def _pair_sums(gs, gots, its, pos, name):
    n = len(its)
    nb = _SUM_STEPS
    g2 = [g.reshape(-1, g.shape[-1]) for g in gs]
    got2 = [t.reshape(-1, t.shape[-1]) for t in gots]

    def body(pos_ref, *refs):
        for g_ref, got_ref, o_ref in zip(refs[:n], refs[n:2 * n], refs[2 * n:]):
            o_ref[...] = (g_ref[...].astype(F32) + got_ref[...].astype(F32)).astype(BF16)

    g_specs, got_specs = [], []
    for it, t in zip(its, got2):
        rows, cols = t.shape
        blk = (rows // nb, cols)
        g_map = (lambda i, pos: (pos[1] * nb + i, 0)) if it.half_axis == 0 else (lambda i, pos: (i, pos[1]))
        g_specs.append(pl.BlockSpec(blk, g_map))
        got_specs.append(pl.BlockSpec(blk, lambda i, pos: (i, 0)))
    outs = pl.pallas_call(
        body, grid_spec=pltpu.PrefetchScalarGridSpec(
            num_scalar_prefetch=1, grid=(nb,), in_specs=g_specs + got_specs, out_specs=got_specs),
        out_shape=[_sds(t.shape, BF16) for t in got2], name=name, compiler_params=_cparams("parallel"),
    )(pos, *g2, *got2)
    return [o.reshape(t.shape) for o, t in zip(outs, gots)]


_FLIP_SLOT = {2: 0, 1: 1, 3: 2}


def _chip_sums(pairs, slots, its, pos, name):
    n = len(its)
    nb = _SUM_STEPS

    def body(pos_ref, *refs):
        chip = pos_ref[0]
        for own in range(4):
            @pl.when(chip == own)
            def _():
                for p_ref, s_ref, o_ref in zip(refs[:n], refs[n:2 * n], refs[2 * n:]):
                    acc = None
                    for k in range(4):
                        v = (p_ref[...] if k == own else s_ref[_FLIP_SLOT[own ^ k]]).astype(F32)
                        acc = v if acc is None else acc + v
                    o_ref[...] = acc

    p_specs, s_specs, o_specs, shapes = [], [], [], []
    for it in its:
        shape = it.sized(shard=True, half=True)
        blk = (shape[0] // nb,) + shape[1:]
        rest = (0,) * (len(shape) - 1)

        def p_map(i, pos, it=it, nd=len(shape)):
            lead = i + (pos[0] * nb if it.shard_axis == 0 else 0)
            return (lead,) + tuple(pos[0] if ax == it.shard_axis else 0 for ax in range(1, nd))

        p_specs.append(pl.BlockSpec(blk, p_map))
        s_specs.append(pl.BlockSpec((3,) + blk, lambda i, pos, rest=rest: (0, i) + rest))
        o_specs.append(pl.BlockSpec(blk, lambda i, pos, rest=rest: (i,) + rest))
        shapes.append(_sds(shape, F32))
    return pl.pallas_call(
        body, grid_spec=pltpu.PrefetchScalarGridSpec(
            num_scalar_prefetch=1, grid=(nb,), in_specs=p_specs + s_specs, out_specs=o_specs),
        out_shape=shapes, name=name, compiler_params=_cparams("parallel"),
    )(pos, *pairs, *slots)


_GRAD_KEYS = ("pool_w_in", "pool_w_grp", "pool_w_out", "na_w_in", "na_w_out", "conv_w_in", "conv_w_out")


def _adamw_matrix(w, m, v, owns, others, it, pos, name):
    nl = w.shape[0]
    rows_split = it.half_axis == 0
    r, cdim = int(np.prod(w.shape[1:-1])), w.shape[-1]
    hr, hc = (r // 2, cdim) if rows_split else (r, cdim // 2)
    br = min(hr, 256)
    nb = hr // br
    c1 = 1.0 - ADAM_B1 ** ADAM_STEP
    c2 = 1.0 - ADAM_B2 ** ADAM_STEP

    def body(pos_ref, w_ref, m_ref, v_ref, *rest):
        own_refs, other_refs = rest[:nl], rest[nl:2 * nl]
        g_ref, d_ref, nm_ref, nv_ref = rest[2 * nl:]
        j, h = pl.program_id(0), pl.program_id(1)
        own, other = own_refs[0][...], other_refs[0][...]
        for q in range(1, nl):
            own = jnp.where(j == q, own_refs[q][...], own)
            other = jnp.where(j == q, other_refs[q][...], other)
        gv = jnp.where(h == pos_ref[1], own, other)
        nm = ADAM_B1 * m_ref[...] + (1.0 - ADAM_B1) * gv
        nv = ADAM_B2 * v_ref[...] + (1.0 - ADAM_B2) * (gv * gv)
        g_ref[...] = gv
        nm_ref[...] = nm
        nv_ref[...] = nv
        d_ref[...] = -ADAM_LR * ((nm / c1) / (jnp.sqrt(nv / c2) + ADAM_EPS) + ADAM_WD * w_ref[...])

    if rows_split:
        full = pl.BlockSpec((None, br, hc), lambda j, h, i, pos: (j, h * nb + i, 0))
    else:
        full = pl.BlockSpec((None, br, hc), lambda j, h, i, pos: (j, i, h))
    half = pl.BlockSpec((br, hc), lambda j, h, i, pos: (i, 0))
    flat = lambda t: t.reshape(nl, r, cdim)
    outs = pl.pallas_call(
        body, grid_spec=pltpu.PrefetchScalarGridSpec(
            num_scalar_prefetch=1, grid=(nl, 2, nb), in_specs=[full] * 3 + [half] * (2 * nl), out_specs=[full] * 4),
        out_shape=[_sds((nl, r, cdim), F32)] * 4, name=name,
        compiler_params=_cparams("parallel", "parallel", "parallel"),
    )(pos, flat(w), flat(m), flat(v), *[t.reshape(hr, hc) for t in list(owns) + list(others)])
    return tuple(t.reshape(w.shape) for t in outs)


_WEIGHTS = ("c_ctx", "norm_g", "ada_w", "ada_b", "pool_w_in", "pool_w_grp", "pool_scale", "pool_w_out", "na_w_in",
            "na_rpb", "na_w_out", "conv_w_in", "conv_dw", "conv_db", "conv_w_out", "final_g")
_COND_ROWS = 16


def _modulations(cond, ada_w, ada_b_cols):
    nl, d, n = ada_w.shape
    return _matmul(
        cond, ada_w, mode="nn", grid=(nl, 1), a_silu=True, epilogue="bias",
        a_spec=pl.BlockSpec((_COND_ROWS, d), lambda i, j: (0, 0)), b_spec=pl.BlockSpec((None, d, n), lambda i, j: (i, 0, 0)),
        extra=(ada_b_cols,), extra_specs=(pl.BlockSpec((None, 1, n), lambda i, j: (i, 0, 0)),),
        out_shapes=[_sds((nl, _COND_ROWS, n), F32)], out_specs=[pl.BlockSpec((None, _COND_ROWS, n), lambda i, j: (i, 0, 0))],
        name="modulations")[0]


def _ada_w_step(cond, dm_cols, w, m, v):
    nl, d, n = w.shape
    tr = d // 2
    c1 = 1.0 - ADAM_B1 ** ADAM_STEP
    c2 = 1.0 - ADAM_B2 ** ADAM_STEP

    def body(c_ref, dm_ref, w_ref, m_ref, v_ref, g_ref, d_ref, nm_ref, nv_ref):
        gv = lax.dot_general(_silu(c_ref[...]).astype(BF16), dm_ref[...].astype(BF16), _DIMS["tn"],
                             preferred_element_type=F32)
        nm = ADAM_B1 * m_ref[...] + (1.0 - ADAM_B1) * gv
        nv = ADAM_B2 * v_ref[...] + (1.0 - ADAM_B2) * (gv * gv)
        g_ref[...] = gv
        nm_ref[...] = nm
        nv_ref[...] = nv
        d_ref[...] = -ADAM_LR * ((nm / c1) / (jnp.sqrt(nv / c2) + ADAM_EPS) + ADAM_WD * w_ref[...])

    blk = pl.BlockSpec((None, tr, n), lambda l, i: (l, i, 0))
    return _call(
        body, (cond, dm_cols, w, m, v), grid=(nl, d // tr),
        in_specs=[pl.BlockSpec((_COND_ROWS, tr), lambda l, i: (0, i)),
                  pl.BlockSpec((None, _COND_ROWS, n), lambda l, i: (l, 0, 0)), blk, blk, blk],
        out_specs=[blk] * 4, out_shape=[_sds(w.shape, F32)] * 4, name="adamw_ada_w")


def _cond_grad(dm_cols, ada_w):
    nl, d, n = ada_w.shape
    return _matmul(
        dm_cols, ada_w, mode="nt", grid=(1, nl), nk=nl, acc_shape=(_COND_ROWS, d),
        a_spec=pl.BlockSpec((None, _COND_ROWS, n), lambda i, q: (q, 0, 0)), b_spec=pl.BlockSpec((None, d, n), lambda i, q: (q, 0, 0)),
        out_shapes=[_sds((_COND_ROWS, d), F32)], out_specs=[pl.BlockSpec((_COND_ROWS, d), lambda i, q: (0, 0))],
        name="cond_grad")[0]


def _pack(parts):
    flat = [p.reshape(-1) for p in parts]
    sizes = [f.shape[0] for f in flat]
    total = sum(sizes)
    rows = -(-total // 1024) * 8
    packed = jnp.concatenate(flat + [jnp.zeros((rows * 128 - total,), F32)]).reshape(rows, 128)
    offs = np.concatenate([[0], np.cumsum(sizes)])[:-1]
    return packed, [(int(o), p.shape) for o, p in zip(offs, parts)]


def _unpack(flat, layout, k):
    off, shape = layout[k]
    return flat[..., off:off + int(np.prod(shape))].reshape(flat.shape[:-1] + tuple(shape))


def kernel(x, c, ctx, c_ctx, norm_g, ada_w, ada_b, pool_w_in, pool_w_grp, pool_scale, pool_w_out, na_w_in, na_rpb, na_w_out, conv_w_in, conv_dw, conv_db, conv_w_out, final_g, loss_target, m_c_ctx, m_norm_g, m_ada_w, m_ada_b, m_pool_w_in, m_pool_w_grp, m_pool_scale, m_pool_w_out, m_na_w_in, m_na_rpb, m_na_w_out, m_conv_w_in, m_conv_dw, m_conv_db, m_conv_w_out, m_final_g, v_c_ctx, v_norm_g, v_ada_w, v_ada_b, v_pool_w_in, v_pool_w_grp, v_pool_scale, v_pool_w_out, v_na_w_in, v_na_rpb, v_na_w_out, v_conv_w_in, v_conv_dw, v_conv_db, v_conv_w_out, v_final_g):
    params = dict(c_ctx=c_ctx, norm_g=norm_g, ada_w=ada_w, ada_b=ada_b, pool_w_in=pool_w_in, pool_w_grp=pool_w_grp,
                  pool_scale=pool_scale, pool_w_out=pool_w_out, na_w_in=na_w_in, na_rpb=na_rpb, na_w_out=na_w_out,
                  conv_w_in=conv_w_in, conv_dw=conv_dw, conv_db=conv_db, conv_w_out=conv_w_out, final_g=final_g)
    mom1 = dict(c_ctx=m_c_ctx, norm_g=m_norm_g, ada_w=m_ada_w, ada_b=m_ada_b, pool_w_in=m_pool_w_in,
                pool_w_grp=m_pool_w_grp, pool_scale=m_pool_scale, pool_w_out=m_pool_w_out, na_w_in=m_na_w_in,
                na_rpb=m_na_rpb, na_w_out=m_na_w_out, conv_w_in=m_conv_w_in, conv_dw=m_conv_dw, conv_db=m_conv_db,
                conv_w_out=m_conv_w_out, final_g=m_final_g)
    mom2 = dict(c_ctx=v_c_ctx, norm_g=v_norm_g, ada_w=v_ada_w, ada_b=v_ada_b, pool_w_in=v_pool_w_in,
                pool_w_grp=v_pool_w_grp, pool_scale=v_pool_scale, pool_w_out=v_pool_w_out, na_w_in=v_na_w_in,
                na_rpb=v_na_rpb, na_w_out=v_na_w_out, conv_w_in=v_conv_w_in, conv_dw=v_conv_dw, conv_db=v_conv_db,
                conv_w_out=v_conv_w_out, final_g=v_final_g)
    d = x.shape[-1]
    w = na_w_out.shape[1] * 4
    xi, yi, ci = _position()
    chip = 2 * xi + yi
    dev = 2 * chip + ci
    n_ada = ada_w.shape[-1]

    def chip_cols(a, size):
        return lax.dynamic_slice_in_dim(a, chip * size, size, axis=a.ndim - 1)

    items = _items(d, w)
    first = [it for it in items if it.key.startswith("pool") and it.layer == 0]
    na = [it for it in items if it.key.startswith("na")]
    late = [it for it in items if it not in first + na]
    shards_of = lambda its: [params[it.key][it.layer].astype(BF16) for it in its]
    empties = lambda its: [lax.empty(it.shape, BF16) for it in its]
    first_copies, na_copies = _gather_ici_copies(first), _gather_ici_copies(na)

    conds = _all_gather(c.reshape(8, d // 8), _AXES, "gather_cond").reshape(8, d)
    behind = conds[0, 0] * 0.0
    first_handle, token = _split_start(first_copies, [s + behind.astype(BF16) for s in shards_of(first)],
                                       empties(first), 3 * len(first), "gather_first_start")
    cond = jnp.concatenate([conds + token[0, 0], c_ctx[None], jnp.zeros((_COND_ROWS - 9, d), F32)], axis=0)
    mod_cols = _modulations(cond, ada_w, chip_cols(ada_b, n_ada)[:, None, :])
    small_pack, small_layout = _pack([pool_scale, conv_dw, conv_db])
    (mod_all,), (small,) = _run_comms([_all_gather_comm(mod_cols, ("x", "y")),
                                       _all_gather_comm(small_pack, ("x", "y"))], "gather_mod")
    behind = mod_all[0, 0, 0, 0] * 0.0
    na_handle, token = _split_start(na_copies, [s + behind.astype(BF16) for s in shards_of(na)], empties(na),
                                    3 * len(na), "gather_na_start")
    first_shards, first_mats = _split_wait(first_copies, first_handle, token, "gather_first_wait")
    first_mats = _gather_pair_finish(first_shards, first_mats, first, "gather_first_pair")
    mod_all = mod_all.transpose(1, 2, 0, 3).reshape(4, _COND_ROWS, 3, d)
    mod = jnp.stack([lax.dynamic_index_in_dim(mod_all, dev, axis=1, keepdims=False), mod_all[:, 8]], axis=1)
    full = {(it.key, it.layer): mat for it, mat in zip(first, first_mats)}
    late_comm = _gather_comm(shards_of(late), late)

    def na_weights(after):
        na_shards, na_mats = _split_wait(na_copies, na_handle, after, "gather_na_wait")
        na_mats = _gather_pair_finish(na_shards, na_mats, na, "gather_na_pair")
        return {it.key: mat for it, mat in zip(na, na_mats)}

    def late_weights(mats):
        full.update({(it.key, it.layer): mat for it, mat in zip(late, mats)})
        return dict(pool_w_in=[full[("pool_w_in", j)] for j in range(2)],
                    pool_w_grp=[full[("pool_w_grp", j)] for j in range(2)],
                    pool_w_out=[full[("pool_w_out", j)] for j in range(2)],
                    conv_w_in=full[("conv_w_in", 0)], conv_w_out=full[("conv_w_out", 0)])

    small = small.reshape(4, -1)

    def whole(k):
        parts = _unpack(small, small_layout, k)
        return jnp.moveaxis(parts, 0, -2).reshape(parts.shape[1:-1] + (-1,))

    wts = dict(pool_w_in=[full[("pool_w_in", 0)]], pool_w_grp=[full[("pool_w_grp", 0)]],
               pool_w_out=[full[("pool_w_out", 0)]], pool_scale=whole(0), na_rpb=na_rpb[0], conv_dw=whole(1)[0],
               conv_db=whole(2))
    pos = jnp.stack([chip, ci]).astype(jnp.int32)

    def layer_grads(its, by_layer):
        pick = {"pool_w_in": "w_in", "pool_w_grp": "w_grp", "pool_w_out": "w_out", "na_w_in": "w_in",
                "na_w_out": "w_out", "conv_w_in": "w_in", "conv_w_out": "w_out"}
        return [by_layer[(it.key.split("_")[0], it.layer)][pick[it.key]] for it in its]

    pairs, handles = dict(), dict()
    half_windows = lambda its: [(lambda ref, half, it=it: it.window(ref, half=half)) for it in its]
    half_shapes = lambda its: [_sds(it.sized(half=True), BF16) for it in its]

    def pair_sums(its, mats, tag):
        got = _pair_swap(mats, half_windows(its), half_shapes(its), f"pair_exchange_{tag}")
        return _pair_sums(mats, got, its, pos, f"pair_sum_{tag}")

    def grad_comm(gr3, gr2):
        pairs["late"] = pair_sums(late, layer_grads(late, {("pool", 1): gr3, ("conv", 0): gr2}), "late")
        return _chip_exchange_comm(pairs["late"], late)

    slot_zones = lambda its: [lax.empty((3,) + it.sized(shard=True, half=True), BF16) for it in its]
    na_xcopies, first_xcopies = _chip_exchange_copies(na), _chip_exchange_copies(first)

    def na_grads_start(gr1):
        pairs["na"] = pair_sums(na, layer_grads(na, {("na", 0): gr1}), "na")
        handles["na"], started = _split_start(na_xcopies, pairs["na"], slot_zones(na), 3 * len(na),
                                              "exchange_na_start")
        return started

    res = _example_step(x[0], ctx[0], loss_target[0], mod, norm_g, final_g[None], wts, dict(
        na_weights=na_weights, late_comm=late_comm, late_weights=late_weights, grad_comm=grad_comm,
        na_grads_start=na_grads_start))
    g0, g1, g2, g3 = res["layers"]
    pairs["na"], na_slots = _split_wait(na_xcopies, handles["na"], g0["w_in"], "exchange_na_wait")
    first_grads = layer_grads(first, {("pool", 0): g0})
    packed, layout = _pack([res["dfinal_g"], res["dnorm_g"], res["dmod"], g1["rpb"],
                            jnp.concatenate([g0["scale"], g3["scale"]], axis=0), g2["dw"], g2["db"],
                            res["loss"][0, :1]])
    first_got, (every,) = _run_comms([_pair_swap_comm(first_grads, half_windows(first), half_shapes(first)),
                                      _all_gather_two_level_comm(packed)], "pair_exchange_first")
    pairs["first"] = _pair_sums(first_grads, first_got, first, pos, "pair_sum_first")

    grads = dict()
    total = _sum_lead(every, "sum_vec_grads").reshape(-1)
    every = every.reshape(8, -1)
    grads["final_g"] = _unpack(total, layout, 0).reshape(final_g.shape)
    grads["norm_g"] = _unpack(total, layout, 1)
    grads["na_rpb"] = _unpack(total, layout, 3)[None]
    grads["pool_scale"] = chip_cols(_unpack(total, layout, 4), pool_scale.shape[-1])
    grads["conv_dw"] = chip_cols(_unpack(total, layout, 5), conv_dw.shape[-1])[None]
    grads["conv_db"] = chip_cols(_unpack(total, layout, 6), conv_db.shape[-1])
    dmod_sum = _unpack(total, layout, 2).reshape(4, 2, 3 * d)
    dmod_each = _unpack(every, layout, 2).reshape(8, 4, 2, 3 * d)
    grads["ada_b"] = dmod_sum[:, 0] + dmod_sum[:, 1]
    dm = jnp.concatenate([dmod_each[:, :, 0].transpose(1, 0, 2), dmod_sum[:, 1][:, None],
                          jnp.zeros((4, _COND_ROWS - 9, 3 * d), F32)], axis=1)
    dm_cols = chip_cols(dm, n_ada)
    dcond = _cond_grad(dm_cols, ada_w)[8].reshape(8, d // 8)
    dcond_all = _all_gather(dcond, ("x", "y"), "gather_cond_grad")
    behind = dcond_all[0, 0, 0] * 0.0
    handles["first"], token = _split_start(first_xcopies, [p + behind.astype(BF16) for p in pairs["first"]],
                                           slot_zones(first), 3 * len(first), "exchange_first_start")
    grads["ada_w"], *ada_w_step = _ada_w_step(cond, dm_cols + token[0, 0], ada_w, m_ada_w, v_ada_w)
    grads["c_ctx"] = _sum_lead(dcond_all, "sum_cond_grad").reshape(d) * _dsilu(c_ctx)
    vector_out = {k: _adamw(params[k], grads[k], mom1[k], mom2[k], f"adamw_{k}")
                  for k in _WEIGHTS if k not in _GRAD_KEYS + ("ada_w",)}
    vector_out["ada_w"] = tuple(ada_w_step)
    pairs["first"], first_slots = _split_wait(first_xcopies, handles["first"], vector_out["ada_w"][2],
                                              "exchange_first_wait")

    slots = dict(zip(late, res["carried"]))
    slots.update(zip(first, first_slots))
    slots.update(zip(na, na_slots))
    pair_of = dict(zip(late, pairs["late"]))
    pair_of.update(zip(first, pairs["first"]))
    pair_of.update(zip(na, pairs["na"]))
    reduced = _chip_sums([pair_of[it] for it in items], [slots[it] for it in items], items, pos, "chip_sum")
    theirs = _pair_swap(reduced, [lambda ref, half: ref] * len(items),
                        [_sds(t.shape, F32) for t in reduced], "pair_return")
    matrix_out = dict()
    for k in _GRAD_KEYS:
        idx = [i for i, it in enumerate(items) if it.key == k]
        res_k = _adamw_matrix(params[k], mom1[k], mom2[k], [reduced[i] for i in idx], [theirs[i] for i in idx],
                              items[idx[0]], pos, f"adamw_{k}")
        grads[k], matrix_out[k] = res_k[0], res_k[1:]

    outs = [[], [], []]
    for k in _WEIGHTS:
        step = matrix_out[k] if k in matrix_out else vector_out[k]
        for lst, val in zip(outs, step):
            lst.append(val)
    loss = _unpack(total, layout, 7)[0]
    return (loss, res["grad_x"][None], *[grads[k].reshape(params[k].shape) for k in _WEIGHTS],
            *outs[0], *outs[1], *outs[2])
```

```python
import functools

import numpy as np
import jax
import jax.numpy as jnp
from jax import lax
from jax.experimental import pallas as pl
from jax.experimental.pallas import tpu as pltpu

F32 = jnp.float32
BF16 = jnp.bfloat16

EPS = 1e-6
GRID_W = 64
HEAD_DIM = 64
WIN_ROWS = 8
WIN_COLS = 16
POOL_WINDOWS = (2, 4, 8, 16)
Q_ROWS = 4
K_ROWS = 12
PAD_ROWS = 4
NEG = -1e30

ADAM_LR = 0.001
ADAM_B1 = 0.9
ADAM_B2 = 0.999
ADAM_EPS = 1e-08
ADAM_WD = 0.01
ADAM_STEP = 10

ROW_BLOCK = 256
VMEM_LIMIT = 56 * 1024 * 1024
ACT = BF16

MESH = pl.DeviceIdType.MESH
HBM_SPEC = pl.BlockSpec(memory_space=pltpu.HBM)


def _cparams(*sem):
    return pltpu.CompilerParams(dimension_semantics=sem or None, vmem_limit_bytes=VMEM_LIMIT)


def _sds(shape, dtype):
    return jax.ShapeDtypeStruct(tuple(shape), dtype)


def _call(body, args, *, grid, in_specs, out_specs, out_shape, name, scratch_shapes=()):
    return list(pl.pallas_call(
        body, grid=grid, in_specs=list(in_specs), out_specs=list(out_specs), out_shape=list(out_shape),
        scratch_shapes=list(scratch_shapes), name=name, compiler_params=_cparams(*(("arbitrary",) * len(grid))),
    )(*args))


def _sigmoid(x):
    return 1.0 / (1.0 + jnp.exp(-x))


def _silu(x):
    return x * _sigmoid(x)


def _dsilu(x):
    s = _sigmoid(x)
    return s * (1.0 + x * (1.0 - s))


_DIMS = {
    "nn": (((1,), (0,)), ((), ())),
    "nt": (((1,), (1,)), ((), ())),
    "tn": (((0,), (0,)), ((), ())),
}


def _matmul(a, b, *, mode, grid, a_spec, b_spec, out_shapes, out_specs, name, nk=1,
            a_silu=False, exact=False, epilogue=None, extra=(), extra_specs=(), acc_shape=None):
    n_extra = len(extra)
    n_out = len(out_shapes)

    def body(*refs):
        a_ref, b_ref = refs[:2]
        ex = refs[2:2 + n_extra]
        outs = refs[2 + n_extra:2 + n_extra + n_out]
        av = a_ref[...]
        bv = b_ref[...]
        if a_silu:
            av = _silu(av.astype(F32))
        if exact:
            prod = lax.dot_general(av.astype(F32), bv.astype(F32), _DIMS[mode],
                                   precision=lax.Precision.HIGHEST, preferred_element_type=F32)
        else:
            prod = lax.dot_general(av.astype(BF16), bv.astype(BF16), _DIMS[mode], preferred_element_type=F32)

        def finish(res):
            if epilogue == "bias":
                res = res + ex[0][...]
            outs[0][...] = res.astype(outs[0].dtype)

        if nk == 1:
            finish(prod)
        else:
            acc = refs[-1]
            k = pl.program_id(len(grid) - 1)

            @pl.when(k == 0)
            def _():
                acc[...] = prod

            @pl.when(k > 0)
            def _():
                acc[...] += prod

            @pl.when(k == nk - 1)
            def _():
                finish(acc[...])

    scratch = [pltpu.VMEM(acc_shape, F32)] if nk > 1 else []
    sem = ("parallel",) * (len(grid) - 1) + ("arbitrary",)
    return pl.pallas_call(
        body, grid=grid, in_specs=[a_spec, b_spec, *extra_specs], out_specs=list(out_specs),
        out_shape=list(out_shapes), scratch_shapes=scratch, name=name, compiler_params=_cparams(*sem),
    )(a, b, *extra)


def _row_tile(rows):
    for t in (768, 512, 256):
        if rows % t == 0:
            return t
    return rows


def _mm_tn(a, b, name, out_dtype, tm=512):
    r, m = a.shape
    n = b.shape[1]
    tm = min(tm, m)
    tn = min(1024, n)
    return _matmul(
        a, b, mode="tn", grid=(m // tm, n // tn),
        a_spec=pl.BlockSpec((r, tm), lambda i, j: (0, i)), b_spec=pl.BlockSpec((r, tn), lambda i, j: (0, j)),
        out_shapes=[_sds((m, n), out_dtype)], out_specs=[pl.BlockSpec((tm, tn), lambda i, j: (i, j))], name=name)[0]


def _mm_tn_parts(a, b, name, out_dtype, tm=512):
    r, m = a.shape
    p, _, np_ = b.shape
    tm = min(tm, m)
    return _matmul(
        a, b, mode="tn", grid=(m // tm, p),
        a_spec=pl.BlockSpec((r, tm), lambda i, q: (0, i)), b_spec=pl.BlockSpec((None, r, np_), lambda i, q: (q, 0, 0)),
        out_shapes=[_sds((m, p * np_), out_dtype)], out_specs=[pl.BlockSpec((tm, np_), lambda i, q: (i, q))],
        name=name)[0]


def _row_vec(ref, is_ctx):
    return ref[0] if is_ctx is None else jnp.where(is_ctx, ref[1], ref[0])


def _ctx_rows(i, tm, nx, nseg):
    if nseg == 1:
        return None
    return i * tm + lax.broadcasted_iota(jnp.int32, (tm, 1), 0) >= nx


def _seg_sums(ref, val, is_ctx, first):
    if is_ctx is None:
        parts = [jnp.sum(val, axis=0, keepdims=True)]
    else:
        parts = [jnp.sum(jnp.where(is_ctx, 0.0, val), axis=0, keepdims=True),
                 jnp.sum(jnp.where(is_ctx, val, 0.0), axis=0, keepdims=True)]

    @pl.when(first)
    def _():
        for k, p in enumerate(parts):
            ref[k] = p

    @pl.when(jnp.logical_not(first))
    def _():
        for k, p in enumerate(parts):
            ref[k] += p


def _w_out_resid(a, w_out, xres, gate, nx, name):
    m, k = a.shape
    n = w_out.shape[1]
    nseg = gate.shape[0]
    tm = _row_tile(m)

    def body(a_ref, w_ref, x_ref, gt_ref, yx_ref, xo_ref):
        yx = jnp.dot(a_ref[...], w_ref[...], preferred_element_type=F32)
        yx_ref[...] = yx.astype(ACT)
        xo_ref[...] = x_ref[...] + _row_vec(gt_ref, _ctx_rows(pl.program_id(0), tm, nx, nseg)) * yx

    row = pl.BlockSpec((tm, n), lambda i: (i, 0))
    return pl.pallas_call(
        body, grid=(m // tm,),
        in_specs=[pl.BlockSpec((tm, k), lambda i: (i, 0)), pl.BlockSpec((k, n), lambda i: (0, 0)), row,
                  pl.BlockSpec((nseg, 1, n), lambda i: (0, 0, 0))],
        out_specs=[row, row], out_shape=[_sds((m, n), ACT), _sds((m, n), F32)],
        name=name, compiler_params=_cparams("parallel"),
    )(a, w_out, xres, gate)


def _norm_w_in(x, g, scale, shift, w_in, nx, name):
    rows, d = x.shape
    n = w_in.shape[1]
    nseg = scale.shape[0]
    tm = _row_tile(rows)
    tn = min(1024, n)

    def body(x_ref, g_ref, sc_ref, sh_ref, w_ref, h_ref, r_ref, p_ref):
        i, j = pl.program_id(0), pl.program_id(1)

        @pl.when(j == 0)
        def _():
            xv = x_ref[...]
            r = lax.rsqrt(jnp.mean(xv * xv, axis=-1, keepdims=True) + EPS)
            is_ctx = _ctx_rows(i, tm, nx, nseg)
            h = (xv * r) * g_ref[...] * (1.0 + _row_vec(sc_ref, is_ctx)) + _row_vec(sh_ref, is_ctx)
            h_ref[...] = h.astype(BF16)
            r_ref[...] = r

        p_ref[...] = jnp.dot(h_ref[...], w_ref[...], preferred_element_type=F32).astype(ACT)

    vec = pl.BlockSpec((nseg, 1, d), lambda i, j: (0, 0, 0))
    return _call(
        body, (x, g, scale, shift, w_in), grid=(rows // tm, n // tn),
        in_specs=[pl.BlockSpec((tm, d), lambda i, j: (i, 0)), pl.BlockSpec((1, d), lambda i, j: (0, 0)), vec, vec,
                  pl.BlockSpec((d, tn), lambda i, j: (0, j))],
        out_specs=[pl.BlockSpec((tm, d), lambda i, j: (i, 0)), pl.BlockSpec((tm, 1), lambda i, j: (i, 0)),
                   pl.BlockSpec((tm, tn), lambda i, j: (i, j))],
        out_shape=[_sds((rows, d), BF16), _sds((rows, 1), F32), _sds((rows, n), ACT)], name=name)


def _gate_w_out_bwd(dxo, yx, gate, w_out, nx, name):
    rows, d = yx.shape
    w = w_out.shape[0]
    nseg = gate.shape[0]
    tm = _row_tile(rows)

    def body(dx_ref, yx_ref, gt_ref, w_ref, dyx_ref, da_ref, dg_ref):
        i = pl.program_id(0)
        is_ctx = _ctx_rows(i, tm, nx, nseg)
        dxv = dx_ref[...]
        dyx = (dxv * _row_vec(gt_ref, is_ctx)).astype(BF16)
        dyx_ref[...] = dyx
        da_ref[...] = lax.dot_general(dyx, w_ref[...], _DIMS["nt"], preferred_element_type=F32).astype(ACT)
        _seg_sums(dg_ref, dxv * yx_ref[...].astype(F32), is_ctx, i == 0)

    row = pl.BlockSpec((tm, d), lambda i: (i, 0))
    vec = pl.BlockSpec((nseg, 1, d), lambda i: (0, 0, 0))
    return _call(
        body, (dxo, yx, gate, w_out), grid=(rows // tm,),
        in_specs=[row, row, vec, pl.BlockSpec((w, d), lambda i: (0, 0))],
        out_specs=[row, pl.BlockSpec((tm, w), lambda i: (i, 0)), vec],
        out_shape=[_sds((rows, d), BF16), _sds((rows, w), ACT), _sds((nseg, 1, d), F32)], name=name)


def _w_in_bwd_norm(dparts, w_in, x, r, g, scale, dres, nx, name):
    np_, rows, kp = dparts.shape
    d = w_in.shape[0]
    nseg = scale.shape[0]
    tm = _row_tile(rows)
    nsub = tm // ROW_BLOCK
    nres_blocks = dres.shape[0] // ROW_BLOCK

    def body(dp_ref, w_ref, x_ref, r_ref, g_ref, sc_ref, *rest):
        dres_refs = rest[:nsub]
        dx_ref, dsh_ref, dge_ref, acc = rest[nsub:]
        i, k = pl.program_id(0), pl.program_id(1)
        prod = lax.dot_general(dp_ref[...], w_ref[...], _DIMS["nt"], preferred_element_type=F32)

        @pl.when(k == 0)
        def _():
            acc[...] = prod

        @pl.when(k > 0)
        def _():
            acc[...] += prod

        @pl.when(k == np_ - 1)
        def _():
            is_ctx = _ctx_rows(i, tm, nx, nseg)
            dhv = acc[...]
            rv = r_ref[...]
            xn = x_ref[...] * rv
            dxn = dhv * (g_ref[...] * (1.0 + _row_vec(sc_ref, is_ctx)))
            dx = rv * (dxn - xn * jnp.mean(dxn * xn, axis=-1, keepdims=True))
            for s in range(nsub):
                piece = slice(s * ROW_BLOCK, (s + 1) * ROW_BLOCK)
                res = dres_refs[s][...]
                if nres_blocks * ROW_BLOCK < rows:
                    res = jnp.where(i * nsub + s < nres_blocks, res, 0.0)
                dx_ref[piece, :] = dx[piece, :] + res
            _seg_sums(dsh_ref, dhv, is_ctx, i == 0)
            _seg_sums(dge_ref, dhv * xn, is_ctx, i == 0)

    row = pl.BlockSpec((tm, d), lambda i, k: (i, 0))
    vec = pl.BlockSpec((nseg, 1, d), lambda i, k: (0, 0, 0))
    return _call(
        body, (dparts, w_in, x, r, g, scale, *([dres] * nsub)), grid=(rows // tm, np_),
        in_specs=[pl.BlockSpec((None, tm, kp), lambda i, k: (k, i, 0)), pl.BlockSpec((d, kp), lambda i, k: (0, k)),
                  row, pl.BlockSpec((tm, 1), lambda i, k: (i, 0)), pl.BlockSpec((1, d), lambda i, k: (0, 0)), vec]
        + [pl.BlockSpec((ROW_BLOCK, d), (lambda i, k, s=s: (jnp.minimum(i * nsub + s, nres_blocks - 1), 0)))
           for s in range(nsub)],
        out_specs=[row, vec, vec],
        out_shape=[_sds((rows, d), F32), _sds((nseg, 1, d), F32), _sds((nseg, 1, d), F32)],
        scratch_shapes=[pltpu.VMEM((tm, d), F32)], name=name)


_PAD_TOP = 16
_PAD_BOT = 32


def _window_sum(buf, xv, lo, n):
    t = xv.shape[0]
    c = xv.shape[1]
    tp = t + _PAD_TOP + _PAD_BOT
    buf[pl.ds(0, _PAD_TOP), :] = jnp.zeros((_PAD_TOP, c), F32)
    buf[pl.ds(_PAD_TOP, t), :] = xv
    buf[pl.ds(_PAD_TOP + t, _PAD_BOT), :] = jnp.zeros((_PAD_BOT, c), F32)
    p = buf[...]
    k = 1
    while k < n:
        p = p + pltpu.roll(p, tp - k, 0)
        k *= 2
    if lo:
        p = pltpu.roll(p, -lo, 0)
    buf[...] = p
    return buf[pl.ds(_PAD_TOP, t), :]


def _window_count(t, half):
    pos = lax.broadcasted_iota(jnp.int32, (t, 1), 0)
    return (jnp.minimum(pos + half, t) - jnp.maximum(pos - half, 0)).astype(F32)


def _segments(rows, nx):
    return [(0, nx)] + ([(nx, rows - nx)] if rows > nx else [])


def _pool_scratch(rows, nx, cols):
    return [pltpu.VMEM((length + _PAD_TOP + _PAD_BOT, cols), F32) for _, length in _segments(rows, nx)]


def _per_group(g, fn):
    for gi, win in enumerate(POOL_WINDOWS):
        pl.when(g == gi)(functools.partial(fn, win))


def _pool_grp_fwd(uv, w_grp, scale, nx, name):
    rows = uv.shape[0]
    ng, gc, _ = w_grp.shape
    w = ng * gc
    segs = _segments(rows, nx)

    def body(u_ref, gt_ref, w_ref, sc_ref, z_ref, mx_ref, a_ref, *bufs):
        def pool(win):
            half = win // 2
            for (start, length), buf in zip(segs, bufs):
                uvv = u_ref[pl.ds(start, length), :].astype(F32)
                s = _window_sum(buf, uvv, -half, win)
                z_ref[pl.ds(start, length), :] = (s / _window_count(length, half) - uvv).astype(BF16)

        _per_group(pl.program_id(0), pool)
        mixed = jnp.dot(z_ref[...], w_ref[...], preferred_element_type=F32)
        mx_ref[...] = mixed.astype(ACT)
        a_ref[...] = (mixed * sc_ref[...] * _silu(gt_ref[...].astype(F32))).astype(BF16)

    col = pl.BlockSpec((rows, gc), lambda g: (0, g))
    return _call(
        body, (uv, uv, w_grp, scale), grid=(ng,),
        in_specs=[col, pl.BlockSpec((rows, gc), lambda g: (0, ng + g)), pl.BlockSpec((None, gc, gc), lambda g: (g, 0, 0)),
                  pl.BlockSpec((1, gc), lambda g: (0, g))],
        out_specs=[col, col, col], out_shape=[_sds((rows, w), BF16), _sds((rows, w), ACT), _sds((rows, w), BF16)],
        scratch_shapes=_pool_scratch(rows, nx, gc), name=name)


def _pool_grp_bwd(da, mixed, uv, scale, w_grp, nx, name):
    rows, w = da.shape
    ng, gc, _ = w_grp.shape
    segs = _segments(rows, nx)

    def body(da_ref, mx_ref, gt_ref, sc_ref, w_ref, dm_ref, duv_ref, dsc_ref, dz_ref, *bufs):
        dav = da_ref[...].astype(F32)
        mixed = mx_ref[...].astype(F32)
        gt = gt_ref[...].astype(F32)
        sg = _silu(gt)
        sc = sc_ref[...]
        dm = (dav * sc * sg).astype(BF16)
        dm_ref[...] = dm
        dz_ref[...] = lax.dot_general(dm, w_ref[...], _DIMS["nt"], preferred_element_type=F32)
        duv_ref[1] = (dav * mixed * sc * _dsilu(gt)).astype(BF16)
        dsc_ref[...] = jnp.sum(dav * mixed * sg, axis=0, keepdims=True)

        def unpool(win):
            half = win // 2
            for (start, length), buf in zip(segs, bufs):
                dzv = dz_ref[pl.ds(start, length), :]
                s = _window_sum(buf, dzv / _window_count(length, half), 1 - half, win)
                duv_ref[0, pl.ds(start, length), :] = (s - dzv).astype(BF16)

        _per_group(pl.program_id(0), unpool)

    col = pl.BlockSpec((rows, gc), lambda g: (0, g))
    vec = pl.BlockSpec((1, gc), lambda g: (0, g))
    return pl.pallas_call(
        body, grid=(ng,),
        in_specs=[col, col, pl.BlockSpec((rows, gc), lambda g: (0, ng + g)), vec,
                  pl.BlockSpec((None, gc, gc), lambda g: (g, 0, 0))],
        out_specs=[col, pl.BlockSpec((2, rows, gc), lambda g: (0, 0, g)), vec],
        out_shape=[_sds((rows, w), BF16), _sds((2, rows, w), BF16), _sds((1, w), F32)],
        scratch_shapes=[pltpu.VMEM((rows, gc), F32)] + _pool_scratch(rows, nx, gc),
        name=name, compiler_params=_cparams("parallel"),
    )(da, mixed, uv, scale, w_grp)


def _grp_wgrad(z, dm, ng, name, out_dtype):
    rows, w = z.shape
    gc = w // ng

    def body(z_ref, dm_ref, o_ref):
        o_ref[...] = lax.dot_general(z_ref[...], dm_ref[...], _DIMS["tn"],
                                     preferred_element_type=F32).astype(o_ref.dtype)

    blk = pl.BlockSpec((rows, gc), lambda g: (0, g))
    return pl.pallas_call(
        body, grid=(ng,), in_specs=[blk, blk], out_specs=pl.BlockSpec((None, gc, gc), lambda g: (g, 0, 0)),
        out_shape=_sds((ng, gc, gc), out_dtype), name=name, compiler_params=_cparams("parallel"),
    )(z, dm)


def _shift_rows(v, by):
    t = v.shape[0]
    pos = lax.broadcasted_iota(jnp.int32, v.shape, 0)
    rolled = pltpu.roll(v, by % t, 0)
    keep = pos >= by if by > 0 else pos < t + by
    return jnp.where(keep, rolled, 0.0)


def _conv_specs(t, w, cb):
    return [pl.BlockSpec((t, cb), (lambda j, q=q: (0, q * (w // cb) + j))) for q in range(4)]


def _conv_fwd(p4, dw, db, name):
    t = p4.shape[0]
    w = p4.shape[1] // 4
    cb = 128

    def body(bg_ref, cg_ref, v_ref, g_ref, dw_ref, db_ref, a_ref):
        tv = cg_ref[...].astype(F32) * v_ref[...].astype(F32)
        conv = (dw_ref[0:1, :] * _shift_rows(tv, 1) + dw_ref[1:2, :] * tv + dw_ref[2:3, :] * _shift_rows(tv, -1)
                + db_ref[...])
        a_ref[...] = (bg_ref[...].astype(F32) * conv * _silu(g_ref[...].astype(F32))).astype(BF16)

    return pl.pallas_call(
        body, grid=(w // cb,),
        in_specs=_conv_specs(t, w, cb) + [pl.BlockSpec((3, cb), lambda j: (0, j)), pl.BlockSpec((1, cb), lambda j: (0, j))],
        out_specs=pl.BlockSpec((t, cb), lambda j: (0, j)), out_shape=_sds((t, w), BF16),
        name=name, compiler_params=_cparams("parallel"),
    )(p4, p4, p4, p4, dw, db)


def _conv_bwd(da, p4, dw, db, name):
    t, w = da.shape
    cb = 128

    def body(da_ref, bg_ref, cg_ref, v_ref, g_ref, dw_ref, db_ref, d4_ref, ddw_ref, ddb_ref):
        cg = cg_ref[...].astype(F32)
        vv = v_ref[...].astype(F32)
        bg = bg_ref[...].astype(F32)
        gv = g_ref[...].astype(F32)
        tv = cg * vv
        tm1 = _shift_rows(tv, 1)
        tp1 = _shift_rows(tv, -1)
        w0, w1, w2 = dw_ref[0:1, :], dw_ref[1:2, :], dw_ref[2:3, :]
        conv = w0 * tm1 + w1 * tv + w2 * tp1 + db_ref[...]
        y = bg * conv
        dav = da_ref[...].astype(F32)
        dy = dav * _silu(gv)
        d4_ref[3] = (dav * y * _dsilu(gv)).astype(BF16)
        d4_ref[0] = (dy * conv).astype(BF16)
        dconv = dy * bg
        ddb_ref[...] = jnp.sum(dconv, axis=0, keepdims=True)
        ddw_ref[0:1, :] = jnp.sum(dconv * tm1, axis=0, keepdims=True)
        ddw_ref[1:2, :] = jnp.sum(dconv * tv, axis=0, keepdims=True)
        ddw_ref[2:3, :] = jnp.sum(dconv * tp1, axis=0, keepdims=True)
        dt = w0 * _shift_rows(dconv, -1) + w1 * dconv + w2 * _shift_rows(dconv, 1)
        d4_ref[1] = (dt * vv).astype(BF16)
        d4_ref[2] = (dt * cg).astype(BF16)

    col = pl.BlockSpec((t, cb), lambda j: (0, j))
    tap = pl.BlockSpec((3, cb), lambda j: (0, j))
    bias = pl.BlockSpec((1, cb), lambda j: (0, j))
    return pl.pallas_call(
        body, grid=(w // cb,), in_specs=[col] + _conv_specs(t, w, cb) + [tap, bias],
        out_specs=[pl.BlockSpec((4, t, cb), lambda j: (0, 0, j)), tap, bias],
        out_shape=[_sds((4, t, w), BF16), _sds((3, w), F32), _sds((1, w), F32)],
        name=name, compiler_params=_cparams("parallel"),
    )(da, p4, p4, p4, p4, dw, db)


def _attn_mask():
    qn, kn = Q_ROWS * GRID_W, K_ROWS * GRID_W
    qr, qc = np.divmod(np.arange(qn), GRID_W)
    kr, kc = np.divmod(np.arange(kn), GRID_W)
    col0 = np.clip(qc - WIN_COLS // 2, 0, GRID_W - WIN_COLS)
    col_ok = (kc[None, :] >= col0[:, None]) & (kc[None, :] < col0[:, None] + WIN_COLS)
    first = np.zeros(qn, np.int64)
    last = np.full(qn, K_ROWS - WIN_ROWS)
    out = []
    for row0 in (first, qr, last):
        row_ok = (kr[None, :] >= row0[:, None]) & (kr[None, :] < row0[:, None] + WIN_ROWS)
        out.append(np.where(row_ok & col_ok, 0.0, NEG))
    return jnp.asarray(np.stack(out), F32)


_KW = K_ROWS * GRID_W
_QB = Q_ROWS * GRID_W
_PAIR = 2 * HEAD_DIM
_N_DR = 2 * WIN_ROWS - 1
_N_DC = 2 * WIN_COLS - 1
_RP_ROWS = 24
_N_TILES = _N_DR + 1
_BIAS_BASE = (WIN_ROWS - 1, WIN_ROWS // 2 - 1, -1)


class _Comm:
    def __init__(self, ins, outs, sems, start, finish):
        self.ins, self.outs, self.sems, self.start, self.finish = list(ins), list(outs), list(sems), start, finish


def _bias_pieces(cls):
    out = []
    for qr in range(Q_ROWS):
        for kr in range(0, K_ROWS, 2):
            tile = _BIAS_BASE[cls] - qr + kr + 1
            out.append((qr, kr, tile if 0 <= tile < _N_TILES else None))
    return out


def _toeplitz_pair(left_row, right_row):
    lane = lax.broadcasted_iota(jnp.int32, (GRID_W, _PAIR), 1)
    shape = (GRID_W, _PAIR)
    left = pltpu.roll(jnp.broadcast_to(left_row, shape), _PAIR - (WIN_COLS - 1), 1, stride=1, stride_axis=0)
    right = pltpu.roll(jnp.broadcast_to(right_row, shape), GRID_W - (WIN_COLS - 1), 1, stride=1, stride_axis=0)
    return jnp.where(lane < GRID_W, left, right)


def _build_tiles(tiles_ref, rp_ref):
    for h in range(2):
        for t in range(_N_TILES):
            tiles_ref[h, t] = _toeplitz_pair(rp_ref[h, t:t + 1, :], rp_ref[h, t + 1:t + 2, :])


def _block_class(b, nblk, fn, entering=False):
    interior = (b == 1) if entering else jnp.logical_and(b > 0, b < nblk - 1)
    for cls, cond in enumerate((b == 0, interior, b == nblk - 1)):
        pl.when(cond)(functools.partial(fn, cls))


def _attn_geometry(p4, nx):
    rows = p4.shape[0]
    w = p4.shape[1] // 4
    nhp = w // _PAIR
    nblk = nx // _QB
    qspec = lambda col: pl.BlockSpec((_QB, _PAIR), lambda hp, b: (b, col * nhp + hp))
    kspec = lambda col: pl.BlockSpec((rows, _PAIR), lambda hp, b: (0, col * nhp + hp))
    tspec = pl.BlockSpec((2, _RP_ROWS, _PAIR), lambda hp, b: (hp, 0, 0))
    mspec = pl.BlockSpec((None, _QB, _KW), lambda hp, b: (jnp.where(b == 0, 0, jnp.where(b == nblk - 1, 2, 1)), 0, 0))
    lspec = pl.BlockSpec((None, _QB, 2), lambda hp, b: (hp, b, 0))
    ospec = pl.BlockSpec((_QB, _PAIR), lambda hp, b: (b, hp))
    return rows, w, nhp, nblk, qspec, kspec, tspec, mspec, lspec, ospec


def _window_start(b, nx):
    return pl.multiple_of(jnp.clip(b * _QB - PAD_ROWS * GRID_W, 0, nx - _KW), _QB)


def _load_bias(bias_ref, tiles_ref, rp_ref, m_ref, b, nblk):
    pl.when(b == 0)(lambda: _build_tiles(tiles_ref, rp_ref))

    def fill(cls):
        for h in range(2):
            for qr, kr, tile in _bias_pieces(cls):
                rows = slice(qr * GRID_W, (qr + 1) * GRID_W)
                cols = slice(kr * GRID_W, (kr + 2) * GRID_W)
                m = m_ref[rows, cols]
                bias_ref[h, rows, cols] = m if tile is None else tiles_ref[h, tile] + m

    _block_class(b, nblk, fill, entering=True)


def _attn_fwd(p4, rp, mask, nx, name, comm=None):
    rows, w, nhp, nblk, qspec, kspec, tspec, mspec, lspec, ospec = _attn_geometry(p4, nx)
    n_ctx = rows - nx
    n_cin, n_cout = (len(comm.ins), len(comm.outs)) if comm else (0, 0)

    def body(*refs):
        q_ref, k_ref, v_ref, g_ref, rp_ref, m_ref = refs[:6]
        cin = refs[6:6 + n_cin]
        a_ref, o_ref, lse_ref = refs[6 + n_cin:9 + n_cin]
        cout = refs[9 + n_cin:9 + n_cin + n_cout]
        bias_ref, tiles_ref = refs[9 + n_cin + n_cout:11 + n_cin + n_cout]
        sems = refs[11 + n_cin + n_cout:]
        hp, b = pl.program_id(0), pl.program_id(1)
        if comm:
            pl.when(jnp.logical_and(hp == 0, b == 0))(lambda: comm.start(cin, cout, sems))
        start = _window_start(b, nx)
        _load_bias(bias_ref, tiles_ref, rp_ref, m_ref, b, nblk)
        qf = q_ref[...].astype(F32) * HEAD_DIM ** -0.5
        kw = k_ref[pl.ds(start, _KW), :].astype(BF16)
        vw = v_ref[pl.ds(start, _KW), :].astype(BF16)
        kcv = k_ref[pl.ds(nx, n_ctx), :].astype(BF16)
        vcv = v_ref[pl.ds(nx, n_ctx), :].astype(BF16)
        lane = lax.broadcasted_iota(jnp.int32, (1, _PAIR), 1)
        outs, lses = [], []
        for h in range(2):
            mine = (lane >= HEAD_DIM) if h else (lane < HEAD_DIM)
            qm = jnp.where(mine, qf, 0.0).astype(BF16)
            s_loc = lax.dot_general(qm, kw, _DIMS["nt"], preferred_element_type=F32) + bias_ref[h]
            s_ctx = lax.dot_general(qm, kcv, _DIMS["nt"], preferred_element_type=F32)
            mx = jnp.maximum(jnp.max(s_loc, axis=-1, keepdims=True), jnp.max(s_ctx, axis=-1, keepdims=True))
            p_loc = jnp.exp(s_loc - mx)
            p_ctx = jnp.exp(s_ctx - mx)
            den = jnp.sum(p_loc, axis=-1, keepdims=True) + jnp.sum(p_ctx, axis=-1, keepdims=True)
            o = jnp.dot(p_loc.astype(BF16), vw, preferred_element_type=F32)
            o = o + jnp.dot(p_ctx.astype(BF16), vcv, preferred_element_type=F32)
            outs.append(o * (1.0 / den))
            lses.append(mx + jnp.log(den))
        o = jnp.where(lane < HEAD_DIM, outs[0], outs[1])
        o_ref[...] = o.astype(ACT)
        a_ref[...] = (o * _silu(g_ref[...].astype(F32))).astype(BF16)
        col = lax.broadcasted_iota(jnp.int32, (1, 2), 1)
        lse_ref[...] = jnp.where(col == 0, lses[0], lses[1])
        if comm:
            pl.when(jnp.logical_and(hp == nhp - 1, b == nblk - 1))(lambda: comm.finish(cin, cout, sems))

    res = pl.pallas_call(
        body, grid=(nhp, nblk),
        in_specs=[qspec(0), kspec(1), kspec(2), qspec(3), tspec, mspec] + [HBM_SPEC] * n_cin,
        out_specs=[ospec, ospec, lspec] + [HBM_SPEC] * n_cout,
        out_shape=[_sds((nx, w), BF16), _sds((nx, w), ACT), _sds((nhp, nx, 2), F32)] + (comm.outs if comm else []),
        scratch_shapes=[pltpu.VMEM((2, _QB, _KW), F32), pltpu.VMEM((2, _N_TILES, GRID_W, _PAIR), F32)]
        + (comm.sems if comm else []),
        name=name, compiler_params=_cparams("arbitrary", "arbitrary"),
    )(p4, p4, p4, p4, rp, mask, *(comm.ins if comm else []))
    return res[:3], res[3:]


def _fold_tiles(dtiles_ref, drp_ref):
    shape = (GRID_W, _PAIR)
    lane = lax.broadcasted_iota(jnp.int32, shape, 1)
    flip = (lax.broadcasted_iota(jnp.int32, (_PAIR, _PAIR), 0)
            + lax.broadcasted_iota(jnp.int32, (_PAIR, _PAIR), 1) == _PAIR - 1).astype(F32)
    drp_ref[...] = jnp.zeros(drp_ref.shape, F32)
    for h in range(2):
        stack = dtiles_ref[h].reshape(_N_TILES * GRID_W, _PAIR)
        rev = jnp.dot(stack, flip, precision=lax.Precision.HIGHEST, preferred_element_type=F32)
        for t in range(_N_TILES):
            tile = rev[t * GRID_W:(t + 1) * GRID_W, :]
            for side in (0, 1):
                shift = _PAIR - GRID_W * side - (WIN_COLS - 1)
                half = jnp.where((lane < GRID_W) if side else (lane >= GRID_W), tile, 0.0)
                diag = pltpu.roll(half, shift, 1, stride=1, stride_axis=0)
                drp_ref[h, t + side:t + side + 1, :] += jnp.sum(diag, axis=0, keepdims=True)


def _attn_bwd(p4, rp, mask, o, lse, da, nx, name, comm=None):
    rows, w, nhp, nblk, qspec, kspec, tspec, mspec, lspec, ospec = _attn_geometry(p4, nx)
    n_ctx = rows - nx
    n_cin, n_cout = (len(comm.ins), len(comm.outs)) if comm else (0, 0)

    def body(*refs):
        q_ref, k_ref, v_ref, g_ref, rp_ref, m_ref, o_ref, lse_ref, da_ref = refs[:9]
        cin = refs[9:9 + n_cin]
        d4_ref, drp_ref = refs[9 + n_cin:11 + n_cin]
        cout = refs[11 + n_cin:11 + n_cin + n_cout]
        bias_ref, tiles_ref, ds_ref, dtiles_ref, dk_ref, dv_ref = refs[11 + n_cin + n_cout:17 + n_cin + n_cout]
        sems = refs[17 + n_cin + n_cout:]
        hp, b = pl.program_id(0), pl.program_id(1)
        if comm:
            pl.when(jnp.logical_and(hp == 0, b == 0))(lambda: comm.start(cin, cout, sems))
        start = _window_start(b, nx)
        here = pl.multiple_of(b * _QB, _QB)

        @pl.when(b == 0)
        def _():
            dk_ref[...] = jnp.zeros(dk_ref.shape, F32)
            dv_ref[...] = jnp.zeros(dv_ref.shape, F32)
            dtiles_ref[...] = jnp.zeros(dtiles_ref.shape, F32)
            d4_ref[0, pl.ds(nx, n_ctx), :] = jnp.zeros((n_ctx, _PAIR), BF16)
            d4_ref[3, pl.ds(nx, n_ctx), :] = jnp.zeros((n_ctx, _PAIR), BF16)

        _load_bias(bias_ref, tiles_ref, rp_ref, m_ref, b, nblk)
        gv = g_ref[...].astype(F32)
        dav = da_ref[...].astype(F32)
        ov = o_ref[...].astype(F32)
        dov = dav * _silu(gv)
        d4_ref[3, pl.ds(here, _QB), :] = (dav * ov * _dsilu(gv)).astype(BF16)
        qf = q_ref[...].astype(F32) * HEAD_DIM ** -0.5
        kw = k_ref[pl.ds(start, _KW), :].astype(BF16)
        vw = v_ref[pl.ds(start, _KW), :].astype(BF16)
        kcv = k_ref[pl.ds(nx, n_ctx), :].astype(BF16)
        vcv = v_ref[pl.ds(nx, n_ctx), :].astype(BF16)
        lane = lax.broadcasted_iota(jnp.int32, (1, _PAIR), 1)
        dq = jnp.zeros((_QB, _PAIR), F32)
        for h in range(2):
            mine = (lane >= HEAD_DIM) if h else (lane < HEAD_DIM)
            qm = jnp.where(mine, qf, 0.0).astype(BF16)
            dom = jnp.where(mine, dov, 0.0)
            dob = dom.astype(BF16)
            lse = lse_ref[:, h:h + 1]
            s_loc = lax.dot_general(qm, kw, _DIMS["nt"], preferred_element_type=F32)
            p_loc = jnp.exp(s_loc + bias_ref[h] - lse)
            p_ctx = jnp.exp(lax.dot_general(qm, kcv, _DIMS["nt"], preferred_element_type=F32) - lse)
            delta = jnp.sum(dom * ov, axis=-1, keepdims=True)
            ds_loc = p_loc * (lax.dot_general(dob, vw, _DIMS["nt"], preferred_element_type=F32) - delta)
            ds_ctx = p_ctx * (lax.dot_general(dob, vcv, _DIMS["nt"], preferred_element_type=F32) - delta)
            dsb_loc = ds_loc.astype(BF16)
            dsb_ctx = ds_ctx.astype(BF16)
            dq_h = (jnp.dot(dsb_loc, kw, preferred_element_type=F32)
                    + jnp.dot(dsb_ctx, kcv, preferred_element_type=F32))
            dq = dq + jnp.where(mine, dq_h, 0.0)
            dk_ref[pl.ds(start, _KW), :] += lax.dot_general(dsb_loc, qm, _DIMS["tn"], preferred_element_type=F32)
            dv_ref[pl.ds(start, _KW), :] += lax.dot_general(p_loc.astype(BF16), dob, _DIMS["tn"],
                                                            preferred_element_type=F32)
            dk_ref[pl.ds(nx, n_ctx), :] += lax.dot_general(dsb_ctx, qm, _DIMS["tn"], preferred_element_type=F32)
            dv_ref[pl.ds(nx, n_ctx), :] += lax.dot_general(p_ctx.astype(BF16), dob, _DIMS["tn"],
                                                           preferred_element_type=F32)
            ds_ref[h] = ds_loc
        d4_ref[0, pl.ds(here, _QB), :] = (dq * HEAD_DIM ** -0.5).astype(BF16)

        def scatter(cls):
            for h in range(2):
                for qr, kr, tile in _bias_pieces(cls):
                    if tile is not None:
                        dtiles_ref[h, tile] += ds_ref[h, qr * GRID_W:(qr + 1) * GRID_W, kr * GRID_W:(kr + 2) * GRID_W]

        _block_class(b, nblk, scatter)

        @pl.when(b == nblk - 1)
        def _():
            d4_ref[1] = dk_ref[...].astype(BF16)
            d4_ref[2] = dv_ref[...].astype(BF16)
            _fold_tiles(dtiles_ref, drp_ref)

        if comm:
            pl.when(jnp.logical_and(hp == nhp - 1, b == nblk - 1))(lambda: comm.finish(cin, cout, sems))

    tiles = pltpu.VMEM((2, _N_TILES, GRID_W, _PAIR), F32)
    block = pltpu.VMEM((2, _QB, _KW), F32)
    res = pl.pallas_call(
        body, grid=(nhp, nblk),
        in_specs=[qspec(0), kspec(1), kspec(2), qspec(3), tspec, mspec, ospec, lspec, ospec] + [HBM_SPEC] * n_cin,
        out_specs=[pl.BlockSpec((4, rows, _PAIR), lambda hp, b: (0, 0, hp)), tspec] + [HBM_SPEC] * n_cout,
        out_shape=[_sds((4, rows, w), BF16), _sds(rp.shape, F32)] + (comm.outs if comm else []),
        scratch_shapes=[block, tiles, block, tiles, pltpu.VMEM((rows, _PAIR), F32), pltpu.VMEM((rows, _PAIR), F32)]
        + (comm.sems if comm else []),
        name=name, compiler_params=_cparams("arbitrary", "arbitrary"),
    )(p4, p4, p4, p4, rp, mask, o, lse, da, *(comm.ins if comm else []))
    return res[:2], res[2:]


def _w_out_loss(a, w_out, xres, gate, g, target, name):
    m, k = a.shape
    d = w_out.shape[1]
    assert gate.shape[0] == 1
    tm = _row_tile(m)
    nblk = m // tm

    def body(a_ref, w_ref, x_ref, gt_ref, g_ref, t_ref, yx_ref, loss_ref, dx_ref, dg_ref, acc_ref):
        i = pl.program_id(0)
        yx = jnp.dot(a_ref[...], w_ref[...], preferred_element_type=F32)
        yx_ref[...] = yx.astype(ACT)
        xv = x_ref[...] + gt_ref[0] * yx
        gv = g_ref[...]
        r = lax.rsqrt(jnp.mean(xv * xv, axis=-1, keepdims=True) + EPS)
        xn = xv * r
        err = xn * gv - t_ref[...]
        dy = err * (1.0 / d)
        dxn = dy * gv
        dx_ref[...] = r * (dxn - xn * jnp.mean(dxn * xn, axis=-1, keepdims=True))
        s_g = jnp.sum(dy * xn, axis=0, keepdims=True)
        s_l = jnp.sum(jnp.mean(err * err, axis=-1, keepdims=True), axis=0, keepdims=True)

        @pl.when(i == 0)
        def _():
            dg_ref[...] = s_g
            acc_ref[...] = s_l

        @pl.when(i > 0)
        def _():
            dg_ref[...] += s_g
            acc_ref[...] += s_l

        @pl.when(i == nblk - 1)
        def _():
            loss_ref[...] = jnp.broadcast_to(0.5 * acc_ref[...], loss_ref.shape)

    row = pl.BlockSpec((tm, d), lambda i: (i, 0))
    vec = pl.BlockSpec((1, d), lambda i: (0, 0))
    return pl.pallas_call(
        body, grid=(nblk,),
        in_specs=[pl.BlockSpec((tm, k), lambda i: (i, 0)), pl.BlockSpec((k, d), lambda i: (0, 0)), row,
                  pl.BlockSpec((1, 1, d), lambda i: (0, 0, 0)), vec, row],
        out_specs=[row, pl.BlockSpec((1, 128), lambda i: (0, 0)), row, vec],
        out_shape=[_sds((m, d), ACT), _sds((1, 128), F32), _sds((m, d), F32), _sds((1, d), F32)],
        scratch_shapes=[pltpu.VMEM((1, 1), F32)], name=name, compiler_params=_cparams("arbitrary"),
    )(a, w_out, xres, gate, g, target)


def _as2d(a):
    if a.ndim == 1:
        return a.reshape(-1, 128) if a.shape[0] % 128 == 0 else a.reshape(1, -1)
    return a.reshape(-1, a.shape[-1])


def _adamw(w, g, m, v, name):
    shape = w.shape
    w2, g2, m2, v2 = (_as2d(t) for t in (w, g.reshape(shape), m, v))
    rows, cols = w2.shape
    tr = 512 if rows % 512 == 0 else rows
    c1 = 1.0 - ADAM_B1 ** ADAM_STEP
    c2 = 1.0 - ADAM_B2 ** ADAM_STEP

    def body(w_ref, g_ref, m_ref, v_ref, d_ref, nm_ref, nv_ref):
        gv = g_ref[...]
        nm = ADAM_B1 * m_ref[...] + (1.0 - ADAM_B1) * gv
        nv = ADAM_B2 * v_ref[...] + (1.0 - ADAM_B2) * (gv * gv)
        nm_ref[...] = nm
        nv_ref[...] = nv
        d_ref[...] = -ADAM_LR * ((nm / c1) / (jnp.sqrt(nv / c2) + ADAM_EPS) + ADAM_WD * w_ref[...])

    blk = pl.BlockSpec((tr, cols), lambda i: (i, 0))
    outs = _call(body, (w2, g2, m2, v2), grid=(rows // tr,), in_specs=[blk] * 4, out_specs=[blk] * 3,
                 out_shape=[_sds((rows, cols), F32)] * 3, name=name)
    return tuple(t.reshape(shape) for t in outs)


def _sum_lead(x, name, out_dtype=F32):
    n, rows, cols = x.shape
    tr = 512 if rows % 512 == 0 else rows

    def body(x_ref, o_ref):
        acc = x_ref[0].astype(F32)
        for k in range(1, n):
            acc = acc + x_ref[k].astype(F32)
        o_ref[...] = acc.astype(out_dtype)

    return pl.pallas_call(
        body, grid=(rows // tr,), in_specs=[pl.BlockSpec((n, tr, cols), lambda i: (0, i, 0))],
        out_specs=pl.BlockSpec((tr, cols), lambda i: (i, 0)), out_shape=_sds((rows, cols), out_dtype),
        name=name, compiler_params=_cparams("parallel"),
    )(x)


def _seg_vecs(mod_l, which, nseg):
    return mod_l[:nseg, which][:, None, :]


def _norm_grads(dshift, dgeff, dgate, g, scale):
    nseg, _, d = dshift.shape
    dmod = jnp.stack([dshift[:, 0], dgeff[:, 0] * g, dgate[:, 0]], axis=1)
    if nseg == 1:
        dmod = jnp.concatenate([dmod, jnp.zeros((1, 3, d), F32)], axis=0)
    dg = jnp.sum(dgeff[:, 0] * (1.0 + scale[:, 0]), axis=0)
    return dmod, dg


def _pool_layer(xin, g, mod_l, w_in, w_grp, w_out, pscale, nx, tag, head=None):
    rows = xin.shape[0]
    nseg = 2 if rows > nx else 1
    shift, scale, gate = (_seg_vecs(mod_l, k, nseg) for k in range(3))
    h, r, uv = _norm_w_in(xin, g, scale, shift, w_in, nx, f"w_in_fwd_{tag}")
    z, mixed, a = _pool_grp_fwd(uv, w_grp, pscale, nx, f"pool_fwd_{tag}")
    if head is None:
        yx, xout = _w_out_resid(a, w_out, xin, gate, nx, f"w_out_fwd_{tag}")
    else:
        yx, *xout = _w_out_loss(a, w_out, xin, gate, *head, f"w_out_loss_{tag}")

    def backward(dxo, token=None):
        gate_b = gate if token is None else gate + token[0, 0]
        dyx, da, dgate = _gate_w_out_bwd(dxo, yx, gate_b, w_out, nx, f"w_out_bwd_{tag}")
        gw_out = _mm_tn(a, dyx, f"w_out_grad_{tag}", BF16)
        dm, duv, dscale = _pool_grp_bwd(da, mixed, uv, pscale, w_grp, nx, f"pool_bwd_{tag}")
        gw_grp = _grp_wgrad(z, dm, w_grp.shape[0], f"grp_grad_{tag}", BF16)
        gw_in = _mm_tn_parts(h, duv, f"w_in_grad_{tag}", BF16)
        dx, dshift, dgeff = _w_in_bwd_norm(duv, w_in, xin, r, g, scale, dxo, nx, f"w_in_bwd_{tag}")
        dmod, dg = _norm_grads(dshift, dgeff, dgate, g[0], scale)
        return dx, dmod, dg, dict(w_in=gw_in, w_grp=gw_grp, w_out=gw_out, scale=dscale)

    return xout, backward


def _na_layer(xc, g, mod_l, w_in, rpb, w_out, nx, mask, comm=None):
    nh, n_dr, n_dc = rpb.shape
    shift, scale = _seg_vecs(mod_l, 0, 2), _seg_vecs(mod_l, 1, 2)
    gate = _seg_vecs(mod_l, 2, 1)
    h, r, p4 = _norm_w_in(xc, g, scale, shift, w_in, nx, "w_in_fwd_na")
    rp = jnp.pad(rpb, ((0, 0), (1, _RP_ROWS - 1 - n_dr), (0, _PAIR - n_dc)))
    (a, o, lse), carried = _attn_fwd(p4, rp, mask, nx, "attn_fwd", comm)
    yx, xout = _w_out_resid(a, w_out, xc, gate, nx, "w_out_fwd_na")

    def backward(dxo, comm=None):
        dyx, da, dgate = _gate_w_out_bwd(dxo, yx, gate, w_out, nx, "w_out_bwd_na")
        gw_out = _mm_tn(a, dyx, "w_out_grad_na", BF16)
        (d4, drp), carried_bwd = _attn_bwd(p4, rp, mask, o, lse, da, nx, "attn_bwd", comm)
        gw_in = _mm_tn_parts(h, d4, "w_in_grad_na", BF16)
        dx, dshift, dgeff = _w_in_bwd_norm(d4, w_in, xc, r, g, scale, dxo, nx, "w_in_bwd_na")
        dgate2 = jnp.concatenate([dgate, jnp.zeros_like(dgate)], axis=0)
        dmod, dg = _norm_grads(dshift, dgeff, dgate2, g[0], scale)
        drpb = drp[:, 1:1 + n_dr, ::-1][:, :, :n_dc]
        return dx, dmod, dg, dict(w_in=gw_in, w_out=gw_out, rpb=drpb), carried_bwd

    return xout, backward, carried


def _conv_layer(xin, g, mod_l, w_in, dw, db, w_out):
    shift, scale, gate = (_seg_vecs(mod_l, k, 1) for k in range(3))
    nx = xin.shape[0]
    h, r, p4 = _norm_w_in(xin, g, scale, shift, w_in, nx, "w_in_fwd_conv")
    a = _conv_fwd(p4, dw, db, "conv_fwd")
    yx, xout = _w_out_resid(a, w_out, xin, gate, nx, "w_out_fwd_conv")

    def backward(dxo):
        dyx, da, dgate = _gate_w_out_bwd(dxo, yx, gate, w_out, nx, "w_out_bwd_conv")
        gw_out = _mm_tn(a, dyx, "w_out_grad_conv", BF16)
        d4, ddw, ddb = _conv_bwd(da, p4, dw, db, "conv_bwd")
        gw_in = _mm_tn_parts(h, d4, "w_in_grad_conv", BF16)
        dx, dshift, dgeff = _w_in_bwd_norm(d4, w_in, xin, r, g, scale, dxo, nx, "w_in_bwd_conv")
        dmod, dg = _norm_grads(dshift, dgeff, dgate, g[0], scale)
        return dx, dmod, dg, dict(w_in=gw_in, w_out=gw_out, dw=ddw, db=ddb)

    return xout, backward


def _example_step(x, ctx, target, mod, norm_g, final_g, wts, hooks=None):
    hooks = hooks or {}
    na_weights, late_comm, late_weights = (hooks.get(k) for k in ("na_weights", "late_comm", "late_weights"))
    nx = x.shape[0]
    consts = _attn_mask()
    g_rows = [norm_g[i:i + 1] for i in range(4)]
    xc0 = jnp.concatenate([x, ctx], axis=0)
    xc1, bwd0 = _pool_layer(xc0, g_rows[0], mod[0], wts["pool_w_in"][0], wts["pool_w_grp"][0],
                            wts["pool_w_out"][0], wts["pool_scale"][0:1], nx, "p0")
    if na_weights is not None:
        wts = {**wts, **na_weights(xc1)}
    x2, bwd1, carried = _na_layer(xc1, g_rows[1], mod[1], wts["na_w_in"], wts["na_rpb"], wts["na_w_out"], nx, consts,
                                  late_comm)
    if late_weights is not None:
        wts = {**wts, **late_weights(carried)}
    x3, bwd2 = _conv_layer(x2, g_rows[2], mod[2], wts["conv_w_in"], wts["conv_dw"], wts["conv_db"], wts["conv_w_out"])
    (loss, dx4, dfinal_g), bwd3 = _pool_layer(x3, g_rows[3], mod[3], wts["pool_w_in"][1], wts["pool_w_grp"][1],
                                              wts["pool_w_out"][1], wts["pool_scale"][1:2], nx, "p3",
                                              head=(final_g, target))
    call = lambda k, *args: hooks[k](*args) if k in hooks else None
    dx3, dmod3, dg3, gr3 = bwd3(dx4)
    dx2, dmod2, dg2, gr2 = bwd2(dx3)
    dxc1, dmod1, dg1, gr1, carried_bwd = bwd1(dx2, call("grad_comm", gr3, gr2))
    dxc0, dmod0, dg0, gr0 = bwd0(dxc1, call("na_grads_start", gr1))
    return dict(
        loss=loss, grad_x=dxc0[:nx], dmod=jnp.stack([dmod0, dmod1, dmod2, dmod3]),
        dnorm_g=jnp.stack([dg0, dg1, dg2, dg3]), dfinal_g=dfinal_g, layers=(gr0, gr1, gr2, gr3), carried=carried_bwd)


_AXES = ("x", "y", "c")
_CHIP_FLIPS = ((1, 0), (0, 1), (1, 1))


def _position():
    return tuple(lax.axis_index(a) for a in _AXES)


def _flipped(pos, flip):
    return tuple(1 - p if f else p for p, f in zip(pos, flip))


def _join_comms(comms):
    n_in = [len(c.ins) for c in comms]
    n_out = [len(c.outs) for c in comms]
    n_sem = [len(c.sems) for c in comms]

    def parts(ins, outs, sems):
        for k in range(len(comms)):
            a, b, s = sum(n_in[:k]), sum(n_out[:k]), sum(n_sem[:k])
            yield comms[k], (ins[a:a + n_in[k]], outs[b:b + n_out[k]], sems[s:s + n_sem[k]])

    def start(ins, outs, sems):
        for c, part in parts(ins, outs, sems):
            c.start(*part)

    def finish(ins, outs, sems):
        for c, part in parts(ins, outs, sems):
            c.finish(*part)

    joint = _Comm([a for c in comms for a in c.ins], [o for c in comms for o in c.outs],
                  [s for c in comms for s in c.sems], start, finish)
    return joint, lambda res: [list(res[sum(n_out[:k]):sum(n_out[:k + 1])]) for k in range(len(comms))]


def _run_comms(comms, name):
    joint, split = _join_comms(comms)

    def body(*refs):
        n_in, n_out = len(joint.ins), len(joint.outs)
        joint.start(refs[:n_in], refs[n_in:n_in + n_out], refs[n_in + n_out:])
        joint.finish(refs[:n_in], refs[n_in:n_in + n_out], refs[n_in + n_out:])

    res = pl.pallas_call(
        body, in_specs=[HBM_SPEC] * len(joint.ins), out_specs=[HBM_SPEC] * len(joint.outs), out_shape=joint.outs,
        scratch_shapes=joint.sems, name=name,
    )(*joint.ins)
    return split(res)


def _all_gather_comm(v, axes):
    flips = [f for f in np.ndindex(2, 2, 2) if any(f) and all(a in axes or not b for a, b in zip(_AXES, f))]
    n = len(flips) + 1

    def copies(ins, outs, sems):
        (v_ref,), (o_ref,), (send_sems, recv_sems, local_sem) = ins, outs, sems
        pos = _position()
        slot = 0
        for a, p in zip(_AXES, pos):
            if a in axes:
                slot = 2 * slot + p
        local = pltpu.make_async_copy(v_ref, o_ref.at[slot], local_sem)
        remote = [pltpu.make_async_remote_copy(v_ref, o_ref.at[slot], send_sems.at[k], recv_sems.at[k],
                                               device_id=_flipped(pos, flip), device_id_type=MESH)
                  for k, flip in enumerate(flips)]
        return [local] + remote

    def start(ins, outs, sems):
        for cp in copies(ins, outs, sems):
            cp.start()

    def finish(ins, outs, sems):
        for cp in copies(ins, outs, sems):
            cp.wait()

    sems = [pltpu.SemaphoreType.DMA((n - 1,)), pltpu.SemaphoreType.DMA((n - 1,)), pltpu.SemaphoreType.DMA(())]
    return _Comm([v], [_sds((n,) + v.shape, v.dtype)], sems, start, finish)


def _all_gather_two_level_comm(v):
    def copies(ins, outs, sems, onward):
        (v_ref,), (o_ref,), (send_sems, recv_sems, local_sem) = ins, outs, sems
        x, y, c = _position()
        sibling = (x, y, 1 - c)
        slot = lambda px, py, pc: o_ref.at[4 * px + 2 * py + pc]
        own = pltpu.make_async_copy(v_ref, slot(x, y, c), local_sem)
        first = [pltpu.make_async_remote_copy(v_ref, slot(x, y, c), send_sems.at[0], recv_sems.at[0],
                                              device_id=sibling, device_id_type=MESH)]
        fwd = []
        for k, flip in enumerate(_CHIP_FLIPS):
            px, py = _flipped((x, y), flip)
            first.append(pltpu.make_async_remote_copy(v_ref, slot(x, y, c), send_sems.at[1 + k], recv_sems.at[1 + k],
                                                      device_id=(px, py, c), device_id_type=MESH))
            if onward:
                fwd.append(pltpu.make_async_remote_copy(slot(px, py, c), slot(px, py, c), send_sems.at[4 + k],
                                                        recv_sems.at[4 + k], device_id=sibling, device_id_type=MESH))
        return own, first, fwd

    def start(ins, outs, sems):
        own, first, _ = copies(ins, outs, sems, False)
        for cp in [own] + first:
            cp.start()

    def finish(ins, outs, sems):
        own, first, fwd = copies(ins, outs, sems, True)
        for arrived, onward in zip(first[1:], fwd):
            arrived.wait_recv()
            onward.start()
        first[0].wait_recv()
        for cp in fwd:
            cp.wait_recv()
        for cp in first + fwd:
            cp.wait_send()
        own.wait()

    sems = [pltpu.SemaphoreType.DMA((7,)), pltpu.SemaphoreType.DMA((7,)), pltpu.SemaphoreType.DMA(())]
    return _Comm([v], [_sds((8,) + v.shape, v.dtype)], sems, start, finish)


def _all_gather(v, axes, name):
    return _run_comms([_all_gather_comm(v, axes)], name)[0][0]


class _Item:
    def __init__(self, key, layer, shape, shard_axis, half_axis):
        self.key, self.layer, self.shape = key, layer, tuple(shape)
        self.shard_axis, self.half_axis = shard_axis, half_axis
        self.shard = shape[shard_axis] // 4
        self.half = shape[half_axis] // 2

    def sized(self, shard=False, half=False):
        s = list(self.shape)
        if shard:
            s[self.shard_axis] = self.shard
        if half:
            s[self.half_axis] = self.half
        return tuple(s)

    def window(self, ref, chip=None, half=None):
        idx = [slice(None)] * len(self.shape)
        if chip is not None:
            idx[self.shard_axis] = pl.ds(chip * self.shard, self.shard)
        if half is not None:
            idx[self.half_axis] = pl.ds(half * self.half, self.half)
        return ref.at[tuple(idx)]


def _items(d, w):
    out = []
    for j in range(2):
        out += [_Item("pool_w_in", j, (d, 2 * w), 1, 0), _Item("pool_w_grp", j, (4, w // 4, w // 4), 1, 0),
                _Item("pool_w_out", j, (w, d), 0, 1)]
    out += [_Item("na_w_in", 0, (d, 4 * w), 1, 0), _Item("na_w_out", 0, (w, d), 0, 1),
            _Item("conv_w_in", 0, (d, 4 * w), 1, 0), _Item("conv_w_out", 0, (w, d), 0, 1)]
    return out


def _gather_comm(shards, items):
    n = len(items)

    def copies(src, dst, sems, onward):
        send_a, recv_a, send_b, recv_b, send_c, recv_c = sems
        x, y, c = _position()
        chip = 2 * x + y
        sibling = (x, y, 1 - c)
        own, out, fwd, fwd_in = [], [], [], []
        for i, it in enumerate(items):
            own.append(pltpu.make_async_remote_copy(src[i], it.window(dst[i], chip=chip), send_c.at[i], recv_c.at[i],
                                                    device_id=sibling, device_id_type=MESH))
            for k, flip in enumerate(_CHIP_FLIPS):
                px, py = _flipped((x, y), flip)
                s = 3 * i + k
                out.append(pltpu.make_async_remote_copy(
                    it.window(src[i], half=c), it.window(dst[i], chip=chip, half=c), send_a.at[s], recv_a.at[s],
                    device_id=(px, py, c), device_id_type=MESH))
                if onward:
                    got = it.window(dst[i], chip=2 * px + py, half=c)
                    fwd.append(pltpu.make_async_remote_copy(got, got, send_b.at[s], recv_b.at[s],
                                                            device_id=sibling, device_id_type=MESH))
                    other = it.window(dst[i], chip=2 * px + py, half=1 - c)
                    fwd_in.append(pltpu.make_async_remote_copy(other, other, send_b.at[s], recv_b.at[s],
                                                               device_id=sibling, device_id_type=MESH))
        return own, out, fwd, fwd_in

    def start(src, dst, sems):
        own, out, _, _ = copies(src, dst, sems, False)
        for cp in own + out:
            cp.start()

    def finish(src, dst, sems):
        own, out, fwd, fwd_in = copies(src, dst, sems, True)
        for arrived, onward in zip(out, fwd):
            arrived.wait_recv()
            onward.start()
        for cp in fwd_in:
            cp.wait_recv()
        for cp in out + fwd:
            cp.wait_send()
        for cp in own:
            cp.wait()

    sems = [pltpu.SemaphoreType.DMA((3 * n,)) for _ in range(4)] + [pltpu.SemaphoreType.DMA((n,)) for _ in range(2)]
    return _Comm(shards, [_sds(it.shape, BF16) for it in items], sems, start, finish)


def _pair_swap_copies(windows):
    def copies(src, got, sems):
        send_sems, recv_sems = sems
        x, y, c = _position()
        return [pltpu.make_async_remote_copy(windows[i](src[i], 1 - c), got[i], send_sems.at[i], recv_sems.at[i],
                                             device_id=(x, y, 1 - c), device_id_type=MESH)
                for i in range(len(windows))]

    return copies


def _pair_swap_comm(arrays, windows, out_shapes):
    n = len(arrays)
    copies = _pair_swap_copies(windows)

    def start(src, got, sems):
        for cp in copies(src, got, sems):
            cp.start()

    def finish(src, got, sems):
        for cp in copies(src, got, sems):
            cp.wait()

    return _Comm(arrays, out_shapes, [pltpu.SemaphoreType.DMA((n,)), pltpu.SemaphoreType.DMA((n,))], start, finish)


def _pair_swap(arrays, windows, out_shapes, name):
    return _run_comms([_pair_swap_comm(arrays, windows, out_shapes)], name)[0]


def _chip_exchange_copies(items):
    def copies(src, dst, sems):
        send_sems, recv_sems = sems
        x, y, c = _position()
        out = []
        for i, it in enumerate(items):
            for k, flip in enumerate(_CHIP_FLIPS):
                px, py = _flipped((x, y), flip)
                out.append(pltpu.make_async_remote_copy(
                    it.window(src[i], chip=2 * px + py), dst[i].at[k], send_sems.at[3 * i + k],
                    recv_sems.at[3 * i + k], device_id=(px, py, c), device_id_type=MESH))
        return out

    return copies


_SEM_SPEC = pl.BlockSpec(memory_space=pltpu.SEMAPHORE)
_DATAFLOW = pltpu.SideEffectType.DATAFLOW_SIDE_EFFECTING


def _split_start(copies, srcs, zones, n_copies, name):
    n, nz = len(srcs), len(zones)

    def body(*refs):
        src, land = refs[:n], refs[n:n + nz]
        send_sems, recv_sems = refs[n + nz:n + nz + 2]
        token = refs[-1]
        for cp in copies(src, land, (send_sems, recv_sems)):
            cp.start()
        token[...] = jnp.zeros(token.shape, F32)

    hbm = lambda t: pltpu.HBM(t.shape, t.dtype)
    res = pl.pallas_call(
        body, name=name,
        out_shape=(pltpu.SemaphoreType.DMA((n_copies,)), pltpu.SemaphoreType.DMA((n_copies,)),
                   *[hbm(t) for t in list(srcs) + list(zones)], _sds((8, 128), F32)),
        in_specs=[HBM_SPEC] * (n + nz),
        out_specs=(_SEM_SPEC, _SEM_SPEC, *[HBM_SPEC] * (n + nz), pl.BlockSpec(memory_space=pltpu.VMEM)),
        input_output_aliases={i: 2 + i for i in range(n + nz)},
        compiler_params=pltpu.CompilerParams(has_side_effects=_DATAFLOW),
    )(*[pltpu.with_memory_space_constraint(t, pltpu.HBM) for t in list(srcs) + list(zones)])
    return (res[0], res[1], list(res[2:2 + n]), list(res[2 + n:2 + n + nz])), res[-1]


def _split_wait(copies, handle, after, name):
    send_sems, recv_sems, srcs, zones = handle
    n, nz = len(srcs), len(zones)

    def body(*refs):
        src, land = refs[:n], refs[n:n + nz]
        send, recv = refs[n + nz:n + nz + 2]
        for cp in copies(src, land, (send, recv)):
            cp.wait_send()
            cp.wait_recv()

    hbm = lambda t: pltpu.HBM(t.shape, t.dtype)
    res = pl.pallas_call(
        body, name=name, out_shape=tuple(hbm(t) for t in list(srcs) + list(zones)),
        in_specs=[HBM_SPEC] * (n + nz) + [_SEM_SPEC, _SEM_SPEC, pl.BlockSpec(memory_space=pl.ANY)],
        out_specs=tuple([HBM_SPEC] * (n + nz)), input_output_aliases={i: i for i in range(n + nz)},
        compiler_params=pltpu.CompilerParams(has_side_effects=_DATAFLOW),
    )(*srcs, *zones, send_sems, recv_sems, after)
    return list(res[:n]), list(res[n:])


def _gather_ici_copies(items):
    def copies(src, dst, sems):
        send_sems, recv_sems = sems
        x, y, c = _position()
        chip = 2 * x + y
        out = []
        for i, it in enumerate(items):
            for k, flip in enumerate(_CHIP_FLIPS):
                px, py = _flipped((x, y), flip)
                out.append(pltpu.make_async_remote_copy(
                    it.window(src[i], half=c), it.window(dst[i], chip=chip, half=c), send_sems.at[3 * i + k],
                    recv_sems.at[3 * i + k], device_id=(px, py, c), device_id_type=MESH))
        return out

    return copies


def _gather_pair_finish(shards, mats, items, name):
    n = len(items)

    def body(*refs):
        src, dst = refs[:n], refs[2 * n:3 * n]
        send_own, recv_own, send_fwd, recv_fwd = refs[3 * n:]
        x, y, c = _position()
        chip = 2 * x + y
        sibling = (x, y, 1 - c)
        copies = []
        for i, it in enumerate(items):
            copies.append(pltpu.make_async_remote_copy(src[i], it.window(dst[i], chip=chip), send_own.at[i],
                                                       recv_own.at[i], device_id=sibling, device_id_type=MESH))
            for k, flip in enumerate(_CHIP_FLIPS):
                px, py = _flipped((x, y), flip)
                got = it.window(dst[i], chip=2 * px + py, half=c)
                copies.append(pltpu.make_async_remote_copy(got, got, send_fwd.at[3 * i + k], recv_fwd.at[3 * i + k],
                                                           device_id=sibling, device_id_type=MESH))
        for cp in copies:
            cp.start()
        for cp in copies:
            cp.wait()

    return pl.pallas_call(
        body, in_specs=[HBM_SPEC] * (2 * n), out_specs=[HBM_SPEC] * n, out_shape=[_sds(it.shape, BF16) for it in items],
        input_output_aliases={n + i: i for i in range(n)},
        scratch_shapes=[pltpu.SemaphoreType.DMA((n,)), pltpu.SemaphoreType.DMA((n,)),
                        pltpu.SemaphoreType.DMA((3 * n,)), pltpu.SemaphoreType.DMA((3 * n,))], name=name,
    )(*shards, *mats)


def _chip_exchange_comm(partials, items):
    n = len(items)
    copies = _chip_exchange_copies(items)

    def start(src, dst, sems):
        for cp in copies(src, dst, sems):
            cp.start()

    def finish(src, dst, sems):
        for cp in copies(src, dst, sems):
            cp.wait()

    return _Comm(partials, [_sds((3,) + it.sized(shard=True, half=True), BF16) for it in items],
                 [pltpu.SemaphoreType.DMA((3 * n,)), pltpu.SemaphoreType.DMA((3 * n,))], start, finish)


_SUM_STEPS = 2


def _pair_sums(gs, gots, its, pos, name):
    n = len(its)
    nb = _SUM_STEPS
    g2 = [g.reshape(-1, g.shape[-1]) for g in gs]
    got2 = [t.reshape(-1, t.shape[-1]) for t in gots]

    def body(pos_ref, *refs):
        for g_ref, got_ref, o_ref in zip(refs[:n], refs[n:2 * n], refs[2 * n:]):
            o_ref[...] = (g_ref[...].astype(F32) + got_ref[...].astype(F32)).astype(BF16)

    g_specs, got_specs = [], []
    for it, t in zip(its, got2):
        rows, cols = t.shape
        blk = (rows // nb, cols)
        g_map = (lambda i, pos: (pos[1] * nb + i, 0)) if it.half_axis == 0 else (lambda i, pos: (i, pos[1]))
        g_specs.append(pl.BlockSpec(blk, g_map))
        got_specs.append(pl.BlockSpec(blk, lambda i, pos: (i, 0)))
    outs = pl.pallas_call(
        body, grid_spec=pltpu.PrefetchScalarGridSpec(
            num_scalar_prefetch=1, grid=(nb,), in_specs=g_specs + got_specs, out_specs=got_specs),
        out_shape=[_sds(t.shape, BF16) for t in got2], name=name, compiler_params=_cparams("parallel"),
    )(pos, *g2, *got2)
    return [o.reshape(t.shape) for o, t in zip(outs, gots)]


_FLIP_SLOT = {2: 0, 1: 1, 3: 2}


def _chip_sums(pairs, slots, its, pos, name):
    n = len(its)
    nb = _SUM_STEPS

    def body(pos_ref, *refs):
        chip = pos_ref[0]
        for own in range(4):
            @pl.when(chip == own)
            def _():
                for p_ref, s_ref, o_ref in zip(refs[:n], refs[n:2 * n], refs[2 * n:]):
                    acc = None
                    for k in range(4):
                        v = (p_ref[...] if k == own else s_ref[_FLIP_SLOT[own ^ k]]).astype(F32)
                        acc = v if acc is None else acc + v
                    o_ref[...] = acc

    p_specs, s_specs, o_specs, shapes = [], [], [], []
    for it in its:
        shape = it.sized(shard=True, half=True)
        blk = (shape[0] // nb,) + shape[1:]
        rest = (0,) * (len(shape) - 1)

        def p_map(i, pos, it=it, nd=len(shape)):
            lead = i + (pos[0] * nb if it.shard_axis == 0 else 0)
            return (lead,) + tuple(pos[0] if ax == it.shard_axis else 0 for ax in range(1, nd))

        p_specs.append(pl.BlockSpec(blk, p_map))
        s_specs.append(pl.BlockSpec((3,) + blk, lambda i, pos, rest=rest: (0, i) + rest))
        o_specs.append(pl.BlockSpec(blk, lambda i, pos, rest=rest: (i,) + rest))
        shapes.append(_sds(shape, F32))
    return pl.pallas_call(
        body, grid_spec=pltpu.PrefetchScalarGridSpec(
            num_scalar_prefetch=1, grid=(nb,), in_specs=p_specs + s_specs, out_specs=o_specs),
        out_shape=shapes, name=name, compiler_params=_cparams("parallel"),
    )(pos, *pairs, *slots)


_GRAD_KEYS = ("pool_w_in", "pool_w_grp", "pool_w_out", "na_w_in", "na_w_out", "conv_w_in", "conv_w_out")


def _adamw_matrix(w, m, v, owns, others, it, pos, name):
    nl = w.shape[0]
    rows_split = it.half_axis == 0
    r, cdim = int(np.prod(w.shape[1:-1])), w.shape[-1]
    hr, hc = (r // 2, cdim) if rows_split else (r, cdim // 2)
    br = min(hr, 256)
    nb = hr // br
    c1 = 1.0 - ADAM_B1 ** ADAM_STEP
    c2 = 1.0 - ADAM_B2 ** ADAM_STEP

    def body(pos_ref, w_ref, m_ref, v_ref, *rest):
        own_refs, other_refs = rest[:nl], rest[nl:2 * nl]
        g_ref, d_ref, nm_ref, nv_ref = rest[2 * nl:]
        j, h = pl.program_id(0), pl.program_id(1)
        own, other = own_refs[0][...], other_refs[0][...]
        for q in range(1, nl):
            own = jnp.where(j == q, own_refs[q][...], own)
            other = jnp.where(j == q, other_refs[q][...], other)
        gv = jnp.where(h == pos_ref[1], own, other)
        nm = ADAM_B1 * m_ref[...] + (1.0 - ADAM_B1) * gv
        nv = ADAM_B2 * v_ref[...] + (1.0 - ADAM_B2) * (gv * gv)
        g_ref[...] = gv
        nm_ref[...] = nm
        nv_ref[...] = nv
        d_ref[...] = -ADAM_LR * ((nm / c1) / (jnp.sqrt(nv / c2) + ADAM_EPS) + ADAM_WD * w_ref[...])

    if rows_split:
        full = pl.BlockSpec((None, br, hc), lambda j, h, i, pos: (j, h * nb + i, 0))
    else:
        full = pl.BlockSpec((None, br, hc), lambda j, h, i, pos: (j, i, h))
    half = pl.BlockSpec((br, hc), lambda j, h, i, pos: (i, 0))
    flat = lambda t: t.reshape(nl, r, cdim)
    outs = pl.pallas_call(
        body, grid_spec=pltpu.PrefetchScalarGridSpec(
            num_scalar_prefetch=1, grid=(nl, 2, nb), in_specs=[full] * 3 + [half] * (2 * nl), out_specs=[full] * 4),
        out_shape=[_sds((nl, r, cdim), F32)] * 4, name=name,
        compiler_params=_cparams("parallel", "parallel", "parallel"),
    )(pos, flat(w), flat(m), flat(v), *[t.reshape(hr, hc) for t in list(owns) + list(others)])
    return tuple(t.reshape(w.shape) for t in outs)


_WEIGHTS = ("c_ctx", "norm_g", "ada_w", "ada_b", "pool_w_in", "pool_w_grp", "pool_scale", "pool_w_out", "na_w_in",
            "na_rpb", "na_w_out", "conv_w_in", "conv_dw", "conv_db", "conv_w_out", "final_g")
_COND_ROWS = 16


def _modulations(cond, ada_w, ada_b_cols):
    nl, d, n = ada_w.shape
    return _matmul(
        cond, ada_w, mode="nn", grid=(nl, 1), a_silu=True, epilogue="bias",
        a_spec=pl.BlockSpec((_COND_ROWS, d), lambda i, j: (0, 0)), b_spec=pl.BlockSpec((None, d, n), lambda i, j: (i, 0, 0)),
        extra=(ada_b_cols,), extra_specs=(pl.BlockSpec((None, 1, n), lambda i, j: (i, 0, 0)),),
        out_shapes=[_sds((nl, _COND_ROWS, n), F32)], out_specs=[pl.BlockSpec((None, _COND_ROWS, n), lambda i, j: (i, 0, 0))],
        name="modulations")[0]


def _ada_w_step(cond, dm_cols, w, m, v):
    nl, d, n = w.shape
    tr = d // 2
    c1 = 1.0 - ADAM_B1 ** ADAM_STEP
    c2 = 1.0 - ADAM_B2 ** ADAM_STEP

    def body(c_ref, dm_ref, w_ref, m_ref, v_ref, g_ref, d_ref, nm_ref, nv_ref):
        gv = lax.dot_general(_silu(c_ref[...]).astype(BF16), dm_ref[...].astype(BF16), _DIMS["tn"],
                             preferred_element_type=F32)
        nm = ADAM_B1 * m_ref[...] + (1.0 - ADAM_B1) * gv
        nv = ADAM_B2 * v_ref[...] + (1.0 - ADAM_B2) * (gv * gv)
        g_ref[...] = gv
        nm_ref[...] = nm
        nv_ref[...] = nv
        d_ref[...] = -ADAM_LR * ((nm / c1) / (jnp.sqrt(nv / c2) + ADAM_EPS) + ADAM_WD * w_ref[...])

    blk = pl.BlockSpec((None, tr, n), lambda l, i: (l, i, 0))
    return _call(
        body, (cond, dm_cols, w, m, v), grid=(nl, d // tr),
        in_specs=[pl.BlockSpec((_COND_ROWS, tr), lambda l, i: (0, i)),
                  pl.BlockSpec((None, _COND_ROWS, n), lambda l, i: (l, 0, 0)), blk, blk, blk],
        out_specs=[blk] * 4, out_shape=[_sds(w.shape, F32)] * 4, name="adamw_ada_w")


def _cond_grad(dm_cols, ada_w):
    nl, d, n = ada_w.shape
    return _matmul(
        dm_cols, ada_w, mode="nt", grid=(1, nl), nk=nl, acc_shape=(_COND_ROWS, d),
        a_spec=pl.BlockSpec((None, _COND_ROWS, n), lambda i, q: (q, 0, 0)), b_spec=pl.BlockSpec((None, d, n), lambda i, q: (q, 0, 0)),
        out_shapes=[_sds((_COND_ROWS, d), F32)], out_specs=[pl.BlockSpec((_COND_ROWS, d), lambda i, q: (0, 0))],
        name="cond_grad")[0]


def _pack(parts):
    flat = [p.reshape(-1) for p in parts]
    sizes = [f.shape[0] for f in flat]
    total = sum(sizes)
    rows = -(-total // 1024) * 8
    packed = jnp.concatenate(flat + [jnp.zeros((rows * 128 - total,), F32)]).reshape(rows, 128)
    offs = np.concatenate([[0], np.cumsum(sizes)])[:-1]
    return packed, [(int(o), p.shape) for o, p in zip(offs, parts)]


def _unpack(flat, layout, k):
    off, shape = layout[k]
    return flat[..., off:off + int(np.prod(shape))].reshape(flat.shape[:-1] + tuple(shape))


def kernel(x, c, ctx, c_ctx, norm_g, ada_w, ada_b, pool_w_in, pool_w_grp, pool_scale, pool_w_out, na_w_in, na_rpb, na_w_out, conv_w_in, conv_dw, conv_db, conv_w_out, final_g, loss_target, m_c_ctx, m_norm_g, m_ada_w, m_ada_b, m_pool_w_in, m_pool_w_grp, m_pool_scale, m_pool_w_out, m_na_w_in, m_na_rpb, m_na_w_out, m_conv_w_in, m_conv_dw, m_conv_db, m_conv_w_out, m_final_g, v_c_ctx, v_norm_g, v_ada_w, v_ada_b, v_pool_w_in, v_pool_w_grp, v_pool_scale, v_pool_w_out, v_na_w_in, v_na_rpb, v_na_w_out, v_conv_w_in, v_conv_dw, v_conv_db, v_conv_w_out, v_final_g):
    params = dict(c_ctx=c_ctx, norm_g=norm_g, ada_w=ada_w, ada_b=ada_b, pool_w_in=pool_w_in, pool_w_grp=pool_w_grp,
                  pool_scale=pool_scale, pool_w_out=pool_w_out, na_w_in=na_w_in, na_rpb=na_rpb, na_w_out=na_w_out,
                  conv_w_in=conv_w_in, conv_dw=conv_dw, conv_db=conv_db, conv_w_out=conv_w_out, final_g=final_g)
    mom1 = dict(c_ctx=m_c_ctx, norm_g=m_norm_g, ada_w=m_ada_w, ada_b=m_ada_b, pool_w_in=m_pool_w_in,
                pool_w_grp=m_pool_w_grp, pool_scale=m_pool_scale, pool_w_out=m_pool_w_out, na_w_in=m_na_w_in,
                na_rpb=m_na_rpb, na_w_out=m_na_w_out, conv_w_in=m_conv_w_in, conv_dw=m_conv_dw, conv_db=m_conv_db,
                conv_w_out=m_conv_w_out, final_g=m_final_g)
    mom2 = dict(c_ctx=v_c_ctx, norm_g=v_norm_g, ada_w=v_ada_w, ada_b=v_ada_b, pool_w_in=v_pool_w_in,
                pool_w_grp=v_pool_w_grp, pool_scale=v_pool_scale, pool_w_out=v_pool_w_out, na_w_in=v_na_w_in,
                na_rpb=v_na_rpb, na_w_out=v_na_w_out, conv_w_in=v_conv_w_in, conv_dw=v_conv_dw, conv_db=v_conv_db,
                conv_w_out=v_conv_w_out, final_g=v_final_g)
    d = x.shape[-1]
    w = na_w_out.shape[1] * 4
    xi, yi, ci = _position()
    chip = 2 * xi + yi
    dev = 2 * chip + ci
    n_ada = ada_w.shape[-1]

    def chip_cols(a, size):
        return lax.dynamic_slice_in_dim(a, chip * size, size, axis=a.ndim - 1)

    items = _items(d, w)
    first = [it for it in items if it.key.startswith("pool") and it.layer == 0]
    na = [it for it in items if it.key.startswith("na")]
    late = [it for it in items if it not in first + na]
    shards_of = lambda its: [params[it.key][it.layer].astype(BF16) for it in its]
    empties = lambda its: [lax.empty(it.shape, BF16) for it in its]
    first_copies, na_copies = _gather_ici_copies(first), _gather_ici_copies(na)

    conds = _all_gather(c.reshape(8, d // 8), _AXES, "gather_cond").reshape(8, d)
    behind = conds[0, 0] * 0.0
    first_handle, token = _split_start(first_copies, [s + behind.astype(BF16) for s in shards_of(first)],
                                       empties(first), 3 * len(first), "gather_first_start")
    cond = jnp.concatenate([conds + token[0, 0], c_ctx[None], jnp.zeros((_COND_ROWS - 9, d), F32)], axis=0)
    mod_cols = _modulations(cond, ada_w, chip_cols(ada_b, n_ada)[:, None, :])
    small_pack, small_layout = _pack([pool_scale, conv_dw, conv_db])
    (mod_all,), (small,) = _run_comms([_all_gather_comm(mod_cols, ("x", "y")),
                                       _all_gather_comm(small_pack, ("x", "y"))], "gather_mod")
    behind = mod_all[0, 0, 0, 0] * 0.0
    na_handle, token = _split_start(na_copies, [s + behind.astype(BF16) for s in shards_of(na)], empties(na),
                                    3 * len(na), "gather_na_start")
    first_shards, first_mats = _split_wait(first_copies, first_handle, token, "gather_first_wait")
    first_mats = _gather_pair_finish(first_shards, first_mats, first, "gather_first_pair")
    mod_all = mod_all.transpose(1, 2, 0, 3).reshape(4, _COND_ROWS, 3, d)
    mod = jnp.stack([lax.dynamic_index_in_dim(mod_all, dev, axis=1, keepdims=False), mod_all[:, 8]], axis=1)
    full = {(it.key, it.layer): mat for it, mat in zip(first, first_mats)}
    late_comm = _gather_comm(shards_of(late), late)

    def na_weights(after):
        na_shards, na_mats = _split_wait(na_copies, na_handle, after, "gather_na_wait")
        na_mats = _gather_pair_finish(na_shards, na_mats, na, "gather_na_pair")
        return {it.key: mat for it, mat in zip(na, na_mats)}

    def late_weights(mats):
        full.update({(it.key, it.layer): mat for it, mat in zip(late, mats)})
        return dict(pool_w_in=[full[("pool_w_in", j)] for j in range(2)],
                    pool_w_grp=[full[("pool_w_grp", j)] for j in range(2)],
                    pool_w_out=[full[("pool_w_out", j)] for j in range(2)],
                    conv_w_in=full[("conv_w_in", 0)], conv_w_out=full[("conv_w_out", 0)])

    small = small.reshape(4, -1)

    def whole(k):
        parts = _unpack(small, small_layout, k)
        return jnp.moveaxis(parts, 0, -2).reshape(parts.shape[1:-1] + (-1,))

    wts = dict(pool_w_in=[full[("pool_w_in", 0)]], pool_w_grp=[full[("pool_w_grp", 0)]],
               pool_w_out=[full[("pool_w_out", 0)]], pool_scale=whole(0), na_rpb=na_rpb[0], conv_dw=whole(1)[0],
               conv_db=whole(2))
    pos = jnp.stack([chip, ci]).astype(jnp.int32)

    def layer_grads(its, by_layer):
        pick = {"pool_w_in": "w_in", "pool_w_grp": "w_grp", "pool_w_out": "w_out", "na_w_in": "w_in",
                "na_w_out": "w_out", "conv_w_in": "w_in", "conv_w_out": "w_out"}
        return [by_layer[(it.key.split("_")[0], it.layer)][pick[it.key]] for it in its]

    pairs, handles = dict(), dict()
    half_windows = lambda its: [(lambda ref, half, it=it: it.window(ref, half=half)) for it in its]
    half_shapes = lambda its: [_sds(it.sized(half=True), BF16) for it in its]

    def pair_sums(its, mats, tag):
        got = _pair_swap(mats, half_windows(its), half_shapes(its), f"pair_exchange_{tag}")
        return _pair_sums(mats, got, its, pos, f"pair_sum_{tag}")

    def grad_comm(gr3, gr2):
        pairs["late"] = pair_sums(late, layer_grads(late, {("pool", 1): gr3, ("conv", 0): gr2}), "late")
        return _chip_exchange_comm(pairs["late"], late)

    slot_zones = lambda its: [lax.empty((3,) + it.sized(shard=True, half=True), BF16) for it in its]
    na_xcopies, first_xcopies = _chip_exchange_copies(na), _chip_exchange_copies(first)

    def na_grads_start(gr1):
        pairs["na"] = pair_sums(na, layer_grads(na, {("na", 0): gr1}), "na")
        handles["na"], started = _split_start(na_xcopies, pairs["na"], slot_zones(na), 3 * len(na),
                                              "exchange_na_start")
        return started

    res = _example_step(x[0], ctx[0], loss_target[0], mod, norm_g, final_g[None], wts, dict(
        na_weights=na_weights, late_comm=late_comm, late_weights=late_weights, grad_comm=grad_comm,
        na_grads_start=na_grads_start))
    g0, g1, g2, g3 = res["layers"]
    pairs["na"], na_slots = _split_wait(na_xcopies, handles["na"], g0["w_in"], "exchange_na_wait")
    first_grads = layer_grads(first, {("pool", 0): g0})
    packed, layout = _pack([res["dfinal_g"], res["dnorm_g"], res["dmod"], g1["rpb"],
                            jnp.concatenate([g0["scale"], g3["scale"]], axis=0), g2["dw"], g2["db"],
                            res["loss"][0, :1]])
    first_got, (every,) = _run_comms([_pair_swap_comm(first_grads, half_windows(first), half_shapes(first)),
                                      _all_gather_two_level_comm(packed)], "pair_exchange_first")
    pairs["first"] = _pair_sums(first_grads, first_got, first, pos, "pair_sum_first")

    grads = dict()
    total = _sum_lead(every, "sum_vec_grads").reshape(-1)
    every = every.reshape(8, -1)
    grads["final_g"] = _unpack(total, layout, 0).reshape(final_g.shape)
    grads["norm_g"] = _unpack(total, layout, 1)
    grads["na_rpb"] = _unpack(total, layout, 3)[None]
    grads["pool_scale"] = chip_cols(_unpack(total, layout, 4), pool_scale.shape[-1])
    grads["conv_dw"] = chip_cols(_unpack(total, layout, 5), conv_dw.shape[-1])[None]
    grads["conv_db"] = chip_cols(_unpack(total, layout, 6), conv_db.shape[-1])
    dmod_sum = _unpack(total, layout, 2).reshape(4, 2, 3 * d)
    dmod_each = _unpack(every, layout, 2).reshape(8, 4, 2, 3 * d)
    grads["ada_b"] = dmod_sum[:, 0] + dmod_sum[:, 1]
    dm = jnp.concatenate([dmod_each[:, :, 0].transpose(1, 0, 2), dmod_sum[:, 1][:, None],
                          jnp.zeros((4, _COND_ROWS - 9, 3 * d), F32)], axis=1)
    dm_cols = chip_cols(dm, n_ada)
    dcond = _cond_grad(dm_cols, ada_w)[8].reshape(8, d // 8)
    dcond_all = _all_gather(dcond, ("x", "y"), "gather_cond_grad")
    behind = dcond_all[0, 0, 0] * 0.0
    handles["first"], token = _split_start(first_xcopies, [p + behind.astype(BF16) for p in pairs["first"]],
                                           slot_zones(first), 3 * len(first), "exchange_first_start")
    grads["ada_w"], *ada_w_step = _ada_w_step(cond, dm_cols + token[0, 0], ada_w, m_ada_w, v_ada_w)
    grads["c_ctx"] = _sum_lead(dcond_all, "sum_cond_grad").reshape(d) * _dsilu(c_ctx)
    vector_out = {k: _adamw(params[k], grads[k], mom1[k], mom2[k], f"adamw_{k}")
                  for k in _WEIGHTS if k not in _GRAD_KEYS + ("ada_w",)}
    vector_out["ada_w"] = tuple(ada_w_step)
    pairs["first"], first_slots = _split_wait(first_xcopies, handles["first"], vector_out["ada_w"][2],
                                              "exchange_first_wait")

    slots = dict(zip(late, res["carried"]))
    slots.update(zip(first, first_slots))
    slots.update(zip(na, na_slots))
    pair_of = dict(zip(late, pairs["late"]))
    pair_of.update(zip(first, pairs["first"]))
    pair_of.update(zip(na, pairs["na"]))
    reduced = _chip_sums([pair_of[it] for it in items], [slots[it] for it in items], items, pos, "chip_sum")
    theirs = _pair_swap(reduced, [lambda ref, half: ref] * len(items),
                        [_sds(t.shape, F32) for t in reduced], "pair_return")
    matrix_out = dict()
    for k in _GRAD_KEYS:
        idx = [i for i, it in enumerate(items) if it.key == k]
        res_k = _adamw_matrix(params[k], mom1[k], mom2[k], [reduced[i] for i in idx], [theirs[i] for i in idx],
                              items[idx[0]], pos, f"adamw_{k}")
        grads[k], matrix_out[k] = res_k[0], res_k[1:]

    outs = [[], [], []]
    for k in _WEIGHTS:
        step = matrix_out[k] if k in matrix_out else vector_out[k]
        for lst, val in zip(outs, step):
            lst.append(val)
    loss = _unpack(total, layout, 7)[0]
    return (loss, res["grad_x"][None], *[grads[k].reshape(params[k].shape) for k in _WEIGHTS],
            *outs[0], *outs[1], *outs[2])
```

```python
import functools

import numpy as np
import jax
import jax.numpy as jnp
from jax import lax
from jax.experimental import pallas as pl
from jax.experimental.pallas import tpu as pltpu

F32 = jnp.float32
BF16 = jnp.bfloat16

EPS = 1e-6
GRID_W = 64
HEAD_DIM = 64
WIN_ROWS = 8
WIN_COLS = 16
POOL_WINDOWS = (2, 4, 8, 16)
Q_ROWS = 4
K_ROWS = 12
PAD_ROWS = 4
NEG = -1e30

ADAM_LR = 0.001
ADAM_B1 = 0.9
ADAM_B2 = 0.999
ADAM_EPS = 1e-08
ADAM_WD = 0.01
ADAM_STEP = 10

ROW_BLOCK = 256
VMEM_LIMIT = 56 * 1024 * 1024
ACT = BF16

MESH = pl.DeviceIdType.MESH
HBM_SPEC = pl.BlockSpec(memory_space=pltpu.HBM)


def _cparams(*sem):
    return pltpu.CompilerParams(dimension_semantics=sem or None, vmem_limit_bytes=VMEM_LIMIT)


def _sds(shape, dtype):
    return jax.ShapeDtypeStruct(tuple(shape), dtype)


def _call(body, args, *, grid, in_specs, out_specs, out_shape, name, scratch_shapes=()):
    return list(pl.pallas_call(
        body, grid=grid, in_specs=list(in_specs), out_specs=list(out_specs), out_shape=list(out_shape),
        scratch_shapes=list(scratch_shapes), name=name, compiler_params=_cparams(*(("arbitrary",) * len(grid))),
    )(*args))


def _sigmoid(x):
    return 1.0 / (1.0 + jnp.exp(-x))


def _silu(x):
    return x * _sigmoid(x)


def _dsilu(x):
    s = _sigmoid(x)
    return s * (1.0 + x * (1.0 - s))


_DIMS = {
    "nn": (((1,), (0,)), ((), ())),
    "nt": (((1,), (1,)), ((), ())),
    "tn": (((0,), (0,)), ((), ())),
}


def _matmul(a, b, *, mode, grid, a_spec, b_spec, out_shapes, out_specs, name, nk=1,
            a_silu=False, exact=False, epilogue=None, extra=(), extra_specs=(), acc_shape=None):
    n_extra = len(extra)
    n_out = len(out_shapes)

    def body(*refs):
        a_ref, b_ref = refs[:2]
        ex = refs[2:2 + n_extra]
        outs = refs[2 + n_extra:2 + n_extra + n_out]
        av = a_ref[...]
        bv = b_ref[...]
        if a_silu:
            av = _silu(av.astype(F32))
        if exact:
            prod = lax.dot_general(av.astype(F32), bv.astype(F32), _DIMS[mode],
                                   precision=lax.Precision.HIGHEST, preferred_element_type=F32)
        else:
            prod = lax.dot_general(av.astype(BF16), bv.astype(BF16), _DIMS[mode], preferred_element_type=F32)

        def finish(res):
            if epilogue == "bias":
                res = res + ex[0][...]
            outs[0][...] = res.astype(outs[0].dtype)

        if nk == 1:
            finish(prod)
        else:
            acc = refs[-1]
            k = pl.program_id(len(grid) - 1)

            @pl.when(k == 0)
            def _():
                acc[...] = prod

            @pl.when(k > 0)
            def _():
                acc[...] += prod

            @pl.when(k == nk - 1)
            def _():
                finish(acc[...])

    scratch = [pltpu.VMEM(acc_shape, F32)] if nk > 1 else []
    sem = ("parallel",) * (len(grid) - 1) + ("arbitrary",)
    return pl.pallas_call(
        body, grid=grid, in_specs=[a_spec, b_spec, *extra_specs], out_specs=list(out_specs),
        out_shape=list(out_shapes), scratch_shapes=scratch, name=name, compiler_params=_cparams(*sem),
    )(a, b, *extra)


def _row_tile(rows):
    for t in (768, 512, 256):
        if rows % t == 0:
            return t
    return rows


def _mm_tn(a, b, name, out_dtype, tm=512):
    r, m = a.shape
    n = b.shape[1]
    tm = min(tm, m)
    tn = min(1024, n)
    return _matmul(
        a, b, mode="tn", grid=(m // tm, n // tn),
        a_spec=pl.BlockSpec((r, tm), lambda i, j: (0, i)), b_spec=pl.BlockSpec((r, tn), lambda i, j: (0, j)),
        out_shapes=[_sds((m, n), out_dtype)], out_specs=[pl.BlockSpec((tm, tn), lambda i, j: (i, j))], name=name)[0]


def _mm_tn_parts(a, b, name, out_dtype, tm=512):
    r, m = a.shape
    p, _, np_ = b.shape
    tm = min(tm, m)
    return _matmul(
        a, b, mode="tn", grid=(m // tm, p),
        a_spec=pl.BlockSpec((r, tm), lambda i, q: (0, i)), b_spec=pl.BlockSpec((None, r, np_), lambda i, q: (q, 0, 0)),
        out_shapes=[_sds((m, p * np_), out_dtype)], out_specs=[pl.BlockSpec((tm, np_), lambda i, q: (i, q))],
        name=name)[0]


def _row_vec(ref, is_ctx):
    return ref[0] if is_ctx is None else jnp.where(is_ctx, ref[1], ref[0])


def _ctx_rows(i, tm, nx, nseg):
    if nseg == 1:
        return None
    return i * tm + lax.broadcasted_iota(jnp.int32, (tm, 1), 0) >= nx


def _seg_sums(ref, val, is_ctx, first):
    if is_ctx is None:
        parts = [jnp.sum(val, axis=0, keepdims=True)]
    else:
        parts = [jnp.sum(jnp.where(is_ctx, 0.0, val), axis=0, keepdims=True),
                 jnp.sum(jnp.where(is_ctx, val, 0.0), axis=0, keepdims=True)]

    @pl.when(first)
    def _():
        for k, p in enumerate(parts):
            ref[k] = p

    @pl.when(jnp.logical_not(first))
    def _():
        for k, p in enumerate(parts):
            ref[k] += p


def _w_out_resid(a, w_out, xres, gate, nx, name):
    m, k = a.shape
    n = w_out.shape[1]
    nseg = gate.shape[0]
    tm = _row_tile(m)

    def body(a_ref, w_ref, x_ref, gt_ref, yx_ref, xo_ref):
        yx = jnp.dot(a_ref[...], w_ref[...], preferred_element_type=F32)
        yx_ref[...] = yx.astype(ACT)
        xo_ref[...] = x_ref[...] + _row_vec(gt_ref, _ctx_rows(pl.program_id(0), tm, nx, nseg)) * yx

    row = pl.BlockSpec((tm, n), lambda i: (i, 0))
    return pl.pallas_call(
        body, grid=(m // tm,),
        in_specs=[pl.BlockSpec((tm, k), lambda i: (i, 0)), pl.BlockSpec((k, n), lambda i: (0, 0)), row,
                  pl.BlockSpec((nseg, 1, n), lambda i: (0, 0, 0))],
        out_specs=[row, row], out_shape=[_sds((m, n), ACT), _sds((m, n), F32)],
        name=name, compiler_params=_cparams("parallel"),
    )(a, w_out, xres, gate)


def _norm_w_in(x, g, scale, shift, w_in, nx, name, ctx=None):
    d = x.shape[1]
    rows = x.shape[0] + (0 if ctx is None else ctx.shape[0])
    n = w_in.shape[1]
    nseg = scale.shape[0]
    tm = _row_tile(rows)
    tn = min(1024, n)
    row = pl.BlockSpec((tm, d), lambda i, j: (i, 0))
    if ctx is None:
        row_args, row_specs = (x,), [row]
    else:
        assert ctx.shape[0] == ROW_BLOCK and tm % ROW_BLOCK == 0 and nx % ROW_BLOCK == 0
        nsub, x_blocks = tm // ROW_BLOCK, nx // ROW_BLOCK
        row_args = (x,) * nsub + (ctx,)
        row_specs = [pl.BlockSpec((ROW_BLOCK, d), lambda i, j, s=s: (jnp.minimum(i * nsub + s, x_blocks - 1), 0))
                     for s in range(nsub)] + [pl.BlockSpec((ROW_BLOCK, d), lambda i, j: (0, 0))]

    def body(*refs):
        x_refs, (g_ref, sc_ref, sh_ref, w_ref), outs = refs[:len(row_args)], refs[len(row_args):][:4], refs[-3:]
        h_ref, r_ref, p_ref = outs
        i, j = pl.program_id(0), pl.program_id(1)

        @pl.when(j == 0)
        def _():
            if ctx is None:
                xv = x_refs[0][...]
            else:
                xv = jnp.concatenate([jnp.where(i * nsub + s >= x_blocks, x_refs[-1][...], x_refs[s][...])
                                      for s in range(nsub)], axis=0)
                refs[-4][...] = xv
            r = lax.rsqrt(jnp.mean(xv * xv, axis=-1, keepdims=True) + EPS)
            is_ctx = _ctx_rows(i, tm, nx, nseg)
            h = (xv * r) * g_ref[...] * (1.0 + _row_vec(sc_ref, is_ctx)) + _row_vec(sh_ref, is_ctx)
            h_ref[...] = h.astype(BF16)
            r_ref[...] = r

        p_ref[...] = jnp.dot(h_ref[...], w_ref[...], preferred_element_type=F32).astype(ACT)

    vec = pl.BlockSpec((nseg, 1, d), lambda i, j: (0, 0, 0))
    joined = [] if ctx is None else [(row, _sds((rows, d), F32))]
    out_specs, out_shape = zip(*joined, (row, _sds((rows, d), BF16)),
                               (pl.BlockSpec((tm, 1), lambda i, j: (i, 0)), _sds((rows, 1), F32)),
                               (pl.BlockSpec((tm, tn), lambda i, j: (i, j)), _sds((rows, n), ACT)))
    return _call(
        body, (*row_args, g, scale, shift, w_in), grid=(rows // tm, n // tn),
        in_specs=[*row_specs, pl.BlockSpec((1, d), lambda i, j: (0, 0)), vec, vec,
                  pl.BlockSpec((d, tn), lambda i, j: (0, j))],
        out_specs=list(out_specs), out_shape=list(out_shape), name=name)


def _gate_w_out_bwd(dxo, yx, gate, w_out, nx, name):
    rows, d = yx.shape
    w = w_out.shape[0]
    nseg = gate.shape[0]
    tm = _row_tile(rows)

    def body(dx_ref, yx_ref, gt_ref, w_ref, dyx_ref, da_ref, dg_ref):
        i = pl.program_id(0)
        is_ctx = _ctx_rows(i, tm, nx, nseg)
        dxv = dx_ref[...]
        dyx = (dxv * _row_vec(gt_ref, is_ctx)).astype(BF16)
        dyx_ref[...] = dyx
        da_ref[...] = lax.dot_general(dyx, w_ref[...], _DIMS["nt"], preferred_element_type=F32).astype(ACT)
        _seg_sums(dg_ref, dxv * yx_ref[...].astype(F32), is_ctx, i == 0)

    row = pl.BlockSpec((tm, d), lambda i: (i, 0))
    vec = pl.BlockSpec((nseg, 1, d), lambda i: (0, 0, 0))
    return _call(
        body, (dxo, yx, gate, w_out), grid=(rows // tm,),
        in_specs=[row, row, vec, pl.BlockSpec((w, d), lambda i: (0, 0))],
        out_specs=[row, pl.BlockSpec((tm, w), lambda i: (i, 0)), vec],
        out_shape=[_sds((rows, d), BF16), _sds((rows, w), ACT), _sds((nseg, 1, d), F32)], name=name)


def _w_in_bwd_norm(dparts, w_in, x, r, g, scale, dres, nx, name):
    np_, rows, kp = dparts.shape
    d = w_in.shape[0]
    nseg = scale.shape[0]
    tm = _row_tile(rows)
    nsub = tm // ROW_BLOCK
    nres_blocks = dres.shape[0] // ROW_BLOCK

    def body(dp_ref, w_ref, x_ref, r_ref, g_ref, sc_ref, *rest):
        dres_refs = rest[:nsub]
        dx_ref, dsh_ref, dge_ref, acc = rest[nsub:]
        i, k = pl.program_id(0), pl.program_id(1)
        prod = lax.dot_general(dp_ref[...], w_ref[...], _DIMS["nt"], preferred_element_type=F32)

        @pl.when(k == 0)
        def _():
            acc[...] = prod

        @pl.when(k > 0)
        def _():
            acc[...] += prod

        @pl.when(k == np_ - 1)
        def _():
            is_ctx = _ctx_rows(i, tm, nx, nseg)
            dhv = acc[...]
            rv = r_ref[...]
            xn = x_ref[...] * rv
            dxn = dhv * (g_ref[...] * (1.0 + _row_vec(sc_ref, is_ctx)))
            dx = rv * (dxn - xn * jnp.mean(dxn * xn, axis=-1, keepdims=True))
            for s in range(nsub):
                piece = slice(s * ROW_BLOCK, (s + 1) * ROW_BLOCK)
                res = dres_refs[s][...]
                if nres_blocks * ROW_BLOCK < rows:
                    res = jnp.where(i * nsub + s < nres_blocks, res, 0.0)
                dx_ref[piece, :] = dx[piece, :] + res
            _seg_sums(dsh_ref, dhv, is_ctx, i == 0)
            _seg_sums(dge_ref, dhv * xn, is_ctx, i == 0)

    row = pl.BlockSpec((tm, d), lambda i, k: (i, 0))
    vec = pl.BlockSpec((nseg, 1, d), lambda i, k: (0, 0, 0))
    return _call(
        body, (dparts, w_in, x, r, g, scale, *([dres] * nsub)), grid=(rows // tm, np_),
        in_specs=[pl.BlockSpec((None, tm, kp), lambda i, k: (k, i, 0)), pl.BlockSpec((d, kp), lambda i, k: (0, k)),
                  row, pl.BlockSpec((tm, 1), lambda i, k: (i, 0)), pl.BlockSpec((1, d), lambda i, k: (0, 0)), vec]
        + [pl.BlockSpec((ROW_BLOCK, d), (lambda i, k, s=s: (jnp.minimum(i * nsub + s, nres_blocks - 1), 0)))
           for s in range(nsub)],
        out_specs=[row, vec, vec],
        out_shape=[_sds((rows, d), F32), _sds((nseg, 1, d), F32), _sds((nseg, 1, d), F32)],
        scratch_shapes=[pltpu.VMEM((tm, d), F32)], name=name)


_PAD_TOP = 16
_PAD_BOT = 32


def _window_sum(buf, xv, lo, n):
    t = xv.shape[0]
    c = xv.shape[1]
    tp = t + _PAD_TOP + _PAD_BOT
    buf[pl.ds(0, _PAD_TOP), :] = jnp.zeros((_PAD_TOP, c), F32)
    buf[pl.ds(_PAD_TOP, t), :] = xv
    buf[pl.ds(_PAD_TOP + t, _PAD_BOT), :] = jnp.zeros((_PAD_BOT, c), F32)
    p = buf[...]
    k = 1
    while k < n:
        p = p + pltpu.roll(p, tp - k, 0)
        k *= 2
    if lo:
        p = pltpu.roll(p, -lo, 0)
    buf[...] = p
    return buf[pl.ds(_PAD_TOP, t), :]


def _window_count(t, half):
    pos = lax.broadcasted_iota(jnp.int32, (t, 1), 0)
    return (jnp.minimum(pos + half, t) - jnp.maximum(pos - half, 0)).astype(F32)


def _segments(rows, nx):
    return [(0, nx)] + ([(nx, rows - nx)] if rows > nx else [])


def _pool_scratch(rows, nx, cols):
    return [pltpu.VMEM((length + _PAD_TOP + _PAD_BOT, cols), F32) for _, length in _segments(rows, nx)]


def _per_group(g, fn):
    for gi, win in enumerate(POOL_WINDOWS):
        pl.when(g == gi)(functools.partial(fn, win))


def _pool_grp_fwd(uv, w_grp, scale, nx, name):
    rows = uv.shape[0]
    ng, gc, _ = w_grp.shape
    w = ng * gc
    segs = _segments(rows, nx)

    def body(u_ref, gt_ref, w_ref, sc_ref, z_ref, mx_ref, a_ref, *bufs):
        def pool(win):
            half = win // 2
            for (start, length), buf in zip(segs, bufs):
                uvv = u_ref[pl.ds(start, length), :].astype(F32)
                s = _window_sum(buf, uvv, -half, win)
                z_ref[pl.ds(start, length), :] = (s / _window_count(length, half) - uvv).astype(BF16)

        _per_group(pl.program_id(0), pool)
        mixed = jnp.dot(z_ref[...], w_ref[...], preferred_element_type=F32)
        mx_ref[...] = mixed.astype(ACT)
        a_ref[...] = (mixed * sc_ref[...] * _silu(gt_ref[...].astype(F32))).astype(BF16)

    col = pl.BlockSpec((rows, gc), lambda g: (0, g))
    return _call(
        body, (uv, uv, w_grp, scale), grid=(ng,),
        in_specs=[col, pl.BlockSpec((rows, gc), lambda g: (0, ng + g)), pl.BlockSpec((None, gc, gc), lambda g: (g, 0, 0)),
                  pl.BlockSpec((1, gc), lambda g: (0, g))],
        out_specs=[col, col, col], out_shape=[_sds((rows, w), BF16), _sds((rows, w), ACT), _sds((rows, w), BF16)],
        scratch_shapes=_pool_scratch(rows, nx, gc), name=name)


def _pool_grp_bwd(da, mixed, uv, scale, w_grp, nx, name):
    rows, w = da.shape
    ng, gc, _ = w_grp.shape
    segs = _segments(rows, nx)

    def body(da_ref, mx_ref, gt_ref, sc_ref, w_ref, dm_ref, duv_ref, dsc_ref, dz_ref, *bufs):
        dav = da_ref[...].astype(F32)
        mixed = mx_ref[...].astype(F32)
        gt = gt_ref[...].astype(F32)
        sg = _silu(gt)
        sc = sc_ref[...]
        dm = (dav * sc * sg).astype(BF16)
        dm_ref[...] = dm
        dz_ref[...] = lax.dot_general(dm, w_ref[...], _DIMS["nt"], preferred_element_type=F32)
        duv_ref[1] = (dav * mixed * sc * _dsilu(gt)).astype(BF16)
        dsc_ref[...] = jnp.sum(dav * mixed * sg, axis=0, keepdims=True)

        def unpool(win):
            half = win // 2
            for (start, length), buf in zip(segs, bufs):
                dzv = dz_ref[pl.ds(start, length), :]
                s = _window_sum(buf, dzv / _window_count(length, half), 1 - half, win)
                duv_ref[0, pl.ds(start, length), :] = (s - dzv).astype(BF16)

        _per_group(pl.program_id(0), unpool)

    col = pl.BlockSpec((rows, gc), lambda g: (0, g))
    vec = pl.BlockSpec((1, gc), lambda g: (0, g))
    return pl.pallas_call(
        body, grid=(ng,),
        in_specs=[col, col, pl.BlockSpec((rows, gc), lambda g: (0, ng + g)), vec,
                  pl.BlockSpec((None, gc, gc), lambda g: (g, 0, 0))],
        out_specs=[col, pl.BlockSpec((2, rows, gc), lambda g: (0, 0, g)), vec],
        out_shape=[_sds((rows, w), BF16), _sds((2, rows, w), BF16), _sds((1, w), F32)],
        scratch_shapes=[pltpu.VMEM((rows, gc), F32)] + _pool_scratch(rows, nx, gc),
        name=name, compiler_params=_cparams("parallel"),
    )(da, mixed, uv, scale, w_grp)


def _grp_wgrad(z, dm, ng, name, out_dtype):
    rows, w = z.shape
    gc = w // ng

    def body(z_ref, dm_ref, o_ref):
        o_ref[...] = lax.dot_general(z_ref[...], dm_ref[...], _DIMS["tn"],
                                     preferred_element_type=F32).astype(o_ref.dtype)

    blk = pl.BlockSpec((rows, gc), lambda g: (0, g))
    return pl.pallas_call(
        body, grid=(ng,), in_specs=[blk, blk], out_specs=pl.BlockSpec((None, gc, gc), lambda g: (g, 0, 0)),
        out_shape=_sds((ng, gc, gc), out_dtype), name=name, compiler_params=_cparams("parallel"),
    )(z, dm)


def _shift_rows(v, by):
    t = v.shape[0]
    pos = lax.broadcasted_iota(jnp.int32, v.shape, 0)
    rolled = pltpu.roll(v, by % t, 0)
    keep = pos >= by if by > 0 else pos < t + by
    return jnp.where(keep, rolled, 0.0)


def _conv_specs(t, w, cb):
    return [pl.BlockSpec((t, cb), (lambda j, q=q: (0, q * (w // cb) + j))) for q in range(4)]


def _conv_fwd(p4, dw, db, name):
    t = p4.shape[0]
    w = p4.shape[1] // 4
    cb = 128

    def body(bg_ref, cg_ref, v_ref, g_ref, dw_ref, db_ref, a_ref):
        tv = cg_ref[...].astype(F32) * v_ref[...].astype(F32)
        conv = (dw_ref[0:1, :] * _shift_rows(tv, 1) + dw_ref[1:2, :] * tv + dw_ref[2:3, :] * _shift_rows(tv, -1)
                + db_ref[...])
        a_ref[...] = (bg_ref[...].astype(F32) * conv * _silu(g_ref[...].astype(F32))).astype(BF16)

    return pl.pallas_call(
        body, grid=(w // cb,),
        in_specs=_conv_specs(t, w, cb) + [pl.BlockSpec((3, cb), lambda j: (0, j)), pl.BlockSpec((1, cb), lambda j: (0, j))],
        out_specs=pl.BlockSpec((t, cb), lambda j: (0, j)), out_shape=_sds((t, w), BF16),
        name=name, compiler_params=_cparams("parallel"),
    )(p4, p4, p4, p4, dw, db)


def _conv_bwd(da, p4, dw, db, name):
    t, w = da.shape
    cb = 128

    def body(da_ref, bg_ref, cg_ref, v_ref, g_ref, dw_ref, db_ref, d4_ref, ddw_ref, ddb_ref):
        cg = cg_ref[...].astype(F32)
        vv = v_ref[...].astype(F32)
        bg = bg_ref[...].astype(F32)
        gv = g_ref[...].astype(F32)
        tv = cg * vv
        tm1 = _shift_rows(tv, 1)
        tp1 = _shift_rows(tv, -1)
        w0, w1, w2 = dw_ref[0:1, :], dw_ref[1:2, :], dw_ref[2:3, :]
        conv = w0 * tm1 + w1 * tv + w2 * tp1 + db_ref[...]
        y = bg * conv
        dav = da_ref[...].astype(F32)
        dy = dav * _silu(gv)
        d4_ref[3] = (dav * y * _dsilu(gv)).astype(BF16)
        d4_ref[0] = (dy * conv).astype(BF16)
        dconv = dy * bg
        ddb_ref[...] = jnp.sum(dconv, axis=0, keepdims=True)
        ddw_ref[0:1, :] = jnp.sum(dconv * tm1, axis=0, keepdims=True)
        ddw_ref[1:2, :] = jnp.sum(dconv * tv, axis=0, keepdims=True)
        ddw_ref[2:3, :] = jnp.sum(dconv * tp1, axis=0, keepdims=True)
        dt = w0 * _shift_rows(dconv, -1) + w1 * dconv + w2 * _shift_rows(dconv, 1)
        d4_ref[1] = (dt * vv).astype(BF16)
        d4_ref[2] = (dt * cg).astype(BF16)

    col = pl.BlockSpec((t, cb), lambda j: (0, j))
    tap = pl.BlockSpec((3, cb), lambda j: (0, j))
    bias = pl.BlockSpec((1, cb), lambda j: (0, j))
    return pl.pallas_call(
        body, grid=(w // cb,), in_specs=[col] + _conv_specs(t, w, cb) + [tap, bias],
        out_specs=[pl.BlockSpec((4, t, cb), lambda j: (0, 0, j)), tap, bias],
        out_shape=[_sds((4, t, w), BF16), _sds((3, w), F32), _sds((1, w), F32)],
        name=name, compiler_params=_cparams("parallel"),
    )(da, p4, p4, p4, p4, dw, db)


def _attn_mask():
    qn, kn = Q_ROWS * GRID_W, K_ROWS * GRID_W
    qr, qc = np.divmod(np.arange(qn), GRID_W)
    kr, kc = np.divmod(np.arange(kn), GRID_W)
    col0 = np.clip(qc - WIN_COLS // 2, 0, GRID_W - WIN_COLS)
    col_ok = (kc[None, :] >= col0[:, None]) & (kc[None, :] < col0[:, None] + WIN_COLS)
    first = np.zeros(qn, np.int64)
    last = np.full(qn, K_ROWS - WIN_ROWS)
    out = []
    for row0 in (first, qr, last):
        row_ok = (kr[None, :] >= row0[:, None]) & (kr[None, :] < row0[:, None] + WIN_ROWS)
        out.append(np.where(row_ok & col_ok, 0.0, NEG))
    return jnp.asarray(np.stack(out), F32)


_KW = K_ROWS * GRID_W
_QB = Q_ROWS * GRID_W
_PAIR = 2 * HEAD_DIM
_N_DR = 2 * WIN_ROWS - 1
_N_DC = 2 * WIN_COLS - 1
_RP_ROWS = 24
_N_TILES = _N_DR + 1
_BIAS_BASE = (WIN_ROWS - 1, WIN_ROWS // 2 - 1, -1)


class _Comm:
    def __init__(self, ins, outs, sems, start, finish):
        self.ins, self.outs, self.sems, self.start, self.finish = list(ins), list(outs), list(sems), start, finish


def _bias_pieces(cls):
    out = []
    for qr in range(Q_ROWS):
        for kr in range(0, K_ROWS, 2):
            tile = _BIAS_BASE[cls] - qr + kr + 1
            out.append((qr, kr, tile if 0 <= tile < _N_TILES else None))
    return out


def _toeplitz_pair(left_row, right_row):
    lane = lax.broadcasted_iota(jnp.int32, (GRID_W, _PAIR), 1)
    shape = (GRID_W, _PAIR)
    left = pltpu.roll(jnp.broadcast_to(left_row, shape), _PAIR - (WIN_COLS - 1), 1, stride=1, stride_axis=0)
    right = pltpu.roll(jnp.broadcast_to(right_row, shape), GRID_W - (WIN_COLS - 1), 1, stride=1, stride_axis=0)
    return jnp.where(lane < GRID_W, left, right)


def _build_tiles(tiles_ref, rp_ref):
    for h in range(2):
        for t in range(_N_TILES):
            tiles_ref[h, t] = _toeplitz_pair(rp_ref[h, t:t + 1, :], rp_ref[h, t + 1:t + 2, :])


def _block_class(b, nblk, fn, entering=False):
    interior = (b == 1) if entering else jnp.logical_and(b > 0, b < nblk - 1)
    for cls, cond in enumerate((b == 0, interior, b == nblk - 1)):
        pl.when(cond)(functools.partial(fn, cls))


def _attn_geometry(p4, nx):
    rows = p4.shape[0]
    w = p4.shape[1] // 4
    nhp = w // _PAIR
    nblk = nx // _QB
    qspec = lambda col: pl.BlockSpec((_QB, _PAIR), lambda hp, b: (b, col * nhp + hp))
    kspec = lambda col: pl.BlockSpec((rows, _PAIR), lambda hp, b: (0, col * nhp + hp))
    tspec = pl.BlockSpec((2, _RP_ROWS, _PAIR), lambda hp, b: (hp, 0, 0))
    mspec = pl.BlockSpec((None, _QB, _KW), lambda hp, b: (jnp.where(b == 0, 0, jnp.where(b == nblk - 1, 2, 1)), 0, 0))
    lspec = pl.BlockSpec((None, _QB, 2), lambda hp, b: (hp, b, 0))
    ospec = pl.BlockSpec((_QB, _PAIR), lambda hp, b: (b, hp))
    return rows, w, nhp, nblk, qspec, kspec, tspec, mspec, lspec, ospec


def _window_start(b, nx):
    return pl.multiple_of(jnp.clip(b * _QB - PAD_ROWS * GRID_W, 0, nx - _KW), _QB)


def _load_bias(bias_ref, tiles_ref, rp_ref, m_ref, b, nblk):
    pl.when(b == 0)(lambda: _build_tiles(tiles_ref, rp_ref))

    def fill(cls):
        for h in range(2):
            for qr, kr, tile in _bias_pieces(cls):
                rows = slice(qr * GRID_W, (qr + 1) * GRID_W)
                cols = slice(kr * GRID_W, (kr + 2) * GRID_W)
                m = m_ref[rows, cols]
                bias_ref[h, rows, cols] = m if tile is None else tiles_ref[h, tile] + m

    _block_class(b, nblk, fill, entering=True)


def _attn_fwd(p4, rp, mask, nx, name, comm=None):
    rows, w, nhp, nblk, qspec, kspec, tspec, mspec, lspec, ospec = _attn_geometry(p4, nx)
    n_ctx = rows - nx
    n_cin, n_cout = (len(comm.ins), len(comm.outs)) if comm else (0, 0)

    def body(*refs):
        q_ref, k_ref, v_ref, g_ref, rp_ref, m_ref = refs[:6]
        cin = refs[6:6 + n_cin]
        a_ref, o_ref, lse_ref = refs[6 + n_cin:9 + n_cin]
        cout = refs[9 + n_cin:9 + n_cin + n_cout]
        bias_ref, tiles_ref = refs[9 + n_cin + n_cout:11 + n_cin + n_cout]
        sems = refs[11 + n_cin + n_cout:]
        hp, b = pl.program_id(0), pl.program_id(1)
        if comm:
            pl.when(jnp.logical_and(hp == 0, b == 0))(lambda: comm.start(cin, cout, sems))
        start = _window_start(b, nx)
        _load_bias(bias_ref, tiles_ref, rp_ref, m_ref, b, nblk)
        qf = q_ref[...].astype(F32) * HEAD_DIM ** -0.5
        kw = k_ref[pl.ds(start, _KW), :].astype(BF16)
        vw = v_ref[pl.ds(start, _KW), :].astype(BF16)
        kcv = k_ref[pl.ds(nx, n_ctx), :].astype(BF16)
        vcv = v_ref[pl.ds(nx, n_ctx), :].astype(BF16)
        lane = lax.broadcasted_iota(jnp.int32, (1, _PAIR), 1)
        outs, lses = [], []
        for h in range(2):
            mine = (lane >= HEAD_DIM) if h else (lane < HEAD_DIM)
            qm = jnp.where(mine, qf, 0.0).astype(BF16)
            s_loc = lax.dot_general(qm, kw, _DIMS["nt"], preferred_element_type=F32) + bias_ref[h]
            s_ctx = lax.dot_general(qm, kcv, _DIMS["nt"], preferred_element_type=F32)
            mx = jnp.maximum(jnp.max(s_loc, axis=-1, keepdims=True), jnp.max(s_ctx, axis=-1, keepdims=True))
            p_loc = jnp.exp(s_loc - mx)
            p_ctx = jnp.exp(s_ctx - mx)
            den = jnp.sum(p_loc, axis=-1, keepdims=True) + jnp.sum(p_ctx, axis=-1, keepdims=True)
            o = jnp.dot(p_loc.astype(BF16), vw, preferred_element_type=F32)
            o = o + jnp.dot(p_ctx.astype(BF16), vcv, preferred_element_type=F32)
            outs.append(o * (1.0 / den))
            lses.append(mx + jnp.log(den))
        o = jnp.where(lane < HEAD_DIM, outs[0], outs[1])
        o_ref[...] = o.astype(ACT)
        a_ref[...] = (o * _silu(g_ref[...].astype(F32))).astype(BF16)
        col = lax.broadcasted_iota(jnp.int32, (1, 2), 1)
        lse_ref[...] = jnp.where(col == 0, lses[0], lses[1])
        if comm:
            pl.when(jnp.logical_and(hp == nhp - 1, b == nblk - 1))(lambda: comm.finish(cin, cout, sems))

    res = pl.pallas_call(
        body, grid=(nhp, nblk),
        in_specs=[qspec(0), kspec(1), kspec(2), qspec(3), tspec, mspec] + [HBM_SPEC] * n_cin,
        out_specs=[ospec, ospec, lspec] + [HBM_SPEC] * n_cout,
        out_shape=[_sds((nx, w), BF16), _sds((nx, w), ACT), _sds((nhp, nx, 2), F32)] + (comm.outs if comm else []),
        scratch_shapes=[pltpu.VMEM((2, _QB, _KW), F32), pltpu.VMEM((2, _N_TILES, GRID_W, _PAIR), F32)]
        + (comm.sems if comm else []),
        name=name, compiler_params=_cparams("arbitrary", "arbitrary"),
    )(p4, p4, p4, p4, rp, mask, *(comm.ins if comm else []))
    return res[:3], res[3:]


def _fold_tiles(dtiles_ref, drp_ref):
    shape = (GRID_W, _PAIR)
    lane = lax.broadcasted_iota(jnp.int32, shape, 1)
    flip = (lax.broadcasted_iota(jnp.int32, (_PAIR, _PAIR), 0)
            + lax.broadcasted_iota(jnp.int32, (_PAIR, _PAIR), 1) == _PAIR - 1).astype(F32)
    drp_ref[...] = jnp.zeros(drp_ref.shape, F32)
    for h in range(2):
        stack = dtiles_ref[h].reshape(_N_TILES * GRID_W, _PAIR)
        rev = jnp.dot(stack, flip, precision=lax.Precision.HIGHEST, preferred_element_type=F32)
        for t in range(_N_TILES):
            tile = rev[t * GRID_W:(t + 1) * GRID_W, :]
            for side in (0, 1):
                shift = _PAIR - GRID_W * side - (WIN_COLS - 1)
                half = jnp.where((lane < GRID_W) if side else (lane >= GRID_W), tile, 0.0)
                diag = pltpu.roll(half, shift, 1, stride=1, stride_axis=0)
                drp_ref[h, t + side:t + side + 1, :] += jnp.sum(diag, axis=0, keepdims=True)


def _attn_bwd(p4, rp, mask, o, lse, da, nx, name, comm=None):
    rows, w, nhp, nblk, qspec, kspec, tspec, mspec, lspec, ospec = _attn_geometry(p4, nx)
    n_ctx = rows - nx
    n_cin, n_cout = (len(comm.ins), len(comm.outs)) if comm else (0, 0)

    def body(*refs):
        q_ref, k_ref, v_ref, g_ref, rp_ref, m_ref, o_ref, lse_ref, da_ref = refs[:9]
        cin = refs[9:9 + n_cin]
        d4_ref, drp_ref = refs[9 + n_cin:11 + n_cin]
        cout = refs[11 + n_cin:11 + n_cin + n_cout]
        bias_ref, tiles_ref, ds_ref, dtiles_ref, dk_ref, dv_ref = refs[11 + n_cin + n_cout:17 + n_cin + n_cout]
        sems = refs[17 + n_cin + n_cout:]
        hp, b = pl.program_id(0), pl.program_id(1)
        if comm:
            pl.when(jnp.logical_and(hp == 0, b == 0))(lambda: comm.start(cin, cout, sems))
        start = _window_start(b, nx)
        here = pl.multiple_of(b * _QB, _QB)

        @pl.when(b == 0)
        def _():
            dk_ref[...] = jnp.zeros(dk_ref.shape, F32)
            dv_ref[...] = jnp.zeros(dv_ref.shape, F32)
            dtiles_ref[...] = jnp.zeros(dtiles_ref.shape, F32)
            d4_ref[0, pl.ds(nx, n_ctx), :] = jnp.zeros((n_ctx, _PAIR), BF16)
            d4_ref[3, pl.ds(nx, n_ctx), :] = jnp.zeros((n_ctx, _PAIR), BF16)

        _load_bias(bias_ref, tiles_ref, rp_ref, m_ref, b, nblk)
        gv = g_ref[...].astype(F32)
        dav = da_ref[...].astype(F32)
        ov = o_ref[...].astype(F32)
        dov = dav * _silu(gv)
        d4_ref[3, pl.ds(here, _QB), :] = (dav * ov * _dsilu(gv)).astype(BF16)
        qf = q_ref[...].astype(F32) * HEAD_DIM ** -0.5
        kw = k_ref[pl.ds(start, _KW), :].astype(BF16)
        vw = v_ref[pl.ds(start, _KW), :].astype(BF16)
        kcv = k_ref[pl.ds(nx, n_ctx), :].astype(BF16)
        vcv = v_ref[pl.ds(nx, n_ctx), :].astype(BF16)
        lane = lax.broadcasted_iota(jnp.int32, (1, _PAIR), 1)
        dq = jnp.zeros((_QB, _PAIR), F32)
        for h in range(2):
            mine = (lane >= HEAD_DIM) if h else (lane < HEAD_DIM)
            qm = jnp.where(mine, qf, 0.0).astype(BF16)
            dom = jnp.where(mine, dov, 0.0)
            dob = dom.astype(BF16)
            lse = lse_ref[:, h:h + 1]
            s_loc = lax.dot_general(qm, kw, _DIMS["nt"], preferred_element_type=F32)
            p_loc = jnp.exp(s_loc + bias_ref[h] - lse)
            p_ctx = jnp.exp(lax.dot_general(qm, kcv, _DIMS["nt"], preferred_element_type=F32) - lse)
            delta = jnp.sum(dom * ov, axis=-1, keepdims=True)
            ds_loc = p_loc * (lax.dot_general(dob, vw, _DIMS["nt"], preferred_element_type=F32) - delta)
            ds_ctx = p_ctx * (lax.dot_general(dob, vcv, _DIMS["nt"], preferred_element_type=F32) - delta)
            dsb_loc = ds_loc.astype(BF16)
            dsb_ctx = ds_ctx.astype(BF16)
            dq_h = (jnp.dot(dsb_loc, kw, preferred_element_type=F32)
                    + jnp.dot(dsb_ctx, kcv, preferred_element_type=F32))
            dq = dq + jnp.where(mine, dq_h, 0.0)
            dk_ref[pl.ds(start, _KW), :] += lax.dot_general(dsb_loc, qm, _DIMS["tn"], preferred_element_type=F32)
            dv_ref[pl.ds(start, _KW), :] += lax.dot_general(p_loc.astype(BF16), dob, _DIMS["tn"],
                                                            preferred_element_type=F32)
            dk_ref[pl.ds(nx, n_ctx), :] += lax.dot_general(dsb_ctx, qm, _DIMS["tn"], preferred_element_type=F32)
            dv_ref[pl.ds(nx, n_ctx), :] += lax.dot_general(p_ctx.astype(BF16), dob, _DIMS["tn"],
                                                           preferred_element_type=F32)
            ds_ref[h] = ds_loc
        d4_ref[0, pl.ds(here, _QB), :] = (dq * HEAD_DIM ** -0.5).astype(BF16)

        def scatter(cls):
            for h in range(2):
                for qr, kr, tile in _bias_pieces(cls):
                    if tile is not None:
                        dtiles_ref[h, tile] += ds_ref[h, qr * GRID_W:(qr + 1) * GRID_W, kr * GRID_W:(kr + 2) * GRID_W]

        _block_class(b, nblk, scatter)

        @pl.when(b == nblk - 1)
        def _():
            d4_ref[1] = dk_ref[...].astype(BF16)
            d4_ref[2] = dv_ref[...].astype(BF16)
            _fold_tiles(dtiles_ref, drp_ref)

        if comm:
            pl.when(jnp.logical_and(hp == nhp - 1, b == nblk - 1))(lambda: comm.finish(cin, cout, sems))

    tiles = pltpu.VMEM((2, _N_TILES, GRID_W, _PAIR), F32)
    block = pltpu.VMEM((2, _QB, _KW), F32)
    res = pl.pallas_call(
        body, grid=(nhp, nblk),
        in_specs=[qspec(0), kspec(1), kspec(2), qspec(3), tspec, mspec, ospec, lspec, ospec] + [HBM_SPEC] * n_cin,
        out_specs=[pl.BlockSpec((4, rows, _PAIR), lambda hp, b: (0, 0, hp)), tspec] + [HBM_SPEC] * n_cout,
        out_shape=[_sds((4, rows, w), BF16), _sds(rp.shape, F32)] + (comm.outs if comm else []),
        scratch_shapes=[block, tiles, block, tiles, pltpu.VMEM((rows, _PAIR), F32), pltpu.VMEM((rows, _PAIR), F32)]
        + (comm.sems if comm else []),
        name=name, compiler_params=_cparams("arbitrary", "arbitrary"),
    )(p4, p4, p4, p4, rp, mask, o, lse, da, *(comm.ins if comm else []))
    return res[:2], res[2:]


def _w_out_loss(a, w_out, xres, gate, g, target, name):
    m, k = a.shape
    d = w_out.shape[1]
    assert gate.shape[0] == 1
    tm = _row_tile(m)
    nblk = m // tm

    def body(a_ref, w_ref, x_ref, gt_ref, g_ref, t_ref, yx_ref, loss_ref, dx_ref, dg_ref, acc_ref):
        i = pl.program_id(0)
        yx = jnp.dot(a_ref[...], w_ref[...], preferred_element_type=F32)
        yx_ref[...] = yx.astype(ACT)
        xv = x_ref[...] + gt_ref[0] * yx
        gv = g_ref[...]
        r = lax.rsqrt(jnp.mean(xv * xv, axis=-1, keepdims=True) + EPS)
        xn = xv * r
        err = xn * gv - t_ref[...]
        dy = err * (1.0 / d)
        dxn = dy * gv
        dx_ref[...] = r * (dxn - xn * jnp.mean(dxn * xn, axis=-1, keepdims=True))
        s_g = jnp.sum(dy * xn, axis=0, keepdims=True)
        s_l = jnp.sum(jnp.mean(err * err, axis=-1, keepdims=True), axis=0, keepdims=True)

        @pl.when(i == 0)
        def _():
            dg_ref[...] = s_g
            acc_ref[...] = s_l

        @pl.when(i > 0)
        def _():
            dg_ref[...] += s_g
            acc_ref[...] += s_l

        @pl.when(i == nblk - 1)
        def _():
            loss_ref[...] = jnp.broadcast_to(0.5 * acc_ref[...], loss_ref.shape)

    row = pl.BlockSpec((tm, d), lambda i: (i, 0))
    vec = pl.BlockSpec((1, d), lambda i: (0, 0))
    return pl.pallas_call(
        body, grid=(nblk,),
        in_specs=[pl.BlockSpec((tm, k), lambda i: (i, 0)), pl.BlockSpec((k, d), lambda i: (0, 0)), row,
                  pl.BlockSpec((1, 1, d), lambda i: (0, 0, 0)), vec, row],
        out_specs=[row, pl.BlockSpec((1, 128), lambda i: (0, 0)), row, vec],
        out_shape=[_sds((m, d), ACT), _sds((1, 128), F32), _sds((m, d), F32), _sds((1, d), F32)],
        scratch_shapes=[pltpu.VMEM((1, 1), F32)], name=name, compiler_params=_cparams("arbitrary"),
    )(a, w_out, xres, gate, g, target)


def _as2d(a):
    if a.ndim == 1:
        return a.reshape(-1, 128) if a.shape[0] % 128 == 0 else a.reshape(1, -1)
    return a.reshape(-1, a.shape[-1])


def _adamw(w, g, m, v, name):
    shape = w.shape
    w2, g2, m2, v2 = (_as2d(t) for t in (w, g.reshape(shape), m, v))
    rows, cols = w2.shape
    tr = 512 if rows % 512 == 0 else rows
    c1 = 1.0 - ADAM_B1 ** ADAM_STEP
    c2 = 1.0 - ADAM_B2 ** ADAM_STEP

    def body(w_ref, g_ref, m_ref, v_ref, d_ref, nm_ref, nv_ref):
        gv = g_ref[...]
        nm = ADAM_B1 * m_ref[...] + (1.0 - ADAM_B1) * gv
        nv = ADAM_B2 * v_ref[...] + (1.0 - ADAM_B2) * (gv * gv)
        nm_ref[...] = nm
        nv_ref[...] = nv
        d_ref[...] = -ADAM_LR * ((nm / c1) / (jnp.sqrt(nv / c2) + ADAM_EPS) + ADAM_WD * w_ref[...])

    blk = pl.BlockSpec((tr, cols), lambda i: (i, 0))
    outs = _call(body, (w2, g2, m2, v2), grid=(rows // tr,), in_specs=[blk] * 4, out_specs=[blk] * 3,
                 out_shape=[_sds((rows, cols), F32)] * 3, name=name)
    return tuple(t.reshape(shape) for t in outs)


def _sum_lead(x, name, out_dtype=F32):
    n, rows, cols = x.shape
    tr = 512 if rows % 512 == 0 else rows

    def body(x_ref, o_ref):
        acc = x_ref[0].astype(F32)
        for k in range(1, n):
            acc = acc + x_ref[k].astype(F32)
        o_ref[...] = acc.astype(out_dtype)

    return pl.pallas_call(
        body, grid=(rows // tr,), in_specs=[pl.BlockSpec((n, tr, cols), lambda i: (0, i, 0))],
        out_specs=pl.BlockSpec((tr, cols), lambda i: (i, 0)), out_shape=_sds((rows, cols), out_dtype),
        name=name, compiler_params=_cparams("parallel"),
    )(x)


def _seg_vecs(mod_l, which, nseg):
    return mod_l[:nseg, which][:, None, :]


def _norm_grads(dshift, dgeff, dgate, g, scale):
    nseg, _, d = dshift.shape
    dmod = jnp.stack([dshift[:, 0], dgeff[:, 0] * g, dgate[:, 0]], axis=1)
    if nseg == 1:
        dmod = jnp.concatenate([dmod, jnp.zeros((1, 3, d), F32)], axis=0)
    dg = jnp.sum(dgeff[:, 0] * (1.0 + scale[:, 0]), axis=0)
    return dmod, dg


def _pool_layer(xin, g, mod_l, w_in, w_grp, w_out, pscale, nx, tag, ctx=None, head=None):
    nseg = 1 if ctx is None else 2
    shift, scale, gate = (_seg_vecs(mod_l, k, nseg) for k in range(3))
    *joined, h, r, uv = _norm_w_in(xin, g, scale, shift, w_in, nx, f"w_in_fwd_{tag}", ctx)
    if joined:
        xin, = joined
    z, mixed, a = _pool_grp_fwd(uv, w_grp, pscale, nx, f"pool_fwd_{tag}")
    if head is None:
        yx, xout = _w_out_resid(a, w_out, xin, gate, nx, f"w_out_fwd_{tag}")
    else:
        yx, *xout = _w_out_loss(a, w_out, xin, gate, *head, f"w_out_loss_{tag}")

    def backward(dxo, token=None):
        gate_b = gate if token is None else gate + token[0, 0]
        dyx, da, dgate = _gate_w_out_bwd(dxo, yx, gate_b, w_out, nx, f"w_out_bwd_{tag}")
        gw_out = _mm_tn(a, dyx, f"w_out_grad_{tag}", BF16)
        dm, duv, dscale = _pool_grp_bwd(da, mixed, uv, pscale, w_grp, nx, f"pool_bwd_{tag}")
        gw_grp = _grp_wgrad(z, dm, w_grp.shape[0], f"grp_grad_{tag}", BF16)
        gw_in = _mm_tn_parts(h, duv, f"w_in_grad_{tag}", BF16)
        dx, dshift, dgeff = _w_in_bwd_norm(duv, w_in, xin, r, g, scale, dxo, nx, f"w_in_bwd_{tag}")
        dmod, dg = _norm_grads(dshift, dgeff, dgate, g[0], scale)
        return dx, dmod, dg, dict(w_in=gw_in, w_grp=gw_grp, w_out=gw_out, scale=dscale)

    return xout, backward


def _na_layer(xc, g, mod_l, w_in, rpb, w_out, nx, mask, comm=None):
    nh, n_dr, n_dc = rpb.shape
    shift, scale = _seg_vecs(mod_l, 0, 2), _seg_vecs(mod_l, 1, 2)
    gate = _seg_vecs(mod_l, 2, 1)
    h, r, p4 = _norm_w_in(xc, g, scale, shift, w_in, nx, "w_in_fwd_na")
    rp = jnp.pad(rpb, ((0, 0), (1, _RP_ROWS - 1 - n_dr), (0, _PAIR - n_dc)))
    (a, o, lse), carried = _attn_fwd(p4, rp, mask, nx, "attn_fwd", comm)
    yx, xout = _w_out_resid(a, w_out, xc, gate, nx, "w_out_fwd_na")

    def backward(dxo, comm=None):
        dyx, da, dgate = _gate_w_out_bwd(dxo, yx, gate, w_out, nx, "w_out_bwd_na")
        gw_out = _mm_tn(a, dyx, "w_out_grad_na", BF16)
        (d4, drp), carried_bwd = _attn_bwd(p4, rp, mask, o, lse, da, nx, "attn_bwd", comm)
        gw_in = _mm_tn_parts(h, d4, "w_in_grad_na", BF16)
        dx, dshift, dgeff = _w_in_bwd_norm(d4, w_in, xc, r, g, scale, dxo, nx, "w_in_bwd_na")
        dgate2 = jnp.concatenate([dgate, jnp.zeros_like(dgate)], axis=0)
        dmod, dg = _norm_grads(dshift, dgeff, dgate2, g[0], scale)
        drpb = drp[:, 1:1 + n_dr, ::-1][:, :, :n_dc]
        return dx, dmod, dg, dict(w_in=gw_in, w_out=gw_out, rpb=drpb), carried_bwd

    return xout, backward, carried


def _conv_layer(xin, g, mod_l, w_in, dw, db, w_out):
    shift, scale, gate = (_seg_vecs(mod_l, k, 1) for k in range(3))
    nx = xin.shape[0]
    h, r, p4 = _norm_w_in(xin, g, scale, shift, w_in, nx, "w_in_fwd_conv")
    a = _conv_fwd(p4, dw, db, "conv_fwd")
    yx, xout = _w_out_resid(a, w_out, xin, gate, nx, "w_out_fwd_conv")

    def backward(dxo):
        dyx, da, dgate = _gate_w_out_bwd(dxo, yx, gate, w_out, nx, "w_out_bwd_conv")
        gw_out = _mm_tn(a, dyx, "w_out_grad_conv", BF16)
        d4, ddw, ddb = _conv_bwd(da, p4, dw, db, "conv_bwd")
        gw_in = _mm_tn_parts(h, d4, "w_in_grad_conv", BF16)
        dx, dshift, dgeff = _w_in_bwd_norm(d4, w_in, xin, r, g, scale, dxo, nx, "w_in_bwd_conv")
        dmod, dg = _norm_grads(dshift, dgeff, dgate, g[0], scale)
        return dx, dmod, dg, dict(w_in=gw_in, w_out=gw_out, dw=ddw, db=ddb)

    return xout, backward


def _example_step(x, ctx, target, mod, norm_g, final_g, wts, hooks=None):
    hooks = hooks or {}
    na_weights, late_comm, late_weights = (hooks.get(k) for k in ("na_weights", "late_comm", "late_weights"))
    nx = x.shape[0]
    consts = _attn_mask()
    g_rows = [norm_g[i:i + 1] for i in range(4)]
    xc1, bwd0 = _pool_layer(x, g_rows[0], mod[0], wts["pool_w_in"][0], wts["pool_w_grp"][0],
                            wts["pool_w_out"][0], wts["pool_scale"][0:1], nx, "p0", ctx=ctx)
    if na_weights is not None:
        wts = {**wts, **na_weights(xc1)}
    x2, bwd1, carried = _na_layer(xc1, g_rows[1], mod[1], wts["na_w_in"], wts["na_rpb"], wts["na_w_out"], nx, consts,
                                  late_comm)
    if late_weights is not None:
        wts = {**wts, **late_weights(carried)}
    x3, bwd2 = _conv_layer(x2, g_rows[2], mod[2], wts["conv_w_in"], wts["conv_dw"], wts["conv_db"], wts["conv_w_out"])
    (loss, dx4, dfinal_g), bwd3 = _pool_layer(x3, g_rows[3], mod[3], wts["pool_w_in"][1], wts["pool_w_grp"][1],
                                              wts["pool_w_out"][1], wts["pool_scale"][1:2], nx, "p3",
                                              head=(final_g, target))
    call = lambda k, *args: hooks[k](*args) if k in hooks else None
    dx3, dmod3, dg3, gr3 = bwd3(dx4)
    dx2, dmod2, dg2, gr2 = bwd2(dx3)
    dxc1, dmod1, dg1, gr1, carried_bwd = bwd1(dx2, call("grad_comm", gr3, gr2))
    dxc0, dmod0, dg0, gr0 = bwd0(dxc1, call("na_grads_start", gr1))
    return dict(
        loss=loss, grad_x=dxc0[:nx], dmod=jnp.stack([dmod0, dmod1, dmod2, dmod3]),
        dnorm_g=jnp.stack([dg0, dg1, dg2, dg3]), dfinal_g=dfinal_g, layers=(gr0, gr1, gr2, gr3), carried=carried_bwd)


_AXES = ("x", "y", "c")
_CHIP_FLIPS = ((1, 0), (0, 1), (1, 1))


def _position():
    return tuple(lax.axis_index(a) for a in _AXES)


def _flipped(pos, flip):
    return tuple(1 - p if f else p for p, f in zip(pos, flip))


def _join_comms(comms):
    n_in = [len(c.ins) for c in comms]
    n_out = [len(c.outs) for c in comms]
    n_sem = [len(c.sems) for c in comms]

    def parts(ins, outs, sems):
        for k in range(len(comms)):
            a, b, s = sum(n_in[:k]), sum(n_out[:k]), sum(n_sem[:k])
            yield comms[k], (ins[a:a + n_in[k]], outs[b:b + n_out[k]], sems[s:s + n_sem[k]])

    def start(ins, outs, sems):
        for c, part in parts(ins, outs, sems):
            c.start(*part)

    def finish(ins, outs, sems):
        for c, part in parts(ins, outs, sems):
            c.finish(*part)

    joint = _Comm([a for c in comms for a in c.ins], [o for c in comms for o in c.outs],
                  [s for c in comms for s in c.sems], start, finish)
    return joint, lambda res: [list(res[sum(n_out[:k]):sum(n_out[:k + 1])]) for k in range(len(comms))]


def _run_comms(comms, name):
    joint, split = _join_comms(comms)

    def body(*refs):
        n_in, n_out = len(joint.ins), len(joint.outs)
        joint.start(refs[:n_in], refs[n_in:n_in + n_out], refs[n_in + n_out:])
        joint.finish(refs[:n_in], refs[n_in:n_in + n_out], refs[n_in + n_out:])

    res = pl.pallas_call(
        body, in_specs=[HBM_SPEC] * len(joint.ins), out_specs=[HBM_SPEC] * len(joint.outs), out_shape=joint.outs,
        scratch_shapes=joint.sems, name=name,
    )(*joint.ins)
    return split(res)


def _all_gather_comm(v, axes):
    flips = [f for f in np.ndindex(2, 2, 2) if any(f) and all(a in axes or not b for a, b in zip(_AXES, f))]
    n = len(flips) + 1

    def copies(ins, outs, sems):
        (v_ref,), (o_ref,), (send_sems, recv_sems, local_sem) = ins, outs, sems
        pos = _position()
        slot = 0
        for a, p in zip(_AXES, pos):
            if a in axes:
                slot = 2 * slot + p
        local = pltpu.make_async_copy(v_ref, o_ref.at[slot], local_sem)
        remote = [pltpu.make_async_remote_copy(v_ref, o_ref.at[slot], send_sems.at[k], recv_sems.at[k],
                                               device_id=_flipped(pos, flip), device_id_type=MESH)
                  for k, flip in enumerate(flips)]
        return [local] + remote

    def start(ins, outs, sems):
        for cp in copies(ins, outs, sems):
            cp.start()

    def finish(ins, outs, sems):
        for cp in copies(ins, outs, sems):
            cp.wait()

    sems = [pltpu.SemaphoreType.DMA((n - 1,)), pltpu.SemaphoreType.DMA((n - 1,)), pltpu.SemaphoreType.DMA(())]
    return _Comm([v], [_sds((n,) + v.shape, v.dtype)], sems, start, finish)


def _all_gather_two_level_comm(v):
    def copies(ins, outs, sems, onward):
        (v_ref,), (o_ref,), (send_sems, recv_sems, local_sem) = ins, outs, sems
        x, y, c = _position()
        sibling = (x, y, 1 - c)
        slot = lambda px, py, pc: o_ref.at[4 * px + 2 * py + pc]
        own = pltpu.make_async_copy(v_ref, slot(x, y, c), local_sem)
        first = [pltpu.make_async_remote_copy(v_ref, slot(x, y, c), send_sems.at[0], recv_sems.at[0],
                                              device_id=sibling, device_id_type=MESH)]
        fwd = []
        for k, flip in enumerate(_CHIP_FLIPS):
            px, py = _flipped((x, y), flip)
            first.append(pltpu.make_async_remote_copy(v_ref, slot(x, y, c), send_sems.at[1 + k], recv_sems.at[1 + k],
                                                      device_id=(px, py, c), device_id_type=MESH))
            if onward:
                fwd.append(pltpu.make_async_remote_copy(slot(px, py, c), slot(px, py, c), send_sems.at[4 + k],
                                                        recv_sems.at[4 + k], device_id=sibling, device_id_type=MESH))
        return own, first, fwd

    def start(ins, outs, sems):
        own, first, _ = copies(ins, outs, sems, False)
        for cp in [own] + first:
            cp.start()

    def finish(ins, outs, sems):
        own, first, fwd = copies(ins, outs, sems, True)
        for arrived, onward in zip(first[1:], fwd):
            arrived.wait_recv()
            onward.start()
        first[0].wait_recv()
        for cp in fwd:
            cp.wait_recv()
        for cp in first + fwd:
            cp.wait_send()
        own.wait()

    sems = [pltpu.SemaphoreType.DMA((7,)), pltpu.SemaphoreType.DMA((7,)), pltpu.SemaphoreType.DMA(())]
    return _Comm([v], [_sds((8,) + v.shape, v.dtype)], sems, start, finish)


def _all_gather(v, axes, name):
    return _run_comms([_all_gather_comm(v, axes)], name)[0][0]


class _Item:
    def __init__(self, key, layer, shape, shard_axis, half_axis):
        self.key, self.layer, self.shape = key, layer, tuple(shape)
        self.shard_axis, self.half_axis = shard_axis, half_axis
        self.shard = shape[shard_axis] // 4
        self.half = shape[half_axis] // 2

    def sized(self, shard=False, half=False):
        s = list(self.shape)
        if shard:
            s[self.shard_axis] = self.shard
        if half:
            s[self.half_axis] = self.half
        return tuple(s)

    def window(self, ref, chip=None, half=None):
        idx = [slice(None)] * len(self.shape)
        if chip is not None:
            idx[self.shard_axis] = pl.ds(chip * self.shard, self.shard)
        if half is not None:
            idx[self.half_axis] = pl.ds(half * self.half, self.half)
        return ref.at[tuple(idx)]


def _items(d, w):
    out = []
    for j in range(2):
        out += [_Item("pool_w_in", j, (d, 2 * w), 1, 0), _Item("pool_w_grp", j, (4, w // 4, w // 4), 1, 0),
                _Item("pool_w_out", j, (w, d), 0, 1)]
    out += [_Item("na_w_in", 0, (d, 4 * w), 1, 0), _Item("na_w_out", 0, (w, d), 0, 1),
            _Item("conv_w_in", 0, (d, 4 * w), 1, 0), _Item("conv_w_out", 0, (w, d), 0, 1)]
    return out


def _gather_comm(shards, items):
    n = len(items)

    def copies(src, dst, sems, onward):
        send_a, recv_a, send_b, recv_b, send_c, recv_c = sems
        x, y, c = _position()
        chip = 2 * x + y
        sibling = (x, y, 1 - c)
        own, out, fwd, fwd_in = [], [], [], []
        for i, it in enumerate(items):
            own.append(pltpu.make_async_remote_copy(src[i], it.window(dst[i], chip=chip), send_c.at[i], recv_c.at[i],
                                                    device_id=sibling, device_id_type=MESH))
            for k, flip in enumerate(_CHIP_FLIPS):
                px, py = _flipped((x, y), flip)
                s = 3 * i + k
                out.append(pltpu.make_async_remote_copy(
                    it.window(src[i], half=c), it.window(dst[i], chip=chip, half=c), send_a.at[s], recv_a.at[s],
                    device_id=(px, py, c), device_id_type=MESH))
                if onward:
                    got = it.window(dst[i], chip=2 * px + py, half=c)
                    fwd.append(pltpu.make_async_remote_copy(got, got, send_b.at[s], recv_b.at[s],
                                                            device_id=sibling, device_id_type=MESH))
                    other = it.window(dst[i], chip=2 * px + py, half=1 - c)
                    fwd_in.append(pltpu.make_async_remote_copy(other, other, send_b.at[s], recv_b.at[s],
                                                               device_id=sibling, device_id_type=MESH))
        return own, out, fwd, fwd_in

    def start(src, dst, sems):
        own, out, _, _ = copies(src, dst, sems, False)
        for cp in own + out:
            cp.start()

    def finish(src, dst, sems):
        own, out, fwd, fwd_in = copies(src, dst, sems, True)
        for arrived, onward in zip(out, fwd):
            arrived.wait_recv()
            onward.start()
        for cp in fwd_in:
            cp.wait_recv()
        for cp in out + fwd:
            cp.wait_send()
        for cp in own:
            cp.wait()

    sems = [pltpu.SemaphoreType.DMA((3 * n,)) for _ in range(4)] + [pltpu.SemaphoreType.DMA((n,)) for _ in range(2)]
    return _Comm(shards, [_sds(it.shape, BF16) for it in items], sems, start, finish)


def _pair_swap_copies(windows):
    def copies(src, got, sems):
        send_sems, recv_sems = sems
        x, y, c = _position()
        return [pltpu.make_async_remote_copy(windows[i](src[i], 1 - c), got[i], send_sems.at[i], recv_sems.at[i],
                                             device_id=(x, y, 1 - c), device_id_type=MESH)
                for i in range(len(windows))]

    return copies


def _pair_swap_comm(arrays, windows, out_shapes):
    n = len(arrays)
    copies = _pair_swap_copies(windows)

    def start(src, got, sems):
        for cp in copies(src, got, sems):
            cp.start()

    def finish(src, got, sems):
        for cp in copies(src, got, sems):
            cp.wait()

    return _Comm(arrays, out_shapes, [pltpu.SemaphoreType.DMA((n,)), pltpu.SemaphoreType.DMA((n,))], start, finish)


def _pair_swap(arrays, windows, out_shapes, name):
    return _run_comms([_pair_swap_comm(arrays, windows, out_shapes)], name)[0]


def _chip_exchange_copies(items):
    def copies(src, dst, sems):
        send_sems, recv_sems = sems
        x, y, c = _position()
        out = []
        for i, it in enumerate(items):
            for k, flip in enumerate(_CHIP_FLIPS):
                px, py = _flipped((x, y), flip)
                out.append(pltpu.make_async_remote_copy(
                    it.window(src[i], chip=2 * px + py), dst[i].at[k], send_sems.at[3 * i + k],
                    recv_sems.at[3 * i + k], device_id=(px, py, c), device_id_type=MESH))
        return out

    return copies


_SEM_SPEC = pl.BlockSpec(memory_space=pltpu.SEMAPHORE)
_DATAFLOW = pltpu.SideEffectType.DATAFLOW_SIDE_EFFECTING


def _split_start(copies, srcs, zones, n_copies, name):
    n, nz = len(srcs), len(zones)

    def body(*refs):
        src, land = refs[:n], refs[n:n + nz]
        send_sems, recv_sems = refs[n + nz:n + nz + 2]
        token = refs[-1]
        for cp in copies(src, land, (send_sems, recv_sems)):
            cp.start()
        token[...] = jnp.zeros(token.shape, F32)

    hbm = lambda t: pltpu.HBM(t.shape, t.dtype)
    res = pl.pallas_call(
        body, name=name,
        out_shape=(pltpu.SemaphoreType.DMA((n_copies,)), pltpu.SemaphoreType.DMA((n_copies,)),
                   *[hbm(t) for t in list(srcs) + list(zones)], _sds((8, 128), F32)),
        in_specs=[HBM_SPEC] * (n + nz),
        out_specs=(_SEM_SPEC, _SEM_SPEC, *[HBM_SPEC] * (n + nz), pl.BlockSpec(memory_space=pltpu.VMEM)),
        input_output_aliases={i: 2 + i for i in range(n + nz)},
        compiler_params=pltpu.CompilerParams(has_side_effects=_DATAFLOW),
    )(*[pltpu.with_memory_space_constraint(t, pltpu.HBM) for t in list(srcs) + list(zones)])
    return (res[0], res[1], list(res[2:2 + n]), list(res[2 + n:2 + n + nz])), res[-1]


def _split_wait(copies, handle, after, name):
    send_sems, recv_sems, srcs, zones = handle
    n, nz = len(srcs), len(zones)

    def body(*refs):
        src, land = refs[:n], refs[n:n + nz]
        send, recv = refs[n + nz:n + nz + 2]
        for cp in copies(src, land, (send, recv)):
            cp.wait_send()
            cp.wait_recv()

    hbm = lambda t: pltpu.HBM(t.shape, t.dtype)
    res = pl.pallas_call(
        body, name=name, out_shape=tuple(hbm(t) for t in list(srcs) + list(zones)),
        in_specs=[HBM_SPEC] * (n + nz) + [_SEM_SPEC, _SEM_SPEC, pl.BlockSpec(memory_space=pl.ANY)],
        out_specs=tuple([HBM_SPEC] * (n + nz)), input_output_aliases={i: i for i in range(n + nz)},
        compiler_params=pltpu.CompilerParams(has_side_effects=_DATAFLOW),
    )(*srcs, *zones, send_sems, recv_sems, after)
    return list(res[:n]), list(res[n:])


def _gather_ici_copies(items):
    def copies(src, dst, sems):
        send_sems, recv_sems = sems
        x, y, c = _position()
        chip = 2 * x + y
        out = []
        for i, it in enumerate(items):
            for k, flip in enumerate(_CHIP_FLIPS):
                px, py = _flipped((x, y), flip)
                out.append(pltpu.make_async_remote_copy(
                    it.window(src[i], half=c), it.window(dst[i], chip=chip, half=c), send_sems.at[3 * i + k],
                    recv_sems.at[3 * i + k], device_id=(px, py, c), device_id_type=MESH))
        return out

    return copies


def _gather_pair_finish(shards, mats, items, name):
    n = len(items)

    def body(*refs):
        src, dst = refs[:n], refs[2 * n:3 * n]
        send_own, recv_own, send_fwd, recv_fwd = refs[3 * n:]
        x, y, c = _position()
        chip = 2 * x + y
        sibling = (x, y, 1 - c)
        copies = []
        for i, it in enumerate(items):
            copies.append(pltpu.make_async_remote_copy(src[i], it.window(dst[i], chip=chip), send_own.at[i],
                                                       recv_own.at[i], device_id=sibling, device_id_type=MESH))
            for k, flip in enumerate(_CHIP_FLIPS):
                px, py = _flipped((x, y), flip)
                got = it.window(dst[i], chip=2 * px + py, half=c)
                copies.append(pltpu.make_async_remote_copy(got, got, send_fwd.at[3 * i + k], recv_fwd.at[3 * i + k],
                                                           device_id=sibling, device_id_type=MESH))
        for cp in copies:
            cp.start()
        for cp in copies:
            cp.wait()

    return pl.pallas_call(
        body, in_specs=[HBM_SPEC] * (2 * n), out_specs=[HBM_SPEC] * n, out_shape=[_sds(it.shape, BF16) for it in items],
        input_output_aliases={n + i: i for i in range(n)},
        scratch_shapes=[pltpu.SemaphoreType.DMA((n,)), pltpu.SemaphoreType.DMA((n,)),
                        pltpu.SemaphoreType.DMA((3 * n,)), pltpu.SemaphoreType.DMA((3 * n,))], name=name,
    )(*shards, *mats)


def _chip_exchange_comm(partials, items):
    n = len(items)
    copies = _chip_exchange_copies(items)

    def start(src, dst, sems):
        for cp in copies(src, dst, sems):
            cp.start()

    def finish(src, dst, sems):
        for cp in copies(src, dst, sems):
            cp.wait()

    return _Comm(partials, [_sds((3,) + it.sized(shard=True, half=True), BF16) for it in items],
                 [pltpu.SemaphoreType.DMA((3 * n,)), pltpu.SemaphoreType.DMA((3 * n,))], start, finish)


_SUM_STEPS = 2


def _pair_sums(gs, gots, its, pos, name):
    n = len(its)
    nb = _SUM_STEPS
    g2 = [g.reshape(-1, g.shape[-1]) for g in gs]
    got2 = [t.reshape(-1, t.shape[-1]) for t in gots]

    def body(pos_ref, *refs):
        for g_ref, got_ref, o_ref in zip(refs[:n], refs[n:2 * n], refs[2 * n:]):
            o_ref[...] = (g_ref[...].astype(F32) + got_ref[...].astype(F32)).astype(BF16)

    g_specs, got_specs = [], []
    for it, t in zip(its, got2):
        rows, cols = t.shape
        blk = (rows // nb, cols)
        g_map = (lambda i, pos: (pos[1] * nb + i, 0)) if it.half_axis == 0 else (lambda i, pos: (i, pos[1]))
        g_specs.append(pl.BlockSpec(blk, g_map))
        got_specs.append(pl.BlockSpec(blk, lambda i, pos: (i, 0)))
    outs = pl.pallas_call(
        body, grid_spec=pltpu.PrefetchScalarGridSpec(
            num_scalar_prefetch=1, grid=(nb,), in_specs=g_specs + got_specs, out_specs=got_specs),
        out_shape=[_sds(t.shape, BF16) for t in got2], name=name, compiler_params=_cparams("parallel"),
    )(pos, *g2, *got2)
    return [o.reshape(t.shape) for o, t in zip(outs, gots)]


_FLIP_SLOT = {2: 0, 1: 1, 3: 2}


def _chip_sums(pairs, slots, its, pos, name):
    n = len(its)
    nb = _SUM_STEPS

    def body(pos_ref, *refs):
        chip = pos_ref[0]
        for own in range(4):
            @pl.when(chip == own)
            def _():
                for p_ref, s_ref, o_ref in zip(refs[:n], refs[n:2 * n], refs[2 * n:]):
                    acc = None
                    for k in range(4):
                        v = (p_ref[...] if k == own else s_ref[_FLIP_SLOT[own ^ k]]).astype(F32)
                        acc = v if acc is None else acc + v
                    o_ref[...] = acc

    p_specs, s_specs, o_specs, shapes = [], [], [], []
    for it in its:
        shape = it.sized(shard=True, half=True)
        blk = (shape[0] // nb,) + shape[1:]
        rest = (0,) * (len(shape) - 1)

        def p_map(i, pos, it=it, nd=len(shape)):
            lead = i + (pos[0] * nb if it.shard_axis == 0 else 0)
            return (lead,) + tuple(pos[0] if ax == it.shard_axis else 0 for ax in range(1, nd))

        p_specs.append(pl.BlockSpec(blk, p_map))
        s_specs.append(pl.BlockSpec((3,) + blk, lambda i, pos, rest=rest: (0, i) + rest))
        o_specs.append(pl.BlockSpec(blk, lambda i, pos, rest=rest: (i,) + rest))
        shapes.append(_sds(shape, F32))
    return pl.pallas_call(
        body, grid_spec=pltpu.PrefetchScalarGridSpec(
            num_scalar_prefetch=1, grid=(nb,), in_specs=p_specs + s_specs, out_specs=o_specs),
        out_shape=shapes, name=name, compiler_params=_cparams("parallel"),
    )(pos, *pairs, *slots)


_GRAD_KEYS = ("pool_w_in", "pool_w_grp", "pool_w_out", "na_w_in", "na_w_out", "conv_w_in", "conv_w_out")


def _adamw_matrix(w, m, v, owns, others, it, pos, name):
    nl = w.shape[0]
    rows_split = it.half_axis == 0
    r, cdim = int(np.prod(w.shape[1:-1])), w.shape[-1]
    hr, hc = (r // 2, cdim) if rows_split else (r, cdim // 2)
    br = min(hr, 256)
    nb = hr // br
    c1 = 1.0 - ADAM_B1 ** ADAM_STEP
    c2 = 1.0 - ADAM_B2 ** ADAM_STEP

    def body(pos_ref, w_ref, m_ref, v_ref, *rest):
        own_refs, other_refs = rest[:nl], rest[nl:2 * nl]
        g_ref, d_ref, nm_ref, nv_ref = rest[2 * nl:]
        j, h = pl.program_id(0), pl.program_id(1)
        own, other = own_refs[0][...], other_refs[0][...]
        for q in range(1, nl):
            own = jnp.where(j == q, own_refs[q][...], own)
            other = jnp.where(j == q, other_refs[q][...], other)
        gv = jnp.where(h == pos_ref[1], own, other)
        nm = ADAM_B1 * m_ref[...] + (1.0 - ADAM_B1) * gv
        nv = ADAM_B2 * v_ref[...] + (1.0 - ADAM_B2) * (gv * gv)
        g_ref[...] = gv
        nm_ref[...] = nm
        nv_ref[...] = nv
        d_ref[...] = -ADAM_LR * ((nm / c1) / (jnp.sqrt(nv / c2) + ADAM_EPS) + ADAM_WD * w_ref[...])

    if rows_split:
        full = pl.BlockSpec((None, br, hc), lambda j, h, i, pos: (j, h * nb + i, 0))
    else:
        full = pl.BlockSpec((None, br, hc), lambda j, h, i, pos: (j, i, h))
    half = pl.BlockSpec((br, hc), lambda j, h, i, pos: (i, 0))
    flat = lambda t: t.reshape(nl, r, cdim)
    outs = pl.pallas_call(
        body, grid_spec=pltpu.PrefetchScalarGridSpec(
            num_scalar_prefetch=1, grid=(nl, 2, nb), in_specs=[full] * 3 + [half] * (2 * nl), out_specs=[full] * 4),
        out_shape=[_sds((nl, r, cdim), F32)] * 4, name=name,
        compiler_params=_cparams("parallel", "parallel", "parallel"),
    )(pos, flat(w), flat(m), flat(v), *[t.reshape(hr, hc) for t in list(owns) + list(others)])
    return tuple(t.reshape(w.shape) for t in outs)


_WEIGHTS = ("c_ctx", "norm_g", "ada_w", "ada_b", "pool_w_in", "pool_w_grp", "pool_scale", "pool_w_out", "na_w_in",
            "na_rpb", "na_w_out", "conv_w_in", "conv_dw", "conv_db", "conv_w_out", "final_g")
_COND_ROWS = 16


def _modulations(cond, ada_w, ada_b_cols):
    nl, d, n = ada_w.shape
    return _matmul(
        cond, ada_w, mode="nn", grid=(nl, 1), a_silu=True, epilogue="bias",
        a_spec=pl.BlockSpec((_COND_ROWS, d), lambda i, j: (0, 0)), b_spec=pl.BlockSpec((None, d, n), lambda i, j: (i, 0, 0)),
        extra=(ada_b_cols,), extra_specs=(pl.BlockSpec((None, 1, n), lambda i, j: (i, 0, 0)),),
        out_shapes=[_sds((nl, _COND_ROWS, n), F32)], out_specs=[pl.BlockSpec((None, _COND_ROWS, n), lambda i, j: (i, 0, 0))],
        name="modulations")[0]


def _ada_w_step(cond, dm_cols, w, m, v):
    nl, d, n = w.shape
    tr = d // 2
    c1 = 1.0 - ADAM_B1 ** ADAM_STEP
    c2 = 1.0 - ADAM_B2 ** ADAM_STEP

    def body(c_ref, dm_ref, w_ref, m_ref, v_ref, g_ref, d_ref, nm_ref, nv_ref):
        gv = lax.dot_general(_silu(c_ref[...]).astype(BF16), dm_ref[...].astype(BF16), _DIMS["tn"],
                             preferred_element_type=F32)
        nm = ADAM_B1 * m_ref[...] + (1.0 - ADAM_B1) * gv
        nv = ADAM_B2 * v_ref[...] + (1.0 - ADAM_B2) * (gv * gv)
        g_ref[...] = gv
        nm_ref[...] = nm
        nv_ref[...] = nv
        d_ref[...] = -ADAM_LR * ((nm / c1) / (jnp.sqrt(nv / c2) + ADAM_EPS) + ADAM_WD * w_ref[...])

    blk = pl.BlockSpec((None, tr, n), lambda l, i: (l, i, 0))
    return _call(
        body, (cond, dm_cols, w, m, v), grid=(nl, d // tr),
        in_specs=[pl.BlockSpec((_COND_ROWS, tr), lambda l, i: (0, i)),
                  pl.BlockSpec((None, _COND_ROWS, n), lambda l, i: (l, 0, 0)), blk, blk, blk],
        out_specs=[blk] * 4, out_shape=[_sds(w.shape, F32)] * 4, name="adamw_ada_w")


def _cond_grad(dm_cols, ada_w):
    nl, d, n = ada_w.shape
    return _matmul(
        dm_cols, ada_w, mode="nt", grid=(1, nl), nk=nl, acc_shape=(_COND_ROWS, d),
        a_spec=pl.BlockSpec((None, _COND_ROWS, n), lambda i, q: (q, 0, 0)), b_spec=pl.BlockSpec((None, d, n), lambda i, q: (q, 0, 0)),
        out_shapes=[_sds((_COND_ROWS, d), F32)], out_specs=[pl.BlockSpec((_COND_ROWS, d), lambda i, q: (0, 0))],
        name="cond_grad")[0]


def _pack(parts):
    flat = [p.reshape(-1) for p in parts]
    sizes = [f.shape[0] for f in flat]
    total = sum(sizes)
    rows = -(-total // 1024) * 8
    packed = jnp.concatenate(flat + [jnp.zeros((rows * 128 - total,), F32)]).reshape(rows, 128)
    offs = np.concatenate([[0], np.cumsum(sizes)])[:-1]
    return packed, [(int(o), p.shape) for o, p in zip(offs, parts)]


def _unpack(flat, layout, k):
    off, shape = layout[k]
    return flat[..., off:off + int(np.prod(shape))].reshape(flat.shape[:-1] + tuple(shape))


def kernel(x, c, ctx, c_ctx, norm_g, ada_w, ada_b, pool_w_in, pool_w_grp, pool_scale, pool_w_out, na_w_in, na_rpb, na_w_out, conv_w_in, conv_dw, conv_db, conv_w_out, final_g, loss_target, m_c_ctx, m_norm_g, m_ada_w, m_ada_b, m_pool_w_in, m_pool_w_grp, m_pool_scale, m_pool_w_out, m_na_w_in, m_na_rpb, m_na_w_out, m_conv_w_in, m_conv_dw, m_conv_db, m_conv_w_out, m_final_g, v_c_ctx, v_norm_g, v_ada_w, v_ada_b, v_pool_w_in, v_pool_w_grp, v_pool_scale, v_pool_w_out, v_na_w_in, v_na_rpb, v_na_w_out, v_conv_w_in, v_conv_dw, v_conv_db, v_conv_w_out, v_final_g):
    params = dict(c_ctx=c_ctx, norm_g=norm_g, ada_w=ada_w, ada_b=ada_b, pool_w_in=pool_w_in, pool_w_grp=pool_w_grp,
                  pool_scale=pool_scale, pool_w_out=pool_w_out, na_w_in=na_w_in, na_rpb=na_rpb, na_w_out=na_w_out,
                  conv_w_in=conv_w_in, conv_dw=conv_dw, conv_db=conv_db, conv_w_out=conv_w_out, final_g=final_g)
    mom1 = dict(c_ctx=m_c_ctx, norm_g=m_norm_g, ada_w=m_ada_w, ada_b=m_ada_b, pool_w_in=m_pool_w_in,
                pool_w_grp=m_pool_w_grp, pool_scale=m_pool_scale, pool_w_out=m_pool_w_out, na_w_in=m_na_w_in,
                na_rpb=m_na_rpb, na_w_out=m_na_w_out, conv_w_in=m_conv_w_in, conv_dw=m_conv_dw, conv_db=m_conv_db,
                conv_w_out=m_conv_w_out, final_g=m_final_g)
    mom2 = dict(c_ctx=v_c_ctx, norm_g=v_norm_g, ada_w=v_ada_w, ada_b=v_ada_b, pool_w_in=v_pool_w_in,
                pool_w_grp=v_pool_w_grp, pool_scale=v_pool_scale, pool_w_out=v_pool_w_out, na_w_in=v_na_w_in,
                na_rpb=v_na_rpb, na_w_out=v_na_w_out, conv_w_in=v_conv_w_in, conv_dw=v_conv_dw, conv_db=v_conv_db,
                conv_w_out=v_conv_w_out, final_g=v_final_g)
    d = x.shape[-1]
    w = na_w_out.shape[1] * 4
    xi, yi, ci = _position()
    chip = 2 * xi + yi
    dev = 2 * chip + ci
    n_ada = ada_w.shape[-1]

    def chip_cols(a, size):
        return lax.dynamic_slice_in_dim(a, chip * size, size, axis=a.ndim - 1)

    items = _items(d, w)
    first = [it for it in items if it.key.startswith("pool") and it.layer == 0]
    na = [it for it in items if it.key.startswith("na")]
    late = [it for it in items if it not in first + na]
    shards_of = lambda its: [params[it.key][it.layer].astype(BF16) for it in its]
    empties = lambda its: [lax.empty(it.shape, BF16) for it in its]
    first_copies, na_copies = _gather_ici_copies(first), _gather_ici_copies(na)

    conds = _all_gather(c.reshape(8, d // 8), _AXES, "gather_cond").reshape(8, d)
    behind = conds[0, 0] * 0.0
    first_handle, token = _split_start(first_copies, [s + behind.astype(BF16) for s in shards_of(first)],
                                       empties(first), 3 * len(first), "gather_first_start")
    cond = jnp.concatenate([conds + token[0, 0], c_ctx[None], jnp.zeros((_COND_ROWS - 9, d), F32)], axis=0)
    mod_cols = _modulations(cond, ada_w, chip_cols(ada_b, n_ada)[:, None, :])
    small_pack, small_layout = _pack([pool_scale, conv_dw, conv_db])
    (mod_all,), (small,) = _run_comms([_all_gather_comm(mod_cols, ("x", "y")),
                                       _all_gather_comm(small_pack, ("x", "y"))], "gather_mod")
    behind = mod_all[0, 0, 0, 0] * 0.0
    na_handle, token = _split_start(na_copies, [s + behind.astype(BF16) for s in shards_of(na)], empties(na),
                                    3 * len(na), "gather_na_start")
    first_shards, first_mats = _split_wait(first_copies, first_handle, token, "gather_first_wait")
    first_mats = _gather_pair_finish(first_shards, first_mats, first, "gather_first_pair")
    mod_all = mod_all.transpose(1, 2, 0, 3).reshape(4, _COND_ROWS, 3, d)
    mod = jnp.stack([lax.dynamic_index_in_dim(mod_all, dev, axis=1, keepdims=False), mod_all[:, 8]], axis=1)
    full = {(it.key, it.layer): mat for it, mat in zip(first, first_mats)}
    late_comm = _gather_comm(shards_of(late), late)

    def na_weights(after):
        na_shards, na_mats = _split_wait(na_copies, na_handle, after, "gather_na_wait")
        na_mats = _gather_pair_finish(na_shards, na_mats, na, "gather_na_pair")
        return {it.key: mat for it, mat in zip(na, na_mats)}

    def late_weights(mats):
        full.update({(it.key, it.layer): mat for it, mat in zip(late, mats)})
        return dict(pool_w_in=[full[("pool_w_in", j)] for j in range(2)],
                    pool_w_grp=[full[("pool_w_grp", j)] for j in range(2)],
                    pool_w_out=[full[("pool_w_out", j)] for j in range(2)],
                    conv_w_in=full[("conv_w_in", 0)], conv_w_out=full[("conv_w_out", 0)])

    small = small.reshape(4, -1)

    def whole(k):
        parts = _unpack(small, small_layout, k)
        return jnp.moveaxis(parts, 0, -2).reshape(parts.shape[1:-1] + (-1,))

    wts = dict(pool_w_in=[full[("pool_w_in", 0)]], pool_w_grp=[full[("pool_w_grp", 0)]],
               pool_w_out=[full[("pool_w_out", 0)]], pool_scale=whole(0), na_rpb=na_rpb[0], conv_dw=whole(1)[0],
               conv_db=whole(2))
    pos = jnp.stack([chip, ci]).astype(jnp.int32)

    def layer_grads(its, by_layer):
        pick = {"pool_w_in": "w_in", "pool_w_grp": "w_grp", "pool_w_out": "w_out", "na_w_in": "w_in",
                "na_w_out": "w_out", "conv_w_in": "w_in", "conv_w_out": "w_out"}
        return [by_layer[(it.key.split("_")[0], it.layer)][pick[it.key]] for it in its]

    pairs, handles = dict(), dict()
    half_windows = lambda its: [(lambda ref, half, it=it: it.window(ref, half=half)) for it in its]
    half_shapes = lambda its: [_sds(it.sized(half=True), BF16) for it in its]

    def pair_sums(its, mats, tag):
        got = _pair_swap(mats, half_windows(its), half_shapes(its), f"pair_exchange_{tag}")
        return _pair_sums(mats, got, its, pos, f"pair_sum_{tag}")

    def grad_comm(gr3, gr2):
        pairs["late"] = pair_sums(late, layer_grads(late, {("pool", 1): gr3, ("conv", 0): gr2}), "late")
        return _chip_exchange_comm(pairs["late"], late)

    slot_zones = lambda its: [lax.empty((3,) + it.sized(shard=True, half=True), BF16) for it in its]
    na_xcopies, first_xcopies = _chip_exchange_copies(na), _chip_exchange_copies(first)

    def na_grads_start(gr1):
        pairs["na"] = pair_sums(na, layer_grads(na, {("na", 0): gr1}), "na")
        handles["na"], started = _split_start(na_xcopies, pairs["na"], slot_zones(na), 3 * len(na),
                                              "exchange_na_start")
        return started

    res = _example_step(x[0], ctx[0], loss_target[0], mod, norm_g, final_g[None], wts, dict(
        na_weights=na_weights, late_comm=late_comm, late_weights=late_weights, grad_comm=grad_comm,
        na_grads_start=na_grads_start))
    g0, g1, g2, g3 = res["layers"]
    pairs["na"], na_slots = _split_wait(na_xcopies, handles["na"], g0["w_in"], "exchange_na_wait")
    first_grads = layer_grads(first, {("pool", 0): g0})
    packed, layout = _pack([res["dfinal_g"], res["dnorm_g"], res["dmod"], g1["rpb"],
                            jnp.concatenate([g0["scale"], g3["scale"]], axis=0), g2["dw"], g2["db"],
                            res["loss"][0, :1]])
    first_got, (every,) = _run_comms([_pair_swap_comm(first_grads, half_windows(first), half_shapes(first)),
                                      _all_gather_two_level_comm(packed)], "pair_exchange_first")
    pairs["first"] = _pair_sums(first_grads, first_got, first, pos, "pair_sum_first")

    grads = dict()
    total = _sum_lead(every, "sum_vec_grads").reshape(-1)
    every = every.reshape(8, -1)
    grads["final_g"] = _unpack(total, layout, 0).reshape(final_g.shape)
    grads["norm_g"] = _unpack(total, layout, 1)
    grads["na_rpb"] = _unpack(total, layout, 3)[None]
    grads["pool_scale"] = chip_cols(_unpack(total, layout, 4), pool_scale.shape[-1])
    grads["conv_dw"] = chip_cols(_unpack(total, layout, 5), conv_dw.shape[-1])[None]
    grads["conv_db"] = chip_cols(_unpack(total, layout, 6), conv_db.shape[-1])
    dmod_sum = _unpack(total, layout, 2).reshape(4, 2, 3 * d)
    dmod_each = _unpack(every, layout, 2).reshape(8, 4, 2, 3 * d)
    grads["ada_b"] = dmod_sum[:, 0] + dmod_sum[:, 1]
    dm = jnp.concatenate([dmod_each[:, :, 0].transpose(1, 0, 2), dmod_sum[:, 1][:, None],
                          jnp.zeros((4, _COND_ROWS - 9, 3 * d), F32)], axis=1)
    dm_cols = chip_cols(dm, n_ada)
    dcond = _cond_grad(dm_cols, ada_w)[8].reshape(8, d // 8)
    dcond_all = _all_gather(dcond, ("x", "y"), "gather_cond_grad")
    behind = dcond_all[0, 0, 0] * 0.0
    handles["first"], token = _split_start(first_xcopies, [p + behind.astype(BF16) for p in pairs["first"]],
                                           slot_zones(first), 3 * len(first), "exchange_first_start")
    grads["ada_w"], *ada_w_step = _ada_w_step(cond, dm_cols + token[0, 0], ada_w, m_ada_w, v_ada_w)
    grads["c_ctx"] = _sum_lead(dcond_all, "sum_cond_grad").reshape(d) * _dsilu(c_ctx)
    vector_out = {k: _adamw(params[k], grads[k], mom1[k], mom2[k], f"adamw_{k}")
                  for k in _WEIGHTS if k not in _GRAD_KEYS + ("ada_w",)}
    vector_out["ada_w"] = tuple(ada_w_step)
    pairs["first"], first_slots = _split_wait(first_xcopies, handles["first"], vector_out["ada_w"][2],
                                              "exchange_first_wait")

    slots = dict(zip(late, res["carried"]))
    slots.update(zip(first, first_slots))
    slots.update(zip(na, na_slots))
    pair_of = dict(zip(late, pairs["late"]))
    pair_of.update(zip(first, pairs["first"]))
    pair_of.update(zip(na, pairs["na"]))
    reduced = _chip_sums([pair_of[it] for it in items], [slots[it] for it in items], items, pos, "chip_sum")
    theirs = _pair_swap(reduced, [lambda ref, half: ref] * len(items),
                        [_sds(t.shape, F32) for t in reduced], "pair_return")
    matrix_out = dict()
    for k in _GRAD_KEYS:
        idx = [i for i, it in enumerate(items) if it.key == k]
        res_k = _adamw_matrix(params[k], mom1[k], mom2[k], [reduced[i] for i in idx], [theirs[i] for i in idx],
                              items[idx[0]], pos, f"adamw_{k}")
        grads[k], matrix_out[k] = res_k[0], res_k[1:]

    outs = [[], [], []]
    for k in _WEIGHTS:
        step = matrix_out[k] if k in matrix_out else vector_out[k]
        for lst, val in zip(outs, step):
            lst.append(val)
    loss = _unpack(total, layout, 7)[0]
    return (loss, res["grad_x"][None], *[grads[k].reshape(params[k].shape) for k in _WEIGHTS],
            *outs[0], *outs[1], *outs[2])
```

```python
import functools

import numpy as np
import jax
import jax.numpy as jnp
from jax import lax
from jax.experimental import pallas as pl
from jax.experimental.pallas import tpu as pltpu

F32 = jnp.float32
BF16 = jnp.bfloat16

EPS = 1e-6
GRID_W = 64
HEAD_DIM = 64
WIN_ROWS = 8
WIN_COLS = 16
POOL_WINDOWS = (2, 4, 8, 16)
Q_ROWS = 4
K_ROWS = 12
PAD_ROWS = 4
NEG = -1e30

ADAM_LR = 0.001
ADAM_B1 = 0.9
ADAM_B2 = 0.999
ADAM_EPS = 1e-08
ADAM_WD = 0.01
ADAM_STEP = 10

ROW_BLOCK = 256
VMEM_LIMIT = 56 * 1024 * 1024
ACT = BF16

MESH = pl.DeviceIdType.MESH
HBM_SPEC = pl.BlockSpec(memory_space=pltpu.HBM)


def _cparams(*sem):
    return pltpu.CompilerParams(dimension_semantics=sem or None, vmem_limit_bytes=VMEM_LIMIT)


def _sds(shape, dtype):
    return jax.ShapeDtypeStruct(tuple(shape), dtype)


def _call(body, args, *, grid, in_specs, out_specs, out_shape, name, scratch_shapes=()):
    return list(pl.pallas_call(
        body, grid=grid, in_specs=list(in_specs), out_specs=list(out_specs), out_shape=list(out_shape),
        scratch_shapes=list(scratch_shapes), name=name, compiler_params=_cparams(*(("arbitrary",) * len(grid))),
    )(*args))


def _sigmoid(x):
    return 1.0 / (1.0 + jnp.exp(-x))


def _silu(x):
    return x * _sigmoid(x)


def _dsilu(x):
    s = _sigmoid(x)
    return s * (1.0 + x * (1.0 - s))


_DIMS = {
    "nn": (((1,), (0,)), ((), ())),
    "nt": (((1,), (1,)), ((), ())),
    "tn": (((0,), (0,)), ((), ())),
}


def _matmul(a, b, *, mode, grid, a_spec, b_spec, out_shapes, out_specs, name, nk=1,
            a_silu=False, exact=False, epilogue=None, extra=(), extra_specs=(), acc_shape=None):
    n_extra = len(extra)
    n_out = len(out_shapes)

    def body(*refs):
        a_ref, b_ref = refs[:2]
        ex = refs[2:2 + n_extra]
        outs = refs[2 + n_extra:2 + n_extra + n_out]
        av = a_ref[...]
        bv = b_ref[...]
        if a_silu:
            av = _silu(av.astype(F32))
        if exact:
            prod = lax.dot_general(av.astype(F32), bv.astype(F32), _DIMS[mode],
                                   precision=lax.Precision.HIGHEST, preferred_element_type=F32)
        else:
            prod = lax.dot_general(av.astype(BF16), bv.astype(BF16), _DIMS[mode], preferred_element_type=F32)

        def finish(res):
            if epilogue == "bias":
                res = res + ex[0][...]
            outs[0][...] = res.astype(outs[0].dtype)

        if nk == 1:
            finish(prod)
        else:
            acc = refs[-1]
            k = pl.program_id(len(grid) - 1)

            @pl.when(k == 0)
            def _():
                acc[...] = prod

            @pl.when(k > 0)
            def _():
                acc[...] += prod

            @pl.when(k == nk - 1)
            def _():
                finish(acc[...])

    scratch = [pltpu.VMEM(acc_shape, F32)] if nk > 1 else []
    sem = ("parallel",) * (len(grid) - 1) + ("arbitrary",)
    return pl.pallas_call(
        body, grid=grid, in_specs=[a_spec, b_spec, *extra_specs], out_specs=list(out_specs),
        out_shape=list(out_shapes), scratch_shapes=scratch, name=name, compiler_params=_cparams(*sem),
    )(a, b, *extra)


def _row_tile(rows):
    for t in (768, 512, 256):
        if rows % t == 0:
            return t
    return rows


def _mm_tn(a, b, name, out_dtype, tm=512):
    r, m = a.shape
    n = b.shape[1]
    tm = min(tm, m)
    tn = min(1024, n)
    return _matmul(
        a, b, mode="tn", grid=(m // tm, n // tn),
        a_spec=pl.BlockSpec((r, tm), lambda i, j: (0, i)), b_spec=pl.BlockSpec((r, tn), lambda i, j: (0, j)),
        out_shapes=[_sds((m, n), out_dtype)], out_specs=[pl.BlockSpec((tm, tn), lambda i, j: (i, j))], name=name)[0]


def _mm_tn_parts(a, b, name, out_dtype, tm=512):
    r, m = a.shape
    p, _, np_ = b.shape
    tm = min(tm, m)
    return _matmul(
        a, b, mode="tn", grid=(m // tm, p),
        a_spec=pl.BlockSpec((r, tm), lambda i, q: (0, i)), b_spec=pl.BlockSpec((None, r, np_), lambda i, q: (q, 0, 0)),
        out_shapes=[_sds((m, p * np_), out_dtype)], out_specs=[pl.BlockSpec((tm, np_), lambda i, q: (i, q))],
        name=name)[0]


def _row_vec(ref, is_ctx):
    return ref[0] if is_ctx is None else jnp.where(is_ctx, ref[1], ref[0])


def _ctx_rows(i, tm, nx, nseg):
    if nseg == 1:
        return None
    return i * tm + lax.broadcasted_iota(jnp.int32, (tm, 1), 0) >= nx


def _seg_sums(ref, val, is_ctx, first):
    if is_ctx is None:
        parts = [jnp.sum(val, axis=0, keepdims=True)]
    else:
        parts = [jnp.sum(jnp.where(is_ctx, 0.0, val), axis=0, keepdims=True),
                 jnp.sum(jnp.where(is_ctx, val, 0.0), axis=0, keepdims=True)]

    @pl.when(first)
    def _():
        for k, p in enumerate(parts):
            ref[k] = p

    @pl.when(jnp.logical_not(first))
    def _():
        for k, p in enumerate(parts):
            ref[k] += p


def _w_out_resid(a, w_out, xres, gate, nx, name):
    m, k = a.shape
    n = w_out.shape[1]
    nseg = gate.shape[0]
    tm = _row_tile(m)

    def body(a_ref, w_ref, x_ref, gt_ref, yx_ref, xo_ref):
        yx = jnp.dot(a_ref[...], w_ref[...], preferred_element_type=F32)
        yx_ref[...] = yx.astype(ACT)
        xo_ref[...] = x_ref[...] + _row_vec(gt_ref, _ctx_rows(pl.program_id(0), tm, nx, nseg)) * yx

    row = pl.BlockSpec((tm, n), lambda i: (i, 0))
    return pl.pallas_call(
        body, grid=(m // tm,),
        in_specs=[pl.BlockSpec((tm, k), lambda i: (i, 0)), pl.BlockSpec((k, n), lambda i: (0, 0)), row,
                  pl.BlockSpec((nseg, 1, n), lambda i: (0, 0, 0))],
        out_specs=[row, row], out_shape=[_sds((m, n), ACT), _sds((m, n), F32)],
        name=name, compiler_params=_cparams("parallel"),
    )(a, w_out, xres, gate)


def _norm_w_in(x, g, scale, shift, w_in, nx, name, ctx=None):
    d = x.shape[1]
    rows = x.shape[0] + (0 if ctx is None else ctx.shape[0])
    n = w_in.shape[1]
    nseg = scale.shape[0]
    tm = _row_tile(rows)
    tn = min(1024, n)
    row = pl.BlockSpec((tm, d), lambda i, j: (i, 0))
    if ctx is None:
        row_args, row_specs = (x,), [row]
    else:
        assert ctx.shape[0] == ROW_BLOCK and tm % ROW_BLOCK == 0 and nx % ROW_BLOCK == 0
        nsub, x_blocks = tm // ROW_BLOCK, nx // ROW_BLOCK
        row_args = (x,) * nsub + (ctx,)
        row_specs = [pl.BlockSpec((ROW_BLOCK, d), lambda i, j, s=s: (jnp.minimum(i * nsub + s, x_blocks - 1), 0))
                     for s in range(nsub)] + [pl.BlockSpec((ROW_BLOCK, d), lambda i, j: (0, 0))]

    def body(*refs):
        x_refs, (g_ref, sc_ref, sh_ref, w_ref), outs = refs[:len(row_args)], refs[len(row_args):][:4], refs[-3:]
        h_ref, r_ref, p_ref = outs
        i, j = pl.program_id(0), pl.program_id(1)

        @pl.when(j == 0)
        def _():
            if ctx is None:
                xv = x_refs[0][...]
            else:
                xv = jnp.concatenate([jnp.where(i * nsub + s >= x_blocks, x_refs[-1][...], x_refs[s][...])
                                      for s in range(nsub)], axis=0)
                refs[-4][...] = xv
            r = lax.rsqrt(jnp.mean(xv * xv, axis=-1, keepdims=True) + EPS)
            is_ctx = _ctx_rows(i, tm, nx, nseg)
            h = (xv * r) * g_ref[...] * (1.0 + _row_vec(sc_ref, is_ctx)) + _row_vec(sh_ref, is_ctx)
            h_ref[...] = h.astype(BF16)
            r_ref[...] = r

        p_ref[...] = jnp.dot(h_ref[...], w_ref[...], preferred_element_type=F32).astype(ACT)

    vec = pl.BlockSpec((nseg, 1, d), lambda i, j: (0, 0, 0))
    joined = [] if ctx is None else [(row, _sds((rows, d), F32))]
    out_specs, out_shape = zip(*joined, (row, _sds((rows, d), BF16)),
                               (pl.BlockSpec((tm, 1), lambda i, j: (i, 0)), _sds((rows, 1), F32)),
                               (pl.BlockSpec((tm, tn), lambda i, j: (i, j)), _sds((rows, n), ACT)))
    return _call(
        body, (*row_args, g, scale, shift, w_in), grid=(rows // tm, n // tn),
        in_specs=[*row_specs, pl.BlockSpec((1, d), lambda i, j: (0, 0)), vec, vec,
                  pl.BlockSpec((d, tn), lambda i, j: (0, j))],
        out_specs=list(out_specs), out_shape=list(out_shape), name=name)


def _gate_w_out_bwd(dxo, yx, gate, w_out, nx, name):
    rows, d = yx.shape
    w = w_out.shape[0]
    nseg = gate.shape[0]
    tm = _row_tile(rows)

    def body(dx_ref, yx_ref, gt_ref, w_ref, dyx_ref, da_ref, dg_ref):
        i = pl.program_id(0)
        is_ctx = _ctx_rows(i, tm, nx, nseg)
        dxv = dx_ref[...]
        dyx = (dxv * _row_vec(gt_ref, is_ctx)).astype(BF16)
        dyx_ref[...] = dyx
        da_ref[...] = lax.dot_general(dyx, w_ref[...], _DIMS["nt"], preferred_element_type=F32).astype(ACT)
        _seg_sums(dg_ref, dxv * yx_ref[...].astype(F32), is_ctx, i == 0)

    row = pl.BlockSpec((tm, d), lambda i: (i, 0))
    vec = pl.BlockSpec((nseg, 1, d), lambda i: (0, 0, 0))
    return _call(
        body, (dxo, yx, gate, w_out), grid=(rows // tm,),
        in_specs=[row, row, vec, pl.BlockSpec((w, d), lambda i: (0, 0))],
        out_specs=[row, pl.BlockSpec((tm, w), lambda i: (i, 0)), vec],
        out_shape=[_sds((rows, d), BF16), _sds((rows, w), ACT), _sds((nseg, 1, d), F32)], name=name)


def _w_in_bwd_norm(dparts, w_in, x, r, g, scale, dres, nx, name, dx_rows=None):
    np_, rows, kp = dparts.shape
    d = w_in.shape[0]
    nseg = scale.shape[0]
    tm = _row_tile(rows)
    assert dx_rows is None or rows - tm < dx_rows <= rows
    nsub = tm // ROW_BLOCK
    nres_blocks = dres.shape[0] // ROW_BLOCK

    def body(dp_ref, w_ref, x_ref, r_ref, g_ref, sc_ref, *rest):
        dres_refs = rest[:nsub]
        dx_ref, dsh_ref, dge_ref, acc = rest[nsub:]
        i, k = pl.program_id(0), pl.program_id(1)
        prod = lax.dot_general(dp_ref[...], w_ref[...], _DIMS["nt"], preferred_element_type=F32)

        @pl.when(k == 0)
        def _():
            acc[...] = prod

        @pl.when(k > 0)
        def _():
            acc[...] += prod

        @pl.when(k == np_ - 1)
        def _():
            is_ctx = _ctx_rows(i, tm, nx, nseg)
            dhv = acc[...]
            rv = r_ref[...]
            xn = x_ref[...] * rv
            dxn = dhv * (g_ref[...] * (1.0 + _row_vec(sc_ref, is_ctx)))
            dx = rv * (dxn - xn * jnp.mean(dxn * xn, axis=-1, keepdims=True))
            for s in range(nsub):
                piece = slice(s * ROW_BLOCK, (s + 1) * ROW_BLOCK)
                res = dres_refs[s][...]
                if nres_blocks * ROW_BLOCK < rows:
                    res = jnp.where(i * nsub + s < nres_blocks, res, 0.0)
                dx_ref[piece, :] = dx[piece, :] + res
            _seg_sums(dsh_ref, dhv, is_ctx, i == 0)
            _seg_sums(dge_ref, dhv * xn, is_ctx, i == 0)

    row = pl.BlockSpec((tm, d), lambda i, k: (i, 0))
    vec = pl.BlockSpec((nseg, 1, d), lambda i, k: (0, 0, 0))
    return _call(
        body, (dparts, w_in, x, r, g, scale, *([dres] * nsub)), grid=(rows // tm, np_),
        in_specs=[pl.BlockSpec((None, tm, kp), lambda i, k: (k, i, 0)), pl.BlockSpec((d, kp), lambda i, k: (0, k)),
                  row, pl.BlockSpec((tm, 1), lambda i, k: (i, 0)), pl.BlockSpec((1, d), lambda i, k: (0, 0)), vec]
        + [pl.BlockSpec((ROW_BLOCK, d), (lambda i, k, s=s: (jnp.minimum(i * nsub + s, nres_blocks - 1), 0)))
           for s in range(nsub)],
        out_specs=[row, vec, vec],
        out_shape=[_sds((dx_rows or rows, d), F32), _sds((nseg, 1, d), F32), _sds((nseg, 1, d), F32)],
        scratch_shapes=[pltpu.VMEM((tm, d), F32)], name=name)


_PAD_TOP = 16
_PAD_BOT = 32


def _window_sum(buf, xv, lo, n):
    t = xv.shape[0]
    c = xv.shape[1]
    tp = t + _PAD_TOP + _PAD_BOT
    buf[pl.ds(0, _PAD_TOP), :] = jnp.zeros((_PAD_TOP, c), F32)
    buf[pl.ds(_PAD_TOP, t), :] = xv
    buf[pl.ds(_PAD_TOP + t, _PAD_BOT), :] = jnp.zeros((_PAD_BOT, c), F32)
    p = buf[...]
    k = 1
    while k < n:
        p = p + pltpu.roll(p, tp - k, 0)
        k *= 2
    if lo:
        p = pltpu.roll(p, -lo, 0)
    buf[...] = p
    return buf[pl.ds(_PAD_TOP, t), :]


def _window_count(t, half):
    pos = lax.broadcasted_iota(jnp.int32, (t, 1), 0)
    return (jnp.minimum(pos + half, t) - jnp.maximum(pos - half, 0)).astype(F32)


def _segments(rows, nx):
    return [(0, nx)] + ([(nx, rows - nx)] if rows > nx else [])


def _pool_scratch(rows, nx, cols):
    return [pltpu.VMEM((length + _PAD_TOP + _PAD_BOT, cols), F32) for _, length in _segments(rows, nx)]


def _per_group(g, fn):
    for gi, win in enumerate(POOL_WINDOWS):
        pl.when(g == gi)(functools.partial(fn, win))


def _pool_grp_fwd(uv, w_grp, scale, nx, name):
    rows = uv.shape[0]
    ng, gc, _ = w_grp.shape
    w = ng * gc
    segs = _segments(rows, nx)

    def body(u_ref, gt_ref, w_ref, sc_ref, z_ref, mx_ref, a_ref, *bufs):
        def pool(win):
            half = win // 2
            for (start, length), buf in zip(segs, bufs):
                uvv = u_ref[pl.ds(start, length), :].astype(F32)
                s = _window_sum(buf, uvv, -half, win)
                z_ref[pl.ds(start, length), :] = (s / _window_count(length, half) - uvv).astype(BF16)

        _per_group(pl.program_id(0), pool)
        mixed = jnp.dot(z_ref[...], w_ref[...], preferred_element_type=F32)
        mx_ref[...] = mixed.astype(ACT)
        a_ref[...] = (mixed * sc_ref[...] * _silu(gt_ref[...].astype(F32))).astype(BF16)

    col = pl.BlockSpec((rows, gc), lambda g: (0, g))
    return _call(
        body, (uv, uv, w_grp, scale), grid=(ng,),
        in_specs=[col, pl.BlockSpec((rows, gc), lambda g: (0, ng + g)), pl.BlockSpec((None, gc, gc), lambda g: (g, 0, 0)),
                  pl.BlockSpec((1, gc), lambda g: (0, g))],
        out_specs=[col, col, col], out_shape=[_sds((rows, w), BF16), _sds((rows, w), ACT), _sds((rows, w), BF16)],
        scratch_shapes=_pool_scratch(rows, nx, gc), name=name)


def _pool_grp_bwd(da, mixed, uv, scale, w_grp, nx, name):
    rows, w = da.shape
    ng, gc, _ = w_grp.shape
    segs = _segments(rows, nx)

    def body(da_ref, mx_ref, gt_ref, sc_ref, w_ref, dm_ref, duv_ref, dsc_ref, dz_ref, *bufs):
        dav = da_ref[...].astype(F32)
        mixed = mx_ref[...].astype(F32)
        gt = gt_ref[...].astype(F32)
        sg = _silu(gt)
        sc = sc_ref[...]
        dm = (dav * sc * sg).astype(BF16)
        dm_ref[...] = dm
        dz_ref[...] = lax.dot_general(dm, w_ref[...], _DIMS["nt"], preferred_element_type=F32)
        duv_ref[1] = (dav * mixed * sc * _dsilu(gt)).astype(BF16)
        dsc_ref[...] = jnp.sum(dav * mixed * sg, axis=0, keepdims=True)

        def unpool(win):
            half = win // 2
            for (start, length), buf in zip(segs, bufs):
                dzv = dz_ref[pl.ds(start, length), :]
                s = _window_sum(buf, dzv / _window_count(length, half), 1 - half, win)
                duv_ref[0, pl.ds(start, length), :] = (s - dzv).astype(BF16)

        _per_group(pl.program_id(0), unpool)

    col = pl.BlockSpec((rows, gc), lambda g: (0, g))
    vec = pl.BlockSpec((1, gc), lambda g: (0, g))
    return pl.pallas_call(
        body, grid=(ng,),
        in_specs=[col, col, pl.BlockSpec((rows, gc), lambda g: (0, ng + g)), vec,
                  pl.BlockSpec((None, gc, gc), lambda g: (g, 0, 0))],
        out_specs=[col, pl.BlockSpec((2, rows, gc), lambda g: (0, 0, g)), vec],
        out_shape=[_sds((rows, w), BF16), _sds((2, rows, w), BF16), _sds((1, w), F32)],
        scratch_shapes=[pltpu.VMEM((rows, gc), F32)] + _pool_scratch(rows, nx, gc),
        name=name, compiler_params=_cparams("parallel"),
    )(da, mixed, uv, scale, w_grp)


def _grp_wgrad(z, dm, ng, name, out_dtype):
    rows, w = z.shape
    gc = w // ng

    def body(z_ref, dm_ref, o_ref):
        o_ref[...] = lax.dot_general(z_ref[...], dm_ref[...], _DIMS["tn"],
                                     preferred_element_type=F32).astype(o_ref.dtype)

    blk = pl.BlockSpec((rows, gc), lambda g: (0, g))
    return pl.pallas_call(
        body, grid=(ng,), in_specs=[blk, blk], out_specs=pl.BlockSpec((None, gc, gc), lambda g: (g, 0, 0)),
        out_shape=_sds((ng, gc, gc), out_dtype), name=name, compiler_params=_cparams("parallel"),
    )(z, dm)


def _shift_rows(v, by):
    t = v.shape[0]
    pos = lax.broadcasted_iota(jnp.int32, v.shape, 0)
    rolled = pltpu.roll(v, by % t, 0)
    keep = pos >= by if by > 0 else pos < t + by
    return jnp.where(keep, rolled, 0.0)


def _conv_specs(t, w, cb):
    return [pl.BlockSpec((t, cb), (lambda j, q=q: (0, q * (w // cb) + j))) for q in range(4)]


def _conv_fwd(p4, dw, db, name):
    t = p4.shape[0]
    w = p4.shape[1] // 4
    cb = 128

    def body(bg_ref, cg_ref, v_ref, g_ref, dw_ref, db_ref, a_ref):
        tv = cg_ref[...].astype(F32) * v_ref[...].astype(F32)
        conv = (dw_ref[0:1, :] * _shift_rows(tv, 1) + dw_ref[1:2, :] * tv + dw_ref[2:3, :] * _shift_rows(tv, -1)
                + db_ref[...])
        a_ref[...] = (bg_ref[...].astype(F32) * conv * _silu(g_ref[...].astype(F32))).astype(BF16)

    return pl.pallas_call(
        body, grid=(w // cb,),
        in_specs=_conv_specs(t, w, cb) + [pl.BlockSpec((3, cb), lambda j: (0, j)), pl.BlockSpec((1, cb), lambda j: (0, j))],
        out_specs=pl.BlockSpec((t, cb), lambda j: (0, j)), out_shape=_sds((t, w), BF16),
        name=name, compiler_params=_cparams("parallel"),
    )(p4, p4, p4, p4, dw, db)


def _conv_bwd(da, p4, dw, db, name):
    t, w = da.shape
    cb = 128

    def body(da_ref, bg_ref, cg_ref, v_ref, g_ref, dw_ref, db_ref, d4_ref, ddw_ref, ddb_ref):
        cg = cg_ref[...].astype(F32)
        vv = v_ref[...].astype(F32)
        bg = bg_ref[...].astype(F32)
        gv = g_ref[...].astype(F32)
        tv = cg * vv
        tm1 = _shift_rows(tv, 1)
        tp1 = _shift_rows(tv, -1)
        w0, w1, w2 = dw_ref[0:1, :], dw_ref[1:2, :], dw_ref[2:3, :]
        conv = w0 * tm1 + w1 * tv + w2 * tp1 + db_ref[...]
        y = bg * conv
        dav = da_ref[...].astype(F32)
        dy = dav * _silu(gv)
        d4_ref[3] = (dav * y * _dsilu(gv)).astype(BF16)
        d4_ref[0] = (dy * conv).astype(BF16)
        dconv = dy * bg
        ddb_ref[...] = jnp.sum(dconv, axis=0, keepdims=True)
        ddw_ref[0:1, :] = jnp.sum(dconv * tm1, axis=0, keepdims=True)
        ddw_ref[1:2, :] = jnp.sum(dconv * tv, axis=0, keepdims=True)
        ddw_ref[2:3, :] = jnp.sum(dconv * tp1, axis=0, keepdims=True)
        dt = w0 * _shift_rows(dconv, -1) + w1 * dconv + w2 * _shift_rows(dconv, 1)
        d4_ref[1] = (dt * vv).astype(BF16)
        d4_ref[2] = (dt * cg).astype(BF16)

    col = pl.BlockSpec((t, cb), lambda j: (0, j))
    tap = pl.BlockSpec((3, cb), lambda j: (0, j))
    bias = pl.BlockSpec((1, cb), lambda j: (0, j))
    return pl.pallas_call(
        body, grid=(w // cb,), in_specs=[col] + _conv_specs(t, w, cb) + [tap, bias],
        out_specs=[pl.BlockSpec((4, t, cb), lambda j: (0, 0, j)), tap, bias],
        out_shape=[_sds((4, t, w), BF16), _sds((3, w), F32), _sds((1, w), F32)],
        name=name, compiler_params=_cparams("parallel"),
    )(da, p4, p4, p4, p4, dw, db)


def _attn_mask():
    qn, kn = Q_ROWS * GRID_W, K_ROWS * GRID_W
    qr, qc = np.divmod(np.arange(qn), GRID_W)
    kr, kc = np.divmod(np.arange(kn), GRID_W)
    col0 = np.clip(qc - WIN_COLS // 2, 0, GRID_W - WIN_COLS)
    col_ok = (kc[None, :] >= col0[:, None]) & (kc[None, :] < col0[:, None] + WIN_COLS)
    first = np.zeros(qn, np.int64)
    last = np.full(qn, K_ROWS - WIN_ROWS)
    out = []
    for row0 in (first, qr, last):
        row_ok = (kr[None, :] >= row0[:, None]) & (kr[None, :] < row0[:, None] + WIN_ROWS)
        out.append(np.where(row_ok & col_ok, 0.0, NEG))
    return jnp.asarray(np.stack(out), F32)


_KW = K_ROWS * GRID_W
_QB = Q_ROWS * GRID_W
_PAIR = 2 * HEAD_DIM
_N_DR = 2 * WIN_ROWS - 1
_N_DC = 2 * WIN_COLS - 1
_RP_ROWS = 24
_N_TILES = _N_DR + 1
_BIAS_BASE = (WIN_ROWS - 1, WIN_ROWS // 2 - 1, -1)


class _Comm:
    def __init__(self, ins, outs, sems, start, finish):
        self.ins, self.outs, self.sems, self.start, self.finish = list(ins), list(outs), list(sems), start, finish


def _bias_pieces(cls):
    out = []
    for qr in range(Q_ROWS):
        for kr in range(0, K_ROWS, 2):
            tile = _BIAS_BASE[cls] - qr + kr + 1
            out.append((qr, kr, tile if 0 <= tile < _N_TILES else None))
    return out


def _toeplitz_pair(left_row, right_row):
    lane = lax.broadcasted_iota(jnp.int32, (GRID_W, _PAIR), 1)
    shape = (GRID_W, _PAIR)
    left = pltpu.roll(jnp.broadcast_to(left_row, shape), _PAIR - (WIN_COLS - 1), 1, stride=1, stride_axis=0)
    right = pltpu.roll(jnp.broadcast_to(right_row, shape), GRID_W - (WIN_COLS - 1), 1, stride=1, stride_axis=0)
    return jnp.where(lane < GRID_W, left, right)


def _build_tiles(tiles_ref, rp_ref):
    for h in range(2):
        for t in range(_N_TILES):
            tiles_ref[h, t] = _toeplitz_pair(rp_ref[h, t:t + 1, :], rp_ref[h, t + 1:t + 2, :])


def _block_class(b, nblk, fn, entering=False):
    interior = (b == 1) if entering else jnp.logical_and(b > 0, b < nblk - 1)
    for cls, cond in enumerate((b == 0, interior, b == nblk - 1)):
        pl.when(cond)(functools.partial(fn, cls))


def _attn_geometry(p4, nx):
    rows = p4.shape[0]
    w = p4.shape[1] // 4
    nhp = w // _PAIR
    nblk = nx // _QB
    qspec = lambda col: pl.BlockSpec((_QB, _PAIR), lambda hp, b: (b, col * nhp + hp))
    kspec = lambda col: pl.BlockSpec((rows, _PAIR), lambda hp, b: (0, col * nhp + hp))
    tspec = pl.BlockSpec((2, _RP_ROWS, _PAIR), lambda hp, b: (hp, 0, 0))
    mspec = pl.BlockSpec((None, _QB, _KW), lambda hp, b: (jnp.where(b == 0, 0, jnp.where(b == nblk - 1, 2, 1)), 0, 0))
    lspec = pl.BlockSpec((None, _QB, 2), lambda hp, b: (hp, b, 0))
    ospec = pl.BlockSpec((_QB, _PAIR), lambda hp, b: (b, hp))
    return rows, w, nhp, nblk, qspec, kspec, tspec, mspec, lspec, ospec


def _window_start(b, nx):
    return pl.multiple_of(jnp.clip(b * _QB - PAD_ROWS * GRID_W, 0, nx - _KW), _QB)


def _load_bias(bias_ref, tiles_ref, rp_ref, m_ref, b, nblk):
    pl.when(b == 0)(lambda: _build_tiles(tiles_ref, rp_ref))

    def fill(cls):
        for h in range(2):
            for qr, kr, tile in _bias_pieces(cls):
                rows = slice(qr * GRID_W, (qr + 1) * GRID_W)
                cols = slice(kr * GRID_W, (kr + 2) * GRID_W)
                m = m_ref[rows, cols]
                bias_ref[h, rows, cols] = m if tile is None else tiles_ref[h, tile] + m

    _block_class(b, nblk, fill, entering=True)


def _attn_fwd(p4, rp, mask, nx, name, comm=None):
    rows, w, nhp, nblk, qspec, kspec, tspec, mspec, lspec, ospec = _attn_geometry(p4, nx)
    n_ctx = rows - nx
    n_cin, n_cout = (len(comm.ins), len(comm.outs)) if comm else (0, 0)

    def body(*refs):
        q_ref, k_ref, v_ref, g_ref, rp_ref, m_ref = refs[:6]
        cin = refs[6:6 + n_cin]
        a_ref, o_ref, lse_ref = refs[6 + n_cin:9 + n_cin]
        cout = refs[9 + n_cin:9 + n_cin + n_cout]
        bias_ref, tiles_ref = refs[9 + n_cin + n_cout:11 + n_cin + n_cout]
        sems = refs[11 + n_cin + n_cout:]
        hp, b = pl.program_id(0), pl.program_id(1)
        if comm:
            pl.when(jnp.logical_and(hp == 0, b == 0))(lambda: comm.start(cin, cout, sems))
        start = _window_start(b, nx)
        _load_bias(bias_ref, tiles_ref, rp_ref, m_ref, b, nblk)
        qf = q_ref[...].astype(F32) * HEAD_DIM ** -0.5
        kw = k_ref[pl.ds(start, _KW), :].astype(BF16)
        vw = v_ref[pl.ds(start, _KW), :].astype(BF16)
        kcv = k_ref[pl.ds(nx, n_ctx), :].astype(BF16)
        vcv = v_ref[pl.ds(nx, n_ctx), :].astype(BF16)
        lane = lax.broadcasted_iota(jnp.int32, (1, _PAIR), 1)
        outs, lses = [], []
        for h in range(2):
            mine = (lane >= HEAD_DIM) if h else (lane < HEAD_DIM)
            qm = jnp.where(mine, qf, 0.0).astype(BF16)
            s_loc = lax.dot_general(qm, kw, _DIMS["nt"], preferred_element_type=F32) + bias_ref[h]
            s_ctx = lax.dot_general(qm, kcv, _DIMS["nt"], preferred_element_type=F32)
            mx = jnp.maximum(jnp.max(s_loc, axis=-1, keepdims=True), jnp.max(s_ctx, axis=-1, keepdims=True))
            p_loc = jnp.exp(s_loc - mx)
            p_ctx = jnp.exp(s_ctx - mx)
            den = jnp.sum(p_loc, axis=-1, keepdims=True) + jnp.sum(p_ctx, axis=-1, keepdims=True)
            o = jnp.dot(p_loc.astype(BF16), vw, preferred_element_type=F32)
            o = o + jnp.dot(p_ctx.astype(BF16), vcv, preferred_element_type=F32)
            outs.append(o * (1.0 / den))
            lses.append(mx + jnp.log(den))
        o = jnp.where(lane < HEAD_DIM, outs[0], outs[1])
        o_ref[...] = o.astype(ACT)
        a_ref[...] = (o * _silu(g_ref[...].astype(F32))).astype(BF16)
        col = lax.broadcasted_iota(jnp.int32, (1, 2), 1)
        lse_ref[...] = jnp.where(col == 0, lses[0], lses[1])
        if comm:
            pl.when(jnp.logical_and(hp == nhp - 1, b == nblk - 1))(lambda: comm.finish(cin, cout, sems))

    res = pl.pallas_call(
        body, grid=(nhp, nblk),
        in_specs=[qspec(0), kspec(1), kspec(2), qspec(3), tspec, mspec] + [HBM_SPEC] * n_cin,
        out_specs=[ospec, ospec, lspec] + [HBM_SPEC] * n_cout,
        out_shape=[_sds((nx, w), BF16), _sds((nx, w), ACT), _sds((nhp, nx, 2), F32)] + (comm.outs if comm else []),
        scratch_shapes=[pltpu.VMEM((2, _QB, _KW), F32), pltpu.VMEM((2, _N_TILES, GRID_W, _PAIR), F32)]
        + (comm.sems if comm else []),
        name=name, compiler_params=_cparams("arbitrary", "arbitrary"),
    )(p4, p4, p4, p4, rp, mask, *(comm.ins if comm else []))
    return res[:3], res[3:]


def _fold_tiles(dtiles_ref, drp_ref):
    shape = (GRID_W, _PAIR)
    lane = lax.broadcasted_iota(jnp.int32, shape, 1)
    flip = (lax.broadcasted_iota(jnp.int32, (_PAIR, _PAIR), 0)
            + lax.broadcasted_iota(jnp.int32, (_PAIR, _PAIR), 1) == _PAIR - 1).astype(F32)
    drp_ref[...] = jnp.zeros(drp_ref.shape, F32)
    for h in range(2):
        stack = dtiles_ref[h].reshape(_N_TILES * GRID_W, _PAIR)
        rev = jnp.dot(stack, flip, precision=lax.Precision.HIGHEST, preferred_element_type=F32)
        for t in range(_N_TILES):
            tile = rev[t * GRID_W:(t + 1) * GRID_W, :]
            for side in (0, 1):
                shift = _PAIR - GRID_W * side - (WIN_COLS - 1)
                half = jnp.where((lane < GRID_W) if side else (lane >= GRID_W), tile, 0.0)
                diag = pltpu.roll(half, shift, 1, stride=1, stride_axis=0)
                drp_ref[h, t + side:t + side + 1, :] += jnp.sum(diag, axis=0, keepdims=True)


def _attn_bwd(p4, rp, mask, o, lse, da, nx, name, comm=None):
    rows, w, nhp, nblk, qspec, kspec, tspec, mspec, lspec, ospec = _attn_geometry(p4, nx)
    n_ctx = rows - nx
    n_cin, n_cout = (len(comm.ins), len(comm.outs)) if comm else (0, 0)

    def body(*refs):
        q_ref, k_ref, v_ref, g_ref, rp_ref, m_ref, o_ref, lse_ref, da_ref = refs[:9]
        cin = refs[9:9 + n_cin]
        d4_ref, drp_ref = refs[9 + n_cin:11 + n_cin]
        cout = refs[11 + n_cin:11 + n_cin + n_cout]
        bias_ref, tiles_ref, ds_ref, dtiles_ref, dk_ref, dv_ref = refs[11 + n_cin + n_cout:17 + n_cin + n_cout]
        sems = refs[17 + n_cin + n_cout:]
        hp, b = pl.program_id(0), pl.program_id(1)
        if comm:
            pl.when(jnp.logical_and(hp == 0, b == 0))(lambda: comm.start(cin, cout, sems))
        start = _window_start(b, nx)
        here = pl.multiple_of(b * _QB, _QB)

        @pl.when(b == 0)
        def _():
            dk_ref[...] = jnp.zeros(dk_ref.shape, F32)
            dv_ref[...] = jnp.zeros(dv_ref.shape, F32)
            dtiles_ref[...] = jnp.zeros(dtiles_ref.shape, F32)
            d4_ref[0, pl.ds(nx, n_ctx), :] = jnp.zeros((n_ctx, _PAIR), BF16)
            d4_ref[3, pl.ds(nx, n_ctx), :] = jnp.zeros((n_ctx, _PAIR), BF16)

        _load_bias(bias_ref, tiles_ref, rp_ref, m_ref, b, nblk)
        gv = g_ref[...].astype(F32)
        dav = da_ref[...].astype(F32)
        ov = o_ref[...].astype(F32)
        dov = dav * _silu(gv)
        d4_ref[3, pl.ds(here, _QB), :] = (dav * ov * _dsilu(gv)).astype(BF16)
        qf = q_ref[...].astype(F32) * HEAD_DIM ** -0.5
        kw = k_ref[pl.ds(start, _KW), :].astype(BF16)
        vw = v_ref[pl.ds(start, _KW), :].astype(BF16)
        kcv = k_ref[pl.ds(nx, n_ctx), :].astype(BF16)
        vcv = v_ref[pl.ds(nx, n_ctx), :].astype(BF16)
        lane = lax.broadcasted_iota(jnp.int32, (1, _PAIR), 1)
        dq = jnp.zeros((_QB, _PAIR), F32)
        for h in range(2):
            mine = (lane >= HEAD_DIM) if h else (lane < HEAD_DIM)
            qm = jnp.where(mine, qf, 0.0).astype(BF16)
            dom = jnp.where(mine, dov, 0.0)
            dob = dom.astype(BF16)
            lse = lse_ref[:, h:h + 1]
            s_loc = lax.dot_general(qm, kw, _DIMS["nt"], preferred_element_type=F32)
            p_loc = jnp.exp(s_loc + bias_ref[h] - lse)
            p_ctx = jnp.exp(lax.dot_general(qm, kcv, _DIMS["nt"], preferred_element_type=F32) - lse)
            delta = jnp.sum(dom * ov, axis=-1, keepdims=True)
            ds_loc = p_loc * (lax.dot_general(dob, vw, _DIMS["nt"], preferred_element_type=F32) - delta)
            ds_ctx = p_ctx * (lax.dot_general(dob, vcv, _DIMS["nt"], preferred_element_type=F32) - delta)
            dsb_loc = ds_loc.astype(BF16)
            dsb_ctx = ds_ctx.astype(BF16)
            dq_h = (jnp.dot(dsb_loc, kw, preferred_element_type=F32)
                    + jnp.dot(dsb_ctx, kcv, preferred_element_type=F32))
            dq = dq + jnp.where(mine, dq_h, 0.0)
            dk_ref[pl.ds(start, _KW), :] += lax.dot_general(dsb_loc, qm, _DIMS["tn"], preferred_element_type=F32)
            dv_ref[pl.ds(start, _KW), :] += lax.dot_general(p_loc.astype(BF16), dob, _DIMS["tn"],
                                                            preferred_element_type=F32)
            dk_ref[pl.ds(nx, n_ctx), :] += lax.dot_general(dsb_ctx, qm, _DIMS["tn"], preferred_element_type=F32)
            dv_ref[pl.ds(nx, n_ctx), :] += lax.dot_general(p_ctx.astype(BF16), dob, _DIMS["tn"],
                                                           preferred_element_type=F32)
            ds_ref[h] = ds_loc
        d4_ref[0, pl.ds(here, _QB), :] = (dq * HEAD_DIM ** -0.5).astype(BF16)

        def scatter(cls):
            for h in range(2):
                for qr, kr, tile in _bias_pieces(cls):
                    if tile is not None:
                        dtiles_ref[h, tile] += ds_ref[h, qr * GRID_W:(qr + 1) * GRID_W, kr * GRID_W:(kr + 2) * GRID_W]

        _block_class(b, nblk, scatter)

        @pl.when(b == nblk - 1)
        def _():
            d4_ref[1] = dk_ref[...].astype(BF16)
            d4_ref[2] = dv_ref[...].astype(BF16)
            _fold_tiles(dtiles_ref, drp_ref)

        if comm:
            pl.when(jnp.logical_and(hp == nhp - 1, b == nblk - 1))(lambda: comm.finish(cin, cout, sems))

    tiles = pltpu.VMEM((2, _N_TILES, GRID_W, _PAIR), F32)
    block = pltpu.VMEM((2, _QB, _KW), F32)
    res = pl.pallas_call(
        body, grid=(nhp, nblk),
        in_specs=[qspec(0), kspec(1), kspec(2), qspec(3), tspec, mspec, ospec, lspec, ospec] + [HBM_SPEC] * n_cin,
        out_specs=[pl.BlockSpec((4, rows, _PAIR), lambda hp, b: (0, 0, hp)), tspec] + [HBM_SPEC] * n_cout,
        out_shape=[_sds((4, rows, w), BF16), _sds(rp.shape, F32)] + (comm.outs if comm else []),
        scratch_shapes=[block, tiles, block, tiles, pltpu.VMEM((rows, _PAIR), F32), pltpu.VMEM((rows, _PAIR), F32)]
        + (comm.sems if comm else []),
        name=name, compiler_params=_cparams("arbitrary", "arbitrary"),
    )(p4, p4, p4, p4, rp, mask, o, lse, da, *(comm.ins if comm else []))
    return res[:2], res[2:]


def _w_out_loss(a, w_out, xres, gate, g, target, name):
    m, k = a.shape
    d = w_out.shape[1]
    assert gate.shape[0] == 1
    tm = _row_tile(m)
    nblk = m // tm

    def body(a_ref, w_ref, x_ref, gt_ref, g_ref, t_ref, yx_ref, loss_ref, dx_ref, dg_ref, acc_ref):
        i = pl.program_id(0)
        yx = jnp.dot(a_ref[...], w_ref[...], preferred_element_type=F32)
        yx_ref[...] = yx.astype(ACT)
        xv = x_ref[...] + gt_ref[0] * yx
        gv = g_ref[...]
        r = lax.rsqrt(jnp.mean(xv * xv, axis=-1, keepdims=True) + EPS)
        xn = xv * r
        err = xn * gv - t_ref[...]
        dy = err * (1.0 / d)
        dxn = dy * gv
        dx_ref[...] = r * (dxn - xn * jnp.mean(dxn * xn, axis=-1, keepdims=True))
        s_g = jnp.sum(dy * xn, axis=0, keepdims=True)
        s_l = jnp.sum(jnp.mean(err * err, axis=-1, keepdims=True), axis=0, keepdims=True)

        @pl.when(i == 0)
        def _():
            dg_ref[...] = s_g
            acc_ref[...] = s_l

        @pl.when(i > 0)
        def _():
            dg_ref[...] += s_g
            acc_ref[...] += s_l

        @pl.when(i == nblk - 1)
        def _():
            loss_ref[...] = jnp.broadcast_to(0.5 * acc_ref[...], loss_ref.shape)

    row = pl.BlockSpec((tm, d), lambda i: (i, 0))
    vec = pl.BlockSpec((1, d), lambda i: (0, 0))
    return pl.pallas_call(
        body, grid=(nblk,),
        in_specs=[pl.BlockSpec((tm, k), lambda i: (i, 0)), pl.BlockSpec((k, d), lambda i: (0, 0)), row,
                  pl.BlockSpec((1, 1, d), lambda i: (0, 0, 0)), vec, row],
        out_specs=[row, pl.BlockSpec((1, 128), lambda i: (0, 0)), row, vec],
        out_shape=[_sds((m, d), ACT), _sds((1, 128), F32), _sds((m, d), F32), _sds((1, d), F32)],
        scratch_shapes=[pltpu.VMEM((1, 1), F32)], name=name, compiler_params=_cparams("arbitrary"),
    )(a, w_out, xres, gate, g, target)


def _as2d(a):
    if a.ndim == 1:
        return a.reshape(-1, 128) if a.shape[0] % 128 == 0 else a.reshape(1, -1)
    return a.reshape(-1, a.shape[-1])


def _adamw(w, g, m, v, name):
    shape = w.shape
    w2, g2, m2, v2 = (_as2d(t) for t in (w, g.reshape(shape), m, v))
    rows, cols = w2.shape
    tr = 512 if rows % 512 == 0 else rows
    c1 = 1.0 - ADAM_B1 ** ADAM_STEP
    c2 = 1.0 - ADAM_B2 ** ADAM_STEP

    def body(w_ref, g_ref, m_ref, v_ref, d_ref, nm_ref, nv_ref):
        gv = g_ref[...]
        nm = ADAM_B1 * m_ref[...] + (1.0 - ADAM_B1) * gv
        nv = ADAM_B2 * v_ref[...] + (1.0 - ADAM_B2) * (gv * gv)
        nm_ref[...] = nm
        nv_ref[...] = nv
        d_ref[...] = -ADAM_LR * ((nm / c1) / (jnp.sqrt(nv / c2) + ADAM_EPS) + ADAM_WD * w_ref[...])

    blk = pl.BlockSpec((tr, cols), lambda i: (i, 0))
    outs = _call(body, (w2, g2, m2, v2), grid=(rows // tr,), in_specs=[blk] * 4, out_specs=[blk] * 3,
                 out_shape=[_sds((rows, cols), F32)] * 3, name=name)
    return tuple(t.reshape(shape) for t in outs)


def _sum_lead(x, name, out_dtype=F32):
    n, rows, cols = x.shape
    tr = 512 if rows % 512 == 0 else rows

    def body(x_ref, o_ref):
        acc = x_ref[0].astype(F32)
        for k in range(1, n):
            acc = acc + x_ref[k].astype(F32)
        o_ref[...] = acc.astype(out_dtype)

    return pl.pallas_call(
        body, grid=(rows // tr,), in_specs=[pl.BlockSpec((n, tr, cols), lambda i: (0, i, 0))],
        out_specs=pl.BlockSpec((tr, cols), lambda i: (i, 0)), out_shape=_sds((rows, cols), out_dtype),
        name=name, compiler_params=_cparams("parallel"),
    )(x)


def _seg_vecs(mod_l, which, nseg):
    return mod_l[:nseg, which][:, None, :]


def _norm_grads(dshift, dgeff, dgate, g, scale):
    nseg, _, d = dshift.shape
    dmod = jnp.stack([dshift[:, 0], dgeff[:, 0] * g, dgate[:, 0]], axis=1)
    if nseg == 1:
        dmod = jnp.concatenate([dmod, jnp.zeros((1, 3, d), F32)], axis=0)
    dg = jnp.sum(dgeff[:, 0] * (1.0 + scale[:, 0]), axis=0)
    return dmod, dg


def _pool_layer(xin, g, mod_l, w_in, w_grp, w_out, pscale, nx, tag, ctx=None, head=None):
    nseg = 1 if ctx is None else 2
    shift, scale, gate = (_seg_vecs(mod_l, k, nseg) for k in range(3))
    *joined, h, r, uv = _norm_w_in(xin, g, scale, shift, w_in, nx, f"w_in_fwd_{tag}", ctx)
    if joined:
        xin, = joined
    z, mixed, a = _pool_grp_fwd(uv, w_grp, pscale, nx, f"pool_fwd_{tag}")
    if head is None:
        yx, xout = _w_out_resid(a, w_out, xin, gate, nx, f"w_out_fwd_{tag}")
    else:
        yx, *xout = _w_out_loss(a, w_out, xin, gate, *head, f"w_out_loss_{tag}")

    def backward(dxo, token=None):
        gate_b = gate if token is None else gate + token[0, 0]
        dyx, da, dgate = _gate_w_out_bwd(dxo, yx, gate_b, w_out, nx, f"w_out_bwd_{tag}")
        gw_out = _mm_tn(a, dyx, f"w_out_grad_{tag}", BF16)
        dm, duv, dscale = _pool_grp_bwd(da, mixed, uv, pscale, w_grp, nx, f"pool_bwd_{tag}")
        gw_grp = _grp_wgrad(z, dm, w_grp.shape[0], f"grp_grad_{tag}", BF16)
        gw_in = _mm_tn_parts(h, duv, f"w_in_grad_{tag}", BF16)
        dx, dshift, dgeff = _w_in_bwd_norm(duv, w_in, xin, r, g, scale, dxo, nx, f"w_in_bwd_{tag}",
                                           dx_rows=None if ctx is None else nx)
        dmod, dg = _norm_grads(dshift, dgeff, dgate, g[0], scale)
        return dx, dmod, dg, dict(w_in=gw_in, w_grp=gw_grp, w_out=gw_out, scale=dscale)

    return xout, backward


def _na_layer(xc, g, mod_l, w_in, rpb, w_out, nx, mask, comm=None):
    nh, n_dr, n_dc = rpb.shape
    shift, scale = _seg_vecs(mod_l, 0, 2), _seg_vecs(mod_l, 1, 2)
    gate = _seg_vecs(mod_l, 2, 1)
    h, r, p4 = _norm_w_in(xc, g, scale, shift, w_in, nx, "w_in_fwd_na")
    rp = jnp.pad(rpb, ((0, 0), (1, _RP_ROWS - 1 - n_dr), (0, _PAIR - n_dc)))
    (a, o, lse), carried = _attn_fwd(p4, rp, mask, nx, "attn_fwd", comm)
    yx, xout = _w_out_resid(a, w_out, xc, gate, nx, "w_out_fwd_na")

    def backward(dxo, comm=None):
        dyx, da, dgate = _gate_w_out_bwd(dxo, yx, gate, w_out, nx, "w_out_bwd_na")
        gw_out = _mm_tn(a, dyx, "w_out_grad_na", BF16)
        (d4, drp), carried_bwd = _attn_bwd(p4, rp, mask, o, lse, da, nx, "attn_bwd", comm)
        gw_in = _mm_tn_parts(h, d4, "w_in_grad_na", BF16)
        dx, dshift, dgeff = _w_in_bwd_norm(d4, w_in, xc, r, g, scale, dxo, nx, "w_in_bwd_na")
        dgate2 = jnp.concatenate([dgate, jnp.zeros_like(dgate)], axis=0)
        dmod, dg = _norm_grads(dshift, dgeff, dgate2, g[0], scale)
        drpb = drp[:, 1:1 + n_dr, ::-1][:, :, :n_dc]
        return dx, dmod, dg, dict(w_in=gw_in, w_out=gw_out, rpb=drpb), carried_bwd

    return xout, backward, carried


def _conv_layer(xin, g, mod_l, w_in, dw, db, w_out):
    shift, scale, gate = (_seg_vecs(mod_l, k, 1) for k in range(3))
    nx = xin.shape[0]
    h, r, p4 = _norm_w_in(xin, g, scale, shift, w_in, nx, "w_in_fwd_conv")
    a = _conv_fwd(p4, dw, db, "conv_fwd")
    yx, xout = _w_out_resid(a, w_out, xin, gate, nx, "w_out_fwd_conv")

    def backward(dxo):
        dyx, da, dgate = _gate_w_out_bwd(dxo, yx, gate, w_out, nx, "w_out_bwd_conv")
        gw_out = _mm_tn(a, dyx, "w_out_grad_conv", BF16)
        d4, ddw, ddb = _conv_bwd(da, p4, dw, db, "conv_bwd")
        gw_in = _mm_tn_parts(h, d4, "w_in_grad_conv", BF16)
        dx, dshift, dgeff = _w_in_bwd_norm(d4, w_in, xin, r, g, scale, dxo, nx, "w_in_bwd_conv")
        dmod, dg = _norm_grads(dshift, dgeff, dgate, g[0], scale)
        return dx, dmod, dg, dict(w_in=gw_in, w_out=gw_out, dw=ddw, db=ddb)

    return xout, backward


def _example_step(x, ctx, target, mod, norm_g, final_g, wts, hooks=None):
    hooks = hooks or {}
    na_weights, late_comm, late_weights = (hooks.get(k) for k in ("na_weights", "late_comm", "late_weights"))
    nx = x.shape[0]
    consts = _attn_mask()
    g_rows = [norm_g[i:i + 1] for i in range(4)]
    xc1, bwd0 = _pool_layer(x, g_rows[0], mod[0], wts["pool_w_in"][0], wts["pool_w_grp"][0],
                            wts["pool_w_out"][0], wts["pool_scale"][0:1], nx, "p0", ctx=ctx)
    if na_weights is not None:
        wts = {**wts, **na_weights(xc1)}
    x2, bwd1, carried = _na_layer(xc1, g_rows[1], mod[1], wts["na_w_in"], wts["na_rpb"], wts["na_w_out"], nx, consts,
                                  late_comm)
    if late_weights is not None:
        wts = {**wts, **late_weights(carried)}
    x3, bwd2 = _conv_layer(x2, g_rows[2], mod[2], wts["conv_w_in"], wts["conv_dw"], wts["conv_db"], wts["conv_w_out"])
    (loss, dx4, dfinal_g), bwd3 = _pool_layer(x3, g_rows[3], mod[3], wts["pool_w_in"][1], wts["pool_w_grp"][1],
                                              wts["pool_w_out"][1], wts["pool_scale"][1:2], nx, "p3",
                                              head=(final_g, target))
    call = lambda k, *args: hooks[k](*args) if k in hooks else None
    dx3, dmod3, dg3, gr3 = bwd3(dx4)
    dx2, dmod2, dg2, gr2 = bwd2(dx3)
    dxc1, dmod1, dg1, gr1, carried_bwd = bwd1(dx2, call("grad_comm", gr3, gr2))
    dx0, dmod0, dg0, gr0 = bwd0(dxc1, call("na_grads_start", gr1))
    return dict(
        loss=loss, grad_x=dx0, dmod=jnp.stack([dmod0, dmod1, dmod2, dmod3]),
        dnorm_g=jnp.stack([dg0, dg1, dg2, dg3]), dfinal_g=dfinal_g, layers=(gr0, gr1, gr2, gr3), carried=carried_bwd)


_AXES = ("x", "y", "c")
_CHIP_FLIPS = ((1, 0), (0, 1), (1, 1))


def _position():
    return tuple(lax.axis_index(a) for a in _AXES)


def _flipped(pos, flip):
    return tuple(1 - p if f else p for p, f in zip(pos, flip))


def _join_comms(comms):
    n_in = [len(c.ins) for c in comms]
    n_out = [len(c.outs) for c in comms]
    n_sem = [len(c.sems) for c in comms]

    def parts(ins, outs, sems):
        for k in range(len(comms)):
            a, b, s = sum(n_in[:k]), sum(n_out[:k]), sum(n_sem[:k])
            yield comms[k], (ins[a:a + n_in[k]], outs[b:b + n_out[k]], sems[s:s + n_sem[k]])

    def start(ins, outs, sems):
        for c, part in parts(ins, outs, sems):
            c.start(*part)

    def finish(ins, outs, sems):
        for c, part in parts(ins, outs, sems):
            c.finish(*part)

    joint = _Comm([a for c in comms for a in c.ins], [o for c in comms for o in c.outs],
                  [s for c in comms for s in c.sems], start, finish)
    return joint, lambda res: [list(res[sum(n_out[:k]):sum(n_out[:k + 1])]) for k in range(len(comms))]


def _run_comms(comms, name):
    joint, split = _join_comms(comms)

    def body(*refs):
        n_in, n_out = len(joint.ins), len(joint.outs)
        joint.start(refs[:n_in], refs[n_in:n_in + n_out], refs[n_in + n_out:])
        joint.finish(refs[:n_in], refs[n_in:n_in + n_out], refs[n_in + n_out:])

    res = pl.pallas_call(
        body, in_specs=[HBM_SPEC] * len(joint.ins), out_specs=[HBM_SPEC] * len(joint.outs), out_shape=joint.outs,
        scratch_shapes=joint.sems, name=name,
    )(*joint.ins)
    return split(res)


def _all_gather_comm(v, axes):
    flips = [f for f in np.ndindex(2, 2, 2) if any(f) and all(a in axes or not b for a, b in zip(_AXES, f))]
    n = len(flips) + 1

    def copies(ins, outs, sems):
        (v_ref,), (o_ref,), (send_sems, recv_sems, local_sem) = ins, outs, sems
        pos = _position()
        slot = 0
        for a, p in zip(_AXES, pos):
            if a in axes:
                slot = 2 * slot + p
        local = pltpu.make_async_copy(v_ref, o_ref.at[slot], local_sem)
        remote = [pltpu.make_async_remote_copy(v_ref, o_ref.at[slot], send_sems.at[k], recv_sems.at[k],
                                               device_id=_flipped(pos, flip), device_id_type=MESH)
                  for k, flip in enumerate(flips)]
        return [local] + remote

    def start(ins, outs, sems):
        for cp in copies(ins, outs, sems):
            cp.start()

    def finish(ins, outs, sems):
        for cp in copies(ins, outs, sems):
            cp.wait()

    sems = [pltpu.SemaphoreType.DMA((n - 1,)), pltpu.SemaphoreType.DMA((n - 1,)), pltpu.SemaphoreType.DMA(())]
    return _Comm([v], [_sds((n,) + v.shape, v.dtype)], sems, start, finish)


def _all_gather_two_level_comm(v):
    def copies(ins, outs, sems, onward):
        (v_ref,), (o_ref,), (send_sems, recv_sems, local_sem) = ins, outs, sems
        x, y, c = _position()
        sibling = (x, y, 1 - c)
        slot = lambda px, py, pc: o_ref.at[4 * px + 2 * py + pc]
        own = pltpu.make_async_copy(v_ref, slot(x, y, c), local_sem)
        first = [pltpu.make_async_remote_copy(v_ref, slot(x, y, c), send_sems.at[0], recv_sems.at[0],
                                              device_id=sibling, device_id_type=MESH)]
        fwd = []
        for k, flip in enumerate(_CHIP_FLIPS):
            px, py = _flipped((x, y), flip)
            first.append(pltpu.make_async_remote_copy(v_ref, slot(x, y, c), send_sems.at[1 + k], recv_sems.at[1 + k],
                                                      device_id=(px, py, c), device_id_type=MESH))
            if onward:
                fwd.append(pltpu.make_async_remote_copy(slot(px, py, c), slot(px, py, c), send_sems.at[4 + k],
                                                        recv_sems.at[4 + k], device_id=sibling, device_id_type=MESH))
        return own, first, fwd

    def start(ins, outs, sems):
        own, first, _ = copies(ins, outs, sems, False)
        for cp in [own] + first:
            cp.start()

    def finish(ins, outs, sems):
        own, first, fwd = copies(ins, outs, sems, True)
        for arrived, onward in zip(first[1:], fwd):
            arrived.wait_recv()
            onward.start()
        first[0].wait_recv()
        for cp in fwd:
            cp.wait_recv()
        for cp in first + fwd:
            cp.wait_send()
        own.wait()

    sems = [pltpu.SemaphoreType.DMA((7,)), pltpu.SemaphoreType.DMA((7,)), pltpu.SemaphoreType.DMA(())]
    return _Comm([v], [_sds((8,) + v.shape, v.dtype)], sems, start, finish)


def _all_gather(v, axes, name):
    return _run_comms([_all_gather_comm(v, axes)], name)[0][0]


class _Item:
    def __init__(self, key, layer, shape, shard_axis, half_axis):
        self.key, self.layer, self.shape = key, layer, tuple(shape)
        self.shard_axis, self.half_axis = shard_axis, half_axis
        self.shard = shape[shard_axis] // 4
        self.half = shape[half_axis] // 2

    def sized(self, shard=False, half=False):
        s = list(self.shape)
        if shard:
            s[self.shard_axis] = self.shard
        if half:
            s[self.half_axis] = self.half
        return tuple(s)

    def window(self, ref, chip=None, half=None):
        idx = [slice(None)] * len(self.shape)
        if chip is not None:
            idx[self.shard_axis] = pl.ds(chip * self.shard, self.shard)
        if half is not None:
            idx[self.half_axis] = pl.ds(half * self.half, self.half)
        return ref.at[tuple(idx)]


def _items(d, w):
    out = []
    for j in range(2):
        out += [_Item("pool_w_in", j, (d, 2 * w), 1, 0), _Item("pool_w_grp", j, (4, w // 4, w // 4), 1, 0),
                _Item("pool_w_out", j, (w, d), 0, 1)]
    out += [_Item("na_w_in", 0, (d, 4 * w), 1, 0), _Item("na_w_out", 0, (w, d), 0, 1),
            _Item("conv_w_in", 0, (d, 4 * w), 1, 0), _Item("conv_w_out", 0, (w, d), 0, 1)]
    return out


def _gather_comm(shards, items):
    n = len(items)

    def copies(src, dst, sems, onward):
        send_a, recv_a, send_b, recv_b, send_c, recv_c = sems
        x, y, c = _position()
        chip = 2 * x + y
        sibling = (x, y, 1 - c)
        own, out, fwd, fwd_in = [], [], [], []
        for i, it in enumerate(items):
            own.append(pltpu.make_async_remote_copy(src[i], it.window(dst[i], chip=chip), send_c.at[i], recv_c.at[i],
                                                    device_id=sibling, device_id_type=MESH))
            for k, flip in enumerate(_CHIP_FLIPS):
                px, py = _flipped((x, y), flip)
                s = 3 * i + k
                out.append(pltpu.make_async_remote_copy(
                    it.window(src[i], half=c), it.window(dst[i], chip=chip, half=c), send_a.at[s], recv_a.at[s],
                    device_id=(px, py, c), device_id_type=MESH))
                if onward:
                    got = it.window(dst[i], chip=2 * px + py, half=c)
                    fwd.append(pltpu.make_async_remote_copy(got, got, send_b.at[s], recv_b.at[s],
                                                            device_id=sibling, device_id_type=MESH))
                    other = it.window(dst[i], chip=2 * px + py, half=1 - c)
                    fwd_in.append(pltpu.make_async_remote_copy(other, other, send_b.at[s], recv_b.at[s],
                                                               device_id=sibling, device_id_type=MESH))
        return own, out, fwd, fwd_in

    def start(src, dst, sems):
        own, out, _, _ = copies(src, dst, sems, False)
        for cp in own + out:
            cp.start()

    def finish(src, dst, sems):
        own, out, fwd, fwd_in = copies(src, dst, sems, True)
        for arrived, onward in zip(out, fwd):
            arrived.wait_recv()
            onward.start()
        for cp in fwd_in:
            cp.wait_recv()
        for cp in out + fwd:
            cp.wait_send()
        for cp in own:
            cp.wait()

    sems = [pltpu.SemaphoreType.DMA((3 * n,)) for _ in range(4)] + [pltpu.SemaphoreType.DMA((n,)) for _ in range(2)]
    return _Comm(shards, [_sds(it.shape, BF16) for it in items], sems, start, finish)


def _pair_swap_copies(windows):
    def copies(src, got, sems):
        send_sems, recv_sems = sems
        x, y, c = _position()
        return [pltpu.make_async_remote_copy(windows[i](src[i], 1 - c), got[i], send_sems.at[i], recv_sems.at[i],
                                             device_id=(x, y, 1 - c), device_id_type=MESH)
                for i in range(len(windows))]

    return copies


def _pair_swap_comm(arrays, windows, out_shapes):
    n = len(arrays)
    copies = _pair_swap_copies(windows)

    def start(src, got, sems):
        for cp in copies(src, got, sems):
            cp.start()

    def finish(src, got, sems):
        for cp in copies(src, got, sems):
            cp.wait()

    return _Comm(arrays, out_shapes, [pltpu.SemaphoreType.DMA((n,)), pltpu.SemaphoreType.DMA((n,))], start, finish)


def _pair_swap(arrays, windows, out_shapes, name):
    return _run_comms([_pair_swap_comm(arrays, windows, out_shapes)], name)[0]


def _chip_exchange_copies(items):
    def copies(src, dst, sems):
        send_sems, recv_sems = sems
        x, y, c = _position()
        out = []
        for i, it in enumerate(items):
            for k, flip in enumerate(_CHIP_FLIPS):
                px, py = _flipped((x, y), flip)
                out.append(pltpu.make_async_remote_copy(
                    it.window(src[i], chip=2 * px + py), dst[i].at[k], send_sems.at[3 * i + k],
                    recv_sems.at[3 * i + k], device_id=(px, py, c), device_id_type=MESH))
        return out

    return copies


_SEM_SPEC = pl.BlockSpec(memory_space=pltpu.SEMAPHORE)
_DATAFLOW = pltpu.SideEffectType.DATAFLOW_SIDE_EFFECTING


def _split_start(copies, srcs, zones, n_copies, name):
    n, nz = len(srcs), len(zones)

    def body(*refs):
        src, land = refs[:n], refs[n:n + nz]
        send_sems, recv_sems = refs[n + nz:n + nz + 2]
        token = refs[-1]
        for cp in copies(src, land, (send_sems, recv_sems)):
            cp.start()
        token[...] = jnp.zeros(token.shape, F32)

    hbm = lambda t: pltpu.HBM(t.shape, t.dtype)
    res = pl.pallas_call(
        body, name=name,
        out_shape=(pltpu.SemaphoreType.DMA((n_copies,)), pltpu.SemaphoreType.DMA((n_copies,)),
                   *[hbm(t) for t in list(srcs) + list(zones)], _sds((8, 128), F32)),
        in_specs=[HBM_SPEC] * (n + nz),
        out_specs=(_SEM_SPEC, _SEM_SPEC, *[HBM_SPEC] * (n + nz), pl.BlockSpec(memory_space=pltpu.VMEM)),
        input_output_aliases={i: 2 + i for i in range(n + nz)},
        compiler_params=pltpu.CompilerParams(has_side_effects=_DATAFLOW),
    )(*[pltpu.with_memory_space_constraint(t, pltpu.HBM) for t in list(srcs) + list(zones)])
    return (res[0], res[1], list(res[2:2 + n]), list(res[2 + n:2 + n + nz])), res[-1]


def _split_wait(copies, handle, after, name):
    send_sems, recv_sems, srcs, zones = handle
    n, nz = len(srcs), len(zones)

    def body(*refs):
        src, land = refs[:n], refs[n:n + nz]
        send, recv = refs[n + nz:n + nz + 2]
        for cp in copies(src, land, (send, recv)):
            cp.wait_send()
            cp.wait_recv()

    hbm = lambda t: pltpu.HBM(t.shape, t.dtype)
    res = pl.pallas_call(
        body, name=name, out_shape=tuple(hbm(t) for t in list(srcs) + list(zones)),
        in_specs=[HBM_SPEC] * (n + nz) + [_SEM_SPEC, _SEM_SPEC, pl.BlockSpec(memory_space=pl.ANY)],
        out_specs=tuple([HBM_SPEC] * (n + nz)), input_output_aliases={i: i for i in range(n + nz)},
        compiler_params=pltpu.CompilerParams(has_side_effects=_DATAFLOW),
    )(*srcs, *zones, send_sems, recv_sems, after)
    return list(res[:n]), list(res[n:])


def _gather_ici_copies(items):
    def copies(src, dst, sems):
        send_sems, recv_sems = sems
        x, y, c = _position()
        chip = 2 * x + y
        out = []
        for i, it in enumerate(items):
            for k, flip in enumerate(_CHIP_FLIPS):
                px, py = _flipped((x, y), flip)
                out.append(pltpu.make_async_remote_copy(
                    it.window(src[i], half=c), it.window(dst[i], chip=chip, half=c), send_sems.at[3 * i + k],
                    recv_sems.at[3 * i + k], device_id=(px, py, c), device_id_type=MESH))
        return out

    return copies


def _gather_pair_finish(shards, mats, items, name):
    n = len(items)

    def body(*refs):
        src, dst = refs[:n], refs[2 * n:3 * n]
        send_own, recv_own, send_fwd, recv_fwd = refs[3 * n:]
        x, y, c = _position()
        chip = 2 * x + y
        sibling = (x, y, 1 - c)
        copies = []
        for i, it in enumerate(items):
            copies.append(pltpu.make_async_remote_copy(src[i], it.window(dst[i], chip=chip), send_own.at[i],
                                                       recv_own.at[i], device_id=sibling, device_id_type=MESH))
            for k, flip in enumerate(_CHIP_FLIPS):
                px, py = _flipped((x, y), flip)
                got = it.window(dst[i], chip=2 * px + py, half=c)
                copies.append(pltpu.make_async_remote_copy(got, got, send_fwd.at[3 * i + k], recv_fwd.at[3 * i + k],
                                                           device_id=sibling, device_id_type=MESH))
        for cp in copies:
            cp.start()
        for cp in copies:
            cp.wait()

    return pl.pallas_call(
        body, in_specs=[HBM_SPEC] * (2 * n), out_specs=[HBM_SPEC] * n, out_shape=[_sds(it.shape, BF16) for it in items],
        input_output_aliases={n + i: i for i in range(n)},
        scratch_shapes=[pltpu.SemaphoreType.DMA((n,)), pltpu.SemaphoreType.DMA((n,)),
                        pltpu.SemaphoreType.DMA((3 * n,)), pltpu.SemaphoreType.DMA((3 * n,))], name=name,
    )(*shards, *mats)


def _chip_exchange_comm(partials, items):
    n = len(items)
    copies = _chip_exchange_copies(items)

    def start(src, dst, sems):
        for cp in copies(src, dst, sems):
            cp.start()

    def finish(src, dst, sems):
        for cp in copies(src, dst, sems):
            cp.wait()

    return _Comm(partials, [_sds((3,) + it.sized(shard=True, half=True), BF16) for it in items],
                 [pltpu.SemaphoreType.DMA((3 * n,)), pltpu.SemaphoreType.DMA((3 * n,))], start, finish)


_SUM_STEPS = 2


def _pair_sums(gs, gots, its, pos, name):
    n = len(its)
    nb = _SUM_STEPS
    g2 = [g.reshape(-1, g.shape[-1]) for g in gs]
    got2 = [t.reshape(-1, t.shape[-1]) for t in gots]

    def body(pos_ref, *refs):
        for g_ref, got_ref, o_ref in zip(refs[:n], refs[n:2 * n], refs[2 * n:]):
            o_ref[...] = (g_ref[...].astype(F32) + got_ref[...].astype(F32)).astype(BF16)

    g_specs, got_specs = [], []
    for it, t in zip(its, got2):
        rows, cols = t.shape
        blk = (rows // nb, cols)
        g_map = (lambda i, pos: (pos[1] * nb + i, 0)) if it.half_axis == 0 else (lambda i, pos: (i, pos[1]))
        g_specs.append(pl.BlockSpec(blk, g_map))
        got_specs.append(pl.BlockSpec(blk, lambda i, pos: (i, 0)))
    outs = pl.pallas_call(
        body, grid_spec=pltpu.PrefetchScalarGridSpec(
            num_scalar_prefetch=1, grid=(nb,), in_specs=g_specs + got_specs, out_specs=got_specs),
        out_shape=[_sds(t.shape, BF16) for t in got2], name=name, compiler_params=_cparams("parallel"),
    )(pos, *g2, *got2)
    return [o.reshape(t.shape) for o, t in zip(outs, gots)]


_FLIP_SLOT = {2: 0, 1: 1, 3: 2}


def _chip_sums(pairs, slots, its, pos, name):
    n = len(its)
    nb = _SUM_STEPS

    def body(pos_ref, *refs):
        chip = pos_ref[0]
        for own in range(4):
            @pl.when(chip == own)
            def _():
                for p_ref, s_ref, o_ref in zip(refs[:n], refs[n:2 * n], refs[2 * n:]):
                    acc = None
                    for k in range(4):
                        v = (p_ref[...] if k == own else s_ref[_FLIP_SLOT[own ^ k]]).astype(F32)
                        acc = v if acc is None else acc + v
                    o_ref[...] = acc

    p_specs, s_specs, o_specs, shapes = [], [], [], []
    for it in its:
        shape = it.sized(shard=True, half=True)
        blk = (shape[0] // nb,) + shape[1:]
        rest = (0,) * (len(shape) - 1)

        def p_map(i, pos, it=it, nd=len(shape)):
            lead = i + (pos[0] * nb if it.shard_axis == 0 else 0)
            return (lead,) + tuple(pos[0] if ax == it.shard_axis else 0 for ax in range(1, nd))

        p_specs.append(pl.BlockSpec(blk, p_map))
        s_specs.append(pl.BlockSpec((3,) + blk, lambda i, pos, rest=rest: (0, i) + rest))
        o_specs.append(pl.BlockSpec(blk, lambda i, pos, rest=rest: (i,) + rest))
        shapes.append(_sds(shape, F32))
    return pl.pallas_call(
        body, grid_spec=pltpu.PrefetchScalarGridSpec(
            num_scalar_prefetch=1, grid=(nb,), in_specs=p_specs + s_specs, out_specs=o_specs),
        out_shape=shapes, name=name, compiler_params=_cparams("parallel"),
    )(pos, *pairs, *slots)


_GRAD_KEYS = ("pool_w_in", "pool_w_grp", "pool_w_out", "na_w_in", "na_w_out", "conv_w_in", "conv_w_out")


def _adamw_matrix(w, m, v, owns, others, it, pos, name):
    nl = w.shape[0]
    rows_split = it.half_axis == 0
    r, cdim = int(np.prod(w.shape[1:-1])), w.shape[-1]
    hr, hc = (r // 2, cdim) if rows_split else (r, cdim // 2)
    br = min(hr, 256)
    nb = hr // br
    c1 = 1.0 - ADAM_B1 ** ADAM_STEP
    c2 = 1.0 - ADAM_B2 ** ADAM_STEP

    def body(pos_ref, w_ref, m_ref, v_ref, *rest):
        own_refs, other_refs = rest[:nl], rest[nl:2 * nl]
        g_ref, d_ref, nm_ref, nv_ref = rest[2 * nl:]
        j, h = pl.program_id(0), pl.program_id(1)
        own, other = own_refs[0][...], other_refs[0][...]
        for q in range(1, nl):
            own = jnp.where(j == q, own_refs[q][...], own)
            other = jnp.where(j == q, other_refs[q][...], other)
        gv = jnp.where(h == pos_ref[1], own, other)
        nm = ADAM_B1 * m_ref[...] + (1.0 - ADAM_B1) * gv
        nv = ADAM_B2 * v_ref[...] + (1.0 - ADAM_B2) * (gv * gv)
        g_ref[...] = gv
        nm_ref[...] = nm
        nv_ref[...] = nv
        d_ref[...] = -ADAM_LR * ((nm / c1) / (jnp.sqrt(nv / c2) + ADAM_EPS) + ADAM_WD * w_ref[...])

    if rows_split:
        full = pl.BlockSpec((None, br, hc), lambda j, h, i, pos: (j, h * nb + i, 0))
    else:
        full = pl.BlockSpec((None, br, hc), lambda j, h, i, pos: (j, i, h))
    half = pl.BlockSpec((br, hc), lambda j, h, i, pos: (i, 0))
    flat = lambda t: t.reshape(nl, r, cdim)
    outs = pl.pallas_call(
        body, grid_spec=pltpu.PrefetchScalarGridSpec(
            num_scalar_prefetch=1, grid=(nl, 2, nb), in_specs=[full] * 3 + [half] * (2 * nl), out_specs=[full] * 4),
        out_shape=[_sds((nl, r, cdim), F32)] * 4, name=name,
        compiler_params=_cparams("parallel", "parallel", "parallel"),
    )(pos, flat(w), flat(m), flat(v), *[t.reshape(hr, hc) for t in list(owns) + list(others)])
    return tuple(t.reshape(w.shape) for t in outs)


_WEIGHTS = ("c_ctx", "norm_g", "ada_w", "ada_b", "pool_w_in", "pool_w_grp", "pool_scale", "pool_w_out", "na_w_in",
            "na_rpb", "na_w_out", "conv_w_in", "conv_dw", "conv_db", "conv_w_out", "final_g")
_COND_ROWS = 16


def _modulations(cond, ada_w, ada_b_cols):
    nl, d, n = ada_w.shape
    return _matmul(
        cond, ada_w, mode="nn", grid=(nl, 1), a_silu=True, epilogue="bias",
        a_spec=pl.BlockSpec((_COND_ROWS, d), lambda i, j: (0, 0)), b_spec=pl.BlockSpec((None, d, n), lambda i, j: (i, 0, 0)),
        extra=(ada_b_cols,), extra_specs=(pl.BlockSpec((None, 1, n), lambda i, j: (i, 0, 0)),),
        out_shapes=[_sds((nl, _COND_ROWS, n), F32)], out_specs=[pl.BlockSpec((None, _COND_ROWS, n), lambda i, j: (i, 0, 0))],
        name="modulations")[0]


def _ada_w_step(cond, dm_cols, w, m, v):
    nl, d, n = w.shape
    tr = d // 2
    c1 = 1.0 - ADAM_B1 ** ADAM_STEP
    c2 = 1.0 - ADAM_B2 ** ADAM_STEP

    def body(c_ref, dm_ref, w_ref, m_ref, v_ref, g_ref, d_ref, nm_ref, nv_ref):
        gv = lax.dot_general(_silu(c_ref[...]).astype(BF16), dm_ref[...].astype(BF16), _DIMS["tn"],
                             preferred_element_type=F32)
        nm = ADAM_B1 * m_ref[...] + (1.0 - ADAM_B1) * gv
        nv = ADAM_B2 * v_ref[...] + (1.0 - ADAM_B2) * (gv * gv)
        g_ref[...] = gv
        nm_ref[...] = nm
        nv_ref[...] = nv
        d_ref[...] = -ADAM_LR * ((nm / c1) / (jnp.sqrt(nv / c2) + ADAM_EPS) + ADAM_WD * w_ref[...])

    blk = pl.BlockSpec((None, tr, n), lambda l, i: (l, i, 0))
    return _call(
        body, (cond, dm_cols, w, m, v), grid=(nl, d // tr),
        in_specs=[pl.BlockSpec((_COND_ROWS, tr), lambda l, i: (0, i)),
                  pl.BlockSpec((None, _COND_ROWS, n), lambda l, i: (l, 0, 0)), blk, blk, blk],
        out_specs=[blk] * 4, out_shape=[_sds(w.shape, F32)] * 4, name="adamw_ada_w")


def _cond_grad(dm_cols, ada_w):
    nl, d, n = ada_w.shape
    return _matmul(
        dm_cols, ada_w, mode="nt", grid=(1, nl), nk=nl, acc_shape=(_COND_ROWS, d),
        a_spec=pl.BlockSpec((None, _COND_ROWS, n), lambda i, q: (q, 0, 0)), b_spec=pl.BlockSpec((None, d, n), lambda i, q: (q, 0, 0)),
        out_shapes=[_sds((_COND_ROWS, d), F32)], out_specs=[pl.BlockSpec((_COND_ROWS, d), lambda i, q: (0, 0))],
        name="cond_grad")[0]


def _pack(parts):
    flat = [p.reshape(-1) for p in parts]
    sizes = [f.shape[0] for f in flat]
    total = sum(sizes)
    rows = -(-total // 1024) * 8
    packed = jnp.concatenate(flat + [jnp.zeros((rows * 128 - total,), F32)]).reshape(rows, 128)
    offs = np.concatenate([[0], np.cumsum(sizes)])[:-1]
    return packed, [(int(o), p.shape) for o, p in zip(offs, parts)]


def _unpack(flat, layout, k):
    off, shape = layout[k]
    return flat[..., off:off + int(np.prod(shape))].reshape(flat.shape[:-1] + tuple(shape))


def kernel(x, c, ctx, c_ctx, norm_g, ada_w, ada_b, pool_w_in, pool_w_grp, pool_scale, pool_w_out, na_w_in, na_rpb, na_w_out, conv_w_in, conv_dw, conv_db, conv_w_out, final_g, loss_target, m_c_ctx, m_norm_g, m_ada_w, m_ada_b, m_pool_w_in, m_pool_w_grp, m_pool_scale, m_pool_w_out, m_na_w_in, m_na_rpb, m_na_w_out, m_conv_w_in, m_conv_dw, m_conv_db, m_conv_w_out, m_final_g, v_c_ctx, v_norm_g, v_ada_w, v_ada_b, v_pool_w_in, v_pool_w_grp, v_pool_scale, v_pool_w_out, v_na_w_in, v_na_rpb, v_na_w_out, v_conv_w_in, v_conv_dw, v_conv_db, v_conv_w_out, v_final_g):
    params = dict(c_ctx=c_ctx, norm_g=norm_g, ada_w=ada_w, ada_b=ada_b, pool_w_in=pool_w_in, pool_w_grp=pool_w_grp,
                  pool_scale=pool_scale, pool_w_out=pool_w_out, na_w_in=na_w_in, na_rpb=na_rpb, na_w_out=na_w_out,
                  conv_w_in=conv_w_in, conv_dw=conv_dw, conv_db=conv_db, conv_w_out=conv_w_out, final_g=final_g)
    mom1 = dict(c_ctx=m_c_ctx, norm_g=m_norm_g, ada_w=m_ada_w, ada_b=m_ada_b, pool_w_in=m_pool_w_in,
                pool_w_grp=m_pool_w_grp, pool_scale=m_pool_scale, pool_w_out=m_pool_w_out, na_w_in=m_na_w_in,
                na_rpb=m_na_rpb, na_w_out=m_na_w_out, conv_w_in=m_conv_w_in, conv_dw=m_conv_dw, conv_db=m_conv_db,
                conv_w_out=m_conv_w_out, final_g=m_final_g)
    mom2 = dict(c_ctx=v_c_ctx, norm_g=v_norm_g, ada_w=v_ada_w, ada_b=v_ada_b, pool_w_in=v_pool_w_in,
                pool_w_grp=v_pool_w_grp, pool_scale=v_pool_scale, pool_w_out=v_pool_w_out, na_w_in=v_na_w_in,
                na_rpb=v_na_rpb, na_w_out=v_na_w_out, conv_w_in=v_conv_w_in, conv_dw=v_conv_dw, conv_db=v_conv_db,
                conv_w_out=v_conv_w_out, final_g=v_final_g)
    d = x.shape[-1]
    w = na_w_out.shape[1] * 4
    xi, yi, ci = _position()
    chip = 2 * xi + yi
    dev = 2 * chip + ci
    n_ada = ada_w.shape[-1]

    def chip_cols(a, size):
        return lax.dynamic_slice_in_dim(a, chip * size, size, axis=a.ndim - 1)

    items = _items(d, w)
    first = [it for it in items if it.key.startswith("pool") and it.layer == 0]
    na = [it for it in items if it.key.startswith("na")]
    late = [it for it in items if it not in first + na]
    shards_of = lambda its: [params[it.key][it.layer].astype(BF16) for it in its]
    empties = lambda its: [lax.empty(it.shape, BF16) for it in its]
    first_copies, na_copies = _gather_ici_copies(first), _gather_ici_copies(na)

    conds = _all_gather(c.reshape(8, d // 8), _AXES, "gather_cond").reshape(8, d)
    behind = conds[0, 0] * 0.0
    first_handle, token = _split_start(first_copies, [s + behind.astype(BF16) for s in shards_of(first)],
                                       empties(first), 3 * len(first), "gather_first_start")
    cond = jnp.concatenate([conds + token[0, 0], c_ctx[None], jnp.zeros((_COND_ROWS - 9, d), F32)], axis=0)
    mod_cols = _modulations(cond, ada_w, chip_cols(ada_b, n_ada)[:, None, :])
    small_pack, small_layout = _pack([pool_scale, conv_dw, conv_db])
    (mod_all,), (small,) = _run_comms([_all_gather_comm(mod_cols, ("x", "y")),
                                       _all_gather_comm(small_pack, ("x", "y"))], "gather_mod")
    behind = mod_all[0, 0, 0, 0] * 0.0
    na_handle, token = _split_start(na_copies, [s + behind.astype(BF16) for s in shards_of(na)], empties(na),
                                    3 * len(na), "gather_na_start")
    first_shards, first_mats = _split_wait(first_copies, first_handle, token, "gather_first_wait")
    first_mats = _gather_pair_finish(first_shards, first_mats, first, "gather_first_pair")
    mod_all = mod_all.transpose(1, 2, 0, 3).reshape(4, _COND_ROWS, 3, d)
    mod = jnp.stack([lax.dynamic_index_in_dim(mod_all, dev, axis=1, keepdims=False), mod_all[:, 8]], axis=1)
    full = {(it.key, it.layer): mat for it, mat in zip(first, first_mats)}
    late_comm = _gather_comm(shards_of(late), late)

    def na_weights(after):
        na_shards, na_mats = _split_wait(na_copies, na_handle, after, "gather_na_wait")
        na_mats = _gather_pair_finish(na_shards, na_mats, na, "gather_na_pair")
        return {it.key: mat for it, mat in zip(na, na_mats)}

    def late_weights(mats):
        full.update({(it.key, it.layer): mat for it, mat in zip(late, mats)})
        return dict(pool_w_in=[full[("pool_w_in", j)] for j in range(2)],
                    pool_w_grp=[full[("pool_w_grp", j)] for j in range(2)],
                    pool_w_out=[full[("pool_w_out", j)] for j in range(2)],
                    conv_w_in=full[("conv_w_in", 0)], conv_w_out=full[("conv_w_out", 0)])

    small = small.reshape(4, -1)

    def whole(k):
        parts = _unpack(small, small_layout, k)
        return jnp.moveaxis(parts, 0, -2).reshape(parts.shape[1:-1] + (-1,))

    wts = dict(pool_w_in=[full[("pool_w_in", 0)]], pool_w_grp=[full[("pool_w_grp", 0)]],
               pool_w_out=[full[("pool_w_out", 0)]], pool_scale=whole(0), na_rpb=na_rpb[0], conv_dw=whole(1)[0],
               conv_db=whole(2))
    pos = jnp.stack([chip, ci]).astype(jnp.int32)

    def layer_grads(its, by_layer):
        pick = {"pool_w_in": "w_in", "pool_w_grp": "w_grp", "pool_w_out": "w_out", "na_w_in": "w_in",
                "na_w_out": "w_out", "conv_w_in": "w_in", "conv_w_out": "w_out"}
        return [by_layer[(it.key.split("_")[0], it.layer)][pick[it.key]] for it in its]

    pairs, handles = dict(), dict()
    half_windows = lambda its: [(lambda ref, half, it=it: it.window(ref, half=half)) for it in its]
    half_shapes = lambda its: [_sds(it.sized(half=True), BF16) for it in its]

    def pair_sums(its, mats, tag):
        got = _pair_swap(mats, half_windows(its), half_shapes(its), f"pair_exchange_{tag}")
        return _pair_sums(mats, got, its, pos, f"pair_sum_{tag}")

    def grad_comm(gr3, gr2):
        pairs["late"] = pair_sums(late, layer_grads(late, {("pool", 1): gr3, ("conv", 0): gr2}), "late")
        return _chip_exchange_comm(pairs["late"], late)

    slot_zones = lambda its: [lax.empty((3,) + it.sized(shard=True, half=True), BF16) for it in its]
    na_xcopies, first_xcopies = _chip_exchange_copies(na), _chip_exchange_copies(first)

    def na_grads_start(gr1):
        pairs["na"] = pair_sums(na, layer_grads(na, {("na", 0): gr1}), "na")
        handles["na"], started = _split_start(na_xcopies, pairs["na"], slot_zones(na), 3 * len(na),
                                              "exchange_na_start")
        return started

    res = _example_step(x[0], ctx[0], loss_target[0], mod, norm_g, final_g[None], wts, dict(
        na_weights=na_weights, late_comm=late_comm, late_weights=late_weights, grad_comm=grad_comm,
        na_grads_start=na_grads_start))
    g0, g1, g2, g3 = res["layers"]
    pairs["na"], na_slots = _split_wait(na_xcopies, handles["na"], g0["w_in"], "exchange_na_wait")
    first_grads = layer_grads(first, {("pool", 0): g0})
    packed, layout = _pack([res["dfinal_g"], res["dnorm_g"], res["dmod"], g1["rpb"],
                            jnp.concatenate([g0["scale"], g3["scale"]], axis=0), g2["dw"], g2["db"],
                            res["loss"][0, :1]])
    first_got, (every,) = _run_comms([_pair_swap_comm(first_grads, half_windows(first), half_shapes(first)),
                                      _all_gather_two_level_comm(packed)], "pair_exchange_first")
    pairs["first"] = _pair_sums(first_grads, first_got, first, pos, "pair_sum_first")

    grads = dict()
    total = _sum_lead(every, "sum_vec_grads").reshape(-1)
    every = every.reshape(8, -1)
    grads["final_g"] = _unpack(total, layout, 0).reshape(final_g.shape)
    grads["norm_g"] = _unpack(total, layout, 1)
    grads["na_rpb"] = _unpack(total, layout, 3)[None]
    grads["pool_scale"] = chip_cols(_unpack(total, layout, 4), pool_scale.shape[-1])
    grads["conv_dw"] = chip_cols(_unpack(total, layout, 5), conv_dw.shape[-1])[None]
    grads["conv_db"] = chip_cols(_unpack(total, layout, 6), conv_db.shape[-1])
    dmod_sum = _unpack(total, layout, 2).reshape(4, 2, 3 * d)
    dmod_each = _unpack(every, layout, 2).reshape(8, 4, 2, 3 * d)
    grads["ada_b"] = dmod_sum[:, 0] + dmod_sum[:, 1]
    dm = jnp.concatenate([dmod_each[:, :, 0].transpose(1, 0, 2), dmod_sum[:, 1][:, None],
                          jnp.zeros((4, _COND_ROWS - 9, 3 * d), F32)], axis=1)
    dm_cols = chip_cols(dm, n_ada)
    dcond = _cond_grad(dm_cols, ada_w)[8].reshape(8, d // 8)
    dcond_all = _all_gather(dcond, ("x", "y"), "gather_cond_grad")
    behind = dcond_all[0, 0, 0] * 0.0
    handles["first"], token = _split_start(first_xcopies, [p + behind.astype(BF16) for p in pairs["first"]],
                                           slot_zones(first), 3 * len(first), "exchange_first_start")
    grads["ada_w"], *ada_w_step = _ada_w_step(cond, dm_cols + token[0, 0], ada_w, m_ada_w, v_ada_w)
    grads["c_ctx"] = _sum_lead(dcond_all, "sum_cond_grad").reshape(d) * _dsilu(c_ctx)
    vector_out = {k: _adamw(params[k], grads[k], mom1[k], mom2[k], f"adamw_{k}")
                  for k in _WEIGHTS if k not in _GRAD_KEYS + ("ada_w",)}
    vector_out["ada_w"] = tuple(ada_w_step)
    pairs["first"], first_slots = _split_wait(first_xcopies, handles["first"], vector_out["ada_w"][2],
                                              "exchange_first_wait")

    slots = dict(zip(late, res["carried"]))
    slots.update(zip(first, first_slots))
    slots.update(zip(na, na_slots))
    pair_of = dict(zip(late, pairs["late"]))
    pair_of.update(zip(first, pairs["first"]))
    pair_of.update(zip(na, pairs["na"]))
    reduced = _chip_sums([pair_of[it] for it in items], [slots[it] for it in items], items, pos, "chip_sum")
    theirs = _pair_swap(reduced, [lambda ref, half: ref] * len(items),
                        [_sds(t.shape, F32) for t in reduced], "pair_return")
    matrix_out = dict()
    for k in _GRAD_KEYS:
        idx = [i for i, it in enumerate(items) if it.key == k]
        res_k = _adamw_matrix(params[k], mom1[k], mom2[k], [reduced[i] for i in idx], [theirs[i] for i in idx],
                              items[idx[0]], pos, f"adamw_{k}")
        grads[k], matrix_out[k] = res_k[0], res_k[1:]

    outs = [[], [], []]
    for k in _WEIGHTS:
        step = matrix_out[k] if k in matrix_out else vector_out[k]
        for lst, val in zip(outs, step):
            lst.append(val)
    loss = _unpack(total, layout, 7)[0]
    return (loss, res["grad_x"][None], *[grads[k].reshape(params[k].shape) for k in _WEIGHTS],
            *outs[0], *outs[1], *outs[2])
```

```python
import functools

import numpy as np
import jax
import jax.numpy as jnp
from jax import lax
from jax.experimental import pallas as pl
from jax.experimental.pallas import tpu as pltpu

F32 = jnp.float32
BF16 = jnp.bfloat16

EPS = 1e-6
GRID_W = 64
HEAD_DIM = 64
WIN_ROWS = 8
WIN_COLS = 16
POOL_WINDOWS = (2, 4, 8, 16)
Q_ROWS = 4
K_ROWS = 12
PAD_ROWS = 4
NEG = -1e30

ADAM_LR = 0.001
ADAM_B1 = 0.9
ADAM_B2 = 0.999
ADAM_EPS = 1e-08
ADAM_WD = 0.01
ADAM_STEP = 10

ROW_BLOCK = 256
VMEM_LIMIT = 56 * 1024 * 1024
ACT = BF16

MESH = pl.DeviceIdType.MESH
HBM_SPEC = pl.BlockSpec(memory_space=pltpu.HBM)


def _cparams(*sem):
    return pltpu.CompilerParams(dimension_semantics=sem or None, vmem_limit_bytes=VMEM_LIMIT)


def _sds(shape, dtype):
    return jax.ShapeDtypeStruct(tuple(shape), dtype)


def _call(body, args, *, grid, in_specs, out_specs, out_shape, name, scratch_shapes=()):
    return list(pl.pallas_call(
        body, grid=grid, in_specs=list(in_specs), out_specs=list(out_specs), out_shape=list(out_shape),
        scratch_shapes=list(scratch_shapes), name=name, compiler_params=_cparams(*(("arbitrary",) * len(grid))),
    )(*args))


def _sigmoid(x):
    return 1.0 / (1.0 + jnp.exp(-x))


def _silu(x):
    return x * _sigmoid(x)


def _dsilu(x):
    s = _sigmoid(x)
    return s * (1.0 + x * (1.0 - s))


_DIMS = {
    "nn": (((1,), (0,)), ((), ())),
    "nt": (((1,), (1,)), ((), ())),
    "tn": (((0,), (0,)), ((), ())),
}


def _matmul(a, b, *, mode, grid, a_spec, b_spec, out_shapes, out_specs, name, nk=1,
            a_silu=False, exact=False, epilogue=None, extra=(), extra_specs=(), acc_shape=None):
    n_extra = len(extra)
    n_out = len(out_shapes)

    def body(*refs):
        a_ref, b_ref = refs[:2]
        ex = refs[2:2 + n_extra]
        outs = refs[2 + n_extra:2 + n_extra + n_out]
        av = a_ref[...]
        bv = b_ref[...]
        if a_silu:
            av = _silu(av.astype(F32))
        if exact:
            prod = lax.dot_general(av.astype(F32), bv.astype(F32), _DIMS[mode],
                                   precision=lax.Precision.HIGHEST, preferred_element_type=F32)
        else:
            prod = lax.dot_general(av.astype(BF16), bv.astype(BF16), _DIMS[mode], preferred_element_type=F32)

        def finish(res):
            if epilogue == "bias":
                res = res + ex[0][...]
            outs[0][...] = res.astype(outs[0].dtype)

        if nk == 1:
            finish(prod)
        else:
            acc = refs[-1]
            k = pl.program_id(len(grid) - 1)

            @pl.when(k == 0)
            def _():
                acc[...] = prod

            @pl.when(k > 0)
            def _():
                acc[...] += prod

            @pl.when(k == nk - 1)
            def _():
                finish(acc[...])

    scratch = [pltpu.VMEM(acc_shape, F32)] if nk > 1 else []
    sem = ("parallel",) * (len(grid) - 1) + ("arbitrary",)
    return pl.pallas_call(
        body, grid=grid, in_specs=[a_spec, b_spec, *extra_specs], out_specs=list(out_specs),
        out_shape=list(out_shapes), scratch_shapes=scratch, name=name, compiler_params=_cparams(*sem),
    )(a, b, *extra)


def _row_tile(rows):
    for t in (768, 512, 256):
        if rows % t == 0:
            return t
    return rows


def _mm_tn(a, b, name, out_dtype, tm=512):
    r, m = a.shape
    n = b.shape[1]
    tm = min(tm, m)
    tn = min(1024, n)
    return _matmul(
        a, b, mode="tn", grid=(m // tm, n // tn),
        a_spec=pl.BlockSpec((r, tm), lambda i, j: (0, i)), b_spec=pl.BlockSpec((r, tn), lambda i, j: (0, j)),
        out_shapes=[_sds((m, n), out_dtype)], out_specs=[pl.BlockSpec((tm, tn), lambda i, j: (i, j))], name=name)[0]


def _mm_tn_parts(a, b, name, out_dtype, tm=512):
    r, m = a.shape
    p, _, np_ = b.shape
    tm = min(tm, m)
    return _matmul(
        a, b, mode="tn", grid=(m // tm, p),
        a_spec=pl.BlockSpec((r, tm), lambda i, q: (0, i)), b_spec=pl.BlockSpec((None, r, np_), lambda i, q: (q, 0, 0)),
        out_shapes=[_sds((m, p * np_), out_dtype)], out_specs=[pl.BlockSpec((tm, np_), lambda i, q: (i, q))],
        name=name)[0]


def _row_vec(ref, is_ctx):
    return ref[0] if is_ctx is None else jnp.where(is_ctx, ref[1], ref[0])


def _ctx_rows(i, tm, nx, nseg):
    if nseg == 1:
        return None
    return i * tm + lax.broadcasted_iota(jnp.int32, (tm, 1), 0) >= nx


def _seg_sums(ref, val, is_ctx, first):
    if is_ctx is None:
        parts = [jnp.sum(val, axis=0, keepdims=True)]
    else:
        parts = [jnp.sum(jnp.where(is_ctx, 0.0, val), axis=0, keepdims=True),
                 jnp.sum(jnp.where(is_ctx, val, 0.0), axis=0, keepdims=True)]

    @pl.when(first)
    def _():
        for k, p in enumerate(parts):
            ref[k] = p

    @pl.when(jnp.logical_not(first))
    def _():
        for k, p in enumerate(parts):
            ref[k] += p


def _w_out_resid(a, w_out, xres, gate, nx, name):
    m, k = a.shape
    n = w_out.shape[1]
    nseg = gate.shape[0]
    tm = _row_tile(m)

    def body(a_ref, w_ref, x_ref, gt_ref, yx_ref, xo_ref):
        yx = jnp.dot(a_ref[...], w_ref[...], preferred_element_type=F32)
        yx_ref[...] = yx.astype(ACT)
        xo_ref[...] = x_ref[...] + _row_vec(gt_ref, _ctx_rows(pl.program_id(0), tm, nx, nseg)) * yx

    row = pl.BlockSpec((tm, n), lambda i: (i, 0))
    return pl.pallas_call(
        body, grid=(m // tm,),
        in_specs=[pl.BlockSpec((tm, k), lambda i: (i, 0)), pl.BlockSpec((k, n), lambda i: (0, 0)), row,
                  pl.BlockSpec((nseg, 1, n), lambda i: (0, 0, 0))],
        out_specs=[row, row], out_shape=[_sds((m, n), ACT), _sds((m, n), F32)],
        name=name, compiler_params=_cparams("parallel"),
    )(a, w_out, xres, gate)


def _norm_w_in(x, g, scale, shift, w_in, nx, name, ctx=None):
    d = x.shape[1]
    rows = x.shape[0] + (0 if ctx is None else ctx.shape[0])
    n = w_in.shape[1]
    nseg = scale.shape[0]
    tm = _row_tile(rows)
    tn = min(1024, n)
    row = pl.BlockSpec((tm, d), lambda i, j: (i, 0))
    if ctx is None:
        row_args, row_specs = (x,), [row]
    else:
        assert ctx.shape[0] == ROW_BLOCK and tm % ROW_BLOCK == 0 and nx % ROW_BLOCK == 0
        nsub, x_blocks = tm // ROW_BLOCK, nx // ROW_BLOCK
        row_args = (x,) * nsub + (ctx,)
        row_specs = [pl.BlockSpec((ROW_BLOCK, d), lambda i, j, s=s: (jnp.minimum(i * nsub + s, x_blocks - 1), 0))
                     for s in range(nsub)] + [pl.BlockSpec((ROW_BLOCK, d), lambda i, j: (0, 0))]

    def body(*refs):
        x_refs, (g_ref, sc_ref, sh_ref, w_ref), outs = refs[:len(row_args)], refs[len(row_args):][:4], refs[-3:]
        h_ref, r_ref, p_ref = outs
        i, j = pl.program_id(0), pl.program_id(1)

        @pl.when(j == 0)
        def _():
            if ctx is None:
                xv = x_refs[0][...]
            else:
                xv = jnp.concatenate([jnp.where(i * nsub + s >= x_blocks, x_refs[-1][...], x_refs[s][...])
                                      for s in range(nsub)], axis=0)
                refs[-4][...] = xv
            r = lax.rsqrt(jnp.mean(xv * xv, axis=-1, keepdims=True) + EPS)
            is_ctx = _ctx_rows(i, tm, nx, nseg)
            h = (xv * r) * g_ref[...] * (1.0 + _row_vec(sc_ref, is_ctx)) + _row_vec(sh_ref, is_ctx)
            h_ref[...] = h.astype(BF16)
            r_ref[...] = r

        p_ref[...] = jnp.dot(h_ref[...], w_ref[...], preferred_element_type=F32).astype(ACT)

    vec = pl.BlockSpec((nseg, 1, d), lambda i, j: (0, 0, 0))
    joined = [] if ctx is None else [(row, _sds((rows, d), F32))]
    out_specs, out_shape = zip(*joined, (row, _sds((rows, d), BF16)),
                               (pl.BlockSpec((tm, 1), lambda i, j: (i, 0)), _sds((rows, 1), F32)),
                               (pl.BlockSpec((tm, tn), lambda i, j: (i, j)), _sds((rows, n), ACT)))
    return _call(
        body, (*row_args, g, scale, shift, w_in), grid=(rows // tm, n // tn),
        in_specs=[*row_specs, pl.BlockSpec((1, d), lambda i, j: (0, 0)), vec, vec,
                  pl.BlockSpec((d, tn), lambda i, j: (0, j))],
        out_specs=list(out_specs), out_shape=list(out_shape), name=name)


def _gate_w_out_bwd(dxo, yx, gate, w_out, nx, name):
    rows, d = yx.shape
    w = w_out.shape[0]
    nseg = gate.shape[0]
    tm = _row_tile(rows)

    def body(dx_ref, yx_ref, gt_ref, w_ref, dyx_ref, da_ref, dg_ref):
        i = pl.program_id(0)
        is_ctx = _ctx_rows(i, tm, nx, nseg)
        dxv = dx_ref[...]
        dyx = (dxv * _row_vec(gt_ref, is_ctx)).astype(BF16)
        dyx_ref[...] = dyx
        da_ref[...] = lax.dot_general(dyx, w_ref[...], _DIMS["nt"], preferred_element_type=F32).astype(ACT)
        _seg_sums(dg_ref, dxv * yx_ref[...].astype(F32), is_ctx, i == 0)

    row = pl.BlockSpec((tm, d), lambda i: (i, 0))
    vec = pl.BlockSpec((nseg, 1, d), lambda i: (0, 0, 0))
    return _call(
        body, (dxo, yx, gate, w_out), grid=(rows // tm,),
        in_specs=[row, row, vec, pl.BlockSpec((w, d), lambda i: (0, 0))],
        out_specs=[row, pl.BlockSpec((tm, w), lambda i: (i, 0)), vec],
        out_shape=[_sds((rows, d), BF16), _sds((rows, w), ACT), _sds((nseg, 1, d), F32)], name=name)


def _w_in_bwd_norm(dparts, w_in, x, r, g, scale, dres, nx, name, dx_rows=None):
    np_, rows, kp = dparts.shape
    d = w_in.shape[0]
    nseg = scale.shape[0]
    tm = _row_tile(rows)
    assert dx_rows is None or rows - tm < dx_rows <= rows
    nsub = tm // ROW_BLOCK
    nres_blocks = dres.shape[0] // ROW_BLOCK

    def body(dp_ref, w_ref, x_ref, r_ref, g_ref, sc_ref, *rest):
        dres_refs = rest[:nsub]
        dx_ref, dsh_ref, dge_ref, acc = rest[nsub:]
        i, k = pl.program_id(0), pl.program_id(1)
        prod = lax.dot_general(dp_ref[...], w_ref[...], _DIMS["nt"], preferred_element_type=F32)

        @pl.when(k == 0)
        def _():
            acc[...] = prod

        @pl.when(k > 0)
        def _():
            acc[...] += prod

        @pl.when(k == np_ - 1)
        def _():
            is_ctx = _ctx_rows(i, tm, nx, nseg)
            dhv = acc[...]
            rv = r_ref[...]
            xn = x_ref[...] * rv
            dxn = dhv * (g_ref[...] * (1.0 + _row_vec(sc_ref, is_ctx)))
            dx = rv * (dxn - xn * jnp.mean(dxn * xn, axis=-1, keepdims=True))
            for s in range(nsub):
                piece = slice(s * ROW_BLOCK, (s + 1) * ROW_BLOCK)
                res = dres_refs[s][...]
                if nres_blocks * ROW_BLOCK < rows:
                    res = jnp.where(i * nsub + s < nres_blocks, res, 0.0)
                dx_ref[piece, :] = dx[piece, :] + res
            _seg_sums(dsh_ref, dhv, is_ctx, i == 0)
            _seg_sums(dge_ref, dhv * xn, is_ctx, i == 0)

    row = pl.BlockSpec((tm, d), lambda i, k: (i, 0))
    vec = pl.BlockSpec((nseg, 1, d), lambda i, k: (0, 0, 0))
    return _call(
        body, (dparts, w_in, x, r, g, scale, *([dres] * nsub)), grid=(rows // tm, np_),
        in_specs=[pl.BlockSpec((None, tm, kp), lambda i, k: (k, i, 0)), pl.BlockSpec((d, kp), lambda i, k: (0, k)),
                  row, pl.BlockSpec((tm, 1), lambda i, k: (i, 0)), pl.BlockSpec((1, d), lambda i, k: (0, 0)), vec]
        + [pl.BlockSpec((ROW_BLOCK, d), (lambda i, k, s=s: (jnp.minimum(i * nsub + s, nres_blocks - 1), 0)))
           for s in range(nsub)],
        out_specs=[row, vec, vec],
        out_shape=[_sds((dx_rows or rows, d), F32), _sds((nseg, 1, d), F32), _sds((nseg, 1, d), F32)],
        scratch_shapes=[pltpu.VMEM((tm, d), F32)], name=name)


_PAD_TOP = 16
_PAD_BOT = 32


def _window_sum(buf, xv, lo, n):
    t = xv.shape[0]
    c = xv.shape[1]
    tp = t + _PAD_TOP + _PAD_BOT
    buf[pl.ds(0, _PAD_TOP), :] = jnp.zeros((_PAD_TOP, c), F32)
    buf[pl.ds(_PAD_TOP, t), :] = xv
    buf[pl.ds(_PAD_TOP + t, _PAD_BOT), :] = jnp.zeros((_PAD_BOT, c), F32)
    p = buf[...]
    k = 1
    while k < n:
        p = p + pltpu.roll(p, tp - k, 0)
        k *= 2
    if lo:
        p = pltpu.roll(p, -lo, 0)
    buf[...] = p
    return buf[pl.ds(_PAD_TOP, t), :]


def _window_count(t, half):
    pos = lax.broadcasted_iota(jnp.int32, (t, 1), 0)
    return (jnp.minimum(pos + half, t) - jnp.maximum(pos - half, 0)).astype(F32)


def _segments(rows, nx):
    return [(0, nx)] + ([(nx, rows - nx)] if rows > nx else [])


def _pool_scratch(rows, nx, cols):
    return [pltpu.VMEM((length + _PAD_TOP + _PAD_BOT, cols), F32) for _, length in _segments(rows, nx)]


def _per_group(g, fn):
    for gi, win in enumerate(POOL_WINDOWS):
        pl.when(g == gi)(functools.partial(fn, win))


def _pool_grp_fwd(uv, w_grp, scale, nx, name):
    rows = uv.shape[0]
    ng, gc, _ = w_grp.shape
    w = ng * gc
    segs = _segments(rows, nx)

    def body(u_ref, gt_ref, w_ref, sc_ref, z_ref, mx_ref, a_ref, *bufs):
        def pool(win):
            half = win // 2
            for (start, length), buf in zip(segs, bufs):
                uvv = u_ref[pl.ds(start, length), :].astype(F32)
                s = _window_sum(buf, uvv, -half, win)
                z_ref[pl.ds(start, length), :] = (s / _window_count(length, half) - uvv).astype(BF16)

        _per_group(pl.program_id(0), pool)
        mixed = jnp.dot(z_ref[...], w_ref[...], preferred_element_type=F32)
        mx_ref[...] = mixed.astype(ACT)
        a_ref[...] = (mixed * sc_ref[...] * _silu(gt_ref[...].astype(F32))).astype(BF16)

    col = pl.BlockSpec((rows, gc), lambda g: (0, g))
    return _call(
        body, (uv, uv, w_grp, scale), grid=(ng,),
        in_specs=[col, pl.BlockSpec((rows, gc), lambda g: (0, ng + g)), pl.BlockSpec((None, gc, gc), lambda g: (g, 0, 0)),
                  pl.BlockSpec((1, gc), lambda g: (0, g))],
        out_specs=[col, col, col], out_shape=[_sds((rows, w), BF16), _sds((rows, w), ACT), _sds((rows, w), BF16)],
        scratch_shapes=_pool_scratch(rows, nx, gc), name=name)


def _pool_grp_bwd(da, mixed, uv, scale, w_grp, nx, name):
    rows, w = da.shape
    ng, gc, _ = w_grp.shape
    segs = _segments(rows, nx)

    def body(da_ref, mx_ref, gt_ref, sc_ref, w_ref, dm_ref, duv_ref, dsc_ref, dz_ref, *bufs):
        dav = da_ref[...].astype(F32)
        mixed = mx_ref[...].astype(F32)
        gt = gt_ref[...].astype(F32)
        sg = _silu(gt)
        sc = sc_ref[...]
        dm = (dav * sc * sg).astype(BF16)
        dm_ref[...] = dm
        dz_ref[...] = lax.dot_general(dm, w_ref[...], _DIMS["nt"], preferred_element_type=F32)
        duv_ref[1] = (dav * mixed * sc * _dsilu(gt)).astype(BF16)
        dsc_ref[...] = jnp.sum(dav * mixed * sg, axis=0, keepdims=True)

        def unpool(win):
            half = win // 2
            for (start, length), buf in zip(segs, bufs):
                dzv = dz_ref[pl.ds(start, length), :]
                s = _window_sum(buf, dzv / _window_count(length, half), 1 - half, win)
                duv_ref[0, pl.ds(start, length), :] = (s - dzv).astype(BF16)

        _per_group(pl.program_id(0), unpool)

    col = pl.BlockSpec((rows, gc), lambda g: (0, g))
    vec = pl.BlockSpec((1, gc), lambda g: (0, g))
    return pl.pallas_call(
        body, grid=(ng,),
        in_specs=[col, col, pl.BlockSpec((rows, gc), lambda g: (0, ng + g)), vec,
                  pl.BlockSpec((None, gc, gc), lambda g: (g, 0, 0))],
        out_specs=[col, pl.BlockSpec((2, rows, gc), lambda g: (0, 0, g)), vec],
        out_shape=[_sds((rows, w), BF16), _sds((2, rows, w), BF16), _sds((1, w), F32)],
        scratch_shapes=[pltpu.VMEM((rows, gc), F32)] + _pool_scratch(rows, nx, gc),
        name=name, compiler_params=_cparams("parallel"),
    )(da, mixed, uv, scale, w_grp)


def _grp_wgrad(z, dm, ng, name, out_dtype):
    rows, w = z.shape
    gc = w // ng

    def body(z_ref, dm_ref, o_ref):
        o_ref[...] = lax.dot_general(z_ref[...], dm_ref[...], _DIMS["tn"],
                                     preferred_element_type=F32).astype(o_ref.dtype)

    blk = pl.BlockSpec((rows, gc), lambda g: (0, g))
    return pl.pallas_call(
        body, grid=(ng,), in_specs=[blk, blk], out_specs=pl.BlockSpec((None, gc, gc), lambda g: (g, 0, 0)),
        out_shape=_sds((ng, gc, gc), out_dtype), name=name, compiler_params=_cparams("parallel"),
    )(z, dm)


def _shift_rows(v, by):
    t = v.shape[0]
    pos = lax.broadcasted_iota(jnp.int32, v.shape, 0)
    rolled = pltpu.roll(v, by % t, 0)
    keep = pos >= by if by > 0 else pos < t + by
    return jnp.where(keep, rolled, 0.0)


def _conv_specs(t, w, cb):
    return [pl.BlockSpec((t, cb), (lambda j, q=q: (0, q * (w // cb) + j))) for q in range(4)]


def _conv_fwd(p4, dw, db, name):
    t = p4.shape[0]
    w = p4.shape[1] // 4
    cb = 128

    def body(bg_ref, cg_ref, v_ref, g_ref, dw_ref, db_ref, a_ref):
        tv = cg_ref[...].astype(F32) * v_ref[...].astype(F32)
        conv = (dw_ref[0:1, :] * _shift_rows(tv, 1) + dw_ref[1:2, :] * tv + dw_ref[2:3, :] * _shift_rows(tv, -1)
                + db_ref[...])
        a_ref[...] = (bg_ref[...].astype(F32) * conv * _silu(g_ref[...].astype(F32))).astype(BF16)

    return pl.pallas_call(
        body, grid=(w // cb,),
        in_specs=_conv_specs(t, w, cb) + [pl.BlockSpec((3, cb), lambda j: (0, j)), pl.BlockSpec((1, cb), lambda j: (0, j))],
        out_specs=pl.BlockSpec((t, cb), lambda j: (0, j)), out_shape=_sds((t, w), BF16),
        name=name, compiler_params=_cparams("parallel"),
    )(p4, p4, p4, p4, dw, db)


def _conv_bwd(da, p4, dw, db, name):
    t, w = da.shape
    cb = 128

    def body(da_ref, bg_ref, cg_ref, v_ref, g_ref, dw_ref, db_ref, d4_ref, ddw_ref, ddb_ref):
        cg = cg_ref[...].astype(F32)
        vv = v_ref[...].astype(F32)
        bg = bg_ref[...].astype(F32)
        gv = g_ref[...].astype(F32)
        tv = cg * vv
        tm1 = _shift_rows(tv, 1)
        tp1 = _shift_rows(tv, -1)
        w0, w1, w2 = dw_ref[0:1, :], dw_ref[1:2, :], dw_ref[2:3, :]
        conv = w0 * tm1 + w1 * tv + w2 * tp1 + db_ref[...]
        y = bg * conv
        dav = da_ref[...].astype(F32)
        dy = dav * _silu(gv)
        d4_ref[3] = (dav * y * _dsilu(gv)).astype(BF16)
        d4_ref[0] = (dy * conv).astype(BF16)
        dconv = dy * bg
        ddb_ref[...] = jnp.sum(dconv, axis=0, keepdims=True)
        ddw_ref[0:1, :] = jnp.sum(dconv * tm1, axis=0, keepdims=True)
        ddw_ref[1:2, :] = jnp.sum(dconv * tv, axis=0, keepdims=True)
        ddw_ref[2:3, :] = jnp.sum(dconv * tp1, axis=0, keepdims=True)
        dt = w0 * _shift_rows(dconv, -1) + w1 * dconv + w2 * _shift_rows(dconv, 1)
        d4_ref[1] = (dt * vv).astype(BF16)
        d4_ref[2] = (dt * cg).astype(BF16)

    col = pl.BlockSpec((t, cb), lambda j: (0, j))
    tap = pl.BlockSpec((3, cb), lambda j: (0, j))
    bias = pl.BlockSpec((1, cb), lambda j: (0, j))
    return pl.pallas_call(
        body, grid=(w // cb,), in_specs=[col] + _conv_specs(t, w, cb) + [tap, bias],
        out_specs=[pl.BlockSpec((4, t, cb), lambda j: (0, 0, j)), tap, bias],
        out_shape=[_sds((4, t, w), BF16), _sds((3, w), F32), _sds((1, w), F32)],
        name=name, compiler_params=_cparams("parallel"),
    )(da, p4, p4, p4, p4, dw, db)


def _attn_mask():
    qn, kn = Q_ROWS * GRID_W, K_ROWS * GRID_W
    qr, qc = np.divmod(np.arange(qn), GRID_W)
    kr, kc = np.divmod(np.arange(kn), GRID_W)
    col0 = np.clip(qc - WIN_COLS // 2, 0, GRID_W - WIN_COLS)
    col_ok = (kc[None, :] >= col0[:, None]) & (kc[None, :] < col0[:, None] + WIN_COLS)
    first = np.zeros(qn, np.int64)
    last = np.full(qn, K_ROWS - WIN_ROWS)
    out = []
    for row0 in (first, qr, last):
        row_ok = (kr[None, :] >= row0[:, None]) & (kr[None, :] < row0[:, None] + WIN_ROWS)
        out.append(np.where(row_ok & col_ok, 0.0, NEG))
    return jnp.asarray(np.stack(out), F32)


_KW = K_ROWS * GRID_W
_QB = Q_ROWS * GRID_W
_PAIR = 2 * HEAD_DIM
_N_DR = 2 * WIN_ROWS - 1
_N_DC = 2 * WIN_COLS - 1
_RP_ROWS = 24
_N_TILES = _N_DR + 1
_BIAS_BASE = (WIN_ROWS - 1, WIN_ROWS // 2 - 1, -1)


class _Comm:
    def __init__(self, ins, outs, sems, start, finish):
        self.ins, self.outs, self.sems, self.start, self.finish = list(ins), list(outs), list(sems), start, finish


def _bias_pieces(cls):
    out = []
    for qr in range(Q_ROWS):
        for kr in range(0, K_ROWS, 2):
            tile = _BIAS_BASE[cls] - qr + kr + 1
            out.append((qr, kr, tile if 0 <= tile < _N_TILES else None))
    return out


def _toeplitz_pair(left_row, right_row):
    lane = lax.broadcasted_iota(jnp.int32, (GRID_W, _PAIR), 1)
    shape = (GRID_W, _PAIR)
    left = pltpu.roll(jnp.broadcast_to(left_row, shape), _PAIR - (WIN_COLS - 1), 1, stride=1, stride_axis=0)
    right = pltpu.roll(jnp.broadcast_to(right_row, shape), GRID_W - (WIN_COLS - 1), 1, stride=1, stride_axis=0)
    return jnp.where(lane < GRID_W, left, right)


def _build_tiles(tiles_ref, rp_ref):
    for h in range(2):
        for t in range(_N_TILES):
            tiles_ref[h, t] = _toeplitz_pair(rp_ref[h, t:t + 1, :], rp_ref[h, t + 1:t + 2, :])


def _block_class(b, nblk, fn, entering=False):
    interior = (b == 1) if entering else jnp.logical_and(b > 0, b < nblk - 1)
    for cls, cond in enumerate((b == 0, interior, b == nblk - 1)):
        pl.when(cond)(functools.partial(fn, cls))


def _attn_geometry(p4, nx):
    rows = p4.shape[0]
    w = p4.shape[1] // 4
    nhp = w // _PAIR
    nblk = nx // _QB
    qspec = lambda col: pl.BlockSpec((_QB, _PAIR), lambda hp, b: (b, col * nhp + hp))
    kspec = lambda col: pl.BlockSpec((rows, _PAIR), lambda hp, b: (0, col * nhp + hp))
    tspec = pl.BlockSpec((2, _RP_ROWS, _PAIR), lambda hp, b: (hp, 0, 0))
    mspec = pl.BlockSpec((None, _QB, _KW), lambda hp, b: (jnp.where(b == 0, 0, jnp.where(b == nblk - 1, 2, 1)), 0, 0))
    lspec = pl.BlockSpec((None, _QB, 2), lambda hp, b: (hp, b, 0))
    ospec = pl.BlockSpec((_QB, _PAIR), lambda hp, b: (b, hp))
    return rows, w, nhp, nblk, qspec, kspec, tspec, mspec, lspec, ospec


def _window_start(b, nx):
    return pl.multiple_of(jnp.clip(b * _QB - PAD_ROWS * GRID_W, 0, nx - _KW), _QB)


def _load_bias(bias_ref, tiles_ref, rp_ref, m_ref, b, nblk):
    pl.when(b == 0)(lambda: _build_tiles(tiles_ref, rp_ref))

    def fill(cls):
        for h in range(2):
            for qr, kr, tile in _bias_pieces(cls):
                rows = slice(qr * GRID_W, (qr + 1) * GRID_W)
                cols = slice(kr * GRID_W, (kr + 2) * GRID_W)
                m = m_ref[rows, cols]
                bias_ref[h, rows, cols] = m if tile is None else tiles_ref[h, tile] + m

    _block_class(b, nblk, fill, entering=True)


def _attn_fwd(p4, rp, mask, nx, name, comm=None):
    rows, w, nhp, nblk, qspec, kspec, tspec, mspec, lspec, ospec = _attn_geometry(p4, nx)
    n_ctx = rows - nx
    n_cin, n_cout = (len(comm.ins), len(comm.outs)) if comm else (0, 0)

    def body(*refs):
        q_ref, k_ref, v_ref, g_ref, rp_ref, m_ref = refs[:6]
        cin = refs[6:6 + n_cin]
        a_ref, o_ref, lse_ref = refs[6 + n_cin:9 + n_cin]
        cout = refs[9 + n_cin:9 + n_cin + n_cout]
        bias_ref, tiles_ref = refs[9 + n_cin + n_cout:11 + n_cin + n_cout]
        sems = refs[11 + n_cin + n_cout:]
        hp, b = pl.program_id(0), pl.program_id(1)
        if comm:
            pl.when(jnp.logical_and(hp == 0, b == 0))(lambda: comm.start(cin, cout, sems))
        start = _window_start(b, nx)
        _load_bias(bias_ref, tiles_ref, rp_ref, m_ref, b, nblk)
        qf = q_ref[...].astype(F32) * HEAD_DIM ** -0.5
        kw = k_ref[pl.ds(start, _KW), :].astype(BF16)
        vw = v_ref[pl.ds(start, _KW), :].astype(BF16)
        kcv = k_ref[pl.ds(nx, n_ctx), :].astype(BF16)
        vcv = v_ref[pl.ds(nx, n_ctx), :].astype(BF16)
        lane = lax.broadcasted_iota(jnp.int32, (1, _PAIR), 1)
        outs, lses = [], []
        for h in range(2):
            mine = (lane >= HEAD_DIM) if h else (lane < HEAD_DIM)
            qm = jnp.where(mine, qf, 0.0).astype(BF16)
            s_loc = lax.dot_general(qm, kw, _DIMS["nt"], preferred_element_type=F32) + bias_ref[h]
            s_ctx = lax.dot_general(qm, kcv, _DIMS["nt"], preferred_element_type=F32)
            mx = jnp.maximum(jnp.max(s_loc, axis=-1, keepdims=True), jnp.max(s_ctx, axis=-1, keepdims=True))
            p_loc = jnp.exp(s_loc - mx)
            p_ctx = jnp.exp(s_ctx - mx)
            den = jnp.sum(p_loc, axis=-1, keepdims=True) + jnp.sum(p_ctx, axis=-1, keepdims=True)
            o = jnp.dot(p_loc.astype(BF16), vw, preferred_element_type=F32)
            o = o + jnp.dot(p_ctx.astype(BF16), vcv, preferred_element_type=F32)
            outs.append(o * (1.0 / den))
            lses.append(mx + jnp.log(den))
        o = jnp.where(lane < HEAD_DIM, outs[0], outs[1])
        o_ref[...] = o.astype(ACT)
        a_ref[...] = (o * _silu(g_ref[...].astype(F32))).astype(BF16)
        col = lax.broadcasted_iota(jnp.int32, (1, 2), 1)
        lse_ref[...] = jnp.where(col == 0, lses[0], lses[1])
        if comm:
            pl.when(jnp.logical_and(hp == nhp - 1, b == nblk - 1))(lambda: comm.finish(cin, cout, sems))

    res = pl.pallas_call(
        body, grid=(nhp, nblk),
        in_specs=[qspec(0), kspec(1), kspec(2), qspec(3), tspec, mspec] + [HBM_SPEC] * n_cin,
        out_specs=[ospec, ospec, lspec] + [HBM_SPEC] * n_cout,
        out_shape=[_sds((nx, w), BF16), _sds((nx, w), ACT), _sds((nhp, nx, 2), F32)] + (comm.outs if comm else []),
        scratch_shapes=[pltpu.VMEM((2, _QB, _KW), F32), pltpu.VMEM((2, _N_TILES, GRID_W, _PAIR), F32)]
        + (comm.sems if comm else []),
        name=name, compiler_params=_cparams("arbitrary", "arbitrary"),
    )(p4, p4, p4, p4, rp, mask, *(comm.ins if comm else []))
    return res[:3], res[3:]


def _fold_tiles(dtiles_ref, drp_ref):
    shape = (GRID_W, _PAIR)
    lane = lax.broadcasted_iota(jnp.int32, shape, 1)
    flip = (lax.broadcasted_iota(jnp.int32, (_PAIR, _PAIR), 0)
            + lax.broadcasted_iota(jnp.int32, (_PAIR, _PAIR), 1) == _PAIR - 1).astype(F32)
    drp_ref[...] = jnp.zeros(drp_ref.shape, F32)
    for h in range(2):
        stack = dtiles_ref[h].reshape(_N_TILES * GRID_W, _PAIR)
        rev = jnp.dot(stack, flip, precision=lax.Precision.HIGHEST, preferred_element_type=F32)
        for t in range(_N_TILES):
            tile = rev[t * GRID_W:(t + 1) * GRID_W, :]
            for side in (0, 1):
                shift = _PAIR - GRID_W * side - (WIN_COLS - 1)
                half = jnp.where((lane < GRID_W) if side else (lane >= GRID_W), tile, 0.0)
                diag = pltpu.roll(half, shift, 1, stride=1, stride_axis=0)
                drp_ref[h, t + side:t + side + 1, :] += jnp.sum(diag, axis=0, keepdims=True)


def _attn_bwd(p4, rp, mask, o, lse, da, nx, name, comm=None):
    rows, w, nhp, nblk, qspec, kspec, tspec, mspec, lspec, ospec = _attn_geometry(p4, nx)
    n_ctx = rows - nx
    n_cin, n_cout = (len(comm.ins), len(comm.outs)) if comm else (0, 0)

    def body(*refs):
        q_ref, k_ref, v_ref, g_ref, rp_ref, m_ref, o_ref, lse_ref, da_ref = refs[:9]
        cin = refs[9:9 + n_cin]
        d4_ref, drp_ref = refs[9 + n_cin:11 + n_cin]
        cout = refs[11 + n_cin:11 + n_cin + n_cout]
        bias_ref, tiles_ref, ds_ref, dtiles_ref, dk_ref, dv_ref = refs[11 + n_cin + n_cout:17 + n_cin + n_cout]
        sems = refs[17 + n_cin + n_cout:]
        hp, b = pl.program_id(0), pl.program_id(1)
        if comm:
            pl.when(jnp.logical_and(hp == 0, b == 0))(lambda: comm.start(cin, cout, sems))
        start = _window_start(b, nx)
        here = pl.multiple_of(b * _QB, _QB)

        @pl.when(b == 0)
        def _():
            dk_ref[...] = jnp.zeros(dk_ref.shape, F32)
            dv_ref[...] = jnp.zeros(dv_ref.shape, F32)
            dtiles_ref[...] = jnp.zeros(dtiles_ref.shape, F32)
            d4_ref[0, pl.ds(nx, n_ctx), :] = jnp.zeros((n_ctx, _PAIR), BF16)
            d4_ref[3, pl.ds(nx, n_ctx), :] = jnp.zeros((n_ctx, _PAIR), BF16)

        _load_bias(bias_ref, tiles_ref, rp_ref, m_ref, b, nblk)
        gv = g_ref[...].astype(F32)
        dav = da_ref[...].astype(F32)
        ov = o_ref[...].astype(F32)
        dov = dav * _silu(gv)
        d4_ref[3, pl.ds(here, _QB), :] = (dav * ov * _dsilu(gv)).astype(BF16)
        qf = q_ref[...].astype(F32) * HEAD_DIM ** -0.5
        kw = k_ref[pl.ds(start, _KW), :].astype(BF16)
        vw = v_ref[pl.ds(start, _KW), :].astype(BF16)
        kcv = k_ref[pl.ds(nx, n_ctx), :].astype(BF16)
        vcv = v_ref[pl.ds(nx, n_ctx), :].astype(BF16)
        lane = lax.broadcasted_iota(jnp.int32, (1, _PAIR), 1)
        dq = jnp.zeros((_QB, _PAIR), F32)
        for h in range(2):
            mine = (lane >= HEAD_DIM) if h else (lane < HEAD_DIM)
            qm = jnp.where(mine, qf, 0.0).astype(BF16)
            dom = jnp.where(mine, dov, 0.0)
            dob = dom.astype(BF16)
            lse = lse_ref[:, h:h + 1]
            s_loc = lax.dot_general(qm, kw, _DIMS["nt"], preferred_element_type=F32)
            p_loc = jnp.exp(s_loc + bias_ref[h] - lse)
            p_ctx = jnp.exp(lax.dot_general(qm, kcv, _DIMS["nt"], preferred_element_type=F32) - lse)
            delta = jnp.sum(dom * ov, axis=-1, keepdims=True)
            ds_loc = p_loc * (lax.dot_general(dob, vw, _DIMS["nt"], preferred_element_type=F32) - delta)
            ds_ctx = p_ctx * (lax.dot_general(dob, vcv, _DIMS["nt"], preferred_element_type=F32) - delta)
            dsb_loc = ds_loc.astype(BF16)
            dsb_ctx = ds_ctx.astype(BF16)
            dq_h = (jnp.dot(dsb_loc, kw, preferred_element_type=F32)
                    + jnp.dot(dsb_ctx, kcv, preferred_element_type=F32))
            dq = dq + jnp.where(mine, dq_h, 0.0)
            dk_ref[pl.ds(start, _KW), :] += lax.dot_general(dsb_loc, qm, _DIMS["tn"], preferred_element_type=F32)
            dv_ref[pl.ds(start, _KW), :] += lax.dot_general(p_loc.astype(BF16), dob, _DIMS["tn"],
                                                            preferred_element_type=F32)
            dk_ref[pl.ds(nx, n_ctx), :] += lax.dot_general(dsb_ctx, qm, _DIMS["tn"], preferred_element_type=F32)
            dv_ref[pl.ds(nx, n_ctx), :] += lax.dot_general(p_ctx.astype(BF16), dob, _DIMS["tn"],
                                                           preferred_element_type=F32)
            ds_ref[h] = ds_loc
        d4_ref[0, pl.ds(here, _QB), :] = (dq * HEAD_DIM ** -0.5).astype(BF16)

        def scatter(cls):
            for h in range(2):
                for qr, kr, tile in _bias_pieces(cls):
                    if tile is not None:
                        dtiles_ref[h, tile] += ds_ref[h, qr * GRID_W:(qr + 1) * GRID_W, kr * GRID_W:(kr + 2) * GRID_W]

        _block_class(b, nblk, scatter)

        @pl.when(b == nblk - 1)
        def _():
            d4_ref[1] = dk_ref[...].astype(BF16)
            d4_ref[2] = dv_ref[...].astype(BF16)
            _fold_tiles(dtiles_ref, drp_ref)

        if comm:
            pl.when(jnp.logical_and(hp == nhp - 1, b == nblk - 1))(lambda: comm.finish(cin, cout, sems))

    tiles = pltpu.VMEM((2, _N_TILES, GRID_W, _PAIR), F32)
    block = pltpu.VMEM((2, _QB, _KW), F32)
    res = pl.pallas_call(
        body, grid=(nhp, nblk),
        in_specs=[qspec(0), kspec(1), kspec(2), qspec(3), tspec, mspec, ospec, lspec, ospec] + [HBM_SPEC] * n_cin,
        out_specs=[pl.BlockSpec((4, rows, _PAIR), lambda hp, b: (0, 0, hp)), tspec] + [HBM_SPEC] * n_cout,
        out_shape=[_sds((4, rows, w), BF16), _sds(rp.shape, F32)] + (comm.outs if comm else []),
        scratch_shapes=[block, tiles, block, tiles, pltpu.VMEM((rows, _PAIR), F32), pltpu.VMEM((rows, _PAIR), F32)]
        + (comm.sems if comm else []),
        name=name, compiler_params=_cparams("arbitrary", "arbitrary"),
    )(p4, p4, p4, p4, rp, mask, o, lse, da, *(comm.ins if comm else []))
    return res[:2], res[2:]


def _w_out_loss(a, w_out, xres, gate, g, target, name):
    m, k = a.shape
    d = w_out.shape[1]
    assert gate.shape[0] == 1
    tm = _row_tile(m)
    nblk = m // tm

    def body(a_ref, w_ref, x_ref, gt_ref, g_ref, t_ref, yx_ref, loss_ref, dx_ref, dg_ref, acc_ref):
        i = pl.program_id(0)
        yx = jnp.dot(a_ref[...], w_ref[...], preferred_element_type=F32)
        yx_ref[...] = yx.astype(ACT)
        xv = x_ref[...] + gt_ref[0] * yx
        gv = g_ref[...]
        r = lax.rsqrt(jnp.mean(xv * xv, axis=-1, keepdims=True) + EPS)
        xn = xv * r
        err = xn * gv - t_ref[...]
        dy = err * (1.0 / d)
        dxn = dy * gv
        dx_ref[...] = r * (dxn - xn * jnp.mean(dxn * xn, axis=-1, keepdims=True))
        s_g = jnp.sum(dy * xn, axis=0, keepdims=True)
        s_l = jnp.sum(jnp.mean(err * err, axis=-1, keepdims=True), axis=0, keepdims=True)

        @pl.when(i == 0)
        def _():
            dg_ref[...] = s_g
            acc_ref[...] = s_l

        @pl.when(i > 0)
        def _():
            dg_ref[...] += s_g
            acc_ref[...] += s_l

        @pl.when(i == nblk - 1)
        def _():
            loss_ref[...] = jnp.broadcast_to(0.5 * acc_ref[...], loss_ref.shape)

    row = pl.BlockSpec((tm, d), lambda i: (i, 0))
    vec = pl.BlockSpec((1, d), lambda i: (0, 0))
    return pl.pallas_call(
        body, grid=(nblk,),
        in_specs=[pl.BlockSpec((tm, k), lambda i: (i, 0)), pl.BlockSpec((k, d), lambda i: (0, 0)), row,
                  pl.BlockSpec((1, 1, d), lambda i: (0, 0, 0)), vec, row],
        out_specs=[row, pl.BlockSpec((1, 128), lambda i: (0, 0)), row, vec],
        out_shape=[_sds((m, d), ACT), _sds((1, 128), F32), _sds((m, d), F32), _sds((1, d), F32)],
        scratch_shapes=[pltpu.VMEM((1, 1), F32)], name=name, compiler_params=_cparams("arbitrary"),
    )(a, w_out, xres, gate, g, target)


def _as2d(a):
    if a.ndim == 1:
        return a.reshape(-1, 128) if a.shape[0] % 128 == 0 else a.reshape(1, -1)
    return a.reshape(-1, a.shape[-1])


def _adamw(w, g, m, v, name):
    shape = w.shape
    w2, g2, m2, v2 = (_as2d(t) for t in (w, g.reshape(shape), m, v))
    rows, cols = w2.shape
    tr = 512 if rows % 512 == 0 else rows
    c1 = 1.0 - ADAM_B1 ** ADAM_STEP
    c2 = 1.0 - ADAM_B2 ** ADAM_STEP

    def body(w_ref, g_ref, m_ref, v_ref, d_ref, nm_ref, nv_ref):
        gv = g_ref[...]
        nm = ADAM_B1 * m_ref[...] + (1.0 - ADAM_B1) * gv
        nv = ADAM_B2 * v_ref[...] + (1.0 - ADAM_B2) * (gv * gv)
        nm_ref[...] = nm
        nv_ref[...] = nv
        d_ref[...] = -ADAM_LR * ((nm / c1) / (jnp.sqrt(nv / c2) + ADAM_EPS) + ADAM_WD * w_ref[...])

    blk = pl.BlockSpec((tr, cols), lambda i: (i, 0))
    outs = _call(body, (w2, g2, m2, v2), grid=(rows // tr,), in_specs=[blk] * 4, out_specs=[blk] * 3,
                 out_shape=[_sds((rows, cols), F32)] * 3, name=name)
    return tuple(t.reshape(shape) for t in outs)


def _sum_lead(x, name, out_dtype=F32):
    n, rows, cols = x.shape
    tr = 512 if rows % 512 == 0 else rows

    def body(x_ref, o_ref):
        acc = x_ref[0].astype(F32)
        for k in range(1, n):
            acc = acc + x_ref[k].astype(F32)
        o_ref[...] = acc.astype(out_dtype)

    return pl.pallas_call(
        body, grid=(rows // tr,), in_specs=[pl.BlockSpec((n, tr, cols), lambda i: (0, i, 0))],
        out_specs=pl.BlockSpec((tr, cols), lambda i: (i, 0)), out_shape=_sds((rows, cols), out_dtype),
        name=name, compiler_params=_cparams("parallel"),
    )(x)


def _seg_vecs(mod_l, which, nseg):
    return mod_l[:nseg, which][:, None, :]


def _norm_grads(dshift, dgeff, dgate, g, scale):
    nseg, _, d = dshift.shape
    dmod = jnp.stack([dshift[:, 0], dgeff[:, 0] * g, dgate[:, 0]], axis=1)
    if nseg == 1:
        dmod = jnp.concatenate([dmod, jnp.zeros((1, 3, d), F32)], axis=0)
    dg = jnp.sum(dgeff[:, 0] * (1.0 + scale[:, 0]), axis=0)
    return dmod, dg


def _pool_layer(xin, g, mod_l, w_in, w_grp, w_out, pscale, nx, tag, ctx=None, head=None):
    nseg = 1 if ctx is None else 2
    shift, scale, gate = (_seg_vecs(mod_l, k, nseg) for k in range(3))
    *joined, h, r, uv = _norm_w_in(xin, g, scale, shift, w_in, nx, f"w_in_fwd_{tag}", ctx)
    if joined:
        xin, = joined
    z, mixed, a = _pool_grp_fwd(uv, w_grp, pscale, nx, f"pool_fwd_{tag}")
    if head is None:
        yx, xout = _w_out_resid(a, w_out, xin, gate, nx, f"w_out_fwd_{tag}")
    else:
        yx, *xout = _w_out_loss(a, w_out, xin, gate, *head, f"w_out_loss_{tag}")

    def backward(dxo, token=None):
        gate_b = gate if token is None else gate + token[0, 0]
        dyx, da, dgate = _gate_w_out_bwd(dxo, yx, gate_b, w_out, nx, f"w_out_bwd_{tag}")
        gw_out = _mm_tn(a, dyx, f"w_out_grad_{tag}", BF16)
        dm, duv, dscale = _pool_grp_bwd(da, mixed, uv, pscale, w_grp, nx, f"pool_bwd_{tag}")
        gw_grp = _grp_wgrad(z, dm, w_grp.shape[0], f"grp_grad_{tag}", BF16)
        gw_in = _mm_tn_parts(h, duv, f"w_in_grad_{tag}", BF16)
        dx, dshift, dgeff = _w_in_bwd_norm(duv, w_in, xin, r, g, scale, dxo, nx, f"w_in_bwd_{tag}",
                                           dx_rows=None if ctx is None else nx)
        dmod, dg = _norm_grads(dshift, dgeff, dgate, g[0], scale)
        return dx, dmod, dg, dict(w_in=gw_in, w_grp=gw_grp, w_out=gw_out, scale=dscale)

    return xout, backward


def _na_layer(xc, g, mod_l, w_in, rpb, w_out, nx, mask, comm=None):
    nh, n_dr, n_dc = rpb.shape
    shift, scale = _seg_vecs(mod_l, 0, 2), _seg_vecs(mod_l, 1, 2)
    gate = _seg_vecs(mod_l, 2, 1)
    h, r, p4 = _norm_w_in(xc, g, scale, shift, w_in, nx, "w_in_fwd_na")
    rp = jnp.pad(rpb, ((0, 0), (1, _RP_ROWS - 1 - n_dr), (0, _PAIR - n_dc)))
    (a, o, lse), carried = _attn_fwd(p4, rp, mask, nx, "attn_fwd", comm)
    yx, xout = _w_out_resid(a, w_out, xc, gate, nx, "w_out_fwd_na")

    def backward(dxo, comm=None):
        dyx, da, dgate = _gate_w_out_bwd(dxo, yx, gate, w_out, nx, "w_out_bwd_na")
        gw_out = _mm_tn(a, dyx, "w_out_grad_na", BF16)
        (d4, drp), carried_bwd = _attn_bwd(p4, rp, mask, o, lse, da, nx, "attn_bwd", comm)
        gw_in = _mm_tn_parts(h, d4, "w_in_grad_na", BF16)
        dx, dshift, dgeff = _w_in_bwd_norm(d4, w_in, xc, r, g, scale, dxo, nx, "w_in_bwd_na")
        dgate2 = jnp.concatenate([dgate, jnp.zeros_like(dgate)], axis=0)
        dmod, dg = _norm_grads(dshift, dgeff, dgate2, g[0], scale)
        drpb = drp[:, 1:1 + n_dr, ::-1][:, :, :n_dc]
        return dx, dmod, dg, dict(w_in=gw_in, w_out=gw_out, rpb=drpb), carried_bwd

    return xout, backward, carried


def _conv_layer(xin, g, mod_l, w_in, dw, db, w_out):
    shift, scale, gate = (_seg_vecs(mod_l, k, 1) for k in range(3))
    nx = xin.shape[0]
    h, r, p4 = _norm_w_in(xin, g, scale, shift, w_in, nx, "w_in_fwd_conv")
    a = _conv_fwd(p4, dw, db, "conv_fwd")
    yx, xout = _w_out_resid(a, w_out, xin, gate, nx, "w_out_fwd_conv")

    def backward(dxo):
        dyx, da, dgate = _gate_w_out_bwd(dxo, yx, gate, w_out, nx, "w_out_bwd_conv")
        gw_out = _mm_tn(a, dyx, "w_out_grad_conv", BF16)
        d4, ddw, ddb = _conv_bwd(da, p4, dw, db, "conv_bwd")
        gw_in = _mm_tn_parts(h, d4, "w_in_grad_conv", BF16)
        dx, dshift, dgeff = _w_in_bwd_norm(d4, w_in, xin, r, g, scale, dxo, nx, "w_in_bwd_conv")
        dmod, dg = _norm_grads(dshift, dgeff, dgate, g[0], scale)
        return dx, dmod, dg, dict(w_in=gw_in, w_out=gw_out, dw=ddw, db=ddb)

    return xout, backward


def _example_step(x, ctx, target, mod, norm_g, final_g, wts, hooks=None):
    hooks = hooks or {}
    na_weights, late_comm, late_weights = (hooks.get(k) for k in ("na_weights", "late_comm", "late_weights"))
    nx = x.shape[0]
    consts = _attn_mask()
    g_rows = [norm_g[i:i + 1] for i in range(4)]
    xc1, bwd0 = _pool_layer(x, g_rows[0], mod[0], wts["pool_w_in"][0], wts["pool_w_grp"][0],
                            wts["pool_w_out"][0], wts["pool_scale"][0:1], nx, "p0", ctx=ctx)
    if na_weights is not None:
        wts = {**wts, **na_weights(xc1)}
    x2, bwd1, carried = _na_layer(xc1, g_rows[1], mod[1], wts["na_w_in"], wts["na_rpb"], wts["na_w_out"], nx, consts,
                                  late_comm)
    if late_weights is not None:
        wts = {**wts, **late_weights(carried)}
    x3, bwd2 = _conv_layer(x2, g_rows[2], mod[2], wts["conv_w_in"], wts["conv_dw"], wts["conv_db"], wts["conv_w_out"])
    (loss, dx4, dfinal_g), bwd3 = _pool_layer(x3, g_rows[3], mod[3], wts["pool_w_in"][1], wts["pool_w_grp"][1],
                                              wts["pool_w_out"][1], wts["pool_scale"][1:2], nx, "p3",
                                              head=(final_g, target))
    call = lambda k, *args: hooks[k](*args) if k in hooks else None
    dx3, dmod3, dg3, gr3 = bwd3(dx4)
    dx2, dmod2, dg2, gr2 = bwd2(dx3)
    dxc1, dmod1, dg1, gr1, carried_bwd = bwd1(dx2, call("grad_comm", gr3, gr2))
    dx0, dmod0, dg0, gr0 = bwd0(dxc1, call("na_grads_start", gr1))
    return dict(
        loss=loss, grad_x=dx0, dmod=jnp.stack([dmod0, dmod1, dmod2, dmod3]),
        dnorm_g=jnp.stack([dg0, dg1, dg2, dg3]), dfinal_g=dfinal_g, layers=(gr0, gr1, gr2, gr3), carried=carried_bwd)


_AXES = ("x", "y", "c")
_CHIP_FLIPS = ((1, 0), (0, 1), (1, 1))


def _position():
    return tuple(lax.axis_index(a) for a in _AXES)


def _flipped(pos, flip):
    return tuple(1 - p if f else p for p, f in zip(pos, flip))


def _join_comms(comms):
    n_in = [len(c.ins) for c in comms]
    n_out = [len(c.outs) for c in comms]
    n_sem = [len(c.sems) for c in comms]

    def parts(ins, outs, sems):
        for k in range(len(comms)):
            a, b, s = sum(n_in[:k]), sum(n_out[:k]), sum(n_sem[:k])
            yield comms[k], (ins[a:a + n_in[k]], outs[b:b + n_out[k]], sems[s:s + n_sem[k]])

    def start(ins, outs, sems):
        for c, part in parts(ins, outs, sems):
            c.start(*part)

    def finish(ins, outs, sems):
        for c, part in parts(ins, outs, sems):
            c.finish(*part)

    joint = _Comm([a for c in comms for a in c.ins], [o for c in comms for o in c.outs],
                  [s for c in comms for s in c.sems], start, finish)
    return joint, lambda res: [list(res[sum(n_out[:k]):sum(n_out[:k + 1])]) for k in range(len(comms))]


def _run_comms(comms, name):
    joint, split = _join_comms(comms)

    def body(*refs):
        n_in, n_out = len(joint.ins), len(joint.outs)
        joint.start(refs[:n_in], refs[n_in:n_in + n_out], refs[n_in + n_out:])
        joint.finish(refs[:n_in], refs[n_in:n_in + n_out], refs[n_in + n_out:])

    res = pl.pallas_call(
        body, in_specs=[HBM_SPEC] * len(joint.ins), out_specs=[HBM_SPEC] * len(joint.outs), out_shape=joint.outs,
        scratch_shapes=joint.sems, name=name,
    )(*joint.ins)
    return split(res)


def _all_gather_comm(v, axes):
    flips = [f for f in np.ndindex(2, 2, 2) if any(f) and all(a in axes or not b for a, b in zip(_AXES, f))]
    n = len(flips) + 1

    def copies(ins, outs, sems):
        (v_ref,), (o_ref,), (send_sems, recv_sems, local_sem) = ins, outs, sems
        pos = _position()
        slot = 0
        for a, p in zip(_AXES, pos):
            if a in axes:
                slot = 2 * slot + p
        local = pltpu.make_async_copy(v_ref, o_ref.at[slot], local_sem)
        remote = [pltpu.make_async_remote_copy(v_ref, o_ref.at[slot], send_sems.at[k], recv_sems.at[k],
                                               device_id=_flipped(pos, flip), device_id_type=MESH)
                  for k, flip in enumerate(flips)]
        return [local] + remote

    def start(ins, outs, sems):
        for cp in copies(ins, outs, sems):
            cp.start()

    def finish(ins, outs, sems):
        for cp in copies(ins, outs, sems):
            cp.wait()

    sems = [pltpu.SemaphoreType.DMA((n - 1,)), pltpu.SemaphoreType.DMA((n - 1,)), pltpu.SemaphoreType.DMA(())]
    return _Comm([v], [_sds((n,) + v.shape, v.dtype)], sems, start, finish)


def _all_gather_two_level_comm(v):
    def copies(ins, outs, sems, onward):
        (v_ref,), (o_ref,), (send_sems, recv_sems, local_sem) = ins, outs, sems
        x, y, c = _position()
        sibling = (x, y, 1 - c)
        slot = lambda px, py, pc: o_ref.at[4 * px + 2 * py + pc]
        own = pltpu.make_async_copy(v_ref, slot(x, y, c), local_sem)
        first = [pltpu.make_async_remote_copy(v_ref, slot(x, y, c), send_sems.at[0], recv_sems.at[0],
                                              device_id=sibling, device_id_type=MESH)]
        fwd = []
        for k, flip in enumerate(_CHIP_FLIPS):
            px, py = _flipped((x, y), flip)
            first.append(pltpu.make_async_remote_copy(v_ref, slot(x, y, c), send_sems.at[1 + k], recv_sems.at[1 + k],
                                                      device_id=(px, py, c), device_id_type=MESH))
            if onward:
                fwd.append(pltpu.make_async_remote_copy(slot(px, py, c), slot(px, py, c), send_sems.at[4 + k],
                                                        recv_sems.at[4 + k], device_id=sibling, device_id_type=MESH))
        return own, first, fwd

    def start(ins, outs, sems):
        own, first, _ = copies(ins, outs, sems, False)
        for cp in [own] + first:
            cp.start()

    def finish(ins, outs, sems):
        own, first, fwd = copies(ins, outs, sems, True)
        for arrived, onward in zip(first[1:], fwd):
            arrived.wait_recv()
            onward.start()
        first[0].wait_recv()
        for cp in fwd:
            cp.wait_recv()
        for cp in first + fwd:
            cp.wait_send()
        own.wait()

    sems = [pltpu.SemaphoreType.DMA((7,)), pltpu.SemaphoreType.DMA((7,)), pltpu.SemaphoreType.DMA(())]
    return _Comm([v], [_sds((8,) + v.shape, v.dtype)], sems, start, finish)


def _all_gather(v, axes, name):
    return _run_comms([_all_gather_comm(v, axes)], name)[0][0]


class _Item:
    def __init__(self, key, layer, shape, shard_axis, half_axis):
        self.key, self.layer, self.shape = key, layer, tuple(shape)
        self.shard_axis, self.half_axis = shard_axis, half_axis
        self.shard = shape[shard_axis] // 4
        self.half = shape[half_axis] // 2

    def sized(self, shard=False, half=False):
        s = list(self.shape)
        if shard:
            s[self.shard_axis] = self.shard
        if half:
            s[self.half_axis] = self.half
        return tuple(s)

    def window(self, ref, chip=None, half=None):
        idx = [slice(None)] * len(self.shape)
        if chip is not None:
            idx[self.shard_axis] = pl.ds(chip * self.shard, self.shard)
        if half is not None:
            idx[self.half_axis] = pl.ds(half * self.half, self.half)
        return ref.at[tuple(idx)]


def _items(d, w):
    out = []
    for j in range(2):
        out += [_Item("pool_w_in", j, (d, 2 * w), 1, 0), _Item("pool_w_grp", j, (4, w // 4, w // 4), 1, 0),
                _Item("pool_w_out", j, (w, d), 0, 1)]
    out += [_Item("na_w_in", 0, (d, 4 * w), 1, 0), _Item("na_w_out", 0, (w, d), 0, 1),
            _Item("conv_w_in", 0, (d, 4 * w), 1, 0), _Item("conv_w_out", 0, (w, d), 0, 1)]
    return out


def _gather_comm(shards, items):
    n = len(items)

    def copies(src, dst, sems, onward):
        send_a, recv_a, send_b, recv_b, send_c, recv_c = sems
        x, y, c = _position()
        chip = 2 * x + y
        sibling = (x, y, 1 - c)
        own, out, fwd, fwd_in = [], [], [], []
        for i, it in enumerate(items):
            own.append(pltpu.make_async_remote_copy(src[i], it.window(dst[i], chip=chip), send_c.at[i], recv_c.at[i],
                                                    device_id=sibling, device_id_type=MESH))
            for k, flip in enumerate(_CHIP_FLIPS):
                px, py = _flipped((x, y), flip)
                s = 3 * i + k
                out.append(pltpu.make_async_remote_copy(
                    it.window(src[i], half=c), it.window(dst[i], chip=chip, half=c), send_a.at[s], recv_a.at[s],
                    device_id=(px, py, c), device_id_type=MESH))
                if onward:
                    got = it.window(dst[i], chip=2 * px + py, half=c)
                    fwd.append(pltpu.make_async_remote_copy(got, got, send_b.at[s], recv_b.at[s],
                                                            device_id=sibling, device_id_type=MESH))
                    other = it.window(dst[i], chip=2 * px + py, half=1 - c)
                    fwd_in.append(pltpu.make_async_remote_copy(other, other, send_b.at[s], recv_b.at[s],
                                                               device_id=sibling, device_id_type=MESH))
        return own, out, fwd, fwd_in

    def start(src, dst, sems):
        own, out, _, _ = copies(src, dst, sems, False)
        for cp in own + out:
            cp.start()

    def finish(src, dst, sems):
        own, out, fwd, fwd_in = copies(src, dst, sems, True)
        for arrived, onward in zip(out, fwd):
            arrived.wait_recv()
            onward.start()
        for cp in fwd_in:
            cp.wait_recv()
        for cp in out + fwd:
            cp.wait_send()
        for cp in own:
            cp.wait()

    sems = [pltpu.SemaphoreType.DMA((3 * n,)) for _ in range(4)] + [pltpu.SemaphoreType.DMA((n,)) for _ in range(2)]
    return _Comm(shards, [_sds(it.shape, BF16) for it in items], sems, start, finish)


def _pair_swap_copies(windows):
    def copies(src, got, sems):
        send_sems, recv_sems = sems
        x, y, c = _position()
        return [pltpu.make_async_remote_copy(windows[i](src[i], 1 - c), got[i], send_sems.at[i], recv_sems.at[i],
                                             device_id=(x, y, 1 - c), device_id_type=MESH)
                for i in range(len(windows))]

    return copies


def _pair_swap_comm(arrays, windows, out_shapes):
    n = len(arrays)
    copies = _pair_swap_copies(windows)

    def start(src, got, sems):
        for cp in copies(src, got, sems):
            cp.start()

    def finish(src, got, sems):
        for cp in copies(src, got, sems):
            cp.wait()

    return _Comm(arrays, out_shapes, [pltpu.SemaphoreType.DMA((n,)), pltpu.SemaphoreType.DMA((n,))], start, finish)


def _pair_swap(arrays, windows, out_shapes, name):
    return _run_comms([_pair_swap_comm(arrays, windows, out_shapes)], name)[0]


def _chip_exchange_copies(items):
    def copies(src, dst, sems):
        send_sems, recv_sems = sems
        x, y, c = _position()
        out = []
        for i, it in enumerate(items):
            for k, flip in enumerate(_CHIP_FLIPS):
                px, py = _flipped((x, y), flip)
                out.append(pltpu.make_async_remote_copy(
                    it.window(src[i], chip=2 * px + py), dst[i].at[k], send_sems.at[3 * i + k],
                    recv_sems.at[3 * i + k], device_id=(px, py, c), device_id_type=MESH))
        return out

    return copies


_SEM_SPEC = pl.BlockSpec(memory_space=pltpu.SEMAPHORE)
_DATAFLOW = pltpu.SideEffectType.DATAFLOW_SIDE_EFFECTING


def _split_start(copies, srcs, zones, n_copies, name):
    n, nz = len(srcs), len(zones)

    def body(*refs):
        src, land = refs[:n], refs[n:n + nz]
        send_sems, recv_sems = refs[n + nz:n + nz + 2]
        token = refs[-1]
        for cp in copies(src, land, (send_sems, recv_sems)):
            cp.start()
        token[...] = jnp.zeros(token.shape, F32)

    hbm = lambda t: pltpu.HBM(t.shape, t.dtype)
    res = pl.pallas_call(
        body, name=name,
        out_shape=(pltpu.SemaphoreType.DMA((n_copies,)), pltpu.SemaphoreType.DMA((n_copies,)),
                   *[hbm(t) for t in list(srcs) + list(zones)], _sds((8, 128), F32)),
        in_specs=[HBM_SPEC] * (n + nz),
        out_specs=(_SEM_SPEC, _SEM_SPEC, *[HBM_SPEC] * (n + nz), pl.BlockSpec(memory_space=pltpu.VMEM)),
        input_output_aliases={i: 2 + i for i in range(n + nz)},
        compiler_params=pltpu.CompilerParams(has_side_effects=_DATAFLOW),
    )(*[pltpu.with_memory_space_constraint(t, pltpu.HBM) for t in list(srcs) + list(zones)])
    return (res[0], res[1], list(res[2:2 + n]), list(res[2 + n:2 + n + nz])), res[-1]


def _split_wait(copies, handle, after, name):
    send_sems, recv_sems, srcs, zones = handle
    n, nz = len(srcs), len(zones)

    def body(*refs):
        src, land = refs[:n], refs[n:n + nz]
        send, recv = refs[n + nz:n + nz + 2]
        for cp in copies(src, land, (send, recv)):
            cp.wait_send()
            cp.wait_recv()

    hbm = lambda t: pltpu.HBM(t.shape, t.dtype)
    res = pl.pallas_call(
        body, name=name, out_shape=tuple(hbm(t) for t in list(srcs) + list(zones)),
        in_specs=[HBM_SPEC] * (n + nz) + [_SEM_SPEC, _SEM_SPEC, pl.BlockSpec(memory_space=pl.ANY)],
        out_specs=tuple([HBM_SPEC] * (n + nz)), input_output_aliases={i: i for i in range(n + nz)},
        compiler_params=pltpu.CompilerParams(has_side_effects=_DATAFLOW),
    )(*srcs, *zones, send_sems, recv_sems, after)
    return list(res[:n]), list(res[n:])


def _gather_ici_copies(items):
    def copies(src, dst, sems):
        send_sems, recv_sems = sems
        x, y, c = _position()
        chip = 2 * x + y
        out = []
        for i, it in enumerate(items):
            for k, flip in enumerate(_CHIP_FLIPS):
                px, py = _flipped((x, y), flip)
                out.append(pltpu.make_async_remote_copy(
                    it.window(src[i], half=c), it.window(dst[i], chip=chip, half=c), send_sems.at[3 * i + k],
                    recv_sems.at[3 * i + k], device_id=(px, py, c), device_id_type=MESH))
        return out

    return copies


def _gather_pair_finish(shards, mats, items, name):
    n = len(items)

    def body(*refs):
        src, dst = refs[:n], refs[2 * n:3 * n]
        send_own, recv_own, send_fwd, recv_fwd = refs[3 * n:]
        x, y, c = _position()
        chip = 2 * x + y
        sibling = (x, y, 1 - c)
        copies = []
        for i, it in enumerate(items):
            copies.append(pltpu.make_async_remote_copy(src[i], it.window(dst[i], chip=chip), send_own.at[i],
                                                       recv_own.at[i], device_id=sibling, device_id_type=MESH))
            for k, flip in enumerate(_CHIP_FLIPS):
                px, py = _flipped((x, y), flip)
                got = it.window(dst[i], chip=2 * px + py, half=c)
                copies.append(pltpu.make_async_remote_copy(got, got, send_fwd.at[3 * i + k], recv_fwd.at[3 * i + k],
                                                           device_id=sibling, device_id_type=MESH))
        for cp in copies:
            cp.start()
        for cp in copies:
            cp.wait()

    return pl.pallas_call(
        body, in_specs=[HBM_SPEC] * (2 * n), out_specs=[HBM_SPEC] * n, out_shape=[_sds(it.shape, BF16) for it in items],
        input_output_aliases={n + i: i for i in range(n)},
        scratch_shapes=[pltpu.SemaphoreType.DMA((n,)), pltpu.SemaphoreType.DMA((n,)),
                        pltpu.SemaphoreType.DMA((3 * n,)), pltpu.SemaphoreType.DMA((3 * n,))], name=name,
    )(*shards, *mats)


def _chip_exchange_comm(partials, items):
    n = len(items)
    copies = _chip_exchange_copies(items)

    def start(src, dst, sems):
        for cp in copies(src, dst, sems):
            cp.start()

    def finish(src, dst, sems):
        for cp in copies(src, dst, sems):
            cp.wait()

    return _Comm(partials, [_sds((3,) + it.sized(shard=True, half=True), BF16) for it in items],
                 [pltpu.SemaphoreType.DMA((3 * n,)), pltpu.SemaphoreType.DMA((3 * n,))], start, finish)


_SUM_STEPS = 2


def _pair_sums(gs, gots, its, pos, name):
    n = len(its)
    nb = _SUM_STEPS
    g2 = [g.reshape(-1, g.shape[-1]) for g in gs]
    got2 = [t.reshape(-1, t.shape[-1]) for t in gots]

    def body(pos_ref, *refs):
        for g_ref, got_ref, o_ref in zip(refs[:n], refs[n:2 * n], refs[2 * n:]):
            o_ref[...] = (g_ref[...].astype(F32) + got_ref[...].astype(F32)).astype(BF16)

    g_specs, got_specs = [], []
    for it, t in zip(its, got2):
        rows, cols = t.shape
        blk = (rows // nb, cols)
        g_map = (lambda i, pos: (pos[1] * nb + i, 0)) if it.half_axis == 0 else (lambda i, pos: (i, pos[1]))
        g_specs.append(pl.BlockSpec(blk, g_map))
        got_specs.append(pl.BlockSpec(blk, lambda i, pos: (i, 0)))
    outs = pl.pallas_call(
        body, grid_spec=pltpu.PrefetchScalarGridSpec(
            num_scalar_prefetch=1, grid=(nb,), in_specs=g_specs + got_specs, out_specs=got_specs),
        out_shape=[_sds(t.shape, BF16) for t in got2], name=name, compiler_params=_cparams("parallel"),
    )(pos, *g2, *got2)
    return [o.reshape(t.shape) for o, t in zip(outs, gots)]


_FLIP_SLOT = {2: 0, 1: 1, 3: 2}


def _chip_sums(pairs, slots, its, pos, name, after=None):
    n = len(its)
    nb = _SUM_STEPS
    behind = [] if after is None else [after]

    def body(pos_ref, *refs):
        chip = pos_ref[0]
        for own in range(4):
            @pl.when(chip == own)
            def _():
                for p_ref, s_ref, o_ref in zip(refs[:n], refs[n:2 * n], refs[-n:]):
                    acc = None
                    for k in range(4):
                        v = (p_ref[...] if k == own else s_ref[_FLIP_SLOT[own ^ k]]).astype(F32)
                        acc = v if acc is None else acc + v
                    o_ref[...] = acc

    p_specs, s_specs, o_specs, shapes = [], [], [], []
    for it in its:
        shape = it.sized(shard=True, half=True)
        blk = (shape[0] // nb,) + shape[1:]
        rest = (0,) * (len(shape) - 1)

        def p_map(i, pos, it=it, nd=len(shape)):
            lead = i + (pos[0] * nb if it.shard_axis == 0 else 0)
            return (lead,) + tuple(pos[0] if ax == it.shard_axis else 0 for ax in range(1, nd))

        p_specs.append(pl.BlockSpec(blk, p_map))
        s_specs.append(pl.BlockSpec((3,) + blk, lambda i, pos, rest=rest: (0, i) + rest))
        o_specs.append(pl.BlockSpec(blk, lambda i, pos, rest=rest: (i,) + rest))
        shapes.append(_sds(shape, F32))
    return pl.pallas_call(
        body, grid_spec=pltpu.PrefetchScalarGridSpec(
            num_scalar_prefetch=1, grid=(nb,), out_specs=o_specs,
            in_specs=p_specs + s_specs + [pl.BlockSpec(memory_space=pl.ANY)] * len(behind)),
        out_shape=shapes, name=name, compiler_params=_cparams("parallel"),
    )(pos, *pairs, *slots, *behind)


_GRAD_KEYS = ("pool_w_in", "pool_w_grp", "pool_w_out", "na_w_in", "na_w_out", "conv_w_in", "conv_w_out")


def _adamw_matrix(w, m, v, owns, others, it, pos, name):
    nl = w.shape[0]
    rows_split = it.half_axis == 0
    r, cdim = int(np.prod(w.shape[1:-1])), w.shape[-1]
    hr, hc = (r // 2, cdim) if rows_split else (r, cdim // 2)
    br = min(hr, 256)
    nb = hr // br
    c1 = 1.0 - ADAM_B1 ** ADAM_STEP
    c2 = 1.0 - ADAM_B2 ** ADAM_STEP

    def body(pos_ref, w_ref, m_ref, v_ref, *rest):
        own_refs, other_refs = rest[:nl], rest[nl:2 * nl]
        g_ref, d_ref, nm_ref, nv_ref = rest[2 * nl:]
        j, h = pl.program_id(0), pl.program_id(1)
        own, other = own_refs[0][...], other_refs[0][...]
        for q in range(1, nl):
            own = jnp.where(j == q, own_refs[q][...], own)
            other = jnp.where(j == q, other_refs[q][...], other)
        gv = jnp.where(h == pos_ref[1], own, other)
        nm = ADAM_B1 * m_ref[...] + (1.0 - ADAM_B1) * gv
        nv = ADAM_B2 * v_ref[...] + (1.0 - ADAM_B2) * (gv * gv)
        g_ref[...] = gv
        nm_ref[...] = nm
        nv_ref[...] = nv
        d_ref[...] = -ADAM_LR * ((nm / c1) / (jnp.sqrt(nv / c2) + ADAM_EPS) + ADAM_WD * w_ref[...])

    if rows_split:
        full = pl.BlockSpec((None, br, hc), lambda j, h, i, pos: (j, h * nb + i, 0))
    else:
        full = pl.BlockSpec((None, br, hc), lambda j, h, i, pos: (j, i, h))
    half = pl.BlockSpec((br, hc), lambda j, h, i, pos: (i, 0))
    flat = lambda t: t.reshape(nl, r, cdim)
    outs = pl.pallas_call(
        body, grid_spec=pltpu.PrefetchScalarGridSpec(
            num_scalar_prefetch=1, grid=(nl, 2, nb), in_specs=[full] * 3 + [half] * (2 * nl), out_specs=[full] * 4),
        out_shape=[_sds((nl, r, cdim), F32)] * 4, name=name,
        compiler_params=_cparams("parallel", "parallel", "parallel"),
    )(pos, flat(w), flat(m), flat(v), *[t.reshape(hr, hc) for t in list(owns) + list(others)])
    return tuple(t.reshape(w.shape) for t in outs)


_WEIGHTS = ("c_ctx", "norm_g", "ada_w", "ada_b", "pool_w_in", "pool_w_grp", "pool_scale", "pool_w_out", "na_w_in",
            "na_rpb", "na_w_out", "conv_w_in", "conv_dw", "conv_db", "conv_w_out", "final_g")
_COND_ROWS = 16


def _modulations(cond, ada_w, ada_b_cols):
    nl, d, n = ada_w.shape
    return _matmul(
        cond, ada_w, mode="nn", grid=(nl, 1), a_silu=True, epilogue="bias",
        a_spec=pl.BlockSpec((_COND_ROWS, d), lambda i, j: (0, 0)), b_spec=pl.BlockSpec((None, d, n), lambda i, j: (i, 0, 0)),
        extra=(ada_b_cols,), extra_specs=(pl.BlockSpec((None, 1, n), lambda i, j: (i, 0, 0)),),
        out_shapes=[_sds((nl, _COND_ROWS, n), F32)], out_specs=[pl.BlockSpec((None, _COND_ROWS, n), lambda i, j: (i, 0, 0))],
        name="modulations")[0]


def _ada_w_step(cond, dm_cols, w, m, v):
    nl, d, n = w.shape
    tr = d // 2
    c1 = 1.0 - ADAM_B1 ** ADAM_STEP
    c2 = 1.0 - ADAM_B2 ** ADAM_STEP

    def body(c_ref, dm_ref, w_ref, m_ref, v_ref, g_ref, d_ref, nm_ref, nv_ref):
        gv = lax.dot_general(_silu(c_ref[...]).astype(BF16), dm_ref[...].astype(BF16), _DIMS["tn"],
                             preferred_element_type=F32)
        nm = ADAM_B1 * m_ref[...] + (1.0 - ADAM_B1) * gv
        nv = ADAM_B2 * v_ref[...] + (1.0 - ADAM_B2) * (gv * gv)
        g_ref[...] = gv
        nm_ref[...] = nm
        nv_ref[...] = nv
        d_ref[...] = -ADAM_LR * ((nm / c1) / (jnp.sqrt(nv / c2) + ADAM_EPS) + ADAM_WD * w_ref[...])

    blk = pl.BlockSpec((None, tr, n), lambda l, i: (l, i, 0))
    return _call(
        body, (cond, dm_cols, w, m, v), grid=(nl, d // tr),
        in_specs=[pl.BlockSpec((_COND_ROWS, tr), lambda l, i: (0, i)),
                  pl.BlockSpec((None, _COND_ROWS, n), lambda l, i: (l, 0, 0)), blk, blk, blk],
        out_specs=[blk] * 4, out_shape=[_sds(w.shape, F32)] * 4, name="adamw_ada_w")


def _cond_grad(dm_cols, ada_w):
    nl, d, n = ada_w.shape
    return _matmul(
        dm_cols, ada_w, mode="nt", grid=(1, nl), nk=nl, acc_shape=(_COND_ROWS, d),
        a_spec=pl.BlockSpec((None, _COND_ROWS, n), lambda i, q: (q, 0, 0)), b_spec=pl.BlockSpec((None, d, n), lambda i, q: (q, 0, 0)),
        out_shapes=[_sds((_COND_ROWS, d), F32)], out_specs=[pl.BlockSpec((_COND_ROWS, d), lambda i, q: (0, 0))],
        name="cond_grad")[0]


def _pack(parts):
    flat = [p.reshape(-1) for p in parts]
    sizes = [f.shape[0] for f in flat]
    total = sum(sizes)
    rows = -(-total // 1024) * 8
    packed = jnp.concatenate(flat + [jnp.zeros((rows * 128 - total,), F32)]).reshape(rows, 128)
    offs = np.concatenate([[0], np.cumsum(sizes)])[:-1]
    return packed, [(int(o), p.shape) for o, p in zip(offs, parts)]


def _unpack(flat, layout, k):
    off, shape = layout[k]
    return flat[..., off:off + int(np.prod(shape))].reshape(flat.shape[:-1] + tuple(shape))


def kernel(x, c, ctx, c_ctx, norm_g, ada_w, ada_b, pool_w_in, pool_w_grp, pool_scale, pool_w_out, na_w_in, na_rpb, na_w_out, conv_w_in, conv_dw, conv_db, conv_w_out, final_g, loss_target, m_c_ctx, m_norm_g, m_ada_w, m_ada_b, m_pool_w_in, m_pool_w_grp, m_pool_scale, m_pool_w_out, m_na_w_in, m_na_rpb, m_na_w_out, m_conv_w_in, m_conv_dw, m_conv_db, m_conv_w_out, m_final_g, v_c_ctx, v_norm_g, v_ada_w, v_ada_b, v_pool_w_in, v_pool_w_grp, v_pool_scale, v_pool_w_out, v_na_w_in, v_na_rpb, v_na_w_out, v_conv_w_in, v_conv_dw, v_conv_db, v_conv_w_out, v_final_g):
    params = dict(c_ctx=c_ctx, norm_g=norm_g, ada_w=ada_w, ada_b=ada_b, pool_w_in=pool_w_in, pool_w_grp=pool_w_grp,
                  pool_scale=pool_scale, pool_w_out=pool_w_out, na_w_in=na_w_in, na_rpb=na_rpb, na_w_out=na_w_out,
                  conv_w_in=conv_w_in, conv_dw=conv_dw, conv_db=conv_db, conv_w_out=conv_w_out, final_g=final_g)
    mom1 = dict(c_ctx=m_c_ctx, norm_g=m_norm_g, ada_w=m_ada_w, ada_b=m_ada_b, pool_w_in=m_pool_w_in,
                pool_w_grp=m_pool_w_grp, pool_scale=m_pool_scale, pool_w_out=m_pool_w_out, na_w_in=m_na_w_in,
                na_rpb=m_na_rpb, na_w_out=m_na_w_out, conv_w_in=m_conv_w_in, conv_dw=m_conv_dw, conv_db=m_conv_db,
                conv_w_out=m_conv_w_out, final_g=m_final_g)
    mom2 = dict(c_ctx=v_c_ctx, norm_g=v_norm_g, ada_w=v_ada_w, ada_b=v_ada_b, pool_w_in=v_pool_w_in,
                pool_w_grp=v_pool_w_grp, pool_scale=v_pool_scale, pool_w_out=v_pool_w_out, na_w_in=v_na_w_in,
                na_rpb=v_na_rpb, na_w_out=v_na_w_out, conv_w_in=v_conv_w_in, conv_dw=v_conv_dw, conv_db=v_conv_db,
                conv_w_out=v_conv_w_out, final_g=v_final_g)
    d = x.shape[-1]
    w = na_w_out.shape[1] * 4
    xi, yi, ci = _position()
    chip = 2 * xi + yi
    dev = 2 * chip + ci
    n_ada = ada_w.shape[-1]

    def chip_cols(a, size):
        return lax.dynamic_slice_in_dim(a, chip * size, size, axis=a.ndim - 1)

    items = _items(d, w)
    first = [it for it in items if it.key.startswith("pool") and it.layer == 0]
    na = [it for it in items if it.key.startswith("na")]
    late = [it for it in items if it not in first + na]
    shards_of = lambda its: [params[it.key][it.layer].astype(BF16) for it in its]
    empties = lambda its: [lax.empty(it.shape, BF16) for it in its]
    first_copies, na_copies = _gather_ici_copies(first), _gather_ici_copies(na)

    conds = _all_gather(c.reshape(8, d // 8), _AXES, "gather_cond").reshape(8, d)
    behind = conds[0, 0] * 0.0
    first_handle, token = _split_start(first_copies, [s + behind.astype(BF16) for s in shards_of(first)],
                                       empties(first), 3 * len(first), "gather_first_start")
    cond = jnp.concatenate([conds + token[0, 0], c_ctx[None], jnp.zeros((_COND_ROWS - 9, d), F32)], axis=0)
    mod_cols = _modulations(cond, ada_w, chip_cols(ada_b, n_ada)[:, None, :])
    small_pack, small_layout = _pack([pool_scale, conv_dw, conv_db])
    (mod_all,), (small,) = _run_comms([_all_gather_comm(mod_cols, ("x", "y")),
                                       _all_gather_comm(small_pack, ("x", "y"))], "gather_mod")
    behind = mod_all[0, 0, 0, 0] * 0.0
    na_handle, token = _split_start(na_copies, [s + behind.astype(BF16) for s in shards_of(na)], empties(na),
                                    3 * len(na), "gather_na_start")
    first_shards, first_mats = _split_wait(first_copies, first_handle, token, "gather_first_wait")
    first_mats = _gather_pair_finish(first_shards, first_mats, first, "gather_first_pair")
    mod_all = mod_all.transpose(1, 2, 0, 3).reshape(4, _COND_ROWS, 3, d)
    mod = jnp.stack([lax.dynamic_index_in_dim(mod_all, dev, axis=1, keepdims=False), mod_all[:, 8]], axis=1)
    full = {(it.key, it.layer): mat for it, mat in zip(first, first_mats)}
    late_comm = _gather_comm(shards_of(late), late)

    def na_weights(after):
        na_shards, na_mats = _split_wait(na_copies, na_handle, after, "gather_na_wait")
        na_mats = _gather_pair_finish(na_shards, na_mats, na, "gather_na_pair")
        return {it.key: mat for it, mat in zip(na, na_mats)}

    def late_weights(mats):
        full.update({(it.key, it.layer): mat for it, mat in zip(late, mats)})
        return dict(pool_w_in=[full[("pool_w_in", j)] for j in range(2)],
                    pool_w_grp=[full[("pool_w_grp", j)] for j in range(2)],
                    pool_w_out=[full[("pool_w_out", j)] for j in range(2)],
                    conv_w_in=full[("conv_w_in", 0)], conv_w_out=full[("conv_w_out", 0)])

    small = small.reshape(4, -1)

    def whole(k):
        parts = _unpack(small, small_layout, k)
        return jnp.moveaxis(parts, 0, -2).reshape(parts.shape[1:-1] + (-1,))

    wts = dict(pool_w_in=[full[("pool_w_in", 0)]], pool_w_grp=[full[("pool_w_grp", 0)]],
               pool_w_out=[full[("pool_w_out", 0)]], pool_scale=whole(0), na_rpb=na_rpb[0], conv_dw=whole(1)[0],
               conv_db=whole(2))
    pos = jnp.stack([chip, ci]).astype(jnp.int32)

    def layer_grads(its, by_layer):
        pick = {"pool_w_in": "w_in", "pool_w_grp": "w_grp", "pool_w_out": "w_out", "na_w_in": "w_in",
                "na_w_out": "w_out", "conv_w_in": "w_in", "conv_w_out": "w_out"}
        return [by_layer[(it.key.split("_")[0], it.layer)][pick[it.key]] for it in its]

    pairs, handles = dict(), dict()
    half_windows = lambda its: [(lambda ref, half, it=it: it.window(ref, half=half)) for it in its]
    half_shapes = lambda its: [_sds(it.sized(half=True), BF16) for it in its]

    def pair_sums(its, mats, tag):
        got = _pair_swap(mats, half_windows(its), half_shapes(its), f"pair_exchange_{tag}")
        return _pair_sums(mats, got, its, pos, f"pair_sum_{tag}")

    def grad_comm(gr3, gr2):
        pairs["late"] = pair_sums(late, layer_grads(late, {("pool", 1): gr3, ("conv", 0): gr2}), "late")
        return _chip_exchange_comm(pairs["late"], late)

    slot_zones = lambda its: [lax.empty((3,) + it.sized(shard=True, half=True), BF16) for it in its]
    na_xcopies, first_xcopies = _chip_exchange_copies(na), _chip_exchange_copies(first)

    def na_grads_start(gr1):
        pairs["na"] = pair_sums(na, layer_grads(na, {("na", 0): gr1}), "na")
        handles["na"], started = _split_start(na_xcopies, pairs["na"], slot_zones(na), 3 * len(na),
                                              "exchange_na_start")
        return started

    res = _example_step(x[0], ctx[0], loss_target[0], mod, norm_g, final_g[None], wts, dict(
        na_weights=na_weights, late_comm=late_comm, late_weights=late_weights, grad_comm=grad_comm,
        na_grads_start=na_grads_start))
    g0, g1, g2, g3 = res["layers"]
    pairs["na"], na_slots = _split_wait(na_xcopies, handles["na"], g0["w_in"], "exchange_na_wait")
    first_grads = layer_grads(first, {("pool", 0): g0})
    packed, layout = _pack([res["dfinal_g"], res["dnorm_g"], res["dmod"], g1["rpb"],
                            jnp.concatenate([g0["scale"], g3["scale"]], axis=0), g2["dw"], g2["db"],
                            res["loss"][0, :1]])
    first_got, (every,) = _run_comms([_pair_swap_comm(first_grads, half_windows(first), half_shapes(first)),
                                      _all_gather_two_level_comm(packed)], "pair_exchange_first")
    pairs["first"] = _pair_sums(first_grads, first_got, first, pos, "pair_sum_first")

    grads = dict()
    total = _sum_lead(every, "sum_vec_grads").reshape(-1)
    every = every.reshape(8, -1)
    grads["final_g"] = _unpack(total, layout, 0).reshape(final_g.shape)
    grads["norm_g"] = _unpack(total, layout, 1)
    grads["na_rpb"] = _unpack(total, layout, 3)[None]
    grads["pool_scale"] = chip_cols(_unpack(total, layout, 4), pool_scale.shape[-1])
    grads["conv_dw"] = chip_cols(_unpack(total, layout, 5), conv_dw.shape[-1])[None]
    grads["conv_db"] = chip_cols(_unpack(total, layout, 6), conv_db.shape[-1])
    dmod_sum = _unpack(total, layout, 2).reshape(4, 2, 3 * d)
    dmod_each = _unpack(every, layout, 2).reshape(8, 4, 2, 3 * d)
    grads["ada_b"] = dmod_sum[:, 0] + dmod_sum[:, 1]
    dm = jnp.concatenate([dmod_each[:, :, 0].transpose(1, 0, 2), dmod_sum[:, 1][:, None],
                          jnp.zeros((4, _COND_ROWS - 9, 3 * d), F32)], axis=1)
    dm_cols = chip_cols(dm, n_ada)
    dcond = _cond_grad(dm_cols, ada_w)[8].reshape(8, d // 8)
    dcond_all = _all_gather(dcond, ("x", "y"), "gather_cond_grad")
    behind = dcond_all[0, 0, 0] * 0.0
    handles["first"], token = _split_start(first_xcopies, [p + behind.astype(BF16) for p in pairs["first"]],
                                           slot_zones(first), 3 * len(first), "exchange_first_start")
    early = late + na
    slots = dict(zip(early, list(res["carried"]) + list(na_slots)))
    pair_of = dict(zip(early, list(pairs["late"]) + list(pairs["na"])))
    whole = lambda its: [lambda ref, half: ref] * len(its)
    reduced_early = _chip_sums([pair_of[it] for it in early], [slots[it] for it in early], early, pos,
                               "chip_sum_early", after=token)
    return_copies = _pair_swap_copies(whole(early))
    handles["return"], token = _split_start(return_copies, reduced_early,
                                            [lax.empty(t.shape, F32) for t in reduced_early], len(early),
                                            "pair_return_early_start")
    grads["ada_w"], *ada_w_step = _ada_w_step(cond, dm_cols + token[0, 0], ada_w, m_ada_w, v_ada_w)
    grads["c_ctx"] = _sum_lead(dcond_all, "sum_cond_grad").reshape(d) * _dsilu(c_ctx)
    vector_out = {k: _adamw(params[k], grads[k], mom1[k], mom2[k], f"adamw_{k}")
                  for k in _WEIGHTS if k not in _GRAD_KEYS + ("ada_w",)}
    vector_out["ada_w"] = tuple(ada_w_step)
    pairs["first"], first_slots = _split_wait(first_xcopies, handles["first"], vector_out["ada_w"][2],
                                              "exchange_first_wait")
    reduced_first = _chip_sums(pairs["first"], first_slots, first, pos, "chip_sum_first")
    theirs_first = _pair_swap(reduced_first, whole(first), [_sds(t.shape, F32) for t in reduced_first],
                              "pair_return_first")
    reduced_early, theirs_early = _split_wait(return_copies, handles["return"], theirs_first[0],
                                              "pair_return_early_wait")
    reduced = dict(zip(early + first, list(reduced_early) + list(reduced_first)))
    theirs = dict(zip(early + first, list(theirs_early) + list(theirs_first)))
    matrix_out = dict()
    for k in _GRAD_KEYS:
        its = [it for it in items if it.key == k]
        res_k = _adamw_matrix(params[k], mom1[k], mom2[k], [reduced[it] for it in its], [theirs[it] for it in its],
                              its[0], pos, f"adamw_{k}")
        grads[k], matrix_out[k] = res_k[0], res_k[1:]

    outs = [[], [], []]
    for k in _WEIGHTS:
        step = matrix_out[k] if k in matrix_out else vector_out[k]
        for lst, val in zip(outs, step):
            lst.append(val)
    loss = _unpack(total, layout, 7)[0]
    return (loss, res["grad_x"][None], *[grads[k].reshape(params[k].shape) for k in _WEIGHTS],
            *outs[0], *outs[1], *outs[2])
```

```python
import functools

import numpy as np
import jax
import jax.numpy as jnp
from jax import lax
from jax.experimental import pallas as pl
from jax.experimental.pallas import tpu as pltpu

F32 = jnp.float32
BF16 = jnp.bfloat16

EPS = 1e-6
GRID_W = 64
HEAD_DIM = 64
WIN_ROWS = 8
WIN_COLS = 16
POOL_WINDOWS = (2, 4, 8, 16)
Q_ROWS = 4
K_ROWS = 12
PAD_ROWS = 4
NEG = -1e30

ADAM_LR = 0.001
ADAM_B1 = 0.9
ADAM_B2 = 0.999
ADAM_EPS = 1e-08
ADAM_WD = 0.01
ADAM_STEP = 10

ROW_BLOCK = 256
VMEM_LIMIT = 56 * 1024 * 1024
ACT = BF16

MESH = pl.DeviceIdType.MESH
HBM_SPEC = pl.BlockSpec(memory_space=pltpu.HBM)


def _cparams(*sem):
    return pltpu.CompilerParams(dimension_semantics=sem or None, vmem_limit_bytes=VMEM_LIMIT)


def _sds(shape, dtype):
    return jax.ShapeDtypeStruct(tuple(shape), dtype)


def _call(body, args, *, grid, in_specs, out_specs, out_shape, name, scratch_shapes=()):
    return list(pl.pallas_call(
        body, grid=grid, in_specs=list(in_specs), out_specs=list(out_specs), out_shape=list(out_shape),
        scratch_shapes=list(scratch_shapes), name=name, compiler_params=_cparams(*(("arbitrary",) * len(grid))),
    )(*args))


def _sigmoid(x):
    return 1.0 / (1.0 + jnp.exp(-x))


def _silu(x):
    return x * _sigmoid(x)


def _dsilu(x):
    s = _sigmoid(x)
    return s * (1.0 + x * (1.0 - s))


_DIMS = {
    "nn": (((1,), (0,)), ((), ())),
    "nt": (((1,), (1,)), ((), ())),
    "tn": (((0,), (0,)), ((), ())),
}


def _matmul(a, b, *, mode, grid, a_spec, b_spec, out_shapes, out_specs, name, nk=1,
            a_silu=False, exact=False, epilogue=None, extra=(), extra_specs=(), acc_shape=None):
    n_extra = len(extra)
    n_out = len(out_shapes)

    def body(*refs):
        a_ref, b_ref = refs[:2]
        ex = refs[2:2 + n_extra]
        outs = refs[2 + n_extra:2 + n_extra + n_out]
        av = a_ref[...]
        bv = b_ref[...]
        if a_silu:
            av = _silu(av.astype(F32))
        if exact:
            prod = lax.dot_general(av.astype(F32), bv.astype(F32), _DIMS[mode],
                                   precision=lax.Precision.HIGHEST, preferred_element_type=F32)
        else:
            prod = lax.dot_general(av.astype(BF16), bv.astype(BF16), _DIMS[mode], preferred_element_type=F32)

        def finish(res):
            if epilogue == "bias":
                res = res + ex[0][...]
            outs[0][...] = res.astype(outs[0].dtype)

        if nk == 1:
            finish(prod)
        else:
            acc = refs[-1]
            k = pl.program_id(len(grid) - 1)

            @pl.when(k == 0)
            def _():
                acc[...] = prod

            @pl.when(k > 0)
            def _():
                acc[...] += prod

            @pl.when(k == nk - 1)
            def _():
                finish(acc[...])

    scratch = [pltpu.VMEM(acc_shape, F32)] if nk > 1 else []
    sem = ("parallel",) * (len(grid) - 1) + ("arbitrary",)
    return pl.pallas_call(
        body, grid=grid, in_specs=[a_spec, b_spec, *extra_specs], out_specs=list(out_specs),
        out_shape=list(out_shapes), scratch_shapes=scratch, name=name, compiler_params=_cparams(*sem),
    )(a, b, *extra)


def _row_tile(rows):
    for t in (768, 512, 256):
        if rows % t == 0:
            return t
    return rows


def _mm_tn(a, b, name, out_dtype, tm=512):
    r, m = a.shape
    n = b.shape[1]
    tm = min(tm, m)
    tn = min(1024, n)
    return _matmul(
        a, b, mode="tn", grid=(m // tm, n // tn),
        a_spec=pl.BlockSpec((r, tm), lambda i, j: (0, i)), b_spec=pl.BlockSpec((r, tn), lambda i, j: (0, j)),
        out_shapes=[_sds((m, n), out_dtype)], out_specs=[pl.BlockSpec((tm, tn), lambda i, j: (i, j))], name=name)[0]


def _mm_tn_parts(a, b, name, out_dtype, tm=512):
    r, m = a.shape
    p, _, np_ = b.shape
    tm = min(tm, m)
    return _matmul(
        a, b, mode="tn", grid=(m // tm, p),
        a_spec=pl.BlockSpec((r, tm), lambda i, q: (0, i)), b_spec=pl.BlockSpec((None, r, np_), lambda i, q: (q, 0, 0)),
        out_shapes=[_sds((m, p * np_), out_dtype)], out_specs=[pl.BlockSpec((tm, np_), lambda i, q: (i, q))],
        name=name)[0]


def _row_vec(ref, is_ctx):
    return ref[0] if is_ctx is None else jnp.where(is_ctx, ref[1], ref[0])


def _ctx_rows(i, tm, nx, nseg):
    if nseg == 1:
        return None
    return i * tm + lax.broadcasted_iota(jnp.int32, (tm, 1), 0) >= nx


def _seg_sums(ref, val, is_ctx, first):
    if is_ctx is None:
        parts = [jnp.sum(val, axis=0, keepdims=True)]
    else:
        parts = [jnp.sum(jnp.where(is_ctx, 0.0, val), axis=0, keepdims=True),
                 jnp.sum(jnp.where(is_ctx, val, 0.0), axis=0, keepdims=True)]

    @pl.when(first)
    def _():
        for k, p in enumerate(parts):
            ref[k] = p

    @pl.when(jnp.logical_not(first))
    def _():
        for k, p in enumerate(parts):
            ref[k] += p


def _w_out_resid(a, w_out, xres, gate, nx, name):
    m, k = a.shape
    n = w_out.shape[1]
    nseg = gate.shape[0]
    tm = _row_tile(m)

    def body(a_ref, w_ref, x_ref, gt_ref, yx_ref, xo_ref):
        yx = jnp.dot(a_ref[...], w_ref[...], preferred_element_type=F32)
        yx_ref[...] = yx.astype(ACT)
        xo_ref[...] = x_ref[...] + _row_vec(gt_ref, _ctx_rows(pl.program_id(0), tm, nx, nseg)) * yx

    row = pl.BlockSpec((tm, n), lambda i: (i, 0))
    return pl.pallas_call(
        body, grid=(m // tm,),
        in_specs=[pl.BlockSpec((tm, k), lambda i: (i, 0)), pl.BlockSpec((k, n), lambda i: (0, 0)), row,
                  pl.BlockSpec((nseg, 1, n), lambda i: (0, 0, 0))],
        out_specs=[row, row], out_shape=[_sds((m, n), ACT), _sds((m, n), F32)],
        name=name, compiler_params=_cparams("parallel"),
    )(a, w_out, xres, gate)


def _norm_w_in(x, g, scale, shift, w_in, nx, name, ctx=None):
    d = x.shape[1]
    rows = x.shape[0] + (0 if ctx is None else ctx.shape[0])
    n = w_in.shape[1]
    nseg = scale.shape[0]
    tm = _row_tile(rows)
    tn = min(2048, n)
    row = pl.BlockSpec((tm, d), lambda i, j: (i, 0))
    if ctx is None:
        row_args, row_specs = (x,), [row]
    else:
        assert ctx.shape[0] == ROW_BLOCK and tm % ROW_BLOCK == 0 and nx % ROW_BLOCK == 0
        nsub, x_blocks = tm // ROW_BLOCK, nx // ROW_BLOCK
        row_args = (x,) * nsub + (ctx,)
        row_specs = [pl.BlockSpec((ROW_BLOCK, d), lambda i, j, s=s: (jnp.minimum(i * nsub + s, x_blocks - 1), 0))
                     for s in range(nsub)] + [pl.BlockSpec((ROW_BLOCK, d), lambda i, j: (0, 0))]

    def body(*refs):
        x_refs, (g_ref, sc_ref, sh_ref, w_ref), outs = refs[:len(row_args)], refs[len(row_args):][:4], refs[-3:]
        h_ref, r_ref, p_ref = outs
        i, j = pl.program_id(0), pl.program_id(1)

        @pl.when(j == 0)
        def _():
            if ctx is None:
                xv = x_refs[0][...]
            else:
                xv = jnp.concatenate([jnp.where(i * nsub + s >= x_blocks, x_refs[-1][...], x_refs[s][...])
                                      for s in range(nsub)], axis=0)
                refs[-4][...] = xv
            r = lax.rsqrt(jnp.mean(xv * xv, axis=-1, keepdims=True) + EPS)
            is_ctx = _ctx_rows(i, tm, nx, nseg)
            h = (xv * r) * g_ref[...] * (1.0 + _row_vec(sc_ref, is_ctx)) + _row_vec(sh_ref, is_ctx)
            h_ref[...] = h.astype(BF16)
            r_ref[...] = r

        p_ref[...] = jnp.dot(h_ref[...], w_ref[...], preferred_element_type=F32).astype(ACT)

    vec = pl.BlockSpec((nseg, 1, d), lambda i, j: (0, 0, 0))
    joined = [] if ctx is None else [(row, _sds((rows, d), F32))]
    out_specs, out_shape = zip(*joined, (row, _sds((rows, d), BF16)),
                               (pl.BlockSpec((tm, 1), lambda i, j: (i, 0)), _sds((rows, 1), F32)),
                               (pl.BlockSpec((tm, tn), lambda i, j: (i, j)), _sds((rows, n), ACT)))
    return _call(
        body, (*row_args, g, scale, shift, w_in), grid=(rows // tm, n // tn),
        in_specs=[*row_specs, pl.BlockSpec((1, d), lambda i, j: (0, 0)), vec, vec,
                  pl.BlockSpec((d, tn), lambda i, j: (0, j))],
        out_specs=list(out_specs), out_shape=list(out_shape), name=name)


def _gate_w_out_bwd(dxo, yx, gate, w_out, nx, name):
    rows, d = yx.shape
    w = w_out.shape[0]
    nseg = gate.shape[0]
    tm = _row_tile(rows)

    def body(dx_ref, yx_ref, gt_ref, w_ref, dyx_ref, da_ref, dg_ref):
        i = pl.program_id(0)
        is_ctx = _ctx_rows(i, tm, nx, nseg)
        dxv = dx_ref[...]
        dyx = (dxv * _row_vec(gt_ref, is_ctx)).astype(BF16)
        dyx_ref[...] = dyx
        da_ref[...] = lax.dot_general(dyx, w_ref[...], _DIMS["nt"], preferred_element_type=F32).astype(ACT)
        _seg_sums(dg_ref, dxv * yx_ref[...].astype(F32), is_ctx, i == 0)

    row = pl.BlockSpec((tm, d), lambda i: (i, 0))
    vec = pl.BlockSpec((nseg, 1, d), lambda i: (0, 0, 0))
    return _call(
        body, (dxo, yx, gate, w_out), grid=(rows // tm,),
        in_specs=[row, row, vec, pl.BlockSpec((w, d), lambda i: (0, 0))],
        out_specs=[row, pl.BlockSpec((tm, w), lambda i: (i, 0)), vec],
        out_shape=[_sds((rows, d), BF16), _sds((rows, w), ACT), _sds((nseg, 1, d), F32)], name=name)


def _w_in_bwd_norm(dparts, w_in, x, r, g, scale, dres, nx, name, dx_rows=None):
    np_, rows, kp = dparts.shape
    d = w_in.shape[0]
    nseg = scale.shape[0]
    tm = _row_tile(rows)
    assert dx_rows is None or rows - tm < dx_rows <= rows
    nsub = tm // ROW_BLOCK
    nres_blocks = dres.shape[0] // ROW_BLOCK

    def body(dp_ref, w_ref, x_ref, r_ref, g_ref, sc_ref, *rest):
        dres_refs = rest[:nsub]
        dx_ref, dsh_ref, dge_ref, acc = rest[nsub:]
        i, k = pl.program_id(0), pl.program_id(1)
        prod = lax.dot_general(dp_ref[...], w_ref[...], _DIMS["nt"], preferred_element_type=F32)

        @pl.when(k == 0)
        def _():
            acc[...] = prod

        @pl.when(k > 0)
        def _():
            acc[...] += prod

        @pl.when(k == np_ - 1)
        def _():
            is_ctx = _ctx_rows(i, tm, nx, nseg)
            dhv = acc[...]
            rv = r_ref[...]
            xn = x_ref[...] * rv
            dxn = dhv * (g_ref[...] * (1.0 + _row_vec(sc_ref, is_ctx)))
            dx = rv * (dxn - xn * jnp.mean(dxn * xn, axis=-1, keepdims=True))
            for s in range(nsub):
                piece = slice(s * ROW_BLOCK, (s + 1) * ROW_BLOCK)
                res = dres_refs[s][...]
                if nres_blocks * ROW_BLOCK < rows:
                    res = jnp.where(i * nsub + s < nres_blocks, res, 0.0)
                dx_ref[piece, :] = dx[piece, :] + res
            _seg_sums(dsh_ref, dhv, is_ctx, i == 0)
            _seg_sums(dge_ref, dhv * xn, is_ctx, i == 0)

    row = pl.BlockSpec((tm, d), lambda i, k: (i, 0))
    vec = pl.BlockSpec((nseg, 1, d), lambda i, k: (0, 0, 0))
    return _call(
        body, (dparts, w_in, x, r, g, scale, *([dres] * nsub)), grid=(rows // tm, np_),
        in_specs=[pl.BlockSpec((None, tm, kp), lambda i, k: (k, i, 0)), pl.BlockSpec((d, kp), lambda i, k: (0, k)),
                  row, pl.BlockSpec((tm, 1), lambda i, k: (i, 0)), pl.BlockSpec((1, d), lambda i, k: (0, 0)), vec]
        + [pl.BlockSpec((ROW_BLOCK, d), (lambda i, k, s=s: (jnp.minimum(i * nsub + s, nres_blocks - 1), 0)))
           for s in range(nsub)],
        out_specs=[row, vec, vec],
        out_shape=[_sds((dx_rows or rows, d), F32), _sds((nseg, 1, d), F32), _sds((nseg, 1, d), F32)],
        scratch_shapes=[pltpu.VMEM((tm, d), F32)], name=name)


_PAD_TOP = 16
_PAD_BOT = 32


def _window_sum(buf, xv, lo, n):
    t = xv.shape[0]
    c = xv.shape[1]
    tp = t + _PAD_TOP + _PAD_BOT
    buf[pl.ds(0, _PAD_TOP), :] = jnp.zeros((_PAD_TOP, c), F32)
    buf[pl.ds(_PAD_TOP, t), :] = xv
    buf[pl.ds(_PAD_TOP + t, _PAD_BOT), :] = jnp.zeros((_PAD_BOT, c), F32)
    p = buf[...]
    k = 1
    while k < n:
        p = p + pltpu.roll(p, tp - k, 0)
        k *= 2
    if lo:
        p = pltpu.roll(p, -lo, 0)
    buf[...] = p
    return buf[pl.ds(_PAD_TOP, t), :]


def _window_count(t, half):
    pos = lax.broadcasted_iota(jnp.int32, (t, 1), 0)
    return (jnp.minimum(pos + half, t) - jnp.maximum(pos - half, 0)).astype(F32)


def _segments(rows, nx):
    return [(0, nx)] + ([(nx, rows - nx)] if rows > nx else [])


def _pool_scratch(rows, nx, cols):
    return [pltpu.VMEM((length + _PAD_TOP + _PAD_BOT, cols), F32) for _, length in _segments(rows, nx)]


def _per_group(g, fn):
    for gi, win in enumerate(POOL_WINDOWS):
        pl.when(g == gi)(functools.partial(fn, win))


def _pool_grp_fwd(uv, w_grp, scale, nx, name):
    rows = uv.shape[0]
    ng, gc, _ = w_grp.shape
    w = ng * gc
    segs = _segments(rows, nx)

    def body(u_ref, gt_ref, w_ref, sc_ref, z_ref, mx_ref, a_ref, *bufs):
        def pool(win):
            half = win // 2
            for (start, length), buf in zip(segs, bufs):
                uvv = u_ref[pl.ds(start, length), :].astype(F32)
                s = _window_sum(buf, uvv, -half, win)
                z_ref[pl.ds(start, length), :] = (s / _window_count(length, half) - uvv).astype(BF16)

        _per_group(pl.program_id(0), pool)
        mixed = jnp.dot(z_ref[...], w_ref[...], preferred_element_type=F32)
        mx_ref[...] = mixed.astype(ACT)
        a_ref[...] = (mixed * sc_ref[...] * _silu(gt_ref[...].astype(F32))).astype(BF16)

    col = pl.BlockSpec((rows, gc), lambda g: (0, g))
    return _call(
        body, (uv, uv, w_grp, scale), grid=(ng,),
        in_specs=[col, pl.BlockSpec((rows, gc), lambda g: (0, ng + g)), pl.BlockSpec((None, gc, gc), lambda g: (g, 0, 0)),
                  pl.BlockSpec((1, gc), lambda g: (0, g))],
        out_specs=[col, col, col], out_shape=[_sds((rows, w), BF16), _sds((rows, w), ACT), _sds((rows, w), BF16)],
        scratch_shapes=_pool_scratch(rows, nx, gc), name=name)


def _pool_grp_bwd(da, mixed, uv, scale, w_grp, nx, name):
    rows, w = da.shape
    ng, gc, _ = w_grp.shape
    segs = _segments(rows, nx)

    def body(da_ref, mx_ref, gt_ref, sc_ref, w_ref, dm_ref, duv_ref, dsc_ref, dz_ref, *bufs):
        dav = da_ref[...].astype(F32)
        mixed = mx_ref[...].astype(F32)
        gt = gt_ref[...].astype(F32)
        sg = _silu(gt)
        sc = sc_ref[...]
        dm = (dav * sc * sg).astype(BF16)
        dm_ref[...] = dm
        dz_ref[...] = lax.dot_general(dm, w_ref[...], _DIMS["nt"], preferred_element_type=F32)
        duv_ref[1] = (dav * mixed * sc * _dsilu(gt)).astype(BF16)
        dsc_ref[...] = jnp.sum(dav * mixed * sg, axis=0, keepdims=True)

        def unpool(win):
            half = win // 2
            for (start, length), buf in zip(segs, bufs):
                dzv = dz_ref[pl.ds(start, length), :]
                s = _window_sum(buf, dzv / _window_count(length, half), 1 - half, win)
                duv_ref[0, pl.ds(start, length), :] = (s - dzv).astype(BF16)

        _per_group(pl.program_id(0), unpool)

    col = pl.BlockSpec((rows, gc), lambda g: (0, g))
    vec = pl.BlockSpec((1, gc), lambda g: (0, g))
    return pl.pallas_call(
        body, grid=(ng,),
        in_specs=[col, col, pl.BlockSpec((rows, gc), lambda g: (0, ng + g)), vec,
                  pl.BlockSpec((None, gc, gc), lambda g: (g, 0, 0))],
        out_specs=[col, pl.BlockSpec((2, rows, gc), lambda g: (0, 0, g)), vec],
        out_shape=[_sds((rows, w), BF16), _sds((2, rows, w), BF16), _sds((1, w), F32)],
        scratch_shapes=[pltpu.VMEM((rows, gc), F32)] + _pool_scratch(rows, nx, gc),
        name=name, compiler_params=_cparams("parallel"),
    )(da, mixed, uv, scale, w_grp)


def _grp_wgrad(z, dm, ng, name, out_dtype):
    rows, w = z.shape
    gc = w // ng

    def body(z_ref, dm_ref, o_ref):
        o_ref[...] = lax.dot_general(z_ref[...], dm_ref[...], _DIMS["tn"],
                                     preferred_element_type=F32).astype(o_ref.dtype)

    blk = pl.BlockSpec((rows, gc), lambda g: (0, g))
    return pl.pallas_call(
        body, grid=(ng,), in_specs=[blk, blk], out_specs=pl.BlockSpec((None, gc, gc), lambda g: (g, 0, 0)),
        out_shape=_sds((ng, gc, gc), out_dtype), name=name, compiler_params=_cparams("parallel"),
    )(z, dm)


def _shift_rows(v, by):
    t = v.shape[0]
    pos = lax.broadcasted_iota(jnp.int32, v.shape, 0)
    rolled = pltpu.roll(v, by % t, 0)
    keep = pos >= by if by > 0 else pos < t + by
    return jnp.where(keep, rolled, 0.0)


def _conv_specs(t, w, cb):
    return [pl.BlockSpec((t, cb), (lambda j, q=q: (0, q * (w // cb) + j))) for q in range(4)]


def _conv_fwd(p4, dw, db, name):
    t = p4.shape[0]
    w = p4.shape[1] // 4
    cb = 128

    def body(bg_ref, cg_ref, v_ref, g_ref, dw_ref, db_ref, a_ref):
        tv = cg_ref[...].astype(F32) * v_ref[...].astype(F32)
        conv = (dw_ref[0:1, :] * _shift_rows(tv, 1) + dw_ref[1:2, :] * tv + dw_ref[2:3, :] * _shift_rows(tv, -1)
                + db_ref[...])
        a_ref[...] = (bg_ref[...].astype(F32) * conv * _silu(g_ref[...].astype(F32))).astype(BF16)

    return pl.pallas_call(
        body, grid=(w // cb,),
        in_specs=_conv_specs(t, w, cb) + [pl.BlockSpec((3, cb), lambda j: (0, j)), pl.BlockSpec((1, cb), lambda j: (0, j))],
        out_specs=pl.BlockSpec((t, cb), lambda j: (0, j)), out_shape=_sds((t, w), BF16),
        name=name, compiler_params=_cparams("parallel"),
    )(p4, p4, p4, p4, dw, db)


def _conv_bwd(da, p4, dw, db, name):
    t, w = da.shape
    cb = 128

    def body(da_ref, bg_ref, cg_ref, v_ref, g_ref, dw_ref, db_ref, d4_ref, ddw_ref, ddb_ref):
        cg = cg_ref[...].astype(F32)
        vv = v_ref[...].astype(F32)
        bg = bg_ref[...].astype(F32)
        gv = g_ref[...].astype(F32)
        tv = cg * vv
        tm1 = _shift_rows(tv, 1)
        tp1 = _shift_rows(tv, -1)
        w0, w1, w2 = dw_ref[0:1, :], dw_ref[1:2, :], dw_ref[2:3, :]
        conv = w0 * tm1 + w1 * tv + w2 * tp1 + db_ref[...]
        y = bg * conv
        dav = da_ref[...].astype(F32)
        dy = dav * _silu(gv)
        d4_ref[3] = (dav * y * _dsilu(gv)).astype(BF16)
        d4_ref[0] = (dy * conv).astype(BF16)
        dconv = dy * bg
        ddb_ref[...] = jnp.sum(dconv, axis=0, keepdims=True)
        ddw_ref[0:1, :] = jnp.sum(dconv * tm1, axis=0, keepdims=True)
        ddw_ref[1:2, :] = jnp.sum(dconv * tv, axis=0, keepdims=True)
        ddw_ref[2:3, :] = jnp.sum(dconv * tp1, axis=0, keepdims=True)
        dt = w0 * _shift_rows(dconv, -1) + w1 * dconv + w2 * _shift_rows(dconv, 1)
        d4_ref[1] = (dt * vv).astype(BF16)
        d4_ref[2] = (dt * cg).astype(BF16)

    col = pl.BlockSpec((t, cb), lambda j: (0, j))
    tap = pl.BlockSpec((3, cb), lambda j: (0, j))
    bias = pl.BlockSpec((1, cb), lambda j: (0, j))
    return pl.pallas_call(
        body, grid=(w // cb,), in_specs=[col] + _conv_specs(t, w, cb) + [tap, bias],
        out_specs=[pl.BlockSpec((4, t, cb), lambda j: (0, 0, j)), tap, bias],
        out_shape=[_sds((4, t, w), BF16), _sds((3, w), F32), _sds((1, w), F32)],
        name=name, compiler_params=_cparams("parallel"),
    )(da, p4, p4, p4, p4, dw, db)


def _attn_mask():
    qn, kn = Q_ROWS * GRID_W, K_ROWS * GRID_W
    qr, qc = np.divmod(np.arange(qn), GRID_W)
    kr, kc = np.divmod(np.arange(kn), GRID_W)
    col0 = np.clip(qc - WIN_COLS // 2, 0, GRID_W - WIN_COLS)
    col_ok = (kc[None, :] >= col0[:, None]) & (kc[None, :] < col0[:, None] + WIN_COLS)
    first = np.zeros(qn, np.int64)
    last = np.full(qn, K_ROWS - WIN_ROWS)
    out = []
    for row0 in (first, qr, last):
        row_ok = (kr[None, :] >= row0[:, None]) & (kr[None, :] < row0[:, None] + WIN_ROWS)
        out.append(np.where(row_ok & col_ok, 0.0, NEG))
    return jnp.asarray(np.stack(out), F32)


_KW = K_ROWS * GRID_W
_QB = Q_ROWS * GRID_W
_PAIR = 2 * HEAD_DIM
_N_DR = 2 * WIN_ROWS - 1
_N_DC = 2 * WIN_COLS - 1
_RP_ROWS = 24
_N_TILES = _N_DR + 1
_BIAS_BASE = (WIN_ROWS - 1, WIN_ROWS // 2 - 1, -1)


class _Comm:
    def __init__(self, ins, outs, sems, start, finish):
        self.ins, self.outs, self.sems, self.start, self.finish = list(ins), list(outs), list(sems), start, finish


def _bias_pieces(cls):
    out = []
    for qr in range(Q_ROWS):
        for kr in range(0, K_ROWS, 2):
            tile = _BIAS_BASE[cls] - qr + kr + 1
            out.append((qr, kr, tile if 0 <= tile < _N_TILES else None))
    return out


def _toeplitz_pair(left_row, right_row):
    lane = lax.broadcasted_iota(jnp.int32, (GRID_W, _PAIR), 1)
    shape = (GRID_W, _PAIR)
    left = pltpu.roll(jnp.broadcast_to(left_row, shape), _PAIR - (WIN_COLS - 1), 1, stride=1, stride_axis=0)
    right = pltpu.roll(jnp.broadcast_to(right_row, shape), GRID_W - (WIN_COLS - 1), 1, stride=1, stride_axis=0)
    return jnp.where(lane < GRID_W, left, right)


def _build_tiles(tiles_ref, rp_ref):
    for h in range(2):
        for t in range(_N_TILES):
            tiles_ref[h, t] = _toeplitz_pair(rp_ref[h, t:t + 1, :], rp_ref[h, t + 1:t + 2, :])


def _block_class(b, nblk, fn, entering=False):
    interior = (b == 1) if entering else jnp.logical_and(b > 0, b < nblk - 1)
    for cls, cond in enumerate((b == 0, interior, b == nblk - 1)):
        pl.when(cond)(functools.partial(fn, cls))


def _attn_geometry(p4, nx):
    rows = p4.shape[0]
    w = p4.shape[1] // 4
    nhp = w // _PAIR
    nblk = nx // _QB
    qspec = lambda col: pl.BlockSpec((_QB, _PAIR), lambda hp, b: (b, col * nhp + hp))
    kspec = lambda col: pl.BlockSpec((rows, _PAIR), lambda hp, b: (0, col * nhp + hp))
    tspec = pl.BlockSpec((2, _RP_ROWS, _PAIR), lambda hp, b: (hp, 0, 0))
    mspec = pl.BlockSpec((None, _QB, _KW), lambda hp, b: (jnp.where(b == 0, 0, jnp.where(b == nblk - 1, 2, 1)), 0, 0))
    lspec = pl.BlockSpec((None, _QB, 2), lambda hp, b: (hp, b, 0))
    ospec = pl.BlockSpec((_QB, _PAIR), lambda hp, b: (b, hp))
    return rows, w, nhp, nblk, qspec, kspec, tspec, mspec, lspec, ospec


def _window_start(b, nx):
    return pl.multiple_of(jnp.clip(b * _QB - PAD_ROWS * GRID_W, 0, nx - _KW), _QB)


def _load_bias(bias_ref, tiles_ref, rp_ref, m_ref, b, nblk):
    pl.when(b == 0)(lambda: _build_tiles(tiles_ref, rp_ref))

    def fill(cls):
        for h in range(2):
            for qr, kr, tile in _bias_pieces(cls):
                rows = slice(qr * GRID_W, (qr + 1) * GRID_W)
                cols = slice(kr * GRID_W, (kr + 2) * GRID_W)
                m = m_ref[rows, cols]
                bias_ref[h, rows, cols] = m if tile is None else tiles_ref[h, tile] + m

    _block_class(b, nblk, fill, entering=True)


def _attn_fwd(p4, rp, mask, nx, name, comm=None):
    rows, w, nhp, nblk, qspec, kspec, tspec, mspec, lspec, ospec = _attn_geometry(p4, nx)
    n_ctx = rows - nx
    n_cin, n_cout = (len(comm.ins), len(comm.outs)) if comm else (0, 0)

    def body(*refs):
        q_ref, k_ref, v_ref, g_ref, rp_ref, m_ref = refs[:6]
        cin = refs[6:6 + n_cin]
        a_ref, o_ref, lse_ref = refs[6 + n_cin:9 + n_cin]
        cout = refs[9 + n_cin:9 + n_cin + n_cout]
        bias_ref, tiles_ref = refs[9 + n_cin + n_cout:11 + n_cin + n_cout]
        sems = refs[11 + n_cin + n_cout:]
        hp, b = pl.program_id(0), pl.program_id(1)
        if comm:
            pl.when(jnp.logical_and(hp == 0, b == 0))(lambda: comm.start(cin, cout, sems))
        start = _window_start(b, nx)
        _load_bias(bias_ref, tiles_ref, rp_ref, m_ref, b, nblk)
        qf = q_ref[...].astype(F32) * HEAD_DIM ** -0.5
        kw = k_ref[pl.ds(start, _KW), :].astype(BF16)
        vw = v_ref[pl.ds(start, _KW), :].astype(BF16)
        kcv = k_ref[pl.ds(nx, n_ctx), :].astype(BF16)
        vcv = v_ref[pl.ds(nx, n_ctx), :].astype(BF16)
        lane = lax.broadcasted_iota(jnp.int32, (1, _PAIR), 1)
        outs, lses = [], []
        for h in range(2):
            mine = (lane >= HEAD_DIM) if h else (lane < HEAD_DIM)
            qm = jnp.where(mine, qf, 0.0).astype(BF16)
            s_loc = lax.dot_general(qm, kw, _DIMS["nt"], preferred_element_type=F32) + bias_ref[h]
            s_ctx = lax.dot_general(qm, kcv, _DIMS["nt"], preferred_element_type=F32)
            mx = jnp.maximum(jnp.max(s_loc, axis=-1, keepdims=True), jnp.max(s_ctx, axis=-1, keepdims=True))
            p_loc = jnp.exp(s_loc - mx)
            p_ctx = jnp.exp(s_ctx - mx)
            den = jnp.sum(p_loc, axis=-1, keepdims=True) + jnp.sum(p_ctx, axis=-1, keepdims=True)
            o = jnp.dot(p_loc.astype(BF16), vw, preferred_element_type=F32)
            o = o + jnp.dot(p_ctx.astype(BF16), vcv, preferred_element_type=F32)
            outs.append(o * (1.0 / den))
            lses.append(mx + jnp.log(den))
        o = jnp.where(lane < HEAD_DIM, outs[0], outs[1])
        o_ref[...] = o.astype(ACT)
        a_ref[...] = (o * _silu(g_ref[...].astype(F32))).astype(BF16)
        col = lax.broadcasted_iota(jnp.int32, (1, 2), 1)
        lse_ref[...] = jnp.where(col == 0, lses[0], lses[1])
        if comm:
            pl.when(jnp.logical_and(hp == nhp - 1, b == nblk - 1))(lambda: comm.finish(cin, cout, sems))

    res = pl.pallas_call(
        body, grid=(nhp, nblk),
        in_specs=[qspec(0), kspec(1), kspec(2), qspec(3), tspec, mspec] + [HBM_SPEC] * n_cin,
        out_specs=[ospec, ospec, lspec] + [HBM_SPEC] * n_cout,
        out_shape=[_sds((nx, w), BF16), _sds((nx, w), ACT), _sds((nhp, nx, 2), F32)] + (comm.outs if comm else []),
        scratch_shapes=[pltpu.VMEM((2, _QB, _KW), F32), pltpu.VMEM((2, _N_TILES, GRID_W, _PAIR), F32)]
        + (comm.sems if comm else []),
        name=name, compiler_params=_cparams("arbitrary", "arbitrary"),
    )(p4, p4, p4, p4, rp, mask, *(comm.ins if comm else []))
    return res[:3], res[3:]


def _fold_tiles(dtiles_ref, drp_ref):
    shape = (GRID_W, _PAIR)
    lane = lax.broadcasted_iota(jnp.int32, shape, 1)
    flip = (lax.broadcasted_iota(jnp.int32, (_PAIR, _PAIR), 0)
            + lax.broadcasted_iota(jnp.int32, (_PAIR, _PAIR), 1) == _PAIR - 1).astype(F32)
    drp_ref[...] = jnp.zeros(drp_ref.shape, F32)
    for h in range(2):
        stack = dtiles_ref[h].reshape(_N_TILES * GRID_W, _PAIR)
        rev = jnp.dot(stack, flip, precision=lax.Precision.HIGHEST, preferred_element_type=F32)
        for t in range(_N_TILES):
            tile = rev[t * GRID_W:(t + 1) * GRID_W, :]
            for side in (0, 1):
                shift = _PAIR - GRID_W * side - (WIN_COLS - 1)
                half = jnp.where((lane < GRID_W) if side else (lane >= GRID_W), tile, 0.0)
                diag = pltpu.roll(half, shift, 1, stride=1, stride_axis=0)
                drp_ref[h, t + side:t + side + 1, :] += jnp.sum(diag, axis=0, keepdims=True)


def _attn_bwd(p4, rp, mask, o, lse, da, nx, name, comm=None):
    rows, w, nhp, nblk, qspec, kspec, tspec, mspec, lspec, ospec = _attn_geometry(p4, nx)
    n_ctx = rows - nx
    n_cin, n_cout = (len(comm.ins), len(comm.outs)) if comm else (0, 0)

    def body(*refs):
        q_ref, k_ref, v_ref, g_ref, rp_ref, m_ref, o_ref, lse_ref, da_ref = refs[:9]
        cin = refs[9:9 + n_cin]
        d4_ref, drp_ref = refs[9 + n_cin:11 + n_cin]
        cout = refs[11 + n_cin:11 + n_cin + n_cout]
        bias_ref, tiles_ref, ds_ref, dtiles_ref, dk_ref, dv_ref = refs[11 + n_cin + n_cout:17 + n_cin + n_cout]
        sems = refs[17 + n_cin + n_cout:]
        hp, b = pl.program_id(0), pl.program_id(1)
        if comm:
            pl.when(jnp.logical_and(hp == 0, b == 0))(lambda: comm.start(cin, cout, sems))
        start = _window_start(b, nx)
        here = pl.multiple_of(b * _QB, _QB)

        @pl.when(b == 0)
        def _():
            dk_ref[...] = jnp.zeros(dk_ref.shape, F32)
            dv_ref[...] = jnp.zeros(dv_ref.shape, F32)
            dtiles_ref[...] = jnp.zeros(dtiles_ref.shape, F32)
            d4_ref[0, pl.ds(nx, n_ctx), :] = jnp.zeros((n_ctx, _PAIR), BF16)
            d4_ref[3, pl.ds(nx, n_ctx), :] = jnp.zeros((n_ctx, _PAIR), BF16)

        _load_bias(bias_ref, tiles_ref, rp_ref, m_ref, b, nblk)
        gv = g_ref[...].astype(F32)
        dav = da_ref[...].astype(F32)
        ov = o_ref[...].astype(F32)
        dov = dav * _silu(gv)
        d4_ref[3, pl.ds(here, _QB), :] = (dav * ov * _dsilu(gv)).astype(BF16)
        qf = q_ref[...].astype(F32) * HEAD_DIM ** -0.5
        kw = k_ref[pl.ds(start, _KW), :].astype(BF16)
        vw = v_ref[pl.ds(start, _KW), :].astype(BF16)
        kcv = k_ref[pl.ds(nx, n_ctx), :].astype(BF16)
        vcv = v_ref[pl.ds(nx, n_ctx), :].astype(BF16)
        lane = lax.broadcasted_iota(jnp.int32, (1, _PAIR), 1)
        dq = jnp.zeros((_QB, _PAIR), F32)
        for h in range(2):
            mine = (lane >= HEAD_DIM) if h else (lane < HEAD_DIM)
            qm = jnp.where(mine, qf, 0.0).astype(BF16)
            dom = jnp.where(mine, dov, 0.0)
            dob = dom.astype(BF16)
            lse = lse_ref[:, h:h + 1]
            s_loc = lax.dot_general(qm, kw, _DIMS["nt"], preferred_element_type=F32)
            p_loc = jnp.exp(s_loc + bias_ref[h] - lse)
            p_ctx = jnp.exp(lax.dot_general(qm, kcv, _DIMS["nt"], preferred_element_type=F32) - lse)
            delta = jnp.sum(dom * ov, axis=-1, keepdims=True)
            ds_loc = p_loc * (lax.dot_general(dob, vw, _DIMS["nt"], preferred_element_type=F32) - delta)
            ds_ctx = p_ctx * (lax.dot_general(dob, vcv, _DIMS["nt"], preferred_element_type=F32) - delta)
            dsb_loc = ds_loc.astype(BF16)
            dsb_ctx = ds_ctx.astype(BF16)
            dq_h = (jnp.dot(dsb_loc, kw, preferred_element_type=F32)
                    + jnp.dot(dsb_ctx, kcv, preferred_element_type=F32))
            dq = dq + jnp.where(mine, dq_h, 0.0)
            dk_ref[pl.ds(start, _KW), :] += lax.dot_general(dsb_loc, qm, _DIMS["tn"], preferred_element_type=F32)
            dv_ref[pl.ds(start, _KW), :] += lax.dot_general(p_loc.astype(BF16), dob, _DIMS["tn"],
                                                            preferred_element_type=F32)
            dk_ref[pl.ds(nx, n_ctx), :] += lax.dot_general(dsb_ctx, qm, _DIMS["tn"], preferred_element_type=F32)
            dv_ref[pl.ds(nx, n_ctx), :] += lax.dot_general(p_ctx.astype(BF16), dob, _DIMS["tn"],
                                                           preferred_element_type=F32)
            ds_ref[h] = ds_loc
        d4_ref[0, pl.ds(here, _QB), :] = (dq * HEAD_DIM ** -0.5).astype(BF16)

        def scatter(cls):
            for h in range(2):
                for qr, kr, tile in _bias_pieces(cls):
                    if tile is not None:
                        dtiles_ref[h, tile] += ds_ref[h, qr * GRID_W:(qr + 1) * GRID_W, kr * GRID_W:(kr + 2) * GRID_W]

        _block_class(b, nblk, scatter)

        @pl.when(b == nblk - 1)
        def _():
            d4_ref[1] = dk_ref[...].astype(BF16)
            d4_ref[2] = dv_ref[...].astype(BF16)
            _fold_tiles(dtiles_ref, drp_ref)

        if comm:
            pl.when(jnp.logical_and(hp == nhp - 1, b == nblk - 1))(lambda: comm.finish(cin, cout, sems))

    tiles = pltpu.VMEM((2, _N_TILES, GRID_W, _PAIR), F32)
    block = pltpu.VMEM((2, _QB, _KW), F32)
    res = pl.pallas_call(
        body, grid=(nhp, nblk),
        in_specs=[qspec(0), kspec(1), kspec(2), qspec(3), tspec, mspec, ospec, lspec, ospec] + [HBM_SPEC] * n_cin,
        out_specs=[pl.BlockSpec((4, rows, _PAIR), lambda hp, b: (0, 0, hp)), tspec] + [HBM_SPEC] * n_cout,
        out_shape=[_sds((4, rows, w), BF16), _sds(rp.shape, F32)] + (comm.outs if comm else []),
        scratch_shapes=[block, tiles, block, tiles, pltpu.VMEM((rows, _PAIR), F32), pltpu.VMEM((rows, _PAIR), F32)]
        + (comm.sems if comm else []),
        name=name, compiler_params=_cparams("arbitrary", "arbitrary"),
    )(p4, p4, p4, p4, rp, mask, o, lse, da, *(comm.ins if comm else []))
    return res[:2], res[2:]


def _w_out_loss(a, w_out, xres, gate, g, target, name):
    m, k = a.shape
    d = w_out.shape[1]
    assert gate.shape[0] == 1
    tm = _row_tile(m)
    nblk = m // tm

    def body(a_ref, w_ref, x_ref, gt_ref, g_ref, t_ref, yx_ref, loss_ref, dx_ref, dg_ref, acc_ref):
        i = pl.program_id(0)
        yx = jnp.dot(a_ref[...], w_ref[...], preferred_element_type=F32)
        yx_ref[...] = yx.astype(ACT)
        xv = x_ref[...] + gt_ref[0] * yx
        gv = g_ref[...]
        r = lax.rsqrt(jnp.mean(xv * xv, axis=-1, keepdims=True) + EPS)
        xn = xv * r
        err = xn * gv - t_ref[...]
        dy = err * (1.0 / d)
        dxn = dy * gv
        dx_ref[...] = r * (dxn - xn * jnp.mean(dxn * xn, axis=-1, keepdims=True))
        s_g = jnp.sum(dy * xn, axis=0, keepdims=True)
        s_l = jnp.sum(jnp.mean(err * err, axis=-1, keepdims=True), axis=0, keepdims=True)

        @pl.when(i == 0)
        def _():
            dg_ref[...] = s_g
            acc_ref[...] = s_l

        @pl.when(i > 0)
        def _():
            dg_ref[...] += s_g
            acc_ref[...] += s_l

        @pl.when(i == nblk - 1)
        def _():
            loss_ref[...] = jnp.broadcast_to(0.5 * acc_ref[...], loss_ref.shape)

    row = pl.BlockSpec((tm, d), lambda i: (i, 0))
    vec = pl.BlockSpec((1, d), lambda i: (0, 0))
    return pl.pallas_call(
        body, grid=(nblk,),
        in_specs=[pl.BlockSpec((tm, k), lambda i: (i, 0)), pl.BlockSpec((k, d), lambda i: (0, 0)), row,
                  pl.BlockSpec((1, 1, d), lambda i: (0, 0, 0)), vec, row],
        out_specs=[row, pl.BlockSpec((1, 128), lambda i: (0, 0)), row, vec],
        out_shape=[_sds((m, d), ACT), _sds((1, 128), F32), _sds((m, d), F32), _sds((1, d), F32)],
        scratch_shapes=[pltpu.VMEM((1, 1), F32)], name=name, compiler_params=_cparams("arbitrary"),
    )(a, w_out, xres, gate, g, target)


def _as2d(a):
    if a.ndim == 1:
        return a.reshape(-1, 128) if a.shape[0] % 128 == 0 else a.reshape(1, -1)
    return a.reshape(-1, a.shape[-1])


def _adamw(w, g, m, v, name):
    shape = w.shape
    w2, g2, m2, v2 = (_as2d(t) for t in (w, g.reshape(shape), m, v))
    rows, cols = w2.shape
    tr = 512 if rows % 512 == 0 else rows
    c1 = 1.0 - ADAM_B1 ** ADAM_STEP
    c2 = 1.0 - ADAM_B2 ** ADAM_STEP

    def body(w_ref, g_ref, m_ref, v_ref, d_ref, nm_ref, nv_ref):
        gv = g_ref[...]
        nm = ADAM_B1 * m_ref[...] + (1.0 - ADAM_B1) * gv
        nv = ADAM_B2 * v_ref[...] + (1.0 - ADAM_B2) * (gv * gv)
        nm_ref[...] = nm
        nv_ref[...] = nv
        d_ref[...] = -ADAM_LR * ((nm / c1) / (jnp.sqrt(nv / c2) + ADAM_EPS) + ADAM_WD * w_ref[...])

    blk = pl.BlockSpec((tr, cols), lambda i: (i, 0))
    outs = _call(body, (w2, g2, m2, v2), grid=(rows // tr,), in_specs=[blk] * 4, out_specs=[blk] * 3,
                 out_shape=[_sds((rows, cols), F32)] * 3, name=name)
    return tuple(t.reshape(shape) for t in outs)


def _sum_lead(x, name, out_dtype=F32):
    n, rows, cols = x.shape
    tr = 512 if rows % 512 == 0 else rows

    def body(x_ref, o_ref):
        acc = x_ref[0].astype(F32)
        for k in range(1, n):
            acc = acc + x_ref[k].astype(F32)
        o_ref[...] = acc.astype(out_dtype)

    return pl.pallas_call(
        body, grid=(rows // tr,), in_specs=[pl.BlockSpec((n, tr, cols), lambda i: (0, i, 0))],
        out_specs=pl.BlockSpec((tr, cols), lambda i: (i, 0)), out_shape=_sds((rows, cols), out_dtype),
        name=name, compiler_params=_cparams("parallel"),
    )(x)


def _seg_vecs(mod_l, which, nseg):
    return mod_l[:nseg, which][:, None, :]


def _norm_grads(dshift, dgeff, dgate, g, scale):
    nseg, _, d = dshift.shape
    dmod = jnp.stack([dshift[:, 0], dgeff[:, 0] * g, dgate[:, 0]], axis=1)
    if nseg == 1:
        dmod = jnp.concatenate([dmod, jnp.zeros((1, 3, d), F32)], axis=0)
    dg = jnp.sum(dgeff[:, 0] * (1.0 + scale[:, 0]), axis=0)
    return dmod, dg


def _pool_layer(xin, g, mod_l, w_in, w_grp, w_out, pscale, nx, tag, ctx=None, head=None):
    nseg = 1 if ctx is None else 2
    shift, scale, gate = (_seg_vecs(mod_l, k, nseg) for k in range(3))
    *joined, h, r, uv = _norm_w_in(xin, g, scale, shift, w_in, nx, f"w_in_fwd_{tag}", ctx)
    if joined:
        xin, = joined
    z, mixed, a = _pool_grp_fwd(uv, w_grp, pscale, nx, f"pool_fwd_{tag}")
    if head is None:
        yx, xout = _w_out_resid(a, w_out, xin, gate, nx, f"w_out_fwd_{tag}")
    else:
        yx, *xout = _w_out_loss(a, w_out, xin, gate, *head, f"w_out_loss_{tag}")

    def backward(dxo, token=None):
        gate_b = gate if token is None else gate + token[0, 0]
        dyx, da, dgate = _gate_w_out_bwd(dxo, yx, gate_b, w_out, nx, f"w_out_bwd_{tag}")
        gw_out = _mm_tn(a, dyx, f"w_out_grad_{tag}", BF16)
        dm, duv, dscale = _pool_grp_bwd(da, mixed, uv, pscale, w_grp, nx, f"pool_bwd_{tag}")
        gw_grp = _grp_wgrad(z, dm, w_grp.shape[0], f"grp_grad_{tag}", BF16)
        gw_in = _mm_tn_parts(h, duv, f"w_in_grad_{tag}", BF16)
        dx, dshift, dgeff = _w_in_bwd_norm(duv, w_in, xin, r, g, scale, dxo, nx, f"w_in_bwd_{tag}",
                                           dx_rows=None if ctx is None else nx)
        dmod, dg = _norm_grads(dshift, dgeff, dgate, g[0], scale)
        return dx, dmod, dg, dict(w_in=gw_in, w_grp=gw_grp, w_out=gw_out, scale=dscale)

    return xout, backward


def _na_layer(xc, g, mod_l, w_in, rpb, w_out, nx, mask, comm=None):
    nh, n_dr, n_dc = rpb.shape
    shift, scale = _seg_vecs(mod_l, 0, 2), _seg_vecs(mod_l, 1, 2)
    gate = _seg_vecs(mod_l, 2, 1)
    h, r, p4 = _norm_w_in(xc, g, scale, shift, w_in, nx, "w_in_fwd_na")
    rp = jnp.pad(rpb, ((0, 0), (1, _RP_ROWS - 1 - n_dr), (0, _PAIR - n_dc)))
    (a, o, lse), carried = _attn_fwd(p4, rp, mask, nx, "attn_fwd", comm)
    yx, xout = _w_out_resid(a, w_out, xc, gate, nx, "w_out_fwd_na")

    def backward(dxo, comm=None):
        dyx, da, dgate = _gate_w_out_bwd(dxo, yx, gate, w_out, nx, "w_out_bwd_na")
        gw_out = _mm_tn(a, dyx, "w_out_grad_na", BF16)
        (d4, drp), carried_bwd = _attn_bwd(p4, rp, mask, o, lse, da, nx, "attn_bwd", comm)
        gw_in = _mm_tn_parts(h, d4, "w_in_grad_na", BF16)
        dx, dshift, dgeff = _w_in_bwd_norm(d4, w_in, xc, r, g, scale, dxo, nx, "w_in_bwd_na")
        dgate2 = jnp.concatenate([dgate, jnp.zeros_like(dgate)], axis=0)
        dmod, dg = _norm_grads(dshift, dgeff, dgate2, g[0], scale)
        drpb = drp[:, 1:1 + n_dr, ::-1][:, :, :n_dc]
        return dx, dmod, dg, dict(w_in=gw_in, w_out=gw_out, rpb=drpb), carried_bwd

    return xout, backward, carried


def _conv_layer(xin, g, mod_l, w_in, dw, db, w_out):
    shift, scale, gate = (_seg_vecs(mod_l, k, 1) for k in range(3))
    nx = xin.shape[0]
    h, r, p4 = _norm_w_in(xin, g, scale, shift, w_in, nx, "w_in_fwd_conv")
    a = _conv_fwd(p4, dw, db, "conv_fwd")
    yx, xout = _w_out_resid(a, w_out, xin, gate, nx, "w_out_fwd_conv")

    def backward(dxo):
        dyx, da, dgate = _gate_w_out_bwd(dxo, yx, gate, w_out, nx, "w_out_bwd_conv")
        gw_out = _mm_tn(a, dyx, "w_out_grad_conv", BF16)
        d4, ddw, ddb = _conv_bwd(da, p4, dw, db, "conv_bwd")
        gw_in = _mm_tn_parts(h, d4, "w_in_grad_conv", BF16)
        dx, dshift, dgeff = _w_in_bwd_norm(d4, w_in, xin, r, g, scale, dxo, nx, "w_in_bwd_conv")
        dmod, dg = _norm_grads(dshift, dgeff, dgate, g[0], scale)
        return dx, dmod, dg, dict(w_in=gw_in, w_out=gw_out, dw=ddw, db=ddb)

    return xout, backward


def _example_step(x, ctx, target, mod, norm_g, final_g, wts, hooks=None):
    hooks = hooks or {}
    na_weights, late_comm, late_weights = (hooks.get(k) for k in ("na_weights", "late_comm", "late_weights"))
    nx = x.shape[0]
    consts = _attn_mask()
    g_rows = [norm_g[i:i + 1] for i in range(4)]
    xc1, bwd0 = _pool_layer(x, g_rows[0], mod[0], wts["pool_w_in"][0], wts["pool_w_grp"][0],
                            wts["pool_w_out"][0], wts["pool_scale"][0:1], nx, "p0", ctx=ctx)
    if na_weights is not None:
        wts = {**wts, **na_weights(xc1)}
    x2, bwd1, carried = _na_layer(xc1, g_rows[1], mod[1], wts["na_w_in"], wts["na_rpb"], wts["na_w_out"], nx, consts,
                                  late_comm)
    if late_weights is not None:
        wts = {**wts, **late_weights(carried)}
    x3, bwd2 = _conv_layer(x2, g_rows[2], mod[2], wts["conv_w_in"], wts["conv_dw"], wts["conv_db"], wts["conv_w_out"])
    (loss, dx4, dfinal_g), bwd3 = _pool_layer(x3, g_rows[3], mod[3], wts["pool_w_in"][1], wts["pool_w_grp"][1],
                                              wts["pool_w_out"][1], wts["pool_scale"][1:2], nx, "p3",
                                              head=(final_g, target))
    call = lambda k, *args: hooks[k](*args) if k in hooks else None
    dx3, dmod3, dg3, gr3 = bwd3(dx4)
    dx2, dmod2, dg2, gr2 = bwd2(dx3)
    dxc1, dmod1, dg1, gr1, carried_bwd = bwd1(dx2, call("grad_comm", gr3, gr2))
    dx0, dmod0, dg0, gr0 = bwd0(dxc1, call("na_grads_start", gr1))
    return dict(
        loss=loss, grad_x=dx0, dmod=jnp.stack([dmod0, dmod1, dmod2, dmod3]),
        dnorm_g=jnp.stack([dg0, dg1, dg2, dg3]), dfinal_g=dfinal_g, layers=(gr0, gr1, gr2, gr3), carried=carried_bwd)


_AXES = ("x", "y", "c")
_CHIP_FLIPS = ((1, 0), (0, 1), (1, 1))


def _position():
    return tuple(lax.axis_index(a) for a in _AXES)


def _flipped(pos, flip):
    return tuple(1 - p if f else p for p, f in zip(pos, flip))


def _join_comms(comms):
    n_in = [len(c.ins) for c in comms]
    n_out = [len(c.outs) for c in comms]
    n_sem = [len(c.sems) for c in comms]

    def parts(ins, outs, sems):
        for k in range(len(comms)):
            a, b, s = sum(n_in[:k]), sum(n_out[:k]), sum(n_sem[:k])
            yield comms[k], (ins[a:a + n_in[k]], outs[b:b + n_out[k]], sems[s:s + n_sem[k]])

    def start(ins, outs, sems):
        for c, part in parts(ins, outs, sems):
            c.start(*part)

    def finish(ins, outs, sems):
        for c, part in parts(ins, outs, sems):
            c.finish(*part)

    joint = _Comm([a for c in comms for a in c.ins], [o for c in comms for o in c.outs],
                  [s for c in comms for s in c.sems], start, finish)
    return joint, lambda res: [list(res[sum(n_out[:k]):sum(n_out[:k + 1])]) for k in range(len(comms))]


def _run_comms(comms, name):
    joint, split = _join_comms(comms)

    def body(*refs):
        n_in, n_out = len(joint.ins), len(joint.outs)
        joint.start(refs[:n_in], refs[n_in:n_in + n_out], refs[n_in + n_out:])
        joint.finish(refs[:n_in], refs[n_in:n_in + n_out], refs[n_in + n_out:])

    res = pl.pallas_call(
        body, in_specs=[HBM_SPEC] * len(joint.ins), out_specs=[HBM_SPEC] * len(joint.outs), out_shape=joint.outs,
        scratch_shapes=joint.sems, name=name,
    )(*joint.ins)
    return split(res)


def _all_gather_comm(v, axes):
    flips = [f for f in np.ndindex(2, 2, 2) if any(f) and all(a in axes or not b for a, b in zip(_AXES, f))]
    n = len(flips) + 1

    def copies(ins, outs, sems):
        (v_ref,), (o_ref,), (send_sems, recv_sems, local_sem) = ins, outs, sems
        pos = _position()
        slot = 0
        for a, p in zip(_AXES, pos):
            if a in axes:
                slot = 2 * slot + p
        local = pltpu.make_async_copy(v_ref, o_ref.at[slot], local_sem)
        remote = [pltpu.make_async_remote_copy(v_ref, o_ref.at[slot], send_sems.at[k], recv_sems.at[k],
                                               device_id=_flipped(pos, flip), device_id_type=MESH)
                  for k, flip in enumerate(flips)]
        return [local] + remote

    def start(ins, outs, sems):
        for cp in copies(ins, outs, sems):
            cp.start()

    def finish(ins, outs, sems):
        for cp in copies(ins, outs, sems):
            cp.wait()

    sems = [pltpu.SemaphoreType.DMA((n - 1,)), pltpu.SemaphoreType.DMA((n - 1,)), pltpu.SemaphoreType.DMA(())]
    return _Comm([v], [_sds((n,) + v.shape, v.dtype)], sems, start, finish)


def _all_gather_two_level_comm(v):
    def copies(ins, outs, sems, onward):
        (v_ref,), (o_ref,), (send_sems, recv_sems, local_sem) = ins, outs, sems
        x, y, c = _position()
        sibling = (x, y, 1 - c)
        slot = lambda px, py, pc: o_ref.at[4 * px + 2 * py + pc]
        own = pltpu.make_async_copy(v_ref, slot(x, y, c), local_sem)
        first = [pltpu.make_async_remote_copy(v_ref, slot(x, y, c), send_sems.at[0], recv_sems.at[0],
                                              device_id=sibling, device_id_type=MESH)]
        fwd = []
        for k, flip in enumerate(_CHIP_FLIPS):
            px, py = _flipped((x, y), flip)
            first.append(pltpu.make_async_remote_copy(v_ref, slot(x, y, c), send_sems.at[1 + k], recv_sems.at[1 + k],
                                                      device_id=(px, py, c), device_id_type=MESH))
            if onward:
                fwd.append(pltpu.make_async_remote_copy(slot(px, py, c), slot(px, py, c), send_sems.at[4 + k],
                                                        recv_sems.at[4 + k], device_id=sibling, device_id_type=MESH))
        return own, first, fwd

    def start(ins, outs, sems):
        own, first, _ = copies(ins, outs, sems, False)
        for cp in [own] + first:
            cp.start()

    def finish(ins, outs, sems):
        own, first, fwd = copies(ins, outs, sems, True)
        for arrived, onward in zip(first[1:], fwd):
            arrived.wait_recv()
            onward.start()
        first[0].wait_recv()
        for cp in fwd:
            cp.wait_recv()
        for cp in first + fwd:
            cp.wait_send()
        own.wait()

    sems = [pltpu.SemaphoreType.DMA((7,)), pltpu.SemaphoreType.DMA((7,)), pltpu.SemaphoreType.DMA(())]
    return _Comm([v], [_sds((8,) + v.shape, v.dtype)], sems, start, finish)


def _all_gather(v, axes, name):
    return _run_comms([_all_gather_comm(v, axes)], name)[0][0]


class _Item:
    def __init__(self, key, layer, shape, shard_axis, half_axis):
        self.key, self.layer, self.shape = key, layer, tuple(shape)
        self.shard_axis, self.half_axis = shard_axis, half_axis
        self.shard = shape[shard_axis] // 4
        self.half = shape[half_axis] // 2

    def sized(self, shard=False, half=False):
        s = list(self.shape)
        if shard:
            s[self.shard_axis] = self.shard
        if half:
            s[self.half_axis] = self.half
        return tuple(s)

    def window(self, ref, chip=None, half=None):
        idx = [slice(None)] * len(self.shape)
        if chip is not None:
            idx[self.shard_axis] = pl.ds(chip * self.shard, self.shard)
        if half is not None:
            idx[self.half_axis] = pl.ds(half * self.half, self.half)
        return ref.at[tuple(idx)]


def _items(d, w):
    out = []
    for j in range(2):
        out += [_Item("pool_w_in", j, (d, 2 * w), 1, 0), _Item("pool_w_grp", j, (4, w // 4, w // 4), 1, 0),
                _Item("pool_w_out", j, (w, d), 0, 1)]
    out += [_Item("na_w_in", 0, (d, 4 * w), 1, 0), _Item("na_w_out", 0, (w, d), 0, 1),
            _Item("conv_w_in", 0, (d, 4 * w), 1, 0), _Item("conv_w_out", 0, (w, d), 0, 1)]
    return out


def _gather_comm(shards, items):
    n = len(items)

    def copies(src, dst, sems, onward):
        send_a, recv_a, send_b, recv_b, send_c, recv_c = sems
        x, y, c = _position()
        chip = 2 * x + y
        sibling = (x, y, 1 - c)
        own, out, fwd, fwd_in = [], [], [], []
        for i, it in enumerate(items):
            own.append(pltpu.make_async_remote_copy(src[i], it.window(dst[i], chip=chip), send_c.at[i], recv_c.at[i],
                                                    device_id=sibling, device_id_type=MESH))
            for k, flip in enumerate(_CHIP_FLIPS):
                px, py = _flipped((x, y), flip)
                s = 3 * i + k
                out.append(pltpu.make_async_remote_copy(
                    it.window(src[i], half=c), it.window(dst[i], chip=chip, half=c), send_a.at[s], recv_a.at[s],
                    device_id=(px, py, c), device_id_type=MESH))
                if onward:
                    got = it.window(dst[i], chip=2 * px + py, half=c)
                    fwd.append(pltpu.make_async_remote_copy(got, got, send_b.at[s], recv_b.at[s],
                                                            device_id=sibling, device_id_type=MESH))
                    other = it.window(dst[i], chip=2 * px + py, half=1 - c)
                    fwd_in.append(pltpu.make_async_remote_copy(other, other, send_b.at[s], recv_b.at[s],
                                                               device_id=sibling, device_id_type=MESH))
        return own, out, fwd, fwd_in

    def start(src, dst, sems):
        own, out, _, _ = copies(src, dst, sems, False)
        for cp in own + out:
            cp.start()

    def finish(src, dst, sems):
        own, out, fwd, fwd_in = copies(src, dst, sems, True)
        for arrived, onward in zip(out, fwd):
            arrived.wait_recv()
            onward.start()
        for cp in fwd_in:
            cp.wait_recv()
        for cp in out + fwd:
            cp.wait_send()
        for cp in own:
            cp.wait()

    sems = [pltpu.SemaphoreType.DMA((3 * n,)) for _ in range(4)] + [pltpu.SemaphoreType.DMA((n,)) for _ in range(2)]
    return _Comm(shards, [_sds(it.shape, BF16) for it in items], sems, start, finish)


def _pair_swap_copies(windows):
    def copies(src, got, sems):
        send_sems, recv_sems = sems
        x, y, c = _position()
        return [pltpu.make_async_remote_copy(windows[i](src[i], 1 - c), got[i], send_sems.at[i], recv_sems.at[i],
                                             device_id=(x, y, 1 - c), device_id_type=MESH)
                for i in range(len(windows))]

    return copies


def _pair_swap_comm(arrays, windows, out_shapes):
    n = len(arrays)
    copies = _pair_swap_copies(windows)

    def start(src, got, sems):
        for cp in copies(src, got, sems):
            cp.start()

    def finish(src, got, sems):
        for cp in copies(src, got, sems):
            cp.wait()

    return _Comm(arrays, out_shapes, [pltpu.SemaphoreType.DMA((n,)), pltpu.SemaphoreType.DMA((n,))], start, finish)


def _pair_swap(arrays, windows, out_shapes, name):
    return _run_comms([_pair_swap_comm(arrays, windows, out_shapes)], name)[0]


def _chip_exchange_copies(items):
    def copies(src, dst, sems):
        send_sems, recv_sems = sems
        x, y, c = _position()
        out = []
        for i, it in enumerate(items):
            for k, flip in enumerate(_CHIP_FLIPS):
                px, py = _flipped((x, y), flip)
                out.append(pltpu.make_async_remote_copy(
                    it.window(src[i], chip=2 * px + py), dst[i].at[k], send_sems.at[3 * i + k],
                    recv_sems.at[3 * i + k], device_id=(px, py, c), device_id_type=MESH))
        return out

    return copies


_SEM_SPEC = pl.BlockSpec(memory_space=pltpu.SEMAPHORE)
_DATAFLOW = pltpu.SideEffectType.DATAFLOW_SIDE_EFFECTING


def _split_start(copies, srcs, zones, n_copies, name):
    n, nz = len(srcs), len(zones)

    def body(*refs):
        src, land = refs[:n], refs[n:n + nz]
        send_sems, recv_sems = refs[n + nz:n + nz + 2]
        token = refs[-1]
        for cp in copies(src, land, (send_sems, recv_sems)):
            cp.start()
        token[...] = jnp.zeros(token.shape, F32)

    hbm = lambda t: pltpu.HBM(t.shape, t.dtype)
    res = pl.pallas_call(
        body, name=name,
        out_shape=(pltpu.SemaphoreType.DMA((n_copies,)), pltpu.SemaphoreType.DMA((n_copies,)),
                   *[hbm(t) for t in list(srcs) + list(zones)], _sds((8, 128), F32)),
        in_specs=[HBM_SPEC] * (n + nz),
        out_specs=(_SEM_SPEC, _SEM_SPEC, *[HBM_SPEC] * (n + nz), pl.BlockSpec(memory_space=pltpu.VMEM)),
        input_output_aliases={i: 2 + i for i in range(n + nz)},
        compiler_params=pltpu.CompilerParams(has_side_effects=_DATAFLOW),
    )(*[pltpu.with_memory_space_constraint(t, pltpu.HBM) for t in list(srcs) + list(zones)])
    return (res[0], res[1], list(res[2:2 + n]), list(res[2 + n:2 + n + nz])), res[-1]


def _split_wait(copies, handle, after, name):
    send_sems, recv_sems, srcs, zones = handle
    n, nz = len(srcs), len(zones)

    def body(*refs):
        src, land = refs[:n], refs[n:n + nz]
        send, recv = refs[n + nz:n + nz + 2]
        for cp in copies(src, land, (send, recv)):
            cp.wait_send()
            cp.wait_recv()

    hbm = lambda t: pltpu.HBM(t.shape, t.dtype)
    res = pl.pallas_call(
        body, name=name, out_shape=tuple(hbm(t) for t in list(srcs) + list(zones)),
        in_specs=[HBM_SPEC] * (n + nz) + [_SEM_SPEC, _SEM_SPEC, pl.BlockSpec(memory_space=pl.ANY)],
        out_specs=tuple([HBM_SPEC] * (n + nz)), input_output_aliases={i: i for i in range(n + nz)},
        compiler_params=pltpu.CompilerParams(has_side_effects=_DATAFLOW),
    )(*srcs, *zones, send_sems, recv_sems, after)
    return list(res[:n]), list(res[n:])


def _gather_ici_copies(items):
    def copies(src, dst, sems):
        send_sems, recv_sems = sems
        x, y, c = _position()
        chip = 2 * x + y
        out = []
        for i, it in enumerate(items):
            for k, flip in enumerate(_CHIP_FLIPS):
                px, py = _flipped((x, y), flip)
                out.append(pltpu.make_async_remote_copy(
                    it.window(src[i], half=c), it.window(dst[i], chip=chip, half=c), send_sems.at[3 * i + k],
                    recv_sems.at[3 * i + k], device_id=(px, py, c), device_id_type=MESH))
        return out

    return copies


def _gather_pair_finish(shards, mats, items, name):
    n = len(items)

    def body(*refs):
        src, dst = refs[:n], refs[2 * n:3 * n]
        send_own, recv_own, send_fwd, recv_fwd = refs[3 * n:]
        x, y, c = _position()
        chip = 2 * x + y
        sibling = (x, y, 1 - c)
        copies = []
        for i, it in enumerate(items):
            copies.append(pltpu.make_async_remote_copy(src[i], it.window(dst[i], chip=chip), send_own.at[i],
                                                       recv_own.at[i], device_id=sibling, device_id_type=MESH))
            for k, flip in enumerate(_CHIP_FLIPS):
                px, py = _flipped((x, y), flip)
                got = it.window(dst[i], chip=2 * px + py, half=c)
                copies.append(pltpu.make_async_remote_copy(got, got, send_fwd.at[3 * i + k], recv_fwd.at[3 * i + k],
                                                           device_id=sibling, device_id_type=MESH))
        for cp in copies:
            cp.start()
        for cp in copies:
            cp.wait()

    return pl.pallas_call(
        body, in_specs=[HBM_SPEC] * (2 * n), out_specs=[HBM_SPEC] * n, out_shape=[_sds(it.shape, BF16) for it in items],
        input_output_aliases={n + i: i for i in range(n)},
        scratch_shapes=[pltpu.SemaphoreType.DMA((n,)), pltpu.SemaphoreType.DMA((n,)),
                        pltpu.SemaphoreType.DMA((3 * n,)), pltpu.SemaphoreType.DMA((3 * n,))], name=name,
    )(*shards, *mats)


def _chip_exchange_comm(partials, items):
    n = len(items)
    copies = _chip_exchange_copies(items)

    def start(src, dst, sems):
        for cp in copies(src, dst, sems):
            cp.start()

    def finish(src, dst, sems):
        for cp in copies(src, dst, sems):
            cp.wait()

    return _Comm(partials, [_sds((3,) + it.sized(shard=True, half=True), BF16) for it in items],
                 [pltpu.SemaphoreType.DMA((3 * n,)), pltpu.SemaphoreType.DMA((3 * n,))], start, finish)


_SUM_STEPS = 2


def _pair_sums(gs, gots, its, pos, name):
    n = len(its)
    nb = _SUM_STEPS
    g2 = [g.reshape(-1, g.shape[-1]) for g in gs]
    got2 = [t.reshape(-1, t.shape[-1]) for t in gots]

    def body(pos_ref, *refs):
        for g_ref, got_ref, o_ref in zip(refs[:n], refs[n:2 * n], refs[2 * n:]):
            o_ref[...] = (g_ref[...].astype(F32) + got_ref[...].astype(F32)).astype(BF16)

    g_specs, got_specs = [], []
    for it, t in zip(its, got2):
        rows, cols = t.shape
        blk = (rows // nb, cols)
        g_map = (lambda i, pos: (pos[1] * nb + i, 0)) if it.half_axis == 0 else (lambda i, pos: (i, pos[1]))
        g_specs.append(pl.BlockSpec(blk, g_map))
        got_specs.append(pl.BlockSpec(blk, lambda i, pos: (i, 0)))
    outs = pl.pallas_call(
        body, grid_spec=pltpu.PrefetchScalarGridSpec(
            num_scalar_prefetch=1, grid=(nb,), in_specs=g_specs + got_specs, out_specs=got_specs),
        out_shape=[_sds(t.shape, BF16) for t in got2], name=name, compiler_params=_cparams("parallel"),
    )(pos, *g2, *got2)
    return [o.reshape(t.shape) for o, t in zip(outs, gots)]


_FLIP_SLOT = {2: 0, 1: 1, 3: 2}


def _chip_sums(pairs, slots, its, pos, name):
    n = len(its)
    nb = _SUM_STEPS

    def body(pos_ref, *refs):
        chip = pos_ref[0]
        for own in range(4):
            @pl.when(chip == own)
            def _():
                for p_ref, s_ref, o_ref in zip(refs[:n], refs[n:2 * n], refs[2 * n:]):
                    acc = None
                    for k in range(4):
                        v = (p_ref[...] if k == own else s_ref[_FLIP_SLOT[own ^ k]]).astype(F32)
                        acc = v if acc is None else acc + v
                    o_ref[...] = acc

    p_specs, s_specs, o_specs, shapes = [], [], [], []
    for it in its:
        shape = it.sized(shard=True, half=True)
        blk = (shape[0] // nb,) + shape[1:]
        rest = (0,) * (len(shape) - 1)

        def p_map(i, pos, it=it, nd=len(shape)):
            lead = i + (pos[0] * nb if it.shard_axis == 0 else 0)
            return (lead,) + tuple(pos[0] if ax == it.shard_axis else 0 for ax in range(1, nd))

        p_specs.append(pl.BlockSpec(blk, p_map))
        s_specs.append(pl.BlockSpec((3,) + blk, lambda i, pos, rest=rest: (0, i) + rest))
        o_specs.append(pl.BlockSpec(blk, lambda i, pos, rest=rest: (i,) + rest))
        shapes.append(_sds(shape, F32))
    return pl.pallas_call(
        body, grid_spec=pltpu.PrefetchScalarGridSpec(
            num_scalar_prefetch=1, grid=(nb,), in_specs=p_specs + s_specs, out_specs=o_specs),
        out_shape=shapes, name=name, compiler_params=_cparams("parallel"),
    )(pos, *pairs, *slots)


_GRAD_KEYS = ("pool_w_in", "pool_w_grp", "pool_w_out", "na_w_in", "na_w_out", "conv_w_in", "conv_w_out")


def _adamw_matrix(w, m, v, owns, others, it, pos, name):
    nl = w.shape[0]
    rows_split = it.half_axis == 0
    r, cdim = int(np.prod(w.shape[1:-1])), w.shape[-1]
    hr, hc = (r // 2, cdim) if rows_split else (r, cdim // 2)
    br = min(hr, 256)
    nb = hr // br
    c1 = 1.0 - ADAM_B1 ** ADAM_STEP
    c2 = 1.0 - ADAM_B2 ** ADAM_STEP

    def body(pos_ref, w_ref, m_ref, v_ref, *rest):
        own_refs, other_refs = rest[:nl], rest[nl:2 * nl]
        g_ref, d_ref, nm_ref, nv_ref = rest[2 * nl:]
        j, h = pl.program_id(0), pl.program_id(1)
        own, other = own_refs[0][...], other_refs[0][...]
        for q in range(1, nl):
            own = jnp.where(j == q, own_refs[q][...], own)
            other = jnp.where(j == q, other_refs[q][...], other)
        gv = jnp.where(h == pos_ref[1], own, other)
        nm = ADAM_B1 * m_ref[...] + (1.0 - ADAM_B1) * gv
        nv = ADAM_B2 * v_ref[...] + (1.0 - ADAM_B2) * (gv * gv)
        g_ref[...] = gv
        nm_ref[...] = nm
        nv_ref[...] = nv
        d_ref[...] = -ADAM_LR * ((nm / c1) / (jnp.sqrt(nv / c2) + ADAM_EPS) + ADAM_WD * w_ref[...])

    if rows_split:
        full = pl.BlockSpec((None, br, hc), lambda j, h, i, pos: (j, h * nb + i, 0))
    else:
        full = pl.BlockSpec((None, br, hc), lambda j, h, i, pos: (j, i, h))
    half = pl.BlockSpec((br, hc), lambda j, h, i, pos: (i, 0))
    flat = lambda t: t.reshape(nl, r, cdim)
    outs = pl.pallas_call(
        body, grid_spec=pltpu.PrefetchScalarGridSpec(
            num_scalar_prefetch=1, grid=(nl, 2, nb), in_specs=[full] * 3 + [half] * (2 * nl), out_specs=[full] * 4),
        out_shape=[_sds((nl, r, cdim), F32)] * 4, name=name,
        compiler_params=_cparams("parallel", "parallel", "parallel"),
    )(pos, flat(w), flat(m), flat(v), *[t.reshape(hr, hc) for t in list(owns) + list(others)])
    return tuple(t.reshape(w.shape) for t in outs)


_WEIGHTS = ("c_ctx", "norm_g", "ada_w", "ada_b", "pool_w_in", "pool_w_grp", "pool_scale", "pool_w_out", "na_w_in",
            "na_rpb", "na_w_out", "conv_w_in", "conv_dw", "conv_db", "conv_w_out", "final_g")
_COND_ROWS = 16


def _modulations(cond, ada_w, ada_b_cols):
    nl, d, n = ada_w.shape
    return _matmul(
        cond, ada_w, mode="nn", grid=(nl, 1), a_silu=True, epilogue="bias",
        a_spec=pl.BlockSpec((_COND_ROWS, d), lambda i, j: (0, 0)), b_spec=pl.BlockSpec((None, d, n), lambda i, j: (i, 0, 0)),
        extra=(ada_b_cols,), extra_specs=(pl.BlockSpec((None, 1, n), lambda i, j: (i, 0, 0)),),
        out_shapes=[_sds((nl, _COND_ROWS, n), F32)], out_specs=[pl.BlockSpec((None, _COND_ROWS, n), lambda i, j: (i, 0, 0))],
        name="modulations")[0]


def _ada_w_step(cond, dm_cols, w, m, v):
    nl, d, n = w.shape
    tr = d // 2
    c1 = 1.0 - ADAM_B1 ** ADAM_STEP
    c2 = 1.0 - ADAM_B2 ** ADAM_STEP

    def body(c_ref, dm_ref, w_ref, m_ref, v_ref, g_ref, d_ref, nm_ref, nv_ref):
        gv = lax.dot_general(_silu(c_ref[...]).astype(BF16), dm_ref[...].astype(BF16), _DIMS["tn"],
                             preferred_element_type=F32)
        nm = ADAM_B1 * m_ref[...] + (1.0 - ADAM_B1) * gv
        nv = ADAM_B2 * v_ref[...] + (1.0 - ADAM_B2) * (gv * gv)
        g_ref[...] = gv
        nm_ref[...] = nm
        nv_ref[...] = nv
        d_ref[...] = -ADAM_LR * ((nm / c1) / (jnp.sqrt(nv / c2) + ADAM_EPS) + ADAM_WD * w_ref[...])

    blk = pl.BlockSpec((None, tr, n), lambda l, i: (l, i, 0))
    return _call(
        body, (cond, dm_cols, w, m, v), grid=(nl, d // tr),
        in_specs=[pl.BlockSpec((_COND_ROWS, tr), lambda l, i: (0, i)),
                  pl.BlockSpec((None, _COND_ROWS, n), lambda l, i: (l, 0, 0)), blk, blk, blk],
        out_specs=[blk] * 4, out_shape=[_sds(w.shape, F32)] * 4, name="adamw_ada_w")


def _cond_grad(dm_cols, ada_w):
    nl, d, n = ada_w.shape
    return _matmul(
        dm_cols, ada_w, mode="nt", grid=(1, nl), nk=nl, acc_shape=(_COND_ROWS, d),
        a_spec=pl.BlockSpec((None, _COND_ROWS, n), lambda i, q: (q, 0, 0)), b_spec=pl.BlockSpec((None, d, n), lambda i, q: (q, 0, 0)),
        out_shapes=[_sds((_COND_ROWS, d), F32)], out_specs=[pl.BlockSpec((_COND_ROWS, d), lambda i, q: (0, 0))],
        name="cond_grad")[0]


def _pack(parts):
    flat = [p.reshape(-1) for p in parts]
    sizes = [f.shape[0] for f in flat]
    total = sum(sizes)
    rows = -(-total // 1024) * 8
    packed = jnp.concatenate(flat + [jnp.zeros((rows * 128 - total,), F32)]).reshape(rows, 128)
    offs = np.concatenate([[0], np.cumsum(sizes)])[:-1]
    return packed, [(int(o), p.shape) for o, p in zip(offs, parts)]


def _unpack(flat, layout, k):
    off, shape = layout[k]
    return flat[..., off:off + int(np.prod(shape))].reshape(flat.shape[:-1] + tuple(shape))


def kernel(x, c, ctx, c_ctx, norm_g, ada_w, ada_b, pool_w_in, pool_w_grp, pool_scale, pool_w_out, na_w_in, na_rpb, na_w_out, conv_w_in, conv_dw, conv_db, conv_w_out, final_g, loss_target, m_c_ctx, m_norm_g, m_ada_w, m_ada_b, m_pool_w_in, m_pool_w_grp, m_pool_scale, m_pool_w_out, m_na_w_in, m_na_rpb, m_na_w_out, m_conv_w_in, m_conv_dw, m_conv_db, m_conv_w_out, m_final_g, v_c_ctx, v_norm_g, v_ada_w, v_ada_b, v_pool_w_in, v_pool_w_grp, v_pool_scale, v_pool_w_out, v_na_w_in, v_na_rpb, v_na_w_out, v_conv_w_in, v_conv_dw, v_conv_db, v_conv_w_out, v_final_g):
    params = dict(c_ctx=c_ctx, norm_g=norm_g, ada_w=ada_w, ada_b=ada_b, pool_w_in=pool_w_in, pool_w_grp=pool_w_grp,
                  pool_scale=pool_scale, pool_w_out=pool_w_out, na_w_in=na_w_in, na_rpb=na_rpb, na_w_out=na_w_out,
                  conv_w_in=conv_w_in, conv_dw=conv_dw, conv_db=conv_db, conv_w_out=conv_w_out, final_g=final_g)
    mom1 = dict(c_ctx=m_c_ctx, norm_g=m_norm_g, ada_w=m_ada_w, ada_b=m_ada_b, pool_w_in=m_pool_w_in,
                pool_w_grp=m_pool_w_grp, pool_scale=m_pool_scale, pool_w_out=m_pool_w_out, na_w_in=m_na_w_in,
                na_rpb=m_na_rpb, na_w_out=m_na_w_out, conv_w_in=m_conv_w_in, conv_dw=m_conv_dw, conv_db=m_conv_db,
                conv_w_out=m_conv_w_out, final_g=m_final_g)
    mom2 = dict(c_ctx=v_c_ctx, norm_g=v_norm_g, ada_w=v_ada_w, ada_b=v_ada_b, pool_w_in=v_pool_w_in,
                pool_w_grp=v_pool_w_grp, pool_scale=v_pool_scale, pool_w_out=v_pool_w_out, na_w_in=v_na_w_in,
                na_rpb=v_na_rpb, na_w_out=v_na_w_out, conv_w_in=v_conv_w_in, conv_dw=v_conv_dw, conv_db=v_conv_db,
                conv_w_out=v_conv_w_out, final_g=v_final_g)
    d = x.shape[-1]
    w = na_w_out.shape[1] * 4
    xi, yi, ci = _position()
    chip = 2 * xi + yi
    dev = 2 * chip + ci
    n_ada = ada_w.shape[-1]

    def chip_cols(a, size):
        return lax.dynamic_slice_in_dim(a, chip * size, size, axis=a.ndim - 1)

    items = _items(d, w)
    first = [it for it in items if it.key.startswith("pool") and it.layer == 0]
    na = [it for it in items if it.key.startswith("na")]
    late = [it for it in items if it not in first + na]
    shards_of = lambda its: [params[it.key][it.layer].astype(BF16) for it in its]
    empties = lambda its: [lax.empty(it.shape, BF16) for it in its]
    first_copies, na_copies = _gather_ici_copies(first), _gather_ici_copies(na)

    conds = _all_gather(c.reshape(8, d // 8), _AXES, "gather_cond").reshape(8, d)
    behind = conds[0, 0] * 0.0
    first_handle, token = _split_start(first_copies, [s + behind.astype(BF16) for s in shards_of(first)],
                                       empties(first), 3 * len(first), "gather_first_start")
    cond = jnp.concatenate([conds + token[0, 0], c_ctx[None], jnp.zeros((_COND_ROWS - 9, d), F32)], axis=0)
    mod_cols = _modulations(cond, ada_w, chip_cols(ada_b, n_ada)[:, None, :])
    small_pack, small_layout = _pack([pool_scale, conv_dw, conv_db])
    (mod_all,), (small,) = _run_comms([_all_gather_comm(mod_cols, ("x", "y")),
                                       _all_gather_comm(small_pack, ("x", "y"))], "gather_mod")
    behind = mod_all[0, 0, 0, 0] * 0.0
    na_handle, token = _split_start(na_copies, [s + behind.astype(BF16) for s in shards_of(na)], empties(na),
                                    3 * len(na), "gather_na_start")
    first_shards, first_mats = _split_wait(first_copies, first_handle, token, "gather_first_wait")
    first_mats = _gather_pair_finish(first_shards, first_mats, first, "gather_first_pair")
    mod_all = mod_all.transpose(1, 2, 0, 3).reshape(4, _COND_ROWS, 3, d)
    mod = jnp.stack([lax.dynamic_index_in_dim(mod_all, dev, axis=1, keepdims=False), mod_all[:, 8]], axis=1)
    full = {(it.key, it.layer): mat for it, mat in zip(first, first_mats)}
    late_comm = _gather_comm(shards_of(late), late)

    def na_weights(after):
        na_shards, na_mats = _split_wait(na_copies, na_handle, after, "gather_na_wait")
        na_mats = _gather_pair_finish(na_shards, na_mats, na, "gather_na_pair")
        return {it.key: mat for it, mat in zip(na, na_mats)}

    def late_weights(mats):
        full.update({(it.key, it.layer): mat for it, mat in zip(late, mats)})
        return dict(pool_w_in=[full[("pool_w_in", j)] for j in range(2)],
                    pool_w_grp=[full[("pool_w_grp", j)] for j in range(2)],
                    pool_w_out=[full[("pool_w_out", j)] for j in range(2)],
                    conv_w_in=full[("conv_w_in", 0)], conv_w_out=full[("conv_w_out", 0)])

    small = small.reshape(4, -1)

    def whole(k):
        parts = _unpack(small, small_layout, k)
        return jnp.moveaxis(parts, 0, -2).reshape(parts.shape[1:-1] + (-1,))

    wts = dict(pool_w_in=[full[("pool_w_in", 0)]], pool_w_grp=[full[("pool_w_grp", 0)]],
               pool_w_out=[full[("pool_w_out", 0)]], pool_scale=whole(0), na_rpb=na_rpb[0], conv_dw=whole(1)[0],
               conv_db=whole(2))
    pos = jnp.stack([chip, ci]).astype(jnp.int32)

    def layer_grads(its, by_layer):
        pick = {"pool_w_in": "w_in", "pool_w_grp": "w_grp", "pool_w_out": "w_out", "na_w_in": "w_in",
                "na_w_out": "w_out", "conv_w_in": "w_in", "conv_w_out": "w_out"}
        return [by_layer[(it.key.split("_")[0], it.layer)][pick[it.key]] for it in its]

    pairs, handles = dict(), dict()
    half_windows = lambda its: [(lambda ref, half, it=it: it.window(ref, half=half)) for it in its]
    half_shapes = lambda its: [_sds(it.sized(half=True), BF16) for it in its]

    def pair_sums(its, mats, tag):
        got = _pair_swap(mats, half_windows(its), half_shapes(its), f"pair_exchange_{tag}")
        return _pair_sums(mats, got, its, pos, f"pair_sum_{tag}")

    def grad_comm(gr3, gr2):
        pairs["late"] = pair_sums(late, layer_grads(late, {("pool", 1): gr3, ("conv", 0): gr2}), "late")
        return _chip_exchange_comm(pairs["late"], late)

    slot_zones = lambda its: [lax.empty((3,) + it.sized(shard=True, half=True), BF16) for it in its]
    na_xcopies, first_xcopies = _chip_exchange_copies(na), _chip_exchange_copies(first)

    def na_grads_start(gr1):
        pairs["na"] = pair_sums(na, layer_grads(na, {("na", 0): gr1}), "na")
        handles["na"], started = _split_start(na_xcopies, pairs["na"], slot_zones(na), 3 * len(na),
                                              "exchange_na_start")
        return started

    res = _example_step(x[0], ctx[0], loss_target[0], mod, norm_g, final_g[None], wts, dict(
        na_weights=na_weights, late_comm=late_comm, late_weights=late_weights, grad_comm=grad_comm,
        na_grads_start=na_grads_start))
    g0, g1, g2, g3 = res["layers"]
    pairs["na"], na_slots = _split_wait(na_xcopies, handles["na"], g0["w_in"], "exchange_na_wait")
    first_grads = layer_grads(first, {("pool", 0): g0})
    packed, layout = _pack([res["dfinal_g"], res["dnorm_g"], res["dmod"], g1["rpb"],
                            jnp.concatenate([g0["scale"], g3["scale"]], axis=0), g2["dw"], g2["db"],
                            res["loss"][0, :1]])
    first_got, (every,) = _run_comms([_pair_swap_comm(first_grads, half_windows(first), half_shapes(first)),
                                      _all_gather_two_level_comm(packed)], "pair_exchange_first")
    pairs["first"] = _pair_sums(first_grads, first_got, first, pos, "pair_sum_first")

    grads = dict()
    total = _sum_lead(every, "sum_vec_grads").reshape(-1)
    every = every.reshape(8, -1)
    grads["final_g"] = _unpack(total, layout, 0).reshape(final_g.shape)
    grads["norm_g"] = _unpack(total, layout, 1)
    grads["na_rpb"] = _unpack(total, layout, 3)[None]
    grads["pool_scale"] = chip_cols(_unpack(total, layout, 4), pool_scale.shape[-1])
    grads["conv_dw"] = chip_cols(_unpack(total, layout, 5), conv_dw.shape[-1])[None]
    grads["conv_db"] = chip_cols(_unpack(total, layout, 6), conv_db.shape[-1])
    dmod_sum = _unpack(total, layout, 2).reshape(4, 2, 3 * d)
    dmod_each = _unpack(every, layout, 2).reshape(8, 4, 2, 3 * d)
    grads["ada_b"] = dmod_sum[:, 0] + dmod_sum[:, 1]
    dm = jnp.concatenate([dmod_each[:, :, 0].transpose(1, 0, 2), dmod_sum[:, 1][:, None],
                          jnp.zeros((4, _COND_ROWS - 9, 3 * d), F32)], axis=1)
    dm_cols = chip_cols(dm, n_ada)
    dcond = _cond_grad(dm_cols, ada_w)[8].reshape(8, d // 8)
    dcond_all = _all_gather(dcond, ("x", "y"), "gather_cond_grad")
    behind = dcond_all[0, 0, 0] * 0.0
    handles["first"], token = _split_start(first_xcopies, [p + behind.astype(BF16) for p in pairs["first"]],
                                           slot_zones(first), 3 * len(first), "exchange_first_start")
    grads["ada_w"], *ada_w_step = _ada_w_step(cond, dm_cols + token[0, 0], ada_w, m_ada_w, v_ada_w)
    grads["c_ctx"] = _sum_lead(dcond_all, "sum_cond_grad").reshape(d) * _dsilu(c_ctx)
    vector_out = {k: _adamw(params[k], grads[k], mom1[k], mom2[k], f"adamw_{k}")
                  for k in _WEIGHTS if k not in _GRAD_KEYS + ("ada_w",)}
    vector_out["ada_w"] = tuple(ada_w_step)
    pairs["first"], first_slots = _split_wait(first_xcopies, handles["first"], vector_out["ada_w"][2],
                                              "exchange_first_wait")

    slots = dict(zip(late, res["carried"]))
    slots.update(zip(first, first_slots))
    slots.update(zip(na, na_slots))
    pair_of = dict(zip(late, pairs["late"]))
    pair_of.update(zip(first, pairs["first"]))
    pair_of.update(zip(na, pairs["na"]))
    reduced = _chip_sums([pair_of[it] for it in items], [slots[it] for it in items], items, pos, "chip_sum")
    theirs = _pair_swap(reduced, [lambda ref, half: ref] * len(items),
                        [_sds(t.shape, F32) for t in reduced], "pair_return")
    matrix_out = dict()
    for k in _GRAD_KEYS:
        idx = [i for i, it in enumerate(items) if it.key == k]
        res_k = _adamw_matrix(params[k], mom1[k], mom2[k], [reduced[i] for i in idx], [theirs[i] for i in idx],
                              items[idx[0]], pos, f"adamw_{k}")
        grads[k], matrix_out[k] = res_k[0], res_k[1:]

    outs = [[], [], []]
    for k in _WEIGHTS:
        step = matrix_out[k] if k in matrix_out else vector_out[k]
        for lst, val in zip(outs, step):
            lst.append(val)
    loss = _unpack(total, layout, 7)[0]
    return (loss, res["grad_x"][None], *[grads[k].reshape(params[k].shape) for k in _WEIGHTS],
            *outs[0], *outs[1], *outs[2])
```

```python
import functools

import numpy as np
import jax
import jax.numpy as jnp
from jax import lax
from jax.experimental import pallas as pl
from jax.experimental.pallas import tpu as pltpu

F32 = jnp.float32
BF16 = jnp.bfloat16

EPS = 1e-6
GRID_W = 64
HEAD_DIM = 64
WIN_ROWS = 8
WIN_COLS = 16
POOL_WINDOWS = (2, 4, 8, 16)
Q_ROWS = 4
K_ROWS = 12
PAD_ROWS = 4
NEG = -1e30

ADAM_LR = 0.001
ADAM_B1 = 0.9
ADAM_B2 = 0.999
ADAM_EPS = 1e-08
ADAM_WD = 0.01
ADAM_STEP = 10

ROW_BLOCK = 256
VMEM_LIMIT = 56 * 1024 * 1024
ACT = BF16

MESH = pl.DeviceIdType.MESH
HBM_SPEC = pl.BlockSpec(memory_space=pltpu.HBM)


def _cparams(*sem):
    return pltpu.CompilerParams(dimension_semantics=sem or None, vmem_limit_bytes=VMEM_LIMIT)


def _sds(shape, dtype):
    return jax.ShapeDtypeStruct(tuple(shape), dtype)


def _call(body, args, *, grid, in_specs, out_specs, out_shape, name, scratch_shapes=()):
    return list(pl.pallas_call(
        body, grid=grid, in_specs=list(in_specs), out_specs=list(out_specs), out_shape=list(out_shape),
        scratch_shapes=list(scratch_shapes), name=name, compiler_params=_cparams(*(("arbitrary",) * len(grid))),
    )(*args))


def _sigmoid(x):
    return 1.0 / (1.0 + jnp.exp(-x))


def _silu(x):
    return x * _sigmoid(x)


def _dsilu(x):
    s = _sigmoid(x)
    return s * (1.0 + x * (1.0 - s))


_DIMS = {
    "nn": (((1,), (0,)), ((), ())),
    "nt": (((1,), (1,)), ((), ())),
    "tn": (((0,), (0,)), ((), ())),
}


def _matmul(a, b, *, mode, grid, a_spec, b_spec, out_shapes, out_specs, name, nk=1,
            a_silu=False, exact=False, epilogue=None, extra=(), extra_specs=(), acc_shape=None):
    n_extra = len(extra)
    n_out = len(out_shapes)

    def body(*refs):
        a_ref, b_ref = refs[:2]
        ex = refs[2:2 + n_extra]
        outs = refs[2 + n_extra:2 + n_extra + n_out]
        av = a_ref[...]
        bv = b_ref[...]
        if a_silu:
            av = _silu(av.astype(F32))
        if exact:
            prod = lax.dot_general(av.astype(F32), bv.astype(F32), _DIMS[mode],
                                   precision=lax.Precision.HIGHEST, preferred_element_type=F32)
        else:
            prod = lax.dot_general(av.astype(BF16), bv.astype(BF16), _DIMS[mode], preferred_element_type=F32)

        def finish(res):
            if epilogue == "bias":
                res = res + ex[0][...]
            outs[0][...] = res.astype(outs[0].dtype)

        if nk == 1:
            finish(prod)
        else:
            acc = refs[-1]
            k = pl.program_id(len(grid) - 1)

            @pl.when(k == 0)
            def _():
                acc[...] = prod

            @pl.when(k > 0)
            def _():
                acc[...] += prod

            @pl.when(k == nk - 1)
            def _():
                finish(acc[...])

    scratch = [pltpu.VMEM(acc_shape, F32)] if nk > 1 else []
    sem = ("parallel",) * (len(grid) - 1) + ("arbitrary",)
    return pl.pallas_call(
        body, grid=grid, in_specs=[a_spec, b_spec, *extra_specs], out_specs=list(out_specs),
        out_shape=list(out_shapes), scratch_shapes=scratch, name=name, compiler_params=_cparams(*sem),
    )(a, b, *extra)


def _row_tile(rows):
    for t in (768, 512, 256):
        if rows % t == 0:
            return t
    return rows


def _mm_tn(a, b, name, out_dtype, tm=512):
    r, m = a.shape
    n = b.shape[1]
    tm = min(tm, m)
    tn = min(1024, n)
    return _matmul(
        a, b, mode="tn", grid=(m // tm, n // tn),
        a_spec=pl.BlockSpec((r, tm), lambda i, j: (0, i)), b_spec=pl.BlockSpec((r, tn), lambda i, j: (0, j)),
        out_shapes=[_sds((m, n), out_dtype)], out_specs=[pl.BlockSpec((tm, tn), lambda i, j: (i, j))], name=name)[0]


def _mm_tn_parts(a, b, name, out_dtype, tm=512):
    r, m = a.shape
    p, _, np_ = b.shape
    tm = min(tm, m)
    return _matmul(
        a, b, mode="tn", grid=(m // tm, p),
        a_spec=pl.BlockSpec((r, tm), lambda i, q: (0, i)), b_spec=pl.BlockSpec((None, r, np_), lambda i, q: (q, 0, 0)),
        out_shapes=[_sds((m, p * np_), out_dtype)], out_specs=[pl.BlockSpec((tm, np_), lambda i, q: (i, q))],
        name=name)[0]


def _row_vec(ref, is_ctx):
    return ref[0] if is_ctx is None else jnp.where(is_ctx, ref[1], ref[0])


def _ctx_rows(i, tm, nx, nseg):
    if nseg == 1:
        return None
    return i * tm + lax.broadcasted_iota(jnp.int32, (tm, 1), 0) >= nx


def _seg_sums(ref, val, is_ctx, first):
    if is_ctx is None:
        parts = [jnp.sum(val, axis=0, keepdims=True)]
    else:
        parts = [jnp.sum(jnp.where(is_ctx, 0.0, val), axis=0, keepdims=True),
                 jnp.sum(jnp.where(is_ctx, val, 0.0), axis=0, keepdims=True)]

    @pl.when(first)
    def _():
        for k, p in enumerate(parts):
            ref[k] = p

    @pl.when(jnp.logical_not(first))
    def _():
        for k, p in enumerate(parts):
            ref[k] += p


def _w_out_resid(a, w_out, xres, gate, nx, name):
    m, k = a.shape
    n = w_out.shape[1]
    nseg = gate.shape[0]
    tm = _row_tile(m)

    def body(a_ref, w_ref, x_ref, gt_ref, yx_ref, xo_ref):
        yx = jnp.dot(a_ref[...], w_ref[...], preferred_element_type=F32)
        yx_ref[...] = yx.astype(ACT)
        xo_ref[...] = x_ref[...] + _row_vec(gt_ref, _ctx_rows(pl.program_id(0), tm, nx, nseg)) * yx

    row = pl.BlockSpec((tm, n), lambda i: (i, 0))
    return pl.pallas_call(
        body, grid=(m // tm,),
        in_specs=[pl.BlockSpec((tm, k), lambda i: (i, 0)), pl.BlockSpec((k, n), lambda i: (0, 0)), row,
                  pl.BlockSpec((nseg, 1, n), lambda i: (0, 0, 0))],
        out_specs=[row, row], out_shape=[_sds((m, n), ACT), _sds((m, n), F32)],
        name=name, compiler_params=_cparams("parallel"),
    )(a, w_out, xres, gate)


def _norm_w_in(x, g, scale, shift, w_in, nx, name, ctx=None):
    d = x.shape[1]
    rows = x.shape[0] + (0 if ctx is None else ctx.shape[0])
    n = w_in.shape[1]
    nseg = scale.shape[0]
    tm = _row_tile(rows)
    tn = min(2048, n)
    row = pl.BlockSpec((tm, d), lambda i, j: (i, 0))
    if ctx is None:
        row_args, row_specs = (x,), [row]
    else:
        assert ctx.shape[0] == ROW_BLOCK and tm % ROW_BLOCK == 0 and nx % ROW_BLOCK == 0
        nsub, x_blocks = tm // ROW_BLOCK, nx // ROW_BLOCK
        row_args = (x,) * nsub + (ctx,)
        row_specs = [pl.BlockSpec((ROW_BLOCK, d), lambda i, j, s=s: (jnp.minimum(i * nsub + s, x_blocks - 1), 0))
                     for s in range(nsub)] + [pl.BlockSpec((ROW_BLOCK, d), lambda i, j: (0, 0))]

    def body(*refs):
        x_refs, (g_ref, sc_ref, sh_ref, w_ref), outs = refs[:len(row_args)], refs[len(row_args):][:4], refs[-3:]
        h_ref, r_ref, p_ref = outs
        i, j = pl.program_id(0), pl.program_id(1)

        @pl.when(j == 0)
        def _():
            if ctx is None:
                xv = x_refs[0][...]
            else:
                xv = jnp.concatenate([jnp.where(i * nsub + s >= x_blocks, x_refs[-1][...], x_refs[s][...])
                                      for s in range(nsub)], axis=0)
                refs[-4][...] = xv
            r = lax.rsqrt(jnp.mean(xv * xv, axis=-1, keepdims=True) + EPS)
            is_ctx = _ctx_rows(i, tm, nx, nseg)
            h = (xv * r) * g_ref[...] * (1.0 + _row_vec(sc_ref, is_ctx)) + _row_vec(sh_ref, is_ctx)
            h_ref[...] = h.astype(BF16)
            r_ref[...] = r

        p_ref[...] = jnp.dot(h_ref[...], w_ref[...], preferred_element_type=F32).astype(ACT)

    vec = pl.BlockSpec((nseg, 1, d), lambda i, j: (0, 0, 0))
    joined = [] if ctx is None else [(row, _sds((rows, d), F32))]
    out_specs, out_shape = zip(*joined, (row, _sds((rows, d), BF16)),
                               (pl.BlockSpec((tm, 1), lambda i, j: (i, 0)), _sds((rows, 1), F32)),
                               (pl.BlockSpec((tm, tn), lambda i, j: (i, j)), _sds((rows, n), ACT)))
    return _call(
        body, (*row_args, g, scale, shift, w_in), grid=(rows // tm, n // tn),
        in_specs=[*row_specs, pl.BlockSpec((1, d), lambda i, j: (0, 0)), vec, vec,
                  pl.BlockSpec((d, tn), lambda i, j: (0, j))],
        out_specs=list(out_specs), out_shape=list(out_shape), name=name)


def _gate_w_out_bwd(dxo, yx, gate, w_out, nx, name):
    rows, d = yx.shape
    w = w_out.shape[0]
    nseg = gate.shape[0]
    tm = _row_tile(rows)

    def body(dx_ref, yx_ref, gt_ref, w_ref, dyx_ref, da_ref, dg_ref):
        i = pl.program_id(0)
        is_ctx = _ctx_rows(i, tm, nx, nseg)
        dxv = dx_ref[...]
        dyx = (dxv * _row_vec(gt_ref, is_ctx)).astype(BF16)
        dyx_ref[...] = dyx
        da_ref[...] = lax.dot_general(dyx, w_ref[...], _DIMS["nt"], preferred_element_type=F32).astype(ACT)
        _seg_sums(dg_ref, dxv * yx_ref[...].astype(F32), is_ctx, i == 0)

    row = pl.BlockSpec((tm, d), lambda i: (i, 0))
    vec = pl.BlockSpec((nseg, 1, d), lambda i: (0, 0, 0))
    return _call(
        body, (dxo, yx, gate, w_out), grid=(rows // tm,),
        in_specs=[row, row, vec, pl.BlockSpec((w, d), lambda i: (0, 0))],
        out_specs=[row, pl.BlockSpec((tm, w), lambda i: (i, 0)), vec],
        out_shape=[_sds((rows, d), BF16), _sds((rows, w), ACT), _sds((nseg, 1, d), F32)], name=name)


def _w_in_bwd_norm(dparts, w_in, x, r, g, scale, dres, nx, name, dx_rows=None):
    np_, rows, kp = dparts.shape
    d = w_in.shape[0]
    nseg = scale.shape[0]
    tm = _row_tile(rows)
    assert dx_rows is None or rows - tm < dx_rows <= rows
    nsub = tm // ROW_BLOCK
    nres_blocks = dres.shape[0] // ROW_BLOCK
    pp = 2 if np_ % 2 == 0 else 1
    nk = np_ // pp

    def body(dp_ref, w_ref, x_ref, r_ref, g_ref, sc_ref, *rest):
        dres_refs = rest[:nsub]
        dx_ref, dsh_ref, dge_ref, *acc = rest[nsub:]
        i, k = pl.program_id(0), pl.program_id(1)
        prod = sum(lax.dot_general(dp_ref[q], w_ref[:, q * kp:(q + 1) * kp], _DIMS["nt"], preferred_element_type=F32)
                   for q in range(pp))

        def finish(dhv):
            is_ctx = _ctx_rows(i, tm, nx, nseg)
            rv = r_ref[...]
            xn = x_ref[...] * rv
            dxn = dhv * (g_ref[...] * (1.0 + _row_vec(sc_ref, is_ctx)))
            dx = rv * (dxn - xn * jnp.mean(dxn * xn, axis=-1, keepdims=True))
            for s in range(nsub):
                piece = slice(s * ROW_BLOCK, (s + 1) * ROW_BLOCK)
                res = dres_refs[s][...]
                if nres_blocks * ROW_BLOCK < rows:
                    res = jnp.where(i * nsub + s < nres_blocks, res, 0.0)
                dx_ref[piece, :] = dx[piece, :] + res
            _seg_sums(dsh_ref, dhv, is_ctx, i == 0)
            _seg_sums(dge_ref, dhv * xn, is_ctx, i == 0)

        if nk == 1:
            finish(prod)
        else:
            acc_ref, = acc

            @pl.when(k == 0)
            def _():
                acc_ref[...] = prod

            @pl.when(k > 0)
            def _():
                acc_ref[...] += prod

            @pl.when(k == nk - 1)
            def _():
                finish(acc_ref[...])

    row = pl.BlockSpec((tm, d), lambda i, k: (i, 0))
    vec = pl.BlockSpec((nseg, 1, d), lambda i, k: (0, 0, 0))
    return _call(
        body, (dparts, w_in, x, r, g, scale, *([dres] * nsub)), grid=(rows // tm, nk),
        in_specs=[pl.BlockSpec((pp, tm, kp), lambda i, k: (k, i, 0)), pl.BlockSpec((d, pp * kp), lambda i, k: (0, k)),
                  row, pl.BlockSpec((tm, 1), lambda i, k: (i, 0)), pl.BlockSpec((1, d), lambda i, k: (0, 0)), vec]
        + [pl.BlockSpec((ROW_BLOCK, d), (lambda i, k, s=s: (jnp.minimum(i * nsub + s, nres_blocks - 1), 0)))
           for s in range(nsub)],
        out_specs=[row, vec, vec],
        out_shape=[_sds((dx_rows or rows, d), F32), _sds((nseg, 1, d), F32), _sds((nseg, 1, d), F32)],
        scratch_shapes=[pltpu.VMEM((tm, d), F32)] * (nk > 1), name=name)


_PAD_TOP = 16
_PAD_BOT = 32


def _window_sum(buf, xv, lo, n):
    t = xv.shape[0]
    c = xv.shape[1]
    tp = t + _PAD_TOP + _PAD_BOT
    buf[pl.ds(0, _PAD_TOP), :] = jnp.zeros((_PAD_TOP, c), F32)
    buf[pl.ds(_PAD_TOP, t), :] = xv
    buf[pl.ds(_PAD_TOP + t, _PAD_BOT), :] = jnp.zeros((_PAD_BOT, c), F32)
    p = buf[...]
    k = 1
    while k < n:
        p = p + pltpu.roll(p, tp - k, 0)
        k *= 2
    if lo:
        p = pltpu.roll(p, -lo, 0)
    buf[...] = p
    return buf[pl.ds(_PAD_TOP, t), :]


def _window_count(t, half):
    pos = lax.broadcasted_iota(jnp.int32, (t, 1), 0)
    return (jnp.minimum(pos + half, t) - jnp.maximum(pos - half, 0)).astype(F32)


def _segments(rows, nx):
    return [(0, nx)] + ([(nx, rows - nx)] if rows > nx else [])


def _pool_scratch(rows, nx, cols):
    return [pltpu.VMEM((length + _PAD_TOP + _PAD_BOT, cols), F32) for _, length in _segments(rows, nx)]


def _per_group(g, fn):
    for gi, win in enumerate(POOL_WINDOWS):
        pl.when(g == gi)(functools.partial(fn, win))


def _pool_grp_fwd(uv, w_grp, scale, nx, name):
    rows = uv.shape[0]
    ng, gc, _ = w_grp.shape
    w = ng * gc
    segs = _segments(rows, nx)

    def body(u_ref, gt_ref, w_ref, sc_ref, z_ref, mx_ref, a_ref, *bufs):
        def pool(win):
            half = win // 2
            for (start, length), buf in zip(segs, bufs):
                uvv = u_ref[pl.ds(start, length), :].astype(F32)
                s = _window_sum(buf, uvv, -half, win)
                z_ref[pl.ds(start, length), :] = (s / _window_count(length, half) - uvv).astype(BF16)

        _per_group(pl.program_id(0), pool)
        mixed = jnp.dot(z_ref[...], w_ref[...], preferred_element_type=F32)
        mx_ref[...] = mixed.astype(ACT)
        a_ref[...] = (mixed * sc_ref[...] * _silu(gt_ref[...].astype(F32))).astype(BF16)

    col = pl.BlockSpec((rows, gc), lambda g: (0, g))
    return _call(
        body, (uv, uv, w_grp, scale), grid=(ng,),
        in_specs=[col, pl.BlockSpec((rows, gc), lambda g: (0, ng + g)), pl.BlockSpec((None, gc, gc), lambda g: (g, 0, 0)),
                  pl.BlockSpec((1, gc), lambda g: (0, g))],
        out_specs=[col, col, col], out_shape=[_sds((rows, w), BF16), _sds((rows, w), ACT), _sds((rows, w), BF16)],
        scratch_shapes=_pool_scratch(rows, nx, gc), name=name)


def _pool_grp_bwd(da, mixed, uv, scale, w_grp, nx, name):
    rows, w = da.shape
    ng, gc, _ = w_grp.shape
    segs = _segments(rows, nx)

    def body(da_ref, mx_ref, gt_ref, sc_ref, w_ref, dm_ref, duv_ref, dsc_ref, dz_ref, *bufs):
        dav = da_ref[...].astype(F32)
        mixed = mx_ref[...].astype(F32)
        gt = gt_ref[...].astype(F32)
        sg = _silu(gt)
        sc = sc_ref[...]
        dm = (dav * sc * sg).astype(BF16)
        dm_ref[...] = dm
        dz_ref[...] = lax.dot_general(dm, w_ref[...], _DIMS["nt"], preferred_element_type=F32)
        duv_ref[1] = (dav * mixed * sc * _dsilu(gt)).astype(BF16)
        dsc_ref[...] = jnp.sum(dav * mixed * sg, axis=0, keepdims=True)

        def unpool(win):
            half = win // 2
            for (start, length), buf in zip(segs, bufs):
                dzv = dz_ref[pl.ds(start, length), :]
                s = _window_sum(buf, dzv / _window_count(length, half), 1 - half, win)
                duv_ref[0, pl.ds(start, length), :] = (s - dzv).astype(BF16)

        _per_group(pl.program_id(0), unpool)

    col = pl.BlockSpec((rows, gc), lambda g: (0, g))
    vec = pl.BlockSpec((1, gc), lambda g: (0, g))
    return pl.pallas_call(
        body, grid=(ng,),
        in_specs=[col, col, pl.BlockSpec((rows, gc), lambda g: (0, ng + g)), vec,
                  pl.BlockSpec((None, gc, gc), lambda g: (g, 0, 0))],
        out_specs=[col, pl.BlockSpec((2, rows, gc), lambda g: (0, 0, g)), vec],
        out_shape=[_sds((rows, w), BF16), _sds((2, rows, w), BF16), _sds((1, w), F32)],
        scratch_shapes=[pltpu.VMEM((rows, gc), F32)] + _pool_scratch(rows, nx, gc),
        name=name, compiler_params=_cparams("parallel"),
    )(da, mixed, uv, scale, w_grp)


def _grp_wgrad(z, dm, ng, name, out_dtype):
    rows, w = z.shape
    gc = w // ng

    def body(z_ref, dm_ref, o_ref):
        o_ref[...] = lax.dot_general(z_ref[...], dm_ref[...], _DIMS["tn"],
                                     preferred_element_type=F32).astype(o_ref.dtype)

    blk = pl.BlockSpec((rows, gc), lambda g: (0, g))
    return pl.pallas_call(
        body, grid=(ng,), in_specs=[blk, blk], out_specs=pl.BlockSpec((None, gc, gc), lambda g: (g, 0, 0)),
        out_shape=_sds((ng, gc, gc), out_dtype), name=name, compiler_params=_cparams("parallel"),
    )(z, dm)


def _shift_rows(v, by):
    t = v.shape[0]
    pos = lax.broadcasted_iota(jnp.int32, v.shape, 0)
    rolled = pltpu.roll(v, by % t, 0)
    keep = pos >= by if by > 0 else pos < t + by
    return jnp.where(keep, rolled, 0.0)


def _conv_specs(t, w, cb):
    return [pl.BlockSpec((t, cb), (lambda j, q=q: (0, q * (w // cb) + j))) for q in range(4)]


def _conv_fwd(p4, dw, db, name):
    t = p4.shape[0]
    w = p4.shape[1] // 4
    cb = 128

    def body(bg_ref, cg_ref, v_ref, g_ref, dw_ref, db_ref, a_ref):
        tv = cg_ref[...].astype(F32) * v_ref[...].astype(F32)
        conv = (dw_ref[0:1, :] * _shift_rows(tv, 1) + dw_ref[1:2, :] * tv + dw_ref[2:3, :] * _shift_rows(tv, -1)
                + db_ref[...])
        a_ref[...] = (bg_ref[...].astype(F32) * conv * _silu(g_ref[...].astype(F32))).astype(BF16)

    return pl.pallas_call(
        body, grid=(w // cb,),
        in_specs=_conv_specs(t, w, cb) + [pl.BlockSpec((3, cb), lambda j: (0, j)), pl.BlockSpec((1, cb), lambda j: (0, j))],
        out_specs=pl.BlockSpec((t, cb), lambda j: (0, j)), out_shape=_sds((t, w), BF16),
        name=name, compiler_params=_cparams("parallel"),
    )(p4, p4, p4, p4, dw, db)


def _conv_bwd(da, p4, dw, db, name):
    t, w = da.shape
    cb = 128

    def body(da_ref, bg_ref, cg_ref, v_ref, g_ref, dw_ref, db_ref, d4_ref, ddw_ref, ddb_ref):
        cg = cg_ref[...].astype(F32)
        vv = v_ref[...].astype(F32)
        bg = bg_ref[...].astype(F32)
        gv = g_ref[...].astype(F32)
        tv = cg * vv
        tm1 = _shift_rows(tv, 1)
        tp1 = _shift_rows(tv, -1)
        w0, w1, w2 = dw_ref[0:1, :], dw_ref[1:2, :], dw_ref[2:3, :]
        conv = w0 * tm1 + w1 * tv + w2 * tp1 + db_ref[...]
        y = bg * conv
        dav = da_ref[...].astype(F32)
        dy = dav * _silu(gv)
        d4_ref[3] = (dav * y * _dsilu(gv)).astype(BF16)
        d4_ref[0] = (dy * conv).astype(BF16)
        dconv = dy * bg
        ddb_ref[...] = jnp.sum(dconv, axis=0, keepdims=True)
        ddw_ref[0:1, :] = jnp.sum(dconv * tm1, axis=0, keepdims=True)
        ddw_ref[1:2, :] = jnp.sum(dconv * tv, axis=0, keepdims=True)
        ddw_ref[2:3, :] = jnp.sum(dconv * tp1, axis=0, keepdims=True)
        dt = w0 * _shift_rows(dconv, -1) + w1 * dconv + w2 * _shift_rows(dconv, 1)
        d4_ref[1] = (dt * vv).astype(BF16)
        d4_ref[2] = (dt * cg).astype(BF16)

    col = pl.BlockSpec((t, cb), lambda j: (0, j))
    tap = pl.BlockSpec((3, cb), lambda j: (0, j))
    bias = pl.BlockSpec((1, cb), lambda j: (0, j))
    return pl.pallas_call(
        body, grid=(w // cb,), in_specs=[col] + _conv_specs(t, w, cb) + [tap, bias],
        out_specs=[pl.BlockSpec((4, t, cb), lambda j: (0, 0, j)), tap, bias],
        out_shape=[_sds((4, t, w), BF16), _sds((3, w), F32), _sds((1, w), F32)],
        name=name, compiler_params=_cparams("parallel"),
    )(da, p4, p4, p4, p4, dw, db)


def _attn_mask():
    qn, kn = Q_ROWS * GRID_W, K_ROWS * GRID_W
    qr, qc = np.divmod(np.arange(qn), GRID_W)
    kr, kc = np.divmod(np.arange(kn), GRID_W)
    col0 = np.clip(qc - WIN_COLS // 2, 0, GRID_W - WIN_COLS)
    col_ok = (kc[None, :] >= col0[:, None]) & (kc[None, :] < col0[:, None] + WIN_COLS)
    first = np.zeros(qn, np.int64)
    last = np.full(qn, K_ROWS - WIN_ROWS)
    out = []
    for row0 in (first, qr, last):
        row_ok = (kr[None, :] >= row0[:, None]) & (kr[None, :] < row0[:, None] + WIN_ROWS)
        out.append(np.where(row_ok & col_ok, 0.0, NEG))
    return jnp.asarray(np.stack(out), F32)


_KW = K_ROWS * GRID_W
_QB = Q_ROWS * GRID_W
_PAIR = 2 * HEAD_DIM
_N_DR = 2 * WIN_ROWS - 1
_N_DC = 2 * WIN_COLS - 1
_RP_ROWS = 24
_N_TILES = _N_DR + 1
_BIAS_BASE = (WIN_ROWS - 1, WIN_ROWS // 2 - 1, -1)


class _Comm:
    def __init__(self, ins, outs, sems, start, finish):
        self.ins, self.outs, self.sems, self.start, self.finish = list(ins), list(outs), list(sems), start, finish


def _bias_pieces(cls):
    out = []
    for qr in range(Q_ROWS):
        for kr in range(0, K_ROWS, 2):
            tile = _BIAS_BASE[cls] - qr + kr + 1
            out.append((qr, kr, tile if 0 <= tile < _N_TILES else None))
    return out


def _toeplitz_pair(left_row, right_row):
    lane = lax.broadcasted_iota(jnp.int32, (GRID_W, _PAIR), 1)
    shape = (GRID_W, _PAIR)
    left = pltpu.roll(jnp.broadcast_to(left_row, shape), _PAIR - (WIN_COLS - 1), 1, stride=1, stride_axis=0)
    right = pltpu.roll(jnp.broadcast_to(right_row, shape), GRID_W - (WIN_COLS - 1), 1, stride=1, stride_axis=0)
    return jnp.where(lane < GRID_W, left, right)


def _build_tiles(tiles_ref, rp_ref):
    for h in range(2):
        for t in range(_N_TILES):
            tiles_ref[h, t] = _toeplitz_pair(rp_ref[h, t:t + 1, :], rp_ref[h, t + 1:t + 2, :])


def _block_class(b, nblk, fn, entering=False):
    interior = (b == 1) if entering else jnp.logical_and(b > 0, b < nblk - 1)
    for cls, cond in enumerate((b == 0, interior, b == nblk - 1)):
        pl.when(cond)(functools.partial(fn, cls))


def _attn_geometry(p4, nx):
    rows = p4.shape[0]
    w = p4.shape[1] // 4
    nhp = w // _PAIR
    nblk = nx // _QB
    qspec = lambda col: pl.BlockSpec((_QB, _PAIR), lambda hp, b: (b, col * nhp + hp))
    kspec = lambda col: pl.BlockSpec((rows, _PAIR), lambda hp, b: (0, col * nhp + hp))
    tspec = pl.BlockSpec((2, _RP_ROWS, _PAIR), lambda hp, b: (hp, 0, 0))
    mspec = pl.BlockSpec((None, _QB, _KW), lambda hp, b: (jnp.where(b == 0, 0, jnp.where(b == nblk - 1, 2, 1)), 0, 0))
    lspec = pl.BlockSpec((None, _QB, 2), lambda hp, b: (hp, b, 0))
    ospec = pl.BlockSpec((_QB, _PAIR), lambda hp, b: (b, hp))
    return rows, w, nhp, nblk, qspec, kspec, tspec, mspec, lspec, ospec


def _window_start(b, nx):
    return pl.multiple_of(jnp.clip(b * _QB - PAD_ROWS * GRID_W, 0, nx - _KW), _QB)


def _load_bias(bias_ref, tiles_ref, rp_ref, m_ref, b, nblk):
    pl.when(b == 0)(lambda: _build_tiles(tiles_ref, rp_ref))

    def fill(cls):
        for h in range(2):
            for qr, kr, tile in _bias_pieces(cls):
                rows = slice(qr * GRID_W, (qr + 1) * GRID_W)
                cols = slice(kr * GRID_W, (kr + 2) * GRID_W)
                m = m_ref[rows, cols]
                bias_ref[h, rows, cols] = m if tile is None else tiles_ref[h, tile] + m

    _block_class(b, nblk, fill, entering=True)


def _attn_fwd(p4, rp, mask, nx, name, comm=None):
    rows, w, nhp, nblk, qspec, kspec, tspec, mspec, lspec, ospec = _attn_geometry(p4, nx)
    n_ctx = rows - nx
    n_cin, n_cout = (len(comm.ins), len(comm.outs)) if comm else (0, 0)

    def body(*refs):
        q_ref, k_ref, v_ref, g_ref, rp_ref, m_ref = refs[:6]
        cin = refs[6:6 + n_cin]
        a_ref, o_ref, lse_ref = refs[6 + n_cin:9 + n_cin]
        cout = refs[9 + n_cin:9 + n_cin + n_cout]
        bias_ref, tiles_ref = refs[9 + n_cin + n_cout:11 + n_cin + n_cout]
        sems = refs[11 + n_cin + n_cout:]
        hp, b = pl.program_id(0), pl.program_id(1)
        if comm:
            pl.when(jnp.logical_and(hp == 0, b == 0))(lambda: comm.start(cin, cout, sems))
        start = _window_start(b, nx)
        _load_bias(bias_ref, tiles_ref, rp_ref, m_ref, b, nblk)
        qf = q_ref[...].astype(F32) * HEAD_DIM ** -0.5
        kw = k_ref[pl.ds(start, _KW), :].astype(BF16)
        vw = v_ref[pl.ds(start, _KW), :].astype(BF16)
        kcv = k_ref[pl.ds(nx, n_ctx), :].astype(BF16)
        vcv = v_ref[pl.ds(nx, n_ctx), :].astype(BF16)
        lane = lax.broadcasted_iota(jnp.int32, (1, _PAIR), 1)
        outs, lses = [], []
        for h in range(2):
            mine = (lane >= HEAD_DIM) if h else (lane < HEAD_DIM)
            qm = jnp.where(mine, qf, 0.0).astype(BF16)
            s_loc = lax.dot_general(qm, kw, _DIMS["nt"], preferred_element_type=F32) + bias_ref[h]
            s_ctx = lax.dot_general(qm, kcv, _DIMS["nt"], preferred_element_type=F32)
            mx = jnp.maximum(jnp.max(s_loc, axis=-1, keepdims=True), jnp.max(s_ctx, axis=-1, keepdims=True))
            p_loc = jnp.exp(s_loc - mx)
            p_ctx = jnp.exp(s_ctx - mx)
            den = jnp.sum(p_loc, axis=-1, keepdims=True) + jnp.sum(p_ctx, axis=-1, keepdims=True)
            o = jnp.dot(p_loc.astype(BF16), vw, preferred_element_type=F32)
            o = o + jnp.dot(p_ctx.astype(BF16), vcv, preferred_element_type=F32)
            outs.append(o * (1.0 / den))
            lses.append(mx + jnp.log(den))
        o = jnp.where(lane < HEAD_DIM, outs[0], outs[1])
        o_ref[...] = o.astype(ACT)
        a_ref[...] = (o * _silu(g_ref[...].astype(F32))).astype(BF16)
        col = lax.broadcasted_iota(jnp.int32, (1, 2), 1)
        lse_ref[...] = jnp.where(col == 0, lses[0], lses[1])
        if comm:
            pl.when(jnp.logical_and(hp == nhp - 1, b == nblk - 1))(lambda: comm.finish(cin, cout, sems))

    res = pl.pallas_call(
        body, grid=(nhp, nblk),
        in_specs=[qspec(0), kspec(1), kspec(2), qspec(3), tspec, mspec] + [HBM_SPEC] * n_cin,
        out_specs=[ospec, ospec, lspec] + [HBM_SPEC] * n_cout,
        out_shape=[_sds((nx, w), BF16), _sds((nx, w), ACT), _sds((nhp, nx, 2), F32)] + (comm.outs if comm else []),
        scratch_shapes=[pltpu.VMEM((2, _QB, _KW), F32), pltpu.VMEM((2, _N_TILES, GRID_W, _PAIR), F32)]
        + (comm.sems if comm else []),
        name=name, compiler_params=_cparams("arbitrary", "arbitrary"),
    )(p4, p4, p4, p4, rp, mask, *(comm.ins if comm else []))
    return res[:3], res[3:]


def _fold_tiles(dtiles_ref, drp_ref):
    shape = (GRID_W, _PAIR)
    lane = lax.broadcasted_iota(jnp.int32, shape, 1)
    flip = (lax.broadcasted_iota(jnp.int32, (_PAIR, _PAIR), 0)
            + lax.broadcasted_iota(jnp.int32, (_PAIR, _PAIR), 1) == _PAIR - 1).astype(F32)
    drp_ref[...] = jnp.zeros(drp_ref.shape, F32)
    for h in range(2):
        stack = dtiles_ref[h].reshape(_N_TILES * GRID_W, _PAIR)
        rev = jnp.dot(stack, flip, precision=lax.Precision.HIGHEST, preferred_element_type=F32)
        for t in range(_N_TILES):
            tile = rev[t * GRID_W:(t + 1) * GRID_W, :]
            for side in (0, 1):
                shift = _PAIR - GRID_W * side - (WIN_COLS - 1)
                half = jnp.where((lane < GRID_W) if side else (lane >= GRID_W), tile, 0.0)
                diag = pltpu.roll(half, shift, 1, stride=1, stride_axis=0)
                drp_ref[h, t + side:t + side + 1, :] += jnp.sum(diag, axis=0, keepdims=True)


def _attn_bwd(p4, rp, mask, o, lse, da, nx, name, comm=None):
    rows, w, nhp, nblk, qspec, kspec, tspec, mspec, lspec, ospec = _attn_geometry(p4, nx)
    n_ctx = rows - nx
    n_cin, n_cout = (len(comm.ins), len(comm.outs)) if comm else (0, 0)

    def body(*refs):
        q_ref, k_ref, v_ref, g_ref, rp_ref, m_ref, o_ref, lse_ref, da_ref = refs[:9]
        cin = refs[9:9 + n_cin]
        d4_ref, drp_ref = refs[9 + n_cin:11 + n_cin]
        cout = refs[11 + n_cin:11 + n_cin + n_cout]
        bias_ref, tiles_ref, ds_ref, dtiles_ref, dk_ref, dv_ref = refs[11 + n_cin + n_cout:17 + n_cin + n_cout]
        sems = refs[17 + n_cin + n_cout:]
        hp, b = pl.program_id(0), pl.program_id(1)
        if comm:
            pl.when(jnp.logical_and(hp == 0, b == 0))(lambda: comm.start(cin, cout, sems))
        start = _window_start(b, nx)
        here = pl.multiple_of(b * _QB, _QB)

        @pl.when(b == 0)
        def _():
            dk_ref[...] = jnp.zeros(dk_ref.shape, F32)
            dv_ref[...] = jnp.zeros(dv_ref.shape, F32)
            dtiles_ref[...] = jnp.zeros(dtiles_ref.shape, F32)
            d4_ref[0, pl.ds(nx, n_ctx), :] = jnp.zeros((n_ctx, _PAIR), BF16)
            d4_ref[3, pl.ds(nx, n_ctx), :] = jnp.zeros((n_ctx, _PAIR), BF16)

        _load_bias(bias_ref, tiles_ref, rp_ref, m_ref, b, nblk)
        gv = g_ref[...].astype(F32)
        dav = da_ref[...].astype(F32)
        ov = o_ref[...].astype(F32)
        dov = dav * _silu(gv)
        d4_ref[3, pl.ds(here, _QB), :] = (dav * ov * _dsilu(gv)).astype(BF16)
        qf = q_ref[...].astype(F32) * HEAD_DIM ** -0.5
        kw = k_ref[pl.ds(start, _KW), :].astype(BF16)
        vw = v_ref[pl.ds(start, _KW), :].astype(BF16)
        kcv = k_ref[pl.ds(nx, n_ctx), :].astype(BF16)
        vcv = v_ref[pl.ds(nx, n_ctx), :].astype(BF16)
        lane = lax.broadcasted_iota(jnp.int32, (1, _PAIR), 1)
        dq = jnp.zeros((_QB, _PAIR), F32)
        for h in range(2):
            mine = (lane >= HEAD_DIM) if h else (lane < HEAD_DIM)
            qm = jnp.where(mine, qf, 0.0).astype(BF16)
            dom = jnp.where(mine, dov, 0.0)
            dob = dom.astype(BF16)
            lse = lse_ref[:, h:h + 1]
            s_loc = lax.dot_general(qm, kw, _DIMS["nt"], preferred_element_type=F32)
            p_loc = jnp.exp(s_loc + bias_ref[h] - lse)
            p_ctx = jnp.exp(lax.dot_general(qm, kcv, _DIMS["nt"], preferred_element_type=F32) - lse)
            delta = jnp.sum(dom * ov, axis=-1, keepdims=True)
            ds_loc = p_loc * (lax.dot_general(dob, vw, _DIMS["nt"], preferred_element_type=F32) - delta)
            ds_ctx = p_ctx * (lax.dot_general(dob, vcv, _DIMS["nt"], preferred_element_type=F32) - delta)
            dsb_loc = ds_loc.astype(BF16)
            dsb_ctx = ds_ctx.astype(BF16)
            dq_h = (jnp.dot(dsb_loc, kw, preferred_element_type=F32)
                    + jnp.dot(dsb_ctx, kcv, preferred_element_type=F32))
            dq = dq + jnp.where(mine, dq_h, 0.0)
            dk_ref[pl.ds(start, _KW), :] += lax.dot_general(dsb_loc, qm, _DIMS["tn"], preferred_element_type=F32)
            dv_ref[pl.ds(start, _KW), :] += lax.dot_general(p_loc.astype(BF16), dob, _DIMS["tn"],
                                                            preferred_element_type=F32)
            dk_ref[pl.ds(nx, n_ctx), :] += lax.dot_general(dsb_ctx, qm, _DIMS["tn"], preferred_element_type=F32)
            dv_ref[pl.ds(nx, n_ctx), :] += lax.dot_general(p_ctx.astype(BF16), dob, _DIMS["tn"],
                                                           preferred_element_type=F32)
            ds_ref[h] = ds_loc
        d4_ref[0, pl.ds(here, _QB), :] = (dq * HEAD_DIM ** -0.5).astype(BF16)

        def scatter(cls):
            for h in range(2):
                for qr, kr, tile in _bias_pieces(cls):
                    if tile is not None:
                        dtiles_ref[h, tile] += ds_ref[h, qr * GRID_W:(qr + 1) * GRID_W, kr * GRID_W:(kr + 2) * GRID_W]

        _block_class(b, nblk, scatter)

        @pl.when(b == nblk - 1)
        def _():
            d4_ref[1] = dk_ref[...].astype(BF16)
            d4_ref[2] = dv_ref[...].astype(BF16)
            _fold_tiles(dtiles_ref, drp_ref)

        if comm:
            pl.when(jnp.logical_and(hp == nhp - 1, b == nblk - 1))(lambda: comm.finish(cin, cout, sems))

    tiles = pltpu.VMEM((2, _N_TILES, GRID_W, _PAIR), F32)
    block = pltpu.VMEM((2, _QB, _KW), F32)
    res = pl.pallas_call(
        body, grid=(nhp, nblk),
        in_specs=[qspec(0), kspec(1), kspec(2), qspec(3), tspec, mspec, ospec, lspec, ospec] + [HBM_SPEC] * n_cin,
        out_specs=[pl.BlockSpec((4, rows, _PAIR), lambda hp, b: (0, 0, hp)), tspec] + [HBM_SPEC] * n_cout,
        out_shape=[_sds((4, rows, w), BF16), _sds(rp.shape, F32)] + (comm.outs if comm else []),
        scratch_shapes=[block, tiles, block, tiles, pltpu.VMEM((rows, _PAIR), F32), pltpu.VMEM((rows, _PAIR), F32)]
        + (comm.sems if comm else []),
        name=name, compiler_params=_cparams("arbitrary", "arbitrary"),
    )(p4, p4, p4, p4, rp, mask, o, lse, da, *(comm.ins if comm else []))
    return res[:2], res[2:]


def _w_out_loss(a, w_out, xres, gate, g, target, name):
    m, k = a.shape
    d = w_out.shape[1]
    assert gate.shape[0] == 1
    tm = _row_tile(m)
    nblk = m // tm

    def body(a_ref, w_ref, x_ref, gt_ref, g_ref, t_ref, yx_ref, loss_ref, dx_ref, dg_ref, acc_ref):
        i = pl.program_id(0)
        yx = jnp.dot(a_ref[...], w_ref[...], preferred_element_type=F32)
        yx_ref[...] = yx.astype(ACT)
        xv = x_ref[...] + gt_ref[0] * yx
        gv = g_ref[...]
        r = lax.rsqrt(jnp.mean(xv * xv, axis=-1, keepdims=True) + EPS)
        xn = xv * r
        err = xn * gv - t_ref[...]
        dy = err * (1.0 / d)
        dxn = dy * gv
        dx_ref[...] = r * (dxn - xn * jnp.mean(dxn * xn, axis=-1, keepdims=True))
        s_g = jnp.sum(dy * xn, axis=0, keepdims=True)
        s_l = jnp.sum(jnp.mean(err * err, axis=-1, keepdims=True), axis=0, keepdims=True)

        @pl.when(i == 0)
        def _():
            dg_ref[...] = s_g
            acc_ref[...] = s_l

        @pl.when(i > 0)
        def _():
            dg_ref[...] += s_g
            acc_ref[...] += s_l

        @pl.when(i == nblk - 1)
        def _():
            loss_ref[...] = jnp.broadcast_to(0.5 * acc_ref[...], loss_ref.shape)

    row = pl.BlockSpec((tm, d), lambda i: (i, 0))
    vec = pl.BlockSpec((1, d), lambda i: (0, 0))
    return pl.pallas_call(
        body, grid=(nblk,),
        in_specs=[pl.BlockSpec((tm, k), lambda i: (i, 0)), pl.BlockSpec((k, d), lambda i: (0, 0)), row,
                  pl.BlockSpec((1, 1, d), lambda i: (0, 0, 0)), vec, row],
        out_specs=[row, pl.BlockSpec((1, 128), lambda i: (0, 0)), row, vec],
        out_shape=[_sds((m, d), ACT), _sds((1, 128), F32), _sds((m, d), F32), _sds((1, d), F32)],
        scratch_shapes=[pltpu.VMEM((1, 1), F32)], name=name, compiler_params=_cparams("arbitrary"),
    )(a, w_out, xres, gate, g, target)


def _as2d(a):
    if a.ndim == 1:
        return a.reshape(-1, 128) if a.shape[0] % 128 == 0 else a.reshape(1, -1)
    return a.reshape(-1, a.shape[-1])


def _adamw(w, g, m, v, name):
    shape = w.shape
    w2, g2, m2, v2 = (_as2d(t) for t in (w, g.reshape(shape), m, v))
    rows, cols = w2.shape
    tr = 512 if rows % 512 == 0 else rows
    c1 = 1.0 - ADAM_B1 ** ADAM_STEP
    c2 = 1.0 - ADAM_B2 ** ADAM_STEP

    def body(w_ref, g_ref, m_ref, v_ref, d_ref, nm_ref, nv_ref):
        gv = g_ref[...]
        nm = ADAM_B1 * m_ref[...] + (1.0 - ADAM_B1) * gv
        nv = ADAM_B2 * v_ref[...] + (1.0 - ADAM_B2) * (gv * gv)
        nm_ref[...] = nm
        nv_ref[...] = nv
        d_ref[...] = -ADAM_LR * ((nm / c1) / (jnp.sqrt(nv / c2) + ADAM_EPS) + ADAM_WD * w_ref[...])

    blk = pl.BlockSpec((tr, cols), lambda i: (i, 0))
    outs = _call(body, (w2, g2, m2, v2), grid=(rows // tr,), in_specs=[blk] * 4, out_specs=[blk] * 3,
                 out_shape=[_sds((rows, cols), F32)] * 3, name=name)
    return tuple(t.reshape(shape) for t in outs)


def _sum_lead(x, name, out_dtype=F32):
    n, rows, cols = x.shape
    tr = 512 if rows % 512 == 0 else rows

    def body(x_ref, o_ref):
        acc = x_ref[0].astype(F32)
        for k in range(1, n):
            acc = acc + x_ref[k].astype(F32)
        o_ref[...] = acc.astype(out_dtype)

    return pl.pallas_call(
        body, grid=(rows // tr,), in_specs=[pl.BlockSpec((n, tr, cols), lambda i: (0, i, 0))],
        out_specs=pl.BlockSpec((tr, cols), lambda i: (i, 0)), out_shape=_sds((rows, cols), out_dtype),
        name=name, compiler_params=_cparams("parallel"),
    )(x)


def _seg_vecs(mod_l, which, nseg):
    return mod_l[:nseg, which][:, None, :]


def _norm_grads(dshift, dgeff, dgate, g, scale):
    nseg, _, d = dshift.shape
    dmod = jnp.stack([dshift[:, 0], dgeff[:, 0] * g, dgate[:, 0]], axis=1)
    if nseg == 1:
        dmod = jnp.concatenate([dmod, jnp.zeros((1, 3, d), F32)], axis=0)
    dg = jnp.sum(dgeff[:, 0] * (1.0 + scale[:, 0]), axis=0)
    return dmod, dg


def _pool_layer(xin, g, mod_l, w_in, w_grp, w_out, pscale, nx, tag, ctx=None, head=None):
    nseg = 1 if ctx is None else 2
    shift, scale, gate = (_seg_vecs(mod_l, k, nseg) for k in range(3))
    *joined, h, r, uv = _norm_w_in(xin, g, scale, shift, w_in, nx, f"w_in_fwd_{tag}", ctx)
    if joined:
        xin, = joined
    z, mixed, a = _pool_grp_fwd(uv, w_grp, pscale, nx, f"pool_fwd_{tag}")
    if head is None:
        yx, xout = _w_out_resid(a, w_out, xin, gate, nx, f"w_out_fwd_{tag}")
    else:
        yx, *xout = _w_out_loss(a, w_out, xin, gate, *head, f"w_out_loss_{tag}")

    def backward(dxo, token=None):
        gate_b = gate if token is None else gate + token[0, 0]
        dyx, da, dgate = _gate_w_out_bwd(dxo, yx, gate_b, w_out, nx, f"w_out_bwd_{tag}")
        gw_out = _mm_tn(a, dyx, f"w_out_grad_{tag}", BF16)
        dm, duv, dscale = _pool_grp_bwd(da, mixed, uv, pscale, w_grp, nx, f"pool_bwd_{tag}")
        gw_grp = _grp_wgrad(z, dm, w_grp.shape[0], f"grp_grad_{tag}", BF16)
        gw_in = _mm_tn_parts(h, duv, f"w_in_grad_{tag}", BF16)
        dx, dshift, dgeff = _w_in_bwd_norm(duv, w_in, xin, r, g, scale, dxo, nx, f"w_in_bwd_{tag}",
                                           dx_rows=None if ctx is None else nx)
        dmod, dg = _norm_grads(dshift, dgeff, dgate, g[0], scale)
        return dx, dmod, dg, dict(w_in=gw_in, w_grp=gw_grp, w_out=gw_out, scale=dscale)

    return xout, backward


def _na_layer(xc, g, mod_l, w_in, rpb, w_out, nx, mask, comm=None):
    nh, n_dr, n_dc = rpb.shape
    shift, scale = _seg_vecs(mod_l, 0, 2), _seg_vecs(mod_l, 1, 2)
    gate = _seg_vecs(mod_l, 2, 1)
    h, r, p4 = _norm_w_in(xc, g, scale, shift, w_in, nx, "w_in_fwd_na")
    rp = jnp.pad(rpb, ((0, 0), (1, _RP_ROWS - 1 - n_dr), (0, _PAIR - n_dc)))
    (a, o, lse), carried = _attn_fwd(p4, rp, mask, nx, "attn_fwd", comm)
    yx, xout = _w_out_resid(a, w_out, xc, gate, nx, "w_out_fwd_na")

    def backward(dxo, comm=None):
        dyx, da, dgate = _gate_w_out_bwd(dxo, yx, gate, w_out, nx, "w_out_bwd_na")
        gw_out = _mm_tn(a, dyx, "w_out_grad_na", BF16)
        (d4, drp), carried_bwd = _attn_bwd(p4, rp, mask, o, lse, da, nx, "attn_bwd", comm)
        gw_in = _mm_tn_parts(h, d4, "w_in_grad_na", BF16)
        dx, dshift, dgeff = _w_in_bwd_norm(d4, w_in, xc, r, g, scale, dxo, nx, "w_in_bwd_na")
        dgate2 = jnp.concatenate([dgate, jnp.zeros_like(dgate)], axis=0)
        dmod, dg = _norm_grads(dshift, dgeff, dgate2, g[0], scale)
        drpb = drp[:, 1:1 + n_dr, ::-1][:, :, :n_dc]
        return dx, dmod, dg, dict(w_in=gw_in, w_out=gw_out, rpb=drpb), carried_bwd

    return xout, backward, carried


def _conv_layer(xin, g, mod_l, w_in, dw, db, w_out):
    shift, scale, gate = (_seg_vecs(mod_l, k, 1) for k in range(3))
    nx = xin.shape[0]
    h, r, p4 = _norm_w_in(xin, g, scale, shift, w_in, nx, "w_in_fwd_conv")
    a = _conv_fwd(p4, dw, db, "conv_fwd")
    yx, xout = _w_out_resid(a, w_out, xin, gate, nx, "w_out_fwd_conv")

    def backward(dxo):
        dyx, da, dgate = _gate_w_out_bwd(dxo, yx, gate, w_out, nx, "w_out_bwd_conv")
        gw_out = _mm_tn(a, dyx, "w_out_grad_conv", BF16)
        d4, ddw, ddb = _conv_bwd(da, p4, dw, db, "conv_bwd")
        gw_in = _mm_tn_parts(h, d4, "w_in_grad_conv", BF16)
        dx, dshift, dgeff = _w_in_bwd_norm(d4, w_in, xin, r, g, scale, dxo, nx, "w_in_bwd_conv")
        dmod, dg = _norm_grads(dshift, dgeff, dgate, g[0], scale)
        return dx, dmod, dg, dict(w_in=gw_in, w_out=gw_out, dw=ddw, db=ddb)

    return xout, backward


def _example_step(x, ctx, target, mod, norm_g, final_g, wts, hooks=None):
    hooks = hooks or {}
    na_weights, late_comm, late_weights = (hooks.get(k) for k in ("na_weights", "late_comm", "late_weights"))
    nx = x.shape[0]
    consts = _attn_mask()
    g_rows = [norm_g[i:i + 1] for i in range(4)]
    xc1, bwd0 = _pool_layer(x, g_rows[0], mod[0], wts["pool_w_in"][0], wts["pool_w_grp"][0],
                            wts["pool_w_out"][0], wts["pool_scale"][0:1], nx, "p0", ctx=ctx)
    if na_weights is not None:
        wts = {**wts, **na_weights(xc1)}
    x2, bwd1, carried = _na_layer(xc1, g_rows[1], mod[1], wts["na_w_in"], wts["na_rpb"], wts["na_w_out"], nx, consts,
                                  late_comm)
    if late_weights is not None:
        wts = {**wts, **late_weights(carried)}
    x3, bwd2 = _conv_layer(x2, g_rows[2], mod[2], wts["conv_w_in"], wts["conv_dw"], wts["conv_db"], wts["conv_w_out"])
    (loss, dx4, dfinal_g), bwd3 = _pool_layer(x3, g_rows[3], mod[3], wts["pool_w_in"][1], wts["pool_w_grp"][1],
                                              wts["pool_w_out"][1], wts["pool_scale"][1:2], nx, "p3",
                                              head=(final_g, target))
    call = lambda k, *args: hooks[k](*args) if k in hooks else None
    dx3, dmod3, dg3, gr3 = bwd3(dx4)
    dx2, dmod2, dg2, gr2 = bwd2(dx3)
    dxc1, dmod1, dg1, gr1, carried_bwd = bwd1(dx2, call("grad_comm", gr3, gr2))
    dx0, dmod0, dg0, gr0 = bwd0(dxc1, call("na_grads_start", gr1))
    return dict(
        loss=loss, grad_x=dx0, dmod=jnp.stack([dmod0, dmod1, dmod2, dmod3]),
        dnorm_g=jnp.stack([dg0, dg1, dg2, dg3]), dfinal_g=dfinal_g, layers=(gr0, gr1, gr2, gr3), carried=carried_bwd)


_AXES = ("x", "y", "c")
_CHIP_FLIPS = ((1, 0), (0, 1), (1, 1))


def _position():
    return tuple(lax.axis_index(a) for a in _AXES)


def _flipped(pos, flip):
    return tuple(1 - p if f else p for p, f in zip(pos, flip))


def _join_comms(comms):
    n_in = [len(c.ins) for c in comms]
    n_out = [len(c.outs) for c in comms]
    n_sem = [len(c.sems) for c in comms]

    def parts(ins, outs, sems):
        for k in range(len(comms)):
            a, b, s = sum(n_in[:k]), sum(n_out[:k]), sum(n_sem[:k])
            yield comms[k], (ins[a:a + n_in[k]], outs[b:b + n_out[k]], sems[s:s + n_sem[k]])

    def start(ins, outs, sems):
        for c, part in parts(ins, outs, sems):
            c.start(*part)

    def finish(ins, outs, sems):
        for c, part in parts(ins, outs, sems):
            c.finish(*part)

    joint = _Comm([a for c in comms for a in c.ins], [o for c in comms for o in c.outs],
                  [s for c in comms for s in c.sems], start, finish)
    return joint, lambda res: [list(res[sum(n_out[:k]):sum(n_out[:k + 1])]) for k in range(len(comms))]


def _run_comms(comms, name):
    joint, split = _join_comms(comms)

    def body(*refs):
        n_in, n_out = len(joint.ins), len(joint.outs)
        joint.start(refs[:n_in], refs[n_in:n_in + n_out], refs[n_in + n_out:])
        joint.finish(refs[:n_in], refs[n_in:n_in + n_out], refs[n_in + n_out:])

    res = pl.pallas_call(
        body, in_specs=[HBM_SPEC] * len(joint.ins), out_specs=[HBM_SPEC] * len(joint.outs), out_shape=joint.outs,
        scratch_shapes=joint.sems, name=name,
    )(*joint.ins)
    return split(res)


def _all_gather_comm(v, axes):
    flips = [f for f in np.ndindex(2, 2, 2) if any(f) and all(a in axes or not b for a, b in zip(_AXES, f))]
    n = len(flips) + 1

    def copies(ins, outs, sems):
        (v_ref,), (o_ref,), (send_sems, recv_sems, local_sem) = ins, outs, sems
        pos = _position()
        slot = 0
        for a, p in zip(_AXES, pos):
            if a in axes:
                slot = 2 * slot + p
        local = pltpu.make_async_copy(v_ref, o_ref.at[slot], local_sem)
        remote = [pltpu.make_async_remote_copy(v_ref, o_ref.at[slot], send_sems.at[k], recv_sems.at[k],
                                               device_id=_flipped(pos, flip), device_id_type=MESH)
                  for k, flip in enumerate(flips)]
        return [local] + remote

    def start(ins, outs, sems):
        for cp in copies(ins, outs, sems):
            cp.start()

    def finish(ins, outs, sems):
        for cp in copies(ins, outs, sems):
            cp.wait()

    sems = [pltpu.SemaphoreType.DMA((n - 1,)), pltpu.SemaphoreType.DMA((n - 1,)), pltpu.SemaphoreType.DMA(())]
    return _Comm([v], [_sds((n,) + v.shape, v.dtype)], sems, start, finish)


def _all_gather_two_level_comm(v):
    def copies(ins, outs, sems, onward):
        (v_ref,), (o_ref,), (send_sems, recv_sems, local_sem) = ins, outs, sems
        x, y, c = _position()
        sibling = (x, y, 1 - c)
        slot = lambda px, py, pc: o_ref.at[4 * px + 2 * py + pc]
        own = pltpu.make_async_copy(v_ref, slot(x, y, c), local_sem)
        first = [pltpu.make_async_remote_copy(v_ref, slot(x, y, c), send_sems.at[0], recv_sems.at[0],
                                              device_id=sibling, device_id_type=MESH)]
        fwd = []
        for k, flip in enumerate(_CHIP_FLIPS):
            px, py = _flipped((x, y), flip)
            first.append(pltpu.make_async_remote_copy(v_ref, slot(x, y, c), send_sems.at[1 + k], recv_sems.at[1 + k],
                                                      device_id=(px, py, c), device_id_type=MESH))
            if onward:
                fwd.append(pltpu.make_async_remote_copy(slot(px, py, c), slot(px, py, c), send_sems.at[4 + k],
                                                        recv_sems.at[4 + k], device_id=sibling, device_id_type=MESH))
        return own, first, fwd

    def start(ins, outs, sems):
        own, first, _ = copies(ins, outs, sems, False)
        for cp in [own] + first:
            cp.start()

    def finish(ins, outs, sems):
        own, first, fwd = copies(ins, outs, sems, True)
        for arrived, onward in zip(first[1:], fwd):
            arrived.wait_recv()
            onward.start()
        first[0].wait_recv()
        for cp in fwd:
            cp.wait_recv()
        for cp in first + fwd:
            cp.wait_send()
        own.wait()

    sems = [pltpu.SemaphoreType.DMA((7,)), pltpu.SemaphoreType.DMA((7,)), pltpu.SemaphoreType.DMA(())]
    return _Comm([v], [_sds((8,) + v.shape, v.dtype)], sems, start, finish)


def _all_gather(v, axes, name):
    return _run_comms([_all_gather_comm(v, axes)], name)[0][0]


class _Item:
    def __init__(self, key, layer, shape, shard_axis, half_axis):
        self.key, self.layer, self.shape = key, layer, tuple(shape)
        self.shard_axis, self.half_axis = shard_axis, half_axis
        self.shard = shape[shard_axis] // 4
        self.half = shape[half_axis] // 2

    def sized(self, shard=False, half=False):
        s = list(self.shape)
        if shard:
            s[self.shard_axis] = self.shard
        if half:
            s[self.half_axis] = self.half
        return tuple(s)

    def window(self, ref, chip=None, half=None):
        idx = [slice(None)] * len(self.shape)
        if chip is not None:
            idx[self.shard_axis] = pl.ds(chip * self.shard, self.shard)
        if half is not None:
            idx[self.half_axis] = pl.ds(half * self.half, self.half)
        return ref.at[tuple(idx)]


def _items(d, w):
    out = []
    for j in range(2):
        out += [_Item("pool_w_in", j, (d, 2 * w), 1, 0), _Item("pool_w_grp", j, (4, w // 4, w // 4), 1, 0),
                _Item("pool_w_out", j, (w, d), 0, 1)]
    out += [_Item("na_w_in", 0, (d, 4 * w), 1, 0), _Item("na_w_out", 0, (w, d), 0, 1),
            _Item("conv_w_in", 0, (d, 4 * w), 1, 0), _Item("conv_w_out", 0, (w, d), 0, 1)]
    return out


def _gather_comm(shards, items):
    n = len(items)

    def copies(src, dst, sems, onward):
        send_a, recv_a, send_b, recv_b, send_c, recv_c = sems
        x, y, c = _position()
        chip = 2 * x + y
        sibling = (x, y, 1 - c)
        own, out, fwd, fwd_in = [], [], [], []
        for i, it in enumerate(items):
            own.append(pltpu.make_async_remote_copy(src[i], it.window(dst[i], chip=chip), send_c.at[i], recv_c.at[i],
                                                    device_id=sibling, device_id_type=MESH))
            for k, flip in enumerate(_CHIP_FLIPS):
                px, py = _flipped((x, y), flip)
                s = 3 * i + k
                out.append(pltpu.make_async_remote_copy(
                    it.window(src[i], half=c), it.window(dst[i], chip=chip, half=c), send_a.at[s], recv_a.at[s],
                    device_id=(px, py, c), device_id_type=MESH))
                if onward:
                    got = it.window(dst[i], chip=2 * px + py, half=c)
                    fwd.append(pltpu.make_async_remote_copy(got, got, send_b.at[s], recv_b.at[s],
                                                            device_id=sibling, device_id_type=MESH))
                    other = it.window(dst[i], chip=2 * px + py, half=1 - c)
                    fwd_in.append(pltpu.make_async_remote_copy(other, other, send_b.at[s], recv_b.at[s],
                                                               device_id=sibling, device_id_type=MESH))
        return own, out, fwd, fwd_in

    def start(src, dst, sems):
        own, out, _, _ = copies(src, dst, sems, False)
        for cp in own + out:
            cp.start()

    def finish(src, dst, sems):
        own, out, fwd, fwd_in = copies(src, dst, sems, True)
        for arrived, onward in zip(out, fwd):
            arrived.wait_recv()
            onward.start()
        for cp in fwd_in:
            cp.wait_recv()
        for cp in out + fwd:
            cp.wait_send()
        for cp in own:
            cp.wait()

    sems = [pltpu.SemaphoreType.DMA((3 * n,)) for _ in range(4)] + [pltpu.SemaphoreType.DMA((n,)) for _ in range(2)]
    return _Comm(shards, [_sds(it.shape, BF16) for it in items], sems, start, finish)


def _pair_swap_copies(windows):
    def copies(src, got, sems):
        send_sems, recv_sems = sems
        x, y, c = _position()
        return [pltpu.make_async_remote_copy(windows[i](src[i], 1 - c), got[i], send_sems.at[i], recv_sems.at[i],
                                             device_id=(x, y, 1 - c), device_id_type=MESH)
                for i in range(len(windows))]

    return copies


def _pair_swap_comm(arrays, windows, out_shapes):
    n = len(arrays)
    copies = _pair_swap_copies(windows)

    def start(src, got, sems):
        for cp in copies(src, got, sems):
            cp.start()

    def finish(src, got, sems):
        for cp in copies(src, got, sems):
            cp.wait()

    return _Comm(arrays, out_shapes, [pltpu.SemaphoreType.DMA((n,)), pltpu.SemaphoreType.DMA((n,))], start, finish)


def _pair_swap(arrays, windows, out_shapes, name):
    return _run_comms([_pair_swap_comm(arrays, windows, out_shapes)], name)[0]


def _chip_exchange_copies(items):
    def copies(src, dst, sems):
        send_sems, recv_sems = sems
        x, y, c = _position()
        out = []
        for i, it in enumerate(items):
            for k, flip in enumerate(_CHIP_FLIPS):
                px, py = _flipped((x, y), flip)
                out.append(pltpu.make_async_remote_copy(
                    it.window(src[i], chip=2 * px + py), dst[i].at[k], send_sems.at[3 * i + k],
                    recv_sems.at[3 * i + k], device_id=(px, py, c), device_id_type=MESH))
        return out

    return copies


_SEM_SPEC = pl.BlockSpec(memory_space=pltpu.SEMAPHORE)
_DATAFLOW = pltpu.SideEffectType.DATAFLOW_SIDE_EFFECTING


def _split_start(copies, srcs, zones, n_copies, name):
    n, nz = len(srcs), len(zones)

    def body(*refs):
        src, land = refs[:n], refs[n:n + nz]
        send_sems, recv_sems = refs[n + nz:n + nz + 2]
        token = refs[-1]
        for cp in copies(src, land, (send_sems, recv_sems)):
            cp.start()
        token[...] = jnp.zeros(token.shape, F32)

    hbm = lambda t: pltpu.HBM(t.shape, t.dtype)
    res = pl.pallas_call(
        body, name=name,
        out_shape=(pltpu.SemaphoreType.DMA((n_copies,)), pltpu.SemaphoreType.DMA((n_copies,)),
                   *[hbm(t) for t in list(srcs) + list(zones)], _sds((8, 128), F32)),
        in_specs=[HBM_SPEC] * (n + nz),
        out_specs=(_SEM_SPEC, _SEM_SPEC, *[HBM_SPEC] * (n + nz), pl.BlockSpec(memory_space=pltpu.VMEM)),
        input_output_aliases={i: 2 + i for i in range(n + nz)},
        compiler_params=pltpu.CompilerParams(has_side_effects=_DATAFLOW),
    )(*[pltpu.with_memory_space_constraint(t, pltpu.HBM) for t in list(srcs) + list(zones)])
    return (res[0], res[1], list(res[2:2 + n]), list(res[2 + n:2 + n + nz])), res[-1]


def _split_wait(copies, handle, after, name):
    send_sems, recv_sems, srcs, zones = handle
    n, nz = len(srcs), len(zones)

    def body(*refs):
        src, land = refs[:n], refs[n:n + nz]
        send, recv = refs[n + nz:n + nz + 2]
        for cp in copies(src, land, (send, recv)):
            cp.wait_send()
            cp.wait_recv()

    hbm = lambda t: pltpu.HBM(t.shape, t.dtype)
    res = pl.pallas_call(
        body, name=name, out_shape=tuple(hbm(t) for t in list(srcs) + list(zones)),
        in_specs=[HBM_SPEC] * (n + nz) + [_SEM_SPEC, _SEM_SPEC, pl.BlockSpec(memory_space=pl.ANY)],
        out_specs=tuple([HBM_SPEC] * (n + nz)), input_output_aliases={i: i for i in range(n + nz)},
        compiler_params=pltpu.CompilerParams(has_side_effects=_DATAFLOW),
    )(*srcs, *zones, send_sems, recv_sems, after)
    return list(res[:n]), list(res[n:])


def _gather_ici_copies(items):
    def copies(src, dst, sems):
        send_sems, recv_sems = sems
        x, y, c = _position()
        chip = 2 * x + y
        out = []
        for i, it in enumerate(items):
            for k, flip in enumerate(_CHIP_FLIPS):
                px, py = _flipped((x, y), flip)
                out.append(pltpu.make_async_remote_copy(
                    it.window(src[i], half=c), it.window(dst[i], chip=chip, half=c), send_sems.at[3 * i + k],
                    recv_sems.at[3 * i + k], device_id=(px, py, c), device_id_type=MESH))
        return out

    return copies


def _gather_pair_finish(shards, mats, items, name):
    n = len(items)

    def body(*refs):
        src, dst = refs[:n], refs[2 * n:3 * n]
        send_own, recv_own, send_fwd, recv_fwd = refs[3 * n:]
        x, y, c = _position()
        chip = 2 * x + y
        sibling = (x, y, 1 - c)
        copies = []
        for i, it in enumerate(items):
            copies.append(pltpu.make_async_remote_copy(src[i], it.window(dst[i], chip=chip), send_own.at[i],
                                                       recv_own.at[i], device_id=sibling, device_id_type=MESH))
            for k, flip in enumerate(_CHIP_FLIPS):
                px, py = _flipped((x, y), flip)
                got = it.window(dst[i], chip=2 * px + py, half=c)
                copies.append(pltpu.make_async_remote_copy(got, got, send_fwd.at[3 * i + k], recv_fwd.at[3 * i + k],
                                                           device_id=sibling, device_id_type=MESH))
        for cp in copies:
            cp.start()
        for cp in copies:
            cp.wait()

    return pl.pallas_call(
        body, in_specs=[HBM_SPEC] * (2 * n), out_specs=[HBM_SPEC] * n, out_shape=[_sds(it.shape, BF16) for it in items],
        input_output_aliases={n + i: i for i in range(n)},
        scratch_shapes=[pltpu.SemaphoreType.DMA((n,)), pltpu.SemaphoreType.DMA((n,)),
                        pltpu.SemaphoreType.DMA((3 * n,)), pltpu.SemaphoreType.DMA((3 * n,))], name=name,
    )(*shards, *mats)


def _chip_exchange_comm(partials, items):
    n = len(items)
    copies = _chip_exchange_copies(items)

    def start(src, dst, sems):
        for cp in copies(src, dst, sems):
            cp.start()

    def finish(src, dst, sems):
        for cp in copies(src, dst, sems):
            cp.wait()

    return _Comm(partials, [_sds((3,) + it.sized(shard=True, half=True), BF16) for it in items],
                 [pltpu.SemaphoreType.DMA((3 * n,)), pltpu.SemaphoreType.DMA((3 * n,))], start, finish)


_SUM_STEPS = 2


def _pair_sums(gs, gots, its, pos, name):
    n = len(its)
    nb = _SUM_STEPS
    g2 = [g.reshape(-1, g.shape[-1]) for g in gs]
    got2 = [t.reshape(-1, t.shape[-1]) for t in gots]

    def body(pos_ref, *refs):
        for g_ref, got_ref, o_ref in zip(refs[:n], refs[n:2 * n], refs[2 * n:]):
            o_ref[...] = (g_ref[...].astype(F32) + got_ref[...].astype(F32)).astype(BF16)

    g_specs, got_specs = [], []
    for it, t in zip(its, got2):
        rows, cols = t.shape
        blk = (rows // nb, cols)
        g_map = (lambda i, pos: (pos[1] * nb + i, 0)) if it.half_axis == 0 else (lambda i, pos: (i, pos[1]))
        g_specs.append(pl.BlockSpec(blk, g_map))
        got_specs.append(pl.BlockSpec(blk, lambda i, pos: (i, 0)))
    outs = pl.pallas_call(
        body, grid_spec=pltpu.PrefetchScalarGridSpec(
            num_scalar_prefetch=1, grid=(nb,), in_specs=g_specs + got_specs, out_specs=got_specs),
        out_shape=[_sds(t.shape, BF16) for t in got2], name=name, compiler_params=_cparams("parallel"),
    )(pos, *g2, *got2)
    return [o.reshape(t.shape) for o, t in zip(outs, gots)]


_FLIP_SLOT = {2: 0, 1: 1, 3: 2}


def _chip_sums(pairs, slots, its, pos, name):
    n = len(its)
    nb = _SUM_STEPS

    def body(pos_ref, *refs):
        chip = pos_ref[0]
        for own in range(4):
            @pl.when(chip == own)
            def _():
                for p_ref, s_ref, o_ref in zip(refs[:n], refs[n:2 * n], refs[2 * n:]):
                    acc = None
                    for k in range(4):
                        v = (p_ref[...] if k == own else s_ref[_FLIP_SLOT[own ^ k]]).astype(F32)
                        acc = v if acc is None else acc + v
                    o_ref[...] = acc

    p_specs, s_specs, o_specs, shapes = [], [], [], []
    for it in its:
        shape = it.sized(shard=True, half=True)
        blk = (shape[0] // nb,) + shape[1:]
        rest = (0,) * (len(shape) - 1)

        def p_map(i, pos, it=it, nd=len(shape)):
            lead = i + (pos[0] * nb if it.shard_axis == 0 else 0)
            return (lead,) + tuple(pos[0] if ax == it.shard_axis else 0 for ax in range(1, nd))

        p_specs.append(pl.BlockSpec(blk, p_map))
        s_specs.append(pl.BlockSpec((3,) + blk, lambda i, pos, rest=rest: (0, i) + rest))
        o_specs.append(pl.BlockSpec(blk, lambda i, pos, rest=rest: (i,) + rest))
        shapes.append(_sds(shape, F32))
    return pl.pallas_call(
        body, grid_spec=pltpu.PrefetchScalarGridSpec(
            num_scalar_prefetch=1, grid=(nb,), in_specs=p_specs + s_specs, out_specs=o_specs),
        out_shape=shapes, name=name, compiler_params=_cparams("parallel"),
    )(pos, *pairs, *slots)


_GRAD_KEYS = ("pool_w_in", "pool_w_grp", "pool_w_out", "na_w_in", "na_w_out", "conv_w_in", "conv_w_out")


def _adamw_matrix(w, m, v, owns, others, it, pos, name):
    nl = w.shape[0]
    rows_split = it.half_axis == 0
    r, cdim = int(np.prod(w.shape[1:-1])), w.shape[-1]
    hr, hc = (r // 2, cdim) if rows_split else (r, cdim // 2)
    br = min(hr, 256)
    nb = hr // br
    c1 = 1.0 - ADAM_B1 ** ADAM_STEP
    c2 = 1.0 - ADAM_B2 ** ADAM_STEP

    def body(pos_ref, w_ref, m_ref, v_ref, *rest):
        own_refs, other_refs = rest[:nl], rest[nl:2 * nl]
        g_ref, d_ref, nm_ref, nv_ref = rest[2 * nl:]
        j, h = pl.program_id(0), pl.program_id(1)
        own, other = own_refs[0][...], other_refs[0][...]
        for q in range(1, nl):
            own = jnp.where(j == q, own_refs[q][...], own)
            other = jnp.where(j == q, other_refs[q][...], other)
        gv = jnp.where(h == pos_ref[1], own, other)
        nm = ADAM_B1 * m_ref[...] + (1.0 - ADAM_B1) * gv
        nv = ADAM_B2 * v_ref[...] + (1.0 - ADAM_B2) * (gv * gv)
        g_ref[...] = gv
        nm_ref[...] = nm
        nv_ref[...] = nv
        d_ref[...] = -ADAM_LR * ((nm / c1) / (jnp.sqrt(nv / c2) + ADAM_EPS) + ADAM_WD * w_ref[...])

    if rows_split:
        full = pl.BlockSpec((None, br, hc), lambda j, h, i, pos: (j, h * nb + i, 0))
    else:
        full = pl.BlockSpec((None, br, hc), lambda j, h, i, pos: (j, i, h))
    half = pl.BlockSpec((br, hc), lambda j, h, i, pos: (i, 0))
    flat = lambda t: t.reshape(nl, r, cdim)
    outs = pl.pallas_call(
        body, grid_spec=pltpu.PrefetchScalarGridSpec(
            num_scalar_prefetch=1, grid=(nl, 2, nb), in_specs=[full] * 3 + [half] * (2 * nl), out_specs=[full] * 4),
        out_shape=[_sds((nl, r, cdim), F32)] * 4, name=name,
        compiler_params=_cparams("parallel", "parallel", "parallel"),
    )(pos, flat(w), flat(m), flat(v), *[t.reshape(hr, hc) for t in list(owns) + list(others)])
    return tuple(t.reshape(w.shape) for t in outs)


_WEIGHTS = ("c_ctx", "norm_g", "ada_w", "ada_b", "pool_w_in", "pool_w_grp", "pool_scale", "pool_w_out", "na_w_in",
            "na_rpb", "na_w_out", "conv_w_in", "conv_dw", "conv_db", "conv_w_out", "final_g")
_COND_ROWS = 16


def _modulations(cond, ada_w, ada_b_cols):
    nl, d, n = ada_w.shape
    return _matmul(
        cond, ada_w, mode="nn", grid=(nl, 1), a_silu=True, epilogue="bias",
        a_spec=pl.BlockSpec((_COND_ROWS, d), lambda i, j: (0, 0)), b_spec=pl.BlockSpec((None, d, n), lambda i, j: (i, 0, 0)),
        extra=(ada_b_cols,), extra_specs=(pl.BlockSpec((None, 1, n), lambda i, j: (i, 0, 0)),),
        out_shapes=[_sds((nl, _COND_ROWS, n), F32)], out_specs=[pl.BlockSpec((None, _COND_ROWS, n), lambda i, j: (i, 0, 0))],
        name="modulations")[0]


def _ada_w_step(cond, dm_cols, w, m, v):
    nl, d, n = w.shape
    tr = d // 2
    c1 = 1.0 - ADAM_B1 ** ADAM_STEP
    c2 = 1.0 - ADAM_B2 ** ADAM_STEP

    def body(c_ref, dm_ref, w_ref, m_ref, v_ref, g_ref, d_ref, nm_ref, nv_ref):
        gv = lax.dot_general(_silu(c_ref[...]).astype(BF16), dm_ref[...].astype(BF16), _DIMS["tn"],
                             preferred_element_type=F32)
        nm = ADAM_B1 * m_ref[...] + (1.0 - ADAM_B1) * gv
        nv = ADAM_B2 * v_ref[...] + (1.0 - ADAM_B2) * (gv * gv)
        g_ref[...] = gv
        nm_ref[...] = nm
        nv_ref[...] = nv
        d_ref[...] = -ADAM_LR * ((nm / c1) / (jnp.sqrt(nv / c2) + ADAM_EPS) + ADAM_WD * w_ref[...])

    blk = pl.BlockSpec((None, tr, n), lambda l, i: (l, i, 0))
    return _call(
        body, (cond, dm_cols, w, m, v), grid=(nl, d // tr),
        in_specs=[pl.BlockSpec((_COND_ROWS, tr), lambda l, i: (0, i)),
                  pl.BlockSpec((None, _COND_ROWS, n), lambda l, i: (l, 0, 0)), blk, blk, blk],
        out_specs=[blk] * 4, out_shape=[_sds(w.shape, F32)] * 4, name="adamw_ada_w")


def _cond_grad(dm_cols, ada_w):
    nl, d, n = ada_w.shape
    return _matmul(
        dm_cols, ada_w, mode="nt", grid=(1, nl), nk=nl, acc_shape=(_COND_ROWS, d),
        a_spec=pl.BlockSpec((None, _COND_ROWS, n), lambda i, q: (q, 0, 0)), b_spec=pl.BlockSpec((None, d, n), lambda i, q: (q, 0, 0)),
        out_shapes=[_sds((_COND_ROWS, d), F32)], out_specs=[pl.BlockSpec((_COND_ROWS, d), lambda i, q: (0, 0))],
        name="cond_grad")[0]


def _pack(parts):
    flat = [p.reshape(-1) for p in parts]
    sizes = [f.shape[0] for f in flat]
    total = sum(sizes)
    rows = -(-total // 1024) * 8
    packed = jnp.concatenate(flat + [jnp.zeros((rows * 128 - total,), F32)]).reshape(rows, 128)
    offs = np.concatenate([[0], np.cumsum(sizes)])[:-1]
    return packed, [(int(o), p.shape) for o, p in zip(offs, parts)]


def _unpack(flat, layout, k):
    off, shape = layout[k]
    return flat[..., off:off + int(np.prod(shape))].reshape(flat.shape[:-1] + tuple(shape))


def kernel(x, c, ctx, c_ctx, norm_g, ada_w, ada_b, pool_w_in, pool_w_grp, pool_scale, pool_w_out, na_w_in, na_rpb, na_w_out, conv_w_in, conv_dw, conv_db, conv_w_out, final_g, loss_target, m_c_ctx, m_norm_g, m_ada_w, m_ada_b, m_pool_w_in, m_pool_w_grp, m_pool_scale, m_pool_w_out, m_na_w_in, m_na_rpb, m_na_w_out, m_conv_w_in, m_conv_dw, m_conv_db, m_conv_w_out, m_final_g, v_c_ctx, v_norm_g, v_ada_w, v_ada_b, v_pool_w_in, v_pool_w_grp, v_pool_scale, v_pool_w_out, v_na_w_in, v_na_rpb, v_na_w_out, v_conv_w_in, v_conv_dw, v_conv_db, v_conv_w_out, v_final_g):
    params = dict(c_ctx=c_ctx, norm_g=norm_g, ada_w=ada_w, ada_b=ada_b, pool_w_in=pool_w_in, pool_w_grp=pool_w_grp,
                  pool_scale=pool_scale, pool_w_out=pool_w_out, na_w_in=na_w_in, na_rpb=na_rpb, na_w_out=na_w_out,
                  conv_w_in=conv_w_in, conv_dw=conv_dw, conv_db=conv_db, conv_w_out=conv_w_out, final_g=final_g)
    mom1 = dict(c_ctx=m_c_ctx, norm_g=m_norm_g, ada_w=m_ada_w, ada_b=m_ada_b, pool_w_in=m_pool_w_in,
                pool_w_grp=m_pool_w_grp, pool_scale=m_pool_scale, pool_w_out=m_pool_w_out, na_w_in=m_na_w_in,
                na_rpb=m_na_rpb, na_w_out=m_na_w_out, conv_w_in=m_conv_w_in, conv_dw=m_conv_dw, conv_db=m_conv_db,
                conv_w_out=m_conv_w_out, final_g=m_final_g)
    mom2 = dict(c_ctx=v_c_ctx, norm_g=v_norm_g, ada_w=v_ada_w, ada_b=v_ada_b, pool_w_in=v_pool_w_in,
                pool_w_grp=v_pool_w_grp, pool_scale=v_pool_scale, pool_w_out=v_pool_w_out, na_w_in=v_na_w_in,
                na_rpb=v_na_rpb, na_w_out=v_na_w_out, conv_w_in=v_conv_w_in, conv_dw=v_conv_dw, conv_db=v_conv_db,
                conv_w_out=v_conv_w_out, final_g=v_final_g)
    d = x.shape[-1]
    w = na_w_out.shape[1] * 4
    xi, yi, ci = _position()
    chip = 2 * xi + yi
    dev = 2 * chip + ci
    n_ada = ada_w.shape[-1]

    def chip_cols(a, size):
        return lax.dynamic_slice_in_dim(a, chip * size, size, axis=a.ndim - 1)

    items = _items(d, w)
    first = [it for it in items if it.key.startswith("pool") and it.layer == 0]
    na = [it for it in items if it.key.startswith("na")]
    late = [it for it in items if it not in first + na]
    shards_of = lambda its: [params[it.key][it.layer].astype(BF16) for it in its]
    empties = lambda its: [lax.empty(it.shape, BF16) for it in its]
    first_copies, na_copies = _gather_ici_copies(first), _gather_ici_copies(na)

    conds = _all_gather(c.reshape(8, d // 8), _AXES, "gather_cond").reshape(8, d)
    behind = conds[0, 0] * 0.0
    first_handle, token = _split_start(first_copies, [s + behind.astype(BF16) for s in shards_of(first)],
                                       empties(first), 3 * len(first), "gather_first_start")
    cond = jnp.concatenate([conds + token[0, 0], c_ctx[None], jnp.zeros((_COND_ROWS - 9, d), F32)], axis=0)
    mod_cols = _modulations(cond, ada_w, chip_cols(ada_b, n_ada)[:, None, :])
    small_pack, small_layout = _pack([pool_scale, conv_dw, conv_db])
    (mod_all,), (small,) = _run_comms([_all_gather_comm(mod_cols, ("x", "y")),
                                       _all_gather_comm(small_pack, ("x", "y"))], "gather_mod")
    behind = mod_all[0, 0, 0, 0] * 0.0
    na_handle, token = _split_start(na_copies, [s + behind.astype(BF16) for s in shards_of(na)], empties(na),
                                    3 * len(na), "gather_na_start")
    first_shards, first_mats = _split_wait(first_copies, first_handle, token, "gather_first_wait")
    first_mats = _gather_pair_finish(first_shards, first_mats, first, "gather_first_pair")
    mod_all = mod_all.transpose(1, 2, 0, 3).reshape(4, _COND_ROWS, 3, d)
    mod = jnp.stack([lax.dynamic_index_in_dim(mod_all, dev, axis=1, keepdims=False), mod_all[:, 8]], axis=1)
    full = {(it.key, it.layer): mat for it, mat in zip(first, first_mats)}
    late_comm = _gather_comm(shards_of(late), late)

    def na_weights(after):
        na_shards, na_mats = _split_wait(na_copies, na_handle, after, "gather_na_wait")
        na_mats = _gather_pair_finish(na_shards, na_mats, na, "gather_na_pair")
        return {it.key: mat for it, mat in zip(na, na_mats)}

    def late_weights(mats):
        full.update({(it.key, it.layer): mat for it, mat in zip(late, mats)})
        return dict(pool_w_in=[full[("pool_w_in", j)] for j in range(2)],
                    pool_w_grp=[full[("pool_w_grp", j)] for j in range(2)],
                    pool_w_out=[full[("pool_w_out", j)] for j in range(2)],
                    conv_w_in=full[("conv_w_in", 0)], conv_w_out=full[("conv_w_out", 0)])

    small = small.reshape(4, -1)

    def whole(k):
        parts = _unpack(small, small_layout, k)
        return jnp.moveaxis(parts, 0, -2).reshape(parts.shape[1:-1] + (-1,))

    wts = dict(pool_w_in=[full[("pool_w_in", 0)]], pool_w_grp=[full[("pool_w_grp", 0)]],
               pool_w_out=[full[("pool_w_out", 0)]], pool_scale=whole(0), na_rpb=na_rpb[0], conv_dw=whole(1)[0],
               conv_db=whole(2))
    pos = jnp.stack([chip, ci]).astype(jnp.int32)

    def layer_grads(its, by_layer):
        pick = {"pool_w_in": "w_in", "pool_w_grp": "w_grp", "pool_w_out": "w_out", "na_w_in": "w_in",
                "na_w_out": "w_out", "conv_w_in": "w_in", "conv_w_out": "w_out"}
        return [by_layer[(it.key.split("_")[0], it.layer)][pick[it.key]] for it in its]

    pairs, handles = dict(), dict()
    half_windows = lambda its: [(lambda ref, half, it=it: it.window(ref, half=half)) for it in its]
    half_shapes = lambda its: [_sds(it.sized(half=True), BF16) for it in its]

    def pair_sums(its, mats, tag):
        got = _pair_swap(mats, half_windows(its), half_shapes(its), f"pair_exchange_{tag}")
        return _pair_sums(mats, got, its, pos, f"pair_sum_{tag}")

    def grad_comm(gr3, gr2):
        pairs["late"] = pair_sums(late, layer_grads(late, {("pool", 1): gr3, ("conv", 0): gr2}), "late")
        return _chip_exchange_comm(pairs["late"], late)

    slot_zones = lambda its: [lax.empty((3,) + it.sized(shard=True, half=True), BF16) for it in its]
    na_xcopies, first_xcopies = _chip_exchange_copies(na), _chip_exchange_copies(first)

    def na_grads_start(gr1):
        pairs["na"] = pair_sums(na, layer_grads(na, {("na", 0): gr1}), "na")
        handles["na"], started = _split_start(na_xcopies, pairs["na"], slot_zones(na), 3 * len(na),
                                              "exchange_na_start")
        return started

    res = _example_step(x[0], ctx[0], loss_target[0], mod, norm_g, final_g[None], wts, dict(
        na_weights=na_weights, late_comm=late_comm, late_weights=late_weights, grad_comm=grad_comm,
        na_grads_start=na_grads_start))
    g0, g1, g2, g3 = res["layers"]
    pairs["na"], na_slots = _split_wait(na_xcopies, handles["na"], g0["w_in"], "exchange_na_wait")
    first_grads = layer_grads(first, {("pool", 0): g0})
    packed, layout = _pack([res["dfinal_g"], res["dnorm_g"], res["dmod"], g1["rpb"],
                            jnp.concatenate([g0["scale"], g3["scale"]], axis=0), g2["dw"], g2["db"],
                            res["loss"][0, :1]])
    first_got, (every,) = _run_comms([_pair_swap_comm(first_grads, half_windows(first), half_shapes(first)),
                                      _all_gather_two_level_comm(packed)], "pair_exchange_first")
    pairs["first"] = _pair_sums(first_grads, first_got, first, pos, "pair_sum_first")

    grads = dict()
    total = _sum_lead(every, "sum_vec_grads").reshape(-1)
    every = every.reshape(8, -1)
    grads["final_g"] = _unpack(total, layout, 0).reshape(final_g.shape)
    grads["norm_g"] = _unpack(total, layout, 1)
    grads["na_rpb"] = _unpack(total, layout, 3)[None]
    grads["pool_scale"] = chip_cols(_unpack(total, layout, 4), pool_scale.shape[-1])
    grads["conv_dw"] = chip_cols(_unpack(total, layout, 5), conv_dw.shape[-1])[None]
    grads["conv_db"] = chip_cols(_unpack(total, layout, 6), conv_db.shape[-1])
    dmod_sum = _unpack(total, layout, 2).reshape(4, 2, 3 * d)
    dmod_each = _unpack(every, layout, 2).reshape(8, 4, 2, 3 * d)
    grads["ada_b"] = dmod_sum[:, 0] + dmod_sum[:, 1]
    dm = jnp.concatenate([dmod_each[:, :, 0].transpose(1, 0, 2), dmod_sum[:, 1][:, None],
                          jnp.zeros((4, _COND_ROWS - 9, 3 * d), F32)], axis=1)
    dm_cols = chip_cols(dm, n_ada)
    dcond = _cond_grad(dm_cols, ada_w)[8].reshape(8, d // 8)
    dcond_all = _all_gather(dcond, ("x", "y"), "gather_cond_grad")
    behind = dcond_all[0, 0, 0] * 0.0
    handles["first"], token = _split_start(first_xcopies, [p + behind.astype(BF16) for p in pairs["first"]],
                                           slot_zones(first), 3 * len(first), "exchange_first_start")
    grads["ada_w"], *ada_w_step = _ada_w_step(cond, dm_cols + token[0, 0], ada_w, m_ada_w, v_ada_w)
    grads["c_ctx"] = _sum_lead(dcond_all, "sum_cond_grad").reshape(d) * _dsilu(c_ctx)
    vector_out = {k: _adamw(params[k], grads[k], mom1[k], mom2[k], f"adamw_{k}")
                  for k in _WEIGHTS if k not in _GRAD_KEYS + ("ada_w",)}
    vector_out["ada_w"] = tuple(ada_w_step)
    pairs["first"], first_slots = _split_wait(first_xcopies, handles["first"], vector_out["ada_w"][2],
                                              "exchange_first_wait")

    slots = dict(zip(late, res["carried"]))
    slots.update(zip(first, first_slots))
    slots.update(zip(na, na_slots))
    pair_of = dict(zip(late, pairs["late"]))
    pair_of.update(zip(first, pairs["first"]))
    pair_of.update(zip(na, pairs["na"]))
    reduced = _chip_sums([pair_of[it] for it in items], [slots[it] for it in items], items, pos, "chip_sum")
    theirs = _pair_swap(reduced, [lambda ref, half: ref] * len(items),
                        [_sds(t.shape, F32) for t in reduced], "pair_return")
    matrix_out = dict()
    for k in _GRAD_KEYS:
        idx = [i for i, it in enumerate(items) if it.key == k]
        res_k = _adamw_matrix(params[k], mom1[k], mom2[k], [reduced[i] for i in idx], [theirs[i] for i in idx],
                              items[idx[0]], pos, f"adamw_{k}")
        grads[k], matrix_out[k] = res_k[0], res_k[1:]

    outs = [[], [], []]
    for k in _WEIGHTS:
        step = matrix_out[k] if k in matrix_out else vector_out[k]
        for lst, val in zip(outs, step):
            lst.append(val)
    loss = _unpack(total, layout, 7)[0]
    return (loss, res["grad_x"][None], *[grads[k].reshape(params[k].shape) for k in _WEIGHTS],
            *outs[0], *outs[1], *outs[2])
```

```python
import functools

import numpy as np
import jax
import jax.numpy as jnp
from jax import lax
from jax.experimental import pallas as pl
from jax.experimental.pallas import tpu as pltpu

F32 = jnp.float32
BF16 = jnp.bfloat16

EPS = 1e-6
GRID_W = 64
HEAD_DIM = 64
WIN_ROWS = 8
WIN_COLS = 16
POOL_WINDOWS = (2, 4, 8, 16)
Q_ROWS = 4
K_ROWS = 12
PAD_ROWS = 4
NEG = -1e30

ADAM_LR = 0.001
ADAM_B1 = 0.9
ADAM_B2 = 0.999
ADAM_EPS = 1e-08
ADAM_WD = 0.01
ADAM_STEP = 10

ROW_BLOCK = 256
VMEM_LIMIT = 56 * 1024 * 1024
MAX_OUT_BLOCK_BYTES = 4 * 1024 * 1024
ACT = BF16

MESH = pl.DeviceIdType.MESH
HBM_SPEC = pl.BlockSpec(memory_space=pltpu.HBM)


def _cparams(*sem):
    return pltpu.CompilerParams(dimension_semantics=sem or None, vmem_limit_bytes=VMEM_LIMIT)


def _sds(shape, dtype):
    return jax.ShapeDtypeStruct(tuple(shape), dtype)


def _call(body, args, *, grid, in_specs, out_specs, out_shape, name, scratch_shapes=()):
    return list(pl.pallas_call(
        body, grid=grid, in_specs=list(in_specs), out_specs=list(out_specs), out_shape=list(out_shape),
        scratch_shapes=list(scratch_shapes), name=name, compiler_params=_cparams(*(("arbitrary",) * len(grid))),
    )(*args))


def _sigmoid(x):
    return 1.0 / (1.0 + jnp.exp(-x))


def _silu(x):
    return x * _sigmoid(x)


def _dsilu(x):
    s = _sigmoid(x)
    return s * (1.0 + x * (1.0 - s))


_DIMS = {
    "nn": (((1,), (0,)), ((), ())),
    "nt": (((1,), (1,)), ((), ())),
    "tn": (((0,), (0,)), ((), ())),
}


def _matmul(a, b, *, mode, grid, a_spec, b_spec, out_shapes, out_specs, name, nk=1,
            a_silu=False, exact=False, epilogue=None, extra=(), extra_specs=(), acc_shape=None):
    n_extra = len(extra)
    n_out = len(out_shapes)

    def body(*refs):
        a_ref, b_ref = refs[:2]
        ex = refs[2:2 + n_extra]
        outs = refs[2 + n_extra:2 + n_extra + n_out]
        av = a_ref[...]
        bv = b_ref[...]
        if a_silu:
            av = _silu(av.astype(F32))
        if exact:
            prod = lax.dot_general(av.astype(F32), bv.astype(F32), _DIMS[mode],
                                   precision=lax.Precision.HIGHEST, preferred_element_type=F32)
        else:
            prod = lax.dot_general(av.astype(BF16), bv.astype(BF16), _DIMS[mode], preferred_element_type=F32)

        def finish(res):
            if epilogue == "bias":
                res = res + ex[0][...]
            outs[0][...] = res.astype(outs[0].dtype)

        if nk == 1:
            finish(prod)
        else:
            acc = refs[-1]
            k = pl.program_id(len(grid) - 1)

            @pl.when(k == 0)
            def _():
                acc[...] = prod

            @pl.when(k > 0)
            def _():
                acc[...] += prod

            @pl.when(k == nk - 1)
            def _():
                finish(acc[...])

    scratch = [pltpu.VMEM(acc_shape, F32)] if nk > 1 else []
    sem = ("parallel",) * (len(grid) - 1) + ("arbitrary",)
    return pl.pallas_call(
        body, grid=grid, in_specs=[a_spec, b_spec, *extra_specs], out_specs=list(out_specs),
        out_shape=list(out_shapes), scratch_shapes=scratch, name=name, compiler_params=_cparams(*sem),
    )(a, b, *extra)


def _row_tile(rows):
    for t in (768, 512, 256):
        if rows % t == 0:
            return t
    return rows


def _mm_tn(a, b, name, out_dtype, tm=512):
    r, m = a.shape
    n = b.shape[1]
    tm = min(tm, m)
    tn = min(1024, n)
    return _matmul(
        a, b, mode="tn", grid=(m // tm, n // tn),
        a_spec=pl.BlockSpec((r, tm), lambda i, j: (0, i)), b_spec=pl.BlockSpec((r, tn), lambda i, j: (0, j)),
        out_shapes=[_sds((m, n), out_dtype)], out_specs=[pl.BlockSpec((tm, tn), lambda i, j: (i, j))], name=name)[0]


def _mm_tn_parts(a, b, name, out_dtype, tm=1024):
    r, m = a.shape
    p, _, np_ = b.shape
    tm = min(tm, m)
    return _matmul(
        a, b, mode="tn", grid=(m // tm, p),
        a_spec=pl.BlockSpec((r, tm), lambda i, q: (0, i)), b_spec=pl.BlockSpec((None, r, np_), lambda i, q: (q, 0, 0)),
        out_shapes=[_sds((m, p * np_), out_dtype)], out_specs=[pl.BlockSpec((tm, np_), lambda i, q: (i, q))],
        name=name)[0]


def _row_vec(ref, is_ctx):
    return ref[0] if is_ctx is None else jnp.where(is_ctx, ref[1], ref[0])


def _ctx_rows(i, tm, nx, nseg):
    if nseg == 1:
        return None
    return i * tm + lax.broadcasted_iota(jnp.int32, (tm, 1), 0) >= nx


def _seg_sums(ref, val, is_ctx, first):
    if is_ctx is None:
        parts = [jnp.sum(val, axis=0, keepdims=True)]
    else:
        parts = [jnp.sum(jnp.where(is_ctx, 0.0, val), axis=0, keepdims=True),
                 jnp.sum(jnp.where(is_ctx, val, 0.0), axis=0, keepdims=True)]

    @pl.when(first)
    def _():
        for k, p in enumerate(parts):
            ref[k] = p

    @pl.when(jnp.logical_not(first))
    def _():
        for k, p in enumerate(parts):
            ref[k] += p


def _w_out_resid(a, w_out, xres, gate, nx, name):
    m, k = a.shape
    n = w_out.shape[1]
    nseg = gate.shape[0]
    tm = _row_tile(m)

    def body(a_ref, w_ref, x_ref, gt_ref, yx_ref, xo_ref):
        yx = jnp.dot(a_ref[...], w_ref[...], preferred_element_type=F32)
        yx_ref[...] = yx.astype(ACT)
        xo_ref[...] = x_ref[...] + _row_vec(gt_ref, _ctx_rows(pl.program_id(0), tm, nx, nseg)) * yx

    row = pl.BlockSpec((tm, n), lambda i: (i, 0))
    return pl.pallas_call(
        body, grid=(m // tm,),
        in_specs=[pl.BlockSpec((tm, k), lambda i: (i, 0)), pl.BlockSpec((k, n), lambda i: (0, 0)), row,
                  pl.BlockSpec((nseg, 1, n), lambda i: (0, 0, 0))],
        out_specs=[row, row], out_shape=[_sds((m, n), ACT), _sds((m, n), F32)],
        name=name, compiler_params=_cparams("parallel"),
    )(a, w_out, xres, gate)


def _norm_w_in(x, g, scale, shift, w_in, nx, name, ctx=None):
    d = x.shape[1]
    rows = x.shape[0] + (0 if ctx is None else ctx.shape[0])
    n = w_in.shape[1]
    nseg = scale.shape[0]
    tm = _row_tile(rows)
    tn = n
    while tm * tn * jnp.dtype(ACT).itemsize > MAX_OUT_BLOCK_BYTES:
        tn //= 2
    row = pl.BlockSpec((tm, d), lambda i, j: (i, 0))
    if ctx is None:
        row_args, row_specs = (x,), [row]
    else:
        assert ctx.shape[0] == ROW_BLOCK and tm % ROW_BLOCK == 0 and nx % ROW_BLOCK == 0
        nsub, x_blocks = tm // ROW_BLOCK, nx // ROW_BLOCK
        row_args = (x,) * nsub + (ctx,)
        row_specs = [pl.BlockSpec((ROW_BLOCK, d), lambda i, j, s=s: (jnp.minimum(i * nsub + s, x_blocks - 1), 0))
                     for s in range(nsub)] + [pl.BlockSpec((ROW_BLOCK, d), lambda i, j: (0, 0))]

    def body(*refs):
        x_refs, (g_ref, sc_ref, sh_ref, w_ref), outs = refs[:len(row_args)], refs[len(row_args):][:4], refs[-3:]
        h_ref, r_ref, p_ref = outs
        i, j = pl.program_id(0), pl.program_id(1)

        @pl.when(j == 0)
        def _():
            if ctx is None:
                xv = x_refs[0][...]
            else:
                xv = jnp.concatenate([jnp.where(i * nsub + s >= x_blocks, x_refs[-1][...], x_refs[s][...])
                                      for s in range(nsub)], axis=0)
                refs[-4][...] = xv
            r = lax.rsqrt(jnp.mean(xv * xv, axis=-1, keepdims=True) + EPS)
            is_ctx = _ctx_rows(i, tm, nx, nseg)
            h = (xv * r) * g_ref[...] * (1.0 + _row_vec(sc_ref, is_ctx)) + _row_vec(sh_ref, is_ctx)
            h_ref[...] = h.astype(BF16)
            r_ref[...] = r

        p_ref[...] = jnp.dot(h_ref[...], w_ref[...], preferred_element_type=F32).astype(ACT)

    vec = pl.BlockSpec((nseg, 1, d), lambda i, j: (0, 0, 0))
    joined = [] if ctx is None else [(row, _sds((rows, d), F32))]
    out_specs, out_shape = zip(*joined, (row, _sds((rows, d), BF16)),
                               (pl.BlockSpec((tm, 1), lambda i, j: (i, 0)), _sds((rows, 1), F32)),
                               (pl.BlockSpec((tm, tn), lambda i, j: (i, j)), _sds((rows, n), ACT)))
    return _call(
        body, (*row_args, g, scale, shift, w_in), grid=(rows // tm, n // tn),
        in_specs=[*row_specs, pl.BlockSpec((1, d), lambda i, j: (0, 0)), vec, vec,
                  pl.BlockSpec((d, tn), lambda i, j: (0, j))],
        out_specs=list(out_specs), out_shape=list(out_shape), name=name)


def _gate_w_out_bwd(dxo, yx, gate, w_out, nx, name):
    rows, d = yx.shape
    w = w_out.shape[0]
    nseg = gate.shape[0]
    tm = _row_tile(rows)

    def body(dx_ref, yx_ref, gt_ref, w_ref, dyx_ref, da_ref, dg_ref):
        i = pl.program_id(0)
        is_ctx = _ctx_rows(i, tm, nx, nseg)
        dxv = dx_ref[...]
        dyx = (dxv * _row_vec(gt_ref, is_ctx)).astype(BF16)
        dyx_ref[...] = dyx
        da_ref[...] = lax.dot_general(dyx, w_ref[...], _DIMS["nt"], preferred_element_type=F32).astype(ACT)
        _seg_sums(dg_ref, dxv * yx_ref[...].astype(F32), is_ctx, i == 0)

    row = pl.BlockSpec((tm, d), lambda i: (i, 0))
    vec = pl.BlockSpec((nseg, 1, d), lambda i: (0, 0, 0))
    return _call(
        body, (dxo, yx, gate, w_out), grid=(rows // tm,),
        in_specs=[row, row, vec, pl.BlockSpec((w, d), lambda i: (0, 0))],
        out_specs=[row, pl.BlockSpec((tm, w), lambda i: (i, 0)), vec],
        out_shape=[_sds((rows, d), BF16), _sds((rows, w), ACT), _sds((nseg, 1, d), F32)], name=name)


def _w_in_bwd_norm(dparts, w_in, x, r, g, scale, dres, nx, name, dx_rows=None):
    np_, rows, kp = dparts.shape
    d = w_in.shape[0]
    nseg = scale.shape[0]
    tm = _row_tile(rows)
    assert dx_rows is None or rows - tm < dx_rows <= rows
    nsub = tm // ROW_BLOCK
    nres_blocks = dres.shape[0] // ROW_BLOCK
    pp = 2 if np_ % 2 == 0 else 1
    nk = np_ // pp

    def body(dp_ref, w_ref, x_ref, r_ref, g_ref, sc_ref, *rest):
        dres_refs = rest[:nsub]
        dx_ref, dsh_ref, dge_ref, *acc = rest[nsub:]
        i, k = pl.program_id(0), pl.program_id(1)
        prod = sum(lax.dot_general(dp_ref[q], w_ref[:, q * kp:(q + 1) * kp], _DIMS["nt"], preferred_element_type=F32)
                   for q in range(pp))

        def finish(dhv):
            is_ctx = _ctx_rows(i, tm, nx, nseg)
            rv = r_ref[...]
            xn = x_ref[...] * rv
            dxn = dhv * (g_ref[...] * (1.0 + _row_vec(sc_ref, is_ctx)))
            dx = rv * (dxn - xn * jnp.mean(dxn * xn, axis=-1, keepdims=True))
            for s in range(nsub):
                piece = slice(s * ROW_BLOCK, (s + 1) * ROW_BLOCK)
                res = dres_refs[s][...]
                if nres_blocks * ROW_BLOCK < rows:
                    res = jnp.where(i * nsub + s < nres_blocks, res, 0.0)
                dx_ref[piece, :] = dx[piece, :] + res
            _seg_sums(dsh_ref, dhv, is_ctx, i == 0)
            _seg_sums(dge_ref, dhv * xn, is_ctx, i == 0)

        if nk == 1:
            finish(prod)
        else:
            acc_ref, = acc

            @pl.when(k == 0)
            def _():
                acc_ref[...] = prod

            @pl.when(k > 0)
            def _():
                acc_ref[...] += prod

            @pl.when(k == nk - 1)
            def _():
                finish(acc_ref[...])

    row = pl.BlockSpec((tm, d), lambda i, k: (i, 0))
    vec = pl.BlockSpec((nseg, 1, d), lambda i, k: (0, 0, 0))
    return _call(
        body, (dparts, w_in, x, r, g, scale, *([dres] * nsub)), grid=(rows // tm, nk),
        in_specs=[pl.BlockSpec((pp, tm, kp), lambda i, k: (k, i, 0)), pl.BlockSpec((d, pp * kp), lambda i, k: (0, k)),
                  row, pl.BlockSpec((tm, 1), lambda i, k: (i, 0)), pl.BlockSpec((1, d), lambda i, k: (0, 0)), vec]
        + [pl.BlockSpec((ROW_BLOCK, d), (lambda i, k, s=s: (jnp.minimum(i * nsub + s, nres_blocks - 1), 0)))
           for s in range(nsub)],
        out_specs=[row, vec, vec],
        out_shape=[_sds((dx_rows or rows, d), F32), _sds((nseg, 1, d), F32), _sds((nseg, 1, d), F32)],
        scratch_shapes=[pltpu.VMEM((tm, d), F32)] * (nk > 1), name=name)


_PAD_TOP = 16
_PAD_BOT = 32


def _window_sum(buf, xv, lo, n):
    t = xv.shape[0]
    c = xv.shape[1]
    tp = t + _PAD_TOP + _PAD_BOT
    buf[pl.ds(0, _PAD_TOP), :] = jnp.zeros((_PAD_TOP, c), F32)
    buf[pl.ds(_PAD_TOP, t), :] = xv
    buf[pl.ds(_PAD_TOP + t, _PAD_BOT), :] = jnp.zeros((_PAD_BOT, c), F32)
    p = buf[...]
    k = 1
    while k < n:
        p = p + pltpu.roll(p, tp - k, 0)
        k *= 2
    if lo:
        p = pltpu.roll(p, -lo, 0)
    buf[...] = p
    return buf[pl.ds(_PAD_TOP, t), :]


def _window_count(t, half):
    pos = lax.broadcasted_iota(jnp.int32, (t, 1), 0)
    return (jnp.minimum(pos + half, t) - jnp.maximum(pos - half, 0)).astype(F32)


def _segments(rows, nx):
    return [(0, nx)] + ([(nx, rows - nx)] if rows > nx else [])


def _pool_scratch(rows, nx, cols):
    return [pltpu.VMEM((length + _PAD_TOP + _PAD_BOT, cols), F32) for _, length in _segments(rows, nx)]


def _per_group(g, fn):
    for gi, win in enumerate(POOL_WINDOWS):
        pl.when(g == gi)(functools.partial(fn, win))


def _pool_grp_fwd(uv, w_grp, scale, nx, name):
    rows = uv.shape[0]
    ng, gc, _ = w_grp.shape
    w = ng * gc
    segs = _segments(rows, nx)

    def body(u_ref, gt_ref, w_ref, sc_ref, z_ref, mx_ref, a_ref, *bufs):
        def pool(win):
            half = win // 2
            for (start, length), buf in zip(segs, bufs):
                uvv = u_ref[pl.ds(start, length), :].astype(F32)
                s = _window_sum(buf, uvv, -half, win)
                z_ref[pl.ds(start, length), :] = (s / _window_count(length, half) - uvv).astype(BF16)

        _per_group(pl.program_id(0), pool)
        mixed = jnp.dot(z_ref[...], w_ref[...], preferred_element_type=F32)
        mx_ref[...] = mixed.astype(ACT)
        a_ref[...] = (mixed * sc_ref[...] * _silu(gt_ref[...].astype(F32))).astype(BF16)

    col = pl.BlockSpec((rows, gc), lambda g: (0, g))
    return _call(
        body, (uv, uv, w_grp, scale), grid=(ng,),
        in_specs=[col, pl.BlockSpec((rows, gc), lambda g: (0, ng + g)), pl.BlockSpec((None, gc, gc), lambda g: (g, 0, 0)),
                  pl.BlockSpec((1, gc), lambda g: (0, g))],
        out_specs=[col, col, col], out_shape=[_sds((rows, w), BF16), _sds((rows, w), ACT), _sds((rows, w), BF16)],
        scratch_shapes=_pool_scratch(rows, nx, gc), name=name)


def _pool_grp_bwd(da, mixed, uv, scale, w_grp, nx, name):
    rows, w = da.shape
    ng, gc, _ = w_grp.shape
    segs = _segments(rows, nx)

    def body(da_ref, mx_ref, gt_ref, sc_ref, w_ref, dm_ref, duv_ref, dsc_ref, dz_ref, *bufs):
        dav = da_ref[...].astype(F32)
        mixed = mx_ref[...].astype(F32)
        gt = gt_ref[...].astype(F32)
        sg = _silu(gt)
        sc = sc_ref[...]
        dm = (dav * sc * sg).astype(BF16)
        dm_ref[...] = dm
        dz_ref[...] = lax.dot_general(dm, w_ref[...], _DIMS["nt"], preferred_element_type=F32)
        duv_ref[1] = (dav * mixed * sc * _dsilu(gt)).astype(BF16)
        dsc_ref[...] = jnp.sum(dav * mixed * sg, axis=0, keepdims=True)

        def unpool(win):
            half = win // 2
            for (start, length), buf in zip(segs, bufs):
                dzv = dz_ref[pl.ds(start, length), :]
                s = _window_sum(buf, dzv / _window_count(length, half), 1 - half, win)
                duv_ref[0, pl.ds(start, length), :] = (s - dzv).astype(BF16)

        _per_group(pl.program_id(0), unpool)

    col = pl.BlockSpec((rows, gc), lambda g: (0, g))
    vec = pl.BlockSpec((1, gc), lambda g: (0, g))
    return pl.pallas_call(
        body, grid=(ng,),
        in_specs=[col, col, pl.BlockSpec((rows, gc), lambda g: (0, ng + g)), vec,
                  pl.BlockSpec((None, gc, gc), lambda g: (g, 0, 0))],
        out_specs=[col, pl.BlockSpec((2, rows, gc), lambda g: (0, 0, g)), vec],
        out_shape=[_sds((rows, w), BF16), _sds((2, rows, w), BF16), _sds((1, w), F32)],
        scratch_shapes=[pltpu.VMEM((rows, gc), F32)] + _pool_scratch(rows, nx, gc),
        name=name, compiler_params=_cparams("parallel"),
    )(da, mixed, uv, scale, w_grp)


def _grp_wgrad(z, dm, ng, name, out_dtype):
    rows, w = z.shape
    gc = w // ng

    def body(z_ref, dm_ref, o_ref):
        o_ref[...] = lax.dot_general(z_ref[...], dm_ref[...], _DIMS["tn"],
                                     preferred_element_type=F32).astype(o_ref.dtype)

    blk = pl.BlockSpec((rows, gc), lambda g: (0, g))
    return pl.pallas_call(
        body, grid=(ng,), in_specs=[blk, blk], out_specs=pl.BlockSpec((None, gc, gc), lambda g: (g, 0, 0)),
        out_shape=_sds((ng, gc, gc), out_dtype), name=name, compiler_params=_cparams("parallel"),
    )(z, dm)


def _shift_rows(v, by):
    t = v.shape[0]
    pos = lax.broadcasted_iota(jnp.int32, v.shape, 0)
    rolled = pltpu.roll(v, by % t, 0)
    keep = pos >= by if by > 0 else pos < t + by
    return jnp.where(keep, rolled, 0.0)


def _conv_specs(t, w, cb):
    return [pl.BlockSpec((t, cb), (lambda j, q=q: (0, q * (w // cb) + j))) for q in range(4)]


def _conv_fwd(p4, dw, db, name):
    t = p4.shape[0]
    w = p4.shape[1] // 4
    cb = 128

    def body(bg_ref, cg_ref, v_ref, g_ref, dw_ref, db_ref, a_ref):
        tv = cg_ref[...].astype(F32) * v_ref[...].astype(F32)
        conv = (dw_ref[0:1, :] * _shift_rows(tv, 1) + dw_ref[1:2, :] * tv + dw_ref[2:3, :] * _shift_rows(tv, -1)
                + db_ref[...])
        a_ref[...] = (bg_ref[...].astype(F32) * conv * _silu(g_ref[...].astype(F32))).astype(BF16)

    return pl.pallas_call(
        body, grid=(w // cb,),
        in_specs=_conv_specs(t, w, cb) + [pl.BlockSpec((3, cb), lambda j: (0, j)), pl.BlockSpec((1, cb), lambda j: (0, j))],
        out_specs=pl.BlockSpec((t, cb), lambda j: (0, j)), out_shape=_sds((t, w), BF16),
        name=name, compiler_params=_cparams("parallel"),
    )(p4, p4, p4, p4, dw, db)


def _conv_bwd(da, p4, dw, db, name):
    t, w = da.shape
    cb = 128

    def body(da_ref, bg_ref, cg_ref, v_ref, g_ref, dw_ref, db_ref, d4_ref, ddw_ref, ddb_ref):
        cg = cg_ref[...].astype(F32)
        vv = v_ref[...].astype(F32)
        bg = bg_ref[...].astype(F32)
        gv = g_ref[...].astype(F32)
        tv = cg * vv
        tm1 = _shift_rows(tv, 1)
        tp1 = _shift_rows(tv, -1)
        w0, w1, w2 = dw_ref[0:1, :], dw_ref[1:2, :], dw_ref[2:3, :]
        conv = w0 * tm1 + w1 * tv + w2 * tp1 + db_ref[...]
        y = bg * conv
        dav = da_ref[...].astype(F32)
        dy = dav * _silu(gv)
        d4_ref[3] = (dav * y * _dsilu(gv)).astype(BF16)
        d4_ref[0] = (dy * conv).astype(BF16)
        dconv = dy * bg
        ddb_ref[...] = jnp.sum(dconv, axis=0, keepdims=True)
        ddw_ref[0:1, :] = jnp.sum(dconv * tm1, axis=0, keepdims=True)
        ddw_ref[1:2, :] = jnp.sum(dconv * tv, axis=0, keepdims=True)
        ddw_ref[2:3, :] = jnp.sum(dconv * tp1, axis=0, keepdims=True)
        dt = w0 * _shift_rows(dconv, -1) + w1 * dconv + w2 * _shift_rows(dconv, 1)
        d4_ref[1] = (dt * vv).astype(BF16)
        d4_ref[2] = (dt * cg).astype(BF16)

    col = pl.BlockSpec((t, cb), lambda j: (0, j))
    tap = pl.BlockSpec((3, cb), lambda j: (0, j))
    bias = pl.BlockSpec((1, cb), lambda j: (0, j))
    return pl.pallas_call(
        body, grid=(w // cb,), in_specs=[col] + _conv_specs(t, w, cb) + [tap, bias],
        out_specs=[pl.BlockSpec((4, t, cb), lambda j: (0, 0, j)), tap, bias],
        out_shape=[_sds((4, t, w), BF16), _sds((3, w), F32), _sds((1, w), F32)],
        name=name, compiler_params=_cparams("parallel"),
    )(da, p4, p4, p4, p4, dw, db)


def _attn_mask():
    qn, kn = Q_ROWS * GRID_W, K_ROWS * GRID_W
    qr, qc = np.divmod(np.arange(qn), GRID_W)
    kr, kc = np.divmod(np.arange(kn), GRID_W)
    col0 = np.clip(qc - WIN_COLS // 2, 0, GRID_W - WIN_COLS)
    col_ok = (kc[None, :] >= col0[:, None]) & (kc[None, :] < col0[:, None] + WIN_COLS)
    first = np.zeros(qn, np.int64)
    last = np.full(qn, K_ROWS - WIN_ROWS)
    out = []
    for row0 in (first, qr, last):
        row_ok = (kr[None, :] >= row0[:, None]) & (kr[None, :] < row0[:, None] + WIN_ROWS)
        out.append(np.where(row_ok & col_ok, 0.0, NEG))
    return jnp.asarray(np.stack(out), F32)


_KW = K_ROWS * GRID_W
_QB = Q_ROWS * GRID_W
_PAIR = 2 * HEAD_DIM
_N_DR = 2 * WIN_ROWS - 1
_N_DC = 2 * WIN_COLS - 1
_RP_ROWS = 24
_N_TILES = _N_DR + 1
_BIAS_BASE = (WIN_ROWS - 1, WIN_ROWS // 2 - 1, -1)


class _Comm:
    def __init__(self, ins, outs, sems, start, finish):
        self.ins, self.outs, self.sems, self.start, self.finish = list(ins), list(outs), list(sems), start, finish


def _bias_pieces(cls):
    out = []
    for qr in range(Q_ROWS):
        for kr in range(0, K_ROWS, 2):
            tile = _BIAS_BASE[cls] - qr + kr + 1
            out.append((qr, kr, tile if 0 <= tile < _N_TILES else None))
    return out


def _toeplitz_pair(left_row, right_row):
    lane = lax.broadcasted_iota(jnp.int32, (GRID_W, _PAIR), 1)
    shape = (GRID_W, _PAIR)
    left = pltpu.roll(jnp.broadcast_to(left_row, shape), _PAIR - (WIN_COLS - 1), 1, stride=1, stride_axis=0)
    right = pltpu.roll(jnp.broadcast_to(right_row, shape), GRID_W - (WIN_COLS - 1), 1, stride=1, stride_axis=0)
    return jnp.where(lane < GRID_W, left, right)


def _build_tiles(tiles_ref, rp_ref):
    for h in range(2):
        for t in range(_N_TILES):
            tiles_ref[h, t] = _toeplitz_pair(rp_ref[h, t:t + 1, :], rp_ref[h, t + 1:t + 2, :])


def _block_class(b, nblk, fn, entering=False):
    interior = (b == 1) if entering else jnp.logical_and(b > 0, b < nblk - 1)
    for cls, cond in enumerate((b == 0, interior, b == nblk - 1)):
        pl.when(cond)(functools.partial(fn, cls))


def _attn_geometry(p4, nx):
    rows = p4.shape[0]
    w = p4.shape[1] // 4
    nhp = w // _PAIR
    nblk = nx // _QB
    qspec = lambda col: pl.BlockSpec((_QB, _PAIR), lambda hp, b: (b, col * nhp + hp))
    kspec = lambda col: pl.BlockSpec((rows, _PAIR), lambda hp, b: (0, col * nhp + hp))
    tspec = pl.BlockSpec((2, _RP_ROWS, _PAIR), lambda hp, b: (hp, 0, 0))
    mspec = pl.BlockSpec((None, _QB, _KW), lambda hp, b: (jnp.where(b == 0, 0, jnp.where(b == nblk - 1, 2, 1)), 0, 0))
    lspec = pl.BlockSpec((None, _QB, 2), lambda hp, b: (hp, b, 0))
    ospec = pl.BlockSpec((_QB, _PAIR), lambda hp, b: (b, hp))
    return rows, w, nhp, nblk, qspec, kspec, tspec, mspec, lspec, ospec


def _window_start(b, nx):
    return pl.multiple_of(jnp.clip(b * _QB - PAD_ROWS * GRID_W, 0, nx - _KW), _QB)


def _load_bias(bias_ref, tiles_ref, rp_ref, m_ref, b, nblk):
    pl.when(b == 0)(lambda: _build_tiles(tiles_ref, rp_ref))

    def fill(cls):
        for h in range(2):
            for qr, kr, tile in _bias_pieces(cls):
                rows = slice(qr * GRID_W, (qr + 1) * GRID_W)
                cols = slice(kr * GRID_W, (kr + 2) * GRID_W)
                m = m_ref[rows, cols]
                bias_ref[h, rows, cols] = m if tile is None else tiles_ref[h, tile] + m

    _block_class(b, nblk, fill, entering=True)


def _attn_fwd(p4, rp, mask, nx, name, comm=None):
    rows, w, nhp, nblk, qspec, kspec, tspec, mspec, lspec, ospec = _attn_geometry(p4, nx)
    n_ctx = rows - nx
    n_cin, n_cout = (len(comm.ins), len(comm.outs)) if comm else (0, 0)

    def body(*refs):
        q_ref, k_ref, v_ref, g_ref, rp_ref, m_ref = refs[:6]
        cin = refs[6:6 + n_cin]
        a_ref, o_ref, lse_ref = refs[6 + n_cin:9 + n_cin]
        cout = refs[9 + n_cin:9 + n_cin + n_cout]
        bias_ref, tiles_ref = refs[9 + n_cin + n_cout:11 + n_cin + n_cout]
        sems = refs[11 + n_cin + n_cout:]
        hp, b = pl.program_id(0), pl.program_id(1)
        if comm:
            pl.when(jnp.logical_and(hp == 0, b == 0))(lambda: comm.start(cin, cout, sems))
        start = _window_start(b, nx)
        _load_bias(bias_ref, tiles_ref, rp_ref, m_ref, b, nblk)
        qf = q_ref[...].astype(F32) * HEAD_DIM ** -0.5
        kw = k_ref[pl.ds(start, _KW), :].astype(BF16)
        vw = v_ref[pl.ds(start, _KW), :].astype(BF16)
        kcv = k_ref[pl.ds(nx, n_ctx), :].astype(BF16)
        vcv = v_ref[pl.ds(nx, n_ctx), :].astype(BF16)
        lane = lax.broadcasted_iota(jnp.int32, (1, _PAIR), 1)
        outs, lses = [], []
        for h in range(2):
            mine = (lane >= HEAD_DIM) if h else (lane < HEAD_DIM)
            qm = jnp.where(mine, qf, 0.0).astype(BF16)
            s_loc = lax.dot_general(qm, kw, _DIMS["nt"], preferred_element_type=F32) + bias_ref[h]
            s_ctx = lax.dot_general(qm, kcv, _DIMS["nt"], preferred_element_type=F32)
            mx = jnp.maximum(jnp.max(s_loc, axis=-1, keepdims=True), jnp.max(s_ctx, axis=-1, keepdims=True))
            p_loc = jnp.exp(s_loc - mx)
            p_ctx = jnp.exp(s_ctx - mx)
            den = jnp.sum(p_loc, axis=-1, keepdims=True) + jnp.sum(p_ctx, axis=-1, keepdims=True)
            o = jnp.dot(p_loc.astype(BF16), vw, preferred_element_type=F32)
            o = o + jnp.dot(p_ctx.astype(BF16), vcv, preferred_element_type=F32)
            outs.append(o * (1.0 / den))
            lses.append(mx + jnp.log(den))
        o = jnp.where(lane < HEAD_DIM, outs[0], outs[1])
        o_ref[...] = o.astype(ACT)
        a_ref[...] = (o * _silu(g_ref[...].astype(F32))).astype(BF16)
        col = lax.broadcasted_iota(jnp.int32, (1, 2), 1)
        lse_ref[...] = jnp.where(col == 0, lses[0], lses[1])
        if comm:
            pl.when(jnp.logical_and(hp == nhp - 1, b == nblk - 1))(lambda: comm.finish(cin, cout, sems))

    res = pl.pallas_call(
        body, grid=(nhp, nblk),
        in_specs=[qspec(0), kspec(1), kspec(2), qspec(3), tspec, mspec] + [HBM_SPEC] * n_cin,
        out_specs=[ospec, ospec, lspec] + [HBM_SPEC] * n_cout,
        out_shape=[_sds((nx, w), BF16), _sds((nx, w), ACT), _sds((nhp, nx, 2), F32)] + (comm.outs if comm else []),
        scratch_shapes=[pltpu.VMEM((2, _QB, _KW), F32), pltpu.VMEM((2, _N_TILES, GRID_W, _PAIR), F32)]
        + (comm.sems if comm else []),
        name=name, compiler_params=_cparams("arbitrary", "arbitrary"),
    )(p4, p4, p4, p4, rp, mask, *(comm.ins if comm else []))
    return res[:3], res[3:]


def _fold_tiles(dtiles_ref, drp_ref):
    shape = (GRID_W, _PAIR)
    lane = lax.broadcasted_iota(jnp.int32, shape, 1)
    flip = (lax.broadcasted_iota(jnp.int32, (_PAIR, _PAIR), 0)
            + lax.broadcasted_iota(jnp.int32, (_PAIR, _PAIR), 1) == _PAIR - 1).astype(F32)
    drp_ref[...] = jnp.zeros(drp_ref.shape, F32)
    for h in range(2):
        stack = dtiles_ref[h].reshape(_N_TILES * GRID_W, _PAIR)
        rev = jnp.dot(stack, flip, precision=lax.Precision.HIGHEST, preferred_element_type=F32)
        for t in range(_N_TILES):
            tile = rev[t * GRID_W:(t + 1) * GRID_W, :]
            for side in (0, 1):
                shift = _PAIR - GRID_W * side - (WIN_COLS - 1)
                half = jnp.where((lane < GRID_W) if side else (lane >= GRID_W), tile, 0.0)
                diag = pltpu.roll(half, shift, 1, stride=1, stride_axis=0)
                drp_ref[h, t + side:t + side + 1, :] += jnp.sum(diag, axis=0, keepdims=True)


def _attn_bwd(p4, rp, mask, o, lse, da, nx, name, comm=None):
    rows, w, nhp, nblk, qspec, kspec, tspec, mspec, lspec, ospec = _attn_geometry(p4, nx)
    n_ctx = rows - nx
    n_cin, n_cout = (len(comm.ins), len(comm.outs)) if comm else (0, 0)

    def body(*refs):
        q_ref, k_ref, v_ref, g_ref, rp_ref, m_ref, o_ref, lse_ref, da_ref = refs[:9]
        cin = refs[9:9 + n_cin]
        d4_ref, drp_ref = refs[9 + n_cin:11 + n_cin]
        cout = refs[11 + n_cin:11 + n_cin + n_cout]
        bias_ref, tiles_ref, ds_ref, dtiles_ref, dk_ref, dv_ref = refs[11 + n_cin + n_cout:17 + n_cin + n_cout]
        sems = refs[17 + n_cin + n_cout:]
        hp, b = pl.program_id(0), pl.program_id(1)
        if comm:
            pl.when(jnp.logical_and(hp == 0, b == 0))(lambda: comm.start(cin, cout, sems))
        start = _window_start(b, nx)
        here = pl.multiple_of(b * _QB, _QB)

        @pl.when(b == 0)
        def _():
            dk_ref[...] = jnp.zeros(dk_ref.shape, F32)
            dv_ref[...] = jnp.zeros(dv_ref.shape, F32)
            dtiles_ref[...] = jnp.zeros(dtiles_ref.shape, F32)
            d4_ref[0, pl.ds(nx, n_ctx), :] = jnp.zeros((n_ctx, _PAIR), BF16)
            d4_ref[3, pl.ds(nx, n_ctx), :] = jnp.zeros((n_ctx, _PAIR), BF16)

        _load_bias(bias_ref, tiles_ref, rp_ref, m_ref, b, nblk)
        gv = g_ref[...].astype(F32)
        dav = da_ref[...].astype(F32)
        ov = o_ref[...].astype(F32)
        dov = dav * _silu(gv)
        d4_ref[3, pl.ds(here, _QB), :] = (dav * ov * _dsilu(gv)).astype(BF16)
        qf = q_ref[...].astype(F32) * HEAD_DIM ** -0.5
        kw = k_ref[pl.ds(start, _KW), :].astype(BF16)
        vw = v_ref[pl.ds(start, _KW), :].astype(BF16)
        kcv = k_ref[pl.ds(nx, n_ctx), :].astype(BF16)
        vcv = v_ref[pl.ds(nx, n_ctx), :].astype(BF16)
        lane = lax.broadcasted_iota(jnp.int32, (1, _PAIR), 1)
        dq = jnp.zeros((_QB, _PAIR), F32)
        for h in range(2):
            mine = (lane >= HEAD_DIM) if h else (lane < HEAD_DIM)
            qm = jnp.where(mine, qf, 0.0).astype(BF16)
            dom = jnp.where(mine, dov, 0.0)
            dob = dom.astype(BF16)
            lse = lse_ref[:, h:h + 1]
            s_loc = lax.dot_general(qm, kw, _DIMS["nt"], preferred_element_type=F32)
            p_loc = jnp.exp(s_loc + bias_ref[h] - lse)
            p_ctx = jnp.exp(lax.dot_general(qm, kcv, _DIMS["nt"], preferred_element_type=F32) - lse)
            delta = jnp.sum(dom * ov, axis=-1, keepdims=True)
            ds_loc = p_loc * (lax.dot_general(dob, vw, _DIMS["nt"], preferred_element_type=F32) - delta)
            ds_ctx = p_ctx * (lax.dot_general(dob, vcv, _DIMS["nt"], preferred_element_type=F32) - delta)
            dsb_loc = ds_loc.astype(BF16)
            dsb_ctx = ds_ctx.astype(BF16)
            dq_h = (jnp.dot(dsb_loc, kw, preferred_element_type=F32)
                    + jnp.dot(dsb_ctx, kcv, preferred_element_type=F32))
            dq = dq + jnp.where(mine, dq_h, 0.0)
            dk_ref[pl.ds(start, _KW), :] += lax.dot_general(dsb_loc, qm, _DIMS["tn"], preferred_element_type=F32)
            dv_ref[pl.ds(start, _KW), :] += lax.dot_general(p_loc.astype(BF16), dob, _DIMS["tn"],
                                                            preferred_element_type=F32)
            dk_ref[pl.ds(nx, n_ctx), :] += lax.dot_general(dsb_ctx, qm, _DIMS["tn"], preferred_element_type=F32)
            dv_ref[pl.ds(nx, n_ctx), :] += lax.dot_general(p_ctx.astype(BF16), dob, _DIMS["tn"],
                                                           preferred_element_type=F32)
            ds_ref[h] = ds_loc
        d4_ref[0, pl.ds(here, _QB), :] = (dq * HEAD_DIM ** -0.5).astype(BF16)

        def scatter(cls):
            for h in range(2):
                for qr, kr, tile in _bias_pieces(cls):
                    if tile is not None:
                        dtiles_ref[h, tile] += ds_ref[h, qr * GRID_W:(qr + 1) * GRID_W, kr * GRID_W:(kr + 2) * GRID_W]

        _block_class(b, nblk, scatter)

        @pl.when(b == nblk - 1)
        def _():
            d4_ref[1] = dk_ref[...].astype(BF16)
            d4_ref[2] = dv_ref[...].astype(BF16)
            _fold_tiles(dtiles_ref, drp_ref)

        if comm:
            pl.when(jnp.logical_and(hp == nhp - 1, b == nblk - 1))(lambda: comm.finish(cin, cout, sems))

    tiles = pltpu.VMEM((2, _N_TILES, GRID_W, _PAIR), F32)
    block = pltpu.VMEM((2, _QB, _KW), F32)
    res = pl.pallas_call(
        body, grid=(nhp, nblk),
        in_specs=[qspec(0), kspec(1), kspec(2), qspec(3), tspec, mspec, ospec, lspec, ospec] + [HBM_SPEC] * n_cin,
        out_specs=[pl.BlockSpec((4, rows, _PAIR), lambda hp, b: (0, 0, hp)), tspec] + [HBM_SPEC] * n_cout,
        out_shape=[_sds((4, rows, w), BF16), _sds(rp.shape, F32)] + (comm.outs if comm else []),
        scratch_shapes=[block, tiles, block, tiles, pltpu.VMEM((rows, _PAIR), F32), pltpu.VMEM((rows, _PAIR), F32)]
        + (comm.sems if comm else []),
        name=name, compiler_params=_cparams("arbitrary", "arbitrary"),
    )(p4, p4, p4, p4, rp, mask, o, lse, da, *(comm.ins if comm else []))
    return res[:2], res[2:]


def _w_out_loss(a, w_out, xres, gate, g, target, name):
    m, k = a.shape
    d = w_out.shape[1]
    assert gate.shape[0] == 1
    tm = _row_tile(m)
    nblk = m // tm

    def body(a_ref, w_ref, x_ref, gt_ref, g_ref, t_ref, yx_ref, loss_ref, dx_ref, dg_ref, acc_ref):
        i = pl.program_id(0)
        yx = jnp.dot(a_ref[...], w_ref[...], preferred_element_type=F32)
        yx_ref[...] = yx.astype(ACT)
        xv = x_ref[...] + gt_ref[0] * yx
        gv = g_ref[...]
        r = lax.rsqrt(jnp.mean(xv * xv, axis=-1, keepdims=True) + EPS)
        xn = xv * r
        err = xn * gv - t_ref[...]
        dy = err * (1.0 / d)
        dxn = dy * gv
        dx_ref[...] = r * (dxn - xn * jnp.mean(dxn * xn, axis=-1, keepdims=True))
        s_g = jnp.sum(dy * xn, axis=0, keepdims=True)
        s_l = jnp.sum(jnp.mean(err * err, axis=-1, keepdims=True), axis=0, keepdims=True)

        @pl.when(i == 0)
        def _():
            dg_ref[...] = s_g
            acc_ref[...] = s_l

        @pl.when(i > 0)
        def _():
            dg_ref[...] += s_g
            acc_ref[...] += s_l

        @pl.when(i == nblk - 1)
        def _():
            loss_ref[...] = jnp.broadcast_to(0.5 * acc_ref[...], loss_ref.shape)

    row = pl.BlockSpec((tm, d), lambda i: (i, 0))
    vec = pl.BlockSpec((1, d), lambda i: (0, 0))
    return pl.pallas_call(
        body, grid=(nblk,),
        in_specs=[pl.BlockSpec((tm, k), lambda i: (i, 0)), pl.BlockSpec((k, d), lambda i: (0, 0)), row,
                  pl.BlockSpec((1, 1, d), lambda i: (0, 0, 0)), vec, row],
        out_specs=[row, pl.BlockSpec((1, 128), lambda i: (0, 0)), row, vec],
        out_shape=[_sds((m, d), ACT), _sds((1, 128), F32), _sds((m, d), F32), _sds((1, d), F32)],
        scratch_shapes=[pltpu.VMEM((1, 1), F32)], name=name, compiler_params=_cparams("arbitrary"),
    )(a, w_out, xres, gate, g, target)


def _as2d(a):
    if a.ndim == 1:
        return a.reshape(-1, 128) if a.shape[0] % 128 == 0 else a.reshape(1, -1)
    return a.reshape(-1, a.shape[-1])


def _adamw(w, g, m, v, name):
    shape = w.shape
    w2, g2, m2, v2 = (_as2d(t) for t in (w, g.reshape(shape), m, v))
    rows, cols = w2.shape
    tr = 512 if rows % 512 == 0 else rows
    c1 = 1.0 - ADAM_B1 ** ADAM_STEP
    c2 = 1.0 - ADAM_B2 ** ADAM_STEP

    def body(w_ref, g_ref, m_ref, v_ref, d_ref, nm_ref, nv_ref):
        gv = g_ref[...]
        nm = ADAM_B1 * m_ref[...] + (1.0 - ADAM_B1) * gv
        nv = ADAM_B2 * v_ref[...] + (1.0 - ADAM_B2) * (gv * gv)
        nm_ref[...] = nm
        nv_ref[...] = nv
        d_ref[...] = -ADAM_LR * ((nm / c1) / (jnp.sqrt(nv / c2) + ADAM_EPS) + ADAM_WD * w_ref[...])

    blk = pl.BlockSpec((tr, cols), lambda i: (i, 0))
    outs = _call(body, (w2, g2, m2, v2), grid=(rows // tr,), in_specs=[blk] * 4, out_specs=[blk] * 3,
                 out_shape=[_sds((rows, cols), F32)] * 3, name=name)
    return tuple(t.reshape(shape) for t in outs)


def _sum_lead(x, name, out_dtype=F32):
    n, rows, cols = x.shape
    tr = 512 if rows % 512 == 0 else rows

    def body(x_ref, o_ref):
        acc = x_ref[0].astype(F32)
        for k in range(1, n):
            acc = acc + x_ref[k].astype(F32)
        o_ref[...] = acc.astype(out_dtype)

    return pl.pallas_call(
        body, grid=(rows // tr,), in_specs=[pl.BlockSpec((n, tr, cols), lambda i: (0, i, 0))],
        out_specs=pl.BlockSpec((tr, cols), lambda i: (i, 0)), out_shape=_sds((rows, cols), out_dtype),
        name=name, compiler_params=_cparams("parallel"),
    )(x)


def _seg_vecs(mod_l, which, nseg):
    return mod_l[:nseg, which][:, None, :]


def _norm_grads(dshift, dgeff, dgate, g, scale):
    nseg, _, d = dshift.shape
    dmod = jnp.stack([dshift[:, 0], dgeff[:, 0] * g, dgate[:, 0]], axis=1)
    if nseg == 1:
        dmod = jnp.concatenate([dmod, jnp.zeros((1, 3, d), F32)], axis=0)
    dg = jnp.sum(dgeff[:, 0] * (1.0 + scale[:, 0]), axis=0)
    return dmod, dg


def _pool_layer(xin, g, mod_l, w_in, w_grp, w_out, pscale, nx, tag, ctx=None, head=None):
    nseg = 1 if ctx is None else 2
    shift, scale, gate = (_seg_vecs(mod_l, k, nseg) for k in range(3))
    *joined, h, r, uv = _norm_w_in(xin, g, scale, shift, w_in, nx, f"w_in_fwd_{tag}", ctx)
    if joined:
        xin, = joined
    z, mixed, a = _pool_grp_fwd(uv, w_grp, pscale, nx, f"pool_fwd_{tag}")
    if head is None:
        yx, xout = _w_out_resid(a, w_out, xin, gate, nx, f"w_out_fwd_{tag}")
    else:
        yx, *xout = _w_out_loss(a, w_out, xin, gate, *head, f"w_out_loss_{tag}")

    def backward(dxo, token=None):
        gate_b = gate if token is None else gate + token[0, 0]
        dyx, da, dgate = _gate_w_out_bwd(dxo, yx, gate_b, w_out, nx, f"w_out_bwd_{tag}")
        gw_out = _mm_tn(a, dyx, f"w_out_grad_{tag}", BF16)
        dm, duv, dscale = _pool_grp_bwd(da, mixed, uv, pscale, w_grp, nx, f"pool_bwd_{tag}")
        gw_grp = _grp_wgrad(z, dm, w_grp.shape[0], f"grp_grad_{tag}", BF16)
        gw_in = _mm_tn_parts(h, duv, f"w_in_grad_{tag}", BF16)
        dx, dshift, dgeff = _w_in_bwd_norm(duv, w_in, xin, r, g, scale, dxo, nx, f"w_in_bwd_{tag}",
                                           dx_rows=None if ctx is None else nx)
        dmod, dg = _norm_grads(dshift, dgeff, dgate, g[0], scale)
        return dx, dmod, dg, dict(w_in=gw_in, w_grp=gw_grp, w_out=gw_out, scale=dscale)

    return xout, backward


def _na_layer(xc, g, mod_l, w_in, rpb, w_out, nx, mask, comm=None):
    nh, n_dr, n_dc = rpb.shape
    shift, scale = _seg_vecs(mod_l, 0, 2), _seg_vecs(mod_l, 1, 2)
    gate = _seg_vecs(mod_l, 2, 1)
    h, r, p4 = _norm_w_in(xc, g, scale, shift, w_in, nx, "w_in_fwd_na")
    rp = jnp.pad(rpb, ((0, 0), (1, _RP_ROWS - 1 - n_dr), (0, _PAIR - n_dc)))
    (a, o, lse), carried = _attn_fwd(p4, rp, mask, nx, "attn_fwd", comm)
    yx, xout = _w_out_resid(a, w_out, xc, gate, nx, "w_out_fwd_na")

    def backward(dxo, comm=None):
        dyx, da, dgate = _gate_w_out_bwd(dxo, yx, gate, w_out, nx, "w_out_bwd_na")
        gw_out = _mm_tn(a, dyx, "w_out_grad_na", BF16)
        (d4, drp), carried_bwd = _attn_bwd(p4, rp, mask, o, lse, da, nx, "attn_bwd", comm)
        gw_in = _mm_tn_parts(h, d4, "w_in_grad_na", BF16)
        dx, dshift, dgeff = _w_in_bwd_norm(d4, w_in, xc, r, g, scale, dxo, nx, "w_in_bwd_na")
        dgate2 = jnp.concatenate([dgate, jnp.zeros_like(dgate)], axis=0)
        dmod, dg = _norm_grads(dshift, dgeff, dgate2, g[0], scale)
        drpb = drp[:, 1:1 + n_dr, ::-1][:, :, :n_dc]
        return dx, dmod, dg, dict(w_in=gw_in, w_out=gw_out, rpb=drpb), carried_bwd

    return xout, backward, carried


def _conv_layer(xin, g, mod_l, w_in, dw, db, w_out):
    shift, scale, gate = (_seg_vecs(mod_l, k, 1) for k in range(3))
    nx = xin.shape[0]
    h, r, p4 = _norm_w_in(xin, g, scale, shift, w_in, nx, "w_in_fwd_conv")
    a = _conv_fwd(p4, dw, db, "conv_fwd")
    yx, xout = _w_out_resid(a, w_out, xin, gate, nx, "w_out_fwd_conv")

    def backward(dxo):
        dyx, da, dgate = _gate_w_out_bwd(dxo, yx, gate, w_out, nx, "w_out_bwd_conv")
        gw_out = _mm_tn(a, dyx, "w_out_grad_conv", BF16)
        d4, ddw, ddb = _conv_bwd(da, p4, dw, db, "conv_bwd")
        gw_in = _mm_tn_parts(h, d4, "w_in_grad_conv", BF16)
        dx, dshift, dgeff = _w_in_bwd_norm(d4, w_in, xin, r, g, scale, dxo, nx, "w_in_bwd_conv")
        dmod, dg = _norm_grads(dshift, dgeff, dgate, g[0], scale)
        return dx, dmod, dg, dict(w_in=gw_in, w_out=gw_out, dw=ddw, db=ddb)

    return xout, backward


def _example_step(x, ctx, target, mod, norm_g, final_g, wts, hooks=None):
    hooks = hooks or {}
    na_weights, late_comm, late_weights = (hooks.get(k) for k in ("na_weights", "late_comm", "late_weights"))
    nx = x.shape[0]
    consts = _attn_mask()
    g_rows = [norm_g[i:i + 1] for i in range(4)]
    xc1, bwd0 = _pool_layer(x, g_rows[0], mod[0], wts["pool_w_in"][0], wts["pool_w_grp"][0],
                            wts["pool_w_out"][0], wts["pool_scale"][0:1], nx, "p0", ctx=ctx)
    if na_weights is not None:
        wts = {**wts, **na_weights(xc1)}
    x2, bwd1, carried = _na_layer(xc1, g_rows[1], mod[1], wts["na_w_in"], wts["na_rpb"], wts["na_w_out"], nx, consts,
                                  late_comm)
    if late_weights is not None:
        wts = {**wts, **late_weights(carried)}
    x3, bwd2 = _conv_layer(x2, g_rows[2], mod[2], wts["conv_w_in"], wts["conv_dw"], wts["conv_db"], wts["conv_w_out"])
    (loss, dx4, dfinal_g), bwd3 = _pool_layer(x3, g_rows[3], mod[3], wts["pool_w_in"][1], wts["pool_w_grp"][1],
                                              wts["pool_w_out"][1], wts["pool_scale"][1:2], nx, "p3",
                                              head=(final_g, target))
    call = lambda k, *args: hooks[k](*args) if k in hooks else None
    dx3, dmod3, dg3, gr3 = bwd3(dx4)
    dx2, dmod2, dg2, gr2 = bwd2(dx3)
    dxc1, dmod1, dg1, gr1, carried_bwd = bwd1(dx2, call("grad_comm", gr3, gr2))
    dx0, dmod0, dg0, gr0 = bwd0(dxc1, call("na_grads_start", gr1))
    return dict(
        loss=loss, grad_x=dx0, dmod=jnp.stack([dmod0, dmod1, dmod2, dmod3]),
        dnorm_g=jnp.stack([dg0, dg1, dg2, dg3]), dfinal_g=dfinal_g, layers=(gr0, gr1, gr2, gr3), carried=carried_bwd)


_AXES = ("x", "y", "c")
_CHIP_FLIPS = ((1, 0), (0, 1), (1, 1))


def _position():
    return tuple(lax.axis_index(a) for a in _AXES)


def _flipped(pos, flip):
    return tuple(1 - p if f else p for p, f in zip(pos, flip))


def _join_comms(comms):
    n_in = [len(c.ins) for c in comms]
    n_out = [len(c.outs) for c in comms]
    n_sem = [len(c.sems) for c in comms]

    def parts(ins, outs, sems):
        for k in range(len(comms)):
            a, b, s = sum(n_in[:k]), sum(n_out[:k]), sum(n_sem[:k])
            yield comms[k], (ins[a:a + n_in[k]], outs[b:b + n_out[k]], sems[s:s + n_sem[k]])

    def start(ins, outs, sems):
        for c, part in parts(ins, outs, sems):
            c.start(*part)

    def finish(ins, outs, sems):
        for c, part in parts(ins, outs, sems):
            c.finish(*part)

    joint = _Comm([a for c in comms for a in c.ins], [o for c in comms for o in c.outs],
                  [s for c in comms for s in c.sems], start, finish)
    return joint, lambda res: [list(res[sum(n_out[:k]):sum(n_out[:k + 1])]) for k in range(len(comms))]


def _run_comms(comms, name):
    joint, split = _join_comms(comms)

    def body(*refs):
        n_in, n_out = len(joint.ins), len(joint.outs)
        joint.start(refs[:n_in], refs[n_in:n_in + n_out], refs[n_in + n_out:])
        joint.finish(refs[:n_in], refs[n_in:n_in + n_out], refs[n_in + n_out:])

    res = pl.pallas_call(
        body, in_specs=[HBM_SPEC] * len(joint.ins), out_specs=[HBM_SPEC] * len(joint.outs), out_shape=joint.outs,
        scratch_shapes=joint.sems, name=name,
    )(*joint.ins)
    return split(res)


def _all_gather_comm(v, axes):
    flips = [f for f in np.ndindex(2, 2, 2) if any(f) and all(a in axes or not b for a, b in zip(_AXES, f))]
    n = len(flips) + 1

    def copies(ins, outs, sems):
        (v_ref,), (o_ref,), (send_sems, recv_sems, local_sem) = ins, outs, sems
        pos = _position()
        slot = 0
        for a, p in zip(_AXES, pos):
            if a in axes:
                slot = 2 * slot + p
        local = pltpu.make_async_copy(v_ref, o_ref.at[slot], local_sem)
        remote = [pltpu.make_async_remote_copy(v_ref, o_ref.at[slot], send_sems.at[k], recv_sems.at[k],
                                               device_id=_flipped(pos, flip), device_id_type=MESH)
                  for k, flip in enumerate(flips)]
        return [local] + remote

    def start(ins, outs, sems):
        for cp in copies(ins, outs, sems):
            cp.start()

    def finish(ins, outs, sems):
        for cp in copies(ins, outs, sems):
            cp.wait()

    sems = [pltpu.SemaphoreType.DMA((n - 1,)), pltpu.SemaphoreType.DMA((n - 1,)), pltpu.SemaphoreType.DMA(())]
    return _Comm([v], [_sds((n,) + v.shape, v.dtype)], sems, start, finish)


def _all_gather_two_level_comm(v):
    def copies(ins, outs, sems, onward):
        (v_ref,), (o_ref,), (send_sems, recv_sems, local_sem) = ins, outs, sems
        x, y, c = _position()
        sibling = (x, y, 1 - c)
        slot = lambda px, py, pc: o_ref.at[4 * px + 2 * py + pc]
        own = pltpu.make_async_copy(v_ref, slot(x, y, c), local_sem)
        first = [pltpu.make_async_remote_copy(v_ref, slot(x, y, c), send_sems.at[0], recv_sems.at[0],
                                              device_id=sibling, device_id_type=MESH)]
        fwd = []
        for k, flip in enumerate(_CHIP_FLIPS):
            px, py = _flipped((x, y), flip)
            first.append(pltpu.make_async_remote_copy(v_ref, slot(x, y, c), send_sems.at[1 + k], recv_sems.at[1 + k],
                                                      device_id=(px, py, c), device_id_type=MESH))
            if onward:
                fwd.append(pltpu.make_async_remote_copy(slot(px, py, c), slot(px, py, c), send_sems.at[4 + k],
                                                        recv_sems.at[4 + k], device_id=sibling, device_id_type=MESH))
        return own, first, fwd

    def start(ins, outs, sems):
        own, first, _ = copies(ins, outs, sems, False)
        for cp in [own] + first:
            cp.start()

    def finish(ins, outs, sems):
        own, first, fwd = copies(ins, outs, sems, True)
        for arrived, onward in zip(first[1:], fwd):
            arrived.wait_recv()
            onward.start()
        first[0].wait_recv()
        for cp in fwd:
            cp.wait_recv()
        for cp in first + fwd:
            cp.wait_send()
        own.wait()

    sems = [pltpu.SemaphoreType.DMA((7,)), pltpu.SemaphoreType.DMA((7,)), pltpu.SemaphoreType.DMA(())]
    return _Comm([v], [_sds((8,) + v.shape, v.dtype)], sems, start, finish)


def _all_gather(v, axes, name):
    return _run_comms([_all_gather_comm(v, axes)], name)[0][0]


class _Item:
    def __init__(self, key, layer, shape, shard_axis, half_axis):
        self.key, self.layer, self.shape = key, layer, tuple(shape)
        self.shard_axis, self.half_axis = shard_axis, half_axis
        self.shard = shape[shard_axis] // 4
        self.half = shape[half_axis] // 2

    def sized(self, shard=False, half=False):
        s = list(self.shape)
        if shard:
            s[self.shard_axis] = self.shard
        if half:
            s[self.half_axis] = self.half
        return tuple(s)

    def window(self, ref, chip=None, half=None):
        idx = [slice(None)] * len(self.shape)
        if chip is not None:
            idx[self.shard_axis] = pl.ds(chip * self.shard, self.shard)
        if half is not None:
            idx[self.half_axis] = pl.ds(half * self.half, self.half)
        return ref.at[tuple(idx)]


def _items(d, w):
    out = []
    for j in range(2):
        out += [_Item("pool_w_in", j, (d, 2 * w), 1, 0), _Item("pool_w_grp", j, (4, w // 4, w // 4), 1, 0),
                _Item("pool_w_out", j, (w, d), 0, 1)]
    out += [_Item("na_w_in", 0, (d, 4 * w), 1, 0), _Item("na_w_out", 0, (w, d), 0, 1),
            _Item("conv_w_in", 0, (d, 4 * w), 1, 0), _Item("conv_w_out", 0, (w, d), 0, 1)]
    return out


def _gather_comm(shards, items):
    n = len(items)

    def copies(src, dst, sems, onward):
        send_a, recv_a, send_b, recv_b, send_c, recv_c = sems
        x, y, c = _position()
        chip = 2 * x + y
        sibling = (x, y, 1 - c)
        own, out, fwd, fwd_in = [], [], [], []
        for i, it in enumerate(items):
            own.append(pltpu.make_async_remote_copy(src[i], it.window(dst[i], chip=chip), send_c.at[i], recv_c.at[i],
                                                    device_id=sibling, device_id_type=MESH))
            for k, flip in enumerate(_CHIP_FLIPS):
                px, py = _flipped((x, y), flip)
                s = 3 * i + k
                out.append(pltpu.make_async_remote_copy(
                    it.window(src[i], half=c), it.window(dst[i], chip=chip, half=c), send_a.at[s], recv_a.at[s],
                    device_id=(px, py, c), device_id_type=MESH))
                if onward:
                    got = it.window(dst[i], chip=2 * px + py, half=c)
                    fwd.append(pltpu.make_async_remote_copy(got, got, send_b.at[s], recv_b.at[s],
                                                            device_id=sibling, device_id_type=MESH))
                    other = it.window(dst[i], chip=2 * px + py, half=1 - c)
                    fwd_in.append(pltpu.make_async_remote_copy(other, other, send_b.at[s], recv_b.at[s],
                                                               device_id=sibling, device_id_type=MESH))
        return own, out, fwd, fwd_in

    def start(src, dst, sems):
        own, out, _, _ = copies(src, dst, sems, False)
        for cp in own + out:
            cp.start()

    def finish(src, dst, sems):
        own, out, fwd, fwd_in = copies(src, dst, sems, True)
        for arrived, onward in zip(out, fwd):
            arrived.wait_recv()
            onward.start()
        for cp in fwd_in:
            cp.wait_recv()
        for cp in out + fwd:
            cp.wait_send()
        for cp in own:
            cp.wait()

    sems = [pltpu.SemaphoreType.DMA((3 * n,)) for _ in range(4)] + [pltpu.SemaphoreType.DMA((n,)) for _ in range(2)]
    return _Comm(shards, [_sds(it.shape, BF16) for it in items], sems, start, finish)


def _pair_swap_copies(windows):
    def copies(src, got, sems):
        send_sems, recv_sems = sems
        x, y, c = _position()
        return [pltpu.make_async_remote_copy(windows[i](src[i], 1 - c), got[i], send_sems.at[i], recv_sems.at[i],
                                             device_id=(x, y, 1 - c), device_id_type=MESH)
                for i in range(len(windows))]

    return copies


def _pair_swap_comm(arrays, windows, out_shapes):
    n = len(arrays)
    copies = _pair_swap_copies(windows)

    def start(src, got, sems):
        for cp in copies(src, got, sems):
            cp.start()

    def finish(src, got, sems):
        for cp in copies(src, got, sems):
            cp.wait()

    return _Comm(arrays, out_shapes, [pltpu.SemaphoreType.DMA((n,)), pltpu.SemaphoreType.DMA((n,))], start, finish)


def _pair_swap(arrays, windows, out_shapes, name):
    return _run_comms([_pair_swap_comm(arrays, windows, out_shapes)], name)[0]


def _chip_exchange_copies(items):
    def copies(src, dst, sems):
        send_sems, recv_sems = sems
        x, y, c = _position()
        out = []
        for i, it in enumerate(items):
            for k, flip in enumerate(_CHIP_FLIPS):
                px, py = _flipped((x, y), flip)
                out.append(pltpu.make_async_remote_copy(
                    it.window(src[i], chip=2 * px + py), dst[i].at[k], send_sems.at[3 * i + k],
                    recv_sems.at[3 * i + k], device_id=(px, py, c), device_id_type=MESH))
        return out

    return copies


_SEM_SPEC = pl.BlockSpec(memory_space=pltpu.SEMAPHORE)
_DATAFLOW = pltpu.SideEffectType.DATAFLOW_SIDE_EFFECTING


def _split_start(copies, srcs, zones, n_copies, name):
    n, nz = len(srcs), len(zones)

    def body(*refs):
        src, land = refs[:n], refs[n:n + nz]
        send_sems, recv_sems = refs[n + nz:n + nz + 2]
        token = refs[-1]
        for cp in copies(src, land, (send_sems, recv_sems)):
            cp.start()
        token[...] = jnp.zeros(token.shape, F32)

    hbm = lambda t: pltpu.HBM(t.shape, t.dtype)
    res = pl.pallas_call(
        body, name=name,
        out_shape=(pltpu.SemaphoreType.DMA((n_copies,)), pltpu.SemaphoreType.DMA((n_copies,)),
                   *[hbm(t) for t in list(srcs) + list(zones)], _sds((8, 128), F32)),
        in_specs=[HBM_SPEC] * (n + nz),
        out_specs=(_SEM_SPEC, _SEM_SPEC, *[HBM_SPEC] * (n + nz), pl.BlockSpec(memory_space=pltpu.VMEM)),
        input_output_aliases={i: 2 + i for i in range(n + nz)},
        compiler_params=pltpu.CompilerParams(has_side_effects=_DATAFLOW),
    )(*[pltpu.with_memory_space_constraint(t, pltpu.HBM) for t in list(srcs) + list(zones)])
    return (res[0], res[1], list(res[2:2 + n]), list(res[2 + n:2 + n + nz])), res[-1]


def _split_wait(copies, handle, after, name):
    send_sems, recv_sems, srcs, zones = handle
    n, nz = len(srcs), len(zones)

    def body(*refs):
        src, land = refs[:n], refs[n:n + nz]
        send, recv = refs[n + nz:n + nz + 2]
        for cp in copies(src, land, (send, recv)):
            cp.wait_send()
            cp.wait_recv()

    hbm = lambda t: pltpu.HBM(t.shape, t.dtype)
    res = pl.pallas_call(
        body, name=name, out_shape=tuple(hbm(t) for t in list(srcs) + list(zones)),
        in_specs=[HBM_SPEC] * (n + nz) + [_SEM_SPEC, _SEM_SPEC, pl.BlockSpec(memory_space=pl.ANY)],
        out_specs=tuple([HBM_SPEC] * (n + nz)), input_output_aliases={i: i for i in range(n + nz)},
        compiler_params=pltpu.CompilerParams(has_side_effects=_DATAFLOW),
    )(*srcs, *zones, send_sems, recv_sems, after)
    return list(res[:n]), list(res[n:])


def _gather_ici_copies(items):
    def copies(src, dst, sems):
        send_sems, recv_sems = sems
        x, y, c = _position()
        chip = 2 * x + y
        out = []
        for i, it in enumerate(items):
            for k, flip in enumerate(_CHIP_FLIPS):
                px, py = _flipped((x, y), flip)
                out.append(pltpu.make_async_remote_copy(
                    it.window(src[i], half=c), it.window(dst[i], chip=chip, half=c), send_sems.at[3 * i + k],
                    recv_sems.at[3 * i + k], device_id=(px, py, c), device_id_type=MESH))
        return out

    return copies


def _gather_pair_finish(shards, mats, items, name):
    n = len(items)

    def body(*refs):
        src, dst = refs[:n], refs[2 * n:3 * n]
        send_own, recv_own, send_fwd, recv_fwd = refs[3 * n:]
        x, y, c = _position()
        chip = 2 * x + y
        sibling = (x, y, 1 - c)
        copies = []
        for i, it in enumerate(items):
            copies.append(pltpu.make_async_remote_copy(src[i], it.window(dst[i], chip=chip), send_own.at[i],
                                                       recv_own.at[i], device_id=sibling, device_id_type=MESH))
            for k, flip in enumerate(_CHIP_FLIPS):
                px, py = _flipped((x, y), flip)
                got = it.window(dst[i], chip=2 * px + py, half=c)
                copies.append(pltpu.make_async_remote_copy(got, got, send_fwd.at[3 * i + k], recv_fwd.at[3 * i + k],
                                                           device_id=sibling, device_id_type=MESH))
        for cp in copies:
            cp.start()
        for cp in copies:
            cp.wait()

    return pl.pallas_call(
        body, in_specs=[HBM_SPEC] * (2 * n), out_specs=[HBM_SPEC] * n, out_shape=[_sds(it.shape, BF16) for it in items],
        input_output_aliases={n + i: i for i in range(n)},
        scratch_shapes=[pltpu.SemaphoreType.DMA((n,)), pltpu.SemaphoreType.DMA((n,)),
                        pltpu.SemaphoreType.DMA((3 * n,)), pltpu.SemaphoreType.DMA((3 * n,))], name=name,
    )(*shards, *mats)


def _chip_exchange_comm(partials, items):
    n = len(items)
    copies = _chip_exchange_copies(items)

    def start(src, dst, sems):
        for cp in copies(src, dst, sems):
            cp.start()

    def finish(src, dst, sems):
        for cp in copies(src, dst, sems):
            cp.wait()

    return _Comm(partials, [_sds((3,) + it.sized(shard=True, half=True), BF16) for it in items],
                 [pltpu.SemaphoreType.DMA((3 * n,)), pltpu.SemaphoreType.DMA((3 * n,))], start, finish)


_SUM_STEPS = 2


def _pair_sums(gs, gots, its, pos, name):
    n = len(its)
    nb = _SUM_STEPS
    g2 = [g.reshape(-1, g.shape[-1]) for g in gs]
    got2 = [t.reshape(-1, t.shape[-1]) for t in gots]

    def body(pos_ref, *refs):
        for g_ref, got_ref, o_ref in zip(refs[:n], refs[n:2 * n], refs[2 * n:]):
            o_ref[...] = (g_ref[...].astype(F32) + got_ref[...].astype(F32)).astype(BF16)

    g_specs, got_specs = [], []
    for it, t in zip(its, got2):
        rows, cols = t.shape
        blk = (rows // nb, cols)
        g_map = (lambda i, pos: (pos[1] * nb + i, 0)) if it.half_axis == 0 else (lambda i, pos: (i, pos[1]))
        g_specs.append(pl.BlockSpec(blk, g_map))
        got_specs.append(pl.BlockSpec(blk, lambda i, pos: (i, 0)))
    outs = pl.pallas_call(
        body, grid_spec=pltpu.PrefetchScalarGridSpec(
            num_scalar_prefetch=1, grid=(nb,), in_specs=g_specs + got_specs, out_specs=got_specs),
        out_shape=[_sds(t.shape, BF16) for t in got2], name=name, compiler_params=_cparams("parallel"),
    )(pos, *g2, *got2)
    return [o.reshape(t.shape) for o, t in zip(outs, gots)]


_FLIP_SLOT = {2: 0, 1: 1, 3: 2}


def _chip_sums(pairs, slots, its, pos, name):
    n = len(its)
    nb = _SUM_STEPS

    def body(pos_ref, *refs):
        chip = pos_ref[0]
        for own in range(4):
            @pl.when(chip == own)
            def _():
                for p_ref, s_ref, o_ref in zip(refs[:n], refs[n:2 * n], refs[2 * n:]):
                    acc = None
                    for k in range(4):
                        v = (p_ref[...] if k == own else s_ref[_FLIP_SLOT[own ^ k]]).astype(F32)
                        acc = v if acc is None else acc + v
                    o_ref[...] = acc

    p_specs, s_specs, o_specs, shapes = [], [], [], []
    for it in its:
        shape = it.sized(shard=True, half=True)
        blk = (shape[0] // nb,) + shape[1:]
        rest = (0,) * (len(shape) - 1)

        def p_map(i, pos, it=it, nd=len(shape)):
            lead = i + (pos[0] * nb if it.shard_axis == 0 else 0)
            return (lead,) + tuple(pos[0] if ax == it.shard_axis else 0 for ax in range(1, nd))

        p_specs.append(pl.BlockSpec(blk, p_map))
        s_specs.append(pl.BlockSpec((3,) + blk, lambda i, pos, rest=rest: (0, i) + rest))
        o_specs.append(pl.BlockSpec(blk, lambda i, pos, rest=rest: (i,) + rest))
        shapes.append(_sds(shape, F32))
    return pl.pallas_call(
        body, grid_spec=pltpu.PrefetchScalarGridSpec(
            num_scalar_prefetch=1, grid=(nb,), in_specs=p_specs + s_specs, out_specs=o_specs),
        out_shape=shapes, name=name, compiler_params=_cparams("parallel"),
    )(pos, *pairs, *slots)


_GRAD_KEYS = ("pool_w_in", "pool_w_grp", "pool_w_out", "na_w_in", "na_w_out", "conv_w_in", "conv_w_out")


def _adamw_matrix(w, m, v, owns, others, it, pos, name):
    nl = w.shape[0]
    rows_split = it.half_axis == 0
    r, cdim = int(np.prod(w.shape[1:-1])), w.shape[-1]
    hr, hc = (r // 2, cdim) if rows_split else (r, cdim // 2)
    br = min(hr, 256)
    nb = hr // br
    c1 = 1.0 - ADAM_B1 ** ADAM_STEP
    c2 = 1.0 - ADAM_B2 ** ADAM_STEP

    def body(pos_ref, w_ref, m_ref, v_ref, *rest):
        own_refs, other_refs = rest[:nl], rest[nl:2 * nl]
        g_ref, d_ref, nm_ref, nv_ref = rest[2 * nl:]
        j, h = pl.program_id(0), pl.program_id(1)
        own, other = own_refs[0][...], other_refs[0][...]
        for q in range(1, nl):
            own = jnp.where(j == q, own_refs[q][...], own)
            other = jnp.where(j == q, other_refs[q][...], other)
        gv = jnp.where(h == pos_ref[1], own, other)
        nm = ADAM_B1 * m_ref[...] + (1.0 - ADAM_B1) * gv
        nv = ADAM_B2 * v_ref[...] + (1.0 - ADAM_B2) * (gv * gv)
        g_ref[...] = gv
        nm_ref[...] = nm
        nv_ref[...] = nv
        d_ref[...] = -ADAM_LR * ((nm / c1) / (jnp.sqrt(nv / c2) + ADAM_EPS) + ADAM_WD * w_ref[...])

    if rows_split:
        full = pl.BlockSpec((None, br, hc), lambda j, h, i, pos: (j, h * nb + i, 0))
    else:
        full = pl.BlockSpec((None, br, hc), lambda j, h, i, pos: (j, i, h))
    half = pl.BlockSpec((br, hc), lambda j, h, i, pos: (i, 0))
    flat = lambda t: t.reshape(nl, r, cdim)
    outs = pl.pallas_call(
        body, grid_spec=pltpu.PrefetchScalarGridSpec(
            num_scalar_prefetch=1, grid=(nl, 2, nb), in_specs=[full] * 3 + [half] * (2 * nl), out_specs=[full] * 4),
        out_shape=[_sds((nl, r, cdim), F32)] * 4, name=name,
        compiler_params=_cparams("parallel", "parallel", "parallel"),
    )(pos, flat(w), flat(m), flat(v), *[t.reshape(hr, hc) for t in list(owns) + list(others)])
    return tuple(t.reshape(w.shape) for t in outs)


_WEIGHTS = ("c_ctx", "norm_g", "ada_w", "ada_b", "pool_w_in", "pool_w_grp", "pool_scale", "pool_w_out", "na_w_in",
            "na_rpb", "na_w_out", "conv_w_in", "conv_dw", "conv_db", "conv_w_out", "final_g")
_COND_ROWS = 16


def _modulations(cond, ada_w, ada_b_cols):
    nl, d, n = ada_w.shape
    return _matmul(
        cond, ada_w, mode="nn", grid=(nl, 1), a_silu=True, epilogue="bias",
        a_spec=pl.BlockSpec((_COND_ROWS, d), lambda i, j: (0, 0)), b_spec=pl.BlockSpec((None, d, n), lambda i, j: (i, 0, 0)),
        extra=(ada_b_cols,), extra_specs=(pl.BlockSpec((None, 1, n), lambda i, j: (i, 0, 0)),),
        out_shapes=[_sds((nl, _COND_ROWS, n), F32)], out_specs=[pl.BlockSpec((None, _COND_ROWS, n), lambda i, j: (i, 0, 0))],
        name="modulations")[0]


def _ada_w_step(cond, dm_cols, w, m, v):
    nl, d, n = w.shape
    tr = d // 2
    c1 = 1.0 - ADAM_B1 ** ADAM_STEP
    c2 = 1.0 - ADAM_B2 ** ADAM_STEP

    def body(c_ref, dm_ref, w_ref, m_ref, v_ref, g_ref, d_ref, nm_ref, nv_ref):
        gv = lax.dot_general(_silu(c_ref[...]).astype(BF16), dm_ref[...].astype(BF16), _DIMS["tn"],
                             preferred_element_type=F32)
        nm = ADAM_B1 * m_ref[...] + (1.0 - ADAM_B1) * gv
        nv = ADAM_B2 * v_ref[...] + (1.0 - ADAM_B2) * (gv * gv)
        g_ref[...] = gv
        nm_ref[...] = nm
        nv_ref[...] = nv
        d_ref[...] = -ADAM_LR * ((nm / c1) / (jnp.sqrt(nv / c2) + ADAM_EPS) + ADAM_WD * w_ref[...])

    blk = pl.BlockSpec((None, tr, n), lambda l, i: (l, i, 0))
    return _call(
        body, (cond, dm_cols, w, m, v), grid=(nl, d // tr),
        in_specs=[pl.BlockSpec((_COND_ROWS, tr), lambda l, i: (0, i)),
                  pl.BlockSpec((None, _COND_ROWS, n), lambda l, i: (l, 0, 0)), blk, blk, blk],
        out_specs=[blk] * 4, out_shape=[_sds(w.shape, F32)] * 4, name="adamw_ada_w")


def _cond_grad(dm_cols, ada_w):
    nl, d, n = ada_w.shape
    return _matmul(
        dm_cols, ada_w, mode="nt", grid=(1, nl), nk=nl, acc_shape=(_COND_ROWS, d),
        a_spec=pl.BlockSpec((None, _COND_ROWS, n), lambda i, q: (q, 0, 0)), b_spec=pl.BlockSpec((None, d, n), lambda i, q: (q, 0, 0)),
        out_shapes=[_sds((_COND_ROWS, d), F32)], out_specs=[pl.BlockSpec((_COND_ROWS, d), lambda i, q: (0, 0))],
        name="cond_grad")[0]


def _pack(parts):
    flat = [p.reshape(-1) for p in parts]
    sizes = [f.shape[0] for f in flat]
    total = sum(sizes)
    rows = -(-total // 1024) * 8
    packed = jnp.concatenate(flat + [jnp.zeros((rows * 128 - total,), F32)]).reshape(rows, 128)
    offs = np.concatenate([[0], np.cumsum(sizes)])[:-1]
    return packed, [(int(o), p.shape) for o, p in zip(offs, parts)]


def _unpack(flat, layout, k):
    off, shape = layout[k]
    return flat[..., off:off + int(np.prod(shape))].reshape(flat.shape[:-1] + tuple(shape))


def kernel(x, c, ctx, c_ctx, norm_g, ada_w, ada_b, pool_w_in, pool_w_grp, pool_scale, pool_w_out, na_w_in, na_rpb, na_w_out, conv_w_in, conv_dw, conv_db, conv_w_out, final_g, loss_target, m_c_ctx, m_norm_g, m_ada_w, m_ada_b, m_pool_w_in, m_pool_w_grp, m_pool_scale, m_pool_w_out, m_na_w_in, m_na_rpb, m_na_w_out, m_conv_w_in, m_conv_dw, m_conv_db, m_conv_w_out, m_final_g, v_c_ctx, v_norm_g, v_ada_w, v_ada_b, v_pool_w_in, v_pool_w_grp, v_pool_scale, v_pool_w_out, v_na_w_in, v_na_rpb, v_na_w_out, v_conv_w_in, v_conv_dw, v_conv_db, v_conv_w_out, v_final_g):
    params = dict(c_ctx=c_ctx, norm_g=norm_g, ada_w=ada_w, ada_b=ada_b, pool_w_in=pool_w_in, pool_w_grp=pool_w_grp,
                  pool_scale=pool_scale, pool_w_out=pool_w_out, na_w_in=na_w_in, na_rpb=na_rpb, na_w_out=na_w_out,
                  conv_w_in=conv_w_in, conv_dw=conv_dw, conv_db=conv_db, conv_w_out=conv_w_out, final_g=final_g)
    mom1 = dict(c_ctx=m_c_ctx, norm_g=m_norm_g, ada_w=m_ada_w, ada_b=m_ada_b, pool_w_in=m_pool_w_in,
                pool_w_grp=m_pool_w_grp, pool_scale=m_pool_scale, pool_w_out=m_pool_w_out, na_w_in=m_na_w_in,
                na_rpb=m_na_rpb, na_w_out=m_na_w_out, conv_w_in=m_conv_w_in, conv_dw=m_conv_dw, conv_db=m_conv_db,
                conv_w_out=m_conv_w_out, final_g=m_final_g)
    mom2 = dict(c_ctx=v_c_ctx, norm_g=v_norm_g, ada_w=v_ada_w, ada_b=v_ada_b, pool_w_in=v_pool_w_in,
                pool_w_grp=v_pool_w_grp, pool_scale=v_pool_scale, pool_w_out=v_pool_w_out, na_w_in=v_na_w_in,
                na_rpb=v_na_rpb, na_w_out=v_na_w_out, conv_w_in=v_conv_w_in, conv_dw=v_conv_dw, conv_db=v_conv_db,
                conv_w_out=v_conv_w_out, final_g=v_final_g)
    d = x.shape[-1]
    w = na_w_out.shape[1] * 4
    xi, yi, ci = _position()
    chip = 2 * xi + yi
    dev = 2 * chip + ci
    n_ada = ada_w.shape[-1]

    def chip_cols(a, size):
        return lax.dynamic_slice_in_dim(a, chip * size, size, axis=a.ndim - 1)

    items = _items(d, w)
    first = [it for it in items if it.key.startswith("pool") and it.layer == 0]
    na = [it for it in items if it.key.startswith("na")]
    late = [it for it in items if it not in first + na]
    shards_of = lambda its: [params[it.key][it.layer].astype(BF16) for it in its]
    empties = lambda its: [lax.empty(it.shape, BF16) for it in its]
    first_copies, na_copies = _gather_ici_copies(first), _gather_ici_copies(na)

    conds = _all_gather(c.reshape(8, d // 8), _AXES, "gather_cond").reshape(8, d)
    behind = conds[0, 0] * 0.0
    first_handle, token = _split_start(first_copies, [s + behind.astype(BF16) for s in shards_of(first)],
                                       empties(first), 3 * len(first), "gather_first_start")
    cond = jnp.concatenate([conds + token[0, 0], c_ctx[None], jnp.zeros((_COND_ROWS - 9, d), F32)], axis=0)
    mod_cols = _modulations(cond, ada_w, chip_cols(ada_b, n_ada)[:, None, :])
    small_pack, small_layout = _pack([pool_scale, conv_dw, conv_db])
    (mod_all,), (small,) = _run_comms([_all_gather_comm(mod_cols, ("x", "y")),
                                       _all_gather_comm(small_pack, ("x", "y"))], "gather_mod")
    behind = mod_all[0, 0, 0, 0] * 0.0
    na_handle, token = _split_start(na_copies, [s + behind.astype(BF16) for s in shards_of(na)], empties(na),
                                    3 * len(na), "gather_na_start")
    first_shards, first_mats = _split_wait(first_copies, first_handle, token, "gather_first_wait")
    first_mats = _gather_pair_finish(first_shards, first_mats, first, "gather_first_pair")
    mod_all = mod_all.transpose(1, 2, 0, 3).reshape(4, _COND_ROWS, 3, d)
    mod = jnp.stack([lax.dynamic_index_in_dim(mod_all, dev, axis=1, keepdims=False), mod_all[:, 8]], axis=1)
    full = {(it.key, it.layer): mat for it, mat in zip(first, first_mats)}
    late_comm = _gather_comm(shards_of(late), late)

    def na_weights(after):
        na_shards, na_mats = _split_wait(na_copies, na_handle, after, "gather_na_wait")
        na_mats = _gather_pair_finish(na_shards, na_mats, na, "gather_na_pair")
        return {it.key: mat for it, mat in zip(na, na_mats)}

    def late_weights(mats):
        full.update({(it.key, it.layer): mat for it, mat in zip(late, mats)})
        return dict(pool_w_in=[full[("pool_w_in", j)] for j in range(2)],
                    pool_w_grp=[full[("pool_w_grp", j)] for j in range(2)],
                    pool_w_out=[full[("pool_w_out", j)] for j in range(2)],
                    conv_w_in=full[("conv_w_in", 0)], conv_w_out=full[("conv_w_out", 0)])

    small = small.reshape(4, -1)

    def whole(k):
        parts = _unpack(small, small_layout, k)
        return jnp.moveaxis(parts, 0, -2).reshape(parts.shape[1:-1] + (-1,))

    wts = dict(pool_w_in=[full[("pool_w_in", 0)]], pool_w_grp=[full[("pool_w_grp", 0)]],
               pool_w_out=[full[("pool_w_out", 0)]], pool_scale=whole(0), na_rpb=na_rpb[0], conv_dw=whole(1)[0],
               conv_db=whole(2))
    pos = jnp.stack([chip, ci]).astype(jnp.int32)

    def layer_grads(its, by_layer):
        pick = {"pool_w_in": "w_in", "pool_w_grp": "w_grp", "pool_w_out": "w_out", "na_w_in": "w_in",
                "na_w_out": "w_out", "conv_w_in": "w_in", "conv_w_out": "w_out"}
        return [by_layer[(it.key.split("_")[0], it.layer)][pick[it.key]] for it in its]

    pairs, handles = dict(), dict()
    half_windows = lambda its: [(lambda ref, half, it=it: it.window(ref, half=half)) for it in its]
    half_shapes = lambda its: [_sds(it.sized(half=True), BF16) for it in its]

    def pair_sums(its, mats, tag):
        got = _pair_swap(mats, half_windows(its), half_shapes(its), f"pair_exchange_{tag}")
        return _pair_sums(mats, got, its, pos, f"pair_sum_{tag}")

    def grad_comm(gr3, gr2):
        pairs["late"] = pair_sums(late, layer_grads(late, {("pool", 1): gr3, ("conv", 0): gr2}), "late")
        return _chip_exchange_comm(pairs["late"], late)

    slot_zones = lambda its: [lax.empty((3,) + it.sized(shard=True, half=True), BF16) for it in its]
    na_xcopies, first_xcopies = _chip_exchange_copies(na), _chip_exchange_copies(first)

    def na_grads_start(gr1):
        pairs["na"] = pair_sums(na, layer_grads(na, {("na", 0): gr1}), "na")
        handles["na"], started = _split_start(na_xcopies, pairs["na"], slot_zones(na), 3 * len(na),
                                              "exchange_na_start")
        return started

    res = _example_step(x[0], ctx[0], loss_target[0], mod, norm_g, final_g[None], wts, dict(
        na_weights=na_weights, late_comm=late_comm, late_weights=late_weights, grad_comm=grad_comm,
        na_grads_start=na_grads_start))
    g0, g1, g2, g3 = res["layers"]
    pairs["na"], na_slots = _split_wait(na_xcopies, handles["na"], g0["w_in"], "exchange_na_wait")
    first_grads = layer_grads(first, {("pool", 0): g0})
    packed, layout = _pack([res["dfinal_g"], res["dnorm_g"], res["dmod"], g1["rpb"],
                            jnp.concatenate([g0["scale"], g3["scale"]], axis=0), g2["dw"], g2["db"],
                            res["loss"][0, :1]])
    first_got, (every,) = _run_comms([_pair_swap_comm(first_grads, half_windows(first), half_shapes(first)),
                                      _all_gather_two_level_comm(packed)], "pair_exchange_first")
    pairs["first"] = _pair_sums(first_grads, first_got, first, pos, "pair_sum_first")

    grads = dict()
    total = _sum_lead(every, "sum_vec_grads").reshape(-1)
    every = every.reshape(8, -1)
    grads["final_g"] = _unpack(total, layout, 0).reshape(final_g.shape)
    grads["norm_g"] = _unpack(total, layout, 1)
    grads["na_rpb"] = _unpack(total, layout, 3)[None]
    grads["pool_scale"] = chip_cols(_unpack(total, layout, 4), pool_scale.shape[-1])
    grads["conv_dw"] = chip_cols(_unpack(total, layout, 5), conv_dw.shape[-1])[None]
    grads["conv_db"] = chip_cols(_unpack(total, layout, 6), conv_db.shape[-1])
    dmod_sum = _unpack(total, layout, 2).reshape(4, 2, 3 * d)
    dmod_each = _unpack(every, layout, 2).reshape(8, 4, 2, 3 * d)
    grads["ada_b"] = dmod_sum[:, 0] + dmod_sum[:, 1]
    dm = jnp.concatenate([dmod_each[:, :, 0].transpose(1, 0, 2), dmod_sum[:, 1][:, None],
                          jnp.zeros((4, _COND_ROWS - 9, 3 * d), F32)], axis=1)
    dm_cols = chip_cols(dm, n_ada)
    dcond = _cond_grad(dm_cols, ada_w)[8].reshape(8, d // 8)
    dcond_all = _all_gather(dcond, ("x", "y"), "gather_cond_grad")
    behind = dcond_all[0, 0, 0] * 0.0
    handles["first"], token = _split_start(first_xcopies, [p + behind.astype(BF16) for p in pairs["first"]],
                                           slot_zones(first), 3 * len(first), "exchange_first_start")
    grads["ada_w"], *ada_w_step = _ada_w_step(cond, dm_cols + token[0, 0], ada_w, m_ada_w, v_ada_w)
    grads["c_ctx"] = _sum_lead(dcond_all, "sum_cond_grad").reshape(d) * _dsilu(c_ctx)
    vector_out = {k: _adamw(params[k], grads[k], mom1[k], mom2[k], f"adamw_{k}")
                  for k in _WEIGHTS if k not in _GRAD_KEYS + ("ada_w",)}
    vector_out["ada_w"] = tuple(ada_w_step)
    pairs["first"], first_slots = _split_wait(first_xcopies, handles["first"], vector_out["ada_w"][2],
                                              "exchange_first_wait")

    slots = dict(zip(late, res["carried"]))
    slots.update(zip(first, first_slots))
    slots.update(zip(na, na_slots))
    pair_of = dict(zip(late, pairs["late"]))
    pair_of.update(zip(first, pairs["first"]))
    pair_of.update(zip(na, pairs["na"]))
    reduced = _chip_sums([pair_of[it] for it in items], [slots[it] for it in items], items, pos, "chip_sum")
    theirs = _pair_swap(reduced, [lambda ref, half: ref] * len(items),
                        [_sds(t.shape, F32) for t in reduced], "pair_return")
    matrix_out = dict()
    for k in _GRAD_KEYS:
        idx = [i for i, it in enumerate(items) if it.key == k]
        res_k = _adamw_matrix(params[k], mom1[k], mom2[k], [reduced[i] for i in idx], [theirs[i] for i in idx],
                              items[idx[0]], pos, f"adamw_{k}")
        grads[k], matrix_out[k] = res_k[0], res_k[1:]

    outs = [[], [], []]
    for k in _WEIGHTS:
        step = matrix_out[k] if k in matrix_out else vector_out[k]
        for lst, val in zip(outs, step):
            lst.append(val)
    loss = _unpack(total, layout, 7)[0]
    return (loss, res["grad_x"][None], *[grads[k].reshape(params[k].shape) for k in _WEIGHTS],
            *outs[0], *outs[1], *outs[2])
```

```python
import functools

import numpy as np
import jax
import jax.numpy as jnp
from jax import lax
from jax.experimental import pallas as pl
from jax.experimental.pallas import tpu as pltpu

F32 = jnp.float32
BF16 = jnp.bfloat16

EPS = 1e-6
GRID_W = 64
HEAD_DIM = 64
WIN_ROWS = 8
WIN_COLS = 16
POOL_WINDOWS = (2, 4, 8, 16)
Q_ROWS = 4
K_ROWS = 12
PAD_ROWS = 4
NEG = -1e30

ADAM_LR = 0.001
ADAM_B1 = 0.9
ADAM_B2 = 0.999
ADAM_EPS = 1e-08
ADAM_WD = 0.01
ADAM_STEP = 10

ROW_BLOCK = 256
VMEM_LIMIT = 56 * 1024 * 1024
MAX_BLOCK_BYTES = 4 * 1024 * 1024
ACT = BF16

MESH = pl.DeviceIdType.MESH
HBM_SPEC = pl.BlockSpec(memory_space=pltpu.HBM)


def _cparams(*sem):
    return pltpu.CompilerParams(dimension_semantics=sem or None, vmem_limit_bytes=VMEM_LIMIT)


def _sds(shape, dtype):
    return jax.ShapeDtypeStruct(tuple(shape), dtype)


def _call(body, args, *, grid, in_specs, out_specs, out_shape, name, scratch_shapes=()):
    return list(pl.pallas_call(
        body, grid=grid, in_specs=list(in_specs), out_specs=list(out_specs), out_shape=list(out_shape),
        scratch_shapes=list(scratch_shapes), name=name, compiler_params=_cparams(*(("arbitrary",) * len(grid))),
    )(*args))


def _sigmoid(x):
    return 1.0 / (1.0 + jnp.exp(-x))


def _silu(x):
    return x * _sigmoid(x)


def _dsilu(x):
    s = _sigmoid(x)
    return s * (1.0 + x * (1.0 - s))


_DIMS = {
    "nn": (((1,), (0,)), ((), ())),
    "nt": (((1,), (1,)), ((), ())),
    "tn": (((0,), (0,)), ((), ())),
}


def _matmul(a, b, *, mode, grid, a_spec, b_spec, out_shapes, out_specs, name, nk=1,
            a_silu=False, exact=False, epilogue=None, extra=(), extra_specs=(), acc_shape=None):
    n_extra = len(extra)
    n_out = len(out_shapes)

    def body(*refs):
        a_ref, b_ref = refs[:2]
        ex = refs[2:2 + n_extra]
        outs = refs[2 + n_extra:2 + n_extra + n_out]
        av = a_ref[...]
        bv = b_ref[...]
        if a_silu:
            av = _silu(av.astype(F32))
        if exact:
            prod = lax.dot_general(av.astype(F32), bv.astype(F32), _DIMS[mode],
                                   precision=lax.Precision.HIGHEST, preferred_element_type=F32)
        else:
            prod = lax.dot_general(av.astype(BF16), bv.astype(BF16), _DIMS[mode], preferred_element_type=F32)

        def finish(res):
            if epilogue == "bias":
                res = res + ex[0][...]
            outs[0][...] = res.astype(outs[0].dtype)

        if nk == 1:
            finish(prod)
        else:
            acc = refs[-1]
            k = pl.program_id(len(grid) - 1)

            @pl.when(k == 0)
            def _():
                acc[...] = prod

            @pl.when(k > 0)
            def _():
                acc[...] += prod

            @pl.when(k == nk - 1)
            def _():
                finish(acc[...])

    scratch = [pltpu.VMEM(acc_shape, F32)] if nk > 1 else []
    sem = ("parallel",) * (len(grid) - 1) + ("arbitrary",)
    return pl.pallas_call(
        body, grid=grid, in_specs=[a_spec, b_spec, *extra_specs], out_specs=list(out_specs),
        out_shape=list(out_shapes), scratch_shapes=scratch, name=name, compiler_params=_cparams(*sem),
    )(a, b, *extra)


def _row_tile(rows):
    for t in (768, 512, 256):
        if rows % t == 0:
            return t
    return rows


def _mm_tn(a, b, name, out_dtype, tm=512):
    r, m = a.shape
    n = b.shape[1]
    tm = min(tm, m)
    tn = min(1024, n)
    return _matmul(
        a, b, mode="tn", grid=(m // tm, n // tn),
        a_spec=pl.BlockSpec((r, tm), lambda i, j: (0, i)), b_spec=pl.BlockSpec((r, tn), lambda i, j: (0, j)),
        out_shapes=[_sds((m, n), out_dtype)], out_specs=[pl.BlockSpec((tm, tn), lambda i, j: (i, j))], name=name)[0]


def _mm_tn_parts(a, b, name, out_dtype, tm=1024):
    r, m = a.shape
    p, _, np_ = b.shape
    tm = min(tm, m)
    return _matmul(
        a, b, mode="tn", grid=(m // tm, p),
        a_spec=pl.BlockSpec((r, tm), lambda i, q: (0, i)), b_spec=pl.BlockSpec((None, r, np_), lambda i, q: (q, 0, 0)),
        out_shapes=[_sds((m, p * np_), out_dtype)], out_specs=[pl.BlockSpec((tm, np_), lambda i, q: (i, q))],
        name=name)[0]


def _row_vec(ref, is_ctx):
    return ref[0] if is_ctx is None else jnp.where(is_ctx, ref[1], ref[0])


def _ctx_rows(i, tm, nx, nseg):
    if nseg == 1:
        return None
    return i * tm + lax.broadcasted_iota(jnp.int32, (tm, 1), 0) >= nx


def _seg_sums(ref, val, is_ctx, first):
    if is_ctx is None:
        parts = [jnp.sum(val, axis=0, keepdims=True)]
    else:
        parts = [jnp.sum(jnp.where(is_ctx, 0.0, val), axis=0, keepdims=True),
                 jnp.sum(jnp.where(is_ctx, val, 0.0), axis=0, keepdims=True)]

    @pl.when(first)
    def _():
        for k, p in enumerate(parts):
            ref[k] = p

    @pl.when(jnp.logical_not(first))
    def _():
        for k, p in enumerate(parts):
            ref[k] += p


def _w_out_resid(a, w_out, xres, gate, nx, name):
    m, k = a.shape
    n = w_out.shape[1]
    nseg = gate.shape[0]
    tm = _row_tile(m)

    def body(a_ref, w_ref, x_ref, gt_ref, yx_ref, xo_ref):
        yx = jnp.dot(a_ref[...], w_ref[...], preferred_element_type=F32)
        yx_ref[...] = yx.astype(ACT)
        xo_ref[...] = x_ref[...] + _row_vec(gt_ref, _ctx_rows(pl.program_id(0), tm, nx, nseg)) * yx

    row = pl.BlockSpec((tm, n), lambda i: (i, 0))
    return pl.pallas_call(
        body, grid=(m // tm,),
        in_specs=[pl.BlockSpec((tm, k), lambda i: (i, 0)), pl.BlockSpec((k, n), lambda i: (0, 0)), row,
                  pl.BlockSpec((nseg, 1, n), lambda i: (0, 0, 0))],
        out_specs=[row, row], out_shape=[_sds((m, n), ACT), _sds((m, n), F32)],
        name=name, compiler_params=_cparams("parallel"),
    )(a, w_out, xres, gate)


def _norm_w_in(x, g, scale, shift, w_in, nx, name, ctx=None):
    d = x.shape[1]
    rows = x.shape[0] + (0 if ctx is None else ctx.shape[0])
    n = w_in.shape[1]
    nseg = scale.shape[0]
    tm = _row_tile(rows)
    tn = n
    while tm * tn * jnp.dtype(ACT).itemsize > MAX_BLOCK_BYTES:
        tn //= 2
    row = pl.BlockSpec((tm, d), lambda i, j: (i, 0))
    if ctx is None:
        row_args, row_specs = (x,), [row]
    else:
        assert ctx.shape[0] == ROW_BLOCK and tm % ROW_BLOCK == 0 and nx % ROW_BLOCK == 0
        nsub, x_blocks = tm // ROW_BLOCK, nx // ROW_BLOCK
        row_args = (x,) * nsub + (ctx,)
        row_specs = [pl.BlockSpec((ROW_BLOCK, d), lambda i, j, s=s: (jnp.minimum(i * nsub + s, x_blocks - 1), 0))
                     for s in range(nsub)] + [pl.BlockSpec((ROW_BLOCK, d), lambda i, j: (0, 0))]

    def body(*refs):
        x_refs, (g_ref, sc_ref, sh_ref, w_ref), outs = refs[:len(row_args)], refs[len(row_args):][:4], refs[-3:]
        h_ref, r_ref, p_ref = outs
        i, j = pl.program_id(0), pl.program_id(1)

        @pl.when(j == 0)
        def _():
            if ctx is None:
                xv = x_refs[0][...]
            else:
                xv = jnp.concatenate([jnp.where(i * nsub + s >= x_blocks, x_refs[-1][...], x_refs[s][...])
                                      for s in range(nsub)], axis=0)
                refs[-4][...] = xv
            r = lax.rsqrt(jnp.mean(xv * xv, axis=-1, keepdims=True) + EPS)
            is_ctx = _ctx_rows(i, tm, nx, nseg)
            h = (xv * r) * g_ref[...] * (1.0 + _row_vec(sc_ref, is_ctx)) + _row_vec(sh_ref, is_ctx)
            h_ref[...] = h.astype(BF16)
            r_ref[...] = r

        p_ref[...] = jnp.dot(h_ref[...], w_ref[...], preferred_element_type=F32).astype(ACT)

    vec = pl.BlockSpec((nseg, 1, d), lambda i, j: (0, 0, 0))
    joined = [] if ctx is None else [(row, _sds((rows, d), F32))]
    out_specs, out_shape = zip(*joined, (row, _sds((rows, d), BF16)),
                               (pl.BlockSpec((tm, 1), lambda i, j: (i, 0)), _sds((rows, 1), F32)),
                               (pl.BlockSpec((tm, tn), lambda i, j: (i, j)), _sds((rows, n), ACT)))
    return _call(
        body, (*row_args, g, scale, shift, w_in), grid=(rows // tm, n // tn),
        in_specs=[*row_specs, pl.BlockSpec((1, d), lambda i, j: (0, 0)), vec, vec,
                  pl.BlockSpec((d, tn), lambda i, j: (0, j))],
        out_specs=list(out_specs), out_shape=list(out_shape), name=name)


def _gate_w_out_bwd(dxo, yx, gate, w_out, nx, name):
    rows, d = yx.shape
    w = w_out.shape[0]
    nseg = gate.shape[0]
    tm = _row_tile(rows)

    def body(dx_ref, yx_ref, gt_ref, w_ref, dyx_ref, da_ref, dg_ref):
        i = pl.program_id(0)
        is_ctx = _ctx_rows(i, tm, nx, nseg)
        dxv = dx_ref[...]
        dyx = (dxv * _row_vec(gt_ref, is_ctx)).astype(BF16)
        dyx_ref[...] = dyx
        da_ref[...] = lax.dot_general(dyx, w_ref[...], _DIMS["nt"], preferred_element_type=F32).astype(ACT)
        _seg_sums(dg_ref, dxv * yx_ref[...].astype(F32), is_ctx, i == 0)

    row = pl.BlockSpec((tm, d), lambda i: (i, 0))
    vec = pl.BlockSpec((nseg, 1, d), lambda i: (0, 0, 0))
    return _call(
        body, (dxo, yx, gate, w_out), grid=(rows // tm,),
        in_specs=[row, row, vec, pl.BlockSpec((w, d), lambda i: (0, 0))],
        out_specs=[row, pl.BlockSpec((tm, w), lambda i: (i, 0)), vec],
        out_shape=[_sds((rows, d), BF16), _sds((rows, w), ACT), _sds((nseg, 1, d), F32)], name=name)


def _w_in_bwd_norm(dparts, w_in, x, r, g, scale, dres, nx, name, dx_rows=None):
    np_, rows, kp = dparts.shape
    d = w_in.shape[0]
    nseg = scale.shape[0]
    tm = _row_tile(rows)
    assert dx_rows is None or rows - tm < dx_rows <= rows
    nsub = tm // ROW_BLOCK
    nres_blocks = dres.shape[0] // ROW_BLOCK
    pp = np_
    while pp % 2 == 0 and pp * tm * kp * dparts.dtype.itemsize > MAX_BLOCK_BYTES:
        pp //= 2
    nk = np_ // pp

    def body(dp_ref, w_ref, x_ref, r_ref, g_ref, sc_ref, *rest):
        dres_refs = rest[:nsub]
        dx_ref, dsh_ref, dge_ref, *acc = rest[nsub:]
        i, k = pl.program_id(0), pl.program_id(1)
        prod = sum(lax.dot_general(dp_ref[q], w_ref[:, q * kp:(q + 1) * kp], _DIMS["nt"], preferred_element_type=F32)
                   for q in range(pp))

        def finish(dhv):
            is_ctx = _ctx_rows(i, tm, nx, nseg)
            rv = r_ref[...]
            xn = x_ref[...] * rv
            dxn = dhv * (g_ref[...] * (1.0 + _row_vec(sc_ref, is_ctx)))
            dx = rv * (dxn - xn * jnp.mean(dxn * xn, axis=-1, keepdims=True))
            for s in range(nsub):
                piece = slice(s * ROW_BLOCK, (s + 1) * ROW_BLOCK)
                res = dres_refs[s][...]
                if nres_blocks * ROW_BLOCK < rows:
                    res = jnp.where(i * nsub + s < nres_blocks, res, 0.0)
                dx_ref[piece, :] = dx[piece, :] + res
            _seg_sums(dsh_ref, dhv, is_ctx, i == 0)
            _seg_sums(dge_ref, dhv * xn, is_ctx, i == 0)

        if nk == 1:
            finish(prod)
        else:
            acc_ref, = acc

            @pl.when(k == 0)
            def _():
                acc_ref[...] = prod

            @pl.when(k > 0)
            def _():
                acc_ref[...] += prod

            @pl.when(k == nk - 1)
            def _():
                finish(acc_ref[...])

    row = pl.BlockSpec((tm, d), lambda i, k: (i, 0))
    vec = pl.BlockSpec((nseg, 1, d), lambda i, k: (0, 0, 0))
    return _call(
        body, (dparts, w_in, x, r, g, scale, *([dres] * nsub)), grid=(rows // tm, nk),
        in_specs=[pl.BlockSpec((pp, tm, kp), lambda i, k: (k, i, 0)), pl.BlockSpec((d, pp * kp), lambda i, k: (0, k)),
                  row, pl.BlockSpec((tm, 1), lambda i, k: (i, 0)), pl.BlockSpec((1, d), lambda i, k: (0, 0)), vec]
        + [pl.BlockSpec((ROW_BLOCK, d), (lambda i, k, s=s: (jnp.minimum(i * nsub + s, nres_blocks - 1), 0)))
           for s in range(nsub)],
        out_specs=[row, vec, vec],
        out_shape=[_sds((dx_rows or rows, d), F32), _sds((nseg, 1, d), F32), _sds((nseg, 1, d), F32)],
        scratch_shapes=[pltpu.VMEM((tm, d), F32)] * (nk > 1), name=name)


_PAD_TOP = 16
_PAD_BOT = 32


def _window_sum(buf, xv, lo, n):
    t = xv.shape[0]
    c = xv.shape[1]
    tp = t + _PAD_TOP + _PAD_BOT
    buf[pl.ds(0, _PAD_TOP), :] = jnp.zeros((_PAD_TOP, c), F32)
    buf[pl.ds(_PAD_TOP, t), :] = xv
    buf[pl.ds(_PAD_TOP + t, _PAD_BOT), :] = jnp.zeros((_PAD_BOT, c), F32)
    p = buf[...]
    k = 1
    while k < n:
        p = p + pltpu.roll(p, tp - k, 0)
        k *= 2
    if lo:
        p = pltpu.roll(p, -lo, 0)
    buf[...] = p
    return buf[pl.ds(_PAD_TOP, t), :]


def _window_count(t, half):
    pos = lax.broadcasted_iota(jnp.int32, (t, 1), 0)
    return (jnp.minimum(pos + half, t) - jnp.maximum(pos - half, 0)).astype(F32)


def _segments(rows, nx):
    return [(0, nx)] + ([(nx, rows - nx)] if rows > nx else [])


def _pool_scratch(rows, nx, cols):
    return [pltpu.VMEM((length + _PAD_TOP + _PAD_BOT, cols), F32) for _, length in _segments(rows, nx)]


def _per_group(g, fn):
    for gi, win in enumerate(POOL_WINDOWS):
        pl.when(g == gi)(functools.partial(fn, win))


def _pool_grp_fwd(uv, w_grp, scale, nx, name):
    rows = uv.shape[0]
    ng, gc, _ = w_grp.shape
    w = ng * gc
    segs = _segments(rows, nx)

    def body(u_ref, gt_ref, w_ref, sc_ref, z_ref, mx_ref, a_ref, *bufs):
        def pool(win):
            half = win // 2
            for (start, length), buf in zip(segs, bufs):
                uvv = u_ref[pl.ds(start, length), :].astype(F32)
                s = _window_sum(buf, uvv, -half, win)
                z_ref[pl.ds(start, length), :] = (s / _window_count(length, half) - uvv).astype(BF16)

        _per_group(pl.program_id(0), pool)
        mixed = jnp.dot(z_ref[...], w_ref[...], preferred_element_type=F32)
        mx_ref[...] = mixed.astype(ACT)
        a_ref[...] = (mixed * sc_ref[...] * _silu(gt_ref[...].astype(F32))).astype(BF16)

    col = pl.BlockSpec((rows, gc), lambda g: (0, g))
    return _call(
        body, (uv, uv, w_grp, scale), grid=(ng,),
        in_specs=[col, pl.BlockSpec((rows, gc), lambda g: (0, ng + g)), pl.BlockSpec((None, gc, gc), lambda g: (g, 0, 0)),
                  pl.BlockSpec((1, gc), lambda g: (0, g))],
        out_specs=[col, col, col], out_shape=[_sds((rows, w), BF16), _sds((rows, w), ACT), _sds((rows, w), BF16)],
        scratch_shapes=_pool_scratch(rows, nx, gc), name=name)


def _pool_grp_bwd(da, mixed, uv, scale, w_grp, nx, name):
    rows, w = da.shape
    ng, gc, _ = w_grp.shape
    segs = _segments(rows, nx)

    def body(da_ref, mx_ref, gt_ref, sc_ref, w_ref, dm_ref, duv_ref, dsc_ref, dz_ref, *bufs):
        dav = da_ref[...].astype(F32)
        mixed = mx_ref[...].astype(F32)
        gt = gt_ref[...].astype(F32)
        sg = _silu(gt)
        sc = sc_ref[...]
        dm = (dav * sc * sg).astype(BF16)
        dm_ref[...] = dm
        dz_ref[...] = lax.dot_general(dm, w_ref[...], _DIMS["nt"], preferred_element_type=F32)
        duv_ref[1] = (dav * mixed * sc * _dsilu(gt)).astype(BF16)
        dsc_ref[...] = jnp.sum(dav * mixed * sg, axis=0, keepdims=True)

        def unpool(win):
            half = win // 2
            for (start, length), buf in zip(segs, bufs):
                dzv = dz_ref[pl.ds(start, length), :]
                s = _window_sum(buf, dzv / _window_count(length, half), 1 - half, win)
                duv_ref[0, pl.ds(start, length), :] = (s - dzv).astype(BF16)

        _per_group(pl.program_id(0), unpool)

    col = pl.BlockSpec((rows, gc), lambda g: (0, g))
    vec = pl.BlockSpec((1, gc), lambda g: (0, g))
    return pl.pallas_call(
        body, grid=(ng,),
        in_specs=[col, col, pl.BlockSpec((rows, gc), lambda g: (0, ng + g)), vec,
                  pl.BlockSpec((None, gc, gc), lambda g: (g, 0, 0))],
        out_specs=[col, pl.BlockSpec((2, rows, gc), lambda g: (0, 0, g)), vec],
        out_shape=[_sds((rows, w), BF16), _sds((2, rows, w), BF16), _sds((1, w), F32)],
        scratch_shapes=[pltpu.VMEM((rows, gc), F32)] + _pool_scratch(rows, nx, gc),
        name=name, compiler_params=_cparams("parallel"),
    )(da, mixed, uv, scale, w_grp)


def _grp_wgrad(z, dm, ng, name, out_dtype):
    rows, w = z.shape
    gc = w // ng

    def body(z_ref, dm_ref, o_ref):
        o_ref[...] = lax.dot_general(z_ref[...], dm_ref[...], _DIMS["tn"],
                                     preferred_element_type=F32).astype(o_ref.dtype)

    blk = pl.BlockSpec((rows, gc), lambda g: (0, g))
    return pl.pallas_call(
        body, grid=(ng,), in_specs=[blk, blk], out_specs=pl.BlockSpec((None, gc, gc), lambda g: (g, 0, 0)),
        out_shape=_sds((ng, gc, gc), out_dtype), name=name, compiler_params=_cparams("parallel"),
    )(z, dm)


def _shift_rows(v, by):
    t = v.shape[0]
    pos = lax.broadcasted_iota(jnp.int32, v.shape, 0)
    rolled = pltpu.roll(v, by % t, 0)
    keep = pos >= by if by > 0 else pos < t + by
    return jnp.where(keep, rolled, 0.0)


def _conv_specs(t, w, cb):
    return [pl.BlockSpec((t, cb), (lambda j, q=q: (0, q * (w // cb) + j))) for q in range(4)]


def _conv_fwd(p4, dw, db, name):
    t = p4.shape[0]
    w = p4.shape[1] // 4
    cb = 128

    def body(bg_ref, cg_ref, v_ref, g_ref, dw_ref, db_ref, a_ref):
        tv = cg_ref[...].astype(F32) * v_ref[...].astype(F32)
        conv = (dw_ref[0:1, :] * _shift_rows(tv, 1) + dw_ref[1:2, :] * tv + dw_ref[2:3, :] * _shift_rows(tv, -1)
                + db_ref[...])
        a_ref[...] = (bg_ref[...].astype(F32) * conv * _silu(g_ref[...].astype(F32))).astype(BF16)

    return pl.pallas_call(
        body, grid=(w // cb,),
        in_specs=_conv_specs(t, w, cb) + [pl.BlockSpec((3, cb), lambda j: (0, j)), pl.BlockSpec((1, cb), lambda j: (0, j))],
        out_specs=pl.BlockSpec((t, cb), lambda j: (0, j)), out_shape=_sds((t, w), BF16),
        name=name, compiler_params=_cparams("parallel"),
    )(p4, p4, p4, p4, dw, db)


def _conv_bwd(da, p4, dw, db, name):
    t, w = da.shape
    cb = 128

    def body(da_ref, bg_ref, cg_ref, v_ref, g_ref, dw_ref, db_ref, d4_ref, ddw_ref, ddb_ref):
        cg = cg_ref[...].astype(F32)
        vv = v_ref[...].astype(F32)
        bg = bg_ref[...].astype(F32)
        gv = g_ref[...].astype(F32)
        tv = cg * vv
        tm1 = _shift_rows(tv, 1)
        tp1 = _shift_rows(tv, -1)
        w0, w1, w2 = dw_ref[0:1, :], dw_ref[1:2, :], dw_ref[2:3, :]
        conv = w0 * tm1 + w1 * tv + w2 * tp1 + db_ref[...]
        y = bg * conv
        dav = da_ref[...].astype(F32)
        dy = dav * _silu(gv)
        d4_ref[3] = (dav * y * _dsilu(gv)).astype(BF16)
        d4_ref[0] = (dy * conv).astype(BF16)
        dconv = dy * bg
        ddb_ref[...] = jnp.sum(dconv, axis=0, keepdims=True)
        ddw_ref[0:1, :] = jnp.sum(dconv * tm1, axis=0, keepdims=True)
        ddw_ref[1:2, :] = jnp.sum(dconv * tv, axis=0, keepdims=True)
        ddw_ref[2:3, :] = jnp.sum(dconv * tp1, axis=0, keepdims=True)
        dt = w0 * _shift_rows(dconv, -1) + w1 * dconv + w2 * _shift_rows(dconv, 1)
        d4_ref[1] = (dt * vv).astype(BF16)
        d4_ref[2] = (dt * cg).astype(BF16)

    col = pl.BlockSpec((t, cb), lambda j: (0, j))
    tap = pl.BlockSpec((3, cb), lambda j: (0, j))
    bias = pl.BlockSpec((1, cb), lambda j: (0, j))
    return pl.pallas_call(
        body, grid=(w // cb,), in_specs=[col] + _conv_specs(t, w, cb) + [tap, bias],
        out_specs=[pl.BlockSpec((4, t, cb), lambda j: (0, 0, j)), tap, bias],
        out_shape=[_sds((4, t, w), BF16), _sds((3, w), F32), _sds((1, w), F32)],
        name=name, compiler_params=_cparams("parallel"),
    )(da, p4, p4, p4, p4, dw, db)


def _attn_mask():
    qn, kn = Q_ROWS * GRID_W, K_ROWS * GRID_W
    qr, qc = np.divmod(np.arange(qn), GRID_W)
    kr, kc = np.divmod(np.arange(kn), GRID_W)
    col0 = np.clip(qc - WIN_COLS // 2, 0, GRID_W - WIN_COLS)
    col_ok = (kc[None, :] >= col0[:, None]) & (kc[None, :] < col0[:, None] + WIN_COLS)
    first = np.zeros(qn, np.int64)
    last = np.full(qn, K_ROWS - WIN_ROWS)
    out = []
    for row0 in (first, qr, last):
        row_ok = (kr[None, :] >= row0[:, None]) & (kr[None, :] < row0[:, None] + WIN_ROWS)
        out.append(np.where(row_ok & col_ok, 0.0, NEG))
    return jnp.asarray(np.stack(out), F32)


_KW = K_ROWS * GRID_W
_QB = Q_ROWS * GRID_W
_PAIR = 2 * HEAD_DIM
_N_DR = 2 * WIN_ROWS - 1
_N_DC = 2 * WIN_COLS - 1
_RP_ROWS = 24
_N_TILES = _N_DR + 1
_BIAS_BASE = (WIN_ROWS - 1, WIN_ROWS // 2 - 1, -1)


class _Comm:
    def __init__(self, ins, outs, sems, start, finish):
        self.ins, self.outs, self.sems, self.start, self.finish = list(ins), list(outs), list(sems), start, finish


def _bias_pieces(cls):
    out = []
    for qr in range(Q_ROWS):
        for kr in range(0, K_ROWS, 2):
            tile = _BIAS_BASE[cls] - qr + kr + 1
            out.append((qr, kr, tile if 0 <= tile < _N_TILES else None))
    return out


def _toeplitz_pair(left_row, right_row):
    lane = lax.broadcasted_iota(jnp.int32, (GRID_W, _PAIR), 1)
    shape = (GRID_W, _PAIR)
    left = pltpu.roll(jnp.broadcast_to(left_row, shape), _PAIR - (WIN_COLS - 1), 1, stride=1, stride_axis=0)
    right = pltpu.roll(jnp.broadcast_to(right_row, shape), GRID_W - (WIN_COLS - 1), 1, stride=1, stride_axis=0)
    return jnp.where(lane < GRID_W, left, right)


def _build_tiles(tiles_ref, rp_ref):
    for h in range(2):
        for t in range(_N_TILES):
            tiles_ref[h, t] = _toeplitz_pair(rp_ref[h, t:t + 1, :], rp_ref[h, t + 1:t + 2, :])


def _block_class(b, nblk, fn, entering=False):
    interior = (b == 1) if entering else jnp.logical_and(b > 0, b < nblk - 1)
    for cls, cond in enumerate((b == 0, interior, b == nblk - 1)):
        pl.when(cond)(functools.partial(fn, cls))


def _attn_geometry(p4, nx):
    rows = p4.shape[0]
    w = p4.shape[1] // 4
    nhp = w // _PAIR
    nblk = nx // _QB
    qspec = lambda col: pl.BlockSpec((_QB, _PAIR), lambda hp, b: (b, col * nhp + hp))
    kspec = lambda col: pl.BlockSpec((rows, _PAIR), lambda hp, b: (0, col * nhp + hp))
    tspec = pl.BlockSpec((2, _RP_ROWS, _PAIR), lambda hp, b: (hp, 0, 0))
    mspec = pl.BlockSpec((None, _QB, _KW), lambda hp, b: (jnp.where(b == 0, 0, jnp.where(b == nblk - 1, 2, 1)), 0, 0))
    lspec = pl.BlockSpec((None, _QB, 2), lambda hp, b: (hp, b, 0))
    ospec = pl.BlockSpec((_QB, _PAIR), lambda hp, b: (b, hp))
    return rows, w, nhp, nblk, qspec, kspec, tspec, mspec, lspec, ospec


def _window_start(b, nx):
    return pl.multiple_of(jnp.clip(b * _QB - PAD_ROWS * GRID_W, 0, nx - _KW), _QB)


def _load_bias(bias_ref, tiles_ref, rp_ref, m_ref, b, nblk):
    pl.when(b == 0)(lambda: _build_tiles(tiles_ref, rp_ref))

    def fill(cls):
        for h in range(2):
            for qr, kr, tile in _bias_pieces(cls):
                rows = slice(qr * GRID_W, (qr + 1) * GRID_W)
                cols = slice(kr * GRID_W, (kr + 2) * GRID_W)
                m = m_ref[rows, cols]
                bias_ref[h, rows, cols] = m if tile is None else tiles_ref[h, tile] + m

    _block_class(b, nblk, fill, entering=True)


def _attn_fwd(p4, rp, mask, nx, name, comm=None):
    rows, w, nhp, nblk, qspec, kspec, tspec, mspec, lspec, ospec = _attn_geometry(p4, nx)
    n_ctx = rows - nx
    n_cin, n_cout = (len(comm.ins), len(comm.outs)) if comm else (0, 0)

    def body(*refs):
        q_ref, k_ref, v_ref, g_ref, rp_ref, m_ref = refs[:6]
        cin = refs[6:6 + n_cin]
        a_ref, o_ref, lse_ref = refs[6 + n_cin:9 + n_cin]
        cout = refs[9 + n_cin:9 + n_cin + n_cout]
        bias_ref, tiles_ref = refs[9 + n_cin + n_cout:11 + n_cin + n_cout]
        sems = refs[11 + n_cin + n_cout:]
        hp, b = pl.program_id(0), pl.program_id(1)
        if comm:
            pl.when(jnp.logical_and(hp == 0, b == 0))(lambda: comm.start(cin, cout, sems))
        start = _window_start(b, nx)
        _load_bias(bias_ref, tiles_ref, rp_ref, m_ref, b, nblk)
        qf = q_ref[...].astype(F32) * HEAD_DIM ** -0.5
        kw = k_ref[pl.ds(start, _KW), :].astype(BF16)
        vw = v_ref[pl.ds(start, _KW), :].astype(BF16)
        kcv = k_ref[pl.ds(nx, n_ctx), :].astype(BF16)
        vcv = v_ref[pl.ds(nx, n_ctx), :].astype(BF16)
        lane = lax.broadcasted_iota(jnp.int32, (1, _PAIR), 1)
        outs, lses = [], []
        for h in range(2):
            mine = (lane >= HEAD_DIM) if h else (lane < HEAD_DIM)
            qm = jnp.where(mine, qf, 0.0).astype(BF16)
            s_loc = lax.dot_general(qm, kw, _DIMS["nt"], preferred_element_type=F32) + bias_ref[h]
            s_ctx = lax.dot_general(qm, kcv, _DIMS["nt"], preferred_element_type=F32)
            mx = jnp.maximum(jnp.max(s_loc, axis=-1, keepdims=True), jnp.max(s_ctx, axis=-1, keepdims=True))
            p_loc = jnp.exp(s_loc - mx)
            p_ctx = jnp.exp(s_ctx - mx)
            den = jnp.sum(p_loc, axis=-1, keepdims=True) + jnp.sum(p_ctx, axis=-1, keepdims=True)
            o = jnp.dot(p_loc.astype(BF16), vw, preferred_element_type=F32)
            o = o + jnp.dot(p_ctx.astype(BF16), vcv, preferred_element_type=F32)
            outs.append(o * (1.0 / den))
            lses.append(mx + jnp.log(den))
        o = jnp.where(lane < HEAD_DIM, outs[0], outs[1])
        o_ref[...] = o.astype(ACT)
        a_ref[...] = (o * _silu(g_ref[...].astype(F32))).astype(BF16)
        col = lax.broadcasted_iota(jnp.int32, (1, 2), 1)
        lse_ref[...] = jnp.where(col == 0, lses[0], lses[1])
        if comm:
            pl.when(jnp.logical_and(hp == nhp - 1, b == nblk - 1))(lambda: comm.finish(cin, cout, sems))

    res = pl.pallas_call(
        body, grid=(nhp, nblk),
        in_specs=[qspec(0), kspec(1), kspec(2), qspec(3), tspec, mspec] + [HBM_SPEC] * n_cin,
        out_specs=[ospec, ospec, lspec] + [HBM_SPEC] * n_cout,
        out_shape=[_sds((nx, w), BF16), _sds((nx, w), ACT), _sds((nhp, nx, 2), F32)] + (comm.outs if comm else []),
        scratch_shapes=[pltpu.VMEM((2, _QB, _KW), F32), pltpu.VMEM((2, _N_TILES, GRID_W, _PAIR), F32)]
        + (comm.sems if comm else []),
        name=name, compiler_params=_cparams("arbitrary", "arbitrary"),
    )(p4, p4, p4, p4, rp, mask, *(comm.ins if comm else []))
    return res[:3], res[3:]


def _fold_tiles(dtiles_ref, drp_ref):
    shape = (GRID_W, _PAIR)
    lane = lax.broadcasted_iota(jnp.int32, shape, 1)
    flip = (lax.broadcasted_iota(jnp.int32, (_PAIR, _PAIR), 0)
            + lax.broadcasted_iota(jnp.int32, (_PAIR, _PAIR), 1) == _PAIR - 1).astype(F32)
    drp_ref[...] = jnp.zeros(drp_ref.shape, F32)
    for h in range(2):
        stack = dtiles_ref[h].reshape(_N_TILES * GRID_W, _PAIR)
        rev = jnp.dot(stack, flip, precision=lax.Precision.HIGHEST, preferred_element_type=F32)
        for t in range(_N_TILES):
            tile = rev[t * GRID_W:(t + 1) * GRID_W, :]
            for side in (0, 1):
                shift = _PAIR - GRID_W * side - (WIN_COLS - 1)
                half = jnp.where((lane < GRID_W) if side else (lane >= GRID_W), tile, 0.0)
                diag = pltpu.roll(half, shift, 1, stride=1, stride_axis=0)
                drp_ref[h, t + side:t + side + 1, :] += jnp.sum(diag, axis=0, keepdims=True)


def _attn_bwd(p4, rp, mask, o, lse, da, nx, name, comm=None):
    rows, w, nhp, nblk, qspec, kspec, tspec, mspec, lspec, ospec = _attn_geometry(p4, nx)
    n_ctx = rows - nx
    n_cin, n_cout = (len(comm.ins), len(comm.outs)) if comm else (0, 0)

    def body(*refs):
        q_ref, k_ref, v_ref, g_ref, rp_ref, m_ref, o_ref, lse_ref, da_ref = refs[:9]
        cin = refs[9:9 + n_cin]
        d4_ref, drp_ref = refs[9 + n_cin:11 + n_cin]
        cout = refs[11 + n_cin:11 + n_cin + n_cout]
        bias_ref, tiles_ref, ds_ref, dtiles_ref, dk_ref, dv_ref = refs[11 + n_cin + n_cout:17 + n_cin + n_cout]
        sems = refs[17 + n_cin + n_cout:]
        hp, b = pl.program_id(0), pl.program_id(1)
        if comm:
            pl.when(jnp.logical_and(hp == 0, b == 0))(lambda: comm.start(cin, cout, sems))
        start = _window_start(b, nx)
        here = pl.multiple_of(b * _QB, _QB)

        @pl.when(b == 0)
        def _():
            dk_ref[...] = jnp.zeros(dk_ref.shape, F32)
            dv_ref[...] = jnp.zeros(dv_ref.shape, F32)
            dtiles_ref[...] = jnp.zeros(dtiles_ref.shape, F32)
            d4_ref[0, pl.ds(nx, n_ctx), :] = jnp.zeros((n_ctx, _PAIR), BF16)
            d4_ref[3, pl.ds(nx, n_ctx), :] = jnp.zeros((n_ctx, _PAIR), BF16)

        _load_bias(bias_ref, tiles_ref, rp_ref, m_ref, b, nblk)
        gv = g_ref[...].astype(F32)
        dav = da_ref[...].astype(F32)
        ov = o_ref[...].astype(F32)
        dov = dav * _silu(gv)
        d4_ref[3, pl.ds(here, _QB), :] = (dav * ov * _dsilu(gv)).astype(BF16)
        qf = q_ref[...].astype(F32) * HEAD_DIM ** -0.5
        kw = k_ref[pl.ds(start, _KW), :].astype(BF16)
        vw = v_ref[pl.ds(start, _KW), :].astype(BF16)
        kcv = k_ref[pl.ds(nx, n_ctx), :].astype(BF16)
        vcv = v_ref[pl.ds(nx, n_ctx), :].astype(BF16)
        lane = lax.broadcasted_iota(jnp.int32, (1, _PAIR), 1)
        dq = jnp.zeros((_QB, _PAIR), F32)
        for h in range(2):
            mine = (lane >= HEAD_DIM) if h else (lane < HEAD_DIM)
            qm = jnp.where(mine, qf, 0.0).astype(BF16)
            dom = jnp.where(mine, dov, 0.0)
            dob = dom.astype(BF16)
            lse = lse_ref[:, h:h + 1]
            s_loc = lax.dot_general(qm, kw, _DIMS["nt"], preferred_element_type=F32)
            p_loc = jnp.exp(s_loc + bias_ref[h] - lse)
            p_ctx = jnp.exp(lax.dot_general(qm, kcv, _DIMS["nt"], preferred_element_type=F32) - lse)
            delta = jnp.sum(dom * ov, axis=-1, keepdims=True)
            ds_loc = p_loc * (lax.dot_general(dob, vw, _DIMS["nt"], preferred_element_type=F32) - delta)
            ds_ctx = p_ctx * (lax.dot_general(dob, vcv, _DIMS["nt"], preferred_element_type=F32) - delta)
            dsb_loc = ds_loc.astype(BF16)
            dsb_ctx = ds_ctx.astype(BF16)
            dq_h = (jnp.dot(dsb_loc, kw, preferred_element_type=F32)
                    + jnp.dot(dsb_ctx, kcv, preferred_element_type=F32))
            dq = dq + jnp.where(mine, dq_h, 0.0)
            dk_ref[pl.ds(start, _KW), :] += lax.dot_general(dsb_loc, qm, _DIMS["tn"], preferred_element_type=F32)
            dv_ref[pl.ds(start, _KW), :] += lax.dot_general(p_loc.astype(BF16), dob, _DIMS["tn"],
                                                            preferred_element_type=F32)
            dk_ref[pl.ds(nx, n_ctx), :] += lax.dot_general(dsb_ctx, qm, _DIMS["tn"], preferred_element_type=F32)
            dv_ref[pl.ds(nx, n_ctx), :] += lax.dot_general(p_ctx.astype(BF16), dob, _DIMS["tn"],
                                                           preferred_element_type=F32)
            ds_ref[h] = ds_loc
        d4_ref[0, pl.ds(here, _QB), :] = (dq * HEAD_DIM ** -0.5).astype(BF16)

        def scatter(cls):
            for h in range(2):
                for qr, kr, tile in _bias_pieces(cls):
                    if tile is not None:
                        dtiles_ref[h, tile] += ds_ref[h, qr * GRID_W:(qr + 1) * GRID_W, kr * GRID_W:(kr + 2) * GRID_W]

        _block_class(b, nblk, scatter)

        @pl.when(b == nblk - 1)
        def _():
            d4_ref[1] = dk_ref[...].astype(BF16)
            d4_ref[2] = dv_ref[...].astype(BF16)
            _fold_tiles(dtiles_ref, drp_ref)

        if comm:
            pl.when(jnp.logical_and(hp == nhp - 1, b == nblk - 1))(lambda: comm.finish(cin, cout, sems))

    tiles = pltpu.VMEM((2, _N_TILES, GRID_W, _PAIR), F32)
    block = pltpu.VMEM((2, _QB, _KW), F32)
    res = pl.pallas_call(
        body, grid=(nhp, nblk),
        in_specs=[qspec(0), kspec(1), kspec(2), qspec(3), tspec, mspec, ospec, lspec, ospec] + [HBM_SPEC] * n_cin,
        out_specs=[pl.BlockSpec((4, rows, _PAIR), lambda hp, b: (0, 0, hp)), tspec] + [HBM_SPEC] * n_cout,
        out_shape=[_sds((4, rows, w), BF16), _sds(rp.shape, F32)] + (comm.outs if comm else []),
        scratch_shapes=[block, tiles, block, tiles, pltpu.VMEM((rows, _PAIR), F32), pltpu.VMEM((rows, _PAIR), F32)]
        + (comm.sems if comm else []),
        name=name, compiler_params=_cparams("arbitrary", "arbitrary"),
    )(p4, p4, p4, p4, rp, mask, o, lse, da, *(comm.ins if comm else []))
    return res[:2], res[2:]


def _w_out_loss(a, w_out, xres, gate, g, target, name):
    m, k = a.shape
    d = w_out.shape[1]
    assert gate.shape[0] == 1
    tm = _row_tile(m)
    nblk = m // tm

    def body(a_ref, w_ref, x_ref, gt_ref, g_ref, t_ref, yx_ref, loss_ref, dx_ref, dg_ref, acc_ref):
        i = pl.program_id(0)
        yx = jnp.dot(a_ref[...], w_ref[...], preferred_element_type=F32)
        yx_ref[...] = yx.astype(ACT)
        xv = x_ref[...] + gt_ref[0] * yx
        gv = g_ref[...]
        r = lax.rsqrt(jnp.mean(xv * xv, axis=-1, keepdims=True) + EPS)
        xn = xv * r
        err = xn * gv - t_ref[...]
        dy = err * (1.0 / d)
        dxn = dy * gv
        dx_ref[...] = r * (dxn - xn * jnp.mean(dxn * xn, axis=-1, keepdims=True))
        s_g = jnp.sum(dy * xn, axis=0, keepdims=True)
        s_l = jnp.sum(jnp.mean(err * err, axis=-1, keepdims=True), axis=0, keepdims=True)

        @pl.when(i == 0)
        def _():
            dg_ref[...] = s_g
            acc_ref[...] = s_l

        @pl.when(i > 0)
        def _():
            dg_ref[...] += s_g
            acc_ref[...] += s_l

        @pl.when(i == nblk - 1)
        def _():
            loss_ref[...] = jnp.broadcast_to(0.5 * acc_ref[...], loss_ref.shape)

    row = pl.BlockSpec((tm, d), lambda i: (i, 0))
    vec = pl.BlockSpec((1, d), lambda i: (0, 0))
    return pl.pallas_call(
        body, grid=(nblk,),
        in_specs=[pl.BlockSpec((tm, k), lambda i: (i, 0)), pl.BlockSpec((k, d), lambda i: (0, 0)), row,
                  pl.BlockSpec((1, 1, d), lambda i: (0, 0, 0)), vec, row],
        out_specs=[row, pl.BlockSpec((1, 128), lambda i: (0, 0)), row, vec],
        out_shape=[_sds((m, d), ACT), _sds((1, 128), F32), _sds((m, d), F32), _sds((1, d), F32)],
        scratch_shapes=[pltpu.VMEM((1, 1), F32)], name=name, compiler_params=_cparams("arbitrary"),
    )(a, w_out, xres, gate, g, target)


def _as2d(a):
    if a.ndim == 1:
        return a.reshape(-1, 128) if a.shape[0] % 128 == 0 else a.reshape(1, -1)
    return a.reshape(-1, a.shape[-1])


def _adamw(w, g, m, v, name):
    shape = w.shape
    w2, g2, m2, v2 = (_as2d(t) for t in (w, g.reshape(shape), m, v))
    rows, cols = w2.shape
    tr = 512 if rows % 512 == 0 else rows
    c1 = 1.0 - ADAM_B1 ** ADAM_STEP
    c2 = 1.0 - ADAM_B2 ** ADAM_STEP

    def body(w_ref, g_ref, m_ref, v_ref, d_ref, nm_ref, nv_ref):
        gv = g_ref[...]
        nm = ADAM_B1 * m_ref[...] + (1.0 - ADAM_B1) * gv
        nv = ADAM_B2 * v_ref[...] + (1.0 - ADAM_B2) * (gv * gv)
        nm_ref[...] = nm
        nv_ref[...] = nv
        d_ref[...] = -ADAM_LR * ((nm / c1) / (jnp.sqrt(nv / c2) + ADAM_EPS) + ADAM_WD * w_ref[...])

    blk = pl.BlockSpec((tr, cols), lambda i: (i, 0))
    outs = _call(body, (w2, g2, m2, v2), grid=(rows // tr,), in_specs=[blk] * 4, out_specs=[blk] * 3,
                 out_shape=[_sds((rows, cols), F32)] * 3, name=name)
    return tuple(t.reshape(shape) for t in outs)


def _sum_lead(x, name, out_dtype=F32):
    n, rows, cols = x.shape
    tr = 512 if rows % 512 == 0 else rows

    def body(x_ref, o_ref):
        acc = x_ref[0].astype(F32)
        for k in range(1, n):
            acc = acc + x_ref[k].astype(F32)
        o_ref[...] = acc.astype(out_dtype)

    return pl.pallas_call(
        body, grid=(rows // tr,), in_specs=[pl.BlockSpec((n, tr, cols), lambda i: (0, i, 0))],
        out_specs=pl.BlockSpec((tr, cols), lambda i: (i, 0)), out_shape=_sds((rows, cols), out_dtype),
        name=name, compiler_params=_cparams("parallel"),
    )(x)


def _seg_vecs(mod_l, which, nseg):
    return mod_l[:nseg, which][:, None, :]


def _norm_grads(dshift, dgeff, dgate, g, scale):
    nseg, _, d = dshift.shape
    dmod = jnp.stack([dshift[:, 0], dgeff[:, 0] * g, dgate[:, 0]], axis=1)
    if nseg == 1:
        dmod = jnp.concatenate([dmod, jnp.zeros((1, 3, d), F32)], axis=0)
    dg = jnp.sum(dgeff[:, 0] * (1.0 + scale[:, 0]), axis=0)
    return dmod, dg


def _pool_layer(xin, g, mod_l, w_in, w_grp, w_out, pscale, nx, tag, ctx=None, head=None):
    nseg = 1 if ctx is None else 2
    shift, scale, gate = (_seg_vecs(mod_l, k, nseg) for k in range(3))
    *joined, h, r, uv = _norm_w_in(xin, g, scale, shift, w_in, nx, f"w_in_fwd_{tag}", ctx)
    if joined:
        xin, = joined
    z, mixed, a = _pool_grp_fwd(uv, w_grp, pscale, nx, f"pool_fwd_{tag}")
    if head is None:
        yx, xout = _w_out_resid(a, w_out, xin, gate, nx, f"w_out_fwd_{tag}")
    else:
        yx, *xout = _w_out_loss(a, w_out, xin, gate, *head, f"w_out_loss_{tag}")

    def backward(dxo, token=None):
        gate_b = gate if token is None else gate + token[0, 0]
        dyx, da, dgate = _gate_w_out_bwd(dxo, yx, gate_b, w_out, nx, f"w_out_bwd_{tag}")
        gw_out = _mm_tn(a, dyx, f"w_out_grad_{tag}", BF16)
        dm, duv, dscale = _pool_grp_bwd(da, mixed, uv, pscale, w_grp, nx, f"pool_bwd_{tag}")
        gw_grp = _grp_wgrad(z, dm, w_grp.shape[0], f"grp_grad_{tag}", BF16)
        gw_in = _mm_tn_parts(h, duv, f"w_in_grad_{tag}", BF16)
        dx, dshift, dgeff = _w_in_bwd_norm(duv, w_in, xin, r, g, scale, dxo, nx, f"w_in_bwd_{tag}",
                                           dx_rows=None if ctx is None else nx)
        dmod, dg = _norm_grads(dshift, dgeff, dgate, g[0], scale)
        return dx, dmod, dg, dict(w_in=gw_in, w_grp=gw_grp, w_out=gw_out, scale=dscale)

    return xout, backward


def _na_layer(xc, g, mod_l, w_in, rpb, w_out, nx, mask, comm=None):
    nh, n_dr, n_dc = rpb.shape
    shift, scale = _seg_vecs(mod_l, 0, 2), _seg_vecs(mod_l, 1, 2)
    gate = _seg_vecs(mod_l, 2, 1)
    h, r, p4 = _norm_w_in(xc, g, scale, shift, w_in, nx, "w_in_fwd_na")
    rp = jnp.pad(rpb, ((0, 0), (1, _RP_ROWS - 1 - n_dr), (0, _PAIR - n_dc)))
    (a, o, lse), carried = _attn_fwd(p4, rp, mask, nx, "attn_fwd", comm)
    yx, xout = _w_out_resid(a, w_out, xc, gate, nx, "w_out_fwd_na")

    def backward(dxo, comm=None):
        dyx, da, dgate = _gate_w_out_bwd(dxo, yx, gate, w_out, nx, "w_out_bwd_na")
        gw_out = _mm_tn(a, dyx, "w_out_grad_na", BF16)
        (d4, drp), carried_bwd = _attn_bwd(p4, rp, mask, o, lse, da, nx, "attn_bwd", comm)
        gw_in = _mm_tn_parts(h, d4, "w_in_grad_na", BF16)
        dx, dshift, dgeff = _w_in_bwd_norm(d4, w_in, xc, r, g, scale, dxo, nx, "w_in_bwd_na")
        dgate2 = jnp.concatenate([dgate, jnp.zeros_like(dgate)], axis=0)
        dmod, dg = _norm_grads(dshift, dgeff, dgate2, g[0], scale)
        drpb = drp[:, 1:1 + n_dr, ::-1][:, :, :n_dc]
        return dx, dmod, dg, dict(w_in=gw_in, w_out=gw_out, rpb=drpb), carried_bwd

    return xout, backward, carried


def _conv_layer(xin, g, mod_l, w_in, dw, db, w_out):
    shift, scale, gate = (_seg_vecs(mod_l, k, 1) for k in range(3))
    nx = xin.shape[0]
    h, r, p4 = _norm_w_in(xin, g, scale, shift, w_in, nx, "w_in_fwd_conv")
    a = _conv_fwd(p4, dw, db, "conv_fwd")
    yx, xout = _w_out_resid(a, w_out, xin, gate, nx, "w_out_fwd_conv")

    def backward(dxo):
        dyx, da, dgate = _gate_w_out_bwd(dxo, yx, gate, w_out, nx, "w_out_bwd_conv")
        gw_out = _mm_tn(a, dyx, "w_out_grad_conv", BF16)
        d4, ddw, ddb = _conv_bwd(da, p4, dw, db, "conv_bwd")
        gw_in = _mm_tn_parts(h, d4, "w_in_grad_conv", BF16)
        dx, dshift, dgeff = _w_in_bwd_norm(d4, w_in, xin, r, g, scale, dxo, nx, "w_in_bwd_conv")
        dmod, dg = _norm_grads(dshift, dgeff, dgate, g[0], scale)
        return dx, dmod, dg, dict(w_in=gw_in, w_out=gw_out, dw=ddw, db=ddb)

    return xout, backward


def _example_step(x, ctx, target, mod, norm_g, final_g, wts, hooks=None):
    hooks = hooks or {}
    na_weights, late_comm, late_weights = (hooks.get(k) for k in ("na_weights", "late_comm", "late_weights"))
    nx = x.shape[0]
    consts = _attn_mask()
    g_rows = [norm_g[i:i + 1] for i in range(4)]
    xc1, bwd0 = _pool_layer(x, g_rows[0], mod[0], wts["pool_w_in"][0], wts["pool_w_grp"][0],
                            wts["pool_w_out"][0], wts["pool_scale"][0:1], nx, "p0", ctx=ctx)
    if na_weights is not None:
        wts = {**wts, **na_weights(xc1)}
    x2, bwd1, carried = _na_layer(xc1, g_rows[1], mod[1], wts["na_w_in"], wts["na_rpb"], wts["na_w_out"], nx, consts,
                                  late_comm)
    if late_weights is not None:
        wts = {**wts, **late_weights(carried)}
    x3, bwd2 = _conv_layer(x2, g_rows[2], mod[2], wts["conv_w_in"], wts["conv_dw"], wts["conv_db"], wts["conv_w_out"])
    (loss, dx4, dfinal_g), bwd3 = _pool_layer(x3, g_rows[3], mod[3], wts["pool_w_in"][1], wts["pool_w_grp"][1],
                                              wts["pool_w_out"][1], wts["pool_scale"][1:2], nx, "p3",
                                              head=(final_g, target))
    call = lambda k, *args: hooks[k](*args) if k in hooks else None
    dx3, dmod3, dg3, gr3 = bwd3(dx4)
    dx2, dmod2, dg2, gr2 = bwd2(dx3)
    dxc1, dmod1, dg1, gr1, carried_bwd = bwd1(dx2, call("grad_comm", gr3, gr2))
    dx0, dmod0, dg0, gr0 = bwd0(dxc1, call("na_grads_start", gr1))
    return dict(
        loss=loss, grad_x=dx0, dmod=jnp.stack([dmod0, dmod1, dmod2, dmod3]),
        dnorm_g=jnp.stack([dg0, dg1, dg2, dg3]), dfinal_g=dfinal_g, layers=(gr0, gr1, gr2, gr3), carried=carried_bwd)


_AXES = ("x", "y", "c")
_CHIP_FLIPS = ((1, 0), (0, 1), (1, 1))


def _position():
    return tuple(lax.axis_index(a) for a in _AXES)


def _flipped(pos, flip):
    return tuple(1 - p if f else p for p, f in zip(pos, flip))


def _join_comms(comms):
    n_in = [len(c.ins) for c in comms]
    n_out = [len(c.outs) for c in comms]
    n_sem = [len(c.sems) for c in comms]

    def parts(ins, outs, sems):
        for k in range(len(comms)):
            a, b, s = sum(n_in[:k]), sum(n_out[:k]), sum(n_sem[:k])
            yield comms[k], (ins[a:a + n_in[k]], outs[b:b + n_out[k]], sems[s:s + n_sem[k]])

    def start(ins, outs, sems):
        for c, part in parts(ins, outs, sems):
            c.start(*part)

    def finish(ins, outs, sems):
        for c, part in parts(ins, outs, sems):
            c.finish(*part)

    joint = _Comm([a for c in comms for a in c.ins], [o for c in comms for o in c.outs],
                  [s for c in comms for s in c.sems], start, finish)
    return joint, lambda res: [list(res[sum(n_out[:k]):sum(n_out[:k + 1])]) for k in range(len(comms))]


def _run_comms(comms, name):
    joint, split = _join_comms(comms)

    def body(*refs):
        n_in, n_out = len(joint.ins), len(joint.outs)
        joint.start(refs[:n_in], refs[n_in:n_in + n_out], refs[n_in + n_out:])
        joint.finish(refs[:n_in], refs[n_in:n_in + n_out], refs[n_in + n_out:])

    res = pl.pallas_call(
        body, in_specs=[HBM_SPEC] * len(joint.ins), out_specs=[HBM_SPEC] * len(joint.outs), out_shape=joint.outs,
        scratch_shapes=joint.sems, name=name,
    )(*joint.ins)
    return split(res)


def _all_gather_comm(v, axes):
    flips = [f for f in np.ndindex(2, 2, 2) if any(f) and all(a in axes or not b for a, b in zip(_AXES, f))]
    n = len(flips) + 1

    def copies(ins, outs, sems):
        (v_ref,), (o_ref,), (send_sems, recv_sems, local_sem) = ins, outs, sems
        pos = _position()
        slot = 0
        for a, p in zip(_AXES, pos):
            if a in axes:
                slot = 2 * slot + p
        local = pltpu.make_async_copy(v_ref, o_ref.at[slot], local_sem)
        remote = [pltpu.make_async_remote_copy(v_ref, o_ref.at[slot], send_sems.at[k], recv_sems.at[k],
                                               device_id=_flipped(pos, flip), device_id_type=MESH)
                  for k, flip in enumerate(flips)]
        return [local] + remote

    def start(ins, outs, sems):
        for cp in copies(ins, outs, sems):
            cp.start()

    def finish(ins, outs, sems):
        for cp in copies(ins, outs, sems):
            cp.wait()

    sems = [pltpu.SemaphoreType.DMA((n - 1,)), pltpu.SemaphoreType.DMA((n - 1,)), pltpu.SemaphoreType.DMA(())]
    return _Comm([v], [_sds((n,) + v.shape, v.dtype)], sems, start, finish)


def _all_gather_two_level_comm(v):
    def copies(ins, outs, sems, onward):
        (v_ref,), (o_ref,), (send_sems, recv_sems, local_sem) = ins, outs, sems
        x, y, c = _position()
        sibling = (x, y, 1 - c)
        slot = lambda px, py, pc: o_ref.at[4 * px + 2 * py + pc]
        own = pltpu.make_async_copy(v_ref, slot(x, y, c), local_sem)
        first = [pltpu.make_async_remote_copy(v_ref, slot(x, y, c), send_sems.at[0], recv_sems.at[0],
                                              device_id=sibling, device_id_type=MESH)]
        fwd = []
        for k, flip in enumerate(_CHIP_FLIPS):
            px, py = _flipped((x, y), flip)
            first.append(pltpu.make_async_remote_copy(v_ref, slot(x, y, c), send_sems.at[1 + k], recv_sems.at[1 + k],
                                                      device_id=(px, py, c), device_id_type=MESH))
            if onward:
                fwd.append(pltpu.make_async_remote_copy(slot(px, py, c), slot(px, py, c), send_sems.at[4 + k],
                                                        recv_sems.at[4 + k], device_id=sibling, device_id_type=MESH))
        return own, first, fwd

    def start(ins, outs, sems):
        own, first, _ = copies(ins, outs, sems, False)
        for cp in [own] + first:
            cp.start()

    def finish(ins, outs, sems):
        own, first, fwd = copies(ins, outs, sems, True)
        for arrived, onward in zip(first[1:], fwd):
            arrived.wait_recv()
            onward.start()
        first[0].wait_recv()
        for cp in fwd:
            cp.wait_recv()
        for cp in first + fwd:
            cp.wait_send()
        own.wait()

    sems = [pltpu.SemaphoreType.DMA((7,)), pltpu.SemaphoreType.DMA((7,)), pltpu.SemaphoreType.DMA(())]
    return _Comm([v], [_sds((8,) + v.shape, v.dtype)], sems, start, finish)


def _all_gather(v, axes, name):
    return _run_comms([_all_gather_comm(v, axes)], name)[0][0]


class _Item:
    def __init__(self, key, layer, shape, shard_axis, half_axis):
        self.key, self.layer, self.shape = key, layer, tuple(shape)
        self.shard_axis, self.half_axis = shard_axis, half_axis
        self.shard = shape[shard_axis] // 4
        self.half = shape[half_axis] // 2

    def sized(self, shard=False, half=False):
        s = list(self.shape)
        if shard:
            s[self.shard_axis] = self.shard
        if half:
            s[self.half_axis] = self.half
        return tuple(s)

    def window(self, ref, chip=None, half=None):
        idx = [slice(None)] * len(self.shape)
        if chip is not None:
            idx[self.shard_axis] = pl.ds(chip * self.shard, self.shard)
        if half is not None:
            idx[self.half_axis] = pl.ds(half * self.half, self.half)
        return ref.at[tuple(idx)]


def _items(d, w):
    out = []
    for j in range(2):
        out += [_Item("pool_w_in", j, (d, 2 * w), 1, 0), _Item("pool_w_grp", j, (4, w // 4, w // 4), 1, 0),
                _Item("pool_w_out", j, (w, d), 0, 1)]
    out += [_Item("na_w_in", 0, (d, 4 * w), 1, 0), _Item("na_w_out", 0, (w, d), 0, 1),
            _Item("conv_w_in", 0, (d, 4 * w), 1, 0), _Item("conv_w_out", 0, (w, d), 0, 1)]
    return out


def _gather_comm(shards, items):
    n = len(items)

    def copies(src, dst, sems, onward):
        send_a, recv_a, send_b, recv_b, send_c, recv_c = sems
        x, y, c = _position()
        chip = 2 * x + y
        sibling = (x, y, 1 - c)
        own, out, fwd, fwd_in = [], [], [], []
        for i, it in enumerate(items):
            own.append(pltpu.make_async_remote_copy(src[i], it.window(dst[i], chip=chip), send_c.at[i], recv_c.at[i],
                                                    device_id=sibling, device_id_type=MESH))
            for k, flip in enumerate(_CHIP_FLIPS):
                px, py = _flipped((x, y), flip)
                s = 3 * i + k
                out.append(pltpu.make_async_remote_copy(
                    it.window(src[i], half=c), it.window(dst[i], chip=chip, half=c), send_a.at[s], recv_a.at[s],
                    device_id=(px, py, c), device_id_type=MESH))
                if onward:
                    got = it.window(dst[i], chip=2 * px + py, half=c)
                    fwd.append(pltpu.make_async_remote_copy(got, got, send_b.at[s], recv_b.at[s],
                                                            device_id=sibling, device_id_type=MESH))
                    other = it.window(dst[i], chip=2 * px + py, half=1 - c)
                    fwd_in.append(pltpu.make_async_remote_copy(other, other, send_b.at[s], recv_b.at[s],
                                                               device_id=sibling, device_id_type=MESH))
        return own, out, fwd, fwd_in

    def start(src, dst, sems):
        own, out, _, _ = copies(src, dst, sems, False)
        for cp in own + out:
            cp.start()

    def finish(src, dst, sems):
        own, out, fwd, fwd_in = copies(src, dst, sems, True)
        for arrived, onward in zip(out, fwd):
            arrived.wait_recv()
            onward.start()
        for cp in fwd_in:
            cp.wait_recv()
        for cp in out + fwd:
            cp.wait_send()
        for cp in own:
            cp.wait()

    sems = [pltpu.SemaphoreType.DMA((3 * n,)) for _ in range(4)] + [pltpu.SemaphoreType.DMA((n,)) for _ in range(2)]
    return _Comm(shards, [_sds(it.shape, BF16) for it in items], sems, start, finish)


def _pair_swap_copies(windows):
    def copies(src, got, sems):
        send_sems, recv_sems = sems
        x, y, c = _position()
        return [pltpu.make_async_remote_copy(windows[i](src[i], 1 - c), got[i], send_sems.at[i], recv_sems.at[i],
                                             device_id=(x, y, 1 - c), device_id_type=MESH)
                for i in range(len(windows))]

    return copies


def _pair_swap_comm(arrays, windows, out_shapes):
    n = len(arrays)
    copies = _pair_swap_copies(windows)

    def start(src, got, sems):
        for cp in copies(src, got, sems):
            cp.start()

    def finish(src, got, sems):
        for cp in copies(src, got, sems):
            cp.wait()

    return _Comm(arrays, out_shapes, [pltpu.SemaphoreType.DMA((n,)), pltpu.SemaphoreType.DMA((n,))], start, finish)


def _pair_swap(arrays, windows, out_shapes, name):
    return _run_comms([_pair_swap_comm(arrays, windows, out_shapes)], name)[0]


def _chip_exchange_copies(items):
    def copies(src, dst, sems):
        send_sems, recv_sems = sems
        x, y, c = _position()
        out = []
        for i, it in enumerate(items):
            for k, flip in enumerate(_CHIP_FLIPS):
                px, py = _flipped((x, y), flip)
                out.append(pltpu.make_async_remote_copy(
                    it.window(src[i], chip=2 * px + py), dst[i].at[k], send_sems.at[3 * i + k],
                    recv_sems.at[3 * i + k], device_id=(px, py, c), device_id_type=MESH))
        return out

    return copies


_SEM_SPEC = pl.BlockSpec(memory_space=pltpu.SEMAPHORE)
_DATAFLOW = pltpu.SideEffectType.DATAFLOW_SIDE_EFFECTING


def _split_start(copies, srcs, zones, n_copies, name):
    n, nz = len(srcs), len(zones)

    def body(*refs):
        src, land = refs[:n], refs[n:n + nz]
        send_sems, recv_sems = refs[n + nz:n + nz + 2]
        token = refs[-1]
        for cp in copies(src, land, (send_sems, recv_sems)):
            cp.start()
        token[...] = jnp.zeros(token.shape, F32)

    hbm = lambda t: pltpu.HBM(t.shape, t.dtype)
    res = pl.pallas_call(
        body, name=name,
        out_shape=(pltpu.SemaphoreType.DMA((n_copies,)), pltpu.SemaphoreType.DMA((n_copies,)),
                   *[hbm(t) for t in list(srcs) + list(zones)], _sds((8, 128), F32)),
        in_specs=[HBM_SPEC] * (n + nz),
        out_specs=(_SEM_SPEC, _SEM_SPEC, *[HBM_SPEC] * (n + nz), pl.BlockSpec(memory_space=pltpu.VMEM)),
        input_output_aliases={i: 2 + i for i in range(n + nz)},
        compiler_params=pltpu.CompilerParams(has_side_effects=_DATAFLOW),
    )(*[pltpu.with_memory_space_constraint(t, pltpu.HBM) for t in list(srcs) + list(zones)])
    return (res[0], res[1], list(res[2:2 + n]), list(res[2 + n:2 + n + nz])), res[-1]


def _split_wait(copies, handle, after, name):
    send_sems, recv_sems, srcs, zones = handle
    n, nz = len(srcs), len(zones)

    def body(*refs):
        src, land = refs[:n], refs[n:n + nz]
        send, recv = refs[n + nz:n + nz + 2]
        for cp in copies(src, land, (send, recv)):
            cp.wait_send()
            cp.wait_recv()

    hbm = lambda t: pltpu.HBM(t.shape, t.dtype)
    res = pl.pallas_call(
        body, name=name, out_shape=tuple(hbm(t) for t in list(srcs) + list(zones)),
        in_specs=[HBM_SPEC] * (n + nz) + [_SEM_SPEC, _SEM_SPEC, pl.BlockSpec(memory_space=pl.ANY)],
        out_specs=tuple([HBM_SPEC] * (n + nz)), input_output_aliases={i: i for i in range(n + nz)},
        compiler_params=pltpu.CompilerParams(has_side_effects=_DATAFLOW),
    )(*srcs, *zones, send_sems, recv_sems, after)
    return list(res[:n]), list(res[n:])


def _gather_ici_copies(items):
    def copies(src, dst, sems):
        send_sems, recv_sems = sems
        x, y, c = _position()
        chip = 2 * x + y
        out = []
        for i, it in enumerate(items):
            for k, flip in enumerate(_CHIP_FLIPS):
                px, py = _flipped((x, y), flip)
                out.append(pltpu.make_async_remote_copy(
                    it.window(src[i], half=c), it.window(dst[i], chip=chip, half=c), send_sems.at[3 * i + k],
                    recv_sems.at[3 * i + k], device_id=(px, py, c), device_id_type=MESH))
        return out

    return copies


def _gather_pair_finish(shards, mats, items, name):
    n = len(items)

    def body(*refs):
        src, dst = refs[:n], refs[2 * n:3 * n]
        send_own, recv_own, send_fwd, recv_fwd = refs[3 * n:]
        x, y, c = _position()
        chip = 2 * x + y
        sibling = (x, y, 1 - c)
        copies = []
        for i, it in enumerate(items):
            copies.append(pltpu.make_async_remote_copy(src[i], it.window(dst[i], chip=chip), send_own.at[i],
                                                       recv_own.at[i], device_id=sibling, device_id_type=MESH))
            for k, flip in enumerate(_CHIP_FLIPS):
                px, py = _flipped((x, y), flip)
                got = it.window(dst[i], chip=2 * px + py, half=c)
                copies.append(pltpu.make_async_remote_copy(got, got, send_fwd.at[3 * i + k], recv_fwd.at[3 * i + k],
                                                           device_id=sibling, device_id_type=MESH))
        for cp in copies:
            cp.start()
        for cp in copies:
            cp.wait()

    return pl.pallas_call(
        body, in_specs=[HBM_SPEC] * (2 * n), out_specs=[HBM_SPEC] * n, out_shape=[_sds(it.shape, BF16) for it in items],
        input_output_aliases={n + i: i for i in range(n)},
        scratch_shapes=[pltpu.SemaphoreType.DMA((n,)), pltpu.SemaphoreType.DMA((n,)),
                        pltpu.SemaphoreType.DMA((3 * n,)), pltpu.SemaphoreType.DMA((3 * n,))], name=name,
    )(*shards, *mats)


def _chip_exchange_comm(partials, items):
    n = len(items)
    copies = _chip_exchange_copies(items)

    def start(src, dst, sems):
        for cp in copies(src, dst, sems):
            cp.start()

    def finish(src, dst, sems):
        for cp in copies(src, dst, sems):
            cp.wait()

    return _Comm(partials, [_sds((3,) + it.sized(shard=True, half=True), BF16) for it in items],
                 [pltpu.SemaphoreType.DMA((3 * n,)), pltpu.SemaphoreType.DMA((3 * n,))], start, finish)


_SUM_STEPS = 2


def _pair_sums(gs, gots, its, pos, name):
    n = len(its)
    nb = _SUM_STEPS
    g2 = [g.reshape(-1, g.shape[-1]) for g in gs]
    got2 = [t.reshape(-1, t.shape[-1]) for t in gots]

    def body(pos_ref, *refs):
        for g_ref, got_ref, o_ref in zip(refs[:n], refs[n:2 * n], refs[2 * n:]):
            o_ref[...] = (g_ref[...].astype(F32) + got_ref[...].astype(F32)).astype(BF16)

    g_specs, got_specs = [], []
    for it, t in zip(its, got2):
        rows, cols = t.shape
        blk = (rows // nb, cols)
        g_map = (lambda i, pos: (pos[1] * nb + i, 0)) if it.half_axis == 0 else (lambda i, pos: (i, pos[1]))
        g_specs.append(pl.BlockSpec(blk, g_map))
        got_specs.append(pl.BlockSpec(blk, lambda i, pos: (i, 0)))
    outs = pl.pallas_call(
        body, grid_spec=pltpu.PrefetchScalarGridSpec(
            num_scalar_prefetch=1, grid=(nb,), in_specs=g_specs + got_specs, out_specs=got_specs),
        out_shape=[_sds(t.shape, BF16) for t in got2], name=name, compiler_params=_cparams("parallel"),
    )(pos, *g2, *got2)
    return [o.reshape(t.shape) for o, t in zip(outs, gots)]


_FLIP_SLOT = {2: 0, 1: 1, 3: 2}


def _chip_sums(pairs, slots, its, pos, name):
    n = len(its)
    nb = _SUM_STEPS

    def body(pos_ref, *refs):
        chip = pos_ref[0]
        for own in range(4):
            @pl.when(chip == own)
            def _():
                for p_ref, s_ref, o_ref in zip(refs[:n], refs[n:2 * n], refs[2 * n:]):
                    acc = None
                    for k in range(4):
                        v = (p_ref[...] if k == own else s_ref[_FLIP_SLOT[own ^ k]]).astype(F32)
                        acc = v if acc is None else acc + v
                    o_ref[...] = acc

    p_specs, s_specs, o_specs, shapes = [], [], [], []
    for it in its:
        shape = it.sized(shard=True, half=True)
        blk = (shape[0] // nb,) + shape[1:]
        rest = (0,) * (len(shape) - 1)

        def p_map(i, pos, it=it, nd=len(shape)):
            lead = i + (pos[0] * nb if it.shard_axis == 0 else 0)
            return (lead,) + tuple(pos[0] if ax == it.shard_axis else 0 for ax in range(1, nd))

        p_specs.append(pl.BlockSpec(blk, p_map))
        s_specs.append(pl.BlockSpec((3,) + blk, lambda i, pos, rest=rest: (0, i) + rest))
        o_specs.append(pl.BlockSpec(blk, lambda i, pos, rest=rest: (i,) + rest))
        shapes.append(_sds(shape, F32))
    return pl.pallas_call(
        body, grid_spec=pltpu.PrefetchScalarGridSpec(
            num_scalar_prefetch=1, grid=(nb,), in_specs=p_specs + s_specs, out_specs=o_specs),
        out_shape=shapes, name=name, compiler_params=_cparams("parallel"),
    )(pos, *pairs, *slots)


_GRAD_KEYS = ("pool_w_in", "pool_w_grp", "pool_w_out", "na_w_in", "na_w_out", "conv_w_in", "conv_w_out")


def _adamw_matrix(w, m, v, owns, others, it, pos, name):
    nl = w.shape[0]
    rows_split = it.half_axis == 0
    r, cdim = int(np.prod(w.shape[1:-1])), w.shape[-1]
    hr, hc = (r // 2, cdim) if rows_split else (r, cdim // 2)
    br = min(hr, 256)
    nb = hr // br
    c1 = 1.0 - ADAM_B1 ** ADAM_STEP
    c2 = 1.0 - ADAM_B2 ** ADAM_STEP

    def body(pos_ref, w_ref, m_ref, v_ref, *rest):
        own_refs, other_refs = rest[:nl], rest[nl:2 * nl]
        g_ref, d_ref, nm_ref, nv_ref = rest[2 * nl:]
        j, h = pl.program_id(0), pl.program_id(1)
        own, other = own_refs[0][...], other_refs[0][...]
        for q in range(1, nl):
            own = jnp.where(j == q, own_refs[q][...], own)
            other = jnp.where(j == q, other_refs[q][...], other)
        gv = jnp.where(h == pos_ref[1], own, other)
        nm = ADAM_B1 * m_ref[...] + (1.0 - ADAM_B1) * gv
        nv = ADAM_B2 * v_ref[...] + (1.0 - ADAM_B2) * (gv * gv)
        g_ref[...] = gv
        nm_ref[...] = nm
        nv_ref[...] = nv
        d_ref[...] = -ADAM_LR * ((nm / c1) / (jnp.sqrt(nv / c2) + ADAM_EPS) + ADAM_WD * w_ref[...])

    if rows_split:
        full = pl.BlockSpec((None, br, hc), lambda j, h, i, pos: (j, h * nb + i, 0))
    else:
        full = pl.BlockSpec((None, br, hc), lambda j, h, i, pos: (j, i, h))
    half = pl.BlockSpec((br, hc), lambda j, h, i, pos: (i, 0))
    flat = lambda t: t.reshape(nl, r, cdim)
    outs = pl.pallas_call(
        body, grid_spec=pltpu.PrefetchScalarGridSpec(
            num_scalar_prefetch=1, grid=(nl, 2, nb), in_specs=[full] * 3 + [half] * (2 * nl), out_specs=[full] * 4),
        out_shape=[_sds((nl, r, cdim), F32)] * 4, name=name,
        compiler_params=_cparams("parallel", "parallel", "parallel"),
    )(pos, flat(w), flat(m), flat(v), *[t.reshape(hr, hc) for t in list(owns) + list(others)])
    return tuple(t.reshape(w.shape) for t in outs)


_WEIGHTS = ("c_ctx", "norm_g", "ada_w", "ada_b", "pool_w_in", "pool_w_grp", "pool_scale", "pool_w_out", "na_w_in",
            "na_rpb", "na_w_out", "conv_w_in", "conv_dw", "conv_db", "conv_w_out", "final_g")
_COND_ROWS = 16


def _modulations(cond, ada_w, ada_b_cols):
    nl, d, n = ada_w.shape
    return _matmul(
        cond, ada_w, mode="nn", grid=(nl, 1), a_silu=True, epilogue="bias",
        a_spec=pl.BlockSpec((_COND_ROWS, d), lambda i, j: (0, 0)), b_spec=pl.BlockSpec((None, d, n), lambda i, j: (i, 0, 0)),
        extra=(ada_b_cols,), extra_specs=(pl.BlockSpec((None, 1, n), lambda i, j: (i, 0, 0)),),
        out_shapes=[_sds((nl, _COND_ROWS, n), F32)], out_specs=[pl.BlockSpec((None, _COND_ROWS, n), lambda i, j: (i, 0, 0))],
        name="modulations")[0]


def _ada_w_step(cond, dm_cols, w, m, v):
    nl, d, n = w.shape
    tr = d // 2
    c1 = 1.0 - ADAM_B1 ** ADAM_STEP
    c2 = 1.0 - ADAM_B2 ** ADAM_STEP

    def body(c_ref, dm_ref, w_ref, m_ref, v_ref, g_ref, d_ref, nm_ref, nv_ref):
        gv = lax.dot_general(_silu(c_ref[...]).astype(BF16), dm_ref[...].astype(BF16), _DIMS["tn"],
                             preferred_element_type=F32)
        nm = ADAM_B1 * m_ref[...] + (1.0 - ADAM_B1) * gv
        nv = ADAM_B2 * v_ref[...] + (1.0 - ADAM_B2) * (gv * gv)
        g_ref[...] = gv
        nm_ref[...] = nm
        nv_ref[...] = nv
        d_ref[...] = -ADAM_LR * ((nm / c1) / (jnp.sqrt(nv / c2) + ADAM_EPS) + ADAM_WD * w_ref[...])

    blk = pl.BlockSpec((None, tr, n), lambda l, i: (l, i, 0))
    return _call(
        body, (cond, dm_cols, w, m, v), grid=(nl, d // tr),
        in_specs=[pl.BlockSpec((_COND_ROWS, tr), lambda l, i: (0, i)),
                  pl.BlockSpec((None, _COND_ROWS, n), lambda l, i: (l, 0, 0)), blk, blk, blk],
        out_specs=[blk] * 4, out_shape=[_sds(w.shape, F32)] * 4, name="adamw_ada_w")


def _cond_grad(dm_cols, ada_w):
    nl, d, n = ada_w.shape
    return _matmul(
        dm_cols, ada_w, mode="nt", grid=(1, nl), nk=nl, acc_shape=(_COND_ROWS, d),
        a_spec=pl.BlockSpec((None, _COND_ROWS, n), lambda i, q: (q, 0, 0)), b_spec=pl.BlockSpec((None, d, n), lambda i, q: (q, 0, 0)),
        out_shapes=[_sds((_COND_ROWS, d), F32)], out_specs=[pl.BlockSpec((_COND_ROWS, d), lambda i, q: (0, 0))],
        name="cond_grad")[0]


def _pack(parts):
    flat = [p.reshape(-1) for p in parts]
    sizes = [f.shape[0] for f in flat]
    total = sum(sizes)
    rows = -(-total // 1024) * 8
    packed = jnp.concatenate(flat + [jnp.zeros((rows * 128 - total,), F32)]).reshape(rows, 128)
    offs = np.concatenate([[0], np.cumsum(sizes)])[:-1]
    return packed, [(int(o), p.shape) for o, p in zip(offs, parts)]


def _unpack(flat, layout, k):
    off, shape = layout[k]
    return flat[..., off:off + int(np.prod(shape))].reshape(flat.shape[:-1] + tuple(shape))


def kernel(x, c, ctx, c_ctx, norm_g, ada_w, ada_b, pool_w_in, pool_w_grp, pool_scale, pool_w_out, na_w_in, na_rpb, na_w_out, conv_w_in, conv_dw, conv_db, conv_w_out, final_g, loss_target, m_c_ctx, m_norm_g, m_ada_w, m_ada_b, m_pool_w_in, m_pool_w_grp, m_pool_scale, m_pool_w_out, m_na_w_in, m_na_rpb, m_na_w_out, m_conv_w_in, m_conv_dw, m_conv_db, m_conv_w_out, m_final_g, v_c_ctx, v_norm_g, v_ada_w, v_ada_b, v_pool_w_in, v_pool_w_grp, v_pool_scale, v_pool_w_out, v_na_w_in, v_na_rpb, v_na_w_out, v_conv_w_in, v_conv_dw, v_conv_db, v_conv_w_out, v_final_g):
    params = dict(c_ctx=c_ctx, norm_g=norm_g, ada_w=ada_w, ada_b=ada_b, pool_w_in=pool_w_in, pool_w_grp=pool_w_grp,
                  pool_scale=pool_scale, pool_w_out=pool_w_out, na_w_in=na_w_in, na_rpb=na_rpb, na_w_out=na_w_out,
                  conv_w_in=conv_w_in, conv_dw=conv_dw, conv_db=conv_db, conv_w_out=conv_w_out, final_g=final_g)
    mom1 = dict(c_ctx=m_c_ctx, norm_g=m_norm_g, ada_w=m_ada_w, ada_b=m_ada_b, pool_w_in=m_pool_w_in,
                pool_w_grp=m_pool_w_grp, pool_scale=m_pool_scale, pool_w_out=m_pool_w_out, na_w_in=m_na_w_in,
                na_rpb=m_na_rpb, na_w_out=m_na_w_out, conv_w_in=m_conv_w_in, conv_dw=m_conv_dw, conv_db=m_conv_db,
                conv_w_out=m_conv_w_out, final_g=m_final_g)
    mom2 = dict(c_ctx=v_c_ctx, norm_g=v_norm_g, ada_w=v_ada_w, ada_b=v_ada_b, pool_w_in=v_pool_w_in,
                pool_w_grp=v_pool_w_grp, pool_scale=v_pool_scale, pool_w_out=v_pool_w_out, na_w_in=v_na_w_in,
                na_rpb=v_na_rpb, na_w_out=v_na_w_out, conv_w_in=v_conv_w_in, conv_dw=v_conv_dw, conv_db=v_conv_db,
                conv_w_out=v_conv_w_out, final_g=v_final_g)
    d = x.shape[-1]
    w = na_w_out.shape[1] * 4
    xi, yi, ci = _position()
    chip = 2 * xi + yi
    dev = 2 * chip + ci
    n_ada = ada_w.shape[-1]

    def chip_cols(a, size):
        return lax.dynamic_slice_in_dim(a, chip * size, size, axis=a.ndim - 1)

    items = _items(d, w)
    first = [it for it in items if it.key.startswith("pool") and it.layer == 0]
    na = [it for it in items if it.key.startswith("na")]
    late = [it for it in items if it not in first + na]
    shards_of = lambda its: [params[it.key][it.layer].astype(BF16) for it in its]
    empties = lambda its: [lax.empty(it.shape, BF16) for it in its]
    first_copies, na_copies = _gather_ici_copies(first), _gather_ici_copies(na)

    conds = _all_gather(c.reshape(8, d // 8), _AXES, "gather_cond").reshape(8, d)
    behind = conds[0, 0] * 0.0
    first_handle, token = _split_start(first_copies, [s + behind.astype(BF16) for s in shards_of(first)],
                                       empties(first), 3 * len(first), "gather_first_start")
    cond = jnp.concatenate([conds + token[0, 0], c_ctx[None], jnp.zeros((_COND_ROWS - 9, d), F32)], axis=0)
    mod_cols = _modulations(cond, ada_w, chip_cols(ada_b, n_ada)[:, None, :])
    small_pack, small_layout = _pack([pool_scale, conv_dw, conv_db])
    (mod_all,), (small,) = _run_comms([_all_gather_comm(mod_cols, ("x", "y")),
                                       _all_gather_comm(small_pack, ("x", "y"))], "gather_mod")
    behind = mod_all[0, 0, 0, 0] * 0.0
    na_handle, token = _split_start(na_copies, [s + behind.astype(BF16) for s in shards_of(na)], empties(na),
                                    3 * len(na), "gather_na_start")
    first_shards, first_mats = _split_wait(first_copies, first_handle, token, "gather_first_wait")
    first_mats = _gather_pair_finish(first_shards, first_mats, first, "gather_first_pair")
    mod_all = mod_all.transpose(1, 2, 0, 3).reshape(4, _COND_ROWS, 3, d)
    mod = jnp.stack([lax.dynamic_index_in_dim(mod_all, dev, axis=1, keepdims=False), mod_all[:, 8]], axis=1)
    full = {(it.key, it.layer): mat for it, mat in zip(first, first_mats)}
    late_comm = _gather_comm(shards_of(late), late)

    def na_weights(after):
        na_shards, na_mats = _split_wait(na_copies, na_handle, after, "gather_na_wait")
        na_mats = _gather_pair_finish(na_shards, na_mats, na, "gather_na_pair")
        return {it.key: mat for it, mat in zip(na, na_mats)}

    def late_weights(mats):
        full.update({(it.key, it.layer): mat for it, mat in zip(late, mats)})
        return dict(pool_w_in=[full[("pool_w_in", j)] for j in range(2)],
                    pool_w_grp=[full[("pool_w_grp", j)] for j in range(2)],
                    pool_w_out=[full[("pool_w_out", j)] for j in range(2)],
                    conv_w_in=full[("conv_w_in", 0)], conv_w_out=full[("conv_w_out", 0)])

    small = small.reshape(4, -1)

    def whole(k):
        parts = _unpack(small, small_layout, k)
        return jnp.moveaxis(parts, 0, -2).reshape(parts.shape[1:-1] + (-1,))

    wts = dict(pool_w_in=[full[("pool_w_in", 0)]], pool_w_grp=[full[("pool_w_grp", 0)]],
               pool_w_out=[full[("pool_w_out", 0)]], pool_scale=whole(0), na_rpb=na_rpb[0], conv_dw=whole(1)[0],
               conv_db=whole(2))
    pos = jnp.stack([chip, ci]).astype(jnp.int32)

    def layer_grads(its, by_layer):
        pick = {"pool_w_in": "w_in", "pool_w_grp": "w_grp", "pool_w_out": "w_out", "na_w_in": "w_in",
                "na_w_out": "w_out", "conv_w_in": "w_in", "conv_w_out": "w_out"}
        return [by_layer[(it.key.split("_")[0], it.layer)][pick[it.key]] for it in its]

    pairs, handles = dict(), dict()
    half_windows = lambda its: [(lambda ref, half, it=it: it.window(ref, half=half)) for it in its]
    half_shapes = lambda its: [_sds(it.sized(half=True), BF16) for it in its]

    def pair_sums(its, mats, tag):
        got = _pair_swap(mats, half_windows(its), half_shapes(its), f"pair_exchange_{tag}")
        return _pair_sums(mats, got, its, pos, f"pair_sum_{tag}")

    def grad_comm(gr3, gr2):
        pairs["late"] = pair_sums(late, layer_grads(late, {("pool", 1): gr3, ("conv", 0): gr2}), "late")
        return _chip_exchange_comm(pairs["late"], late)

    slot_zones = lambda its: [lax.empty((3,) + it.sized(shard=True, half=True), BF16) for it in its]
    na_xcopies, first_xcopies = _chip_exchange_copies(na), _chip_exchange_copies(first)

    def na_grads_start(gr1):
        pairs["na"] = pair_sums(na, layer_grads(na, {("na", 0): gr1}), "na")
        handles["na"], started = _split_start(na_xcopies, pairs["na"], slot_zones(na), 3 * len(na),
                                              "exchange_na_start")
        return started

    res = _example_step(x[0], ctx[0], loss_target[0], mod, norm_g, final_g[None], wts, dict(
        na_weights=na_weights, late_comm=late_comm, late_weights=late_weights, grad_comm=grad_comm,
        na_grads_start=na_grads_start))
    g0, g1, g2, g3 = res["layers"]
    pairs["na"], na_slots = _split_wait(na_xcopies, handles["na"], g0["w_in"], "exchange_na_wait")
    first_grads = layer_grads(first, {("pool", 0): g0})
    packed, layout = _pack([res["dfinal_g"], res["dnorm_g"], res["dmod"], g1["rpb"],
                            jnp.concatenate([g0["scale"], g3["scale"]], axis=0), g2["dw"], g2["db"],
                            res["loss"][0, :1]])
    first_got, (every,) = _run_comms([_pair_swap_comm(first_grads, half_windows(first), half_shapes(first)),
                                      _all_gather_two_level_comm(packed)], "pair_exchange_first")
    pairs["first"] = _pair_sums(first_grads, first_got, first, pos, "pair_sum_first")

    grads = dict()
    total = _sum_lead(every, "sum_vec_grads").reshape(-1)
    every = every.reshape(8, -1)
    grads["final_g"] = _unpack(total, layout, 0).reshape(final_g.shape)
    grads["norm_g"] = _unpack(total, layout, 1)
    grads["na_rpb"] = _unpack(total, layout, 3)[None]
    grads["pool_scale"] = chip_cols(_unpack(total, layout, 4), pool_scale.shape[-1])
    grads["conv_dw"] = chip_cols(_unpack(total, layout, 5), conv_dw.shape[-1])[None]
    grads["conv_db"] = chip_cols(_unpack(total, layout, 6), conv_db.shape[-1])
    dmod_sum = _unpack(total, layout, 2).reshape(4, 2, 3 * d)
    dmod_each = _unpack(every, layout, 2).reshape(8, 4, 2, 3 * d)
    grads["ada_b"] = dmod_sum[:, 0] + dmod_sum[:, 1]
    dm = jnp.concatenate([dmod_each[:, :, 0].transpose(1, 0, 2), dmod_sum[:, 1][:, None],
                          jnp.zeros((4, _COND_ROWS - 9, 3 * d), F32)], axis=1)
    dm_cols = chip_cols(dm, n_ada)
    dcond = _cond_grad(dm_cols, ada_w)[8].reshape(8, d // 8)
    dcond_all = _all_gather(dcond, ("x", "y"), "gather_cond_grad")
    behind = dcond_all[0, 0, 0] * 0.0
    handles["first"], token = _split_start(first_xcopies, [p + behind.astype(BF16) for p in pairs["first"]],
                                           slot_zones(first), 3 * len(first), "exchange_first_start")
    grads["ada_w"], *ada_w_step = _ada_w_step(cond, dm_cols + token[0, 0], ada_w, m_ada_w, v_ada_w)
    grads["c_ctx"] = _sum_lead(dcond_all, "sum_cond_grad").reshape(d) * _dsilu(c_ctx)
    vector_out = {k: _adamw(params[k], grads[k], mom1[k], mom2[k], f"adamw_{k}")
                  for k in _WEIGHTS if k not in _GRAD_KEYS + ("ada_w",)}
    vector_out["ada_w"] = tuple(ada_w_step)
    pairs["first"], first_slots = _split_wait(first_xcopies, handles["first"], vector_out["ada_w"][2],
                                              "exchange_first_wait")

    slots = dict(zip(late, res["carried"]))
    slots.update(zip(first, first_slots))
    slots.update(zip(na, na_slots))
    pair_of = dict(zip(late, pairs["late"]))
    pair_of.update(zip(first, pairs["first"]))
    pair_of.update(zip(na, pairs["na"]))
    reduced = _chip_sums([pair_of[it] for it in items], [slots[it] for it in items], items, pos, "chip_sum")
    theirs = _pair_swap(reduced, [lambda ref, half: ref] * len(items),
                        [_sds(t.shape, F32) for t in reduced], "pair_return")
    matrix_out = dict()
    for k in _GRAD_KEYS:
        idx = [i for i, it in enumerate(items) if it.key == k]
        res_k = _adamw_matrix(params[k], mom1[k], mom2[k], [reduced[i] for i in idx], [theirs[i] for i in idx],
                              items[idx[0]], pos, f"adamw_{k}")
        grads[k], matrix_out[k] = res_k[0], res_k[1:]

    outs = [[], [], []]
    for k in _WEIGHTS:
        step = matrix_out[k] if k in matrix_out else vector_out[k]
        for lst, val in zip(outs, step):
            lst.append(val)
    loss = _unpack(total, layout, 7)[0]
    return (loss, res["grad_x"][None], *[grads[k].reshape(params[k].shape) for k in _WEIGHTS],
            *outs[0], *outs[1], *outs[2])
```

```python
import functools

import numpy as np
import jax
import jax.numpy as jnp
from jax import lax
from jax.experimental import pallas as pl
from jax.experimental.pallas import tpu as pltpu

F32 = jnp.float32
BF16 = jnp.bfloat16

EPS = 1e-6
GRID_W = 64
HEAD_DIM = 64
WIN_ROWS = 8
WIN_COLS = 16
POOL_WINDOWS = (2, 4, 8, 16)
Q_ROWS = 4
K_ROWS = 12
PAD_ROWS = 4
NEG = -1e30

ADAM_LR = 0.001
ADAM_B1 = 0.9
ADAM_B2 = 0.999
ADAM_EPS = 1e-08
ADAM_WD = 0.01
ADAM_STEP = 10

ROW_BLOCK = 256
VMEM_LIMIT = 56 * 1024 * 1024
MAX_BLOCK_BYTES = 4 * 1024 * 1024
ACT = BF16

MESH = pl.DeviceIdType.MESH
HBM_SPEC = pl.BlockSpec(memory_space=pltpu.HBM)


def _cparams(*sem):
    return pltpu.CompilerParams(dimension_semantics=sem or None, vmem_limit_bytes=VMEM_LIMIT)


def _sds(shape, dtype):
    return jax.ShapeDtypeStruct(tuple(shape), dtype)


def _call(body, args, *, grid, in_specs, out_specs, out_shape, name, scratch_shapes=()):
    return list(pl.pallas_call(
        body, grid=grid, in_specs=list(in_specs), out_specs=list(out_specs), out_shape=list(out_shape),
        scratch_shapes=list(scratch_shapes), name=name, compiler_params=_cparams(*(("arbitrary",) * len(grid))),
    )(*args))


def _sigmoid(x):
    return 1.0 / (1.0 + jnp.exp(-x))


def _silu(x):
    return x * _sigmoid(x)


def _dsilu(x):
    s = _sigmoid(x)
    return s * (1.0 + x * (1.0 - s))


_DIMS = {
    "nn": (((1,), (0,)), ((), ())),
    "nt": (((1,), (1,)), ((), ())),
    "tn": (((0,), (0,)), ((), ())),
}


def _matmul(a, b, *, mode, grid, a_spec, b_spec, out_shapes, out_specs, name, nk=1,
            a_silu=False, exact=False, epilogue=None, extra=(), extra_specs=(), acc_shape=None):
    n_extra = len(extra)
    n_out = len(out_shapes)

    def body(*refs):
        a_ref, b_ref = refs[:2]
        ex = refs[2:2 + n_extra]
        outs = refs[2 + n_extra:2 + n_extra + n_out]
        av = a_ref[...]
        bv = b_ref[...]
        if a_silu:
            av = _silu(av.astype(F32))
        if exact:
            prod = lax.dot_general(av.astype(F32), bv.astype(F32), _DIMS[mode],
                                   precision=lax.Precision.HIGHEST, preferred_element_type=F32)
        else:
            prod = lax.dot_general(av.astype(BF16), bv.astype(BF16), _DIMS[mode], preferred_element_type=F32)

        def finish(res):
            if epilogue == "bias":
                res = res + ex[0][...]
            outs[0][...] = res.astype(outs[0].dtype)

        if nk == 1:
            finish(prod)
        else:
            acc = refs[-1]
            k = pl.program_id(len(grid) - 1)

            @pl.when(k == 0)
            def _():
                acc[...] = prod

            @pl.when(k > 0)
            def _():
                acc[...] += prod

            @pl.when(k == nk - 1)
            def _():
                finish(acc[...])

    scratch = [pltpu.VMEM(acc_shape, F32)] if nk > 1 else []
    sem = ("parallel",) * (len(grid) - 1) + ("arbitrary",)
    return pl.pallas_call(
        body, grid=grid, in_specs=[a_spec, b_spec, *extra_specs], out_specs=list(out_specs),
        out_shape=list(out_shapes), scratch_shapes=scratch, name=name, compiler_params=_cparams(*sem),
    )(a, b, *extra)


def _row_tile(rows):
    for t in (768, 512, 256):
        if rows % t == 0:
            return t
    return rows


def _mm_tn(a, b, name, out_dtype, tm=512):
    r, m = a.shape
    n = b.shape[1]
    tm = min(tm, m)
    tn = min(1024, n)
    return _matmul(
        a, b, mode="tn", grid=(m // tm, n // tn),
        a_spec=pl.BlockSpec((r, tm), lambda i, j: (0, i)), b_spec=pl.BlockSpec((r, tn), lambda i, j: (0, j)),
        out_shapes=[_sds((m, n), out_dtype)], out_specs=[pl.BlockSpec((tm, tn), lambda i, j: (i, j))], name=name)[0]


def _mm_tn_parts(a, b, name, out_dtype, tm=1024):
    r, m = a.shape
    p, _, np_ = b.shape
    tm = min(tm, m)
    return _matmul(
        a, b, mode="tn", grid=(m // tm, p),
        a_spec=pl.BlockSpec((r, tm), lambda i, q: (0, i)), b_spec=pl.BlockSpec((None, r, np_), lambda i, q: (q, 0, 0)),
        out_shapes=[_sds((m, p * np_), out_dtype)], out_specs=[pl.BlockSpec((tm, np_), lambda i, q: (i, q))],
        name=name)[0]


def _row_vec(ref, is_ctx):
    return ref[0] if is_ctx is None else jnp.where(is_ctx, ref[1], ref[0])


def _ctx_rows(i, tm, nx, nseg):
    if nseg == 1:
        return None
    return i * tm + lax.broadcasted_iota(jnp.int32, (tm, 1), 0) >= nx


def _seg_sums(ref, val, is_ctx, first):
    if is_ctx is None:
        parts = [jnp.sum(val, axis=0, keepdims=True)]
    else:
        parts = [jnp.sum(jnp.where(is_ctx, 0.0, val), axis=0, keepdims=True),
                 jnp.sum(jnp.where(is_ctx, val, 0.0), axis=0, keepdims=True)]

    @pl.when(first)
    def _():
        for k, p in enumerate(parts):
            ref[k] = p

    @pl.when(jnp.logical_not(first))
    def _():
        for k, p in enumerate(parts):
            ref[k] += p


def _w_out_resid(a, w_out, xres, gate, nx, name):
    m, k = a.shape
    n = w_out.shape[1]
    nseg = gate.shape[0]
    tm = _row_tile(m)

    def body(a_ref, w_ref, x_ref, gt_ref, yx_ref, xo_ref):
        yx = jnp.dot(a_ref[...], w_ref[...], preferred_element_type=F32)
        yx_ref[...] = yx.astype(ACT)
        xo_ref[...] = x_ref[...] + _row_vec(gt_ref, _ctx_rows(pl.program_id(0), tm, nx, nseg)) * yx

    row = pl.BlockSpec((tm, n), lambda i: (i, 0))
    return pl.pallas_call(
        body, grid=(m // tm,),
        in_specs=[pl.BlockSpec((tm, k), lambda i: (i, 0)), pl.BlockSpec((k, n), lambda i: (0, 0)), row,
                  pl.BlockSpec((nseg, 1, n), lambda i: (0, 0, 0))],
        out_specs=[row, row], out_shape=[_sds((m, n), ACT), _sds((m, n), F32)],
        name=name, compiler_params=_cparams("parallel"),
    )(a, w_out, xres, gate)


def _norm_w_in(x, g, scale, shift, w_in, nx, name, ctx=None):
    d = x.shape[1]
    rows = x.shape[0] + (0 if ctx is None else ctx.shape[0])
    n = w_in.shape[1]
    nseg = scale.shape[0]
    tm = _row_tile(rows)
    tn = n
    row = pl.BlockSpec((tm, d), lambda i, j: (i, 0))
    if ctx is None:
        row_args, row_specs = (x,), [row]
    else:
        assert ctx.shape[0] == ROW_BLOCK and tm % ROW_BLOCK == 0 and nx % ROW_BLOCK == 0
        nsub, x_blocks = tm // ROW_BLOCK, nx // ROW_BLOCK
        row_args = (x,) * nsub + (ctx,)
        row_specs = [pl.BlockSpec((ROW_BLOCK, d), lambda i, j, s=s: (jnp.minimum(i * nsub + s, x_blocks - 1), 0))
                     for s in range(nsub)] + [pl.BlockSpec((ROW_BLOCK, d), lambda i, j: (0, 0))]

    def body(*refs):
        x_refs, (g_ref, sc_ref, sh_ref, w_ref), outs = refs[:len(row_args)], refs[len(row_args):][:4], refs[-3:]
        h_ref, r_ref, p_ref = outs
        i, j = pl.program_id(0), pl.program_id(1)

        @pl.when(j == 0)
        def _():
            if ctx is None:
                xv = x_refs[0][...]
            else:
                xv = jnp.concatenate([jnp.where(i * nsub + s >= x_blocks, x_refs[-1][...], x_refs[s][...])
                                      for s in range(nsub)], axis=0)
                refs[-4][...] = xv
            r = lax.rsqrt(jnp.mean(xv * xv, axis=-1, keepdims=True) + EPS)
            is_ctx = _ctx_rows(i, tm, nx, nseg)
            h = (xv * r) * g_ref[...] * (1.0 + _row_vec(sc_ref, is_ctx)) + _row_vec(sh_ref, is_ctx)
            h_ref[...] = h.astype(BF16)
            r_ref[...] = r

        p_ref[...] = jnp.dot(h_ref[...], w_ref[...], preferred_element_type=F32).astype(ACT)

    vec = pl.BlockSpec((nseg, 1, d), lambda i, j: (0, 0, 0))
    joined = [] if ctx is None else [(row, _sds((rows, d), F32))]
    out_specs, out_shape = zip(*joined, (row, _sds((rows, d), BF16)),
                               (pl.BlockSpec((tm, 1), lambda i, j: (i, 0)), _sds((rows, 1), F32)),
                               (pl.BlockSpec((tm, tn), lambda i, j: (i, j)), _sds((rows, n), ACT)))
    return _call(
        body, (*row_args, g, scale, shift, w_in), grid=(rows // tm, n // tn),
        in_specs=[*row_specs, pl.BlockSpec((1, d), lambda i, j: (0, 0)), vec, vec,
                  pl.BlockSpec((d, tn), lambda i, j: (0, j))],
        out_specs=list(out_specs), out_shape=list(out_shape), name=name)


def _gate_w_out_bwd(dxo, yx, gate, w_out, nx, name):
    rows, d = yx.shape
    w = w_out.shape[0]
    nseg = gate.shape[0]
    tm = _row_tile(rows)

    def body(dx_ref, yx_ref, gt_ref, w_ref, dyx_ref, da_ref, dg_ref):
        i = pl.program_id(0)
        is_ctx = _ctx_rows(i, tm, nx, nseg)
        dxv = dx_ref[...]
        dyx = (dxv * _row_vec(gt_ref, is_ctx)).astype(BF16)
        dyx_ref[...] = dyx
        da_ref[...] = lax.dot_general(dyx, w_ref[...], _DIMS["nt"], preferred_element_type=F32).astype(ACT)
        _seg_sums(dg_ref, dxv * yx_ref[...].astype(F32), is_ctx, i == 0)

    row = pl.BlockSpec((tm, d), lambda i: (i, 0))
    vec = pl.BlockSpec((nseg, 1, d), lambda i: (0, 0, 0))
    return _call(
        body, (dxo, yx, gate, w_out), grid=(rows // tm,),
        in_specs=[row, row, vec, pl.BlockSpec((w, d), lambda i: (0, 0))],
        out_specs=[row, pl.BlockSpec((tm, w), lambda i: (i, 0)), vec],
        out_shape=[_sds((rows, d), BF16), _sds((rows, w), ACT), _sds((nseg, 1, d), F32)], name=name)


def _w_in_bwd_norm(dparts, w_in, x, r, g, scale, dres, nx, name, dx_rows=None):
    np_, rows, kp = dparts.shape
    d = w_in.shape[0]
    nseg = scale.shape[0]
    tm = _row_tile(rows)
    assert dx_rows is None or rows - tm < dx_rows <= rows
    nsub = tm // ROW_BLOCK
    nres_blocks = dres.shape[0] // ROW_BLOCK
    pp = np_
    while pp % 2 == 0 and pp * tm * kp * dparts.dtype.itemsize > MAX_BLOCK_BYTES:
        pp //= 2
    nk = np_ // pp

    def body(dp_ref, w_ref, x_ref, r_ref, g_ref, sc_ref, *rest):
        dres_refs = rest[:nsub]
        dx_ref, dsh_ref, dge_ref, *acc = rest[nsub:]
        i, k = pl.program_id(0), pl.program_id(1)
        prod = sum(lax.dot_general(dp_ref[q], w_ref[:, q * kp:(q + 1) * kp], _DIMS["nt"], preferred_element_type=F32)
                   for q in range(pp))

        def finish(dhv):
            is_ctx = _ctx_rows(i, tm, nx, nseg)
            rv = r_ref[...]
            xn = x_ref[...] * rv
            dxn = dhv * (g_ref[...] * (1.0 + _row_vec(sc_ref, is_ctx)))
            dx = rv * (dxn - xn * jnp.mean(dxn * xn, axis=-1, keepdims=True))
            for s in range(nsub):
                piece = slice(s * ROW_BLOCK, (s + 1) * ROW_BLOCK)
                res = dres_refs[s][...]
                if nres_blocks * ROW_BLOCK < rows:
                    res = jnp.where(i * nsub + s < nres_blocks, res, 0.0)
                dx_ref[piece, :] = dx[piece, :] + res
            _seg_sums(dsh_ref, dhv, is_ctx, i == 0)
            _seg_sums(dge_ref, dhv * xn, is_ctx, i == 0)

        if nk == 1:
            finish(prod)
        else:
            acc_ref, = acc

            @pl.when(k == 0)
            def _():
                acc_ref[...] = prod

            @pl.when(k > 0)
            def _():
                acc_ref[...] += prod

            @pl.when(k == nk - 1)
            def _():
                finish(acc_ref[...])

    row = pl.BlockSpec((tm, d), lambda i, k: (i, 0))
    vec = pl.BlockSpec((nseg, 1, d), lambda i, k: (0, 0, 0))
    return _call(
        body, (dparts, w_in, x, r, g, scale, *([dres] * nsub)), grid=(rows // tm, nk),
        in_specs=[pl.BlockSpec((pp, tm, kp), lambda i, k: (k, i, 0)), pl.BlockSpec((d, pp * kp), lambda i, k: (0, k)),
                  row, pl.BlockSpec((tm, 1), lambda i, k: (i, 0)), pl.BlockSpec((1, d), lambda i, k: (0, 0)), vec]
        + [pl.BlockSpec((ROW_BLOCK, d), (lambda i, k, s=s: (jnp.minimum(i * nsub + s, nres_blocks - 1), 0)))
           for s in range(nsub)],
        out_specs=[row, vec, vec],
        out_shape=[_sds((dx_rows or rows, d), F32), _sds((nseg, 1, d), F32), _sds((nseg, 1, d), F32)],
        scratch_shapes=[pltpu.VMEM((tm, d), F32)] * (nk > 1), name=name)


_PAD_TOP = 16
_PAD_BOT = 32


def _window_sum(buf, xv, lo, n):
    t = xv.shape[0]
    c = xv.shape[1]
    tp = t + _PAD_TOP + _PAD_BOT
    buf[pl.ds(0, _PAD_TOP), :] = jnp.zeros((_PAD_TOP, c), F32)
    buf[pl.ds(_PAD_TOP, t), :] = xv
    buf[pl.ds(_PAD_TOP + t, _PAD_BOT), :] = jnp.zeros((_PAD_BOT, c), F32)
    p = buf[...]
    k = 1
    while k < n:
        p = p + pltpu.roll(p, tp - k, 0)
        k *= 2
    if lo:
        p = pltpu.roll(p, -lo, 0)
    buf[...] = p
    return buf[pl.ds(_PAD_TOP, t), :]


def _window_count(t, half):
    pos = lax.broadcasted_iota(jnp.int32, (t, 1), 0)
    return (jnp.minimum(pos + half, t) - jnp.maximum(pos - half, 0)).astype(F32)


def _segments(rows, nx):
    return [(0, nx)] + ([(nx, rows - nx)] if rows > nx else [])


def _pool_scratch(rows, nx, cols):
    return [pltpu.VMEM((length + _PAD_TOP + _PAD_BOT, cols), F32) for _, length in _segments(rows, nx)]


def _per_group(g, fn):
    for gi, win in enumerate(POOL_WINDOWS):
        pl.when(g == gi)(functools.partial(fn, win))


def _pool_grp_fwd(uv, w_grp, scale, nx, name):
    rows = uv.shape[0]
    ng, gc, _ = w_grp.shape
    w = ng * gc
    segs = _segments(rows, nx)

    def body(u_ref, gt_ref, w_ref, sc_ref, z_ref, mx_ref, a_ref, *bufs):
        def pool(win):
            half = win // 2
            for (start, length), buf in zip(segs, bufs):
                uvv = u_ref[pl.ds(start, length), :].astype(F32)
                s = _window_sum(buf, uvv, -half, win)
                z_ref[pl.ds(start, length), :] = (s / _window_count(length, half) - uvv).astype(BF16)

        _per_group(pl.program_id(0), pool)
        mixed = jnp.dot(z_ref[...], w_ref[...], preferred_element_type=F32)
        mx_ref[...] = mixed.astype(ACT)
        a_ref[...] = (mixed * sc_ref[...] * _silu(gt_ref[...].astype(F32))).astype(BF16)

    col = pl.BlockSpec((rows, gc), lambda g: (0, g))
    return _call(
        body, (uv, uv, w_grp, scale), grid=(ng,),
        in_specs=[col, pl.BlockSpec((rows, gc), lambda g: (0, ng + g)), pl.BlockSpec((None, gc, gc), lambda g: (g, 0, 0)),
                  pl.BlockSpec((1, gc), lambda g: (0, g))],
        out_specs=[col, col, col], out_shape=[_sds((rows, w), BF16), _sds((rows, w), ACT), _sds((rows, w), BF16)],
        scratch_shapes=_pool_scratch(rows, nx, gc), name=name)


def _pool_grp_bwd(da, mixed, uv, scale, w_grp, nx, name):
    rows, w = da.shape
    ng, gc, _ = w_grp.shape
    segs = _segments(rows, nx)

    def body(da_ref, mx_ref, gt_ref, sc_ref, w_ref, dm_ref, duv_ref, dsc_ref, dz_ref, *bufs):
        dav = da_ref[...].astype(F32)
        mixed = mx_ref[...].astype(F32)
        gt = gt_ref[...].astype(F32)
        sg = _silu(gt)
        sc = sc_ref[...]
        dm = (dav * sc * sg).astype(BF16)
        dm_ref[...] = dm
        dz_ref[...] = lax.dot_general(dm, w_ref[...], _DIMS["nt"], preferred_element_type=F32)
        duv_ref[1] = (dav * mixed * sc * _dsilu(gt)).astype(BF16)
        dsc_ref[...] = jnp.sum(dav * mixed * sg, axis=0, keepdims=True)

        def unpool(win):
            half = win // 2
            for (start, length), buf in zip(segs, bufs):
                dzv = dz_ref[pl.ds(start, length), :]
                s = _window_sum(buf, dzv / _window_count(length, half), 1 - half, win)
                duv_ref[0, pl.ds(start, length), :] = (s - dzv).astype(BF16)

        _per_group(pl.program_id(0), unpool)

    col = pl.BlockSpec((rows, gc), lambda g: (0, g))
    vec = pl.BlockSpec((1, gc), lambda g: (0, g))
    return pl.pallas_call(
        body, grid=(ng,),
        in_specs=[col, col, pl.BlockSpec((rows, gc), lambda g: (0, ng + g)), vec,
                  pl.BlockSpec((None, gc, gc), lambda g: (g, 0, 0))],
        out_specs=[col, pl.BlockSpec((2, rows, gc), lambda g: (0, 0, g)), vec],
        out_shape=[_sds((rows, w), BF16), _sds((2, rows, w), BF16), _sds((1, w), F32)],
        scratch_shapes=[pltpu.VMEM((rows, gc), F32)] + _pool_scratch(rows, nx, gc),
        name=name, compiler_params=_cparams("parallel"),
    )(da, mixed, uv, scale, w_grp)


def _grp_wgrad(z, dm, ng, name, out_dtype):
    rows, w = z.shape
    gc = w // ng

    def body(z_ref, dm_ref, o_ref):
        o_ref[...] = lax.dot_general(z_ref[...], dm_ref[...], _DIMS["tn"],
                                     preferred_element_type=F32).astype(o_ref.dtype)

    blk = pl.BlockSpec((rows, gc), lambda g: (0, g))
    return pl.pallas_call(
        body, grid=(ng,), in_specs=[blk, blk], out_specs=pl.BlockSpec((None, gc, gc), lambda g: (g, 0, 0)),
        out_shape=_sds((ng, gc, gc), out_dtype), name=name, compiler_params=_cparams("parallel"),
    )(z, dm)


def _shift_rows(v, by):
    t = v.shape[0]
    pos = lax.broadcasted_iota(jnp.int32, v.shape, 0)
    rolled = pltpu.roll(v, by % t, 0)
    keep = pos >= by if by > 0 else pos < t + by
    return jnp.where(keep, rolled, 0.0)


def _conv_specs(t, w, cb):
    return [pl.BlockSpec((t, cb), (lambda j, q=q: (0, q * (w // cb) + j))) for q in range(4)]


def _conv_fwd(p4, dw, db, name):
    t = p4.shape[0]
    w = p4.shape[1] // 4
    cb = 128

    def body(bg_ref, cg_ref, v_ref, g_ref, dw_ref, db_ref, a_ref):
        tv = cg_ref[...].astype(F32) * v_ref[...].astype(F32)
        conv = (dw_ref[0:1, :] * _shift_rows(tv, 1) + dw_ref[1:2, :] * tv + dw_ref[2:3, :] * _shift_rows(tv, -1)
                + db_ref[...])
        a_ref[...] = (bg_ref[...].astype(F32) * conv * _silu(g_ref[...].astype(F32))).astype(BF16)

    return pl.pallas_call(
        body, grid=(w // cb,),
        in_specs=_conv_specs(t, w, cb) + [pl.BlockSpec((3, cb), lambda j: (0, j)), pl.BlockSpec((1, cb), lambda j: (0, j))],
        out_specs=pl.BlockSpec((t, cb), lambda j: (0, j)), out_shape=_sds((t, w), BF16),
        name=name, compiler_params=_cparams("parallel"),
    )(p4, p4, p4, p4, dw, db)


def _conv_bwd(da, p4, dw, db, name):
    t, w = da.shape
    cb = 128

    def body(da_ref, bg_ref, cg_ref, v_ref, g_ref, dw_ref, db_ref, d4_ref, ddw_ref, ddb_ref):
        cg = cg_ref[...].astype(F32)
        vv = v_ref[...].astype(F32)
        bg = bg_ref[...].astype(F32)
        gv = g_ref[...].astype(F32)
        tv = cg * vv
        tm1 = _shift_rows(tv, 1)
        tp1 = _shift_rows(tv, -1)
        w0, w1, w2 = dw_ref[0:1, :], dw_ref[1:2, :], dw_ref[2:3, :]
        conv = w0 * tm1 + w1 * tv + w2 * tp1 + db_ref[...]
        y = bg * conv
        dav = da_ref[...].astype(F32)
        dy = dav * _silu(gv)
        d4_ref[3] = (dav * y * _dsilu(gv)).astype(BF16)
        d4_ref[0] = (dy * conv).astype(BF16)
        dconv = dy * bg
        ddb_ref[...] = jnp.sum(dconv, axis=0, keepdims=True)
        ddw_ref[0:1, :] = jnp.sum(dconv * tm1, axis=0, keepdims=True)
        ddw_ref[1:2, :] = jnp.sum(dconv * tv, axis=0, keepdims=True)
        ddw_ref[2:3, :] = jnp.sum(dconv * tp1, axis=0, keepdims=True)
        dt = w0 * _shift_rows(dconv, -1) + w1 * dconv + w2 * _shift_rows(dconv, 1)
        d4_ref[1] = (dt * vv).astype(BF16)
        d4_ref[2] = (dt * cg).astype(BF16)

    col = pl.BlockSpec((t, cb), lambda j: (0, j))
    tap = pl.BlockSpec((3, cb), lambda j: (0, j))
    bias = pl.BlockSpec((1, cb), lambda j: (0, j))
    return pl.pallas_call(
        body, grid=(w // cb,), in_specs=[col] + _conv_specs(t, w, cb) + [tap, bias],
        out_specs=[pl.BlockSpec((4, t, cb), lambda j: (0, 0, j)), tap, bias],
        out_shape=[_sds((4, t, w), BF16), _sds((3, w), F32), _sds((1, w), F32)],
        name=name, compiler_params=_cparams("parallel"),
    )(da, p4, p4, p4, p4, dw, db)


def _attn_mask():
    qn, kn = Q_ROWS * GRID_W, K_ROWS * GRID_W
    qr, qc = np.divmod(np.arange(qn), GRID_W)
    kr, kc = np.divmod(np.arange(kn), GRID_W)
    col0 = np.clip(qc - WIN_COLS // 2, 0, GRID_W - WIN_COLS)
    col_ok = (kc[None, :] >= col0[:, None]) & (kc[None, :] < col0[:, None] + WIN_COLS)
    first = np.zeros(qn, np.int64)
    last = np.full(qn, K_ROWS - WIN_ROWS)
    out = []
    for row0 in (first, qr, last):
        row_ok = (kr[None, :] >= row0[:, None]) & (kr[None, :] < row0[:, None] + WIN_ROWS)
        out.append(np.where(row_ok & col_ok, 0.0, NEG))
    return jnp.asarray(np.stack(out), F32)


_KW = K_ROWS * GRID_W
_QB = Q_ROWS * GRID_W
_PAIR = 2 * HEAD_DIM
_N_DR = 2 * WIN_ROWS - 1
_N_DC = 2 * WIN_COLS - 1
_RP_ROWS = 24
_N_TILES = _N_DR + 1
_BIAS_BASE = (WIN_ROWS - 1, WIN_ROWS // 2 - 1, -1)


class _Comm:
    def __init__(self, ins, outs, sems, start, finish):
        self.ins, self.outs, self.sems, self.start, self.finish = list(ins), list(outs), list(sems), start, finish


def _bias_pieces(cls):
    out = []
    for qr in range(Q_ROWS):
        for kr in range(0, K_ROWS, 2):
            tile = _BIAS_BASE[cls] - qr + kr + 1
            out.append((qr, kr, tile if 0 <= tile < _N_TILES else None))
    return out


def _toeplitz_pair(left_row, right_row):
    lane = lax.broadcasted_iota(jnp.int32, (GRID_W, _PAIR), 1)
    shape = (GRID_W, _PAIR)
    left = pltpu.roll(jnp.broadcast_to(left_row, shape), _PAIR - (WIN_COLS - 1), 1, stride=1, stride_axis=0)
    right = pltpu.roll(jnp.broadcast_to(right_row, shape), GRID_W - (WIN_COLS - 1), 1, stride=1, stride_axis=0)
    return jnp.where(lane < GRID_W, left, right)


def _build_tiles(tiles_ref, rp_ref):
    for h in range(2):
        for t in range(_N_TILES):
            tiles_ref[h, t] = _toeplitz_pair(rp_ref[h, t:t + 1, :], rp_ref[h, t + 1:t + 2, :])


def _block_class(b, nblk, fn, entering=False):
    interior = (b == 1) if entering else jnp.logical_and(b > 0, b < nblk - 1)
    for cls, cond in enumerate((b == 0, interior, b == nblk - 1)):
        pl.when(cond)(functools.partial(fn, cls))


def _attn_geometry(p4, nx):
    rows = p4.shape[0]
    w = p4.shape[1] // 4
    nhp = w // _PAIR
    nblk = nx // _QB
    qspec = lambda col: pl.BlockSpec((_QB, _PAIR), lambda hp, b: (b, col * nhp + hp))
    kspec = lambda col: pl.BlockSpec((rows, _PAIR), lambda hp, b: (0, col * nhp + hp))
    tspec = pl.BlockSpec((2, _RP_ROWS, _PAIR), lambda hp, b: (hp, 0, 0))
    mspec = pl.BlockSpec((None, _QB, _KW), lambda hp, b: (jnp.where(b == 0, 0, jnp.where(b == nblk - 1, 2, 1)), 0, 0))
    lspec = pl.BlockSpec((None, _QB, 2), lambda hp, b: (hp, b, 0))
    ospec = pl.BlockSpec((_QB, _PAIR), lambda hp, b: (b, hp))
    return rows, w, nhp, nblk, qspec, kspec, tspec, mspec, lspec, ospec


def _window_start(b, nx):
    return pl.multiple_of(jnp.clip(b * _QB - PAD_ROWS * GRID_W, 0, nx - _KW), _QB)


def _load_bias(bias_ref, tiles_ref, rp_ref, m_ref, b, nblk):
    pl.when(b == 0)(lambda: _build_tiles(tiles_ref, rp_ref))

    def fill(cls):
        for h in range(2):
            for qr, kr, tile in _bias_pieces(cls):
                rows = slice(qr * GRID_W, (qr + 1) * GRID_W)
                cols = slice(kr * GRID_W, (kr + 2) * GRID_W)
                m = m_ref[rows, cols]
                bias_ref[h, rows, cols] = m if tile is None else tiles_ref[h, tile] + m

    _block_class(b, nblk, fill, entering=True)


def _attn_fwd(p4, rp, mask, nx, name, comm=None):
    rows, w, nhp, nblk, qspec, kspec, tspec, mspec, lspec, ospec = _attn_geometry(p4, nx)
    n_ctx = rows - nx
    n_cin, n_cout = (len(comm.ins), len(comm.outs)) if comm else (0, 0)

    def body(*refs):
        q_ref, k_ref, v_ref, g_ref, rp_ref, m_ref = refs[:6]
        cin = refs[6:6 + n_cin]
        a_ref, o_ref, lse_ref = refs[6 + n_cin:9 + n_cin]
        cout = refs[9 + n_cin:9 + n_cin + n_cout]
        bias_ref, tiles_ref = refs[9 + n_cin + n_cout:11 + n_cin + n_cout]
        sems = refs[11 + n_cin + n_cout:]
        hp, b = pl.program_id(0), pl.program_id(1)
        if comm:
            pl.when(jnp.logical_and(hp == 0, b == 0))(lambda: comm.start(cin, cout, sems))
        start = _window_start(b, nx)
        _load_bias(bias_ref, tiles_ref, rp_ref, m_ref, b, nblk)
        qf = q_ref[...].astype(F32) * HEAD_DIM ** -0.5
        kw = k_ref[pl.ds(start, _KW), :].astype(BF16)
        vw = v_ref[pl.ds(start, _KW), :].astype(BF16)
        kcv = k_ref[pl.ds(nx, n_ctx), :].astype(BF16)
        vcv = v_ref[pl.ds(nx, n_ctx), :].astype(BF16)
        lane = lax.broadcasted_iota(jnp.int32, (1, _PAIR), 1)
        outs, lses = [], []
        for h in range(2):
            mine = (lane >= HEAD_DIM) if h else (lane < HEAD_DIM)
            qm = jnp.where(mine, qf, 0.0).astype(BF16)
            s_loc = lax.dot_general(qm, kw, _DIMS["nt"], preferred_element_type=F32) + bias_ref[h]
            s_ctx = lax.dot_general(qm, kcv, _DIMS["nt"], preferred_element_type=F32)
            mx = jnp.maximum(jnp.max(s_loc, axis=-1, keepdims=True), jnp.max(s_ctx, axis=-1, keepdims=True))
            p_loc = jnp.exp(s_loc - mx)
            p_ctx = jnp.exp(s_ctx - mx)
            den = jnp.sum(p_loc, axis=-1, keepdims=True) + jnp.sum(p_ctx, axis=-1, keepdims=True)
            o = jnp.dot(p_loc.astype(BF16), vw, preferred_element_type=F32)
            o = o + jnp.dot(p_ctx.astype(BF16), vcv, preferred_element_type=F32)
            outs.append(o * (1.0 / den))
            lses.append(mx + jnp.log(den))
        o = jnp.where(lane < HEAD_DIM, outs[0], outs[1])
        o_ref[...] = o.astype(ACT)
        a_ref[...] = (o * _silu(g_ref[...].astype(F32))).astype(BF16)
        col = lax.broadcasted_iota(jnp.int32, (1, 2), 1)
        lse_ref[...] = jnp.where(col == 0, lses[0], lses[1])
        if comm:
            pl.when(jnp.logical_and(hp == nhp - 1, b == nblk - 1))(lambda: comm.finish(cin, cout, sems))

    res = pl.pallas_call(
        body, grid=(nhp, nblk),
        in_specs=[qspec(0), kspec(1), kspec(2), qspec(3), tspec, mspec] + [HBM_SPEC] * n_cin,
        out_specs=[ospec, ospec, lspec] + [HBM_SPEC] * n_cout,
        out_shape=[_sds((nx, w), BF16), _sds((nx, w), ACT), _sds((nhp, nx, 2), F32)] + (comm.outs if comm else []),
        scratch_shapes=[pltpu.VMEM((2, _QB, _KW), F32), pltpu.VMEM((2, _N_TILES, GRID_W, _PAIR), F32)]
        + (comm.sems if comm else []),
        name=name, compiler_params=_cparams("arbitrary", "arbitrary"),
    )(p4, p4, p4, p4, rp, mask, *(comm.ins if comm else []))
    return res[:3], res[3:]


def _fold_tiles(dtiles_ref, drp_ref):
    shape = (GRID_W, _PAIR)
    lane = lax.broadcasted_iota(jnp.int32, shape, 1)
    flip = (lax.broadcasted_iota(jnp.int32, (_PAIR, _PAIR), 0)
            + lax.broadcasted_iota(jnp.int32, (_PAIR, _PAIR), 1) == _PAIR - 1).astype(F32)
    drp_ref[...] = jnp.zeros(drp_ref.shape, F32)
    for h in range(2):
        stack = dtiles_ref[h].reshape(_N_TILES * GRID_W, _PAIR)
        rev = jnp.dot(stack, flip, precision=lax.Precision.HIGHEST, preferred_element_type=F32)
        for t in range(_N_TILES):
            tile = rev[t * GRID_W:(t + 1) * GRID_W, :]
            for side in (0, 1):
                shift = _PAIR - GRID_W * side - (WIN_COLS - 1)
                half = jnp.where((lane < GRID_W) if side else (lane >= GRID_W), tile, 0.0)
                diag = pltpu.roll(half, shift, 1, stride=1, stride_axis=0)
                drp_ref[h, t + side:t + side + 1, :] += jnp.sum(diag, axis=0, keepdims=True)


def _attn_bwd(p4, rp, mask, o, lse, da, nx, name, comm=None):
    rows, w, nhp, nblk, qspec, kspec, tspec, mspec, lspec, ospec = _attn_geometry(p4, nx)
    n_ctx = rows - nx
    n_cin, n_cout = (len(comm.ins), len(comm.outs)) if comm else (0, 0)

    def body(*refs):
        q_ref, k_ref, v_ref, g_ref, rp_ref, m_ref, o_ref, lse_ref, da_ref = refs[:9]
        cin = refs[9:9 + n_cin]
        d4_ref, drp_ref = refs[9 + n_cin:11 + n_cin]
        cout = refs[11 + n_cin:11 + n_cin + n_cout]
        bias_ref, tiles_ref, ds_ref, dtiles_ref, dk_ref, dv_ref = refs[11 + n_cin + n_cout:17 + n_cin + n_cout]
        sems = refs[17 + n_cin + n_cout:]
        hp, b = pl.program_id(0), pl.program_id(1)
        if comm:
            pl.when(jnp.logical_and(hp == 0, b == 0))(lambda: comm.start(cin, cout, sems))
        start = _window_start(b, nx)
        here = pl.multiple_of(b * _QB, _QB)

        @pl.when(b == 0)
        def _():
            dk_ref[...] = jnp.zeros(dk_ref.shape, F32)
            dv_ref[...] = jnp.zeros(dv_ref.shape, F32)
            dtiles_ref[...] = jnp.zeros(dtiles_ref.shape, F32)
            d4_ref[0, pl.ds(nx, n_ctx), :] = jnp.zeros((n_ctx, _PAIR), BF16)
            d4_ref[3, pl.ds(nx, n_ctx), :] = jnp.zeros((n_ctx, _PAIR), BF16)

        _load_bias(bias_ref, tiles_ref, rp_ref, m_ref, b, nblk)
        gv = g_ref[...].astype(F32)
        dav = da_ref[...].astype(F32)
        ov = o_ref[...].astype(F32)
        dov = dav * _silu(gv)
        d4_ref[3, pl.ds(here, _QB), :] = (dav * ov * _dsilu(gv)).astype(BF16)
        qf = q_ref[...].astype(F32) * HEAD_DIM ** -0.5
        kw = k_ref[pl.ds(start, _KW), :].astype(BF16)
        vw = v_ref[pl.ds(start, _KW), :].astype(BF16)
        kcv = k_ref[pl.ds(nx, n_ctx), :].astype(BF16)
        vcv = v_ref[pl.ds(nx, n_ctx), :].astype(BF16)
        lane = lax.broadcasted_iota(jnp.int32, (1, _PAIR), 1)
        dq = jnp.zeros((_QB, _PAIR), F32)
        for h in range(2):
            mine = (lane >= HEAD_DIM) if h else (lane < HEAD_DIM)
            qm = jnp.where(mine, qf, 0.0).astype(BF16)
            dom = jnp.where(mine, dov, 0.0)
            dob = dom.astype(BF16)
            lse = lse_ref[:, h:h + 1]
            s_loc = lax.dot_general(qm, kw, _DIMS["nt"], preferred_element_type=F32)
            p_loc = jnp.exp(s_loc + bias_ref[h] - lse)
            p_ctx = jnp.exp(lax.dot_general(qm, kcv, _DIMS["nt"], preferred_element_type=F32) - lse)
            delta = jnp.sum(dom * ov, axis=-1, keepdims=True)
            ds_loc = p_loc * (lax.dot_general(dob, vw, _DIMS["nt"], preferred_element_type=F32) - delta)
            ds_ctx = p_ctx * (lax.dot_general(dob, vcv, _DIMS["nt"], preferred_element_type=F32) - delta)
            dsb_loc = ds_loc.astype(BF16)
            dsb_ctx = ds_ctx.astype(BF16)
            dq_h = (jnp.dot(dsb_loc, kw, preferred_element_type=F32)
                    + jnp.dot(dsb_ctx, kcv, preferred_element_type=F32))
            dq = dq + jnp.where(mine, dq_h, 0.0)
            dk_ref[pl.ds(start, _KW), :] += lax.dot_general(dsb_loc, qm, _DIMS["tn"], preferred_element_type=F32)
            dv_ref[pl.ds(start, _KW), :] += lax.dot_general(p_loc.astype(BF16), dob, _DIMS["tn"],
                                                            preferred_element_type=F32)
            dk_ref[pl.ds(nx, n_ctx), :] += lax.dot_general(dsb_ctx, qm, _DIMS["tn"], preferred_element_type=F32)
            dv_ref[pl.ds(nx, n_ctx), :] += lax.dot_general(p_ctx.astype(BF16), dob, _DIMS["tn"],
                                                           preferred_element_type=F32)
            ds_ref[h] = ds_loc
        d4_ref[0, pl.ds(here, _QB), :] = (dq * HEAD_DIM ** -0.5).astype(BF16)

        def scatter(cls):
            for h in range(2):
                for qr, kr, tile in _bias_pieces(cls):
                    if tile is not None:
                        dtiles_ref[h, tile] += ds_ref[h, qr * GRID_W:(qr + 1) * GRID_W, kr * GRID_W:(kr + 2) * GRID_W]

        _block_class(b, nblk, scatter)

        @pl.when(b == nblk - 1)
        def _():
            d4_ref[1] = dk_ref[...].astype(BF16)
            d4_ref[2] = dv_ref[...].astype(BF16)
            _fold_tiles(dtiles_ref, drp_ref)

        if comm:
            pl.when(jnp.logical_and(hp == nhp - 1, b == nblk - 1))(lambda: comm.finish(cin, cout, sems))

    tiles = pltpu.VMEM((2, _N_TILES, GRID_W, _PAIR), F32)
    block = pltpu.VMEM((2, _QB, _KW), F32)
    res = pl.pallas_call(
        body, grid=(nhp, nblk),
        in_specs=[qspec(0), kspec(1), kspec(2), qspec(3), tspec, mspec, ospec, lspec, ospec] + [HBM_SPEC] * n_cin,
        out_specs=[pl.BlockSpec((4, rows, _PAIR), lambda hp, b: (0, 0, hp)), tspec] + [HBM_SPEC] * n_cout,
        out_shape=[_sds((4, rows, w), BF16), _sds(rp.shape, F32)] + (comm.outs if comm else []),
        scratch_shapes=[block, tiles, block, tiles, pltpu.VMEM((rows, _PAIR), F32), pltpu.VMEM((rows, _PAIR), F32)]
        + (comm.sems if comm else []),
        name=name, compiler_params=_cparams("arbitrary", "arbitrary"),
    )(p4, p4, p4, p4, rp, mask, o, lse, da, *(comm.ins if comm else []))
    return res[:2], res[2:]


def _w_out_loss(a, w_out, xres, gate, g, target, name):
    m, k = a.shape
    d = w_out.shape[1]
    assert gate.shape[0] == 1
    tm = _row_tile(m)
    nblk = m // tm

    def body(a_ref, w_ref, x_ref, gt_ref, g_ref, t_ref, yx_ref, loss_ref, dx_ref, dg_ref, acc_ref):
        i = pl.program_id(0)
        yx = jnp.dot(a_ref[...], w_ref[...], preferred_element_type=F32)
        yx_ref[...] = yx.astype(ACT)
        xv = x_ref[...] + gt_ref[0] * yx
        gv = g_ref[...]
        r = lax.rsqrt(jnp.mean(xv * xv, axis=-1, keepdims=True) + EPS)
        xn = xv * r
        err = xn * gv - t_ref[...]
        dy = err * (1.0 / d)
        dxn = dy * gv
        dx_ref[...] = r * (dxn - xn * jnp.mean(dxn * xn, axis=-1, keepdims=True))
        s_g = jnp.sum(dy * xn, axis=0, keepdims=True)
        s_l = jnp.sum(jnp.mean(err * err, axis=-1, keepdims=True), axis=0, keepdims=True)

        @pl.when(i == 0)
        def _():
            dg_ref[...] = s_g
            acc_ref[...] = s_l

        @pl.when(i > 0)
        def _():
            dg_ref[...] += s_g
            acc_ref[...] += s_l

        @pl.when(i == nblk - 1)
        def _():
            loss_ref[...] = jnp.broadcast_to(0.5 * acc_ref[...], loss_ref.shape)

    row = pl.BlockSpec((tm, d), lambda i: (i, 0))
    vec = pl.BlockSpec((1, d), lambda i: (0, 0))
    return pl.pallas_call(
        body, grid=(nblk,),
        in_specs=[pl.BlockSpec((tm, k), lambda i: (i, 0)), pl.BlockSpec((k, d), lambda i: (0, 0)), row,
                  pl.BlockSpec((1, 1, d), lambda i: (0, 0, 0)), vec, row],
        out_specs=[row, pl.BlockSpec((1, 128), lambda i: (0, 0)), row, vec],
        out_shape=[_sds((m, d), ACT), _sds((1, 128), F32), _sds((m, d), F32), _sds((1, d), F32)],
        scratch_shapes=[pltpu.VMEM((1, 1), F32)], name=name, compiler_params=_cparams("arbitrary"),
    )(a, w_out, xres, gate, g, target)


def _as2d(a):
    if a.ndim == 1:
        return a.reshape(-1, 128) if a.shape[0] % 128 == 0 else a.reshape(1, -1)
    return a.reshape(-1, a.shape[-1])


def _adamw(w, g, m, v, name):
    shape = w.shape
    w2, g2, m2, v2 = (_as2d(t) for t in (w, g.reshape(shape), m, v))
    rows, cols = w2.shape
    tr = 512 if rows % 512 == 0 else rows
    c1 = 1.0 - ADAM_B1 ** ADAM_STEP
    c2 = 1.0 - ADAM_B2 ** ADAM_STEP

    def body(w_ref, g_ref, m_ref, v_ref, d_ref, nm_ref, nv_ref):
        gv = g_ref[...]
        nm = ADAM_B1 * m_ref[...] + (1.0 - ADAM_B1) * gv
        nv = ADAM_B2 * v_ref[...] + (1.0 - ADAM_B2) * (gv * gv)
        nm_ref[...] = nm
        nv_ref[...] = nv
        d_ref[...] = -ADAM_LR * ((nm / c1) / (jnp.sqrt(nv / c2) + ADAM_EPS) + ADAM_WD * w_ref[...])

    blk = pl.BlockSpec((tr, cols), lambda i: (i, 0))
    outs = _call(body, (w2, g2, m2, v2), grid=(rows // tr,), in_specs=[blk] * 4, out_specs=[blk] * 3,
                 out_shape=[_sds((rows, cols), F32)] * 3, name=name)
    return tuple(t.reshape(shape) for t in outs)


def _sum_lead(x, name, out_dtype=F32):
    n, rows, cols = x.shape
    tr = 512 if rows % 512 == 0 else rows

    def body(x_ref, o_ref):
        acc = x_ref[0].astype(F32)
        for k in range(1, n):
            acc = acc + x_ref[k].astype(F32)
        o_ref[...] = acc.astype(out_dtype)

    return pl.pallas_call(
        body, grid=(rows // tr,), in_specs=[pl.BlockSpec((n, tr, cols), lambda i: (0, i, 0))],
        out_specs=pl.BlockSpec((tr, cols), lambda i: (i, 0)), out_shape=_sds((rows, cols), out_dtype),
        name=name, compiler_params=_cparams("parallel"),
    )(x)


def _seg_vecs(mod_l, which, nseg):
    return mod_l[:nseg, which][:, None, :]


def _norm_grads(dshift, dgeff, dgate, g, scale):
    nseg, _, d = dshift.shape
    dmod = jnp.stack([dshift[:, 0], dgeff[:, 0] * g, dgate[:, 0]], axis=1)
    if nseg == 1:
        dmod = jnp.concatenate([dmod, jnp.zeros((1, 3, d), F32)], axis=0)
    dg = jnp.sum(dgeff[:, 0] * (1.0 + scale[:, 0]), axis=0)
    return dmod, dg


def _pool_layer(xin, g, mod_l, w_in, w_grp, w_out, pscale, nx, tag, ctx=None, head=None):
    nseg = 1 if ctx is None else 2
    shift, scale, gate = (_seg_vecs(mod_l, k, nseg) for k in range(3))
    *joined, h, r, uv = _norm_w_in(xin, g, scale, shift, w_in, nx, f"w_in_fwd_{tag}", ctx)
    if joined:
        xin, = joined
    z, mixed, a = _pool_grp_fwd(uv, w_grp, pscale, nx, f"pool_fwd_{tag}")
    if head is None:
        yx, xout = _w_out_resid(a, w_out, xin, gate, nx, f"w_out_fwd_{tag}")
    else:
        yx, *xout = _w_out_loss(a, w_out, xin, gate, *head, f"w_out_loss_{tag}")

    def backward(dxo, token=None):
        gate_b = gate if token is None else gate + token[0, 0]
        dyx, da, dgate = _gate_w_out_bwd(dxo, yx, gate_b, w_out, nx, f"w_out_bwd_{tag}")
        gw_out = _mm_tn(a, dyx, f"w_out_grad_{tag}", BF16)
        dm, duv, dscale = _pool_grp_bwd(da, mixed, uv, pscale, w_grp, nx, f"pool_bwd_{tag}")
        gw_grp = _grp_wgrad(z, dm, w_grp.shape[0], f"grp_grad_{tag}", BF16)
        gw_in = _mm_tn_parts(h, duv, f"w_in_grad_{tag}", BF16)
        dx, dshift, dgeff = _w_in_bwd_norm(duv, w_in, xin, r, g, scale, dxo, nx, f"w_in_bwd_{tag}",
                                           dx_rows=None if ctx is None else nx)
        dmod, dg = _norm_grads(dshift, dgeff, dgate, g[0], scale)
        return dx, dmod, dg, dict(w_in=gw_in, w_grp=gw_grp, w_out=gw_out, scale=dscale)

    return xout, backward


def _na_layer(xc, g, mod_l, w_in, rpb, w_out, nx, mask, comm=None):
    nh, n_dr, n_dc = rpb.shape
    shift, scale = _seg_vecs(mod_l, 0, 2), _seg_vecs(mod_l, 1, 2)
    gate = _seg_vecs(mod_l, 2, 1)
    h, r, p4 = _norm_w_in(xc, g, scale, shift, w_in, nx, "w_in_fwd_na")
    rp = jnp.pad(rpb, ((0, 0), (1, _RP_ROWS - 1 - n_dr), (0, _PAIR - n_dc)))
    (a, o, lse), carried = _attn_fwd(p4, rp, mask, nx, "attn_fwd", comm)
    yx, xout = _w_out_resid(a, w_out, xc, gate, nx, "w_out_fwd_na")

    def backward(dxo, comm=None):
        dyx, da, dgate = _gate_w_out_bwd(dxo, yx, gate, w_out, nx, "w_out_bwd_na")
        gw_out = _mm_tn(a, dyx, "w_out_grad_na", BF16)
        (d4, drp), carried_bwd = _attn_bwd(p4, rp, mask, o, lse, da, nx, "attn_bwd", comm)
        gw_in = _mm_tn_parts(h, d4, "w_in_grad_na", BF16)
        dx, dshift, dgeff = _w_in_bwd_norm(d4, w_in, xc, r, g, scale, dxo, nx, "w_in_bwd_na")
        dgate2 = jnp.concatenate([dgate, jnp.zeros_like(dgate)], axis=0)
        dmod, dg = _norm_grads(dshift, dgeff, dgate2, g[0], scale)
        drpb = drp[:, 1:1 + n_dr, ::-1][:, :, :n_dc]
        return dx, dmod, dg, dict(w_in=gw_in, w_out=gw_out, rpb=drpb), carried_bwd

    return xout, backward, carried


def _conv_layer(xin, g, mod_l, w_in, dw, db, w_out):
    shift, scale, gate = (_seg_vecs(mod_l, k, 1) for k in range(3))
    nx = xin.shape[0]
    h, r, p4 = _norm_w_in(xin, g, scale, shift, w_in, nx, "w_in_fwd_conv")
    a = _conv_fwd(p4, dw, db, "conv_fwd")
    yx, xout = _w_out_resid(a, w_out, xin, gate, nx, "w_out_fwd_conv")

    def backward(dxo):
        dyx, da, dgate = _gate_w_out_bwd(dxo, yx, gate, w_out, nx, "w_out_bwd_conv")
        gw_out = _mm_tn(a, dyx, "w_out_grad_conv", BF16)
        d4, ddw, ddb = _conv_bwd(da, p4, dw, db, "conv_bwd")
        gw_in = _mm_tn_parts(h, d4, "w_in_grad_conv", BF16)
        dx, dshift, dgeff = _w_in_bwd_norm(d4, w_in, xin, r, g, scale, dxo, nx, "w_in_bwd_conv")
        dmod, dg = _norm_grads(dshift, dgeff, dgate, g[0], scale)
        return dx, dmod, dg, dict(w_in=gw_in, w_out=gw_out, dw=ddw, db=ddb)

    return xout, backward


def _example_step(x, ctx, target, mod, norm_g, final_g, wts, hooks=None):
    hooks = hooks or {}
    na_weights, late_comm, late_weights = (hooks.get(k) for k in ("na_weights", "late_comm", "late_weights"))
    nx = x.shape[0]
    consts = _attn_mask()
    g_rows = [norm_g[i:i + 1] for i in range(4)]
    xc1, bwd0 = _pool_layer(x, g_rows[0], mod[0], wts["pool_w_in"][0], wts["pool_w_grp"][0],
                            wts["pool_w_out"][0], wts["pool_scale"][0:1], nx, "p0", ctx=ctx)
    if na_weights is not None:
        wts = {**wts, **na_weights(xc1)}
    x2, bwd1, carried = _na_layer(xc1, g_rows[1], mod[1], wts["na_w_in"], wts["na_rpb"], wts["na_w_out"], nx, consts,
                                  late_comm)
    if late_weights is not None:
        wts = {**wts, **late_weights(carried)}
    x3, bwd2 = _conv_layer(x2, g_rows[2], mod[2], wts["conv_w_in"], wts["conv_dw"], wts["conv_db"], wts["conv_w_out"])
    (loss, dx4, dfinal_g), bwd3 = _pool_layer(x3, g_rows[3], mod[3], wts["pool_w_in"][1], wts["pool_w_grp"][1],
                                              wts["pool_w_out"][1], wts["pool_scale"][1:2], nx, "p3",
                                              head=(final_g, target))
    call = lambda k, *args: hooks[k](*args) if k in hooks else None
    dx3, dmod3, dg3, gr3 = bwd3(dx4)
    dx2, dmod2, dg2, gr2 = bwd2(dx3)
    dxc1, dmod1, dg1, gr1, carried_bwd = bwd1(dx2, call("grad_comm", gr3, gr2))
    dx0, dmod0, dg0, gr0 = bwd0(dxc1, call("na_grads_start", gr1))
    return dict(
        loss=loss, grad_x=dx0, dmod=jnp.stack([dmod0, dmod1, dmod2, dmod3]),
        dnorm_g=jnp.stack([dg0, dg1, dg2, dg3]), dfinal_g=dfinal_g, layers=(gr0, gr1, gr2, gr3), carried=carried_bwd)


_AXES = ("x", "y", "c")
_CHIP_FLIPS = ((1, 0), (0, 1), (1, 1))


def _position():
    return tuple(lax.axis_index(a) for a in _AXES)


def _flipped(pos, flip):
    return tuple(1 - p if f else p for p, f in zip(pos, flip))


def _join_comms(comms):
    n_in = [len(c.ins) for c in comms]
    n_out = [len(c.outs) for c in comms]
    n_sem = [len(c.sems) for c in comms]

    def parts(ins, outs, sems):
        for k in range(len(comms)):
            a, b, s = sum(n_in[:k]), sum(n_out[:k]), sum(n_sem[:k])
            yield comms[k], (ins[a:a + n_in[k]], outs[b:b + n_out[k]], sems[s:s + n_sem[k]])

    def start(ins, outs, sems):
        for c, part in parts(ins, outs, sems):
            c.start(*part)

    def finish(ins, outs, sems):
        for c, part in parts(ins, outs, sems):
            c.finish(*part)

    joint = _Comm([a for c in comms for a in c.ins], [o for c in comms for o in c.outs],
                  [s for c in comms for s in c.sems], start, finish)
    return joint, lambda res: [list(res[sum(n_out[:k]):sum(n_out[:k + 1])]) for k in range(len(comms))]


def _run_comms(comms, name):
    joint, split = _join_comms(comms)

    def body(*refs):
        n_in, n_out = len(joint.ins), len(joint.outs)
        joint.start(refs[:n_in], refs[n_in:n_in + n_out], refs[n_in + n_out:])
        joint.finish(refs[:n_in], refs[n_in:n_in + n_out], refs[n_in + n_out:])

    res = pl.pallas_call(
        body, in_specs=[HBM_SPEC] * len(joint.ins), out_specs=[HBM_SPEC] * len(joint.outs), out_shape=joint.outs,
        scratch_shapes=joint.sems, name=name,
    )(*joint.ins)
    return split(res)


def _all_gather_comm(v, axes):
    flips = [f for f in np.ndindex(2, 2, 2) if any(f) and all(a in axes or not b for a, b in zip(_AXES, f))]
    n = len(flips) + 1

    def copies(ins, outs, sems):
        (v_ref,), (o_ref,), (send_sems, recv_sems, local_sem) = ins, outs, sems
        pos = _position()
        slot = 0
        for a, p in zip(_AXES, pos):
            if a in axes:
                slot = 2 * slot + p
        local = pltpu.make_async_copy(v_ref, o_ref.at[slot], local_sem)
        remote = [pltpu.make_async_remote_copy(v_ref, o_ref.at[slot], send_sems.at[k], recv_sems.at[k],
                                               device_id=_flipped(pos, flip), device_id_type=MESH)
                  for k, flip in enumerate(flips)]
        return [local] + remote

    def start(ins, outs, sems):
        for cp in copies(ins, outs, sems):
            cp.start()

    def finish(ins, outs, sems):
        for cp in copies(ins, outs, sems):
            cp.wait()

    sems = [pltpu.SemaphoreType.DMA((n - 1,)), pltpu.SemaphoreType.DMA((n - 1,)), pltpu.SemaphoreType.DMA(())]
    return _Comm([v], [_sds((n,) + v.shape, v.dtype)], sems, start, finish)


def _all_gather_two_level_comm(v):
    def copies(ins, outs, sems, onward):
        (v_ref,), (o_ref,), (send_sems, recv_sems, local_sem) = ins, outs, sems
        x, y, c = _position()
        sibling = (x, y, 1 - c)
        slot = lambda px, py, pc: o_ref.at[4 * px + 2 * py + pc]
        own = pltpu.make_async_copy(v_ref, slot(x, y, c), local_sem)
        first = [pltpu.make_async_remote_copy(v_ref, slot(x, y, c), send_sems.at[0], recv_sems.at[0],
                                              device_id=sibling, device_id_type=MESH)]
        fwd = []
        for k, flip in enumerate(_CHIP_FLIPS):
            px, py = _flipped((x, y), flip)
            first.append(pltpu.make_async_remote_copy(v_ref, slot(x, y, c), send_sems.at[1 + k], recv_sems.at[1 + k],
                                                      device_id=(px, py, c), device_id_type=MESH))
            if onward:
                fwd.append(pltpu.make_async_remote_copy(slot(px, py, c), slot(px, py, c), send_sems.at[4 + k],
                                                        recv_sems.at[4 + k], device_id=sibling, device_id_type=MESH))
        return own, first, fwd

    def start(ins, outs, sems):
        own, first, _ = copies(ins, outs, sems, False)
        for cp in [own] + first:
            cp.start()

    def finish(ins, outs, sems):
        own, first, fwd = copies(ins, outs, sems, True)
        for arrived, onward in zip(first[1:], fwd):
            arrived.wait_recv()
            onward.start()
        first[0].wait_recv()
        for cp in fwd:
            cp.wait_recv()
        for cp in first + fwd:
            cp.wait_send()
        own.wait()

    sems = [pltpu.SemaphoreType.DMA((7,)), pltpu.SemaphoreType.DMA((7,)), pltpu.SemaphoreType.DMA(())]
    return _Comm([v], [_sds((8,) + v.shape, v.dtype)], sems, start, finish)


def _all_gather(v, axes, name):
    return _run_comms([_all_gather_comm(v, axes)], name)[0][0]


class _Item:
    def __init__(self, key, layer, shape, shard_axis, half_axis):
        self.key, self.layer, self.shape = key, layer, tuple(shape)
        self.shard_axis, self.half_axis = shard_axis, half_axis
        self.shard = shape[shard_axis] // 4
        self.half = shape[half_axis] // 2

    def sized(self, shard=False, half=False):
        s = list(self.shape)
        if shard:
            s[self.shard_axis] = self.shard
        if half:
            s[self.half_axis] = self.half
        return tuple(s)

    def window(self, ref, chip=None, half=None):
        idx = [slice(None)] * len(self.shape)
        if chip is not None:
            idx[self.shard_axis] = pl.ds(chip * self.shard, self.shard)
        if half is not None:
            idx[self.half_axis] = pl.ds(half * self.half, self.half)
        return ref.at[tuple(idx)]


def _items(d, w):
    out = []
    for j in range(2):
        out += [_Item("pool_w_in", j, (d, 2 * w), 1, 0), _Item("pool_w_grp", j, (4, w // 4, w // 4), 1, 0),
                _Item("pool_w_out", j, (w, d), 0, 1)]
    out += [_Item("na_w_in", 0, (d, 4 * w), 1, 0), _Item("na_w_out", 0, (w, d), 0, 1),
            _Item("conv_w_in", 0, (d, 4 * w), 1, 0), _Item("conv_w_out", 0, (w, d), 0, 1)]
    return out


def _gather_comm(shards, items):
    n = len(items)

    def copies(src, dst, sems, onward):
        send_a, recv_a, send_b, recv_b, send_c, recv_c = sems
        x, y, c = _position()
        chip = 2 * x + y
        sibling = (x, y, 1 - c)
        own, out, fwd, fwd_in = [], [], [], []
        for i, it in enumerate(items):
            own.append(pltpu.make_async_remote_copy(src[i], it.window(dst[i], chip=chip), send_c.at[i], recv_c.at[i],
                                                    device_id=sibling, device_id_type=MESH))
            for k, flip in enumerate(_CHIP_FLIPS):
                px, py = _flipped((x, y), flip)
                s = 3 * i + k
                out.append(pltpu.make_async_remote_copy(
                    it.window(src[i], half=c), it.window(dst[i], chip=chip, half=c), send_a.at[s], recv_a.at[s],
                    device_id=(px, py, c), device_id_type=MESH))
                if onward:
                    got = it.window(dst[i], chip=2 * px + py, half=c)
                    fwd.append(pltpu.make_async_remote_copy(got, got, send_b.at[s], recv_b.at[s],
                                                            device_id=sibling, device_id_type=MESH))
                    other = it.window(dst[i], chip=2 * px + py, half=1 - c)
                    fwd_in.append(pltpu.make_async_remote_copy(other, other, send_b.at[s], recv_b.at[s],
                                                               device_id=sibling, device_id_type=MESH))
        return own, out, fwd, fwd_in

    def start(src, dst, sems):
        own, out, _, _ = copies(src, dst, sems, False)
        for cp in own + out:
            cp.start()

    def finish(src, dst, sems):
        own, out, fwd, fwd_in = copies(src, dst, sems, True)
        for arrived, onward in zip(out, fwd):
            arrived.wait_recv()
            onward.start()
        for cp in fwd_in:
            cp.wait_recv()
        for cp in out + fwd:
            cp.wait_send()
        for cp in own:
            cp.wait()

    sems = [pltpu.SemaphoreType.DMA((3 * n,)) for _ in range(4)] + [pltpu.SemaphoreType.DMA((n,)) for _ in range(2)]
    return _Comm(shards, [_sds(it.shape, BF16) for it in items], sems, start, finish)


def _pair_swap_copies(windows):
    def copies(src, got, sems):
        send_sems, recv_sems = sems
        x, y, c = _position()
        return [pltpu.make_async_remote_copy(windows[i](src[i], 1 - c), got[i], send_sems.at[i], recv_sems.at[i],
                                             device_id=(x, y, 1 - c), device_id_type=MESH)
                for i in range(len(windows))]

    return copies


def _pair_swap_comm(arrays, windows, out_shapes):
    n = len(arrays)
    copies = _pair_swap_copies(windows)

    def start(src, got, sems):
        for cp in copies(src, got, sems):
            cp.start()

    def finish(src, got, sems):
        for cp in copies(src, got, sems):
            cp.wait()

    return _Comm(arrays, out_shapes, [pltpu.SemaphoreType.DMA((n,)), pltpu.SemaphoreType.DMA((n,))], start, finish)


def _pair_swap(arrays, windows, out_shapes, name):
    return _run_comms([_pair_swap_comm(arrays, windows, out_shapes)], name)[0]


def _chip_exchange_copies(items):
    def copies(src, dst, sems):
        send_sems, recv_sems = sems
        x, y, c = _position()
        out = []
        for i, it in enumerate(items):
            for k, flip in enumerate(_CHIP_FLIPS):
                px, py = _flipped((x, y), flip)
                out.append(pltpu.make_async_remote_copy(
                    it.window(src[i], chip=2 * px + py), dst[i].at[k], send_sems.at[3 * i + k],
                    recv_sems.at[3 * i + k], device_id=(px, py, c), device_id_type=MESH))
        return out

    return copies


_SEM_SPEC = pl.BlockSpec(memory_space=pltpu.SEMAPHORE)
_DATAFLOW = pltpu.SideEffectType.DATAFLOW_SIDE_EFFECTING


def _split_start(copies, srcs, zones, n_copies, name):
    n, nz = len(srcs), len(zones)

    def body(*refs):
        src, land = refs[:n], refs[n:n + nz]
        send_sems, recv_sems = refs[n + nz:n + nz + 2]
        token = refs[-1]
        for cp in copies(src, land, (send_sems, recv_sems)):
            cp.start()
        token[...] = jnp.zeros(token.shape, F32)

    hbm = lambda t: pltpu.HBM(t.shape, t.dtype)
    res = pl.pallas_call(
        body, name=name,
        out_shape=(pltpu.SemaphoreType.DMA((n_copies,)), pltpu.SemaphoreType.DMA((n_copies,)),
                   *[hbm(t) for t in list(srcs) + list(zones)], _sds((8, 128), F32)),
        in_specs=[HBM_SPEC] * (n + nz),
        out_specs=(_SEM_SPEC, _SEM_SPEC, *[HBM_SPEC] * (n + nz), pl.BlockSpec(memory_space=pltpu.VMEM)),
        input_output_aliases={i: 2 + i for i in range(n + nz)},
        compiler_params=pltpu.CompilerParams(has_side_effects=_DATAFLOW),
    )(*[pltpu.with_memory_space_constraint(t, pltpu.HBM) for t in list(srcs) + list(zones)])
    return (res[0], res[1], list(res[2:2 + n]), list(res[2 + n:2 + n + nz])), res[-1]


def _split_wait(copies, handle, after, name):
    send_sems, recv_sems, srcs, zones = handle
    n, nz = len(srcs), len(zones)

    def body(*refs):
        src, land = refs[:n], refs[n:n + nz]
        send, recv = refs[n + nz:n + nz + 2]
        for cp in copies(src, land, (send, recv)):
            cp.wait_send()
            cp.wait_recv()

    hbm = lambda t: pltpu.HBM(t.shape, t.dtype)
    res = pl.pallas_call(
        body, name=name, out_shape=tuple(hbm(t) for t in list(srcs) + list(zones)),
        in_specs=[HBM_SPEC] * (n + nz) + [_SEM_SPEC, _SEM_SPEC, pl.BlockSpec(memory_space=pl.ANY)],
        out_specs=tuple([HBM_SPEC] * (n + nz)), input_output_aliases={i: i for i in range(n + nz)},
        compiler_params=pltpu.CompilerParams(has_side_effects=_DATAFLOW),
    )(*srcs, *zones, send_sems, recv_sems, after)
    return list(res[:n]), list(res[n:])


def _gather_ici_copies(items):
    def copies(src, dst, sems):
        send_sems, recv_sems = sems
        x, y, c = _position()
        chip = 2 * x + y
        out = []
        for i, it in enumerate(items):
            for k, flip in enumerate(_CHIP_FLIPS):
                px, py = _flipped((x, y), flip)
                out.append(pltpu.make_async_remote_copy(
                    it.window(src[i], half=c), it.window(dst[i], chip=chip, half=c), send_sems.at[3 * i + k],
                    recv_sems.at[3 * i + k], device_id=(px, py, c), device_id_type=MESH))
        return out

    return copies


def _gather_pair_finish(shards, mats, items, name):
    n = len(items)

    def body(*refs):
        src, dst = refs[:n], refs[2 * n:3 * n]
        send_own, recv_own, send_fwd, recv_fwd = refs[3 * n:]
        x, y, c = _position()
        chip = 2 * x + y
        sibling = (x, y, 1 - c)
        copies = []
        for i, it in enumerate(items):
            copies.append(pltpu.make_async_remote_copy(src[i], it.window(dst[i], chip=chip), send_own.at[i],
                                                       recv_own.at[i], device_id=sibling, device_id_type=MESH))
            for k, flip in enumerate(_CHIP_FLIPS):
                px, py = _flipped((x, y), flip)
                got = it.window(dst[i], chip=2 * px + py, half=c)
                copies.append(pltpu.make_async_remote_copy(got, got, send_fwd.at[3 * i + k], recv_fwd.at[3 * i + k],
                                                           device_id=sibling, device_id_type=MESH))
        for cp in copies:
            cp.start()
        for cp in copies:
            cp.wait()

    return pl.pallas_call(
        body, in_specs=[HBM_SPEC] * (2 * n), out_specs=[HBM_SPEC] * n, out_shape=[_sds(it.shape, BF16) for it in items],
        input_output_aliases={n + i: i for i in range(n)},
        scratch_shapes=[pltpu.SemaphoreType.DMA((n,)), pltpu.SemaphoreType.DMA((n,)),
                        pltpu.SemaphoreType.DMA((3 * n,)), pltpu.SemaphoreType.DMA((3 * n,))], name=name,
    )(*shards, *mats)


def _chip_exchange_comm(partials, items):
    n = len(items)
    copies = _chip_exchange_copies(items)

    def start(src, dst, sems):
        for cp in copies(src, dst, sems):
            cp.start()

    def finish(src, dst, sems):
        for cp in copies(src, dst, sems):
            cp.wait()

    return _Comm(partials, [_sds((3,) + it.sized(shard=True, half=True), BF16) for it in items],
                 [pltpu.SemaphoreType.DMA((3 * n,)), pltpu.SemaphoreType.DMA((3 * n,))], start, finish)


_SUM_STEPS = 2


def _pair_sums(gs, gots, its, pos, name):
    n = len(its)
    nb = _SUM_STEPS
    g2 = [g.reshape(-1, g.shape[-1]) for g in gs]
    got2 = [t.reshape(-1, t.shape[-1]) for t in gots]

    def body(pos_ref, *refs):
        for g_ref, got_ref, o_ref in zip(refs[:n], refs[n:2 * n], refs[2 * n:]):
            o_ref[...] = (g_ref[...].astype(F32) + got_ref[...].astype(F32)).astype(BF16)

    g_specs, got_specs = [], []
    for it, t in zip(its, got2):
        rows, cols = t.shape
        blk = (rows // nb, cols)
        g_map = (lambda i, pos: (pos[1] * nb + i, 0)) if it.half_axis == 0 else (lambda i, pos: (i, pos[1]))
        g_specs.append(pl.BlockSpec(blk, g_map))
        got_specs.append(pl.BlockSpec(blk, lambda i, pos: (i, 0)))
    outs = pl.pallas_call(
        body, grid_spec=pltpu.PrefetchScalarGridSpec(
            num_scalar_prefetch=1, grid=(nb,), in_specs=g_specs + got_specs, out_specs=got_specs),
        out_shape=[_sds(t.shape, BF16) for t in got2], name=name, compiler_params=_cparams("parallel"),
    )(pos, *g2, *got2)
    return [o.reshape(t.shape) for o, t in zip(outs, gots)]


_FLIP_SLOT = {2: 0, 1: 1, 3: 2}


def _chip_sums(pairs, slots, its, pos, name):
    n = len(its)
    nb = _SUM_STEPS

    def body(pos_ref, *refs):
        chip = pos_ref[0]
        for own in range(4):
            @pl.when(chip == own)
            def _():
                for p_ref, s_ref, o_ref in zip(refs[:n], refs[n:2 * n], refs[2 * n:]):
                    acc = None
                    for k in range(4):
                        v = (p_ref[...] if k == own else s_ref[_FLIP_SLOT[own ^ k]]).astype(F32)
                        acc = v if acc is None else acc + v
                    o_ref[...] = acc

    p_specs, s_specs, o_specs, shapes = [], [], [], []
    for it in its:
        shape = it.sized(shard=True, half=True)
        blk = (shape[0] // nb,) + shape[1:]
        rest = (0,) * (len(shape) - 1)

        def p_map(i, pos, it=it, nd=len(shape)):
            lead = i + (pos[0] * nb if it.shard_axis == 0 else 0)
            return (lead,) + tuple(pos[0] if ax == it.shard_axis else 0 for ax in range(1, nd))

        p_specs.append(pl.BlockSpec(blk, p_map))
        s_specs.append(pl.BlockSpec((3,) + blk, lambda i, pos, rest=rest: (0, i) + rest))
        o_specs.append(pl.BlockSpec(blk, lambda i, pos, rest=rest: (i,) + rest))
        shapes.append(_sds(shape, F32))
    return pl.pallas_call(
        body, grid_spec=pltpu.PrefetchScalarGridSpec(
            num_scalar_prefetch=1, grid=(nb,), in_specs=p_specs + s_specs, out_specs=o_specs),
        out_shape=shapes, name=name, compiler_params=_cparams("parallel"),
    )(pos, *pairs, *slots)


_GRAD_KEYS = ("pool_w_in", "pool_w_grp", "pool_w_out", "na_w_in", "na_w_out", "conv_w_in", "conv_w_out")


def _adamw_matrix(w, m, v, owns, others, it, pos, name):
    nl = w.shape[0]
    rows_split = it.half_axis == 0
    r, cdim = int(np.prod(w.shape[1:-1])), w.shape[-1]
    hr, hc = (r // 2, cdim) if rows_split else (r, cdim // 2)
    br = min(hr, 256)
    nb = hr // br
    c1 = 1.0 - ADAM_B1 ** ADAM_STEP
    c2 = 1.0 - ADAM_B2 ** ADAM_STEP

    def body(pos_ref, w_ref, m_ref, v_ref, *rest):
        own_refs, other_refs = rest[:nl], rest[nl:2 * nl]
        g_ref, d_ref, nm_ref, nv_ref = rest[2 * nl:]
        j, h = pl.program_id(0), pl.program_id(1)
        own, other = own_refs[0][...], other_refs[0][...]
        for q in range(1, nl):
            own = jnp.where(j == q, own_refs[q][...], own)
            other = jnp.where(j == q, other_refs[q][...], other)
        gv = jnp.where(h == pos_ref[1], own, other)
        nm = ADAM_B1 * m_ref[...] + (1.0 - ADAM_B1) * gv
        nv = ADAM_B2 * v_ref[...] + (1.0 - ADAM_B2) * (gv * gv)
        g_ref[...] = gv
        nm_ref[...] = nm
        nv_ref[...] = nv
        d_ref[...] = -ADAM_LR * ((nm / c1) / (jnp.sqrt(nv / c2) + ADAM_EPS) + ADAM_WD * w_ref[...])

    if rows_split:
        full = pl.BlockSpec((None, br, hc), lambda j, h, i, pos: (j, h * nb + i, 0))
    else:
        full = pl.BlockSpec((None, br, hc), lambda j, h, i, pos: (j, i, h))
    half = pl.BlockSpec((br, hc), lambda j, h, i, pos: (i, 0))
    flat = lambda t: t.reshape(nl, r, cdim)
    outs = pl.pallas_call(
        body, grid_spec=pltpu.PrefetchScalarGridSpec(
            num_scalar_prefetch=1, grid=(nl, 2, nb), in_specs=[full] * 3 + [half] * (2 * nl), out_specs=[full] * 4),
        out_shape=[_sds((nl, r, cdim), F32)] * 4, name=name,
        compiler_params=_cparams("parallel", "parallel", "parallel"),
    )(pos, flat(w), flat(m), flat(v), *[t.reshape(hr, hc) for t in list(owns) + list(others)])
    return tuple(t.reshape(w.shape) for t in outs)


_WEIGHTS = ("c_ctx", "norm_g", "ada_w", "ada_b", "pool_w_in", "pool_w_grp", "pool_scale", "pool_w_out", "na_w_in",
            "na_rpb", "na_w_out", "conv_w_in", "conv_dw", "conv_db", "conv_w_out", "final_g")
_COND_ROWS = 16


def _modulations(cond, ada_w, ada_b_cols):
    nl, d, n = ada_w.shape
    return _matmul(
        cond, ada_w, mode="nn", grid=(nl, 1), a_silu=True, epilogue="bias",
        a_spec=pl.BlockSpec((_COND_ROWS, d), lambda i, j: (0, 0)), b_spec=pl.BlockSpec((None, d, n), lambda i, j: (i, 0, 0)),
        extra=(ada_b_cols,), extra_specs=(pl.BlockSpec((None, 1, n), lambda i, j: (i, 0, 0)),),
        out_shapes=[_sds((nl, _COND_ROWS, n), F32)], out_specs=[pl.BlockSpec((None, _COND_ROWS, n), lambda i, j: (i, 0, 0))],
        name="modulations")[0]


def _ada_w_step(cond, dm_cols, w, m, v):
    nl, d, n = w.shape
    tr = d // 2
    c1 = 1.0 - ADAM_B1 ** ADAM_STEP
    c2 = 1.0 - ADAM_B2 ** ADAM_STEP

    def body(c_ref, dm_ref, w_ref, m_ref, v_ref, g_ref, d_ref, nm_ref, nv_ref):
        gv = lax.dot_general(_silu(c_ref[...]).astype(BF16), dm_ref[...].astype(BF16), _DIMS["tn"],
                             preferred_element_type=F32)
        nm = ADAM_B1 * m_ref[...] + (1.0 - ADAM_B1) * gv
        nv = ADAM_B2 * v_ref[...] + (1.0 - ADAM_B2) * (gv * gv)
        g_ref[...] = gv
        nm_ref[...] = nm
        nv_ref[...] = nv
        d_ref[...] = -ADAM_LR * ((nm / c1) / (jnp.sqrt(nv / c2) + ADAM_EPS) + ADAM_WD * w_ref[...])

    blk = pl.BlockSpec((None, tr, n), lambda l, i: (l, i, 0))
    return _call(
        body, (cond, dm_cols, w, m, v), grid=(nl, d // tr),
        in_specs=[pl.BlockSpec((_COND_ROWS, tr), lambda l, i: (0, i)),
                  pl.BlockSpec((None, _COND_ROWS, n), lambda l, i: (l, 0, 0)), blk, blk, blk],
        out_specs=[blk] * 4, out_shape=[_sds(w.shape, F32)] * 4, name="adamw_ada_w")


def _cond_grad(dm_cols, ada_w):
    nl, d, n = ada_w.shape
    return _matmul(
        dm_cols, ada_w, mode="nt", grid=(1, nl), nk=nl, acc_shape=(_COND_ROWS, d),
        a_spec=pl.BlockSpec((None, _COND_ROWS, n), lambda i, q: (q, 0, 0)), b_spec=pl.BlockSpec((None, d, n), lambda i, q: (q, 0, 0)),
        out_shapes=[_sds((_COND_ROWS, d), F32)], out_specs=[pl.BlockSpec((_COND_ROWS, d), lambda i, q: (0, 0))],
        name="cond_grad")[0]


def _pack(parts):
    flat = [p.reshape(-1) for p in parts]
    sizes = [f.shape[0] for f in flat]
    total = sum(sizes)
    rows = -(-total // 1024) * 8
    packed = jnp.concatenate(flat + [jnp.zeros((rows * 128 - total,), F32)]).reshape(rows, 128)
    offs = np.concatenate([[0], np.cumsum(sizes)])[:-1]
    return packed, [(int(o), p.shape) for o, p in zip(offs, parts)]


def _unpack(flat, layout, k):
    off, shape = layout[k]
    return flat[..., off:off + int(np.prod(shape))].reshape(flat.shape[:-1] + tuple(shape))


def kernel(x, c, ctx, c_ctx, norm_g, ada_w, ada_b, pool_w_in, pool_w_grp, pool_scale, pool_w_out, na_w_in, na_rpb, na_w_out, conv_w_in, conv_dw, conv_db, conv_w_out, final_g, loss_target, m_c_ctx, m_norm_g, m_ada_w, m_ada_b, m_pool_w_in, m_pool_w_grp, m_pool_scale, m_pool_w_out, m_na_w_in, m_na_rpb, m_na_w_out, m_conv_w_in, m_conv_dw, m_conv_db, m_conv_w_out, m_final_g, v_c_ctx, v_norm_g, v_ada_w, v_ada_b, v_pool_w_in, v_pool_w_grp, v_pool_scale, v_pool_w_out, v_na_w_in, v_na_rpb, v_na_w_out, v_conv_w_in, v_conv_dw, v_conv_db, v_conv_w_out, v_final_g):
    params = dict(c_ctx=c_ctx, norm_g=norm_g, ada_w=ada_w, ada_b=ada_b, pool_w_in=pool_w_in, pool_w_grp=pool_w_grp,
                  pool_scale=pool_scale, pool_w_out=pool_w_out, na_w_in=na_w_in, na_rpb=na_rpb, na_w_out=na_w_out,
                  conv_w_in=conv_w_in, conv_dw=conv_dw, conv_db=conv_db, conv_w_out=conv_w_out, final_g=final_g)
    mom1 = dict(c_ctx=m_c_ctx, norm_g=m_norm_g, ada_w=m_ada_w, ada_b=m_ada_b, pool_w_in=m_pool_w_in,
                pool_w_grp=m_pool_w_grp, pool_scale=m_pool_scale, pool_w_out=m_pool_w_out, na_w_in=m_na_w_in,
                na_rpb=m_na_rpb, na_w_out=m_na_w_out, conv_w_in=m_conv_w_in, conv_dw=m_conv_dw, conv_db=m_conv_db,
                conv_w_out=m_conv_w_out, final_g=m_final_g)
    mom2 = dict(c_ctx=v_c_ctx, norm_g=v_norm_g, ada_w=v_ada_w, ada_b=v_ada_b, pool_w_in=v_pool_w_in,
                pool_w_grp=v_pool_w_grp, pool_scale=v_pool_scale, pool_w_out=v_pool_w_out, na_w_in=v_na_w_in,
                na_rpb=v_na_rpb, na_w_out=v_na_w_out, conv_w_in=v_conv_w_in, conv_dw=v_conv_dw, conv_db=v_conv_db,
                conv_w_out=v_conv_w_out, final_g=v_final_g)
    d = x.shape[-1]
    w = na_w_out.shape[1] * 4
    xi, yi, ci = _position()
    chip = 2 * xi + yi
    dev = 2 * chip + ci
    n_ada = ada_w.shape[-1]

    def chip_cols(a, size):
        return lax.dynamic_slice_in_dim(a, chip * size, size, axis=a.ndim - 1)

    items = _items(d, w)
    first = [it for it in items if it.key.startswith("pool") and it.layer == 0]
    na = [it for it in items if it.key.startswith("na")]
    late = [it for it in items if it not in first + na]
    shards_of = lambda its: [params[it.key][it.layer].astype(BF16) for it in its]
    empties = lambda its: [lax.empty(it.shape, BF16) for it in its]
    first_copies, na_copies = _gather_ici_copies(first), _gather_ici_copies(na)

    conds = _all_gather(c.reshape(8, d // 8), _AXES, "gather_cond").reshape(8, d)
    behind = conds[0, 0] * 0.0
    first_handle, token = _split_start(first_copies, [s + behind.astype(BF16) for s in shards_of(first)],
                                       empties(first), 3 * len(first), "gather_first_start")
    cond = jnp.concatenate([conds + token[0, 0], c_ctx[None], jnp.zeros((_COND_ROWS - 9, d), F32)], axis=0)
    mod_cols = _modulations(cond, ada_w, chip_cols(ada_b, n_ada)[:, None, :])
    small_pack, small_layout = _pack([pool_scale, conv_dw, conv_db])
    (mod_all,), (small,) = _run_comms([_all_gather_comm(mod_cols, ("x", "y")),
                                       _all_gather_comm(small_pack, ("x", "y"))], "gather_mod")
    behind = mod_all[0, 0, 0, 0] * 0.0
    na_handle, token = _split_start(na_copies, [s + behind.astype(BF16) for s in shards_of(na)], empties(na),
                                    3 * len(na), "gather_na_start")
    first_shards, first_mats = _split_wait(first_copies, first_handle, token, "gather_first_wait")
    first_mats = _gather_pair_finish(first_shards, first_mats, first, "gather_first_pair")
    mod_all = mod_all.transpose(1, 2, 0, 3).reshape(4, _COND_ROWS, 3, d)
    mod = jnp.stack([lax.dynamic_index_in_dim(mod_all, dev, axis=1, keepdims=False), mod_all[:, 8]], axis=1)
    full = {(it.key, it.layer): mat for it, mat in zip(first, first_mats)}
    late_comm = _gather_comm(shards_of(late), late)

    def na_weights(after):
        na_shards, na_mats = _split_wait(na_copies, na_handle, after, "gather_na_wait")
        na_mats = _gather_pair_finish(na_shards, na_mats, na, "gather_na_pair")
        return {it.key: mat for it, mat in zip(na, na_mats)}

    def late_weights(mats):
        full.update({(it.key, it.layer): mat for it, mat in zip(late, mats)})
        return dict(pool_w_in=[full[("pool_w_in", j)] for j in range(2)],
                    pool_w_grp=[full[("pool_w_grp", j)] for j in range(2)],
                    pool_w_out=[full[("pool_w_out", j)] for j in range(2)],
                    conv_w_in=full[("conv_w_in", 0)], conv_w_out=full[("conv_w_out", 0)])

    small = small.reshape(4, -1)

    def whole(k):
        parts = _unpack(small, small_layout, k)
        return jnp.moveaxis(parts, 0, -2).reshape(parts.shape[1:-1] + (-1,))

    wts = dict(pool_w_in=[full[("pool_w_in", 0)]], pool_w_grp=[full[("pool_w_grp", 0)]],
               pool_w_out=[full[("pool_w_out", 0)]], pool_scale=whole(0), na_rpb=na_rpb[0], conv_dw=whole(1)[0],
               conv_db=whole(2))
    pos = jnp.stack([chip, ci]).astype(jnp.int32)

    def layer_grads(its, by_layer):
        pick = {"pool_w_in": "w_in", "pool_w_grp": "w_grp", "pool_w_out": "w_out", "na_w_in": "w_in",
                "na_w_out": "w_out", "conv_w_in": "w_in", "conv_w_out": "w_out"}
        return [by_layer[(it.key.split("_")[0], it.layer)][pick[it.key]] for it in its]

    pairs, handles = dict(), dict()
    half_windows = lambda its: [(lambda ref, half, it=it: it.window(ref, half=half)) for it in its]
    half_shapes = lambda its: [_sds(it.sized(half=True), BF16) for it in its]

    def pair_sums(its, mats, tag):
        got = _pair_swap(mats, half_windows(its), half_shapes(its), f"pair_exchange_{tag}")
        return _pair_sums(mats, got, its, pos, f"pair_sum_{tag}")

    def grad_comm(gr3, gr2):
        pairs["late"] = pair_sums(late, layer_grads(late, {("pool", 1): gr3, ("conv", 0): gr2}), "late")
        return _chip_exchange_comm(pairs["late"], late)

    slot_zones = lambda its: [lax.empty((3,) + it.sized(shard=True, half=True), BF16) for it in its]
    na_xcopies, first_xcopies = _chip_exchange_copies(na), _chip_exchange_copies(first)

    def na_grads_start(gr1):
        pairs["na"] = pair_sums(na, layer_grads(na, {("na", 0): gr1}), "na")
        handles["na"], started = _split_start(na_xcopies, pairs["na"], slot_zones(na), 3 * len(na),
                                              "exchange_na_start")
        return started

    res = _example_step(x[0], ctx[0], loss_target[0], mod, norm_g, final_g[None], wts, dict(
        na_weights=na_weights, late_comm=late_comm, late_weights=late_weights, grad_comm=grad_comm,
        na_grads_start=na_grads_start))
    g0, g1, g2, g3 = res["layers"]
    pairs["na"], na_slots = _split_wait(na_xcopies, handles["na"], g0["w_in"], "exchange_na_wait")
    first_grads = layer_grads(first, {("pool", 0): g0})
    packed, layout = _pack([res["dfinal_g"], res["dnorm_g"], res["dmod"], g1["rpb"],
                            jnp.concatenate([g0["scale"], g3["scale"]], axis=0), g2["dw"], g2["db"],
                            res["loss"][0, :1]])
    first_got, (every,) = _run_comms([_pair_swap_comm(first_grads, half_windows(first), half_shapes(first)),
                                      _all_gather_two_level_comm(packed)], "pair_exchange_first")
    pairs["first"] = _pair_sums(first_grads, first_got, first, pos, "pair_sum_first")

    grads = dict()
    total = _sum_lead(every, "sum_vec_grads").reshape(-1)
    every = every.reshape(8, -1)
    grads["final_g"] = _unpack(total, layout, 0).reshape(final_g.shape)
    grads["norm_g"] = _unpack(total, layout, 1)
    grads["na_rpb"] = _unpack(total, layout, 3)[None]
    grads["pool_scale"] = chip_cols(_unpack(total, layout, 4), pool_scale.shape[-1])
    grads["conv_dw"] = chip_cols(_unpack(total, layout, 5), conv_dw.shape[-1])[None]
    grads["conv_db"] = chip_cols(_unpack(total, layout, 6), conv_db.shape[-1])
    dmod_sum = _unpack(total, layout, 2).reshape(4, 2, 3 * d)
    dmod_each = _unpack(every, layout, 2).reshape(8, 4, 2, 3 * d)
    grads["ada_b"] = dmod_sum[:, 0] + dmod_sum[:, 1]
    dm = jnp.concatenate([dmod_each[:, :, 0].transpose(1, 0, 2), dmod_sum[:, 1][:, None],
                          jnp.zeros((4, _COND_ROWS - 9, 3 * d), F32)], axis=1)
    dm_cols = chip_cols(dm, n_ada)
    dcond = _cond_grad(dm_cols, ada_w)[8].reshape(8, d // 8)
    dcond_all = _all_gather(dcond, ("x", "y"), "gather_cond_grad")
    behind = dcond_all[0, 0, 0] * 0.0
    handles["first"], token = _split_start(first_xcopies, [p + behind.astype(BF16) for p in pairs["first"]],
                                           slot_zones(first), 3 * len(first), "exchange_first_start")
    grads["ada_w"], *ada_w_step = _ada_w_step(cond, dm_cols + token[0, 0], ada_w, m_ada_w, v_ada_w)
    grads["c_ctx"] = _sum_lead(dcond_all, "sum_cond_grad").reshape(d) * _dsilu(c_ctx)
    vector_out = {k: _adamw(params[k], grads[k], mom1[k], mom2[k], f"adamw_{k}")
                  for k in _WEIGHTS if k not in _GRAD_KEYS + ("ada_w",)}
    vector_out["ada_w"] = tuple(ada_w_step)
    pairs["first"], first_slots = _split_wait(first_xcopies, handles["first"], vector_out["ada_w"][2],
                                              "exchange_first_wait")

    slots = dict(zip(late, res["carried"]))
    slots.update(zip(first, first_slots))
    slots.update(zip(na, na_slots))
    pair_of = dict(zip(late, pairs["late"]))
    pair_of.update(zip(first, pairs["first"]))
    pair_of.update(zip(na, pairs["na"]))
    reduced = _chip_sums([pair_of[it] for it in items], [slots[it] for it in items], items, pos, "chip_sum")
    theirs = _pair_swap(reduced, [lambda ref, half: ref] * len(items),
                        [_sds(t.shape, F32) for t in reduced], "pair_return")
    matrix_out = dict()
    for k in _GRAD_KEYS:
        idx = [i for i, it in enumerate(items) if it.key == k]
        res_k = _adamw_matrix(params[k], mom1[k], mom2[k], [reduced[i] for i in idx], [theirs[i] for i in idx],
                              items[idx[0]], pos, f"adamw_{k}")
        grads[k], matrix_out[k] = res_k[0], res_k[1:]

    outs = [[], [], []]
    for k in _WEIGHTS:
        step = matrix_out[k] if k in matrix_out else vector_out[k]
        for lst, val in zip(outs, step):
            lst.append(val)
    loss = _unpack(total, layout, 7)[0]
    return (loss, res["grad_x"][None], *[grads[k].reshape(params[k].shape) for k in _WEIGHTS],
            *outs[0], *outs[1], *outs[2])
```

```python
import functools

import numpy as np
import jax
import jax.numpy as jnp
from jax import lax
from jax.experimental import pallas as pl
from jax.experimental.pallas import tpu as pltpu

F32 = jnp.float32
BF16 = jnp.bfloat16

EPS = 1e-6
GRID_W = 64
HEAD_DIM = 64
WIN_ROWS = 8
WIN_COLS = 16
POOL_WINDOWS = (2, 4, 8, 16)
Q_ROWS = 4
K_ROWS = 12
PAD_ROWS = 4
NEG = -1e30

ADAM_LR = 0.001
ADAM_B1 = 0.9
ADAM_B2 = 0.999
ADAM_EPS = 1e-08
ADAM_WD = 0.01
ADAM_STEP = 10

ROW_BLOCK = 256
VMEM_LIMIT = 56 * 1024 * 1024
MAX_BLOCK_BYTES = 6 * 1024 * 1024
ACT = BF16

MESH = pl.DeviceIdType.MESH
HBM_SPEC = pl.BlockSpec(memory_space=pltpu.HBM)


def _cparams(*sem):
    return pltpu.CompilerParams(dimension_semantics=sem or None, vmem_limit_bytes=VMEM_LIMIT)


def _sds(shape, dtype):
    return jax.ShapeDtypeStruct(tuple(shape), dtype)


def _call(body, args, *, grid, in_specs, out_specs, out_shape, name, scratch_shapes=()):
    return list(pl.pallas_call(
        body, grid=grid, in_specs=list(in_specs), out_specs=list(out_specs), out_shape=list(out_shape),
        scratch_shapes=list(scratch_shapes), name=name, compiler_params=_cparams(*(("arbitrary",) * len(grid))),
    )(*args))


def _sigmoid(x):
    return 1.0 / (1.0 + jnp.exp(-x))


def _silu(x):
    return x * _sigmoid(x)


def _dsilu(x):
    s = _sigmoid(x)
    return s * (1.0 + x * (1.0 - s))


_DIMS = {
    "nn": (((1,), (0,)), ((), ())),
    "nt": (((1,), (1,)), ((), ())),
    "tn": (((0,), (0,)), ((), ())),
}


def _matmul(a, b, *, mode, grid, a_spec, b_spec, out_shapes, out_specs, name, nk=1,
            a_silu=False, exact=False, epilogue=None, extra=(), extra_specs=(), acc_shape=None):
    n_extra = len(extra)
    n_out = len(out_shapes)

    def body(*refs):
        a_ref, b_ref = refs[:2]
        ex = refs[2:2 + n_extra]
        outs = refs[2 + n_extra:2 + n_extra + n_out]
        av = a_ref[...]
        bv = b_ref[...]
        if a_silu:
            av = _silu(av.astype(F32))
        if exact:
            prod = lax.dot_general(av.astype(F32), bv.astype(F32), _DIMS[mode],
                                   precision=lax.Precision.HIGHEST, preferred_element_type=F32)
        else:
            prod = lax.dot_general(av.astype(BF16), bv.astype(BF16), _DIMS[mode], preferred_element_type=F32)

        def finish(res):
            if epilogue == "bias":
                res = res + ex[0][...]
            outs[0][...] = res.astype(outs[0].dtype)

        if nk == 1:
            finish(prod)
        else:
            acc = refs[-1]
            k = pl.program_id(len(grid) - 1)

            @pl.when(k == 0)
            def _():
                acc[...] = prod

            @pl.when(k > 0)
            def _():
                acc[...] += prod

            @pl.when(k == nk - 1)
            def _():
                finish(acc[...])

    scratch = [pltpu.VMEM(acc_shape, F32)] if nk > 1 else []
    sem = ("parallel",) * (len(grid) - 1) + ("arbitrary",)
    return pl.pallas_call(
        body, grid=grid, in_specs=[a_spec, b_spec, *extra_specs], out_specs=list(out_specs),
        out_shape=list(out_shapes), scratch_shapes=scratch, name=name, compiler_params=_cparams(*sem),
    )(a, b, *extra)


def _row_tile(rows):
    for t in (768, 512, 256):
        if rows % t == 0:
            return t
    return rows


def _mm_tn(a, b, name, out_dtype, tm=512):
    r, m = a.shape
    n = b.shape[1]
    tm = min(tm, m)
    tn = min(1024, n)
    return _matmul(
        a, b, mode="tn", grid=(m // tm, n // tn),
        a_spec=pl.BlockSpec((r, tm), lambda i, j: (0, i)), b_spec=pl.BlockSpec((r, tn), lambda i, j: (0, j)),
        out_shapes=[_sds((m, n), out_dtype)], out_specs=[pl.BlockSpec((tm, tn), lambda i, j: (i, j))], name=name)[0]


def _mm_tn_parts(a, b, name, out_dtype, tm=1024):
    r, m = a.shape
    p, _, np_ = b.shape
    tm = min(tm, m)
    return _matmul(
        a, b, mode="tn", grid=(m // tm, p),
        a_spec=pl.BlockSpec((r, tm), lambda i, q: (0, i)), b_spec=pl.BlockSpec((None, r, np_), lambda i, q: (q, 0, 0)),
        out_shapes=[_sds((m, p * np_), out_dtype)], out_specs=[pl.BlockSpec((tm, np_), lambda i, q: (i, q))],
        name=name)[0]


def _row_vec(ref, is_ctx):
    return ref[0] if is_ctx is None else jnp.where(is_ctx, ref[1], ref[0])


def _ctx_rows(i, tm, nx, nseg):
    if nseg == 1:
        return None
    return i * tm + lax.broadcasted_iota(jnp.int32, (tm, 1), 0) >= nx


def _seg_sums(ref, val, is_ctx, first):
    if is_ctx is None:
        parts = [jnp.sum(val, axis=0, keepdims=True)]
    else:
        parts = [jnp.sum(jnp.where(is_ctx, 0.0, val), axis=0, keepdims=True),
                 jnp.sum(jnp.where(is_ctx, val, 0.0), axis=0, keepdims=True)]

    @pl.when(first)
    def _():
        for k, p in enumerate(parts):
            ref[k] = p

    @pl.when(jnp.logical_not(first))
    def _():
        for k, p in enumerate(parts):
            ref[k] += p


def _w_out_resid(a, w_out, xres, gate, nx, name):
    m, k = a.shape
    n = w_out.shape[1]
    nseg = gate.shape[0]
    tm = _row_tile(m)

    def body(a_ref, w_ref, x_ref, gt_ref, yx_ref, xo_ref):
        yx = jnp.dot(a_ref[...], w_ref[...], preferred_element_type=F32)
        yx_ref[...] = yx.astype(ACT)
        xo_ref[...] = x_ref[...] + _row_vec(gt_ref, _ctx_rows(pl.program_id(0), tm, nx, nseg)) * yx

    row = pl.BlockSpec((tm, n), lambda i: (i, 0))
    return pl.pallas_call(
        body, grid=(m // tm,),
        in_specs=[pl.BlockSpec((tm, k), lambda i: (i, 0)), pl.BlockSpec((k, n), lambda i: (0, 0)), row,
                  pl.BlockSpec((nseg, 1, n), lambda i: (0, 0, 0))],
        out_specs=[row, row], out_shape=[_sds((m, n), ACT), _sds((m, n), F32)],
        name=name, compiler_params=_cparams("parallel"),
    )(a, w_out, xres, gate)


def _norm_w_in(x, g, scale, shift, w_in, nx, name, ctx=None):
    d = x.shape[1]
    rows = x.shape[0] + (0 if ctx is None else ctx.shape[0])
    n = w_in.shape[1]
    nseg = scale.shape[0]
    tm = _row_tile(rows)
    tn = n
    row = pl.BlockSpec((tm, d), lambda i, j: (i, 0))
    if ctx is None:
        row_args, row_specs = (x,), [row]
    else:
        assert ctx.shape[0] == ROW_BLOCK and tm % ROW_BLOCK == 0 and nx % ROW_BLOCK == 0
        nsub, x_blocks = tm // ROW_BLOCK, nx // ROW_BLOCK
        row_args = (x,) * nsub + (ctx,)
        row_specs = [pl.BlockSpec((ROW_BLOCK, d), lambda i, j, s=s: (jnp.minimum(i * nsub + s, x_blocks - 1), 0))
                     for s in range(nsub)] + [pl.BlockSpec((ROW_BLOCK, d), lambda i, j: (0, 0))]

    def body(*refs):
        x_refs, (g_ref, sc_ref, sh_ref, w_ref), outs = refs[:len(row_args)], refs[len(row_args):][:4], refs[-3:]
        h_ref, r_ref, p_ref = outs
        i, j = pl.program_id(0), pl.program_id(1)

        @pl.when(j == 0)
        def _():
            if ctx is None:
                xv = x_refs[0][...]
            else:
                xv = jnp.concatenate([jnp.where(i * nsub + s >= x_blocks, x_refs[-1][...], x_refs[s][...])
                                      for s in range(nsub)], axis=0)
                refs[-4][...] = xv
            r = lax.rsqrt(jnp.mean(xv * xv, axis=-1, keepdims=True) + EPS)
            is_ctx = _ctx_rows(i, tm, nx, nseg)
            h = (xv * r) * g_ref[...] * (1.0 + _row_vec(sc_ref, is_ctx)) + _row_vec(sh_ref, is_ctx)
            h_ref[...] = h.astype(BF16)
            r_ref[...] = r

        p_ref[...] = jnp.dot(h_ref[...], w_ref[...], preferred_element_type=F32).astype(ACT)

    vec = pl.BlockSpec((nseg, 1, d), lambda i, j: (0, 0, 0))
    joined = [] if ctx is None else [(row, _sds((rows, d), F32))]
    out_specs, out_shape = zip(*joined, (row, _sds((rows, d), BF16)),
                               (pl.BlockSpec((tm, 1), lambda i, j: (i, 0)), _sds((rows, 1), F32)),
                               (pl.BlockSpec((tm, tn), lambda i, j: (i, j)), _sds((rows, n), ACT)))
    return _call(
        body, (*row_args, g, scale, shift, w_in), grid=(rows // tm, n // tn),
        in_specs=[*row_specs, pl.BlockSpec((1, d), lambda i, j: (0, 0)), vec, vec,
                  pl.BlockSpec((d, tn), lambda i, j: (0, j))],
        out_specs=list(out_specs), out_shape=list(out_shape), name=name)


def _gate_w_out_bwd(dxo, yx, gate, w_out, nx, name):
    rows, d = yx.shape
    w = w_out.shape[0]
    nseg = gate.shape[0]
    tm = _row_tile(rows)

    def body(dx_ref, yx_ref, gt_ref, w_ref, dyx_ref, da_ref, dg_ref):
        i = pl.program_id(0)
        is_ctx = _ctx_rows(i, tm, nx, nseg)
        dxv = dx_ref[...]
        dyx = (dxv * _row_vec(gt_ref, is_ctx)).astype(BF16)
        dyx_ref[...] = dyx
        da_ref[...] = lax.dot_general(dyx, w_ref[...], _DIMS["nt"], preferred_element_type=F32).astype(ACT)
        _seg_sums(dg_ref, dxv * yx_ref[...].astype(F32), is_ctx, i == 0)

    row = pl.BlockSpec((tm, d), lambda i: (i, 0))
    vec = pl.BlockSpec((nseg, 1, d), lambda i: (0, 0, 0))
    return _call(
        body, (dxo, yx, gate, w_out), grid=(rows // tm,),
        in_specs=[row, row, vec, pl.BlockSpec((w, d), lambda i: (0, 0))],
        out_specs=[row, pl.BlockSpec((tm, w), lambda i: (i, 0)), vec],
        out_shape=[_sds((rows, d), BF16), _sds((rows, w), ACT), _sds((nseg, 1, d), F32)], name=name)


def _w_in_bwd_norm(dparts, w_in, x, r, g, scale, dres, nx, name, dx_rows=None):
    np_, rows, kp = dparts.shape
    d = w_in.shape[0]
    nseg = scale.shape[0]
    tm = _row_tile(rows)
    assert dx_rows is None or rows - tm < dx_rows <= rows
    nsub = tm // ROW_BLOCK
    nres_blocks = dres.shape[0] // ROW_BLOCK
    pp = np_
    while pp % 2 == 0 and pp * tm * kp * dparts.dtype.itemsize > MAX_BLOCK_BYTES:
        pp //= 2
    nk = np_ // pp

    def body(dp_ref, w_ref, x_ref, r_ref, g_ref, sc_ref, *rest):
        dres_refs = rest[:nsub]
        dx_ref, dsh_ref, dge_ref, *acc = rest[nsub:]
        i, k = pl.program_id(0), pl.program_id(1)
        prod = sum(lax.dot_general(dp_ref[q], w_ref[:, q * kp:(q + 1) * kp], _DIMS["nt"], preferred_element_type=F32)
                   for q in range(pp))

        def finish(dhv):
            is_ctx = _ctx_rows(i, tm, nx, nseg)
            rv = r_ref[...]
            xn = x_ref[...] * rv
            dxn = dhv * (g_ref[...] * (1.0 + _row_vec(sc_ref, is_ctx)))
            dx = rv * (dxn - xn * jnp.mean(dxn * xn, axis=-1, keepdims=True))
            for s in range(nsub):
                piece = slice(s * ROW_BLOCK, (s + 1) * ROW_BLOCK)
                res = dres_refs[s][...]
                if nres_blocks * ROW_BLOCK < rows:
                    res = jnp.where(i * nsub + s < nres_blocks, res, 0.0)
                dx_ref[piece, :] = dx[piece, :] + res
            _seg_sums(dsh_ref, dhv, is_ctx, i == 0)
            _seg_sums(dge_ref, dhv * xn, is_ctx, i == 0)

        if nk == 1:
            finish(prod)
        else:
            acc_ref, = acc

            @pl.when(k == 0)
            def _():
                acc_ref[...] = prod

            @pl.when(k > 0)
            def _():
                acc_ref[...] += prod

            @pl.when(k == nk - 1)
            def _():
                finish(acc_ref[...])

    row = pl.BlockSpec((tm, d), lambda i, k: (i, 0))
    vec = pl.BlockSpec((nseg, 1, d), lambda i, k: (0, 0, 0))
    return _call(
        body, (dparts, w_in, x, r, g, scale, *([dres] * nsub)), grid=(rows // tm, nk),
        in_specs=[pl.BlockSpec((pp, tm, kp), lambda i, k: (k, i, 0)), pl.BlockSpec((d, pp * kp), lambda i, k: (0, k)),
                  row, pl.BlockSpec((tm, 1), lambda i, k: (i, 0)), pl.BlockSpec((1, d), lambda i, k: (0, 0)), vec]
        + [pl.BlockSpec((ROW_BLOCK, d), (lambda i, k, s=s: (jnp.minimum(i * nsub + s, nres_blocks - 1), 0)))
           for s in range(nsub)],
        out_specs=[row, vec, vec],
        out_shape=[_sds((dx_rows or rows, d), F32), _sds((nseg, 1, d), F32), _sds((nseg, 1, d), F32)],
        scratch_shapes=[pltpu.VMEM((tm, d), F32)] * (nk > 1), name=name)


_PAD_TOP = 16
_PAD_BOT = 32


def _window_sum(buf, xv, lo, n):
    t = xv.shape[0]
    c = xv.shape[1]
    tp = t + _PAD_TOP + _PAD_BOT
    buf[pl.ds(0, _PAD_TOP), :] = jnp.zeros((_PAD_TOP, c), F32)
    buf[pl.ds(_PAD_TOP, t), :] = xv
    buf[pl.ds(_PAD_TOP + t, _PAD_BOT), :] = jnp.zeros((_PAD_BOT, c), F32)
    p = buf[...]
    k = 1
    while k < n:
        p = p + pltpu.roll(p, tp - k, 0)
        k *= 2
    if lo:
        p = pltpu.roll(p, -lo, 0)
    buf[...] = p
    return buf[pl.ds(_PAD_TOP, t), :]


def _window_count(t, half):
    pos = lax.broadcasted_iota(jnp.int32, (t, 1), 0)
    return (jnp.minimum(pos + half, t) - jnp.maximum(pos - half, 0)).astype(F32)


def _segments(rows, nx):
    return [(0, nx)] + ([(nx, rows - nx)] if rows > nx else [])


def _pool_scratch(rows, nx, cols):
    return [pltpu.VMEM((length + _PAD_TOP + _PAD_BOT, cols), F32) for _, length in _segments(rows, nx)]


def _per_group(g, fn):
    for gi, win in enumerate(POOL_WINDOWS):
        pl.when(g == gi)(functools.partial(fn, win))


def _pool_grp_fwd(uv, w_grp, scale, nx, name):
    rows = uv.shape[0]
    ng, gc, _ = w_grp.shape
    w = ng * gc
    segs = _segments(rows, nx)

    def body(u_ref, gt_ref, w_ref, sc_ref, z_ref, mx_ref, a_ref, *bufs):
        def pool(win):
            half = win // 2
            for (start, length), buf in zip(segs, bufs):
                uvv = u_ref[pl.ds(start, length), :].astype(F32)
                s = _window_sum(buf, uvv, -half, win)
                z_ref[pl.ds(start, length), :] = (s / _window_count(length, half) - uvv).astype(BF16)

        _per_group(pl.program_id(0), pool)
        mixed = jnp.dot(z_ref[...], w_ref[...], preferred_element_type=F32)
        mx_ref[...] = mixed.astype(ACT)
        a_ref[...] = (mixed * sc_ref[...] * _silu(gt_ref[...].astype(F32))).astype(BF16)

    col = pl.BlockSpec((rows, gc), lambda g: (0, g))
    return _call(
        body, (uv, uv, w_grp, scale), grid=(ng,),
        in_specs=[col, pl.BlockSpec((rows, gc), lambda g: (0, ng + g)), pl.BlockSpec((None, gc, gc), lambda g: (g, 0, 0)),
                  pl.BlockSpec((1, gc), lambda g: (0, g))],
        out_specs=[col, col, col], out_shape=[_sds((rows, w), BF16), _sds((rows, w), ACT), _sds((rows, w), BF16)],
        scratch_shapes=_pool_scratch(rows, nx, gc), name=name)


def _pool_grp_bwd(da, mixed, uv, scale, w_grp, nx, name):
    rows, w = da.shape
    ng, gc, _ = w_grp.shape
    segs = _segments(rows, nx)

    def body(da_ref, mx_ref, gt_ref, sc_ref, w_ref, dm_ref, duv_ref, dsc_ref, dz_ref, *bufs):
        dav = da_ref[...].astype(F32)
        mixed = mx_ref[...].astype(F32)
        gt = gt_ref[...].astype(F32)
        sg = _silu(gt)
        sc = sc_ref[...]
        dm = (dav * sc * sg).astype(BF16)
        dm_ref[...] = dm
        dz_ref[...] = lax.dot_general(dm, w_ref[...], _DIMS["nt"], preferred_element_type=F32)
        duv_ref[1] = (dav * mixed * sc * _dsilu(gt)).astype(BF16)
        dsc_ref[...] = jnp.sum(dav * mixed * sg, axis=0, keepdims=True)

        def unpool(win):
            half = win // 2
            for (start, length), buf in zip(segs, bufs):
                dzv = dz_ref[pl.ds(start, length), :]
                s = _window_sum(buf, dzv / _window_count(length, half), 1 - half, win)
                duv_ref[0, pl.ds(start, length), :] = (s - dzv).astype(BF16)

        _per_group(pl.program_id(0), unpool)

    col = pl.BlockSpec((rows, gc), lambda g: (0, g))
    vec = pl.BlockSpec((1, gc), lambda g: (0, g))
    return pl.pallas_call(
        body, grid=(ng,),
        in_specs=[col, col, pl.BlockSpec((rows, gc), lambda g: (0, ng + g)), vec,
                  pl.BlockSpec((None, gc, gc), lambda g: (g, 0, 0))],
        out_specs=[col, pl.BlockSpec((2, rows, gc), lambda g: (0, 0, g)), vec],
        out_shape=[_sds((rows, w), BF16), _sds((2, rows, w), BF16), _sds((1, w), F32)],
        scratch_shapes=[pltpu.VMEM((rows, gc), F32)] + _pool_scratch(rows, nx, gc),
        name=name, compiler_params=_cparams("parallel"),
    )(da, mixed, uv, scale, w_grp)


def _grp_wgrad(z, dm, ng, name, out_dtype):
    rows, w = z.shape
    gc = w // ng

    def body(z_ref, dm_ref, o_ref):
        o_ref[...] = lax.dot_general(z_ref[...], dm_ref[...], _DIMS["tn"],
                                     preferred_element_type=F32).astype(o_ref.dtype)

    blk = pl.BlockSpec((rows, gc), lambda g: (0, g))
    return pl.pallas_call(
        body, grid=(ng,), in_specs=[blk, blk], out_specs=pl.BlockSpec((None, gc, gc), lambda g: (g, 0, 0)),
        out_shape=_sds((ng, gc, gc), out_dtype), name=name, compiler_params=_cparams("parallel"),
    )(z, dm)


def _shift_rows(v, by):
    t = v.shape[0]
    pos = lax.broadcasted_iota(jnp.int32, v.shape, 0)
    rolled = pltpu.roll(v, by % t, 0)
    keep = pos >= by if by > 0 else pos < t + by
    return jnp.where(keep, rolled, 0.0)


def _conv_specs(t, w, cb):
    return [pl.BlockSpec((t, cb), (lambda j, q=q: (0, q * (w // cb) + j))) for q in range(4)]


def _conv_fwd(p4, dw, db, name):
    t = p4.shape[0]
    w = p4.shape[1] // 4
    cb = 128

    def body(bg_ref, cg_ref, v_ref, g_ref, dw_ref, db_ref, a_ref):
        tv = cg_ref[...].astype(F32) * v_ref[...].astype(F32)
        conv = (dw_ref[0:1, :] * _shift_rows(tv, 1) + dw_ref[1:2, :] * tv + dw_ref[2:3, :] * _shift_rows(tv, -1)
                + db_ref[...])
        a_ref[...] = (bg_ref[...].astype(F32) * conv * _silu(g_ref[...].astype(F32))).astype(BF16)

    return pl.pallas_call(
        body, grid=(w // cb,),
        in_specs=_conv_specs(t, w, cb) + [pl.BlockSpec((3, cb), lambda j: (0, j)), pl.BlockSpec((1, cb), lambda j: (0, j))],
        out_specs=pl.BlockSpec((t, cb), lambda j: (0, j)), out_shape=_sds((t, w), BF16),
        name=name, compiler_params=_cparams("parallel"),
    )(p4, p4, p4, p4, dw, db)


def _conv_bwd(da, p4, dw, db, name):
    t, w = da.shape
    cb = 128

    def body(da_ref, bg_ref, cg_ref, v_ref, g_ref, dw_ref, db_ref, d4_ref, ddw_ref, ddb_ref):
        cg = cg_ref[...].astype(F32)
        vv = v_ref[...].astype(F32)
        bg = bg_ref[...].astype(F32)
        gv = g_ref[...].astype(F32)
        tv = cg * vv
        tm1 = _shift_rows(tv, 1)
        tp1 = _shift_rows(tv, -1)
        w0, w1, w2 = dw_ref[0:1, :], dw_ref[1:2, :], dw_ref[2:3, :]
        conv = w0 * tm1 + w1 * tv + w2 * tp1 + db_ref[...]
        y = bg * conv
        dav = da_ref[...].astype(F32)
        dy = dav * _silu(gv)
        d4_ref[3] = (dav * y * _dsilu(gv)).astype(BF16)
        d4_ref[0] = (dy * conv).astype(BF16)
        dconv = dy * bg
        ddb_ref[...] = jnp.sum(dconv, axis=0, keepdims=True)
        ddw_ref[0:1, :] = jnp.sum(dconv * tm1, axis=0, keepdims=True)
        ddw_ref[1:2, :] = jnp.sum(dconv * tv, axis=0, keepdims=True)
        ddw_ref[2:3, :] = jnp.sum(dconv * tp1, axis=0, keepdims=True)
        dt = w0 * _shift_rows(dconv, -1) + w1 * dconv + w2 * _shift_rows(dconv, 1)
        d4_ref[1] = (dt * vv).astype(BF16)
        d4_ref[2] = (dt * cg).astype(BF16)

    col = pl.BlockSpec((t, cb), lambda j: (0, j))
    tap = pl.BlockSpec((3, cb), lambda j: (0, j))
    bias = pl.BlockSpec((1, cb), lambda j: (0, j))
    return pl.pallas_call(
        body, grid=(w // cb,), in_specs=[col] + _conv_specs(t, w, cb) + [tap, bias],
        out_specs=[pl.BlockSpec((4, t, cb), lambda j: (0, 0, j)), tap, bias],
        out_shape=[_sds((4, t, w), BF16), _sds((3, w), F32), _sds((1, w), F32)],
        name=name, compiler_params=_cparams("parallel"),
    )(da, p4, p4, p4, p4, dw, db)


def _attn_mask():
    qn, kn = Q_ROWS * GRID_W, K_ROWS * GRID_W
    qr, qc = np.divmod(np.arange(qn), GRID_W)
    kr, kc = np.divmod(np.arange(kn), GRID_W)
    col0 = np.clip(qc - WIN_COLS // 2, 0, GRID_W - WIN_COLS)
    col_ok = (kc[None, :] >= col0[:, None]) & (kc[None, :] < col0[:, None] + WIN_COLS)
    first = np.zeros(qn, np.int64)
    last = np.full(qn, K_ROWS - WIN_ROWS)
    out = []
    for row0 in (first, qr, last):
        row_ok = (kr[None, :] >= row0[:, None]) & (kr[None, :] < row0[:, None] + WIN_ROWS)
        out.append(np.where(row_ok & col_ok, 0.0, NEG))
    return jnp.asarray(np.stack(out), F32)


_KW = K_ROWS * GRID_W
_QB = Q_ROWS * GRID_W
_PAIR = 2 * HEAD_DIM
_N_DR = 2 * WIN_ROWS - 1
_N_DC = 2 * WIN_COLS - 1
_RP_ROWS = 24
_N_TILES = _N_DR + 1
_BIAS_BASE = (WIN_ROWS - 1, WIN_ROWS // 2 - 1, -1)


class _Comm:
    def __init__(self, ins, outs, sems, start, finish):
        self.ins, self.outs, self.sems, self.start, self.finish = list(ins), list(outs), list(sems), start, finish


def _bias_pieces(cls):
    out = []
    for qr in range(Q_ROWS):
        for kr in range(0, K_ROWS, 2):
            tile = _BIAS_BASE[cls] - qr + kr + 1
            out.append((qr, kr, tile if 0 <= tile < _N_TILES else None))
    return out


def _toeplitz_pair(left_row, right_row):
    lane = lax.broadcasted_iota(jnp.int32, (GRID_W, _PAIR), 1)
    shape = (GRID_W, _PAIR)
    left = pltpu.roll(jnp.broadcast_to(left_row, shape), _PAIR - (WIN_COLS - 1), 1, stride=1, stride_axis=0)
    right = pltpu.roll(jnp.broadcast_to(right_row, shape), GRID_W - (WIN_COLS - 1), 1, stride=1, stride_axis=0)
    return jnp.where(lane < GRID_W, left, right)


def _build_tiles(tiles_ref, rp_ref):
    for h in range(2):
        for t in range(_N_TILES):
            tiles_ref[h, t] = _toeplitz_pair(rp_ref[h, t:t + 1, :], rp_ref[h, t + 1:t + 2, :])


def _block_class(b, nblk, fn, entering=False):
    interior = (b == 1) if entering else jnp.logical_and(b > 0, b < nblk - 1)
    for cls, cond in enumerate((b == 0, interior, b == nblk - 1)):
        pl.when(cond)(functools.partial(fn, cls))


def _attn_geometry(p4, nx):
    rows = p4.shape[0]
    w = p4.shape[1] // 4
    nhp = w // _PAIR
    nblk = nx // _QB
    qspec = lambda col: pl.BlockSpec((_QB, _PAIR), lambda hp, b: (b, col * nhp + hp))
    kspec = lambda col: pl.BlockSpec((rows, _PAIR), lambda hp, b: (0, col * nhp + hp))
    tspec = pl.BlockSpec((2, _RP_ROWS, _PAIR), lambda hp, b: (hp, 0, 0))
    mspec = pl.BlockSpec((None, _QB, _KW), lambda hp, b: (jnp.where(b == 0, 0, jnp.where(b == nblk - 1, 2, 1)), 0, 0))
    lspec = pl.BlockSpec((None, _QB, 2), lambda hp, b: (hp, b, 0))
    ospec = pl.BlockSpec((_QB, _PAIR), lambda hp, b: (b, hp))
    return rows, w, nhp, nblk, qspec, kspec, tspec, mspec, lspec, ospec


def _window_start(b, nx):
    return pl.multiple_of(jnp.clip(b * _QB - PAD_ROWS * GRID_W, 0, nx - _KW), _QB)


def _load_bias(bias_ref, tiles_ref, rp_ref, m_ref, b, nblk):
    pl.when(b == 0)(lambda: _build_tiles(tiles_ref, rp_ref))

    def fill(cls):
        for h in range(2):
            for qr, kr, tile in _bias_pieces(cls):
                rows = slice(qr * GRID_W, (qr + 1) * GRID_W)
                cols = slice(kr * GRID_W, (kr + 2) * GRID_W)
                m = m_ref[rows, cols]
                bias_ref[h, rows, cols] = m if tile is None else tiles_ref[h, tile] + m

    _block_class(b, nblk, fill, entering=True)


def _attn_fwd(p4, rp, mask, nx, name, comm=None):
    rows, w, nhp, nblk, qspec, kspec, tspec, mspec, lspec, ospec = _attn_geometry(p4, nx)
    n_ctx = rows - nx
    n_cin, n_cout = (len(comm.ins), len(comm.outs)) if comm else (0, 0)

    def body(*refs):
        q_ref, k_ref, v_ref, g_ref, rp_ref, m_ref = refs[:6]
        cin = refs[6:6 + n_cin]
        a_ref, o_ref, lse_ref = refs[6 + n_cin:9 + n_cin]
        cout = refs[9 + n_cin:9 + n_cin + n_cout]
        bias_ref, tiles_ref = refs[9 + n_cin + n_cout:11 + n_cin + n_cout]
        sems = refs[11 + n_cin + n_cout:]
        hp, b = pl.program_id(0), pl.program_id(1)
        if comm:
            pl.when(jnp.logical_and(hp == 0, b == 0))(lambda: comm.start(cin, cout, sems))
        start = _window_start(b, nx)
        _load_bias(bias_ref, tiles_ref, rp_ref, m_ref, b, nblk)
        qf = q_ref[...].astype(F32) * HEAD_DIM ** -0.5
        kw = k_ref[pl.ds(start, _KW), :].astype(BF16)
        vw = v_ref[pl.ds(start, _KW), :].astype(BF16)
        kcv = k_ref[pl.ds(nx, n_ctx), :].astype(BF16)
        vcv = v_ref[pl.ds(nx, n_ctx), :].astype(BF16)
        lane = lax.broadcasted_iota(jnp.int32, (1, _PAIR), 1)
        outs, lses = [], []
        for h in range(2):
            mine = (lane >= HEAD_DIM) if h else (lane < HEAD_DIM)
            qm = jnp.where(mine, qf, 0.0).astype(BF16)
            s_loc = lax.dot_general(qm, kw, _DIMS["nt"], preferred_element_type=F32) + bias_ref[h]
            s_ctx = lax.dot_general(qm, kcv, _DIMS["nt"], preferred_element_type=F32)
            mx = jnp.maximum(jnp.max(s_loc, axis=-1, keepdims=True), jnp.max(s_ctx, axis=-1, keepdims=True))
            p_loc = jnp.exp(s_loc - mx)
            p_ctx = jnp.exp(s_ctx - mx)
            den = jnp.sum(p_loc, axis=-1, keepdims=True) + jnp.sum(p_ctx, axis=-1, keepdims=True)
            o = jnp.dot(p_loc.astype(BF16), vw, preferred_element_type=F32)
            o = o + jnp.dot(p_ctx.astype(BF16), vcv, preferred_element_type=F32)
            outs.append(o * (1.0 / den))
            lses.append(mx + jnp.log(den))
        o = jnp.where(lane < HEAD_DIM, outs[0], outs[1])
        o_ref[...] = o.astype(ACT)
        a_ref[...] = (o * _silu(g_ref[...].astype(F32))).astype(BF16)
        col = lax.broadcasted_iota(jnp.int32, (1, 2), 1)
        lse_ref[...] = jnp.where(col == 0, lses[0], lses[1])
        if comm:
            pl.when(jnp.logical_and(hp == nhp - 1, b == nblk - 1))(lambda: comm.finish(cin, cout, sems))

    res = pl.pallas_call(
        body, grid=(nhp, nblk),
        in_specs=[qspec(0), kspec(1), kspec(2), qspec(3), tspec, mspec] + [HBM_SPEC] * n_cin,
        out_specs=[ospec, ospec, lspec] + [HBM_SPEC] * n_cout,
        out_shape=[_sds((nx, w), BF16), _sds((nx, w), ACT), _sds((nhp, nx, 2), F32)] + (comm.outs if comm else []),
        scratch_shapes=[pltpu.VMEM((2, _QB, _KW), F32), pltpu.VMEM((2, _N_TILES, GRID_W, _PAIR), F32)]
        + (comm.sems if comm else []),
        name=name, compiler_params=_cparams("arbitrary", "arbitrary"),
    )(p4, p4, p4, p4, rp, mask, *(comm.ins if comm else []))
    return res[:3], res[3:]


def _fold_tiles(dtiles_ref, drp_ref):
    shape = (GRID_W, _PAIR)
    lane = lax.broadcasted_iota(jnp.int32, shape, 1)
    flip = (lax.broadcasted_iota(jnp.int32, (_PAIR, _PAIR), 0)
            + lax.broadcasted_iota(jnp.int32, (_PAIR, _PAIR), 1) == _PAIR - 1).astype(F32)
    drp_ref[...] = jnp.zeros(drp_ref.shape, F32)
    for h in range(2):
        stack = dtiles_ref[h].reshape(_N_TILES * GRID_W, _PAIR)
        rev = jnp.dot(stack, flip, precision=lax.Precision.HIGHEST, preferred_element_type=F32)
        for t in range(_N_TILES):
            tile = rev[t * GRID_W:(t + 1) * GRID_W, :]
            for side in (0, 1):
                shift = _PAIR - GRID_W * side - (WIN_COLS - 1)
                half = jnp.where((lane < GRID_W) if side else (lane >= GRID_W), tile, 0.0)
                diag = pltpu.roll(half, shift, 1, stride=1, stride_axis=0)
                drp_ref[h, t + side:t + side + 1, :] += jnp.sum(diag, axis=0, keepdims=True)


def _attn_bwd(p4, rp, mask, o, lse, da, nx, name, comm=None):
    rows, w, nhp, nblk, qspec, kspec, tspec, mspec, lspec, ospec = _attn_geometry(p4, nx)
    n_ctx = rows - nx
    n_cin, n_cout = (len(comm.ins), len(comm.outs)) if comm else (0, 0)

    def body(*refs):
        q_ref, k_ref, v_ref, g_ref, rp_ref, m_ref, o_ref, lse_ref, da_ref = refs[:9]
        cin = refs[9:9 + n_cin]
        d4_ref, drp_ref = refs[9 + n_cin:11 + n_cin]
        cout = refs[11 + n_cin:11 + n_cin + n_cout]
        bias_ref, tiles_ref, ds_ref, dtiles_ref, dk_ref, dv_ref = refs[11 + n_cin + n_cout:17 + n_cin + n_cout]
        sems = refs[17 + n_cin + n_cout:]
        hp, b = pl.program_id(0), pl.program_id(1)
        if comm:
            pl.when(jnp.logical_and(hp == 0, b == 0))(lambda: comm.start(cin, cout, sems))
        start = _window_start(b, nx)
        here = pl.multiple_of(b * _QB, _QB)

        @pl.when(b == 0)
        def _():
            dk_ref[...] = jnp.zeros(dk_ref.shape, F32)
            dv_ref[...] = jnp.zeros(dv_ref.shape, F32)
            dtiles_ref[...] = jnp.zeros(dtiles_ref.shape, F32)
            d4_ref[0, pl.ds(nx, n_ctx), :] = jnp.zeros((n_ctx, _PAIR), BF16)
            d4_ref[3, pl.ds(nx, n_ctx), :] = jnp.zeros((n_ctx, _PAIR), BF16)

        _load_bias(bias_ref, tiles_ref, rp_ref, m_ref, b, nblk)
        gv = g_ref[...].astype(F32)
        dav = da_ref[...].astype(F32)
        ov = o_ref[...].astype(F32)
        dov = dav * _silu(gv)
        d4_ref[3, pl.ds(here, _QB), :] = (dav * ov * _dsilu(gv)).astype(BF16)
        qf = q_ref[...].astype(F32) * HEAD_DIM ** -0.5
        kw = k_ref[pl.ds(start, _KW), :].astype(BF16)
        vw = v_ref[pl.ds(start, _KW), :].astype(BF16)
        kcv = k_ref[pl.ds(nx, n_ctx), :].astype(BF16)
        vcv = v_ref[pl.ds(nx, n_ctx), :].astype(BF16)
        lane = lax.broadcasted_iota(jnp.int32, (1, _PAIR), 1)
        dq = jnp.zeros((_QB, _PAIR), F32)
        for h in range(2):
            mine = (lane >= HEAD_DIM) if h else (lane < HEAD_DIM)
            qm = jnp.where(mine, qf, 0.0).astype(BF16)
            dom = jnp.where(mine, dov, 0.0)
            dob = dom.astype(BF16)
            lse = lse_ref[:, h:h + 1]
            s_loc = lax.dot_general(qm, kw, _DIMS["nt"], preferred_element_type=F32)
            p_loc = jnp.exp(s_loc + bias_ref[h] - lse)
            p_ctx = jnp.exp(lax.dot_general(qm, kcv, _DIMS["nt"], preferred_element_type=F32) - lse)
            delta = jnp.sum(dom * ov, axis=-1, keepdims=True)
            ds_loc = p_loc * (lax.dot_general(dob, vw, _DIMS["nt"], preferred_element_type=F32) - delta)
            ds_ctx = p_ctx * (lax.dot_general(dob, vcv, _DIMS["nt"], preferred_element_type=F32) - delta)
            dsb_loc = ds_loc.astype(BF16)
            dsb_ctx = ds_ctx.astype(BF16)
            dq_h = (jnp.dot(dsb_loc, kw, preferred_element_type=F32)
                    + jnp.dot(dsb_ctx, kcv, preferred_element_type=F32))
            dq = dq + jnp.where(mine, dq_h, 0.0)
            dk_ref[pl.ds(start, _KW), :] += lax.dot_general(dsb_loc, qm, _DIMS["tn"], preferred_element_type=F32)
            dv_ref[pl.ds(start, _KW), :] += lax.dot_general(p_loc.astype(BF16), dob, _DIMS["tn"],
                                                            preferred_element_type=F32)
            dk_ref[pl.ds(nx, n_ctx), :] += lax.dot_general(dsb_ctx, qm, _DIMS["tn"], preferred_element_type=F32)
            dv_ref[pl.ds(nx, n_ctx), :] += lax.dot_general(p_ctx.astype(BF16), dob, _DIMS["tn"],
                                                           preferred_element_type=F32)
            ds_ref[h] = ds_loc
        d4_ref[0, pl.ds(here, _QB), :] = (dq * HEAD_DIM ** -0.5).astype(BF16)

        def scatter(cls):
            for h in range(2):
                for qr, kr, tile in _bias_pieces(cls):
                    if tile is not None:
                        dtiles_ref[h, tile] += ds_ref[h, qr * GRID_W:(qr + 1) * GRID_W, kr * GRID_W:(kr + 2) * GRID_W]

        _block_class(b, nblk, scatter)

        @pl.when(b == nblk - 1)
        def _():
            d4_ref[1] = dk_ref[...].astype(BF16)
            d4_ref[2] = dv_ref[...].astype(BF16)
            _fold_tiles(dtiles_ref, drp_ref)

        if comm:
            pl.when(jnp.logical_and(hp == nhp - 1, b == nblk - 1))(lambda: comm.finish(cin, cout, sems))

    tiles = pltpu.VMEM((2, _N_TILES, GRID_W, _PAIR), F32)
    block = pltpu.VMEM((2, _QB, _KW), F32)
    res = pl.pallas_call(
        body, grid=(nhp, nblk),
        in_specs=[qspec(0), kspec(1), kspec(2), qspec(3), tspec, mspec, ospec, lspec, ospec] + [HBM_SPEC] * n_cin,
        out_specs=[pl.BlockSpec((4, rows, _PAIR), lambda hp, b: (0, 0, hp)), tspec] + [HBM_SPEC] * n_cout,
        out_shape=[_sds((4, rows, w), BF16), _sds(rp.shape, F32)] + (comm.outs if comm else []),
        scratch_shapes=[block, tiles, block, tiles, pltpu.VMEM((rows, _PAIR), F32), pltpu.VMEM((rows, _PAIR), F32)]
        + (comm.sems if comm else []),
        name=name, compiler_params=_cparams("arbitrary", "arbitrary"),
    )(p4, p4, p4, p4, rp, mask, o, lse, da, *(comm.ins if comm else []))
    return res[:2], res[2:]


def _w_out_loss(a, w_out, xres, gate, g, target, name):
    m, k = a.shape
    d = w_out.shape[1]
    assert gate.shape[0] == 1
    tm = _row_tile(m)
    nblk = m // tm

    def body(a_ref, w_ref, x_ref, gt_ref, g_ref, t_ref, yx_ref, loss_ref, dx_ref, dg_ref, acc_ref):
        i = pl.program_id(0)
        yx = jnp.dot(a_ref[...], w_ref[...], preferred_element_type=F32)
        yx_ref[...] = yx.astype(ACT)
        xv = x_ref[...] + gt_ref[0] * yx
        gv = g_ref[...]
        r = lax.rsqrt(jnp.mean(xv * xv, axis=-1, keepdims=True) + EPS)
        xn = xv * r
        err = xn * gv - t_ref[...]
        dy = err * (1.0 / d)
        dxn = dy * gv
        dx_ref[...] = r * (dxn - xn * jnp.mean(dxn * xn, axis=-1, keepdims=True))
        s_g = jnp.sum(dy * xn, axis=0, keepdims=True)
        s_l = jnp.sum(jnp.mean(err * err, axis=-1, keepdims=True), axis=0, keepdims=True)

        @pl.when(i == 0)
        def _():
            dg_ref[...] = s_g
            acc_ref[...] = s_l

        @pl.when(i > 0)
        def _():
            dg_ref[...] += s_g
            acc_ref[...] += s_l

        @pl.when(i == nblk - 1)
        def _():
            loss_ref[...] = jnp.broadcast_to(0.5 * acc_ref[...], loss_ref.shape)

    row = pl.BlockSpec((tm, d), lambda i: (i, 0))
    vec = pl.BlockSpec((1, d), lambda i: (0, 0))
    return pl.pallas_call(
        body, grid=(nblk,),
        in_specs=[pl.BlockSpec((tm, k), lambda i: (i, 0)), pl.BlockSpec((k, d), lambda i: (0, 0)), row,
                  pl.BlockSpec((1, 1, d), lambda i: (0, 0, 0)), vec, row],
        out_specs=[row, pl.BlockSpec((1, 128), lambda i: (0, 0)), row, vec],
        out_shape=[_sds((m, d), ACT), _sds((1, 128), F32), _sds((m, d), F32), _sds((1, d), F32)],
        scratch_shapes=[pltpu.VMEM((1, 1), F32)], name=name, compiler_params=_cparams("arbitrary"),
    )(a, w_out, xres, gate, g, target)


def _as2d(a):
    if a.ndim == 1:
        return a.reshape(-1, 128) if a.shape[0] % 128 == 0 else a.reshape(1, -1)
    return a.reshape(-1, a.shape[-1])


def _adamw(w, g, m, v, name):
    shape = w.shape
    w2, g2, m2, v2 = (_as2d(t) for t in (w, g.reshape(shape), m, v))
    rows, cols = w2.shape
    tr = 512 if rows % 512 == 0 else rows
    c1 = 1.0 - ADAM_B1 ** ADAM_STEP
    c2 = 1.0 - ADAM_B2 ** ADAM_STEP

    def body(w_ref, g_ref, m_ref, v_ref, d_ref, nm_ref, nv_ref):
        gv = g_ref[...]
        nm = ADAM_B1 * m_ref[...] + (1.0 - ADAM_B1) * gv
        nv = ADAM_B2 * v_ref[...] + (1.0 - ADAM_B2) * (gv * gv)
        nm_ref[...] = nm
        nv_ref[...] = nv
        d_ref[...] = -ADAM_LR * ((nm / c1) / (jnp.sqrt(nv / c2) + ADAM_EPS) + ADAM_WD * w_ref[...])

    blk = pl.BlockSpec((tr, cols), lambda i: (i, 0))
    outs = _call(body, (w2, g2, m2, v2), grid=(rows // tr,), in_specs=[blk] * 4, out_specs=[blk] * 3,
                 out_shape=[_sds((rows, cols), F32)] * 3, name=name)
    return tuple(t.reshape(shape) for t in outs)


def _sum_lead(x, name, out_dtype=F32):
    n, rows, cols = x.shape
    tr = 512 if rows % 512 == 0 else rows

    def body(x_ref, o_ref):
        acc = x_ref[0].astype(F32)
        for k in range(1, n):
            acc = acc + x_ref[k].astype(F32)
        o_ref[...] = acc.astype(out_dtype)

    return pl.pallas_call(
        body, grid=(rows // tr,), in_specs=[pl.BlockSpec((n, tr, cols), lambda i: (0, i, 0))],
        out_specs=pl.BlockSpec((tr, cols), lambda i: (i, 0)), out_shape=_sds((rows, cols), out_dtype),
        name=name, compiler_params=_cparams("parallel"),
    )(x)


def _seg_vecs(mod_l, which, nseg):
    return mod_l[:nseg, which][:, None, :]


def _norm_grads(dshift, dgeff, dgate, g, scale):
    nseg, _, d = dshift.shape
    dmod = jnp.stack([dshift[:, 0], dgeff[:, 0] * g, dgate[:, 0]], axis=1)
    if nseg == 1:
        dmod = jnp.concatenate([dmod, jnp.zeros((1, 3, d), F32)], axis=0)
    dg = jnp.sum(dgeff[:, 0] * (1.0 + scale[:, 0]), axis=0)
    return dmod, dg


def _pool_layer(xin, g, mod_l, w_in, w_grp, w_out, pscale, nx, tag, ctx=None, head=None):
    nseg = 1 if ctx is None else 2
    shift, scale, gate = (_seg_vecs(mod_l, k, nseg) for k in range(3))
    *joined, h, r, uv = _norm_w_in(xin, g, scale, shift, w_in, nx, f"w_in_fwd_{tag}", ctx)
    if joined:
        xin, = joined
    z, mixed, a = _pool_grp_fwd(uv, w_grp, pscale, nx, f"pool_fwd_{tag}")
    if head is None:
        yx, xout = _w_out_resid(a, w_out, xin, gate, nx, f"w_out_fwd_{tag}")
    else:
        yx, *xout = _w_out_loss(a, w_out, xin, gate, *head, f"w_out_loss_{tag}")

    def backward(dxo, token=None):
        gate_b = gate if token is None else gate + token[0, 0]
        dyx, da, dgate = _gate_w_out_bwd(dxo, yx, gate_b, w_out, nx, f"w_out_bwd_{tag}")
        gw_out = _mm_tn(a, dyx, f"w_out_grad_{tag}", BF16)
        dm, duv, dscale = _pool_grp_bwd(da, mixed, uv, pscale, w_grp, nx, f"pool_bwd_{tag}")
        gw_grp = _grp_wgrad(z, dm, w_grp.shape[0], f"grp_grad_{tag}", BF16)
        gw_in = _mm_tn_parts(h, duv, f"w_in_grad_{tag}", BF16)
        dx, dshift, dgeff = _w_in_bwd_norm(duv, w_in, xin, r, g, scale, dxo, nx, f"w_in_bwd_{tag}",
                                           dx_rows=None if ctx is None else nx)
        dmod, dg = _norm_grads(dshift, dgeff, dgate, g[0], scale)
        return dx, dmod, dg, dict(w_in=gw_in, w_grp=gw_grp, w_out=gw_out, scale=dscale)

    return xout, backward


def _na_layer(xc, g, mod_l, w_in, rpb, w_out, nx, mask, comm=None):
    nh, n_dr, n_dc = rpb.shape
    shift, scale = _seg_vecs(mod_l, 0, 2), _seg_vecs(mod_l, 1, 2)
    gate = _seg_vecs(mod_l, 2, 1)
    h, r, p4 = _norm_w_in(xc, g, scale, shift, w_in, nx, "w_in_fwd_na")
    rp = jnp.pad(rpb, ((0, 0), (1, _RP_ROWS - 1 - n_dr), (0, _PAIR - n_dc)))
    (a, o, lse), carried = _attn_fwd(p4, rp, mask, nx, "attn_fwd", comm)
    yx, xout = _w_out_resid(a, w_out, xc, gate, nx, "w_out_fwd_na")

    def backward(dxo, comm=None):
        dyx, da, dgate = _gate_w_out_bwd(dxo, yx, gate, w_out, nx, "w_out_bwd_na")
        gw_out = _mm_tn(a, dyx, "w_out_grad_na", BF16)
        (d4, drp), carried_bwd = _attn_bwd(p4, rp, mask, o, lse, da, nx, "attn_bwd", comm)
        gw_in = _mm_tn_parts(h, d4, "w_in_grad_na", BF16)
        dx, dshift, dgeff = _w_in_bwd_norm(d4, w_in, xc, r, g, scale, dxo, nx, "w_in_bwd_na")
        dgate2 = jnp.concatenate([dgate, jnp.zeros_like(dgate)], axis=0)
        dmod, dg = _norm_grads(dshift, dgeff, dgate2, g[0], scale)
        drpb = drp[:, 1:1 + n_dr, ::-1][:, :, :n_dc]
        return dx, dmod, dg, dict(w_in=gw_in, w_out=gw_out, rpb=drpb), carried_bwd

    return xout, backward, carried


def _conv_layer(xin, g, mod_l, w_in, dw, db, w_out):
    shift, scale, gate = (_seg_vecs(mod_l, k, 1) for k in range(3))
    nx = xin.shape[0]
    h, r, p4 = _norm_w_in(xin, g, scale, shift, w_in, nx, "w_in_fwd_conv")
    a = _conv_fwd(p4, dw, db, "conv_fwd")
    yx, xout = _w_out_resid(a, w_out, xin, gate, nx, "w_out_fwd_conv")

    def backward(dxo):
        dyx, da, dgate = _gate_w_out_bwd(dxo, yx, gate, w_out, nx, "w_out_bwd_conv")
        gw_out = _mm_tn(a, dyx, "w_out_grad_conv", BF16)
        d4, ddw, ddb = _conv_bwd(da, p4, dw, db, "conv_bwd")
        gw_in = _mm_tn_parts(h, d4, "w_in_grad_conv", BF16)
        dx, dshift, dgeff = _w_in_bwd_norm(d4, w_in, xin, r, g, scale, dxo, nx, "w_in_bwd_conv")
        dmod, dg = _norm_grads(dshift, dgeff, dgate, g[0], scale)
        return dx, dmod, dg, dict(w_in=gw_in, w_out=gw_out, dw=ddw, db=ddb)

    return xout, backward


def _example_step(x, ctx, target, mod, norm_g, final_g, wts, hooks=None):
    hooks = hooks or {}
    na_weights, late_comm, late_weights = (hooks.get(k) for k in ("na_weights", "late_comm", "late_weights"))
    nx = x.shape[0]
    consts = _attn_mask()
    g_rows = [norm_g[i:i + 1] for i in range(4)]
    xc1, bwd0 = _pool_layer(x, g_rows[0], mod[0], wts["pool_w_in"][0], wts["pool_w_grp"][0],
                            wts["pool_w_out"][0], wts["pool_scale"][0:1], nx, "p0", ctx=ctx)
    if na_weights is not None:
        wts = {**wts, **na_weights(xc1)}
    x2, bwd1, carried = _na_layer(xc1, g_rows[1], mod[1], wts["na_w_in"], wts["na_rpb"], wts["na_w_out"], nx, consts,
                                  late_comm)
    if late_weights is not None:
        wts = {**wts, **late_weights(carried)}
    x3, bwd2 = _conv_layer(x2, g_rows[2], mod[2], wts["conv_w_in"], wts["conv_dw"], wts["conv_db"], wts["conv_w_out"])
    (loss, dx4, dfinal_g), bwd3 = _pool_layer(x3, g_rows[3], mod[3], wts["pool_w_in"][1], wts["pool_w_grp"][1],
                                              wts["pool_w_out"][1], wts["pool_scale"][1:2], nx, "p3",
                                              head=(final_g, target))
    call = lambda k, *args: hooks[k](*args) if k in hooks else None
    dx3, dmod3, dg3, gr3 = bwd3(dx4)
    dx2, dmod2, dg2, gr2 = bwd2(dx3)
    dxc1, dmod1, dg1, gr1, carried_bwd = bwd1(dx2, call("grad_comm", gr3, gr2))
    dx0, dmod0, dg0, gr0 = bwd0(dxc1, call("na_grads_start", gr1))
    return dict(
        loss=loss, grad_x=dx0, dmod=jnp.stack([dmod0, dmod1, dmod2, dmod3]),
        dnorm_g=jnp.stack([dg0, dg1, dg2, dg3]), dfinal_g=dfinal_g, layers=(gr0, gr1, gr2, gr3), carried=carried_bwd)


_AXES = ("x", "y", "c")
_CHIP_FLIPS = ((1, 0), (0, 1), (1, 1))


def _position():
    return tuple(lax.axis_index(a) for a in _AXES)


def _flipped(pos, flip):
    return tuple(1 - p if f else p for p, f in zip(pos, flip))


def _join_comms(comms):
    n_in = [len(c.ins) for c in comms]
    n_out = [len(c.outs) for c in comms]
    n_sem = [len(c.sems) for c in comms]

    def parts(ins, outs, sems):
        for k in range(len(comms)):
            a, b, s = sum(n_in[:k]), sum(n_out[:k]), sum(n_sem[:k])
            yield comms[k], (ins[a:a + n_in[k]], outs[b:b + n_out[k]], sems[s:s + n_sem[k]])

    def start(ins, outs, sems):
        for c, part in parts(ins, outs, sems):
            c.start(*part)

    def finish(ins, outs, sems):
        for c, part in parts(ins, outs, sems):
            c.finish(*part)

    joint = _Comm([a for c in comms for a in c.ins], [o for c in comms for o in c.outs],
                  [s for c in comms for s in c.sems], start, finish)
    return joint, lambda res: [list(res[sum(n_out[:k]):sum(n_out[:k + 1])]) for k in range(len(comms))]


def _run_comms(comms, name):
    joint, split = _join_comms(comms)

    def body(*refs):
        n_in, n_out = len(joint.ins), len(joint.outs)
        joint.start(refs[:n_in], refs[n_in:n_in + n_out], refs[n_in + n_out:])
        joint.finish(refs[:n_in], refs[n_in:n_in + n_out], refs[n_in + n_out:])

    res = pl.pallas_call(
        body, in_specs=[HBM_SPEC] * len(joint.ins), out_specs=[HBM_SPEC] * len(joint.outs), out_shape=joint.outs,
        scratch_shapes=joint.sems, name=name,
    )(*joint.ins)
    return split(res)


def _all_gather_comm(v, axes):
    flips = [f for f in np.ndindex(2, 2, 2) if any(f) and all(a in axes or not b for a, b in zip(_AXES, f))]
    n = len(flips) + 1

    def copies(ins, outs, sems):
        (v_ref,), (o_ref,), (send_sems, recv_sems, local_sem) = ins, outs, sems
        pos = _position()
        slot = 0
        for a, p in zip(_AXES, pos):
            if a in axes:
                slot = 2 * slot + p
        local = pltpu.make_async_copy(v_ref, o_ref.at[slot], local_sem)
        remote = [pltpu.make_async_remote_copy(v_ref, o_ref.at[slot], send_sems.at[k], recv_sems.at[k],
                                               device_id=_flipped(pos, flip), device_id_type=MESH)
                  for k, flip in enumerate(flips)]
        return [local] + remote

    def start(ins, outs, sems):
        for cp in copies(ins, outs, sems):
            cp.start()

    def finish(ins, outs, sems):
        for cp in copies(ins, outs, sems):
            cp.wait()

    sems = [pltpu.SemaphoreType.DMA((n - 1,)), pltpu.SemaphoreType.DMA((n - 1,)), pltpu.SemaphoreType.DMA(())]
    return _Comm([v], [_sds((n,) + v.shape, v.dtype)], sems, start, finish)


def _all_gather_two_level_comm(v):
    def copies(ins, outs, sems, onward):
        (v_ref,), (o_ref,), (send_sems, recv_sems, local_sem) = ins, outs, sems
        x, y, c = _position()
        sibling = (x, y, 1 - c)
        slot = lambda px, py, pc: o_ref.at[4 * px + 2 * py + pc]
        own = pltpu.make_async_copy(v_ref, slot(x, y, c), local_sem)
        first = [pltpu.make_async_remote_copy(v_ref, slot(x, y, c), send_sems.at[0], recv_sems.at[0],
                                              device_id=sibling, device_id_type=MESH)]
        fwd = []
        for k, flip in enumerate(_CHIP_FLIPS):
            px, py = _flipped((x, y), flip)
            first.append(pltpu.make_async_remote_copy(v_ref, slot(x, y, c), send_sems.at[1 + k], recv_sems.at[1 + k],
                                                      device_id=(px, py, c), device_id_type=MESH))
            if onward:
                fwd.append(pltpu.make_async_remote_copy(slot(px, py, c), slot(px, py, c), send_sems.at[4 + k],
                                                        recv_sems.at[4 + k], device_id=sibling, device_id_type=MESH))
        return own, first, fwd

    def start(ins, outs, sems):
        own, first, _ = copies(ins, outs, sems, False)
        for cp in [own] + first:
            cp.start()

    def finish(ins, outs, sems):
        own, first, fwd = copies(ins, outs, sems, True)
        for arrived, onward in zip(first[1:], fwd):
            arrived.wait_recv()
            onward.start()
        first[0].wait_recv()
        for cp in fwd:
            cp.wait_recv()
        for cp in first + fwd:
            cp.wait_send()
        own.wait()

    sems = [pltpu.SemaphoreType.DMA((7,)), pltpu.SemaphoreType.DMA((7,)), pltpu.SemaphoreType.DMA(())]
    return _Comm([v], [_sds((8,) + v.shape, v.dtype)], sems, start, finish)


def _all_gather(v, axes, name):
    return _run_comms([_all_gather_comm(v, axes)], name)[0][0]


class _Item:
    def __init__(self, key, layer, shape, shard_axis, half_axis):
        self.key, self.layer, self.shape = key, layer, tuple(shape)
        self.shard_axis, self.half_axis = shard_axis, half_axis
        self.shard = shape[shard_axis] // 4
        self.half = shape[half_axis] // 2

    def sized(self, shard=False, half=False):
        s = list(self.shape)
        if shard:
            s[self.shard_axis] = self.shard
        if half:
            s[self.half_axis] = self.half
        return tuple(s)

    def window(self, ref, chip=None, half=None):
        idx = [slice(None)] * len(self.shape)
        if chip is not None:
            idx[self.shard_axis] = pl.ds(chip * self.shard, self.shard)
        if half is not None:
            idx[self.half_axis] = pl.ds(half * self.half, self.half)
        return ref.at[tuple(idx)]


def _items(d, w):
    out = []
    for j in range(2):
        out += [_Item("pool_w_in", j, (d, 2 * w), 1, 0), _Item("pool_w_grp", j, (4, w // 4, w // 4), 1, 0),
                _Item("pool_w_out", j, (w, d), 0, 1)]
    out += [_Item("na_w_in", 0, (d, 4 * w), 1, 0), _Item("na_w_out", 0, (w, d), 0, 1),
            _Item("conv_w_in", 0, (d, 4 * w), 1, 0), _Item("conv_w_out", 0, (w, d), 0, 1)]
    return out


def _gather_comm(shards, items):
    n = len(items)

    def copies(src, dst, sems, onward):
        send_a, recv_a, send_b, recv_b, send_c, recv_c = sems
        x, y, c = _position()
        chip = 2 * x + y
        sibling = (x, y, 1 - c)
        own, out, fwd, fwd_in = [], [], [], []
        for i, it in enumerate(items):
            own.append(pltpu.make_async_remote_copy(src[i], it.window(dst[i], chip=chip), send_c.at[i], recv_c.at[i],
                                                    device_id=sibling, device_id_type=MESH))
            for k, flip in enumerate(_CHIP_FLIPS):
                px, py = _flipped((x, y), flip)
                s = 3 * i + k
                out.append(pltpu.make_async_remote_copy(
                    it.window(src[i], half=c), it.window(dst[i], chip=chip, half=c), send_a.at[s], recv_a.at[s],
                    device_id=(px, py, c), device_id_type=MESH))
                if onward:
                    got = it.window(dst[i], chip=2 * px + py, half=c)
                    fwd.append(pltpu.make_async_remote_copy(got, got, send_b.at[s], recv_b.at[s],
                                                            device_id=sibling, device_id_type=MESH))
                    other = it.window(dst[i], chip=2 * px + py, half=1 - c)
                    fwd_in.append(pltpu.make_async_remote_copy(other, other, send_b.at[s], recv_b.at[s],
                                                               device_id=sibling, device_id_type=MESH))
        return own, out, fwd, fwd_in

    def start(src, dst, sems):
        own, out, _, _ = copies(src, dst, sems, False)
        for cp in own + out:
            cp.start()

    def finish(src, dst, sems):
        own, out, fwd, fwd_in = copies(src, dst, sems, True)
        for arrived, onward in zip(out, fwd):
            arrived.wait_recv()
            onward.start()
        for cp in fwd_in:
            cp.wait_recv()
        for cp in out + fwd:
            cp.wait_send()
        for cp in own:
            cp.wait()

    sems = [pltpu.SemaphoreType.DMA((3 * n,)) for _ in range(4)] + [pltpu.SemaphoreType.DMA((n,)) for _ in range(2)]
    return _Comm(shards, [_sds(it.shape, BF16) for it in items], sems, start, finish)


def _pair_swap_copies(windows):
    def copies(src, got, sems):
        send_sems, recv_sems = sems
        x, y, c = _position()
        return [pltpu.make_async_remote_copy(windows[i](src[i], 1 - c), got[i], send_sems.at[i], recv_sems.at[i],
                                             device_id=(x, y, 1 - c), device_id_type=MESH)
                for i in range(len(windows))]

    return copies


def _pair_swap_comm(arrays, windows, out_shapes):
    n = len(arrays)
    copies = _pair_swap_copies(windows)

    def start(src, got, sems):
        for cp in copies(src, got, sems):
            cp.start()

    def finish(src, got, sems):
        for cp in copies(src, got, sems):
            cp.wait()

    return _Comm(arrays, out_shapes, [pltpu.SemaphoreType.DMA((n,)), pltpu.SemaphoreType.DMA((n,))], start, finish)


def _pair_swap(arrays, windows, out_shapes, name):
    return _run_comms([_pair_swap_comm(arrays, windows, out_shapes)], name)[0]


def _chip_exchange_copies(items):
    def copies(src, dst, sems):
        send_sems, recv_sems = sems
        x, y, c = _position()
        out = []
        for i, it in enumerate(items):
            for k, flip in enumerate(_CHIP_FLIPS):
                px, py = _flipped((x, y), flip)
                out.append(pltpu.make_async_remote_copy(
                    it.window(src[i], chip=2 * px + py), dst[i].at[k], send_sems.at[3 * i + k],
                    recv_sems.at[3 * i + k], device_id=(px, py, c), device_id_type=MESH))
        return out

    return copies


_SEM_SPEC = pl.BlockSpec(memory_space=pltpu.SEMAPHORE)
_DATAFLOW = pltpu.SideEffectType.DATAFLOW_SIDE_EFFECTING


def _split_start(copies, srcs, zones, n_copies, name):
    n, nz = len(srcs), len(zones)

    def body(*refs):
        src, land = refs[:n], refs[n:n + nz]
        send_sems, recv_sems = refs[n + nz:n + nz + 2]
        token = refs[-1]
        for cp in copies(src, land, (send_sems, recv_sems)):
            cp.start()
        token[...] = jnp.zeros(token.shape, F32)

    hbm = lambda t: pltpu.HBM(t.shape, t.dtype)
    res = pl.pallas_call(
        body, name=name,
        out_shape=(pltpu.SemaphoreType.DMA((n_copies,)), pltpu.SemaphoreType.DMA((n_copies,)),
                   *[hbm(t) for t in list(srcs) + list(zones)], _sds((8, 128), F32)),
        in_specs=[HBM_SPEC] * (n + nz),
        out_specs=(_SEM_SPEC, _SEM_SPEC, *[HBM_SPEC] * (n + nz), pl.BlockSpec(memory_space=pltpu.VMEM)),
        input_output_aliases={i: 2 + i for i in range(n + nz)},
        compiler_params=pltpu.CompilerParams(has_side_effects=_DATAFLOW),
    )(*[pltpu.with_memory_space_constraint(t, pltpu.HBM) for t in list(srcs) + list(zones)])
    return (res[0], res[1], list(res[2:2 + n]), list(res[2 + n:2 + n + nz])), res[-1]


def _split_wait(copies, handle, after, name):
    send_sems, recv_sems, srcs, zones = handle
    n, nz = len(srcs), len(zones)

    def body(*refs):
        src, land = refs[:n], refs[n:n + nz]
        send, recv = refs[n + nz:n + nz + 2]
        for cp in copies(src, land, (send, recv)):
            cp.wait_send()
            cp.wait_recv()

    hbm = lambda t: pltpu.HBM(t.shape, t.dtype)
    res = pl.pallas_call(
        body, name=name, out_shape=tuple(hbm(t) for t in list(srcs) + list(zones)),
        in_specs=[HBM_SPEC] * (n + nz) + [_SEM_SPEC, _SEM_SPEC, pl.BlockSpec(memory_space=pl.ANY)],
        out_specs=tuple([HBM_SPEC] * (n + nz)), input_output_aliases={i: i for i in range(n + nz)},
        compiler_params=pltpu.CompilerParams(has_side_effects=_DATAFLOW),
    )(*srcs, *zones, send_sems, recv_sems, after)
    return list(res[:n]), list(res[n:])


def _gather_ici_copies(items):
    def copies(src, dst, sems):
        send_sems, recv_sems = sems
        x, y, c = _position()
        chip = 2 * x + y
        out = []
        for i, it in enumerate(items):
            for k, flip in enumerate(_CHIP_FLIPS):
                px, py = _flipped((x, y), flip)
                out.append(pltpu.make_async_remote_copy(
                    it.window(src[i], half=c), it.window(dst[i], chip=chip, half=c), send_sems.at[3 * i + k],
                    recv_sems.at[3 * i + k], device_id=(px, py, c), device_id_type=MESH))
        return out

    return copies


def _gather_pair_finish(shards, mats, items, name):
    n = len(items)

    def body(*refs):
        src, dst = refs[:n], refs[2 * n:3 * n]
        send_own, recv_own, send_fwd, recv_fwd = refs[3 * n:]
        x, y, c = _position()
        chip = 2 * x + y
        sibling = (x, y, 1 - c)
        copies = []
        for i, it in enumerate(items):
            copies.append(pltpu.make_async_remote_copy(src[i], it.window(dst[i], chip=chip), send_own.at[i],
                                                       recv_own.at[i], device_id=sibling, device_id_type=MESH))
            for k, flip in enumerate(_CHIP_FLIPS):
                px, py = _flipped((x, y), flip)
                got = it.window(dst[i], chip=2 * px + py, half=c)
                copies.append(pltpu.make_async_remote_copy(got, got, send_fwd.at[3 * i + k], recv_fwd.at[3 * i + k],
                                                           device_id=sibling, device_id_type=MESH))
        for cp in copies:
            cp.start()
        for cp in copies:
            cp.wait()

    return pl.pallas_call(
        body, in_specs=[HBM_SPEC] * (2 * n), out_specs=[HBM_SPEC] * n, out_shape=[_sds(it.shape, BF16) for it in items],
        input_output_aliases={n + i: i for i in range(n)},
        scratch_shapes=[pltpu.SemaphoreType.DMA((n,)), pltpu.SemaphoreType.DMA((n,)),
                        pltpu.SemaphoreType.DMA((3 * n,)), pltpu.SemaphoreType.DMA((3 * n,))], name=name,
    )(*shards, *mats)


def _chip_exchange_comm(partials, items):
    n = len(items)
    copies = _chip_exchange_copies(items)

    def start(src, dst, sems):
        for cp in copies(src, dst, sems):
            cp.start()

    def finish(src, dst, sems):
        for cp in copies(src, dst, sems):
            cp.wait()

    return _Comm(partials, [_sds((3,) + it.sized(shard=True, half=True), BF16) for it in items],
                 [pltpu.SemaphoreType.DMA((3 * n,)), pltpu.SemaphoreType.DMA((3 * n,))], start, finish)


_SUM_STEPS = 2


def _pair_sums(gs, gots, its, pos, name):
    n = len(its)
    nb = _SUM_STEPS
    g2 = [g.reshape(-1, g.shape[-1]) for g in gs]
    got2 = [t.reshape(-1, t.shape[-1]) for t in gots]

    def body(pos_ref, *refs):
        for g_ref, got_ref, o_ref in zip(refs[:n], refs[n:2 * n], refs[2 * n:]):
            o_ref[...] = (g_ref[...].astype(F32) + got_ref[...].astype(F32)).astype(BF16)

    g_specs, got_specs = [], []
    for it, t in zip(its, got2):
        rows, cols = t.shape
        blk = (rows // nb, cols)
        g_map = (lambda i, pos: (pos[1] * nb + i, 0)) if it.half_axis == 0 else (lambda i, pos: (i, pos[1]))
        g_specs.append(pl.BlockSpec(blk, g_map))
        got_specs.append(pl.BlockSpec(blk, lambda i, pos: (i, 0)))
    outs = pl.pallas_call(
        body, grid_spec=pltpu.PrefetchScalarGridSpec(
            num_scalar_prefetch=1, grid=(nb,), in_specs=g_specs + got_specs, out_specs=got_specs),
        out_shape=[_sds(t.shape, BF16) for t in got2], name=name, compiler_params=_cparams("parallel"),
    )(pos, *g2, *got2)
    return [o.reshape(t.shape) for o, t in zip(outs, gots)]


_FLIP_SLOT = {2: 0, 1: 1, 3: 2}


def _chip_sums(pairs, slots, its, pos, name):
    n = len(its)
    nb = _SUM_STEPS

    def body(pos_ref, *refs):
        chip = pos_ref[0]
        for own in range(4):
            @pl.when(chip == own)
            def _():
                for p_ref, s_ref, o_ref in zip(refs[:n], refs[n:2 * n], refs[2 * n:]):
                    acc = None
                    for k in range(4):
                        v = (p_ref[...] if k == own else s_ref[_FLIP_SLOT[own ^ k]]).astype(F32)
                        acc = v if acc is None else acc + v
                    o_ref[...] = acc

    p_specs, s_specs, o_specs, shapes = [], [], [], []
    for it in its:
        shape = it.sized(shard=True, half=True)
        blk = (shape[0] // nb,) + shape[1:]
        rest = (0,) * (len(shape) - 1)

        def p_map(i, pos, it=it, nd=len(shape)):
            lead = i + (pos[0] * nb if it.shard_axis == 0 else 0)
            return (lead,) + tuple(pos[0] if ax == it.shard_axis else 0 for ax in range(1, nd))

        p_specs.append(pl.BlockSpec(blk, p_map))
        s_specs.append(pl.BlockSpec((3,) + blk, lambda i, pos, rest=rest: (0, i) + rest))
        o_specs.append(pl.BlockSpec(blk, lambda i, pos, rest=rest: (i,) + rest))
        shapes.append(_sds(shape, F32))
    return pl.pallas_call(
        body, grid_spec=pltpu.PrefetchScalarGridSpec(
            num_scalar_prefetch=1, grid=(nb,), in_specs=p_specs + s_specs, out_specs=o_specs),
        out_shape=shapes, name=name, compiler_params=_cparams("parallel"),
    )(pos, *pairs, *slots)


_GRAD_KEYS = ("pool_w_in", "pool_w_grp", "pool_w_out", "na_w_in", "na_w_out", "conv_w_in", "conv_w_out")


def _adamw_matrix(w, m, v, owns, others, it, pos, name):
    nl = w.shape[0]
    rows_split = it.half_axis == 0
    r, cdim = int(np.prod(w.shape[1:-1])), w.shape[-1]
    hr, hc = (r // 2, cdim) if rows_split else (r, cdim // 2)
    br = min(hr, 256)
    nb = hr // br
    c1 = 1.0 - ADAM_B1 ** ADAM_STEP
    c2 = 1.0 - ADAM_B2 ** ADAM_STEP

    def body(pos_ref, w_ref, m_ref, v_ref, *rest):
        own_refs, other_refs = rest[:nl], rest[nl:2 * nl]
        g_ref, d_ref, nm_ref, nv_ref = rest[2 * nl:]
        j, h = pl.program_id(0), pl.program_id(1)
        own, other = own_refs[0][...], other_refs[0][...]
        for q in range(1, nl):
            own = jnp.where(j == q, own_refs[q][...], own)
            other = jnp.where(j == q, other_refs[q][...], other)
        gv = jnp.where(h == pos_ref[1], own, other)
        nm = ADAM_B1 * m_ref[...] + (1.0 - ADAM_B1) * gv
        nv = ADAM_B2 * v_ref[...] + (1.0 - ADAM_B2) * (gv * gv)
        g_ref[...] = gv
        nm_ref[...] = nm
        nv_ref[...] = nv
        d_ref[...] = -ADAM_LR * ((nm / c1) / (jnp.sqrt(nv / c2) + ADAM_EPS) + ADAM_WD * w_ref[...])

    if rows_split:
        full = pl.BlockSpec((None, br, hc), lambda j, h, i, pos: (j, h * nb + i, 0))
    else:
        full = pl.BlockSpec((None, br, hc), lambda j, h, i, pos: (j, i, h))
    half = pl.BlockSpec((br, hc), lambda j, h, i, pos: (i, 0))
    flat = lambda t: t.reshape(nl, r, cdim)
    outs = pl.pallas_call(
        body, grid_spec=pltpu.PrefetchScalarGridSpec(
            num_scalar_prefetch=1, grid=(nl, 2, nb), in_specs=[full] * 3 + [half] * (2 * nl), out_specs=[full] * 4),
        out_shape=[_sds((nl, r, cdim), F32)] * 4, name=name,
        compiler_params=_cparams("parallel", "parallel", "parallel"),
    )(pos, flat(w), flat(m), flat(v), *[t.reshape(hr, hc) for t in list(owns) + list(others)])
    return tuple(t.reshape(w.shape) for t in outs)


_WEIGHTS = ("c_ctx", "norm_g", "ada_w", "ada_b", "pool_w_in", "pool_w_grp", "pool_scale", "pool_w_out", "na_w_in",
            "na_rpb", "na_w_out", "conv_w_in", "conv_dw", "conv_db", "conv_w_out", "final_g")
_COND_ROWS = 16


def _modulations(cond, ada_w, ada_b_cols):
    nl, d, n = ada_w.shape
    return _matmul(
        cond, ada_w, mode="nn", grid=(nl, 1), a_silu=True, epilogue="bias",
        a_spec=pl.BlockSpec((_COND_ROWS, d), lambda i, j: (0, 0)), b_spec=pl.BlockSpec((None, d, n), lambda i, j: (i, 0, 0)),
        extra=(ada_b_cols,), extra_specs=(pl.BlockSpec((None, 1, n), lambda i, j: (i, 0, 0)),),
        out_shapes=[_sds((nl, _COND_ROWS, n), F32)], out_specs=[pl.BlockSpec((None, _COND_ROWS, n), lambda i, j: (i, 0, 0))],
        name="modulations")[0]


def _ada_w_step(cond, dm_cols, w, m, v):
    nl, d, n = w.shape
    tr = d // 2
    c1 = 1.0 - ADAM_B1 ** ADAM_STEP
    c2 = 1.0 - ADAM_B2 ** ADAM_STEP

    def body(c_ref, dm_ref, w_ref, m_ref, v_ref, g_ref, d_ref, nm_ref, nv_ref):
        gv = lax.dot_general(_silu(c_ref[...]).astype(BF16), dm_ref[...].astype(BF16), _DIMS["tn"],
                             preferred_element_type=F32)
        nm = ADAM_B1 * m_ref[...] + (1.0 - ADAM_B1) * gv
        nv = ADAM_B2 * v_ref[...] + (1.0 - ADAM_B2) * (gv * gv)
        g_ref[...] = gv
        nm_ref[...] = nm
        nv_ref[...] = nv
        d_ref[...] = -ADAM_LR * ((nm / c1) / (jnp.sqrt(nv / c2) + ADAM_EPS) + ADAM_WD * w_ref[...])

    blk = pl.BlockSpec((None, tr, n), lambda l, i: (l, i, 0))
    return _call(
        body, (cond, dm_cols, w, m, v), grid=(nl, d // tr),
        in_specs=[pl.BlockSpec((_COND_ROWS, tr), lambda l, i: (0, i)),
                  pl.BlockSpec((None, _COND_ROWS, n), lambda l, i: (l, 0, 0)), blk, blk, blk],
        out_specs=[blk] * 4, out_shape=[_sds(w.shape, F32)] * 4, name="adamw_ada_w")


def _cond_grad(dm_cols, ada_w):
    nl, d, n = ada_w.shape
    return _matmul(
        dm_cols, ada_w, mode="nt", grid=(1, nl), nk=nl, acc_shape=(_COND_ROWS, d),
        a_spec=pl.BlockSpec((None, _COND_ROWS, n), lambda i, q: (q, 0, 0)), b_spec=pl.BlockSpec((None, d, n), lambda i, q: (q, 0, 0)),
        out_shapes=[_sds((_COND_ROWS, d), F32)], out_specs=[pl.BlockSpec((_COND_ROWS, d), lambda i, q: (0, 0))],
        name="cond_grad")[0]


def _pack(parts):
    flat = [p.reshape(-1) for p in parts]
    sizes = [f.shape[0] for f in flat]
    total = sum(sizes)
    rows = -(-total // 1024) * 8
    packed = jnp.concatenate(flat + [jnp.zeros((rows * 128 - total,), F32)]).reshape(rows, 128)
    offs = np.concatenate([[0], np.cumsum(sizes)])[:-1]
    return packed, [(int(o), p.shape) for o, p in zip(offs, parts)]


def _unpack(flat, layout, k):
    off, shape = layout[k]
    return flat[..., off:off + int(np.prod(shape))].reshape(flat.shape[:-1] + tuple(shape))


def kernel(x, c, ctx, c_ctx, norm_g, ada_w, ada_b, pool_w_in, pool_w_grp, pool_scale, pool_w_out, na_w_in, na_rpb, na_w_out, conv_w_in, conv_dw, conv_db, conv_w_out, final_g, loss_target, m_c_ctx, m_norm_g, m_ada_w, m_ada_b, m_pool_w_in, m_pool_w_grp, m_pool_scale, m_pool_w_out, m_na_w_in, m_na_rpb, m_na_w_out, m_conv_w_in, m_conv_dw, m_conv_db, m_conv_w_out, m_final_g, v_c_ctx, v_norm_g, v_ada_w, v_ada_b, v_pool_w_in, v_pool_w_grp, v_pool_scale, v_pool_w_out, v_na_w_in, v_na_rpb, v_na_w_out, v_conv_w_in, v_conv_dw, v_conv_db, v_conv_w_out, v_final_g):
    params = dict(c_ctx=c_ctx, norm_g=norm_g, ada_w=ada_w, ada_b=ada_b, pool_w_in=pool_w_in, pool_w_grp=pool_w_grp,
                  pool_scale=pool_scale, pool_w_out=pool_w_out, na_w_in=na_w_in, na_rpb=na_rpb, na_w_out=na_w_out,
                  conv_w_in=conv_w_in, conv_dw=conv_dw, conv_db=conv_db, conv_w_out=conv_w_out, final_g=final_g)
    mom1 = dict(c_ctx=m_c_ctx, norm_g=m_norm_g, ada_w=m_ada_w, ada_b=m_ada_b, pool_w_in=m_pool_w_in,
                pool_w_grp=m_pool_w_grp, pool_scale=m_pool_scale, pool_w_out=m_pool_w_out, na_w_in=m_na_w_in,
                na_rpb=m_na_rpb, na_w_out=m_na_w_out, conv_w_in=m_conv_w_in, conv_dw=m_conv_dw, conv_db=m_conv_db,
                conv_w_out=m_conv_w_out, final_g=m_final_g)
    mom2 = dict(c_ctx=v_c_ctx, norm_g=v_norm_g, ada_w=v_ada_w, ada_b=v_ada_b, pool_w_in=v_pool_w_in,
                pool_w_grp=v_pool_w_grp, pool_scale=v_pool_scale, pool_w_out=v_pool_w_out, na_w_in=v_na_w_in,
                na_rpb=v_na_rpb, na_w_out=v_na_w_out, conv_w_in=v_conv_w_in, conv_dw=v_conv_dw, conv_db=v_conv_db,
                conv_w_out=v_conv_w_out, final_g=v_final_g)
    d = x.shape[-1]
    w = na_w_out.shape[1] * 4
    xi, yi, ci = _position()
    chip = 2 * xi + yi
    dev = 2 * chip + ci
    n_ada = ada_w.shape[-1]

    def chip_cols(a, size):
        return lax.dynamic_slice_in_dim(a, chip * size, size, axis=a.ndim - 1)

    items = _items(d, w)
    first = [it for it in items if it.key.startswith("pool") and it.layer == 0]
    na = [it for it in items if it.key.startswith("na")]
    late = [it for it in items if it not in first + na]
    shards_of = lambda its: [params[it.key][it.layer].astype(BF16) for it in its]
    empties = lambda its: [lax.empty(it.shape, BF16) for it in its]
    first_copies, na_copies = _gather_ici_copies(first), _gather_ici_copies(na)

    conds = _all_gather(c.reshape(8, d // 8), _AXES, "gather_cond").reshape(8, d)
    behind = conds[0, 0] * 0.0
    first_handle, token = _split_start(first_copies, [s + behind.astype(BF16) for s in shards_of(first)],
                                       empties(first), 3 * len(first), "gather_first_start")
    cond = jnp.concatenate([conds + token[0, 0], c_ctx[None], jnp.zeros((_COND_ROWS - 9, d), F32)], axis=0)
    mod_cols = _modulations(cond, ada_w, chip_cols(ada_b, n_ada)[:, None, :])
    small_pack, small_layout = _pack([pool_scale, conv_dw, conv_db])
    (mod_all,), (small,) = _run_comms([_all_gather_comm(mod_cols, ("x", "y")),
                                       _all_gather_comm(small_pack, ("x", "y"))], "gather_mod")
    behind = mod_all[0, 0, 0, 0] * 0.0
    na_handle, token = _split_start(na_copies, [s + behind.astype(BF16) for s in shards_of(na)], empties(na),
                                    3 * len(na), "gather_na_start")
    first_shards, first_mats = _split_wait(first_copies, first_handle, token, "gather_first_wait")
    first_mats = _gather_pair_finish(first_shards, first_mats, first, "gather_first_pair")
    mod_all = mod_all.transpose(1, 2, 0, 3).reshape(4, _COND_ROWS, 3, d)
    mod = jnp.stack([lax.dynamic_index_in_dim(mod_all, dev, axis=1, keepdims=False), mod_all[:, 8]], axis=1)
    full = {(it.key, it.layer): mat for it, mat in zip(first, first_mats)}
    late_comm = _gather_comm(shards_of(late), late)

    def na_weights(after):
        na_shards, na_mats = _split_wait(na_copies, na_handle, after, "gather_na_wait")
        na_mats = _gather_pair_finish(na_shards, na_mats, na, "gather_na_pair")
        return {it.key: mat for it, mat in zip(na, na_mats)}

    def late_weights(mats):
        full.update({(it.key, it.layer): mat for it, mat in zip(late, mats)})
        return dict(pool_w_in=[full[("pool_w_in", j)] for j in range(2)],
                    pool_w_grp=[full[("pool_w_grp", j)] for j in range(2)],
                    pool_w_out=[full[("pool_w_out", j)] for j in range(2)],
                    conv_w_in=full[("conv_w_in", 0)], conv_w_out=full[("conv_w_out", 0)])

    small = small.reshape(4, -1)

    def whole(k):
        parts = _unpack(small, small_layout, k)
        return jnp.moveaxis(parts, 0, -2).reshape(parts.shape[1:-1] + (-1,))

    wts = dict(pool_w_in=[full[("pool_w_in", 0)]], pool_w_grp=[full[("pool_w_grp", 0)]],
               pool_w_out=[full[("pool_w_out", 0)]], pool_scale=whole(0), na_rpb=na_rpb[0], conv_dw=whole(1)[0],
               conv_db=whole(2))
    pos = jnp.stack([chip, ci]).astype(jnp.int32)

    def layer_grads(its, by_layer):
        pick = {"pool_w_in": "w_in", "pool_w_grp": "w_grp", "pool_w_out": "w_out", "na_w_in": "w_in",
                "na_w_out": "w_out", "conv_w_in": "w_in", "conv_w_out": "w_out"}
        return [by_layer[(it.key.split("_")[0], it.layer)][pick[it.key]] for it in its]

    pairs, handles = dict(), dict()
    half_windows = lambda its: [(lambda ref, half, it=it: it.window(ref, half=half)) for it in its]
    half_shapes = lambda its: [_sds(it.sized(half=True), BF16) for it in its]

    def pair_sums(its, mats, tag):
        got = _pair_swap(mats, half_windows(its), half_shapes(its), f"pair_exchange_{tag}")
        return _pair_sums(mats, got, its, pos, f"pair_sum_{tag}")

    def grad_comm(gr3, gr2):
        pairs["late"] = pair_sums(late, layer_grads(late, {("pool", 1): gr3, ("conv", 0): gr2}), "late")
        return _chip_exchange_comm(pairs["late"], late)

    slot_zones = lambda its: [lax.empty((3,) + it.sized(shard=True, half=True), BF16) for it in its]
    na_xcopies, first_xcopies = _chip_exchange_copies(na), _chip_exchange_copies(first)

    def na_grads_start(gr1):
        pairs["na"] = pair_sums(na, layer_grads(na, {("na", 0): gr1}), "na")
        handles["na"], started = _split_start(na_xcopies, pairs["na"], slot_zones(na), 3 * len(na),
                                              "exchange_na_start")
        return started

    res = _example_step(x[0], ctx[0], loss_target[0], mod, norm_g, final_g[None], wts, dict(
        na_weights=na_weights, late_comm=late_comm, late_weights=late_weights, grad_comm=grad_comm,
        na_grads_start=na_grads_start))
    g0, g1, g2, g3 = res["layers"]
    pairs["na"], na_slots = _split_wait(na_xcopies, handles["na"], g0["w_in"], "exchange_na_wait")
    first_grads = layer_grads(first, {("pool", 0): g0})
    packed, layout = _pack([res["dfinal_g"], res["dnorm_g"], res["dmod"], g1["rpb"],
                            jnp.concatenate([g0["scale"], g3["scale"]], axis=0), g2["dw"], g2["db"],
                            res["loss"][0, :1]])
    first_got, (every,) = _run_comms([_pair_swap_comm(first_grads, half_windows(first), half_shapes(first)),
                                      _all_gather_two_level_comm(packed)], "pair_exchange_first")
    pairs["first"] = _pair_sums(first_grads, first_got, first, pos, "pair_sum_first")

    grads = dict()
    total = _sum_lead(every, "sum_vec_grads").reshape(-1)
    every = every.reshape(8, -1)
    grads["final_g"] = _unpack(total, layout, 0).reshape(final_g.shape)
    grads["norm_g"] = _unpack(total, layout, 1)
    grads["na_rpb"] = _unpack(total, layout, 3)[None]
    grads["pool_scale"] = chip_cols(_unpack(total, layout, 4), pool_scale.shape[-1])
    grads["conv_dw"] = chip_cols(_unpack(total, layout, 5), conv_dw.shape[-1])[None]
    grads["conv_db"] = chip_cols(_unpack(total, layout, 6), conv_db.shape[-1])
    dmod_sum = _unpack(total, layout, 2).reshape(4, 2, 3 * d)
    dmod_each = _unpack(every, layout, 2).reshape(8, 4, 2, 3 * d)
    grads["ada_b"] = dmod_sum[:, 0] + dmod_sum[:, 1]
    dm = jnp.concatenate([dmod_each[:, :, 0].transpose(1, 0, 2), dmod_sum[:, 1][:, None],
                          jnp.zeros((4, _COND_ROWS - 9, 3 * d), F32)], axis=1)
    dm_cols = chip_cols(dm, n_ada)
    dcond = _cond_grad(dm_cols, ada_w)[8].reshape(8, d // 8)
    dcond_all = _all_gather(dcond, ("x", "y"), "gather_cond_grad")
    behind = dcond_all[0, 0, 0] * 0.0
    handles["first"], token = _split_start(first_xcopies, [p + behind.astype(BF16) for p in pairs["first"]],
                                           slot_zones(first), 3 * len(first), "exchange_first_start")
    grads["ada_w"], *ada_w_step = _ada_w_step(cond, dm_cols + token[0, 0], ada_w, m_ada_w, v_ada_w)
    grads["c_ctx"] = _sum_lead(dcond_all, "sum_cond_grad").reshape(d) * _dsilu(c_ctx)
    vector_out = {k: _adamw(params[k], grads[k], mom1[k], mom2[k], f"adamw_{k}")
                  for k in _WEIGHTS if k not in _GRAD_KEYS + ("ada_w",)}
    vector_out["ada_w"] = tuple(ada_w_step)
    pairs["first"], first_slots = _split_wait(first_xcopies, handles["first"], vector_out["ada_w"][2],
                                              "exchange_first_wait")

    slots = dict(zip(late, res["carried"]))
    slots.update(zip(first, first_slots))
    slots.update(zip(na, na_slots))
    pair_of = dict(zip(late, pairs["late"]))
    pair_of.update(zip(first, pairs["first"]))
    pair_of.update(zip(na, pairs["na"]))
    reduced = _chip_sums([pair_of[it] for it in items], [slots[it] for it in items], items, pos, "chip_sum")
    theirs = _pair_swap(reduced, [lambda ref, half: ref] * len(items),
                        [_sds(t.shape, F32) for t in reduced], "pair_return")
    matrix_out = dict()
    for k in _GRAD_KEYS:
        idx = [i for i, it in enumerate(items) if it.key == k]
        res_k = _adamw_matrix(params[k], mom1[k], mom2[k], [reduced[i] for i in idx], [theirs[i] for i in idx],
                              items[idx[0]], pos, f"adamw_{k}")
        grads[k], matrix_out[k] = res_k[0], res_k[1:]

    outs = [[], [], []]
    for k in _WEIGHTS:
        step = matrix_out[k] if k in matrix_out else vector_out[k]
        for lst, val in zip(outs, step):
            lst.append(val)
    loss = _unpack(total, layout, 7)[0]
    return (loss, res["grad_x"][None], *[grads[k].reshape(params[k].shape) for k in _WEIGHTS],
            *outs[0], *outs[1], *outs[2])
```
